```python
import jax, jax.numpy as jnp
from jax import lax
import numpy as np

D_MODEL = 1024
BATCH = 16
SEQ = 2048
DEPTH = 2

CHUNK = 64
N_MIXERS = 2
HEAD_DIM = 64
MEM_TOKENS = 256
MEM_HEADS = 4
MEM_WIDTH = MEM_HEADS * HEAD_DIM
TOK_WIDTH = D_MODEL - MEM_WIDTH
ATT_HEADS = TOK_WIDTH // HEAD_DIM
LEFT_CHUNKS = 8
BAND = (LEFT_CHUNKS + 1) * CHUNK
BAND_PAD = LEFT_CHUNKS * CHUNK
REL_CLIP = 128
N_REL = REL_CLIP + CHUNK
CONV_WIDTH = 31
CONV_CH = TOK_WIDTH
A_IN = 3 * TOK_WIDTH + MEM_WIDTH
B_IN = 2 * CONV_CH + MEM_WIDTH
D_FF = -(-8 * D_MODEL // (3 * 256)) * 256
EPS = 1e-6
NEG_INF = -1e30
ATTN_SCALE = HEAD_DIM ** -0.5

_DIST = np.arange(CHUNK)[:, None] - np.arange(BAND)[None, :] + BAND_PAD
REL_IDX = np.clip(_DIST, -(CHUNK - 1), REL_CLIP) + (CHUNK - 1)
BAND_OFF = np.arange(BAND) - BAND_PAD

kernel_name = "hybrid_chunkattn_conformerconv_memxattn"


def rms_norm(x, g):
    xf = x.astype(jnp.float32)
    y = xf * lax.rsqrt(jnp.mean(xf * xf, axis=-1, keepdims=True) + EPS)
    return (y * g.astype(jnp.float32)).astype(x.dtype)


def layer_norm(x, g, b):
    xf = x.astype(jnp.float32)
    mu = jnp.mean(xf, axis=-1, keepdims=True)
    xc = xf - mu
    y = xc * lax.rsqrt(jnp.mean(xc * xc, axis=-1, keepdims=True) + EPS)
    return (y * g.astype(jnp.float32) + b.astype(jnp.float32)).astype(x.dtype)


def chunk_relpos_attention(q, k, v, rel_bias):
    B, S, H, Dh = q.shape
    nc = S // CHUNK
    kp = jnp.pad(k, ((0, 0), (BAND_PAD, 0), (0, 0), (0, 0)))
    vp = jnp.pad(v, ((0, 0), (BAND_PAD, 0), (0, 0), (0, 0)))
    qc = q.reshape(B, nc, CHUNK, H, Dh).transpose(1, 0, 2, 3, 4)
    bias = rel_bias[:, REL_IDX].astype(jnp.float32)
    band_off = jnp.asarray(BAND_OFF, dtype=jnp.int32)

    def one_chunk(args):
        c, q_c = args
        start = c * CHUNK
        k_b = lax.dynamic_slice_in_dim(kp, start, BAND, axis=1)
        v_b = lax.dynamic_slice_in_dim(vp, start, BAND, axis=1)
        s = jnp.einsum('bqhd,bkhd->bhqk', q_c, k_b).astype(jnp.float32) * ATTN_SCALE + bias
        valid = (start + band_off) >= 0
        s = jnp.where(valid, s, NEG_INF)
        p = jax.nn.softmax(s, axis=-1).astype(v.dtype)
        return jnp.einsum('bhqk,bkhd->bqhd', p, v_b)

    out = lax.map(one_chunk, (jnp.arange(nc, dtype=jnp.int32), qc))
    return out.transpose(1, 0, 2, 3, 4).reshape(B, S, H * Dh)


def conformer_conv(u, conv_w, conv_b, ln_g, ln_b):
    a, gate = jnp.split(u, 2, axis=-1)
    h = a * jax.nn.sigmoid(gate)
    hp = jnp.pad(h, ((0, 0), (CONV_WIDTH - 1, 0), (0, 0)))
    y = lax.conv_general_dilated(
        hp, conv_w[:, None, :].astype(h.dtype), window_strides=(1,), padding='VALID',
        dimension_numbers=('NWC', 'WIO', 'NWC'), feature_group_count=CONV_CH)
    y = y + conv_b
    return jax.nn.silu(layer_norm(y, ln_g, ln_b))


def memory_attention(qm, mem_n, w_mem_kv, mq_g, mk_g):
    B, S, _ = qm.shape
    M = mem_n.shape[1]
    km, vm = jnp.split(mem_n @ w_mem_kv, 2, axis=-1)
    q = rms_norm(qm.reshape(B, S, MEM_HEADS, HEAD_DIM), mq_g)
    k = rms_norm(km.reshape(B, M, MEM_HEADS, HEAD_DIM), mk_g)
    v = vm.reshape(B, M, MEM_HEADS, HEAD_DIM)
    s = jnp.einsum('bshd,bmhd->bhsm', q, k).astype(jnp.float32) * ATTN_SCALE
    p = jax.nn.softmax(s, axis=-1).astype(v.dtype)
    return jnp.einsum('bhsm,bmhd->bshd', p, v).reshape(B, S, MEM_WIDTH)


def _fwd_setup_inputs(seed: int = 0) -> dict:
    key = jax.random.key(seed)
    ks = jax.random.split(key, 22)
    n_a = (DEPTH + N_MIXERS - 1) // N_MIXERS
    n_b = DEPTH // N_MIXERS
    f32 = jnp.float32

    def w(k, shape, fan_in):
        return jax.random.normal(k, shape, f32) * (fan_in ** -0.5)

    def gain(k, shape):
        return 1.0 + 0.05 * jax.random.normal(k, shape, f32)

    def small(k, shape, s=0.02):
        return s * jax.random.normal(k, shape, f32)

    return {
        "x": jax.random.normal(ks[0], (BATCH, SEQ, D_MODEL), f32),
        "mem": jax.random.normal(ks[1], (BATCH, MEM_TOKENS, D_MODEL), f32),
        "norm1_g": gain(ks[2], (DEPTH, D_MODEL)),
        "mem_norm_g": gain(ks[3], (DEPTH, D_MODEL)),
        "a_w_in": w(ks[4], (n_a, D_MODEL, A_IN), D_MODEL),
        "a_q_g": gain(ks[5], (n_a, HEAD_DIM)),
        "a_k_g": gain(ks[6], (n_a, HEAD_DIM)),
        "a_rel_bias": small(ks[7], (n_a, ATT_HEADS, N_REL), 0.3),
        "b_w_in": w(ks[8], (n_b, D_MODEL, B_IN), D_MODEL),
        "b_b_in": small(ks[9], (n_b, B_IN)),
        "b_conv_w": w(ks[10], (n_b, CONV_WIDTH, CONV_CH), CONV_WIDTH),
        "b_conv_b": small(ks[11], (n_b, CONV_CH)),
        "b_ln_g": gain(ks[12], (n_b, CONV_CH)),
        "b_ln_b": small(ks[13], (n_b, CONV_CH)),
        "mq_g": gain(ks[14], (DEPTH, HEAD_DIM)),
        "mk_g": gain(ks[15], (DEPTH, HEAD_DIM)),
        "w_mem_kv": w(ks[16], (DEPTH, D_MODEL, 2 * MEM_WIDTH), D_MODEL),
        "w_out": w(ks[17], (DEPTH, D_MODEL, D_MODEL), D_MODEL),
        "norm2_g": gain(ks[18], (DEPTH, D_MODEL)),
        "w_gate": w(ks[19], (DEPTH, D_MODEL, D_FF), D_MODEL),
        "w_up": w(ks[20], (DEPTH, D_MODEL, D_FF), D_MODEL),
        "w_down": w(ks[21], (DEPTH, D_FF, D_MODEL), D_FF),
    }


def _fwd_reference(x, mem, norm1_g, mem_norm_g, a_w_in, a_q_g, a_k_g, a_rel_bias,
              b_w_in, b_b_in, b_conv_w, b_conv_b, b_ln_g, b_ln_b,
              mq_g, mk_g, w_mem_kv, w_out, norm2_g, w_gate, w_up, w_down):
    B, S, _ = x.shape
    for i in range(DEPTH):
        j = i // N_MIXERS
        h = rms_norm(x, norm1_g[i])
        mem_n = rms_norm(mem, mem_norm_g[i])
        if i % N_MIXERS == 0:
            z = h @ a_w_in[j]
            q, k, v, qm = jnp.split(z, [TOK_WIDTH, 2 * TOK_WIDTH, 3 * TOK_WIDTH], axis=-1)
            q = rms_norm(q.reshape(B, S, ATT_HEADS, HEAD_DIM), a_q_g[j])
            k = rms_norm(k.reshape(B, S, ATT_HEADS, HEAD_DIM), a_k_g[j])
            v = v.reshape(B, S, ATT_HEADS, HEAD_DIM)
            tok = chunk_relpos_attention(q, k, v, a_rel_bias[j])
        else:
            z = h @ b_w_in[j] + b_b_in[j]
            u, qm = jnp.split(z, [2 * CONV_CH], axis=-1)
            tok = conformer_conv(u, b_conv_w[j], b_conv_b[j], b_ln_g[j], b_ln_b[j])
        memo = memory_attention(qm, mem_n, w_mem_kv[i], mq_g[i], mk_g[i])
        x = x + jnp.concatenate([tok, memo], axis=-1) @ w_out[i]
        h2 = rms_norm(x, norm2_g[i])
        x = x + (jax.nn.silu(h2 @ w_gate[i]) * (h2 @ w_up[i])) @ w_down[i]
    return x


import jax as _jax
import jax.numpy as _jnp

TWIN_FORMAT = 'train_step'
FWD_PARAMS = ['x', 'mem', 'norm1_g', 'mem_norm_g', 'a_w_in', 'a_q_g', 'a_k_g', 'a_rel_bias', 'b_w_in', 'b_b_in', 'b_conv_w', 'b_conv_b', 'b_ln_g', 'b_ln_b', 'mq_g', 'mk_g', 'w_mem_kv', 'w_out', 'norm2_g', 'w_gate', 'w_up', 'w_down']
TWIN_WEIGHTS = ['norm1_g', 'mem_norm_g', 'a_w_in', 'a_q_g', 'a_k_g', 'a_rel_bias', 'b_w_in', 'b_b_in', 'b_conv_w', 'b_conv_b', 'b_ln_g', 'b_ln_b', 'mq_g', 'mk_g', 'w_mem_kv', 'w_out', 'norm2_g', 'w_gate', 'w_up', 'w_down']
TWIN_DIFF_INPUT = 'x'
TWIN_INPUTS = ['x', 'mem', 'norm1_g', 'mem_norm_g', 'a_w_in', 'a_q_g', 'a_k_g', 'a_rel_bias', 'b_w_in', 'b_b_in', 'b_conv_w', 'b_conv_b', 'b_ln_g', 'b_ln_b', 'mq_g', 'mk_g', 'w_mem_kv', 'w_out', 'norm2_g', 'w_gate', 'w_up', 'w_down', 'loss_target', 'm_norm1_g', 'm_mem_norm_g', 'm_a_w_in', 'm_a_q_g', 'm_a_k_g', 'm_a_rel_bias', 'm_b_w_in', 'm_b_b_in', 'm_b_conv_w', 'm_b_conv_b', 'm_b_ln_g', 'm_b_ln_b', 'm_mq_g', 'm_mk_g', 'm_w_mem_kv', 'm_w_out', 'm_norm2_g', 'm_w_gate', 'm_w_up', 'm_w_down', 'v_norm1_g', 'v_mem_norm_g', 'v_a_w_in', 'v_a_q_g', 'v_a_k_g', 'v_a_rel_bias', 'v_b_w_in', 'v_b_b_in', 'v_b_conv_w', 'v_b_conv_b', 'v_b_ln_g', 'v_b_ln_b', 'v_mq_g', 'v_mk_g', 'v_w_mem_kv', 'v_w_out', 'v_norm2_g', 'v_w_gate', 'v_w_up', 'v_w_down']
TWIN_OUTPUTS = ['loss', 'grad_x', 'grad_norm1_g', 'grad_mem_norm_g', 'grad_a_w_in', 'grad_a_q_g', 'grad_a_k_g', 'grad_a_rel_bias', 'grad_b_w_in', 'grad_b_b_in', 'grad_b_conv_w', 'grad_b_conv_b', 'grad_b_ln_g', 'grad_b_ln_b', 'grad_mq_g', 'grad_mk_g', 'grad_w_mem_kv', 'grad_w_out', 'grad_norm2_g', 'grad_w_gate', 'grad_w_up', 'grad_w_down', 'delta_norm1_g', 'delta_mem_norm_g', 'delta_a_w_in', 'delta_a_q_g', 'delta_a_k_g', 'delta_a_rel_bias', 'delta_b_w_in', 'delta_b_b_in', 'delta_b_conv_w', 'delta_b_conv_b', 'delta_b_ln_g', 'delta_b_ln_b', 'delta_mq_g', 'delta_mk_g', 'delta_w_mem_kv', 'delta_w_out', 'delta_norm2_g', 'delta_w_gate', 'delta_w_up', 'delta_w_down', 'new_m_norm1_g', 'new_m_mem_norm_g', 'new_m_a_w_in', 'new_m_a_q_g', 'new_m_a_k_g', 'new_m_a_rel_bias', 'new_m_b_w_in', 'new_m_b_b_in', 'new_m_b_conv_w', 'new_m_b_conv_b', 'new_m_b_ln_g', 'new_m_b_ln_b', 'new_m_mq_g', 'new_m_mk_g', 'new_m_w_mem_kv', 'new_m_w_out', 'new_m_norm2_g', 'new_m_w_gate', 'new_m_w_up', 'new_m_w_down', 'new_v_norm1_g', 'new_v_mem_norm_g', 'new_v_a_w_in', 'new_v_a_q_g', 'new_v_a_k_g', 'new_v_a_rel_bias', 'new_v_b_w_in', 'new_v_b_b_in', 'new_v_b_conv_w', 'new_v_b_conv_b', 'new_v_b_ln_g', 'new_v_b_ln_b', 'new_v_mq_g', 'new_v_mk_g', 'new_v_w_mem_kv', 'new_v_w_out', 'new_v_norm2_g', 'new_v_w_gate', 'new_v_w_up', 'new_v_w_down']
TWIN_LEAF_KINDS = {'loss': 'loss', 'grad_x': 'grad_x', 'grad_norm1_g': 'grad_w', 'grad_mem_norm_g': 'grad_w', 'grad_a_w_in': 'grad_w', 'grad_a_q_g': 'grad_w', 'grad_a_k_g': 'grad_w', 'grad_a_rel_bias': 'grad_w', 'grad_b_w_in': 'grad_w', 'grad_b_b_in': 'grad_w', 'grad_b_conv_w': 'grad_w', 'grad_b_conv_b': 'grad_w', 'grad_b_ln_g': 'grad_w', 'grad_b_ln_b': 'grad_w', 'grad_mq_g': 'grad_w', 'grad_mk_g': 'grad_w', 'grad_w_mem_kv': 'grad_w', 'grad_w_out': 'grad_w', 'grad_norm2_g': 'grad_w', 'grad_w_gate': 'grad_w', 'grad_w_up': 'grad_w', 'grad_w_down': 'grad_w', 'delta_norm1_g': 'delta_w', 'delta_mem_norm_g': 'delta_w', 'delta_a_w_in': 'delta_w', 'delta_a_q_g': 'delta_w', 'delta_a_k_g': 'delta_w', 'delta_a_rel_bias': 'delta_w', 'delta_b_w_in': 'delta_w', 'delta_b_b_in': 'delta_w', 'delta_b_conv_w': 'delta_w', 'delta_b_conv_b': 'delta_w', 'delta_b_ln_g': 'delta_w', 'delta_b_ln_b': 'delta_w', 'delta_mq_g': 'delta_w', 'delta_mk_g': 'delta_w', 'delta_w_mem_kv': 'delta_w', 'delta_w_out': 'delta_w', 'delta_norm2_g': 'delta_w', 'delta_w_gate': 'delta_w', 'delta_w_up': 'delta_w', 'delta_w_down': 'delta_w', 'new_m_norm1_g': 'new_m', 'new_m_mem_norm_g': 'new_m', 'new_m_a_w_in': 'new_m', 'new_m_a_q_g': 'new_m', 'new_m_a_k_g': 'new_m', 'new_m_a_rel_bias': 'new_m', 'new_m_b_w_in': 'new_m', 'new_m_b_b_in': 'new_m', 'new_m_b_conv_w': 'new_m', 'new_m_b_conv_b': 'new_m', 'new_m_b_ln_g': 'new_m', 'new_m_b_ln_b': 'new_m', 'new_m_mq_g': 'new_m', 'new_m_mk_g': 'new_m', 'new_m_w_mem_kv': 'new_m', 'new_m_w_out': 'new_m', 'new_m_norm2_g': 'new_m', 'new_m_w_gate': 'new_m', 'new_m_w_up': 'new_m', 'new_m_w_down': 'new_m', 'new_v_norm1_g': 'new_v', 'new_v_mem_norm_g': 'new_v', 'new_v_a_w_in': 'new_v', 'new_v_a_q_g': 'new_v', 'new_v_a_k_g': 'new_v', 'new_v_a_rel_bias': 'new_v', 'new_v_b_w_in': 'new_v', 'new_v_b_b_in': 'new_v', 'new_v_b_conv_w': 'new_v', 'new_v_b_conv_b': 'new_v', 'new_v_b_ln_g': 'new_v', 'new_v_b_ln_b': 'new_v', 'new_v_mq_g': 'new_v', 'new_v_mk_g': 'new_v', 'new_v_w_mem_kv': 'new_v', 'new_v_w_out': 'new_v', 'new_v_norm2_g': 'new_v', 'new_v_w_gate': 'new_v', 'new_v_w_up': 'new_v', 'new_v_w_down': 'new_v'}


def _forward(args):
    return _fwd_reference(*[args[k] for k in FWD_PARAMS])


def _output_shape():
    out = _jax.eval_shape(lambda: _forward(_fwd_setup_inputs(0)))
    return out.shape, out.dtype

N_MICROBATCH = 1
ADAM_LR = 0.001
ADAM_B1 = 0.9
ADAM_B2 = 0.999
ADAM_EPS = 1e-08
ADAM_WD = 0.01
ADAM_STEP = 10
PER_EXAMPLE_BATCH_AXIS = {'x': 0, 'mem': 0, 'loss_target': 0}
SHARED_INPUTS = []
_WEIGHT_DTYPES = {'norm1_g': _jnp.float32, 'mem_norm_g': _jnp.float32, 'a_w_in': _jnp.float32, 'a_q_g': _jnp.float32, 'a_k_g': _jnp.float32, 'a_rel_bias': _jnp.float32, 'b_w_in': _jnp.float32, 'b_b_in': _jnp.float32, 'b_conv_w': _jnp.float32, 'b_conv_b': _jnp.float32, 'b_ln_g': _jnp.float32, 'b_ln_b': _jnp.float32, 'mq_g': _jnp.float32, 'mk_g': _jnp.float32, 'w_mem_kv': _jnp.float32, 'w_out': _jnp.float32, 'norm2_g': _jnp.float32, 'w_gate': _jnp.float32, 'w_up': _jnp.float32, 'w_down': _jnp.float32}
MOMENT_SCALE = {'norm1_g': 2.116497e-01, 'mem_norm_g': 1.244089e-01, 'a_w_in': 9.741065e-02, 'a_q_g': 1.351128e+00, 'a_k_g': 1.369862e+00, 'a_rel_bias': 5.103570e-02, 'b_w_in': 1.890975e-01, 'b_b_in': 2.721860e+00, 'b_conv_w': 4.330171e-01, 'b_conv_b': 7.107127e+00, 'b_ln_g': 1.417915e+01, 'b_ln_b': 9.705352e+00, 'mq_g': 1.256270e+00, 'mk_g': 1.258209e+00, 'w_mem_kv': 1.436314e-01, 'w_out': 8.031370e-01, 'norm2_g': 2.489698e+01, 'w_gate': 2.833126e-01, 'w_up': 2.042413e-01, 'w_down': 3.193203e-01}


def _to_microbatches(a, axis):
    t = _jnp.moveaxis(a, axis, 0)
    t = t.reshape((N_MICROBATCH, t.shape[0] // N_MICROBATCH) + t.shape[1:])
    return _jnp.moveaxis(t, 1, axis + 1)


def setup_inputs(seed: int = 0) -> dict:
    inp = _fwd_setup_inputs(seed)
    key = _jax.random.fold_in(_jax.random.key(seed), 7919)
    shape, _ = _output_shape()
    out = dict(inp)
    out["loss_target"] = _jax.random.normal(_jax.random.fold_in(key, 0), shape, _jnp.float32)
    for i, name in enumerate(TWIN_WEIGHTS):
        w = inp[name].astype(_jnp.float32)
        if MOMENT_SCALE is None:
            s = _jnp.sqrt(_jnp.mean(_jnp.square(w)) + 1e-30)
        else:
            s = MOMENT_SCALE[name]
        km, kv = _jax.random.split(_jax.random.fold_in(key, i + 1))
        out[name] = w
        out["m_" + name] = s * _jax.random.normal(km, w.shape, _jnp.float32)
        out["v_" + name] = (s * s) * _jax.random.uniform(kv, w.shape, _jnp.float32, 0.5, 1.5)
    if N_MICROBATCH > 1:
        for name, axis in PER_EXAMPLE_BATCH_AXIS.items():
            out[name] = _to_microbatches(out[name], axis)
    return {'x': out['x'], 'mem': out['mem'], 'norm1_g': out['norm1_g'], 'mem_norm_g': out['mem_norm_g'], 'a_w_in': out['a_w_in'], 'a_q_g': out['a_q_g'], 'a_k_g': out['a_k_g'], 'a_rel_bias': out['a_rel_bias'], 'b_w_in': out['b_w_in'], 'b_b_in': out['b_b_in'], 'b_conv_w': out['b_conv_w'], 'b_conv_b': out['b_conv_b'], 'b_ln_g': out['b_ln_g'], 'b_ln_b': out['b_ln_b'], 'mq_g': out['mq_g'], 'mk_g': out['mk_g'], 'w_mem_kv': out['w_mem_kv'], 'w_out': out['w_out'], 'norm2_g': out['norm2_g'], 'w_gate': out['w_gate'], 'w_up': out['w_up'], 'w_down': out['w_down'], 'loss_target': out['loss_target'], 'm_norm1_g': out['m_norm1_g'], 'm_mem_norm_g': out['m_mem_norm_g'], 'm_a_w_in': out['m_a_w_in'], 'm_a_q_g': out['m_a_q_g'], 'm_a_k_g': out['m_a_k_g'], 'm_a_rel_bias': out['m_a_rel_bias'], 'm_b_w_in': out['m_b_w_in'], 'm_b_b_in': out['m_b_b_in'], 'm_b_conv_w': out['m_b_conv_w'], 'm_b_conv_b': out['m_b_conv_b'], 'm_b_ln_g': out['m_b_ln_g'], 'm_b_ln_b': out['m_b_ln_b'], 'm_mq_g': out['m_mq_g'], 'm_mk_g': out['m_mk_g'], 'm_w_mem_kv': out['m_w_mem_kv'], 'm_w_out': out['m_w_out'], 'm_norm2_g': out['m_norm2_g'], 'm_w_gate': out['m_w_gate'], 'm_w_up': out['m_w_up'], 'm_w_down': out['m_w_down'], 'v_norm1_g': out['v_norm1_g'], 'v_mem_norm_g': out['v_mem_norm_g'], 'v_a_w_in': out['v_a_w_in'], 'v_a_q_g': out['v_a_q_g'], 'v_a_k_g': out['v_a_k_g'], 'v_a_rel_bias': out['v_a_rel_bias'], 'v_b_w_in': out['v_b_w_in'], 'v_b_b_in': out['v_b_b_in'], 'v_b_conv_w': out['v_b_conv_w'], 'v_b_conv_b': out['v_b_conv_b'], 'v_b_ln_g': out['v_b_ln_g'], 'v_b_ln_b': out['v_b_ln_b'], 'v_mq_g': out['v_mq_g'], 'v_mk_g': out['v_mk_g'], 'v_w_mem_kv': out['v_w_mem_kv'], 'v_w_out': out['v_w_out'], 'v_norm2_g': out['v_norm2_g'], 'v_w_gate': out['v_w_gate'], 'v_w_up': out['v_w_up'], 'v_w_down': out['v_w_down']}


def _loss(weights, diff, rest, loss_target):
    with _jax.named_scope("forward"):
        args = {**rest, TWIN_DIFF_INPUT: diff, **{k: w.astype(_WEIGHT_DTYPES[k]) for k, w in weights.items()}}
        y = _forward(args)
    with _jax.named_scope("loss_head"):
        err = _jnp.square(y.astype(_jnp.float32) - loss_target)
        return 0.5 * _jnp.sum(_jnp.mean(err, axis=-1)) if err.ndim else 0.5 * err


def _adamw(w, g, m, v):
    m = ADAM_B1 * m + (1.0 - ADAM_B1) * g
    v = ADAM_B2 * v + (1.0 - ADAM_B2) * _jnp.square(g)
    m_hat = m / (1.0 - ADAM_B1 ** ADAM_STEP)
    v_hat = v / (1.0 - ADAM_B2 ** ADAM_STEP)
    delta = -ADAM_LR * (m_hat / (_jnp.sqrt(v_hat) + ADAM_EPS) + ADAM_WD * w)
    return delta, m, v


def reference(x, mem, norm1_g, mem_norm_g, a_w_in, a_q_g, a_k_g, a_rel_bias, b_w_in, b_b_in, b_conv_w, b_conv_b, b_ln_g, b_ln_b, mq_g, mk_g, w_mem_kv, w_out, norm2_g, w_gate, w_up, w_down, loss_target, m_norm1_g, m_mem_norm_g, m_a_w_in, m_a_q_g, m_a_k_g, m_a_rel_bias, m_b_w_in, m_b_b_in, m_b_conv_w, m_b_conv_b, m_b_ln_g, m_b_ln_b, m_mq_g, m_mk_g, m_w_mem_kv, m_w_out, m_norm2_g, m_w_gate, m_w_up, m_w_down, v_norm1_g, v_mem_norm_g, v_a_w_in, v_a_q_g, v_a_k_g, v_a_rel_bias, v_b_w_in, v_b_b_in, v_b_conv_w, v_b_conv_b, v_b_ln_g, v_b_ln_b, v_mq_g, v_mk_g, v_w_mem_kv, v_w_out, v_norm2_g, v_w_gate, v_w_up, v_w_down):
    given = dict(x=x, mem=mem, norm1_g=norm1_g, mem_norm_g=mem_norm_g, a_w_in=a_w_in, a_q_g=a_q_g, a_k_g=a_k_g, a_rel_bias=a_rel_bias, b_w_in=b_w_in, b_b_in=b_b_in, b_conv_w=b_conv_w, b_conv_b=b_conv_b, b_ln_g=b_ln_g, b_ln_b=b_ln_b, mq_g=mq_g, mk_g=mk_g, w_mem_kv=w_mem_kv, w_out=w_out, norm2_g=norm2_g, w_gate=w_gate, w_up=w_up, w_down=w_down, loss_target=loss_target, m_norm1_g=m_norm1_g, m_mem_norm_g=m_mem_norm_g, m_a_w_in=m_a_w_in, m_a_q_g=m_a_q_g, m_a_k_g=m_a_k_g, m_a_rel_bias=m_a_rel_bias, m_b_w_in=m_b_w_in, m_b_b_in=m_b_b_in, m_b_conv_w=m_b_conv_w, m_b_conv_b=m_b_conv_b, m_b_ln_g=m_b_ln_g, m_b_ln_b=m_b_ln_b, m_mq_g=m_mq_g, m_mk_g=m_mk_g, m_w_mem_kv=m_w_mem_kv, m_w_out=m_w_out, m_norm2_g=m_norm2_g, m_w_gate=m_w_gate, m_w_up=m_w_up, m_w_down=m_w_down, v_norm1_g=v_norm1_g, v_mem_norm_g=v_mem_norm_g, v_a_w_in=v_a_w_in, v_a_q_g=v_a_q_g, v_a_k_g=v_a_k_g, v_a_rel_bias=v_a_rel_bias, v_b_w_in=v_b_w_in, v_b_b_in=v_b_b_in, v_b_conv_w=v_b_conv_w, v_b_conv_b=v_b_conv_b, v_b_ln_g=v_b_ln_g, v_b_ln_b=v_b_ln_b, v_mq_g=v_mq_g, v_mk_g=v_mk_g, v_w_mem_kv=v_w_mem_kv, v_w_out=v_w_out, v_norm2_g=v_norm2_g, v_w_gate=v_w_gate, v_w_up=v_w_up, v_w_down=v_w_down)
    weights = {n: given[n] for n in TWIN_WEIGHTS}
    shared = {n: given[n] for n in SHARED_INPUTS}
    per_example = {n: given[n] for n in ['x', 'mem']}
    grad_fn = _jax.value_and_grad(_loss, argnums=(0, 1))

    def one_microbatch(ex, loss_target):
        ex = dict(ex)
        diff = ex.pop(TWIN_DIFF_INPUT)
        return grad_fn(weights, diff, {**shared, **ex}, loss_target)

    if N_MICROBATCH == 1:
        loss, (grad_w, grad_x) = one_microbatch(per_example, given["loss_target"])
    else:
        def body(carry, xs):
            loss_sum, grad_sum = carry
            l_k, (gw_k, gx_k) = one_microbatch(xs[0], xs[1])
            with _jax.named_scope("update"):
                return (loss_sum + l_k, _jax.tree.map(_jnp.add, grad_sum, gw_k)), gx_k

        init = (_jnp.zeros((), _jnp.float32), _jax.tree.map(_jnp.zeros_like, weights))
        (loss, grad_w), grad_x = _jax.lax.scan(body, init, (per_example, given["loss_target"]))
    with _jax.named_scope("update"):
        delta_w, new_m, new_v = {}, {}, {}
        for n in TWIN_WEIGHTS:
            delta_w[n], new_m[n], new_v[n] = _adamw(weights[n], grad_w[n], given["m_" + n], given["v_" + n])
    return (loss, grad_x, *[grad_w[n] for n in TWIN_WEIGHTS], *[delta_w[n] for n in TWIN_WEIGHTS],
            *[new_m[n] for n in TWIN_WEIGHTS], *[new_v[n] for n in TWIN_WEIGHTS])
```

```python
import functools

import jax
import jax.numpy as jnp
from jax import lax
from jax.experimental import pallas as pl
from jax.experimental.pallas import tpu as pltpu

F32 = jnp.float32
BF16 = jnp.bfloat16
HIGHEST = lax.Precision.HIGHEST
MESH = pl.DeviceIdType.MESH
ANY = pl.BlockSpec(memory_space=pl.ANY)

N_DEV = 8
D_MODEL = 1024
HEAD_DIM = 64
TOK_WIDTH = 768
MEM_WIDTH = 256
CHUNK = 64
Q_BLOCK = 256
KEY_WIN = 768
BAND = 576
N_REL = 192
CONV_W = 31
CONV_HALO = 32
NORM_EPS = 1e-6
NEG_INF = -1e30
ATTN_SCALE = HEAD_DIM ** -0.5
LANES = 128
ROW_TILE = 512
VMEM_LIMIT = 56 * 1024 * 1024

ADAM_LR, ADAM_B1, ADAM_B2, ADAM_EPS, ADAM_WD, ADAM_STEP = 0.001, 0.9, 0.999, 1e-08, 0.01, 10


def _params(*sem):
    return pltpu.CompilerParams(dimension_semantics=sem, vmem_limit_bytes=VMEM_LIMIT)


def _row_tile(m):
    return ROW_TILE if m % ROW_TILE == 0 else m


def _col_tile(n, cap=1408):
    best = None
    for t in range(LANES, min(n, cap) + 1, LANES):
        if n % t == 0:
            best = t
    return best if best is not None else n


def _dot(a, b, ca, cb):
    return lax.dot_general(a, b, (((ca,), (cb,)), ((), ())), preferred_element_type=F32)


def _sigmoid(x):
    return 1.0 / (1.0 + jnp.exp(-x))


def mm_nt(a, b, bias=None, out_dtype=BF16, name="mm_nt"):
    m, k = a.shape
    n = b.shape[0]
    tm, tn = _row_tile(m), _col_tile(n)

    def body(*refs):
        a_ref, b_ref = refs[0], refs[1]
        o_ref = refs[-1]
        acc = _dot(a_ref[...].astype(BF16), b_ref[...].astype(BF16), 1, 1)
        if bias is not None:
            acc = acc + refs[2][...]
        o_ref[...] = acc.astype(o_ref.dtype)

    in_specs = [pl.BlockSpec((tm, k), lambda i, j: (i, 0)), pl.BlockSpec((tn, k), lambda i, j: (j, 0))]
    args = [a, b]
    if bias is not None:
        in_specs.append(pl.BlockSpec((1, tn), lambda i, j: (0, j)))
        args.append(bias)
    return pl.pallas_call(
        body, out_shape=jax.ShapeDtypeStruct((m, n), out_dtype), grid=(m // tm, n // tn),
        in_specs=in_specs, out_specs=pl.BlockSpec((tm, tn), lambda i, j: (i, j)),
        compiler_params=_params("parallel", "arbitrary"), name=name)(*args)


def mm_nn(a, b, res=None, out_dtype=F32, name="mm_nn"):
    m, k = a.shape
    n = b.shape[1]
    tm, tn = _row_tile(m), _col_tile(n, 1024)

    def body(*refs):
        a_ref, b_ref = refs[0], refs[1]
        o_ref = refs[-1]
        acc = _dot(a_ref[...].astype(BF16), b_ref[...].astype(BF16), 1, 0)
        if res is not None:
            acc = acc + refs[2][...]
        o_ref[...] = acc.astype(o_ref.dtype)

    in_specs = [pl.BlockSpec((tm, k), lambda i, j: (i, 0)), pl.BlockSpec((k, tn), lambda i, j: (0, j))]
    args = [a, b]
    if res is not None:
        in_specs.append(pl.BlockSpec((tm, tn), lambda i, j: (i, j)))
        args.append(res)
    return pl.pallas_call(
        body, out_shape=jax.ShapeDtypeStruct((m, n), out_dtype), grid=(m // tm, n // tn),
        in_specs=in_specs, out_specs=pl.BlockSpec((tm, tn), lambda i, j: (i, j)),
        compiler_params=_params("parallel", "arbitrary"), name=name)(*args)


def mm2_nn(a1, b1, a2, b2, name="mm2_nn"):
    m, k = a1.shape
    n = b1.shape[1]
    tm, tn = _row_tile(m), _col_tile(n, 512)

    def body(a1_ref, b1_ref, a2_ref, b2_ref, o_ref):
        o_ref[...] = _dot(a1_ref[...], b1_ref[...], 1, 0) + _dot(a2_ref[...], b2_ref[...], 1, 0)

    a_spec = pl.BlockSpec((tm, k), lambda i, j: (i, 0))
    b_spec = pl.BlockSpec((k, tn), lambda i, j: (0, j))
    return pl.pallas_call(
        body, out_shape=jax.ShapeDtypeStruct((m, n), F32), grid=(m // tm, n // tn),
        in_specs=[a_spec, b_spec, a_spec, b_spec], out_specs=pl.BlockSpec((tm, tn), lambda i, j: (i, j)),
        compiler_params=_params("parallel", "arbitrary"), name=name)(a1, b1, a2, b2)


def mm_tn(a, b, out_dtype=BF16, name="mm_tn"):
    t, r = a.shape
    c = b.shape[1]
    tt, tr = _row_tile(t), _col_tile(r)
    nt = t // tt

    def body(a_ref, b_ref, o_ref, acc_ref):
        step = pl.program_id(1)

        @pl.when(step == 0)
        def _():
            acc_ref[...] = jnp.zeros_like(acc_ref)

        acc_ref[...] += _dot(a_ref[...].astype(BF16), b_ref[...].astype(BF16), 0, 0)

        @pl.when(step == nt - 1)
        def _():
            o_ref[...] = acc_ref[...].astype(o_ref.dtype)

    return pl.pallas_call(
        body, out_shape=jax.ShapeDtypeStruct((r, c), out_dtype), grid=(r // tr, nt),
        in_specs=[pl.BlockSpec((tt, tr), lambda i, s: (s, i)), pl.BlockSpec((tt, c), lambda i, s: (s, 0))],
        out_specs=pl.BlockSpec((tr, c), lambda i, s: (i, 0)),
        scratch_shapes=[pltpu.VMEM((tr, c), F32)],
        compiler_params=_params("parallel", "arbitrary"), name=name)(a, b)


def rms_fwd(x, g, name="rms_fwd"):
    n, d = x.shape
    tm = _row_tile(n)

    def body(x_ref, g_ref, o_ref):
        xv = x_ref[...]
        r = lax.rsqrt(jnp.mean(xv * xv, axis=-1, keepdims=True) + NORM_EPS)
        o_ref[...] = (xv * r * g_ref[...]).astype(o_ref.dtype)

    return pl.pallas_call(
        body, out_shape=jax.ShapeDtypeStruct((n, d), BF16), grid=(n // tm,),
        in_specs=[pl.BlockSpec((tm, d), lambda i: (i, 0)), pl.BlockSpec((1, d), lambda i: (0, 0))],
        out_specs=pl.BlockSpec((tm, d), lambda i: (i, 0)),
        compiler_params=_params("parallel"), name=name)(x, g)


def rms_bwd(dh, x, g, dres, name="rms_bwd"):
    n, d = x.shape
    tm = _row_tile(n)

    def body(dh_ref, x_ref, g_ref, dres_ref, dx_ref, dg_ref):
        @pl.when(pl.program_id(0) == 0)
        def _():
            dg_ref[...] = jnp.zeros_like(dg_ref)

        xv = x_ref[...]
        dhv = dh_ref[...].astype(F32)
        r = lax.rsqrt(jnp.mean(xv * xv, axis=-1, keepdims=True) + NORM_EPS)
        xhat = xv * r
        dg_ref[...] += jnp.sum(dhv * xhat, axis=0, keepdims=True)
        dxhat = dhv * g_ref[...]
        mean_t = jnp.mean(dxhat * xhat, axis=-1, keepdims=True)
        dx_ref[...] = dres_ref[...] + r * (dxhat - xhat * mean_t)

    row = pl.BlockSpec((tm, d), lambda i: (i, 0))
    vec = pl.BlockSpec((1, d), lambda i: (0, 0))
    return pl.pallas_call(
        body, out_shape=(jax.ShapeDtypeStruct((n, d), F32), jax.ShapeDtypeStruct((1, d), F32)), grid=(n // tm,),
        in_specs=[row, row, vec, row], out_specs=(row, vec),
        compiler_params=_params("arbitrary"), name=name)(dh, x, g, dres)


def gate_up(h2, wg_t, wu_t, name="gate_up"):
    n, d = h2.shape
    f = wg_t.shape[0]
    tm, tn = _row_tile(n), _col_tile(f)

    def body(h_ref, wg_ref, wu_ref, g_ref, u_ref, a_ref):
        hv = h_ref[...]
        gv = _dot(hv, wg_ref[...], 1, 1)
        uv = _dot(hv, wu_ref[...], 1, 1)
        g_ref[...] = gv.astype(BF16)
        u_ref[...] = uv.astype(BF16)
        a_ref[...] = (gv * _sigmoid(gv) * uv).astype(BF16)

    w_spec = pl.BlockSpec((tn, d), lambda i, j: (j, 0))
    o_spec = pl.BlockSpec((tm, tn), lambda i, j: (i, j))
    o_shape = jax.ShapeDtypeStruct((n, f), BF16)
    return pl.pallas_call(
        body, out_shape=(o_shape, o_shape, o_shape), grid=(n // tm, f // tn),
        in_specs=[pl.BlockSpec((tm, d), lambda i, j: (i, 0)), w_spec, w_spec], out_specs=(o_spec, o_spec, o_spec),
        compiler_params=_params("parallel", "arbitrary"), name=name)(h2, wg_t, wu_t)


def ffn_bwd_act(dx, wd, gate, up, name="ffn_bwd_act"):
    n, d = dx.shape
    f = wd.shape[0]
    tm, tn = _row_tile(n), _col_tile(f)

    def body(dx_ref, wd_ref, g_ref, u_ref, dg_ref, du_ref):
        dact = _dot(dx_ref[...].astype(BF16), wd_ref[...], 1, 1)
        gv = g_ref[...].astype(F32)
        uv = u_ref[...].astype(F32)
        sg = _sigmoid(gv)
        dg_ref[...] = (dact * uv * sg * (1.0 + gv * (1.0 - sg))).astype(BF16)
        du_ref[...] = (dact * gv * sg).astype(BF16)

    t_spec = pl.BlockSpec((tm, tn), lambda i, j: (i, j))
    o_shape = jax.ShapeDtypeStruct((n, f), BF16)
    return pl.pallas_call(
        body, out_shape=(o_shape, o_shape), grid=(n // tm, f // tn),
        in_specs=[pl.BlockSpec((tm, d), lambda i, j: (i, 0)), pl.BlockSpec((tn, d), lambda i, j: (j, 0)), t_spec, t_spec],
        out_specs=(t_spec, t_spec),
        compiler_params=_params("parallel", "arbitrary"), name=name)(dx, wd, gate, up)


def _group_masks(width):
    lane = lax.broadcasted_iota(jnp.int32, (1, width), 1)
    return [(lane >= HEAD_DIM * g) & (lane < HEAD_DIM * (g + 1)) for g in range(width // HEAD_DIM)]


def _group_sum(x, masks):
    out = jnp.zeros_like(x)
    for msk in masks:
        s = jnp.sum(jnp.where(msk, x, 0.0), axis=-1, keepdims=True)
        out = jnp.where(msk, s, out)
    return out


def _head_norm(x, gain, masks):
    r = lax.rsqrt(_group_sum(x * x, masks) * (1.0 / HEAD_DIM) + NORM_EPS)
    xhat = x * r
    return xhat * gain, xhat, r


def _head_norm_bwd(dxn, xhat, r, gain, masks):
    dgain = jnp.sum(dxn * xhat, axis=0, keepdims=True)
    dxhat = dxn * gain
    mean_t = _group_sum(dxhat * xhat, masks) * (1.0 / HEAD_DIM)
    return r * (dxhat - xhat * mean_t), dgain


def _softmax_rows(s):
    e = jnp.exp(s - jnp.max(s, axis=-1, keepdims=True))
    return e * (1.0 / jnp.sum(e, axis=-1, keepdims=True))


def _rel_onehot():
    col = lax.broadcasted_iota(jnp.int32, (1, KEY_WIN), 1)
    off = jnp.where(col < KEY_WIN - LANES, col, col - KEY_WIN)
    idx = jnp.clip(8 * CHUNK - off, -(CHUNK - 1), LANES) + (CHUNK - 1)
    return (lax.broadcasted_iota(jnp.int32, (N_REL, KEY_WIN), 0) == idx).astype(F32)


def bias_blocks(rel16):
    heads = TOK_WIDTH // HEAD_DIM

    def body(rel_ref, o_ref, u_ref):
        u_ref[...] = jnp.dot(rel_ref[...], _rel_onehot(), precision=HIGHEST, preferred_element_type=F32)
        row = lax.broadcasted_iota(jnp.int32, (CHUNK, KEY_WIN), 0)
        col = lax.broadcasted_iota(jnp.int32, (CHUNK, KEY_WIN), 1)
        for h in range(heads):
            xv = jnp.broadcast_to(u_ref[h:h + 1, :], (CHUNK, KEY_WIN))
            for b in range(6):
                xv = jnp.where(((row >> b) & 1) == 1, pltpu.roll(xv, 1 << b, axis=1), xv)
            xv = jnp.where(col < BAND, xv, NEG_INF)
            for i in range(Q_BLOCK // CHUNK):
                o_ref[h, CHUNK * i:CHUNK * (i + 1), :] = pltpu.roll(xv, CHUNK * i, axis=1) if i else xv

    return pl.pallas_call(
        body, out_shape=jax.ShapeDtypeStruct((heads, Q_BLOCK, KEY_WIN), F32),
        scratch_shapes=[pltpu.VMEM((16, KEY_WIN), F32)], name="bias_blocks")(rel16)


def bias_grad(dbias):
    heads = dbias.shape[0]

    def body(db_ref, o_ref, y_ref):
        y_ref[...] = jnp.zeros_like(y_ref)
        row = lax.broadcasted_iota(jnp.int32, (CHUNK, KEY_WIN), 0)
        for h in range(heads):
            fv = db_ref[h, 0:CHUNK, :]
            for i in range(1, Q_BLOCK // CHUNK):
                fv = fv + pltpu.roll(db_ref[h, CHUNK * i:CHUNK * (i + 1), :], KEY_WIN - CHUNK * i, axis=1)
            for b in range(6):
                fv = jnp.where(((row >> b) & 1) == 1, pltpu.roll(fv, KEY_WIN - (1 << b), axis=1), fv)
            y_ref[h:h + 1, :] = jnp.sum(fv, axis=0, keepdims=True)
        o_ref[...] = lax.dot_general(y_ref[...], _rel_onehot(), (((1,), (1,)), ((), ())),
                                     precision=HIGHEST, preferred_element_type=F32)

    return pl.pallas_call(
        body, out_shape=jax.ShapeDtypeStruct((16, N_REL), F32),
        scratch_shapes=[pltpu.VMEM((16, KEY_WIN), F32)], name="bias_grad")(dbias)


def _attn_windows(seq):
    out = []
    for j in range(seq // Q_BLOCK):
        r0 = j * Q_BLOCK
        k0 = max(0, r0 - 8 * CHUNK)
        width = r0 + Q_BLOCK - k0
        out.append((r0, k0, width, KEY_WIN - width))
    return out


def attn_fwd(z, gq2, gk2, bias, batch, seq):
    n = z.shape[0]
    pairs = TOK_WIDTH // LANES

    def body(q_ref, k_ref, v_ref, gq_ref, gk_ref, b_ref, o_ref, qn_s, kn_s):
        masks = _group_masks(LANES)
        qn_s[...] = _head_norm(q_ref[...].astype(F32), gq_ref[...], masks)[0].astype(BF16)
        kn_s[...] = _head_norm(k_ref[...].astype(F32), gk_ref[...], masks)[0].astype(BF16)
        for r0, k0, width, c0 in _attn_windows(seq):
            qb = qn_s[r0:r0 + Q_BLOCK, :]
            kw = kn_s[k0:k0 + width, :]
            vw = v_ref[k0:k0 + width, :]
            out = jnp.zeros((Q_BLOCK, LANES), F32)
            for h, msk in enumerate(masks):
                qh = jnp.where(msk, qb, jnp.zeros_like(qb))
                s = _dot(qh, kw, 1, 1) * ATTN_SCALE + b_ref[h, :, c0:KEY_WIN]
                p = _softmax_rows(s).astype(BF16)
                out = jnp.where(msk, _dot(p, vw, 1, 0), out)
            o_ref[r0:r0 + Q_BLOCK, :] = out.astype(o_ref.dtype)

    def col(off):
        return pl.BlockSpec((seq, LANES), lambda b, p: (b, off + p))

    vec = pl.BlockSpec((1, LANES), lambda b, p: (0, 0))
    return pl.pallas_call(
        body, out_shape=jax.ShapeDtypeStruct((n, TOK_WIDTH), BF16), grid=(batch, pairs),
        in_specs=[col(0), col(pairs), col(2 * pairs), vec, vec,
                  pl.BlockSpec((2, Q_BLOCK, KEY_WIN), lambda b, p: (p, 0, 0))],
        out_specs=pl.BlockSpec((seq, LANES), lambda b, p: (b, p)),
        scratch_shapes=[pltpu.VMEM((seq, LANES), BF16), pltpu.VMEM((seq, LANES), BF16)],
        compiler_params=_params("parallel", "arbitrary"), name="attn_fwd")(z, z, z, gq2, gk2, bias)


def attn_bwd(z, dcat, gq2, gk2, bias, batch, seq):
    n = z.shape[0]
    pairs = TOK_WIDTH // LANES

    def body(q_ref, k_ref, v_ref, do_ref, gq_ref, gk_ref, b_ref,
             dq_ref, dk_ref, dv_ref, db_ref, dgq_ref, dgk_ref, qn_s, kn_s, dqn_s, dkn_s, dv_s):
        bi, pi = pl.program_id(1), pl.program_id(0)
        masks = _group_masks(LANES)

        @pl.when(bi == 0)
        def _():
            db_ref[...] = jnp.zeros_like(db_ref)

        @pl.when((bi == 0) & (pi == 0))
        def _():
            dgq_ref[...] = jnp.zeros_like(dgq_ref)
            dgk_ref[...] = jnp.zeros_like(dgk_ref)

        qn, qhat, rq = _head_norm(q_ref[...].astype(F32), gq_ref[...], masks)
        kn, khat, rk = _head_norm(k_ref[...].astype(F32), gk_ref[...], masks)
        qn_s[...] = qn.astype(BF16)
        kn_s[...] = kn.astype(BF16)
        dkn_s[...] = jnp.zeros_like(dkn_s)
        dv_s[...] = jnp.zeros_like(dv_s)
        for r0, k0, width, c0 in _attn_windows(seq):
            qb = qn_s[r0:r0 + Q_BLOCK, :]
            dob = do_ref[r0:r0 + Q_BLOCK, :]
            kw = kn_s[k0:k0 + width, :]
            vw = v_ref[k0:k0 + width, :]
            dq_acc = jnp.zeros((Q_BLOCK, LANES), F32)
            dk_acc = jnp.zeros((width, LANES), F32)
            dv_acc = jnp.zeros((width, LANES), F32)
            for h, msk in enumerate(masks):
                qh = jnp.where(msk, qb, jnp.zeros_like(qb))
                doh = jnp.where(msk, dob, jnp.zeros_like(dob))
                s = _dot(qh, kw, 1, 1) * ATTN_SCALE + b_ref[h, :, c0:KEY_WIN]
                p = _softmax_rows(s)
                dp = _dot(doh, vw, 1, 1)
                ds = p * (dp - jnp.sum(p * dp, axis=-1, keepdims=True))
                db_ref[h, :, c0:KEY_WIN] += ds
                dsb = (ds * ATTN_SCALE).astype(BF16)
                dq_acc = jnp.where(msk, _dot(dsb, kw, 1, 0), dq_acc)
                dk_acc = jnp.where(msk, _dot(dsb, qb, 0, 0), dk_acc)
                dv_acc = jnp.where(msk, _dot(p.astype(BF16), dob, 0, 0), dv_acc)
            dqn_s[r0:r0 + Q_BLOCK, :] = dq_acc
            dkn_s[k0:k0 + width, :] += dk_acc
            dv_s[k0:k0 + width, :] += dv_acc
        dq, dgq = _head_norm_bwd(dqn_s[...], qhat, rq, gq_ref[...], masks)
        dk, dgk = _head_norm_bwd(dkn_s[...], khat, rk, gk_ref[...], masks)
        dq_ref[...] = dq.astype(dq_ref.dtype)
        dk_ref[...] = dk.astype(dk_ref.dtype)
        dv_ref[...] = dv_s[...].astype(dv_ref.dtype)
        dgq_ref[...] += dgq
        dgk_ref[...] += dgk

    def col(off):
        return pl.BlockSpec((seq, LANES), lambda p, b: (b, off + p))

    vec = pl.BlockSpec((1, LANES), lambda p, b: (0, 0))
    blk = pl.BlockSpec((2, Q_BLOCK, KEY_WIN), lambda p, b: (p, 0, 0))
    o_shape = jax.ShapeDtypeStruct((n, TOK_WIDTH), BF16)
    v_shape = jax.ShapeDtypeStruct((1, LANES), F32)
    return pl.pallas_call(
        body,
        out_shape=(o_shape, o_shape, o_shape, jax.ShapeDtypeStruct(bias.shape, F32), v_shape, v_shape),
        grid=(pairs, batch),
        in_specs=[col(0), col(pairs), col(2 * pairs), col(0), vec, vec, blk],
        out_specs=(col(0), col(0), col(0), blk, vec, vec),
        scratch_shapes=[pltpu.VMEM((seq, LANES), BF16), pltpu.VMEM((seq, LANES), BF16),
                        pltpu.VMEM((seq, LANES), F32), pltpu.VMEM((seq, LANES), F32), pltpu.VMEM((seq, LANES), F32)],
        compiler_params=_params("arbitrary", "arbitrary"), name="attn_bwd")(z, z, z, dcat, gq2, gk2, bias)


MEM_ROWS = 512


def memattn_fwd(z, kv, gq4, gk4, batch, seq, qcol, name):
    n = z.shape[0]
    mtok = kv.shape[0] // batch
    rows = min(MEM_ROWS, seq)

    def body(q_ref, kv_ref, gq_ref, gk_ref, o_ref):
        masks = _group_masks(MEM_WIDTH)
        kn = _head_norm(kv_ref[:, 0:MEM_WIDTH], gk_ref[...], masks)[0].astype(BF16)
        vm = kv_ref[:, MEM_WIDTH:2 * MEM_WIDTH].astype(BF16)
        for t in range(seq // rows):
            sl = slice(t * rows, (t + 1) * rows)
            qn = _head_norm(q_ref[sl, :].astype(F32), gq_ref[...], masks)[0].astype(BF16)
            out = jnp.zeros((rows, MEM_WIDTH), F32)
            for msk in masks:
                qh = jnp.where(msk, qn, jnp.zeros_like(qn))
                p = _softmax_rows(_dot(qh, kn, 1, 1) * ATTN_SCALE).astype(BF16)
                out = jnp.where(msk, _dot(p, vm, 1, 0), out)
            o_ref[sl, :] = out.astype(o_ref.dtype)

    vec = pl.BlockSpec((1, MEM_WIDTH), lambda b: (0, 0))
    return pl.pallas_call(
        body, out_shape=jax.ShapeDtypeStruct((n, MEM_WIDTH), BF16), grid=(batch,),
        in_specs=[pl.BlockSpec((seq, MEM_WIDTH), lambda b: (b, qcol)),
                  pl.BlockSpec((mtok, 2 * MEM_WIDTH), lambda b: (b, 0)), vec, vec],
        out_specs=pl.BlockSpec((seq, MEM_WIDTH), lambda b: (b, 0)),
        compiler_params=_params("parallel"), name=name)(z, kv, gq4, gk4)


def memattn_bwd(z, kv, dcat, gq4, gk4, batch, seq, qcol, name):
    n = z.shape[0]
    mtok = kv.shape[0] // batch
    rows = min(MEM_ROWS, seq)

    def body(q_ref, kv_ref, do_ref, gq_ref, gk_ref, dq_ref, dkv_ref, dgq_ref, dgk_ref):
        @pl.when(pl.program_id(0) == 0)
        def _():
            dgq_ref[...] = jnp.zeros_like(dgq_ref)
            dgk_ref[...] = jnp.zeros_like(dgk_ref)

        masks = _group_masks(MEM_WIDTH)
        kn_f, khat, rk = _head_norm(kv_ref[:, 0:MEM_WIDTH], gk_ref[...], masks)
        kn = kn_f.astype(BF16)
        vm = kv_ref[:, MEM_WIDTH:2 * MEM_WIDTH].astype(BF16)
        dkn = jnp.zeros((mtok, MEM_WIDTH), F32)
        dvm = jnp.zeros((mtok, MEM_WIDTH), F32)
        dgq = jnp.zeros((1, MEM_WIDTH), F32)
        for t in range(seq // rows):
            sl = slice(t * rows, (t + 1) * rows)
            qn_f, qhat, rq = _head_norm(q_ref[sl, :].astype(F32), gq_ref[...], masks)
            qn = qn_f.astype(BF16)
            dob = do_ref[sl, :]
            dqn = jnp.zeros((rows, MEM_WIDTH), F32)
            for msk in masks:
                qh = jnp.where(msk, qn, jnp.zeros_like(qn))
                doh = jnp.where(msk, dob, jnp.zeros_like(dob))
                p = _softmax_rows(_dot(qh, kn, 1, 1) * ATTN_SCALE)
                dp = _dot(doh, vm, 1, 1)
                ds = p * (dp - jnp.sum(p * dp, axis=-1, keepdims=True))
                dsb = (ds * ATTN_SCALE).astype(BF16)
                dqn = jnp.where(msk, _dot(dsb, kn, 1, 0), dqn)
                dkn = dkn + jnp.where(msk, _dot(dsb, qn, 0, 0), 0.0)
                dvm = dvm + jnp.where(msk, _dot(p.astype(BF16), dob, 0, 0), 0.0)
            dq, dg = _head_norm_bwd(dqn, qhat, rq, gq_ref[...], masks)
            dq_ref[sl, :] = dq.astype(dq_ref.dtype)
            dgq = dgq + dg
        dk, dgk = _head_norm_bwd(dkn, khat, rk, gk_ref[...], masks)
        dkv_ref[:, 0:MEM_WIDTH] = dk
        dkv_ref[:, MEM_WIDTH:2 * MEM_WIDTH] = dvm
        dgq_ref[...] += dgq
        dgk_ref[...] += dgk

    vec = pl.BlockSpec((1, MEM_WIDTH), lambda b: (0, 0))
    kv_spec = pl.BlockSpec((mtok, 2 * MEM_WIDTH), lambda b: (b, 0))
    v_shape = jax.ShapeDtypeStruct((1, MEM_WIDTH), F32)
    return pl.pallas_call(
        body,
        out_shape=(jax.ShapeDtypeStruct((n, MEM_WIDTH), BF16), jax.ShapeDtypeStruct(kv.shape, F32), v_shape, v_shape),
        grid=(batch,),
        in_specs=[pl.BlockSpec((seq, MEM_WIDTH), lambda b: (b, qcol)), kv_spec,
                  pl.BlockSpec((seq, MEM_WIDTH), lambda b: (b, TOK_WIDTH // MEM_WIDTH)), vec, vec],
        out_specs=(pl.BlockSpec((seq, MEM_WIDTH), lambda b: (b, 0)), kv_spec, vec, vec),
        compiler_params=_params("arbitrary"), name=name)(z, kv, dcat, gq4, gk4)


CONV_ROWS = 256


def _glu(a_ref, g_ref):
    return a_ref[...].astype(F32) * _sigmoid(g_ref[...].astype(F32))


def _layer_norm_stats(y):
    mu = jnp.mean(y, axis=-1, keepdims=True)
    yc = y - mu
    rstd = lax.rsqrt(jnp.mean(yc * yc, axis=-1, keepdims=True) + NORM_EPS)
    return yc * rstd, rstd


def conv_fwd(z, cw, cb, lg, lb, batch, seq):
    n = z.shape[0]
    nt = seq // CONV_ROWS
    sub = CONV_ROWS // CONV_HALO
    lead = CONV_HALO - (CONV_W - 1)

    def body(a_ref, g_ref, ap_ref, gp_ref, cw_ref, cb_ref, lg_ref, lb_ref, o_ref, win):
        first = pl.program_id(1) == 0
        win[0:CONV_HALO, :] = jnp.where(first, 0.0, _glu(ap_ref, gp_ref))
        win[CONV_HALO:CONV_HALO + CONV_ROWS, :] = _glu(a_ref, g_ref)
        y = jnp.zeros((CONV_ROWS, TOK_WIDTH), F32) + cb_ref[...]
        for w in range(CONV_W):
            y = y + win[lead + w:lead + w + CONV_ROWS, :] * cw_ref[w:w + 1, :]
        yh, _ = _layer_norm_stats(y)
        t = yh * lg_ref[...] + lb_ref[...]
        o_ref[...] = (t * _sigmoid(t)).astype(o_ref.dtype)

    def cur(c):
        return pl.BlockSpec((CONV_ROWS, TOK_WIDTH), lambda b, i: (b * nt + i, c))

    def prev(c):
        return pl.BlockSpec((CONV_HALO, TOK_WIDTH), lambda b, i: (jnp.maximum((b * nt + i) * sub - 1, 0), c))

    vec = pl.BlockSpec((1, TOK_WIDTH), lambda b, i: (0, 0))
    return pl.pallas_call(
        body, out_shape=jax.ShapeDtypeStruct((n, TOK_WIDTH), BF16), grid=(batch, nt),
        in_specs=[cur(0), cur(1), prev(0), prev(1), pl.BlockSpec((32, TOK_WIDTH), lambda b, i: (0, 0)), vec, vec, vec],
        out_specs=pl.BlockSpec((CONV_ROWS, TOK_WIDTH), lambda b, i: (b * nt + i, 0)),
        scratch_shapes=[pltpu.VMEM((CONV_HALO + CONV_ROWS, TOK_WIDTH), F32)],
        compiler_params=_params("parallel", "arbitrary"), name="conv_fwd")(z, z, z, z, cw, cb, lg, lb)


def conv_bwd(z, dcat, cw, cb, lg, lb, batch, seq):
    n = z.shape[0]
    nt = seq // CONV_ROWS
    sub = CONV_ROWS // CONV_HALO
    lead = CONV_HALO - (CONV_W - 1)
    ext = CONV_ROWS + CONV_HALO
    last_blk = n // CONV_HALO - 1

    def body(a_ref, g_ref, ap_ref, gp_ref, an_ref, gn_ref, do_ref, don_ref, cw_ref, cb_ref, lg_ref, lb_ref,
             da_ref, dg_ref, dcw_ref, dsm_ref, win, dyw):
        b, i = pl.program_id(0), pl.program_id(1)
        first, last = i == 0, i == nt - 1

        @pl.when((b == 0) & (i == 0))
        def _():
            dcw_ref[...] = jnp.zeros_like(dcw_ref)
            dsm_ref[...] = jnp.zeros_like(dsm_ref)

        av = a_ref[...].astype(F32)
        sg = _sigmoid(g_ref[...].astype(F32))
        win[0:CONV_HALO, :] = jnp.where(first, 0.0, _glu(ap_ref, gp_ref))
        win[CONV_HALO:CONV_HALO + CONV_ROWS, :] = av * sg
        win[CONV_HALO + CONV_ROWS:, :] = jnp.where(last, 0.0, _glu(an_ref, gn_ref))
        y = jnp.zeros((ext, TOK_WIDTH), F32) + cb_ref[...]
        for w in range(CONV_W):
            y = y + win[lead + w:lead + w + ext, :] * cw_ref[w:w + 1, :]
        yh, rstd = _layer_norm_stats(y)
        t = yh * lg_ref[...] + lb_ref[...]
        st = _sigmoid(t)
        dout = jnp.concatenate(
            [do_ref[...].astype(F32), jnp.where(last, 0.0, don_ref[...].astype(F32))], axis=0)
        dt = dout * st * (1.0 + t * (1.0 - st))
        dyh = dt * lg_ref[...]
        dy = rstd * (dyh - jnp.mean(dyh, axis=-1, keepdims=True) - yh * jnp.mean(dyh * yh, axis=-1, keepdims=True))
        dyw[...] = dy
        dsm_ref[0:1, :] += jnp.sum(dy[0:CONV_ROWS], axis=0, keepdims=True)
        dsm_ref[1:2, :] += jnp.sum((dt * yh)[0:CONV_ROWS], axis=0, keepdims=True)
        dsm_ref[2:3, :] += jnp.sum(dt[0:CONV_ROWS], axis=0, keepdims=True)
        dyc = dyw[0:CONV_ROWS, :]
        dglu = jnp.zeros((CONV_ROWS, TOK_WIDTH), F32)
        for w in range(CONV_W):
            dcw_ref[w:w + 1, :] += jnp.sum(dyc * win[lead + w:lead + w + CONV_ROWS, :], axis=0, keepdims=True)
            back = CONV_W - 1 - w
            dglu = dglu + dyw[back:back + CONV_ROWS, :] * cw_ref[w:w + 1, :]
        da_ref[...] = (dglu * sg).astype(da_ref.dtype)
        dg_ref[...] = (dglu * av * sg * (1.0 - sg)).astype(dg_ref.dtype)

    def cur(c):
        return pl.BlockSpec((CONV_ROWS, TOK_WIDTH), lambda b, i: (b * nt + i, c))

    def prev(c):
        return pl.BlockSpec((CONV_HALO, TOK_WIDTH), lambda b, i: (jnp.maximum((b * nt + i) * sub - 1, 0), c))

    def nxt(c):
        return pl.BlockSpec((CONV_HALO, TOK_WIDTH), lambda b, i: (jnp.minimum((b * nt + i + 1) * sub, last_blk), c))

    vec = pl.BlockSpec((1, TOK_WIDTH), lambda b, i: (0, 0))
    full32 = pl.BlockSpec((32, TOK_WIDTH), lambda b, i: (0, 0))
    o_shape = jax.ShapeDtypeStruct((n, TOK_WIDTH), BF16)
    return pl.pallas_call(
        body,
        out_shape=(o_shape, o_shape, jax.ShapeDtypeStruct((32, TOK_WIDTH), F32), jax.ShapeDtypeStruct((8, TOK_WIDTH), F32)),
        grid=(batch, nt),
        in_specs=[cur(0), cur(1), prev(0), prev(1), nxt(0), nxt(1), cur(0), nxt(0), full32, vec, vec, vec],
        out_specs=(cur(0), cur(0), full32, pl.BlockSpec((8, TOK_WIDTH), lambda b, i: (0, 0))),
        scratch_shapes=[pltpu.VMEM((CONV_HALO + ext, TOK_WIDTH), F32), pltpu.VMEM((ext, TOK_WIDTH), F32)],
        compiler_params=_params("arbitrary", "arbitrary"), name="conv_bwd")(
            z, z, z, z, z, z, dcat, dcat, cw, cb, lg, lb)


def loss_head(y, target):
    n, d = y.shape
    tm = _row_tile(n)
    nt = n // tm

    def body(y_ref, t_ref, dy_ref, l_ref, acc_ref):
        i = pl.program_id(0)

        @pl.when(i == 0)
        def _():
            acc_ref[...] = jnp.zeros_like(acc_ref)

        err = y_ref[...] - t_ref[...]
        dy_ref[...] = err * (1.0 / d)
        acc_ref[...] += jnp.sum(err * err, axis=0, keepdims=True)

        @pl.when(i == nt - 1)
        def _():
            total = jnp.sum(acc_ref[...], axis=-1, keepdims=True) * (0.5 / d)
            l_ref[...] = jnp.broadcast_to(total, l_ref.shape)

    row = pl.BlockSpec((tm, d), lambda i: (i, 0))
    return pl.pallas_call(
        body, out_shape=(jax.ShapeDtypeStruct((n, d), F32), jax.ShapeDtypeStruct((8, LANES), F32)), grid=(nt,),
        in_specs=[row, row], out_specs=(row, pl.BlockSpec((8, LANES), lambda i: (0, 0))),
        scratch_shapes=[pltpu.VMEM((1, d), F32)],
        compiler_params=_params("arbitrary"), name="loss_head")(y, target)


def col_sum(x, name="col_sum"):
    n, c = x.shape
    tm = _row_tile(n)

    def body(x_ref, o_ref):
        @pl.when(pl.program_id(0) == 0)
        def _():
            o_ref[...] = jnp.zeros_like(o_ref)

        o_ref[...] += jnp.sum(x_ref[...].astype(F32), axis=0, keepdims=True)

    return pl.pallas_call(
        body, out_shape=jax.ShapeDtypeStruct((1, c), F32), grid=(n // tm,),
        in_specs=[pl.BlockSpec((tm, c), lambda i: (i, 0))], out_specs=pl.BlockSpec((1, c), lambda i: (0, 0)),
        compiler_params=_params("arbitrary"), name=name)(x)


def adamw(w, g, m, v, name="adamw"):
    rows, cols = w.shape
    tr = rows
    for cand in (512, 256, 128, 64, 32, 16, 8):
        if rows % cand == 0 and rows > cand:
            tr = cand
            break
    c1 = 1.0 / (1.0 - ADAM_B1 ** ADAM_STEP)
    c2 = 1.0 / (1.0 - ADAM_B2 ** ADAM_STEP)

    def body(w_ref, g_ref, m_ref, v_ref, d_ref, nm_ref, nv_ref):
        gv = g_ref[...]
        nm = ADAM_B1 * m_ref[...] + (1.0 - ADAM_B1) * gv
        nv = ADAM_B2 * v_ref[...] + (1.0 - ADAM_B2) * (gv * gv)
        nm_ref[...] = nm
        nv_ref[...] = nv
        d_ref[...] = -ADAM_LR * ((nm * c1) / (jnp.sqrt(nv * c2) + ADAM_EPS) + ADAM_WD * w_ref[...])

    spec = pl.BlockSpec((tr, cols), lambda i: (i, 0))
    shape = jax.ShapeDtypeStruct((rows, cols), F32)
    return pl.pallas_call(
        body, out_shape=(shape, shape, shape), grid=(rows // tr,),
        in_specs=[spec, spec, spec, spec], out_specs=(spec, spec, spec),
        compiler_params=_params("parallel"), name=name)(w, g, m, v)


def _place():
    return lax.axis_index("x"), lax.axis_index("y"), lax.axis_index("c")


def _other_chips(x, y):
    return [(1 - x, y), (x, 1 - y), (1 - x, 1 - y)]


def small_exchange(slab, reduce):
    r = slab.shape[0]

    def body(in_ref, o_ref, *scratch):
        if reduce:
            buf, send_sems, recv_sems = scratch
        else:
            buf = o_ref
            send_sems, recv_sems = scratch
        x, y, c = _place()
        me = 4 * x + 2 * y + c
        buf[me] = in_ref[...]
        copies = []
        for k in range(1, N_DEV):
            peer = (x ^ (k >> 2), y ^ ((k >> 1) & 1), c ^ (k & 1))
            cp = pltpu.make_async_remote_copy(
                src_ref=in_ref, dst_ref=buf.at[me], send_sem=send_sems.at[k - 1], recv_sem=recv_sems.at[k - 1],
                device_id=peer, device_id_type=MESH)
            cp.start()
            copies.append(cp)
        for k in range(1, N_DEV):
            src = 4 * (x ^ (k >> 2)) + 2 * (y ^ ((k >> 1) & 1)) + (c ^ (k & 1))
            pltpu.make_async_remote_copy(
                src_ref=in_ref, dst_ref=buf.at[src], send_sem=send_sems.at[k - 1], recv_sem=recv_sems.at[k - 1],
                device_id=(x, y, c), device_id_type=MESH).wait_recv()
        for cp in copies:
            cp.wait_send()
        if reduce:
            total = buf[0]
            for d in range(1, N_DEV):
                total = total + buf[d]
            o_ref[...] = total

    sems = [pltpu.SemaphoreType.DMA((N_DEV - 1,)), pltpu.SemaphoreType.DMA((N_DEV - 1,))]
    if reduce:
        out_shape = jax.ShapeDtypeStruct((r, LANES), F32)
        scratch = [pltpu.VMEM((N_DEV, r, LANES), F32)] + sems
    else:
        out_shape = jax.ShapeDtypeStruct((N_DEV, r, LANES), F32)
        scratch = sems
    vmem = pl.BlockSpec(memory_space=pltpu.VMEM)
    return pl.pallas_call(
        body, out_shape=out_shape, in_specs=[vmem], out_specs=vmem, scratch_shapes=scratch,
        compiler_params=pltpu.CompilerParams(vmem_limit_bytes=VMEM_LIMIT),
        name="small_reduce" if reduce else "small_gather")(slab)


def gather_weights(shards):
    nw = len(shards)
    ns = [s.shape[0] for s in shards]

    def body(*refs):
        ins, outs = refs[:nw], refs[nw:2 * nw]
        send_sems, recv_sems, local_sems = refs[2 * nw:]
        x, y, c = _place()
        me, sib = (x, y, c), (x, y, 1 - c)
        chips = _other_chips(x, y)

        def rows(w, dev):
            return outs[w].at[pl.ds((4 * dev[0] + 2 * dev[1] + dev[2]) * ns[w], ns[w]), :]

        def copy(w, k, block, to, src=None):
            return pltpu.make_async_remote_copy(
                src_ref=rows(w, block) if src is None else src, dst_ref=rows(w, block),
                send_sem=send_sems.at[w, k], recv_sem=recv_sems.at[w, k], device_id=to, device_id_type=MESH)

        started = []
        for w in range(nw):
            mine = pltpu.make_async_copy(ins[w], rows(w, me), local_sems.at[w])
            mine.start()
            started.append(mine)
        sends = []
        for w in range(nw):
            first = [copy(w, 0, me, sib, src=ins[w])]
            first += [copy(w, 1 + j, me, (*chip, c), src=ins[w]) for j, chip in enumerate(chips)]
            for cp in first:
                cp.start()
            sends += first
        for w in range(nw):
            for j, chip in enumerate(chips):
                copy(w, 1 + j, (*chip, c), me).wait_recv()
                fwd = copy(w, 4 + j, (*chip, c), sib)
                fwd.start()
                sends.append(fwd)
        for w in range(nw):
            copy(w, 0, sib, me).wait_recv()
            for j, chip in enumerate(chips):
                copy(w, 4 + j, (*chip, 1 - c), me).wait_recv()
        for cp in sends:
            cp.wait_send()
        for mine in started:
            mine.wait()

    out_shape = tuple(jax.ShapeDtypeStruct((N_DEV * s.shape[0], s.shape[1]), s.dtype) for s in shards)
    return pl.pallas_call(
        body, out_shape=out_shape, in_specs=[ANY] * nw, out_specs=tuple([ANY] * nw),
        scratch_shapes=[pltpu.SemaphoreType.DMA((nw, 7)), pltpu.SemaphoreType.DMA((nw, 7)),
                        pltpu.SemaphoreType.DMA((nw,))],
        name="gather_weights")(*shards)


def scatter_to_sibling(grads):
    nw = len(grads)
    ns = [g.shape[0] // N_DEV for g in grads]

    def body(*refs):
        ins, outs = refs[:nw], refs[nw:2 * nw]
        send_sems, recv_sems = refs[2 * nw:]
        x, y, c = _place()
        sib = (x, y, 1 - c)
        copies = []
        for w in range(nw):
            for q in range(4):
                cp = pltpu.make_async_remote_copy(
                    src_ref=ins[w].at[pl.ds((2 * q + 1 - c) * ns[w], ns[w]), :],
                    dst_ref=outs[w].at[pl.ds(q * ns[w], ns[w]), :],
                    send_sem=send_sems.at[w, q], recv_sem=recv_sems.at[w, q], device_id=sib, device_id_type=MESH)
                cp.start()
                copies.append(cp)
        for cp in copies:
            cp.wait_recv()
        for cp in copies:
            cp.wait_send()

    out_shape = tuple(jax.ShapeDtypeStruct((4 * n, g.shape[1]), g.dtype) for n, g in zip(ns, grads))
    return pl.pallas_call(
        body, out_shape=out_shape, in_specs=[ANY] * nw, out_specs=tuple([ANY] * nw),
        scratch_shapes=[pltpu.SemaphoreType.DMA((nw, 4)), pltpu.SemaphoreType.DMA((nw, 4))],
        name="scatter_to_sibling")(*grads)


def scatter_to_chips(parts):
    nw = len(parts)
    ns = [p.shape[0] // 4 for p in parts]

    def body(*refs):
        ins, outs = refs[:nw], refs[nw:2 * nw]
        send_sems, recv_sems = refs[2 * nw:]
        x, y, c = _place()
        copies = []
        for w in range(nw):
            for j, chip in enumerate(_other_chips(x, y)):
                cp = pltpu.make_async_remote_copy(
                    src_ref=ins[w].at[pl.ds((2 * chip[0] + chip[1]) * ns[w], ns[w]), :],
                    dst_ref=outs[w].at[pl.ds(j * ns[w], ns[w]), :],
                    send_sem=send_sems.at[w, j], recv_sem=recv_sems.at[w, j],
                    device_id=(*chip, c), device_id_type=MESH)
                cp.start()
                copies.append(cp)
        for cp in copies:
            cp.wait_recv()
        for cp in copies:
            cp.wait_send()

    out_shape = tuple(jax.ShapeDtypeStruct((3 * n, p.shape[1]), p.dtype) for n, p in zip(ns, parts))
    return pl.pallas_call(
        body, out_shape=out_shape, in_specs=[ANY] * nw, out_specs=tuple([ANY] * nw),
        scratch_shapes=[pltpu.SemaphoreType.DMA((nw, 3)), pltpu.SemaphoreType.DMA((nw, 3))],
        name="scatter_to_chips")(*parts)


def add_sibling(grad, landed, core, name):
    n = landed.shape[0] // 4
    cols = grad.shape[1]

    def body(c_ref, g_ref, l_ref, o_ref):
        o_ref[...] = (g_ref[...].astype(F32) + l_ref[...].astype(F32)).astype(o_ref.dtype)

    grid_spec = pltpu.PrefetchScalarGridSpec(
        num_scalar_prefetch=1, grid=(4,),
        in_specs=[pl.BlockSpec((n, cols), lambda q, c_ref: (2 * q + c_ref[0], 0)),
                  pl.BlockSpec((n, cols), lambda q, c_ref: (q, 0))],
        out_specs=pl.BlockSpec((n, cols), lambda q, c_ref: (q, 0)))
    return pl.pallas_call(
        body, out_shape=jax.ShapeDtypeStruct(landed.shape, landed.dtype), grid_spec=grid_spec,
        compiler_params=_params("arbitrary"), name=name)(core, grad, landed)


def add_chips(part, landed, chip, name):
    n = landed.shape[0] // 3
    cols = part.shape[1]

    def body(q_ref, p_ref, l0_ref, l1_ref, l2_ref, o_ref):
        o_ref[...] = ((p_ref[...].astype(F32) + l0_ref[...].astype(F32)) + l1_ref[...].astype(F32)) \
            + l2_ref[...].astype(F32)

    def landed_spec(j):
        return pl.BlockSpec((n, cols), lambda i, q_ref: (j, 0))

    grid_spec = pltpu.PrefetchScalarGridSpec(
        num_scalar_prefetch=1, grid=(1,),
        in_specs=[pl.BlockSpec((n, cols), lambda i, q_ref: (q_ref[0], 0)), landed_spec(0), landed_spec(1), landed_spec(2)],
        out_specs=pl.BlockSpec((n, cols), lambda i, q_ref: (0, 0)))
    return pl.pallas_call(
        body, out_shape=jax.ShapeDtypeStruct((n, cols), F32), grid_spec=grid_spec,
        compiler_params=_params("arbitrary"), name=name)(chip, part, landed, landed, landed)


def _pack(arrays):
    flat = jnp.concatenate([a.reshape(-1).astype(F32) for a in arrays])
    pad = (-flat.shape[0]) % (8 * LANES)
    return jnp.pad(flat, (0, pad)).reshape(-1, LANES)


def _unpack(slab, shapes):
    flat = slab.reshape(slab.shape[:-2] + (-1,))
    out, off = [], 0
    for shp in shapes:
        size = 1
        for s in shp:
            size *= s
        out.append(flat[..., off:off + size].reshape(flat.shape[:-1] + tuple(shp)))
        off += size
    return out


def kernel(x, mem, norm1_g, mem_norm_g, a_w_in, a_q_g, a_k_g, a_rel_bias, b_w_in, b_b_in, b_conv_w, b_conv_b, b_ln_g, b_ln_b, mq_g, mk_g, w_mem_kv, w_out, norm2_g, w_gate, w_up, w_down, loss_target, m_norm1_g, m_mem_norm_g, m_a_w_in, m_a_q_g, m_a_k_g, m_a_rel_bias, m_b_w_in, m_b_b_in, m_b_conv_w, m_b_conv_b, m_b_ln_g, m_b_ln_b, m_mq_g, m_mk_g, m_w_mem_kv, m_w_out, m_norm2_g, m_w_gate, m_w_up, m_w_down, v_norm1_g, v_mem_norm_g, v_a_w_in, v_a_q_g, v_a_k_g, v_a_rel_bias, v_b_w_in, v_b_b_in, v_b_conv_w, v_b_conv_b, v_b_ln_g, v_b_ln_b, v_mq_g, v_mk_g, v_w_mem_kv, v_w_out, v_norm2_g, v_w_gate, v_w_up, v_w_down):
    batch, seq, d = x.shape
    mtok = mem.shape[1]
    n = batch * seq
    ax, ay, ac = _place()
    me = 4 * ax + 2 * ay + ac
    core_arr = jnp.reshape(ac, (1,)).astype(jnp.int32)
    chip_arr = jnp.reshape(2 * ax + ay, (1,)).astype(jnp.int32)

    def t_bf16(w):
        return jnp.transpose(w).astype(BF16)

    shard_list = [
        t_bf16(a_w_in[0]), t_bf16(b_w_in[0]),
        t_bf16(w_gate[0]), t_bf16(w_up[0]), w_down[0].astype(BF16), w_out[0].astype(BF16), w_mem_kv[0].astype(BF16),
        t_bf16(w_gate[1]), t_bf16(w_up[1]), w_down[1].astype(BF16), w_out[1].astype(BF16), w_mem_kv[1].astype(BF16),
    ]
    full = gather_weights(shard_list)
    a_win_t, b_win_t = full[0], full[1]
    wg_t, wu_t, wd, wo, wkv = ([full[2 + 5 * l + k] for l in range(2)] for k in range(5))

    f_loc = b_b_in.shape[1]
    c_loc = b_conv_b.shape[1]
    small_shapes = [(f_loc,), (CONV_W, c_loc), (c_loc,), (c_loc,), (c_loc,)]
    gathered = small_exchange(_pack([b_b_in, b_conv_w, b_conv_b, b_ln_g, b_ln_b]), reduce=False)
    bb_g, cw_g, cb_g, lg_g, lb_g = _unpack(gathered, small_shapes)
    bb_full = bb_g.reshape(1, -1)
    cw_full = jnp.pad(jnp.transpose(cw_g, (1, 0, 2)).reshape(CONV_W, -1), ((0, 32 - CONV_W), (0, 0)))
    cb_full, lg_full, lb_full = cb_g.reshape(1, -1), lg_g.reshape(1, -1), lb_g.reshape(1, -1)

    def two(g):
        return jnp.concatenate([g, g], axis=-1)

    gq2, gk2 = two(a_q_g), two(a_k_g)
    rel16 = jnp.pad(a_rel_bias[0], ((0, 16 - a_rel_bias.shape[1]), (0, 0)))
    bias = bias_blocks(rel16)

    x0 = x.reshape(n, d)
    mem2 = mem.reshape(batch * mtok, d)
    zero_mem = jnp.zeros_like(mem2)

    saved = []
    xin = x0
    for l in range(2):
        h = rms_fwd(xin, norm1_g[l:l + 1], name=f"rms1_fwd_{l}")
        mem_n = rms_fwd(mem2, mem_norm_g[l:l + 1], name=f"rms_mem_fwd_{l}")
        kv = mm_nn(mem_n, wkv[l], name=f"mem_kv_{l}")
        gq4 = jnp.tile(mq_g[l:l + 1], (1, 4))
        gk4 = jnp.tile(mk_g[l:l + 1], (1, 4))
        if l == 0:
            z = mm_nt(h, a_win_t, name="in_proj_a")
            tok = attn_fwd(z, gq2, gk2, bias, batch, seq)
            qcol = 3 * TOK_WIDTH // MEM_WIDTH
        else:
            z = mm_nt(h, b_win_t, bias=bb_full, name="in_proj_b")
            tok = conv_fwd(z, cw_full, cb_full, lg_full, lb_full, batch, seq)
            qcol = 2 * TOK_WIDTH // MEM_WIDTH
        memo = memattn_fwd(z, kv, gq4, gk4, batch, seq, qcol, name=f"memattn_fwd_{l}")
        cat = jnp.concatenate([tok, memo], axis=-1)
        x1 = mm_nn(cat, wo[l], res=xin, name=f"out_proj_{l}")
        h2 = rms_fwd(x1, norm2_g[l:l + 1], name=f"rms2_fwd_{l}")
        gate, up, act = gate_up(h2, wg_t[l], wu_t[l], name=f"gate_up_{l}")
        x2 = mm_nn(act, wd[l], res=x1, name=f"down_proj_{l}")
        saved.append(dict(xin=xin, h=h, mem_n=mem_n, kv=kv, gq4=gq4, gk4=gk4, z=z, qcol=qcol, cat=cat, x1=x1, h2=h2,
                          gate=gate, up=up, act=act))
        xin = x2

    dx, loss_blk = loss_head(xin, loss_target.reshape(n, d))
    loss = lax.psum(loss_blk[0, 0], ("x", "y", "c"))

    big = {}
    small = {}
    for l in (1, 0):
        sv = saved[l]
        dgate, dup = ffn_bwd_act(dx, wd[l], sv["gate"], sv["up"], name=f"ffn_bwd_act_{l}")
        big[f"wd{l}"] = mm_tn(sv["act"], dx, name=f"grad_wd_{l}")
        dh2 = mm2_nn(dgate, wg_t[l], dup, wu_t[l], name=f"ffn_bwd_h_{l}")
        big[f"wg{l}"] = mm_tn(dgate, sv["h2"], name=f"grad_wg_{l}")
        big[f"wu{l}"] = mm_tn(dup, sv["h2"], name=f"grad_wu_{l}")
        dx1, small[f"norm2_{l}"] = rms_bwd(dh2, sv["x1"], norm2_g[l:l + 1], dx, name=f"rms2_bwd_{l}")
        dcat = mm_nt(dx1, wo[l], name=f"out_proj_bwd_{l}")
        big[f"wo{l}"] = mm_tn(sv["cat"], dx1, name=f"grad_wo_{l}")
        dqm, dkv, small[f"mq_{l}"], small[f"mk_{l}"] = memattn_bwd(
            sv["z"], sv["kv"], dcat, sv["gq4"], sv["gk4"], batch, seq, sv["qcol"], name=f"memattn_bwd_{l}")
        if l == 0:
            dq, dk, dv, dbias, small["a_q"], small["a_k"] = attn_bwd(sv["z"], dcat, gq2, gk2, bias, batch, seq)
            small["rel"] = bias_grad(dbias)
            dz = jnp.concatenate([dq, dk, dv, dqm], axis=-1)
            win_t = a_win_t
        else:
            da, dg, small["cw"], small["csum"] = conv_bwd(
                sv["z"], dcat, cw_full, cb_full, lg_full, lb_full, batch, seq)
            dz = jnp.concatenate([da, dg, dqm], axis=-1)
            small["bb"] = col_sum(dz, name="grad_b_in")
            win_t = b_win_t
        big[f"win{l}"] = mm_tn(dz, sv["h"], name=f"grad_win_{l}")
        dh = mm_nn(dz, win_t, name=f"in_proj_bwd_{l}")
        big[f"wkv{l}"] = mm_tn(sv["mem_n"], dkv, name=f"grad_wkv_{l}")
        dmem_n = mm_nt(dkv, wkv[l], out_dtype=F32, name=f"mem_kv_bwd_{l}")
        _, small[f"memnorm_{l}"] = rms_bwd(dmem_n, mem2, mem_norm_g[l:l + 1], zero_mem, name=f"rms_mem_bwd_{l}")
        dx, small[f"norm1_{l}"] = rms_bwd(dh, sv["xin"], norm1_g[l:l + 1], dx1, name=f"rms1_bwd_{l}")
    grad_x = dx.reshape(batch, seq, d)

    order = ["win0", "win1", "wg0", "wu0", "wd0", "wo0", "wkv0", "wg1", "wu1", "wd1", "wo1", "wkv1"]
    glist = [big[k] for k in order]
    landed1 = scatter_to_sibling(glist)
    parts = [add_sibling(g, ld, core_arr, name=f"add_sibling_{k}") for k, g, ld in zip(order, glist, landed1)]
    landed2 = scatter_to_chips(parts)
    red = {k: add_chips(p, ld, chip_arr, name=f"add_chips_{k}") for k, p, ld in zip(order, parts, landed2)}

    g_a_w_in = jnp.transpose(red["win0"])[None]
    g_b_w_in = jnp.transpose(red["win1"])[None]
    g_w_gate = jnp.stack([jnp.transpose(red["wg0"]), jnp.transpose(red["wg1"])])
    g_w_up = jnp.stack([jnp.transpose(red["wu0"]), jnp.transpose(red["wu1"])])
    g_w_down = jnp.stack([red["wd0"], red["wd1"]])
    g_w_out = jnp.stack([red["wo0"], red["wo1"]])
    g_w_mem_kv = jnp.stack([red["wkv0"], red["wkv1"]])

    def fold(v, groups):
        return jnp.sum(v.reshape(groups, HEAD_DIM), axis=0, keepdims=True)

    heads = a_rel_bias.shape[1]
    small_list = [
        jnp.concatenate([small["norm1_0"], small["norm1_1"]]),
        jnp.concatenate([small["memnorm_0"], small["memnorm_1"]]),
        fold(small["a_q"], 2), fold(small["a_k"], 2), small["rel"][:heads][None],
        small["bb"], small["cw"][:CONV_W][None], small["csum"][0:1], small["csum"][1:2], small["csum"][2:3],
        jnp.concatenate([fold(small["mq_0"], 4), fold(small["mq_1"], 4)]),
        jnp.concatenate([fold(small["mk_0"], 4), fold(small["mk_1"], 4)]),
        jnp.concatenate([small["norm2_0"], small["norm2_1"]]),
    ]
    small_full_shapes = [a.shape for a in small_list]
    summed = _unpack(small_exchange(_pack(small_list), reduce=True), small_full_shapes)
    (g_norm1, g_memnorm, g_aq, g_ak, g_rel, g_bb_full, g_cw_full, g_cb_full, g_lg_full, g_lb_full,
     g_mq, g_mk, g_norm2) = summed
    g_bb = lax.dynamic_slice_in_dim(g_bb_full, me * f_loc, f_loc, axis=1)
    g_cw = lax.dynamic_slice_in_dim(g_cw_full, me * c_loc, c_loc, axis=2)
    g_cb = lax.dynamic_slice_in_dim(g_cb_full, me * c_loc, c_loc, axis=1)
    g_lg = lax.dynamic_slice_in_dim(g_lg_full, me * c_loc, c_loc, axis=1)
    g_lb = lax.dynamic_slice_in_dim(g_lb_full, me * c_loc, c_loc, axis=1)

    grads = [g_norm1, g_memnorm, g_a_w_in, g_aq, g_ak, g_rel, g_b_w_in, g_bb, g_cw, g_cb, g_lg, g_lb,
             g_mq, g_mk, g_w_mem_kv, g_w_out, g_norm2, g_w_gate, g_w_up, g_w_down]
    weights = [norm1_g, mem_norm_g, a_w_in, a_q_g, a_k_g, a_rel_bias, b_w_in, b_b_in, b_conv_w, b_conv_b, b_ln_g,
               b_ln_b, mq_g, mk_g, w_mem_kv, w_out, norm2_g, w_gate, w_up, w_down]
    moms = [m_norm1_g, m_mem_norm_g, m_a_w_in, m_a_q_g, m_a_k_g, m_a_rel_bias, m_b_w_in, m_b_b_in, m_b_conv_w,
            m_b_conv_b, m_b_ln_g, m_b_ln_b, m_mq_g, m_mk_g, m_w_mem_kv, m_w_out, m_norm2_g, m_w_gate, m_w_up, m_w_down]
    vels = [v_norm1_g, v_mem_norm_g, v_a_w_in, v_a_q_g, v_a_k_g, v_a_rel_bias, v_b_w_in, v_b_b_in, v_b_conv_w,
            v_b_conv_b, v_b_ln_g, v_b_ln_b, v_mq_g, v_mk_g, v_w_mem_kv, v_w_out, v_norm2_g, v_w_gate, v_w_up, v_w_down]

    large = {2, 6, 14, 15, 17, 18, 19}
    deltas, new_m, new_v = [None] * 20, [None] * 20, [None] * 20
    for i in sorted(large):
        shp = weights[i].shape
        flat = (shp[0] * shp[1], shp[2])
        dl, nm, nv = adamw(weights[i].reshape(flat), grads[i].reshape(flat), moms[i].reshape(flat),
                           vels[i].reshape(flat), name=f"adamw_{i}")
        deltas[i], new_m[i], new_v[i] = dl.reshape(shp), nm.reshape(shp), nv.reshape(shp)
    small_idx = [i for i in range(20) if i not in large]
    small_shapes2 = [weights[i].shape for i in small_idx]
    dl, nm, nv = adamw(_pack([weights[i] for i in small_idx]), _pack([grads[i] for i in small_idx]),
                       _pack([moms[i] for i in small_idx]), _pack([vels[i] for i in small_idx]), name="adamw_small")
    for i, a, b, cc in zip(small_idx, _unpack(dl, small_shapes2), _unpack(nm, small_shapes2), _unpack(nv, small_shapes2)):
        deltas[i], new_m[i], new_v[i] = a, b, cc

    return (loss, grad_x, *grads, *deltas, *new_m, *new_v)
```

```python
import functools

import jax
import jax.numpy as jnp
from jax import lax
from jax.experimental import pallas as pl
from jax.experimental.pallas import tpu as pltpu
from jax.experimental.pallas import tpu_sc as plsc

F32 = jnp.float32
BF16 = jnp.bfloat16
HIGHEST = lax.Precision.HIGHEST
MESH = pl.DeviceIdType.MESH
ANY = pl.BlockSpec(memory_space=pl.ANY)

N_DEV = 8
D_MODEL = 1024
HEAD_DIM = 64
TOK_WIDTH = 768
MEM_WIDTH = 256
CHUNK = 64
Q_BLOCK = 256
KEY_WIN = 768
BAND = 576
N_REL = 192
CONV_W = 31
CONV_HALO = 32
NORM_EPS = 1e-6
NEG_INF = -1e30
ATTN_SCALE = HEAD_DIM ** -0.5
LANES = 128
ROW_TILE = 512
VMEM_LIMIT = 56 * 1024 * 1024

ADAM_LR, ADAM_B1, ADAM_B2, ADAM_EPS, ADAM_WD, ADAM_STEP = 0.001, 0.9, 0.999, 1e-08, 0.01, 10


def _params(*sem):
    return pltpu.CompilerParams(dimension_semantics=sem, vmem_limit_bytes=VMEM_LIMIT)


def _row_tile(m):
    return ROW_TILE if m % ROW_TILE == 0 else m


def _col_tile(n, cap=1408):
    best = None
    for t in range(LANES, min(n, cap) + 1, LANES):
        if n % t == 0:
            best = t
    return best if best is not None else n


def _dot(a, b, ca, cb):
    return lax.dot_general(a, b, (((ca,), (cb,)), ((), ())), preferred_element_type=F32)


def _sigmoid(x):
    return 1.0 / (1.0 + jnp.exp(-x))


def mm_nt(a, b, bias=None, out_dtype=BF16, name="mm_nt"):
    m, k = a.shape
    n = b.shape[0]
    tm, tn = _row_tile(m), _col_tile(n)

    def body(*refs):
        a_ref, b_ref = refs[0], refs[1]
        o_ref = refs[-1]
        acc = _dot(a_ref[...].astype(BF16), b_ref[...].astype(BF16), 1, 1)
        if bias is not None:
            acc = acc + refs[2][...]
        o_ref[...] = acc.astype(o_ref.dtype)

    in_specs = [pl.BlockSpec((tm, k), lambda i, j: (i, 0)), pl.BlockSpec((tn, k), lambda i, j: (j, 0))]
    args = [a, b]
    if bias is not None:
        in_specs.append(pl.BlockSpec((1, tn), lambda i, j: (0, j)))
        args.append(bias)
    return pl.pallas_call(
        body, out_shape=jax.ShapeDtypeStruct((m, n), out_dtype), grid=(m // tm, n // tn),
        in_specs=in_specs, out_specs=pl.BlockSpec((tm, tn), lambda i, j: (i, j)),
        compiler_params=_params("parallel", "arbitrary"), name=name)(*args)


def mm_nn(a, b, res=None, out_dtype=F32, name="mm_nn"):
    m, k = a.shape
    n = b.shape[1]
    tm, tn = _row_tile(m), _col_tile(n, 1024)

    def body(*refs):
        a_ref, b_ref = refs[0], refs[1]
        o_ref = refs[-1]
        acc = _dot(a_ref[...].astype(BF16), b_ref[...].astype(BF16), 1, 0)
        if res is not None:
            acc = acc + refs[2][...]
        o_ref[...] = acc.astype(o_ref.dtype)

    in_specs = [pl.BlockSpec((tm, k), lambda i, j: (i, 0)), pl.BlockSpec((k, tn), lambda i, j: (0, j))]
    args = [a, b]
    if res is not None:
        in_specs.append(pl.BlockSpec((tm, tn), lambda i, j: (i, j)))
        args.append(res)
    return pl.pallas_call(
        body, out_shape=jax.ShapeDtypeStruct((m, n), out_dtype), grid=(m // tm, n // tn),
        in_specs=in_specs, out_specs=pl.BlockSpec((tm, tn), lambda i, j: (i, j)),
        compiler_params=_params("parallel", "arbitrary"), name=name)(*args)


def mm2_nn(a1, b1, a2, b2, name="mm2_nn"):
    m, k = a1.shape
    n = b1.shape[1]
    tm, tn = _row_tile(m), _col_tile(n, 512)

    def body(a1_ref, b1_ref, a2_ref, b2_ref, o_ref):
        o_ref[...] = _dot(a1_ref[...], b1_ref[...], 1, 0) + _dot(a2_ref[...], b2_ref[...], 1, 0)

    a_spec = pl.BlockSpec((tm, k), lambda i, j: (i, 0))
    b_spec = pl.BlockSpec((k, tn), lambda i, j: (0, j))
    return pl.pallas_call(
        body, out_shape=jax.ShapeDtypeStruct((m, n), F32), grid=(m // tm, n // tn),
        in_specs=[a_spec, b_spec, a_spec, b_spec], out_specs=pl.BlockSpec((tm, tn), lambda i, j: (i, j)),
        compiler_params=_params("parallel", "arbitrary"), name=name)(a1, b1, a2, b2)


def mm_tn(a, b, out_dtype=BF16, name="mm_tn"):
    t, r = a.shape
    c = b.shape[1]
    tt, tr = _row_tile(t), _col_tile(r)
    nt = t // tt

    def body(a_ref, b_ref, o_ref, acc_ref):
        step = pl.program_id(1)

        @pl.when(step == 0)
        def _():
            acc_ref[...] = jnp.zeros_like(acc_ref)

        acc_ref[...] += _dot(a_ref[...].astype(BF16), b_ref[...].astype(BF16), 0, 0)

        @pl.when(step == nt - 1)
        def _():
            o_ref[...] = acc_ref[...].astype(o_ref.dtype)

    return pl.pallas_call(
        body, out_shape=jax.ShapeDtypeStruct((r, c), out_dtype), grid=(r // tr, nt),
        in_specs=[pl.BlockSpec((tt, tr), lambda i, s: (s, i)), pl.BlockSpec((tt, c), lambda i, s: (s, 0))],
        out_specs=pl.BlockSpec((tr, c), lambda i, s: (i, 0)),
        scratch_shapes=[pltpu.VMEM((tr, c), F32)],
        compiler_params=_params("parallel", "arbitrary"), name=name)(a, b)


def rms_fwd(x, g, name="rms_fwd"):
    n, d = x.shape
    tm = _row_tile(n)

    def body(x_ref, g_ref, o_ref):
        xv = x_ref[...]
        r = lax.rsqrt(jnp.mean(xv * xv, axis=-1, keepdims=True) + NORM_EPS)
        o_ref[...] = (xv * r * g_ref[...]).astype(o_ref.dtype)

    return pl.pallas_call(
        body, out_shape=jax.ShapeDtypeStruct((n, d), BF16), grid=(n // tm,),
        in_specs=[pl.BlockSpec((tm, d), lambda i: (i, 0)), pl.BlockSpec((1, d), lambda i: (0, 0))],
        out_specs=pl.BlockSpec((tm, d), lambda i: (i, 0)),
        compiler_params=_params("parallel"), name=name)(x, g)


def rms_bwd(dh, x, g, dres, name="rms_bwd"):
    n, d = x.shape
    tm = _row_tile(n)

    def body(dh_ref, x_ref, g_ref, dres_ref, dx_ref, dg_ref):
        @pl.when(pl.program_id(0) == 0)
        def _():
            dg_ref[...] = jnp.zeros_like(dg_ref)

        xv = x_ref[...]
        dhv = dh_ref[...].astype(F32)
        r = lax.rsqrt(jnp.mean(xv * xv, axis=-1, keepdims=True) + NORM_EPS)
        xhat = xv * r
        dg_ref[...] += jnp.sum(dhv * xhat, axis=0, keepdims=True)
        dxhat = dhv * g_ref[...]
        mean_t = jnp.mean(dxhat * xhat, axis=-1, keepdims=True)
        dx_ref[...] = dres_ref[...] + r * (dxhat - xhat * mean_t)

    row = pl.BlockSpec((tm, d), lambda i: (i, 0))
    vec = pl.BlockSpec((1, d), lambda i: (0, 0))
    return pl.pallas_call(
        body, out_shape=(jax.ShapeDtypeStruct((n, d), F32), jax.ShapeDtypeStruct((1, d), F32)), grid=(n // tm,),
        in_specs=[row, row, vec, row], out_specs=(row, vec),
        compiler_params=_params("arbitrary"), name=name)(dh, x, g, dres)


def gate_up(h2, wg_t, wu_t, name="gate_up"):
    n, d = h2.shape
    f = wg_t.shape[0]
    tm, tn = _row_tile(n), _col_tile(f)

    def body(h_ref, wg_ref, wu_ref, g_ref, u_ref, a_ref):
        hv = h_ref[...]
        gv = _dot(hv, wg_ref[...], 1, 1)
        uv = _dot(hv, wu_ref[...], 1, 1)
        g_ref[...] = gv.astype(BF16)
        u_ref[...] = uv.astype(BF16)
        a_ref[...] = (gv * _sigmoid(gv) * uv).astype(BF16)

    w_spec = pl.BlockSpec((tn, d), lambda i, j: (j, 0))
    o_spec = pl.BlockSpec((tm, tn), lambda i, j: (i, j))
    o_shape = jax.ShapeDtypeStruct((n, f), BF16)
    return pl.pallas_call(
        body, out_shape=(o_shape, o_shape, o_shape), grid=(n // tm, f // tn),
        in_specs=[pl.BlockSpec((tm, d), lambda i, j: (i, 0)), w_spec, w_spec], out_specs=(o_spec, o_spec, o_spec),
        compiler_params=_params("parallel", "arbitrary"), name=name)(h2, wg_t, wu_t)


def ffn_bwd_act(dx, wd, gate, up, name="ffn_bwd_act"):
    n, d = dx.shape
    f = wd.shape[0]
    tm, tn = _row_tile(n), _col_tile(f)

    def body(dx_ref, wd_ref, g_ref, u_ref, dg_ref, du_ref):
        dact = _dot(dx_ref[...].astype(BF16), wd_ref[...], 1, 1)
        gv = g_ref[...].astype(F32)
        uv = u_ref[...].astype(F32)
        sg = _sigmoid(gv)
        dg_ref[...] = (dact * uv * sg * (1.0 + gv * (1.0 - sg))).astype(BF16)
        du_ref[...] = (dact * gv * sg).astype(BF16)

    t_spec = pl.BlockSpec((tm, tn), lambda i, j: (i, j))
    o_shape = jax.ShapeDtypeStruct((n, f), BF16)
    return pl.pallas_call(
        body, out_shape=(o_shape, o_shape), grid=(n // tm, f // tn),
        in_specs=[pl.BlockSpec((tm, d), lambda i, j: (i, 0)), pl.BlockSpec((tn, d), lambda i, j: (j, 0)), t_spec, t_spec],
        out_specs=(t_spec, t_spec),
        compiler_params=_params("parallel", "arbitrary"), name=name)(dx, wd, gate, up)


def _group_masks(width):
    lane = lax.broadcasted_iota(jnp.int32, (1, width), 1)
    return [(lane >= HEAD_DIM * g) & (lane < HEAD_DIM * (g + 1)) for g in range(width // HEAD_DIM)]


def _group_sum(x, masks):
    out = jnp.zeros_like(x)
    for msk in masks:
        s = jnp.sum(jnp.where(msk, x, 0.0), axis=-1, keepdims=True)
        out = jnp.where(msk, s, out)
    return out


def _head_norm(x, gain, masks):
    r = lax.rsqrt(_group_sum(x * x, masks) * (1.0 / HEAD_DIM) + NORM_EPS)
    xhat = x * r
    return xhat * gain, xhat, r


def _head_norm_bwd(dxn, xhat, r, gain, masks):
    dgain = jnp.sum(dxn * xhat, axis=0, keepdims=True)
    dxhat = dxn * gain
    mean_t = _group_sum(dxhat * xhat, masks) * (1.0 / HEAD_DIM)
    return r * (dxhat - xhat * mean_t), dgain


def _softmax_rows(s):
    e = jnp.exp(s - jnp.max(s, axis=-1, keepdims=True))
    return e * (1.0 / jnp.sum(e, axis=-1, keepdims=True))


def _rel_onehot():
    col = lax.broadcasted_iota(jnp.int32, (1, KEY_WIN), 1)
    off = jnp.where(col < KEY_WIN - LANES, col, col - KEY_WIN)
    idx = jnp.clip(8 * CHUNK - off, -(CHUNK - 1), LANES) + (CHUNK - 1)
    return (lax.broadcasted_iota(jnp.int32, (N_REL, KEY_WIN), 0) == idx).astype(F32)


def bias_blocks(rel16):
    heads = TOK_WIDTH // HEAD_DIM

    def body(rel_ref, o_ref, u_ref):
        u_ref[...] = jnp.dot(rel_ref[...], _rel_onehot(), precision=HIGHEST, preferred_element_type=F32)
        row = lax.broadcasted_iota(jnp.int32, (CHUNK, KEY_WIN), 0)
        col = lax.broadcasted_iota(jnp.int32, (CHUNK, KEY_WIN), 1)
        for h in range(heads):
            xv = jnp.broadcast_to(u_ref[h:h + 1, :], (CHUNK, KEY_WIN))
            for b in range(6):
                xv = jnp.where(((row >> b) & 1) == 1, pltpu.roll(xv, 1 << b, axis=1), xv)
            xv = jnp.where(col < BAND, xv, NEG_INF)
            for i in range(Q_BLOCK // CHUNK):
                o_ref[h, CHUNK * i:CHUNK * (i + 1), :] = pltpu.roll(xv, CHUNK * i, axis=1) if i else xv

    return pl.pallas_call(
        body, out_shape=jax.ShapeDtypeStruct((heads, Q_BLOCK, KEY_WIN), F32),
        scratch_shapes=[pltpu.VMEM((16, KEY_WIN), F32)], name="bias_blocks")(rel16)


def bias_grad(dbias):
    heads = dbias.shape[0]

    def body(db_ref, o_ref, y_ref):
        y_ref[...] = jnp.zeros_like(y_ref)
        row = lax.broadcasted_iota(jnp.int32, (CHUNK, KEY_WIN), 0)
        for h in range(heads):
            fv = db_ref[h, 0:CHUNK, :]
            for i in range(1, Q_BLOCK // CHUNK):
                fv = fv + pltpu.roll(db_ref[h, CHUNK * i:CHUNK * (i + 1), :], KEY_WIN - CHUNK * i, axis=1)
            for b in range(6):
                fv = jnp.where(((row >> b) & 1) == 1, pltpu.roll(fv, KEY_WIN - (1 << b), axis=1), fv)
            y_ref[h:h + 1, :] = jnp.sum(fv, axis=0, keepdims=True)
        o_ref[...] = lax.dot_general(y_ref[...], _rel_onehot(), (((1,), (1,)), ((), ())),
                                     precision=HIGHEST, preferred_element_type=F32)

    return pl.pallas_call(
        body, out_shape=jax.ShapeDtypeStruct((16, N_REL), F32),
        scratch_shapes=[pltpu.VMEM((16, KEY_WIN), F32)], name="bias_grad")(dbias)


def _attn_windows(seq):
    out = []
    for j in range(seq // Q_BLOCK):
        r0 = j * Q_BLOCK
        k0 = max(0, r0 - 8 * CHUNK)
        width = r0 + Q_BLOCK - k0
        out.append((r0, k0, width, KEY_WIN - width))
    return out


def attn_fwd(z, gq2, gk2, bias, batch, seq):
    n = z.shape[0]
    pairs = TOK_WIDTH // LANES

    def body(q_ref, k_ref, v_ref, gq_ref, gk_ref, b_ref, o_ref, qn_s, kn_s):
        masks = _group_masks(LANES)
        qn_s[...] = _head_norm(q_ref[...].astype(F32), gq_ref[...], masks)[0].astype(BF16)
        kn_s[...] = _head_norm(k_ref[...].astype(F32), gk_ref[...], masks)[0].astype(BF16)
        for r0, k0, width, c0 in _attn_windows(seq):
            qb = qn_s[r0:r0 + Q_BLOCK, :]
            kw = kn_s[k0:k0 + width, :]
            vw = v_ref[k0:k0 + width, :]
            out = jnp.zeros((Q_BLOCK, LANES), F32)
            for h, msk in enumerate(masks):
                qh = jnp.where(msk, qb, jnp.zeros_like(qb))
                s = _dot(qh, kw, 1, 1) * ATTN_SCALE + b_ref[h, :, c0:KEY_WIN]
                p = _softmax_rows(s).astype(BF16)
                out = jnp.where(msk, _dot(p, vw, 1, 0), out)
            o_ref[r0:r0 + Q_BLOCK, :] = out.astype(o_ref.dtype)

    def col(off):
        return pl.BlockSpec((seq, LANES), lambda b, p: (b, off + p))

    vec = pl.BlockSpec((1, LANES), lambda b, p: (0, 0))
    return pl.pallas_call(
        body, out_shape=jax.ShapeDtypeStruct((n, TOK_WIDTH), BF16), grid=(batch, pairs),
        in_specs=[col(0), col(pairs), col(2 * pairs), vec, vec,
                  pl.BlockSpec((2, Q_BLOCK, KEY_WIN), lambda b, p: (p, 0, 0))],
        out_specs=pl.BlockSpec((seq, LANES), lambda b, p: (b, p)),
        scratch_shapes=[pltpu.VMEM((seq, LANES), BF16), pltpu.VMEM((seq, LANES), BF16)],
        compiler_params=_params("parallel", "arbitrary"), name="attn_fwd")(z, z, z, gq2, gk2, bias)


def attn_bwd(z, dcat, gq2, gk2, bias, batch, seq):
    n = z.shape[0]
    pairs = TOK_WIDTH // LANES

    def body(q_ref, k_ref, v_ref, do_ref, gq_ref, gk_ref, b_ref,
             dq_ref, dk_ref, dv_ref, db_ref, dgq_ref, dgk_ref, qn_s, kn_s, dqn_s, dkn_s, dv_s):
        bi, pi = pl.program_id(1), pl.program_id(0)
        masks = _group_masks(LANES)

        @pl.when(bi == 0)
        def _():
            db_ref[...] = jnp.zeros_like(db_ref)

        @pl.when((bi == 0) & (pi == 0))
        def _():
            dgq_ref[...] = jnp.zeros_like(dgq_ref)
            dgk_ref[...] = jnp.zeros_like(dgk_ref)

        qn, qhat, rq = _head_norm(q_ref[...].astype(F32), gq_ref[...], masks)
        kn, khat, rk = _head_norm(k_ref[...].astype(F32), gk_ref[...], masks)
        qn_s[...] = qn.astype(BF16)
        kn_s[...] = kn.astype(BF16)
        dkn_s[...] = jnp.zeros_like(dkn_s)
        dv_s[...] = jnp.zeros_like(dv_s)
        for r0, k0, width, c0 in _attn_windows(seq):
            qb = qn_s[r0:r0 + Q_BLOCK, :]
            dob = do_ref[r0:r0 + Q_BLOCK, :]
            kw = kn_s[k0:k0 + width, :]
            vw = v_ref[k0:k0 + width, :]
            dq_acc = jnp.zeros((Q_BLOCK, LANES), F32)
            dk_acc = jnp.zeros((width, LANES), F32)
            dv_acc = jnp.zeros((width, LANES), F32)
            for h, msk in enumerate(masks):
                qh = jnp.where(msk, qb, jnp.zeros_like(qb))
                doh = jnp.where(msk, dob, jnp.zeros_like(dob))
                s = _dot(qh, kw, 1, 1) * ATTN_SCALE + b_ref[h, :, c0:KEY_WIN]
                p = _softmax_rows(s)
                dp = _dot(doh, vw, 1, 1)
                ds = p * (dp - jnp.sum(p * dp, axis=-1, keepdims=True))
                db_ref[h, :, c0:KEY_WIN] += ds
                dsb = (ds * ATTN_SCALE).astype(BF16)
                dq_acc = jnp.where(msk, _dot(dsb, kw, 1, 0), dq_acc)
                dk_acc = jnp.where(msk, _dot(dsb, qb, 0, 0), dk_acc)
                dv_acc = jnp.where(msk, _dot(p.astype(BF16), dob, 0, 0), dv_acc)
            dqn_s[r0:r0 + Q_BLOCK, :] = dq_acc
            dkn_s[k0:k0 + width, :] += dk_acc
            dv_s[k0:k0 + width, :] += dv_acc
        dq, dgq = _head_norm_bwd(dqn_s[...], qhat, rq, gq_ref[...], masks)
        dk, dgk = _head_norm_bwd(dkn_s[...], khat, rk, gk_ref[...], masks)
        dq_ref[...] = dq.astype(dq_ref.dtype)
        dk_ref[...] = dk.astype(dk_ref.dtype)
        dv_ref[...] = dv_s[...].astype(dv_ref.dtype)
        dgq_ref[...] += dgq
        dgk_ref[...] += dgk

    def col(off):
        return pl.BlockSpec((seq, LANES), lambda p, b: (b, off + p))

    vec = pl.BlockSpec((1, LANES), lambda p, b: (0, 0))
    blk = pl.BlockSpec((2, Q_BLOCK, KEY_WIN), lambda p, b: (p, 0, 0))
    o_shape = jax.ShapeDtypeStruct((n, TOK_WIDTH), BF16)
    v_shape = jax.ShapeDtypeStruct((1, LANES), F32)
    return pl.pallas_call(
        body,
        out_shape=(o_shape, o_shape, o_shape, jax.ShapeDtypeStruct(bias.shape, F32), v_shape, v_shape),
        grid=(pairs, batch),
        in_specs=[col(0), col(pairs), col(2 * pairs), col(0), vec, vec, blk],
        out_specs=(col(0), col(0), col(0), blk, vec, vec),
        scratch_shapes=[pltpu.VMEM((seq, LANES), BF16), pltpu.VMEM((seq, LANES), BF16),
                        pltpu.VMEM((seq, LANES), F32), pltpu.VMEM((seq, LANES), F32), pltpu.VMEM((seq, LANES), F32)],
        compiler_params=_params("arbitrary", "arbitrary"), name="attn_bwd")(z, z, z, dcat, gq2, gk2, bias)


MEM_ROWS = 512


def memattn_fwd(z, kv, gq4, gk4, batch, seq, qcol, name):
    n = z.shape[0]
    mtok = kv.shape[0] // batch
    rows = min(MEM_ROWS, seq)

    def body(q_ref, kv_ref, gq_ref, gk_ref, o_ref):
        masks = _group_masks(MEM_WIDTH)
        kn = _head_norm(kv_ref[:, 0:MEM_WIDTH], gk_ref[...], masks)[0].astype(BF16)
        vm = kv_ref[:, MEM_WIDTH:2 * MEM_WIDTH].astype(BF16)
        for t in range(seq // rows):
            sl = slice(t * rows, (t + 1) * rows)
            qn = _head_norm(q_ref[sl, :].astype(F32), gq_ref[...], masks)[0].astype(BF16)
            out = jnp.zeros((rows, MEM_WIDTH), F32)
            for msk in masks:
                qh = jnp.where(msk, qn, jnp.zeros_like(qn))
                p = _softmax_rows(_dot(qh, kn, 1, 1) * ATTN_SCALE).astype(BF16)
                out = jnp.where(msk, _dot(p, vm, 1, 0), out)
            o_ref[sl, :] = out.astype(o_ref.dtype)

    vec = pl.BlockSpec((1, MEM_WIDTH), lambda b: (0, 0))
    return pl.pallas_call(
        body, out_shape=jax.ShapeDtypeStruct((n, MEM_WIDTH), BF16), grid=(batch,),
        in_specs=[pl.BlockSpec((seq, MEM_WIDTH), lambda b: (b, qcol)),
                  pl.BlockSpec((mtok, 2 * MEM_WIDTH), lambda b: (b, 0)), vec, vec],
        out_specs=pl.BlockSpec((seq, MEM_WIDTH), lambda b: (b, 0)),
        compiler_params=_params("parallel"), name=name)(z, kv, gq4, gk4)


def memattn_bwd(z, kv, dcat, gq4, gk4, batch, seq, qcol, name):
    n = z.shape[0]
    mtok = kv.shape[0] // batch
    rows = min(MEM_ROWS, seq)

    def body(q_ref, kv_ref, do_ref, gq_ref, gk_ref, dq_ref, dkv_ref, dgq_ref, dgk_ref):
        @pl.when(pl.program_id(0) == 0)
        def _():
            dgq_ref[...] = jnp.zeros_like(dgq_ref)
            dgk_ref[...] = jnp.zeros_like(dgk_ref)

        masks = _group_masks(MEM_WIDTH)
        kn_f, khat, rk = _head_norm(kv_ref[:, 0:MEM_WIDTH], gk_ref[...], masks)
        kn = kn_f.astype(BF16)
        vm = kv_ref[:, MEM_WIDTH:2 * MEM_WIDTH].astype(BF16)
        dkn = jnp.zeros((mtok, MEM_WIDTH), F32)
        dvm = jnp.zeros((mtok, MEM_WIDTH), F32)
        dgq = jnp.zeros((1, MEM_WIDTH), F32)
        for t in range(seq // rows):
            sl = slice(t * rows, (t + 1) * rows)
            qn_f, qhat, rq = _head_norm(q_ref[sl, :].astype(F32), gq_ref[...], masks)
            qn = qn_f.astype(BF16)
            dob = do_ref[sl, :]
            dqn = jnp.zeros((rows, MEM_WIDTH), F32)
            for msk in masks:
                qh = jnp.where(msk, qn, jnp.zeros_like(qn))
                doh = jnp.where(msk, dob, jnp.zeros_like(dob))
                p = _softmax_rows(_dot(qh, kn, 1, 1) * ATTN_SCALE)
                dp = _dot(doh, vm, 1, 1)
                ds = p * (dp - jnp.sum(p * dp, axis=-1, keepdims=True))
                dsb = (ds * ATTN_SCALE).astype(BF16)
                dqn = jnp.where(msk, _dot(dsb, kn, 1, 0), dqn)
                dkn = dkn + jnp.where(msk, _dot(dsb, qn, 0, 0), 0.0)
                dvm = dvm + jnp.where(msk, _dot(p.astype(BF16), dob, 0, 0), 0.0)
            dq, dg = _head_norm_bwd(dqn, qhat, rq, gq_ref[...], masks)
            dq_ref[sl, :] = dq.astype(dq_ref.dtype)
            dgq = dgq + dg
        dk, dgk = _head_norm_bwd(dkn, khat, rk, gk_ref[...], masks)
        dkv_ref[:, 0:MEM_WIDTH] = dk
        dkv_ref[:, MEM_WIDTH:2 * MEM_WIDTH] = dvm
        dgq_ref[...] += dgq
        dgk_ref[...] += dgk

    vec = pl.BlockSpec((1, MEM_WIDTH), lambda b: (0, 0))
    kv_spec = pl.BlockSpec((mtok, 2 * MEM_WIDTH), lambda b: (b, 0))
    v_shape = jax.ShapeDtypeStruct((1, MEM_WIDTH), F32)
    return pl.pallas_call(
        body,
        out_shape=(jax.ShapeDtypeStruct((n, MEM_WIDTH), BF16), jax.ShapeDtypeStruct(kv.shape, F32), v_shape, v_shape),
        grid=(batch,),
        in_specs=[pl.BlockSpec((seq, MEM_WIDTH), lambda b: (b, qcol)), kv_spec,
                  pl.BlockSpec((seq, MEM_WIDTH), lambda b: (b, TOK_WIDTH // MEM_WIDTH)), vec, vec],
        out_specs=(pl.BlockSpec((seq, MEM_WIDTH), lambda b: (b, 0)), kv_spec, vec, vec),
        compiler_params=_params("arbitrary"), name=name)(z, kv, dcat, gq4, gk4)


CONV_ROWS = 256


def _glu(a_ref, g_ref):
    return a_ref[...].astype(F32) * _sigmoid(g_ref[...].astype(F32))


def _layer_norm_stats(y):
    mu = jnp.mean(y, axis=-1, keepdims=True)
    yc = y - mu
    rstd = lax.rsqrt(jnp.mean(yc * yc, axis=-1, keepdims=True) + NORM_EPS)
    return yc * rstd, rstd


def conv_fwd(z, cw, cb, lg, lb, batch, seq):
    n = z.shape[0]
    nt = seq // CONV_ROWS
    sub = CONV_ROWS // CONV_HALO
    lead = CONV_HALO - (CONV_W - 1)

    def body(a_ref, g_ref, ap_ref, gp_ref, cw_ref, cb_ref, lg_ref, lb_ref, o_ref, win):
        first = pl.program_id(1) == 0
        win[0:CONV_HALO, :] = jnp.where(first, 0.0, _glu(ap_ref, gp_ref))
        win[CONV_HALO:CONV_HALO + CONV_ROWS, :] = _glu(a_ref, g_ref)
        y = jnp.zeros((CONV_ROWS, TOK_WIDTH), F32) + cb_ref[...]
        for w in range(CONV_W):
            y = y + win[lead + w:lead + w + CONV_ROWS, :] * cw_ref[w:w + 1, :]
        yh, _ = _layer_norm_stats(y)
        t = yh * lg_ref[...] + lb_ref[...]
        o_ref[...] = (t * _sigmoid(t)).astype(o_ref.dtype)

    def cur(c):
        return pl.BlockSpec((CONV_ROWS, TOK_WIDTH), lambda b, i: (b * nt + i, c))

    def prev(c):
        return pl.BlockSpec((CONV_HALO, TOK_WIDTH), lambda b, i: (jnp.maximum((b * nt + i) * sub - 1, 0), c))

    vec = pl.BlockSpec((1, TOK_WIDTH), lambda b, i: (0, 0))
    return pl.pallas_call(
        body, out_shape=jax.ShapeDtypeStruct((n, TOK_WIDTH), BF16), grid=(batch, nt),
        in_specs=[cur(0), cur(1), prev(0), prev(1), pl.BlockSpec((32, TOK_WIDTH), lambda b, i: (0, 0)), vec, vec, vec],
        out_specs=pl.BlockSpec((CONV_ROWS, TOK_WIDTH), lambda b, i: (b * nt + i, 0)),
        scratch_shapes=[pltpu.VMEM((CONV_HALO + CONV_ROWS, TOK_WIDTH), F32)],
        compiler_params=_params("parallel", "arbitrary"), name="conv_fwd")(z, z, z, z, cw, cb, lg, lb)


def conv_bwd(z, dcat, cw, cb, lg, lb, batch, seq):
    n = z.shape[0]
    nt = seq // CONV_ROWS
    sub = CONV_ROWS // CONV_HALO
    lead = CONV_HALO - (CONV_W - 1)
    ext = CONV_ROWS + CONV_HALO
    last_blk = n // CONV_HALO - 1

    def body(a_ref, g_ref, ap_ref, gp_ref, an_ref, gn_ref, do_ref, don_ref, cw_ref, cb_ref, lg_ref, lb_ref,
             da_ref, dg_ref, dcw_ref, dsm_ref, win, dyw):
        b, i = pl.program_id(0), pl.program_id(1)
        first, last = i == 0, i == nt - 1

        @pl.when((b == 0) & (i == 0))
        def _():
            dcw_ref[...] = jnp.zeros_like(dcw_ref)
            dsm_ref[...] = jnp.zeros_like(dsm_ref)

        av = a_ref[...].astype(F32)
        sg = _sigmoid(g_ref[...].astype(F32))
        win[0:CONV_HALO, :] = jnp.where(first, 0.0, _glu(ap_ref, gp_ref))
        win[CONV_HALO:CONV_HALO + CONV_ROWS, :] = av * sg
        win[CONV_HALO + CONV_ROWS:, :] = jnp.where(last, 0.0, _glu(an_ref, gn_ref))
        y = jnp.zeros((ext, TOK_WIDTH), F32) + cb_ref[...]
        for w in range(CONV_W):
            y = y + win[lead + w:lead + w + ext, :] * cw_ref[w:w + 1, :]
        yh, rstd = _layer_norm_stats(y)
        t = yh * lg_ref[...] + lb_ref[...]
        st = _sigmoid(t)
        dout = jnp.concatenate(
            [do_ref[...].astype(F32), jnp.where(last, 0.0, don_ref[...].astype(F32))], axis=0)
        dt = dout * st * (1.0 + t * (1.0 - st))
        dyh = dt * lg_ref[...]
        dy = rstd * (dyh - jnp.mean(dyh, axis=-1, keepdims=True) - yh * jnp.mean(dyh * yh, axis=-1, keepdims=True))
        dyw[...] = dy
        dsm_ref[0:1, :] += jnp.sum(dy[0:CONV_ROWS], axis=0, keepdims=True)
        dsm_ref[1:2, :] += jnp.sum((dt * yh)[0:CONV_ROWS], axis=0, keepdims=True)
        dsm_ref[2:3, :] += jnp.sum(dt[0:CONV_ROWS], axis=0, keepdims=True)
        dyc = dyw[0:CONV_ROWS, :]
        dglu = jnp.zeros((CONV_ROWS, TOK_WIDTH), F32)
        for w in range(CONV_W):
            dcw_ref[w:w + 1, :] += jnp.sum(dyc * win[lead + w:lead + w + CONV_ROWS, :], axis=0, keepdims=True)
            back = CONV_W - 1 - w
            dglu = dglu + dyw[back:back + CONV_ROWS, :] * cw_ref[w:w + 1, :]
        da_ref[...] = (dglu * sg).astype(da_ref.dtype)
        dg_ref[...] = (dglu * av * sg * (1.0 - sg)).astype(dg_ref.dtype)

    def cur(c):
        return pl.BlockSpec((CONV_ROWS, TOK_WIDTH), lambda b, i: (b * nt + i, c))

    def prev(c):
        return pl.BlockSpec((CONV_HALO, TOK_WIDTH), lambda b, i: (jnp.maximum((b * nt + i) * sub - 1, 0), c))

    def nxt(c):
        return pl.BlockSpec((CONV_HALO, TOK_WIDTH), lambda b, i: (jnp.minimum((b * nt + i + 1) * sub, last_blk), c))

    vec = pl.BlockSpec((1, TOK_WIDTH), lambda b, i: (0, 0))
    full32 = pl.BlockSpec((32, TOK_WIDTH), lambda b, i: (0, 0))
    o_shape = jax.ShapeDtypeStruct((n, TOK_WIDTH), BF16)
    return pl.pallas_call(
        body,
        out_shape=(o_shape, o_shape, jax.ShapeDtypeStruct((32, TOK_WIDTH), F32), jax.ShapeDtypeStruct((8, TOK_WIDTH), F32)),
        grid=(batch, nt),
        in_specs=[cur(0), cur(1), prev(0), prev(1), nxt(0), nxt(1), cur(0), nxt(0), full32, vec, vec, vec],
        out_specs=(cur(0), cur(0), full32, pl.BlockSpec((8, TOK_WIDTH), lambda b, i: (0, 0))),
        scratch_shapes=[pltpu.VMEM((CONV_HALO + ext, TOK_WIDTH), F32), pltpu.VMEM((ext, TOK_WIDTH), F32)],
        compiler_params=_params("arbitrary", "arbitrary"), name="conv_bwd")(
            z, z, z, z, z, z, dcat, dcat, cw, cb, lg, lb)


def loss_head(y, target):
    n, d = y.shape
    tm = _row_tile(n)
    nt = n // tm

    def body(y_ref, t_ref, dy_ref, l_ref, acc_ref):
        i = pl.program_id(0)

        @pl.when(i == 0)
        def _():
            acc_ref[...] = jnp.zeros_like(acc_ref)

        err = y_ref[...] - t_ref[...]
        dy_ref[...] = err * (1.0 / d)
        acc_ref[...] += jnp.sum(err * err, axis=0, keepdims=True)

        @pl.when(i == nt - 1)
        def _():
            total = jnp.sum(acc_ref[...], axis=-1, keepdims=True) * (0.5 / d)
            l_ref[...] = jnp.broadcast_to(total, l_ref.shape)

    row = pl.BlockSpec((tm, d), lambda i: (i, 0))
    return pl.pallas_call(
        body, out_shape=(jax.ShapeDtypeStruct((n, d), F32), jax.ShapeDtypeStruct((8, LANES), F32)), grid=(nt,),
        in_specs=[row, row], out_specs=(row, pl.BlockSpec((8, LANES), lambda i: (0, 0))),
        scratch_shapes=[pltpu.VMEM((1, d), F32)],
        compiler_params=_params("arbitrary"), name="loss_head")(y, target)


def col_sum(x, name="col_sum"):
    n, c = x.shape
    tm = _row_tile(n)

    def body(x_ref, o_ref):
        @pl.when(pl.program_id(0) == 0)
        def _():
            o_ref[...] = jnp.zeros_like(o_ref)

        o_ref[...] += jnp.sum(x_ref[...].astype(F32), axis=0, keepdims=True)

    return pl.pallas_call(
        body, out_shape=jax.ShapeDtypeStruct((1, c), F32), grid=(n // tm,),
        in_specs=[pl.BlockSpec((tm, c), lambda i: (i, 0))], out_specs=pl.BlockSpec((1, c), lambda i: (0, 0)),
        compiler_params=_params("arbitrary"), name=name)(x)


def adamw(w, g, m, v, name="adamw"):
    rows, cols = w.shape
    tr = rows
    for cand in (512, 256, 128, 64, 32, 16, 8):
        if rows % cand == 0 and rows > cand:
            tr = cand
            break
    c1 = 1.0 / (1.0 - ADAM_B1 ** ADAM_STEP)
    c2 = 1.0 / (1.0 - ADAM_B2 ** ADAM_STEP)

    def body(w_ref, g_ref, m_ref, v_ref, d_ref, nm_ref, nv_ref):
        gv = g_ref[...]
        nm = ADAM_B1 * m_ref[...] + (1.0 - ADAM_B1) * gv
        nv = ADAM_B2 * v_ref[...] + (1.0 - ADAM_B2) * (gv * gv)
        nm_ref[...] = nm
        nv_ref[...] = nv
        d_ref[...] = -ADAM_LR * ((nm * c1) / (jnp.sqrt(nv * c2) + ADAM_EPS) + ADAM_WD * w_ref[...])

    spec = pl.BlockSpec((tr, cols), lambda i: (i, 0))
    shape = jax.ShapeDtypeStruct((rows, cols), F32)
    return pl.pallas_call(
        body, out_shape=(shape, shape, shape), grid=(rows // tr,),
        in_specs=[spec, spec, spec, spec], out_specs=(spec, spec, spec),
        compiler_params=_params("parallel"), name=name)(w, g, m, v)


def _place():
    return lax.axis_index("x"), lax.axis_index("y"), lax.axis_index("c")


def _other_chips(x, y):
    return [(1 - x, y), (x, 1 - y), (1 - x, 1 - y)]


def small_exchange(slab, reduce):
    r = slab.shape[0]

    def body(in_ref, o_ref, *scratch):
        if reduce:
            buf, send_sems, recv_sems = scratch
        else:
            buf = o_ref
            send_sems, recv_sems = scratch
        x, y, c = _place()
        me = 4 * x + 2 * y + c
        buf[me] = in_ref[...]
        copies = []
        for k in range(1, N_DEV):
            peer = (x ^ (k >> 2), y ^ ((k >> 1) & 1), c ^ (k & 1))
            cp = pltpu.make_async_remote_copy(
                src_ref=in_ref, dst_ref=buf.at[me], send_sem=send_sems.at[k - 1], recv_sem=recv_sems.at[k - 1],
                device_id=peer, device_id_type=MESH)
            cp.start()
            copies.append(cp)
        for k in range(1, N_DEV):
            src = 4 * (x ^ (k >> 2)) + 2 * (y ^ ((k >> 1) & 1)) + (c ^ (k & 1))
            pltpu.make_async_remote_copy(
                src_ref=in_ref, dst_ref=buf.at[src], send_sem=send_sems.at[k - 1], recv_sem=recv_sems.at[k - 1],
                device_id=(x, y, c), device_id_type=MESH).wait_recv()
        for cp in copies:
            cp.wait_send()
        if reduce:
            total = buf[0]
            for d in range(1, N_DEV):
                total = total + buf[d]
            o_ref[...] = total

    sems = [pltpu.SemaphoreType.DMA((N_DEV - 1,)), pltpu.SemaphoreType.DMA((N_DEV - 1,))]
    if reduce:
        out_shape = jax.ShapeDtypeStruct((r, LANES), F32)
        scratch = [pltpu.VMEM((N_DEV, r, LANES), F32)] + sems
    else:
        out_shape = jax.ShapeDtypeStruct((N_DEV, r, LANES), F32)
        scratch = sems
    vmem = pl.BlockSpec(memory_space=pltpu.VMEM)
    return pl.pallas_call(
        body, out_shape=out_shape, in_specs=[vmem], out_specs=vmem, scratch_shapes=scratch,
        compiler_params=pltpu.CompilerParams(vmem_limit_bytes=VMEM_LIMIT),
        name="small_reduce" if reduce else "small_gather")(slab)


def gather_weights(shards, name, collective_id):
    nw = len(shards)
    ns = [s.shape[0] for s in shards]
    in_refs = [jax.new_ref(s, memory_space=pltpu.MemorySpace.HBM) for s in shards]
    out_refs = [jax.empty_ref(jax.ShapeDtypeStruct((N_DEV * s.shape[0], s.shape[1]), s.dtype),
                              memory_space=pltpu.MemorySpace.HBM) for s in shards]

    @pl.kernel(mesh=plsc.ScalarSubcoreMesh(axis_name="seq", num_cores=1), name=name,
               scratch_types=(pltpu.SemaphoreType.DMA((nw, 7)), pltpu.SemaphoreType.DMA((nw, 7)),
                              pltpu.SemaphoreType.DMA((nw,))),
               compiler_params=pltpu.CompilerParams(collective_id=collective_id))
    def launch(send_sems, recv_sems, local_sems):
        x, y, c = _place()
        me, sib = (x, y, c), (x, y, 1 - c)
        chips = _other_chips(x, y)
        barrier = pltpu.get_barrier_semaphore()
        for peer in [sib] + [(*chip, c) for chip in chips]:
            pl.semaphore_signal(barrier, inc=1, device_id=peer, device_id_type=MESH)
        pl.semaphore_wait(barrier, 4)

        def rows(w, dev):
            return out_refs[w].at[pl.ds((4 * dev[0] + 2 * dev[1] + dev[2]) * ns[w], ns[w]), :]

        def copy(w, k, block, to, src=None):
            return pltpu.make_async_remote_copy(
                src_ref=rows(w, block) if src is None else src, dst_ref=rows(w, block),
                send_sem=send_sems.at[w, k], recv_sem=recv_sems.at[w, k], device_id=to, device_id_type=MESH)

        started, sends = [], []
        for w in range(nw):
            mine = pltpu.make_async_copy(in_refs[w], rows(w, me), local_sems.at[w])
            mine.start()
            started.append(mine)
            first = [copy(w, 0, me, sib, src=in_refs[w])]
            first += [copy(w, 1 + j, me, (*chip, c), src=in_refs[w]) for j, chip in enumerate(chips)]
            for cp in first:
                cp.start()
            sends += first
        for w in range(nw):
            for j, chip in enumerate(chips):
                copy(w, 1 + j, (*chip, c), me).wait_recv()
                fwd = copy(w, 4 + j, (*chip, c), sib)
                fwd.start()
                sends.append(fwd)
        for w in range(nw):
            copy(w, 0, sib, me).wait_recv()
            for j, chip in enumerate(chips):
                copy(w, 4 + j, (*chip, 1 - c), me).wait_recv()
        for cp in sends:
            cp.wait_send()
        for mine in started:
            mine.wait()

    launch()
    return [r[...] for r in out_refs]


def scatter_to_sibling(grads):
    nw = len(grads)
    ns = [g.shape[0] // N_DEV for g in grads]

    def body(*refs):
        ins, outs = refs[:nw], refs[nw:2 * nw]
        send_sems, recv_sems = refs[2 * nw:]
        x, y, c = _place()
        sib = (x, y, 1 - c)
        copies = []
        for w in range(nw):
            for q in range(4):
                cp = pltpu.make_async_remote_copy(
                    src_ref=ins[w].at[pl.ds((2 * q + 1 - c) * ns[w], ns[w]), :],
                    dst_ref=outs[w].at[pl.ds(q * ns[w], ns[w]), :],
                    send_sem=send_sems.at[w, q], recv_sem=recv_sems.at[w, q], device_id=sib, device_id_type=MESH)
                cp.start()
                copies.append(cp)
        for cp in copies:
            cp.wait_recv()
        for cp in copies:
            cp.wait_send()

    out_shape = tuple(jax.ShapeDtypeStruct((4 * n, g.shape[1]), g.dtype) for n, g in zip(ns, grads))
    return pl.pallas_call(
        body, out_shape=out_shape, in_specs=[ANY] * nw, out_specs=tuple([ANY] * nw),
        scratch_shapes=[pltpu.SemaphoreType.DMA((nw, 4)), pltpu.SemaphoreType.DMA((nw, 4))],
        name="scatter_to_sibling")(*grads)


def scatter_to_chips(parts):
    nw = len(parts)
    ns = [p.shape[0] // 4 for p in parts]

    def body(*refs):
        ins, outs = refs[:nw], refs[nw:2 * nw]
        send_sems, recv_sems = refs[2 * nw:]
        x, y, c = _place()
        copies = []
        for w in range(nw):
            for j, chip in enumerate(_other_chips(x, y)):
                cp = pltpu.make_async_remote_copy(
                    src_ref=ins[w].at[pl.ds((2 * chip[0] + chip[1]) * ns[w], ns[w]), :],
                    dst_ref=outs[w].at[pl.ds(j * ns[w], ns[w]), :],
                    send_sem=send_sems.at[w, j], recv_sem=recv_sems.at[w, j],
                    device_id=(*chip, c), device_id_type=MESH)
                cp.start()
                copies.append(cp)
        for cp in copies:
            cp.wait_recv()
        for cp in copies:
            cp.wait_send()

    out_shape = tuple(jax.ShapeDtypeStruct((3 * n, p.shape[1]), p.dtype) for n, p in zip(ns, parts))
    return pl.pallas_call(
        body, out_shape=out_shape, in_specs=[ANY] * nw, out_specs=tuple([ANY] * nw),
        scratch_shapes=[pltpu.SemaphoreType.DMA((nw, 3)), pltpu.SemaphoreType.DMA((nw, 3))],
        name="scatter_to_chips")(*parts)


def add_sibling(grad, landed, core, name):
    n = landed.shape[0] // 4
    cols = grad.shape[1]

    def body(c_ref, g_ref, l_ref, o_ref):
        o_ref[...] = (g_ref[...].astype(F32) + l_ref[...].astype(F32)).astype(o_ref.dtype)

    grid_spec = pltpu.PrefetchScalarGridSpec(
        num_scalar_prefetch=1, grid=(4,),
        in_specs=[pl.BlockSpec((n, cols), lambda q, c_ref: (2 * q + c_ref[0], 0)),
                  pl.BlockSpec((n, cols), lambda q, c_ref: (q, 0))],
        out_specs=pl.BlockSpec((n, cols), lambda q, c_ref: (q, 0)))
    return pl.pallas_call(
        body, out_shape=jax.ShapeDtypeStruct(landed.shape, landed.dtype), grid_spec=grid_spec,
        compiler_params=_params("arbitrary"), name=name)(core, grad, landed)


def add_chips(part, landed, chip, name):
    n = landed.shape[0] // 3
    cols = part.shape[1]

    def body(q_ref, p_ref, l0_ref, l1_ref, l2_ref, o_ref):
        o_ref[...] = ((p_ref[...].astype(F32) + l0_ref[...].astype(F32)) + l1_ref[...].astype(F32)) \
            + l2_ref[...].astype(F32)

    def landed_spec(j):
        return pl.BlockSpec((n, cols), lambda i, q_ref: (j, 0))

    grid_spec = pltpu.PrefetchScalarGridSpec(
        num_scalar_prefetch=1, grid=(1,),
        in_specs=[pl.BlockSpec((n, cols), lambda i, q_ref: (q_ref[0], 0)), landed_spec(0), landed_spec(1), landed_spec(2)],
        out_specs=pl.BlockSpec((n, cols), lambda i, q_ref: (0, 0)))
    return pl.pallas_call(
        body, out_shape=jax.ShapeDtypeStruct((n, cols), F32), grid_spec=grid_spec,
        compiler_params=_params("arbitrary"), name=name)(chip, part, landed, landed, landed)


def _pack(arrays):
    flat = jnp.concatenate([a.reshape(-1).astype(F32) for a in arrays])
    pad = (-flat.shape[0]) % (8 * LANES)
    return jnp.pad(flat, (0, pad)).reshape(-1, LANES)


def _unpack(slab, shapes):
    flat = slab.reshape(slab.shape[:-2] + (-1,))
    out, off = [], 0
    for shp in shapes:
        size = 1
        for s in shp:
            size *= s
        out.append(flat[..., off:off + size].reshape(flat.shape[:-1] + tuple(shp)))
        off += size
    return out


def kernel(x, mem, norm1_g, mem_norm_g, a_w_in, a_q_g, a_k_g, a_rel_bias, b_w_in, b_b_in, b_conv_w, b_conv_b, b_ln_g, b_ln_b, mq_g, mk_g, w_mem_kv, w_out, norm2_g, w_gate, w_up, w_down, loss_target, m_norm1_g, m_mem_norm_g, m_a_w_in, m_a_q_g, m_a_k_g, m_a_rel_bias, m_b_w_in, m_b_b_in, m_b_conv_w, m_b_conv_b, m_b_ln_g, m_b_ln_b, m_mq_g, m_mk_g, m_w_mem_kv, m_w_out, m_norm2_g, m_w_gate, m_w_up, m_w_down, v_norm1_g, v_mem_norm_g, v_a_w_in, v_a_q_g, v_a_k_g, v_a_rel_bias, v_b_w_in, v_b_b_in, v_b_conv_w, v_b_conv_b, v_b_ln_g, v_b_ln_b, v_mq_g, v_mk_g, v_w_mem_kv, v_w_out, v_norm2_g, v_w_gate, v_w_up, v_w_down):
    batch, seq, d = x.shape
    mtok = mem.shape[1]
    n = batch * seq
    ax, ay, ac = _place()
    me = 4 * ax + 2 * ay + ac
    core_arr = jnp.reshape(ac, (1,)).astype(jnp.int32)
    chip_arr = jnp.reshape(2 * ax + ay, (1,)).astype(jnp.int32)

    def t_bf16(w):
        return jnp.transpose(w).astype(BF16)

    a_win_t, = gather_weights([t_bf16(a_w_in[0])], "gather_in_a", 1)
    wkv0, wo0, wg0, wu0, wd0 = gather_weights(
        [w_mem_kv[0].astype(BF16), w_out[0].astype(BF16), t_bf16(w_gate[0]), t_bf16(w_up[0]), w_down[0].astype(BF16)],
        "gather_layer_a", 2)
    b_win_t, wkv1, wo1 = gather_weights(
        [t_bf16(b_w_in[0]), w_mem_kv[1].astype(BF16), w_out[1].astype(BF16)], "gather_in_b", 3)
    wg1, wu1, wd1 = gather_weights([t_bf16(w_gate[1]), t_bf16(w_up[1]), w_down[1].astype(BF16)], "gather_ffn_b", 4)
    wg_t, wu_t, wd, wo, wkv = [wg0, wg1], [wu0, wu1], [wd0, wd1], [wo0, wo1], [wkv0, wkv1]

    f_loc = b_b_in.shape[1]
    c_loc = b_conv_b.shape[1]
    small_shapes = [(f_loc,), (CONV_W, c_loc), (c_loc,), (c_loc,), (c_loc,)]
    gathered = small_exchange(_pack([b_b_in, b_conv_w, b_conv_b, b_ln_g, b_ln_b]), reduce=False)
    bb_g, cw_g, cb_g, lg_g, lb_g = _unpack(gathered, small_shapes)
    bb_full = bb_g.reshape(1, -1)
    cw_full = jnp.pad(jnp.transpose(cw_g, (1, 0, 2)).reshape(CONV_W, -1), ((0, 32 - CONV_W), (0, 0)))
    cb_full, lg_full, lb_full = cb_g.reshape(1, -1), lg_g.reshape(1, -1), lb_g.reshape(1, -1)

    def two(g):
        return jnp.concatenate([g, g], axis=-1)

    gq2, gk2 = two(a_q_g), two(a_k_g)
    rel16 = jnp.pad(a_rel_bias[0], ((0, 16 - a_rel_bias.shape[1]), (0, 0)))
    bias = bias_blocks(rel16)

    x0 = x.reshape(n, d)
    mem2 = mem.reshape(batch * mtok, d)
    zero_mem = jnp.zeros_like(mem2)

    saved = []
    xin = x0
    for l in range(2):
        h = rms_fwd(xin, norm1_g[l:l + 1], name=f"rms1_fwd_{l}")
        mem_n = rms_fwd(mem2, mem_norm_g[l:l + 1], name=f"rms_mem_fwd_{l}")
        kv = mm_nn(mem_n, wkv[l], name=f"mem_kv_{l}")
        gq4 = jnp.tile(mq_g[l:l + 1], (1, 4))
        gk4 = jnp.tile(mk_g[l:l + 1], (1, 4))
        if l == 0:
            z = mm_nt(h, a_win_t, name="in_proj_a")
            tok = attn_fwd(z, gq2, gk2, bias, batch, seq)
            qcol = 3 * TOK_WIDTH // MEM_WIDTH
        else:
            z = mm_nt(h, b_win_t, bias=bb_full, name="in_proj_b")
            tok = conv_fwd(z, cw_full, cb_full, lg_full, lb_full, batch, seq)
            qcol = 2 * TOK_WIDTH // MEM_WIDTH
        memo = memattn_fwd(z, kv, gq4, gk4, batch, seq, qcol, name=f"memattn_fwd_{l}")
        cat = jnp.concatenate([tok, memo], axis=-1)
        x1 = mm_nn(cat, wo[l], res=xin, name=f"out_proj_{l}")
        h2 = rms_fwd(x1, norm2_g[l:l + 1], name=f"rms2_fwd_{l}")
        gate, up, act = gate_up(h2, wg_t[l], wu_t[l], name=f"gate_up_{l}")
        x2 = mm_nn(act, wd[l], res=x1, name=f"down_proj_{l}")
        saved.append(dict(xin=xin, h=h, mem_n=mem_n, kv=kv, gq4=gq4, gk4=gk4, z=z, qcol=qcol, cat=cat, x1=x1, h2=h2,
                          gate=gate, up=up, act=act))
        xin = x2

    dx, loss_blk = loss_head(xin, loss_target.reshape(n, d))
    loss = lax.psum(loss_blk[0, 0], ("x", "y", "c"))

    big = {}
    small = {}
    for l in (1, 0):
        sv = saved[l]
        dgate, dup = ffn_bwd_act(dx, wd[l], sv["gate"], sv["up"], name=f"ffn_bwd_act_{l}")
        big[f"wd{l}"] = mm_tn(sv["act"], dx, name=f"grad_wd_{l}")
        dh2 = mm2_nn(dgate, wg_t[l], dup, wu_t[l], name=f"ffn_bwd_h_{l}")
        big[f"wg{l}"] = mm_tn(dgate, sv["h2"], name=f"grad_wg_{l}")
        big[f"wu{l}"] = mm_tn(dup, sv["h2"], name=f"grad_wu_{l}")
        dx1, small[f"norm2_{l}"] = rms_bwd(dh2, sv["x1"], norm2_g[l:l + 1], dx, name=f"rms2_bwd_{l}")
        dcat = mm_nt(dx1, wo[l], name=f"out_proj_bwd_{l}")
        big[f"wo{l}"] = mm_tn(sv["cat"], dx1, name=f"grad_wo_{l}")
        dqm, dkv, small[f"mq_{l}"], small[f"mk_{l}"] = memattn_bwd(
            sv["z"], sv["kv"], dcat, sv["gq4"], sv["gk4"], batch, seq, sv["qcol"], name=f"memattn_bwd_{l}")
        if l == 0:
            dq, dk, dv, dbias, small["a_q"], small["a_k"] = attn_bwd(sv["z"], dcat, gq2, gk2, bias, batch, seq)
            small["rel"] = bias_grad(dbias)
            dz = jnp.concatenate([dq, dk, dv, dqm], axis=-1)
            win_t = a_win_t
        else:
            da, dg, small["cw"], small["csum"] = conv_bwd(
                sv["z"], dcat, cw_full, cb_full, lg_full, lb_full, batch, seq)
            dz = jnp.concatenate([da, dg, dqm], axis=-1)
            small["bb"] = col_sum(dz, name="grad_b_in")
            win_t = b_win_t
        big[f"win{l}"] = mm_tn(dz, sv["h"], name=f"grad_win_{l}")
        dh = mm_nn(dz, win_t, name=f"in_proj_bwd_{l}")
        big[f"wkv{l}"] = mm_tn(sv["mem_n"], dkv, name=f"grad_wkv_{l}")
        dmem_n = mm_nt(dkv, wkv[l], out_dtype=F32, name=f"mem_kv_bwd_{l}")
        _, small[f"memnorm_{l}"] = rms_bwd(dmem_n, mem2, mem_norm_g[l:l + 1], zero_mem, name=f"rms_mem_bwd_{l}")
        dx, small[f"norm1_{l}"] = rms_bwd(dh, sv["xin"], norm1_g[l:l + 1], dx1, name=f"rms1_bwd_{l}")
    grad_x = dx.reshape(batch, seq, d)

    order = ["win0", "win1", "wg0", "wu0", "wd0", "wo0", "wkv0", "wg1", "wu1", "wd1", "wo1", "wkv1"]
    glist = [big[k] for k in order]
    landed1 = scatter_to_sibling(glist)
    parts = [add_sibling(g, ld, core_arr, name=f"add_sibling_{k}") for k, g, ld in zip(order, glist, landed1)]
    landed2 = scatter_to_chips(parts)
    red = {k: add_chips(p, ld, chip_arr, name=f"add_chips_{k}") for k, p, ld in zip(order, parts, landed2)}

    g_a_w_in = jnp.transpose(red["win0"])[None]
    g_b_w_in = jnp.transpose(red["win1"])[None]
    g_w_gate = jnp.stack([jnp.transpose(red["wg0"]), jnp.transpose(red["wg1"])])
    g_w_up = jnp.stack([jnp.transpose(red["wu0"]), jnp.transpose(red["wu1"])])
    g_w_down = jnp.stack([red["wd0"], red["wd1"]])
    g_w_out = jnp.stack([red["wo0"], red["wo1"]])
    g_w_mem_kv = jnp.stack([red["wkv0"], red["wkv1"]])

    def fold(v, groups):
        return jnp.sum(v.reshape(groups, HEAD_DIM), axis=0, keepdims=True)

    heads = a_rel_bias.shape[1]
    small_list = [
        jnp.concatenate([small["norm1_0"], small["norm1_1"]]),
        jnp.concatenate([small["memnorm_0"], small["memnorm_1"]]),
        fold(small["a_q"], 2), fold(small["a_k"], 2), small["rel"][:heads][None],
        small["bb"], small["cw"][:CONV_W][None], small["csum"][0:1], small["csum"][1:2], small["csum"][2:3],
        jnp.concatenate([fold(small["mq_0"], 4), fold(small["mq_1"], 4)]),
        jnp.concatenate([fold(small["mk_0"], 4), fold(small["mk_1"], 4)]),
        jnp.concatenate([small["norm2_0"], small["norm2_1"]]),
    ]
    small_full_shapes = [a.shape for a in small_list]
    summed = _unpack(small_exchange(_pack(small_list), reduce=True), small_full_shapes)
    (g_norm1, g_memnorm, g_aq, g_ak, g_rel, g_bb_full, g_cw_full, g_cb_full, g_lg_full, g_lb_full,
     g_mq, g_mk, g_norm2) = summed
    g_bb = lax.dynamic_slice_in_dim(g_bb_full, me * f_loc, f_loc, axis=1)
    g_cw = lax.dynamic_slice_in_dim(g_cw_full, me * c_loc, c_loc, axis=2)
    g_cb = lax.dynamic_slice_in_dim(g_cb_full, me * c_loc, c_loc, axis=1)
    g_lg = lax.dynamic_slice_in_dim(g_lg_full, me * c_loc, c_loc, axis=1)
    g_lb = lax.dynamic_slice_in_dim(g_lb_full, me * c_loc, c_loc, axis=1)

    grads = [g_norm1, g_memnorm, g_a_w_in, g_aq, g_ak, g_rel, g_b_w_in, g_bb, g_cw, g_cb, g_lg, g_lb,
             g_mq, g_mk, g_w_mem_kv, g_w_out, g_norm2, g_w_gate, g_w_up, g_w_down]
    weights = [norm1_g, mem_norm_g, a_w_in, a_q_g, a_k_g, a_rel_bias, b_w_in, b_b_in, b_conv_w, b_conv_b, b_ln_g,
               b_ln_b, mq_g, mk_g, w_mem_kv, w_out, norm2_g, w_gate, w_up, w_down]
    moms = [m_norm1_g, m_mem_norm_g, m_a_w_in, m_a_q_g, m_a_k_g, m_a_rel_bias, m_b_w_in, m_b_b_in, m_b_conv_w,
            m_b_conv_b, m_b_ln_g, m_b_ln_b, m_mq_g, m_mk_g, m_w_mem_kv, m_w_out, m_norm2_g, m_w_gate, m_w_up, m_w_down]
    vels = [v_norm1_g, v_mem_norm_g, v_a_w_in, v_a_q_g, v_a_k_g, v_a_rel_bias, v_b_w_in, v_b_b_in, v_b_conv_w,
            v_b_conv_b, v_b_ln_g, v_b_ln_b, v_mq_g, v_mk_g, v_w_mem_kv, v_w_out, v_norm2_g, v_w_gate, v_w_up, v_w_down]

    large = {2, 6, 14, 15, 17, 18, 19}
    deltas, new_m, new_v = [None] * 20, [None] * 20, [None] * 20
    for i in sorted(large):
        shp = weights[i].shape
        flat = (shp[0] * shp[1], shp[2])
        dl, nm, nv = adamw(weights[i].reshape(flat), grads[i].reshape(flat), moms[i].reshape(flat),
                           vels[i].reshape(flat), name=f"adamw_{i}")
        deltas[i], new_m[i], new_v[i] = dl.reshape(shp), nm.reshape(shp), nv.reshape(shp)
    small_idx = [i for i in range(20) if i not in large]
    small_shapes2 = [weights[i].shape for i in small_idx]
    dl, nm, nv = adamw(_pack([weights[i] for i in small_idx]), _pack([grads[i] for i in small_idx]),
                       _pack([moms[i] for i in small_idx]), _pack([vels[i] for i in small_idx]), name="adamw_small")
    for i, a, b, cc in zip(small_idx, _unpack(dl, small_shapes2), _unpack(nm, small_shapes2), _unpack(nv, small_shapes2)):
        deltas[i], new_m[i], new_v[i] = a, b, cc

    return (loss, grad_x, *grads, *deltas, *new_m, *new_v)
```

```python
import functools

import jax
import jax.numpy as jnp
from jax import lax
from jax.experimental import pallas as pl
from jax.experimental.pallas import tpu as pltpu
from jax.experimental.pallas import tpu_sc as plsc

F32 = jnp.float32
BF16 = jnp.bfloat16
HIGHEST = lax.Precision.HIGHEST
MESH = pl.DeviceIdType.MESH
ANY = pl.BlockSpec(memory_space=pl.ANY)

N_DEV = 8
D_MODEL = 1024
HEAD_DIM = 64
TOK_WIDTH = 768
MEM_WIDTH = 256
CHUNK = 64
Q_BLOCK = 256
KEY_WIN = 768
BAND = 576
N_REL = 192
CONV_W = 31
CONV_HALO = 32
NORM_EPS = 1e-6
NEG_INF = -1e30
ATTN_SCALE = HEAD_DIM ** -0.5
LANES = 128
ROW_TILE = 512
VMEM_LIMIT = 56 * 1024 * 1024

ADAM_LR, ADAM_B1, ADAM_B2, ADAM_EPS, ADAM_WD, ADAM_STEP = 0.001, 0.9, 0.999, 1e-08, 0.01, 10


def _params(*sem):
    return pltpu.CompilerParams(dimension_semantics=sem, vmem_limit_bytes=VMEM_LIMIT)


def _row_tile(m):
    return ROW_TILE if m % ROW_TILE == 0 else m


def _col_tile(n, cap=1408):
    best = None
    for t in range(LANES, min(n, cap) + 1, LANES):
        if n % t == 0:
            best = t
    return best if best is not None else n


def _dot(a, b, ca, cb):
    return lax.dot_general(a, b, (((ca,), (cb,)), ((), ())), preferred_element_type=F32)


def _sigmoid(x):
    return 1.0 / (1.0 + jnp.exp(-x))


def mm_nt(a, b, bias=None, out_dtype=BF16, name="mm_nt"):
    m, k = a.shape
    n = b.shape[0]
    tm, tn = _row_tile(m), _col_tile(n)

    def body(*refs):
        a_ref, b_ref = refs[0], refs[1]
        o_ref = refs[-1]
        acc = _dot(a_ref[...].astype(BF16), b_ref[...].astype(BF16), 1, 1)
        if bias is not None:
            acc = acc + refs[2][...]
        o_ref[...] = acc.astype(o_ref.dtype)

    in_specs = [pl.BlockSpec((tm, k), lambda i, j: (i, 0)), pl.BlockSpec((tn, k), lambda i, j: (j, 0))]
    args = [a, b]
    if bias is not None:
        in_specs.append(pl.BlockSpec((1, tn), lambda i, j: (0, j)))
        args.append(bias)
    return pl.pallas_call(
        body, out_shape=jax.ShapeDtypeStruct((m, n), out_dtype), grid=(m // tm, n // tn),
        in_specs=in_specs, out_specs=pl.BlockSpec((tm, tn), lambda i, j: (i, j)),
        compiler_params=_params("parallel", "arbitrary"), name=name)(*args)


def mm_nn(a, b, res=None, out_dtype=F32, name="mm_nn"):
    m, k = a.shape
    n = b.shape[1]
    tm, tn = _row_tile(m), _col_tile(n, 1024)

    def body(*refs):
        a_ref, b_ref = refs[0], refs[1]
        o_ref = refs[-1]
        acc = _dot(a_ref[...].astype(BF16), b_ref[...].astype(BF16), 1, 0)
        if res is not None:
            acc = acc + refs[2][...]
        o_ref[...] = acc.astype(o_ref.dtype)

    in_specs = [pl.BlockSpec((tm, k), lambda i, j: (i, 0)), pl.BlockSpec((k, tn), lambda i, j: (0, j))]
    args = [a, b]
    if res is not None:
        in_specs.append(pl.BlockSpec((tm, tn), lambda i, j: (i, j)))
        args.append(res)
    return pl.pallas_call(
        body, out_shape=jax.ShapeDtypeStruct((m, n), out_dtype), grid=(m // tm, n // tn),
        in_specs=in_specs, out_specs=pl.BlockSpec((tm, tn), lambda i, j: (i, j)),
        compiler_params=_params("parallel", "arbitrary"), name=name)(*args)


def mm2_nn(a1, b1, a2, b2, name="mm2_nn"):
    m, k = a1.shape
    n = b1.shape[1]
    tm, tn = _row_tile(m), _col_tile(n, 512)

    def body(a1_ref, b1_ref, a2_ref, b2_ref, o_ref):
        o_ref[...] = _dot(a1_ref[...], b1_ref[...], 1, 0) + _dot(a2_ref[...], b2_ref[...], 1, 0)

    a_spec = pl.BlockSpec((tm, k), lambda i, j: (i, 0))
    b_spec = pl.BlockSpec((k, tn), lambda i, j: (0, j))
    return pl.pallas_call(
        body, out_shape=jax.ShapeDtypeStruct((m, n), F32), grid=(m // tm, n // tn),
        in_specs=[a_spec, b_spec, a_spec, b_spec], out_specs=pl.BlockSpec((tm, tn), lambda i, j: (i, j)),
        compiler_params=_params("parallel", "arbitrary"), name=name)(a1, b1, a2, b2)


def mm_tn(a, b, out_dtype=BF16, name="mm_tn"):
    t, r = a.shape
    c = b.shape[1]
    tt, tr = _row_tile(t), _col_tile(r)
    nt = t // tt

    def body(a_ref, b_ref, o_ref, acc_ref):
        step = pl.program_id(1)

        @pl.when(step == 0)
        def _():
            acc_ref[...] = jnp.zeros_like(acc_ref)

        acc_ref[...] += _dot(a_ref[...].astype(BF16), b_ref[...].astype(BF16), 0, 0)

        @pl.when(step == nt - 1)
        def _():
            o_ref[...] = acc_ref[...].astype(o_ref.dtype)

    return pl.pallas_call(
        body, out_shape=jax.ShapeDtypeStruct((r, c), out_dtype), grid=(r // tr, nt),
        in_specs=[pl.BlockSpec((tt, tr), lambda i, s: (s, i)), pl.BlockSpec((tt, c), lambda i, s: (s, 0))],
        out_specs=pl.BlockSpec((tr, c), lambda i, s: (i, 0)),
        scratch_shapes=[pltpu.VMEM((tr, c), F32)],
        compiler_params=_params("parallel", "arbitrary"), name=name)(a, b)


def rms_fwd(x, g, name="rms_fwd"):
    n, d = x.shape
    tm = _row_tile(n)

    def body(x_ref, g_ref, o_ref):
        xv = x_ref[...]
        r = lax.rsqrt(jnp.mean(xv * xv, axis=-1, keepdims=True) + NORM_EPS)
        o_ref[...] = (xv * r * g_ref[...]).astype(o_ref.dtype)

    return pl.pallas_call(
        body, out_shape=jax.ShapeDtypeStruct((n, d), BF16), grid=(n // tm,),
        in_specs=[pl.BlockSpec((tm, d), lambda i: (i, 0)), pl.BlockSpec((1, d), lambda i: (0, 0))],
        out_specs=pl.BlockSpec((tm, d), lambda i: (i, 0)),
        compiler_params=_params("parallel"), name=name)(x, g)


def rms_bwd(dh, x, g, dres, name="rms_bwd"):
    n, d = x.shape
    tm = _row_tile(n)

    def body(dh_ref, x_ref, g_ref, dres_ref, dx_ref, dg_ref):
        @pl.when(pl.program_id(0) == 0)
        def _():
            dg_ref[...] = jnp.zeros_like(dg_ref)

        xv = x_ref[...]
        dhv = dh_ref[...].astype(F32)
        r = lax.rsqrt(jnp.mean(xv * xv, axis=-1, keepdims=True) + NORM_EPS)
        xhat = xv * r
        dg_ref[...] += jnp.sum(dhv * xhat, axis=0, keepdims=True)
        dxhat = dhv * g_ref[...]
        mean_t = jnp.mean(dxhat * xhat, axis=-1, keepdims=True)
        dx_ref[...] = dres_ref[...] + r * (dxhat - xhat * mean_t)

    row = pl.BlockSpec((tm, d), lambda i: (i, 0))
    vec = pl.BlockSpec((1, d), lambda i: (0, 0))
    return pl.pallas_call(
        body, out_shape=(jax.ShapeDtypeStruct((n, d), F32), jax.ShapeDtypeStruct((1, d), F32)), grid=(n // tm,),
        in_specs=[row, row, vec, row], out_specs=(row, vec),
        compiler_params=_params("arbitrary"), name=name)(dh, x, g, dres)


def gate_up(h2, wg_t, wu_t, name="gate_up"):
    n, d = h2.shape
    f = wg_t.shape[0]
    tm, tn = _row_tile(n), _col_tile(f)

    def body(h_ref, wg_ref, wu_ref, g_ref, u_ref, a_ref):
        hv = h_ref[...]
        gv = _dot(hv, wg_ref[...], 1, 1)
        uv = _dot(hv, wu_ref[...], 1, 1)
        g_ref[...] = gv.astype(BF16)
        u_ref[...] = uv.astype(BF16)
        a_ref[...] = (gv * _sigmoid(gv) * uv).astype(BF16)

    w_spec = pl.BlockSpec((tn, d), lambda i, j: (j, 0))
    o_spec = pl.BlockSpec((tm, tn), lambda i, j: (i, j))
    o_shape = jax.ShapeDtypeStruct((n, f), BF16)
    return pl.pallas_call(
        body, out_shape=(o_shape, o_shape, o_shape), grid=(n // tm, f // tn),
        in_specs=[pl.BlockSpec((tm, d), lambda i, j: (i, 0)), w_spec, w_spec], out_specs=(o_spec, o_spec, o_spec),
        compiler_params=_params("parallel", "arbitrary"), name=name)(h2, wg_t, wu_t)


def ffn_bwd_act(dx, wd, gate, up, name="ffn_bwd_act"):
    n, d = dx.shape
    f = wd.shape[0]
    tm, tn = _row_tile(n), _col_tile(f)

    def body(dx_ref, wd_ref, g_ref, u_ref, dg_ref, du_ref):
        dact = _dot(dx_ref[...].astype(BF16), wd_ref[...], 1, 1)
        gv = g_ref[...].astype(F32)
        uv = u_ref[...].astype(F32)
        sg = _sigmoid(gv)
        dg_ref[...] = (dact * uv * sg * (1.0 + gv * (1.0 - sg))).astype(BF16)
        du_ref[...] = (dact * gv * sg).astype(BF16)

    t_spec = pl.BlockSpec((tm, tn), lambda i, j: (i, j))
    o_shape = jax.ShapeDtypeStruct((n, f), BF16)
    return pl.pallas_call(
        body, out_shape=(o_shape, o_shape), grid=(n // tm, f // tn),
        in_specs=[pl.BlockSpec((tm, d), lambda i, j: (i, 0)), pl.BlockSpec((tn, d), lambda i, j: (j, 0)), t_spec, t_spec],
        out_specs=(t_spec, t_spec),
        compiler_params=_params("parallel", "arbitrary"), name=name)(dx, wd, gate, up)


def _group_masks(width):
    lane = lax.broadcasted_iota(jnp.int32, (1, width), 1)
    return [(lane >= HEAD_DIM * g) & (lane < HEAD_DIM * (g + 1)) for g in range(width // HEAD_DIM)]


def _group_sum(x, masks):
    out = jnp.zeros_like(x)
    for msk in masks:
        s = jnp.sum(jnp.where(msk, x, 0.0), axis=-1, keepdims=True)
        out = jnp.where(msk, s, out)
    return out


def _head_norm(x, gain, masks):
    r = lax.rsqrt(_group_sum(x * x, masks) * (1.0 / HEAD_DIM) + NORM_EPS)
    xhat = x * r
    return xhat * gain, xhat, r


def _head_norm_bwd(dxn, xhat, r, gain, masks):
    dgain = jnp.sum(dxn * xhat, axis=0, keepdims=True)
    dxhat = dxn * gain
    mean_t = _group_sum(dxhat * xhat, masks) * (1.0 / HEAD_DIM)
    return r * (dxhat - xhat * mean_t), dgain


def _softmax_rows(s):
    e = jnp.exp(s - jnp.max(s, axis=-1, keepdims=True))
    return e * (1.0 / jnp.sum(e, axis=-1, keepdims=True))


def _rel_onehot():
    col = lax.broadcasted_iota(jnp.int32, (1, KEY_WIN), 1)
    off = jnp.where(col < KEY_WIN - LANES, col, col - KEY_WIN)
    idx = jnp.clip(8 * CHUNK - off, -(CHUNK - 1), LANES) + (CHUNK - 1)
    return (lax.broadcasted_iota(jnp.int32, (N_REL, KEY_WIN), 0) == idx).astype(F32)


def bias_blocks(rel16):
    heads = TOK_WIDTH // HEAD_DIM

    def body(rel_ref, o_ref, u_ref):
        u_ref[...] = jnp.dot(rel_ref[...], _rel_onehot(), precision=HIGHEST, preferred_element_type=F32)
        row = lax.broadcasted_iota(jnp.int32, (CHUNK, KEY_WIN), 0)
        col = lax.broadcasted_iota(jnp.int32, (CHUNK, KEY_WIN), 1)
        for h in range(heads):
            xv = jnp.broadcast_to(u_ref[h:h + 1, :], (CHUNK, KEY_WIN))
            for b in range(6):
                xv = jnp.where(((row >> b) & 1) == 1, pltpu.roll(xv, 1 << b, axis=1), xv)
            xv = jnp.where(col < BAND, xv, NEG_INF)
            for i in range(Q_BLOCK // CHUNK):
                o_ref[h, CHUNK * i:CHUNK * (i + 1), :] = pltpu.roll(xv, CHUNK * i, axis=1) if i else xv

    return pl.pallas_call(
        body, out_shape=jax.ShapeDtypeStruct((heads, Q_BLOCK, KEY_WIN), F32),
        scratch_shapes=[pltpu.VMEM((16, KEY_WIN), F32)], name="bias_blocks")(rel16)


def bias_grad(dbias):
    heads = dbias.shape[0]

    def body(db_ref, o_ref, y_ref):
        y_ref[...] = jnp.zeros_like(y_ref)
        row = lax.broadcasted_iota(jnp.int32, (CHUNK, KEY_WIN), 0)
        for h in range(heads):
            fv = db_ref[h, 0:CHUNK, :]
            for i in range(1, Q_BLOCK // CHUNK):
                fv = fv + pltpu.roll(db_ref[h, CHUNK * i:CHUNK * (i + 1), :], KEY_WIN - CHUNK * i, axis=1)
            for b in range(6):
                fv = jnp.where(((row >> b) & 1) == 1, pltpu.roll(fv, KEY_WIN - (1 << b), axis=1), fv)
            y_ref[h:h + 1, :] = jnp.sum(fv, axis=0, keepdims=True)
        o_ref[...] = lax.dot_general(y_ref[...], _rel_onehot(), (((1,), (1,)), ((), ())),
                                     precision=HIGHEST, preferred_element_type=F32)

    return pl.pallas_call(
        body, out_shape=jax.ShapeDtypeStruct((16, N_REL), F32),
        scratch_shapes=[pltpu.VMEM((16, KEY_WIN), F32)], name="bias_grad")(dbias)


def _attn_windows(seq):
    out = []
    for j in range(seq // Q_BLOCK):
        r0 = j * Q_BLOCK
        k0 = max(0, r0 - 8 * CHUNK)
        width = r0 + Q_BLOCK - k0
        out.append((r0, k0, width, KEY_WIN - width))
    return out


def attn_fwd(z, gq2, gk2, bias, batch, seq):
    n = z.shape[0]
    pairs = TOK_WIDTH // LANES

    def body(q_ref, k_ref, v_ref, gq_ref, gk_ref, b_ref, o_ref, qn_s, kn_s):
        masks = _group_masks(LANES)
        qn_s[...] = _head_norm(q_ref[...].astype(F32), gq_ref[...], masks)[0].astype(BF16)
        kn_s[...] = _head_norm(k_ref[...].astype(F32), gk_ref[...], masks)[0].astype(BF16)
        for r0, k0, width, c0 in _attn_windows(seq):
            qb = qn_s[r0:r0 + Q_BLOCK, :]
            kw = kn_s[k0:k0 + width, :]
            vw = v_ref[k0:k0 + width, :]
            out = jnp.zeros((Q_BLOCK, LANES), F32)
            for h, msk in enumerate(masks):
                qh = jnp.where(msk, qb, jnp.zeros_like(qb))
                s = _dot(qh, kw, 1, 1) * ATTN_SCALE + b_ref[h, :, c0:KEY_WIN]
                p = _softmax_rows(s).astype(BF16)
                out = jnp.where(msk, _dot(p, vw, 1, 0), out)
            o_ref[r0:r0 + Q_BLOCK, :] = out.astype(o_ref.dtype)

    def col(off):
        return pl.BlockSpec((seq, LANES), lambda b, p: (b, off + p))

    vec = pl.BlockSpec((1, LANES), lambda b, p: (0, 0))
    return pl.pallas_call(
        body, out_shape=jax.ShapeDtypeStruct((n, TOK_WIDTH), BF16), grid=(batch, pairs),
        in_specs=[col(0), col(pairs), col(2 * pairs), vec, vec,
                  pl.BlockSpec((2, Q_BLOCK, KEY_WIN), lambda b, p: (p, 0, 0))],
        out_specs=pl.BlockSpec((seq, LANES), lambda b, p: (b, p)),
        scratch_shapes=[pltpu.VMEM((seq, LANES), BF16), pltpu.VMEM((seq, LANES), BF16)],
        compiler_params=_params("parallel", "arbitrary"), name="attn_fwd")(z, z, z, gq2, gk2, bias)


def attn_bwd(z, dcat, gq2, gk2, bias, batch, seq):
    n = z.shape[0]
    pairs = TOK_WIDTH // LANES

    def body(q_ref, k_ref, v_ref, do_ref, gq_ref, gk_ref, b_ref,
             dq_ref, dk_ref, dv_ref, db_ref, dgq_ref, dgk_ref, qn_s, kn_s, dqn_s, dkn_s, dv_s):
        bi, pi = pl.program_id(1), pl.program_id(0)
        masks = _group_masks(LANES)

        @pl.when(bi == 0)
        def _():
            db_ref[...] = jnp.zeros_like(db_ref)

        @pl.when((bi == 0) & (pi == 0))
        def _():
            dgq_ref[...] = jnp.zeros_like(dgq_ref)
            dgk_ref[...] = jnp.zeros_like(dgk_ref)

        qn, qhat, rq = _head_norm(q_ref[...].astype(F32), gq_ref[...], masks)
        kn, khat, rk = _head_norm(k_ref[...].astype(F32), gk_ref[...], masks)
        qn_s[...] = qn.astype(BF16)
        kn_s[...] = kn.astype(BF16)
        dkn_s[...] = jnp.zeros_like(dkn_s)
        dv_s[...] = jnp.zeros_like(dv_s)
        for r0, k0, width, c0 in _attn_windows(seq):
            qb = qn_s[r0:r0 + Q_BLOCK, :]
            dob = do_ref[r0:r0 + Q_BLOCK, :]
            kw = kn_s[k0:k0 + width, :]
            vw = v_ref[k0:k0 + width, :]
            dq_acc = jnp.zeros((Q_BLOCK, LANES), F32)
            dk_acc = jnp.zeros((width, LANES), F32)
            dv_acc = jnp.zeros((width, LANES), F32)
            for h, msk in enumerate(masks):
                qh = jnp.where(msk, qb, jnp.zeros_like(qb))
                doh = jnp.where(msk, dob, jnp.zeros_like(dob))
                s = _dot(qh, kw, 1, 1) * ATTN_SCALE + b_ref[h, :, c0:KEY_WIN]
                p = _softmax_rows(s)
                dp = _dot(doh, vw, 1, 1)
                ds = p * (dp - jnp.sum(p * dp, axis=-1, keepdims=True))
                db_ref[h, :, c0:KEY_WIN] += ds
                dsb = (ds * ATTN_SCALE).astype(BF16)
                dq_acc = jnp.where(msk, _dot(dsb, kw, 1, 0), dq_acc)
                dk_acc = jnp.where(msk, _dot(dsb, qb, 0, 0), dk_acc)
                dv_acc = jnp.where(msk, _dot(p.astype(BF16), dob, 0, 0), dv_acc)
            dqn_s[r0:r0 + Q_BLOCK, :] = dq_acc
            dkn_s[k0:k0 + width, :] += dk_acc
            dv_s[k0:k0 + width, :] += dv_acc
        dq, dgq = _head_norm_bwd(dqn_s[...], qhat, rq, gq_ref[...], masks)
        dk, dgk = _head_norm_bwd(dkn_s[...], khat, rk, gk_ref[...], masks)
        dq_ref[...] = dq.astype(dq_ref.dtype)
        dk_ref[...] = dk.astype(dk_ref.dtype)
        dv_ref[...] = dv_s[...].astype(dv_ref.dtype)
        dgq_ref[...] += dgq
        dgk_ref[...] += dgk

    def col(off):
        return pl.BlockSpec((seq, LANES), lambda p, b: (b, off + p))

    vec = pl.BlockSpec((1, LANES), lambda p, b: (0, 0))
    blk = pl.BlockSpec((2, Q_BLOCK, KEY_WIN), lambda p, b: (p, 0, 0))
    o_shape = jax.ShapeDtypeStruct((n, TOK_WIDTH), BF16)
    v_shape = jax.ShapeDtypeStruct((1, LANES), F32)
    return pl.pallas_call(
        body,
        out_shape=(o_shape, o_shape, o_shape, jax.ShapeDtypeStruct(bias.shape, F32), v_shape, v_shape),
        grid=(pairs, batch),
        in_specs=[col(0), col(pairs), col(2 * pairs), col(0), vec, vec, blk],
        out_specs=(col(0), col(0), col(0), blk, vec, vec),
        scratch_shapes=[pltpu.VMEM((seq, LANES), BF16), pltpu.VMEM((seq, LANES), BF16),
                        pltpu.VMEM((seq, LANES), F32), pltpu.VMEM((seq, LANES), F32), pltpu.VMEM((seq, LANES), F32)],
        compiler_params=_params("arbitrary", "arbitrary"), name="attn_bwd")(z, z, z, dcat, gq2, gk2, bias)


MEM_ROWS = 512


def memattn_fwd(z, kv, gq4, gk4, batch, seq, qcol, name):
    n = z.shape[0]
    mtok = kv.shape[0] // batch
    rows = min(MEM_ROWS, seq)

    def body(q_ref, kv_ref, gq_ref, gk_ref, o_ref):
        masks = _group_masks(MEM_WIDTH)
        kn = _head_norm(kv_ref[:, 0:MEM_WIDTH], gk_ref[...], masks)[0].astype(BF16)
        vm = kv_ref[:, MEM_WIDTH:2 * MEM_WIDTH].astype(BF16)
        for t in range(seq // rows):
            sl = slice(t * rows, (t + 1) * rows)
            qn = _head_norm(q_ref[sl, :].astype(F32), gq_ref[...], masks)[0].astype(BF16)
            out = jnp.zeros((rows, MEM_WIDTH), F32)
            for msk in masks:
                qh = jnp.where(msk, qn, jnp.zeros_like(qn))
                p = _softmax_rows(_dot(qh, kn, 1, 1) * ATTN_SCALE).astype(BF16)
                out = jnp.where(msk, _dot(p, vm, 1, 0), out)
            o_ref[sl, :] = out.astype(o_ref.dtype)

    vec = pl.BlockSpec((1, MEM_WIDTH), lambda b: (0, 0))
    return pl.pallas_call(
        body, out_shape=jax.ShapeDtypeStruct((n, MEM_WIDTH), BF16), grid=(batch,),
        in_specs=[pl.BlockSpec((seq, MEM_WIDTH), lambda b: (b, qcol)),
                  pl.BlockSpec((mtok, 2 * MEM_WIDTH), lambda b: (b, 0)), vec, vec],
        out_specs=pl.BlockSpec((seq, MEM_WIDTH), lambda b: (b, 0)),
        compiler_params=_params("parallel"), name=name)(z, kv, gq4, gk4)


def memattn_bwd(z, kv, dcat, gq4, gk4, batch, seq, qcol, name):
    n = z.shape[0]
    mtok = kv.shape[0] // batch
    rows = min(MEM_ROWS, seq)

    def body(q_ref, kv_ref, do_ref, gq_ref, gk_ref, dq_ref, dkv_ref, dgq_ref, dgk_ref):
        @pl.when(pl.program_id(0) == 0)
        def _():
            dgq_ref[...] = jnp.zeros_like(dgq_ref)
            dgk_ref[...] = jnp.zeros_like(dgk_ref)

        masks = _group_masks(MEM_WIDTH)
        kn_f, khat, rk = _head_norm(kv_ref[:, 0:MEM_WIDTH], gk_ref[...], masks)
        kn = kn_f.astype(BF16)
        vm = kv_ref[:, MEM_WIDTH:2 * MEM_WIDTH].astype(BF16)
        dkn = jnp.zeros((mtok, MEM_WIDTH), F32)
        dvm = jnp.zeros((mtok, MEM_WIDTH), F32)
        dgq = jnp.zeros((1, MEM_WIDTH), F32)
        for t in range(seq // rows):
            sl = slice(t * rows, (t + 1) * rows)
            qn_f, qhat, rq = _head_norm(q_ref[sl, :].astype(F32), gq_ref[...], masks)
            qn = qn_f.astype(BF16)
            dob = do_ref[sl, :]
            dqn = jnp.zeros((rows, MEM_WIDTH), F32)
            for msk in masks:
                qh = jnp.where(msk, qn, jnp.zeros_like(qn))
                doh = jnp.where(msk, dob, jnp.zeros_like(dob))
                p = _softmax_rows(_dot(qh, kn, 1, 1) * ATTN_SCALE)
                dp = _dot(doh, vm, 1, 1)
                ds = p * (dp - jnp.sum(p * dp, axis=-1, keepdims=True))
                dsb = (ds * ATTN_SCALE).astype(BF16)
                dqn = jnp.where(msk, _dot(dsb, kn, 1, 0), dqn)
                dkn = dkn + jnp.where(msk, _dot(dsb, qn, 0, 0), 0.0)
                dvm = dvm + jnp.where(msk, _dot(p.astype(BF16), dob, 0, 0), 0.0)
            dq, dg = _head_norm_bwd(dqn, qhat, rq, gq_ref[...], masks)
            dq_ref[sl, :] = dq.astype(dq_ref.dtype)
            dgq = dgq + dg
        dk, dgk = _head_norm_bwd(dkn, khat, rk, gk_ref[...], masks)
        dkv_ref[:, 0:MEM_WIDTH] = dk
        dkv_ref[:, MEM_WIDTH:2 * MEM_WIDTH] = dvm
        dgq_ref[...] += dgq
        dgk_ref[...] += dgk

    vec = pl.BlockSpec((1, MEM_WIDTH), lambda b: (0, 0))
    kv_spec = pl.BlockSpec((mtok, 2 * MEM_WIDTH), lambda b: (b, 0))
    v_shape = jax.ShapeDtypeStruct((1, MEM_WIDTH), F32)
    return pl.pallas_call(
        body,
        out_shape=(jax.ShapeDtypeStruct((n, MEM_WIDTH), BF16), jax.ShapeDtypeStruct(kv.shape, F32), v_shape, v_shape),
        grid=(batch,),
        in_specs=[pl.BlockSpec((seq, MEM_WIDTH), lambda b: (b, qcol)), kv_spec,
                  pl.BlockSpec((seq, MEM_WIDTH), lambda b: (b, TOK_WIDTH // MEM_WIDTH)), vec, vec],
        out_specs=(pl.BlockSpec((seq, MEM_WIDTH), lambda b: (b, 0)), kv_spec, vec, vec),
        compiler_params=_params("arbitrary"), name=name)(z, kv, dcat, gq4, gk4)


CONV_ROWS = 256


def _glu(a_ref, g_ref):
    return a_ref[...].astype(F32) * _sigmoid(g_ref[...].astype(F32))


def _layer_norm_stats(y):
    mu = jnp.mean(y, axis=-1, keepdims=True)
    yc = y - mu
    rstd = lax.rsqrt(jnp.mean(yc * yc, axis=-1, keepdims=True) + NORM_EPS)
    return yc * rstd, rstd


def conv_fwd(z, cw, cb, lg, lb, batch, seq):
    n = z.shape[0]
    nt = seq // CONV_ROWS
    sub = CONV_ROWS // CONV_HALO
    lead = CONV_HALO - (CONV_W - 1)

    def body(a_ref, g_ref, ap_ref, gp_ref, cw_ref, cb_ref, lg_ref, lb_ref, o_ref, win):
        first = pl.program_id(1) == 0
        win[0:CONV_HALO, :] = jnp.where(first, 0.0, _glu(ap_ref, gp_ref))
        win[CONV_HALO:CONV_HALO + CONV_ROWS, :] = _glu(a_ref, g_ref)
        y = jnp.zeros((CONV_ROWS, TOK_WIDTH), F32) + cb_ref[...]
        for w in range(CONV_W):
            y = y + win[lead + w:lead + w + CONV_ROWS, :] * cw_ref[w:w + 1, :]
        yh, _ = _layer_norm_stats(y)
        t = yh * lg_ref[...] + lb_ref[...]
        o_ref[...] = (t * _sigmoid(t)).astype(o_ref.dtype)

    def cur(c):
        return pl.BlockSpec((CONV_ROWS, TOK_WIDTH), lambda b, i: (b * nt + i, c))

    def prev(c):
        return pl.BlockSpec((CONV_HALO, TOK_WIDTH), lambda b, i: (jnp.maximum((b * nt + i) * sub - 1, 0), c))

    vec = pl.BlockSpec((1, TOK_WIDTH), lambda b, i: (0, 0))
    return pl.pallas_call(
        body, out_shape=jax.ShapeDtypeStruct((n, TOK_WIDTH), BF16), grid=(batch, nt),
        in_specs=[cur(0), cur(1), prev(0), prev(1), pl.BlockSpec((32, TOK_WIDTH), lambda b, i: (0, 0)), vec, vec, vec],
        out_specs=pl.BlockSpec((CONV_ROWS, TOK_WIDTH), lambda b, i: (b * nt + i, 0)),
        scratch_shapes=[pltpu.VMEM((CONV_HALO + CONV_ROWS, TOK_WIDTH), F32)],
        compiler_params=_params("parallel", "arbitrary"), name="conv_fwd")(z, z, z, z, cw, cb, lg, lb)


def conv_bwd(z, dcat, cw, cb, lg, lb, batch, seq):
    n = z.shape[0]
    nt = seq // CONV_ROWS
    sub = CONV_ROWS // CONV_HALO
    lead = CONV_HALO - (CONV_W - 1)
    ext = CONV_ROWS + CONV_HALO
    last_blk = n // CONV_HALO - 1

    def body(a_ref, g_ref, ap_ref, gp_ref, an_ref, gn_ref, do_ref, don_ref, cw_ref, cb_ref, lg_ref, lb_ref,
             da_ref, dg_ref, dcw_ref, dsm_ref, win, dyw):
        b, i = pl.program_id(0), pl.program_id(1)
        first, last = i == 0, i == nt - 1

        @pl.when((b == 0) & (i == 0))
        def _():
            dcw_ref[...] = jnp.zeros_like(dcw_ref)
            dsm_ref[...] = jnp.zeros_like(dsm_ref)

        av = a_ref[...].astype(F32)
        sg = _sigmoid(g_ref[...].astype(F32))
        win[0:CONV_HALO, :] = jnp.where(first, 0.0, _glu(ap_ref, gp_ref))
        win[CONV_HALO:CONV_HALO + CONV_ROWS, :] = av * sg
        win[CONV_HALO + CONV_ROWS:, :] = jnp.where(last, 0.0, _glu(an_ref, gn_ref))
        y = jnp.zeros((ext, TOK_WIDTH), F32) + cb_ref[...]
        for w in range(CONV_W):
            y = y + win[lead + w:lead + w + ext, :] * cw_ref[w:w + 1, :]
        yh, rstd = _layer_norm_stats(y)
        t = yh * lg_ref[...] + lb_ref[...]
        st = _sigmoid(t)
        dout = jnp.concatenate(
            [do_ref[...].astype(F32), jnp.where(last, 0.0, don_ref[...].astype(F32))], axis=0)
        dt = dout * st * (1.0 + t * (1.0 - st))
        dyh = dt * lg_ref[...]
        dy = rstd * (dyh - jnp.mean(dyh, axis=-1, keepdims=True) - yh * jnp.mean(dyh * yh, axis=-1, keepdims=True))
        dyw[...] = dy
        dsm_ref[0:1, :] += jnp.sum(dy[0:CONV_ROWS], axis=0, keepdims=True)
        dsm_ref[1:2, :] += jnp.sum((dt * yh)[0:CONV_ROWS], axis=0, keepdims=True)
        dsm_ref[2:3, :] += jnp.sum(dt[0:CONV_ROWS], axis=0, keepdims=True)
        dyc = dyw[0:CONV_ROWS, :]
        dglu = jnp.zeros((CONV_ROWS, TOK_WIDTH), F32)
        for w in range(CONV_W):
            dcw_ref[w:w + 1, :] += jnp.sum(dyc * win[lead + w:lead + w + CONV_ROWS, :], axis=0, keepdims=True)
            back = CONV_W - 1 - w
            dglu = dglu + dyw[back:back + CONV_ROWS, :] * cw_ref[w:w + 1, :]
        da_ref[...] = (dglu * sg).astype(da_ref.dtype)
        dg_ref[...] = (dglu * av * sg * (1.0 - sg)).astype(dg_ref.dtype)

    def cur(c):
        return pl.BlockSpec((CONV_ROWS, TOK_WIDTH), lambda b, i: (b * nt + i, c))

    def prev(c):
        return pl.BlockSpec((CONV_HALO, TOK_WIDTH), lambda b, i: (jnp.maximum((b * nt + i) * sub - 1, 0), c))

    def nxt(c):
        return pl.BlockSpec((CONV_HALO, TOK_WIDTH), lambda b, i: (jnp.minimum((b * nt + i + 1) * sub, last_blk), c))

    vec = pl.BlockSpec((1, TOK_WIDTH), lambda b, i: (0, 0))
    full32 = pl.BlockSpec((32, TOK_WIDTH), lambda b, i: (0, 0))
    o_shape = jax.ShapeDtypeStruct((n, TOK_WIDTH), BF16)
    return pl.pallas_call(
        body,
        out_shape=(o_shape, o_shape, jax.ShapeDtypeStruct((32, TOK_WIDTH), F32), jax.ShapeDtypeStruct((8, TOK_WIDTH), F32)),
        grid=(batch, nt),
        in_specs=[cur(0), cur(1), prev(0), prev(1), nxt(0), nxt(1), cur(0), nxt(0), full32, vec, vec, vec],
        out_specs=(cur(0), cur(0), full32, pl.BlockSpec((8, TOK_WIDTH), lambda b, i: (0, 0))),
        scratch_shapes=[pltpu.VMEM((CONV_HALO + ext, TOK_WIDTH), F32), pltpu.VMEM((ext, TOK_WIDTH), F32)],
        compiler_params=_params("arbitrary", "arbitrary"), name="conv_bwd")(
            z, z, z, z, z, z, dcat, dcat, cw, cb, lg, lb)


def loss_head(y, target):
    n, d = y.shape
    tm = _row_tile(n)
    nt = n // tm

    def body(y_ref, t_ref, dy_ref, l_ref, acc_ref):
        i = pl.program_id(0)

        @pl.when(i == 0)
        def _():
            acc_ref[...] = jnp.zeros_like(acc_ref)

        err = y_ref[...] - t_ref[...]
        dy_ref[...] = err * (1.0 / d)
        acc_ref[...] += jnp.sum(err * err, axis=0, keepdims=True)

        @pl.when(i == nt - 1)
        def _():
            total = jnp.sum(acc_ref[...], axis=-1, keepdims=True) * (0.5 / d)
            l_ref[...] = jnp.broadcast_to(total, l_ref.shape)

    row = pl.BlockSpec((tm, d), lambda i: (i, 0))
    return pl.pallas_call(
        body, out_shape=(jax.ShapeDtypeStruct((n, d), F32), jax.ShapeDtypeStruct((8, LANES), F32)), grid=(nt,),
        in_specs=[row, row], out_specs=(row, pl.BlockSpec((8, LANES), lambda i: (0, 0))),
        scratch_shapes=[pltpu.VMEM((1, d), F32)],
        compiler_params=_params("arbitrary"), name="loss_head")(y, target)


def col_sum(x, name="col_sum"):
    n, c = x.shape
    tm = _row_tile(n)

    def body(x_ref, o_ref):
        @pl.when(pl.program_id(0) == 0)
        def _():
            o_ref[...] = jnp.zeros_like(o_ref)

        o_ref[...] += jnp.sum(x_ref[...].astype(F32), axis=0, keepdims=True)

    return pl.pallas_call(
        body, out_shape=jax.ShapeDtypeStruct((1, c), F32), grid=(n // tm,),
        in_specs=[pl.BlockSpec((tm, c), lambda i: (i, 0))], out_specs=pl.BlockSpec((1, c), lambda i: (0, 0)),
        compiler_params=_params("arbitrary"), name=name)(x)


def adamw(w, g, m, v, name="adamw"):
    rows, cols = w.shape
    tr = rows
    for cand in (512, 256, 128, 64, 32, 16, 8):
        if rows % cand == 0 and rows > cand:
            tr = cand
            break
    c1 = 1.0 / (1.0 - ADAM_B1 ** ADAM_STEP)
    c2 = 1.0 / (1.0 - ADAM_B2 ** ADAM_STEP)

    def body(w_ref, g_ref, m_ref, v_ref, d_ref, nm_ref, nv_ref):
        gv = g_ref[...]
        nm = ADAM_B1 * m_ref[...] + (1.0 - ADAM_B1) * gv
        nv = ADAM_B2 * v_ref[...] + (1.0 - ADAM_B2) * (gv * gv)
        nm_ref[...] = nm
        nv_ref[...] = nv
        d_ref[...] = -ADAM_LR * ((nm * c1) / (jnp.sqrt(nv * c2) + ADAM_EPS) + ADAM_WD * w_ref[...])

    spec = pl.BlockSpec((tr, cols), lambda i: (i, 0))
    shape = jax.ShapeDtypeStruct((rows, cols), F32)
    return pl.pallas_call(
        body, out_shape=(shape, shape, shape), grid=(rows // tr,),
        in_specs=[spec, spec, spec, spec], out_specs=(spec, spec, spec),
        compiler_params=_params("parallel"), name=name)(w, g, m, v)


def _place():
    return lax.axis_index("x"), lax.axis_index("y"), lax.axis_index("c")


def _other_chips(x, y):
    return [(1 - x, y), (x, 1 - y), (1 - x, 1 - y)]


def small_exchange(slab, reduce):
    r = slab.shape[0]

    def body(in_ref, o_ref, *scratch):
        if reduce:
            buf, send_sems, recv_sems = scratch
        else:
            buf = o_ref
            send_sems, recv_sems = scratch
        x, y, c = _place()
        me = 4 * x + 2 * y + c
        buf[me] = in_ref[...]
        copies = []
        for k in range(1, N_DEV):
            peer = (x ^ (k >> 2), y ^ ((k >> 1) & 1), c ^ (k & 1))
            cp = pltpu.make_async_remote_copy(
                src_ref=in_ref, dst_ref=buf.at[me], send_sem=send_sems.at[k - 1], recv_sem=recv_sems.at[k - 1],
                device_id=peer, device_id_type=MESH)
            cp.start()
            copies.append(cp)
        for k in range(1, N_DEV):
            src = 4 * (x ^ (k >> 2)) + 2 * (y ^ ((k >> 1) & 1)) + (c ^ (k & 1))
            pltpu.make_async_remote_copy(
                src_ref=in_ref, dst_ref=buf.at[src], send_sem=send_sems.at[k - 1], recv_sem=recv_sems.at[k - 1],
                device_id=(x, y, c), device_id_type=MESH).wait_recv()
        for cp in copies:
            cp.wait_send()
        if reduce:
            total = buf[0]
            for d in range(1, N_DEV):
                total = total + buf[d]
            o_ref[...] = total

    sems = [pltpu.SemaphoreType.DMA((N_DEV - 1,)), pltpu.SemaphoreType.DMA((N_DEV - 1,))]
    if reduce:
        out_shape = jax.ShapeDtypeStruct((r, LANES), F32)
        scratch = [pltpu.VMEM((N_DEV, r, LANES), F32)] + sems
    else:
        out_shape = jax.ShapeDtypeStruct((N_DEV, r, LANES), F32)
        scratch = sems
    vmem = pl.BlockSpec(memory_space=pltpu.VMEM)
    return pl.pallas_call(
        body, out_shape=out_shape, in_specs=[vmem], out_specs=vmem, scratch_shapes=scratch,
        compiler_params=pltpu.CompilerParams(vmem_limit_bytes=VMEM_LIMIT),
        name="small_reduce" if reduce else "small_gather")(slab)


def gather_weights(shards, name, collective_id):
    nw = len(shards)
    ns = [s.shape[0] for s in shards]
    in_refs = [jax.new_ref(s, memory_space=pltpu.MemorySpace.HBM) for s in shards]
    out_refs = [jax.empty_ref(jax.ShapeDtypeStruct((N_DEV * s.shape[0], s.shape[1]), s.dtype),
                              memory_space=pltpu.MemorySpace.HBM) for s in shards]

    @pl.kernel(mesh=plsc.ScalarSubcoreMesh(axis_name="seq", num_cores=1), name=name,
               scratch_types=(pltpu.SemaphoreType.DMA((nw, 7)), pltpu.SemaphoreType.DMA((nw, 7)),
                              pltpu.SemaphoreType.DMA((nw,))),
               compiler_params=pltpu.CompilerParams(collective_id=collective_id))
    def launch(send_sems, recv_sems, local_sems):
        x, y, c = _place()
        me, sib = (x, y, c), (x, y, 1 - c)
        chips = _other_chips(x, y)
        barrier = pltpu.get_barrier_semaphore()
        for peer in [sib] + [(*chip, c) for chip in chips]:
            pl.semaphore_signal(barrier, inc=1, device_id=peer, device_id_type=MESH)
        pl.semaphore_wait(barrier, 4)

        def rows(w, dev):
            return out_refs[w].at[pl.ds((4 * dev[0] + 2 * dev[1] + dev[2]) * ns[w], ns[w]), :]

        def copy(w, k, block, to, src=None):
            return pltpu.make_async_remote_copy(
                src_ref=rows(w, block) if src is None else src, dst_ref=rows(w, block),
                send_sem=send_sems.at[w, k], recv_sem=recv_sems.at[w, k], device_id=to, device_id_type=MESH)

        started, sends = [], []
        for w in range(nw):
            mine = pltpu.make_async_copy(in_refs[w], rows(w, me), local_sems.at[w])
            mine.start()
            started.append(mine)
            first = [copy(w, 0, me, sib, src=in_refs[w])]
            first += [copy(w, 1 + j, me, (*chip, c), src=in_refs[w]) for j, chip in enumerate(chips)]
            for cp in first:
                cp.start()
            sends += first
        for w in range(nw):
            for j, chip in enumerate(chips):
                copy(w, 1 + j, (*chip, c), me).wait_recv()
                fwd = copy(w, 4 + j, (*chip, c), sib)
                fwd.start()
                sends.append(fwd)
        for w in range(nw):
            copy(w, 0, sib, me).wait_recv()
            for j, chip in enumerate(chips):
                copy(w, 4 + j, (*chip, 1 - c), me).wait_recv()
        for cp in sends:
            cp.wait_send()
        for mine in started:
            mine.wait()

    launch()
    return [r[...] for r in out_refs]


def _sequencer_exchange(sources, out_rows, peers_of, copies_of, name, collective_id):
    nw = len(sources)
    in_refs = [jax.new_ref(s, memory_space=pltpu.MemorySpace.HBM) for s in sources]
    out_refs = [jax.empty_ref(jax.ShapeDtypeStruct((rows, s.shape[1]), s.dtype), memory_space=pltpu.MemorySpace.HBM)
                for rows, s in zip(out_rows, sources)]
    per = len(copies_of(0, 0, 0, 0))

    @pl.kernel(mesh=plsc.ScalarSubcoreMesh(axis_name="seq", num_cores=1), name=name,
               scratch_types=(pltpu.SemaphoreType.DMA((nw, per)), pltpu.SemaphoreType.DMA((nw, per))),
               compiler_params=pltpu.CompilerParams(collective_id=collective_id))
    def launch(send_sems, recv_sems):
        x, y, c = _place()
        peers = peers_of(x, y, c)
        barrier = pltpu.get_barrier_semaphore()
        for peer in peers:
            pl.semaphore_signal(barrier, inc=1, device_id=peer, device_id_type=MESH)
        pl.semaphore_wait(barrier, len(peers))
        copies = []
        for w in range(nw):
            for k, (src_blk, dst_blk, rows, peer) in enumerate(copies_of(x, y, c, w)):
                cp = pltpu.make_async_remote_copy(
                    src_ref=in_refs[w].at[pl.ds(src_blk * rows, rows), :],
                    dst_ref=out_refs[w].at[pl.ds(dst_blk * rows, rows), :],
                    send_sem=send_sems.at[w, k], recv_sem=recv_sems.at[w, k], device_id=peer, device_id_type=MESH)
                cp.start()
                copies.append(cp)
        for cp in copies:
            cp.wait_recv()
        for cp in copies:
            cp.wait_send()

    launch()
    return [r[...] for r in out_refs]


def scatter_to_sibling(grads, name, collective_id):
    ns = [g.shape[0] // N_DEV for g in grads]
    return _sequencer_exchange(
        grads, [4 * n for n in ns],
        lambda x, y, c: [(x, y, 1 - c)],
        lambda x, y, c, w: [(2 * q + 1 - c, q, ns[w], (x, y, 1 - c)) for q in range(4)],
        name, collective_id)


def scatter_to_chips(parts, name, collective_id):
    ns = [p.shape[0] // 4 for p in parts]
    return _sequencer_exchange(
        parts, [3 * n for n in ns],
        lambda x, y, c: [(*chip, c) for chip in _other_chips(x, y)],
        lambda x, y, c, w: [(2 * chip[0] + chip[1], j, ns[w], (*chip, c)) for j, chip in enumerate(_other_chips(x, y))],
        name, collective_id)


def add_sibling(grad, landed, core, name):
    n = landed.shape[0] // 4
    cols = grad.shape[1]

    def body(c_ref, g_ref, l_ref, o_ref):
        o_ref[...] = (g_ref[...].astype(F32) + l_ref[...].astype(F32)).astype(o_ref.dtype)

    grid_spec = pltpu.PrefetchScalarGridSpec(
        num_scalar_prefetch=1, grid=(4,),
        in_specs=[pl.BlockSpec((n, cols), lambda q, c_ref: (2 * q + c_ref[0], 0)),
                  pl.BlockSpec((n, cols), lambda q, c_ref: (q, 0))],
        out_specs=pl.BlockSpec((n, cols), lambda q, c_ref: (q, 0)))
    return pl.pallas_call(
        body, out_shape=jax.ShapeDtypeStruct(landed.shape, landed.dtype), grid_spec=grid_spec,
        compiler_params=_params("arbitrary"), name=name)(core, grad, landed)


def add_chips(part, landed, chip, name):
    n = landed.shape[0] // 3
    cols = part.shape[1]

    def body(q_ref, p_ref, l0_ref, l1_ref, l2_ref, o_ref):
        o_ref[...] = ((p_ref[...].astype(F32) + l0_ref[...].astype(F32)) + l1_ref[...].astype(F32)) \
            + l2_ref[...].astype(F32)

    def landed_spec(j):
        return pl.BlockSpec((n, cols), lambda i, q_ref: (j, 0))

    grid_spec = pltpu.PrefetchScalarGridSpec(
        num_scalar_prefetch=1, grid=(1,),
        in_specs=[pl.BlockSpec((n, cols), lambda i, q_ref: (q_ref[0], 0)), landed_spec(0), landed_spec(1), landed_spec(2)],
        out_specs=pl.BlockSpec((n, cols), lambda i, q_ref: (0, 0)))
    return pl.pallas_call(
        body, out_shape=jax.ShapeDtypeStruct((n, cols), F32), grid_spec=grid_spec,
        compiler_params=_params("arbitrary"), name=name)(chip, part, landed, landed, landed)


def _pack(arrays):
    flat = jnp.concatenate([a.reshape(-1).astype(F32) for a in arrays])
    pad = (-flat.shape[0]) % (8 * LANES)
    return jnp.pad(flat, (0, pad)).reshape(-1, LANES)


def _unpack(slab, shapes):
    flat = slab.reshape(slab.shape[:-2] + (-1,))
    out, off = [], 0
    for shp in shapes:
        size = 1
        for s in shp:
            size *= s
        out.append(flat[..., off:off + size].reshape(flat.shape[:-1] + tuple(shp)))
        off += size
    return out


def kernel(x, mem, norm1_g, mem_norm_g, a_w_in, a_q_g, a_k_g, a_rel_bias, b_w_in, b_b_in, b_conv_w, b_conv_b, b_ln_g, b_ln_b, mq_g, mk_g, w_mem_kv, w_out, norm2_g, w_gate, w_up, w_down, loss_target, m_norm1_g, m_mem_norm_g, m_a_w_in, m_a_q_g, m_a_k_g, m_a_rel_bias, m_b_w_in, m_b_b_in, m_b_conv_w, m_b_conv_b, m_b_ln_g, m_b_ln_b, m_mq_g, m_mk_g, m_w_mem_kv, m_w_out, m_norm2_g, m_w_gate, m_w_up, m_w_down, v_norm1_g, v_mem_norm_g, v_a_w_in, v_a_q_g, v_a_k_g, v_a_rel_bias, v_b_w_in, v_b_b_in, v_b_conv_w, v_b_conv_b, v_b_ln_g, v_b_ln_b, v_mq_g, v_mk_g, v_w_mem_kv, v_w_out, v_norm2_g, v_w_gate, v_w_up, v_w_down):
    batch, seq, d = x.shape
    mtok = mem.shape[1]
    n = batch * seq
    ax, ay, ac = _place()
    me = 4 * ax + 2 * ay + ac
    core_arr = jnp.reshape(ac, (1,)).astype(jnp.int32)
    chip_arr = jnp.reshape(2 * ax + ay, (1,)).astype(jnp.int32)

    def t_bf16(w):
        return jnp.transpose(w).astype(BF16)

    a_win_t, = gather_weights([t_bf16(a_w_in[0])], "gather_in_a", 1)
    wkv0, wo0, wg0, wu0, wd0 = gather_weights(
        [w_mem_kv[0].astype(BF16), w_out[0].astype(BF16), t_bf16(w_gate[0]), t_bf16(w_up[0]), w_down[0].astype(BF16)],
        "gather_layer_a", 2)
    b_win_t, wkv1, wo1 = gather_weights(
        [t_bf16(b_w_in[0]), w_mem_kv[1].astype(BF16), w_out[1].astype(BF16)], "gather_in_b", 3)
    wg1, wu1, wd1 = gather_weights([t_bf16(w_gate[1]), t_bf16(w_up[1]), w_down[1].astype(BF16)], "gather_ffn_b", 4)
    wg_t, wu_t, wd, wo, wkv = [wg0, wg1], [wu0, wu1], [wd0, wd1], [wo0, wo1], [wkv0, wkv1]

    f_loc = b_b_in.shape[1]
    c_loc = b_conv_b.shape[1]
    small_shapes = [(f_loc,), (CONV_W, c_loc), (c_loc,), (c_loc,), (c_loc,)]
    gathered = small_exchange(_pack([b_b_in, b_conv_w, b_conv_b, b_ln_g, b_ln_b]), reduce=False)
    bb_g, cw_g, cb_g, lg_g, lb_g = _unpack(gathered, small_shapes)
    bb_full = bb_g.reshape(1, -1)
    cw_full = jnp.pad(jnp.transpose(cw_g, (1, 0, 2)).reshape(CONV_W, -1), ((0, 32 - CONV_W), (0, 0)))
    cb_full, lg_full, lb_full = cb_g.reshape(1, -1), lg_g.reshape(1, -1), lb_g.reshape(1, -1)

    def two(g):
        return jnp.concatenate([g, g], axis=-1)

    gq2, gk2 = two(a_q_g), two(a_k_g)
    rel16 = jnp.pad(a_rel_bias[0], ((0, 16 - a_rel_bias.shape[1]), (0, 0)))
    bias = bias_blocks(rel16)

    x0 = x.reshape(n, d)
    mem2 = mem.reshape(batch * mtok, d)
    zero_mem = jnp.zeros_like(mem2)

    saved = []
    xin = x0
    for l in range(2):
        h = rms_fwd(xin, norm1_g[l:l + 1], name=f"rms1_fwd_{l}")
        mem_n = rms_fwd(mem2, mem_norm_g[l:l + 1], name=f"rms_mem_fwd_{l}")
        kv = mm_nn(mem_n, wkv[l], name=f"mem_kv_{l}")
        gq4 = jnp.tile(mq_g[l:l + 1], (1, 4))
        gk4 = jnp.tile(mk_g[l:l + 1], (1, 4))
        if l == 0:
            z = mm_nt(h, a_win_t, name="in_proj_a")
            tok = attn_fwd(z, gq2, gk2, bias, batch, seq)
            qcol = 3 * TOK_WIDTH // MEM_WIDTH
        else:
            z = mm_nt(h, b_win_t, bias=bb_full, name="in_proj_b")
            tok = conv_fwd(z, cw_full, cb_full, lg_full, lb_full, batch, seq)
            qcol = 2 * TOK_WIDTH // MEM_WIDTH
        memo = memattn_fwd(z, kv, gq4, gk4, batch, seq, qcol, name=f"memattn_fwd_{l}")
        cat = jnp.concatenate([tok, memo], axis=-1)
        x1 = mm_nn(cat, wo[l], res=xin, name=f"out_proj_{l}")
        h2 = rms_fwd(x1, norm2_g[l:l + 1], name=f"rms2_fwd_{l}")
        gate, up, act = gate_up(h2, wg_t[l], wu_t[l], name=f"gate_up_{l}")
        x2 = mm_nn(act, wd[l], res=x1, name=f"down_proj_{l}")
        saved.append(dict(xin=xin, h=h, mem_n=mem_n, kv=kv, gq4=gq4, gk4=gk4, z=z, qcol=qcol, cat=cat, x1=x1, h2=h2,
                          gate=gate, up=up, act=act))
        xin = x2

    dx, loss_blk = loss_head(xin, loss_target.reshape(n, d))
    loss = lax.psum(loss_blk[0, 0], ("x", "y", "c"))

    big = {}
    small = {}
    red = {}
    groups = 0

    def reduce_group(keys):
        nonlocal groups
        gid = groups
        groups += 1
        glist = [big[k] for k in keys]
        landed1 = scatter_to_sibling(glist, f"scatter_sibling_{gid}", 8 + 2 * gid)
        parts = [add_sibling(g, ld, core_arr, name=f"add_sibling_{k}") for k, g, ld in zip(keys, glist, landed1)]
        landed2 = scatter_to_chips(parts, f"scatter_chips_{gid}", 9 + 2 * gid)
        for k, p, ld in zip(keys, parts, landed2):
            red[k] = add_chips(p, ld, chip_arr, name=f"add_chips_{k}")

    for l in (1, 0):
        sv = saved[l]
        dgate, dup = ffn_bwd_act(dx, wd[l], sv["gate"], sv["up"], name=f"ffn_bwd_act_{l}")
        big[f"wd{l}"] = mm_tn(sv["act"], dx, name=f"grad_wd_{l}")
        dh2 = mm2_nn(dgate, wg_t[l], dup, wu_t[l], name=f"ffn_bwd_h_{l}")
        big[f"wg{l}"] = mm_tn(dgate, sv["h2"], name=f"grad_wg_{l}")
        big[f"wu{l}"] = mm_tn(dup, sv["h2"], name=f"grad_wu_{l}")
        dx1, small[f"norm2_{l}"] = rms_bwd(dh2, sv["x1"], norm2_g[l:l + 1], dx, name=f"rms2_bwd_{l}")
        dcat = mm_nt(dx1, wo[l], name=f"out_proj_bwd_{l}")
        big[f"wo{l}"] = mm_tn(sv["cat"], dx1, name=f"grad_wo_{l}")
        reduce_group([f"wd{l}", f"wg{l}", f"wu{l}", f"wo{l}"])
        dqm, dkv, small[f"mq_{l}"], small[f"mk_{l}"] = memattn_bwd(
            sv["z"], sv["kv"], dcat, sv["gq4"], sv["gk4"], batch, seq, sv["qcol"], name=f"memattn_bwd_{l}")
        if l == 0:
            dq, dk, dv, dbias, small["a_q"], small["a_k"] = attn_bwd(sv["z"], dcat, gq2, gk2, bias, batch, seq)
            small["rel"] = bias_grad(dbias)
            dz = jnp.concatenate([dq, dk, dv, dqm], axis=-1)
            win_t = a_win_t
        else:
            da, dg, small["cw"], small["csum"] = conv_bwd(
                sv["z"], dcat, cw_full, cb_full, lg_full, lb_full, batch, seq)
            dz = jnp.concatenate([da, dg, dqm], axis=-1)
            small["bb"] = col_sum(dz, name="grad_b_in")
            win_t = b_win_t
        big[f"win{l}"] = mm_tn(dz, sv["h"], name=f"grad_win_{l}")
        dh = mm_nn(dz, win_t, name=f"in_proj_bwd_{l}")
        big[f"wkv{l}"] = mm_tn(sv["mem_n"], dkv, name=f"grad_wkv_{l}")
        reduce_group([f"win{l}", f"wkv{l}"])
        dmem_n = mm_nt(dkv, wkv[l], out_dtype=F32, name=f"mem_kv_bwd_{l}")
        _, small[f"memnorm_{l}"] = rms_bwd(dmem_n, mem2, mem_norm_g[l:l + 1], zero_mem, name=f"rms_mem_bwd_{l}")
        dx, small[f"norm1_{l}"] = rms_bwd(dh, sv["xin"], norm1_g[l:l + 1], dx1, name=f"rms1_bwd_{l}")
    grad_x = dx.reshape(batch, seq, d)

    g_a_w_in = jnp.transpose(red["win0"])[None]
    g_b_w_in = jnp.transpose(red["win1"])[None]
    g_w_gate = jnp.stack([jnp.transpose(red["wg0"]), jnp.transpose(red["wg1"])])
    g_w_up = jnp.stack([jnp.transpose(red["wu0"]), jnp.transpose(red["wu1"])])
    g_w_down = jnp.stack([red["wd0"], red["wd1"]])
    g_w_out = jnp.stack([red["wo0"], red["wo1"]])
    g_w_mem_kv = jnp.stack([red["wkv0"], red["wkv1"]])

    def fold(v, groups):
        return jnp.sum(v.reshape(groups, HEAD_DIM), axis=0, keepdims=True)

    heads = a_rel_bias.shape[1]
    small_list = [
        jnp.concatenate([small["norm1_0"], small["norm1_1"]]),
        jnp.concatenate([small["memnorm_0"], small["memnorm_1"]]),
        fold(small["a_q"], 2), fold(small["a_k"], 2), small["rel"][:heads][None],
        small["bb"], small["cw"][:CONV_W][None], small["csum"][0:1], small["csum"][1:2], small["csum"][2:3],
        jnp.concatenate([fold(small["mq_0"], 4), fold(small["mq_1"], 4)]),
        jnp.concatenate([fold(small["mk_0"], 4), fold(small["mk_1"], 4)]),
        jnp.concatenate([small["norm2_0"], small["norm2_1"]]),
    ]
    small_full_shapes = [a.shape for a in small_list]
    summed = _unpack(small_exchange(_pack(small_list), reduce=True), small_full_shapes)
    (g_norm1, g_memnorm, g_aq, g_ak, g_rel, g_bb_full, g_cw_full, g_cb_full, g_lg_full, g_lb_full,
     g_mq, g_mk, g_norm2) = summed
    g_bb = lax.dynamic_slice_in_dim(g_bb_full, me * f_loc, f_loc, axis=1)
    g_cw = lax.dynamic_slice_in_dim(g_cw_full, me * c_loc, c_loc, axis=2)
    g_cb = lax.dynamic_slice_in_dim(g_cb_full, me * c_loc, c_loc, axis=1)
    g_lg = lax.dynamic_slice_in_dim(g_lg_full, me * c_loc, c_loc, axis=1)
    g_lb = lax.dynamic_slice_in_dim(g_lb_full, me * c_loc, c_loc, axis=1)

    grads = [g_norm1, g_memnorm, g_a_w_in, g_aq, g_ak, g_rel, g_b_w_in, g_bb, g_cw, g_cb, g_lg, g_lb,
             g_mq, g_mk, g_w_mem_kv, g_w_out, g_norm2, g_w_gate, g_w_up, g_w_down]
    weights = [norm1_g, mem_norm_g, a_w_in, a_q_g, a_k_g, a_rel_bias, b_w_in, b_b_in, b_conv_w, b_conv_b, b_ln_g,
               b_ln_b, mq_g, mk_g, w_mem_kv, w_out, norm2_g, w_gate, w_up, w_down]
    moms = [m_norm1_g, m_mem_norm_g, m_a_w_in, m_a_q_g, m_a_k_g, m_a_rel_bias, m_b_w_in, m_b_b_in, m_b_conv_w,
            m_b_conv_b, m_b_ln_g, m_b_ln_b, m_mq_g, m_mk_g, m_w_mem_kv, m_w_out, m_norm2_g, m_w_gate, m_w_up, m_w_down]
    vels = [v_norm1_g, v_mem_norm_g, v_a_w_in, v_a_q_g, v_a_k_g, v_a_rel_bias, v_b_w_in, v_b_b_in, v_b_conv_w,
            v_b_conv_b, v_b_ln_g, v_b_ln_b, v_mq_g, v_mk_g, v_w_mem_kv, v_w_out, v_norm2_g, v_w_gate, v_w_up, v_w_down]

    large = {2, 6, 14, 15, 17, 18, 19}
    deltas, new_m, new_v = [None] * 20, [None] * 20, [None] * 20
    for i in sorted(large):
        shp = weights[i].shape
        flat = (shp[0] * shp[1], shp[2])
        dl, nm, nv = adamw(weights[i].reshape(flat), grads[i].reshape(flat), moms[i].reshape(flat),
                           vels[i].reshape(flat), name=f"adamw_{i}")
        deltas[i], new_m[i], new_v[i] = dl.reshape(shp), nm.reshape(shp), nv.reshape(shp)
    small_idx = [i for i in range(20) if i not in large]
    small_shapes2 = [weights[i].shape for i in small_idx]
    dl, nm, nv = adamw(_pack([weights[i] for i in small_idx]), _pack([grads[i] for i in small_idx]),
                       _pack([moms[i] for i in small_idx]), _pack([vels[i] for i in small_idx]), name="adamw_small")
    for i, a, b, cc in zip(small_idx, _unpack(dl, small_shapes2), _unpack(nm, small_shapes2), _unpack(nv, small_shapes2)):
        deltas[i], new_m[i], new_v[i] = a, b, cc

    return (loss, grad_x, *grads, *deltas, *new_m, *new_v)
```

```python
import functools

import jax
import jax.numpy as jnp
from jax import lax
from jax.experimental import pallas as pl
from jax.experimental.pallas import tpu as pltpu
from jax.experimental.pallas import tpu_sc as plsc

F32 = jnp.float32
BF16 = jnp.bfloat16
HIGHEST = lax.Precision.HIGHEST
MESH = pl.DeviceIdType.MESH
ANY = pl.BlockSpec(memory_space=pl.ANY)

N_DEV = 8
D_MODEL = 1024
HEAD_DIM = 64
TOK_WIDTH = 768
MEM_WIDTH = 256
CHUNK = 64
Q_BLOCK = 256
KEY_WIN = 768
BAND = 576
N_REL = 192
CONV_W = 31
CONV_HALO = 32
NORM_EPS = 1e-6
NEG_INF = -1e30
ATTN_SCALE = HEAD_DIM ** -0.5
LANES = 128
ROW_TILE = 512
VMEM_LIMIT = 56 * 1024 * 1024

ADAM_LR, ADAM_B1, ADAM_B2, ADAM_EPS, ADAM_WD, ADAM_STEP = 0.001, 0.9, 0.999, 1e-08, 0.01, 10


def _params(*sem):
    return pltpu.CompilerParams(dimension_semantics=sem, vmem_limit_bytes=VMEM_LIMIT)


def _row_tile(m):
    return ROW_TILE if m % ROW_TILE == 0 else m


def _col_tile(n, cap=1408):
    best = None
    for t in range(LANES, min(n, cap) + 1, LANES):
        if n % t == 0:
            best = t
    return best if best is not None else n


def _dot(a, b, ca, cb):
    return lax.dot_general(a, b, (((ca,), (cb,)), ((), ())), preferred_element_type=F32)


def _sigmoid(x):
    return 0.5 * jnp.tanh(0.5 * x) + 0.5


def mm_nt(a, b, bias=None, out_dtype=BF16, name="mm_nt"):
    m, k = a.shape
    n = b.shape[0]
    tm, tn = _row_tile(m), _col_tile(n)

    def body(*refs):
        a_ref, b_ref = refs[0], refs[1]
        o_ref = refs[-1]
        acc = _dot(a_ref[...].astype(BF16), b_ref[...].astype(BF16), 1, 1)
        if bias is not None:
            acc = acc + refs[2][...]
        o_ref[...] = acc.astype(o_ref.dtype)

    in_specs = [pl.BlockSpec((tm, k), lambda i, j: (i, 0)), pl.BlockSpec((tn, k), lambda i, j: (j, 0))]
    args = [a, b]
    if bias is not None:
        in_specs.append(pl.BlockSpec((1, tn), lambda i, j: (0, j)))
        args.append(bias)
    return pl.pallas_call(
        body, out_shape=jax.ShapeDtypeStruct((m, n), out_dtype), grid=(m // tm, n // tn),
        in_specs=in_specs, out_specs=pl.BlockSpec((tm, tn), lambda i, j: (i, j)),
        compiler_params=_params("parallel", "arbitrary"), name=name)(*args)


def mm_nn(a, b, res=None, out_dtype=F32, name="mm_nn"):
    m, k = a.shape
    n = b.shape[1]
    tm, tn = _row_tile(m), _col_tile(n, 1024)

    def body(*refs):
        a_ref, b_ref = refs[0], refs[1]
        o_ref = refs[-1]
        acc = _dot(a_ref[...].astype(BF16), b_ref[...].astype(BF16), 1, 0)
        if res is not None:
            acc = acc + refs[2][...]
        o_ref[...] = acc.astype(o_ref.dtype)

    in_specs = [pl.BlockSpec((tm, k), lambda i, j: (i, 0)), pl.BlockSpec((k, tn), lambda i, j: (0, j))]
    args = [a, b]
    if res is not None:
        in_specs.append(pl.BlockSpec((tm, tn), lambda i, j: (i, j)))
        args.append(res)
    return pl.pallas_call(
        body, out_shape=jax.ShapeDtypeStruct((m, n), out_dtype), grid=(m // tm, n // tn),
        in_specs=in_specs, out_specs=pl.BlockSpec((tm, tn), lambda i, j: (i, j)),
        compiler_params=_params("parallel", "arbitrary"), name=name)(*args)


def mm2_nn(a1, b1, a2, b2, name="mm2_nn"):
    m, k = a1.shape
    n = b1.shape[1]
    tm, tn = _row_tile(m), _col_tile(n, 512)

    def body(a1_ref, b1_ref, a2_ref, b2_ref, o_ref):
        o_ref[...] = _dot(a1_ref[...], b1_ref[...], 1, 0) + _dot(a2_ref[...], b2_ref[...], 1, 0)

    a_spec = pl.BlockSpec((tm, k), lambda i, j: (i, 0))
    b_spec = pl.BlockSpec((k, tn), lambda i, j: (0, j))
    return pl.pallas_call(
        body, out_shape=jax.ShapeDtypeStruct((m, n), F32), grid=(m // tm, n // tn),
        in_specs=[a_spec, b_spec, a_spec, b_spec], out_specs=pl.BlockSpec((tm, tn), lambda i, j: (i, j)),
        compiler_params=_params("parallel", "arbitrary"), name=name)(a1, b1, a2, b2)


def mm_tn(a, b, out_dtype=BF16, name="mm_tn"):
    t, r = a.shape
    c = b.shape[1]
    tt, tr = _row_tile(t), _col_tile(r)
    nt = t // tt

    def body(a_ref, b_ref, o_ref, acc_ref):
        step = pl.program_id(1)

        @pl.when(step == 0)
        def _():
            acc_ref[...] = jnp.zeros_like(acc_ref)

        acc_ref[...] += _dot(a_ref[...].astype(BF16), b_ref[...].astype(BF16), 0, 0)

        @pl.when(step == nt - 1)
        def _():
            o_ref[...] = acc_ref[...].astype(o_ref.dtype)

    return pl.pallas_call(
        body, out_shape=jax.ShapeDtypeStruct((r, c), out_dtype), grid=(r // tr, nt),
        in_specs=[pl.BlockSpec((tt, tr), lambda i, s: (s, i)), pl.BlockSpec((tt, c), lambda i, s: (s, 0))],
        out_specs=pl.BlockSpec((tr, c), lambda i, s: (i, 0)),
        scratch_shapes=[pltpu.VMEM((tr, c), F32)],
        compiler_params=_params("parallel", "arbitrary"), name=name)(a, b)


def rms_fwd(x, g, name="rms_fwd"):
    n, d = x.shape
    tm = _row_tile(n)

    def body(x_ref, g_ref, o_ref):
        xv = x_ref[...]
        r = lax.rsqrt(jnp.mean(xv * xv, axis=-1, keepdims=True) + NORM_EPS)
        o_ref[...] = (xv * r * g_ref[...]).astype(o_ref.dtype)

    return pl.pallas_call(
        body, out_shape=jax.ShapeDtypeStruct((n, d), BF16), grid=(n // tm,),
        in_specs=[pl.BlockSpec((tm, d), lambda i: (i, 0)), pl.BlockSpec((1, d), lambda i: (0, 0))],
        out_specs=pl.BlockSpec((tm, d), lambda i: (i, 0)),
        compiler_params=_params("parallel"), name=name)(x, g)


def rms_bwd(dh, x, g, dres, name="rms_bwd"):
    n, d = x.shape
    tm = _row_tile(n)

    def body(dh_ref, x_ref, g_ref, dres_ref, dx_ref, dg_ref):
        @pl.when(pl.program_id(0) == 0)
        def _():
            dg_ref[...] = jnp.zeros_like(dg_ref)

        xv = x_ref[...]
        dhv = dh_ref[...].astype(F32)
        r = lax.rsqrt(jnp.mean(xv * xv, axis=-1, keepdims=True) + NORM_EPS)
        xhat = xv * r
        dg_ref[...] += jnp.sum(dhv * xhat, axis=0, keepdims=True)
        dxhat = dhv * g_ref[...]
        mean_t = jnp.mean(dxhat * xhat, axis=-1, keepdims=True)
        dx_ref[...] = dres_ref[...] + r * (dxhat - xhat * mean_t)

    row = pl.BlockSpec((tm, d), lambda i: (i, 0))
    vec = pl.BlockSpec((1, d), lambda i: (0, 0))
    return pl.pallas_call(
        body, out_shape=(jax.ShapeDtypeStruct((n, d), F32), jax.ShapeDtypeStruct((1, d), F32)), grid=(n // tm,),
        in_specs=[row, row, vec, row], out_specs=(row, vec),
        compiler_params=_params("arbitrary"), name=name)(dh, x, g, dres)


def gate_up(h2, wg_t, wu_t, name="gate_up"):
    n, d = h2.shape
    f = wg_t.shape[0]
    tm, tn = _row_tile(n), _col_tile(f)

    def body(h_ref, wg_ref, wu_ref, g_ref, u_ref, a_ref):
        hv = h_ref[...]
        gv = _dot(hv, wg_ref[...], 1, 1)
        uv = _dot(hv, wu_ref[...], 1, 1)
        g_ref[...] = gv.astype(BF16)
        u_ref[...] = uv.astype(BF16)
        a_ref[...] = (gv * _sigmoid(gv) * uv).astype(BF16)

    w_spec = pl.BlockSpec((tn, d), lambda i, j: (j, 0))
    o_spec = pl.BlockSpec((tm, tn), lambda i, j: (i, j))
    o_shape = jax.ShapeDtypeStruct((n, f), BF16)
    return pl.pallas_call(
        body, out_shape=(o_shape, o_shape, o_shape), grid=(n // tm, f // tn),
        in_specs=[pl.BlockSpec((tm, d), lambda i, j: (i, 0)), w_spec, w_spec], out_specs=(o_spec, o_spec, o_spec),
        compiler_params=_params("parallel", "arbitrary"), name=name)(h2, wg_t, wu_t)


def ffn_bwd_act(dx, wd, gate, up, name="ffn_bwd_act"):
    n, d = dx.shape
    f = wd.shape[0]
    tm, tn = _row_tile(n), _col_tile(f)

    def body(dx_ref, wd_ref, g_ref, u_ref, dg_ref, du_ref):
        dact = _dot(dx_ref[...].astype(BF16), wd_ref[...], 1, 1)
        gv = g_ref[...].astype(F32)
        uv = u_ref[...].astype(F32)
        sg = _sigmoid(gv)
        dg_ref[...] = (dact * uv * sg * (1.0 + gv * (1.0 - sg))).astype(BF16)
        du_ref[...] = (dact * gv * sg).astype(BF16)

    t_spec = pl.BlockSpec((tm, tn), lambda i, j: (i, j))
    o_shape = jax.ShapeDtypeStruct((n, f), BF16)
    return pl.pallas_call(
        body, out_shape=(o_shape, o_shape), grid=(n // tm, f // tn),
        in_specs=[pl.BlockSpec((tm, d), lambda i, j: (i, 0)), pl.BlockSpec((tn, d), lambda i, j: (j, 0)), t_spec, t_spec],
        out_specs=(t_spec, t_spec),
        compiler_params=_params("parallel", "arbitrary"), name=name)(dx, wd, gate, up)


def _group_masks(width):
    lane = lax.broadcasted_iota(jnp.int32, (1, width), 1)
    return [(lane >= HEAD_DIM * g) & (lane < HEAD_DIM * (g + 1)) for g in range(width // HEAD_DIM)]


def _group_sum(x, masks):
    out = jnp.zeros_like(x)
    for msk in masks:
        s = jnp.sum(jnp.where(msk, x, 0.0), axis=-1, keepdims=True)
        out = jnp.where(msk, s, out)
    return out


def _head_norm(x, gain, masks):
    r = lax.rsqrt(_group_sum(x * x, masks) * (1.0 / HEAD_DIM) + NORM_EPS)
    xhat = x * r
    return xhat * gain, xhat, r


def _head_norm_bwd(dxn, xhat, r, gain, masks):
    dgain = jnp.sum(dxn * xhat, axis=0, keepdims=True)
    dxhat = dxn * gain
    mean_t = _group_sum(dxhat * xhat, masks) * (1.0 / HEAD_DIM)
    return r * (dxhat - xhat * mean_t), dgain


def _softmax_rows(s):
    e = jnp.exp(s - jnp.max(s, axis=-1, keepdims=True))
    return e * (1.0 / jnp.sum(e, axis=-1, keepdims=True))


def _rel_onehot():
    col = lax.broadcasted_iota(jnp.int32, (1, KEY_WIN), 1)
    off = jnp.where(col < KEY_WIN - LANES, col, col - KEY_WIN)
    idx = jnp.clip(8 * CHUNK - off, -(CHUNK - 1), LANES) + (CHUNK - 1)
    return (lax.broadcasted_iota(jnp.int32, (N_REL, KEY_WIN), 0) == idx).astype(F32)


def bias_blocks(rel16):
    heads = TOK_WIDTH // HEAD_DIM

    def body(rel_ref, o_ref, u_ref):
        u_ref[...] = jnp.dot(rel_ref[...], _rel_onehot(), precision=HIGHEST, preferred_element_type=F32)
        row = lax.broadcasted_iota(jnp.int32, (CHUNK, KEY_WIN), 0)
        col = lax.broadcasted_iota(jnp.int32, (CHUNK, KEY_WIN), 1)
        for h in range(heads):
            xv = jnp.broadcast_to(u_ref[h:h + 1, :], (CHUNK, KEY_WIN))
            for b in range(6):
                xv = jnp.where(((row >> b) & 1) == 1, pltpu.roll(xv, 1 << b, axis=1), xv)
            xv = jnp.where(col < BAND, xv, NEG_INF)
            for i in range(Q_BLOCK // CHUNK):
                o_ref[h, CHUNK * i:CHUNK * (i + 1), :] = pltpu.roll(xv, CHUNK * i, axis=1) if i else xv

    return pl.pallas_call(
        body, out_shape=jax.ShapeDtypeStruct((heads, Q_BLOCK, KEY_WIN), F32),
        scratch_shapes=[pltpu.VMEM((16, KEY_WIN), F32)], name="bias_blocks")(rel16)


def bias_grad(dbias):
    heads = dbias.shape[0]

    def body(db_ref, o_ref, y_ref):
        y_ref[...] = jnp.zeros_like(y_ref)
        row = lax.broadcasted_iota(jnp.int32, (CHUNK, KEY_WIN), 0)
        for h in range(heads):
            fv = db_ref[h, 0:CHUNK, :]
            for i in range(1, Q_BLOCK // CHUNK):
                fv = fv + pltpu.roll(db_ref[h, CHUNK * i:CHUNK * (i + 1), :], KEY_WIN - CHUNK * i, axis=1)
            for b in range(6):
                fv = jnp.where(((row >> b) & 1) == 1, pltpu.roll(fv, KEY_WIN - (1 << b), axis=1), fv)
            y_ref[h:h + 1, :] = jnp.sum(fv, axis=0, keepdims=True)
        o_ref[...] = lax.dot_general(y_ref[...], _rel_onehot(), (((1,), (1,)), ((), ())),
                                     precision=HIGHEST, preferred_element_type=F32)

    return pl.pallas_call(
        body, out_shape=jax.ShapeDtypeStruct((16, N_REL), F32),
        scratch_shapes=[pltpu.VMEM((16, KEY_WIN), F32)], name="bias_grad")(dbias)


def _attn_windows(seq):
    out = []
    for j in range(seq // Q_BLOCK):
        r0 = j * Q_BLOCK
        k0 = max(0, r0 - 8 * CHUNK)
        width = r0 + Q_BLOCK - k0
        out.append((r0, k0, width, KEY_WIN - width))
    return out


def attn_fwd(z, gq2, gk2, bias, batch, seq):
    n = z.shape[0]
    pairs = TOK_WIDTH // LANES

    def body(q_ref, k_ref, v_ref, gq_ref, gk_ref, b_ref, o_ref, qs_s, kn_s):
        masks = _group_masks(LANES)
        qs_s[...] = (_head_norm(q_ref[...].astype(F32), gq_ref[...], masks)[0] * ATTN_SCALE).astype(BF16)
        kn_s[...] = _head_norm(k_ref[...].astype(F32), gk_ref[...], masks)[0].astype(BF16)
        for r0, k0, width, c0 in _attn_windows(seq):
            qb = qs_s[r0:r0 + Q_BLOCK, :]
            kw = kn_s[k0:k0 + width, :]
            vw = v_ref[k0:k0 + width, :]
            out = jnp.zeros((Q_BLOCK, LANES), F32)
            for h, msk in enumerate(masks):
                qh = jnp.where(msk, qb, jnp.zeros_like(qb))
                s = _dot(qh, kw, 1, 1) + b_ref[h, :, c0:KEY_WIN]
                p = _softmax_rows(s).astype(BF16)
                out = jnp.where(msk, _dot(p, vw, 1, 0), out)
            o_ref[r0:r0 + Q_BLOCK, :] = out.astype(o_ref.dtype)

    def col(off):
        return pl.BlockSpec((seq, LANES), lambda b, p: (b, off + p))

    vec = pl.BlockSpec((1, LANES), lambda b, p: (0, 0))
    return pl.pallas_call(
        body, out_shape=jax.ShapeDtypeStruct((n, D_MODEL), BF16), grid=(batch, pairs),
        in_specs=[col(0), col(pairs), col(2 * pairs), vec, vec,
                  pl.BlockSpec((2, Q_BLOCK, KEY_WIN), lambda b, p: (p, 0, 0))],
        out_specs=pl.BlockSpec((seq, LANES), lambda b, p: (b, p)),
        scratch_shapes=[pltpu.VMEM((seq, LANES), BF16), pltpu.VMEM((seq, LANES), BF16)],
        compiler_params=_params("parallel", "arbitrary"), name="attn_fwd")(z, z, z, gq2, gk2, bias)


def attn_bwd(z, dcat, gq2, gk2, bias, batch, seq):
    n = z.shape[0]
    pairs = TOK_WIDTH // LANES

    def body(q_ref, k_ref, v_ref, do_ref, gq_ref, gk_ref, b_ref,
             dz_ref, db_ref, dgq_ref, dgk_ref, qs_s, kn_s, dqn_s, dkn_s, dv_s, dk_o, dv_o):
        pi, bi, which = pl.program_id(0), pl.program_id(1), pl.program_id(2)

        @pl.when(which == 0)
        def _():
            masks = _group_masks(LANES)

            @pl.when(bi == 0)
            def _():
                db_ref[...] = jnp.zeros_like(db_ref)

            @pl.when((bi == 0) & (pi == 0))
            def _():
                dgq_ref[...] = jnp.zeros_like(dgq_ref)
                dgk_ref[...] = jnp.zeros_like(dgk_ref)

            qn, qhat, rq = _head_norm(q_ref[...].astype(F32), gq_ref[...], masks)
            kn, khat, rk = _head_norm(k_ref[...].astype(F32), gk_ref[...], masks)
            qs_s[...] = (qn * ATTN_SCALE).astype(BF16)
            kn_s[...] = kn.astype(BF16)
            dkn_s[...] = jnp.zeros_like(dkn_s)
            dv_s[...] = jnp.zeros_like(dv_s)
            for r0, k0, width, c0 in _attn_windows(seq):
                qb = qs_s[r0:r0 + Q_BLOCK, :]
                dob = do_ref[r0:r0 + Q_BLOCK, :]
                kw = kn_s[k0:k0 + width, :]
                vw = v_ref[k0:k0 + width, :]
                dq_acc = jnp.zeros((Q_BLOCK, LANES), F32)
                dk_acc = jnp.zeros((width, LANES), F32)
                dv_acc = jnp.zeros((width, LANES), F32)
                for h, msk in enumerate(masks):
                    qh = jnp.where(msk, qb, jnp.zeros_like(qb))
                    doh = jnp.where(msk, dob, jnp.zeros_like(dob))
                    p = _softmax_rows(_dot(qh, kw, 1, 1) + b_ref[h, :, c0:KEY_WIN])
                    dp = _dot(doh, vw, 1, 1)
                    ds = p * (dp - jnp.sum(p * dp, axis=-1, keepdims=True))
                    db_ref[h, :, c0:KEY_WIN] += ds
                    dsb = ds.astype(BF16)
                    dq_acc = jnp.where(msk, _dot(dsb, kw, 1, 0), dq_acc)
                    dk_acc = jnp.where(msk, _dot(dsb, qb, 0, 0), dk_acc)
                    dv_acc = jnp.where(msk, _dot(p.astype(BF16), dob, 0, 0), dv_acc)
                dqn_s[r0:r0 + Q_BLOCK, :] = dq_acc * ATTN_SCALE
                dkn_s[k0:k0 + width, :] += dk_acc
                dv_s[k0:k0 + width, :] += dv_acc
            dq, dgq = _head_norm_bwd(dqn_s[...], qhat, rq, gq_ref[...], masks)
            dk, dgk = _head_norm_bwd(dkn_s[...], khat, rk, gk_ref[...], masks)
            dz_ref[...] = dq.astype(dz_ref.dtype)
            dk_o[...] = dk.astype(dk_o.dtype)
            dv_o[...] = dv_s[...].astype(dv_o.dtype)
            dgq_ref[...] += dgq
            dgk_ref[...] += dgk

        @pl.when(which == 1)
        def _():
            dz_ref[...] = dk_o[...]

        @pl.when(which == 2)
        def _():
            dz_ref[...] = dv_o[...]

    def col(off):
        return pl.BlockSpec((seq, LANES), lambda p, b, t: (b, off + p))

    vec = pl.BlockSpec((1, LANES), lambda p, b, t: (0, 0))
    blk = pl.BlockSpec((2, Q_BLOCK, KEY_WIN), lambda p, b, t: (p, 0, 0))
    v_shape = jax.ShapeDtypeStruct((1, LANES), F32)
    return pl.pallas_call(
        body,
        out_shape=(jax.ShapeDtypeStruct(z.shape, BF16), jax.ShapeDtypeStruct(bias.shape, F32), v_shape, v_shape),
        grid=(pairs, batch, 3),
        in_specs=[col(0), col(pairs), col(2 * pairs), col(0), vec, vec, blk],
        out_specs=(pl.BlockSpec((seq, LANES), lambda p, b, t: (b, t * pairs + p)), blk, vec, vec),
        scratch_shapes=[pltpu.VMEM((seq, LANES), BF16), pltpu.VMEM((seq, LANES), BF16),
                        pltpu.VMEM((seq, LANES), F32), pltpu.VMEM((seq, LANES), F32), pltpu.VMEM((seq, LANES), F32),
                        pltpu.VMEM((seq, LANES), BF16), pltpu.VMEM((seq, LANES), BF16)],
        compiler_params=_params("arbitrary", "arbitrary", "arbitrary"), name="attn_bwd")(
            z, z, z, dcat, gq2, gk2, bias)


MEM_ROWS = 512


def memattn_fwd(z, kv, gq4, gk4, cat, batch, seq, qcol, name):
    mtok = kv.shape[0] // batch
    rows = min(MEM_ROWS, seq)

    def body(q_ref, kv_ref, gq_ref, gk_ref, cat_ref, o_ref):
        del cat_ref
        masks = _group_masks(MEM_WIDTH)
        kn = _head_norm(kv_ref[:, 0:MEM_WIDTH], gk_ref[...], masks)[0].astype(BF16)
        vm = kv_ref[:, MEM_WIDTH:2 * MEM_WIDTH].astype(BF16)
        for t in range(seq // rows):
            sl = slice(t * rows, (t + 1) * rows)
            qs = (_head_norm(q_ref[sl, :].astype(F32), gq_ref[...], masks)[0] * ATTN_SCALE).astype(BF16)
            out = jnp.zeros((rows, MEM_WIDTH), F32)
            for msk in masks:
                qh = jnp.where(msk, qs, jnp.zeros_like(qs))
                p = _softmax_rows(_dot(qh, kn, 1, 1)).astype(BF16)
                out = jnp.where(msk, _dot(p, vm, 1, 0), out)
            o_ref[sl, :] = out.astype(o_ref.dtype)

    vec = pl.BlockSpec((1, MEM_WIDTH), lambda b: (0, 0))
    return pl.pallas_call(
        body, out_shape=jax.ShapeDtypeStruct(cat.shape, cat.dtype), grid=(batch,),
        in_specs=[pl.BlockSpec((seq, MEM_WIDTH), lambda b: (b, qcol)),
                  pl.BlockSpec((mtok, 2 * MEM_WIDTH), lambda b: (b, 0)), vec, vec, ANY],
        out_specs=pl.BlockSpec((seq, MEM_WIDTH), lambda b: (b, TOK_WIDTH // MEM_WIDTH)),
        input_output_aliases={4: 0},
        compiler_params=_params("parallel"), name=name)(z, kv, gq4, gk4, cat)


def memattn_bwd(z, kv, dcat, gq4, gk4, dz, batch, seq, qcol, name):
    mtok = kv.shape[0] // batch
    rows = min(MEM_ROWS, seq)

    def body(q_ref, kv_ref, do_ref, gq_ref, gk_ref, dz_in_ref, dq_ref, dkv_ref, dgq_ref, dgk_ref):
        del dz_in_ref
        @pl.when(pl.program_id(0) == 0)
        def _():
            dgq_ref[...] = jnp.zeros_like(dgq_ref)
            dgk_ref[...] = jnp.zeros_like(dgk_ref)

        masks = _group_masks(MEM_WIDTH)
        kn_f, khat, rk = _head_norm(kv_ref[:, 0:MEM_WIDTH], gk_ref[...], masks)
        kn = kn_f.astype(BF16)
        vm = kv_ref[:, MEM_WIDTH:2 * MEM_WIDTH].astype(BF16)
        dkn = jnp.zeros((mtok, MEM_WIDTH), F32)
        dvm = jnp.zeros((mtok, MEM_WIDTH), F32)
        dgq = jnp.zeros((1, MEM_WIDTH), F32)
        for t in range(seq // rows):
            sl = slice(t * rows, (t + 1) * rows)
            qn_f, qhat, rq = _head_norm(q_ref[sl, :].astype(F32), gq_ref[...], masks)
            qs = (qn_f * ATTN_SCALE).astype(BF16)
            dob = do_ref[sl, :]
            dqn = jnp.zeros((rows, MEM_WIDTH), F32)
            for msk in masks:
                qh = jnp.where(msk, qs, jnp.zeros_like(qs))
                doh = jnp.where(msk, dob, jnp.zeros_like(dob))
                p = _softmax_rows(_dot(qh, kn, 1, 1))
                dp = _dot(doh, vm, 1, 1)
                ds = p * (dp - jnp.sum(p * dp, axis=-1, keepdims=True))
                dsb = ds.astype(BF16)
                dqn = jnp.where(msk, _dot(dsb, kn, 1, 0), dqn)
                dkn = dkn + jnp.where(msk, _dot(dsb, qs, 0, 0), 0.0)
                dvm = dvm + jnp.where(msk, _dot(p.astype(BF16), dob, 0, 0), 0.0)
            dq, dg = _head_norm_bwd(dqn * ATTN_SCALE, qhat, rq, gq_ref[...], masks)
            dq_ref[sl, :] = dq.astype(dq_ref.dtype)
            dgq = dgq + dg
        dk, dgk = _head_norm_bwd(dkn, khat, rk, gk_ref[...], masks)
        dkv_ref[:, 0:MEM_WIDTH] = dk
        dkv_ref[:, MEM_WIDTH:2 * MEM_WIDTH] = dvm
        dgq_ref[...] += dgq
        dgk_ref[...] += dgk

    vec = pl.BlockSpec((1, MEM_WIDTH), lambda b: (0, 0))
    kv_spec = pl.BlockSpec((mtok, 2 * MEM_WIDTH), lambda b: (b, 0))
    v_shape = jax.ShapeDtypeStruct((1, MEM_WIDTH), F32)
    q_spec = pl.BlockSpec((seq, MEM_WIDTH), lambda b: (b, qcol))
    return pl.pallas_call(
        body,
        out_shape=(jax.ShapeDtypeStruct(dz.shape, dz.dtype), jax.ShapeDtypeStruct(kv.shape, F32), v_shape, v_shape),
        grid=(batch,),
        in_specs=[q_spec, kv_spec, pl.BlockSpec((seq, MEM_WIDTH), lambda b: (b, TOK_WIDTH // MEM_WIDTH)), vec, vec, ANY],
        out_specs=(q_spec, kv_spec, vec, vec),
        input_output_aliases={5: 0},
        compiler_params=_params("arbitrary"), name=name)(z, kv, dcat, gq4, gk4, dz)


CONV_ROWS = 256


def _glu(a_ref, g_ref):
    return a_ref[...].astype(F32) * _sigmoid(g_ref[...].astype(F32))


def _layer_norm_stats(y):
    mu = jnp.mean(y, axis=-1, keepdims=True)
    yc = y - mu
    rstd = lax.rsqrt(jnp.mean(yc * yc, axis=-1, keepdims=True) + NORM_EPS)
    return yc * rstd, rstd


CONV_WIN = CONV_HALO + CONV_ROWS
SUBLANES = 8
SHIFT_ROWS = CONV_WIN - SUBLANES


def _preshift(win, shifted):
    for s in range(1, SUBLANES):
        shifted[s - 1, :, :] = win[s:s + SHIFT_ROWS, :]


def _tap(win, shifted, off):
    s = off % SUBLANES
    base = off - s
    if s == 0:
        return win[base:base + CONV_ROWS, :]
    return shifted[s - 1, base:base + CONV_ROWS, :]


def conv_fwd(z, cw, cb, lg, lb, batch, seq):
    n = z.shape[0]
    nt = seq // CONV_ROWS
    sub = CONV_ROWS // CONV_HALO
    lead = CONV_HALO - (CONV_W - 1)

    def body(a_ref, g_ref, ap_ref, gp_ref, cw_ref, cb_ref, lg_ref, lb_ref, o_ref, y_ref, win, shifted):
        first = pl.program_id(1) == 0
        win[0:CONV_HALO, :] = jnp.where(first, 0.0, _glu(ap_ref, gp_ref))
        win[CONV_HALO:CONV_WIN, :] = _glu(a_ref, g_ref)
        _preshift(win, shifted)
        y = jnp.zeros((CONV_ROWS, TOK_WIDTH), F32) + cb_ref[...]
        for w in range(CONV_W):
            y = y + _tap(win, shifted, lead + w) * cw_ref[w:w + 1, :]
        y_ref[...] = y
        yh, _ = _layer_norm_stats(y)
        t = yh * lg_ref[...] + lb_ref[...]
        o_ref[...] = (t * _sigmoid(t)).astype(o_ref.dtype)

    def cur(c):
        return pl.BlockSpec((CONV_ROWS, TOK_WIDTH), lambda b, i: (b * nt + i, c))

    def prev(c):
        return pl.BlockSpec((CONV_HALO, TOK_WIDTH), lambda b, i: (jnp.maximum((b * nt + i) * sub - 1, 0), c))

    vec = pl.BlockSpec((1, TOK_WIDTH), lambda b, i: (0, 0))
    return pl.pallas_call(
        body, out_shape=(jax.ShapeDtypeStruct((n, D_MODEL), BF16), jax.ShapeDtypeStruct((n, TOK_WIDTH), F32)),
        grid=(batch, nt),
        in_specs=[cur(0), cur(1), prev(0), prev(1), pl.BlockSpec((32, TOK_WIDTH), lambda b, i: (0, 0)), vec, vec, vec],
        out_specs=(cur(0), cur(0)),
        scratch_shapes=[pltpu.VMEM((CONV_WIN, TOK_WIDTH), F32), pltpu.VMEM((SUBLANES - 1, SHIFT_ROWS, TOK_WIDTH), F32)],
        compiler_params=_params("parallel", "arbitrary"), name="conv_fwd")(z, z, z, z, cw, cb, lg, lb)


def conv_bwd(z, y, dcat, cw, lg, lb, batch, seq):
    n = z.shape[0]
    nt = seq // CONV_ROWS
    sub = CONV_ROWS // CONV_HALO
    lead = CONV_HALO - (CONV_W - 1)
    last_blk = n // CONV_HALO - 1

    def body(a_ref, g_ref, ap_ref, gp_ref, y_ref, yn_ref, do_ref, don_ref, cw_ref, lg_ref, lb_ref,
             dz_ref, dcw_ref, dsm_ref, win, shifted, dyw, dshifted, dg_o):
        b, i, which = pl.program_id(0), pl.program_id(1), pl.program_id(2)

        @pl.when(which == 0)
        def _():
            first, last = i == 0, i == nt - 1

            @pl.when((b == 0) & (i == 0))
            def _():
                dcw_ref[...] = jnp.zeros_like(dcw_ref)
                dsm_ref[...] = jnp.zeros_like(dsm_ref)

            av = a_ref[...].astype(F32)
            sg = _sigmoid(g_ref[...].astype(F32))
            win[0:CONV_HALO, :] = jnp.where(first, 0.0, _glu(ap_ref, gp_ref))
            win[CONV_HALO:CONV_WIN, :] = av * sg
            _preshift(win, shifted)
            yv = jnp.concatenate([y_ref[...], yn_ref[...]], axis=0)
            yh, rstd = _layer_norm_stats(yv)
            t = yh * lg_ref[...] + lb_ref[...]
            st = _sigmoid(t)
            dout = jnp.concatenate(
                [do_ref[...].astype(F32), jnp.where(last, 0.0, don_ref[...].astype(F32))], axis=0)
            dt = dout * st * (1.0 + t * (1.0 - st))
            dyh = dt * lg_ref[...]
            dy = rstd * (dyh - jnp.mean(dyh, axis=-1, keepdims=True)
                         - yh * jnp.mean(dyh * yh, axis=-1, keepdims=True))
            dyw[...] = dy
            _preshift(dyw, dshifted)
            dsm_ref[0:1, :] += jnp.sum(dy[0:CONV_ROWS], axis=0, keepdims=True)
            dsm_ref[1:2, :] += jnp.sum((dt * yh)[0:CONV_ROWS], axis=0, keepdims=True)
            dsm_ref[2:3, :] += jnp.sum(dt[0:CONV_ROWS], axis=0, keepdims=True)
            dyc = dyw[0:CONV_ROWS, :]
            dglu = jnp.zeros((CONV_ROWS, TOK_WIDTH), F32)
            for w in range(CONV_W):
                dcw_ref[w:w + 1, :] += jnp.sum(dyc * _tap(win, shifted, lead + w), axis=0, keepdims=True)
                dglu = dglu + _tap(dyw, dshifted, CONV_W - 1 - w) * cw_ref[w:w + 1, :]
            dz_ref[...] = (dglu * sg).astype(dz_ref.dtype)
            dg_o[...] = (dglu * av * sg * (1.0 - sg)).astype(dg_o.dtype)

        @pl.when(which == 1)
        def _():
            dz_ref[...] = dg_o[...]

    def cur(c):
        return pl.BlockSpec((CONV_ROWS, TOK_WIDTH), lambda b, i, t: (b * nt + i, c))

    def prev(c):
        return pl.BlockSpec((CONV_HALO, TOK_WIDTH), lambda b, i, t: (jnp.maximum((b * nt + i) * sub - 1, 0), c))

    nxt = pl.BlockSpec((CONV_HALO, TOK_WIDTH), lambda b, i, t: (jnp.minimum((b * nt + i + 1) * sub, last_blk), 0))
    vec = pl.BlockSpec((1, TOK_WIDTH), lambda b, i, t: (0, 0))
    full32 = pl.BlockSpec((32, TOK_WIDTH), lambda b, i, t: (0, 0))
    return pl.pallas_call(
        body,
        out_shape=(jax.ShapeDtypeStruct(z.shape, BF16), jax.ShapeDtypeStruct((32, TOK_WIDTH), F32),
                   jax.ShapeDtypeStruct((8, TOK_WIDTH), F32)),
        grid=(batch, nt, 2),
        in_specs=[cur(0), cur(1), prev(0), prev(1), cur(0), nxt, cur(0), nxt, full32, vec, vec],
        out_specs=(pl.BlockSpec((CONV_ROWS, TOK_WIDTH), lambda b, i, t: (b * nt + i, t)), full32,
                   pl.BlockSpec((8, TOK_WIDTH), lambda b, i, t: (0, 0))),
        scratch_shapes=[pltpu.VMEM((CONV_WIN, TOK_WIDTH), F32), pltpu.VMEM((SUBLANES - 1, SHIFT_ROWS, TOK_WIDTH), F32),
                        pltpu.VMEM((CONV_WIN, TOK_WIDTH), F32), pltpu.VMEM((SUBLANES - 1, SHIFT_ROWS, TOK_WIDTH), F32),
                        pltpu.VMEM((CONV_ROWS, TOK_WIDTH), BF16)],
        compiler_params=_params("arbitrary", "arbitrary", "arbitrary"), name="conv_bwd")(
            z, z, z, z, y, y, dcat, dcat, cw, lg, lb)


def loss_head(y, target):
    n, d = y.shape
    tm = _row_tile(n)
    nt = n // tm

    def body(y_ref, t_ref, dy_ref, l_ref, acc_ref):
        i = pl.program_id(0)

        @pl.when(i == 0)
        def _():
            acc_ref[...] = jnp.zeros_like(acc_ref)

        err = y_ref[...] - t_ref[...]
        dy_ref[...] = err * (1.0 / d)
        acc_ref[...] += jnp.sum(err * err, axis=0, keepdims=True)

        @pl.when(i == nt - 1)
        def _():
            total = jnp.sum(acc_ref[...], axis=-1, keepdims=True) * (0.5 / d)
            l_ref[...] = jnp.broadcast_to(total, l_ref.shape)

    row = pl.BlockSpec((tm, d), lambda i: (i, 0))
    return pl.pallas_call(
        body, out_shape=(jax.ShapeDtypeStruct((n, d), F32), jax.ShapeDtypeStruct((8, LANES), F32)), grid=(nt,),
        in_specs=[row, row], out_specs=(row, pl.BlockSpec((8, LANES), lambda i: (0, 0))),
        scratch_shapes=[pltpu.VMEM((1, d), F32)],
        compiler_params=_params("arbitrary"), name="loss_head")(y, target)


def col_sum(x, name="col_sum"):
    n, c = x.shape
    tm = _row_tile(n)

    def body(x_ref, o_ref):
        @pl.when(pl.program_id(0) == 0)
        def _():
            o_ref[...] = jnp.zeros_like(o_ref)

        o_ref[...] += jnp.sum(x_ref[...].astype(F32), axis=0, keepdims=True)

    return pl.pallas_call(
        body, out_shape=jax.ShapeDtypeStruct((1, c), F32), grid=(n // tm,),
        in_specs=[pl.BlockSpec((tm, c), lambda i: (i, 0))], out_specs=pl.BlockSpec((1, c), lambda i: (0, 0)),
        compiler_params=_params("arbitrary"), name=name)(x)


def adamw(w, g, m, v, name="adamw"):
    rows, cols = w.shape
    tr = rows
    for cand in (512, 256, 128, 64, 32, 16, 8):
        if rows % cand == 0 and rows > cand:
            tr = cand
            break
    c1 = 1.0 / (1.0 - ADAM_B1 ** ADAM_STEP)
    c2 = 1.0 / (1.0 - ADAM_B2 ** ADAM_STEP)

    def body(w_ref, g_ref, m_ref, v_ref, d_ref, nm_ref, nv_ref):
        gv = g_ref[...]
        nm = ADAM_B1 * m_ref[...] + (1.0 - ADAM_B1) * gv
        nv = ADAM_B2 * v_ref[...] + (1.0 - ADAM_B2) * (gv * gv)
        nm_ref[...] = nm
        nv_ref[...] = nv
        d_ref[...] = -ADAM_LR * ((nm * c1) / (jnp.sqrt(nv * c2) + ADAM_EPS) + ADAM_WD * w_ref[...])

    spec = pl.BlockSpec((tr, cols), lambda i: (i, 0))
    shape = jax.ShapeDtypeStruct((rows, cols), F32)
    return pl.pallas_call(
        body, out_shape=(shape, shape, shape), grid=(rows // tr,),
        in_specs=[spec, spec, spec, spec], out_specs=(spec, spec, spec),
        compiler_params=_params("parallel"), name=name)(w, g, m, v)


def _place():
    return lax.axis_index("x"), lax.axis_index("y"), lax.axis_index("c")


def _other_chips(x, y):
    return [(1 - x, y), (x, 1 - y), (1 - x, 1 - y)]


def small_exchange(slab, reduce):
    r = slab.shape[0]

    def body(in_ref, o_ref, *scratch):
        if reduce:
            buf, send_sems, recv_sems = scratch
        else:
            buf = o_ref
            send_sems, recv_sems = scratch
        x, y, c = _place()
        me = 4 * x + 2 * y + c
        buf[me] = in_ref[...]
        copies = []
        for k in range(1, N_DEV):
            peer = (x ^ (k >> 2), y ^ ((k >> 1) & 1), c ^ (k & 1))
            cp = pltpu.make_async_remote_copy(
                src_ref=in_ref, dst_ref=buf.at[me], send_sem=send_sems.at[k - 1], recv_sem=recv_sems.at[k - 1],
                device_id=peer, device_id_type=MESH)
            cp.start()
            copies.append(cp)
        for k in range(1, N_DEV):
            src = 4 * (x ^ (k >> 2)) + 2 * (y ^ ((k >> 1) & 1)) + (c ^ (k & 1))
            pltpu.make_async_remote_copy(
                src_ref=in_ref, dst_ref=buf.at[src], send_sem=send_sems.at[k - 1], recv_sem=recv_sems.at[k - 1],
                device_id=(x, y, c), device_id_type=MESH).wait_recv()
        for cp in copies:
            cp.wait_send()
        if reduce:
            total = buf[0]
            for d in range(1, N_DEV):
                total = total + buf[d]
            o_ref[...] = total

    sems = [pltpu.SemaphoreType.DMA((N_DEV - 1,)), pltpu.SemaphoreType.DMA((N_DEV - 1,))]
    if reduce:
        out_shape = jax.ShapeDtypeStruct((r, LANES), F32)
        scratch = [pltpu.VMEM((N_DEV, r, LANES), F32)] + sems
    else:
        out_shape = jax.ShapeDtypeStruct((N_DEV, r, LANES), F32)
        scratch = sems
    vmem = pl.BlockSpec(memory_space=pltpu.VMEM)
    return pl.pallas_call(
        body, out_shape=out_shape, in_specs=[vmem], out_specs=vmem, scratch_shapes=scratch,
        compiler_params=pltpu.CompilerParams(vmem_limit_bytes=VMEM_LIMIT),
        name="small_reduce" if reduce else "small_gather")(slab)


def gather_weights(shards, name, collective_id):
    nw = len(shards)
    ns = [s.shape[0] for s in shards]
    in_refs = [jax.new_ref(s, memory_space=pltpu.MemorySpace.HBM) for s in shards]
    out_refs = [jax.empty_ref(jax.ShapeDtypeStruct((N_DEV * s.shape[0], s.shape[1]), s.dtype),
                              memory_space=pltpu.MemorySpace.HBM) for s in shards]

    @pl.kernel(mesh=plsc.ScalarSubcoreMesh(axis_name="seq", num_cores=1), name=name,
               scratch_types=(pltpu.SemaphoreType.DMA((nw, 7)), pltpu.SemaphoreType.DMA((nw, 7)),
                              pltpu.SemaphoreType.DMA((nw,))),
               compiler_params=pltpu.CompilerParams(collective_id=collective_id))
    def launch(send_sems, recv_sems, local_sems):
        x, y, c = _place()
        me, sib = (x, y, c), (x, y, 1 - c)
        chips = _other_chips(x, y)
        barrier = pltpu.get_barrier_semaphore()
        for peer in [sib] + [(*chip, c) for chip in chips]:
            pl.semaphore_signal(barrier, inc=1, device_id=peer, device_id_type=MESH)
        pl.semaphore_wait(barrier, 4)

        def rows(w, dev):
            return out_refs[w].at[pl.ds((4 * dev[0] + 2 * dev[1] + dev[2]) * ns[w], ns[w]), :]

        def copy(w, k, block, to, src=None):
            return pltpu.make_async_remote_copy(
                src_ref=rows(w, block) if src is None else src, dst_ref=rows(w, block),
                send_sem=send_sems.at[w, k], recv_sem=recv_sems.at[w, k], device_id=to, device_id_type=MESH)

        started, sends = [], []
        for w in range(nw):
            mine = pltpu.make_async_copy(in_refs[w], rows(w, me), local_sems.at[w])
            mine.start()
            started.append(mine)
            first = [copy(w, 0, me, sib, src=in_refs[w])]
            first += [copy(w, 1 + j, me, (*chip, c), src=in_refs[w]) for j, chip in enumerate(chips)]
            for cp in first:
                cp.start()
            sends += first
        for w in range(nw):
            for j, chip in enumerate(chips):
                copy(w, 1 + j, (*chip, c), me).wait_recv()
                fwd = copy(w, 4 + j, (*chip, c), sib)
                fwd.start()
                sends.append(fwd)
        for w in range(nw):
            copy(w, 0, sib, me).wait_recv()
            for j, chip in enumerate(chips):
                copy(w, 4 + j, (*chip, 1 - c), me).wait_recv()
        for cp in sends:
            cp.wait_send()
        for mine in started:
            mine.wait()

    launch()
    return [r[...] for r in out_refs]


def _sequencer_exchange(sources, out_rows, peers_of, copies_of, name, collective_id):
    nw = len(sources)
    in_refs = [jax.new_ref(s, memory_space=pltpu.MemorySpace.HBM) for s in sources]
    out_refs = [jax.empty_ref(jax.ShapeDtypeStruct((rows, s.shape[1]), s.dtype), memory_space=pltpu.MemorySpace.HBM)
                for rows, s in zip(out_rows, sources)]
    per = len(copies_of(0, 0, 0, 0))

    @pl.kernel(mesh=plsc.ScalarSubcoreMesh(axis_name="seq", num_cores=1), name=name,
               scratch_types=(pltpu.SemaphoreType.DMA((nw, per)), pltpu.SemaphoreType.DMA((nw, per))),
               compiler_params=pltpu.CompilerParams(collective_id=collective_id))
    def launch(send_sems, recv_sems):
        x, y, c = _place()
        peers = peers_of(x, y, c)
        barrier = pltpu.get_barrier_semaphore()
        for peer in peers:
            pl.semaphore_signal(barrier, inc=1, device_id=peer, device_id_type=MESH)
        pl.semaphore_wait(barrier, len(peers))
        copies = []
        for w in range(nw):
            for k, (src_blk, dst_blk, rows, peer) in enumerate(copies_of(x, y, c, w)):
                cp = pltpu.make_async_remote_copy(
                    src_ref=in_refs[w].at[pl.ds(src_blk * rows, rows), :],
                    dst_ref=out_refs[w].at[pl.ds(dst_blk * rows, rows), :],
                    send_sem=send_sems.at[w, k], recv_sem=recv_sems.at[w, k], device_id=peer, device_id_type=MESH)
                cp.start()
                copies.append(cp)
        for cp in copies:
            cp.wait_recv()
        for cp in copies:
            cp.wait_send()

    launch()
    return [r[...] for r in out_refs]


def scatter_to_sibling(grads, name, collective_id):
    ns = [g.shape[0] // N_DEV for g in grads]
    return _sequencer_exchange(
        grads, [4 * n for n in ns],
        lambda x, y, c: [(x, y, 1 - c)],
        lambda x, y, c, w: [(2 * q + 1 - c, q, ns[w], (x, y, 1 - c)) for q in range(4)],
        name, collective_id)


def scatter_to_chips(parts, name, collective_id):
    ns = [p.shape[0] // 4 for p in parts]
    return _sequencer_exchange(
        parts, [3 * n for n in ns],
        lambda x, y, c: [(*chip, c) for chip in _other_chips(x, y)],
        lambda x, y, c, w: [(2 * chip[0] + chip[1], j, ns[w], (*chip, c)) for j, chip in enumerate(_other_chips(x, y))],
        name, collective_id)


def add_sibling(grad, landed, core, name):
    n = landed.shape[0] // 4
    cols = grad.shape[1]

    def body(c_ref, g_ref, l_ref, o_ref):
        o_ref[...] = (g_ref[...].astype(F32) + l_ref[...].astype(F32)).astype(o_ref.dtype)

    grid_spec = pltpu.PrefetchScalarGridSpec(
        num_scalar_prefetch=1, grid=(4,),
        in_specs=[pl.BlockSpec((n, cols), lambda q, c_ref: (2 * q + c_ref[0], 0)),
                  pl.BlockSpec((n, cols), lambda q, c_ref: (q, 0))],
        out_specs=pl.BlockSpec((n, cols), lambda q, c_ref: (q, 0)))
    return pl.pallas_call(
        body, out_shape=jax.ShapeDtypeStruct(landed.shape, landed.dtype), grid_spec=grid_spec,
        compiler_params=_params("arbitrary"), name=name)(core, grad, landed)


def add_chips(part, landed, chip, name):
    n = landed.shape[0] // 3
    cols = part.shape[1]

    def body(q_ref, p_ref, l0_ref, l1_ref, l2_ref, o_ref):
        o_ref[...] = ((p_ref[...].astype(F32) + l0_ref[...].astype(F32)) + l1_ref[...].astype(F32)) \
            + l2_ref[...].astype(F32)

    def landed_spec(j):
        return pl.BlockSpec((n, cols), lambda i, q_ref: (j, 0))

    grid_spec = pltpu.PrefetchScalarGridSpec(
        num_scalar_prefetch=1, grid=(1,),
        in_specs=[pl.BlockSpec((n, cols), lambda i, q_ref: (q_ref[0], 0)), landed_spec(0), landed_spec(1), landed_spec(2)],
        out_specs=pl.BlockSpec((n, cols), lambda i, q_ref: (0, 0)))
    return pl.pallas_call(
        body, out_shape=jax.ShapeDtypeStruct((n, cols), F32), grid_spec=grid_spec,
        compiler_params=_params("arbitrary"), name=name)(chip, part, landed, landed, landed)


def _pack(arrays):
    flat = jnp.concatenate([a.reshape(-1).astype(F32) for a in arrays])
    pad = (-flat.shape[0]) % (8 * LANES)
    return jnp.pad(flat, (0, pad)).reshape(-1, LANES)


def _unpack(slab, shapes):
    flat = slab.reshape(slab.shape[:-2] + (-1,))
    out, off = [], 0
    for shp in shapes:
        size = 1
        for s in shp:
            size *= s
        out.append(flat[..., off:off + size].reshape(flat.shape[:-1] + tuple(shp)))
        off += size
    return out


def kernel(x, mem, norm1_g, mem_norm_g, a_w_in, a_q_g, a_k_g, a_rel_bias, b_w_in, b_b_in, b_conv_w, b_conv_b, b_ln_g, b_ln_b, mq_g, mk_g, w_mem_kv, w_out, norm2_g, w_gate, w_up, w_down, loss_target, m_norm1_g, m_mem_norm_g, m_a_w_in, m_a_q_g, m_a_k_g, m_a_rel_bias, m_b_w_in, m_b_b_in, m_b_conv_w, m_b_conv_b, m_b_ln_g, m_b_ln_b, m_mq_g, m_mk_g, m_w_mem_kv, m_w_out, m_norm2_g, m_w_gate, m_w_up, m_w_down, v_norm1_g, v_mem_norm_g, v_a_w_in, v_a_q_g, v_a_k_g, v_a_rel_bias, v_b_w_in, v_b_b_in, v_b_conv_w, v_b_conv_b, v_b_ln_g, v_b_ln_b, v_mq_g, v_mk_g, v_w_mem_kv, v_w_out, v_norm2_g, v_w_gate, v_w_up, v_w_down):
    batch, seq, d = x.shape
    mtok = mem.shape[1]
    n = batch * seq
    ax, ay, ac = _place()
    me = 4 * ax + 2 * ay + ac
    core_arr = jnp.reshape(ac, (1,)).astype(jnp.int32)
    chip_arr = jnp.reshape(2 * ax + ay, (1,)).astype(jnp.int32)

    def t_bf16(w):
        return jnp.transpose(w).astype(BF16)

    a_win_t, = gather_weights([t_bf16(a_w_in[0])], "gather_in_a", 1)
    wkv0, wo0, wg0, wu0, wd0 = gather_weights(
        [w_mem_kv[0].astype(BF16), w_out[0].astype(BF16), t_bf16(w_gate[0]), t_bf16(w_up[0]), w_down[0].astype(BF16)],
        "gather_layer_a", 2)
    b_win_t, wkv1, wo1 = gather_weights(
        [t_bf16(b_w_in[0]), w_mem_kv[1].astype(BF16), w_out[1].astype(BF16)], "gather_in_b", 3)
    wg1, wu1, wd1 = gather_weights([t_bf16(w_gate[1]), t_bf16(w_up[1]), w_down[1].astype(BF16)], "gather_ffn_b", 4)
    wg_t, wu_t, wd, wo, wkv = [wg0, wg1], [wu0, wu1], [wd0, wd1], [wo0, wo1], [wkv0, wkv1]

    f_loc = b_b_in.shape[1]
    c_loc = b_conv_b.shape[1]
    small_shapes = [(f_loc,), (CONV_W, c_loc), (c_loc,), (c_loc,), (c_loc,)]
    gathered = small_exchange(_pack([b_b_in, b_conv_w, b_conv_b, b_ln_g, b_ln_b]), reduce=False)
    bb_g, cw_g, cb_g, lg_g, lb_g = _unpack(gathered, small_shapes)
    bb_full = bb_g.reshape(1, -1)
    cw_full = jnp.pad(jnp.transpose(cw_g, (1, 0, 2)).reshape(CONV_W, -1), ((0, 32 - CONV_W), (0, 0)))
    cb_full, lg_full, lb_full = cb_g.reshape(1, -1), lg_g.reshape(1, -1), lb_g.reshape(1, -1)

    def two(g):
        return jnp.concatenate([g, g], axis=-1)

    gq2, gk2 = two(a_q_g), two(a_k_g)
    rel16 = jnp.pad(a_rel_bias[0], ((0, 16 - a_rel_bias.shape[1]), (0, 0)))
    bias = bias_blocks(rel16)

    x0 = x.reshape(n, d)
    mem2 = mem.reshape(batch * mtok, d)
    zero_mem = jnp.zeros_like(mem2)

    saved = []
    xin = x0
    for l in range(2):
        h = rms_fwd(xin, norm1_g[l:l + 1], name=f"rms1_fwd_{l}")
        mem_n = rms_fwd(mem2, mem_norm_g[l:l + 1], name=f"rms_mem_fwd_{l}")
        kv = mm_nn(mem_n, wkv[l], name=f"mem_kv_{l}")
        gq4 = jnp.tile(mq_g[l:l + 1], (1, 4))
        gk4 = jnp.tile(mk_g[l:l + 1], (1, 4))
        y_conv = None
        if l == 0:
            z = mm_nt(h, a_win_t, name="in_proj_a")
            cat = attn_fwd(z, gq2, gk2, bias, batch, seq)
            qcol = 3 * TOK_WIDTH // MEM_WIDTH
        else:
            z = mm_nt(h, b_win_t, bias=bb_full, name="in_proj_b")
            cat, y_conv = conv_fwd(z, cw_full, cb_full, lg_full, lb_full, batch, seq)
            qcol = 2 * TOK_WIDTH // MEM_WIDTH
        cat = memattn_fwd(z, kv, gq4, gk4, cat, batch, seq, qcol, name=f"memattn_fwd_{l}")
        x1 = mm_nn(cat, wo[l], res=xin, name=f"out_proj_{l}")
        h2 = rms_fwd(x1, norm2_g[l:l + 1], name=f"rms2_fwd_{l}")
        gate, up, act = gate_up(h2, wg_t[l], wu_t[l], name=f"gate_up_{l}")
        x2 = mm_nn(act, wd[l], res=x1, name=f"down_proj_{l}")
        saved.append(dict(xin=xin, h=h, mem_n=mem_n, kv=kv, gq4=gq4, gk4=gk4, z=z, qcol=qcol, cat=cat, x1=x1, h2=h2,
                          gate=gate, up=up, act=act, y_conv=y_conv))
        xin = x2

    dx, loss_blk = loss_head(xin, loss_target.reshape(n, d))
    loss = lax.psum(loss_blk[0, 0], ("x", "y", "c"))

    big = {}
    small = {}
    red = {}
    groups = 0

    def reduce_group(keys):
        nonlocal groups
        gid = groups
        groups += 1
        glist = [big[k] for k in keys]
        landed1 = scatter_to_sibling(glist, f"scatter_sibling_{gid}", 8 + 2 * gid)
        parts = [add_sibling(g, ld, core_arr, name=f"add_sibling_{k}") for k, g, ld in zip(keys, glist, landed1)]
        landed2 = scatter_to_chips(parts, f"scatter_chips_{gid}", 9 + 2 * gid)
        for k, p, ld in zip(keys, parts, landed2):
            red[k] = add_chips(p, ld, chip_arr, name=f"add_chips_{k}")

    for l in (1, 0):
        sv = saved[l]
        dgate, dup = ffn_bwd_act(dx, wd[l], sv["gate"], sv["up"], name=f"ffn_bwd_act_{l}")
        big[f"wd{l}"] = mm_tn(sv["act"], dx, name=f"grad_wd_{l}")
        dh2 = mm2_nn(dgate, wg_t[l], dup, wu_t[l], name=f"ffn_bwd_h_{l}")
        big[f"wg{l}"] = mm_tn(dgate, sv["h2"], name=f"grad_wg_{l}")
        big[f"wu{l}"] = mm_tn(dup, sv["h2"], name=f"grad_wu_{l}")
        dx1, small[f"norm2_{l}"] = rms_bwd(dh2, sv["x1"], norm2_g[l:l + 1], dx, name=f"rms2_bwd_{l}")
        dcat = mm_nt(dx1, wo[l], name=f"out_proj_bwd_{l}")
        big[f"wo{l}"] = mm_tn(sv["cat"], dx1, name=f"grad_wo_{l}")
        reduce_group([f"wd{l}", f"wg{l}", f"wu{l}", f"wo{l}"])
        if l == 0:
            dz, dbias, small["a_q"], small["a_k"] = attn_bwd(sv["z"], dcat, gq2, gk2, bias, batch, seq)
            small["rel"] = bias_grad(dbias)
            win_t = a_win_t
        else:
            dz, small["cw"], small["csum"] = conv_bwd(sv["z"], sv["y_conv"], dcat, cw_full, lg_full, lb_full, batch, seq)
            win_t = b_win_t
        dz, dkv, small[f"mq_{l}"], small[f"mk_{l}"] = memattn_bwd(
            sv["z"], sv["kv"], dcat, sv["gq4"], sv["gk4"], dz, batch, seq, sv["qcol"], name=f"memattn_bwd_{l}")
        if l == 1:
            small["bb"] = col_sum(dz, name="grad_b_in")
        big[f"win{l}"] = mm_tn(dz, sv["h"], name=f"grad_win_{l}")
        dh = mm_nn(dz, win_t, name=f"in_proj_bwd_{l}")
        big[f"wkv{l}"] = mm_tn(sv["mem_n"], dkv, name=f"grad_wkv_{l}")
        reduce_group([f"win{l}", f"wkv{l}"])
        dmem_n = mm_nt(dkv, wkv[l], out_dtype=F32, name=f"mem_kv_bwd_{l}")
        _, small[f"memnorm_{l}"] = rms_bwd(dmem_n, mem2, mem_norm_g[l:l + 1], zero_mem, name=f"rms_mem_bwd_{l}")
        dx, small[f"norm1_{l}"] = rms_bwd(dh, sv["xin"], norm1_g[l:l + 1], dx1, name=f"rms1_bwd_{l}")
    grad_x = dx.reshape(batch, seq, d)

    g_a_w_in = jnp.transpose(red["win0"])[None]
    g_b_w_in = jnp.transpose(red["win1"])[None]
    g_w_gate = jnp.stack([jnp.transpose(red["wg0"]), jnp.transpose(red["wg1"])])
    g_w_up = jnp.stack([jnp.transpose(red["wu0"]), jnp.transpose(red["wu1"])])
    g_w_down = jnp.stack([red["wd0"], red["wd1"]])
    g_w_out = jnp.stack([red["wo0"], red["wo1"]])
    g_w_mem_kv = jnp.stack([red["wkv0"], red["wkv1"]])

    def fold(v, groups):
        return jnp.sum(v.reshape(groups, HEAD_DIM), axis=0, keepdims=True)

    heads = a_rel_bias.shape[1]
    small_list = [
        jnp.concatenate([small["norm1_0"], small["norm1_1"]]),
        jnp.concatenate([small["memnorm_0"], small["memnorm_1"]]),
        fold(small["a_q"], 2), fold(small["a_k"], 2), small["rel"][:heads][None],
        small["bb"], small["cw"][:CONV_W][None], small["csum"][0:1], small["csum"][1:2], small["csum"][2:3],
        jnp.concatenate([fold(small["mq_0"], 4), fold(small["mq_1"], 4)]),
        jnp.concatenate([fold(small["mk_0"], 4), fold(small["mk_1"], 4)]),
        jnp.concatenate([small["norm2_0"], small["norm2_1"]]),
    ]
    small_full_shapes = [a.shape for a in small_list]
    summed = _unpack(small_exchange(_pack(small_list), reduce=True), small_full_shapes)
    (g_norm1, g_memnorm, g_aq, g_ak, g_rel, g_bb_full, g_cw_full, g_cb_full, g_lg_full, g_lb_full,
     g_mq, g_mk, g_norm2) = summed
    g_bb = lax.dynamic_slice_in_dim(g_bb_full, me * f_loc, f_loc, axis=1)
    g_cw = lax.dynamic_slice_in_dim(g_cw_full, me * c_loc, c_loc, axis=2)
    g_cb = lax.dynamic_slice_in_dim(g_cb_full, me * c_loc, c_loc, axis=1)
    g_lg = lax.dynamic_slice_in_dim(g_lg_full, me * c_loc, c_loc, axis=1)
    g_lb = lax.dynamic_slice_in_dim(g_lb_full, me * c_loc, c_loc, axis=1)

    grads = [g_norm1, g_memnorm, g_a_w_in, g_aq, g_ak, g_rel, g_b_w_in, g_bb, g_cw, g_cb, g_lg, g_lb,
             g_mq, g_mk, g_w_mem_kv, g_w_out, g_norm2, g_w_gate, g_w_up, g_w_down]
    weights = [norm1_g, mem_norm_g, a_w_in, a_q_g, a_k_g, a_rel_bias, b_w_in, b_b_in, b_conv_w, b_conv_b, b_ln_g,
               b_ln_b, mq_g, mk_g, w_mem_kv, w_out, norm2_g, w_gate, w_up, w_down]
    moms = [m_norm1_g, m_mem_norm_g, m_a_w_in, m_a_q_g, m_a_k_g, m_a_rel_bias, m_b_w_in, m_b_b_in, m_b_conv_w,
            m_b_conv_b, m_b_ln_g, m_b_ln_b, m_mq_g, m_mk_g, m_w_mem_kv, m_w_out, m_norm2_g, m_w_gate, m_w_up, m_w_down]
    vels = [v_norm1_g, v_mem_norm_g, v_a_w_in, v_a_q_g, v_a_k_g, v_a_rel_bias, v_b_w_in, v_b_b_in, v_b_conv_w,
            v_b_conv_b, v_b_ln_g, v_b_ln_b, v_mq_g, v_mk_g, v_w_mem_kv, v_w_out, v_norm2_g, v_w_gate, v_w_up, v_w_down]

    large = {2, 6, 14, 15, 17, 18, 19}
    deltas, new_m, new_v = [None] * 20, [None] * 20, [None] * 20
    for i in sorted(large):
        shp = weights[i].shape
        flat = (shp[0] * shp[1], shp[2])
        dl, nm, nv = adamw(weights[i].reshape(flat), grads[i].reshape(flat), moms[i].reshape(flat),
                           vels[i].reshape(flat), name=f"adamw_{i}")
        deltas[i], new_m[i], new_v[i] = dl.reshape(shp), nm.reshape(shp), nv.reshape(shp)
    small_idx = [i for i in range(20) if i not in large]
    small_shapes2 = [weights[i].shape for i in small_idx]
    dl, nm, nv = adamw(_pack([weights[i] for i in small_idx]), _pack([grads[i] for i in small_idx]),
                       _pack([moms[i] for i in small_idx]), _pack([vels[i] for i in small_idx]), name="adamw_small")
    for i, a, b, cc in zip(small_idx, _unpack(dl, small_shapes2), _unpack(nm, small_shapes2), _unpack(nv, small_shapes2)):
        deltas[i], new_m[i], new_v[i] = a, b, cc

    return (loss, grad_x, *grads, *deltas, *new_m, *new_v)
```

```python
import functools

import jax
import jax.numpy as jnp
from jax import lax
from jax.experimental import pallas as pl
from jax.experimental.pallas import tpu as pltpu
from jax.experimental.pallas import tpu_sc as plsc

F32 = jnp.float32
BF16 = jnp.bfloat16
HIGHEST = lax.Precision.HIGHEST
MESH = pl.DeviceIdType.MESH
ANY = pl.BlockSpec(memory_space=pl.ANY)

N_DEV = 8
D_MODEL = 1024
HEAD_DIM = 64
TOK_WIDTH = 768
MEM_WIDTH = 256
CHUNK = 64
Q_BLOCK = 256
KEY_WIN = 768
BAND = 576
N_REL = 192
CONV_W = 31
CONV_HALO = 32
NORM_EPS = 1e-6
NEG_INF = -1e30
ATTN_SCALE = HEAD_DIM ** -0.5
LANES = 128
ROW_TILE = 512
VMEM_LIMIT = 56 * 1024 * 1024

ADAM_LR, ADAM_B1, ADAM_B2, ADAM_EPS, ADAM_WD, ADAM_STEP = 0.001, 0.9, 0.999, 1e-08, 0.01, 10


def _params(*sem):
    return pltpu.CompilerParams(dimension_semantics=sem, vmem_limit_bytes=VMEM_LIMIT)


def _row_tile(m):
    return ROW_TILE if m % ROW_TILE == 0 else m


def _col_tile(n, cap=1408):
    best = None
    for t in range(LANES, min(n, cap) + 1, LANES):
        if n % t == 0:
            best = t
    return best if best is not None else n


def _dot(a, b, ca, cb):
    return lax.dot_general(a, b, (((ca,), (cb,)), ((), ())), preferred_element_type=F32)


def _sigmoid(x):
    return 0.5 * jnp.tanh(0.5 * x) + 0.5


def mm_nt(a, b, bias=None, out_dtype=BF16, name="mm_nt"):
    m, k = a.shape
    n = b.shape[0]
    tm, tn = _row_tile(m), _col_tile(n)

    def body(*refs):
        a_ref, b_ref = refs[0], refs[1]
        o_ref = refs[-1]
        acc = _dot(a_ref[...].astype(BF16), b_ref[...].astype(BF16), 1, 1)
        if bias is not None:
            acc = acc + refs[2][...]
        o_ref[...] = acc.astype(o_ref.dtype)

    in_specs = [pl.BlockSpec((tm, k), lambda j, i: (i, 0)), pl.BlockSpec((tn, k), lambda j, i: (j, 0))]
    args = [a, b]
    if bias is not None:
        in_specs.append(pl.BlockSpec((1, tn), lambda j, i: (0, j)))
        args.append(bias)
    return pl.pallas_call(
        body, out_shape=jax.ShapeDtypeStruct((m, n), out_dtype), grid=(n // tn, m // tm),
        in_specs=in_specs, out_specs=pl.BlockSpec((tm, tn), lambda j, i: (i, j)),
        compiler_params=_params("parallel", "arbitrary"), name=name)(*args)


def mm_nn(a, b, res=None, out_dtype=F32, name="mm_nn"):
    m, k = a.shape
    n = b.shape[1]
    tm, tn = _row_tile(m), _col_tile(n, 1024)

    def body(*refs):
        a_ref, b_ref = refs[0], refs[1]
        o_ref = refs[-1]
        acc = _dot(a_ref[...].astype(BF16), b_ref[...].astype(BF16), 1, 0)
        if res is not None:
            acc = acc + refs[2][...]
        o_ref[...] = acc.astype(o_ref.dtype)

    in_specs = [pl.BlockSpec((tm, k), lambda j, i: (i, 0)), pl.BlockSpec((k, tn), lambda j, i: (0, j))]
    args = [a, b]
    if res is not None:
        in_specs.append(pl.BlockSpec((tm, tn), lambda j, i: (i, j)))
        args.append(res)
    return pl.pallas_call(
        body, out_shape=jax.ShapeDtypeStruct((m, n), out_dtype), grid=(n // tn, m // tm),
        in_specs=in_specs, out_specs=pl.BlockSpec((tm, tn), lambda j, i: (i, j)),
        compiler_params=_params("parallel", "arbitrary"), name=name)(*args)


def mm2_nn(a1, b1, a2, b2, name="mm2_nn"):
    m, k = a1.shape
    n = b1.shape[1]
    tm = _row_tile(m)

    def body(a1_ref, b1_ref, a2_ref, b2_ref, o_ref):
        o_ref[...] = _dot(a1_ref[...], b1_ref[...], 1, 0) + _dot(a2_ref[...], b2_ref[...], 1, 0)

    a_spec = pl.BlockSpec((tm, k), lambda i: (i, 0))
    b_spec = pl.BlockSpec((k, n), lambda i: (0, 0))
    return pl.pallas_call(
        body, out_shape=jax.ShapeDtypeStruct((m, n), F32), grid=(m // tm,),
        in_specs=[a_spec, b_spec, a_spec, b_spec], out_specs=pl.BlockSpec((tm, n), lambda i: (i, 0)),
        compiler_params=_params("parallel"), name=name)(a1, b1, a2, b2)


def mm_tn(a, b, out_dtype=BF16, name="mm_tn"):
    t, r = a.shape
    c = b.shape[1]
    tr = _col_tile(r, 512)

    def body(a_ref, b_ref, o_ref):
        o_ref[...] = _dot(a_ref[...].astype(BF16), b_ref[...].astype(BF16), 0, 0).astype(o_ref.dtype)

    return pl.pallas_call(
        body, out_shape=jax.ShapeDtypeStruct((r, c), out_dtype), grid=(r // tr,),
        in_specs=[pl.BlockSpec((t, tr), lambda i: (0, i)), pl.BlockSpec((t, c), lambda i: (0, 0))],
        out_specs=pl.BlockSpec((tr, c), lambda i: (i, 0)),
        compiler_params=_params("parallel"), name=name)(a, b)


def rms_fwd(x, g, name="rms_fwd"):
    n, d = x.shape
    tm = _row_tile(n)

    def body(x_ref, g_ref, o_ref):
        xv = x_ref[...]
        r = lax.rsqrt(jnp.mean(xv * xv, axis=-1, keepdims=True) + NORM_EPS)
        o_ref[...] = (xv * r * g_ref[...]).astype(o_ref.dtype)

    return pl.pallas_call(
        body, out_shape=jax.ShapeDtypeStruct((n, d), BF16), grid=(n // tm,),
        in_specs=[pl.BlockSpec((tm, d), lambda i: (i, 0)), pl.BlockSpec((1, d), lambda i: (0, 0))],
        out_specs=pl.BlockSpec((tm, d), lambda i: (i, 0)),
        compiler_params=_params("parallel"), name=name)(x, g)


def rms_bwd(dh, x, g, dres, name="rms_bwd"):
    n, d = x.shape
    tm = _row_tile(n)

    def body(dh_ref, x_ref, g_ref, dres_ref, dx_ref, dxb_ref, dg_ref):
        @pl.when(pl.program_id(0) == 0)
        def _():
            dg_ref[...] = jnp.zeros_like(dg_ref)

        xv = x_ref[...]
        dhv = dh_ref[...].astype(F32)
        r = lax.rsqrt(jnp.mean(xv * xv, axis=-1, keepdims=True) + NORM_EPS)
        xhat = xv * r
        dg_ref[...] += jnp.sum(dhv * xhat, axis=0, keepdims=True)
        dxhat = dhv * g_ref[...]
        mean_t = jnp.mean(dxhat * xhat, axis=-1, keepdims=True)
        dx = dres_ref[...] + r * (dxhat - xhat * mean_t)
        dx_ref[...] = dx
        dxb_ref[...] = dx.astype(BF16)

    row = pl.BlockSpec((tm, d), lambda i: (i, 0))
    vec = pl.BlockSpec((1, d), lambda i: (0, 0))
    return pl.pallas_call(
        body, out_shape=(jax.ShapeDtypeStruct((n, d), F32), jax.ShapeDtypeStruct((n, d), BF16),
                         jax.ShapeDtypeStruct((1, d), F32)), grid=(n // tm,),
        in_specs=[row, row, vec, row], out_specs=(row, row, vec),
        compiler_params=_params("arbitrary"), name=name)(dh, x, g, dres)


def gate_up(h2, wg_t, wu_t, name="gate_up"):
    n, d = h2.shape
    f = wg_t.shape[0]
    tm, tn = _row_tile(n), _col_tile(f)

    def body(h_ref, wg_ref, wu_ref, g_ref, u_ref, a_ref):
        hv = h_ref[...]
        gv = _dot(hv, wg_ref[...], 1, 1)
        uv = _dot(hv, wu_ref[...], 1, 1)
        g_ref[...] = gv.astype(BF16)
        u_ref[...] = uv.astype(BF16)
        a_ref[...] = (gv * _sigmoid(gv) * uv).astype(BF16)

    w_spec = pl.BlockSpec((tn, d), lambda j, i: (j, 0))
    o_spec = pl.BlockSpec((tm, tn), lambda j, i: (i, j))
    o_shape = jax.ShapeDtypeStruct((n, f), BF16)
    return pl.pallas_call(
        body, out_shape=(o_shape, o_shape, o_shape), grid=(f // tn, n // tm),
        in_specs=[pl.BlockSpec((tm, d), lambda j, i: (i, 0)), w_spec, w_spec], out_specs=(o_spec, o_spec, o_spec),
        compiler_params=_params("parallel", "arbitrary"), name=name)(h2, wg_t, wu_t)


def ffn_bwd_act(dx, wd, gate, up, name="ffn_bwd_act"):
    n, d = dx.shape
    f = wd.shape[0]
    tm, tn = _row_tile(n), _col_tile(f)

    def body(dx_ref, wd_ref, g_ref, u_ref, dg_ref, du_ref):
        dact = _dot(dx_ref[...].astype(BF16), wd_ref[...], 1, 1)
        gv = g_ref[...].astype(F32)
        uv = u_ref[...].astype(F32)
        sg = _sigmoid(gv)
        dg_ref[...] = (dact * uv * sg * (1.0 + gv * (1.0 - sg))).astype(BF16)
        du_ref[...] = (dact * gv * sg).astype(BF16)

    t_spec = pl.BlockSpec((tm, tn), lambda j, i: (i, j))
    o_shape = jax.ShapeDtypeStruct((n, f), BF16)
    return pl.pallas_call(
        body, out_shape=(o_shape, o_shape), grid=(f // tn, n // tm),
        in_specs=[pl.BlockSpec((tm, d), lambda j, i: (i, 0)), pl.BlockSpec((tn, d), lambda j, i: (j, 0)), t_spec, t_spec],
        out_specs=(t_spec, t_spec),
        compiler_params=_params("parallel", "arbitrary"), name=name)(dx, wd, gate, up)


def _group_masks(width):
    lane = lax.broadcasted_iota(jnp.int32, (1, width), 1)
    return [(lane >= HEAD_DIM * g) & (lane < HEAD_DIM * (g + 1)) for g in range(width // HEAD_DIM)]


def _group_sum(x, masks):
    out = jnp.zeros_like(x)
    for msk in masks:
        s = jnp.sum(jnp.where(msk, x, 0.0), axis=-1, keepdims=True)
        out = jnp.where(msk, s, out)
    return out


def _head_norm(x, gain, masks):
    r = lax.rsqrt(_group_sum(x * x, masks) * (1.0 / HEAD_DIM) + NORM_EPS)
    xhat = x * r
    return xhat * gain, xhat, r


def _head_norm_bwd(dxn, xhat, r, gain, masks):
    dgain = jnp.sum(dxn * xhat, axis=0, keepdims=True)
    dxhat = dxn * gain
    mean_t = _group_sum(dxhat * xhat, masks) * (1.0 / HEAD_DIM)
    return r * (dxhat - xhat * mean_t), dgain


def _softmax_rows(s):
    e = jnp.exp(s - jnp.max(s, axis=-1, keepdims=True))
    return e * (1.0 / jnp.sum(e, axis=-1, keepdims=True))


def _rel_onehot():
    col = lax.broadcasted_iota(jnp.int32, (1, KEY_WIN), 1)
    off = jnp.where(col < KEY_WIN - LANES, col, col - KEY_WIN)
    idx = jnp.clip(8 * CHUNK - off, -(CHUNK - 1), LANES) + (CHUNK - 1)
    return (lax.broadcasted_iota(jnp.int32, (N_REL, KEY_WIN), 0) == idx).astype(F32)


def bias_blocks(rel16):
    heads = TOK_WIDTH // HEAD_DIM

    def body(rel_ref, o_ref, u_ref):
        u_ref[...] = jnp.dot(rel_ref[...], _rel_onehot(), precision=HIGHEST, preferred_element_type=F32)
        row = lax.broadcasted_iota(jnp.int32, (CHUNK, KEY_WIN), 0)
        col = lax.broadcasted_iota(jnp.int32, (CHUNK, KEY_WIN), 1)
        for h in range(heads):
            xv = jnp.broadcast_to(u_ref[h:h + 1, :], (CHUNK, KEY_WIN))
            for b in range(6):
                xv = jnp.where(((row >> b) & 1) == 1, pltpu.roll(xv, 1 << b, axis=1), xv)
            xv = jnp.where(col < BAND, xv, NEG_INF)
            for i in range(Q_BLOCK // CHUNK):
                o_ref[h, CHUNK * i:CHUNK * (i + 1), :] = pltpu.roll(xv, CHUNK * i, axis=1) if i else xv

    return pl.pallas_call(
        body, out_shape=jax.ShapeDtypeStruct((heads, Q_BLOCK, KEY_WIN), F32),
        scratch_shapes=[pltpu.VMEM((16, KEY_WIN), F32)], name="bias_blocks")(rel16)


def bias_grad(dbias):
    heads = dbias.shape[0]

    def body(db_ref, o_ref, y_ref):
        y_ref[...] = jnp.zeros_like(y_ref)
        row = lax.broadcasted_iota(jnp.int32, (CHUNK, KEY_WIN), 0)
        for h in range(heads):
            fv = db_ref[h, 0:CHUNK, :]
            for i in range(1, Q_BLOCK // CHUNK):
                fv = fv + pltpu.roll(db_ref[h, CHUNK * i:CHUNK * (i + 1), :], KEY_WIN - CHUNK * i, axis=1)
            for b in range(6):
                fv = jnp.where(((row >> b) & 1) == 1, pltpu.roll(fv, KEY_WIN - (1 << b), axis=1), fv)
            y_ref[h:h + 1, :] = jnp.sum(fv, axis=0, keepdims=True)
        o_ref[...] = lax.dot_general(y_ref[...], _rel_onehot(), (((1,), (1,)), ((), ())),
                                     precision=HIGHEST, preferred_element_type=F32)

    return pl.pallas_call(
        body, out_shape=jax.ShapeDtypeStruct((16, N_REL), F32),
        scratch_shapes=[pltpu.VMEM((16, KEY_WIN), F32)], name="bias_grad")(dbias)


def _attn_windows(seq):
    out = []
    for j in range(seq // Q_BLOCK):
        r0 = j * Q_BLOCK
        k0 = max(0, r0 - 8 * CHUNK)
        width = r0 + Q_BLOCK - k0
        out.append((r0, k0, width, KEY_WIN - width))
    return out


def attn_fwd(z, gq2, gk2, bias, batch, seq):
    n = z.shape[0]
    pairs = TOK_WIDTH // LANES

    def body(q_ref, k_ref, v_ref, gq_ref, gk_ref, b_ref, o_ref, qs_s, kn_s):
        masks = _group_masks(LANES)
        qs_s[...] = (_head_norm(q_ref[...].astype(F32), gq_ref[...], masks)[0] * ATTN_SCALE).astype(BF16)
        kn_s[...] = _head_norm(k_ref[...].astype(F32), gk_ref[...], masks)[0].astype(BF16)
        for r0, k0, width, c0 in _attn_windows(seq):
            qb = qs_s[r0:r0 + Q_BLOCK, :]
            kw = kn_s[k0:k0 + width, :]
            vw = v_ref[k0:k0 + width, :]
            out = jnp.zeros((Q_BLOCK, LANES), F32)
            for h, msk in enumerate(masks):
                qh = jnp.where(msk, qb, jnp.zeros_like(qb))
                s = _dot(qh, kw, 1, 1) + b_ref[h, :, c0:KEY_WIN]
                p = _softmax_rows(s).astype(BF16)
                out = jnp.where(msk, _dot(p, vw, 1, 0), out)
            o_ref[r0:r0 + Q_BLOCK, :] = out.astype(o_ref.dtype)

    def col(off):
        return pl.BlockSpec((seq, LANES), lambda b, p: (b, off + p))

    vec = pl.BlockSpec((1, LANES), lambda b, p: (0, 0))
    return pl.pallas_call(
        body, out_shape=jax.ShapeDtypeStruct((n, D_MODEL), BF16), grid=(batch, pairs),
        in_specs=[col(0), col(pairs), col(2 * pairs), vec, vec,
                  pl.BlockSpec((2, Q_BLOCK, KEY_WIN), lambda b, p: (p, 0, 0))],
        out_specs=pl.BlockSpec((seq, LANES), lambda b, p: (b, p)),
        scratch_shapes=[pltpu.VMEM((seq, LANES), BF16), pltpu.VMEM((seq, LANES), BF16)],
        compiler_params=_params("parallel", "arbitrary"), name="attn_fwd")(z, z, z, gq2, gk2, bias)


def attn_bwd(z, dcat, gq2, gk2, bias, batch, seq):
    n = z.shape[0]
    pairs = TOK_WIDTH // LANES

    def body(q_ref, k_ref, v_ref, do_ref, gq_ref, gk_ref, b_ref,
             dz_ref, db_ref, dgq_ref, dgk_ref, qs_s, kn_s, dqn_s, dkn_s, dv_s, dk_o, dv_o):
        pi, bi, which = pl.program_id(0), pl.program_id(1), pl.program_id(2)

        @pl.when(which == 0)
        def _():
            masks = _group_masks(LANES)

            @pl.when(bi == 0)
            def _():
                db_ref[...] = jnp.zeros_like(db_ref)

            @pl.when((bi == 0) & (pi == 0))
            def _():
                dgq_ref[...] = jnp.zeros_like(dgq_ref)
                dgk_ref[...] = jnp.zeros_like(dgk_ref)

            qn, qhat, rq = _head_norm(q_ref[...].astype(F32), gq_ref[...], masks)
            kn, khat, rk = _head_norm(k_ref[...].astype(F32), gk_ref[...], masks)
            qs_s[...] = (qn * ATTN_SCALE).astype(BF16)
            kn_s[...] = kn.astype(BF16)
            dkn_s[...] = jnp.zeros_like(dkn_s)
            dv_s[...] = jnp.zeros_like(dv_s)
            for r0, k0, width, c0 in _attn_windows(seq):
                qb = qs_s[r0:r0 + Q_BLOCK, :]
                dob = do_ref[r0:r0 + Q_BLOCK, :]
                kw = kn_s[k0:k0 + width, :]
                vw = v_ref[k0:k0 + width, :]
                dq_acc = jnp.zeros((Q_BLOCK, LANES), F32)
                dk_acc = jnp.zeros((width, LANES), F32)
                dv_acc = jnp.zeros((width, LANES), F32)
                for h, msk in enumerate(masks):
                    qh = jnp.where(msk, qb, jnp.zeros_like(qb))
                    doh = jnp.where(msk, dob, jnp.zeros_like(dob))
                    p = _softmax_rows(_dot(qh, kw, 1, 1) + b_ref[h, :, c0:KEY_WIN])
                    dp = _dot(doh, vw, 1, 1)
                    ds = p * (dp - jnp.sum(p * dp, axis=-1, keepdims=True))
                    db_ref[h, :, c0:KEY_WIN] += ds
                    dsb = ds.astype(BF16)
                    dq_acc = jnp.where(msk, _dot(dsb, kw, 1, 0), dq_acc)
                    dk_acc = jnp.where(msk, _dot(dsb, qb, 0, 0), dk_acc)
                    dv_acc = jnp.where(msk, _dot(p.astype(BF16), dob, 0, 0), dv_acc)
                dqn_s[r0:r0 + Q_BLOCK, :] = dq_acc * ATTN_SCALE
                dkn_s[k0:k0 + width, :] += dk_acc
                dv_s[k0:k0 + width, :] += dv_acc
            dq, dgq = _head_norm_bwd(dqn_s[...], qhat, rq, gq_ref[...], masks)
            dk, dgk = _head_norm_bwd(dkn_s[...], khat, rk, gk_ref[...], masks)
            dz_ref[...] = dq.astype(dz_ref.dtype)
            dk_o[...] = dk.astype(dk_o.dtype)
            dv_o[...] = dv_s[...].astype(dv_o.dtype)
            dgq_ref[...] += dgq
            dgk_ref[...] += dgk

        @pl.when(which == 1)
        def _():
            dz_ref[...] = dk_o[...]

        @pl.when(which == 2)
        def _():
            dz_ref[...] = dv_o[...]

    def ahead(p, b, t):
        nb = b + jnp.where(t > 0, 1, 0)
        wrap = jnp.where(nb >= batch, 1, 0)
        return jnp.minimum(p + wrap, pairs - 1), nb - wrap * batch

    def col(off):
        def index(p, b, t):
            np_, nb = ahead(p, b, t)
            return nb, off + np_
        return pl.BlockSpec((seq, LANES), index)

    vec = pl.BlockSpec((1, LANES), lambda p, b, t: (0, 0))
    blk = pl.BlockSpec((2, Q_BLOCK, KEY_WIN), lambda p, b, t: (p, 0, 0))
    blk_in = pl.BlockSpec((2, Q_BLOCK, KEY_WIN), lambda p, b, t: (ahead(p, b, t)[0], 0, 0))
    v_shape = jax.ShapeDtypeStruct((1, LANES), F32)
    return pl.pallas_call(
        body,
        out_shape=(jax.ShapeDtypeStruct(z.shape, BF16), jax.ShapeDtypeStruct(bias.shape, F32), v_shape, v_shape),
        grid=(pairs, batch, 3),
        in_specs=[col(0), col(pairs), col(2 * pairs), col(0), vec, vec, blk_in],
        out_specs=(pl.BlockSpec((seq, LANES), lambda p, b, t: (b, t * pairs + p)), blk, vec, vec),
        scratch_shapes=[pltpu.VMEM((seq, LANES), BF16), pltpu.VMEM((seq, LANES), BF16),
                        pltpu.VMEM((seq, LANES), F32), pltpu.VMEM((seq, LANES), F32), pltpu.VMEM((seq, LANES), F32),
                        pltpu.VMEM((seq, LANES), BF16), pltpu.VMEM((seq, LANES), BF16)],
        compiler_params=_params("arbitrary", "arbitrary", "arbitrary"), name="attn_bwd")(
            z, z, z, dcat, gq2, gk2, bias)


MEM_ROWS = 512


def memattn_fwd(z, kv, gq4, gk4, cat, batch, seq, qcol, name):
    mtok = kv.shape[0] // batch
    rows = min(MEM_ROWS, seq)

    def body(q_ref, kv_ref, gq_ref, gk_ref, cat_ref, o_ref):
        del cat_ref
        masks = _group_masks(MEM_WIDTH)
        kn = _head_norm(kv_ref[:, 0:MEM_WIDTH], gk_ref[...], masks)[0].astype(BF16)
        vm = kv_ref[:, MEM_WIDTH:2 * MEM_WIDTH].astype(BF16)
        for t in range(seq // rows):
            sl = slice(t * rows, (t + 1) * rows)
            qs = (_head_norm(q_ref[sl, :].astype(F32), gq_ref[...], masks)[0] * ATTN_SCALE).astype(BF16)
            out = jnp.zeros((rows, MEM_WIDTH), F32)
            for msk in masks:
                qh = jnp.where(msk, qs, jnp.zeros_like(qs))
                p = _softmax_rows(_dot(qh, kn, 1, 1)).astype(BF16)
                out = jnp.where(msk, _dot(p, vm, 1, 0), out)
            o_ref[sl, :] = out.astype(o_ref.dtype)

    vec = pl.BlockSpec((1, MEM_WIDTH), lambda b: (0, 0))
    return pl.pallas_call(
        body, out_shape=jax.ShapeDtypeStruct(cat.shape, cat.dtype), grid=(batch,),
        in_specs=[pl.BlockSpec((seq, MEM_WIDTH), lambda b: (b, qcol)),
                  pl.BlockSpec((mtok, 2 * MEM_WIDTH), lambda b: (b, 0)), vec, vec, ANY],
        out_specs=pl.BlockSpec((seq, MEM_WIDTH), lambda b: (b, TOK_WIDTH // MEM_WIDTH)),
        input_output_aliases={4: 0},
        compiler_params=_params("parallel"), name=name)(z, kv, gq4, gk4, cat)


def memattn_bwd(z, kv, dcat, gq4, gk4, dz, batch, seq, qcol, name):
    mtok = kv.shape[0] // batch
    rows = min(MEM_ROWS, seq)

    def body(q_ref, kv_ref, do_ref, gq_ref, gk_ref, dz_in_ref, dq_ref, dkv_ref, dgq_ref, dgk_ref):
        del dz_in_ref
        @pl.when(pl.program_id(0) == 0)
        def _():
            dgq_ref[...] = jnp.zeros_like(dgq_ref)
            dgk_ref[...] = jnp.zeros_like(dgk_ref)

        masks = _group_masks(MEM_WIDTH)
        kn_f, khat, rk = _head_norm(kv_ref[:, 0:MEM_WIDTH], gk_ref[...], masks)
        kn = kn_f.astype(BF16)
        vm = kv_ref[:, MEM_WIDTH:2 * MEM_WIDTH].astype(BF16)
        dkn = jnp.zeros((mtok, MEM_WIDTH), F32)
        dvm = jnp.zeros((mtok, MEM_WIDTH), F32)
        dgq = jnp.zeros((1, MEM_WIDTH), F32)
        for t in range(seq // rows):
            sl = slice(t * rows, (t + 1) * rows)
            qn_f, qhat, rq = _head_norm(q_ref[sl, :].astype(F32), gq_ref[...], masks)
            qs = (qn_f * ATTN_SCALE).astype(BF16)
            dob = do_ref[sl, :]
            dqn = jnp.zeros((rows, MEM_WIDTH), F32)
            for msk in masks:
                qh = jnp.where(msk, qs, jnp.zeros_like(qs))
                doh = jnp.where(msk, dob, jnp.zeros_like(dob))
                p = _softmax_rows(_dot(qh, kn, 1, 1))
                dp = _dot(doh, vm, 1, 1)
                ds = p * (dp - jnp.sum(p * dp, axis=-1, keepdims=True))
                dsb = ds.astype(BF16)
                dqn = jnp.where(msk, _dot(dsb, kn, 1, 0), dqn)
                dkn = dkn + jnp.where(msk, _dot(dsb, qs, 0, 0), 0.0)
                dvm = dvm + jnp.where(msk, _dot(p.astype(BF16), dob, 0, 0), 0.0)
            dq, dg = _head_norm_bwd(dqn * ATTN_SCALE, qhat, rq, gq_ref[...], masks)
            dq_ref[sl, :] = dq.astype(dq_ref.dtype)
            dgq = dgq + dg
        dk, dgk = _head_norm_bwd(dkn, khat, rk, gk_ref[...], masks)
        dkv_ref[:, 0:MEM_WIDTH] = dk
        dkv_ref[:, MEM_WIDTH:2 * MEM_WIDTH] = dvm
        dgq_ref[...] += dgq
        dgk_ref[...] += dgk

    vec = pl.BlockSpec((1, MEM_WIDTH), lambda b: (0, 0))
    kv_spec = pl.BlockSpec((mtok, 2 * MEM_WIDTH), lambda b: (b, 0))
    v_shape = jax.ShapeDtypeStruct((1, MEM_WIDTH), F32)
    q_spec = pl.BlockSpec((seq, MEM_WIDTH), lambda b: (b, qcol))
    return pl.pallas_call(
        body,
        out_shape=(jax.ShapeDtypeStruct(dz.shape, dz.dtype), jax.ShapeDtypeStruct(kv.shape, F32), v_shape, v_shape),
        grid=(batch,),
        in_specs=[q_spec, kv_spec, pl.BlockSpec((seq, MEM_WIDTH), lambda b: (b, TOK_WIDTH // MEM_WIDTH)), vec, vec, ANY],
        out_specs=(q_spec, kv_spec, vec, vec),
        input_output_aliases={5: 0},
        compiler_params=_params("arbitrary"), name=name)(z, kv, dcat, gq4, gk4, dz)


CONV_ROWS = 256


def _glu(a_ref, g_ref):
    return a_ref[...].astype(F32) * _sigmoid(g_ref[...].astype(F32))


def _layer_norm_stats(y):
    mu = jnp.mean(y, axis=-1, keepdims=True)
    yc = y - mu
    rstd = lax.rsqrt(jnp.mean(yc * yc, axis=-1, keepdims=True) + NORM_EPS)
    return yc * rstd, rstd


CONV_WIN = CONV_HALO + CONV_ROWS
SUBLANES = 8
SHIFT_ROWS = CONV_WIN - SUBLANES


def _preshift(win, shifted):
    for s in range(1, SUBLANES):
        shifted[s - 1, :, :] = win[s:s + SHIFT_ROWS, :]


TAP_ROWS = 64
TAP_TILES = [(r0, slice(c0, c0 + LANES)) for c0 in range(0, TOK_WIDTH, LANES) for r0 in range(0, CONV_ROWS, TAP_ROWS)]


def _tap(win, shifted, off, r0, lanes):
    s = off % SUBLANES
    base = off - s + r0
    if s == 0:
        return win[base:base + TAP_ROWS, lanes]
    return shifted[s - 1, base:base + TAP_ROWS, lanes]


def _fold_rows(x):
    return jnp.sum(x.reshape(TAP_ROWS // SUBLANES, SUBLANES, LANES), axis=0)


def conv_fwd(z, cw, cb, lg, lb, batch, seq):
    n = z.shape[0]
    nt = seq // CONV_ROWS
    sub = CONV_ROWS // CONV_HALO
    lead = CONV_HALO - (CONV_W - 1)

    def body(a_ref, g_ref, ap_ref, gp_ref, cw_ref, cb_ref, lg_ref, lb_ref, o_ref, y_ref, win, shifted):
        first = pl.program_id(1) == 0
        win[0:CONV_HALO, :] = jnp.where(first, 0.0, _glu(ap_ref, gp_ref))
        win[CONV_HALO:CONV_WIN, :] = _glu(a_ref, g_ref)
        _preshift(win, shifted)
        for r0, lanes in TAP_TILES:
            acc = jnp.zeros((TAP_ROWS, LANES), F32) + cb_ref[:, lanes]
            for w in range(CONV_W):
                acc = acc + _tap(win, shifted, lead + w, r0, lanes) * cw_ref[w:w + 1, lanes]
            y_ref[r0:r0 + TAP_ROWS, lanes] = acc
        yh, _ = _layer_norm_stats(y_ref[...])
        t = yh * lg_ref[...] + lb_ref[...]
        o_ref[...] = (t * _sigmoid(t)).astype(o_ref.dtype)

    def cur(c):
        return pl.BlockSpec((CONV_ROWS, TOK_WIDTH), lambda b, i: (b * nt + i, c))

    def prev(c):
        return pl.BlockSpec((CONV_HALO, TOK_WIDTH), lambda b, i: (jnp.maximum((b * nt + i) * sub - 1, 0), c))

    vec = pl.BlockSpec((1, TOK_WIDTH), lambda b, i: (0, 0))
    return pl.pallas_call(
        body, out_shape=(jax.ShapeDtypeStruct((n, D_MODEL), BF16), jax.ShapeDtypeStruct((n, TOK_WIDTH), F32)),
        grid=(batch, nt),
        in_specs=[cur(0), cur(1), prev(0), prev(1), pl.BlockSpec((32, TOK_WIDTH), lambda b, i: (0, 0)), vec, vec, vec],
        out_specs=(cur(0), cur(0)),
        scratch_shapes=[pltpu.VMEM((CONV_WIN, TOK_WIDTH), F32), pltpu.VMEM((SUBLANES - 1, SHIFT_ROWS, TOK_WIDTH), F32)],
        compiler_params=_params("parallel", "arbitrary"), name="conv_fwd")(z, z, z, z, cw, cb, lg, lb)


def conv_bwd(z, y, dcat, cw, lg, lb, batch, seq):
    n = z.shape[0]
    nt = seq // CONV_ROWS
    sub = CONV_ROWS // CONV_HALO
    lead = CONV_HALO - (CONV_W - 1)
    last_blk = n // CONV_HALO - 1

    def body(a_ref, g_ref, ap_ref, gp_ref, y_ref, yn_ref, do_ref, don_ref, cw_ref, lg_ref, lb_ref,
             dz_ref, dcw_ref, dsm_ref, win, shifted, dyw, dshifted, dg_o):
        b, i, which = pl.program_id(0), pl.program_id(1), pl.program_id(2)

        @pl.when(which == 0)
        def _():
            first, last = i == 0, i == nt - 1

            @pl.when((b == 0) & (i == 0))
            def _():
                dcw_ref[...] = jnp.zeros_like(dcw_ref)
                dsm_ref[...] = jnp.zeros_like(dsm_ref)

            win[0:CONV_HALO, :] = jnp.where(first, 0.0, _glu(ap_ref, gp_ref))
            win[CONV_HALO:CONV_WIN, :] = _glu(a_ref, g_ref)
            _preshift(win, shifted)
            yv = jnp.concatenate([y_ref[...], yn_ref[...]], axis=0)
            yh, rstd = _layer_norm_stats(yv)
            t = yh * lg_ref[...] + lb_ref[...]
            st = _sigmoid(t)
            dout = jnp.concatenate(
                [do_ref[...].astype(F32), jnp.where(last, 0.0, don_ref[...].astype(F32))], axis=0)
            dt = dout * st * (1.0 + t * (1.0 - st))
            dyh = dt * lg_ref[...]
            dy = rstd * (dyh - jnp.mean(dyh, axis=-1, keepdims=True)
                         - yh * jnp.mean(dyh * yh, axis=-1, keepdims=True))
            dyw[...] = dy
            _preshift(dyw, dshifted)
            dsm_ref[0:1, :] += jnp.sum(dy[0:CONV_ROWS], axis=0, keepdims=True)
            dsm_ref[1:2, :] += jnp.sum((dt * yh)[0:CONV_ROWS], axis=0, keepdims=True)
            dsm_ref[2:3, :] += jnp.sum(dt[0:CONV_ROWS], axis=0, keepdims=True)
            for c0 in range(0, TOK_WIDTH, LANES):
                lanes = slice(c0, c0 + LANES)
                dcw_acc = [jnp.zeros((SUBLANES, LANES), F32) for _ in range(CONV_W)]
                for r0 in range(0, CONV_ROWS, TAP_ROWS):
                    dyt = dyw[r0:r0 + TAP_ROWS, lanes]
                    dglu = jnp.zeros((TAP_ROWS, LANES), F32)
                    for w in range(CONV_W):
                        dcw_acc[w] = dcw_acc[w] + _fold_rows(dyt * _tap(win, shifted, lead + w, r0, lanes))
                        dglu = dglu + _tap(dyw, dshifted, CONV_W - 1 - w, r0, lanes) * cw_ref[w:w + 1, lanes]
                    avt = a_ref[r0:r0 + TAP_ROWS, lanes].astype(F32)
                    sgt = _sigmoid(g_ref[r0:r0 + TAP_ROWS, lanes].astype(F32))
                    dz_ref[r0:r0 + TAP_ROWS, lanes] = (dglu * sgt).astype(dz_ref.dtype)
                    dg_o[r0:r0 + TAP_ROWS, lanes] = (dglu * avt * sgt * (1.0 - sgt)).astype(dg_o.dtype)
                for w in range(CONV_W):
                    dcw_ref[w:w + 1, lanes] += jnp.sum(dcw_acc[w], axis=0, keepdims=True)

        @pl.when(which == 1)
        def _():
            dz_ref[...] = dg_o[...]

    def ahead(b, i, t):
        return jnp.minimum(b * nt + i + t, batch * nt - 1)

    def cur(c):
        return pl.BlockSpec((CONV_ROWS, TOK_WIDTH), lambda b, i, t: (ahead(b, i, t), c))

    def prev(c):
        return pl.BlockSpec((CONV_HALO, TOK_WIDTH), lambda b, i, t: (jnp.maximum(ahead(b, i, t) * sub - 1, 0), c))

    nxt = pl.BlockSpec((CONV_HALO, TOK_WIDTH),
                       lambda b, i, t: (jnp.minimum((ahead(b, i, t) + 1) * sub, last_blk), 0))
    vec = pl.BlockSpec((1, TOK_WIDTH), lambda b, i, t: (0, 0))
    full32 = pl.BlockSpec((32, TOK_WIDTH), lambda b, i, t: (0, 0))
    return pl.pallas_call(
        body,
        out_shape=(jax.ShapeDtypeStruct(z.shape, BF16), jax.ShapeDtypeStruct((32, TOK_WIDTH), F32),
                   jax.ShapeDtypeStruct((8, TOK_WIDTH), F32)),
        grid=(batch, nt, 2),
        in_specs=[cur(0), cur(1), prev(0), prev(1), cur(0), nxt, cur(0), nxt, full32, vec, vec],
        out_specs=(pl.BlockSpec((CONV_ROWS, TOK_WIDTH), lambda b, i, t: (b * nt + i, t)), full32,
                   pl.BlockSpec((8, TOK_WIDTH), lambda b, i, t: (0, 0))),
        scratch_shapes=[pltpu.VMEM((CONV_WIN, TOK_WIDTH), F32), pltpu.VMEM((SUBLANES - 1, SHIFT_ROWS, TOK_WIDTH), F32),
                        pltpu.VMEM((CONV_WIN, TOK_WIDTH), F32), pltpu.VMEM((SUBLANES - 1, SHIFT_ROWS, TOK_WIDTH), F32),
                        pltpu.VMEM((CONV_ROWS, TOK_WIDTH), BF16)],
        compiler_params=_params("arbitrary", "arbitrary", "arbitrary"), name="conv_bwd")(
            z, z, z, z, y, y, dcat, dcat, cw, lg, lb)


def loss_head(y, target):
    n, d = y.shape
    tm = _row_tile(n)
    nt = n // tm

    def body(y_ref, t_ref, dy_ref, dyb_ref, l_ref, acc_ref):
        i = pl.program_id(0)

        @pl.when(i == 0)
        def _():
            acc_ref[...] = jnp.zeros_like(acc_ref)

        err = y_ref[...] - t_ref[...]
        dy = err * (1.0 / d)
        dy_ref[...] = dy
        dyb_ref[...] = dy.astype(BF16)
        acc_ref[...] += jnp.sum(err * err, axis=0, keepdims=True)

        @pl.when(i == nt - 1)
        def _():
            total = jnp.sum(acc_ref[...], axis=-1, keepdims=True) * (0.5 / d)
            l_ref[...] = jnp.broadcast_to(total, l_ref.shape)

    row = pl.BlockSpec((tm, d), lambda i: (i, 0))
    return pl.pallas_call(
        body, out_shape=(jax.ShapeDtypeStruct((n, d), F32), jax.ShapeDtypeStruct((n, d), BF16),
                         jax.ShapeDtypeStruct((8, LANES), F32)), grid=(nt,),
        in_specs=[row, row], out_specs=(row, row, pl.BlockSpec((8, LANES), lambda i: (0, 0))),
        scratch_shapes=[pltpu.VMEM((1, d), F32)],
        compiler_params=_params("arbitrary"), name="loss_head")(y, target)


def col_sum(x, name="col_sum"):
    n, c = x.shape
    tm = _row_tile(n)

    def body(x_ref, o_ref):
        @pl.when(pl.program_id(0) == 0)
        def _():
            o_ref[...] = jnp.zeros_like(o_ref)

        o_ref[...] += jnp.sum(x_ref[...].astype(F32), axis=0, keepdims=True)

    return pl.pallas_call(
        body, out_shape=jax.ShapeDtypeStruct((1, c), F32), grid=(n // tm,),
        in_specs=[pl.BlockSpec((tm, c), lambda i: (i, 0))], out_specs=pl.BlockSpec((1, c), lambda i: (0, 0)),
        compiler_params=_params("arbitrary"), name=name)(x)


def adamw(w, g, m, v, name="adamw"):
    rows, cols = w.shape
    tr = rows
    for cand in (512, 256, 128, 64, 32, 16, 8):
        if rows % cand == 0 and rows > cand:
            tr = cand
            break
    c1 = 1.0 / (1.0 - ADAM_B1 ** ADAM_STEP)
    c2 = 1.0 / (1.0 - ADAM_B2 ** ADAM_STEP)

    def body(w_ref, g_ref, m_ref, v_ref, d_ref, nm_ref, nv_ref):
        gv = g_ref[...]
        nm = ADAM_B1 * m_ref[...] + (1.0 - ADAM_B1) * gv
        nv = ADAM_B2 * v_ref[...] + (1.0 - ADAM_B2) * (gv * gv)
        nm_ref[...] = nm
        nv_ref[...] = nv
        d_ref[...] = -ADAM_LR * ((nm * c1) / (jnp.sqrt(nv * c2) + ADAM_EPS) + ADAM_WD * w_ref[...])

    spec = pl.BlockSpec((tr, cols), lambda i: (i, 0))
    shape = jax.ShapeDtypeStruct((rows, cols), F32)
    return pl.pallas_call(
        body, out_shape=(shape, shape, shape), grid=(rows // tr,),
        in_specs=[spec, spec, spec, spec], out_specs=(spec, spec, spec),
        compiler_params=_params("parallel"), name=name)(w, g, m, v)


def _place():
    return lax.axis_index("x"), lax.axis_index("y"), lax.axis_index("c")


def _other_chips(x, y):
    return [(1 - x, y), (x, 1 - y), (1 - x, 1 - y)]


def small_exchange(slab, reduce):
    r = slab.shape[0]

    def body(in_ref, o_ref, *scratch):
        if reduce:
            buf, send_sems, recv_sems = scratch
        else:
            buf = o_ref
            send_sems, recv_sems = scratch
        x, y, c = _place()
        me = 4 * x + 2 * y + c
        buf[me] = in_ref[...]
        copies = []
        for k in range(1, N_DEV):
            peer = (x ^ (k >> 2), y ^ ((k >> 1) & 1), c ^ (k & 1))
            cp = pltpu.make_async_remote_copy(
                src_ref=in_ref, dst_ref=buf.at[me], send_sem=send_sems.at[k - 1], recv_sem=recv_sems.at[k - 1],
                device_id=peer, device_id_type=MESH)
            cp.start()
            copies.append(cp)
        for k in range(1, N_DEV):
            src = 4 * (x ^ (k >> 2)) + 2 * (y ^ ((k >> 1) & 1)) + (c ^ (k & 1))
            pltpu.make_async_remote_copy(
                src_ref=in_ref, dst_ref=buf.at[src], send_sem=send_sems.at[k - 1], recv_sem=recv_sems.at[k - 1],
                device_id=(x, y, c), device_id_type=MESH).wait_recv()
        for cp in copies:
            cp.wait_send()
        if reduce:
            total = buf[0]
            for d in range(1, N_DEV):
                total = total + buf[d]
            o_ref[...] = total

    sems = [pltpu.SemaphoreType.DMA((N_DEV - 1,)), pltpu.SemaphoreType.DMA((N_DEV - 1,))]
    if reduce:
        out_shape = jax.ShapeDtypeStruct((r, LANES), F32)
        scratch = [pltpu.VMEM((N_DEV, r, LANES), F32)] + sems
    else:
        out_shape = jax.ShapeDtypeStruct((N_DEV, r, LANES), F32)
        scratch = sems
    vmem = pl.BlockSpec(memory_space=pltpu.VMEM)
    return pl.pallas_call(
        body, out_shape=out_shape, in_specs=[vmem], out_specs=vmem, scratch_shapes=scratch,
        compiler_params=pltpu.CompilerParams(vmem_limit_bytes=VMEM_LIMIT),
        name="small_reduce" if reduce else "small_gather")(slab)


def gather_weights(shards, name, collective_id):
    nw = len(shards)
    ns = [s.shape[0] for s in shards]
    in_refs = [jax.new_ref(s, memory_space=pltpu.MemorySpace.HBM) for s in shards]
    out_refs = [jax.empty_ref(jax.ShapeDtypeStruct((N_DEV * s.shape[0], s.shape[1]), s.dtype),
                              memory_space=pltpu.MemorySpace.HBM) for s in shards]

    @pl.kernel(mesh=plsc.ScalarSubcoreMesh(axis_name="seq", num_cores=1), name=name,
               scratch_types=(pltpu.SemaphoreType.DMA((nw, 7)), pltpu.SemaphoreType.DMA((nw, 7)),
                              pltpu.SemaphoreType.DMA((nw,))),
               compiler_params=pltpu.CompilerParams(collective_id=collective_id))
    def launch(send_sems, recv_sems, local_sems):
        x, y, c = _place()
        me, sib = (x, y, c), (x, y, 1 - c)
        chips = _other_chips(x, y)
        barrier = pltpu.get_barrier_semaphore()
        for peer in [sib] + [(*chip, c) for chip in chips]:
            pl.semaphore_signal(barrier, inc=1, device_id=peer, device_id_type=MESH)
        pl.semaphore_wait(barrier, 4)

        def rows(w, dev):
            return out_refs[w].at[pl.ds((4 * dev[0] + 2 * dev[1] + dev[2]) * ns[w], ns[w]), :]

        def copy(w, k, block, to, src=None):
            return pltpu.make_async_remote_copy(
                src_ref=rows(w, block) if src is None else src, dst_ref=rows(w, block),
                send_sem=send_sems.at[w, k], recv_sem=recv_sems.at[w, k], device_id=to, device_id_type=MESH)

        started, sends = [], []
        for w in range(nw):
            mine = pltpu.make_async_copy(in_refs[w], rows(w, me), local_sems.at[w])
            mine.start()
            started.append(mine)
            first = [copy(w, 0, me, sib, src=in_refs[w])]
            first += [copy(w, 1 + j, me, (*chip, c), src=in_refs[w]) for j, chip in enumerate(chips)]
            for cp in first:
                cp.start()
            sends += first
        for w in range(nw):
            for j, chip in enumerate(chips):
                copy(w, 1 + j, (*chip, c), me).wait_recv()
                fwd = copy(w, 4 + j, (*chip, c), sib)
                fwd.start()
                sends.append(fwd)
        for w in range(nw):
            copy(w, 0, sib, me).wait_recv()
            for j, chip in enumerate(chips):
                copy(w, 4 + j, (*chip, 1 - c), me).wait_recv()
        for cp in sends:
            cp.wait_send()
        for mine in started:
            mine.wait()

    launch()
    return [r[...] for r in out_refs]


def _sequencer_exchange(sources, out_rows, peers_of, copies_of, name, collective_id):
    nw = len(sources)
    in_refs = [jax.new_ref(s, memory_space=pltpu.MemorySpace.HBM) for s in sources]
    out_refs = [jax.empty_ref(jax.ShapeDtypeStruct((rows, s.shape[1]), s.dtype), memory_space=pltpu.MemorySpace.HBM)
                for rows, s in zip(out_rows, sources)]
    per = len(copies_of(0, 0, 0, 0))

    @pl.kernel(mesh=plsc.ScalarSubcoreMesh(axis_name="seq", num_cores=1), name=name,
               scratch_types=(pltpu.SemaphoreType.DMA((nw, per)), pltpu.SemaphoreType.DMA((nw, per))),
               compiler_params=pltpu.CompilerParams(collective_id=collective_id))
    def launch(send_sems, recv_sems):
        x, y, c = _place()
        peers = peers_of(x, y, c)
        barrier = pltpu.get_barrier_semaphore()
        for peer in peers:
            pl.semaphore_signal(barrier, inc=1, device_id=peer, device_id_type=MESH)
        pl.semaphore_wait(barrier, len(peers))
        copies = []
        for w in range(nw):
            for k, (src_blk, dst_blk, rows, peer) in enumerate(copies_of(x, y, c, w)):
                cp = pltpu.make_async_remote_copy(
                    src_ref=in_refs[w].at[pl.ds(src_blk * rows, rows), :],
                    dst_ref=out_refs[w].at[pl.ds(dst_blk * rows, rows), :],
                    send_sem=send_sems.at[w, k], recv_sem=recv_sems.at[w, k], device_id=peer, device_id_type=MESH)
                cp.start()
                copies.append(cp)
        for cp in copies:
            cp.wait_recv()
        for cp in copies:
            cp.wait_send()

    launch()
    return [r[...] for r in out_refs]


def scatter_to_sibling(grads, name, collective_id):
    ns = [g.shape[0] // N_DEV for g in grads]
    return _sequencer_exchange(
        grads, [4 * n for n in ns],
        lambda x, y, c: [(x, y, 1 - c)],
        lambda x, y, c, w: [(2 * q + 1 - c, q, ns[w], (x, y, 1 - c)) for q in range(4)],
        name, collective_id)


def scatter_to_chips(parts, name, collective_id):
    ns = [p.shape[0] // 4 for p in parts]
    return _sequencer_exchange(
        parts, [3 * n for n in ns],
        lambda x, y, c: [(*chip, c) for chip in _other_chips(x, y)],
        lambda x, y, c, w: [(2 * chip[0] + chip[1], j, ns[w], (*chip, c)) for j, chip in enumerate(_other_chips(x, y))],
        name, collective_id)


def add_sibling(grad, landed, core, name):
    n = landed.shape[0] // 4
    cols = grad.shape[1]

    def body(c_ref, g_ref, l_ref, o_ref):
        o_ref[...] = (g_ref[...].astype(F32) + l_ref[...].astype(F32)).astype(o_ref.dtype)

    grid_spec = pltpu.PrefetchScalarGridSpec(
        num_scalar_prefetch=1, grid=(4,),
        in_specs=[pl.BlockSpec((n, cols), lambda q, c_ref: (2 * q + c_ref[0], 0)),
                  pl.BlockSpec((n, cols), lambda q, c_ref: (q, 0))],
        out_specs=pl.BlockSpec((n, cols), lambda q, c_ref: (q, 0)))
    return pl.pallas_call(
        body, out_shape=jax.ShapeDtypeStruct(landed.shape, landed.dtype), grid_spec=grid_spec,
        compiler_params=_params("arbitrary"), name=name)(core, grad, landed)


def add_chips(part, landed, chip, name):
    n = landed.shape[0] // 3
    cols = part.shape[1]

    def body(q_ref, p_ref, l0_ref, l1_ref, l2_ref, o_ref):
        o_ref[...] = ((p_ref[...].astype(F32) + l0_ref[...].astype(F32)) + l1_ref[...].astype(F32)) \
            + l2_ref[...].astype(F32)

    def landed_spec(j):
        return pl.BlockSpec((n, cols), lambda i, q_ref: (j, 0))

    grid_spec = pltpu.PrefetchScalarGridSpec(
        num_scalar_prefetch=1, grid=(1,),
        in_specs=[pl.BlockSpec((n, cols), lambda i, q_ref: (q_ref[0], 0)), landed_spec(0), landed_spec(1), landed_spec(2)],
        out_specs=pl.BlockSpec((n, cols), lambda i, q_ref: (0, 0)))
    return pl.pallas_call(
        body, out_shape=jax.ShapeDtypeStruct((n, cols), F32), grid_spec=grid_spec,
        compiler_params=_params("arbitrary"), name=name)(chip, part, landed, landed, landed)


def _pack(arrays):
    flat = jnp.concatenate([a.reshape(-1).astype(F32) for a in arrays])
    pad = (-flat.shape[0]) % (8 * LANES)
    return jnp.pad(flat, (0, pad)).reshape(-1, LANES)


def _unpack(slab, shapes):
    flat = slab.reshape(slab.shape[:-2] + (-1,))
    out, off = [], 0
    for shp in shapes:
        size = 1
        for s in shp:
            size *= s
        out.append(flat[..., off:off + size].reshape(flat.shape[:-1] + tuple(shp)))
        off += size
    return out


def kernel(x, mem, norm1_g, mem_norm_g, a_w_in, a_q_g, a_k_g, a_rel_bias, b_w_in, b_b_in, b_conv_w, b_conv_b, b_ln_g, b_ln_b, mq_g, mk_g, w_mem_kv, w_out, norm2_g, w_gate, w_up, w_down, loss_target, m_norm1_g, m_mem_norm_g, m_a_w_in, m_a_q_g, m_a_k_g, m_a_rel_bias, m_b_w_in, m_b_b_in, m_b_conv_w, m_b_conv_b, m_b_ln_g, m_b_ln_b, m_mq_g, m_mk_g, m_w_mem_kv, m_w_out, m_norm2_g, m_w_gate, m_w_up, m_w_down, v_norm1_g, v_mem_norm_g, v_a_w_in, v_a_q_g, v_a_k_g, v_a_rel_bias, v_b_w_in, v_b_b_in, v_b_conv_w, v_b_conv_b, v_b_ln_g, v_b_ln_b, v_mq_g, v_mk_g, v_w_mem_kv, v_w_out, v_norm2_g, v_w_gate, v_w_up, v_w_down):
    batch, seq, d = x.shape
    mtok = mem.shape[1]
    n = batch * seq
    ax, ay, ac = _place()
    me = 4 * ax + 2 * ay + ac
    core_arr = jnp.reshape(ac, (1,)).astype(jnp.int32)
    chip_arr = jnp.reshape(2 * ax + ay, (1,)).astype(jnp.int32)

    def t_bf16(w):
        return jnp.transpose(w).astype(BF16)

    a_win_t, = gather_weights([t_bf16(a_w_in[0])], "gather_in_a", 1)
    wkv0, wo0, wg0, wu0, wd0 = gather_weights(
        [w_mem_kv[0].astype(BF16), w_out[0].astype(BF16), t_bf16(w_gate[0]), t_bf16(w_up[0]), w_down[0].astype(BF16)],
        "gather_layer_a", 2)
    b_win_t, wkv1, wo1 = gather_weights(
        [t_bf16(b_w_in[0]), w_mem_kv[1].astype(BF16), w_out[1].astype(BF16)], "gather_in_b", 3)
    wg1, wu1, wd1 = gather_weights([t_bf16(w_gate[1]), t_bf16(w_up[1]), w_down[1].astype(BF16)], "gather_ffn_b", 4)
    wg_t, wu_t, wd, wo, wkv = [wg0, wg1], [wu0, wu1], [wd0, wd1], [wo0, wo1], [wkv0, wkv1]

    f_loc = b_b_in.shape[1]
    c_loc = b_conv_b.shape[1]
    small_shapes = [(f_loc,), (CONV_W, c_loc), (c_loc,), (c_loc,), (c_loc,)]
    gathered = small_exchange(_pack([b_b_in, b_conv_w, b_conv_b, b_ln_g, b_ln_b]), reduce=False)
    bb_g, cw_g, cb_g, lg_g, lb_g = _unpack(gathered, small_shapes)
    bb_full = bb_g.reshape(1, -1)
    cw_full = jnp.pad(jnp.transpose(cw_g, (1, 0, 2)).reshape(CONV_W, -1), ((0, 32 - CONV_W), (0, 0)))
    cb_full, lg_full, lb_full = cb_g.reshape(1, -1), lg_g.reshape(1, -1), lb_g.reshape(1, -1)

    def two(g):
        return jnp.concatenate([g, g], axis=-1)

    gq2, gk2 = two(a_q_g), two(a_k_g)
    rel16 = jnp.pad(a_rel_bias[0], ((0, 16 - a_rel_bias.shape[1]), (0, 0)))
    bias = bias_blocks(rel16)

    x0 = x.reshape(n, d)
    mem2 = mem.reshape(batch * mtok, d)
    zero_mem = jnp.zeros_like(mem2)

    saved = []
    xin = x0
    for l in range(2):
        h = rms_fwd(xin, norm1_g[l:l + 1], name=f"rms1_fwd_{l}")
        mem_n = rms_fwd(mem2, mem_norm_g[l:l + 1], name=f"rms_mem_fwd_{l}")
        kv = mm_nn(mem_n, wkv[l], name=f"mem_kv_{l}")
        gq4 = jnp.tile(mq_g[l:l + 1], (1, 4))
        gk4 = jnp.tile(mk_g[l:l + 1], (1, 4))
        y_conv = None
        if l == 0:
            z = mm_nt(h, a_win_t, name="in_proj_a")
            cat = attn_fwd(z, gq2, gk2, bias, batch, seq)
            qcol = 3 * TOK_WIDTH // MEM_WIDTH
        else:
            z = mm_nt(h, b_win_t, bias=bb_full, name="in_proj_b")
            cat, y_conv = conv_fwd(z, cw_full, cb_full, lg_full, lb_full, batch, seq)
            qcol = 2 * TOK_WIDTH // MEM_WIDTH
        cat = memattn_fwd(z, kv, gq4, gk4, cat, batch, seq, qcol, name=f"memattn_fwd_{l}")
        x1 = mm_nn(cat, wo[l], res=xin, name=f"out_proj_{l}")
        h2 = rms_fwd(x1, norm2_g[l:l + 1], name=f"rms2_fwd_{l}")
        gate, up, act = gate_up(h2, wg_t[l], wu_t[l], name=f"gate_up_{l}")
        x2 = mm_nn(act, wd[l], res=x1, name=f"down_proj_{l}")
        saved.append(dict(xin=xin, h=h, mem_n=mem_n, kv=kv, gq4=gq4, gk4=gk4, z=z, qcol=qcol, cat=cat, x1=x1, h2=h2,
                          gate=gate, up=up, act=act, y_conv=y_conv))
        xin = x2

    dx, dx_b, loss_blk = loss_head(xin, loss_target.reshape(n, d))
    loss = lax.psum(loss_blk[0, 0], ("x", "y", "c"))

    big = {}
    small = {}
    red = {}
    groups = 0

    def reduce_group(keys):
        nonlocal groups
        gid = groups
        groups += 1
        glist = [big[k] for k in keys]
        landed1 = scatter_to_sibling(glist, f"scatter_sibling_{gid}", 8 + 2 * gid)
        parts = [add_sibling(g, ld, core_arr, name=f"add_sibling_{k}") for k, g, ld in zip(keys, glist, landed1)]
        landed2 = scatter_to_chips(parts, f"scatter_chips_{gid}", 9 + 2 * gid)
        for k, p, ld in zip(keys, parts, landed2):
            red[k] = add_chips(p, ld, chip_arr, name=f"add_chips_{k}")

    for l in (1, 0):
        sv = saved[l]
        dgate, dup = ffn_bwd_act(dx_b, wd[l], sv["gate"], sv["up"], name=f"ffn_bwd_act_{l}")
        big[f"wd{l}"] = mm_tn(sv["act"], dx_b, name=f"grad_wd_{l}")
        dh2 = mm2_nn(dgate, wg_t[l], dup, wu_t[l], name=f"ffn_bwd_h_{l}")
        big[f"wg{l}"] = mm_tn(dgate, sv["h2"], name=f"grad_wg_{l}")
        big[f"wu{l}"] = mm_tn(dup, sv["h2"], name=f"grad_wu_{l}")
        dx1, dx1_b, small[f"norm2_{l}"] = rms_bwd(dh2, sv["x1"], norm2_g[l:l + 1], dx, name=f"rms2_bwd_{l}")
        dcat = mm_nt(dx1_b, wo[l], name=f"out_proj_bwd_{l}")
        big[f"wo{l}"] = mm_tn(sv["cat"], dx1_b, name=f"grad_wo_{l}")
        reduce_group([f"wd{l}", f"wg{l}", f"wu{l}", f"wo{l}"])
        if l == 0:
            dz, dbias, small["a_q"], small["a_k"] = attn_bwd(sv["z"], dcat, gq2, gk2, bias, batch, seq)
            small["rel"] = bias_grad(dbias)
            win_t = a_win_t
        else:
            dz, small["cw"], small["csum"] = conv_bwd(sv["z"], sv["y_conv"], dcat, cw_full, lg_full, lb_full, batch, seq)
            win_t = b_win_t
        dz, dkv, small[f"mq_{l}"], small[f"mk_{l}"] = memattn_bwd(
            sv["z"], sv["kv"], dcat, sv["gq4"], sv["gk4"], dz, batch, seq, sv["qcol"], name=f"memattn_bwd_{l}")
        if l == 1:
            small["bb"] = col_sum(dz, name="grad_b_in")
        big[f"win{l}"] = mm_tn(dz, sv["h"], name=f"grad_win_{l}")
        dh = mm_nn(dz, win_t, name=f"in_proj_bwd_{l}")
        big[f"wkv{l}"] = mm_tn(sv["mem_n"], dkv, name=f"grad_wkv_{l}")
        reduce_group([f"win{l}", f"wkv{l}"])
        dmem_n = mm_nt(dkv, wkv[l], out_dtype=F32, name=f"mem_kv_bwd_{l}")
        _, _, small[f"memnorm_{l}"] = rms_bwd(dmem_n, mem2, mem_norm_g[l:l + 1], zero_mem, name=f"rms_mem_bwd_{l}")
        dx, dx_b, small[f"norm1_{l}"] = rms_bwd(dh, sv["xin"], norm1_g[l:l + 1], dx1, name=f"rms1_bwd_{l}")
    grad_x = dx.reshape(batch, seq, d)

    g_a_w_in = jnp.transpose(red["win0"])[None]
    g_b_w_in = jnp.transpose(red["win1"])[None]
    g_w_gate = jnp.stack([jnp.transpose(red["wg0"]), jnp.transpose(red["wg1"])])
    g_w_up = jnp.stack([jnp.transpose(red["wu0"]), jnp.transpose(red["wu1"])])
    g_w_down = jnp.stack([red["wd0"], red["wd1"]])
    g_w_out = jnp.stack([red["wo0"], red["wo1"]])
    g_w_mem_kv = jnp.stack([red["wkv0"], red["wkv1"]])

    def fold(v, groups):
        return jnp.sum(v.reshape(groups, HEAD_DIM), axis=0, keepdims=True)

    heads = a_rel_bias.shape[1]
    small_list = [
        jnp.concatenate([small["norm1_0"], small["norm1_1"]]),
        jnp.concatenate([small["memnorm_0"], small["memnorm_1"]]),
        fold(small["a_q"], 2), fold(small["a_k"], 2), small["rel"][:heads][None],
        small["bb"], small["cw"][:CONV_W][None], small["csum"][0:1], small["csum"][1:2], small["csum"][2:3],
        jnp.concatenate([fold(small["mq_0"], 4), fold(small["mq_1"], 4)]),
        jnp.concatenate([fold(small["mk_0"], 4), fold(small["mk_1"], 4)]),
        jnp.concatenate([small["norm2_0"], small["norm2_1"]]),
    ]
    small_full_shapes = [a.shape for a in small_list]
    summed = _unpack(small_exchange(_pack(small_list), reduce=True), small_full_shapes)
    (g_norm1, g_memnorm, g_aq, g_ak, g_rel, g_bb_full, g_cw_full, g_cb_full, g_lg_full, g_lb_full,
     g_mq, g_mk, g_norm2) = summed
    g_bb = lax.dynamic_slice_in_dim(g_bb_full, me * f_loc, f_loc, axis=1)
    g_cw = lax.dynamic_slice_in_dim(g_cw_full, me * c_loc, c_loc, axis=2)
    g_cb = lax.dynamic_slice_in_dim(g_cb_full, me * c_loc, c_loc, axis=1)
    g_lg = lax.dynamic_slice_in_dim(g_lg_full, me * c_loc, c_loc, axis=1)
    g_lb = lax.dynamic_slice_in_dim(g_lb_full, me * c_loc, c_loc, axis=1)

    grads = [g_norm1, g_memnorm, g_a_w_in, g_aq, g_ak, g_rel, g_b_w_in, g_bb, g_cw, g_cb, g_lg, g_lb,
             g_mq, g_mk, g_w_mem_kv, g_w_out, g_norm2, g_w_gate, g_w_up, g_w_down]
    weights = [norm1_g, mem_norm_g, a_w_in, a_q_g, a_k_g, a_rel_bias, b_w_in, b_b_in, b_conv_w, b_conv_b, b_ln_g,
               b_ln_b, mq_g, mk_g, w_mem_kv, w_out, norm2_g, w_gate, w_up, w_down]
    moms = [m_norm1_g, m_mem_norm_g, m_a_w_in, m_a_q_g, m_a_k_g, m_a_rel_bias, m_b_w_in, m_b_b_in, m_b_conv_w,
            m_b_conv_b, m_b_ln_g, m_b_ln_b, m_mq_g, m_mk_g, m_w_mem_kv, m_w_out, m_norm2_g, m_w_gate, m_w_up, m_w_down]
    vels = [v_norm1_g, v_mem_norm_g, v_a_w_in, v_a_q_g, v_a_k_g, v_a_rel_bias, v_b_w_in, v_b_b_in, v_b_conv_w,
            v_b_conv_b, v_b_ln_g, v_b_ln_b, v_mq_g, v_mk_g, v_w_mem_kv, v_w_out, v_norm2_g, v_w_gate, v_w_up, v_w_down]

    large = {2, 6, 14, 15, 17, 18, 19}
    deltas, new_m, new_v = [None] * 20, [None] * 20, [None] * 20
    for i in sorted(large):
        shp = weights[i].shape
        flat = (shp[0] * shp[1], shp[2])
        dl, nm, nv = adamw(weights[i].reshape(flat), grads[i].reshape(flat), moms[i].reshape(flat),
                           vels[i].reshape(flat), name=f"adamw_{i}")
        deltas[i], new_m[i], new_v[i] = dl.reshape(shp), nm.reshape(shp), nv.reshape(shp)
    small_idx = [i for i in range(20) if i not in large]
    small_shapes2 = [weights[i].shape for i in small_idx]
    dl, nm, nv = adamw(_pack([weights[i] for i in small_idx]), _pack([grads[i] for i in small_idx]),
                       _pack([moms[i] for i in small_idx]), _pack([vels[i] for i in small_idx]), name="adamw_small")
    for i, a, b, cc in zip(small_idx, _unpack(dl, small_shapes2), _unpack(nm, small_shapes2), _unpack(nv, small_shapes2)):
        deltas[i], new_m[i], new_v[i] = a, b, cc

    return (loss, grad_x, *grads, *deltas, *new_m, *new_v)
```

```python
import functools

import jax
import jax.numpy as jnp
from jax import lax
from jax.experimental import pallas as pl
from jax.experimental.pallas import tpu as pltpu
from jax.experimental.pallas import tpu_sc as plsc

F32 = jnp.float32
BF16 = jnp.bfloat16
HIGHEST = lax.Precision.HIGHEST
MESH = pl.DeviceIdType.MESH
ANY = pl.BlockSpec(memory_space=pl.ANY)

N_DEV = 8
D_MODEL = 1024
HEAD_DIM = 64
TOK_WIDTH = 768
MEM_WIDTH = 256
CHUNK = 64
Q_BLOCK = 256
KEY_WIN = 768
BAND = 576
N_REL = 192
CONV_W = 31
CONV_HALO = 32
NORM_EPS = 1e-6
NEG_INF = -1e30
ATTN_SCALE = HEAD_DIM ** -0.5
LANES = 128
ROW_TILE = 512
VMEM_LIMIT = 56 * 1024 * 1024

ADAM_LR, ADAM_B1, ADAM_B2, ADAM_EPS, ADAM_WD, ADAM_STEP = 0.001, 0.9, 0.999, 1e-08, 0.01, 10


def _params(*sem):
    return pltpu.CompilerParams(dimension_semantics=sem, vmem_limit_bytes=VMEM_LIMIT)


def _row_tile(m):
    return ROW_TILE if m % ROW_TILE == 0 else m


def _col_tile(n, cap=1408):
    best = None
    for t in range(LANES, min(n, cap) + 1, LANES):
        if n % t == 0:
            best = t
    return best if best is not None else n


def _dot(a, b, ca, cb):
    return lax.dot_general(a, b, (((ca,), (cb,)), ((), ())), preferred_element_type=F32)


def _sigmoid(x):
    return 0.5 * jnp.tanh(0.5 * x) + 0.5


def mm_nt(a, b, bias=None, out_dtype=BF16, name="mm_nt"):
    m, k = a.shape
    n = b.shape[0]
    tm, tn = _row_tile(m), _col_tile(n)

    def body(*refs):
        a_ref, b_ref = refs[0], refs[1]
        o_ref = refs[-1]
        acc = _dot(a_ref[...].astype(BF16), b_ref[...].astype(BF16), 1, 1)
        if bias is not None:
            acc = acc + refs[2][...]
        o_ref[...] = acc.astype(o_ref.dtype)

    in_specs = [pl.BlockSpec((tm, k), lambda j, i: (i, 0)), pl.BlockSpec((tn, k), lambda j, i: (j, 0))]
    args = [a, b]
    if bias is not None:
        in_specs.append(pl.BlockSpec((1, tn), lambda j, i: (0, j)))
        args.append(bias)
    return pl.pallas_call(
        body, out_shape=jax.ShapeDtypeStruct((m, n), out_dtype), grid=(n // tn, m // tm),
        in_specs=in_specs, out_specs=pl.BlockSpec((tm, tn), lambda j, i: (i, j)),
        compiler_params=_params("parallel", "arbitrary"), name=name)(*args)


def mm_nn(a, b, res=None, out_dtype=F32, name="mm_nn"):
    m, k = a.shape
    n = b.shape[1]
    tm, tn = _row_tile(m), _col_tile(n, 1024)

    def body(*refs):
        a_ref, b_ref = refs[0], refs[1]
        o_ref = refs[-1]
        acc = _dot(a_ref[...].astype(BF16), b_ref[...].astype(BF16), 1, 0)
        if res is not None:
            acc = acc + refs[2][...]
        o_ref[...] = acc.astype(o_ref.dtype)

    in_specs = [pl.BlockSpec((tm, k), lambda j, i: (i, 0)), pl.BlockSpec((k, tn), lambda j, i: (0, j))]
    args = [a, b]
    if res is not None:
        in_specs.append(pl.BlockSpec((tm, tn), lambda j, i: (i, j)))
        args.append(res)
    return pl.pallas_call(
        body, out_shape=jax.ShapeDtypeStruct((m, n), out_dtype), grid=(n // tn, m // tm),
        in_specs=in_specs, out_specs=pl.BlockSpec((tm, tn), lambda j, i: (i, j)),
        compiler_params=_params("parallel", "arbitrary"), name=name)(*args)


def mm2_nn(a1, b1, a2, b2, name="mm2_nn"):
    m, k = a1.shape
    n = b1.shape[1]
    tm = _row_tile(m)

    def body(a1_ref, b1_ref, a2_ref, b2_ref, o_ref):
        o_ref[...] = _dot(a1_ref[...], b1_ref[...], 1, 0) + _dot(a2_ref[...], b2_ref[...], 1, 0)

    a_spec = pl.BlockSpec((tm, k), lambda i: (i, 0))
    b_spec = pl.BlockSpec((k, n), lambda i: (0, 0))
    return pl.pallas_call(
        body, out_shape=jax.ShapeDtypeStruct((m, n), F32), grid=(m // tm,),
        in_specs=[a_spec, b_spec, a_spec, b_spec], out_specs=pl.BlockSpec((tm, n), lambda i: (i, 0)),
        compiler_params=_params("parallel"), name=name)(a1, b1, a2, b2)


def mm_tn(a, b, out_dtype=BF16, name="mm_tn"):
    t, r = a.shape
    c = b.shape[1]
    tr = _col_tile(r, 512)

    def body(a_ref, b_ref, o_ref):
        o_ref[...] = _dot(a_ref[...].astype(BF16), b_ref[...].astype(BF16), 0, 0).astype(o_ref.dtype)

    return pl.pallas_call(
        body, out_shape=jax.ShapeDtypeStruct((r, c), out_dtype), grid=(r // tr,),
        in_specs=[pl.BlockSpec((t, tr), lambda i: (0, i)), pl.BlockSpec((t, c), lambda i: (0, 0))],
        out_specs=pl.BlockSpec((tr, c), lambda i: (i, 0)),
        compiler_params=_params("parallel"), name=name)(a, b)


def rms_fwd(x, g, name="rms_fwd"):
    n, d = x.shape
    tm = _row_tile(n)

    def body(x_ref, g_ref, o_ref):
        xv = x_ref[...]
        r = lax.rsqrt(jnp.mean(xv * xv, axis=-1, keepdims=True) + NORM_EPS)
        o_ref[...] = (xv * r * g_ref[...]).astype(o_ref.dtype)

    return pl.pallas_call(
        body, out_shape=jax.ShapeDtypeStruct((n, d), BF16), grid=(n // tm,),
        in_specs=[pl.BlockSpec((tm, d), lambda i: (i, 0)), pl.BlockSpec((1, d), lambda i: (0, 0))],
        out_specs=pl.BlockSpec((tm, d), lambda i: (i, 0)),
        compiler_params=_params("parallel"), name=name)(x, g)


def rms_bwd(dh, x, g, dres, name="rms_bwd"):
    n, d = x.shape
    tm = _row_tile(n)

    def body(dh_ref, x_ref, g_ref, dres_ref, dx_ref, dxb_ref, dg_ref):
        @pl.when(pl.program_id(0) == 0)
        def _():
            dg_ref[...] = jnp.zeros_like(dg_ref)

        xv = x_ref[...]
        dhv = dh_ref[...].astype(F32)
        r = lax.rsqrt(jnp.mean(xv * xv, axis=-1, keepdims=True) + NORM_EPS)
        xhat = xv * r
        dg_ref[...] += jnp.sum(dhv * xhat, axis=0, keepdims=True)
        dxhat = dhv * g_ref[...]
        mean_t = jnp.mean(dxhat * xhat, axis=-1, keepdims=True)
        dx = dres_ref[...] + r * (dxhat - xhat * mean_t)
        dx_ref[...] = dx
        dxb_ref[...] = dx.astype(BF16)

    row = pl.BlockSpec((tm, d), lambda i: (i, 0))
    vec = pl.BlockSpec((1, d), lambda i: (0, 0))
    return pl.pallas_call(
        body, out_shape=(jax.ShapeDtypeStruct((n, d), F32), jax.ShapeDtypeStruct((n, d), BF16),
                         jax.ShapeDtypeStruct((1, d), F32)), grid=(n // tm,),
        in_specs=[row, row, vec, row], out_specs=(row, row, vec),
        compiler_params=_params("arbitrary"), name=name)(dh, x, g, dres)


def gate_up(h2, wg_t, wu_t, name="gate_up"):
    n, d = h2.shape
    f = wg_t.shape[0]
    tm, tn = _row_tile(n), _col_tile(f)

    def body(h_ref, wg_ref, wu_ref, g_ref, u_ref, a_ref):
        hv = h_ref[...]
        gv = _dot(hv, wg_ref[...], 1, 1)
        uv = _dot(hv, wu_ref[...], 1, 1)
        g_ref[...] = gv.astype(BF16)
        u_ref[...] = uv.astype(BF16)
        a_ref[...] = (gv * _sigmoid(gv) * uv).astype(BF16)

    w_spec = pl.BlockSpec((tn, d), lambda j, i: (j, 0))
    o_spec = pl.BlockSpec((tm, tn), lambda j, i: (i, j))
    o_shape = jax.ShapeDtypeStruct((n, f), BF16)
    return pl.pallas_call(
        body, out_shape=(o_shape, o_shape, o_shape), grid=(f // tn, n // tm),
        in_specs=[pl.BlockSpec((tm, d), lambda j, i: (i, 0)), w_spec, w_spec], out_specs=(o_spec, o_spec, o_spec),
        compiler_params=_params("parallel", "arbitrary"), name=name)(h2, wg_t, wu_t)


def ffn_bwd_act(dx, wd, gate, up, name="ffn_bwd_act"):
    n, d = dx.shape
    f = wd.shape[0]
    tm, tn = _row_tile(n), _col_tile(f)

    def body(dx_ref, wd_ref, g_ref, u_ref, dg_ref, du_ref):
        dact = _dot(dx_ref[...].astype(BF16), wd_ref[...], 1, 1)
        gv = g_ref[...].astype(F32)
        uv = u_ref[...].astype(F32)
        sg = _sigmoid(gv)
        dg_ref[...] = (dact * uv * sg * (1.0 + gv * (1.0 - sg))).astype(BF16)
        du_ref[...] = (dact * gv * sg).astype(BF16)

    t_spec = pl.BlockSpec((tm, tn), lambda j, i: (i, j))
    o_shape = jax.ShapeDtypeStruct((n, f), BF16)
    return pl.pallas_call(
        body, out_shape=(o_shape, o_shape), grid=(f // tn, n // tm),
        in_specs=[pl.BlockSpec((tm, d), lambda j, i: (i, 0)), pl.BlockSpec((tn, d), lambda j, i: (j, 0)), t_spec, t_spec],
        out_specs=(t_spec, t_spec),
        compiler_params=_params("parallel", "arbitrary"), name=name)(dx, wd, gate, up)


def _group_masks(width):
    lane = lax.broadcasted_iota(jnp.int32, (1, width), 1)
    return [(lane >= HEAD_DIM * g) & (lane < HEAD_DIM * (g + 1)) for g in range(width // HEAD_DIM)]


def _group_sum(x, masks):
    out = jnp.zeros_like(x)
    for msk in masks:
        s = jnp.sum(jnp.where(msk, x, 0.0), axis=-1, keepdims=True)
        out = jnp.where(msk, s, out)
    return out


def _head_norm(x, gain, masks):
    r = lax.rsqrt(_group_sum(x * x, masks) * (1.0 / HEAD_DIM) + NORM_EPS)
    xhat = x * r
    return xhat * gain, xhat, r


def _head_norm_bwd(dxn, xhat, r, gain, masks):
    dgain = jnp.sum(dxn * xhat, axis=0, keepdims=True)
    dxhat = dxn * gain
    mean_t = _group_sum(dxhat * xhat, masks) * (1.0 / HEAD_DIM)
    return r * (dxhat - xhat * mean_t), dgain


def _softmax_rows(s):
    e = jnp.exp(s - jnp.max(s, axis=-1, keepdims=True))
    return e * (1.0 / jnp.sum(e, axis=-1, keepdims=True))


def _rel_onehot():
    col = lax.broadcasted_iota(jnp.int32, (1, KEY_WIN), 1)
    off = jnp.where(col < KEY_WIN - LANES, col, col - KEY_WIN)
    idx = jnp.clip(8 * CHUNK - off, -(CHUNK - 1), LANES) + (CHUNK - 1)
    return (lax.broadcasted_iota(jnp.int32, (N_REL, KEY_WIN), 0) == idx).astype(F32)


def bias_blocks(rel16):
    heads = TOK_WIDTH // HEAD_DIM

    def body(rel_ref, o_ref, u_ref):
        u_ref[...] = jnp.dot(rel_ref[...], _rel_onehot(), precision=HIGHEST, preferred_element_type=F32)
        row = lax.broadcasted_iota(jnp.int32, (CHUNK, KEY_WIN), 0)
        col = lax.broadcasted_iota(jnp.int32, (CHUNK, KEY_WIN), 1)
        for h in range(heads):
            xv = jnp.broadcast_to(u_ref[h:h + 1, :], (CHUNK, KEY_WIN))
            for b in range(6):
                xv = jnp.where(((row >> b) & 1) == 1, pltpu.roll(xv, 1 << b, axis=1), xv)
            xv = jnp.where(col < BAND, xv, NEG_INF)
            for i in range(Q_BLOCK // CHUNK):
                o_ref[h, CHUNK * i:CHUNK * (i + 1), :] = pltpu.roll(xv, CHUNK * i, axis=1) if i else xv

    return pl.pallas_call(
        body, out_shape=jax.ShapeDtypeStruct((heads, Q_BLOCK, KEY_WIN), F32),
        scratch_shapes=[pltpu.VMEM((16, KEY_WIN), F32)], name="bias_blocks")(rel16)


def bias_grad(dbias):
    heads = dbias.shape[0]

    def body(db_ref, o_ref, y_ref):
        y_ref[...] = jnp.zeros_like(y_ref)
        row = lax.broadcasted_iota(jnp.int32, (CHUNK, KEY_WIN), 0)
        for h in range(heads):
            fv = db_ref[h, 0:CHUNK, :]
            for i in range(1, Q_BLOCK // CHUNK):
                fv = fv + pltpu.roll(db_ref[h, CHUNK * i:CHUNK * (i + 1), :], KEY_WIN - CHUNK * i, axis=1)
            for b in range(6):
                fv = jnp.where(((row >> b) & 1) == 1, pltpu.roll(fv, KEY_WIN - (1 << b), axis=1), fv)
            y_ref[h:h + 1, :] = jnp.sum(fv, axis=0, keepdims=True)
        o_ref[...] = lax.dot_general(y_ref[...], _rel_onehot(), (((1,), (1,)), ((), ())),
                                     precision=HIGHEST, preferred_element_type=F32)

    return pl.pallas_call(
        body, out_shape=jax.ShapeDtypeStruct((16, N_REL), F32),
        scratch_shapes=[pltpu.VMEM((16, KEY_WIN), F32)], name="bias_grad")(dbias)


def _attn_windows(seq):
    out = []
    for j in range(seq // Q_BLOCK):
        r0 = j * Q_BLOCK
        k0 = max(0, r0 - 8 * CHUNK)
        width = r0 + Q_BLOCK - k0
        out.append((r0, k0, width, KEY_WIN - width))
    return out


def attn_fwd(z, gq2, gk2, bias, batch, seq):
    n = z.shape[0]
    pairs = TOK_WIDTH // LANES

    def body(q_ref, k_ref, v_ref, gq_ref, gk_ref, b_ref, o_ref, qs_s, kn_s):
        masks = _group_masks(LANES)
        qs_s[...] = (_head_norm(q_ref[...].astype(F32), gq_ref[...], masks)[0] * ATTN_SCALE).astype(BF16)
        kn_s[...] = _head_norm(k_ref[...].astype(F32), gk_ref[...], masks)[0].astype(BF16)
        for r0, k0, width, c0 in _attn_windows(seq):
            qb = qs_s[r0:r0 + Q_BLOCK, :]
            kw = kn_s[k0:k0 + width, :]
            vw = v_ref[k0:k0 + width, :]
            out = jnp.zeros((Q_BLOCK, LANES), F32)
            for h, msk in enumerate(masks):
                qh = jnp.where(msk, qb, jnp.zeros_like(qb))
                s = _dot(qh, kw, 1, 1) + b_ref[h, :, c0:KEY_WIN]
                p = _softmax_rows(s).astype(BF16)
                out = jnp.where(msk, _dot(p, vw, 1, 0), out)
            o_ref[r0:r0 + Q_BLOCK, :] = out.astype(o_ref.dtype)

    def col(off):
        return pl.BlockSpec((seq, LANES), lambda b, p: (b, off + p))

    vec = pl.BlockSpec((1, LANES), lambda b, p: (0, 0))
    return pl.pallas_call(
        body, out_shape=jax.ShapeDtypeStruct((n, D_MODEL), BF16), grid=(batch, pairs),
        in_specs=[col(0), col(pairs), col(2 * pairs), vec, vec,
                  pl.BlockSpec((2, Q_BLOCK, KEY_WIN), lambda b, p: (p, 0, 0))],
        out_specs=pl.BlockSpec((seq, LANES), lambda b, p: (b, p)),
        scratch_shapes=[pltpu.VMEM((seq, LANES), BF16), pltpu.VMEM((seq, LANES), BF16)],
        compiler_params=_params("parallel", "arbitrary"), name="attn_fwd")(z, z, z, gq2, gk2, bias)


def attn_bwd(z, dcat, gq2, gk2, bias, batch, seq):
    n = z.shape[0]
    pairs = TOK_WIDTH // LANES

    def body(q_ref, k_ref, v_ref, do_ref, gq_ref, gk_ref, b_ref,
             dz_ref, db_ref, dgq_ref, dgk_ref, qs_s, kn_s, dqn_s, dkn_s, dv_s, dk_o, dv_o):
        pi, bi, which = pl.program_id(0), pl.program_id(1), pl.program_id(2)

        @pl.when(which == 0)
        def _():
            masks = _group_masks(LANES)

            @pl.when(bi == 0)
            def _():
                db_ref[...] = jnp.zeros_like(db_ref)

            @pl.when((bi == 0) & (pi == 0))
            def _():
                dgq_ref[...] = jnp.zeros_like(dgq_ref)
                dgk_ref[...] = jnp.zeros_like(dgk_ref)

            qn, qhat, rq = _head_norm(q_ref[...].astype(F32), gq_ref[...], masks)
            kn, khat, rk = _head_norm(k_ref[...].astype(F32), gk_ref[...], masks)
            qs_s[...] = (qn * ATTN_SCALE).astype(BF16)
            kn_s[...] = kn.astype(BF16)
            dkn_s[...] = jnp.zeros_like(dkn_s)
            dv_s[...] = jnp.zeros_like(dv_s)
            for r0, k0, width, c0 in _attn_windows(seq):
                qb = qs_s[r0:r0 + Q_BLOCK, :]
                dob = do_ref[r0:r0 + Q_BLOCK, :]
                kw = kn_s[k0:k0 + width, :]
                vw = v_ref[k0:k0 + width, :]
                dq_acc = jnp.zeros((Q_BLOCK, LANES), F32)
                dk_acc = jnp.zeros((width, LANES), F32)
                dv_acc = jnp.zeros((width, LANES), F32)
                for h, msk in enumerate(masks):
                    qh = jnp.where(msk, qb, jnp.zeros_like(qb))
                    doh = jnp.where(msk, dob, jnp.zeros_like(dob))
                    p = _softmax_rows(_dot(qh, kw, 1, 1) + b_ref[h, :, c0:KEY_WIN])
                    dp = _dot(doh, vw, 1, 1)
                    ds = p * (dp - jnp.sum(p * dp, axis=-1, keepdims=True))
                    db_ref[h, :, c0:KEY_WIN] += ds
                    dsb = ds.astype(BF16)
                    dq_acc = jnp.where(msk, _dot(dsb, kw, 1, 0), dq_acc)
                    dk_acc = jnp.where(msk, _dot(dsb, qb, 0, 0), dk_acc)
                    dv_acc = jnp.where(msk, _dot(p.astype(BF16), dob, 0, 0), dv_acc)
                dqn_s[r0:r0 + Q_BLOCK, :] = dq_acc * ATTN_SCALE
                dkn_s[k0:k0 + width, :] += dk_acc
                dv_s[k0:k0 + width, :] += dv_acc
            dq, dgq = _head_norm_bwd(dqn_s[...], qhat, rq, gq_ref[...], masks)
            dk, dgk = _head_norm_bwd(dkn_s[...], khat, rk, gk_ref[...], masks)
            dz_ref[...] = dq.astype(dz_ref.dtype)
            dk_o[...] = dk.astype(dk_o.dtype)
            dv_o[...] = dv_s[...].astype(dv_o.dtype)
            dgq_ref[...] += dgq
            dgk_ref[...] += dgk

        @pl.when(which == 1)
        def _():
            dz_ref[...] = dk_o[...]

        @pl.when(which == 2)
        def _():
            dz_ref[...] = dv_o[...]

    def ahead(p, b, t):
        nb = b + jnp.where(t > 0, 1, 0)
        wrap = jnp.where(nb >= batch, 1, 0)
        return jnp.minimum(p + wrap, pairs - 1), nb - wrap * batch

    def col(off):
        def index(p, b, t):
            np_, nb = ahead(p, b, t)
            return nb, off + np_
        return pl.BlockSpec((seq, LANES), index)

    vec = pl.BlockSpec((1, LANES), lambda p, b, t: (0, 0))
    blk = pl.BlockSpec((2, Q_BLOCK, KEY_WIN), lambda p, b, t: (p, 0, 0))
    blk_in = pl.BlockSpec((2, Q_BLOCK, KEY_WIN), lambda p, b, t: (ahead(p, b, t)[0], 0, 0))
    v_shape = jax.ShapeDtypeStruct((1, LANES), F32)
    return pl.pallas_call(
        body,
        out_shape=(jax.ShapeDtypeStruct(z.shape, BF16), jax.ShapeDtypeStruct(bias.shape, F32), v_shape, v_shape),
        grid=(pairs, batch, 3),
        in_specs=[col(0), col(pairs), col(2 * pairs), col(0), vec, vec, blk_in],
        out_specs=(pl.BlockSpec((seq, LANES), lambda p, b, t: (b, t * pairs + p)), blk, vec, vec),
        scratch_shapes=[pltpu.VMEM((seq, LANES), BF16), pltpu.VMEM((seq, LANES), BF16),
                        pltpu.VMEM((seq, LANES), F32), pltpu.VMEM((seq, LANES), F32), pltpu.VMEM((seq, LANES), F32),
                        pltpu.VMEM((seq, LANES), BF16), pltpu.VMEM((seq, LANES), BF16)],
        compiler_params=_params("arbitrary", "arbitrary", "arbitrary"), name="attn_bwd")(
            z, z, z, dcat, gq2, gk2, bias)


MEM_ROWS = 512


def memattn_fwd(z, kv, gq4, gk4, cat, batch, seq, qcol, name):
    mtok = kv.shape[0] // batch
    rows = min(MEM_ROWS, seq)

    def body(q_ref, kv_ref, gq_ref, gk_ref, cat_ref, o_ref):
        del cat_ref
        masks = _group_masks(MEM_WIDTH)
        kn = _head_norm(kv_ref[:, 0:MEM_WIDTH], gk_ref[...], masks)[0].astype(BF16)
        vm = kv_ref[:, MEM_WIDTH:2 * MEM_WIDTH].astype(BF16)
        for t in range(seq // rows):
            sl = slice(t * rows, (t + 1) * rows)
            qs = (_head_norm(q_ref[sl, :].astype(F32), gq_ref[...], masks)[0] * ATTN_SCALE).astype(BF16)
            out = jnp.zeros((rows, MEM_WIDTH), F32)
            for msk in masks:
                qh = jnp.where(msk, qs, jnp.zeros_like(qs))
                p = _softmax_rows(_dot(qh, kn, 1, 1)).astype(BF16)
                out = jnp.where(msk, _dot(p, vm, 1, 0), out)
            o_ref[sl, :] = out.astype(o_ref.dtype)

    vec = pl.BlockSpec((1, MEM_WIDTH), lambda b: (0, 0))
    return pl.pallas_call(
        body, out_shape=jax.ShapeDtypeStruct(cat.shape, cat.dtype), grid=(batch,),
        in_specs=[pl.BlockSpec((seq, MEM_WIDTH), lambda b: (b, qcol)),
                  pl.BlockSpec((mtok, 2 * MEM_WIDTH), lambda b: (b, 0)), vec, vec, ANY],
        out_specs=pl.BlockSpec((seq, MEM_WIDTH), lambda b: (b, TOK_WIDTH // MEM_WIDTH)),
        input_output_aliases={4: 0},
        compiler_params=_params("parallel"), name=name)(z, kv, gq4, gk4, cat)


def memattn_bwd(z, kv, dcat, gq4, gk4, dz, batch, seq, qcol, name):
    mtok = kv.shape[0] // batch
    rows = min(MEM_ROWS, seq)

    def body(q_ref, kv_ref, do_ref, gq_ref, gk_ref, dz_in_ref, dq_ref, dkv_ref, dgq_ref, dgk_ref):
        del dz_in_ref
        @pl.when(pl.program_id(0) == 0)
        def _():
            dgq_ref[...] = jnp.zeros_like(dgq_ref)
            dgk_ref[...] = jnp.zeros_like(dgk_ref)

        masks = _group_masks(MEM_WIDTH)
        kn_f, khat, rk = _head_norm(kv_ref[:, 0:MEM_WIDTH], gk_ref[...], masks)
        kn = kn_f.astype(BF16)
        vm = kv_ref[:, MEM_WIDTH:2 * MEM_WIDTH].astype(BF16)
        dkn = jnp.zeros((mtok, MEM_WIDTH), F32)
        dvm = jnp.zeros((mtok, MEM_WIDTH), F32)
        dgq = jnp.zeros((1, MEM_WIDTH), F32)
        for t in range(seq // rows):
            sl = slice(t * rows, (t + 1) * rows)
            qn_f, qhat, rq = _head_norm(q_ref[sl, :].astype(F32), gq_ref[...], masks)
            qs = (qn_f * ATTN_SCALE).astype(BF16)
            dob = do_ref[sl, :]
            dqn = jnp.zeros((rows, MEM_WIDTH), F32)
            for msk in masks:
                qh = jnp.where(msk, qs, jnp.zeros_like(qs))
                doh = jnp.where(msk, dob, jnp.zeros_like(dob))
                p = _softmax_rows(_dot(qh, kn, 1, 1))
                dp = _dot(doh, vm, 1, 1)
                ds = p * (dp - jnp.sum(p * dp, axis=-1, keepdims=True))
                dsb = ds.astype(BF16)
                dqn = jnp.where(msk, _dot(dsb, kn, 1, 0), dqn)
                dkn = dkn + jnp.where(msk, _dot(dsb, qs, 0, 0), 0.0)
                dvm = dvm + jnp.where(msk, _dot(p.astype(BF16), dob, 0, 0), 0.0)
            dq, dg = _head_norm_bwd(dqn * ATTN_SCALE, qhat, rq, gq_ref[...], masks)
            dq_ref[sl, :] = dq.astype(dq_ref.dtype)
            dgq = dgq + dg
        dk, dgk = _head_norm_bwd(dkn, khat, rk, gk_ref[...], masks)
        dkv_ref[:, 0:MEM_WIDTH] = dk
        dkv_ref[:, MEM_WIDTH:2 * MEM_WIDTH] = dvm
        dgq_ref[...] += dgq
        dgk_ref[...] += dgk

    vec = pl.BlockSpec((1, MEM_WIDTH), lambda b: (0, 0))
    kv_spec = pl.BlockSpec((mtok, 2 * MEM_WIDTH), lambda b: (b, 0))
    v_shape = jax.ShapeDtypeStruct((1, MEM_WIDTH), F32)
    q_spec = pl.BlockSpec((seq, MEM_WIDTH), lambda b: (b, qcol))
    return pl.pallas_call(
        body,
        out_shape=(jax.ShapeDtypeStruct(dz.shape, dz.dtype), jax.ShapeDtypeStruct(kv.shape, F32), v_shape, v_shape),
        grid=(batch,),
        in_specs=[q_spec, kv_spec, pl.BlockSpec((seq, MEM_WIDTH), lambda b: (b, TOK_WIDTH // MEM_WIDTH)), vec, vec, ANY],
        out_specs=(q_spec, kv_spec, vec, vec),
        input_output_aliases={5: 0},
        compiler_params=_params("arbitrary"), name=name)(z, kv, dcat, gq4, gk4, dz)


CONV_ROWS = 256


def _glu(a_ref, g_ref):
    return a_ref[...].astype(F32) * _sigmoid(g_ref[...].astype(F32))


def _layer_norm_stats(y):
    mu = jnp.mean(y, axis=-1, keepdims=True)
    yc = y - mu
    rstd = lax.rsqrt(jnp.mean(yc * yc, axis=-1, keepdims=True) + NORM_EPS)
    return yc * rstd, rstd


CONV_WIN = CONV_HALO + CONV_ROWS
SUBLANES = 8
SHIFT_ROWS = CONV_WIN - SUBLANES


def _preshift(win, shifted):
    for s in range(1, SUBLANES):
        shifted[s - 1, :, :] = win[s:s + SHIFT_ROWS, :]


TAP_ROWS = 64
TAP_TILES = [(r0, slice(c0, c0 + LANES)) for c0 in range(0, TOK_WIDTH, LANES) for r0 in range(0, CONV_ROWS, TAP_ROWS)]


def _tap(win, shifted, off, r0, lanes):
    s = off % SUBLANES
    base = off - s + r0
    if s == 0:
        return win[base:base + TAP_ROWS, lanes]
    return shifted[s - 1, base:base + TAP_ROWS, lanes]


def _fold_rows(x):
    return jnp.sum(x.reshape(TAP_ROWS // SUBLANES, SUBLANES, LANES), axis=0)


def conv_fwd(z, cw, cb, lg, lb, batch, seq):
    n = z.shape[0]
    nt = seq // CONV_ROWS
    sub = CONV_ROWS // CONV_HALO
    lead = CONV_HALO - (CONV_W - 1)

    def body(a_ref, g_ref, ap_ref, gp_ref, cw_ref, cb_ref, lg_ref, lb_ref, o_ref, y_ref, win, shifted):
        first = pl.program_id(1) == 0
        win[0:CONV_HALO, :] = jnp.where(first, 0.0, _glu(ap_ref, gp_ref))
        win[CONV_HALO:CONV_WIN, :] = _glu(a_ref, g_ref)
        _preshift(win, shifted)
        for r0, lanes in TAP_TILES:
            acc = jnp.zeros((TAP_ROWS, LANES), F32) + cb_ref[:, lanes]
            for w in range(CONV_W):
                acc = acc + _tap(win, shifted, lead + w, r0, lanes) * cw_ref[w:w + 1, lanes]
            y_ref[r0:r0 + TAP_ROWS, lanes] = acc
        yh, _ = _layer_norm_stats(y_ref[...])
        t = yh * lg_ref[...] + lb_ref[...]
        o_ref[...] = (t * _sigmoid(t)).astype(o_ref.dtype)

    def cur(c):
        return pl.BlockSpec((CONV_ROWS, TOK_WIDTH), lambda b, i: (b * nt + i, c))

    def prev(c):
        return pl.BlockSpec((CONV_HALO, TOK_WIDTH), lambda b, i: (jnp.maximum((b * nt + i) * sub - 1, 0), c))

    vec = pl.BlockSpec((1, TOK_WIDTH), lambda b, i: (0, 0))
    return pl.pallas_call(
        body, out_shape=(jax.ShapeDtypeStruct((n, D_MODEL), BF16), jax.ShapeDtypeStruct((n, TOK_WIDTH), F32)),
        grid=(batch, nt),
        in_specs=[cur(0), cur(1), prev(0), prev(1), pl.BlockSpec((32, TOK_WIDTH), lambda b, i: (0, 0)), vec, vec, vec],
        out_specs=(cur(0), cur(0)),
        scratch_shapes=[pltpu.VMEM((CONV_WIN, TOK_WIDTH), F32), pltpu.VMEM((SUBLANES - 1, SHIFT_ROWS, TOK_WIDTH), F32)],
        compiler_params=_params("parallel", "arbitrary"), name="conv_fwd")(z, z, z, z, cw, cb, lg, lb)


def conv_bwd(z, y, dcat, cw, lg, lb, batch, seq):
    n = z.shape[0]
    nt = seq // CONV_ROWS
    sub = CONV_ROWS // CONV_HALO
    lead = CONV_HALO - (CONV_W - 1)
    last_blk = n // CONV_HALO - 1

    def body(a_ref, g_ref, ap_ref, gp_ref, y_ref, yn_ref, do_ref, don_ref, cw_ref, lg_ref, lb_ref,
             dz_ref, dcw_ref, dsm_ref, win, shifted, dyw, dshifted, dg_o):
        b, i, which = pl.program_id(0), pl.program_id(1), pl.program_id(2)

        @pl.when(which == 0)
        def _():
            first, last = i == 0, i == nt - 1

            @pl.when((b == 0) & (i == 0))
            def _():
                dcw_ref[...] = jnp.zeros_like(dcw_ref)
                dsm_ref[...] = jnp.zeros_like(dsm_ref)

            win[0:CONV_HALO, :] = jnp.where(first, 0.0, _glu(ap_ref, gp_ref))
            win[CONV_HALO:CONV_WIN, :] = _glu(a_ref, g_ref)
            _preshift(win, shifted)
            yv = jnp.concatenate([y_ref[...], yn_ref[...]], axis=0)
            yh, rstd = _layer_norm_stats(yv)
            t = yh * lg_ref[...] + lb_ref[...]
            st = _sigmoid(t)
            dout = jnp.concatenate(
                [do_ref[...].astype(F32), jnp.where(last, 0.0, don_ref[...].astype(F32))], axis=0)
            dt = dout * st * (1.0 + t * (1.0 - st))
            dyh = dt * lg_ref[...]
            dy = rstd * (dyh - jnp.mean(dyh, axis=-1, keepdims=True)
                         - yh * jnp.mean(dyh * yh, axis=-1, keepdims=True))
            dyw[...] = dy
            _preshift(dyw, dshifted)
            dsm_ref[0:1, :] += jnp.sum(dy[0:CONV_ROWS], axis=0, keepdims=True)
            dsm_ref[1:2, :] += jnp.sum((dt * yh)[0:CONV_ROWS], axis=0, keepdims=True)
            dsm_ref[2:3, :] += jnp.sum(dt[0:CONV_ROWS], axis=0, keepdims=True)
            for c0 in range(0, TOK_WIDTH, LANES):
                lanes = slice(c0, c0 + LANES)
                dcw_acc = [jnp.zeros((SUBLANES, LANES), F32) for _ in range(CONV_W)]
                for r0 in range(0, CONV_ROWS, TAP_ROWS):
                    dyt = dyw[r0:r0 + TAP_ROWS, lanes]
                    dglu = jnp.zeros((TAP_ROWS, LANES), F32)
                    for w in range(CONV_W):
                        dcw_acc[w] = dcw_acc[w] + _fold_rows(dyt * _tap(win, shifted, lead + w, r0, lanes))
                        dglu = dglu + _tap(dyw, dshifted, CONV_W - 1 - w, r0, lanes) * cw_ref[w:w + 1, lanes]
                    avt = a_ref[r0:r0 + TAP_ROWS, lanes].astype(F32)
                    sgt = _sigmoid(g_ref[r0:r0 + TAP_ROWS, lanes].astype(F32))
                    dz_ref[r0:r0 + TAP_ROWS, lanes] = (dglu * sgt).astype(dz_ref.dtype)
                    dg_o[r0:r0 + TAP_ROWS, lanes] = (dglu * avt * sgt * (1.0 - sgt)).astype(dg_o.dtype)
                for w in range(CONV_W):
                    dcw_ref[w:w + 1, lanes] += jnp.sum(dcw_acc[w], axis=0, keepdims=True)

        @pl.when(which == 1)
        def _():
            dz_ref[...] = dg_o[...]

    def ahead(b, i, t):
        return jnp.minimum(b * nt + i + t, batch * nt - 1)

    def cur(c):
        return pl.BlockSpec((CONV_ROWS, TOK_WIDTH), lambda b, i, t: (ahead(b, i, t), c))

    def prev(c):
        return pl.BlockSpec((CONV_HALO, TOK_WIDTH), lambda b, i, t: (jnp.maximum(ahead(b, i, t) * sub - 1, 0), c))

    nxt = pl.BlockSpec((CONV_HALO, TOK_WIDTH),
                       lambda b, i, t: (jnp.minimum((ahead(b, i, t) + 1) * sub, last_blk), 0))
    vec = pl.BlockSpec((1, TOK_WIDTH), lambda b, i, t: (0, 0))
    full32 = pl.BlockSpec((32, TOK_WIDTH), lambda b, i, t: (0, 0))
    return pl.pallas_call(
        body,
        out_shape=(jax.ShapeDtypeStruct(z.shape, BF16), jax.ShapeDtypeStruct((32, TOK_WIDTH), F32),
                   jax.ShapeDtypeStruct((8, TOK_WIDTH), F32)),
        grid=(batch, nt, 2),
        in_specs=[cur(0), cur(1), prev(0), prev(1), cur(0), nxt, cur(0), nxt, full32, vec, vec],
        out_specs=(pl.BlockSpec((CONV_ROWS, TOK_WIDTH), lambda b, i, t: (b * nt + i, t)), full32,
                   pl.BlockSpec((8, TOK_WIDTH), lambda b, i, t: (0, 0))),
        scratch_shapes=[pltpu.VMEM((CONV_WIN, TOK_WIDTH), F32), pltpu.VMEM((SUBLANES - 1, SHIFT_ROWS, TOK_WIDTH), F32),
                        pltpu.VMEM((CONV_WIN, TOK_WIDTH), F32), pltpu.VMEM((SUBLANES - 1, SHIFT_ROWS, TOK_WIDTH), F32),
                        pltpu.VMEM((CONV_ROWS, TOK_WIDTH), BF16)],
        compiler_params=_params("arbitrary", "arbitrary", "arbitrary"), name="conv_bwd")(
            z, z, z, z, y, y, dcat, dcat, cw, lg, lb)


def loss_head(y, target):
    n, d = y.shape
    tm = _row_tile(n)
    nt = n // tm

    def body(y_ref, t_ref, dy_ref, dyb_ref, l_ref, acc_ref):
        i = pl.program_id(0)

        @pl.when(i == 0)
        def _():
            acc_ref[...] = jnp.zeros_like(acc_ref)

        err = y_ref[...] - t_ref[...]
        dy = err * (1.0 / d)
        dy_ref[...] = dy
        dyb_ref[...] = dy.astype(BF16)
        acc_ref[...] += jnp.sum(err * err, axis=0, keepdims=True)

        @pl.when(i == nt - 1)
        def _():
            total = jnp.sum(acc_ref[...], axis=-1, keepdims=True) * (0.5 / d)
            l_ref[...] = jnp.broadcast_to(total, l_ref.shape)

    row = pl.BlockSpec((tm, d), lambda i: (i, 0))
    return pl.pallas_call(
        body, out_shape=(jax.ShapeDtypeStruct((n, d), F32), jax.ShapeDtypeStruct((n, d), BF16),
                         jax.ShapeDtypeStruct((8, LANES), F32)), grid=(nt,),
        in_specs=[row, row], out_specs=(row, row, pl.BlockSpec((8, LANES), lambda i: (0, 0))),
        scratch_shapes=[pltpu.VMEM((1, d), F32)],
        compiler_params=_params("arbitrary"), name="loss_head")(y, target)


def col_sum(x, name="col_sum"):
    n, c = x.shape
    tm = _row_tile(n)

    def body(x_ref, o_ref):
        @pl.when(pl.program_id(0) == 0)
        def _():
            o_ref[...] = jnp.zeros_like(o_ref)

        o_ref[...] += jnp.sum(x_ref[...].astype(F32), axis=0, keepdims=True)

    return pl.pallas_call(
        body, out_shape=jax.ShapeDtypeStruct((1, c), F32), grid=(n // tm,),
        in_specs=[pl.BlockSpec((tm, c), lambda i: (i, 0))], out_specs=pl.BlockSpec((1, c), lambda i: (0, 0)),
        compiler_params=_params("arbitrary"), name=name)(x)


def adamw(w, g, m, v, name="adamw"):
    rows, cols = w.shape
    tr = rows
    for cand in (512, 256, 128, 64, 32, 16, 8):
        if rows % cand == 0 and rows > cand:
            tr = cand
            break
    c1 = 1.0 / (1.0 - ADAM_B1 ** ADAM_STEP)
    c2 = 1.0 / (1.0 - ADAM_B2 ** ADAM_STEP)

    def body(w_ref, g_ref, m_ref, v_ref, d_ref, nm_ref, nv_ref):
        gv = g_ref[...]
        nm = ADAM_B1 * m_ref[...] + (1.0 - ADAM_B1) * gv
        nv = ADAM_B2 * v_ref[...] + (1.0 - ADAM_B2) * (gv * gv)
        nm_ref[...] = nm
        nv_ref[...] = nv
        d_ref[...] = -ADAM_LR * ((nm * c1) / (jnp.sqrt(nv * c2) + ADAM_EPS) + ADAM_WD * w_ref[...])

    spec = pl.BlockSpec((tr, cols), lambda i: (i, 0))
    shape = jax.ShapeDtypeStruct((rows, cols), F32)
    return pl.pallas_call(
        body, out_shape=(shape, shape, shape), grid=(rows // tr,),
        in_specs=[spec, spec, spec, spec], out_specs=(spec, spec, spec),
        compiler_params=_params("parallel"), name=name)(w, g, m, v)


def _place():
    return lax.axis_index("x"), lax.axis_index("y"), lax.axis_index("c")


def _other_chips(x, y):
    return [(1 - x, y), (x, 1 - y), (1 - x, 1 - y)]


def small_exchange(slab, reduce):
    r = slab.shape[0]

    def body(in_ref, o_ref, *scratch):
        if reduce:
            buf, send_sems, recv_sems = scratch
        else:
            buf = o_ref
            send_sems, recv_sems = scratch
        x, y, c = _place()
        me = 4 * x + 2 * y + c
        buf[me] = in_ref[...]
        copies = []
        for k in range(1, N_DEV):
            peer = (x ^ (k >> 2), y ^ ((k >> 1) & 1), c ^ (k & 1))
            cp = pltpu.make_async_remote_copy(
                src_ref=in_ref, dst_ref=buf.at[me], send_sem=send_sems.at[k - 1], recv_sem=recv_sems.at[k - 1],
                device_id=peer, device_id_type=MESH)
            cp.start()
            copies.append(cp)
        for k in range(1, N_DEV):
            src = 4 * (x ^ (k >> 2)) + 2 * (y ^ ((k >> 1) & 1)) + (c ^ (k & 1))
            pltpu.make_async_remote_copy(
                src_ref=in_ref, dst_ref=buf.at[src], send_sem=send_sems.at[k - 1], recv_sem=recv_sems.at[k - 1],
                device_id=(x, y, c), device_id_type=MESH).wait_recv()
        for cp in copies:
            cp.wait_send()
        if reduce:
            total = buf[0]
            for d in range(1, N_DEV):
                total = total + buf[d]
            o_ref[...] = total

    sems = [pltpu.SemaphoreType.DMA((N_DEV - 1,)), pltpu.SemaphoreType.DMA((N_DEV - 1,))]
    if reduce:
        out_shape = jax.ShapeDtypeStruct((r, LANES), F32)
        scratch = [pltpu.VMEM((N_DEV, r, LANES), F32)] + sems
    else:
        out_shape = jax.ShapeDtypeStruct((N_DEV, r, LANES), F32)
        scratch = sems
    vmem = pl.BlockSpec(memory_space=pltpu.VMEM)
    return pl.pallas_call(
        body, out_shape=out_shape, in_specs=[vmem], out_specs=vmem, scratch_shapes=scratch,
        compiler_params=pltpu.CompilerParams(vmem_limit_bytes=VMEM_LIMIT),
        name="small_reduce" if reduce else "small_gather")(slab)


def gather_weights(shards, name, collective_id):
    nw = len(shards)
    ns = [s.shape[0] for s in shards]
    in_refs = [jax.new_ref(s, memory_space=pltpu.MemorySpace.HBM) for s in shards]
    out_refs = [jax.empty_ref(jax.ShapeDtypeStruct((N_DEV * s.shape[0], s.shape[1]), s.dtype),
                              memory_space=pltpu.MemorySpace.HBM) for s in shards]

    @pl.kernel(mesh=plsc.ScalarSubcoreMesh(axis_name="seq", num_cores=1), name=name,
               scratch_types=(pltpu.SemaphoreType.DMA((nw, 7)), pltpu.SemaphoreType.DMA((nw, 7)),
                              pltpu.SemaphoreType.DMA((nw,))),
               compiler_params=pltpu.CompilerParams(collective_id=collective_id))
    def launch(send_sems, recv_sems, local_sems):
        x, y, c = _place()
        me, sib = (x, y, c), (x, y, 1 - c)
        chips = _other_chips(x, y)
        barrier = pltpu.get_barrier_semaphore()
        for peer in [sib] + [(*chip, c) for chip in chips]:
            pl.semaphore_signal(barrier, inc=1, device_id=peer, device_id_type=MESH)
        pl.semaphore_wait(barrier, 4)

        def rows(w, dev):
            return out_refs[w].at[pl.ds((4 * dev[0] + 2 * dev[1] + dev[2]) * ns[w], ns[w]), :]

        def copy(w, k, block, to, src=None):
            return pltpu.make_async_remote_copy(
                src_ref=rows(w, block) if src is None else src, dst_ref=rows(w, block),
                send_sem=send_sems.at[w, k], recv_sem=recv_sems.at[w, k], device_id=to, device_id_type=MESH)

        started, sends = [], []
        for w in range(nw):
            mine = pltpu.make_async_copy(in_refs[w], rows(w, me), local_sems.at[w])
            mine.start()
            started.append(mine)
            first = [copy(w, 0, me, sib, src=in_refs[w])]
            first += [copy(w, 1 + j, me, (*chip, c), src=in_refs[w]) for j, chip in enumerate(chips)]
            for cp in first:
                cp.start()
            sends += first
        for w in range(nw):
            for j, chip in enumerate(chips):
                copy(w, 1 + j, (*chip, c), me).wait_recv()
                fwd = copy(w, 4 + j, (*chip, c), sib)
                fwd.start()
                sends.append(fwd)
        for w in range(nw):
            copy(w, 0, sib, me).wait_recv()
            for j, chip in enumerate(chips):
                copy(w, 4 + j, (*chip, 1 - c), me).wait_recv()
        for cp in sends:
            cp.wait_send()
        for mine in started:
            mine.wait()

    launch()
    return [r[...] for r in out_refs]


def _sequencer_exchange(sources, out_rows, peers_of, copies_of, name, collective_id):
    nw = len(sources)
    in_refs = [jax.new_ref(s, memory_space=pltpu.MemorySpace.HBM) for s in sources]
    out_refs = [jax.empty_ref(jax.ShapeDtypeStruct((rows, s.shape[1]), s.dtype), memory_space=pltpu.MemorySpace.HBM)
                for rows, s in zip(out_rows, sources)]
    per = len(copies_of(0, 0, 0, 0))

    @pl.kernel(mesh=plsc.ScalarSubcoreMesh(axis_name="seq", num_cores=1), name=name,
               scratch_types=(pltpu.SemaphoreType.DMA((nw, per)), pltpu.SemaphoreType.DMA((nw, per))),
               compiler_params=pltpu.CompilerParams(collective_id=collective_id))
    def launch(send_sems, recv_sems):
        x, y, c = _place()
        peers = peers_of(x, y, c)
        barrier = pltpu.get_barrier_semaphore()
        for peer in peers:
            pl.semaphore_signal(barrier, inc=1, device_id=peer, device_id_type=MESH)
        pl.semaphore_wait(barrier, len(peers))
        copies = []
        for w in range(nw):
            for k, (src_blk, dst_blk, rows, peer) in enumerate(copies_of(x, y, c, w)):
                cp = pltpu.make_async_remote_copy(
                    src_ref=in_refs[w].at[pl.ds(src_blk * rows, rows), :],
                    dst_ref=out_refs[w].at[pl.ds(dst_blk * rows, rows), :],
                    send_sem=send_sems.at[w, k], recv_sem=recv_sems.at[w, k], device_id=peer, device_id_type=MESH)
                cp.start()
                copies.append(cp)
        for cp in copies:
            cp.wait_recv()
        for cp in copies:
            cp.wait_send()

    launch()
    return [r[...] for r in out_refs]


def scatter_to_sibling(grads, name, collective_id):
    ns = [g.shape[0] // N_DEV for g in grads]
    return _sequencer_exchange(
        grads, [4 * n for n in ns],
        lambda x, y, c: [(x, y, 1 - c)],
        lambda x, y, c, w: [(2 * q + 1 - c, q, ns[w], (x, y, 1 - c)) for q in range(4)],
        name, collective_id)


def scatter_to_chips(parts, name, collective_id):
    ns = [p.shape[0] // 4 for p in parts]
    return _sequencer_exchange(
        parts, [3 * n for n in ns],
        lambda x, y, c: [(*chip, c) for chip in _other_chips(x, y)],
        lambda x, y, c, w: [(2 * chip[0] + chip[1], j, ns[w], (*chip, c)) for j, chip in enumerate(_other_chips(x, y))],
        name, collective_id)


def add_sibling(grad, landed, core, name):
    n = landed.shape[0] // 4
    cols = grad.shape[1]

    def body(c_ref, g_ref, l_ref, o_ref):
        o_ref[...] = (g_ref[...].astype(F32) + l_ref[...].astype(F32)).astype(o_ref.dtype)

    grid_spec = pltpu.PrefetchScalarGridSpec(
        num_scalar_prefetch=1, grid=(4,),
        in_specs=[pl.BlockSpec((n, cols), lambda q, c_ref: (2 * q + c_ref[0], 0)),
                  pl.BlockSpec((n, cols), lambda q, c_ref: (q, 0))],
        out_specs=pl.BlockSpec((n, cols), lambda q, c_ref: (q, 0)))
    return pl.pallas_call(
        body, out_shape=jax.ShapeDtypeStruct(landed.shape, landed.dtype), grid_spec=grid_spec,
        compiler_params=_params("arbitrary"), name=name)(core, grad, landed)


def adamw_shard(layer, w, m, v, part, landed, chip, earlier, name):
    n = landed.shape[0] // 3
    cols = w.shape[1]
    c1 = 1.0 / (1.0 - ADAM_B1 ** ADAM_STEP)
    c2 = 1.0 / (1.0 - ADAM_B2 ** ADAM_STEP)

    def body(q_ref, w_ref, m_ref, v_ref, p_ref, l0_ref, l1_ref, l2_ref, *rest):
        g_ref, d_ref, nm_ref, nv_ref = rest[-4:]
        gv = ((p_ref[...].astype(F32) + l0_ref[...].astype(F32)) + l1_ref[...].astype(F32)) + l2_ref[...].astype(F32)
        nm = ADAM_B1 * m_ref[...] + (1.0 - ADAM_B1) * gv
        nv = ADAM_B2 * v_ref[...] + (1.0 - ADAM_B2) * (gv * gv)
        g_ref[...] = gv
        nm_ref[...] = nm
        nv_ref[...] = nv
        d_ref[...] = -ADAM_LR * ((nm * c1) / (jnp.sqrt(nv * c2) + ADAM_EPS) + ADAM_WD * w_ref[...])

    own = pl.BlockSpec((n, cols), lambda i, q_ref: (layer, 0))

    def landed_spec(j):
        return pl.BlockSpec((n, cols), lambda i, q_ref: (j, 0))

    in_specs = [own, own, own, pl.BlockSpec((n, cols), lambda i, q_ref: (q_ref[0], 0)),
                landed_spec(0), landed_spec(1), landed_spec(2)]
    args = [chip, w, m, v, part, landed, landed, landed]
    aliases = {}
    if earlier is not None:
        in_specs += [ANY] * 4
        args += list(earlier)
        aliases = {8 + k: k for k in range(4)}
    grid_spec = pltpu.PrefetchScalarGridSpec(
        num_scalar_prefetch=1, grid=(1,), in_specs=in_specs, out_specs=(own, own, own, own))
    shape = jax.ShapeDtypeStruct(w.shape, F32)
    return pl.pallas_call(
        body, out_shape=(shape, shape, shape, shape), grid_spec=grid_spec, input_output_aliases=aliases,
        compiler_params=_params("arbitrary"), name=name)(*args)


def _pack(arrays):
    flat = jnp.concatenate([a.reshape(-1).astype(F32) for a in arrays])
    pad = (-flat.shape[0]) % (8 * LANES)
    return jnp.pad(flat, (0, pad)).reshape(-1, LANES)


def _unpack(slab, shapes):
    flat = slab.reshape(slab.shape[:-2] + (-1,))
    out, off = [], 0
    for shp in shapes:
        size = 1
        for s in shp:
            size *= s
        out.append(flat[..., off:off + size].reshape(flat.shape[:-1] + tuple(shp)))
        off += size
    return out


def kernel(x, mem, norm1_g, mem_norm_g, a_w_in, a_q_g, a_k_g, a_rel_bias, b_w_in, b_b_in, b_conv_w, b_conv_b, b_ln_g, b_ln_b, mq_g, mk_g, w_mem_kv, w_out, norm2_g, w_gate, w_up, w_down, loss_target, m_norm1_g, m_mem_norm_g, m_a_w_in, m_a_q_g, m_a_k_g, m_a_rel_bias, m_b_w_in, m_b_b_in, m_b_conv_w, m_b_conv_b, m_b_ln_g, m_b_ln_b, m_mq_g, m_mk_g, m_w_mem_kv, m_w_out, m_norm2_g, m_w_gate, m_w_up, m_w_down, v_norm1_g, v_mem_norm_g, v_a_w_in, v_a_q_g, v_a_k_g, v_a_rel_bias, v_b_w_in, v_b_b_in, v_b_conv_w, v_b_conv_b, v_b_ln_g, v_b_ln_b, v_mq_g, v_mk_g, v_w_mem_kv, v_w_out, v_norm2_g, v_w_gate, v_w_up, v_w_down):
    batch, seq, d = x.shape
    mtok = mem.shape[1]
    n = batch * seq
    ax, ay, ac = _place()
    me = 4 * ax + 2 * ay + ac
    core_arr = jnp.reshape(ac, (1,)).astype(jnp.int32)
    chip_arr = jnp.reshape(2 * ax + ay, (1,)).astype(jnp.int32)

    def t_bf16(w):
        return jnp.transpose(w).astype(BF16)

    def after(value, *earlier):
        return lax.optimization_barrier((value, *earlier))[0]

    def gather_mix(l, when, name, collective_id):
        srcs = [w_mem_kv[l].astype(BF16), w_out[l].astype(BF16)] + ([t_bf16(b_w_in[0])] if l == 1 else [])
        return gather_weights([after(srcs[0], when)] + srcs[1:], name, collective_id)

    def gather_ffn(l, when, name, collective_id):
        return gather_weights(
            [after(t_bf16(w_gate[l]), when), t_bf16(w_up[l]), w_down[l].astype(BF16)], name, collective_id)

    f_loc = b_b_in.shape[1]
    c_loc = b_conv_b.shape[1]

    def two(g):
        return jnp.concatenate([g, g], axis=-1)

    gq2, gk2 = two(a_q_g), two(a_k_g)
    rel16 = jnp.pad(a_rel_bias[0], ((0, 16 - a_rel_bias.shape[1]), (0, 0)))
    bias = bias_blocks(rel16)

    x0 = x.reshape(n, d)
    mem2 = mem.reshape(batch * mtok, d)
    zero_mem = jnp.zeros_like(mem2)

    saved = []
    xin = x0
    a_win_t, = gather_weights([t_bf16(a_w_in[0])], "gather_in_a", 1)
    wg_t, wu_t, wd, wo, wkv = [None] * 2, [None] * 2, [None] * 2, [None] * 2, [None] * 2
    for l in range(2):
        h = rms_fwd(xin, norm1_g[l:l + 1], name=f"rms1_fwd_{l}")
        mem_n = rms_fwd(mem2, mem_norm_g[l:l + 1], name=f"rms_mem_fwd_{l}")
        gq4 = jnp.tile(mq_g[l:l + 1], (1, 4))
        gk4 = jnp.tile(mk_g[l:l + 1], (1, 4))
        y_conv = None
        if l == 0:
            wkv[0], wo[0] = gather_mix(0, h, "gather_mix_a", 2)
            z = mm_nt(h, a_win_t, name="in_proj_a")
            wg_t[0], wu_t[0], wd[0] = gather_ffn(0, z, "gather_ffn_a", 3)
            cat = attn_fwd(z, gq2, gk2, bias, batch, seq)
            wkv[1], wo[1], b_win_t = gather_mix(1, cat, "gather_mix_b", 4)
            qcol = 3 * TOK_WIDTH // MEM_WIDTH
        else:
            small_shapes = [(f_loc,), (CONV_W, c_loc), (c_loc,), (c_loc,), (c_loc,)]
            gathered = small_exchange(after(_pack([b_b_in, b_conv_w, b_conv_b, b_ln_g, b_ln_b]), xin), reduce=False)
            bb_g, cw_g, cb_g, lg_g, lb_g = _unpack(gathered, small_shapes)
            bb_full = bb_g.reshape(1, -1)
            cw_full = jnp.pad(jnp.transpose(cw_g, (1, 0, 2)).reshape(CONV_W, -1), ((0, 32 - CONV_W), (0, 0)))
            cb_full, lg_full, lb_full = cb_g.reshape(1, -1), lg_g.reshape(1, -1), lb_g.reshape(1, -1)
            z = mm_nt(h, b_win_t, bias=bb_full, name="in_proj_b")
            cat, y_conv = conv_fwd(z, cw_full, cb_full, lg_full, lb_full, batch, seq)
            qcol = 2 * TOK_WIDTH // MEM_WIDTH
        kv = mm_nn(mem_n, wkv[l], name=f"mem_kv_{l}")
        cat = memattn_fwd(z, kv, gq4, gk4, cat, batch, seq, qcol, name=f"memattn_fwd_{l}")
        x1 = mm_nn(cat, wo[l], res=xin, name=f"out_proj_{l}")
        if l == 0:
            wg_t[1], wu_t[1], wd[1] = gather_ffn(1, x1, "gather_ffn_b", 5)
        h2 = rms_fwd(x1, norm2_g[l:l + 1], name=f"rms2_fwd_{l}")
        gate, up, act = gate_up(h2, wg_t[l], wu_t[l], name=f"gate_up_{l}")
        x2 = mm_nn(act, wd[l], res=x1, name=f"down_proj_{l}")
        saved.append(dict(xin=xin, h=h, mem_n=mem_n, kv=kv, gq4=gq4, gk4=gk4, z=z, qcol=qcol, cat=cat, x1=x1, h2=h2,
                          gate=gate, up=up, act=act, y_conv=y_conv))
        xin = x2

    dx, dx_b, loss_blk = loss_head(xin, loss_target.reshape(n, d))
    loss = lax.psum(loss_blk[0, 0], ("x", "y", "c"))

    big = {}
    small = {}
    reduced = {}
    groups = 0

    def scatter_siblings(keys):
        nonlocal groups
        gid = groups
        groups += 1
        return gid, keys, scatter_to_sibling([big[k] for k in keys], f"scatter_sibling_{gid}", 8 + 2 * gid)

    def scatter_chips(stage1, when):
        gid, keys, landed1 = stage1
        parts = [add_sibling(after(big[k], when), ld, core_arr, name=f"add_sibling_{k}") for k, ld in zip(keys, landed1)]
        landed2 = scatter_to_chips(parts, f"scatter_chips_{gid}", 9 + 2 * gid)
        for k, p, ld in zip(keys, parts, landed2):
            reduced[k] = (p, ld)
        return parts

    def rows_of(w, transposed):
        w = jnp.swapaxes(w, 1, 2) if transposed else w
        return w.reshape(w.shape[0] * w.shape[1], w.shape[2])

    sharded = {
        "win0": (2, True), "win1": (6, True), "wkv": (14, False), "wo": (15, False),
        "wg": (17, True), "wu": (18, True), "wd": (19, False)}
    weights = [norm1_g, mem_norm_g, a_w_in, a_q_g, a_k_g, a_rel_bias, b_w_in, b_b_in, b_conv_w, b_conv_b, b_ln_g,
               b_ln_b, mq_g, mk_g, w_mem_kv, w_out, norm2_g, w_gate, w_up, w_down]
    moms = [m_norm1_g, m_mem_norm_g, m_a_w_in, m_a_q_g, m_a_k_g, m_a_rel_bias, m_b_w_in, m_b_b_in, m_b_conv_w,
            m_b_conv_b, m_b_ln_g, m_b_ln_b, m_mq_g, m_mk_g, m_w_mem_kv, m_w_out, m_norm2_g, m_w_gate, m_w_up, m_w_down]
    vels = [v_norm1_g, v_mem_norm_g, v_a_w_in, v_a_q_g, v_a_k_g, v_a_rel_bias, v_b_w_in, v_b_b_in, v_b_conv_w,
            v_b_conv_b, v_b_ln_g, v_b_ln_b, v_mq_g, v_mk_g, v_w_mem_kv, v_w_out, v_norm2_g, v_w_gate, v_w_up, v_w_down]
    updated = {}

    def update_layer(l, when):
        for key, (idx, transposed) in sharded.items():
            if key in ("win0", "win1"):
                if key != f"win{l}":
                    continue
                layer, rkey = 0, key
            else:
                layer, rkey = l, f"{key}{l}"
            part, landed = reduced[rkey]
            updated[key] = adamw_shard(
                layer, after(rows_of(weights[idx], transposed), when), rows_of(moms[idx], transposed),
                rows_of(vels[idx], transposed), part, landed, chip_arr, updated.get(key), name=f"adamw_{rkey}")

    for l in (1, 0):
        sv = saved[l]
        dgate, dup = ffn_bwd_act(dx_b, wd[l], sv["gate"], sv["up"], name=f"ffn_bwd_act_{l}")
        big[f"wd{l}"] = mm_tn(sv["act"], dx_b, name=f"grad_wd_{l}")
        dh2 = mm2_nn(dgate, wg_t[l], dup, wu_t[l], name=f"ffn_bwd_h_{l}")
        big[f"wg{l}"] = mm_tn(dgate, sv["h2"], name=f"grad_wg_{l}")
        big[f"wu{l}"] = mm_tn(dup, sv["h2"], name=f"grad_wu_{l}")
        dx1, dx1_b, small[f"norm2_{l}"] = rms_bwd(dh2, sv["x1"], norm2_g[l:l + 1], dx, name=f"rms2_bwd_{l}")
        big[f"wo{l}"] = mm_tn(sv["cat"], dx1_b, name=f"grad_wo_{l}")
        stage1 = scatter_siblings([f"wd{l}", f"wg{l}", f"wu{l}", f"wo{l}"])
        dcat = mm_nt(dx1_b, wo[l], name=f"out_proj_bwd_{l}")
        dcat = after(dcat, *scatter_chips(stage1, dcat))
        if l == 0:
            dz, dbias, small["a_q"], small["a_k"] = attn_bwd(sv["z"], dcat, gq2, gk2, bias, batch, seq)
            small["rel"] = bias_grad(dbias)
            win_t = a_win_t
            update_layer(1, dz)
        else:
            dz, small["cw"], small["csum"] = conv_bwd(sv["z"], sv["y_conv"], dcat, cw_full, lg_full, lb_full, batch, seq)
            win_t = b_win_t
        dz, dkv, small[f"mq_{l}"], small[f"mk_{l}"] = memattn_bwd(
            sv["z"], sv["kv"], dcat, sv["gq4"], sv["gk4"], dz, batch, seq, sv["qcol"], name=f"memattn_bwd_{l}")
        if l == 1:
            small["bb"] = col_sum(dz, name="grad_b_in")
        big[f"win{l}"] = mm_tn(dz, sv["h"], name=f"grad_win_{l}")
        big[f"wkv{l}"] = mm_tn(sv["mem_n"], dkv, name=f"grad_wkv_{l}")
        stage1 = scatter_siblings([f"win{l}", f"wkv{l}"])
        dh = mm_nn(dz, win_t, name=f"in_proj_bwd_{l}")
        dh = after(dh, *scatter_chips(stage1, dh))
        dmem_n = mm_nt(dkv, wkv[l], out_dtype=F32, name=f"mem_kv_bwd_{l}")
        _, _, small[f"memnorm_{l}"] = rms_bwd(dmem_n, mem2, mem_norm_g[l:l + 1], zero_mem, name=f"rms_mem_bwd_{l}")
        dx, dx_b, small[f"norm1_{l}"] = rms_bwd(dh, sv["xin"], norm1_g[l:l + 1], dx1, name=f"rms1_bwd_{l}")
    grad_x = dx.reshape(batch, seq, d)
    update_layer(0, dx)

    def shaped(rows, idx, transposed):
        shp = weights[idx].shape
        if transposed:
            return jnp.swapaxes(rows.reshape(shp[0], shp[2], shp[1]), 1, 2)
        return rows.reshape(shp)

    def fold(v, groups):
        return jnp.sum(v.reshape(groups, HEAD_DIM), axis=0, keepdims=True)

    heads = a_rel_bias.shape[1]
    small_list = [
        jnp.concatenate([small["norm1_0"], small["norm1_1"]]),
        jnp.concatenate([small["memnorm_0"], small["memnorm_1"]]),
        fold(small["a_q"], 2), fold(small["a_k"], 2), small["rel"][:heads][None],
        small["bb"], small["cw"][:CONV_W][None], small["csum"][0:1], small["csum"][1:2], small["csum"][2:3],
        jnp.concatenate([fold(small["mq_0"], 4), fold(small["mq_1"], 4)]),
        jnp.concatenate([fold(small["mk_0"], 4), fold(small["mk_1"], 4)]),
        jnp.concatenate([small["norm2_0"], small["norm2_1"]]),
    ]
    small_full_shapes = [a.shape for a in small_list]
    summed = _unpack(small_exchange(_pack(small_list), reduce=True), small_full_shapes)
    (g_norm1, g_memnorm, g_aq, g_ak, g_rel, g_bb_full, g_cw_full, g_cb_full, g_lg_full, g_lb_full,
     g_mq, g_mk, g_norm2) = summed
    g_bb = lax.dynamic_slice_in_dim(g_bb_full, me * f_loc, f_loc, axis=1)
    g_cw = lax.dynamic_slice_in_dim(g_cw_full, me * c_loc, c_loc, axis=2)
    g_cb = lax.dynamic_slice_in_dim(g_cb_full, me * c_loc, c_loc, axis=1)
    g_lg = lax.dynamic_slice_in_dim(g_lg_full, me * c_loc, c_loc, axis=1)
    g_lb = lax.dynamic_slice_in_dim(g_lb_full, me * c_loc, c_loc, axis=1)

    grads = [g_norm1, g_memnorm, None, g_aq, g_ak, g_rel, None, g_bb, g_cw, g_cb, g_lg, g_lb,
             g_mq, g_mk, None, None, g_norm2, None, None, None]
    deltas, new_m, new_v = [None] * 20, [None] * 20, [None] * 20
    for key, (idx, transposed) in sharded.items():
        grads[idx], deltas[idx], new_m[idx], new_v[idx] = (shaped(r, idx, transposed) for r in updated[key])

    small_idx = [i for i in range(20) if grads[i] is not None and i not in {idx for idx, _ in sharded.values()}]
    small_shapes2 = [weights[i].shape for i in small_idx]
    dl, nm, nv = adamw(_pack([weights[i] for i in small_idx]), _pack([grads[i] for i in small_idx]),
                       _pack([moms[i] for i in small_idx]), _pack([vels[i] for i in small_idx]), name="adamw_small")
    for i, a, b, cc in zip(small_idx, _unpack(dl, small_shapes2), _unpack(nm, small_shapes2), _unpack(nv, small_shapes2)):
        deltas[i], new_m[i], new_v[i] = a, b, cc

    return (loss, grad_x, *grads, *deltas, *new_m, *new_v)
```

```python
import functools

import jax
import jax.numpy as jnp
from jax import lax
from jax.experimental import pallas as pl
from jax.experimental.pallas import tpu as pltpu
from jax.experimental.pallas import tpu_sc as plsc

F32 = jnp.float32
BF16 = jnp.bfloat16
HIGHEST = lax.Precision.HIGHEST
MESH = pl.DeviceIdType.MESH
ANY = pl.BlockSpec(memory_space=pl.ANY)

N_DEV = 8
D_MODEL = 1024
HEAD_DIM = 64
TOK_WIDTH = 768
MEM_WIDTH = 256
CHUNK = 64
Q_BLOCK = 256
KEY_WIN = 768
BAND = 576
N_REL = 192
CONV_W = 31
CONV_HALO = 32
NORM_EPS = 1e-6
NEG_INF = -1e30
ATTN_SCALE = HEAD_DIM ** -0.5
LANES = 128
ROW_TILE = 512
VMEM_LIMIT = 56 * 1024 * 1024

ADAM_LR, ADAM_B1, ADAM_B2, ADAM_EPS, ADAM_WD, ADAM_STEP = 0.001, 0.9, 0.999, 1e-08, 0.01, 10


def _params(*sem):
    return pltpu.CompilerParams(dimension_semantics=sem, vmem_limit_bytes=VMEM_LIMIT)


def _row_tile(m):
    return ROW_TILE if m % ROW_TILE == 0 else m


def _col_tile(n, cap=1408):
    best = None
    for t in range(LANES, min(n, cap) + 1, LANES):
        if n % t == 0:
            best = t
    return best if best is not None else n


def _dot(a, b, ca, cb):
    return lax.dot_general(a, b, (((ca,), (cb,)), ((), ())), preferred_element_type=F32)


def _sigmoid(x):
    return 0.5 * jnp.tanh(0.5 * x) + 0.5


def mm_nt(a, b, bias=None, out_dtype=BF16, name="mm_nt"):
    m, k = a.shape
    n = b.shape[0]
    tm, tn = _row_tile(m), _col_tile(n)

    def body(*refs):
        a_ref, b_ref = refs[0], refs[1]
        o_ref = refs[-1]
        acc = _dot(a_ref[...].astype(BF16), b_ref[...].astype(BF16), 1, 1)
        if bias is not None:
            acc = acc + refs[2][...]
        o_ref[...] = acc.astype(o_ref.dtype)

    in_specs = [pl.BlockSpec((tm, k), lambda j, i: (i, 0)), pl.BlockSpec((tn, k), lambda j, i: (j, 0))]
    args = [a, b]
    if bias is not None:
        in_specs.append(pl.BlockSpec((1, tn), lambda j, i: (0, j)))
        args.append(bias)
    return pl.pallas_call(
        body, out_shape=jax.ShapeDtypeStruct((m, n), out_dtype), grid=(n // tn, m // tm),
        in_specs=in_specs, out_specs=pl.BlockSpec((tm, tn), lambda j, i: (i, j)),
        compiler_params=_params("parallel", "arbitrary"), name=name)(*args)


def mm_nn(a, b, res=None, out_dtype=F32, name="mm_nn"):
    m, k = a.shape
    n = b.shape[1]
    tm, tn = _row_tile(m), _col_tile(n, 1024)

    def body(*refs):
        a_ref, b_ref = refs[0], refs[1]
        o_ref = refs[-1]
        acc = _dot(a_ref[...].astype(BF16), b_ref[...].astype(BF16), 1, 0)
        if res is not None:
            acc = acc + refs[2][...]
        o_ref[...] = acc.astype(o_ref.dtype)

    in_specs = [pl.BlockSpec((tm, k), lambda j, i: (i, 0)), pl.BlockSpec((k, tn), lambda j, i: (0, j))]
    args = [a, b]
    if res is not None:
        in_specs.append(pl.BlockSpec((tm, tn), lambda j, i: (i, j)))
        args.append(res)
    return pl.pallas_call(
        body, out_shape=jax.ShapeDtypeStruct((m, n), out_dtype), grid=(n // tn, m // tm),
        in_specs=in_specs, out_specs=pl.BlockSpec((tm, tn), lambda j, i: (i, j)),
        compiler_params=_params("parallel", "arbitrary"), name=name)(*args)


def mm2_nn(a1, b1, a2, b2, name="mm2_nn"):
    m, k = a1.shape
    n = b1.shape[1]
    tm = _row_tile(m)

    def body(a1_ref, b1_ref, a2_ref, b2_ref, o_ref):
        o_ref[...] = _dot(a1_ref[...], b1_ref[...], 1, 0) + _dot(a2_ref[...], b2_ref[...], 1, 0)

    a_spec = pl.BlockSpec((tm, k), lambda i: (i, 0))
    b_spec = pl.BlockSpec((k, n), lambda i: (0, 0))
    return pl.pallas_call(
        body, out_shape=jax.ShapeDtypeStruct((m, n), F32), grid=(m // tm,),
        in_specs=[a_spec, b_spec, a_spec, b_spec], out_specs=pl.BlockSpec((tm, n), lambda i: (i, 0)),
        compiler_params=_params("parallel"), name=name)(a1, b1, a2, b2)


def mm_tn(a, b, out_dtype=BF16, name="mm_tn"):
    t, r = a.shape
    c = b.shape[1]
    tr = _col_tile(r, 512)

    def body(a_ref, b_ref, o_ref):
        o_ref[...] = _dot(a_ref[...].astype(BF16), b_ref[...].astype(BF16), 0, 0).astype(o_ref.dtype)

    return pl.pallas_call(
        body, out_shape=jax.ShapeDtypeStruct((r, c), out_dtype), grid=(r // tr,),
        in_specs=[pl.BlockSpec((t, tr), lambda i: (0, i)), pl.BlockSpec((t, c), lambda i: (0, 0))],
        out_specs=pl.BlockSpec((tr, c), lambda i: (i, 0)),
        compiler_params=_params("parallel"), name=name)(a, b)


def rms_fwd(x, g, name="rms_fwd"):
    n, d = x.shape
    tm = _row_tile(n)

    def body(x_ref, g_ref, o_ref):
        xv = x_ref[...]
        r = lax.rsqrt(jnp.mean(xv * xv, axis=-1, keepdims=True) + NORM_EPS)
        o_ref[...] = (xv * r * g_ref[...]).astype(o_ref.dtype)

    return pl.pallas_call(
        body, out_shape=jax.ShapeDtypeStruct((n, d), BF16), grid=(n // tm,),
        in_specs=[pl.BlockSpec((tm, d), lambda i: (i, 0)), pl.BlockSpec((1, d), lambda i: (0, 0))],
        out_specs=pl.BlockSpec((tm, d), lambda i: (i, 0)),
        compiler_params=_params("parallel"), name=name)(x, g)


def rms_bwd(dh, x, g, dres, name="rms_bwd"):
    n, d = x.shape
    tm = _row_tile(n)

    def body(dh_ref, x_ref, g_ref, dres_ref, dx_ref, dxb_ref, dg_ref):
        @pl.when(pl.program_id(0) == 0)
        def _():
            dg_ref[...] = jnp.zeros_like(dg_ref)

        xv = x_ref[...]
        dhv = dh_ref[...].astype(F32)
        r = lax.rsqrt(jnp.mean(xv * xv, axis=-1, keepdims=True) + NORM_EPS)
        xhat = xv * r
        dg_ref[...] += jnp.sum(dhv * xhat, axis=0, keepdims=True)
        dxhat = dhv * g_ref[...]
        mean_t = jnp.mean(dxhat * xhat, axis=-1, keepdims=True)
        dx = dres_ref[...] + r * (dxhat - xhat * mean_t)
        dx_ref[...] = dx
        dxb_ref[...] = dx.astype(BF16)

    row = pl.BlockSpec((tm, d), lambda i: (i, 0))
    vec = pl.BlockSpec((1, d), lambda i: (0, 0))
    return pl.pallas_call(
        body, out_shape=(jax.ShapeDtypeStruct((n, d), F32), jax.ShapeDtypeStruct((n, d), BF16),
                         jax.ShapeDtypeStruct((1, d), F32)), grid=(n // tm,),
        in_specs=[row, row, vec, row], out_specs=(row, row, vec),
        compiler_params=_params("arbitrary"), name=name)(dh, x, g, dres)


def gate_up(h2, wg_t, wu_t, name="gate_up"):
    n, d = h2.shape
    f = wg_t.shape[0]
    tm, tn = _row_tile(n), _col_tile(f)

    def body(h_ref, wg_ref, wu_ref, g_ref, u_ref, a_ref):
        hv = h_ref[...]
        gv = _dot(hv, wg_ref[...], 1, 1)
        uv = _dot(hv, wu_ref[...], 1, 1)
        g_ref[...] = gv.astype(BF16)
        u_ref[...] = uv.astype(BF16)
        a_ref[...] = (gv * _sigmoid(gv) * uv).astype(BF16)

    w_spec = pl.BlockSpec((tn, d), lambda j, i: (j, 0))
    o_spec = pl.BlockSpec((tm, tn), lambda j, i: (i, j))
    o_shape = jax.ShapeDtypeStruct((n, f), BF16)
    return pl.pallas_call(
        body, out_shape=(o_shape, o_shape, o_shape), grid=(f // tn, n // tm),
        in_specs=[pl.BlockSpec((tm, d), lambda j, i: (i, 0)), w_spec, w_spec], out_specs=(o_spec, o_spec, o_spec),
        compiler_params=_params("parallel", "arbitrary"), name=name)(h2, wg_t, wu_t)


def ffn_bwd_act(dx, wd, gate, up, name="ffn_bwd_act"):
    n, d = dx.shape
    f = wd.shape[0]
    tm, tn = _row_tile(n), _col_tile(f)

    def body(dx_ref, wd_ref, g_ref, u_ref, dg_ref, du_ref):
        dact = _dot(dx_ref[...].astype(BF16), wd_ref[...], 1, 1)
        gv = g_ref[...].astype(F32)
        uv = u_ref[...].astype(F32)
        sg = _sigmoid(gv)
        dg_ref[...] = (dact * uv * sg * (1.0 + gv * (1.0 - sg))).astype(BF16)
        du_ref[...] = (dact * gv * sg).astype(BF16)

    t_spec = pl.BlockSpec((tm, tn), lambda j, i: (i, j))
    o_shape = jax.ShapeDtypeStruct((n, f), BF16)
    return pl.pallas_call(
        body, out_shape=(o_shape, o_shape), grid=(f // tn, n // tm),
        in_specs=[pl.BlockSpec((tm, d), lambda j, i: (i, 0)), pl.BlockSpec((tn, d), lambda j, i: (j, 0)), t_spec, t_spec],
        out_specs=(t_spec, t_spec),
        compiler_params=_params("parallel", "arbitrary"), name=name)(dx, wd, gate, up)


def _group_masks(width):
    lane = lax.broadcasted_iota(jnp.int32, (1, width), 1)
    return [(lane >= HEAD_DIM * g) & (lane < HEAD_DIM * (g + 1)) for g in range(width // HEAD_DIM)]


def _group_sum(x, masks):
    out = jnp.zeros_like(x)
    for msk in masks:
        s = jnp.sum(jnp.where(msk, x, 0.0), axis=-1, keepdims=True)
        out = jnp.where(msk, s, out)
    return out


def _head_norm(x, gain, masks):
    r = lax.rsqrt(_group_sum(x * x, masks) * (1.0 / HEAD_DIM) + NORM_EPS)
    xhat = x * r
    return xhat * gain, xhat, r


def _head_norm_bwd(dxn, xhat, r, gain, masks):
    dgain = jnp.sum(dxn * xhat, axis=0, keepdims=True)
    dxhat = dxn * gain
    mean_t = _group_sum(dxhat * xhat, masks) * (1.0 / HEAD_DIM)
    return r * (dxhat - xhat * mean_t), dgain


def _softmax_rows(s):
    e = jnp.exp(s - jnp.max(s, axis=-1, keepdims=True))
    return e * (1.0 / jnp.sum(e, axis=-1, keepdims=True))


def _rel_onehot():
    col = lax.broadcasted_iota(jnp.int32, (1, KEY_WIN), 1)
    off = jnp.where(col < KEY_WIN - LANES, col, col - KEY_WIN)
    idx = jnp.clip(8 * CHUNK - off, -(CHUNK - 1), LANES) + (CHUNK - 1)
    return (lax.broadcasted_iota(jnp.int32, (N_REL, KEY_WIN), 0) == idx).astype(F32)


def bias_blocks(rel16):
    heads = TOK_WIDTH // HEAD_DIM

    def body(rel_ref, o_ref, u_ref):
        u_ref[...] = jnp.dot(rel_ref[...], _rel_onehot(), precision=HIGHEST, preferred_element_type=F32)
        row = lax.broadcasted_iota(jnp.int32, (CHUNK, KEY_WIN), 0)
        col = lax.broadcasted_iota(jnp.int32, (CHUNK, KEY_WIN), 1)
        for h in range(heads):
            xv = jnp.broadcast_to(u_ref[h:h + 1, :], (CHUNK, KEY_WIN))
            for b in range(6):
                xv = jnp.where(((row >> b) & 1) == 1, pltpu.roll(xv, 1 << b, axis=1), xv)
            xv = jnp.where(col < BAND, xv, NEG_INF)
            for i in range(Q_BLOCK // CHUNK):
                o_ref[h, CHUNK * i:CHUNK * (i + 1), :] = pltpu.roll(xv, CHUNK * i, axis=1) if i else xv

    return pl.pallas_call(
        body, out_shape=jax.ShapeDtypeStruct((heads, Q_BLOCK, KEY_WIN), F32),
        scratch_shapes=[pltpu.VMEM((16, KEY_WIN), F32)], name="bias_blocks")(rel16)


def bias_grad(dbias):
    heads = dbias.shape[0]

    def body(db_ref, o_ref, y_ref):
        y_ref[...] = jnp.zeros_like(y_ref)
        row = lax.broadcasted_iota(jnp.int32, (CHUNK, KEY_WIN), 0)
        for h in range(heads):
            fv = db_ref[h, 0:CHUNK, :]
            for i in range(1, Q_BLOCK // CHUNK):
                fv = fv + pltpu.roll(db_ref[h, CHUNK * i:CHUNK * (i + 1), :], KEY_WIN - CHUNK * i, axis=1)
            for b in range(6):
                fv = jnp.where(((row >> b) & 1) == 1, pltpu.roll(fv, KEY_WIN - (1 << b), axis=1), fv)
            y_ref[h:h + 1, :] = jnp.sum(fv, axis=0, keepdims=True)
        o_ref[...] = lax.dot_general(y_ref[...], _rel_onehot(), (((1,), (1,)), ((), ())),
                                     precision=HIGHEST, preferred_element_type=F32)

    return pl.pallas_call(
        body, out_shape=jax.ShapeDtypeStruct((16, N_REL), F32),
        scratch_shapes=[pltpu.VMEM((16, KEY_WIN), F32)], name="bias_grad")(dbias)


def _attn_windows(seq):
    out = []
    for j in range(seq // Q_BLOCK):
        r0 = j * Q_BLOCK
        k0 = max(0, r0 - 8 * CHUNK)
        width = r0 + Q_BLOCK - k0
        out.append((r0, k0, width, KEY_WIN - width))
    return out


def attn_fwd(z, gq2, gk2, bias, batch, seq):
    n = z.shape[0]
    pairs = TOK_WIDTH // LANES

    def body(q_ref, k_ref, v_ref, gq_ref, gk_ref, b_ref, o_ref, qs_s, kn_s):
        masks = _group_masks(LANES)
        qs_s[...] = (_head_norm(q_ref[...].astype(F32), gq_ref[...], masks)[0] * ATTN_SCALE).astype(BF16)
        kn_s[...] = _head_norm(k_ref[...].astype(F32), gk_ref[...], masks)[0].astype(BF16)
        for r0, k0, width, c0 in _attn_windows(seq):
            qb = qs_s[r0:r0 + Q_BLOCK, :]
            kw = kn_s[k0:k0 + width, :]
            vw = v_ref[k0:k0 + width, :]
            out = jnp.zeros((Q_BLOCK, LANES), F32)
            for h, msk in enumerate(masks):
                qh = jnp.where(msk, qb, jnp.zeros_like(qb))
                s = _dot(qh, kw, 1, 1) + b_ref[h, :, c0:KEY_WIN]
                p = _softmax_rows(s).astype(BF16)
                out = jnp.where(msk, _dot(p, vw, 1, 0), out)
            o_ref[r0:r0 + Q_BLOCK, :] = out.astype(o_ref.dtype)

    def col(off):
        return pl.BlockSpec((seq, LANES), lambda b, p: (b, off + p))

    vec = pl.BlockSpec((1, LANES), lambda b, p: (0, 0))
    return pl.pallas_call(
        body, out_shape=jax.ShapeDtypeStruct((n, D_MODEL), BF16), grid=(batch, pairs),
        in_specs=[col(0), col(pairs), col(2 * pairs), vec, vec,
                  pl.BlockSpec((2, Q_BLOCK, KEY_WIN), lambda b, p: (p, 0, 0))],
        out_specs=pl.BlockSpec((seq, LANES), lambda b, p: (b, p)),
        scratch_shapes=[pltpu.VMEM((seq, LANES), BF16), pltpu.VMEM((seq, LANES), BF16)],
        compiler_params=_params("parallel", "arbitrary"), name="attn_fwd")(z, z, z, gq2, gk2, bias)


def attn_bwd(z, dcat, gq2, gk2, bias, batch, seq):
    n = z.shape[0]
    pairs = TOK_WIDTH // LANES

    def body(q_ref, k_ref, v_ref, do_ref, gq_ref, gk_ref, b_ref,
             dz_ref, db_ref, dgq_ref, dgk_ref, qs_s, kn_s, dqn_s, dkn_s, dv_s, dk_o, dv_o):
        pi, bi, which = pl.program_id(0), pl.program_id(1), pl.program_id(2)

        @pl.when(which == 0)
        def _():
            masks = _group_masks(LANES)

            @pl.when(bi == 0)
            def _():
                db_ref[...] = jnp.zeros_like(db_ref)

            @pl.when((bi == 0) & (pi == 0))
            def _():
                dgq_ref[...] = jnp.zeros_like(dgq_ref)
                dgk_ref[...] = jnp.zeros_like(dgk_ref)

            qn, qhat, rq = _head_norm(q_ref[...].astype(F32), gq_ref[...], masks)
            kn, khat, rk = _head_norm(k_ref[...].astype(F32), gk_ref[...], masks)
            qs_s[...] = (qn * ATTN_SCALE).astype(BF16)
            kn_s[...] = kn.astype(BF16)
            dkn_s[...] = jnp.zeros_like(dkn_s)
            dv_s[...] = jnp.zeros_like(dv_s)
            for r0, k0, width, c0 in _attn_windows(seq):
                qb = qs_s[r0:r0 + Q_BLOCK, :]
                dob = do_ref[r0:r0 + Q_BLOCK, :]
                kw = kn_s[k0:k0 + width, :]
                vw = v_ref[k0:k0 + width, :]
                dq_acc = jnp.zeros((Q_BLOCK, LANES), F32)
                dk_acc = jnp.zeros((width, LANES), F32)
                dv_acc = jnp.zeros((width, LANES), F32)
                for h, msk in enumerate(masks):
                    qh = jnp.where(msk, qb, jnp.zeros_like(qb))
                    doh = jnp.where(msk, dob, jnp.zeros_like(dob))
                    p = _softmax_rows(_dot(qh, kw, 1, 1) + b_ref[h, :, c0:KEY_WIN])
                    dp = _dot(doh, vw, 1, 1)
                    ds = p * (dp - jnp.sum(p * dp, axis=-1, keepdims=True))
                    db_ref[h, :, c0:KEY_WIN] += ds
                    dsb = ds.astype(BF16)
                    dq_acc = jnp.where(msk, _dot(dsb, kw, 1, 0), dq_acc)
                    dk_acc = jnp.where(msk, _dot(dsb, qb, 0, 0), dk_acc)
                    dv_acc = jnp.where(msk, _dot(p.astype(BF16), dob, 0, 0), dv_acc)
                dqn_s[r0:r0 + Q_BLOCK, :] = dq_acc * ATTN_SCALE
                dkn_s[k0:k0 + width, :] += dk_acc
                dv_s[k0:k0 + width, :] += dv_acc
            dq, dgq = _head_norm_bwd(dqn_s[...], qhat, rq, gq_ref[...], masks)
            dk, dgk = _head_norm_bwd(dkn_s[...], khat, rk, gk_ref[...], masks)
            dz_ref[...] = dq.astype(dz_ref.dtype)
            dk_o[...] = dk.astype(dk_o.dtype)
            dv_o[...] = dv_s[...].astype(dv_o.dtype)
            dgq_ref[...] += dgq
            dgk_ref[...] += dgk

        @pl.when(which == 1)
        def _():
            dz_ref[...] = dk_o[...]

        @pl.when(which == 2)
        def _():
            dz_ref[...] = dv_o[...]

    def ahead(p, b, t):
        nb = b + jnp.where(t > 0, 1, 0)
        wrap = jnp.where(nb >= batch, 1, 0)
        return jnp.minimum(p + wrap, pairs - 1), nb - wrap * batch

    def col(off):
        def index(p, b, t):
            np_, nb = ahead(p, b, t)
            return nb, off + np_
        return pl.BlockSpec((seq, LANES), index)

    vec = pl.BlockSpec((1, LANES), lambda p, b, t: (0, 0))
    blk = pl.BlockSpec((2, Q_BLOCK, KEY_WIN), lambda p, b, t: (p, 0, 0))
    blk_in = pl.BlockSpec((2, Q_BLOCK, KEY_WIN), lambda p, b, t: (ahead(p, b, t)[0], 0, 0))
    v_shape = jax.ShapeDtypeStruct((1, LANES), F32)
    return pl.pallas_call(
        body,
        out_shape=(jax.ShapeDtypeStruct(z.shape, BF16), jax.ShapeDtypeStruct(bias.shape, F32), v_shape, v_shape),
        grid=(pairs, batch, 3),
        in_specs=[col(0), col(pairs), col(2 * pairs), col(0), vec, vec, blk_in],
        out_specs=(pl.BlockSpec((seq, LANES), lambda p, b, t: (b, t * pairs + p)), blk, vec, vec),
        scratch_shapes=[pltpu.VMEM((seq, LANES), BF16), pltpu.VMEM((seq, LANES), BF16),
                        pltpu.VMEM((seq, LANES), F32), pltpu.VMEM((seq, LANES), F32), pltpu.VMEM((seq, LANES), F32),
                        pltpu.VMEM((seq, LANES), BF16), pltpu.VMEM((seq, LANES), BF16)],
        compiler_params=_params("arbitrary", "arbitrary", "arbitrary"), name="attn_bwd")(
            z, z, z, dcat, gq2, gk2, bias)


MEM_ROWS = 512


def memattn_fwd(z, kv, gq4, gk4, cat, batch, seq, qcol, name):
    mtok = kv.shape[0] // batch
    rows = min(MEM_ROWS, seq)

    def body(q_ref, kv_ref, gq_ref, gk_ref, cat_ref, o_ref):
        del cat_ref
        masks = _group_masks(MEM_WIDTH)
        kn = _head_norm(kv_ref[:, 0:MEM_WIDTH], gk_ref[...], masks)[0].astype(BF16)
        vm = kv_ref[:, MEM_WIDTH:2 * MEM_WIDTH].astype(BF16)
        for t in range(seq // rows):
            sl = slice(t * rows, (t + 1) * rows)
            qs = (_head_norm(q_ref[sl, :].astype(F32), gq_ref[...], masks)[0] * ATTN_SCALE).astype(BF16)
            out = jnp.zeros((rows, MEM_WIDTH), F32)
            for msk in masks:
                qh = jnp.where(msk, qs, jnp.zeros_like(qs))
                p = _softmax_rows(_dot(qh, kn, 1, 1)).astype(BF16)
                out = jnp.where(msk, _dot(p, vm, 1, 0), out)
            o_ref[sl, :] = out.astype(o_ref.dtype)

    vec = pl.BlockSpec((1, MEM_WIDTH), lambda b: (0, 0))
    return pl.pallas_call(
        body, out_shape=jax.ShapeDtypeStruct(cat.shape, cat.dtype), grid=(batch,),
        in_specs=[pl.BlockSpec((seq, MEM_WIDTH), lambda b: (b, qcol)),
                  pl.BlockSpec((mtok, 2 * MEM_WIDTH), lambda b: (b, 0)), vec, vec, ANY],
        out_specs=pl.BlockSpec((seq, MEM_WIDTH), lambda b: (b, TOK_WIDTH // MEM_WIDTH)),
        input_output_aliases={4: 0},
        compiler_params=_params("parallel"), name=name)(z, kv, gq4, gk4, cat)


def memattn_bwd(z, kv, dcat, gq4, gk4, dz, batch, seq, qcol, name):
    mtok = kv.shape[0] // batch
    rows = min(MEM_ROWS, seq)

    def body(q_ref, kv_ref, do_ref, gq_ref, gk_ref, dz_in_ref, dq_ref, dkv_ref, dgq_ref, dgk_ref):
        del dz_in_ref
        @pl.when(pl.program_id(0) == 0)
        def _():
            dgq_ref[...] = jnp.zeros_like(dgq_ref)
            dgk_ref[...] = jnp.zeros_like(dgk_ref)

        masks = _group_masks(MEM_WIDTH)
        kn_f, khat, rk = _head_norm(kv_ref[:, 0:MEM_WIDTH], gk_ref[...], masks)
        kn = kn_f.astype(BF16)
        vm = kv_ref[:, MEM_WIDTH:2 * MEM_WIDTH].astype(BF16)
        dkn = jnp.zeros((mtok, MEM_WIDTH), F32)
        dvm = jnp.zeros((mtok, MEM_WIDTH), F32)
        dgq = jnp.zeros((1, MEM_WIDTH), F32)
        for t in range(seq // rows):
            sl = slice(t * rows, (t + 1) * rows)
            qn_f, qhat, rq = _head_norm(q_ref[sl, :].astype(F32), gq_ref[...], masks)
            qs = (qn_f * ATTN_SCALE).astype(BF16)
            dob = do_ref[sl, :]
            dqn = jnp.zeros((rows, MEM_WIDTH), F32)
            for msk in masks:
                qh = jnp.where(msk, qs, jnp.zeros_like(qs))
                doh = jnp.where(msk, dob, jnp.zeros_like(dob))
                p = _softmax_rows(_dot(qh, kn, 1, 1))
                dp = _dot(doh, vm, 1, 1)
                ds = p * (dp - jnp.sum(p * dp, axis=-1, keepdims=True))
                dsb = ds.astype(BF16)
                dqn = jnp.where(msk, _dot(dsb, kn, 1, 0), dqn)
                dkn = dkn + jnp.where(msk, _dot(dsb, qs, 0, 0), 0.0)
                dvm = dvm + jnp.where(msk, _dot(p.astype(BF16), dob, 0, 0), 0.0)
            dq, dg = _head_norm_bwd(dqn * ATTN_SCALE, qhat, rq, gq_ref[...], masks)
            dq_ref[sl, :] = dq.astype(dq_ref.dtype)
            dgq = dgq + dg
        dk, dgk = _head_norm_bwd(dkn, khat, rk, gk_ref[...], masks)
        dkv_ref[:, 0:MEM_WIDTH] = dk
        dkv_ref[:, MEM_WIDTH:2 * MEM_WIDTH] = dvm
        dgq_ref[...] += dgq
        dgk_ref[...] += dgk

    vec = pl.BlockSpec((1, MEM_WIDTH), lambda b: (0, 0))
    kv_spec = pl.BlockSpec((mtok, 2 * MEM_WIDTH), lambda b: (b, 0))
    v_shape = jax.ShapeDtypeStruct((1, MEM_WIDTH), F32)
    q_spec = pl.BlockSpec((seq, MEM_WIDTH), lambda b: (b, qcol))
    return pl.pallas_call(
        body,
        out_shape=(jax.ShapeDtypeStruct(dz.shape, dz.dtype), jax.ShapeDtypeStruct(kv.shape, F32), v_shape, v_shape),
        grid=(batch,),
        in_specs=[q_spec, kv_spec, pl.BlockSpec((seq, MEM_WIDTH), lambda b: (b, TOK_WIDTH // MEM_WIDTH)), vec, vec, ANY],
        out_specs=(q_spec, kv_spec, vec, vec),
        input_output_aliases={5: 0},
        compiler_params=_params("arbitrary"), name=name)(z, kv, dcat, gq4, gk4, dz)


CONV_ROWS = 256


def _glu(a_ref, g_ref):
    return a_ref[...].astype(F32) * _sigmoid(g_ref[...].astype(F32))


def _layer_norm_stats(y):
    mu = jnp.mean(y, axis=-1, keepdims=True)
    yc = y - mu
    rstd = lax.rsqrt(jnp.mean(yc * yc, axis=-1, keepdims=True) + NORM_EPS)
    return yc * rstd, rstd


CONV_WIN = CONV_HALO + CONV_ROWS
SUBLANES = 8
SHIFT_ROWS = CONV_WIN - SUBLANES


def _preshift(win, shifted):
    for s in range(1, SUBLANES):
        shifted[s - 1, :, :] = win[s:s + SHIFT_ROWS, :]


TAP_ROWS = 64
TAP_TILES = [(r0, slice(c0, c0 + LANES)) for c0 in range(0, TOK_WIDTH, LANES) for r0 in range(0, CONV_ROWS, TAP_ROWS)]


def _tap(win, shifted, off, r0, lanes):
    s = off % SUBLANES
    base = off - s + r0
    if s == 0:
        return win[base:base + TAP_ROWS, lanes]
    return shifted[s - 1, base:base + TAP_ROWS, lanes]


def _fold_rows(x):
    return jnp.sum(x.reshape(TAP_ROWS // SUBLANES, SUBLANES, LANES), axis=0)


def conv_fwd(z, cw, cb, lg, lb, batch, seq):
    n = z.shape[0]
    nt = seq // CONV_ROWS
    sub = CONV_ROWS // CONV_HALO
    lead = CONV_HALO - (CONV_W - 1)

    def body(a_ref, g_ref, ap_ref, gp_ref, cw_ref, cb_ref, lg_ref, lb_ref, o_ref, y_ref, win, shifted):
        first = pl.program_id(1) == 0
        win[0:CONV_HALO, :] = jnp.where(first, 0.0, _glu(ap_ref, gp_ref))
        win[CONV_HALO:CONV_WIN, :] = _glu(a_ref, g_ref)
        _preshift(win, shifted)
        for r0, lanes in TAP_TILES:
            acc = jnp.zeros((TAP_ROWS, LANES), F32) + cb_ref[:, lanes]
            for w in range(CONV_W):
                acc = acc + _tap(win, shifted, lead + w, r0, lanes) * cw_ref[w:w + 1, lanes]
            y_ref[r0:r0 + TAP_ROWS, lanes] = acc
        yh, _ = _layer_norm_stats(y_ref[...])
        t = yh * lg_ref[...] + lb_ref[...]
        o_ref[...] = (t * _sigmoid(t)).astype(o_ref.dtype)

    def cur(c):
        return pl.BlockSpec((CONV_ROWS, TOK_WIDTH), lambda b, i: (b * nt + i, c))

    def prev(c):
        return pl.BlockSpec((CONV_HALO, TOK_WIDTH), lambda b, i: (jnp.maximum((b * nt + i) * sub - 1, 0), c))

    vec = pl.BlockSpec((1, TOK_WIDTH), lambda b, i: (0, 0))
    return pl.pallas_call(
        body, out_shape=(jax.ShapeDtypeStruct((n, D_MODEL), BF16), jax.ShapeDtypeStruct((n, TOK_WIDTH), F32)),
        grid=(batch, nt),
        in_specs=[cur(0), cur(1), prev(0), prev(1), pl.BlockSpec((32, TOK_WIDTH), lambda b, i: (0, 0)), vec, vec, vec],
        out_specs=(cur(0), cur(0)),
        scratch_shapes=[pltpu.VMEM((CONV_WIN, TOK_WIDTH), F32), pltpu.VMEM((SUBLANES - 1, SHIFT_ROWS, TOK_WIDTH), F32)],
        compiler_params=_params("parallel", "arbitrary"), name="conv_fwd")(z, z, z, z, cw, cb, lg, lb)


def conv_bwd(z, y, dcat, cw, lg, lb, batch, seq):
    n = z.shape[0]
    nt = seq // CONV_ROWS
    sub = CONV_ROWS // CONV_HALO
    lead = CONV_HALO - (CONV_W - 1)
    last_blk = n // CONV_HALO - 1

    def body(a_ref, g_ref, ap_ref, gp_ref, y_ref, yn_ref, do_ref, don_ref, cw_ref, lg_ref, lb_ref,
             dz_ref, dcw_ref, dsm_ref, win, shifted, dyw, dshifted, dg_o):
        b, i, which = pl.program_id(0), pl.program_id(1), pl.program_id(2)

        @pl.when(which == 0)
        def _():
            first, last = i == 0, i == nt - 1

            @pl.when((b == 0) & (i == 0))
            def _():
                dcw_ref[...] = jnp.zeros_like(dcw_ref)
                dsm_ref[...] = jnp.zeros_like(dsm_ref)

            win[0:CONV_HALO, :] = jnp.where(first, 0.0, _glu(ap_ref, gp_ref))
            win[CONV_HALO:CONV_WIN, :] = _glu(a_ref, g_ref)
            _preshift(win, shifted)
            yv = jnp.concatenate([y_ref[...], yn_ref[...]], axis=0)
            yh, rstd = _layer_norm_stats(yv)
            t = yh * lg_ref[...] + lb_ref[...]
            st = _sigmoid(t)
            dout = jnp.concatenate(
                [do_ref[...].astype(F32), jnp.where(last, 0.0, don_ref[...].astype(F32))], axis=0)
            dt = dout * st * (1.0 + t * (1.0 - st))
            dyh = dt * lg_ref[...]
            dy = rstd * (dyh - jnp.mean(dyh, axis=-1, keepdims=True)
                         - yh * jnp.mean(dyh * yh, axis=-1, keepdims=True))
            dyw[...] = dy
            _preshift(dyw, dshifted)
            dsm_ref[0:1, :] += jnp.sum(dy[0:CONV_ROWS], axis=0, keepdims=True)
            dsm_ref[1:2, :] += jnp.sum((dt * yh)[0:CONV_ROWS], axis=0, keepdims=True)
            dsm_ref[2:3, :] += jnp.sum(dt[0:CONV_ROWS], axis=0, keepdims=True)
            for c0 in range(0, TOK_WIDTH, LANES):
                lanes = slice(c0, c0 + LANES)
                dcw_acc = [jnp.zeros((SUBLANES, LANES), F32) for _ in range(CONV_W)]
                for r0 in range(0, CONV_ROWS, TAP_ROWS):
                    dyt = dyw[r0:r0 + TAP_ROWS, lanes]
                    dglu = jnp.zeros((TAP_ROWS, LANES), F32)
                    for w in range(CONV_W):
                        dcw_acc[w] = dcw_acc[w] + _fold_rows(dyt * _tap(win, shifted, lead + w, r0, lanes))
                        dglu = dglu + _tap(dyw, dshifted, CONV_W - 1 - w, r0, lanes) * cw_ref[w:w + 1, lanes]
                    avt = a_ref[r0:r0 + TAP_ROWS, lanes].astype(F32)
                    sgt = _sigmoid(g_ref[r0:r0 + TAP_ROWS, lanes].astype(F32))
                    dz_ref[r0:r0 + TAP_ROWS, lanes] = (dglu * sgt).astype(dz_ref.dtype)
                    dg_o[r0:r0 + TAP_ROWS, lanes] = (dglu * avt * sgt * (1.0 - sgt)).astype(dg_o.dtype)
                for w in range(CONV_W):
                    dcw_ref[w:w + 1, lanes] += jnp.sum(dcw_acc[w], axis=0, keepdims=True)

        @pl.when(which == 1)
        def _():
            dz_ref[...] = dg_o[...]

    def ahead(b, i, t):
        return jnp.minimum(b * nt + i + t, batch * nt - 1)

    def cur(c):
        return pl.BlockSpec((CONV_ROWS, TOK_WIDTH), lambda b, i, t: (ahead(b, i, t), c))

    def prev(c):
        return pl.BlockSpec((CONV_HALO, TOK_WIDTH), lambda b, i, t: (jnp.maximum(ahead(b, i, t) * sub - 1, 0), c))

    nxt = pl.BlockSpec((CONV_HALO, TOK_WIDTH),
                       lambda b, i, t: (jnp.minimum((ahead(b, i, t) + 1) * sub, last_blk), 0))
    vec = pl.BlockSpec((1, TOK_WIDTH), lambda b, i, t: (0, 0))
    full32 = pl.BlockSpec((32, TOK_WIDTH), lambda b, i, t: (0, 0))
    return pl.pallas_call(
        body,
        out_shape=(jax.ShapeDtypeStruct(z.shape, BF16), jax.ShapeDtypeStruct((32, TOK_WIDTH), F32),
                   jax.ShapeDtypeStruct((8, TOK_WIDTH), F32)),
        grid=(batch, nt, 2),
        in_specs=[cur(0), cur(1), prev(0), prev(1), cur(0), nxt, cur(0), nxt, full32, vec, vec],
        out_specs=(pl.BlockSpec((CONV_ROWS, TOK_WIDTH), lambda b, i, t: (b * nt + i, t)), full32,
                   pl.BlockSpec((8, TOK_WIDTH), lambda b, i, t: (0, 0))),
        scratch_shapes=[pltpu.VMEM((CONV_WIN, TOK_WIDTH), F32), pltpu.VMEM((SUBLANES - 1, SHIFT_ROWS, TOK_WIDTH), F32),
                        pltpu.VMEM((CONV_WIN, TOK_WIDTH), F32), pltpu.VMEM((SUBLANES - 1, SHIFT_ROWS, TOK_WIDTH), F32),
                        pltpu.VMEM((CONV_ROWS, TOK_WIDTH), BF16)],
        compiler_params=_params("arbitrary", "arbitrary", "arbitrary"), name="conv_bwd")(
            z, z, z, z, y, y, dcat, dcat, cw, lg, lb)


def loss_head(y, target):
    n, d = y.shape
    tm = _row_tile(n)
    nt = n // tm

    def body(y_ref, t_ref, dy_ref, dyb_ref, l_ref, acc_ref):
        i = pl.program_id(0)

        @pl.when(i == 0)
        def _():
            acc_ref[...] = jnp.zeros_like(acc_ref)

        err = y_ref[...] - t_ref[...]
        dy = err * (1.0 / d)
        dy_ref[...] = dy
        dyb_ref[...] = dy.astype(BF16)
        acc_ref[...] += jnp.sum(err * err, axis=0, keepdims=True)

        @pl.when(i == nt - 1)
        def _():
            total = jnp.sum(acc_ref[...], axis=-1, keepdims=True) * (0.5 / d)
            l_ref[...] = jnp.broadcast_to(total, l_ref.shape)

    row = pl.BlockSpec((tm, d), lambda i: (i, 0))
    return pl.pallas_call(
        body, out_shape=(jax.ShapeDtypeStruct((n, d), F32), jax.ShapeDtypeStruct((n, d), BF16),
                         jax.ShapeDtypeStruct((8, LANES), F32)), grid=(nt,),
        in_specs=[row, row], out_specs=(row, row, pl.BlockSpec((8, LANES), lambda i: (0, 0))),
        scratch_shapes=[pltpu.VMEM((1, d), F32)],
        compiler_params=_params("arbitrary"), name="loss_head")(y, target)


def col_sum(x, name="col_sum"):
    n, c = x.shape
    tm = _row_tile(n)

    def body(x_ref, o_ref):
        @pl.when(pl.program_id(0) == 0)
        def _():
            o_ref[...] = jnp.zeros_like(o_ref)

        o_ref[...] += jnp.sum(x_ref[...].astype(F32), axis=0, keepdims=True)

    return pl.pallas_call(
        body, out_shape=jax.ShapeDtypeStruct((1, c), F32), grid=(n // tm,),
        in_specs=[pl.BlockSpec((tm, c), lambda i: (i, 0))], out_specs=pl.BlockSpec((1, c), lambda i: (0, 0)),
        compiler_params=_params("arbitrary"), name=name)(x)


def adamw(w, g, m, v, name="adamw"):
    rows, cols = w.shape
    tr = rows
    for cand in (512, 256, 128, 64, 32, 16, 8):
        if rows % cand == 0 and rows > cand:
            tr = cand
            break
    c1 = 1.0 / (1.0 - ADAM_B1 ** ADAM_STEP)
    c2 = 1.0 / (1.0 - ADAM_B2 ** ADAM_STEP)

    def body(w_ref, g_ref, m_ref, v_ref, d_ref, nm_ref, nv_ref):
        gv = g_ref[...]
        nm = ADAM_B1 * m_ref[...] + (1.0 - ADAM_B1) * gv
        nv = ADAM_B2 * v_ref[...] + (1.0 - ADAM_B2) * (gv * gv)
        nm_ref[...] = nm
        nv_ref[...] = nv
        d_ref[...] = -ADAM_LR * ((nm * c1) / (jnp.sqrt(nv * c2) + ADAM_EPS) + ADAM_WD * w_ref[...])

    spec = pl.BlockSpec((tr, cols), lambda i: (i, 0))
    shape = jax.ShapeDtypeStruct((rows, cols), F32)
    return pl.pallas_call(
        body, out_shape=(shape, shape, shape), grid=(rows // tr,),
        in_specs=[spec, spec, spec, spec], out_specs=(spec, spec, spec),
        compiler_params=_params("parallel"), name=name)(w, g, m, v)


def _place():
    return lax.axis_index("x"), lax.axis_index("y"), lax.axis_index("c")


def _other_chips(x, y):
    return [(1 - x, y), (x, 1 - y), (1 - x, 1 - y)]


def small_exchange(slab, reduce):
    r = slab.shape[0]

    def body(in_ref, o_ref, *scratch):
        if reduce:
            buf, send_sems, recv_sems = scratch
        else:
            buf = o_ref
            send_sems, recv_sems = scratch
        x, y, c = _place()
        me = 4 * x + 2 * y + c
        buf[me] = in_ref[...]
        copies = []
        for k in range(1, N_DEV):
            peer = (x ^ (k >> 2), y ^ ((k >> 1) & 1), c ^ (k & 1))
            cp = pltpu.make_async_remote_copy(
                src_ref=in_ref, dst_ref=buf.at[me], send_sem=send_sems.at[k - 1], recv_sem=recv_sems.at[k - 1],
                device_id=peer, device_id_type=MESH)
            cp.start()
            copies.append(cp)
        for k in range(1, N_DEV):
            src = 4 * (x ^ (k >> 2)) + 2 * (y ^ ((k >> 1) & 1)) + (c ^ (k & 1))
            pltpu.make_async_remote_copy(
                src_ref=in_ref, dst_ref=buf.at[src], send_sem=send_sems.at[k - 1], recv_sem=recv_sems.at[k - 1],
                device_id=(x, y, c), device_id_type=MESH).wait_recv()
        for cp in copies:
            cp.wait_send()
        if reduce:
            total = buf[0]
            for d in range(1, N_DEV):
                total = total + buf[d]
            o_ref[...] = total

    sems = [pltpu.SemaphoreType.DMA((N_DEV - 1,)), pltpu.SemaphoreType.DMA((N_DEV - 1,))]
    if reduce:
        out_shape = jax.ShapeDtypeStruct((r, LANES), F32)
        scratch = [pltpu.VMEM((N_DEV, r, LANES), F32)] + sems
    else:
        out_shape = jax.ShapeDtypeStruct((N_DEV, r, LANES), F32)
        scratch = sems
    vmem = pl.BlockSpec(memory_space=pltpu.VMEM)
    return pl.pallas_call(
        body, out_shape=out_shape, in_specs=[vmem], out_specs=vmem, scratch_shapes=scratch,
        compiler_params=pltpu.CompilerParams(vmem_limit_bytes=VMEM_LIMIT),
        name="small_reduce" if reduce else "small_gather")(slab)


def gather_weights(shards, name, collective_id):
    nw = len(shards)
    ns = [s.shape[0] for s in shards]
    in_refs = [jax.new_ref(s, memory_space=pltpu.MemorySpace.HBM) for s in shards]
    out_refs = [jax.empty_ref(jax.ShapeDtypeStruct((N_DEV * s.shape[0], s.shape[1]), s.dtype),
                              memory_space=pltpu.MemorySpace.HBM) for s in shards]

    @pl.kernel(mesh=plsc.ScalarSubcoreMesh(axis_name="seq", num_cores=1), name=name,
               scratch_types=(pltpu.SemaphoreType.DMA((nw, 7)), pltpu.SemaphoreType.DMA((nw, 7)),
                              pltpu.SemaphoreType.DMA((nw,))),
               compiler_params=pltpu.CompilerParams(collective_id=collective_id))
    def launch(send_sems, recv_sems, local_sems):
        x, y, c = _place()
        me, sib = (x, y, c), (x, y, 1 - c)
        chips = _other_chips(x, y)
        barrier = pltpu.get_barrier_semaphore()
        for peer in [sib] + [(*chip, c) for chip in chips]:
            pl.semaphore_signal(barrier, inc=1, device_id=peer, device_id_type=MESH)
        pl.semaphore_wait(barrier, 4)

        def rows(w, dev):
            return out_refs[w].at[pl.ds((4 * dev[0] + 2 * dev[1] + dev[2]) * ns[w], ns[w]), :]

        def copy(w, k, block, to, src=None):
            return pltpu.make_async_remote_copy(
                src_ref=rows(w, block) if src is None else src, dst_ref=rows(w, block),
                send_sem=send_sems.at[w, k], recv_sem=recv_sems.at[w, k], device_id=to, device_id_type=MESH)

        started, sends = [], []
        for w in range(nw):
            mine = pltpu.make_async_copy(in_refs[w], rows(w, me), local_sems.at[w])
            mine.start()
            started.append(mine)
            first = [copy(w, 0, me, sib, src=in_refs[w])]
            first += [copy(w, 1 + j, me, (*chip, c), src=in_refs[w]) for j, chip in enumerate(chips)]
            for cp in first:
                cp.start()
            sends += first
        for w in range(nw):
            for j, chip in enumerate(chips):
                copy(w, 1 + j, (*chip, c), me).wait_recv()
                fwd = copy(w, 4 + j, (*chip, c), sib)
                fwd.start()
                sends.append(fwd)
        for w in range(nw):
            copy(w, 0, sib, me).wait_recv()
            for j, chip in enumerate(chips):
                copy(w, 4 + j, (*chip, 1 - c), me).wait_recv()
        for cp in sends:
            cp.wait_send()
        for mine in started:
            mine.wait()

    launch()
    return [r[...] for r in out_refs]


def _sequencer_exchange(sources, out_rows, peers_of, copies_of, name, collective_id):
    nw = len(sources)
    in_refs = [jax.new_ref(s, memory_space=pltpu.MemorySpace.HBM) for s in sources]
    out_refs = [jax.empty_ref(jax.ShapeDtypeStruct((rows, s.shape[1]), s.dtype), memory_space=pltpu.MemorySpace.HBM)
                for rows, s in zip(out_rows, sources)]
    per = len(copies_of(0, 0, 0, 0))

    @pl.kernel(mesh=plsc.ScalarSubcoreMesh(axis_name="seq", num_cores=1), name=name,
               scratch_types=(pltpu.SemaphoreType.DMA((nw, per)), pltpu.SemaphoreType.DMA((nw, per))),
               compiler_params=pltpu.CompilerParams(collective_id=collective_id))
    def launch(send_sems, recv_sems):
        x, y, c = _place()
        peers = peers_of(x, y, c)
        barrier = pltpu.get_barrier_semaphore()
        for peer in peers:
            pl.semaphore_signal(barrier, inc=1, device_id=peer, device_id_type=MESH)
        pl.semaphore_wait(barrier, len(peers))
        copies = []
        for w in range(nw):
            for k, (src_blk, dst_blk, rows, peer) in enumerate(copies_of(x, y, c, w)):
                cp = pltpu.make_async_remote_copy(
                    src_ref=in_refs[w].at[pl.ds(src_blk * rows, rows), :],
                    dst_ref=out_refs[w].at[pl.ds(dst_blk * rows, rows), :],
                    send_sem=send_sems.at[w, k], recv_sem=recv_sems.at[w, k], device_id=peer, device_id_type=MESH)
                cp.start()
                copies.append(cp)
        for cp in copies:
            cp.wait_recv()
        for cp in copies:
            cp.wait_send()

    launch()
    return [r[...] for r in out_refs]


def scatter_to_sibling(grads, name, collective_id):
    ns = [g.shape[0] // N_DEV for g in grads]
    return _sequencer_exchange(
        grads, [4 * n for n in ns],
        lambda x, y, c: [(x, y, 1 - c)],
        lambda x, y, c, w: [(2 * q + 1 - c, q, ns[w], (x, y, 1 - c)) for q in range(4)],
        name, collective_id)


def scatter_to_chips(parts, name, collective_id):
    ns = [p.shape[0] // 4 for p in parts]
    return _sequencer_exchange(
        parts, [3 * n for n in ns],
        lambda x, y, c: [(*chip, c) for chip in _other_chips(x, y)],
        lambda x, y, c, w: [(2 * chip[0] + chip[1], j, ns[w], (*chip, c)) for j, chip in enumerate(_other_chips(x, y))],
        name, collective_id)


def add_sibling(grad, landed, core, name):
    n = landed.shape[0] // 4
    cols = grad.shape[1]

    def body(c_ref, g_ref, l_ref, o_ref):
        o_ref[...] = (g_ref[...].astype(F32) + l_ref[...].astype(F32)).astype(o_ref.dtype)

    grid_spec = pltpu.PrefetchScalarGridSpec(
        num_scalar_prefetch=1, grid=(4,),
        in_specs=[pl.BlockSpec((n, cols), lambda q, c_ref: (2 * q + c_ref[0], 0)),
                  pl.BlockSpec((n, cols), lambda q, c_ref: (q, 0))],
        out_specs=pl.BlockSpec((n, cols), lambda q, c_ref: (q, 0)))
    return pl.pallas_call(
        body, out_shape=jax.ShapeDtypeStruct(landed.shape, landed.dtype), grid_spec=grid_spec,
        compiler_params=_params("arbitrary"), name=name)(core, grad, landed)


def adamw_shard(layer, w, m, v, part, landed, chip, earlier, name):
    n = landed.shape[0] // 3
    cols = w.shape[1]
    c1 = 1.0 / (1.0 - ADAM_B1 ** ADAM_STEP)
    c2 = 1.0 / (1.0 - ADAM_B2 ** ADAM_STEP)

    def body(q_ref, w_ref, m_ref, v_ref, p_ref, l0_ref, l1_ref, l2_ref, *rest):
        g_ref, d_ref, nm_ref, nv_ref = rest[-4:]
        gv = ((p_ref[...].astype(F32) + l0_ref[...].astype(F32)) + l1_ref[...].astype(F32)) + l2_ref[...].astype(F32)
        nm = ADAM_B1 * m_ref[...] + (1.0 - ADAM_B1) * gv
        nv = ADAM_B2 * v_ref[...] + (1.0 - ADAM_B2) * (gv * gv)
        g_ref[...] = gv
        nm_ref[...] = nm
        nv_ref[...] = nv
        d_ref[...] = -ADAM_LR * ((nm * c1) / (jnp.sqrt(nv * c2) + ADAM_EPS) + ADAM_WD * w_ref[...])

    own = pl.BlockSpec((n, cols), lambda i, q_ref: (layer, 0))

    def landed_spec(j):
        return pl.BlockSpec((n, cols), lambda i, q_ref: (j, 0))

    in_specs = [own, own, own, pl.BlockSpec((n, cols), lambda i, q_ref: (q_ref[0], 0)),
                landed_spec(0), landed_spec(1), landed_spec(2)]
    args = [chip, w, m, v, part, landed, landed, landed]
    aliases = {}
    if earlier is not None:
        in_specs += [ANY] * 4
        args += list(earlier)
        aliases = {8 + k: k for k in range(4)}
    grid_spec = pltpu.PrefetchScalarGridSpec(
        num_scalar_prefetch=1, grid=(1,), in_specs=in_specs, out_specs=(own, own, own, own))
    shape = jax.ShapeDtypeStruct(w.shape, F32)
    return pl.pallas_call(
        body, out_shape=(shape, shape, shape, shape), grid_spec=grid_spec, input_output_aliases=aliases,
        compiler_params=_params("arbitrary"), name=name)(*args)


def _pack(arrays):
    flat = jnp.concatenate([a.reshape(-1).astype(F32) for a in arrays])
    pad = (-flat.shape[0]) % (8 * LANES)
    return jnp.pad(flat, (0, pad)).reshape(-1, LANES)


def _unpack(slab, shapes):
    flat = slab.reshape(slab.shape[:-2] + (-1,))
    out, off = [], 0
    for shp in shapes:
        size = 1
        for s in shp:
            size *= s
        out.append(flat[..., off:off + size].reshape(flat.shape[:-1] + tuple(shp)))
        off += size
    return out


def kernel(x, mem, norm1_g, mem_norm_g, a_w_in, a_q_g, a_k_g, a_rel_bias, b_w_in, b_b_in, b_conv_w, b_conv_b, b_ln_g, b_ln_b, mq_g, mk_g, w_mem_kv, w_out, norm2_g, w_gate, w_up, w_down, loss_target, m_norm1_g, m_mem_norm_g, m_a_w_in, m_a_q_g, m_a_k_g, m_a_rel_bias, m_b_w_in, m_b_b_in, m_b_conv_w, m_b_conv_b, m_b_ln_g, m_b_ln_b, m_mq_g, m_mk_g, m_w_mem_kv, m_w_out, m_norm2_g, m_w_gate, m_w_up, m_w_down, v_norm1_g, v_mem_norm_g, v_a_w_in, v_a_q_g, v_a_k_g, v_a_rel_bias, v_b_w_in, v_b_b_in, v_b_conv_w, v_b_conv_b, v_b_ln_g, v_b_ln_b, v_mq_g, v_mk_g, v_w_mem_kv, v_w_out, v_norm2_g, v_w_gate, v_w_up, v_w_down):
    batch, seq, d = x.shape
    mtok = mem.shape[1]
    n = batch * seq
    ax, ay, ac = _place()
    me = 4 * ax + 2 * ay + ac
    core_arr = jnp.reshape(ac, (1,)).astype(jnp.int32)
    chip_arr = jnp.reshape(2 * ax + ay, (1,)).astype(jnp.int32)

    def t_bf16(w):
        return jnp.transpose(w).astype(BF16)

    def after(value, *earlier):
        return lax.optimization_barrier((value, *earlier))[0]

    def gather_mix(l, when, name, collective_id):
        srcs = [w_mem_kv[l].astype(BF16), w_out[l].astype(BF16)] + ([t_bf16(b_w_in[0])] if l == 1 else [])
        return gather_weights([after(srcs[0], when)] + srcs[1:], name, collective_id)

    def gather_ffn(l, when, name, collective_id):
        return gather_weights(
            [after(t_bf16(w_gate[l]), when), t_bf16(w_up[l]), w_down[l].astype(BF16)], name, collective_id)

    f_loc = b_b_in.shape[1]
    c_loc = b_conv_b.shape[1]

    def two(g):
        return jnp.concatenate([g, g], axis=-1)

    gq2, gk2 = two(a_q_g), two(a_k_g)
    rel16 = jnp.pad(a_rel_bias[0], ((0, 16 - a_rel_bias.shape[1]), (0, 0)))
    bias = bias_blocks(rel16)

    x0 = x.reshape(n, d)
    mem2 = mem.reshape(batch * mtok, d)
    zero_mem = jnp.zeros_like(mem2)

    saved = []
    xin = x0
    a_win_t, = gather_weights([t_bf16(a_w_in[0])], "gather_in_a", 1)
    wg_t, wu_t, wd, wo, wkv = [None] * 2, [None] * 2, [None] * 2, [None] * 2, [None] * 2
    for l in range(2):
        h = rms_fwd(xin, norm1_g[l:l + 1], name=f"rms1_fwd_{l}")
        mem_n = rms_fwd(mem2, mem_norm_g[l:l + 1], name=f"rms_mem_fwd_{l}")
        gq4 = jnp.tile(mq_g[l:l + 1], (1, 4))
        gk4 = jnp.tile(mk_g[l:l + 1], (1, 4))
        y_conv = None
        if l == 0:
            wkv[0], wo[0] = gather_mix(0, h, "gather_mix_a", 2)
            z = mm_nt(h, a_win_t, name="in_proj_a")
            wg_t[0], wu_t[0], wd[0] = gather_ffn(0, z, "gather_ffn_a", 3)
            cat = attn_fwd(z, gq2, gk2, bias, batch, seq)
            wkv[1], wo[1], b_win_t = gather_mix(1, cat, "gather_mix_b", 4)
            qcol = 3 * TOK_WIDTH // MEM_WIDTH
        else:
            small_shapes = [(f_loc,), (CONV_W, c_loc), (c_loc,), (c_loc,), (c_loc,)]
            gathered = small_exchange(after(_pack([b_b_in, b_conv_w, b_conv_b, b_ln_g, b_ln_b]), xin), reduce=False)
            bb_g, cw_g, cb_g, lg_g, lb_g = _unpack(gathered, small_shapes)
            bb_full = bb_g.reshape(1, -1)
            cw_full = jnp.pad(jnp.transpose(cw_g, (1, 0, 2)).reshape(CONV_W, -1), ((0, 32 - CONV_W), (0, 0)))
            cb_full, lg_full, lb_full = cb_g.reshape(1, -1), lg_g.reshape(1, -1), lb_g.reshape(1, -1)
            z = mm_nt(h, b_win_t, bias=bb_full, name="in_proj_b")
            cat, y_conv = conv_fwd(z, cw_full, cb_full, lg_full, lb_full, batch, seq)
            qcol = 2 * TOK_WIDTH // MEM_WIDTH
        kv = mm_nn(mem_n, wkv[l], name=f"mem_kv_{l}")
        cat = memattn_fwd(z, kv, gq4, gk4, cat, batch, seq, qcol, name=f"memattn_fwd_{l}")
        x1 = mm_nn(cat, wo[l], res=xin, name=f"out_proj_{l}")
        if l == 0:
            wg_t[1], wu_t[1], wd[1] = gather_ffn(1, x1, "gather_ffn_b", 5)
        h2 = rms_fwd(x1, norm2_g[l:l + 1], name=f"rms2_fwd_{l}")
        gate, up, act = gate_up(h2, wg_t[l], wu_t[l], name=f"gate_up_{l}")
        x2 = mm_nn(act, wd[l], res=x1, name=f"down_proj_{l}")
        saved.append(dict(xin=xin, h=h, mem_n=mem_n, kv=kv, gq4=gq4, gk4=gk4, z=z, qcol=qcol, cat=cat, x1=x1, h2=h2,
                          gate=gate, up=up, act=act, y_conv=y_conv))
        xin = x2

    dx, dx_b, loss_blk = loss_head(xin, loss_target.reshape(n, d))
    loss = lax.psum(loss_blk[0, 0], ("x", "y", "c"))

    big = {}
    small = {}
    reduced = {}
    groups = 0

    def scatter_siblings(keys):
        nonlocal groups
        gid = groups
        groups += 1
        return gid, keys, scatter_to_sibling([big[k] for k in keys], f"scatter_sibling_{gid}", 8 + 2 * gid)

    def scatter_chips(stage1, when):
        gid, keys, landed1 = stage1
        parts = [add_sibling(after(big[k], when), ld, core_arr, name=f"add_sibling_{k}") for k, ld in zip(keys, landed1)]
        landed2 = scatter_to_chips(parts, f"scatter_chips_{gid}", 9 + 2 * gid)
        for k, p, ld in zip(keys, parts, landed2):
            reduced[k] = (p, ld)
        return parts, landed2

    def rows_of(w, transposed):
        w = jnp.swapaxes(w, 1, 2) if transposed else w
        return w.reshape(w.shape[0] * w.shape[1], w.shape[2])

    sharded = {
        "win0": (2, True), "win1": (6, True), "wkv": (14, False), "wo": (15, False),
        "wg": (17, True), "wu": (18, True), "wd": (19, False)}
    weights = [norm1_g, mem_norm_g, a_w_in, a_q_g, a_k_g, a_rel_bias, b_w_in, b_b_in, b_conv_w, b_conv_b, b_ln_g,
               b_ln_b, mq_g, mk_g, w_mem_kv, w_out, norm2_g, w_gate, w_up, w_down]
    moms = [m_norm1_g, m_mem_norm_g, m_a_w_in, m_a_q_g, m_a_k_g, m_a_rel_bias, m_b_w_in, m_b_b_in, m_b_conv_w,
            m_b_conv_b, m_b_ln_g, m_b_ln_b, m_mq_g, m_mk_g, m_w_mem_kv, m_w_out, m_norm2_g, m_w_gate, m_w_up, m_w_down]
    vels = [v_norm1_g, v_mem_norm_g, v_a_w_in, v_a_q_g, v_a_k_g, v_a_rel_bias, v_b_w_in, v_b_b_in, v_b_conv_w,
            v_b_conv_b, v_b_ln_g, v_b_ln_b, v_mq_g, v_mk_g, v_w_mem_kv, v_w_out, v_norm2_g, v_w_gate, v_w_up, v_w_down]
    updated = {}

    def update_layer(l, when):
        for key, (idx, transposed) in sharded.items():
            if key in ("win0", "win1"):
                if key != f"win{l}":
                    continue
                layer, rkey = 0, key
            else:
                layer, rkey = l, f"{key}{l}"
            part, landed = reduced[rkey]
            updated[key] = adamw_shard(
                layer, after(rows_of(weights[idx], transposed), when), rows_of(moms[idx], transposed),
                rows_of(vels[idx], transposed), part, landed, chip_arr, updated.get(key), name=f"adamw_{rkey}")

    mix_landed = None
    for l in (1, 0):
        sv = saved[l]
        dgate, dup = ffn_bwd_act(dx_b, wd[l], sv["gate"], sv["up"], name=f"ffn_bwd_act_{l}")
        big[f"wd{l}"] = mm_tn(sv["act"], dx_b, name=f"grad_wd_{l}")
        if l == 0:
            dgate = after(dgate, *mix_landed)
        dh2 = mm2_nn(dgate, wg_t[l], dup, wu_t[l], name=f"ffn_bwd_h_{l}")
        if l == 0:
            update_layer(1, dh2)
        big[f"wg{l}"] = mm_tn(dgate, sv["h2"], name=f"grad_wg_{l}")
        big[f"wu{l}"] = mm_tn(dup, sv["h2"], name=f"grad_wu_{l}")
        dx1, dx1_b, small[f"norm2_{l}"] = rms_bwd(dh2, sv["x1"], norm2_g[l:l + 1], dx, name=f"rms2_bwd_{l}")
        big[f"wo{l}"] = mm_tn(sv["cat"], dx1_b, name=f"grad_wo_{l}")
        stage1 = scatter_siblings([f"wd{l}", f"wg{l}", f"wu{l}", f"wo{l}"])
        dcat = mm_nt(dx1_b, wo[l], name=f"out_proj_bwd_{l}")
        parts, ffn_landed = scatter_chips(stage1, dcat)
        dcat = after(dcat, *parts)
        if l == 0:
            dz, dbias, small["a_q"], small["a_k"] = attn_bwd(sv["z"], dcat, gq2, gk2, bias, batch, seq)
            small["rel"] = bias_grad(dbias)
            win_t = a_win_t
        else:
            dz, small["cw"], small["csum"] = conv_bwd(sv["z"], sv["y_conv"], dcat, cw_full, lg_full, lb_full, batch, seq)
            win_t = b_win_t
        dz = after(dz, *ffn_landed)
        dz, dkv, small[f"mq_{l}"], small[f"mk_{l}"] = memattn_bwd(
            sv["z"], sv["kv"], dcat, sv["gq4"], sv["gk4"], dz, batch, seq, sv["qcol"], name=f"memattn_bwd_{l}")
        if l == 1:
            small["bb"] = col_sum(dz, name="grad_b_in")
        big[f"win{l}"] = mm_tn(dz, sv["h"], name=f"grad_win_{l}")
        big[f"wkv{l}"] = mm_tn(sv["mem_n"], dkv, name=f"grad_wkv_{l}")
        stage1 = scatter_siblings([f"win{l}", f"wkv{l}"])
        dh = mm_nn(dz, win_t, name=f"in_proj_bwd_{l}")
        parts, mix_landed = scatter_chips(stage1, dh)
        dh = after(dh, *parts)
        dmem_n = mm_nt(dkv, wkv[l], out_dtype=F32, name=f"mem_kv_bwd_{l}")
        _, _, small[f"memnorm_{l}"] = rms_bwd(dmem_n, mem2, mem_norm_g[l:l + 1], zero_mem, name=f"rms_mem_bwd_{l}")
        dx, dx_b, small[f"norm1_{l}"] = rms_bwd(dh, sv["xin"], norm1_g[l:l + 1], dx1, name=f"rms1_bwd_{l}")
    grad_x = dx.reshape(batch, seq, d)
    update_layer(0, dx)

    def shaped(rows, idx, transposed):
        shp = weights[idx].shape
        if transposed:
            return jnp.swapaxes(rows.reshape(shp[0], shp[2], shp[1]), 1, 2)
        return rows.reshape(shp)

    def fold(v, groups):
        return jnp.sum(v.reshape(groups, HEAD_DIM), axis=0, keepdims=True)

    heads = a_rel_bias.shape[1]
    small_list = [
        jnp.concatenate([small["norm1_0"], small["norm1_1"]]),
        jnp.concatenate([small["memnorm_0"], small["memnorm_1"]]),
        fold(small["a_q"], 2), fold(small["a_k"], 2), small["rel"][:heads][None],
        small["bb"], small["cw"][:CONV_W][None], small["csum"][0:1], small["csum"][1:2], small["csum"][2:3],
        jnp.concatenate([fold(small["mq_0"], 4), fold(small["mq_1"], 4)]),
        jnp.concatenate([fold(small["mk_0"], 4), fold(small["mk_1"], 4)]),
        jnp.concatenate([small["norm2_0"], small["norm2_1"]]),
    ]
    small_full_shapes = [a.shape for a in small_list]
    summed = _unpack(small_exchange(_pack(small_list), reduce=True), small_full_shapes)
    (g_norm1, g_memnorm, g_aq, g_ak, g_rel, g_bb_full, g_cw_full, g_cb_full, g_lg_full, g_lb_full,
     g_mq, g_mk, g_norm2) = summed
    g_bb = lax.dynamic_slice_in_dim(g_bb_full, me * f_loc, f_loc, axis=1)
    g_cw = lax.dynamic_slice_in_dim(g_cw_full, me * c_loc, c_loc, axis=2)
    g_cb = lax.dynamic_slice_in_dim(g_cb_full, me * c_loc, c_loc, axis=1)
    g_lg = lax.dynamic_slice_in_dim(g_lg_full, me * c_loc, c_loc, axis=1)
    g_lb = lax.dynamic_slice_in_dim(g_lb_full, me * c_loc, c_loc, axis=1)

    grads = [g_norm1, g_memnorm, None, g_aq, g_ak, g_rel, None, g_bb, g_cw, g_cb, g_lg, g_lb,
             g_mq, g_mk, None, None, g_norm2, None, None, None]
    deltas, new_m, new_v = [None] * 20, [None] * 20, [None] * 20
    for key, (idx, transposed) in sharded.items():
        grads[idx], deltas[idx], new_m[idx], new_v[idx] = (shaped(r, idx, transposed) for r in updated[key])

    small_idx = [i for i in range(20) if grads[i] is not None and i not in {idx for idx, _ in sharded.values()}]
    small_shapes2 = [weights[i].shape for i in small_idx]
    dl, nm, nv = adamw(_pack([weights[i] for i in small_idx]), _pack([grads[i] for i in small_idx]),
                       _pack([moms[i] for i in small_idx]), _pack([vels[i] for i in small_idx]), name="adamw_small")
    for i, a, b, cc in zip(small_idx, _unpack(dl, small_shapes2), _unpack(nm, small_shapes2), _unpack(nv, small_shapes2)):
        deltas[i], new_m[i], new_v[i] = a, b, cc

    return (loss, grad_x, *grads, *deltas, *new_m, *new_v)
```

```python
import functools

import jax
import jax.numpy as jnp
from jax import lax
from jax.experimental import pallas as pl
from jax.experimental.pallas import tpu as pltpu
from jax.experimental.pallas import tpu_sc as plsc

F32 = jnp.float32
BF16 = jnp.bfloat16
HIGHEST = lax.Precision.HIGHEST
MESH = pl.DeviceIdType.MESH
ANY = pl.BlockSpec(memory_space=pl.ANY)

N_DEV = 8
D_MODEL = 1024
HEAD_DIM = 64
TOK_WIDTH = 768
MEM_WIDTH = 256
CHUNK = 64
Q_BLOCK = 256
KEY_WIN = 768
BAND = 576
N_REL = 192
CONV_W = 31
CONV_HALO = 32
NORM_EPS = 1e-6
NEG_INF = -1e30
ATTN_SCALE = HEAD_DIM ** -0.5
LANES = 128
ROW_TILE = 512
VMEM_LIMIT = 56 * 1024 * 1024

ADAM_LR, ADAM_B1, ADAM_B2, ADAM_EPS, ADAM_WD, ADAM_STEP = 0.001, 0.9, 0.999, 1e-08, 0.01, 10


def _params(*sem):
    return pltpu.CompilerParams(dimension_semantics=sem, vmem_limit_bytes=VMEM_LIMIT)


def _row_tile(m):
    return ROW_TILE if m % ROW_TILE == 0 else m


def _col_tile(n, cap=1408):
    best = None
    for t in range(LANES, min(n, cap) + 1, LANES):
        if n % t == 0:
            best = t
    return best if best is not None else n


def _dot(a, b, ca, cb):
    return lax.dot_general(a, b, (((ca,), (cb,)), ((), ())), preferred_element_type=F32)


def _sigmoid(x):
    return 0.5 * jnp.tanh(0.5 * x) + 0.5


def mm_nt(a, b, bias=None, out_dtype=BF16, name="mm_nt"):
    m, k = a.shape
    n = b.shape[0]
    tm, tn = _row_tile(m), _col_tile(n)

    def body(*refs):
        a_ref, b_ref = refs[0], refs[1]
        o_ref = refs[-1]
        acc = _dot(a_ref[...].astype(BF16), b_ref[...].astype(BF16), 1, 1)
        if bias is not None:
            acc = acc + refs[2][...]
        o_ref[...] = acc.astype(o_ref.dtype)

    in_specs = [pl.BlockSpec((tm, k), lambda j, i: (i, 0)), pl.BlockSpec((tn, k), lambda j, i: (j, 0))]
    args = [a, b]
    if bias is not None:
        in_specs.append(pl.BlockSpec((1, tn), lambda j, i: (0, j)))
        args.append(bias)
    return pl.pallas_call(
        body, out_shape=jax.ShapeDtypeStruct((m, n), out_dtype), grid=(n // tn, m // tm),
        in_specs=in_specs, out_specs=pl.BlockSpec((tm, tn), lambda j, i: (i, j)),
        compiler_params=_params("parallel", "arbitrary"), name=name)(*args)


def mm_nn(a, b, res=None, out_dtype=F32, name="mm_nn"):
    m, k = a.shape
    n = b.shape[1]
    tm, tn = _row_tile(m), _col_tile(n, 1024)

    def body(*refs):
        a_ref, b_ref = refs[0], refs[1]
        o_ref = refs[-1]
        acc = _dot(a_ref[...].astype(BF16), b_ref[...].astype(BF16), 1, 0)
        if res is not None:
            acc = acc + refs[2][...]
        o_ref[...] = acc.astype(o_ref.dtype)

    in_specs = [pl.BlockSpec((tm, k), lambda j, i: (i, 0)), pl.BlockSpec((k, tn), lambda j, i: (0, j))]
    args = [a, b]
    if res is not None:
        in_specs.append(pl.BlockSpec((tm, tn), lambda j, i: (i, j)))
        args.append(res)
    return pl.pallas_call(
        body, out_shape=jax.ShapeDtypeStruct((m, n), out_dtype), grid=(n // tn, m // tm),
        in_specs=in_specs, out_specs=pl.BlockSpec((tm, tn), lambda j, i: (i, j)),
        compiler_params=_params("parallel", "arbitrary"), name=name)(*args)


def mm2_nn(a1, b1, a2, b2, name="mm2_nn"):
    m, k = a1.shape
    n = b1.shape[1]
    tm = _row_tile(m)

    def body(a1_ref, b1_ref, a2_ref, b2_ref, o_ref):
        o_ref[...] = _dot(a1_ref[...], b1_ref[...], 1, 0) + _dot(a2_ref[...], b2_ref[...], 1, 0)

    a_spec = pl.BlockSpec((tm, k), lambda i: (i, 0))
    b_spec = pl.BlockSpec((k, n), lambda i: (0, 0))
    return pl.pallas_call(
        body, out_shape=jax.ShapeDtypeStruct((m, n), F32), grid=(m // tm,),
        in_specs=[a_spec, b_spec, a_spec, b_spec], out_specs=pl.BlockSpec((tm, n), lambda i: (i, 0)),
        compiler_params=_params("parallel"), name=name)(a1, b1, a2, b2)


def mm_tn(a, b, out_dtype=BF16, name="mm_tn"):
    t, r = a.shape
    c = b.shape[1]
    tr = _col_tile(r, 512)

    def body(a_ref, b_ref, o_ref):
        o_ref[...] = _dot(a_ref[...].astype(BF16), b_ref[...].astype(BF16), 0, 0).astype(o_ref.dtype)

    return pl.pallas_call(
        body, out_shape=jax.ShapeDtypeStruct((r, c), out_dtype), grid=(r // tr,),
        in_specs=[pl.BlockSpec((t, tr), lambda i: (0, i)), pl.BlockSpec((t, c), lambda i: (0, 0))],
        out_specs=pl.BlockSpec((tr, c), lambda i: (i, 0)),
        compiler_params=_params("parallel"), name=name)(a, b)


def _resident(shape):
    return pl.BlockSpec(shape, lambda i: (0, 0), pipeline_mode=pl.Buffered(1))


def proj_norm(a, b, res, gain, name):
    m, k = a.shape
    n = b.shape[1]
    tm = _row_tile(m)

    def body(a_ref, b_ref, res_ref, g_ref, x_ref, h_ref):
        xv = res_ref[...] + _dot(a_ref[...], b_ref[...], 1, 0)
        x_ref[...] = xv
        r = lax.rsqrt(jnp.mean(xv * xv, axis=-1, keepdims=True) + NORM_EPS)
        h_ref[...] = (xv * r * g_ref[...]).astype(BF16)

    row = pl.BlockSpec((tm, n), lambda i: (i, 0))
    return pl.pallas_call(
        body, out_shape=(jax.ShapeDtypeStruct((m, n), F32), jax.ShapeDtypeStruct((m, n), BF16)), grid=(m // tm,),
        in_specs=[pl.BlockSpec((tm, k), lambda i: (i, 0)), _resident((k, n)), row, _resident((1, n))],
        out_specs=(row, row), compiler_params=_params("parallel"), name=name)(a, b, res, gain)


def proj_loss(a, b, res, target, name):
    m, k = a.shape
    n = b.shape[1]
    tm = _row_tile(m)
    nt = m // tm

    def body(a_ref, b_ref, res_ref, t_ref, dy_ref, dyb_ref, l_ref, acc_ref):
        i = pl.program_id(0)

        @pl.when(i == 0)
        def _():
            acc_ref[...] = jnp.zeros_like(acc_ref)

        err = res_ref[...] + _dot(a_ref[...], b_ref[...], 1, 0) - t_ref[...]
        dy = err * (1.0 / n)
        dy_ref[...] = dy
        dyb_ref[...] = dy.astype(BF16)
        acc_ref[...] += jnp.sum(err * err, axis=0, keepdims=True)

        @pl.when(i == nt - 1)
        def _():
            total = jnp.sum(acc_ref[...], axis=-1, keepdims=True) * (0.5 / n)
            l_ref[...] = jnp.broadcast_to(total, l_ref.shape)

    row = pl.BlockSpec((tm, n), lambda i: (i, 0))
    return pl.pallas_call(
        body, out_shape=(jax.ShapeDtypeStruct((m, n), F32), jax.ShapeDtypeStruct((m, n), BF16),
                         jax.ShapeDtypeStruct((8, LANES), F32)), grid=(nt,),
        in_specs=[pl.BlockSpec((tm, k), lambda i: (i, 0)), _resident((k, n)), row, row],
        out_specs=(row, row, pl.BlockSpec((8, LANES), lambda i: (0, 0))),
        scratch_shapes=[pltpu.VMEM((1, n), F32)],
        compiler_params=_params("arbitrary"), name=name)(a, b, res, target)


def proj_rms_bwd(pairs, x, gain, dres, name):
    m, n = x.shape
    tm = _row_tile(m)
    np_ = len(pairs)

    def body(*refs):
        ab = refs[:2 * np_]
        x_ref, g_ref, dres_ref, dx_ref, dxb_ref, dg_ref = refs[2 * np_:]

        @pl.when(pl.program_id(0) == 0)
        def _():
            dg_ref[...] = jnp.zeros_like(dg_ref)

        dhv = _dot(ab[0][...], ab[1][...], 1, 0)
        for p in range(1, np_):
            dhv = dhv + _dot(ab[2 * p][...], ab[2 * p + 1][...], 1, 0)
        xv = x_ref[...]
        r = lax.rsqrt(jnp.mean(xv * xv, axis=-1, keepdims=True) + NORM_EPS)
        xhat = xv * r
        dg_ref[...] += jnp.sum(dhv * xhat, axis=0, keepdims=True)
        dxhat = dhv * g_ref[...]
        dx = dres_ref[...] + r * (dxhat - xhat * jnp.mean(dxhat * xhat, axis=-1, keepdims=True))
        dx_ref[...] = dx
        dxb_ref[...] = dx.astype(BF16)

    row = pl.BlockSpec((tm, n), lambda i: (i, 0))
    in_specs, args = [], []
    for a, b in pairs:
        in_specs += [pl.BlockSpec((tm, a.shape[1]), lambda i: (i, 0)), _resident(b.shape)]
        args += [a, b]
    return pl.pallas_call(
        body, out_shape=(jax.ShapeDtypeStruct((m, n), F32), jax.ShapeDtypeStruct((m, n), BF16),
                         jax.ShapeDtypeStruct((1, n), F32)), grid=(m // tm,),
        in_specs=in_specs + [row, _resident((1, n)), row], out_specs=(row, row, pl.BlockSpec((1, n), lambda i: (0, 0))),
        compiler_params=_params("arbitrary"), name=name)(*args, x, gain, dres)


def rms_fwd(x, g, name="rms_fwd"):
    n, d = x.shape
    tm = _row_tile(n)

    def body(x_ref, g_ref, o_ref):
        xv = x_ref[...]
        r = lax.rsqrt(jnp.mean(xv * xv, axis=-1, keepdims=True) + NORM_EPS)
        o_ref[...] = (xv * r * g_ref[...]).astype(o_ref.dtype)

    return pl.pallas_call(
        body, out_shape=jax.ShapeDtypeStruct((n, d), BF16), grid=(n // tm,),
        in_specs=[pl.BlockSpec((tm, d), lambda i: (i, 0)), pl.BlockSpec((1, d), lambda i: (0, 0))],
        out_specs=pl.BlockSpec((tm, d), lambda i: (i, 0)),
        compiler_params=_params("parallel"), name=name)(x, g)


def rms_bwd(dh, x, g, dres, name="rms_bwd"):
    n, d = x.shape
    tm = _row_tile(n)

    def body(dh_ref, x_ref, g_ref, dres_ref, dx_ref, dxb_ref, dg_ref):
        @pl.when(pl.program_id(0) == 0)
        def _():
            dg_ref[...] = jnp.zeros_like(dg_ref)

        xv = x_ref[...]
        dhv = dh_ref[...].astype(F32)
        r = lax.rsqrt(jnp.mean(xv * xv, axis=-1, keepdims=True) + NORM_EPS)
        xhat = xv * r
        dg_ref[...] += jnp.sum(dhv * xhat, axis=0, keepdims=True)
        dxhat = dhv * g_ref[...]
        mean_t = jnp.mean(dxhat * xhat, axis=-1, keepdims=True)
        dx = dres_ref[...] + r * (dxhat - xhat * mean_t)
        dx_ref[...] = dx
        dxb_ref[...] = dx.astype(BF16)

    row = pl.BlockSpec((tm, d), lambda i: (i, 0))
    vec = pl.BlockSpec((1, d), lambda i: (0, 0))
    return pl.pallas_call(
        body, out_shape=(jax.ShapeDtypeStruct((n, d), F32), jax.ShapeDtypeStruct((n, d), BF16),
                         jax.ShapeDtypeStruct((1, d), F32)), grid=(n // tm,),
        in_specs=[row, row, vec, row], out_specs=(row, row, vec),
        compiler_params=_params("arbitrary"), name=name)(dh, x, g, dres)


def gate_up(h2, wg_t, wu_t, name="gate_up"):
    n, d = h2.shape
    f = wg_t.shape[0]
    tm, tn = _row_tile(n), _col_tile(f)

    def body(h_ref, wg_ref, wu_ref, g_ref, u_ref, a_ref):
        hv = h_ref[...]
        gv = _dot(hv, wg_ref[...], 1, 1)
        uv = _dot(hv, wu_ref[...], 1, 1)
        g_ref[...] = gv.astype(BF16)
        u_ref[...] = uv.astype(BF16)
        a_ref[...] = (gv * _sigmoid(gv) * uv).astype(BF16)

    w_spec = pl.BlockSpec((tn, d), lambda j, i: (j, 0))
    o_spec = pl.BlockSpec((tm, tn), lambda j, i: (i, j))
    o_shape = jax.ShapeDtypeStruct((n, f), BF16)
    return pl.pallas_call(
        body, out_shape=(o_shape, o_shape, o_shape), grid=(f // tn, n // tm),
        in_specs=[pl.BlockSpec((tm, d), lambda j, i: (i, 0)), w_spec, w_spec], out_specs=(o_spec, o_spec, o_spec),
        compiler_params=_params("parallel", "arbitrary"), name=name)(h2, wg_t, wu_t)


def ffn_bwd_act(dx, wd, gate, up, name="ffn_bwd_act"):
    n, d = dx.shape
    f = wd.shape[0]
    tm, tn = _row_tile(n), _col_tile(f)

    def body(dx_ref, wd_ref, g_ref, u_ref, dg_ref, du_ref):
        dact = _dot(dx_ref[...].astype(BF16), wd_ref[...], 1, 1)
        gv = g_ref[...].astype(F32)
        uv = u_ref[...].astype(F32)
        sg = _sigmoid(gv)
        dg_ref[...] = (dact * uv * sg * (1.0 + gv * (1.0 - sg))).astype(BF16)
        du_ref[...] = (dact * gv * sg).astype(BF16)

    t_spec = pl.BlockSpec((tm, tn), lambda j, i: (i, j))
    o_shape = jax.ShapeDtypeStruct((n, f), BF16)
    return pl.pallas_call(
        body, out_shape=(o_shape, o_shape), grid=(f // tn, n // tm),
        in_specs=[pl.BlockSpec((tm, d), lambda j, i: (i, 0)), pl.BlockSpec((tn, d), lambda j, i: (j, 0)), t_spec, t_spec],
        out_specs=(t_spec, t_spec),
        compiler_params=_params("parallel", "arbitrary"), name=name)(dx, wd, gate, up)


def _group_masks(width):
    lane = lax.broadcasted_iota(jnp.int32, (1, width), 1)
    return [(lane >= HEAD_DIM * g) & (lane < HEAD_DIM * (g + 1)) for g in range(width // HEAD_DIM)]


def _group_sum(x, masks):
    out = jnp.zeros_like(x)
    for msk in masks:
        s = jnp.sum(jnp.where(msk, x, 0.0), axis=-1, keepdims=True)
        out = jnp.where(msk, s, out)
    return out


def _head_norm(x, gain, masks):
    r = lax.rsqrt(_group_sum(x * x, masks) * (1.0 / HEAD_DIM) + NORM_EPS)
    xhat = x * r
    return xhat * gain, xhat, r


def _head_norm_bwd(dxn, xhat, r, gain, masks):
    dgain = jnp.sum(dxn * xhat, axis=0, keepdims=True)
    dxhat = dxn * gain
    mean_t = _group_sum(dxhat * xhat, masks) * (1.0 / HEAD_DIM)
    return r * (dxhat - xhat * mean_t), dgain


def _softmax_rows(s):
    e = jnp.exp(s - jnp.max(s, axis=-1, keepdims=True))
    return e * (1.0 / jnp.sum(e, axis=-1, keepdims=True))


def _rel_onehot():
    col = lax.broadcasted_iota(jnp.int32, (1, KEY_WIN), 1)
    off = jnp.where(col < KEY_WIN - LANES, col, col - KEY_WIN)
    idx = jnp.clip(8 * CHUNK - off, -(CHUNK - 1), LANES) + (CHUNK - 1)
    return (lax.broadcasted_iota(jnp.int32, (N_REL, KEY_WIN), 0) == idx).astype(F32)


def bias_blocks(rel16):
    heads = TOK_WIDTH // HEAD_DIM

    def body(rel_ref, o_ref, u_ref):
        u_ref[...] = jnp.dot(rel_ref[...], _rel_onehot(), precision=HIGHEST, preferred_element_type=F32)
        row = lax.broadcasted_iota(jnp.int32, (CHUNK, KEY_WIN), 0)
        col = lax.broadcasted_iota(jnp.int32, (CHUNK, KEY_WIN), 1)
        for h in range(heads):
            xv = jnp.broadcast_to(u_ref[h:h + 1, :], (CHUNK, KEY_WIN))
            for b in range(6):
                xv = jnp.where(((row >> b) & 1) == 1, pltpu.roll(xv, 1 << b, axis=1), xv)
            xv = jnp.where(col < BAND, xv, NEG_INF)
            for i in range(Q_BLOCK // CHUNK):
                o_ref[h, CHUNK * i:CHUNK * (i + 1), :] = pltpu.roll(xv, CHUNK * i, axis=1) if i else xv

    return pl.pallas_call(
        body, out_shape=jax.ShapeDtypeStruct((heads, Q_BLOCK, KEY_WIN), F32),
        scratch_shapes=[pltpu.VMEM((16, KEY_WIN), F32)], name="bias_blocks")(rel16)


def bias_grad(dbias):
    heads = dbias.shape[0]

    def body(db_ref, o_ref, y_ref):
        y_ref[...] = jnp.zeros_like(y_ref)
        row = lax.broadcasted_iota(jnp.int32, (CHUNK, KEY_WIN), 0)
        for h in range(heads):
            fv = db_ref[h, 0:CHUNK, :]
            for i in range(1, Q_BLOCK // CHUNK):
                fv = fv + pltpu.roll(db_ref[h, CHUNK * i:CHUNK * (i + 1), :], KEY_WIN - CHUNK * i, axis=1)
            for b in range(6):
                fv = jnp.where(((row >> b) & 1) == 1, pltpu.roll(fv, KEY_WIN - (1 << b), axis=1), fv)
            y_ref[h:h + 1, :] = jnp.sum(fv, axis=0, keepdims=True)
        o_ref[...] = lax.dot_general(y_ref[...], _rel_onehot(), (((1,), (1,)), ((), ())),
                                     precision=HIGHEST, preferred_element_type=F32)

    return pl.pallas_call(
        body, out_shape=jax.ShapeDtypeStruct((16, N_REL), F32),
        scratch_shapes=[pltpu.VMEM((16, KEY_WIN), F32)], name="bias_grad")(dbias)


def _attn_windows(seq):
    out = []
    for j in range(seq // Q_BLOCK):
        r0 = j * Q_BLOCK
        k0 = max(0, r0 - 8 * CHUNK)
        width = r0 + Q_BLOCK - k0
        out.append((r0, k0, width, KEY_WIN - width))
    return out


def attn_fwd(z, gq2, gk2, bias, batch, seq):
    n = z.shape[0]
    pairs = TOK_WIDTH // LANES

    def body(q_ref, k_ref, v_ref, gq_ref, gk_ref, b_ref, o_ref, qs_s, kn_s):
        masks = _group_masks(LANES)
        qs_s[...] = (_head_norm(q_ref[...].astype(F32), gq_ref[...], masks)[0] * ATTN_SCALE).astype(BF16)
        kn_s[...] = _head_norm(k_ref[...].astype(F32), gk_ref[...], masks)[0].astype(BF16)
        for r0, k0, width, c0 in _attn_windows(seq):
            qb = qs_s[r0:r0 + Q_BLOCK, :]
            kw = kn_s[k0:k0 + width, :]
            vw = v_ref[k0:k0 + width, :]
            out = jnp.zeros((Q_BLOCK, LANES), F32)
            for h, msk in enumerate(masks):
                qh = jnp.where(msk, qb, jnp.zeros_like(qb))
                s = _dot(qh, kw, 1, 1) + b_ref[h, :, c0:KEY_WIN]
                p = _softmax_rows(s).astype(BF16)
                out = jnp.where(msk, _dot(p, vw, 1, 0), out)
            o_ref[r0:r0 + Q_BLOCK, :] = out.astype(o_ref.dtype)

    def col(off):
        return pl.BlockSpec((seq, LANES), lambda b, p: (b, off + p))

    vec = pl.BlockSpec((1, LANES), lambda b, p: (0, 0))
    return pl.pallas_call(
        body, out_shape=jax.ShapeDtypeStruct((n, D_MODEL), BF16), grid=(batch, pairs),
        in_specs=[col(0), col(pairs), col(2 * pairs), vec, vec,
                  pl.BlockSpec((2, Q_BLOCK, KEY_WIN), lambda b, p: (p, 0, 0))],
        out_specs=pl.BlockSpec((seq, LANES), lambda b, p: (b, p)),
        scratch_shapes=[pltpu.VMEM((seq, LANES), BF16), pltpu.VMEM((seq, LANES), BF16)],
        compiler_params=_params("parallel", "arbitrary"), name="attn_fwd")(z, z, z, gq2, gk2, bias)


def attn_bwd(z, dcat, gq2, gk2, bias, batch, seq):
    n = z.shape[0]
    pairs = TOK_WIDTH // LANES

    def body(q_ref, k_ref, v_ref, do_ref, gq_ref, gk_ref, b_ref,
             dz_ref, db_ref, dgq_ref, dgk_ref, qs_s, kn_s, dqn_s, dkn_s, dv_s, dk_o, dv_o):
        pi, bi, which = pl.program_id(0), pl.program_id(1), pl.program_id(2)

        @pl.when(which == 0)
        def _():
            masks = _group_masks(LANES)

            @pl.when(bi == 0)
            def _():
                db_ref[...] = jnp.zeros_like(db_ref)

            @pl.when((bi == 0) & (pi == 0))
            def _():
                dgq_ref[...] = jnp.zeros_like(dgq_ref)
                dgk_ref[...] = jnp.zeros_like(dgk_ref)

            qn, qhat, rq = _head_norm(q_ref[...].astype(F32), gq_ref[...], masks)
            kn, khat, rk = _head_norm(k_ref[...].astype(F32), gk_ref[...], masks)
            qs_s[...] = (qn * ATTN_SCALE).astype(BF16)
            kn_s[...] = kn.astype(BF16)
            dkn_s[...] = jnp.zeros_like(dkn_s)
            dv_s[...] = jnp.zeros_like(dv_s)
            for r0, k0, width, c0 in _attn_windows(seq):
                qb = qs_s[r0:r0 + Q_BLOCK, :]
                dob = do_ref[r0:r0 + Q_BLOCK, :]
                kw = kn_s[k0:k0 + width, :]
                vw = v_ref[k0:k0 + width, :]
                dq_acc = jnp.zeros((Q_BLOCK, LANES), F32)
                dk_acc = jnp.zeros((width, LANES), F32)
                dv_acc = jnp.zeros((width, LANES), F32)
                for h, msk in enumerate(masks):
                    qh = jnp.where(msk, qb, jnp.zeros_like(qb))
                    doh = jnp.where(msk, dob, jnp.zeros_like(dob))
                    p = _softmax_rows(_dot(qh, kw, 1, 1) + b_ref[h, :, c0:KEY_WIN])
                    dp = _dot(doh, vw, 1, 1)
                    ds = p * (dp - jnp.sum(p * dp, axis=-1, keepdims=True))
                    db_ref[h, :, c0:KEY_WIN] += ds
                    dsb = ds.astype(BF16)
                    dq_acc = jnp.where(msk, _dot(dsb, kw, 1, 0), dq_acc)
                    dk_acc = jnp.where(msk, _dot(dsb, qb, 0, 0), dk_acc)
                    dv_acc = jnp.where(msk, _dot(p.astype(BF16), dob, 0, 0), dv_acc)
                dqn_s[r0:r0 + Q_BLOCK, :] = dq_acc * ATTN_SCALE
                dkn_s[k0:k0 + width, :] += dk_acc
                dv_s[k0:k0 + width, :] += dv_acc
            dq, dgq = _head_norm_bwd(dqn_s[...], qhat, rq, gq_ref[...], masks)
            dk, dgk = _head_norm_bwd(dkn_s[...], khat, rk, gk_ref[...], masks)
            dz_ref[...] = dq.astype(dz_ref.dtype)
            dk_o[...] = dk.astype(dk_o.dtype)
            dv_o[...] = dv_s[...].astype(dv_o.dtype)
            dgq_ref[...] += dgq
            dgk_ref[...] += dgk

        @pl.when(which == 1)
        def _():
            dz_ref[...] = dk_o[...]

        @pl.when(which == 2)
        def _():
            dz_ref[...] = dv_o[...]

    def ahead(p, b, t):
        nb = b + jnp.where(t > 0, 1, 0)
        wrap = jnp.where(nb >= batch, 1, 0)
        return jnp.minimum(p + wrap, pairs - 1), nb - wrap * batch

    def col(off):
        def index(p, b, t):
            np_, nb = ahead(p, b, t)
            return nb, off + np_
        return pl.BlockSpec((seq, LANES), index)

    vec = pl.BlockSpec((1, LANES), lambda p, b, t: (0, 0))
    blk = pl.BlockSpec((2, Q_BLOCK, KEY_WIN), lambda p, b, t: (p, 0, 0))
    blk_in = pl.BlockSpec((2, Q_BLOCK, KEY_WIN), lambda p, b, t: (ahead(p, b, t)[0], 0, 0))
    v_shape = jax.ShapeDtypeStruct((1, LANES), F32)
    return pl.pallas_call(
        body,
        out_shape=(jax.ShapeDtypeStruct(z.shape, BF16), jax.ShapeDtypeStruct(bias.shape, F32), v_shape, v_shape),
        grid=(pairs, batch, 3),
        in_specs=[col(0), col(pairs), col(2 * pairs), col(0), vec, vec, blk_in],
        out_specs=(pl.BlockSpec((seq, LANES), lambda p, b, t: (b, t * pairs + p)), blk, vec, vec),
        scratch_shapes=[pltpu.VMEM((seq, LANES), BF16), pltpu.VMEM((seq, LANES), BF16),
                        pltpu.VMEM((seq, LANES), F32), pltpu.VMEM((seq, LANES), F32), pltpu.VMEM((seq, LANES), F32),
                        pltpu.VMEM((seq, LANES), BF16), pltpu.VMEM((seq, LANES), BF16)],
        compiler_params=_params("arbitrary", "arbitrary", "arbitrary"), name="attn_bwd")(
            z, z, z, dcat, gq2, gk2, bias)


MEM_ROWS = 512


def memattn_fwd(z, kv, gq4, gk4, cat, batch, seq, qcol, name):
    mtok = kv.shape[0] // batch
    rows = min(MEM_ROWS, seq)

    def body(q_ref, kv_ref, gq_ref, gk_ref, cat_ref, o_ref):
        del cat_ref
        masks = _group_masks(MEM_WIDTH)
        kn = _head_norm(kv_ref[:, 0:MEM_WIDTH], gk_ref[...], masks)[0].astype(BF16)
        vm = kv_ref[:, MEM_WIDTH:2 * MEM_WIDTH].astype(BF16)
        for t in range(seq // rows):
            sl = slice(t * rows, (t + 1) * rows)
            qs = (_head_norm(q_ref[sl, :].astype(F32), gq_ref[...], masks)[0] * ATTN_SCALE).astype(BF16)
            out = jnp.zeros((rows, MEM_WIDTH), F32)
            for msk in masks:
                qh = jnp.where(msk, qs, jnp.zeros_like(qs))
                p = _softmax_rows(_dot(qh, kn, 1, 1)).astype(BF16)
                out = jnp.where(msk, _dot(p, vm, 1, 0), out)
            o_ref[sl, :] = out.astype(o_ref.dtype)

    vec = pl.BlockSpec((1, MEM_WIDTH), lambda b: (0, 0))
    return pl.pallas_call(
        body, out_shape=jax.ShapeDtypeStruct(cat.shape, cat.dtype), grid=(batch,),
        in_specs=[pl.BlockSpec((seq, MEM_WIDTH), lambda b: (b, qcol)),
                  pl.BlockSpec((mtok, 2 * MEM_WIDTH), lambda b: (b, 0)), vec, vec, ANY],
        out_specs=pl.BlockSpec((seq, MEM_WIDTH), lambda b: (b, TOK_WIDTH // MEM_WIDTH)),
        input_output_aliases={4: 0},
        compiler_params=_params("parallel"), name=name)(z, kv, gq4, gk4, cat)


def memattn_bwd(z, kv, dcat, gq4, gk4, dz, batch, seq, qcol, name):
    mtok = kv.shape[0] // batch
    rows = min(MEM_ROWS, seq)

    def body(q_ref, kv_ref, do_ref, gq_ref, gk_ref, dz_in_ref, dq_ref, dkv_ref, dgq_ref, dgk_ref):
        del dz_in_ref
        @pl.when(pl.program_id(0) == 0)
        def _():
            dgq_ref[...] = jnp.zeros_like(dgq_ref)
            dgk_ref[...] = jnp.zeros_like(dgk_ref)

        masks = _group_masks(MEM_WIDTH)
        kn_f, khat, rk = _head_norm(kv_ref[:, 0:MEM_WIDTH], gk_ref[...], masks)
        kn = kn_f.astype(BF16)
        vm = kv_ref[:, MEM_WIDTH:2 * MEM_WIDTH].astype(BF16)
        dkn = jnp.zeros((mtok, MEM_WIDTH), F32)
        dvm = jnp.zeros((mtok, MEM_WIDTH), F32)
        dgq = jnp.zeros((1, MEM_WIDTH), F32)
        for t in range(seq // rows):
            sl = slice(t * rows, (t + 1) * rows)
            qn_f, qhat, rq = _head_norm(q_ref[sl, :].astype(F32), gq_ref[...], masks)
            qs = (qn_f * ATTN_SCALE).astype(BF16)
            dob = do_ref[sl, :]
            dqn = jnp.zeros((rows, MEM_WIDTH), F32)
            for msk in masks:
                qh = jnp.where(msk, qs, jnp.zeros_like(qs))
                doh = jnp.where(msk, dob, jnp.zeros_like(dob))
                p = _softmax_rows(_dot(qh, kn, 1, 1))
                dp = _dot(doh, vm, 1, 1)
                ds = p * (dp - jnp.sum(p * dp, axis=-1, keepdims=True))
                dsb = ds.astype(BF16)
                dqn = jnp.where(msk, _dot(dsb, kn, 1, 0), dqn)
                dkn = dkn + jnp.where(msk, _dot(dsb, qs, 0, 0), 0.0)
                dvm = dvm + jnp.where(msk, _dot(p.astype(BF16), dob, 0, 0), 0.0)
            dq, dg = _head_norm_bwd(dqn * ATTN_SCALE, qhat, rq, gq_ref[...], masks)
            dq_ref[sl, :] = dq.astype(dq_ref.dtype)
            dgq = dgq + dg
        dk, dgk = _head_norm_bwd(dkn, khat, rk, gk_ref[...], masks)
        dkv_ref[:, 0:MEM_WIDTH] = dk
        dkv_ref[:, MEM_WIDTH:2 * MEM_WIDTH] = dvm
        dgq_ref[...] += dgq
        dgk_ref[...] += dgk

    vec = pl.BlockSpec((1, MEM_WIDTH), lambda b: (0, 0))
    kv_spec = pl.BlockSpec((mtok, 2 * MEM_WIDTH), lambda b: (b, 0))
    v_shape = jax.ShapeDtypeStruct((1, MEM_WIDTH), F32)
    q_spec = pl.BlockSpec((seq, MEM_WIDTH), lambda b: (b, qcol))
    return pl.pallas_call(
        body,
        out_shape=(jax.ShapeDtypeStruct(dz.shape, dz.dtype), jax.ShapeDtypeStruct(kv.shape, F32), v_shape, v_shape),
        grid=(batch,),
        in_specs=[q_spec, kv_spec, pl.BlockSpec((seq, MEM_WIDTH), lambda b: (b, TOK_WIDTH // MEM_WIDTH)), vec, vec, ANY],
        out_specs=(q_spec, kv_spec, vec, vec),
        input_output_aliases={5: 0},
        compiler_params=_params("arbitrary"), name=name)(z, kv, dcat, gq4, gk4, dz)


CONV_ROWS = 256


def _glu(a_ref, g_ref):
    return a_ref[...].astype(F32) * _sigmoid(g_ref[...].astype(F32))


def _layer_norm_stats(y):
    mu = jnp.mean(y, axis=-1, keepdims=True)
    yc = y - mu
    rstd = lax.rsqrt(jnp.mean(yc * yc, axis=-1, keepdims=True) + NORM_EPS)
    return yc * rstd, rstd


CONV_WIN = CONV_HALO + CONV_ROWS
SUBLANES = 8
SHIFT_ROWS = CONV_WIN - SUBLANES


def _preshift(win, shifted):
    for s in range(1, SUBLANES):
        shifted[s - 1, :, :] = win[s:s + SHIFT_ROWS, :]


TAP_ROWS = 64
TAP_TILES = [(r0, slice(c0, c0 + LANES)) for c0 in range(0, TOK_WIDTH, LANES) for r0 in range(0, CONV_ROWS, TAP_ROWS)]


def _tap(win, shifted, off, r0, lanes):
    s = off % SUBLANES
    base = off - s + r0
    if s == 0:
        return win[base:base + TAP_ROWS, lanes]
    return shifted[s - 1, base:base + TAP_ROWS, lanes]


def _fold_rows(x):
    return jnp.sum(x.reshape(TAP_ROWS // SUBLANES, SUBLANES, LANES), axis=0)


def conv_fwd(z, cw, cb, lg, lb, batch, seq):
    n = z.shape[0]
    nt = seq // CONV_ROWS
    sub = CONV_ROWS // CONV_HALO
    lead = CONV_HALO - (CONV_W - 1)

    def body(a_ref, g_ref, ap_ref, gp_ref, cw_ref, cb_ref, lg_ref, lb_ref, o_ref, y_ref, win, shifted):
        first = pl.program_id(1) == 0
        win[0:CONV_HALO, :] = jnp.where(first, 0.0, _glu(ap_ref, gp_ref))
        win[CONV_HALO:CONV_WIN, :] = _glu(a_ref, g_ref)
        _preshift(win, shifted)
        for r0, lanes in TAP_TILES:
            acc = jnp.zeros((TAP_ROWS, LANES), F32) + cb_ref[:, lanes]
            for w in range(CONV_W):
                acc = acc + _tap(win, shifted, lead + w, r0, lanes) * cw_ref[w:w + 1, lanes]
            y_ref[r0:r0 + TAP_ROWS, lanes] = acc
        yh, _ = _layer_norm_stats(y_ref[...])
        t = yh * lg_ref[...] + lb_ref[...]
        o_ref[...] = (t * _sigmoid(t)).astype(o_ref.dtype)

    def cur(c):
        return pl.BlockSpec((CONV_ROWS, TOK_WIDTH), lambda b, i: (b * nt + i, c))

    def prev(c):
        return pl.BlockSpec((CONV_HALO, TOK_WIDTH), lambda b, i: (jnp.maximum((b * nt + i) * sub - 1, 0), c))

    vec = pl.BlockSpec((1, TOK_WIDTH), lambda b, i: (0, 0))
    return pl.pallas_call(
        body, out_shape=(jax.ShapeDtypeStruct((n, D_MODEL), BF16), jax.ShapeDtypeStruct((n, TOK_WIDTH), F32)),
        grid=(batch, nt),
        in_specs=[cur(0), cur(1), prev(0), prev(1), pl.BlockSpec((32, TOK_WIDTH), lambda b, i: (0, 0)), vec, vec, vec],
        out_specs=(cur(0), cur(0)),
        scratch_shapes=[pltpu.VMEM((CONV_WIN, TOK_WIDTH), F32), pltpu.VMEM((SUBLANES - 1, SHIFT_ROWS, TOK_WIDTH), F32)],
        compiler_params=_params("parallel", "arbitrary"), name="conv_fwd")(z, z, z, z, cw, cb, lg, lb)


def conv_bwd(z, y, dcat, cw, lg, lb, batch, seq):
    n = z.shape[0]
    nt = seq // CONV_ROWS
    sub = CONV_ROWS // CONV_HALO
    lead = CONV_HALO - (CONV_W - 1)
    last_blk = n // CONV_HALO - 1

    def body(a_ref, g_ref, ap_ref, gp_ref, y_ref, yn_ref, do_ref, don_ref, cw_ref, lg_ref, lb_ref,
             dz_ref, dcw_ref, dsm_ref, win, shifted, dyw, dshifted, dg_o):
        b, i, which = pl.program_id(0), pl.program_id(1), pl.program_id(2)

        @pl.when(which == 0)
        def _():
            first, last = i == 0, i == nt - 1

            @pl.when((b == 0) & (i == 0))
            def _():
                dcw_ref[...] = jnp.zeros_like(dcw_ref)
                dsm_ref[...] = jnp.zeros_like(dsm_ref)

            win[0:CONV_HALO, :] = jnp.where(first, 0.0, _glu(ap_ref, gp_ref))
            win[CONV_HALO:CONV_WIN, :] = _glu(a_ref, g_ref)
            _preshift(win, shifted)
            yv = jnp.concatenate([y_ref[...], yn_ref[...]], axis=0)
            yh, rstd = _layer_norm_stats(yv)
            t = yh * lg_ref[...] + lb_ref[...]
            st = _sigmoid(t)
            dout = jnp.concatenate(
                [do_ref[...].astype(F32), jnp.where(last, 0.0, don_ref[...].astype(F32))], axis=0)
            dt = dout * st * (1.0 + t * (1.0 - st))
            dyh = dt * lg_ref[...]
            dy = rstd * (dyh - jnp.mean(dyh, axis=-1, keepdims=True)
                         - yh * jnp.mean(dyh * yh, axis=-1, keepdims=True))
            dyw[...] = dy
            _preshift(dyw, dshifted)
            dsm_ref[0:1, :] += jnp.sum(dy[0:CONV_ROWS], axis=0, keepdims=True)
            dsm_ref[1:2, :] += jnp.sum((dt * yh)[0:CONV_ROWS], axis=0, keepdims=True)
            dsm_ref[2:3, :] += jnp.sum(dt[0:CONV_ROWS], axis=0, keepdims=True)
            for c0 in range(0, TOK_WIDTH, LANES):
                lanes = slice(c0, c0 + LANES)
                dcw_acc = [jnp.zeros((SUBLANES, LANES), F32) for _ in range(CONV_W)]
                for r0 in range(0, CONV_ROWS, TAP_ROWS):
                    dyt = dyw[r0:r0 + TAP_ROWS, lanes]
                    dglu = jnp.zeros((TAP_ROWS, LANES), F32)
                    for w in range(CONV_W):
                        dcw_acc[w] = dcw_acc[w] + _fold_rows(dyt * _tap(win, shifted, lead + w, r0, lanes))
                        dglu = dglu + _tap(dyw, dshifted, CONV_W - 1 - w, r0, lanes) * cw_ref[w:w + 1, lanes]
                    avt = a_ref[r0:r0 + TAP_ROWS, lanes].astype(F32)
                    sgt = _sigmoid(g_ref[r0:r0 + TAP_ROWS, lanes].astype(F32))
                    dz_ref[r0:r0 + TAP_ROWS, lanes] = (dglu * sgt).astype(dz_ref.dtype)
                    dg_o[r0:r0 + TAP_ROWS, lanes] = (dglu * avt * sgt * (1.0 - sgt)).astype(dg_o.dtype)
                for w in range(CONV_W):
                    dcw_ref[w:w + 1, lanes] += jnp.sum(dcw_acc[w], axis=0, keepdims=True)

        @pl.when(which == 1)
        def _():
            dz_ref[...] = dg_o[...]

    def ahead(b, i, t):
        return jnp.minimum(b * nt + i + t, batch * nt - 1)

    def cur(c):
        return pl.BlockSpec((CONV_ROWS, TOK_WIDTH), lambda b, i, t: (ahead(b, i, t), c))

    def prev(c):
        return pl.BlockSpec((CONV_HALO, TOK_WIDTH), lambda b, i, t: (jnp.maximum(ahead(b, i, t) * sub - 1, 0), c))

    nxt = pl.BlockSpec((CONV_HALO, TOK_WIDTH),
                       lambda b, i, t: (jnp.minimum((ahead(b, i, t) + 1) * sub, last_blk), 0))
    vec = pl.BlockSpec((1, TOK_WIDTH), lambda b, i, t: (0, 0))
    full32 = pl.BlockSpec((32, TOK_WIDTH), lambda b, i, t: (0, 0))
    return pl.pallas_call(
        body,
        out_shape=(jax.ShapeDtypeStruct(z.shape, BF16), jax.ShapeDtypeStruct((32, TOK_WIDTH), F32),
                   jax.ShapeDtypeStruct((8, TOK_WIDTH), F32)),
        grid=(batch, nt, 2),
        in_specs=[cur(0), cur(1), prev(0), prev(1), cur(0), nxt, cur(0), nxt, full32, vec, vec],
        out_specs=(pl.BlockSpec((CONV_ROWS, TOK_WIDTH), lambda b, i, t: (b * nt + i, t)), full32,
                   pl.BlockSpec((8, TOK_WIDTH), lambda b, i, t: (0, 0))),
        scratch_shapes=[pltpu.VMEM((CONV_WIN, TOK_WIDTH), F32), pltpu.VMEM((SUBLANES - 1, SHIFT_ROWS, TOK_WIDTH), F32),
                        pltpu.VMEM((CONV_WIN, TOK_WIDTH), F32), pltpu.VMEM((SUBLANES - 1, SHIFT_ROWS, TOK_WIDTH), F32),
                        pltpu.VMEM((CONV_ROWS, TOK_WIDTH), BF16)],
        compiler_params=_params("arbitrary", "arbitrary", "arbitrary"), name="conv_bwd")(
            z, z, z, z, y, y, dcat, dcat, cw, lg, lb)


def loss_head(y, target):
    n, d = y.shape
    tm = _row_tile(n)
    nt = n // tm

    def body(y_ref, t_ref, dy_ref, dyb_ref, l_ref, acc_ref):
        i = pl.program_id(0)

        @pl.when(i == 0)
        def _():
            acc_ref[...] = jnp.zeros_like(acc_ref)

        err = y_ref[...] - t_ref[...]
        dy = err * (1.0 / d)
        dy_ref[...] = dy
        dyb_ref[...] = dy.astype(BF16)
        acc_ref[...] += jnp.sum(err * err, axis=0, keepdims=True)

        @pl.when(i == nt - 1)
        def _():
            total = jnp.sum(acc_ref[...], axis=-1, keepdims=True) * (0.5 / d)
            l_ref[...] = jnp.broadcast_to(total, l_ref.shape)

    row = pl.BlockSpec((tm, d), lambda i: (i, 0))
    return pl.pallas_call(
        body, out_shape=(jax.ShapeDtypeStruct((n, d), F32), jax.ShapeDtypeStruct((n, d), BF16),
                         jax.ShapeDtypeStruct((8, LANES), F32)), grid=(nt,),
        in_specs=[row, row], out_specs=(row, row, pl.BlockSpec((8, LANES), lambda i: (0, 0))),
        scratch_shapes=[pltpu.VMEM((1, d), F32)],
        compiler_params=_params("arbitrary"), name="loss_head")(y, target)


def col_sum(x, name="col_sum"):
    n, c = x.shape
    tm = _row_tile(n)

    def body(x_ref, o_ref):
        @pl.when(pl.program_id(0) == 0)
        def _():
            o_ref[...] = jnp.zeros_like(o_ref)

        o_ref[...] += jnp.sum(x_ref[...].astype(F32), axis=0, keepdims=True)

    return pl.pallas_call(
        body, out_shape=jax.ShapeDtypeStruct((1, c), F32), grid=(n // tm,),
        in_specs=[pl.BlockSpec((tm, c), lambda i: (i, 0))], out_specs=pl.BlockSpec((1, c), lambda i: (0, 0)),
        compiler_params=_params("arbitrary"), name=name)(x)


def adamw(w, g, m, v, name="adamw"):
    rows, cols = w.shape
    tr = rows
    for cand in (512, 256, 128, 64, 32, 16, 8):
        if rows % cand == 0 and rows > cand:
            tr = cand
            break
    c1 = 1.0 / (1.0 - ADAM_B1 ** ADAM_STEP)
    c2 = 1.0 / (1.0 - ADAM_B2 ** ADAM_STEP)

    def body(w_ref, g_ref, m_ref, v_ref, d_ref, nm_ref, nv_ref):
        gv = g_ref[...]
        nm = ADAM_B1 * m_ref[...] + (1.0 - ADAM_B1) * gv
        nv = ADAM_B2 * v_ref[...] + (1.0 - ADAM_B2) * (gv * gv)
        nm_ref[...] = nm
        nv_ref[...] = nv
        d_ref[...] = -ADAM_LR * ((nm * c1) / (jnp.sqrt(nv * c2) + ADAM_EPS) + ADAM_WD * w_ref[...])

    spec = pl.BlockSpec((tr, cols), lambda i: (i, 0))
    shape = jax.ShapeDtypeStruct((rows, cols), F32)
    return pl.pallas_call(
        body, out_shape=(shape, shape, shape), grid=(rows // tr,),
        in_specs=[spec, spec, spec, spec], out_specs=(spec, spec, spec),
        compiler_params=_params("parallel"), name=name)(w, g, m, v)


def _place():
    return lax.axis_index("x"), lax.axis_index("y"), lax.axis_index("c")


def _other_chips(x, y):
    return [(1 - x, y), (x, 1 - y), (1 - x, 1 - y)]


def small_exchange(slab, reduce):
    r = slab.shape[0]

    def body(in_ref, o_ref, *scratch):
        if reduce:
            buf, send_sems, recv_sems = scratch
        else:
            buf = o_ref
            send_sems, recv_sems = scratch
        x, y, c = _place()
        me = 4 * x + 2 * y + c
        buf[me] = in_ref[...]
        copies = []
        for k in range(1, N_DEV):
            peer = (x ^ (k >> 2), y ^ ((k >> 1) & 1), c ^ (k & 1))
            cp = pltpu.make_async_remote_copy(
                src_ref=in_ref, dst_ref=buf.at[me], send_sem=send_sems.at[k - 1], recv_sem=recv_sems.at[k - 1],
                device_id=peer, device_id_type=MESH)
            cp.start()
            copies.append(cp)
        for k in range(1, N_DEV):
            src = 4 * (x ^ (k >> 2)) + 2 * (y ^ ((k >> 1) & 1)) + (c ^ (k & 1))
            pltpu.make_async_remote_copy(
                src_ref=in_ref, dst_ref=buf.at[src], send_sem=send_sems.at[k - 1], recv_sem=recv_sems.at[k - 1],
                device_id=(x, y, c), device_id_type=MESH).wait_recv()
        for cp in copies:
            cp.wait_send()
        if reduce:
            total = buf[0]
            for d in range(1, N_DEV):
                total = total + buf[d]
            o_ref[...] = total

    sems = [pltpu.SemaphoreType.DMA((N_DEV - 1,)), pltpu.SemaphoreType.DMA((N_DEV - 1,))]
    if reduce:
        out_shape = jax.ShapeDtypeStruct((r, LANES), F32)
        scratch = [pltpu.VMEM((N_DEV, r, LANES), F32)] + sems
    else:
        out_shape = jax.ShapeDtypeStruct((N_DEV, r, LANES), F32)
        scratch = sems
    vmem = pl.BlockSpec(memory_space=pltpu.VMEM)
    return pl.pallas_call(
        body, out_shape=out_shape, in_specs=[vmem], out_specs=vmem, scratch_shapes=scratch,
        compiler_params=pltpu.CompilerParams(vmem_limit_bytes=VMEM_LIMIT),
        name="small_reduce" if reduce else "small_gather")(slab)


def gather_weights(shards, name, collective_id):
    nw = len(shards)
    ns = [s.shape[0] for s in shards]
    in_refs = [jax.new_ref(s, memory_space=pltpu.MemorySpace.HBM) for s in shards]
    out_refs = [jax.empty_ref(jax.ShapeDtypeStruct((N_DEV * s.shape[0], s.shape[1]), s.dtype),
                              memory_space=pltpu.MemorySpace.HBM) for s in shards]

    @pl.kernel(mesh=plsc.ScalarSubcoreMesh(axis_name="seq", num_cores=1), name=name,
               scratch_types=(pltpu.SemaphoreType.DMA((nw, 7)), pltpu.SemaphoreType.DMA((nw, 7)),
                              pltpu.SemaphoreType.DMA((nw,))),
               compiler_params=pltpu.CompilerParams(collective_id=collective_id))
    def launch(send_sems, recv_sems, local_sems):
        x, y, c = _place()
        me, sib = (x, y, c), (x, y, 1 - c)
        chips = _other_chips(x, y)
        barrier = pltpu.get_barrier_semaphore()
        for peer in [sib] + [(*chip, c) for chip in chips]:
            pl.semaphore_signal(barrier, inc=1, device_id=peer, device_id_type=MESH)
        pl.semaphore_wait(barrier, 4)

        def rows(w, dev):
            return out_refs[w].at[pl.ds((4 * dev[0] + 2 * dev[1] + dev[2]) * ns[w], ns[w]), :]

        def copy(w, k, block, to, src=None):
            return pltpu.make_async_remote_copy(
                src_ref=rows(w, block) if src is None else src, dst_ref=rows(w, block),
                send_sem=send_sems.at[w, k], recv_sem=recv_sems.at[w, k], device_id=to, device_id_type=MESH)

        started, sends = [], []
        for w in range(nw):
            mine = pltpu.make_async_copy(in_refs[w], rows(w, me), local_sems.at[w])
            mine.start()
            started.append(mine)
            first = [copy(w, 0, me, sib, src=in_refs[w])]
            first += [copy(w, 1 + j, me, (*chip, c), src=in_refs[w]) for j, chip in enumerate(chips)]
            for cp in first:
                cp.start()
            sends += first
        for w in range(nw):
            for j, chip in enumerate(chips):
                copy(w, 1 + j, (*chip, c), me).wait_recv()
                fwd = copy(w, 4 + j, (*chip, c), sib)
                fwd.start()
                sends.append(fwd)
        for w in range(nw):
            copy(w, 0, sib, me).wait_recv()
            for j, chip in enumerate(chips):
                copy(w, 4 + j, (*chip, 1 - c), me).wait_recv()
        for cp in sends:
            cp.wait_send()
        for mine in started:
            mine.wait()

    launch()
    return [r[...] for r in out_refs]


def _sequencer_exchange(sources, out_rows, peers_of, copies_of, name, collective_id):
    nw = len(sources)
    in_refs = [jax.new_ref(s, memory_space=pltpu.MemorySpace.HBM) for s in sources]
    out_refs = [jax.empty_ref(jax.ShapeDtypeStruct((rows, s.shape[1]), s.dtype), memory_space=pltpu.MemorySpace.HBM)
                for rows, s in zip(out_rows, sources)]
    per = len(copies_of(0, 0, 0, 0))

    @pl.kernel(mesh=plsc.ScalarSubcoreMesh(axis_name="seq", num_cores=1), name=name,
               scratch_types=(pltpu.SemaphoreType.DMA((nw, per)), pltpu.SemaphoreType.DMA((nw, per))),
               compiler_params=pltpu.CompilerParams(collective_id=collective_id))
    def launch(send_sems, recv_sems):
        x, y, c = _place()
        peers = peers_of(x, y, c)
        barrier = pltpu.get_barrier_semaphore()
        for peer in peers:
            pl.semaphore_signal(barrier, inc=1, device_id=peer, device_id_type=MESH)
        pl.semaphore_wait(barrier, len(peers))
        copies = []
        for w in range(nw):
            for k, (src_blk, dst_blk, rows, peer) in enumerate(copies_of(x, y, c, w)):
                cp = pltpu.make_async_remote_copy(
                    src_ref=in_refs[w].at[pl.ds(src_blk * rows, rows), :],
                    dst_ref=out_refs[w].at[pl.ds(dst_blk * rows, rows), :],
                    send_sem=send_sems.at[w, k], recv_sem=recv_sems.at[w, k], device_id=peer, device_id_type=MESH)
                cp.start()
                copies.append(cp)
        for cp in copies:
            cp.wait_recv()
        for cp in copies:
            cp.wait_send()

    launch()
    return [r[...] for r in out_refs]


def scatter_to_sibling(grads, name, collective_id):
    ns = [g.shape[0] // N_DEV for g in grads]
    return _sequencer_exchange(
        grads, [4 * n for n in ns],
        lambda x, y, c: [(x, y, 1 - c)],
        lambda x, y, c, w: [(2 * q + 1 - c, q, ns[w], (x, y, 1 - c)) for q in range(4)],
        name, collective_id)


def scatter_to_chips(parts, name, collective_id):
    ns = [p.shape[0] // 4 for p in parts]
    return _sequencer_exchange(
        parts, [3 * n for n in ns],
        lambda x, y, c: [(*chip, c) for chip in _other_chips(x, y)],
        lambda x, y, c, w: [(2 * chip[0] + chip[1], j, ns[w], (*chip, c)) for j, chip in enumerate(_other_chips(x, y))],
        name, collective_id)


def add_sibling(grad, landed, core, name):
    n = landed.shape[0] // 4
    cols = grad.shape[1]

    def body(c_ref, g_ref, l_ref, o_ref):
        o_ref[...] = (g_ref[...].astype(F32) + l_ref[...].astype(F32)).astype(o_ref.dtype)

    grid_spec = pltpu.PrefetchScalarGridSpec(
        num_scalar_prefetch=1, grid=(4,),
        in_specs=[pl.BlockSpec((n, cols), lambda q, c_ref: (2 * q + c_ref[0], 0)),
                  pl.BlockSpec((n, cols), lambda q, c_ref: (q, 0))],
        out_specs=pl.BlockSpec((n, cols), lambda q, c_ref: (q, 0)))
    return pl.pallas_call(
        body, out_shape=jax.ShapeDtypeStruct(landed.shape, landed.dtype), grid_spec=grid_spec,
        compiler_params=_params("arbitrary"), name=name)(core, grad, landed)


def adamw_shard(layer, w, m, v, part, landed, chip, earlier, name):
    n = landed.shape[0] // 3
    cols = w.shape[1]
    c1 = 1.0 / (1.0 - ADAM_B1 ** ADAM_STEP)
    c2 = 1.0 / (1.0 - ADAM_B2 ** ADAM_STEP)

    def body(q_ref, w_ref, m_ref, v_ref, p_ref, l0_ref, l1_ref, l2_ref, *rest):
        g_ref, d_ref, nm_ref, nv_ref = rest[-4:]
        gv = ((p_ref[...].astype(F32) + l0_ref[...].astype(F32)) + l1_ref[...].astype(F32)) + l2_ref[...].astype(F32)
        nm = ADAM_B1 * m_ref[...] + (1.0 - ADAM_B1) * gv
        nv = ADAM_B2 * v_ref[...] + (1.0 - ADAM_B2) * (gv * gv)
        g_ref[...] = gv
        nm_ref[...] = nm
        nv_ref[...] = nv
        d_ref[...] = -ADAM_LR * ((nm * c1) / (jnp.sqrt(nv * c2) + ADAM_EPS) + ADAM_WD * w_ref[...])

    own = pl.BlockSpec((n, cols), lambda i, q_ref: (layer, 0))

    def landed_spec(j):
        return pl.BlockSpec((n, cols), lambda i, q_ref: (j, 0))

    in_specs = [own, own, own, pl.BlockSpec((n, cols), lambda i, q_ref: (q_ref[0], 0)),
                landed_spec(0), landed_spec(1), landed_spec(2)]
    args = [chip, w, m, v, part, landed, landed, landed]
    aliases = {}
    if earlier is not None:
        in_specs += [ANY] * 4
        args += list(earlier)
        aliases = {8 + k: k for k in range(4)}
    grid_spec = pltpu.PrefetchScalarGridSpec(
        num_scalar_prefetch=1, grid=(1,), in_specs=in_specs, out_specs=(own, own, own, own))
    shape = jax.ShapeDtypeStruct(w.shape, F32)
    return pl.pallas_call(
        body, out_shape=(shape, shape, shape, shape), grid_spec=grid_spec, input_output_aliases=aliases,
        compiler_params=_params("arbitrary"), name=name)(*args)


def _pack(arrays):
    flat = jnp.concatenate([a.reshape(-1).astype(F32) for a in arrays])
    pad = (-flat.shape[0]) % (8 * LANES)
    return jnp.pad(flat, (0, pad)).reshape(-1, LANES)


def _unpack(slab, shapes):
    flat = slab.reshape(slab.shape[:-2] + (-1,))
    out, off = [], 0
    for shp in shapes:
        size = 1
        for s in shp:
            size *= s
        out.append(flat[..., off:off + size].reshape(flat.shape[:-1] + tuple(shp)))
        off += size
    return out


def kernel(x, mem, norm1_g, mem_norm_g, a_w_in, a_q_g, a_k_g, a_rel_bias, b_w_in, b_b_in, b_conv_w, b_conv_b, b_ln_g, b_ln_b, mq_g, mk_g, w_mem_kv, w_out, norm2_g, w_gate, w_up, w_down, loss_target, m_norm1_g, m_mem_norm_g, m_a_w_in, m_a_q_g, m_a_k_g, m_a_rel_bias, m_b_w_in, m_b_b_in, m_b_conv_w, m_b_conv_b, m_b_ln_g, m_b_ln_b, m_mq_g, m_mk_g, m_w_mem_kv, m_w_out, m_norm2_g, m_w_gate, m_w_up, m_w_down, v_norm1_g, v_mem_norm_g, v_a_w_in, v_a_q_g, v_a_k_g, v_a_rel_bias, v_b_w_in, v_b_b_in, v_b_conv_w, v_b_conv_b, v_b_ln_g, v_b_ln_b, v_mq_g, v_mk_g, v_w_mem_kv, v_w_out, v_norm2_g, v_w_gate, v_w_up, v_w_down):
    batch, seq, d = x.shape
    mtok = mem.shape[1]
    n = batch * seq
    ax, ay, ac = _place()
    me = 4 * ax + 2 * ay + ac
    core_arr = jnp.reshape(ac, (1,)).astype(jnp.int32)
    chip_arr = jnp.reshape(2 * ax + ay, (1,)).astype(jnp.int32)

    def t_bf16(w):
        return jnp.transpose(w).astype(BF16)

    def after(value, *earlier):
        return lax.optimization_barrier((value, *earlier))[0]

    def gather_mix(l, when, name, collective_id):
        srcs = [w_mem_kv[l].astype(BF16), w_out[l].astype(BF16)] + ([t_bf16(b_w_in[0])] if l == 1 else [])
        return gather_weights([after(srcs[0], when)] + srcs[1:], name, collective_id)

    def gather_ffn(l, when, name, collective_id):
        return gather_weights(
            [after(t_bf16(w_gate[l]), when), t_bf16(w_up[l]), w_down[l].astype(BF16)], name, collective_id)

    f_loc = b_b_in.shape[1]
    c_loc = b_conv_b.shape[1]

    def two(g):
        return jnp.concatenate([g, g], axis=-1)

    gq2, gk2 = two(a_q_g), two(a_k_g)
    rel16 = jnp.pad(a_rel_bias[0], ((0, 16 - a_rel_bias.shape[1]), (0, 0)))
    bias = bias_blocks(rel16)

    x0 = x.reshape(n, d)
    mem2 = mem.reshape(batch * mtok, d)
    zero_mem = jnp.zeros_like(mem2)

    saved = []
    xin = x0
    a_win_t, = gather_weights([t_bf16(a_w_in[0])], "gather_in_a", 1)
    wg_t, wu_t, wd, wo, wkv = [None] * 2, [None] * 2, [None] * 2, [None] * 2, [None] * 2
    h = rms_fwd(xin, norm1_g[0:1], name="rms1_fwd_0")
    target = loss_target.reshape(n, d)
    for l in range(2):
        mem_n = rms_fwd(mem2, mem_norm_g[l:l + 1], name=f"rms_mem_fwd_{l}")
        gq4 = jnp.tile(mq_g[l:l + 1], (1, 4))
        gk4 = jnp.tile(mk_g[l:l + 1], (1, 4))
        y_conv = None
        if l == 0:
            wkv[0], wo[0] = gather_mix(0, h, "gather_mix_a", 2)
            z = mm_nt(h, a_win_t, name="in_proj_a")
            wg_t[0], wu_t[0], wd[0] = gather_ffn(0, z, "gather_ffn_a", 3)
            cat = attn_fwd(z, gq2, gk2, bias, batch, seq)
            wkv[1], wo[1], b_win_t = gather_mix(1, cat, "gather_mix_b", 4)
            qcol = 3 * TOK_WIDTH // MEM_WIDTH
        else:
            small_shapes = [(f_loc,), (CONV_W, c_loc), (c_loc,), (c_loc,), (c_loc,)]
            gathered = small_exchange(after(_pack([b_b_in, b_conv_w, b_conv_b, b_ln_g, b_ln_b]), xin), reduce=False)
            bb_g, cw_g, cb_g, lg_g, lb_g = _unpack(gathered, small_shapes)
            bb_full = bb_g.reshape(1, -1)
            cw_full = jnp.pad(jnp.transpose(cw_g, (1, 0, 2)).reshape(CONV_W, -1), ((0, 32 - CONV_W), (0, 0)))
            cb_full, lg_full, lb_full = cb_g.reshape(1, -1), lg_g.reshape(1, -1), lb_g.reshape(1, -1)
            z = mm_nt(h, b_win_t, bias=bb_full, name="in_proj_b")
            cat, y_conv = conv_fwd(z, cw_full, cb_full, lg_full, lb_full, batch, seq)
            qcol = 2 * TOK_WIDTH // MEM_WIDTH
        kv = mm_nn(mem_n, wkv[l], name=f"mem_kv_{l}")
        cat = memattn_fwd(z, kv, gq4, gk4, cat, batch, seq, qcol, name=f"memattn_fwd_{l}")
        x1, h2 = proj_norm(cat, wo[l], xin, norm2_g[l:l + 1], name=f"out_proj_{l}")
        if l == 0:
            wg_t[1], wu_t[1], wd[1] = gather_ffn(1, x1, "gather_ffn_b", 5)
        gate, up, act = gate_up(h2, wg_t[l], wu_t[l], name=f"gate_up_{l}")
        saved.append(dict(xin=xin, h=h, mem_n=mem_n, kv=kv, gq4=gq4, gk4=gk4, z=z, qcol=qcol, cat=cat, x1=x1, h2=h2,
                          gate=gate, up=up, act=act, y_conv=y_conv))
        if l == 0:
            xin, h = proj_norm(act, wd[0], x1, norm1_g[1:2], name="down_proj_0")
        else:
            dx, dx_b, loss_blk = proj_loss(act, wd[1], x1, target, name="down_proj_1")
    loss = lax.psum(loss_blk[0, 0], ("x", "y", "c"))

    big = {}
    small = {}
    reduced = {}
    groups = 0

    def scatter_siblings(keys):
        nonlocal groups
        gid = groups
        groups += 1
        return gid, keys, scatter_to_sibling([big[k] for k in keys], f"scatter_sibling_{gid}", 8 + 2 * gid)

    def scatter_chips(stage1, when):
        gid, keys, landed1 = stage1
        parts = [add_sibling(after(big[k], when), ld, core_arr, name=f"add_sibling_{k}") for k, ld in zip(keys, landed1)]
        landed2 = scatter_to_chips(parts, f"scatter_chips_{gid}", 9 + 2 * gid)
        for k, p, ld in zip(keys, parts, landed2):
            reduced[k] = (p, ld)
        return parts, landed2

    def rows_of(w, transposed):
        w = jnp.swapaxes(w, 1, 2) if transposed else w
        return w.reshape(w.shape[0] * w.shape[1], w.shape[2])

    sharded = {
        "win0": (2, True), "win1": (6, True), "wkv": (14, False), "wo": (15, False),
        "wg": (17, True), "wu": (18, True), "wd": (19, False)}
    weights = [norm1_g, mem_norm_g, a_w_in, a_q_g, a_k_g, a_rel_bias, b_w_in, b_b_in, b_conv_w, b_conv_b, b_ln_g,
               b_ln_b, mq_g, mk_g, w_mem_kv, w_out, norm2_g, w_gate, w_up, w_down]
    moms = [m_norm1_g, m_mem_norm_g, m_a_w_in, m_a_q_g, m_a_k_g, m_a_rel_bias, m_b_w_in, m_b_b_in, m_b_conv_w,
            m_b_conv_b, m_b_ln_g, m_b_ln_b, m_mq_g, m_mk_g, m_w_mem_kv, m_w_out, m_norm2_g, m_w_gate, m_w_up, m_w_down]
    vels = [v_norm1_g, v_mem_norm_g, v_a_w_in, v_a_q_g, v_a_k_g, v_a_rel_bias, v_b_w_in, v_b_b_in, v_b_conv_w,
            v_b_conv_b, v_b_ln_g, v_b_ln_b, v_mq_g, v_mk_g, v_w_mem_kv, v_w_out, v_norm2_g, v_w_gate, v_w_up, v_w_down]
    updated = {}

    def update_layer(l, when):
        for key, (idx, transposed) in sharded.items():
            if key in ("win0", "win1"):
                if key != f"win{l}":
                    continue
                layer, rkey = 0, key
            else:
                layer, rkey = l, f"{key}{l}"
            part, landed = reduced[rkey]
            updated[key] = adamw_shard(
                layer, after(rows_of(weights[idx], transposed), when), rows_of(moms[idx], transposed),
                rows_of(vels[idx], transposed), part, landed, chip_arr, updated.get(key), name=f"adamw_{rkey}")

    mix_landed = None
    for l in (1, 0):
        sv = saved[l]
        dgate, dup = ffn_bwd_act(dx_b, wd[l], sv["gate"], sv["up"], name=f"ffn_bwd_act_{l}")
        big[f"wd{l}"] = mm_tn(sv["act"], dx_b, name=f"grad_wd_{l}")
        if l == 0:
            dgate = after(dgate, *mix_landed)
        dx1, dx1_b, small[f"norm2_{l}"] = proj_rms_bwd(
            [(dgate, wg_t[l]), (dup, wu_t[l])], sv["x1"], norm2_g[l:l + 1], dx, name=f"ffn_bwd_h_{l}")
        if l == 0:
            update_layer(1, dx1)
        big[f"wg{l}"] = mm_tn(dgate, sv["h2"], name=f"grad_wg_{l}")
        big[f"wu{l}"] = mm_tn(dup, sv["h2"], name=f"grad_wu_{l}")
        big[f"wo{l}"] = mm_tn(sv["cat"], dx1_b, name=f"grad_wo_{l}")
        stage1 = scatter_siblings([f"wd{l}", f"wg{l}", f"wu{l}", f"wo{l}"])
        dcat = mm_nt(dx1_b, wo[l], name=f"out_proj_bwd_{l}")
        parts, ffn_landed = scatter_chips(stage1, dcat)
        dcat = after(dcat, *parts)
        if l == 0:
            dz, dbias, small["a_q"], small["a_k"] = attn_bwd(sv["z"], dcat, gq2, gk2, bias, batch, seq)
            small["rel"] = bias_grad(dbias)
            win_t = a_win_t
        else:
            dz, small["cw"], small["csum"] = conv_bwd(sv["z"], sv["y_conv"], dcat, cw_full, lg_full, lb_full, batch, seq)
            win_t = b_win_t
        dz = after(dz, *ffn_landed)
        dz, dkv, small[f"mq_{l}"], small[f"mk_{l}"] = memattn_bwd(
            sv["z"], sv["kv"], dcat, sv["gq4"], sv["gk4"], dz, batch, seq, sv["qcol"], name=f"memattn_bwd_{l}")
        if l == 1:
            small["bb"] = col_sum(dz, name="grad_b_in")
        big[f"win{l}"] = mm_tn(dz, sv["h"], name=f"grad_win_{l}")
        big[f"wkv{l}"] = mm_tn(sv["mem_n"], dkv, name=f"grad_wkv_{l}")
        stage1 = scatter_siblings([f"win{l}", f"wkv{l}"])
        dx, dx_b, small[f"norm1_{l}"] = proj_rms_bwd(
            [(dz, win_t)], sv["xin"], norm1_g[l:l + 1], dx1, name=f"in_proj_bwd_{l}")
        parts, mix_landed = scatter_chips(stage1, dx)
        dx_b = after(dx_b, *parts)
        dmem_n = mm_nt(dkv, wkv[l], out_dtype=F32, name=f"mem_kv_bwd_{l}")
        _, _, small[f"memnorm_{l}"] = rms_bwd(dmem_n, mem2, mem_norm_g[l:l + 1], zero_mem, name=f"rms_mem_bwd_{l}")
    grad_x = dx.reshape(batch, seq, d)
    update_layer(0, dx)

    def shaped(rows, idx, transposed):
        shp = weights[idx].shape
        if transposed:
            return jnp.swapaxes(rows.reshape(shp[0], shp[2], shp[1]), 1, 2)
        return rows.reshape(shp)

    def fold(v, groups):
        return jnp.sum(v.reshape(groups, HEAD_DIM), axis=0, keepdims=True)

    heads = a_rel_bias.shape[1]
    small_list = [
        jnp.concatenate([small["norm1_0"], small["norm1_1"]]),
        jnp.concatenate([small["memnorm_0"], small["memnorm_1"]]),
        fold(small["a_q"], 2), fold(small["a_k"], 2), small["rel"][:heads][None],
        small["bb"], small["cw"][:CONV_W][None], small["csum"][0:1], small["csum"][1:2], small["csum"][2:3],
        jnp.concatenate([fold(small["mq_0"], 4), fold(small["mq_1"], 4)]),
        jnp.concatenate([fold(small["mk_0"], 4), fold(small["mk_1"], 4)]),
        jnp.concatenate([small["norm2_0"], small["norm2_1"]]),
    ]
    small_full_shapes = [a.shape for a in small_list]
    summed = _unpack(small_exchange(_pack(small_list), reduce=True), small_full_shapes)
    (g_norm1, g_memnorm, g_aq, g_ak, g_rel, g_bb_full, g_cw_full, g_cb_full, g_lg_full, g_lb_full,
     g_mq, g_mk, g_norm2) = summed
    g_bb = lax.dynamic_slice_in_dim(g_bb_full, me * f_loc, f_loc, axis=1)
    g_cw = lax.dynamic_slice_in_dim(g_cw_full, me * c_loc, c_loc, axis=2)
    g_cb = lax.dynamic_slice_in_dim(g_cb_full, me * c_loc, c_loc, axis=1)
    g_lg = lax.dynamic_slice_in_dim(g_lg_full, me * c_loc, c_loc, axis=1)
    g_lb = lax.dynamic_slice_in_dim(g_lb_full, me * c_loc, c_loc, axis=1)

    grads = [g_norm1, g_memnorm, None, g_aq, g_ak, g_rel, None, g_bb, g_cw, g_cb, g_lg, g_lb,
             g_mq, g_mk, None, None, g_norm2, None, None, None]
    deltas, new_m, new_v = [None] * 20, [None] * 20, [None] * 20
    for key, (idx, transposed) in sharded.items():
        grads[idx], deltas[idx], new_m[idx], new_v[idx] = (shaped(r, idx, transposed) for r in updated[key])

    small_idx = [i for i in range(20) if grads[i] is not None and i not in {idx for idx, _ in sharded.values()}]
    small_shapes2 = [weights[i].shape for i in small_idx]
    dl, nm, nv = adamw(_pack([weights[i] for i in small_idx]), _pack([grads[i] for i in small_idx]),
                       _pack([moms[i] for i in small_idx]), _pack([vels[i] for i in small_idx]), name="adamw_small")
    for i, a, b, cc in zip(small_idx, _unpack(dl, small_shapes2), _unpack(nm, small_shapes2), _unpack(nv, small_shapes2)):
        deltas[i], new_m[i], new_v[i] = a, b, cc

    return (loss, grad_x, *grads, *deltas, *new_m, *new_v)
```

```python
import functools

import jax
import jax.numpy as jnp
from jax import lax
from jax.experimental import pallas as pl
from jax.experimental.pallas import tpu as pltpu
from jax.experimental.pallas import tpu_sc as plsc

F32 = jnp.float32
BF16 = jnp.bfloat16
HIGHEST = lax.Precision.HIGHEST
MESH = pl.DeviceIdType.MESH
ANY = pl.BlockSpec(memory_space=pl.ANY)

N_DEV = 8
D_MODEL = 1024
HEAD_DIM = 64
TOK_WIDTH = 768
MEM_WIDTH = 256
CHUNK = 64
Q_BLOCK = 256
KEY_WIN = 768
BAND = 576
N_REL = 192
CONV_W = 31
CONV_HALO = 32
NORM_EPS = 1e-6
NEG_INF = -1e30
ATTN_SCALE = HEAD_DIM ** -0.5
LANES = 128
ROW_TILE = 512
VMEM_LIMIT = 56 * 1024 * 1024

ADAM_LR, ADAM_B1, ADAM_B2, ADAM_EPS, ADAM_WD, ADAM_STEP = 0.001, 0.9, 0.999, 1e-08, 0.01, 10


def _params(*sem):
    return pltpu.CompilerParams(dimension_semantics=sem, vmem_limit_bytes=VMEM_LIMIT)


def _row_tile(m):
    return ROW_TILE if m % ROW_TILE == 0 else m


def _col_tile(n, cap=1408):
    best = None
    for t in range(LANES, min(n, cap) + 1, LANES):
        if n % t == 0:
            best = t
    return best if best is not None else n


def _dot(a, b, ca, cb):
    return lax.dot_general(a, b, (((ca,), (cb,)), ((), ())), preferred_element_type=F32)


def _sigmoid(x):
    return 0.5 * jnp.tanh(0.5 * x) + 0.5


def mm_nt(a, b, bias=None, out_dtype=BF16, name="mm_nt"):
    m, k = a.shape
    n = b.shape[0]
    tm, tn = _row_tile(m), _col_tile(n)

    def body(*refs):
        a_ref, b_ref = refs[0], refs[1]
        o_ref = refs[-1]
        acc = _dot(a_ref[...].astype(BF16), b_ref[...].astype(BF16), 1, 1)
        if bias is not None:
            acc = acc + refs[2][...]
        o_ref[...] = acc.astype(o_ref.dtype)

    in_specs = [pl.BlockSpec((tm, k), lambda j, i: (i, 0)), pl.BlockSpec((tn, k), lambda j, i: (j, 0))]
    args = [a, b]
    if bias is not None:
        in_specs.append(pl.BlockSpec((1, tn), lambda j, i: (0, j)))
        args.append(bias)
    return pl.pallas_call(
        body, out_shape=jax.ShapeDtypeStruct((m, n), out_dtype), grid=(n // tn, m // tm),
        in_specs=in_specs, out_specs=pl.BlockSpec((tm, tn), lambda j, i: (i, j)),
        compiler_params=_params("parallel", "arbitrary"), name=name)(*args)


def mm_nn(a, b, res=None, out_dtype=F32, name="mm_nn"):
    m, k = a.shape
    n = b.shape[1]
    tm, tn = _row_tile(m), _col_tile(n, 1024)

    def body(*refs):
        a_ref, b_ref = refs[0], refs[1]
        o_ref = refs[-1]
        acc = _dot(a_ref[...].astype(BF16), b_ref[...].astype(BF16), 1, 0)
        if res is not None:
            acc = acc + refs[2][...]
        o_ref[...] = acc.astype(o_ref.dtype)

    in_specs = [pl.BlockSpec((tm, k), lambda j, i: (i, 0)), pl.BlockSpec((k, tn), lambda j, i: (0, j))]
    args = [a, b]
    if res is not None:
        in_specs.append(pl.BlockSpec((tm, tn), lambda j, i: (i, j)))
        args.append(res)
    return pl.pallas_call(
        body, out_shape=jax.ShapeDtypeStruct((m, n), out_dtype), grid=(n // tn, m // tm),
        in_specs=in_specs, out_specs=pl.BlockSpec((tm, tn), lambda j, i: (i, j)),
        compiler_params=_params("parallel", "arbitrary"), name=name)(*args)


def mm2_nn(a1, b1, a2, b2, name="mm2_nn"):
    m, k = a1.shape
    n = b1.shape[1]
    tm = _row_tile(m)

    def body(a1_ref, b1_ref, a2_ref, b2_ref, o_ref):
        o_ref[...] = _dot(a1_ref[...], b1_ref[...], 1, 0) + _dot(a2_ref[...], b2_ref[...], 1, 0)

    a_spec = pl.BlockSpec((tm, k), lambda i: (i, 0))
    b_spec = pl.BlockSpec((k, n), lambda i: (0, 0))
    return pl.pallas_call(
        body, out_shape=jax.ShapeDtypeStruct((m, n), F32), grid=(m // tm,),
        in_specs=[a_spec, b_spec, a_spec, b_spec], out_specs=pl.BlockSpec((tm, n), lambda i: (i, 0)),
        compiler_params=_params("parallel"), name=name)(a1, b1, a2, b2)


def mm_tn(a, b, out_dtype=BF16, name="mm_tn"):
    t, r = a.shape
    c = b.shape[1]
    tr = _col_tile(r, 512)

    def body(a_ref, b_ref, o_ref):
        o_ref[...] = _dot(a_ref[...].astype(BF16), b_ref[...].astype(BF16), 0, 0).astype(o_ref.dtype)

    return pl.pallas_call(
        body, out_shape=jax.ShapeDtypeStruct((r, c), out_dtype), grid=(r // tr,),
        in_specs=[pl.BlockSpec((t, tr), lambda i: (0, i)), pl.BlockSpec((t, c), lambda i: (0, 0))],
        out_specs=pl.BlockSpec((tr, c), lambda i: (i, 0)),
        compiler_params=_params("parallel"), name=name)(a, b)


def _resident(shape):
    return pl.BlockSpec(shape, lambda i: (0, 0), pipeline_mode=pl.Buffered(1))


def proj_norm(a, b, res, gain, name):
    m, k = a.shape
    n = b.shape[1]
    tm = _row_tile(m)

    def body(a_ref, b_ref, res_ref, g_ref, x_ref, h_ref):
        xv = res_ref[...] + _dot(a_ref[...], b_ref[...], 1, 0)
        x_ref[...] = xv
        r = lax.rsqrt(jnp.mean(xv * xv, axis=-1, keepdims=True) + NORM_EPS)
        h_ref[...] = (xv * r * g_ref[...]).astype(BF16)

    row = pl.BlockSpec((tm, n), lambda i: (i, 0))
    return pl.pallas_call(
        body, out_shape=(jax.ShapeDtypeStruct((m, n), F32), jax.ShapeDtypeStruct((m, n), BF16)), grid=(m // tm,),
        in_specs=[pl.BlockSpec((tm, k), lambda i: (i, 0)), _resident((k, n)), row, _resident((1, n))],
        out_specs=(row, row), compiler_params=_params("parallel"), name=name)(a, b, res, gain)


def proj_loss(a, b, res, target, name):
    m, k = a.shape
    n = b.shape[1]
    tm = _row_tile(m)
    nt = m // tm

    def body(a_ref, b_ref, res_ref, t_ref, dy_ref, dyb_ref, l_ref, acc_ref):
        i = pl.program_id(0)

        @pl.when(i == 0)
        def _():
            acc_ref[...] = jnp.zeros_like(acc_ref)

        err = res_ref[...] + _dot(a_ref[...], b_ref[...], 1, 0) - t_ref[...]
        dy = err * (1.0 / n)
        dy_ref[...] = dy
        dyb_ref[...] = dy.astype(BF16)
        acc_ref[...] += jnp.sum(err * err, axis=0, keepdims=True)

        @pl.when(i == nt - 1)
        def _():
            total = jnp.sum(acc_ref[...], axis=-1, keepdims=True) * (0.5 / n)
            l_ref[...] = jnp.broadcast_to(total, l_ref.shape)

    row = pl.BlockSpec((tm, n), lambda i: (i, 0))
    return pl.pallas_call(
        body, out_shape=(jax.ShapeDtypeStruct((m, n), F32), jax.ShapeDtypeStruct((m, n), BF16),
                         jax.ShapeDtypeStruct((8, LANES), F32)), grid=(nt,),
        in_specs=[pl.BlockSpec((tm, k), lambda i: (i, 0)), _resident((k, n)), row, row],
        out_specs=(row, row, pl.BlockSpec((8, LANES), lambda i: (0, 0))),
        scratch_shapes=[pltpu.VMEM((1, n), F32)],
        compiler_params=_params("arbitrary"), name=name)(a, b, res, target)


def proj_rms_bwd(pairs, x, gain, dres, name):
    m, n = x.shape
    tm = _row_tile(m)
    np_ = len(pairs)

    def body(*refs):
        ab = refs[:2 * np_]
        x_ref, g_ref, dres_ref, dx_ref, dxb_ref, dg_ref = refs[2 * np_:]

        @pl.when(pl.program_id(0) == 0)
        def _():
            dg_ref[...] = jnp.zeros_like(dg_ref)

        dhv = _dot(ab[0][...], ab[1][...], 1, 0)
        for p in range(1, np_):
            dhv = dhv + _dot(ab[2 * p][...], ab[2 * p + 1][...], 1, 0)
        xv = x_ref[...]
        r = lax.rsqrt(jnp.mean(xv * xv, axis=-1, keepdims=True) + NORM_EPS)
        xhat = xv * r
        dg_ref[...] += jnp.sum(dhv * xhat, axis=0, keepdims=True)
        dxhat = dhv * g_ref[...]
        dx = dres_ref[...] + r * (dxhat - xhat * jnp.mean(dxhat * xhat, axis=-1, keepdims=True))
        dx_ref[...] = dx
        dxb_ref[...] = dx.astype(BF16)

    row = pl.BlockSpec((tm, n), lambda i: (i, 0))
    in_specs, args = [], []
    for a, b in pairs:
        in_specs += [pl.BlockSpec((tm, a.shape[1]), lambda i: (i, 0)), _resident(b.shape)]
        args += [a, b]
    return pl.pallas_call(
        body, out_shape=(jax.ShapeDtypeStruct((m, n), F32), jax.ShapeDtypeStruct((m, n), BF16),
                         jax.ShapeDtypeStruct((1, n), F32)), grid=(m // tm,),
        in_specs=in_specs + [row, _resident((1, n)), row], out_specs=(row, row, pl.BlockSpec((1, n), lambda i: (0, 0))),
        compiler_params=_params("arbitrary"), name=name)(*args, x, gain, dres)


FFN_ROWS = 256


def _ffn_row_tile(m):
    return FFN_ROWS if m % FFN_ROWS == 0 else m


def ffn_fwd(h2, wg_t, wu_t, wd, x1, gain=None, target=None, name="ffn_fwd"):
    n, d = h2.shape
    f = wg_t.shape[0]
    tm = _ffn_row_tile(n)
    nt = n // tm
    last = target is not None

    def body(h_ref, wg_ref, wu_ref, wd_ref, x1_ref, e_ref, g_ref, u_ref, a_ref, *rest):
        hv = h_ref[...]
        gv = _dot(hv, wg_ref[...], 1, 1)
        uv = _dot(hv, wu_ref[...], 1, 1)
        g_ref[...] = gv.astype(BF16)
        u_ref[...] = uv.astype(BF16)
        av = (gv * _sigmoid(gv) * uv).astype(BF16)
        a_ref[...] = av
        xv = x1_ref[...] + _dot(av, wd_ref[...], 1, 0)
        if not last:
            x_ref, hn_ref = rest
            x_ref[...] = xv
            r = lax.rsqrt(jnp.mean(xv * xv, axis=-1, keepdims=True) + NORM_EPS)
            hn_ref[...] = (xv * r * e_ref[...]).astype(BF16)
        else:
            dy_ref, dyb_ref, l_ref, acc_ref = rest
            i = pl.program_id(0)

            @pl.when(i == 0)
            def _():
                acc_ref[...] = jnp.zeros_like(acc_ref)

            err = xv - e_ref[...]
            dy = err * (1.0 / d)
            dy_ref[...] = dy
            dyb_ref[...] = dy.astype(BF16)
            acc_ref[...] += jnp.sum(err * err, axis=0, keepdims=True)

            @pl.when(i == nt - 1)
            def _():
                total = jnp.sum(acc_ref[...], axis=-1, keepdims=True) * (0.5 / d)
                l_ref[...] = jnp.broadcast_to(total, l_ref.shape)

    row_d = pl.BlockSpec((tm, d), lambda i: (i, 0))
    row_f = pl.BlockSpec((tm, f), lambda i: (i, 0))
    act_shape = jax.ShapeDtypeStruct((n, f), BF16)
    if not last:
        extra_in, extra = _resident((1, d)), gain
        out_shape = (act_shape, act_shape, act_shape, jax.ShapeDtypeStruct((n, d), F32), jax.ShapeDtypeStruct((n, d), BF16))
        out_specs = (row_f, row_f, row_f, row_d, row_d)
        scratch = []
    else:
        extra_in, extra = row_d, target
        out_shape = (act_shape, act_shape, act_shape, jax.ShapeDtypeStruct((n, d), F32), jax.ShapeDtypeStruct((n, d), BF16),
                     jax.ShapeDtypeStruct((8, LANES), F32))
        out_specs = (row_f, row_f, row_f, row_d, row_d, pl.BlockSpec((8, LANES), lambda i: (0, 0)))
        scratch = [pltpu.VMEM((1, d), F32)]
    return pl.pallas_call(
        body, out_shape=out_shape, grid=(nt,),
        in_specs=[row_d, _resident((f, d)), _resident((f, d)), _resident((f, d)), row_d, extra_in],
        out_specs=out_specs, scratch_shapes=scratch,
        compiler_params=_params("arbitrary"), name=name)(h2, wg_t, wu_t, wd, x1, extra)


def ffn_bwd(dx_b, wd, gate, up, wg_t, wu_t, x1, gain, dres, name="ffn_bwd"):
    n, d = x1.shape
    f = wd.shape[0]
    tm = _ffn_row_tile(n)

    def body(dxb_ref, wd_ref, g_ref, u_ref, wg_ref, wu_ref, x_ref, gain_ref, dres_ref,
             dg_ref, du_ref, dx_ref, dxo_ref, dgain_ref):
        @pl.when(pl.program_id(0) == 0)
        def _():
            dgain_ref[...] = jnp.zeros_like(dgain_ref)

        dact = _dot(dxb_ref[...], wd_ref[...], 1, 1)
        gv = g_ref[...].astype(F32)
        uv = u_ref[...].astype(F32)
        sg = _sigmoid(gv)
        dgv = (dact * uv * sg * (1.0 + gv * (1.0 - sg))).astype(BF16)
        duv = (dact * gv * sg).astype(BF16)
        dg_ref[...] = dgv
        du_ref[...] = duv
        dhv = _dot(dgv, wg_ref[...], 1, 0) + _dot(duv, wu_ref[...], 1, 0)
        xv = x_ref[...]
        r = lax.rsqrt(jnp.mean(xv * xv, axis=-1, keepdims=True) + NORM_EPS)
        xhat = xv * r
        dgain_ref[...] += jnp.sum(dhv * xhat, axis=0, keepdims=True)
        dxhat = dhv * gain_ref[...]
        dx = dres_ref[...] + r * (dxhat - xhat * jnp.mean(dxhat * xhat, axis=-1, keepdims=True))
        dx_ref[...] = dx
        dxo_ref[...] = dx.astype(BF16)

    row_d = pl.BlockSpec((tm, d), lambda i: (i, 0))
    row_f = pl.BlockSpec((tm, f), lambda i: (i, 0))
    w_spec = _resident((f, d))
    act_shape = jax.ShapeDtypeStruct((n, f), BF16)
    return pl.pallas_call(
        body, out_shape=(act_shape, act_shape, jax.ShapeDtypeStruct((n, d), F32), jax.ShapeDtypeStruct((n, d), BF16),
                         jax.ShapeDtypeStruct((1, d), F32)), grid=(n // tm,),
        in_specs=[row_d, w_spec, row_f, row_f, w_spec, w_spec, row_d, _resident((1, d)), row_d],
        out_specs=(row_f, row_f, row_d, row_d, pl.BlockSpec((1, d), lambda i: (0, 0))),
        compiler_params=_params("arbitrary"), name=name)(dx_b, wd, gate, up, wg_t, wu_t, x1, gain, dres)


def mm_tn2(a1, a2, b, name="mm_tn2"):
    t, r = a1.shape
    c = b.shape[1]
    tr = _col_tile(r, 512)

    def body(a1_ref, a2_ref, b_ref, o1_ref, o2_ref):
        bv = b_ref[...]
        o1_ref[...] = _dot(a1_ref[...], bv, 0, 0).astype(o1_ref.dtype)
        o2_ref[...] = _dot(a2_ref[...], bv, 0, 0).astype(o2_ref.dtype)

    a_spec = pl.BlockSpec((t, tr), lambda i: (0, i))
    o_spec = pl.BlockSpec((tr, c), lambda i: (i, 0))
    shape = jax.ShapeDtypeStruct((r, c), BF16)
    return pl.pallas_call(
        body, out_shape=(shape, shape), grid=(r // tr,), in_specs=[a_spec, a_spec, _resident((t, c))],
        out_specs=(o_spec, o_spec), compiler_params=_params("parallel"), name=name)(a1, a2, b)


def rms_fwd(x, g, name="rms_fwd"):
    n, d = x.shape
    tm = _row_tile(n)

    def body(x_ref, g_ref, o_ref):
        xv = x_ref[...]
        r = lax.rsqrt(jnp.mean(xv * xv, axis=-1, keepdims=True) + NORM_EPS)
        o_ref[...] = (xv * r * g_ref[...]).astype(o_ref.dtype)

    return pl.pallas_call(
        body, out_shape=jax.ShapeDtypeStruct((n, d), BF16), grid=(n // tm,),
        in_specs=[pl.BlockSpec((tm, d), lambda i: (i, 0)), pl.BlockSpec((1, d), lambda i: (0, 0))],
        out_specs=pl.BlockSpec((tm, d), lambda i: (i, 0)),
        compiler_params=_params("parallel"), name=name)(x, g)


def rms_bwd(dh, x, g, dres, name="rms_bwd"):
    n, d = x.shape
    tm = _row_tile(n)

    def body(dh_ref, x_ref, g_ref, dres_ref, dx_ref, dxb_ref, dg_ref):
        @pl.when(pl.program_id(0) == 0)
        def _():
            dg_ref[...] = jnp.zeros_like(dg_ref)

        xv = x_ref[...]
        dhv = dh_ref[...].astype(F32)
        r = lax.rsqrt(jnp.mean(xv * xv, axis=-1, keepdims=True) + NORM_EPS)
        xhat = xv * r
        dg_ref[...] += jnp.sum(dhv * xhat, axis=0, keepdims=True)
        dxhat = dhv * g_ref[...]
        mean_t = jnp.mean(dxhat * xhat, axis=-1, keepdims=True)
        dx = dres_ref[...] + r * (dxhat - xhat * mean_t)
        dx_ref[...] = dx
        dxb_ref[...] = dx.astype(BF16)

    row = pl.BlockSpec((tm, d), lambda i: (i, 0))
    vec = pl.BlockSpec((1, d), lambda i: (0, 0))
    return pl.pallas_call(
        body, out_shape=(jax.ShapeDtypeStruct((n, d), F32), jax.ShapeDtypeStruct((n, d), BF16),
                         jax.ShapeDtypeStruct((1, d), F32)), grid=(n // tm,),
        in_specs=[row, row, vec, row], out_specs=(row, row, vec),
        compiler_params=_params("arbitrary"), name=name)(dh, x, g, dres)


def gate_up(h2, wg_t, wu_t, name="gate_up"):
    n, d = h2.shape
    f = wg_t.shape[0]
    tm, tn = _row_tile(n), _col_tile(f)

    def body(h_ref, wg_ref, wu_ref, g_ref, u_ref, a_ref):
        hv = h_ref[...]
        gv = _dot(hv, wg_ref[...], 1, 1)
        uv = _dot(hv, wu_ref[...], 1, 1)
        g_ref[...] = gv.astype(BF16)
        u_ref[...] = uv.astype(BF16)
        a_ref[...] = (gv * _sigmoid(gv) * uv).astype(BF16)

    w_spec = pl.BlockSpec((tn, d), lambda j, i: (j, 0))
    o_spec = pl.BlockSpec((tm, tn), lambda j, i: (i, j))
    o_shape = jax.ShapeDtypeStruct((n, f), BF16)
    return pl.pallas_call(
        body, out_shape=(o_shape, o_shape, o_shape), grid=(f // tn, n // tm),
        in_specs=[pl.BlockSpec((tm, d), lambda j, i: (i, 0)), w_spec, w_spec], out_specs=(o_spec, o_spec, o_spec),
        compiler_params=_params("parallel", "arbitrary"), name=name)(h2, wg_t, wu_t)


def ffn_bwd_act(dx, wd, gate, up, name="ffn_bwd_act"):
    n, d = dx.shape
    f = wd.shape[0]
    tm, tn = _row_tile(n), _col_tile(f)

    def body(dx_ref, wd_ref, g_ref, u_ref, dg_ref, du_ref):
        dact = _dot(dx_ref[...].astype(BF16), wd_ref[...], 1, 1)
        gv = g_ref[...].astype(F32)
        uv = u_ref[...].astype(F32)
        sg = _sigmoid(gv)
        dg_ref[...] = (dact * uv * sg * (1.0 + gv * (1.0 - sg))).astype(BF16)
        du_ref[...] = (dact * gv * sg).astype(BF16)

    t_spec = pl.BlockSpec((tm, tn), lambda j, i: (i, j))
    o_shape = jax.ShapeDtypeStruct((n, f), BF16)
    return pl.pallas_call(
        body, out_shape=(o_shape, o_shape), grid=(f // tn, n // tm),
        in_specs=[pl.BlockSpec((tm, d), lambda j, i: (i, 0)), pl.BlockSpec((tn, d), lambda j, i: (j, 0)), t_spec, t_spec],
        out_specs=(t_spec, t_spec),
        compiler_params=_params("parallel", "arbitrary"), name=name)(dx, wd, gate, up)


def _group_masks(width):
    lane = lax.broadcasted_iota(jnp.int32, (1, width), 1)
    return [(lane >= HEAD_DIM * g) & (lane < HEAD_DIM * (g + 1)) for g in range(width // HEAD_DIM)]


def _group_sum(x, masks):
    out = jnp.zeros_like(x)
    for msk in masks:
        s = jnp.sum(jnp.where(msk, x, 0.0), axis=-1, keepdims=True)
        out = jnp.where(msk, s, out)
    return out


def _head_norm(x, gain, masks):
    r = lax.rsqrt(_group_sum(x * x, masks) * (1.0 / HEAD_DIM) + NORM_EPS)
    xhat = x * r
    return xhat * gain, xhat, r


def _head_norm_bwd(dxn, xhat, r, gain, masks):
    dgain = jnp.sum(dxn * xhat, axis=0, keepdims=True)
    dxhat = dxn * gain
    mean_t = _group_sum(dxhat * xhat, masks) * (1.0 / HEAD_DIM)
    return r * (dxhat - xhat * mean_t), dgain


def _softmax_rows(s):
    e = jnp.exp(s - jnp.max(s, axis=-1, keepdims=True))
    return e * (1.0 / jnp.sum(e, axis=-1, keepdims=True))


def _rel_onehot():
    col = lax.broadcasted_iota(jnp.int32, (1, KEY_WIN), 1)
    off = jnp.where(col < KEY_WIN - LANES, col, col - KEY_WIN)
    idx = jnp.clip(8 * CHUNK - off, -(CHUNK - 1), LANES) + (CHUNK - 1)
    return (lax.broadcasted_iota(jnp.int32, (N_REL, KEY_WIN), 0) == idx).astype(F32)


def bias_blocks(rel16):
    heads = TOK_WIDTH // HEAD_DIM

    def body(rel_ref, o_ref, u_ref):
        u_ref[...] = jnp.dot(rel_ref[...], _rel_onehot(), precision=HIGHEST, preferred_element_type=F32)
        row = lax.broadcasted_iota(jnp.int32, (CHUNK, KEY_WIN), 0)
        col = lax.broadcasted_iota(jnp.int32, (CHUNK, KEY_WIN), 1)
        for h in range(heads):
            xv = jnp.broadcast_to(u_ref[h:h + 1, :], (CHUNK, KEY_WIN))
            for b in range(6):
                xv = jnp.where(((row >> b) & 1) == 1, pltpu.roll(xv, 1 << b, axis=1), xv)
            xv = jnp.where(col < BAND, xv, NEG_INF)
            for i in range(Q_BLOCK // CHUNK):
                o_ref[h, CHUNK * i:CHUNK * (i + 1), :] = pltpu.roll(xv, CHUNK * i, axis=1) if i else xv

    return pl.pallas_call(
        body, out_shape=jax.ShapeDtypeStruct((heads, Q_BLOCK, KEY_WIN), F32),
        scratch_shapes=[pltpu.VMEM((16, KEY_WIN), F32)], name="bias_blocks")(rel16)


def bias_grad(dbias):
    heads = dbias.shape[0]

    def body(db_ref, o_ref, y_ref):
        y_ref[...] = jnp.zeros_like(y_ref)
        row = lax.broadcasted_iota(jnp.int32, (CHUNK, KEY_WIN), 0)
        for h in range(heads):
            fv = db_ref[h, 0:CHUNK, :]
            for i in range(1, Q_BLOCK // CHUNK):
                fv = fv + pltpu.roll(db_ref[h, CHUNK * i:CHUNK * (i + 1), :], KEY_WIN - CHUNK * i, axis=1)
            for b in range(6):
                fv = jnp.where(((row >> b) & 1) == 1, pltpu.roll(fv, KEY_WIN - (1 << b), axis=1), fv)
            y_ref[h:h + 1, :] = jnp.sum(fv, axis=0, keepdims=True)
        o_ref[...] = lax.dot_general(y_ref[...], _rel_onehot(), (((1,), (1,)), ((), ())),
                                     precision=HIGHEST, preferred_element_type=F32)

    return pl.pallas_call(
        body, out_shape=jax.ShapeDtypeStruct((16, N_REL), F32),
        scratch_shapes=[pltpu.VMEM((16, KEY_WIN), F32)], name="bias_grad")(dbias)


def _attn_windows(seq):
    out = []
    for j in range(seq // Q_BLOCK):
        r0 = j * Q_BLOCK
        k0 = max(0, r0 - 8 * CHUNK)
        width = r0 + Q_BLOCK - k0
        out.append((r0, k0, width, KEY_WIN - width))
    return out


def attn_fwd(z, gq2, gk2, bias, batch, seq):
    n = z.shape[0]
    pairs = TOK_WIDTH // LANES

    def body(q_ref, k_ref, v_ref, gq_ref, gk_ref, b_ref, o_ref, qs_s, kn_s):
        masks = _group_masks(LANES)
        qs_s[...] = (_head_norm(q_ref[...].astype(F32), gq_ref[...], masks)[0] * ATTN_SCALE).astype(BF16)
        kn_s[...] = _head_norm(k_ref[...].astype(F32), gk_ref[...], masks)[0].astype(BF16)
        for r0, k0, width, c0 in _attn_windows(seq):
            qb = qs_s[r0:r0 + Q_BLOCK, :]
            kw = kn_s[k0:k0 + width, :]
            vw = v_ref[k0:k0 + width, :]
            out = jnp.zeros((Q_BLOCK, LANES), F32)
            for h, msk in enumerate(masks):
                qh = jnp.where(msk, qb, jnp.zeros_like(qb))
                s = _dot(qh, kw, 1, 1) + b_ref[h, :, c0:KEY_WIN]
                p = _softmax_rows(s).astype(BF16)
                out = jnp.where(msk, _dot(p, vw, 1, 0), out)
            o_ref[r0:r0 + Q_BLOCK, :] = out.astype(o_ref.dtype)

    def col(off):
        return pl.BlockSpec((seq, LANES), lambda b, p: (b, off + p))

    vec = pl.BlockSpec((1, LANES), lambda b, p: (0, 0))
    return pl.pallas_call(
        body, out_shape=jax.ShapeDtypeStruct((n, D_MODEL), BF16), grid=(batch, pairs),
        in_specs=[col(0), col(pairs), col(2 * pairs), vec, vec,
                  pl.BlockSpec((2, Q_BLOCK, KEY_WIN), lambda b, p: (p, 0, 0))],
        out_specs=pl.BlockSpec((seq, LANES), lambda b, p: (b, p)),
        scratch_shapes=[pltpu.VMEM((seq, LANES), BF16), pltpu.VMEM((seq, LANES), BF16)],
        compiler_params=_params("parallel", "arbitrary"), name="attn_fwd")(z, z, z, gq2, gk2, bias)


def attn_bwd(z, dcat, gq2, gk2, bias, batch, seq):
    n = z.shape[0]
    pairs = TOK_WIDTH // LANES

    def body(q_ref, k_ref, v_ref, do_ref, gq_ref, gk_ref, b_ref,
             dz_ref, db_ref, dgq_ref, dgk_ref, qs_s, kn_s, dqn_s, dkn_s, dv_s, dk_o, dv_o):
        pi, bi, which = pl.program_id(0), pl.program_id(1), pl.program_id(2)

        @pl.when(which == 0)
        def _():
            masks = _group_masks(LANES)

            @pl.when(bi == 0)
            def _():
                db_ref[...] = jnp.zeros_like(db_ref)

            @pl.when((bi == 0) & (pi == 0))
            def _():
                dgq_ref[...] = jnp.zeros_like(dgq_ref)
                dgk_ref[...] = jnp.zeros_like(dgk_ref)

            qn, qhat, rq = _head_norm(q_ref[...].astype(F32), gq_ref[...], masks)
            kn, khat, rk = _head_norm(k_ref[...].astype(F32), gk_ref[...], masks)
            qs_s[...] = (qn * ATTN_SCALE).astype(BF16)
            kn_s[...] = kn.astype(BF16)
            dkn_s[...] = jnp.zeros_like(dkn_s)
            dv_s[...] = jnp.zeros_like(dv_s)
            for r0, k0, width, c0 in _attn_windows(seq):
                qb = qs_s[r0:r0 + Q_BLOCK, :]
                dob = do_ref[r0:r0 + Q_BLOCK, :]
                kw = kn_s[k0:k0 + width, :]
                vw = v_ref[k0:k0 + width, :]
                dq_acc = jnp.zeros((Q_BLOCK, LANES), F32)
                dk_acc = jnp.zeros((width, LANES), F32)
                dv_acc = jnp.zeros((width, LANES), F32)
                for h, msk in enumerate(masks):
                    qh = jnp.where(msk, qb, jnp.zeros_like(qb))
                    doh = jnp.where(msk, dob, jnp.zeros_like(dob))
                    p = _softmax_rows(_dot(qh, kw, 1, 1) + b_ref[h, :, c0:KEY_WIN])
                    dp = _dot(doh, vw, 1, 1)
                    ds = p * (dp - jnp.sum(p * dp, axis=-1, keepdims=True))
                    db_ref[h, :, c0:KEY_WIN] += ds
                    dsb = ds.astype(BF16)
                    dq_acc = jnp.where(msk, _dot(dsb, kw, 1, 0), dq_acc)
                    dk_acc = jnp.where(msk, _dot(dsb, qb, 0, 0), dk_acc)
                    dv_acc = jnp.where(msk, _dot(p.astype(BF16), dob, 0, 0), dv_acc)
                dqn_s[r0:r0 + Q_BLOCK, :] = dq_acc * ATTN_SCALE
                dkn_s[k0:k0 + width, :] += dk_acc
                dv_s[k0:k0 + width, :] += dv_acc
            dq, dgq = _head_norm_bwd(dqn_s[...], qhat, rq, gq_ref[...], masks)
            dk, dgk = _head_norm_bwd(dkn_s[...], khat, rk, gk_ref[...], masks)
            dz_ref[...] = dq.astype(dz_ref.dtype)
            dk_o[...] = dk.astype(dk_o.dtype)
            dv_o[...] = dv_s[...].astype(dv_o.dtype)
            dgq_ref[...] += dgq
            dgk_ref[...] += dgk

        @pl.when(which == 1)
        def _():
            dz_ref[...] = dk_o[...]

        @pl.when(which == 2)
        def _():
            dz_ref[...] = dv_o[...]

    def ahead(p, b, t):
        nb = b + jnp.where(t > 0, 1, 0)
        wrap = jnp.where(nb >= batch, 1, 0)
        return jnp.minimum(p + wrap, pairs - 1), nb - wrap * batch

    def col(off):
        def index(p, b, t):
            np_, nb = ahead(p, b, t)
            return nb, off + np_
        return pl.BlockSpec((seq, LANES), index)

    vec = pl.BlockSpec((1, LANES), lambda p, b, t: (0, 0))
    blk = pl.BlockSpec((2, Q_BLOCK, KEY_WIN), lambda p, b, t: (p, 0, 0))
    blk_in = pl.BlockSpec((2, Q_BLOCK, KEY_WIN), lambda p, b, t: (ahead(p, b, t)[0], 0, 0))
    v_shape = jax.ShapeDtypeStruct((1, LANES), F32)
    return pl.pallas_call(
        body,
        out_shape=(jax.ShapeDtypeStruct(z.shape, BF16), jax.ShapeDtypeStruct(bias.shape, F32), v_shape, v_shape),
        grid=(pairs, batch, 3),
        in_specs=[col(0), col(pairs), col(2 * pairs), col(0), vec, vec, blk_in],
        out_specs=(pl.BlockSpec((seq, LANES), lambda p, b, t: (b, t * pairs + p)), blk, vec, vec),
        scratch_shapes=[pltpu.VMEM((seq, LANES), BF16), pltpu.VMEM((seq, LANES), BF16),
                        pltpu.VMEM((seq, LANES), F32), pltpu.VMEM((seq, LANES), F32), pltpu.VMEM((seq, LANES), F32),
                        pltpu.VMEM((seq, LANES), BF16), pltpu.VMEM((seq, LANES), BF16)],
        compiler_params=_params("arbitrary", "arbitrary", "arbitrary"), name="attn_bwd")(
            z, z, z, dcat, gq2, gk2, bias)


MEM_ROWS = 512


def memattn_fwd(z, kv, gq4, gk4, cat, batch, seq, qcol, name):
    mtok = kv.shape[0] // batch
    rows = min(MEM_ROWS, seq)

    def body(q_ref, kv_ref, gq_ref, gk_ref, cat_ref, o_ref):
        del cat_ref
        masks = _group_masks(MEM_WIDTH)
        kn = _head_norm(kv_ref[:, 0:MEM_WIDTH], gk_ref[...], masks)[0].astype(BF16)
        vm = kv_ref[:, MEM_WIDTH:2 * MEM_WIDTH].astype(BF16)
        for t in range(seq // rows):
            sl = slice(t * rows, (t + 1) * rows)
            qs = (_head_norm(q_ref[sl, :].astype(F32), gq_ref[...], masks)[0] * ATTN_SCALE).astype(BF16)
            out = jnp.zeros((rows, MEM_WIDTH), F32)
            for msk in masks:
                qh = jnp.where(msk, qs, jnp.zeros_like(qs))
                p = _softmax_rows(_dot(qh, kn, 1, 1)).astype(BF16)
                out = jnp.where(msk, _dot(p, vm, 1, 0), out)
            o_ref[sl, :] = out.astype(o_ref.dtype)

    vec = pl.BlockSpec((1, MEM_WIDTH), lambda b: (0, 0))
    return pl.pallas_call(
        body, out_shape=jax.ShapeDtypeStruct(cat.shape, cat.dtype), grid=(batch,),
        in_specs=[pl.BlockSpec((seq, MEM_WIDTH), lambda b: (b, qcol)),
                  pl.BlockSpec((mtok, 2 * MEM_WIDTH), lambda b: (b, 0)), vec, vec, ANY],
        out_specs=pl.BlockSpec((seq, MEM_WIDTH), lambda b: (b, TOK_WIDTH // MEM_WIDTH)),
        input_output_aliases={4: 0},
        compiler_params=_params("parallel"), name=name)(z, kv, gq4, gk4, cat)


def memattn_bwd(z, kv, dcat, gq4, gk4, dz, batch, seq, qcol, name):
    mtok = kv.shape[0] // batch
    rows = min(MEM_ROWS, seq)

    def body(q_ref, kv_ref, do_ref, gq_ref, gk_ref, dz_in_ref, dq_ref, dkv_ref, dgq_ref, dgk_ref):
        del dz_in_ref
        @pl.when(pl.program_id(0) == 0)
        def _():
            dgq_ref[...] = jnp.zeros_like(dgq_ref)
            dgk_ref[...] = jnp.zeros_like(dgk_ref)

        masks = _group_masks(MEM_WIDTH)
        kn_f, khat, rk = _head_norm(kv_ref[:, 0:MEM_WIDTH], gk_ref[...], masks)
        kn = kn_f.astype(BF16)
        vm = kv_ref[:, MEM_WIDTH:2 * MEM_WIDTH].astype(BF16)
        dkn = jnp.zeros((mtok, MEM_WIDTH), F32)
        dvm = jnp.zeros((mtok, MEM_WIDTH), F32)
        dgq = jnp.zeros((1, MEM_WIDTH), F32)
        for t in range(seq // rows):
            sl = slice(t * rows, (t + 1) * rows)
            qn_f, qhat, rq = _head_norm(q_ref[sl, :].astype(F32), gq_ref[...], masks)
            qs = (qn_f * ATTN_SCALE).astype(BF16)
            dob = do_ref[sl, :]
            dqn = jnp.zeros((rows, MEM_WIDTH), F32)
            for msk in masks:
                qh = jnp.where(msk, qs, jnp.zeros_like(qs))
                doh = jnp.where(msk, dob, jnp.zeros_like(dob))
                p = _softmax_rows(_dot(qh, kn, 1, 1))
                dp = _dot(doh, vm, 1, 1)
                ds = p * (dp - jnp.sum(p * dp, axis=-1, keepdims=True))
                dsb = ds.astype(BF16)
                dqn = jnp.where(msk, _dot(dsb, kn, 1, 0), dqn)
                dkn = dkn + jnp.where(msk, _dot(dsb, qs, 0, 0), 0.0)
                dvm = dvm + jnp.where(msk, _dot(p.astype(BF16), dob, 0, 0), 0.0)
            dq, dg = _head_norm_bwd(dqn * ATTN_SCALE, qhat, rq, gq_ref[...], masks)
            dq_ref[sl, :] = dq.astype(dq_ref.dtype)
            dgq = dgq + dg
        dk, dgk = _head_norm_bwd(dkn, khat, rk, gk_ref[...], masks)
        dkv_ref[:, 0:MEM_WIDTH] = dk
        dkv_ref[:, MEM_WIDTH:2 * MEM_WIDTH] = dvm
        dgq_ref[...] += dgq
        dgk_ref[...] += dgk

    vec = pl.BlockSpec((1, MEM_WIDTH), lambda b: (0, 0))
    kv_spec = pl.BlockSpec((mtok, 2 * MEM_WIDTH), lambda b: (b, 0))
    v_shape = jax.ShapeDtypeStruct((1, MEM_WIDTH), F32)
    q_spec = pl.BlockSpec((seq, MEM_WIDTH), lambda b: (b, qcol))
    return pl.pallas_call(
        body,
        out_shape=(jax.ShapeDtypeStruct(dz.shape, dz.dtype), jax.ShapeDtypeStruct(kv.shape, F32), v_shape, v_shape),
        grid=(batch,),
        in_specs=[q_spec, kv_spec, pl.BlockSpec((seq, MEM_WIDTH), lambda b: (b, TOK_WIDTH // MEM_WIDTH)), vec, vec, ANY],
        out_specs=(q_spec, kv_spec, vec, vec),
        input_output_aliases={5: 0},
        compiler_params=_params("arbitrary"), name=name)(z, kv, dcat, gq4, gk4, dz)


CONV_ROWS = 256


def _glu(a_ref, g_ref):
    return a_ref[...].astype(F32) * _sigmoid(g_ref[...].astype(F32))


def _layer_norm_stats(y):
    mu = jnp.mean(y, axis=-1, keepdims=True)
    yc = y - mu
    rstd = lax.rsqrt(jnp.mean(yc * yc, axis=-1, keepdims=True) + NORM_EPS)
    return yc * rstd, rstd


CONV_WIN = CONV_HALO + CONV_ROWS
SUBLANES = 8
SHIFT_ROWS = CONV_WIN - SUBLANES


def _preshift(win, shifted):
    for s in range(1, SUBLANES):
        shifted[s - 1, :, :] = win[s:s + SHIFT_ROWS, :]


TAP_ROWS = 64
TAP_TILES = [(r0, slice(c0, c0 + LANES)) for c0 in range(0, TOK_WIDTH, LANES) for r0 in range(0, CONV_ROWS, TAP_ROWS)]


def _tap(win, shifted, off, r0, lanes):
    s = off % SUBLANES
    base = off - s + r0
    if s == 0:
        return win[base:base + TAP_ROWS, lanes]
    return shifted[s - 1, base:base + TAP_ROWS, lanes]


def _fold_rows(x):
    return jnp.sum(x.reshape(TAP_ROWS // SUBLANES, SUBLANES, LANES), axis=0)


def conv_fwd(z, cw, cb, lg, lb, batch, seq):
    n = z.shape[0]
    nt = seq // CONV_ROWS
    sub = CONV_ROWS // CONV_HALO
    lead = CONV_HALO - (CONV_W - 1)

    def body(a_ref, g_ref, ap_ref, gp_ref, cw_ref, cb_ref, lg_ref, lb_ref, o_ref, y_ref, win, shifted):
        first = pl.program_id(1) == 0
        win[0:CONV_HALO, :] = jnp.where(first, 0.0, _glu(ap_ref, gp_ref))
        win[CONV_HALO:CONV_WIN, :] = _glu(a_ref, g_ref)
        _preshift(win, shifted)
        for r0, lanes in TAP_TILES:
            acc = jnp.zeros((TAP_ROWS, LANES), F32) + cb_ref[:, lanes]
            for w in range(CONV_W):
                acc = acc + _tap(win, shifted, lead + w, r0, lanes) * cw_ref[w:w + 1, lanes]
            y_ref[r0:r0 + TAP_ROWS, lanes] = acc
        yh, _ = _layer_norm_stats(y_ref[...])
        t = yh * lg_ref[...] + lb_ref[...]
        o_ref[...] = (t * _sigmoid(t)).astype(o_ref.dtype)

    def cur(c):
        return pl.BlockSpec((CONV_ROWS, TOK_WIDTH), lambda b, i: (b * nt + i, c))

    def prev(c):
        return pl.BlockSpec((CONV_HALO, TOK_WIDTH), lambda b, i: (jnp.maximum((b * nt + i) * sub - 1, 0), c))

    vec = pl.BlockSpec((1, TOK_WIDTH), lambda b, i: (0, 0))
    return pl.pallas_call(
        body, out_shape=(jax.ShapeDtypeStruct((n, D_MODEL), BF16), jax.ShapeDtypeStruct((n, TOK_WIDTH), F32)),
        grid=(batch, nt),
        in_specs=[cur(0), cur(1), prev(0), prev(1), pl.BlockSpec((32, TOK_WIDTH), lambda b, i: (0, 0)), vec, vec, vec],
        out_specs=(cur(0), cur(0)),
        scratch_shapes=[pltpu.VMEM((CONV_WIN, TOK_WIDTH), F32), pltpu.VMEM((SUBLANES - 1, SHIFT_ROWS, TOK_WIDTH), F32)],
        compiler_params=_params("parallel", "arbitrary"), name="conv_fwd")(z, z, z, z, cw, cb, lg, lb)


def conv_bwd(z, y, dcat, cw, lg, lb, batch, seq):
    n = z.shape[0]
    nt = seq // CONV_ROWS
    sub = CONV_ROWS // CONV_HALO
    lead = CONV_HALO - (CONV_W - 1)
    last_blk = n // CONV_HALO - 1

    def body(a_ref, g_ref, ap_ref, gp_ref, y_ref, yn_ref, do_ref, don_ref, cw_ref, lg_ref, lb_ref,
             dz_ref, dcw_ref, dsm_ref, win, shifted, dyw, dshifted, dg_o):
        b, i, which = pl.program_id(0), pl.program_id(1), pl.program_id(2)

        @pl.when(which == 0)
        def _():
            first, last = i == 0, i == nt - 1

            @pl.when((b == 0) & (i == 0))
            def _():
                dcw_ref[...] = jnp.zeros_like(dcw_ref)
                dsm_ref[...] = jnp.zeros_like(dsm_ref)

            win[0:CONV_HALO, :] = jnp.where(first, 0.0, _glu(ap_ref, gp_ref))
            win[CONV_HALO:CONV_WIN, :] = _glu(a_ref, g_ref)
            _preshift(win, shifted)
            yv = jnp.concatenate([y_ref[...], yn_ref[...]], axis=0)
            yh, rstd = _layer_norm_stats(yv)
            t = yh * lg_ref[...] + lb_ref[...]
            st = _sigmoid(t)
            dout = jnp.concatenate(
                [do_ref[...].astype(F32), jnp.where(last, 0.0, don_ref[...].astype(F32))], axis=0)
            dt = dout * st * (1.0 + t * (1.0 - st))
            dyh = dt * lg_ref[...]
            dy = rstd * (dyh - jnp.mean(dyh, axis=-1, keepdims=True)
                         - yh * jnp.mean(dyh * yh, axis=-1, keepdims=True))
            dyw[...] = dy
            _preshift(dyw, dshifted)
            dsm_ref[0:1, :] += jnp.sum(dy[0:CONV_ROWS], axis=0, keepdims=True)
            dsm_ref[1:2, :] += jnp.sum((dt * yh)[0:CONV_ROWS], axis=0, keepdims=True)
            dsm_ref[2:3, :] += jnp.sum(dt[0:CONV_ROWS], axis=0, keepdims=True)
            for c0 in range(0, TOK_WIDTH, LANES):
                lanes = slice(c0, c0 + LANES)
                dcw_acc = [jnp.zeros((SUBLANES, LANES), F32) for _ in range(CONV_W)]
                for r0 in range(0, CONV_ROWS, TAP_ROWS):
                    dyt = dyw[r0:r0 + TAP_ROWS, lanes]
                    dglu = jnp.zeros((TAP_ROWS, LANES), F32)
                    for w in range(CONV_W):
                        dcw_acc[w] = dcw_acc[w] + _fold_rows(dyt * _tap(win, shifted, lead + w, r0, lanes))
                        dglu = dglu + _tap(dyw, dshifted, CONV_W - 1 - w, r0, lanes) * cw_ref[w:w + 1, lanes]
                    avt = a_ref[r0:r0 + TAP_ROWS, lanes].astype(F32)
                    sgt = _sigmoid(g_ref[r0:r0 + TAP_ROWS, lanes].astype(F32))
                    dz_ref[r0:r0 + TAP_ROWS, lanes] = (dglu * sgt).astype(dz_ref.dtype)
                    dg_o[r0:r0 + TAP_ROWS, lanes] = (dglu * avt * sgt * (1.0 - sgt)).astype(dg_o.dtype)
                for w in range(CONV_W):
                    dcw_ref[w:w + 1, lanes] += jnp.sum(dcw_acc[w], axis=0, keepdims=True)

        @pl.when(which == 1)
        def _():
            dz_ref[...] = dg_o[...]

    def ahead(b, i, t):
        return jnp.minimum(b * nt + i + t, batch * nt - 1)

    def cur(c):
        return pl.BlockSpec((CONV_ROWS, TOK_WIDTH), lambda b, i, t: (ahead(b, i, t), c))

    def prev(c):
        return pl.BlockSpec((CONV_HALO, TOK_WIDTH), lambda b, i, t: (jnp.maximum(ahead(b, i, t) * sub - 1, 0), c))

    nxt = pl.BlockSpec((CONV_HALO, TOK_WIDTH),
                       lambda b, i, t: (jnp.minimum((ahead(b, i, t) + 1) * sub, last_blk), 0))
    vec = pl.BlockSpec((1, TOK_WIDTH), lambda b, i, t: (0, 0))
    full32 = pl.BlockSpec((32, TOK_WIDTH), lambda b, i, t: (0, 0))
    return pl.pallas_call(
        body,
        out_shape=(jax.ShapeDtypeStruct(z.shape, BF16), jax.ShapeDtypeStruct((32, TOK_WIDTH), F32),
                   jax.ShapeDtypeStruct((8, TOK_WIDTH), F32)),
        grid=(batch, nt, 2),
        in_specs=[cur(0), cur(1), prev(0), prev(1), cur(0), nxt, cur(0), nxt, full32, vec, vec],
        out_specs=(pl.BlockSpec((CONV_ROWS, TOK_WIDTH), lambda b, i, t: (b * nt + i, t)), full32,
                   pl.BlockSpec((8, TOK_WIDTH), lambda b, i, t: (0, 0))),
        scratch_shapes=[pltpu.VMEM((CONV_WIN, TOK_WIDTH), F32), pltpu.VMEM((SUBLANES - 1, SHIFT_ROWS, TOK_WIDTH), F32),
                        pltpu.VMEM((CONV_WIN, TOK_WIDTH), F32), pltpu.VMEM((SUBLANES - 1, SHIFT_ROWS, TOK_WIDTH), F32),
                        pltpu.VMEM((CONV_ROWS, TOK_WIDTH), BF16)],
        compiler_params=_params("arbitrary", "arbitrary", "arbitrary"), name="conv_bwd")(
            z, z, z, z, y, y, dcat, dcat, cw, lg, lb)


def loss_head(y, target):
    n, d = y.shape
    tm = _row_tile(n)
    nt = n // tm

    def body(y_ref, t_ref, dy_ref, dyb_ref, l_ref, acc_ref):
        i = pl.program_id(0)

        @pl.when(i == 0)
        def _():
            acc_ref[...] = jnp.zeros_like(acc_ref)

        err = y_ref[...] - t_ref[...]
        dy = err * (1.0 / d)
        dy_ref[...] = dy
        dyb_ref[...] = dy.astype(BF16)
        acc_ref[...] += jnp.sum(err * err, axis=0, keepdims=True)

        @pl.when(i == nt - 1)
        def _():
            total = jnp.sum(acc_ref[...], axis=-1, keepdims=True) * (0.5 / d)
            l_ref[...] = jnp.broadcast_to(total, l_ref.shape)

    row = pl.BlockSpec((tm, d), lambda i: (i, 0))
    return pl.pallas_call(
        body, out_shape=(jax.ShapeDtypeStruct((n, d), F32), jax.ShapeDtypeStruct((n, d), BF16),
                         jax.ShapeDtypeStruct((8, LANES), F32)), grid=(nt,),
        in_specs=[row, row], out_specs=(row, row, pl.BlockSpec((8, LANES), lambda i: (0, 0))),
        scratch_shapes=[pltpu.VMEM((1, d), F32)],
        compiler_params=_params("arbitrary"), name="loss_head")(y, target)


def col_sum(x, name="col_sum"):
    n, c = x.shape
    tm = _row_tile(n)

    def body(x_ref, o_ref):
        @pl.when(pl.program_id(0) == 0)
        def _():
            o_ref[...] = jnp.zeros_like(o_ref)

        o_ref[...] += jnp.sum(x_ref[...].astype(F32), axis=0, keepdims=True)

    return pl.pallas_call(
        body, out_shape=jax.ShapeDtypeStruct((1, c), F32), grid=(n // tm,),
        in_specs=[pl.BlockSpec((tm, c), lambda i: (i, 0))], out_specs=pl.BlockSpec((1, c), lambda i: (0, 0)),
        compiler_params=_params("arbitrary"), name=name)(x)


def adamw(w, g, m, v, name="adamw"):
    rows, cols = w.shape
    tr = rows
    for cand in (512, 256, 128, 64, 32, 16, 8):
        if rows % cand == 0 and rows > cand:
            tr = cand
            break
    c1 = 1.0 / (1.0 - ADAM_B1 ** ADAM_STEP)
    c2 = 1.0 / (1.0 - ADAM_B2 ** ADAM_STEP)

    def body(w_ref, g_ref, m_ref, v_ref, d_ref, nm_ref, nv_ref):
        gv = g_ref[...]
        nm = ADAM_B1 * m_ref[...] + (1.0 - ADAM_B1) * gv
        nv = ADAM_B2 * v_ref[...] + (1.0 - ADAM_B2) * (gv * gv)
        nm_ref[...] = nm
        nv_ref[...] = nv
        d_ref[...] = -ADAM_LR * ((nm * c1) / (jnp.sqrt(nv * c2) + ADAM_EPS) + ADAM_WD * w_ref[...])

    spec = pl.BlockSpec((tr, cols), lambda i: (i, 0))
    shape = jax.ShapeDtypeStruct((rows, cols), F32)
    return pl.pallas_call(
        body, out_shape=(shape, shape, shape), grid=(rows // tr,),
        in_specs=[spec, spec, spec, spec], out_specs=(spec, spec, spec),
        compiler_params=_params("parallel"), name=name)(w, g, m, v)


def _place():
    return lax.axis_index("x"), lax.axis_index("y"), lax.axis_index("c")


def _other_chips(x, y):
    return [(1 - x, y), (x, 1 - y), (1 - x, 1 - y)]


def small_exchange(slab, reduce):
    r = slab.shape[0]

    def body(in_ref, o_ref, *scratch):
        if reduce:
            buf, send_sems, recv_sems = scratch
        else:
            buf = o_ref
            send_sems, recv_sems = scratch
        x, y, c = _place()
        me = 4 * x + 2 * y + c
        buf[me] = in_ref[...]
        copies = []
        for k in range(1, N_DEV):
            peer = (x ^ (k >> 2), y ^ ((k >> 1) & 1), c ^ (k & 1))
            cp = pltpu.make_async_remote_copy(
                src_ref=in_ref, dst_ref=buf.at[me], send_sem=send_sems.at[k - 1], recv_sem=recv_sems.at[k - 1],
                device_id=peer, device_id_type=MESH)
            cp.start()
            copies.append(cp)
        for k in range(1, N_DEV):
            src = 4 * (x ^ (k >> 2)) + 2 * (y ^ ((k >> 1) & 1)) + (c ^ (k & 1))
            pltpu.make_async_remote_copy(
                src_ref=in_ref, dst_ref=buf.at[src], send_sem=send_sems.at[k - 1], recv_sem=recv_sems.at[k - 1],
                device_id=(x, y, c), device_id_type=MESH).wait_recv()
        for cp in copies:
            cp.wait_send()
        if reduce:
            total = buf[0]
            for d in range(1, N_DEV):
                total = total + buf[d]
            o_ref[...] = total

    sems = [pltpu.SemaphoreType.DMA((N_DEV - 1,)), pltpu.SemaphoreType.DMA((N_DEV - 1,))]
    if reduce:
        out_shape = jax.ShapeDtypeStruct((r, LANES), F32)
        scratch = [pltpu.VMEM((N_DEV, r, LANES), F32)] + sems
    else:
        out_shape = jax.ShapeDtypeStruct((N_DEV, r, LANES), F32)
        scratch = sems
    vmem = pl.BlockSpec(memory_space=pltpu.VMEM)
    return pl.pallas_call(
        body, out_shape=out_shape, in_specs=[vmem], out_specs=vmem, scratch_shapes=scratch,
        compiler_params=pltpu.CompilerParams(vmem_limit_bytes=VMEM_LIMIT),
        name="small_reduce" if reduce else "small_gather")(slab)


def gather_weights(shards, name, collective_id):
    nw = len(shards)
    ns = [s.shape[0] for s in shards]
    in_refs = [jax.new_ref(s, memory_space=pltpu.MemorySpace.HBM) for s in shards]
    out_refs = [jax.empty_ref(jax.ShapeDtypeStruct((N_DEV * s.shape[0], s.shape[1]), s.dtype),
                              memory_space=pltpu.MemorySpace.HBM) for s in shards]

    @pl.kernel(mesh=plsc.ScalarSubcoreMesh(axis_name="seq", num_cores=1), name=name,
               scratch_types=(pltpu.SemaphoreType.DMA((nw, 7)), pltpu.SemaphoreType.DMA((nw, 7)),
                              pltpu.SemaphoreType.DMA((nw,))),
               compiler_params=pltpu.CompilerParams(collective_id=collective_id))
    def launch(send_sems, recv_sems, local_sems):
        x, y, c = _place()
        me, sib = (x, y, c), (x, y, 1 - c)
        chips = _other_chips(x, y)
        barrier = pltpu.get_barrier_semaphore()
        for peer in [sib] + [(*chip, c) for chip in chips]:
            pl.semaphore_signal(barrier, inc=1, device_id=peer, device_id_type=MESH)
        pl.semaphore_wait(barrier, 4)

        def rows(w, dev):
            return out_refs[w].at[pl.ds((4 * dev[0] + 2 * dev[1] + dev[2]) * ns[w], ns[w]), :]

        def copy(w, k, block, to, src=None):
            return pltpu.make_async_remote_copy(
                src_ref=rows(w, block) if src is None else src, dst_ref=rows(w, block),
                send_sem=send_sems.at[w, k], recv_sem=recv_sems.at[w, k], device_id=to, device_id_type=MESH)

        started, sends = [], []
        for w in range(nw):
            mine = pltpu.make_async_copy(in_refs[w], rows(w, me), local_sems.at[w])
            mine.start()
            started.append(mine)
            first = [copy(w, 0, me, sib, src=in_refs[w])]
            first += [copy(w, 1 + j, me, (*chip, c), src=in_refs[w]) for j, chip in enumerate(chips)]
            for cp in first:
                cp.start()
            sends += first
        for w in range(nw):
            for j, chip in enumerate(chips):
                copy(w, 1 + j, (*chip, c), me).wait_recv()
                fwd = copy(w, 4 + j, (*chip, c), sib)
                fwd.start()
                sends.append(fwd)
        for w in range(nw):
            copy(w, 0, sib, me).wait_recv()
            for j, chip in enumerate(chips):
                copy(w, 4 + j, (*chip, 1 - c), me).wait_recv()
        for cp in sends:
            cp.wait_send()
        for mine in started:
            mine.wait()

    launch()
    return [r[...] for r in out_refs]


def _sequencer_exchange(sources, out_rows, peers_of, copies_of, name, collective_id):
    nw = len(sources)
    in_refs = [jax.new_ref(s, memory_space=pltpu.MemorySpace.HBM) for s in sources]
    out_refs = [jax.empty_ref(jax.ShapeDtypeStruct((rows, s.shape[1]), s.dtype), memory_space=pltpu.MemorySpace.HBM)
                for rows, s in zip(out_rows, sources)]
    per = len(copies_of(0, 0, 0, 0))

    @pl.kernel(mesh=plsc.ScalarSubcoreMesh(axis_name="seq", num_cores=1), name=name,
               scratch_types=(pltpu.SemaphoreType.DMA((nw, per)), pltpu.SemaphoreType.DMA((nw, per))),
               compiler_params=pltpu.CompilerParams(collective_id=collective_id))
    def launch(send_sems, recv_sems):
        x, y, c = _place()
        peers = peers_of(x, y, c)
        barrier = pltpu.get_barrier_semaphore()
        for peer in peers:
            pl.semaphore_signal(barrier, inc=1, device_id=peer, device_id_type=MESH)
        pl.semaphore_wait(barrier, len(peers))
        copies = []
        for w in range(nw):
            for k, (src_blk, dst_blk, rows, peer) in enumerate(copies_of(x, y, c, w)):
                cp = pltpu.make_async_remote_copy(
                    src_ref=in_refs[w].at[pl.ds(src_blk * rows, rows), :],
                    dst_ref=out_refs[w].at[pl.ds(dst_blk * rows, rows), :],
                    send_sem=send_sems.at[w, k], recv_sem=recv_sems.at[w, k], device_id=peer, device_id_type=MESH)
                cp.start()
                copies.append(cp)
        for cp in copies:
            cp.wait_recv()
        for cp in copies:
            cp.wait_send()

    launch()
    return [r[...] for r in out_refs]


def scatter_to_sibling(grads, name, collective_id):
    ns = [g.shape[0] // N_DEV for g in grads]
    return _sequencer_exchange(
        grads, [4 * n for n in ns],
        lambda x, y, c: [(x, y, 1 - c)],
        lambda x, y, c, w: [(2 * q + 1 - c, q, ns[w], (x, y, 1 - c)) for q in range(4)],
        name, collective_id)


def scatter_to_chips(parts, name, collective_id):
    ns = [p.shape[0] // 4 for p in parts]
    return _sequencer_exchange(
        parts, [3 * n for n in ns],
        lambda x, y, c: [(*chip, c) for chip in _other_chips(x, y)],
        lambda x, y, c, w: [(2 * chip[0] + chip[1], j, ns[w], (*chip, c)) for j, chip in enumerate(_other_chips(x, y))],
        name, collective_id)


def add_sibling(grad, landed, core, name):
    n = landed.shape[0] // 4
    cols = grad.shape[1]

    def body(c_ref, g_ref, l_ref, o_ref):
        o_ref[...] = (g_ref[...].astype(F32) + l_ref[...].astype(F32)).astype(o_ref.dtype)

    grid_spec = pltpu.PrefetchScalarGridSpec(
        num_scalar_prefetch=1, grid=(4,),
        in_specs=[pl.BlockSpec((n, cols), lambda q, c_ref: (2 * q + c_ref[0], 0)),
                  pl.BlockSpec((n, cols), lambda q, c_ref: (q, 0))],
        out_specs=pl.BlockSpec((n, cols), lambda q, c_ref: (q, 0)))
    return pl.pallas_call(
        body, out_shape=jax.ShapeDtypeStruct(landed.shape, landed.dtype), grid_spec=grid_spec,
        compiler_params=_params("arbitrary"), name=name)(core, grad, landed)


def adamw_shard(layer, w, m, v, part, landed, chip, earlier, name):
    n = landed.shape[0] // 3
    cols = w.shape[1]
    c1 = 1.0 / (1.0 - ADAM_B1 ** ADAM_STEP)
    c2 = 1.0 / (1.0 - ADAM_B2 ** ADAM_STEP)

    def body(q_ref, w_ref, m_ref, v_ref, p_ref, l0_ref, l1_ref, l2_ref, *rest):
        g_ref, d_ref, nm_ref, nv_ref = rest[-4:]
        gv = ((p_ref[...].astype(F32) + l0_ref[...].astype(F32)) + l1_ref[...].astype(F32)) + l2_ref[...].astype(F32)
        nm = ADAM_B1 * m_ref[...] + (1.0 - ADAM_B1) * gv
        nv = ADAM_B2 * v_ref[...] + (1.0 - ADAM_B2) * (gv * gv)
        g_ref[...] = gv
        nm_ref[...] = nm
        nv_ref[...] = nv
        d_ref[...] = -ADAM_LR * ((nm * c1) / (jnp.sqrt(nv * c2) + ADAM_EPS) + ADAM_WD * w_ref[...])

    own = pl.BlockSpec((n, cols), lambda i, q_ref: (layer, 0))

    def landed_spec(j):
        return pl.BlockSpec((n, cols), lambda i, q_ref: (j, 0))

    in_specs = [own, own, own, pl.BlockSpec((n, cols), lambda i, q_ref: (q_ref[0], 0)),
                landed_spec(0), landed_spec(1), landed_spec(2)]
    args = [chip, w, m, v, part, landed, landed, landed]
    aliases = {}
    if earlier is not None:
        in_specs += [ANY] * 4
        args += list(earlier)
        aliases = {8 + k: k for k in range(4)}
    grid_spec = pltpu.PrefetchScalarGridSpec(
        num_scalar_prefetch=1, grid=(1,), in_specs=in_specs, out_specs=(own, own, own, own))
    shape = jax.ShapeDtypeStruct(w.shape, F32)
    return pl.pallas_call(
        body, out_shape=(shape, shape, shape, shape), grid_spec=grid_spec, input_output_aliases=aliases,
        compiler_params=_params("arbitrary"), name=name)(*args)


def _pack(arrays):
    flat = jnp.concatenate([a.reshape(-1).astype(F32) for a in arrays])
    pad = (-flat.shape[0]) % (8 * LANES)
    return jnp.pad(flat, (0, pad)).reshape(-1, LANES)


def _unpack(slab, shapes):
    flat = slab.reshape(slab.shape[:-2] + (-1,))
    out, off = [], 0
    for shp in shapes:
        size = 1
        for s in shp:
            size *= s
        out.append(flat[..., off:off + size].reshape(flat.shape[:-1] + tuple(shp)))
        off += size
    return out


def kernel(x, mem, norm1_g, mem_norm_g, a_w_in, a_q_g, a_k_g, a_rel_bias, b_w_in, b_b_in, b_conv_w, b_conv_b, b_ln_g, b_ln_b, mq_g, mk_g, w_mem_kv, w_out, norm2_g, w_gate, w_up, w_down, loss_target, m_norm1_g, m_mem_norm_g, m_a_w_in, m_a_q_g, m_a_k_g, m_a_rel_bias, m_b_w_in, m_b_b_in, m_b_conv_w, m_b_conv_b, m_b_ln_g, m_b_ln_b, m_mq_g, m_mk_g, m_w_mem_kv, m_w_out, m_norm2_g, m_w_gate, m_w_up, m_w_down, v_norm1_g, v_mem_norm_g, v_a_w_in, v_a_q_g, v_a_k_g, v_a_rel_bias, v_b_w_in, v_b_b_in, v_b_conv_w, v_b_conv_b, v_b_ln_g, v_b_ln_b, v_mq_g, v_mk_g, v_w_mem_kv, v_w_out, v_norm2_g, v_w_gate, v_w_up, v_w_down):
    batch, seq, d = x.shape
    mtok = mem.shape[1]
    n = batch * seq
    ax, ay, ac = _place()
    me = 4 * ax + 2 * ay + ac
    core_arr = jnp.reshape(ac, (1,)).astype(jnp.int32)
    chip_arr = jnp.reshape(2 * ax + ay, (1,)).astype(jnp.int32)

    def t_bf16(w):
        return jnp.transpose(w).astype(BF16)

    def after(value, *earlier):
        return lax.optimization_barrier((value, *earlier))[0]

    def gather_mix(l, when, name, collective_id):
        srcs = [w_mem_kv[l].astype(BF16), w_out[l].astype(BF16)] + ([t_bf16(b_w_in[0])] if l == 1 else [])
        return gather_weights([after(srcs[0], when)] + srcs[1:], name, collective_id)

    def gather_ffn(l, when, name, collective_id):
        return gather_weights(
            [after(t_bf16(w_gate[l]), when), t_bf16(w_up[l]), w_down[l].astype(BF16)], name, collective_id)

    f_loc = b_b_in.shape[1]
    c_loc = b_conv_b.shape[1]

    def two(g):
        return jnp.concatenate([g, g], axis=-1)

    gq2, gk2 = two(a_q_g), two(a_k_g)
    rel16 = jnp.pad(a_rel_bias[0], ((0, 16 - a_rel_bias.shape[1]), (0, 0)))
    bias = bias_blocks(rel16)

    x0 = x.reshape(n, d)
    mem2 = mem.reshape(batch * mtok, d)
    zero_mem = jnp.zeros_like(mem2)

    saved = []
    xin = x0
    a_win_t, = gather_weights([t_bf16(a_w_in[0])], "gather_in_a", 1)
    wg_t, wu_t, wd, wo, wkv = [None] * 2, [None] * 2, [None] * 2, [None] * 2, [None] * 2
    h = rms_fwd(xin, norm1_g[0:1], name="rms1_fwd_0")
    target = loss_target.reshape(n, d)
    for l in range(2):
        mem_n = rms_fwd(mem2, mem_norm_g[l:l + 1], name=f"rms_mem_fwd_{l}")
        gq4 = jnp.tile(mq_g[l:l + 1], (1, 4))
        gk4 = jnp.tile(mk_g[l:l + 1], (1, 4))
        y_conv = None
        if l == 0:
            wkv[0], wo[0] = gather_mix(0, h, "gather_mix_a", 2)
            z = mm_nt(h, a_win_t, name="in_proj_a")
            wg_t[0], wu_t[0], wd[0] = gather_ffn(0, z, "gather_ffn_a", 3)
            cat = attn_fwd(z, gq2, gk2, bias, batch, seq)
            wkv[1], wo[1], b_win_t = gather_mix(1, cat, "gather_mix_b", 4)
            qcol = 3 * TOK_WIDTH // MEM_WIDTH
        else:
            small_shapes = [(f_loc,), (CONV_W, c_loc), (c_loc,), (c_loc,), (c_loc,)]
            gathered = small_exchange(after(_pack([b_b_in, b_conv_w, b_conv_b, b_ln_g, b_ln_b]), xin), reduce=False)
            bb_g, cw_g, cb_g, lg_g, lb_g = _unpack(gathered, small_shapes)
            bb_full = bb_g.reshape(1, -1)
            cw_full = jnp.pad(jnp.transpose(cw_g, (1, 0, 2)).reshape(CONV_W, -1), ((0, 32 - CONV_W), (0, 0)))
            cb_full, lg_full, lb_full = cb_g.reshape(1, -1), lg_g.reshape(1, -1), lb_g.reshape(1, -1)
            z = mm_nt(h, b_win_t, bias=bb_full, name="in_proj_b")
            cat, y_conv = conv_fwd(z, cw_full, cb_full, lg_full, lb_full, batch, seq)
            qcol = 2 * TOK_WIDTH // MEM_WIDTH
        kv = mm_nn(mem_n, wkv[l], name=f"mem_kv_{l}")
        cat = memattn_fwd(z, kv, gq4, gk4, cat, batch, seq, qcol, name=f"memattn_fwd_{l}")
        x1, h2 = proj_norm(cat, wo[l], xin, norm2_g[l:l + 1], name=f"out_proj_{l}")
        if l == 0:
            wg_t[1], wu_t[1], wd[1] = gather_ffn(1, x1, "gather_ffn_b", 5)
        if l == 0:
            gate, up, act, x2, h_next = ffn_fwd(h2, wg_t[0], wu_t[0], wd[0], x1, gain=norm1_g[1:2], name="ffn_fwd_0")
        else:
            gate, up, act, dx, dx_b, loss_blk = ffn_fwd(h2, wg_t[1], wu_t[1], wd[1], x1, target=target, name="ffn_fwd_1")
        saved.append(dict(xin=xin, h=h, mem_n=mem_n, kv=kv, gq4=gq4, gk4=gk4, z=z, qcol=qcol, cat=cat, x1=x1, h2=h2,
                          gate=gate, up=up, act=act, y_conv=y_conv))
        if l == 0:
            xin, h = x2, h_next
    loss = lax.psum(loss_blk[0, 0], ("x", "y", "c"))

    big = {}
    small = {}
    reduced = {}
    groups = 0

    def scatter_siblings(keys):
        nonlocal groups
        gid = groups
        groups += 1
        return gid, keys, scatter_to_sibling([big[k] for k in keys], f"scatter_sibling_{gid}", 8 + 2 * gid)

    def scatter_chips(stage1, when):
        gid, keys, landed1 = stage1
        parts = [add_sibling(after(big[k], when), ld, core_arr, name=f"add_sibling_{k}") for k, ld in zip(keys, landed1)]
        landed2 = scatter_to_chips(parts, f"scatter_chips_{gid}", 9 + 2 * gid)
        for k, p, ld in zip(keys, parts, landed2):
            reduced[k] = (p, ld)
        return parts, landed2

    def rows_of(w, transposed):
        w = jnp.swapaxes(w, 1, 2) if transposed else w
        return w.reshape(w.shape[0] * w.shape[1], w.shape[2])

    sharded = {
        "win0": (2, True), "win1": (6, True), "wkv": (14, False), "wo": (15, False),
        "wg": (17, True), "wu": (18, True), "wd": (19, False)}
    weights = [norm1_g, mem_norm_g, a_w_in, a_q_g, a_k_g, a_rel_bias, b_w_in, b_b_in, b_conv_w, b_conv_b, b_ln_g,
               b_ln_b, mq_g, mk_g, w_mem_kv, w_out, norm2_g, w_gate, w_up, w_down]
    moms = [m_norm1_g, m_mem_norm_g, m_a_w_in, m_a_q_g, m_a_k_g, m_a_rel_bias, m_b_w_in, m_b_b_in, m_b_conv_w,
            m_b_conv_b, m_b_ln_g, m_b_ln_b, m_mq_g, m_mk_g, m_w_mem_kv, m_w_out, m_norm2_g, m_w_gate, m_w_up, m_w_down]
    vels = [v_norm1_g, v_mem_norm_g, v_a_w_in, v_a_q_g, v_a_k_g, v_a_rel_bias, v_b_w_in, v_b_b_in, v_b_conv_w,
            v_b_conv_b, v_b_ln_g, v_b_ln_b, v_mq_g, v_mk_g, v_w_mem_kv, v_w_out, v_norm2_g, v_w_gate, v_w_up, v_w_down]
    updated = {}

    def update_layer(l, when):
        for key, (idx, transposed) in sharded.items():
            if key in ("win0", "win1"):
                if key != f"win{l}":
                    continue
                layer, rkey = 0, key
            else:
                layer, rkey = l, f"{key}{l}"
            part, landed = reduced[rkey]
            updated[key] = adamw_shard(
                layer, after(rows_of(weights[idx], transposed), when), rows_of(moms[idx], transposed),
                rows_of(vels[idx], transposed), part, landed, chip_arr, updated.get(key), name=f"adamw_{rkey}")

    mix_landed = None
    for l in (1, 0):
        sv = saved[l]
        big[f"wd{l}"] = mm_tn(sv["act"], dx_b, name=f"grad_wd_{l}")
        dgate, dup, dx1, dx1_b, small[f"norm2_{l}"] = ffn_bwd(
            dx_b, wd[l], sv["gate"], sv["up"], wg_t[l], wu_t[l], sv["x1"], norm2_g[l:l + 1], dx, name=f"ffn_bwd_{l}")
        if l == 0:
            dgate = after(dgate, *mix_landed)
            update_layer(1, dx1)
        big[f"wg{l}"], big[f"wu{l}"] = mm_tn2(dgate, dup, sv["h2"], name=f"grad_wgu_{l}")
        big[f"wo{l}"] = mm_tn(sv["cat"], dx1_b, name=f"grad_wo_{l}")
        stage1 = scatter_siblings([f"wd{l}", f"wg{l}", f"wu{l}", f"wo{l}"])
        dcat = mm_nt(dx1_b, wo[l], name=f"out_proj_bwd_{l}")
        parts, ffn_landed = scatter_chips(stage1, dcat)
        dcat = after(dcat, *parts)
        if l == 0:
            dz, dbias, small["a_q"], small["a_k"] = attn_bwd(sv["z"], dcat, gq2, gk2, bias, batch, seq)
            small["rel"] = bias_grad(dbias)
            win_t = a_win_t
        else:
            dz, small["cw"], small["csum"] = conv_bwd(sv["z"], sv["y_conv"], dcat, cw_full, lg_full, lb_full, batch, seq)
            win_t = b_win_t
        dz = after(dz, *ffn_landed)
        dz, dkv, small[f"mq_{l}"], small[f"mk_{l}"] = memattn_bwd(
            sv["z"], sv["kv"], dcat, sv["gq4"], sv["gk4"], dz, batch, seq, sv["qcol"], name=f"memattn_bwd_{l}")
        if l == 1:
            small["bb"] = col_sum(dz, name="grad_b_in")
        big[f"win{l}"] = mm_tn(dz, sv["h"], name=f"grad_win_{l}")
        big[f"wkv{l}"] = mm_tn(sv["mem_n"], dkv, name=f"grad_wkv_{l}")
        stage1 = scatter_siblings([f"win{l}", f"wkv{l}"])
        dx, dx_b, small[f"norm1_{l}"] = proj_rms_bwd(
            [(dz, win_t)], sv["xin"], norm1_g[l:l + 1], dx1, name=f"in_proj_bwd_{l}")
        parts, mix_landed = scatter_chips(stage1, dx)
        dx_b = after(dx_b, *parts)
        dmem_n = mm_nt(dkv, wkv[l], out_dtype=F32, name=f"mem_kv_bwd_{l}")
        _, _, small[f"memnorm_{l}"] = rms_bwd(dmem_n, mem2, mem_norm_g[l:l + 1], zero_mem, name=f"rms_mem_bwd_{l}")
    grad_x = dx.reshape(batch, seq, d)
    update_layer(0, dx)

    def shaped(rows, idx, transposed):
        shp = weights[idx].shape
        if transposed:
            return jnp.swapaxes(rows.reshape(shp[0], shp[2], shp[1]), 1, 2)
        return rows.reshape(shp)

    def fold(v, groups):
        return jnp.sum(v.reshape(groups, HEAD_DIM), axis=0, keepdims=True)

    heads = a_rel_bias.shape[1]
    small_list = [
        jnp.concatenate([small["norm1_0"], small["norm1_1"]]),
        jnp.concatenate([small["memnorm_0"], small["memnorm_1"]]),
        fold(small["a_q"], 2), fold(small["a_k"], 2), small["rel"][:heads][None],
        small["bb"], small["cw"][:CONV_W][None], small["csum"][0:1], small["csum"][1:2], small["csum"][2:3],
        jnp.concatenate([fold(small["mq_0"], 4), fold(small["mq_1"], 4)]),
        jnp.concatenate([fold(small["mk_0"], 4), fold(small["mk_1"], 4)]),
        jnp.concatenate([small["norm2_0"], small["norm2_1"]]),
    ]
    small_full_shapes = [a.shape for a in small_list]
    summed = _unpack(small_exchange(_pack(small_list), reduce=True), small_full_shapes)
    (g_norm1, g_memnorm, g_aq, g_ak, g_rel, g_bb_full, g_cw_full, g_cb_full, g_lg_full, g_lb_full,
     g_mq, g_mk, g_norm2) = summed
    g_bb = lax.dynamic_slice_in_dim(g_bb_full, me * f_loc, f_loc, axis=1)
    g_cw = lax.dynamic_slice_in_dim(g_cw_full, me * c_loc, c_loc, axis=2)
    g_cb = lax.dynamic_slice_in_dim(g_cb_full, me * c_loc, c_loc, axis=1)
    g_lg = lax.dynamic_slice_in_dim(g_lg_full, me * c_loc, c_loc, axis=1)
    g_lb = lax.dynamic_slice_in_dim(g_lb_full, me * c_loc, c_loc, axis=1)

    grads = [g_norm1, g_memnorm, None, g_aq, g_ak, g_rel, None, g_bb, g_cw, g_cb, g_lg, g_lb,
             g_mq, g_mk, None, None, g_norm2, None, None, None]
    deltas, new_m, new_v = [None] * 20, [None] * 20, [None] * 20
    for key, (idx, transposed) in sharded.items():
        grads[idx], deltas[idx], new_m[idx], new_v[idx] = (shaped(r, idx, transposed) for r in updated[key])

    small_idx = [i for i in range(20) if grads[i] is not None and i not in {idx for idx, _ in sharded.values()}]
    small_shapes2 = [weights[i].shape for i in small_idx]
    dl, nm, nv = adamw(_pack([weights[i] for i in small_idx]), _pack([grads[i] for i in small_idx]),
                       _pack([moms[i] for i in small_idx]), _pack([vels[i] for i in small_idx]), name="adamw_small")
    for i, a, b, cc in zip(small_idx, _unpack(dl, small_shapes2), _unpack(nm, small_shapes2), _unpack(nv, small_shapes2)):
        deltas[i], new_m[i], new_v[i] = a, b, cc

    return (loss, grad_x, *grads, *deltas, *new_m, *new_v)
```

```python
import functools

import jax
import jax.numpy as jnp
from jax import lax
from jax.experimental import pallas as pl
from jax.experimental.pallas import tpu as pltpu
from jax.experimental.pallas import tpu_sc as plsc

F32 = jnp.float32
BF16 = jnp.bfloat16
HIGHEST = lax.Precision.HIGHEST
MESH = pl.DeviceIdType.MESH
ANY = pl.BlockSpec(memory_space=pl.ANY)

N_DEV = 8
D_MODEL = 1024
HEAD_DIM = 64
TOK_WIDTH = 768
MEM_WIDTH = 256
CHUNK = 64
Q_BLOCK = 256
KEY_WIN = 768
BAND = 576
N_REL = 192
CONV_W = 31
CONV_HALO = 32
NORM_EPS = 1e-6
NEG_INF = -1e30
ATTN_SCALE = HEAD_DIM ** -0.5
LANES = 128
ROW_TILE = 512
VMEM_LIMIT = 56 * 1024 * 1024

ADAM_LR, ADAM_B1, ADAM_B2, ADAM_EPS, ADAM_WD, ADAM_STEP = 0.001, 0.9, 0.999, 1e-08, 0.01, 10


def _params(*sem):
    return pltpu.CompilerParams(dimension_semantics=sem, vmem_limit_bytes=VMEM_LIMIT)


def _row_tile(m):
    return ROW_TILE if m % ROW_TILE == 0 else m


def _col_tile(n, cap=1408):
    best = None
    for t in range(LANES, min(n, cap) + 1, LANES):
        if n % t == 0:
            best = t
    return best if best is not None else n


def _dot(a, b, ca, cb):
    return lax.dot_general(a, b, (((ca,), (cb,)), ((), ())), preferred_element_type=F32)


def _sigmoid(x):
    return 0.5 * jnp.tanh(0.5 * x) + 0.5


def mm_nt(a, b, bias=None, out_dtype=BF16, name="mm_nt"):
    m, k = a.shape
    n = b.shape[0]
    tm, tn = _row_tile(m), _col_tile(n)

    def body(*refs):
        a_ref, b_ref = refs[0], refs[1]
        o_ref = refs[-1]
        acc = _dot(a_ref[...].astype(BF16), b_ref[...].astype(BF16), 1, 1)
        if bias is not None:
            acc = acc + refs[2][...]
        o_ref[...] = acc.astype(o_ref.dtype)

    in_specs = [pl.BlockSpec((tm, k), lambda j, i: (i, 0)), pl.BlockSpec((tn, k), lambda j, i: (j, 0))]
    args = [a, b]
    if bias is not None:
        in_specs.append(pl.BlockSpec((1, tn), lambda j, i: (0, j)))
        args.append(bias)
    return pl.pallas_call(
        body, out_shape=jax.ShapeDtypeStruct((m, n), out_dtype), grid=(n // tn, m // tm),
        in_specs=in_specs, out_specs=pl.BlockSpec((tm, tn), lambda j, i: (i, j)),
        compiler_params=_params("parallel", "arbitrary"), name=name)(*args)


def mm_nn(a, b, res=None, out_dtype=F32, name="mm_nn"):
    m, k = a.shape
    n = b.shape[1]
    tm, tn = _row_tile(m), _col_tile(n, 1024)

    def body(*refs):
        a_ref, b_ref = refs[0], refs[1]
        o_ref = refs[-1]
        acc = _dot(a_ref[...].astype(BF16), b_ref[...].astype(BF16), 1, 0)
        if res is not None:
            acc = acc + refs[2][...]
        o_ref[...] = acc.astype(o_ref.dtype)

    in_specs = [pl.BlockSpec((tm, k), lambda j, i: (i, 0)), pl.BlockSpec((k, tn), lambda j, i: (0, j))]
    args = [a, b]
    if res is not None:
        in_specs.append(pl.BlockSpec((tm, tn), lambda j, i: (i, j)))
        args.append(res)
    return pl.pallas_call(
        body, out_shape=jax.ShapeDtypeStruct((m, n), out_dtype), grid=(n // tn, m // tm),
        in_specs=in_specs, out_specs=pl.BlockSpec((tm, tn), lambda j, i: (i, j)),
        compiler_params=_params("parallel", "arbitrary"), name=name)(*args)


def mm2_nn(a1, b1, a2, b2, name="mm2_nn"):
    m, k = a1.shape
    n = b1.shape[1]
    tm = _row_tile(m)

    def body(a1_ref, b1_ref, a2_ref, b2_ref, o_ref):
        o_ref[...] = _dot(a1_ref[...], b1_ref[...], 1, 0) + _dot(a2_ref[...], b2_ref[...], 1, 0)

    a_spec = pl.BlockSpec((tm, k), lambda i: (i, 0))
    b_spec = pl.BlockSpec((k, n), lambda i: (0, 0))
    return pl.pallas_call(
        body, out_shape=jax.ShapeDtypeStruct((m, n), F32), grid=(m // tm,),
        in_specs=[a_spec, b_spec, a_spec, b_spec], out_specs=pl.BlockSpec((tm, n), lambda i: (i, 0)),
        compiler_params=_params("parallel"), name=name)(a1, b1, a2, b2)


def mm_tn(a, b, out_dtype=BF16, name="mm_tn"):
    t, r = a.shape
    c = b.shape[1]
    tr = _col_tile(r, 512)

    def body(a_ref, b_ref, o_ref):
        o_ref[...] = _dot(a_ref[...].astype(BF16), b_ref[...].astype(BF16), 0, 0).astype(o_ref.dtype)

    return pl.pallas_call(
        body, out_shape=jax.ShapeDtypeStruct((r, c), out_dtype), grid=(r // tr,),
        in_specs=[pl.BlockSpec((t, tr), lambda i: (0, i)), pl.BlockSpec((t, c), lambda i: (0, 0))],
        out_specs=pl.BlockSpec((tr, c), lambda i: (i, 0)),
        compiler_params=_params("parallel"), name=name)(a, b)


def _resident(shape):
    return pl.BlockSpec(shape, lambda i: (0, 0), pipeline_mode=pl.Buffered(1))


def proj_norm(a, b, res, gain, name):
    m, k = a.shape
    n = b.shape[1]
    tm = _row_tile(m)

    def body(a_ref, b_ref, res_ref, g_ref, x_ref, h_ref):
        xv = res_ref[...] + _dot(a_ref[...], b_ref[...], 1, 0)
        x_ref[...] = xv
        r = lax.rsqrt(jnp.mean(xv * xv, axis=-1, keepdims=True) + NORM_EPS)
        h_ref[...] = (xv * r * g_ref[...]).astype(BF16)

    row = pl.BlockSpec((tm, n), lambda i: (i, 0))
    return pl.pallas_call(
        body, out_shape=(jax.ShapeDtypeStruct((m, n), F32), jax.ShapeDtypeStruct((m, n), BF16)), grid=(m // tm,),
        in_specs=[pl.BlockSpec((tm, k), lambda i: (i, 0)), _resident((k, n)), row, _resident((1, n))],
        out_specs=(row, row), compiler_params=_params("parallel"), name=name)(a, b, res, gain)


def proj_loss(a, b, res, target, name):
    m, k = a.shape
    n = b.shape[1]
    tm = _row_tile(m)
    nt = m // tm

    def body(a_ref, b_ref, res_ref, t_ref, dy_ref, dyb_ref, l_ref, acc_ref):
        i = pl.program_id(0)

        @pl.when(i == 0)
        def _():
            acc_ref[...] = jnp.zeros_like(acc_ref)

        err = res_ref[...] + _dot(a_ref[...], b_ref[...], 1, 0) - t_ref[...]
        dy = err * (1.0 / n)
        dy_ref[...] = dy
        dyb_ref[...] = dy.astype(BF16)
        acc_ref[...] += jnp.sum(err * err, axis=0, keepdims=True)

        @pl.when(i == nt - 1)
        def _():
            total = jnp.sum(acc_ref[...], axis=-1, keepdims=True) * (0.5 / n)
            l_ref[...] = jnp.broadcast_to(total, l_ref.shape)

    row = pl.BlockSpec((tm, n), lambda i: (i, 0))
    return pl.pallas_call(
        body, out_shape=(jax.ShapeDtypeStruct((m, n), F32), jax.ShapeDtypeStruct((m, n), BF16),
                         jax.ShapeDtypeStruct((8, LANES), F32)), grid=(nt,),
        in_specs=[pl.BlockSpec((tm, k), lambda i: (i, 0)), _resident((k, n)), row, row],
        out_specs=(row, row, pl.BlockSpec((8, LANES), lambda i: (0, 0))),
        scratch_shapes=[pltpu.VMEM((1, n), F32)],
        compiler_params=_params("arbitrary"), name=name)(a, b, res, target)


def proj_rms_bwd(pairs, x, gain, dres, name):
    m, n = x.shape
    tm = _row_tile(m)
    np_ = len(pairs)

    def body(*refs):
        ab = refs[:2 * np_]
        x_ref, g_ref, dres_ref, dx_ref, dxb_ref, dg_ref = refs[2 * np_:]

        @pl.when(pl.program_id(0) == 0)
        def _():
            dg_ref[...] = jnp.zeros_like(dg_ref)

        dhv = _dot(ab[0][...], ab[1][...], 1, 0)
        for p in range(1, np_):
            dhv = dhv + _dot(ab[2 * p][...], ab[2 * p + 1][...], 1, 0)
        xv = x_ref[...]
        r = lax.rsqrt(jnp.mean(xv * xv, axis=-1, keepdims=True) + NORM_EPS)
        xhat = xv * r
        dg_ref[...] += jnp.sum(dhv * xhat, axis=0, keepdims=True)
        dxhat = dhv * g_ref[...]
        dx = dres_ref[...] + r * (dxhat - xhat * jnp.mean(dxhat * xhat, axis=-1, keepdims=True))
        dx_ref[...] = dx
        dxb_ref[...] = dx.astype(BF16)

    row = pl.BlockSpec((tm, n), lambda i: (i, 0))
    in_specs, args = [], []
    for a, b in pairs:
        in_specs += [pl.BlockSpec((tm, a.shape[1]), lambda i: (i, 0)), _resident(b.shape)]
        args += [a, b]
    return pl.pallas_call(
        body, out_shape=(jax.ShapeDtypeStruct((m, n), F32), jax.ShapeDtypeStruct((m, n), BF16),
                         jax.ShapeDtypeStruct((1, n), F32)), grid=(m // tm,),
        in_specs=in_specs + [row, _resident((1, n)), row], out_specs=(row, row, pl.BlockSpec((1, n), lambda i: (0, 0))),
        compiler_params=_params("arbitrary"), name=name)(*args, x, gain, dres)


FFN_ROWS = 256


def _ffn_row_tile(m):
    return FFN_ROWS if m % FFN_ROWS == 0 else m


def ffn_fwd(h2, wg_t, wu_t, wd, x1, gain=None, target=None, name="ffn_fwd"):
    n, d = h2.shape
    f = wg_t.shape[0]
    tm = _ffn_row_tile(n)
    nt = n // tm
    last = target is not None

    def body(h_ref, wg_ref, wu_ref, wd_ref, x1_ref, e_ref, g_ref, u_ref, a_ref, *rest):
        hv = h_ref[...]
        gv = _dot(hv, wg_ref[...], 1, 1)
        uv = _dot(hv, wu_ref[...], 1, 1)
        g_ref[...] = gv.astype(BF16)
        u_ref[...] = uv.astype(BF16)
        av = (gv * _sigmoid(gv) * uv).astype(BF16)
        a_ref[...] = av
        xv = x1_ref[...] + _dot(av, wd_ref[...], 1, 0)
        if not last:
            x_ref, hn_ref = rest
            x_ref[...] = xv
            r = lax.rsqrt(jnp.mean(xv * xv, axis=-1, keepdims=True) + NORM_EPS)
            hn_ref[...] = (xv * r * e_ref[...]).astype(BF16)
        else:
            dy_ref, dyb_ref, l_ref, acc_ref = rest
            i = pl.program_id(0)

            @pl.when(i == 0)
            def _():
                acc_ref[...] = jnp.zeros_like(acc_ref)

            err = xv - e_ref[...]
            dy = err * (1.0 / d)
            dy_ref[...] = dy
            dyb_ref[...] = dy.astype(BF16)
            acc_ref[...] += jnp.sum(err * err, axis=0, keepdims=True)

            @pl.when(i == nt - 1)
            def _():
                total = jnp.sum(acc_ref[...], axis=-1, keepdims=True) * (0.5 / d)
                l_ref[...] = jnp.broadcast_to(total, l_ref.shape)

    row_d = pl.BlockSpec((tm, d), lambda i: (i, 0))
    row_f = pl.BlockSpec((tm, f), lambda i: (i, 0))
    act_shape = jax.ShapeDtypeStruct((n, f), BF16)
    if not last:
        extra_in, extra = _resident((1, d)), gain
        out_shape = (act_shape, act_shape, act_shape, jax.ShapeDtypeStruct((n, d), F32), jax.ShapeDtypeStruct((n, d), BF16))
        out_specs = (row_f, row_f, row_f, row_d, row_d)
        scratch = []
    else:
        extra_in, extra = row_d, target
        out_shape = (act_shape, act_shape, act_shape, jax.ShapeDtypeStruct((n, d), F32), jax.ShapeDtypeStruct((n, d), BF16),
                     jax.ShapeDtypeStruct((8, LANES), F32))
        out_specs = (row_f, row_f, row_f, row_d, row_d, pl.BlockSpec((8, LANES), lambda i: (0, 0)))
        scratch = [pltpu.VMEM((1, d), F32)]
    return pl.pallas_call(
        body, out_shape=out_shape, grid=(nt,),
        in_specs=[row_d, _resident((f, d)), _resident((f, d)), _resident((f, d)), row_d, extra_in],
        out_specs=out_specs, scratch_shapes=scratch,
        compiler_params=_params("arbitrary"), name=name)(h2, wg_t, wu_t, wd, x1, extra)


def ffn_bwd(dx_b, wd, gate, up, wg_t, wu_t, x1, gain, dres, name="ffn_bwd"):
    n, d = x1.shape
    f = wd.shape[0]
    tm = _ffn_row_tile(n)

    def body(dxb_ref, wd_ref, g_ref, u_ref, wg_ref, wu_ref, x_ref, gain_ref, dres_ref,
             dg_ref, du_ref, dx_ref, dxo_ref, dgain_ref):
        @pl.when(pl.program_id(0) == 0)
        def _():
            dgain_ref[...] = jnp.zeros_like(dgain_ref)

        dact = _dot(dxb_ref[...], wd_ref[...], 1, 1)
        gv = g_ref[...].astype(F32)
        uv = u_ref[...].astype(F32)
        sg = _sigmoid(gv)
        dgv = (dact * uv * sg * (1.0 + gv * (1.0 - sg))).astype(BF16)
        duv = (dact * gv * sg).astype(BF16)
        dg_ref[...] = dgv
        du_ref[...] = duv
        dhv = _dot(dgv, wg_ref[...], 1, 0) + _dot(duv, wu_ref[...], 1, 0)
        xv = x_ref[...]
        r = lax.rsqrt(jnp.mean(xv * xv, axis=-1, keepdims=True) + NORM_EPS)
        xhat = xv * r
        dgain_ref[...] += jnp.sum(dhv * xhat, axis=0, keepdims=True)
        dxhat = dhv * gain_ref[...]
        dx = dres_ref[...] + r * (dxhat - xhat * jnp.mean(dxhat * xhat, axis=-1, keepdims=True))
        dx_ref[...] = dx
        dxo_ref[...] = dx.astype(BF16)

    row_d = pl.BlockSpec((tm, d), lambda i: (i, 0))
    row_f = pl.BlockSpec((tm, f), lambda i: (i, 0))
    w_spec = _resident((f, d))
    act_shape = jax.ShapeDtypeStruct((n, f), BF16)
    return pl.pallas_call(
        body, out_shape=(act_shape, act_shape, jax.ShapeDtypeStruct((n, d), F32), jax.ShapeDtypeStruct((n, d), BF16),
                         jax.ShapeDtypeStruct((1, d), F32)), grid=(n // tm,),
        in_specs=[row_d, w_spec, row_f, row_f, w_spec, w_spec, row_d, _resident((1, d)), row_d],
        out_specs=(row_f, row_f, row_d, row_d, pl.BlockSpec((1, d), lambda i: (0, 0))),
        compiler_params=_params("arbitrary"), name=name)(dx_b, wd, gate, up, wg_t, wu_t, x1, gain, dres)


def mm_tn2(a1, a2, b, name="mm_tn2"):
    t, r = a1.shape
    c = b.shape[1]
    tr = _col_tile(r, 512)

    def body(a1_ref, a2_ref, b_ref, o1_ref, o2_ref):
        bv = b_ref[...]
        o1_ref[...] = _dot(a1_ref[...], bv, 0, 0).astype(o1_ref.dtype)
        o2_ref[...] = _dot(a2_ref[...], bv, 0, 0).astype(o2_ref.dtype)

    a_spec = pl.BlockSpec((t, tr), lambda i: (0, i))
    o_spec = pl.BlockSpec((tr, c), lambda i: (i, 0))
    shape = jax.ShapeDtypeStruct((r, c), BF16)
    return pl.pallas_call(
        body, out_shape=(shape, shape), grid=(r // tr,), in_specs=[a_spec, a_spec, _resident((t, c))],
        out_specs=(o_spec, o_spec), compiler_params=_params("parallel"), name=name)(a1, a2, b)


def rms_fwd(x, g, name="rms_fwd"):
    n, d = x.shape
    tm = _row_tile(n)

    def body(x_ref, g_ref, o_ref):
        xv = x_ref[...]
        r = lax.rsqrt(jnp.mean(xv * xv, axis=-1, keepdims=True) + NORM_EPS)
        o_ref[...] = (xv * r * g_ref[...]).astype(o_ref.dtype)

    return pl.pallas_call(
        body, out_shape=jax.ShapeDtypeStruct((n, d), BF16), grid=(n // tm,),
        in_specs=[pl.BlockSpec((tm, d), lambda i: (i, 0)), pl.BlockSpec((1, d), lambda i: (0, 0))],
        out_specs=pl.BlockSpec((tm, d), lambda i: (i, 0)),
        compiler_params=_params("parallel"), name=name)(x, g)


def rms_bwd(dh, x, g, dres, name="rms_bwd"):
    n, d = x.shape
    tm = _row_tile(n)

    def body(dh_ref, x_ref, g_ref, dres_ref, dx_ref, dxb_ref, dg_ref):
        @pl.when(pl.program_id(0) == 0)
        def _():
            dg_ref[...] = jnp.zeros_like(dg_ref)

        xv = x_ref[...]
        dhv = dh_ref[...].astype(F32)
        r = lax.rsqrt(jnp.mean(xv * xv, axis=-1, keepdims=True) + NORM_EPS)
        xhat = xv * r
        dg_ref[...] += jnp.sum(dhv * xhat, axis=0, keepdims=True)
        dxhat = dhv * g_ref[...]
        mean_t = jnp.mean(dxhat * xhat, axis=-1, keepdims=True)
        dx = dres_ref[...] + r * (dxhat - xhat * mean_t)
        dx_ref[...] = dx
        dxb_ref[...] = dx.astype(BF16)

    row = pl.BlockSpec((tm, d), lambda i: (i, 0))
    vec = pl.BlockSpec((1, d), lambda i: (0, 0))
    return pl.pallas_call(
        body, out_shape=(jax.ShapeDtypeStruct((n, d), F32), jax.ShapeDtypeStruct((n, d), BF16),
                         jax.ShapeDtypeStruct((1, d), F32)), grid=(n // tm,),
        in_specs=[row, row, vec, row], out_specs=(row, row, vec),
        compiler_params=_params("arbitrary"), name=name)(dh, x, g, dres)


def gate_up(h2, wg_t, wu_t, name="gate_up"):
    n, d = h2.shape
    f = wg_t.shape[0]
    tm, tn = _row_tile(n), _col_tile(f)

    def body(h_ref, wg_ref, wu_ref, g_ref, u_ref, a_ref):
        hv = h_ref[...]
        gv = _dot(hv, wg_ref[...], 1, 1)
        uv = _dot(hv, wu_ref[...], 1, 1)
        g_ref[...] = gv.astype(BF16)
        u_ref[...] = uv.astype(BF16)
        a_ref[...] = (gv * _sigmoid(gv) * uv).astype(BF16)

    w_spec = pl.BlockSpec((tn, d), lambda j, i: (j, 0))
    o_spec = pl.BlockSpec((tm, tn), lambda j, i: (i, j))
    o_shape = jax.ShapeDtypeStruct((n, f), BF16)
    return pl.pallas_call(
        body, out_shape=(o_shape, o_shape, o_shape), grid=(f // tn, n // tm),
        in_specs=[pl.BlockSpec((tm, d), lambda j, i: (i, 0)), w_spec, w_spec], out_specs=(o_spec, o_spec, o_spec),
        compiler_params=_params("parallel", "arbitrary"), name=name)(h2, wg_t, wu_t)


def ffn_bwd_act(dx, wd, gate, up, name="ffn_bwd_act"):
    n, d = dx.shape
    f = wd.shape[0]
    tm, tn = _row_tile(n), _col_tile(f)

    def body(dx_ref, wd_ref, g_ref, u_ref, dg_ref, du_ref):
        dact = _dot(dx_ref[...].astype(BF16), wd_ref[...], 1, 1)
        gv = g_ref[...].astype(F32)
        uv = u_ref[...].astype(F32)
        sg = _sigmoid(gv)
        dg_ref[...] = (dact * uv * sg * (1.0 + gv * (1.0 - sg))).astype(BF16)
        du_ref[...] = (dact * gv * sg).astype(BF16)

    t_spec = pl.BlockSpec((tm, tn), lambda j, i: (i, j))
    o_shape = jax.ShapeDtypeStruct((n, f), BF16)
    return pl.pallas_call(
        body, out_shape=(o_shape, o_shape), grid=(f // tn, n // tm),
        in_specs=[pl.BlockSpec((tm, d), lambda j, i: (i, 0)), pl.BlockSpec((tn, d), lambda j, i: (j, 0)), t_spec, t_spec],
        out_specs=(t_spec, t_spec),
        compiler_params=_params("parallel", "arbitrary"), name=name)(dx, wd, gate, up)


def _group_masks(width):
    lane = lax.broadcasted_iota(jnp.int32, (1, width), 1)
    return [(lane >= HEAD_DIM * g) & (lane < HEAD_DIM * (g + 1)) for g in range(width // HEAD_DIM)]


def _group_sum(x, masks):
    out = jnp.zeros_like(x)
    for msk in masks:
        s = jnp.sum(jnp.where(msk, x, 0.0), axis=-1, keepdims=True)
        out = jnp.where(msk, s, out)
    return out


def _head_norm(x, gain, masks):
    r = lax.rsqrt(_group_sum(x * x, masks) * (1.0 / HEAD_DIM) + NORM_EPS)
    xhat = x * r
    return xhat * gain, xhat, r


def _head_norm_bwd(dxn, xhat, r, gain, masks):
    dgain = jnp.sum(dxn * xhat, axis=0, keepdims=True)
    dxhat = dxn * gain
    mean_t = _group_sum(dxhat * xhat, masks) * (1.0 / HEAD_DIM)
    return r * (dxhat - xhat * mean_t), dgain


def _softmax_rows(s):
    e = jnp.exp(s - jnp.max(s, axis=-1, keepdims=True))
    return e * (1.0 / jnp.sum(e, axis=-1, keepdims=True))


def _rel_onehot():
    col = lax.broadcasted_iota(jnp.int32, (1, KEY_WIN), 1)
    off = jnp.where(col < KEY_WIN - LANES, col, col - KEY_WIN)
    idx = jnp.clip(8 * CHUNK - off, -(CHUNK - 1), LANES) + (CHUNK - 1)
    return (lax.broadcasted_iota(jnp.int32, (N_REL, KEY_WIN), 0) == idx).astype(F32)


def bias_blocks(rel16):
    heads = TOK_WIDTH // HEAD_DIM

    def body(rel_ref, o_ref, u_ref):
        u_ref[...] = jnp.dot(rel_ref[...], _rel_onehot(), precision=HIGHEST, preferred_element_type=F32)
        row = lax.broadcasted_iota(jnp.int32, (CHUNK, KEY_WIN), 0)
        col = lax.broadcasted_iota(jnp.int32, (CHUNK, KEY_WIN), 1)
        for h in range(heads):
            xv = jnp.broadcast_to(u_ref[h:h + 1, :], (CHUNK, KEY_WIN))
            for b in range(6):
                xv = jnp.where(((row >> b) & 1) == 1, pltpu.roll(xv, 1 << b, axis=1), xv)
            xv = jnp.where(col < BAND, xv, NEG_INF)
            for i in range(Q_BLOCK // CHUNK):
                o_ref[h, CHUNK * i:CHUNK * (i + 1), :] = pltpu.roll(xv, CHUNK * i, axis=1) if i else xv

    return pl.pallas_call(
        body, out_shape=jax.ShapeDtypeStruct((heads, Q_BLOCK, KEY_WIN), F32),
        scratch_shapes=[pltpu.VMEM((16, KEY_WIN), F32)], name="bias_blocks")(rel16)


def bias_grad(dbias):
    heads = dbias.shape[0]

    def body(db_ref, o_ref, y_ref):
        y_ref[...] = jnp.zeros_like(y_ref)
        row = lax.broadcasted_iota(jnp.int32, (CHUNK, KEY_WIN), 0)
        for h in range(heads):
            fv = db_ref[h, 0:CHUNK, :]
            for i in range(1, Q_BLOCK // CHUNK):
                fv = fv + pltpu.roll(db_ref[h, CHUNK * i:CHUNK * (i + 1), :], KEY_WIN - CHUNK * i, axis=1)
            for b in range(6):
                fv = jnp.where(((row >> b) & 1) == 1, pltpu.roll(fv, KEY_WIN - (1 << b), axis=1), fv)
            y_ref[h:h + 1, :] = jnp.sum(fv, axis=0, keepdims=True)
        o_ref[...] = lax.dot_general(y_ref[...], _rel_onehot(), (((1,), (1,)), ((), ())),
                                     precision=HIGHEST, preferred_element_type=F32)

    return pl.pallas_call(
        body, out_shape=jax.ShapeDtypeStruct((16, N_REL), F32),
        scratch_shapes=[pltpu.VMEM((16, KEY_WIN), F32)], name="bias_grad")(dbias)


def _attn_windows(seq):
    out = []
    for j in range(seq // Q_BLOCK):
        r0 = j * Q_BLOCK
        k0 = max(0, r0 - 8 * CHUNK)
        width = r0 + Q_BLOCK - k0
        out.append((r0, k0, width, KEY_WIN - width))
    return out


def attn_fwd(z, gq2, gk2, bias, batch, seq):
    n = z.shape[0]
    pairs = TOK_WIDTH // LANES

    def body(q_ref, k_ref, v_ref, gq_ref, gk_ref, b_ref, o_ref, qs_s, kn_s):
        masks = _group_masks(LANES)
        qs_s[...] = (_head_norm(q_ref[...].astype(F32), gq_ref[...], masks)[0] * ATTN_SCALE).astype(BF16)
        kn_s[...] = _head_norm(k_ref[...].astype(F32), gk_ref[...], masks)[0].astype(BF16)
        for r0, k0, width, c0 in _attn_windows(seq):
            qb = qs_s[r0:r0 + Q_BLOCK, :]
            kw = kn_s[k0:k0 + width, :]
            vw = v_ref[k0:k0 + width, :]
            out = jnp.zeros((Q_BLOCK, LANES), F32)
            for h, msk in enumerate(masks):
                qh = jnp.where(msk, qb, jnp.zeros_like(qb))
                s = _dot(qh, kw, 1, 1) + b_ref[h, :, c0:KEY_WIN]
                p = _softmax_rows(s).astype(BF16)
                out = jnp.where(msk, _dot(p, vw, 1, 0), out)
            o_ref[r0:r0 + Q_BLOCK, :] = out.astype(o_ref.dtype)

    def col(off):
        return pl.BlockSpec((seq, LANES), lambda b, p: (b, off + p))

    vec = pl.BlockSpec((1, LANES), lambda b, p: (0, 0))
    return pl.pallas_call(
        body, out_shape=jax.ShapeDtypeStruct((n, D_MODEL), BF16), grid=(batch, pairs),
        in_specs=[col(0), col(pairs), col(2 * pairs), vec, vec,
                  pl.BlockSpec((2, Q_BLOCK, KEY_WIN), lambda b, p: (p, 0, 0))],
        out_specs=pl.BlockSpec((seq, LANES), lambda b, p: (b, p)),
        scratch_shapes=[pltpu.VMEM((seq, LANES), BF16), pltpu.VMEM((seq, LANES), BF16)],
        compiler_params=_params("parallel", "arbitrary"), name="attn_fwd")(z, z, z, gq2, gk2, bias)


def attn_bwd(z, dcat, gq2, gk2, bias, batch, seq):
    n = z.shape[0]
    pairs = TOK_WIDTH // LANES

    def body(q_ref, k_ref, v_ref, do_ref, gq_ref, gk_ref, b_ref,
             dz_ref, db_ref, dgq_ref, dgk_ref, qs_s, kn_s, dqn_s, dkn_s, dv_s, dk_o, dv_o):
        pi, bi, which = pl.program_id(0), pl.program_id(1), pl.program_id(2)

        @pl.when(which == 0)
        def _():
            masks = _group_masks(LANES)

            @pl.when(bi == 0)
            def _():
                db_ref[...] = jnp.zeros_like(db_ref)

            @pl.when((bi == 0) & (pi == 0))
            def _():
                dgq_ref[...] = jnp.zeros_like(dgq_ref)
                dgk_ref[...] = jnp.zeros_like(dgk_ref)

            qn, qhat, rq = _head_norm(q_ref[...].astype(F32), gq_ref[...], masks)
            kn, khat, rk = _head_norm(k_ref[...].astype(F32), gk_ref[...], masks)
            qs_s[...] = (qn * ATTN_SCALE).astype(BF16)
            kn_s[...] = kn.astype(BF16)
            dkn_s[...] = jnp.zeros_like(dkn_s)
            dv_s[...] = jnp.zeros_like(dv_s)
            for r0, k0, width, c0 in _attn_windows(seq):
                qb = qs_s[r0:r0 + Q_BLOCK, :]
                dob = do_ref[r0:r0 + Q_BLOCK, :]
                kw = kn_s[k0:k0 + width, :]
                vw = v_ref[k0:k0 + width, :]
                dq_acc = jnp.zeros((Q_BLOCK, LANES), F32)
                dk_acc = jnp.zeros((width, LANES), F32)
                dv_acc = jnp.zeros((width, LANES), F32)
                for h, msk in enumerate(masks):
                    qh = jnp.where(msk, qb, jnp.zeros_like(qb))
                    doh = jnp.where(msk, dob, jnp.zeros_like(dob))
                    p = _softmax_rows(_dot(qh, kw, 1, 1) + b_ref[h, :, c0:KEY_WIN])
                    dp = _dot(doh, vw, 1, 1)
                    ds = p * (dp - jnp.sum(p * dp, axis=-1, keepdims=True))
                    db_ref[h, :, c0:KEY_WIN] += ds
                    dsb = ds.astype(BF16)
                    dq_acc = jnp.where(msk, _dot(dsb, kw, 1, 0), dq_acc)
                    dk_acc = jnp.where(msk, _dot(dsb, qb, 0, 0), dk_acc)
                    dv_acc = jnp.where(msk, _dot(p.astype(BF16), dob, 0, 0), dv_acc)
                dqn_s[r0:r0 + Q_BLOCK, :] = dq_acc * ATTN_SCALE
                dkn_s[k0:k0 + width, :] += dk_acc
                dv_s[k0:k0 + width, :] += dv_acc
            dq, dgq = _head_norm_bwd(dqn_s[...], qhat, rq, gq_ref[...], masks)
            dk, dgk = _head_norm_bwd(dkn_s[...], khat, rk, gk_ref[...], masks)
            dz_ref[...] = dq.astype(dz_ref.dtype)
            dk_o[...] = dk.astype(dk_o.dtype)
            dv_o[...] = dv_s[...].astype(dv_o.dtype)
            dgq_ref[...] += dgq
            dgk_ref[...] += dgk

        @pl.when(which == 1)
        def _():
            dz_ref[...] = dk_o[...]

        @pl.when(which == 2)
        def _():
            dz_ref[...] = dv_o[...]

    def ahead(p, b, t):
        nb = b + jnp.where(t > 0, 1, 0)
        wrap = jnp.where(nb >= batch, 1, 0)
        return jnp.minimum(p + wrap, pairs - 1), nb - wrap * batch

    def col(off):
        def index(p, b, t):
            np_, nb = ahead(p, b, t)
            return nb, off + np_
        return pl.BlockSpec((seq, LANES), index)

    vec = pl.BlockSpec((1, LANES), lambda p, b, t: (0, 0))
    blk = pl.BlockSpec((2, Q_BLOCK, KEY_WIN), lambda p, b, t: (p, 0, 0))
    blk_in = pl.BlockSpec((2, Q_BLOCK, KEY_WIN), lambda p, b, t: (ahead(p, b, t)[0], 0, 0))
    v_shape = jax.ShapeDtypeStruct((1, LANES), F32)
    return pl.pallas_call(
        body,
        out_shape=(jax.ShapeDtypeStruct(z.shape, BF16), jax.ShapeDtypeStruct(bias.shape, F32), v_shape, v_shape),
        grid=(pairs, batch, 3),
        in_specs=[col(0), col(pairs), col(2 * pairs), col(0), vec, vec, blk_in],
        out_specs=(pl.BlockSpec((seq, LANES), lambda p, b, t: (b, t * pairs + p)), blk, vec, vec),
        scratch_shapes=[pltpu.VMEM((seq, LANES), BF16), pltpu.VMEM((seq, LANES), BF16),
                        pltpu.VMEM((seq, LANES), F32), pltpu.VMEM((seq, LANES), F32), pltpu.VMEM((seq, LANES), F32),
                        pltpu.VMEM((seq, LANES), BF16), pltpu.VMEM((seq, LANES), BF16)],
        compiler_params=_params("arbitrary", "arbitrary", "arbitrary"), name="attn_bwd")(
            z, z, z, dcat, gq2, gk2, bias)


MEM_ROWS = 512


def memattn_fwd(z, kv, gq4, gk4, cat, batch, seq, qcol, name):
    mtok = kv.shape[0] // batch
    rows = min(MEM_ROWS, seq)

    def body(q_ref, kv_ref, gq_ref, gk_ref, cat_ref, o_ref):
        del cat_ref
        masks = _group_masks(MEM_WIDTH)
        kn = _head_norm(kv_ref[:, 0:MEM_WIDTH], gk_ref[...], masks)[0].astype(BF16)
        vm = kv_ref[:, MEM_WIDTH:2 * MEM_WIDTH].astype(BF16)
        for t in range(seq // rows):
            sl = slice(t * rows, (t + 1) * rows)
            qs = (_head_norm(q_ref[sl, :].astype(F32), gq_ref[...], masks)[0] * ATTN_SCALE).astype(BF16)
            out = jnp.zeros((rows, MEM_WIDTH), F32)
            for msk in masks:
                qh = jnp.where(msk, qs, jnp.zeros_like(qs))
                p = _softmax_rows(_dot(qh, kn, 1, 1)).astype(BF16)
                out = jnp.where(msk, _dot(p, vm, 1, 0), out)
            o_ref[sl, :] = out.astype(o_ref.dtype)

    vec = pl.BlockSpec((1, MEM_WIDTH), lambda b: (0, 0))
    return pl.pallas_call(
        body, out_shape=jax.ShapeDtypeStruct(cat.shape, cat.dtype), grid=(batch,),
        in_specs=[pl.BlockSpec((seq, MEM_WIDTH), lambda b: (b, qcol)),
                  pl.BlockSpec((mtok, 2 * MEM_WIDTH), lambda b: (b, 0)), vec, vec, ANY],
        out_specs=pl.BlockSpec((seq, MEM_WIDTH), lambda b: (b, TOK_WIDTH // MEM_WIDTH)),
        input_output_aliases={4: 0},
        compiler_params=_params("parallel"), name=name)(z, kv, gq4, gk4, cat)


def memattn_bwd(z, kv, dcat, gq4, gk4, dz, batch, seq, qcol, name):
    mtok = kv.shape[0] // batch
    rows = min(MEM_ROWS, seq)

    def body(q_ref, kv_ref, do_ref, gq_ref, gk_ref, dz_in_ref, dq_ref, dkv_ref, dgq_ref, dgk_ref):
        del dz_in_ref
        @pl.when(pl.program_id(0) == 0)
        def _():
            dgq_ref[...] = jnp.zeros_like(dgq_ref)
            dgk_ref[...] = jnp.zeros_like(dgk_ref)

        masks = _group_masks(MEM_WIDTH)
        kn_f, khat, rk = _head_norm(kv_ref[:, 0:MEM_WIDTH], gk_ref[...], masks)
        kn = kn_f.astype(BF16)
        vm = kv_ref[:, MEM_WIDTH:2 * MEM_WIDTH].astype(BF16)
        dkn = jnp.zeros((mtok, MEM_WIDTH), F32)
        dvm = jnp.zeros((mtok, MEM_WIDTH), F32)
        dgq = jnp.zeros((1, MEM_WIDTH), F32)
        for t in range(seq // rows):
            sl = slice(t * rows, (t + 1) * rows)
            qn_f, qhat, rq = _head_norm(q_ref[sl, :].astype(F32), gq_ref[...], masks)
            qs = (qn_f * ATTN_SCALE).astype(BF16)
            dob = do_ref[sl, :]
            dqn = jnp.zeros((rows, MEM_WIDTH), F32)
            for msk in masks:
                qh = jnp.where(msk, qs, jnp.zeros_like(qs))
                doh = jnp.where(msk, dob, jnp.zeros_like(dob))
                p = _softmax_rows(_dot(qh, kn, 1, 1))
                dp = _dot(doh, vm, 1, 1)
                ds = p * (dp - jnp.sum(p * dp, axis=-1, keepdims=True))
                dsb = ds.astype(BF16)
                dqn = jnp.where(msk, _dot(dsb, kn, 1, 0), dqn)
                dkn = dkn + jnp.where(msk, _dot(dsb, qs, 0, 0), 0.0)
                dvm = dvm + jnp.where(msk, _dot(p.astype(BF16), dob, 0, 0), 0.0)
            dq, dg = _head_norm_bwd(dqn * ATTN_SCALE, qhat, rq, gq_ref[...], masks)
            dq_ref[sl, :] = dq.astype(dq_ref.dtype)
            dgq = dgq + dg
        dk, dgk = _head_norm_bwd(dkn, khat, rk, gk_ref[...], masks)
        dkv_ref[:, 0:MEM_WIDTH] = dk
        dkv_ref[:, MEM_WIDTH:2 * MEM_WIDTH] = dvm
        dgq_ref[...] += dgq
        dgk_ref[...] += dgk

    vec = pl.BlockSpec((1, MEM_WIDTH), lambda b: (0, 0))
    kv_spec = pl.BlockSpec((mtok, 2 * MEM_WIDTH), lambda b: (b, 0))
    v_shape = jax.ShapeDtypeStruct((1, MEM_WIDTH), F32)
    q_spec = pl.BlockSpec((seq, MEM_WIDTH), lambda b: (b, qcol))
    return pl.pallas_call(
        body,
        out_shape=(jax.ShapeDtypeStruct(dz.shape, dz.dtype), jax.ShapeDtypeStruct(kv.shape, F32), v_shape, v_shape),
        grid=(batch,),
        in_specs=[q_spec, kv_spec, pl.BlockSpec((seq, MEM_WIDTH), lambda b: (b, TOK_WIDTH // MEM_WIDTH)), vec, vec, ANY],
        out_specs=(q_spec, kv_spec, vec, vec),
        input_output_aliases={5: 0},
        compiler_params=_params("arbitrary"), name=name)(z, kv, dcat, gq4, gk4, dz)


CONV_ROWS = 256


def _glu(a_ref, g_ref):
    return a_ref[...].astype(F32) * _sigmoid(g_ref[...].astype(F32))


def _layer_norm_stats(y):
    mu = jnp.mean(y, axis=-1, keepdims=True)
    yc = y - mu
    rstd = lax.rsqrt(jnp.mean(yc * yc, axis=-1, keepdims=True) + NORM_EPS)
    return yc * rstd, rstd


CONV_WIN = CONV_HALO + CONV_ROWS
SUBLANES = 8
SHIFT_ROWS = CONV_WIN - SUBLANES


def _preshift(win, shifted):
    for s in range(1, SUBLANES):
        shifted[s - 1, :, :] = win[s:s + SHIFT_ROWS, :]


TAP_ROWS = 64
TAP_TILES = [(r0, slice(c0, c0 + LANES)) for c0 in range(0, TOK_WIDTH, LANES) for r0 in range(0, CONV_ROWS, TAP_ROWS)]


def _tap(win, shifted, off, r0, lanes):
    s = off % SUBLANES
    base = off - s + r0
    if s == 0:
        return win[base:base + TAP_ROWS, lanes]
    return shifted[s - 1, base:base + TAP_ROWS, lanes]


def _fold_rows(x):
    return jnp.sum(x.reshape(TAP_ROWS // SUBLANES, SUBLANES, LANES), axis=0)


def conv_fwd(z, cw, cb, lg, lb, batch, seq):
    n = z.shape[0]
    nt = seq // CONV_ROWS
    sub = CONV_ROWS // CONV_HALO
    lead = CONV_HALO - (CONV_W - 1)

    def body(a_ref, g_ref, ap_ref, gp_ref, cw_ref, cb_ref, lg_ref, lb_ref, o_ref, y_ref, win, shifted):
        first = pl.program_id(1) == 0
        win[0:CONV_HALO, :] = jnp.where(first, 0.0, _glu(ap_ref, gp_ref))
        win[CONV_HALO:CONV_WIN, :] = _glu(a_ref, g_ref)
        _preshift(win, shifted)
        for r0, lanes in TAP_TILES:
            acc = jnp.zeros((TAP_ROWS, LANES), F32) + cb_ref[:, lanes]
            for w in range(CONV_W):
                acc = acc + _tap(win, shifted, lead + w, r0, lanes) * cw_ref[w:w + 1, lanes]
            y_ref[r0:r0 + TAP_ROWS, lanes] = acc
        yh, _ = _layer_norm_stats(y_ref[...])
        t = yh * lg_ref[...] + lb_ref[...]
        o_ref[...] = (t * _sigmoid(t)).astype(o_ref.dtype)

    def cur(c):
        return pl.BlockSpec((CONV_ROWS, TOK_WIDTH), lambda b, i: (b * nt + i, c))

    def prev(c):
        return pl.BlockSpec((CONV_HALO, TOK_WIDTH), lambda b, i: (jnp.maximum((b * nt + i) * sub - 1, 0), c))

    vec = pl.BlockSpec((1, TOK_WIDTH), lambda b, i: (0, 0))
    return pl.pallas_call(
        body, out_shape=(jax.ShapeDtypeStruct((n, D_MODEL), BF16), jax.ShapeDtypeStruct((n, TOK_WIDTH), F32)),
        grid=(batch, nt),
        in_specs=[cur(0), cur(1), prev(0), prev(1), pl.BlockSpec((32, TOK_WIDTH), lambda b, i: (0, 0)), vec, vec, vec],
        out_specs=(cur(0), cur(0)),
        scratch_shapes=[pltpu.VMEM((CONV_WIN, TOK_WIDTH), F32), pltpu.VMEM((SUBLANES - 1, SHIFT_ROWS, TOK_WIDTH), F32)],
        compiler_params=_params("parallel", "arbitrary"), name="conv_fwd")(z, z, z, z, cw, cb, lg, lb)


def conv_bwd(z, y, dcat, cw, lg, lb, batch, seq):
    n = z.shape[0]
    nt = seq // CONV_ROWS
    sub = CONV_ROWS // CONV_HALO
    lead = CONV_HALO - (CONV_W - 1)
    last_blk = n // CONV_HALO - 1

    def body(a_ref, g_ref, ap_ref, gp_ref, y_ref, yn_ref, do_ref, don_ref, cw_ref, lg_ref, lb_ref,
             dz_ref, dcw_ref, dsm_ref, win, shifted, dyw, dshifted, dg_o):
        b, i, which = pl.program_id(0), pl.program_id(1), pl.program_id(2)

        @pl.when(which == 0)
        def _():
            first, last = i == 0, i == nt - 1

            @pl.when((b == 0) & (i == 0))
            def _():
                dcw_ref[...] = jnp.zeros_like(dcw_ref)
                dsm_ref[...] = jnp.zeros_like(dsm_ref)

            win[0:CONV_HALO, :] = jnp.where(first, 0.0, _glu(ap_ref, gp_ref))
            win[CONV_HALO:CONV_WIN, :] = _glu(a_ref, g_ref)
            _preshift(win, shifted)
            yv = jnp.concatenate([y_ref[...], yn_ref[...]], axis=0)
            yh, rstd = _layer_norm_stats(yv)
            t = yh * lg_ref[...] + lb_ref[...]
            st = _sigmoid(t)
            dout = jnp.concatenate(
                [do_ref[...].astype(F32), jnp.where(last, 0.0, don_ref[...].astype(F32))], axis=0)
            dt = dout * st * (1.0 + t * (1.0 - st))
            dyh = dt * lg_ref[...]
            dy = rstd * (dyh - jnp.mean(dyh, axis=-1, keepdims=True)
                         - yh * jnp.mean(dyh * yh, axis=-1, keepdims=True))
            dyw[...] = dy
            _preshift(dyw, dshifted)
            dsm_ref[0:1, :] += jnp.sum(dy[0:CONV_ROWS], axis=0, keepdims=True)
            dsm_ref[1:2, :] += jnp.sum((dt * yh)[0:CONV_ROWS], axis=0, keepdims=True)
            dsm_ref[2:3, :] += jnp.sum(dt[0:CONV_ROWS], axis=0, keepdims=True)
            for c0 in range(0, TOK_WIDTH, LANES):
                lanes = slice(c0, c0 + LANES)
                dcw_acc = [jnp.zeros((SUBLANES, LANES), F32) for _ in range(CONV_W)]
                for r0 in range(0, CONV_ROWS, TAP_ROWS):
                    dyt = dyw[r0:r0 + TAP_ROWS, lanes]
                    dglu = jnp.zeros((TAP_ROWS, LANES), F32)
                    for w in range(CONV_W):
                        dcw_acc[w] = dcw_acc[w] + _fold_rows(dyt * _tap(win, shifted, lead + w, r0, lanes))
                        dglu = dglu + _tap(dyw, dshifted, CONV_W - 1 - w, r0, lanes) * cw_ref[w:w + 1, lanes]
                    avt = a_ref[r0:r0 + TAP_ROWS, lanes].astype(F32)
                    sgt = _sigmoid(g_ref[r0:r0 + TAP_ROWS, lanes].astype(F32))
                    dz_ref[r0:r0 + TAP_ROWS, lanes] = (dglu * sgt).astype(dz_ref.dtype)
                    dg_o[r0:r0 + TAP_ROWS, lanes] = (dglu * avt * sgt * (1.0 - sgt)).astype(dg_o.dtype)
                for w in range(CONV_W):
                    dcw_ref[w:w + 1, lanes] += jnp.sum(dcw_acc[w], axis=0, keepdims=True)

        @pl.when(which == 1)
        def _():
            dz_ref[...] = dg_o[...]

    def ahead(b, i, t):
        return jnp.minimum(b * nt + i + t, batch * nt - 1)

    def cur(c):
        return pl.BlockSpec((CONV_ROWS, TOK_WIDTH), lambda b, i, t: (ahead(b, i, t), c))

    def prev(c):
        return pl.BlockSpec((CONV_HALO, TOK_WIDTH), lambda b, i, t: (jnp.maximum(ahead(b, i, t) * sub - 1, 0), c))

    nxt = pl.BlockSpec((CONV_HALO, TOK_WIDTH),
                       lambda b, i, t: (jnp.minimum((ahead(b, i, t) + 1) * sub, last_blk), 0))
    vec = pl.BlockSpec((1, TOK_WIDTH), lambda b, i, t: (0, 0))
    full32 = pl.BlockSpec((32, TOK_WIDTH), lambda b, i, t: (0, 0))
    return pl.pallas_call(
        body,
        out_shape=(jax.ShapeDtypeStruct(z.shape, BF16), jax.ShapeDtypeStruct((32, TOK_WIDTH), F32),
                   jax.ShapeDtypeStruct((8, TOK_WIDTH), F32)),
        grid=(batch, nt, 2),
        in_specs=[cur(0), cur(1), prev(0), prev(1), cur(0), nxt, cur(0), nxt, full32, vec, vec],
        out_specs=(pl.BlockSpec((CONV_ROWS, TOK_WIDTH), lambda b, i, t: (b * nt + i, t)), full32,
                   pl.BlockSpec((8, TOK_WIDTH), lambda b, i, t: (0, 0))),
        scratch_shapes=[pltpu.VMEM((CONV_WIN, TOK_WIDTH), F32), pltpu.VMEM((SUBLANES - 1, SHIFT_ROWS, TOK_WIDTH), F32),
                        pltpu.VMEM((CONV_WIN, TOK_WIDTH), F32), pltpu.VMEM((SUBLANES - 1, SHIFT_ROWS, TOK_WIDTH), F32),
                        pltpu.VMEM((CONV_ROWS, TOK_WIDTH), BF16)],
        compiler_params=_params("arbitrary", "arbitrary", "arbitrary"), name="conv_bwd")(
            z, z, z, z, y, y, dcat, dcat, cw, lg, lb)


def loss_head(y, target):
    n, d = y.shape
    tm = _row_tile(n)
    nt = n // tm

    def body(y_ref, t_ref, dy_ref, dyb_ref, l_ref, acc_ref):
        i = pl.program_id(0)

        @pl.when(i == 0)
        def _():
            acc_ref[...] = jnp.zeros_like(acc_ref)

        err = y_ref[...] - t_ref[...]
        dy = err * (1.0 / d)
        dy_ref[...] = dy
        dyb_ref[...] = dy.astype(BF16)
        acc_ref[...] += jnp.sum(err * err, axis=0, keepdims=True)

        @pl.when(i == nt - 1)
        def _():
            total = jnp.sum(acc_ref[...], axis=-1, keepdims=True) * (0.5 / d)
            l_ref[...] = jnp.broadcast_to(total, l_ref.shape)

    row = pl.BlockSpec((tm, d), lambda i: (i, 0))
    return pl.pallas_call(
        body, out_shape=(jax.ShapeDtypeStruct((n, d), F32), jax.ShapeDtypeStruct((n, d), BF16),
                         jax.ShapeDtypeStruct((8, LANES), F32)), grid=(nt,),
        in_specs=[row, row], out_specs=(row, row, pl.BlockSpec((8, LANES), lambda i: (0, 0))),
        scratch_shapes=[pltpu.VMEM((1, d), F32)],
        compiler_params=_params("arbitrary"), name="loss_head")(y, target)


def col_sum(x, name="col_sum"):
    n, c = x.shape
    tm = _row_tile(n)

    def body(x_ref, o_ref):
        @pl.when(pl.program_id(0) == 0)
        def _():
            o_ref[...] = jnp.zeros_like(o_ref)

        o_ref[...] += jnp.sum(x_ref[...].astype(F32), axis=0, keepdims=True)

    return pl.pallas_call(
        body, out_shape=jax.ShapeDtypeStruct((1, c), F32), grid=(n // tm,),
        in_specs=[pl.BlockSpec((tm, c), lambda i: (i, 0))], out_specs=pl.BlockSpec((1, c), lambda i: (0, 0)),
        compiler_params=_params("arbitrary"), name=name)(x)


def adamw(w, g, m, v, name="adamw"):
    rows, cols = w.shape
    tr = rows
    for cand in (512, 256, 128, 64, 32, 16, 8):
        if rows % cand == 0 and rows > cand:
            tr = cand
            break
    c1 = 1.0 / (1.0 - ADAM_B1 ** ADAM_STEP)
    c2 = 1.0 / (1.0 - ADAM_B2 ** ADAM_STEP)

    def body(w_ref, g_ref, m_ref, v_ref, d_ref, nm_ref, nv_ref):
        gv = g_ref[...]
        nm = ADAM_B1 * m_ref[...] + (1.0 - ADAM_B1) * gv
        nv = ADAM_B2 * v_ref[...] + (1.0 - ADAM_B2) * (gv * gv)
        nm_ref[...] = nm
        nv_ref[...] = nv
        d_ref[...] = -ADAM_LR * ((nm * c1) / (jnp.sqrt(nv * c2) + ADAM_EPS) + ADAM_WD * w_ref[...])

    spec = pl.BlockSpec((tr, cols), lambda i: (i, 0))
    shape = jax.ShapeDtypeStruct((rows, cols), F32)
    return pl.pallas_call(
        body, out_shape=(shape, shape, shape), grid=(rows // tr,),
        in_specs=[spec, spec, spec, spec], out_specs=(spec, spec, spec),
        compiler_params=_params("parallel"), name=name)(w, g, m, v)


def _place():
    return lax.axis_index("x"), lax.axis_index("y"), lax.axis_index("c")


def _other_chips(x, y):
    return [(1 - x, y), (x, 1 - y), (1 - x, 1 - y)]


def small_exchange(slab, reduce):
    r = slab.shape[0]

    def body(in_ref, o_ref, *scratch):
        if reduce:
            buf, send_sems, recv_sems = scratch
        else:
            buf = o_ref
            send_sems, recv_sems = scratch
        x, y, c = _place()
        me = 4 * x + 2 * y + c
        buf[me] = in_ref[...]
        copies = []
        for k in range(1, N_DEV):
            peer = (x ^ (k >> 2), y ^ ((k >> 1) & 1), c ^ (k & 1))
            cp = pltpu.make_async_remote_copy(
                src_ref=in_ref, dst_ref=buf.at[me], send_sem=send_sems.at[k - 1], recv_sem=recv_sems.at[k - 1],
                device_id=peer, device_id_type=MESH)
            cp.start()
            copies.append(cp)
        for k in range(1, N_DEV):
            src = 4 * (x ^ (k >> 2)) + 2 * (y ^ ((k >> 1) & 1)) + (c ^ (k & 1))
            pltpu.make_async_remote_copy(
                src_ref=in_ref, dst_ref=buf.at[src], send_sem=send_sems.at[k - 1], recv_sem=recv_sems.at[k - 1],
                device_id=(x, y, c), device_id_type=MESH).wait_recv()
        for cp in copies:
            cp.wait_send()
        if reduce:
            total = buf[0]
            for d in range(1, N_DEV):
                total = total + buf[d]
            o_ref[...] = total

    sems = [pltpu.SemaphoreType.DMA((N_DEV - 1,)), pltpu.SemaphoreType.DMA((N_DEV - 1,))]
    if reduce:
        out_shape = jax.ShapeDtypeStruct((r, LANES), F32)
        scratch = [pltpu.VMEM((N_DEV, r, LANES), F32)] + sems
    else:
        out_shape = jax.ShapeDtypeStruct((N_DEV, r, LANES), F32)
        scratch = sems
    vmem = pl.BlockSpec(memory_space=pltpu.VMEM)
    return pl.pallas_call(
        body, out_shape=out_shape, in_specs=[vmem], out_specs=vmem, scratch_shapes=scratch,
        compiler_params=pltpu.CompilerParams(vmem_limit_bytes=VMEM_LIMIT),
        name="small_reduce" if reduce else "small_gather")(slab)


def reduce_small(arrays):
    na = len(arrays)

    def body(*refs):
        ins, outs, bufs = refs[:na], refs[na:2 * na], refs[2 * na:3 * na]
        send_sems, recv_sems = refs[3 * na:]
        x, y, c = _place()
        me = 4 * x + 2 * y + c
        copies = []
        for a in range(na):
            bufs[a][me] = ins[a][...]
            for k in range(1, N_DEV):
                cp = pltpu.make_async_remote_copy(
                    src_ref=ins[a], dst_ref=bufs[a].at[me], send_sem=send_sems.at[a, k - 1],
                    recv_sem=recv_sems.at[a, k - 1],
                    device_id=(x ^ (k >> 2), y ^ ((k >> 1) & 1), c ^ (k & 1)), device_id_type=MESH)
                cp.start()
                copies.append(cp)
        for a in range(na):
            for k in range(1, N_DEV):
                src = 4 * (x ^ (k >> 2)) + 2 * (y ^ ((k >> 1) & 1)) + (c ^ (k & 1))
                pltpu.make_async_remote_copy(
                    src_ref=ins[a], dst_ref=bufs[a].at[src], send_sem=send_sems.at[a, k - 1],
                    recv_sem=recv_sems.at[a, k - 1], device_id=(x, y, c), device_id_type=MESH).wait_recv()
        for cp in copies:
            cp.wait_send()
        for a in range(na):
            total = bufs[a][0]
            for dev in range(1, N_DEV):
                total = total + bufs[a][dev]
            outs[a][...] = total

    vmem = pl.BlockSpec(memory_space=pltpu.VMEM)
    return pl.pallas_call(
        body, out_shape=tuple(jax.ShapeDtypeStruct(a.shape, F32) for a in arrays),
        in_specs=[vmem] * na, out_specs=tuple([vmem] * na),
        scratch_shapes=[pltpu.VMEM((N_DEV,) + a.shape, F32) for a in arrays]
        + [pltpu.SemaphoreType.DMA((na, N_DEV - 1)), pltpu.SemaphoreType.DMA((na, N_DEV - 1))],
        compiler_params=pltpu.CompilerParams(vmem_limit_bytes=VMEM_LIMIT), name="small_reduce")(*arrays)


def adamw_small(ws, gs, ms, vs):
    na = len(ws)
    c1 = 1.0 / (1.0 - ADAM_B1 ** ADAM_STEP)
    c2 = 1.0 / (1.0 - ADAM_B2 ** ADAM_STEP)

    def body(*refs):
        w_refs, g_refs, m_refs, v_refs = (refs[i * na:(i + 1) * na] for i in range(4))
        d_refs, nm_refs, nv_refs = (refs[(4 + i) * na:(5 + i) * na] for i in range(3))
        for a in range(na):
            gv = g_refs[a][...]
            nm = ADAM_B1 * m_refs[a][...] + (1.0 - ADAM_B1) * gv
            nv = ADAM_B2 * v_refs[a][...] + (1.0 - ADAM_B2) * (gv * gv)
            nm_refs[a][...] = nm
            nv_refs[a][...] = nv
            d_refs[a][...] = -ADAM_LR * ((nm * c1) / (jnp.sqrt(nv * c2) + ADAM_EPS) + ADAM_WD * w_refs[a][...])

    vmem = pl.BlockSpec(memory_space=pltpu.VMEM)
    shapes = tuple(jax.ShapeDtypeStruct(w.shape, F32) for w in ws)
    outs = pl.pallas_call(
        body, out_shape=shapes * 3, in_specs=[vmem] * (4 * na), out_specs=tuple([vmem] * (3 * na)),
        compiler_params=pltpu.CompilerParams(vmem_limit_bytes=VMEM_LIMIT), name="adamw_small")(*ws, *gs, *ms, *vs)
    return outs[:na], outs[na:2 * na], outs[2 * na:]


def gather_weights(shards, name, collective_id):
    nw = len(shards)
    ns = [s.shape[0] for s in shards]
    in_refs = [jax.new_ref(s, memory_space=pltpu.MemorySpace.HBM) for s in shards]
    out_refs = [jax.empty_ref(jax.ShapeDtypeStruct((N_DEV * s.shape[0], s.shape[1]), s.dtype),
                              memory_space=pltpu.MemorySpace.HBM) for s in shards]

    @pl.kernel(mesh=plsc.ScalarSubcoreMesh(axis_name="seq", num_cores=1), name=name,
               scratch_types=(pltpu.SemaphoreType.DMA((nw, 7)), pltpu.SemaphoreType.DMA((nw, 7)),
                              pltpu.SemaphoreType.DMA((nw,))),
               compiler_params=pltpu.CompilerParams(collective_id=collective_id))
    def launch(send_sems, recv_sems, local_sems):
        x, y, c = _place()
        me, sib = (x, y, c), (x, y, 1 - c)
        chips = _other_chips(x, y)
        barrier = pltpu.get_barrier_semaphore()
        for peer in [sib] + [(*chip, c) for chip in chips]:
            pl.semaphore_signal(barrier, inc=1, device_id=peer, device_id_type=MESH)
        pl.semaphore_wait(barrier, 4)

        def rows(w, dev):
            return out_refs[w].at[pl.ds((4 * dev[0] + 2 * dev[1] + dev[2]) * ns[w], ns[w]), :]

        def copy(w, k, block, to, src=None):
            return pltpu.make_async_remote_copy(
                src_ref=rows(w, block) if src is None else src, dst_ref=rows(w, block),
                send_sem=send_sems.at[w, k], recv_sem=recv_sems.at[w, k], device_id=to, device_id_type=MESH)

        started, sends = [], []
        for w in range(nw):
            mine = pltpu.make_async_copy(in_refs[w], rows(w, me), local_sems.at[w])
            mine.start()
            started.append(mine)
            first = [copy(w, 0, me, sib, src=in_refs[w])]
            first += [copy(w, 1 + j, me, (*chip, c), src=in_refs[w]) for j, chip in enumerate(chips)]
            for cp in first:
                cp.start()
            sends += first
        for w in range(nw):
            for j, chip in enumerate(chips):
                copy(w, 1 + j, (*chip, c), me).wait_recv()
                fwd = copy(w, 4 + j, (*chip, c), sib)
                fwd.start()
                sends.append(fwd)
        for w in range(nw):
            copy(w, 0, sib, me).wait_recv()
            for j, chip in enumerate(chips):
                copy(w, 4 + j, (*chip, 1 - c), me).wait_recv()
        for cp in sends:
            cp.wait_send()
        for mine in started:
            mine.wait()

    launch()
    return [r[...] for r in out_refs]


def _sequencer_exchange(sources, out_rows, peers_of, copies_of, name, collective_id):
    nw = len(sources)
    in_refs = [jax.new_ref(s, memory_space=pltpu.MemorySpace.HBM) for s in sources]
    out_refs = [jax.empty_ref(jax.ShapeDtypeStruct((rows, s.shape[1]), s.dtype), memory_space=pltpu.MemorySpace.HBM)
                for rows, s in zip(out_rows, sources)]
    per = len(copies_of(0, 0, 0, 0))

    @pl.kernel(mesh=plsc.ScalarSubcoreMesh(axis_name="seq", num_cores=1), name=name,
               scratch_types=(pltpu.SemaphoreType.DMA((nw, per)), pltpu.SemaphoreType.DMA((nw, per))),
               compiler_params=pltpu.CompilerParams(collective_id=collective_id))
    def launch(send_sems, recv_sems):
        x, y, c = _place()
        peers = peers_of(x, y, c)
        barrier = pltpu.get_barrier_semaphore()
        for peer in peers:
            pl.semaphore_signal(barrier, inc=1, device_id=peer, device_id_type=MESH)
        pl.semaphore_wait(barrier, len(peers))
        copies = []
        for w in range(nw):
            for k, (src_blk, dst_blk, rows, peer) in enumerate(copies_of(x, y, c, w)):
                cp = pltpu.make_async_remote_copy(
                    src_ref=in_refs[w].at[pl.ds(src_blk * rows, rows), :],
                    dst_ref=out_refs[w].at[pl.ds(dst_blk * rows, rows), :],
                    send_sem=send_sems.at[w, k], recv_sem=recv_sems.at[w, k], device_id=peer, device_id_type=MESH)
                cp.start()
                copies.append(cp)
        for cp in copies:
            cp.wait_recv()
        for cp in copies:
            cp.wait_send()

    launch()
    return [r[...] for r in out_refs]


def scatter_to_sibling(grads, name, collective_id):
    ns = [g.shape[0] // N_DEV for g in grads]
    return _sequencer_exchange(
        grads, [4 * n for n in ns],
        lambda x, y, c: [(x, y, 1 - c)],
        lambda x, y, c, w: [(2 * q + 1 - c, q, ns[w], (x, y, 1 - c)) for q in range(4)],
        name, collective_id)


def scatter_to_chips(parts, name, collective_id):
    ns = [p.shape[0] // 4 for p in parts]
    return _sequencer_exchange(
        parts, [3 * n for n in ns],
        lambda x, y, c: [(*chip, c) for chip in _other_chips(x, y)],
        lambda x, y, c, w: [(2 * chip[0] + chip[1], j, ns[w], (*chip, c)) for j, chip in enumerate(_other_chips(x, y))],
        name, collective_id)


def add_sibling(grads, landeds, core, name):
    nw = len(grads)

    def body(c_ref, *refs):
        for w in range(nw):
            g_ref, l_ref, o_ref = refs[2 * w], refs[2 * w + 1], refs[2 * nw + w]
            o_ref[...] = (g_ref[...].astype(F32) + l_ref[...].astype(F32)).astype(o_ref.dtype)

    in_specs, out_specs, args = [], [], []
    for g, ld in zip(grads, landeds):
        n, cols = ld.shape[0] // 4, g.shape[1]
        in_specs += [pl.BlockSpec((n, cols), lambda q, c_ref: (2 * q + c_ref[0], 0)),
                     pl.BlockSpec((n, cols), lambda q, c_ref: (q, 0))]
        out_specs.append(pl.BlockSpec((n, cols), lambda q, c_ref: (q, 0)))
        args += [g, ld]
    grid_spec = pltpu.PrefetchScalarGridSpec(
        num_scalar_prefetch=1, grid=(4,), in_specs=in_specs, out_specs=tuple(out_specs))
    return pl.pallas_call(
        body, out_shape=tuple(jax.ShapeDtypeStruct(ld.shape, ld.dtype) for ld in landeds), grid_spec=grid_spec,
        compiler_params=_params("arbitrary"), name=name)(core, *args)


def adamw_shard(layer, w, m, v, part, landed, chip, earlier, name):
    n = landed.shape[0] // 3
    cols = w.shape[1]
    c1 = 1.0 / (1.0 - ADAM_B1 ** ADAM_STEP)
    c2 = 1.0 / (1.0 - ADAM_B2 ** ADAM_STEP)

    def body(q_ref, w_ref, m_ref, v_ref, p_ref, l0_ref, l1_ref, l2_ref, *rest):
        g_ref, d_ref, nm_ref, nv_ref = rest[-4:]
        gv = ((p_ref[...].astype(F32) + l0_ref[...].astype(F32)) + l1_ref[...].astype(F32)) + l2_ref[...].astype(F32)
        nm = ADAM_B1 * m_ref[...] + (1.0 - ADAM_B1) * gv
        nv = ADAM_B2 * v_ref[...] + (1.0 - ADAM_B2) * (gv * gv)
        g_ref[...] = gv
        nm_ref[...] = nm
        nv_ref[...] = nv
        d_ref[...] = -ADAM_LR * ((nm * c1) / (jnp.sqrt(nv * c2) + ADAM_EPS) + ADAM_WD * w_ref[...])

    own = pl.BlockSpec((n, cols), lambda i, q_ref: (layer, 0))

    def landed_spec(j):
        return pl.BlockSpec((n, cols), lambda i, q_ref: (j, 0))

    in_specs = [own, own, own, pl.BlockSpec((n, cols), lambda i, q_ref: (q_ref[0], 0)),
                landed_spec(0), landed_spec(1), landed_spec(2)]
    args = [chip, w, m, v, part, landed, landed, landed]
    aliases = {}
    if earlier is not None:
        in_specs += [ANY] * 4
        args += list(earlier)
        aliases = {8 + k: k for k in range(4)}
    grid_spec = pltpu.PrefetchScalarGridSpec(
        num_scalar_prefetch=1, grid=(1,), in_specs=in_specs, out_specs=(own, own, own, own))
    shape = jax.ShapeDtypeStruct(w.shape, F32)
    return pl.pallas_call(
        body, out_shape=(shape, shape, shape, shape), grid_spec=grid_spec, input_output_aliases=aliases,
        compiler_params=_params("arbitrary"), name=name)(*args)


def _pack(arrays):
    flat = jnp.concatenate([a.reshape(-1).astype(F32) for a in arrays])
    pad = (-flat.shape[0]) % (8 * LANES)
    return jnp.pad(flat, (0, pad)).reshape(-1, LANES)


def _unpack(slab, shapes):
    flat = slab.reshape(slab.shape[:-2] + (-1,))
    out, off = [], 0
    for shp in shapes:
        size = 1
        for s in shp:
            size *= s
        out.append(flat[..., off:off + size].reshape(flat.shape[:-1] + tuple(shp)))
        off += size
    return out


def kernel(x, mem, norm1_g, mem_norm_g, a_w_in, a_q_g, a_k_g, a_rel_bias, b_w_in, b_b_in, b_conv_w, b_conv_b, b_ln_g, b_ln_b, mq_g, mk_g, w_mem_kv, w_out, norm2_g, w_gate, w_up, w_down, loss_target, m_norm1_g, m_mem_norm_g, m_a_w_in, m_a_q_g, m_a_k_g, m_a_rel_bias, m_b_w_in, m_b_b_in, m_b_conv_w, m_b_conv_b, m_b_ln_g, m_b_ln_b, m_mq_g, m_mk_g, m_w_mem_kv, m_w_out, m_norm2_g, m_w_gate, m_w_up, m_w_down, v_norm1_g, v_mem_norm_g, v_a_w_in, v_a_q_g, v_a_k_g, v_a_rel_bias, v_b_w_in, v_b_b_in, v_b_conv_w, v_b_conv_b, v_b_ln_g, v_b_ln_b, v_mq_g, v_mk_g, v_w_mem_kv, v_w_out, v_norm2_g, v_w_gate, v_w_up, v_w_down):
    batch, seq, d = x.shape
    mtok = mem.shape[1]
    n = batch * seq
    ax, ay, ac = _place()
    me = 4 * ax + 2 * ay + ac
    core_arr = jnp.reshape(ac, (1,)).astype(jnp.int32)
    chip_arr = jnp.reshape(2 * ax + ay, (1,)).astype(jnp.int32)

    def t_bf16(w):
        return jnp.transpose(w).astype(BF16)

    def after(value, *earlier):
        return lax.optimization_barrier((value, *earlier))[0]

    def gather_mix(l, when, name, collective_id):
        srcs = [w_mem_kv[l].astype(BF16), w_out[l].astype(BF16)] + ([t_bf16(b_w_in[0])] if l == 1 else [])
        return gather_weights([after(srcs[0], when)] + srcs[1:], name, collective_id)

    def gather_ffn(l, when, name, collective_id):
        return gather_weights(
            [after(t_bf16(w_gate[l]), when), t_bf16(w_up[l]), w_down[l].astype(BF16)], name, collective_id)

    f_loc = b_b_in.shape[1]
    c_loc = b_conv_b.shape[1]

    def two(g):
        return jnp.concatenate([g, g], axis=-1)

    gq2, gk2 = two(a_q_g), two(a_k_g)
    rel16 = jnp.pad(a_rel_bias[0], ((0, 16 - a_rel_bias.shape[1]), (0, 0)))
    bias = bias_blocks(rel16)

    x0 = x.reshape(n, d)
    mem2 = mem.reshape(batch * mtok, d)
    zero_mem = jnp.zeros_like(mem2)

    saved = []
    xin = x0
    a_win_t, = gather_weights([t_bf16(a_w_in[0])], "gather_in_a", 1)
    wg_t, wu_t, wd, wo, wkv = [None] * 2, [None] * 2, [None] * 2, [None] * 2, [None] * 2
    h = rms_fwd(xin, norm1_g[0:1], name="rms1_fwd_0")
    target = loss_target.reshape(n, d)
    for l in range(2):
        mem_n = rms_fwd(mem2, mem_norm_g[l:l + 1], name=f"rms_mem_fwd_{l}")
        gq4 = jnp.tile(mq_g[l:l + 1], (1, 4))
        gk4 = jnp.tile(mk_g[l:l + 1], (1, 4))
        y_conv = None
        if l == 0:
            wkv[0], wo[0] = gather_mix(0, h, "gather_mix_a", 2)
            z = mm_nt(h, a_win_t, name="in_proj_a")
            wg_t[0], wu_t[0], wd[0] = gather_ffn(0, z, "gather_ffn_a", 3)
            cat = attn_fwd(z, gq2, gk2, bias, batch, seq)
            wkv[1], wo[1], b_win_t = gather_mix(1, cat, "gather_mix_b", 4)
            qcol = 3 * TOK_WIDTH // MEM_WIDTH
        else:
            small_shapes = [(f_loc,), (CONV_W, c_loc), (c_loc,), (c_loc,), (c_loc,)]
            gathered = small_exchange(after(_pack([b_b_in, b_conv_w, b_conv_b, b_ln_g, b_ln_b]), xin), reduce=False)
            bb_g, cw_g, cb_g, lg_g, lb_g = _unpack(gathered, small_shapes)
            bb_full = bb_g.reshape(1, -1)
            cw_full = jnp.pad(jnp.transpose(cw_g, (1, 0, 2)).reshape(CONV_W, -1), ((0, 32 - CONV_W), (0, 0)))
            cb_full, lg_full, lb_full = cb_g.reshape(1, -1), lg_g.reshape(1, -1), lb_g.reshape(1, -1)
            z = mm_nt(h, b_win_t, bias=bb_full, name="in_proj_b")
            cat, y_conv = conv_fwd(z, cw_full, cb_full, lg_full, lb_full, batch, seq)
            qcol = 2 * TOK_WIDTH // MEM_WIDTH
        kv = mm_nn(mem_n, wkv[l], name=f"mem_kv_{l}")
        cat = memattn_fwd(z, kv, gq4, gk4, cat, batch, seq, qcol, name=f"memattn_fwd_{l}")
        x1, h2 = proj_norm(cat, wo[l], xin, norm2_g[l:l + 1], name=f"out_proj_{l}")
        if l == 0:
            wg_t[1], wu_t[1], wd[1] = gather_ffn(1, x1, "gather_ffn_b", 5)
        if l == 0:
            gate, up, act, x2, h_next = ffn_fwd(h2, wg_t[0], wu_t[0], wd[0], x1, gain=norm1_g[1:2], name="ffn_fwd_0")
        else:
            gate, up, act, dx, dx_b, loss_blk = ffn_fwd(h2, wg_t[1], wu_t[1], wd[1], x1, target=target, name="ffn_fwd_1")
        saved.append(dict(xin=xin, h=h, mem_n=mem_n, kv=kv, gq4=gq4, gk4=gk4, z=z, qcol=qcol, cat=cat, x1=x1, h2=h2,
                          gate=gate, up=up, act=act, y_conv=y_conv))
        if l == 0:
            xin, h = x2, h_next

    big = {}
    small = {}
    reduced = {}
    groups = 0

    def scatter_siblings(keys):
        nonlocal groups
        gid = groups
        groups += 1
        return gid, keys, scatter_to_sibling([big[k] for k in keys], f"scatter_sibling_{gid}", 8 + 2 * gid)

    def scatter_chips(stage1, when):
        gid, keys, landed1 = stage1
        parts = add_sibling([after(big[keys[0]], when)] + [big[k] for k in keys[1:]], landed1, core_arr,
                            name=f"add_sibling_{gid}")
        landed2 = scatter_to_chips(parts, f"scatter_chips_{gid}", 9 + 2 * gid)
        for k, p, ld in zip(keys, parts, landed2):
            reduced[k] = (p, ld)
        return parts, landed2

    def rows_of(w, transposed):
        w = jnp.swapaxes(w, 1, 2) if transposed else w
        return w.reshape(w.shape[0] * w.shape[1], w.shape[2])

    sharded = {
        "win0": (2, True), "win1": (6, True), "wkv": (14, False), "wo": (15, False),
        "wg": (17, True), "wu": (18, True), "wd": (19, False)}
    weights = [norm1_g, mem_norm_g, a_w_in, a_q_g, a_k_g, a_rel_bias, b_w_in, b_b_in, b_conv_w, b_conv_b, b_ln_g,
               b_ln_b, mq_g, mk_g, w_mem_kv, w_out, norm2_g, w_gate, w_up, w_down]
    moms = [m_norm1_g, m_mem_norm_g, m_a_w_in, m_a_q_g, m_a_k_g, m_a_rel_bias, m_b_w_in, m_b_b_in, m_b_conv_w,
            m_b_conv_b, m_b_ln_g, m_b_ln_b, m_mq_g, m_mk_g, m_w_mem_kv, m_w_out, m_norm2_g, m_w_gate, m_w_up, m_w_down]
    vels = [v_norm1_g, v_mem_norm_g, v_a_w_in, v_a_q_g, v_a_k_g, v_a_rel_bias, v_b_w_in, v_b_b_in, v_b_conv_w,
            v_b_conv_b, v_b_ln_g, v_b_ln_b, v_mq_g, v_mk_g, v_w_mem_kv, v_w_out, v_norm2_g, v_w_gate, v_w_up, v_w_down]
    updated = {}

    def update_layer(l, when):
        for key, (idx, transposed) in sharded.items():
            if key in ("win0", "win1"):
                if key != f"win{l}":
                    continue
                layer, rkey = 0, key
            else:
                layer, rkey = l, f"{key}{l}"
            part, landed = reduced[rkey]
            updated[key] = adamw_shard(
                layer, after(rows_of(weights[idx], transposed), when), rows_of(moms[idx], transposed),
                rows_of(vels[idx], transposed), part, landed, chip_arr, updated.get(key), name=f"adamw_{rkey}")

    mix_landed = None
    for l in (1, 0):
        sv = saved[l]
        big[f"wd{l}"] = mm_tn(sv["act"], dx_b, name=f"grad_wd_{l}")
        dgate, dup, dx1, dx1_b, small[f"norm2_{l}"] = ffn_bwd(
            dx_b, wd[l], sv["gate"], sv["up"], wg_t[l], wu_t[l], sv["x1"], norm2_g[l:l + 1], dx, name=f"ffn_bwd_{l}")
        if l == 0:
            dgate = after(dgate, *mix_landed)
            update_layer(1, dx1)
        big[f"wg{l}"], big[f"wu{l}"] = mm_tn2(dgate, dup, sv["h2"], name=f"grad_wgu_{l}")
        big[f"wo{l}"] = mm_tn(sv["cat"], dx1_b, name=f"grad_wo_{l}")
        stage1 = scatter_siblings([f"wd{l}", f"wg{l}", f"wu{l}", f"wo{l}"])
        dcat = mm_nt(dx1_b, wo[l], name=f"out_proj_bwd_{l}")
        parts, ffn_landed = scatter_chips(stage1, dcat)
        dcat = after(dcat, *parts)
        if l == 0:
            dz, dbias, small["a_q"], small["a_k"] = attn_bwd(sv["z"], dcat, gq2, gk2, bias, batch, seq)
            small["rel"] = bias_grad(dbias)
            win_t = a_win_t
        else:
            dz, small["cw"], small["csum"] = conv_bwd(sv["z"], sv["y_conv"], dcat, cw_full, lg_full, lb_full, batch, seq)
            win_t = b_win_t
        dz = after(dz, *ffn_landed)
        dz, dkv, small[f"mq_{l}"], small[f"mk_{l}"] = memattn_bwd(
            sv["z"], sv["kv"], dcat, sv["gq4"], sv["gk4"], dz, batch, seq, sv["qcol"], name=f"memattn_bwd_{l}")
        if l == 1:
            small["bb"] = col_sum(dz, name="grad_b_in")
        big[f"win{l}"] = mm_tn(dz, sv["h"], name=f"grad_win_{l}")
        big[f"wkv{l}"] = mm_tn(sv["mem_n"], dkv, name=f"grad_wkv_{l}")
        stage1 = scatter_siblings([f"win{l}", f"wkv{l}"])
        dx, dx_b, small[f"norm1_{l}"] = proj_rms_bwd(
            [(dz, win_t)], sv["xin"], norm1_g[l:l + 1], dx1, name=f"in_proj_bwd_{l}")
        parts, mix_landed = scatter_chips(stage1, dx)
        dx_b = after(dx_b, *parts)
        dmem_n = mm_nt(dkv, wkv[l], out_dtype=F32, name=f"mem_kv_bwd_{l}")
        _, _, small[f"memnorm_{l}"] = rms_bwd(dmem_n, mem2, mem_norm_g[l:l + 1], zero_mem, name=f"rms_mem_bwd_{l}")
    grad_x = dx.reshape(batch, seq, d)
    update_layer(0, dx)

    def shaped(rows, idx, transposed):
        shp = weights[idx].shape
        if transposed:
            return jnp.swapaxes(rows.reshape(shp[0], shp[2], shp[1]), 1, 2)
        return rows.reshape(shp)

    def fold(v, groups):
        return jnp.sum(v.reshape(groups, HEAD_DIM), axis=0, keepdims=True)

    heads = a_rel_bias.shape[1]
    small_list = [
        jnp.concatenate([small["norm1_0"], small["norm1_1"]]),
        jnp.concatenate([small["memnorm_0"], small["memnorm_1"]]),
        fold(small["a_q"], 2), fold(small["a_k"], 2), small["rel"][:heads][None],
        small["bb"], small["cw"][:CONV_W][None], small["csum"][0:1], small["csum"][1:2], small["csum"][2:3],
        jnp.concatenate([fold(small["mq_0"], 4), fold(small["mq_1"], 4)]),
        jnp.concatenate([fold(small["mk_0"], 4), fold(small["mk_1"], 4)]),
        jnp.concatenate([small["norm2_0"], small["norm2_1"]]),
    ]
    (g_norm1, g_memnorm, g_aq, g_ak, g_rel, g_bb_full, g_cw_full, g_cb_full, g_lg_full, g_lb_full,
     g_mq, g_mk, g_norm2, loss_sum) = reduce_small(small_list + [loss_blk])
    loss = loss_sum[0, 0]
    g_bb = lax.dynamic_slice_in_dim(g_bb_full, me * f_loc, f_loc, axis=1)
    g_cw = lax.dynamic_slice_in_dim(g_cw_full, me * c_loc, c_loc, axis=2)
    g_cb = lax.dynamic_slice_in_dim(g_cb_full, me * c_loc, c_loc, axis=1)
    g_lg = lax.dynamic_slice_in_dim(g_lg_full, me * c_loc, c_loc, axis=1)
    g_lb = lax.dynamic_slice_in_dim(g_lb_full, me * c_loc, c_loc, axis=1)

    grads = [g_norm1, g_memnorm, None, g_aq, g_ak, g_rel, None, g_bb, g_cw, g_cb, g_lg, g_lb,
             g_mq, g_mk, None, None, g_norm2, None, None, None]
    deltas, new_m, new_v = [None] * 20, [None] * 20, [None] * 20
    for key, (idx, transposed) in sharded.items():
        grads[idx], deltas[idx], new_m[idx], new_v[idx] = (shaped(r, idx, transposed) for r in updated[key])

    small_idx = [i for i in range(20) if i not in {idx for idx, _ in sharded.values()}]
    dl, nm, nv = adamw_small([weights[i] for i in small_idx], [grads[i] for i in small_idx],
                             [moms[i] for i in small_idx], [vels[i] for i in small_idx])
    for i, a, b, cc in zip(small_idx, dl, nm, nv):
        deltas[i], new_m[i], new_v[i] = a, b, cc

    return (loss, grad_x, *grads, *deltas, *new_m, *new_v)
```

```python
import functools

import jax
import jax.numpy as jnp
from jax import lax
from jax.experimental import pallas as pl
from jax.experimental.pallas import tpu as pltpu
from jax.experimental.pallas import tpu_sc as plsc

F32 = jnp.float32
BF16 = jnp.bfloat16
HIGHEST = lax.Precision.HIGHEST
MESH = pl.DeviceIdType.MESH
ANY = pl.BlockSpec(memory_space=pl.ANY)

N_DEV = 8
D_MODEL = 1024
HEAD_DIM = 64
TOK_WIDTH = 768
MEM_WIDTH = 256
CHUNK = 64
Q_BLOCK = 256
KEY_WIN = 768
BAND = 576
N_REL = 192
CONV_W = 31
CONV_HALO = 32
NORM_EPS = 1e-6
NEG_INF = -1e30
ATTN_SCALE = HEAD_DIM ** -0.5
LANES = 128
ROW_TILE = 512
VMEM_LIMIT = 56 * 1024 * 1024

ADAM_LR, ADAM_B1, ADAM_B2, ADAM_EPS, ADAM_WD, ADAM_STEP = 0.001, 0.9, 0.999, 1e-08, 0.01, 10


def _params(*sem):
    return pltpu.CompilerParams(dimension_semantics=sem, vmem_limit_bytes=VMEM_LIMIT)


def _row_tile(m):
    return ROW_TILE if m % ROW_TILE == 0 else m


def _col_tile(n, cap=1408):
    best = None
    for t in range(LANES, min(n, cap) + 1, LANES):
        if n % t == 0:
            best = t
    return best if best is not None else n


def _dot(a, b, ca, cb):
    return lax.dot_general(a, b, (((ca,), (cb,)), ((), ())), preferred_element_type=F32)


def _sigmoid(x):
    return 0.5 * jnp.tanh(0.5 * x) + 0.5


def mm_nt(a, b, bias=None, out_dtype=BF16, name="mm_nt"):
    m, k = a.shape
    n = b.shape[0]
    tm, tn = _row_tile(m), _col_tile(n)

    def body(*refs):
        a_ref, b_ref = refs[0], refs[1]
        o_ref = refs[-1]
        acc = _dot(a_ref[...].astype(BF16), b_ref[...].astype(BF16), 1, 1)
        if bias is not None:
            acc = acc + refs[2][...]
        o_ref[...] = acc.astype(o_ref.dtype)

    in_specs = [pl.BlockSpec((tm, k), lambda j, i: (i, 0)), pl.BlockSpec((tn, k), lambda j, i: (j, 0))]
    args = [a, b]
    if bias is not None:
        in_specs.append(pl.BlockSpec((1, tn), lambda j, i: (0, j)))
        args.append(bias)
    return pl.pallas_call(
        body, out_shape=jax.ShapeDtypeStruct((m, n), out_dtype), grid=(n // tn, m // tm),
        in_specs=in_specs, out_specs=pl.BlockSpec((tm, tn), lambda j, i: (i, j)),
        compiler_params=_params("parallel", "arbitrary"), name=name)(*args)


def mm_nn(a, b, res=None, out_dtype=F32, name="mm_nn"):
    m, k = a.shape
    n = b.shape[1]
    tm, tn = _row_tile(m), _col_tile(n, 1024)

    def body(*refs):
        a_ref, b_ref = refs[0], refs[1]
        o_ref = refs[-1]
        acc = _dot(a_ref[...].astype(BF16), b_ref[...].astype(BF16), 1, 0)
        if res is not None:
            acc = acc + refs[2][...]
        o_ref[...] = acc.astype(o_ref.dtype)

    in_specs = [pl.BlockSpec((tm, k), lambda j, i: (i, 0)), pl.BlockSpec((k, tn), lambda j, i: (0, j))]
    args = [a, b]
    if res is not None:
        in_specs.append(pl.BlockSpec((tm, tn), lambda j, i: (i, j)))
        args.append(res)
    return pl.pallas_call(
        body, out_shape=jax.ShapeDtypeStruct((m, n), out_dtype), grid=(n // tn, m // tm),
        in_specs=in_specs, out_specs=pl.BlockSpec((tm, tn), lambda j, i: (i, j)),
        compiler_params=_params("parallel", "arbitrary"), name=name)(*args)


def mm2_nn(a1, b1, a2, b2, name="mm2_nn"):
    m, k = a1.shape
    n = b1.shape[1]
    tm = _row_tile(m)

    def body(a1_ref, b1_ref, a2_ref, b2_ref, o_ref):
        o_ref[...] = _dot(a1_ref[...], b1_ref[...], 1, 0) + _dot(a2_ref[...], b2_ref[...], 1, 0)

    a_spec = pl.BlockSpec((tm, k), lambda i: (i, 0))
    b_spec = pl.BlockSpec((k, n), lambda i: (0, 0))
    return pl.pallas_call(
        body, out_shape=jax.ShapeDtypeStruct((m, n), F32), grid=(m // tm,),
        in_specs=[a_spec, b_spec, a_spec, b_spec], out_specs=pl.BlockSpec((tm, n), lambda i: (i, 0)),
        compiler_params=_params("parallel"), name=name)(a1, b1, a2, b2)


def mm_tn(a, b, out_dtype=BF16, name="mm_tn"):
    t, r = a.shape
    c = b.shape[1]
    tr = _col_tile(r, 512)

    def body(a_ref, b_ref, o_ref):
        o_ref[...] = _dot(a_ref[...].astype(BF16), b_ref[...].astype(BF16), 0, 0).astype(o_ref.dtype)

    return pl.pallas_call(
        body, out_shape=jax.ShapeDtypeStruct((r, c), out_dtype), grid=(r // tr,),
        in_specs=[pl.BlockSpec((t, tr), lambda i: (0, i)), pl.BlockSpec((t, c), lambda i: (0, 0))],
        out_specs=pl.BlockSpec((tr, c), lambda i: (i, 0)),
        compiler_params=_params("parallel"), name=name)(a, b)


def _resident(shape):
    return pl.BlockSpec(shape, lambda i: (0, 0), pipeline_mode=pl.Buffered(1))


def proj_norm(a, b, res, gain, name):
    m, k = a.shape
    n = b.shape[1]
    tm = _row_tile(m)

    def body(a_ref, b_ref, res_ref, g_ref, x_ref, h_ref):
        xv = res_ref[...] + _dot(a_ref[...], b_ref[...], 1, 0)
        x_ref[...] = xv
        r = lax.rsqrt(jnp.mean(xv * xv, axis=-1, keepdims=True) + NORM_EPS)
        h_ref[...] = (xv * r * g_ref[...]).astype(BF16)

    row = pl.BlockSpec((tm, n), lambda i: (i, 0))
    return pl.pallas_call(
        body, out_shape=(jax.ShapeDtypeStruct((m, n), F32), jax.ShapeDtypeStruct((m, n), BF16)), grid=(m // tm,),
        in_specs=[pl.BlockSpec((tm, k), lambda i: (i, 0)), _resident((k, n)), row, _resident((1, n))],
        out_specs=(row, row), compiler_params=_params("parallel"), name=name)(a, b, res, gain)


def proj_loss(a, b, res, target, name):
    m, k = a.shape
    n = b.shape[1]
    tm = _row_tile(m)
    nt = m // tm

    def body(a_ref, b_ref, res_ref, t_ref, dy_ref, dyb_ref, l_ref, acc_ref):
        i = pl.program_id(0)

        @pl.when(i == 0)
        def _():
            acc_ref[...] = jnp.zeros_like(acc_ref)

        err = res_ref[...] + _dot(a_ref[...], b_ref[...], 1, 0) - t_ref[...]
        dy = err * (1.0 / n)
        dy_ref[...] = dy
        dyb_ref[...] = dy.astype(BF16)
        acc_ref[...] += jnp.sum(err * err, axis=0, keepdims=True)

        @pl.when(i == nt - 1)
        def _():
            total = jnp.sum(acc_ref[...], axis=-1, keepdims=True) * (0.5 / n)
            l_ref[...] = jnp.broadcast_to(total, l_ref.shape)

    row = pl.BlockSpec((tm, n), lambda i: (i, 0))
    return pl.pallas_call(
        body, out_shape=(jax.ShapeDtypeStruct((m, n), F32), jax.ShapeDtypeStruct((m, n), BF16),
                         jax.ShapeDtypeStruct((8, LANES), F32)), grid=(nt,),
        in_specs=[pl.BlockSpec((tm, k), lambda i: (i, 0)), _resident((k, n)), row, row],
        out_specs=(row, row, pl.BlockSpec((8, LANES), lambda i: (0, 0))),
        scratch_shapes=[pltpu.VMEM((1, n), F32)],
        compiler_params=_params("arbitrary"), name=name)(a, b, res, target)


def in_proj_bwd(dz, w_t, x, gain, dres, out_dtype, name):
    m, n = x.shape
    k = dz.shape[1]
    tm = _row_tile(m)

    def body(dz_ref, w_ref, x_ref, g_ref, dres_ref, dx_ref, dg_ref, cs_ref):
        @pl.when(pl.program_id(0) == 0)
        def _():
            dg_ref[...] = jnp.zeros_like(dg_ref)
            cs_ref[...] = jnp.zeros_like(cs_ref)

        dzv = dz_ref[...]
        cs_ref[...] += jnp.sum(dzv.astype(F32), axis=0, keepdims=True)
        dhv = _dot(dzv, w_ref[...], 1, 0)
        xv = x_ref[...]
        r = lax.rsqrt(jnp.mean(xv * xv, axis=-1, keepdims=True) + NORM_EPS)
        xhat = xv * r
        dg_ref[...] += jnp.sum(dhv * xhat, axis=0, keepdims=True)
        dxhat = dhv * g_ref[...]
        dx = dres_ref[...].astype(F32) + r * (dxhat - xhat * jnp.mean(dxhat * xhat, axis=-1, keepdims=True))
        dx_ref[...] = dx.astype(dx_ref.dtype)

    row = pl.BlockSpec((tm, n), lambda i: (i, 0))
    return pl.pallas_call(
        body, out_shape=(jax.ShapeDtypeStruct((m, n), out_dtype), jax.ShapeDtypeStruct((1, n), F32),
                         jax.ShapeDtypeStruct((1, k), F32)), grid=(m // tm,),
        in_specs=[pl.BlockSpec((tm, k), lambda i: (i, 0)), _resident(w_t.shape), row, _resident((1, n)), row],
        out_specs=(row, pl.BlockSpec((1, n), lambda i: (0, 0)), pl.BlockSpec((1, k), lambda i: (0, 0))),
        compiler_params=_params("arbitrary"), name=name)(dz, w_t, x, gain, dres)


FFN_ROWS = 256


def _ffn_row_tile(m):
    return FFN_ROWS if m % FFN_ROWS == 0 else m


def ffn_fwd(h2, wg_t, wu_t, wd, x1, gain=None, target=None, name="ffn_fwd"):
    n, d = h2.shape
    f = wg_t.shape[0]
    tm = _ffn_row_tile(n)
    nt = n // tm
    last = target is not None

    def body(h_ref, wg_ref, wu_ref, wd_ref, x1_ref, e_ref, g_ref, u_ref, a_ref, *rest):
        hv = h_ref[...]
        gv = _dot(hv, wg_ref[...], 1, 1)
        uv = _dot(hv, wu_ref[...], 1, 1)
        g_ref[...] = gv.astype(BF16)
        u_ref[...] = uv.astype(BF16)
        av = (gv * _sigmoid(gv) * uv).astype(BF16)
        a_ref[...] = av
        xv = x1_ref[...] + _dot(av, wd_ref[...], 1, 0)
        if not last:
            x_ref, hn_ref = rest
            x_ref[...] = xv
            r = lax.rsqrt(jnp.mean(xv * xv, axis=-1, keepdims=True) + NORM_EPS)
            hn_ref[...] = (xv * r * e_ref[...]).astype(BF16)
        else:
            dyb_ref, l_ref, acc_ref = rest
            i = pl.program_id(0)

            @pl.when(i == 0)
            def _():
                acc_ref[...] = jnp.zeros_like(acc_ref)

            err = xv - e_ref[...]
            dyb_ref[...] = (err * (1.0 / d)).astype(BF16)
            acc_ref[...] += jnp.sum(err * err, axis=0, keepdims=True)

            @pl.when(i == nt - 1)
            def _():
                total = jnp.sum(acc_ref[...], axis=-1, keepdims=True) * (0.5 / d)
                l_ref[...] = jnp.broadcast_to(total, l_ref.shape)

    row_d = pl.BlockSpec((tm, d), lambda i: (i, 0))
    row_f = pl.BlockSpec((tm, f), lambda i: (i, 0))
    act_shape = jax.ShapeDtypeStruct((n, f), BF16)
    if not last:
        extra_in, extra = _resident((1, d)), gain
        out_shape = (act_shape, act_shape, act_shape, jax.ShapeDtypeStruct((n, d), F32), jax.ShapeDtypeStruct((n, d), BF16))
        out_specs = (row_f, row_f, row_f, row_d, row_d)
        scratch = []
    else:
        extra_in, extra = row_d, target
        out_shape = (act_shape, act_shape, act_shape, jax.ShapeDtypeStruct((n, d), BF16),
                     jax.ShapeDtypeStruct((8, LANES), F32))
        out_specs = (row_f, row_f, row_f, row_d, pl.BlockSpec((8, LANES), lambda i: (0, 0)))
        scratch = [pltpu.VMEM((1, d), F32)]
    return pl.pallas_call(
        body, out_shape=out_shape, grid=(nt,),
        in_specs=[row_d, _resident((f, d)), _resident((f, d)), _resident((f, d)), row_d, extra_in],
        out_specs=out_specs, scratch_shapes=scratch,
        compiler_params=_params("arbitrary"), name=name)(h2, wg_t, wu_t, wd, x1, extra)


def ffn_bwd(dx_b, wd, gate, up, wg_t, wu_t, x1, gain, name="ffn_bwd"):
    n, d = x1.shape
    f = wd.shape[0]
    tm = _ffn_row_tile(n)

    def body(dxb_ref, wd_ref, g_ref, u_ref, wg_ref, wu_ref, x_ref, gain_ref, dg_ref, du_ref, dxo_ref, dgain_ref):
        @pl.when(pl.program_id(0) == 0)
        def _():
            dgain_ref[...] = jnp.zeros_like(dgain_ref)

        dact = _dot(dxb_ref[...], wd_ref[...], 1, 1)
        gv = g_ref[...].astype(F32)
        uv = u_ref[...].astype(F32)
        sg = _sigmoid(gv)
        dgv = (dact * uv * sg * (1.0 + gv * (1.0 - sg))).astype(BF16)
        duv = (dact * gv * sg).astype(BF16)
        dg_ref[...] = dgv
        du_ref[...] = duv
        dhv = _dot(dgv, wg_ref[...], 1, 0) + _dot(duv, wu_ref[...], 1, 0)
        xv = x_ref[...]
        r = lax.rsqrt(jnp.mean(xv * xv, axis=-1, keepdims=True) + NORM_EPS)
        xhat = xv * r
        dgain_ref[...] += jnp.sum(dhv * xhat, axis=0, keepdims=True)
        dxhat = dhv * gain_ref[...]
        dx = dxb_ref[...].astype(F32) + r * (dxhat - xhat * jnp.mean(dxhat * xhat, axis=-1, keepdims=True))
        dxo_ref[...] = dx.astype(BF16)

    row_d = pl.BlockSpec((tm, d), lambda i: (i, 0))
    row_f = pl.BlockSpec((tm, f), lambda i: (i, 0))
    w_spec = _resident((f, d))
    act_shape = jax.ShapeDtypeStruct((n, f), BF16)
    return pl.pallas_call(
        body, out_shape=(act_shape, act_shape, jax.ShapeDtypeStruct((n, d), BF16), jax.ShapeDtypeStruct((1, d), F32)),
        grid=(n // tm,),
        in_specs=[row_d, w_spec, row_f, row_f, w_spec, w_spec, row_d, _resident((1, d))],
        out_specs=(row_f, row_f, row_d, pl.BlockSpec((1, d), lambda i: (0, 0))),
        compiler_params=_params("arbitrary"), name=name)(dx_b, wd, gate, up, wg_t, wu_t, x1, gain)


def ffn_weight_grads(dgate, dup, h2, act, dx_b, name="ffn_weight_grads"):
    t, r = dgate.shape
    c = h2.shape[1]
    tr = _col_tile(r, 512)

    def body(a1_ref, a2_ref, a3_ref, b12_ref, b3_ref, o1_ref, o2_ref, o3_ref):
        bv = b12_ref[...]
        o1_ref[...] = _dot(a1_ref[...], bv, 0, 0).astype(o1_ref.dtype)
        o2_ref[...] = _dot(a2_ref[...], bv, 0, 0).astype(o2_ref.dtype)
        o3_ref[...] = _dot(a3_ref[...], b3_ref[...], 0, 0).astype(o3_ref.dtype)

    a_spec = pl.BlockSpec((t, tr), lambda i: (0, i))
    o_spec = pl.BlockSpec((tr, c), lambda i: (i, 0))
    shape = jax.ShapeDtypeStruct((r, c), BF16)
    return pl.pallas_call(
        body, out_shape=(shape, shape, shape), grid=(r // tr,),
        in_specs=[a_spec, a_spec, a_spec, _resident((t, c)), _resident((t, c))],
        out_specs=(o_spec, o_spec, o_spec), compiler_params=_params("parallel"), name=name)(dgate, dup, act, h2, dx_b)


def rms_fwd(x, g, name="rms_fwd"):
    n, d = x.shape
    tm = _row_tile(n)

    def body(x_ref, g_ref, o_ref):
        xv = x_ref[...]
        r = lax.rsqrt(jnp.mean(xv * xv, axis=-1, keepdims=True) + NORM_EPS)
        o_ref[...] = (xv * r * g_ref[...]).astype(o_ref.dtype)

    return pl.pallas_call(
        body, out_shape=jax.ShapeDtypeStruct((n, d), BF16), grid=(n // tm,),
        in_specs=[pl.BlockSpec((tm, d), lambda i: (i, 0)), pl.BlockSpec((1, d), lambda i: (0, 0))],
        out_specs=pl.BlockSpec((tm, d), lambda i: (i, 0)),
        compiler_params=_params("parallel"), name=name)(x, g)


def rms_bwd(dh, x, g, dres, name="rms_bwd"):
    n, d = x.shape
    tm = _row_tile(n)

    def body(dh_ref, x_ref, g_ref, dres_ref, dx_ref, dxb_ref, dg_ref):
        @pl.when(pl.program_id(0) == 0)
        def _():
            dg_ref[...] = jnp.zeros_like(dg_ref)

        xv = x_ref[...]
        dhv = dh_ref[...].astype(F32)
        r = lax.rsqrt(jnp.mean(xv * xv, axis=-1, keepdims=True) + NORM_EPS)
        xhat = xv * r
        dg_ref[...] += jnp.sum(dhv * xhat, axis=0, keepdims=True)
        dxhat = dhv * g_ref[...]
        mean_t = jnp.mean(dxhat * xhat, axis=-1, keepdims=True)
        dx = dres_ref[...] + r * (dxhat - xhat * mean_t)
        dx_ref[...] = dx
        dxb_ref[...] = dx.astype(BF16)

    row = pl.BlockSpec((tm, d), lambda i: (i, 0))
    vec = pl.BlockSpec((1, d), lambda i: (0, 0))
    return pl.pallas_call(
        body, out_shape=(jax.ShapeDtypeStruct((n, d), F32), jax.ShapeDtypeStruct((n, d), BF16),
                         jax.ShapeDtypeStruct((1, d), F32)), grid=(n // tm,),
        in_specs=[row, row, vec, row], out_specs=(row, row, vec),
        compiler_params=_params("arbitrary"), name=name)(dh, x, g, dres)


def gate_up(h2, wg_t, wu_t, name="gate_up"):
    n, d = h2.shape
    f = wg_t.shape[0]
    tm, tn = _row_tile(n), _col_tile(f)

    def body(h_ref, wg_ref, wu_ref, g_ref, u_ref, a_ref):
        hv = h_ref[...]
        gv = _dot(hv, wg_ref[...], 1, 1)
        uv = _dot(hv, wu_ref[...], 1, 1)
        g_ref[...] = gv.astype(BF16)
        u_ref[...] = uv.astype(BF16)
        a_ref[...] = (gv * _sigmoid(gv) * uv).astype(BF16)

    w_spec = pl.BlockSpec((tn, d), lambda j, i: (j, 0))
    o_spec = pl.BlockSpec((tm, tn), lambda j, i: (i, j))
    o_shape = jax.ShapeDtypeStruct((n, f), BF16)
    return pl.pallas_call(
        body, out_shape=(o_shape, o_shape, o_shape), grid=(f // tn, n // tm),
        in_specs=[pl.BlockSpec((tm, d), lambda j, i: (i, 0)), w_spec, w_spec], out_specs=(o_spec, o_spec, o_spec),
        compiler_params=_params("parallel", "arbitrary"), name=name)(h2, wg_t, wu_t)


def ffn_bwd_act(dx, wd, gate, up, name="ffn_bwd_act"):
    n, d = dx.shape
    f = wd.shape[0]
    tm, tn = _row_tile(n), _col_tile(f)

    def body(dx_ref, wd_ref, g_ref, u_ref, dg_ref, du_ref):
        dact = _dot(dx_ref[...].astype(BF16), wd_ref[...], 1, 1)
        gv = g_ref[...].astype(F32)
        uv = u_ref[...].astype(F32)
        sg = _sigmoid(gv)
        dg_ref[...] = (dact * uv * sg * (1.0 + gv * (1.0 - sg))).astype(BF16)
        du_ref[...] = (dact * gv * sg).astype(BF16)

    t_spec = pl.BlockSpec((tm, tn), lambda j, i: (i, j))
    o_shape = jax.ShapeDtypeStruct((n, f), BF16)
    return pl.pallas_call(
        body, out_shape=(o_shape, o_shape), grid=(f // tn, n // tm),
        in_specs=[pl.BlockSpec((tm, d), lambda j, i: (i, 0)), pl.BlockSpec((tn, d), lambda j, i: (j, 0)), t_spec, t_spec],
        out_specs=(t_spec, t_spec),
        compiler_params=_params("parallel", "arbitrary"), name=name)(dx, wd, gate, up)


def _group_masks(width):
    lane = lax.broadcasted_iota(jnp.int32, (1, width), 1)
    return [(lane >= HEAD_DIM * g) & (lane < HEAD_DIM * (g + 1)) for g in range(width // HEAD_DIM)]


def _group_sum(x, masks):
    out = jnp.zeros_like(x)
    for msk in masks:
        s = jnp.sum(jnp.where(msk, x, 0.0), axis=-1, keepdims=True)
        out = jnp.where(msk, s, out)
    return out


def _head_norm(x, gain, masks):
    r = lax.rsqrt(_group_sum(x * x, masks) * (1.0 / HEAD_DIM) + NORM_EPS)
    xhat = x * r
    return xhat * gain, xhat, r


def _head_norm_bwd(dxn, xhat, r, gain, masks):
    dgain = jnp.sum(dxn * xhat, axis=0, keepdims=True)
    dxhat = dxn * gain
    mean_t = _group_sum(dxhat * xhat, masks) * (1.0 / HEAD_DIM)
    return r * (dxhat - xhat * mean_t), dgain


def _softmax_rows(s):
    e = jnp.exp(s - jnp.max(s, axis=-1, keepdims=True))
    return e * (1.0 / jnp.sum(e, axis=-1, keepdims=True))


def _rel_onehot():
    col = lax.broadcasted_iota(jnp.int32, (1, KEY_WIN), 1)
    off = jnp.where(col < KEY_WIN - LANES, col, col - KEY_WIN)
    idx = jnp.clip(8 * CHUNK - off, -(CHUNK - 1), LANES) + (CHUNK - 1)
    return (lax.broadcasted_iota(jnp.int32, (N_REL, KEY_WIN), 0) == idx).astype(F32)


def bias_blocks(rel16):
    heads = TOK_WIDTH // HEAD_DIM

    def body(rel_ref, o_ref, u_ref):
        u_ref[...] = jnp.dot(rel_ref[...], _rel_onehot(), precision=HIGHEST, preferred_element_type=F32)
        row = lax.broadcasted_iota(jnp.int32, (CHUNK, KEY_WIN), 0)
        col = lax.broadcasted_iota(jnp.int32, (CHUNK, KEY_WIN), 1)
        for h in range(heads):
            xv = jnp.broadcast_to(u_ref[h:h + 1, :], (CHUNK, KEY_WIN))
            for b in range(6):
                xv = jnp.where(((row >> b) & 1) == 1, pltpu.roll(xv, 1 << b, axis=1), xv)
            xv = jnp.where(col < BAND, xv, NEG_INF)
            for i in range(Q_BLOCK // CHUNK):
                o_ref[h, CHUNK * i:CHUNK * (i + 1), :] = pltpu.roll(xv, CHUNK * i, axis=1) if i else xv

    return pl.pallas_call(
        body, out_shape=jax.ShapeDtypeStruct((heads, Q_BLOCK, KEY_WIN), F32),
        scratch_shapes=[pltpu.VMEM((16, KEY_WIN), F32)], name="bias_blocks")(rel16)


def bias_grad(dbias):
    heads = dbias.shape[0]

    def body(db_ref, o_ref, y_ref):
        y_ref[...] = jnp.zeros_like(y_ref)
        row = lax.broadcasted_iota(jnp.int32, (CHUNK, KEY_WIN), 0)
        for h in range(heads):
            fv = db_ref[h, 0:CHUNK, :]
            for i in range(1, Q_BLOCK // CHUNK):
                fv = fv + pltpu.roll(db_ref[h, CHUNK * i:CHUNK * (i + 1), :], KEY_WIN - CHUNK * i, axis=1)
            for b in range(6):
                fv = jnp.where(((row >> b) & 1) == 1, pltpu.roll(fv, KEY_WIN - (1 << b), axis=1), fv)
            y_ref[h:h + 1, :] = jnp.sum(fv, axis=0, keepdims=True)
        o_ref[...] = lax.dot_general(y_ref[...], _rel_onehot(), (((1,), (1,)), ((), ())),
                                     precision=HIGHEST, preferred_element_type=F32)

    return pl.pallas_call(
        body, out_shape=jax.ShapeDtypeStruct((16, N_REL), F32),
        scratch_shapes=[pltpu.VMEM((16, KEY_WIN), F32)], name="bias_grad")(dbias)


def _attn_windows(seq):
    out = []
    for j in range(seq // Q_BLOCK):
        r0 = j * Q_BLOCK
        k0 = max(0, r0 - 8 * CHUNK)
        width = r0 + Q_BLOCK - k0
        out.append((r0, k0, width, KEY_WIN - width))
    return out


def attn_fwd(z, gq2, gk2, bias, batch, seq):
    n = z.shape[0]
    pairs = TOK_WIDTH // LANES

    def body(q_ref, k_ref, v_ref, gq_ref, gk_ref, b_ref, o_ref, qs_s, kn_s):
        masks = _group_masks(LANES)
        qs_s[...] = (_head_norm(q_ref[...].astype(F32), gq_ref[...], masks)[0] * ATTN_SCALE).astype(BF16)
        kn_s[...] = _head_norm(k_ref[...].astype(F32), gk_ref[...], masks)[0].astype(BF16)
        for r0, k0, width, c0 in _attn_windows(seq):
            qb = qs_s[r0:r0 + Q_BLOCK, :]
            kw = kn_s[k0:k0 + width, :]
            vw = v_ref[k0:k0 + width, :]
            out = jnp.zeros((Q_BLOCK, LANES), F32)
            for h, msk in enumerate(masks):
                qh = jnp.where(msk, qb, jnp.zeros_like(qb))
                s = _dot(qh, kw, 1, 1) + b_ref[h, :, c0:KEY_WIN]
                p = _softmax_rows(s).astype(BF16)
                out = jnp.where(msk, _dot(p, vw, 1, 0), out)
            o_ref[r0:r0 + Q_BLOCK, :] = out.astype(o_ref.dtype)

    def col(off):
        return pl.BlockSpec((seq, LANES), lambda b, p: (b, off + p))

    vec = pl.BlockSpec((1, LANES), lambda b, p: (0, 0))
    return pl.pallas_call(
        body, out_shape=jax.ShapeDtypeStruct((n, D_MODEL), BF16), grid=(batch, pairs),
        in_specs=[col(0), col(pairs), col(2 * pairs), vec, vec,
                  pl.BlockSpec((2, Q_BLOCK, KEY_WIN), lambda b, p: (p, 0, 0))],
        out_specs=pl.BlockSpec((seq, LANES), lambda b, p: (b, p)),
        scratch_shapes=[pltpu.VMEM((seq, LANES), BF16), pltpu.VMEM((seq, LANES), BF16)],
        compiler_params=_params("parallel", "arbitrary"), name="attn_fwd")(z, z, z, gq2, gk2, bias)


def attn_bwd(z, dcat, gq2, gk2, bias, batch, seq):
    n = z.shape[0]
    pairs = TOK_WIDTH // LANES

    def body(q_ref, k_ref, v_ref, do_ref, gq_ref, gk_ref, b_ref,
             dz_ref, db_ref, dgq_ref, dgk_ref, qs_s, kn_s, dqn_s, dkn_s, dv_s, dk_o, dv_o):
        pi, bi, which = pl.program_id(0), pl.program_id(1), pl.program_id(2)

        @pl.when(which == 0)
        def _():
            masks = _group_masks(LANES)

            @pl.when(bi == 0)
            def _():
                db_ref[...] = jnp.zeros_like(db_ref)

            @pl.when((bi == 0) & (pi == 0))
            def _():
                dgq_ref[...] = jnp.zeros_like(dgq_ref)
                dgk_ref[...] = jnp.zeros_like(dgk_ref)

            qn, qhat, rq = _head_norm(q_ref[...].astype(F32), gq_ref[...], masks)
            kn, khat, rk = _head_norm(k_ref[...].astype(F32), gk_ref[...], masks)
            qs_s[...] = (qn * ATTN_SCALE).astype(BF16)
            kn_s[...] = kn.astype(BF16)
            dkn_s[...] = jnp.zeros_like(dkn_s)
            dv_s[...] = jnp.zeros_like(dv_s)
            for r0, k0, width, c0 in _attn_windows(seq):
                qb = qs_s[r0:r0 + Q_BLOCK, :]
                dob = do_ref[r0:r0 + Q_BLOCK, :]
                kw = kn_s[k0:k0 + width, :]
                vw = v_ref[k0:k0 + width, :]
                dq_acc = jnp.zeros((Q_BLOCK, LANES), F32)
                dk_acc = jnp.zeros((width, LANES), F32)
                dv_acc = jnp.zeros((width, LANES), F32)
                for h, msk in enumerate(masks):
                    qh = jnp.where(msk, qb, jnp.zeros_like(qb))
                    doh = jnp.where(msk, dob, jnp.zeros_like(dob))
                    p = _softmax_rows(_dot(qh, kw, 1, 1) + b_ref[h, :, c0:KEY_WIN])
                    dp = _dot(doh, vw, 1, 1)
                    ds = p * (dp - jnp.sum(p * dp, axis=-1, keepdims=True))
                    db_ref[h, :, c0:KEY_WIN] += ds
                    dsb = ds.astype(BF16)
                    dq_acc = jnp.where(msk, _dot(dsb, kw, 1, 0), dq_acc)
                    dk_acc = jnp.where(msk, _dot(dsb, qb, 0, 0), dk_acc)
                    dv_acc = jnp.where(msk, _dot(p.astype(BF16), dob, 0, 0), dv_acc)
                dqn_s[r0:r0 + Q_BLOCK, :] = dq_acc * ATTN_SCALE
                dkn_s[k0:k0 + width, :] += dk_acc
                dv_s[k0:k0 + width, :] += dv_acc
            dq, dgq = _head_norm_bwd(dqn_s[...], qhat, rq, gq_ref[...], masks)
            dk, dgk = _head_norm_bwd(dkn_s[...], khat, rk, gk_ref[...], masks)
            dz_ref[...] = dq.astype(dz_ref.dtype)
            dk_o[...] = dk.astype(dk_o.dtype)
            dv_o[...] = dv_s[...].astype(dv_o.dtype)
            dgq_ref[...] += dgq
            dgk_ref[...] += dgk

        @pl.when(which == 1)
        def _():
            dz_ref[...] = dk_o[...]

        @pl.when(which == 2)
        def _():
            dz_ref[...] = dv_o[...]

    def ahead(p, b, t):
        nb = b + jnp.where(t > 0, 1, 0)
        wrap = jnp.where(nb >= batch, 1, 0)
        return jnp.minimum(p + wrap, pairs - 1), nb - wrap * batch

    def col(off):
        def index(p, b, t):
            np_, nb = ahead(p, b, t)
            return nb, off + np_
        return pl.BlockSpec((seq, LANES), index)

    vec = pl.BlockSpec((1, LANES), lambda p, b, t: (0, 0))
    blk = pl.BlockSpec((2, Q_BLOCK, KEY_WIN), lambda p, b, t: (p, 0, 0))
    blk_in = pl.BlockSpec((2, Q_BLOCK, KEY_WIN), lambda p, b, t: (ahead(p, b, t)[0], 0, 0))
    v_shape = jax.ShapeDtypeStruct((1, LANES), F32)
    return pl.pallas_call(
        body,
        out_shape=(jax.ShapeDtypeStruct(z.shape, BF16), jax.ShapeDtypeStruct(bias.shape, F32), v_shape, v_shape),
        grid=(pairs, batch, 3),
        in_specs=[col(0), col(pairs), col(2 * pairs), col(0), vec, vec, blk_in],
        out_specs=(pl.BlockSpec((seq, LANES), lambda p, b, t: (b, t * pairs + p)), blk, vec, vec),
        scratch_shapes=[pltpu.VMEM((seq, LANES), BF16), pltpu.VMEM((seq, LANES), BF16),
                        pltpu.VMEM((seq, LANES), F32), pltpu.VMEM((seq, LANES), F32), pltpu.VMEM((seq, LANES), F32),
                        pltpu.VMEM((seq, LANES), BF16), pltpu.VMEM((seq, LANES), BF16)],
        compiler_params=_params("arbitrary", "arbitrary", "arbitrary"), name="attn_bwd")(
            z, z, z, dcat, gq2, gk2, bias)


MEM_ROWS = 512


def memattn_fwd(z, kv, gq4, gk4, cat, batch, seq, qcol, name):
    mtok = kv.shape[0] // batch
    rows = min(MEM_ROWS, seq)

    def body(q_ref, kv_ref, gq_ref, gk_ref, cat_ref, o_ref):
        del cat_ref
        masks = _group_masks(MEM_WIDTH)
        kn = _head_norm(kv_ref[:, 0:MEM_WIDTH], gk_ref[...], masks)[0].astype(BF16)
        vm = kv_ref[:, MEM_WIDTH:2 * MEM_WIDTH].astype(BF16)
        for t in range(seq // rows):
            sl = slice(t * rows, (t + 1) * rows)
            qs = (_head_norm(q_ref[sl, :].astype(F32), gq_ref[...], masks)[0] * ATTN_SCALE).astype(BF16)
            out = jnp.zeros((rows, MEM_WIDTH), F32)
            for msk in masks:
                qh = jnp.where(msk, qs, jnp.zeros_like(qs))
                p = _softmax_rows(_dot(qh, kn, 1, 1)).astype(BF16)
                out = jnp.where(msk, _dot(p, vm, 1, 0), out)
            o_ref[sl, :] = out.astype(o_ref.dtype)

    vec = pl.BlockSpec((1, MEM_WIDTH), lambda b: (0, 0))
    return pl.pallas_call(
        body, out_shape=jax.ShapeDtypeStruct(cat.shape, cat.dtype), grid=(batch,),
        in_specs=[pl.BlockSpec((seq, MEM_WIDTH), lambda b: (b, qcol)),
                  pl.BlockSpec((mtok, 2 * MEM_WIDTH), lambda b: (b, 0)), vec, vec, ANY],
        out_specs=pl.BlockSpec((seq, MEM_WIDTH), lambda b: (b, TOK_WIDTH // MEM_WIDTH)),
        input_output_aliases={4: 0},
        compiler_params=_params("parallel"), name=name)(z, kv, gq4, gk4, cat)


def memattn_bwd(z, kv, dcat, gq4, gk4, dz, batch, seq, qcol, name):
    mtok = kv.shape[0] // batch
    rows = min(MEM_ROWS, seq)

    def body(q_ref, kv_ref, do_ref, gq_ref, gk_ref, dz_in_ref, dq_ref, dkv_ref, dgq_ref, dgk_ref):
        del dz_in_ref
        @pl.when(pl.program_id(0) == 0)
        def _():
            dgq_ref[...] = jnp.zeros_like(dgq_ref)
            dgk_ref[...] = jnp.zeros_like(dgk_ref)

        masks = _group_masks(MEM_WIDTH)
        kn_f, khat, rk = _head_norm(kv_ref[:, 0:MEM_WIDTH], gk_ref[...], masks)
        kn = kn_f.astype(BF16)
        vm = kv_ref[:, MEM_WIDTH:2 * MEM_WIDTH].astype(BF16)
        dkn = jnp.zeros((mtok, MEM_WIDTH), F32)
        dvm = jnp.zeros((mtok, MEM_WIDTH), F32)
        dgq = jnp.zeros((1, MEM_WIDTH), F32)
        for t in range(seq // rows):
            sl = slice(t * rows, (t + 1) * rows)
            qn_f, qhat, rq = _head_norm(q_ref[sl, :].astype(F32), gq_ref[...], masks)
            qs = (qn_f * ATTN_SCALE).astype(BF16)
            dob = do_ref[sl, :]
            dqn = jnp.zeros((rows, MEM_WIDTH), F32)
            for msk in masks:
                qh = jnp.where(msk, qs, jnp.zeros_like(qs))
                doh = jnp.where(msk, dob, jnp.zeros_like(dob))
                p = _softmax_rows(_dot(qh, kn, 1, 1))
                dp = _dot(doh, vm, 1, 1)
                ds = p * (dp - jnp.sum(p * dp, axis=-1, keepdims=True))
                dsb = ds.astype(BF16)
                dqn = jnp.where(msk, _dot(dsb, kn, 1, 0), dqn)
                dkn = dkn + jnp.where(msk, _dot(dsb, qs, 0, 0), 0.0)
                dvm = dvm + jnp.where(msk, _dot(p.astype(BF16), dob, 0, 0), 0.0)
            dq, dg = _head_norm_bwd(dqn * ATTN_SCALE, qhat, rq, gq_ref[...], masks)
            dq_ref[sl, :] = dq.astype(dq_ref.dtype)
            dgq = dgq + dg
        dk, dgk = _head_norm_bwd(dkn, khat, rk, gk_ref[...], masks)
        dkv_ref[:, 0:MEM_WIDTH] = dk
        dkv_ref[:, MEM_WIDTH:2 * MEM_WIDTH] = dvm
        dgq_ref[...] += dgq
        dgk_ref[...] += dgk

    vec = pl.BlockSpec((1, MEM_WIDTH), lambda b: (0, 0))
    kv_spec = pl.BlockSpec((mtok, 2 * MEM_WIDTH), lambda b: (b, 0))
    v_shape = jax.ShapeDtypeStruct((1, MEM_WIDTH), F32)
    q_spec = pl.BlockSpec((seq, MEM_WIDTH), lambda b: (b, qcol))
    return pl.pallas_call(
        body,
        out_shape=(jax.ShapeDtypeStruct(dz.shape, dz.dtype), jax.ShapeDtypeStruct(kv.shape, F32), v_shape, v_shape),
        grid=(batch,),
        in_specs=[q_spec, kv_spec, pl.BlockSpec((seq, MEM_WIDTH), lambda b: (b, TOK_WIDTH // MEM_WIDTH)), vec, vec, ANY],
        out_specs=(q_spec, kv_spec, vec, vec),
        input_output_aliases={5: 0},
        compiler_params=_params("arbitrary"), name=name)(z, kv, dcat, gq4, gk4, dz)


CONV_ROWS = 256


def _glu(a_ref, g_ref):
    return a_ref[...].astype(F32) * _sigmoid(g_ref[...].astype(F32))


def _layer_norm_stats(y):
    mu = jnp.mean(y, axis=-1, keepdims=True)
    yc = y - mu
    rstd = lax.rsqrt(jnp.mean(yc * yc, axis=-1, keepdims=True) + NORM_EPS)
    return yc * rstd, rstd


CONV_WIN = CONV_HALO + CONV_ROWS
SUBLANES = 8
SHIFT_ROWS = CONV_WIN - SUBLANES


def _preshift(win, shifted):
    for s in range(1, SUBLANES):
        shifted[s - 1, :, :] = win[s:s + SHIFT_ROWS, :]


TAP_ROWS = 64
TAP_TILES = [(r0, slice(c0, c0 + LANES)) for c0 in range(0, TOK_WIDTH, LANES) for r0 in range(0, CONV_ROWS, TAP_ROWS)]


def _tap(win, shifted, off, r0, lanes):
    s = off % SUBLANES
    base = off - s + r0
    if s == 0:
        return win[base:base + TAP_ROWS, lanes]
    return shifted[s - 1, base:base + TAP_ROWS, lanes]


def _fold_rows(x):
    return jnp.sum(x.reshape(TAP_ROWS // SUBLANES, SUBLANES, LANES), axis=0)


def conv_fwd(z, cw, cb, lg, lb, batch, seq):
    n = z.shape[0]
    nt = seq // CONV_ROWS
    sub = CONV_ROWS // CONV_HALO
    lead = CONV_HALO - (CONV_W - 1)

    def body(a_ref, g_ref, ap_ref, gp_ref, cw_ref, cb_ref, lg_ref, lb_ref, o_ref, y_ref, win, shifted):
        first = pl.program_id(1) == 0
        win[0:CONV_HALO, :] = jnp.where(first, 0.0, _glu(ap_ref, gp_ref))
        win[CONV_HALO:CONV_WIN, :] = _glu(a_ref, g_ref)
        _preshift(win, shifted)
        for r0, lanes in TAP_TILES:
            acc = jnp.zeros((TAP_ROWS, LANES), F32) + cb_ref[:, lanes]
            for w in range(CONV_W):
                acc = acc + _tap(win, shifted, lead + w, r0, lanes) * cw_ref[w:w + 1, lanes]
            y_ref[r0:r0 + TAP_ROWS, lanes] = acc
        yh, _ = _layer_norm_stats(y_ref[...])
        t = yh * lg_ref[...] + lb_ref[...]
        o_ref[...] = (t * _sigmoid(t)).astype(o_ref.dtype)

    def cur(c):
        return pl.BlockSpec((CONV_ROWS, TOK_WIDTH), lambda b, i: (b * nt + i, c))

    def prev(c):
        return pl.BlockSpec((CONV_HALO, TOK_WIDTH), lambda b, i: (jnp.maximum((b * nt + i) * sub - 1, 0), c))

    vec = pl.BlockSpec((1, TOK_WIDTH), lambda b, i: (0, 0))
    return pl.pallas_call(
        body, out_shape=(jax.ShapeDtypeStruct((n, D_MODEL), BF16), jax.ShapeDtypeStruct((n, TOK_WIDTH), F32)),
        grid=(batch, nt),
        in_specs=[cur(0), cur(1), prev(0), prev(1), pl.BlockSpec((32, TOK_WIDTH), lambda b, i: (0, 0)), vec, vec, vec],
        out_specs=(cur(0), cur(0)),
        scratch_shapes=[pltpu.VMEM((CONV_WIN, TOK_WIDTH), F32), pltpu.VMEM((SUBLANES - 1, SHIFT_ROWS, TOK_WIDTH), F32)],
        compiler_params=_params("parallel", "arbitrary"), name="conv_fwd")(z, z, z, z, cw, cb, lg, lb)


def conv_bwd(z, y, dcat, cw, lg, lb, batch, seq):
    n = z.shape[0]
    nt = seq // CONV_ROWS
    sub = CONV_ROWS // CONV_HALO
    lead = CONV_HALO - (CONV_W - 1)
    last_blk = n // CONV_HALO - 1

    def body(a_ref, g_ref, ap_ref, gp_ref, y_ref, yn_ref, do_ref, don_ref, cw_ref, lg_ref, lb_ref,
             dz_ref, dcw_ref, dsm_ref, win, shifted, dyw, dshifted, dg_o):
        b, i, which = pl.program_id(0), pl.program_id(1), pl.program_id(2)

        @pl.when(which == 0)
        def _():
            first, last = i == 0, i == nt - 1

            @pl.when((b == 0) & (i == 0))
            def _():
                dcw_ref[...] = jnp.zeros_like(dcw_ref)
                dsm_ref[...] = jnp.zeros_like(dsm_ref)

            win[0:CONV_HALO, :] = jnp.where(first, 0.0, _glu(ap_ref, gp_ref))
            win[CONV_HALO:CONV_WIN, :] = _glu(a_ref, g_ref)
            _preshift(win, shifted)
            yv = jnp.concatenate([y_ref[...], yn_ref[...]], axis=0)
            yh, rstd = _layer_norm_stats(yv)
            t = yh * lg_ref[...] + lb_ref[...]
            st = _sigmoid(t)
            dout = jnp.concatenate(
                [do_ref[...].astype(F32), jnp.where(last, 0.0, don_ref[...].astype(F32))], axis=0)
            dt = dout * st * (1.0 + t * (1.0 - st))
            dyh = dt * lg_ref[...]
            dy = rstd * (dyh - jnp.mean(dyh, axis=-1, keepdims=True)
                         - yh * jnp.mean(dyh * yh, axis=-1, keepdims=True))
            dyw[...] = dy
            _preshift(dyw, dshifted)
            dsm_ref[0:1, :] += jnp.sum(dy[0:CONV_ROWS], axis=0, keepdims=True)
            dsm_ref[1:2, :] += jnp.sum((dt * yh)[0:CONV_ROWS], axis=0, keepdims=True)
            dsm_ref[2:3, :] += jnp.sum(dt[0:CONV_ROWS], axis=0, keepdims=True)
            for c0 in range(0, TOK_WIDTH, LANES):
                lanes = slice(c0, c0 + LANES)
                dcw_acc = [jnp.zeros((SUBLANES, LANES), F32) for _ in range(CONV_W)]
                for r0 in range(0, CONV_ROWS, TAP_ROWS):
                    dyt = dyw[r0:r0 + TAP_ROWS, lanes]
                    dglu = jnp.zeros((TAP_ROWS, LANES), F32)
                    for w in range(CONV_W):
                        dcw_acc[w] = dcw_acc[w] + _fold_rows(dyt * _tap(win, shifted, lead + w, r0, lanes))
                        dglu = dglu + _tap(dyw, dshifted, CONV_W - 1 - w, r0, lanes) * cw_ref[w:w + 1, lanes]
                    avt = a_ref[r0:r0 + TAP_ROWS, lanes].astype(F32)
                    sgt = _sigmoid(g_ref[r0:r0 + TAP_ROWS, lanes].astype(F32))
                    dz_ref[r0:r0 + TAP_ROWS, lanes] = (dglu * sgt).astype(dz_ref.dtype)
                    dg_o[r0:r0 + TAP_ROWS, lanes] = (dglu * avt * sgt * (1.0 - sgt)).astype(dg_o.dtype)
                for w in range(CONV_W):
                    dcw_ref[w:w + 1, lanes] += jnp.sum(dcw_acc[w], axis=0, keepdims=True)

        @pl.when(which == 1)
        def _():
            dz_ref[...] = dg_o[...]

    def ahead(b, i, t):
        return jnp.minimum(b * nt + i + t, batch * nt - 1)

    def cur(c):
        return pl.BlockSpec((CONV_ROWS, TOK_WIDTH), lambda b, i, t: (ahead(b, i, t), c))

    def prev(c):
        return pl.BlockSpec((CONV_HALO, TOK_WIDTH), lambda b, i, t: (jnp.maximum(ahead(b, i, t) * sub - 1, 0), c))

    nxt = pl.BlockSpec((CONV_HALO, TOK_WIDTH),
                       lambda b, i, t: (jnp.minimum((ahead(b, i, t) + 1) * sub, last_blk), 0))
    vec = pl.BlockSpec((1, TOK_WIDTH), lambda b, i, t: (0, 0))
    full32 = pl.BlockSpec((32, TOK_WIDTH), lambda b, i, t: (0, 0))
    return pl.pallas_call(
        body,
        out_shape=(jax.ShapeDtypeStruct(z.shape, BF16), jax.ShapeDtypeStruct((32, TOK_WIDTH), F32),
                   jax.ShapeDtypeStruct((8, TOK_WIDTH), F32)),
        grid=(batch, nt, 2),
        in_specs=[cur(0), cur(1), prev(0), prev(1), cur(0), nxt, cur(0), nxt, full32, vec, vec],
        out_specs=(pl.BlockSpec((CONV_ROWS, TOK_WIDTH), lambda b, i, t: (b * nt + i, t)), full32,
                   pl.BlockSpec((8, TOK_WIDTH), lambda b, i, t: (0, 0))),
        scratch_shapes=[pltpu.VMEM((CONV_WIN, TOK_WIDTH), F32), pltpu.VMEM((SUBLANES - 1, SHIFT_ROWS, TOK_WIDTH), F32),
                        pltpu.VMEM((CONV_WIN, TOK_WIDTH), F32), pltpu.VMEM((SUBLANES - 1, SHIFT_ROWS, TOK_WIDTH), F32),
                        pltpu.VMEM((CONV_ROWS, TOK_WIDTH), BF16)],
        compiler_params=_params("arbitrary", "arbitrary", "arbitrary"), name="conv_bwd")(
            z, z, z, z, y, y, dcat, dcat, cw, lg, lb)


def loss_head(y, target):
    n, d = y.shape
    tm = _row_tile(n)
    nt = n // tm

    def body(y_ref, t_ref, dy_ref, dyb_ref, l_ref, acc_ref):
        i = pl.program_id(0)

        @pl.when(i == 0)
        def _():
            acc_ref[...] = jnp.zeros_like(acc_ref)

        err = y_ref[...] - t_ref[...]
        dy = err * (1.0 / d)
        dy_ref[...] = dy
        dyb_ref[...] = dy.astype(BF16)
        acc_ref[...] += jnp.sum(err * err, axis=0, keepdims=True)

        @pl.when(i == nt - 1)
        def _():
            total = jnp.sum(acc_ref[...], axis=-1, keepdims=True) * (0.5 / d)
            l_ref[...] = jnp.broadcast_to(total, l_ref.shape)

    row = pl.BlockSpec((tm, d), lambda i: (i, 0))
    return pl.pallas_call(
        body, out_shape=(jax.ShapeDtypeStruct((n, d), F32), jax.ShapeDtypeStruct((n, d), BF16),
                         jax.ShapeDtypeStruct((8, LANES), F32)), grid=(nt,),
        in_specs=[row, row], out_specs=(row, row, pl.BlockSpec((8, LANES), lambda i: (0, 0))),
        scratch_shapes=[pltpu.VMEM((1, d), F32)],
        compiler_params=_params("arbitrary"), name="loss_head")(y, target)


def col_sum(x, name="col_sum"):
    n, c = x.shape
    tm = _row_tile(n)

    def body(x_ref, o_ref):
        @pl.when(pl.program_id(0) == 0)
        def _():
            o_ref[...] = jnp.zeros_like(o_ref)

        o_ref[...] += jnp.sum(x_ref[...].astype(F32), axis=0, keepdims=True)

    return pl.pallas_call(
        body, out_shape=jax.ShapeDtypeStruct((1, c), F32), grid=(n // tm,),
        in_specs=[pl.BlockSpec((tm, c), lambda i: (i, 0))], out_specs=pl.BlockSpec((1, c), lambda i: (0, 0)),
        compiler_params=_params("arbitrary"), name=name)(x)


def adamw(w, g, m, v, name="adamw"):
    rows, cols = w.shape
    tr = rows
    for cand in (512, 256, 128, 64, 32, 16, 8):
        if rows % cand == 0 and rows > cand:
            tr = cand
            break
    c1 = 1.0 / (1.0 - ADAM_B1 ** ADAM_STEP)
    c2 = 1.0 / (1.0 - ADAM_B2 ** ADAM_STEP)

    def body(w_ref, g_ref, m_ref, v_ref, d_ref, nm_ref, nv_ref):
        gv = g_ref[...]
        nm = ADAM_B1 * m_ref[...] + (1.0 - ADAM_B1) * gv
        nv = ADAM_B2 * v_ref[...] + (1.0 - ADAM_B2) * (gv * gv)
        nm_ref[...] = nm
        nv_ref[...] = nv
        d_ref[...] = -ADAM_LR * ((nm * c1) / (jnp.sqrt(nv * c2) + ADAM_EPS) + ADAM_WD * w_ref[...])

    spec = pl.BlockSpec((tr, cols), lambda i: (i, 0))
    shape = jax.ShapeDtypeStruct((rows, cols), F32)
    return pl.pallas_call(
        body, out_shape=(shape, shape, shape), grid=(rows // tr,),
        in_specs=[spec, spec, spec, spec], out_specs=(spec, spec, spec),
        compiler_params=_params("parallel"), name=name)(w, g, m, v)


def _place():
    return lax.axis_index("x"), lax.axis_index("y"), lax.axis_index("c")


def _other_chips(x, y):
    return [(1 - x, y), (x, 1 - y), (1 - x, 1 - y)]


def small_exchange(slab, reduce):
    r = slab.shape[0]

    def body(in_ref, o_ref, *scratch):
        if reduce:
            buf, send_sems, recv_sems = scratch
        else:
            buf = o_ref
            send_sems, recv_sems = scratch
        x, y, c = _place()
        me = 4 * x + 2 * y + c
        buf[me] = in_ref[...]
        copies = []
        for k in range(1, N_DEV):
            peer = (x ^ (k >> 2), y ^ ((k >> 1) & 1), c ^ (k & 1))
            cp = pltpu.make_async_remote_copy(
                src_ref=in_ref, dst_ref=buf.at[me], send_sem=send_sems.at[k - 1], recv_sem=recv_sems.at[k - 1],
                device_id=peer, device_id_type=MESH)
            cp.start()
            copies.append(cp)
        for k in range(1, N_DEV):
            src = 4 * (x ^ (k >> 2)) + 2 * (y ^ ((k >> 1) & 1)) + (c ^ (k & 1))
            pltpu.make_async_remote_copy(
                src_ref=in_ref, dst_ref=buf.at[src], send_sem=send_sems.at[k - 1], recv_sem=recv_sems.at[k - 1],
                device_id=(x, y, c), device_id_type=MESH).wait_recv()
        for cp in copies:
            cp.wait_send()
        if reduce:
            total = buf[0]
            for d in range(1, N_DEV):
                total = total + buf[d]
            o_ref[...] = total

    sems = [pltpu.SemaphoreType.DMA((N_DEV - 1,)), pltpu.SemaphoreType.DMA((N_DEV - 1,))]
    if reduce:
        out_shape = jax.ShapeDtypeStruct((r, LANES), F32)
        scratch = [pltpu.VMEM((N_DEV, r, LANES), F32)] + sems
    else:
        out_shape = jax.ShapeDtypeStruct((N_DEV, r, LANES), F32)
        scratch = sems
    vmem = pl.BlockSpec(memory_space=pltpu.VMEM)
    return pl.pallas_call(
        body, out_shape=out_shape, in_specs=[vmem], out_specs=vmem, scratch_shapes=scratch,
        compiler_params=pltpu.CompilerParams(vmem_limit_bytes=VMEM_LIMIT),
        name="small_reduce" if reduce else "small_gather")(slab)


def reduce_small(arrays):
    na = len(arrays)

    def body(*refs):
        ins, outs, bufs = refs[:na], refs[na:2 * na], refs[2 * na:3 * na]
        send_sems, recv_sems = refs[3 * na:]
        x, y, c = _place()
        me = 4 * x + 2 * y + c
        copies = []
        for a in range(na):
            bufs[a][me] = ins[a][...]
            for k in range(1, N_DEV):
                cp = pltpu.make_async_remote_copy(
                    src_ref=ins[a], dst_ref=bufs[a].at[me], send_sem=send_sems.at[a, k - 1],
                    recv_sem=recv_sems.at[a, k - 1],
                    device_id=(x ^ (k >> 2), y ^ ((k >> 1) & 1), c ^ (k & 1)), device_id_type=MESH)
                cp.start()
                copies.append(cp)
        for a in range(na):
            for k in range(1, N_DEV):
                src = 4 * (x ^ (k >> 2)) + 2 * (y ^ ((k >> 1) & 1)) + (c ^ (k & 1))
                pltpu.make_async_remote_copy(
                    src_ref=ins[a], dst_ref=bufs[a].at[src], send_sem=send_sems.at[a, k - 1],
                    recv_sem=recv_sems.at[a, k - 1], device_id=(x, y, c), device_id_type=MESH).wait_recv()
        for cp in copies:
            cp.wait_send()
        for a in range(na):
            total = bufs[a][0]
            for dev in range(1, N_DEV):
                total = total + bufs[a][dev]
            outs[a][...] = total

    vmem = pl.BlockSpec(memory_space=pltpu.VMEM)
    return pl.pallas_call(
        body, out_shape=tuple(jax.ShapeDtypeStruct(a.shape, F32) for a in arrays),
        in_specs=[vmem] * na, out_specs=tuple([vmem] * na),
        scratch_shapes=[pltpu.VMEM((N_DEV,) + a.shape, F32) for a in arrays]
        + [pltpu.SemaphoreType.DMA((na, N_DEV - 1)), pltpu.SemaphoreType.DMA((na, N_DEV - 1))],
        compiler_params=pltpu.CompilerParams(vmem_limit_bytes=VMEM_LIMIT), name="small_reduce")(*arrays)


def adamw_small(ws, gs, ms, vs):
    na = len(ws)
    c1 = 1.0 / (1.0 - ADAM_B1 ** ADAM_STEP)
    c2 = 1.0 / (1.0 - ADAM_B2 ** ADAM_STEP)

    def body(*refs):
        w_refs, g_refs, m_refs, v_refs = (refs[i * na:(i + 1) * na] for i in range(4))
        d_refs, nm_refs, nv_refs = (refs[(4 + i) * na:(5 + i) * na] for i in range(3))
        for a in range(na):
            gv = g_refs[a][...]
            nm = ADAM_B1 * m_refs[a][...] + (1.0 - ADAM_B1) * gv
            nv = ADAM_B2 * v_refs[a][...] + (1.0 - ADAM_B2) * (gv * gv)
            nm_refs[a][...] = nm
            nv_refs[a][...] = nv
            d_refs[a][...] = -ADAM_LR * ((nm * c1) / (jnp.sqrt(nv * c2) + ADAM_EPS) + ADAM_WD * w_refs[a][...])

    vmem = pl.BlockSpec(memory_space=pltpu.VMEM)
    shapes = tuple(jax.ShapeDtypeStruct(w.shape, F32) for w in ws)
    outs = pl.pallas_call(
        body, out_shape=shapes * 3, in_specs=[vmem] * (4 * na), out_specs=tuple([vmem] * (3 * na)),
        compiler_params=pltpu.CompilerParams(vmem_limit_bytes=VMEM_LIMIT), name="adamw_small")(*ws, *gs, *ms, *vs)
    return outs[:na], outs[na:2 * na], outs[2 * na:]


def gather_weights(shards, name, collective_id):
    nw = len(shards)
    ns = [s.shape[0] for s in shards]
    in_refs = [jax.new_ref(s, memory_space=pltpu.MemorySpace.HBM) for s in shards]
    out_refs = [jax.empty_ref(jax.ShapeDtypeStruct((N_DEV * s.shape[0], s.shape[1]), s.dtype),
                              memory_space=pltpu.MemorySpace.HBM) for s in shards]

    @pl.kernel(mesh=plsc.ScalarSubcoreMesh(axis_name="seq", num_cores=1), name=name,
               scratch_types=(pltpu.SemaphoreType.DMA((nw, 7)), pltpu.SemaphoreType.DMA((nw, 7)),
                              pltpu.SemaphoreType.DMA((nw,))),
               compiler_params=pltpu.CompilerParams(collective_id=collective_id))
    def launch(send_sems, recv_sems, local_sems):
        x, y, c = _place()
        me, sib = (x, y, c), (x, y, 1 - c)
        chips = _other_chips(x, y)
        barrier = pltpu.get_barrier_semaphore()
        for peer in [sib] + [(*chip, c) for chip in chips]:
            pl.semaphore_signal(barrier, inc=1, device_id=peer, device_id_type=MESH)
        pl.semaphore_wait(barrier, 4)

        def rows(w, dev):
            return out_refs[w].at[pl.ds((4 * dev[0] + 2 * dev[1] + dev[2]) * ns[w], ns[w]), :]

        def copy(w, k, block, to, src=None):
            return pltpu.make_async_remote_copy(
                src_ref=rows(w, block) if src is None else src, dst_ref=rows(w, block),
                send_sem=send_sems.at[w, k], recv_sem=recv_sems.at[w, k], device_id=to, device_id_type=MESH)

        started, sends = [], []
        for w in range(nw):
            mine = pltpu.make_async_copy(in_refs[w], rows(w, me), local_sems.at[w])
            mine.start()
            started.append(mine)
            first = [copy(w, 0, me, sib, src=in_refs[w])]
            first += [copy(w, 1 + j, me, (*chip, c), src=in_refs[w]) for j, chip in enumerate(chips)]
            for cp in first:
                cp.start()
            sends += first
        for w in range(nw):
            for j, chip in enumerate(chips):
                copy(w, 1 + j, (*chip, c), me).wait_recv()
                fwd = copy(w, 4 + j, (*chip, c), sib)
                fwd.start()
                sends.append(fwd)
        for w in range(nw):
            copy(w, 0, sib, me).wait_recv()
            for j, chip in enumerate(chips):
                copy(w, 4 + j, (*chip, 1 - c), me).wait_recv()
        for cp in sends:
            cp.wait_send()
        for mine in started:
            mine.wait()

    launch()
    return [r[...] for r in out_refs]


def _sequencer_exchange(sources, out_rows, peers_of, copies_of, name, collective_id):
    nw = len(sources)
    in_refs = [jax.new_ref(s, memory_space=pltpu.MemorySpace.HBM) for s in sources]
    out_refs = [jax.empty_ref(jax.ShapeDtypeStruct((rows, s.shape[1]), s.dtype), memory_space=pltpu.MemorySpace.HBM)
                for rows, s in zip(out_rows, sources)]
    per = len(copies_of(0, 0, 0, 0))

    @pl.kernel(mesh=plsc.ScalarSubcoreMesh(axis_name="seq", num_cores=1), name=name,
               scratch_types=(pltpu.SemaphoreType.DMA((nw, per)), pltpu.SemaphoreType.DMA((nw, per))),
               compiler_params=pltpu.CompilerParams(collective_id=collective_id))
    def launch(send_sems, recv_sems):
        x, y, c = _place()
        peers = peers_of(x, y, c)
        barrier = pltpu.get_barrier_semaphore()
        for peer in peers:
            pl.semaphore_signal(barrier, inc=1, device_id=peer, device_id_type=MESH)
        pl.semaphore_wait(barrier, len(peers))
        copies = []
        for w in range(nw):
            for k, (src_blk, dst_blk, rows, peer) in enumerate(copies_of(x, y, c, w)):
                cp = pltpu.make_async_remote_copy(
                    src_ref=in_refs[w].at[pl.ds(src_blk * rows, rows), :],
                    dst_ref=out_refs[w].at[pl.ds(dst_blk * rows, rows), :],
                    send_sem=send_sems.at[w, k], recv_sem=recv_sems.at[w, k], device_id=peer, device_id_type=MESH)
                cp.start()
                copies.append(cp)
        for cp in copies:
            cp.wait_recv()
        for cp in copies:
            cp.wait_send()

    launch()
    return [r[...] for r in out_refs]


def scatter_to_sibling(grads, name, collective_id):
    ns = [g.shape[0] // N_DEV for g in grads]
    return _sequencer_exchange(
        grads, [4 * n for n in ns],
        lambda x, y, c: [(x, y, 1 - c)],
        lambda x, y, c, w: [(2 * q + 1 - c, q, ns[w], (x, y, 1 - c)) for q in range(4)],
        name, collective_id)


def scatter_to_chips(parts, name, collective_id):
    ns = [p.shape[0] // 4 for p in parts]
    return _sequencer_exchange(
        parts, [3 * n for n in ns],
        lambda x, y, c: [(*chip, c) for chip in _other_chips(x, y)],
        lambda x, y, c, w: [(2 * chip[0] + chip[1], j, ns[w], (*chip, c)) for j, chip in enumerate(_other_chips(x, y))],
        name, collective_id)


def add_sibling(grads, landeds, core, name):
    nw = len(grads)

    def body(c_ref, *refs):
        for w in range(nw):
            g_ref, l_ref, o_ref = refs[2 * w], refs[2 * w + 1], refs[2 * nw + w]
            o_ref[...] = (g_ref[...].astype(F32) + l_ref[...].astype(F32)).astype(o_ref.dtype)

    in_specs, out_specs, args = [], [], []
    for g, ld in zip(grads, landeds):
        n, cols = ld.shape[0] // 4, g.shape[1]
        in_specs += [pl.BlockSpec((n, cols), lambda q, c_ref: (2 * q + c_ref[0], 0)),
                     pl.BlockSpec((n, cols), lambda q, c_ref: (q, 0))]
        out_specs.append(pl.BlockSpec((n, cols), lambda q, c_ref: (q, 0)))
        args += [g, ld]
    grid_spec = pltpu.PrefetchScalarGridSpec(
        num_scalar_prefetch=1, grid=(4,), in_specs=in_specs, out_specs=tuple(out_specs))
    return pl.pallas_call(
        body, out_shape=tuple(jax.ShapeDtypeStruct(ld.shape, ld.dtype) for ld in landeds), grid_spec=grid_spec,
        compiler_params=_params("arbitrary"), name=name)(core, *args)


def adamw_shard(layer, w, m, v, part, landed, chip, earlier, name):
    n = landed.shape[0] // 3
    cols = w.shape[1]
    c1 = 1.0 / (1.0 - ADAM_B1 ** ADAM_STEP)
    c2 = 1.0 / (1.0 - ADAM_B2 ** ADAM_STEP)

    def body(q_ref, w_ref, m_ref, v_ref, p_ref, l0_ref, l1_ref, l2_ref, *rest):
        g_ref, d_ref, nm_ref, nv_ref = rest[-4:]
        gv = ((p_ref[...].astype(F32) + l0_ref[...].astype(F32)) + l1_ref[...].astype(F32)) + l2_ref[...].astype(F32)
        nm = ADAM_B1 * m_ref[...] + (1.0 - ADAM_B1) * gv
        nv = ADAM_B2 * v_ref[...] + (1.0 - ADAM_B2) * (gv * gv)
        g_ref[...] = gv
        nm_ref[...] = nm
        nv_ref[...] = nv
        d_ref[...] = -ADAM_LR * ((nm * c1) / (jnp.sqrt(nv * c2) + ADAM_EPS) + ADAM_WD * w_ref[...])

    own = pl.BlockSpec((n, cols), lambda i, q_ref: (layer, 0))

    def landed_spec(j):
        return pl.BlockSpec((n, cols), lambda i, q_ref: (j, 0))

    in_specs = [own, own, own, pl.BlockSpec((n, cols), lambda i, q_ref: (q_ref[0], 0)),
                landed_spec(0), landed_spec(1), landed_spec(2)]
    args = [chip, w, m, v, part, landed, landed, landed]
    aliases = {}
    if earlier is not None:
        in_specs += [ANY] * 4
        args += list(earlier)
        aliases = {8 + k: k for k in range(4)}
    grid_spec = pltpu.PrefetchScalarGridSpec(
        num_scalar_prefetch=1, grid=(1,), in_specs=in_specs, out_specs=(own, own, own, own))
    shape = jax.ShapeDtypeStruct(w.shape, F32)
    return pl.pallas_call(
        body, out_shape=(shape, shape, shape, shape), grid_spec=grid_spec, input_output_aliases=aliases,
        compiler_params=_params("arbitrary"), name=name)(*args)


def _pack(arrays):
    flat = jnp.concatenate([a.reshape(-1).astype(F32) for a in arrays])
    pad = (-flat.shape[0]) % (8 * LANES)
    return jnp.pad(flat, (0, pad)).reshape(-1, LANES)


def _unpack(slab, shapes):
    flat = slab.reshape(slab.shape[:-2] + (-1,))
    out, off = [], 0
    for shp in shapes:
        size = 1
        for s in shp:
            size *= s
        out.append(flat[..., off:off + size].reshape(flat.shape[:-1] + tuple(shp)))
        off += size
    return out


def kernel(x, mem, norm1_g, mem_norm_g, a_w_in, a_q_g, a_k_g, a_rel_bias, b_w_in, b_b_in, b_conv_w, b_conv_b, b_ln_g, b_ln_b, mq_g, mk_g, w_mem_kv, w_out, norm2_g, w_gate, w_up, w_down, loss_target, m_norm1_g, m_mem_norm_g, m_a_w_in, m_a_q_g, m_a_k_g, m_a_rel_bias, m_b_w_in, m_b_b_in, m_b_conv_w, m_b_conv_b, m_b_ln_g, m_b_ln_b, m_mq_g, m_mk_g, m_w_mem_kv, m_w_out, m_norm2_g, m_w_gate, m_w_up, m_w_down, v_norm1_g, v_mem_norm_g, v_a_w_in, v_a_q_g, v_a_k_g, v_a_rel_bias, v_b_w_in, v_b_b_in, v_b_conv_w, v_b_conv_b, v_b_ln_g, v_b_ln_b, v_mq_g, v_mk_g, v_w_mem_kv, v_w_out, v_norm2_g, v_w_gate, v_w_up, v_w_down):
    batch, seq, d = x.shape
    mtok = mem.shape[1]
    n = batch * seq
    ax, ay, ac = _place()
    me = 4 * ax + 2 * ay + ac
    core_arr = jnp.reshape(ac, (1,)).astype(jnp.int32)
    chip_arr = jnp.reshape(2 * ax + ay, (1,)).astype(jnp.int32)

    def t_bf16(w):
        return jnp.transpose(w).astype(BF16)

    def after(value, *earlier):
        return lax.optimization_barrier((value, *earlier))[0]

    def gather_mix(l, when, name, collective_id):
        srcs = [w_mem_kv[l].astype(BF16), w_out[l].astype(BF16)] + ([t_bf16(b_w_in[0])] if l == 1 else [])
        return gather_weights([after(srcs[0], when)] + srcs[1:], name, collective_id)

    def gather_ffn(l, when, name, collective_id):
        return gather_weights(
            [after(t_bf16(w_gate[l]), when), t_bf16(w_up[l]), w_down[l].astype(BF16)], name, collective_id)

    f_loc = b_b_in.shape[1]
    c_loc = b_conv_b.shape[1]

    def two(g):
        return jnp.concatenate([g, g], axis=-1)

    gq2, gk2 = two(a_q_g), two(a_k_g)
    rel16 = jnp.pad(a_rel_bias[0], ((0, 16 - a_rel_bias.shape[1]), (0, 0)))
    bias = bias_blocks(rel16)

    x0 = x.reshape(n, d)
    mem2 = mem.reshape(batch * mtok, d)
    zero_mem = jnp.zeros_like(mem2)

    saved = []
    xin = x0
    a_win_t, = gather_weights([t_bf16(a_w_in[0])], "gather_in_a", 1)
    wg_t, wu_t, wd, wo, wkv = [None] * 2, [None] * 2, [None] * 2, [None] * 2, [None] * 2
    h = after(rms_fwd(xin, norm1_g[0:1], name="rms1_fwd_0"), bias)
    target = loss_target.reshape(n, d)
    for l in range(2):
        mem_n = rms_fwd(mem2, mem_norm_g[l:l + 1], name=f"rms_mem_fwd_{l}")
        gq4 = jnp.tile(mq_g[l:l + 1], (1, 4))
        gk4 = jnp.tile(mk_g[l:l + 1], (1, 4))
        y_conv = None
        if l == 0:
            wkv[0], wo[0] = gather_mix(0, h, "gather_mix_a", 2)
            z = mm_nt(h, a_win_t, name="in_proj_a")
            wg_t[0], wu_t[0], wd[0] = gather_ffn(0, z, "gather_ffn_a", 3)
            cat = attn_fwd(z, gq2, gk2, bias, batch, seq)
            wkv[1], wo[1], b_win_t = gather_mix(1, cat, "gather_mix_b", 4)
            qcol = 3 * TOK_WIDTH // MEM_WIDTH
        else:
            small_shapes = [(f_loc,), (CONV_W, c_loc), (c_loc,), (c_loc,), (c_loc,)]
            gathered = small_exchange(after(_pack([b_b_in, b_conv_w, b_conv_b, b_ln_g, b_ln_b]), xin), reduce=False)
            bb_g, cw_g, cb_g, lg_g, lb_g = _unpack(gathered, small_shapes)
            bb_full = bb_g.reshape(1, -1)
            cw_full = jnp.pad(jnp.transpose(cw_g, (1, 0, 2)).reshape(CONV_W, -1), ((0, 32 - CONV_W), (0, 0)))
            cb_full, lg_full, lb_full = cb_g.reshape(1, -1), lg_g.reshape(1, -1), lb_g.reshape(1, -1)
            z = mm_nt(h, b_win_t, bias=bb_full, name="in_proj_b")
            cat, y_conv = conv_fwd(z, cw_full, cb_full, lg_full, lb_full, batch, seq)
            qcol = 2 * TOK_WIDTH // MEM_WIDTH
        kv = mm_nn(mem_n, wkv[l], name=f"mem_kv_{l}")
        cat = memattn_fwd(z, kv, gq4, gk4, cat, batch, seq, qcol, name=f"memattn_fwd_{l}")
        x1, h2 = proj_norm(cat, wo[l], xin, norm2_g[l:l + 1], name=f"out_proj_{l}")
        if l == 0:
            wg_t[1], wu_t[1], wd[1] = gather_ffn(1, x1, "gather_ffn_b", 5)
        if l == 0:
            gate, up, act, x2, h_next = ffn_fwd(h2, wg_t[0], wu_t[0], wd[0], x1, gain=norm1_g[1:2], name="ffn_fwd_0")
        else:
            gate, up, act, dx_b, loss_blk = ffn_fwd(h2, wg_t[1], wu_t[1], wd[1], x1, target=target, name="ffn_fwd_1")
        saved.append(dict(xin=xin, h=h, mem_n=mem_n, kv=kv, gq4=gq4, gk4=gk4, z=z, qcol=qcol, cat=cat, x1=x1, h2=h2,
                          gate=gate, up=up, act=act, y_conv=y_conv))
        if l == 0:
            xin, h = x2, h_next

    big = {}
    small = {}
    reduced = {}
    groups = 0

    def scatter_siblings(keys):
        nonlocal groups
        gid = groups
        groups += 1
        return gid, keys, scatter_to_sibling([big[k] for k in keys], f"scatter_sibling_{gid}", 8 + 2 * gid)

    def scatter_chips(stage1, when):
        gid, keys, landed1 = stage1
        parts = add_sibling([after(big[keys[0]], when)] + [big[k] for k in keys[1:]], landed1, core_arr,
                            name=f"add_sibling_{gid}")
        landed2 = scatter_to_chips(parts, f"scatter_chips_{gid}", 9 + 2 * gid)
        for k, p, ld in zip(keys, parts, landed2):
            reduced[k] = (p, ld)
        return parts, landed2

    def rows_of(w, transposed):
        w = jnp.swapaxes(w, 1, 2) if transposed else w
        return w.reshape(w.shape[0] * w.shape[1], w.shape[2])

    sharded = {
        "win0": (2, True), "win1": (6, True), "wkv": (14, False), "wo": (15, False),
        "wg": (17, True), "wu": (18, True), "wd": (19, False)}
    weights = [norm1_g, mem_norm_g, a_w_in, a_q_g, a_k_g, a_rel_bias, b_w_in, b_b_in, b_conv_w, b_conv_b, b_ln_g,
               b_ln_b, mq_g, mk_g, w_mem_kv, w_out, norm2_g, w_gate, w_up, w_down]
    moms = [m_norm1_g, m_mem_norm_g, m_a_w_in, m_a_q_g, m_a_k_g, m_a_rel_bias, m_b_w_in, m_b_b_in, m_b_conv_w,
            m_b_conv_b, m_b_ln_g, m_b_ln_b, m_mq_g, m_mk_g, m_w_mem_kv, m_w_out, m_norm2_g, m_w_gate, m_w_up, m_w_down]
    vels = [v_norm1_g, v_mem_norm_g, v_a_w_in, v_a_q_g, v_a_k_g, v_a_rel_bias, v_b_w_in, v_b_b_in, v_b_conv_w,
            v_b_conv_b, v_b_ln_g, v_b_ln_b, v_mq_g, v_mk_g, v_w_mem_kv, v_w_out, v_norm2_g, v_w_gate, v_w_up, v_w_down]
    updated = {}

    def update_layer(l, when):
        for key, (idx, transposed) in sharded.items():
            if key in ("win0", "win1"):
                if key != f"win{l}":
                    continue
                layer, rkey = 0, key
            else:
                layer, rkey = l, f"{key}{l}"
            part, landed = reduced[rkey]
            updated[key] = adamw_shard(
                layer, after(rows_of(weights[idx], transposed), when), rows_of(moms[idx], transposed),
                rows_of(vels[idx], transposed), part, landed, chip_arr, updated.get(key), name=f"adamw_{rkey}")

    mix_landed = None
    for l in (1, 0):
        sv = saved[l]
        dgate, dup, dx1_b, small[f"norm2_{l}"] = ffn_bwd(
            dx_b, wd[l], sv["gate"], sv["up"], wg_t[l], wu_t[l], sv["x1"], norm2_g[l:l + 1], name=f"ffn_bwd_{l}")
        if l == 0:
            dgate = after(dgate, *mix_landed)
            update_layer(1, dx1_b)
        big[f"wg{l}"], big[f"wu{l}"], big[f"wd{l}"] = ffn_weight_grads(
            dgate, dup, sv["h2"], sv["act"], dx_b, name=f"grad_ffn_{l}")
        big[f"wo{l}"] = mm_tn(sv["cat"], dx1_b, name=f"grad_wo_{l}")
        stage1 = scatter_siblings([f"wd{l}", f"wg{l}", f"wu{l}", f"wo{l}"])
        dcat = mm_nt(dx1_b, wo[l], name=f"out_proj_bwd_{l}")
        parts, ffn_landed = scatter_chips(stage1, dcat)
        dcat = after(dcat, *parts)
        if l == 0:
            dz, dbias, small["a_q"], small["a_k"] = attn_bwd(sv["z"], dcat, gq2, gk2, bias, batch, seq)
            small["rel"] = bias_grad(dbias)
            win_t = a_win_t
        else:
            dz, small["cw"], small["csum"] = conv_bwd(sv["z"], sv["y_conv"], dcat, cw_full, lg_full, lb_full, batch, seq)
            win_t = b_win_t
        dz = after(dz, *ffn_landed)
        dz, dkv, small[f"mq_{l}"], small[f"mk_{l}"] = memattn_bwd(
            sv["z"], sv["kv"], dcat, sv["gq4"], sv["gk4"], dz, batch, seq, sv["qcol"], name=f"memattn_bwd_{l}")
        big[f"win{l}"] = mm_tn(dz, sv["h"], name=f"grad_win_{l}")
        big[f"wkv{l}"] = mm_tn(sv["mem_n"], dkv, name=f"grad_wkv_{l}")
        stage1 = scatter_siblings([f"win{l}", f"wkv{l}"])
        dx_b, small[f"norm1_{l}"], dz_sum = in_proj_bwd(
            dz, win_t, sv["xin"], norm1_g[l:l + 1], dx1_b, BF16 if l == 1 else F32, name=f"in_proj_bwd_{l}")
        if l == 1:
            small["bb"] = dz_sum
        parts, mix_landed = scatter_chips(stage1, dx_b)
        dx_b = after(dx_b, *parts)
        dmem_n = mm_nt(dkv, wkv[l], out_dtype=F32, name=f"mem_kv_bwd_{l}")
        _, _, small[f"memnorm_{l}"] = rms_bwd(dmem_n, mem2, mem_norm_g[l:l + 1], zero_mem, name=f"rms_mem_bwd_{l}")
    grad_x = dx_b.reshape(batch, seq, d)
    update_layer(0, dx_b)

    def shaped(rows, idx, transposed):
        shp = weights[idx].shape
        if transposed:
            return jnp.swapaxes(rows.reshape(shp[0], shp[2], shp[1]), 1, 2)
        return rows.reshape(shp)

    def fold(v, groups):
        return jnp.sum(v.reshape(groups, HEAD_DIM), axis=0, keepdims=True)

    heads = a_rel_bias.shape[1]
    small_list = [
        jnp.concatenate([small["norm1_0"], small["norm1_1"]]),
        jnp.concatenate([small["memnorm_0"], small["memnorm_1"]]),
        fold(small["a_q"], 2), fold(small["a_k"], 2), small["rel"][:heads][None],
        small["bb"], small["cw"][:CONV_W][None], small["csum"][0:1], small["csum"][1:2], small["csum"][2:3],
        jnp.concatenate([fold(small["mq_0"], 4), fold(small["mq_1"], 4)]),
        jnp.concatenate([fold(small["mk_0"], 4), fold(small["mk_1"], 4)]),
        jnp.concatenate([small["norm2_0"], small["norm2_1"]]),
    ]
    (g_norm1, g_memnorm, g_aq, g_ak, g_rel, g_bb_full, g_cw_full, g_cb_full, g_lg_full, g_lb_full,
     g_mq, g_mk, g_norm2, loss_sum) = reduce_small(small_list + [loss_blk])
    loss = loss_sum[0, 0]
    g_bb = lax.dynamic_slice_in_dim(g_bb_full, me * f_loc, f_loc, axis=1)
    g_cw = lax.dynamic_slice_in_dim(g_cw_full, me * c_loc, c_loc, axis=2)
    g_cb = lax.dynamic_slice_in_dim(g_cb_full, me * c_loc, c_loc, axis=1)
    g_lg = lax.dynamic_slice_in_dim(g_lg_full, me * c_loc, c_loc, axis=1)
    g_lb = lax.dynamic_slice_in_dim(g_lb_full, me * c_loc, c_loc, axis=1)

    grads = [g_norm1, g_memnorm, None, g_aq, g_ak, g_rel, None, g_bb, g_cw, g_cb, g_lg, g_lb,
             g_mq, g_mk, None, None, g_norm2, None, None, None]
    deltas, new_m, new_v = [None] * 20, [None] * 20, [None] * 20
    for key, (idx, transposed) in sharded.items():
        grads[idx], deltas[idx], new_m[idx], new_v[idx] = (shaped(r, idx, transposed) for r in updated[key])

    small_idx = [i for i in range(20) if i not in {idx for idx, _ in sharded.values()}]
    dl, nm, nv = adamw_small([weights[i] for i in small_idx], [grads[i] for i in small_idx],
                             [moms[i] for i in small_idx], [vels[i] for i in small_idx])
    for i, a, b, cc in zip(small_idx, dl, nm, nv):
        deltas[i], new_m[i], new_v[i] = a, b, cc

    return (loss, grad_x, *grads, *deltas, *new_m, *new_v)
```

```python
import functools

import jax
import jax.numpy as jnp
from jax import lax
from jax.experimental import pallas as pl
from jax.experimental.pallas import tpu as pltpu
from jax.experimental.pallas import tpu_sc as plsc

F32 = jnp.float32
BF16 = jnp.bfloat16
HIGHEST = lax.Precision.HIGHEST
MESH = pl.DeviceIdType.MESH
ANY = pl.BlockSpec(memory_space=pl.ANY)

N_DEV = 8
D_MODEL = 1024
HEAD_DIM = 64
TOK_WIDTH = 768
MEM_WIDTH = 256
CHUNK = 64
Q_BLOCK = 256
KEY_WIN = 768
BAND = 576
N_REL = 192
CONV_W = 31
CONV_HALO = 32
NORM_EPS = 1e-6
NEG_INF = -1e30
ATTN_SCALE = HEAD_DIM ** -0.5
LANES = 128
ROW_TILE = 512
VMEM_LIMIT = 56 * 1024 * 1024

ADAM_LR, ADAM_B1, ADAM_B2, ADAM_EPS, ADAM_WD, ADAM_STEP = 0.001, 0.9, 0.999, 1e-08, 0.01, 10


def _params(*sem):
    return pltpu.CompilerParams(dimension_semantics=sem, vmem_limit_bytes=VMEM_LIMIT)


def _row_tile(m):
    return ROW_TILE if m % ROW_TILE == 0 else m


def _col_tile(n, cap=1408):
    best = None
    for t in range(LANES, min(n, cap) + 1, LANES):
        if n % t == 0:
            best = t
    return best if best is not None else n


def _dot(a, b, ca, cb):
    return lax.dot_general(a, b, (((ca,), (cb,)), ((), ())), preferred_element_type=F32)


def _sigmoid(x):
    return 0.5 * jnp.tanh(0.5 * x) + 0.5


def mm_nt(a, b, bias=None, out_dtype=BF16, name="mm_nt"):
    m, k = a.shape
    n = b.shape[0]
    tm, tn = _row_tile(m), _col_tile(n)

    def body(*refs):
        a_ref, b_ref = refs[0], refs[1]
        o_ref = refs[-1]
        acc = _dot(a_ref[...].astype(BF16), b_ref[...].astype(BF16), 1, 1)
        if bias is not None:
            acc = acc + refs[2][...]
        o_ref[...] = acc.astype(o_ref.dtype)

    in_specs = [pl.BlockSpec((tm, k), lambda j, i: (i, 0)), pl.BlockSpec((tn, k), lambda j, i: (j, 0))]
    args = [a, b]
    if bias is not None:
        in_specs.append(pl.BlockSpec((1, tn), lambda j, i: (0, j)))
        args.append(bias)
    return pl.pallas_call(
        body, out_shape=jax.ShapeDtypeStruct((m, n), out_dtype), grid=(n // tn, m // tm),
        in_specs=in_specs, out_specs=pl.BlockSpec((tm, tn), lambda j, i: (i, j)),
        compiler_params=_params("parallel", "arbitrary"), name=name)(*args)


def mm_nn(a, b, res=None, out_dtype=F32, name="mm_nn"):
    m, k = a.shape
    n = b.shape[1]
    tm, tn = _row_tile(m), _col_tile(n, 1024)

    def body(*refs):
        a_ref, b_ref = refs[0], refs[1]
        o_ref = refs[-1]
        acc = _dot(a_ref[...].astype(BF16), b_ref[...].astype(BF16), 1, 0)
        if res is not None:
            acc = acc + refs[2][...]
        o_ref[...] = acc.astype(o_ref.dtype)

    in_specs = [pl.BlockSpec((tm, k), lambda j, i: (i, 0)), pl.BlockSpec((k, tn), lambda j, i: (0, j))]
    args = [a, b]
    if res is not None:
        in_specs.append(pl.BlockSpec((tm, tn), lambda j, i: (i, j)))
        args.append(res)
    return pl.pallas_call(
        body, out_shape=jax.ShapeDtypeStruct((m, n), out_dtype), grid=(n // tn, m // tm),
        in_specs=in_specs, out_specs=pl.BlockSpec((tm, tn), lambda j, i: (i, j)),
        compiler_params=_params("parallel", "arbitrary"), name=name)(*args)


def mm2_nn(a1, b1, a2, b2, name="mm2_nn"):
    m, k = a1.shape
    n = b1.shape[1]
    tm = _row_tile(m)

    def body(a1_ref, b1_ref, a2_ref, b2_ref, o_ref):
        o_ref[...] = _dot(a1_ref[...], b1_ref[...], 1, 0) + _dot(a2_ref[...], b2_ref[...], 1, 0)

    a_spec = pl.BlockSpec((tm, k), lambda i: (i, 0))
    b_spec = pl.BlockSpec((k, n), lambda i: (0, 0))
    return pl.pallas_call(
        body, out_shape=jax.ShapeDtypeStruct((m, n), F32), grid=(m // tm,),
        in_specs=[a_spec, b_spec, a_spec, b_spec], out_specs=pl.BlockSpec((tm, n), lambda i: (i, 0)),
        compiler_params=_params("parallel"), name=name)(a1, b1, a2, b2)


def mm_tn(a, b, out_dtype=BF16, name="mm_tn"):
    t, r = a.shape
    c = b.shape[1]
    tr = _col_tile(r, 512)

    def body(a_ref, b_ref, o_ref):
        o_ref[...] = _dot(a_ref[...].astype(BF16), b_ref[...].astype(BF16), 0, 0).astype(o_ref.dtype)

    return pl.pallas_call(
        body, out_shape=jax.ShapeDtypeStruct((r, c), out_dtype), grid=(r // tr,),
        in_specs=[pl.BlockSpec((t, tr), lambda i: (0, i)), pl.BlockSpec((t, c), lambda i: (0, 0))],
        out_specs=pl.BlockSpec((tr, c), lambda i: (i, 0)),
        compiler_params=_params("parallel"), name=name)(a, b)


def _resident(shape):
    return pl.BlockSpec(shape, lambda i: (0, 0), pipeline_mode=pl.Buffered(1))


def proj_norm(a, b, res, gain, name):
    m, k = a.shape
    n = b.shape[1]
    tm = _row_tile(m)

    def body(a_ref, b_ref, res_ref, g_ref, x_ref, h_ref):
        xv = res_ref[...] + _dot(a_ref[...], b_ref[...], 1, 0)
        x_ref[...] = xv
        r = lax.rsqrt(jnp.mean(xv * xv, axis=-1, keepdims=True) + NORM_EPS)
        h_ref[...] = (xv * r * g_ref[...]).astype(BF16)

    row = pl.BlockSpec((tm, n), lambda i: (i, 0))
    return pl.pallas_call(
        body, out_shape=(jax.ShapeDtypeStruct((m, n), F32), jax.ShapeDtypeStruct((m, n), BF16)), grid=(m // tm,),
        in_specs=[pl.BlockSpec((tm, k), lambda i: (i, 0)), _resident((k, n)), row, _resident((1, n))],
        out_specs=(row, row), compiler_params=_params("parallel"), name=name)(a, b, res, gain)


def proj_loss(a, b, res, target, name):
    m, k = a.shape
    n = b.shape[1]
    tm = _row_tile(m)
    nt = m // tm

    def body(a_ref, b_ref, res_ref, t_ref, dy_ref, dyb_ref, l_ref, acc_ref):
        i = pl.program_id(0)

        @pl.when(i == 0)
        def _():
            acc_ref[...] = jnp.zeros_like(acc_ref)

        err = res_ref[...] + _dot(a_ref[...], b_ref[...], 1, 0) - t_ref[...]
        dy = err * (1.0 / n)
        dy_ref[...] = dy
        dyb_ref[...] = dy.astype(BF16)
        acc_ref[...] += jnp.sum(err * err, axis=0, keepdims=True)

        @pl.when(i == nt - 1)
        def _():
            total = jnp.sum(acc_ref[...], axis=-1, keepdims=True) * (0.5 / n)
            l_ref[...] = jnp.broadcast_to(total, l_ref.shape)

    row = pl.BlockSpec((tm, n), lambda i: (i, 0))
    return pl.pallas_call(
        body, out_shape=(jax.ShapeDtypeStruct((m, n), F32), jax.ShapeDtypeStruct((m, n), BF16),
                         jax.ShapeDtypeStruct((8, LANES), F32)), grid=(nt,),
        in_specs=[pl.BlockSpec((tm, k), lambda i: (i, 0)), _resident((k, n)), row, row],
        out_specs=(row, row, pl.BlockSpec((8, LANES), lambda i: (0, 0))),
        scratch_shapes=[pltpu.VMEM((1, n), F32)],
        compiler_params=_params("arbitrary"), name=name)(a, b, res, target)


def in_proj_bwd(dz, w_t, x, gain, dres, out_dtype, name):
    m, n = x.shape
    k = dz.shape[1]
    tm = _row_tile(m)

    def body(dz_ref, w_ref, x_ref, g_ref, dres_ref, dx_ref, dg_ref, cs_ref):
        @pl.when(pl.program_id(0) == 0)
        def _():
            dg_ref[...] = jnp.zeros_like(dg_ref)
            cs_ref[...] = jnp.zeros_like(cs_ref)

        dzv = dz_ref[...]
        cs_ref[...] += jnp.sum(dzv.astype(F32), axis=0, keepdims=True)
        dhv = _dot(dzv, w_ref[...], 1, 0)
        xv = x_ref[...]
        r = lax.rsqrt(jnp.mean(xv * xv, axis=-1, keepdims=True) + NORM_EPS)
        xhat = xv * r
        dg_ref[...] += jnp.sum(dhv * xhat, axis=0, keepdims=True)
        dxhat = dhv * g_ref[...]
        dx = dres_ref[...].astype(F32) + r * (dxhat - xhat * jnp.mean(dxhat * xhat, axis=-1, keepdims=True))
        dx_ref[...] = dx.astype(dx_ref.dtype)

    row = pl.BlockSpec((tm, n), lambda i: (i, 0))
    return pl.pallas_call(
        body, out_shape=(jax.ShapeDtypeStruct((m, n), out_dtype), jax.ShapeDtypeStruct((1, n), F32),
                         jax.ShapeDtypeStruct((1, k), F32)), grid=(m // tm,),
        in_specs=[pl.BlockSpec((tm, k), lambda i: (i, 0)), _resident(w_t.shape), row, _resident((1, n)), row],
        out_specs=(row, pl.BlockSpec((1, n), lambda i: (0, 0)), pl.BlockSpec((1, k), lambda i: (0, 0))),
        compiler_params=_params("arbitrary"), name=name)(dz, w_t, x, gain, dres)


FFN_ROWS = 256


def _ffn_row_tile(m):
    return FFN_ROWS if m % FFN_ROWS == 0 else m


def ffn_fwd(h2, wg_t, wu_t, wd, x1, gain=None, target=None, name="ffn_fwd"):
    n, d = h2.shape
    f = wg_t.shape[0]
    tm = _ffn_row_tile(n)
    nt = n // tm
    last = target is not None

    def body(h_ref, wg_ref, wu_ref, wd_ref, x1_ref, e_ref, g_ref, u_ref, a_ref, *rest):
        hv = h_ref[...]
        gv = _dot(hv, wg_ref[...], 1, 1)
        uv = _dot(hv, wu_ref[...], 1, 1)
        g_ref[...] = gv.astype(BF16)
        u_ref[...] = uv.astype(BF16)
        av = (gv * _sigmoid(gv) * uv).astype(BF16)
        a_ref[...] = av
        xv = x1_ref[...] + _dot(av, wd_ref[...], 1, 0)
        if not last:
            x_ref, hn_ref = rest
            x_ref[...] = xv
            r = lax.rsqrt(jnp.mean(xv * xv, axis=-1, keepdims=True) + NORM_EPS)
            hn_ref[...] = (xv * r * e_ref[...]).astype(BF16)
        else:
            dyb_ref, l_ref, acc_ref = rest
            i = pl.program_id(0)

            @pl.when(i == 0)
            def _():
                acc_ref[...] = jnp.zeros_like(acc_ref)

            err = xv - e_ref[...]
            dyb_ref[...] = (err * (1.0 / d)).astype(BF16)
            acc_ref[...] += jnp.sum(err * err, axis=0, keepdims=True)

            @pl.when(i == nt - 1)
            def _():
                total = jnp.sum(acc_ref[...], axis=-1, keepdims=True) * (0.5 / d)
                l_ref[...] = jnp.broadcast_to(total, l_ref.shape)

    row_d = pl.BlockSpec((tm, d), lambda i: (i, 0))
    row_f = pl.BlockSpec((tm, f), lambda i: (i, 0))
    act_shape = jax.ShapeDtypeStruct((n, f), BF16)
    if not last:
        extra_in, extra = _resident((1, d)), gain
        out_shape = (act_shape, act_shape, act_shape, jax.ShapeDtypeStruct((n, d), F32), jax.ShapeDtypeStruct((n, d), BF16))
        out_specs = (row_f, row_f, row_f, row_d, row_d)
        scratch = []
    else:
        extra_in, extra = row_d, target
        out_shape = (act_shape, act_shape, act_shape, jax.ShapeDtypeStruct((n, d), BF16),
                     jax.ShapeDtypeStruct((8, LANES), F32))
        out_specs = (row_f, row_f, row_f, row_d, pl.BlockSpec((8, LANES), lambda i: (0, 0)))
        scratch = [pltpu.VMEM((1, d), F32)]
    return pl.pallas_call(
        body, out_shape=out_shape, grid=(nt,),
        in_specs=[row_d, _resident((f, d)), _resident((f, d)), _resident((f, d)), row_d, extra_in],
        out_specs=out_specs, scratch_shapes=scratch,
        compiler_params=_params("arbitrary"), name=name)(h2, wg_t, wu_t, wd, x1, extra)


def ffn_bwd(dx_b, wd, gate, up, wg_t, wu_t, x1, gain, wo, name="ffn_bwd"):
    n, d = x1.shape
    f = wd.shape[0]
    tm = _ffn_row_tile(n)

    def body(dxb_ref, wd_ref, g_ref, u_ref, wg_ref, wu_ref, x_ref, gain_ref, wo_ref,
             dg_ref, du_ref, dxo_ref, dc_ref, dgain_ref):
        @pl.when(pl.program_id(0) == 0)
        def _():
            dgain_ref[...] = jnp.zeros_like(dgain_ref)

        dact = _dot(dxb_ref[...], wd_ref[...], 1, 1)
        gv = g_ref[...].astype(F32)
        uv = u_ref[...].astype(F32)
        sg = _sigmoid(gv)
        dgv = (dact * uv * sg * (1.0 + gv * (1.0 - sg))).astype(BF16)
        duv = (dact * gv * sg).astype(BF16)
        dg_ref[...] = dgv
        du_ref[...] = duv
        dhv = _dot(dgv, wg_ref[...], 1, 0) + _dot(duv, wu_ref[...], 1, 0)
        xv = x_ref[...]
        r = lax.rsqrt(jnp.mean(xv * xv, axis=-1, keepdims=True) + NORM_EPS)
        xhat = xv * r
        dgain_ref[...] += jnp.sum(dhv * xhat, axis=0, keepdims=True)
        dxhat = dhv * gain_ref[...]
        dxb = (dxb_ref[...].astype(F32) + r * (dxhat - xhat * jnp.mean(dxhat * xhat, axis=-1, keepdims=True))).astype(BF16)
        dxo_ref[...] = dxb
        dc_ref[...] = _dot(dxb, wo_ref[...], 1, 1).astype(BF16)

    row_d = pl.BlockSpec((tm, d), lambda i: (i, 0))
    row_f = pl.BlockSpec((tm, f), lambda i: (i, 0))
    w_spec = _resident((f, d))
    act_shape = jax.ShapeDtypeStruct((n, f), BF16)
    row_shape = jax.ShapeDtypeStruct((n, d), BF16)
    return pl.pallas_call(
        body, out_shape=(act_shape, act_shape, row_shape, jax.ShapeDtypeStruct((n, wo.shape[0]), BF16),
                         jax.ShapeDtypeStruct((1, d), F32)),
        grid=(n // tm,),
        in_specs=[row_d, w_spec, row_f, row_f, w_spec, w_spec, row_d, _resident((1, d)), _resident(wo.shape)],
        out_specs=(row_f, row_f, row_d, pl.BlockSpec((tm, wo.shape[0]), lambda i: (i, 0)),
                   pl.BlockSpec((1, d), lambda i: (0, 0))),
        compiler_params=_params("arbitrary"), name=name)(dx_b, wd, gate, up, wg_t, wu_t, x1, gain, wo)


def ffn_weight_grads(dgate, dup, h2, act, dx_b, name="ffn_weight_grads"):
    t, r = dgate.shape
    c = h2.shape[1]
    tr = _col_tile(r, 512)

    def body(a1_ref, a2_ref, a3_ref, b12_ref, b3_ref, o1_ref, o2_ref, o3_ref):
        bv = b12_ref[...]
        o1_ref[...] = _dot(a1_ref[...], bv, 0, 0).astype(o1_ref.dtype)
        o2_ref[...] = _dot(a2_ref[...], bv, 0, 0).astype(o2_ref.dtype)
        o3_ref[...] = _dot(a3_ref[...], b3_ref[...], 0, 0).astype(o3_ref.dtype)

    a_spec = pl.BlockSpec((t, tr), lambda i: (0, i))
    o_spec = pl.BlockSpec((tr, c), lambda i: (i, 0))
    shape = jax.ShapeDtypeStruct((r, c), BF16)
    return pl.pallas_call(
        body, out_shape=(shape, shape, shape), grid=(r // tr,),
        in_specs=[a_spec, a_spec, a_spec, _resident((t, c)), _resident((t, c))],
        out_specs=(o_spec, o_spec, o_spec), compiler_params=_params("parallel"), name=name)(dgate, dup, act, h2, dx_b)


def rms_fwd(x, g, name="rms_fwd"):
    n, d = x.shape
    tm = _row_tile(n)

    def body(x_ref, g_ref, o_ref):
        xv = x_ref[...]
        r = lax.rsqrt(jnp.mean(xv * xv, axis=-1, keepdims=True) + NORM_EPS)
        o_ref[...] = (xv * r * g_ref[...]).astype(o_ref.dtype)

    return pl.pallas_call(
        body, out_shape=jax.ShapeDtypeStruct((n, d), BF16), grid=(n // tm,),
        in_specs=[pl.BlockSpec((tm, d), lambda i: (i, 0)), pl.BlockSpec((1, d), lambda i: (0, 0))],
        out_specs=pl.BlockSpec((tm, d), lambda i: (i, 0)),
        compiler_params=_params("parallel"), name=name)(x, g)


def rms_bwd(dh, x, g, dres, name="rms_bwd"):
    n, d = x.shape
    tm = _row_tile(n)

    def body(dh_ref, x_ref, g_ref, dres_ref, dx_ref, dxb_ref, dg_ref):
        @pl.when(pl.program_id(0) == 0)
        def _():
            dg_ref[...] = jnp.zeros_like(dg_ref)

        xv = x_ref[...]
        dhv = dh_ref[...].astype(F32)
        r = lax.rsqrt(jnp.mean(xv * xv, axis=-1, keepdims=True) + NORM_EPS)
        xhat = xv * r
        dg_ref[...] += jnp.sum(dhv * xhat, axis=0, keepdims=True)
        dxhat = dhv * g_ref[...]
        mean_t = jnp.mean(dxhat * xhat, axis=-1, keepdims=True)
        dx = dres_ref[...] + r * (dxhat - xhat * mean_t)
        dx_ref[...] = dx
        dxb_ref[...] = dx.astype(BF16)

    row = pl.BlockSpec((tm, d), lambda i: (i, 0))
    vec = pl.BlockSpec((1, d), lambda i: (0, 0))
    return pl.pallas_call(
        body, out_shape=(jax.ShapeDtypeStruct((n, d), F32), jax.ShapeDtypeStruct((n, d), BF16),
                         jax.ShapeDtypeStruct((1, d), F32)), grid=(n // tm,),
        in_specs=[row, row, vec, row], out_specs=(row, row, vec),
        compiler_params=_params("arbitrary"), name=name)(dh, x, g, dres)


def gate_up(h2, wg_t, wu_t, name="gate_up"):
    n, d = h2.shape
    f = wg_t.shape[0]
    tm, tn = _row_tile(n), _col_tile(f)

    def body(h_ref, wg_ref, wu_ref, g_ref, u_ref, a_ref):
        hv = h_ref[...]
        gv = _dot(hv, wg_ref[...], 1, 1)
        uv = _dot(hv, wu_ref[...], 1, 1)
        g_ref[...] = gv.astype(BF16)
        u_ref[...] = uv.astype(BF16)
        a_ref[...] = (gv * _sigmoid(gv) * uv).astype(BF16)

    w_spec = pl.BlockSpec((tn, d), lambda j, i: (j, 0))
    o_spec = pl.BlockSpec((tm, tn), lambda j, i: (i, j))
    o_shape = jax.ShapeDtypeStruct((n, f), BF16)
    return pl.pallas_call(
        body, out_shape=(o_shape, o_shape, o_shape), grid=(f // tn, n // tm),
        in_specs=[pl.BlockSpec((tm, d), lambda j, i: (i, 0)), w_spec, w_spec], out_specs=(o_spec, o_spec, o_spec),
        compiler_params=_params("parallel", "arbitrary"), name=name)(h2, wg_t, wu_t)


def ffn_bwd_act(dx, wd, gate, up, name="ffn_bwd_act"):
    n, d = dx.shape
    f = wd.shape[0]
    tm, tn = _row_tile(n), _col_tile(f)

    def body(dx_ref, wd_ref, g_ref, u_ref, dg_ref, du_ref):
        dact = _dot(dx_ref[...].astype(BF16), wd_ref[...], 1, 1)
        gv = g_ref[...].astype(F32)
        uv = u_ref[...].astype(F32)
        sg = _sigmoid(gv)
        dg_ref[...] = (dact * uv * sg * (1.0 + gv * (1.0 - sg))).astype(BF16)
        du_ref[...] = (dact * gv * sg).astype(BF16)

    t_spec = pl.BlockSpec((tm, tn), lambda j, i: (i, j))
    o_shape = jax.ShapeDtypeStruct((n, f), BF16)
    return pl.pallas_call(
        body, out_shape=(o_shape, o_shape), grid=(f // tn, n // tm),
        in_specs=[pl.BlockSpec((tm, d), lambda j, i: (i, 0)), pl.BlockSpec((tn, d), lambda j, i: (j, 0)), t_spec, t_spec],
        out_specs=(t_spec, t_spec),
        compiler_params=_params("parallel", "arbitrary"), name=name)(dx, wd, gate, up)


def _group_masks(width):
    lane = lax.broadcasted_iota(jnp.int32, (1, width), 1)
    return [(lane >= HEAD_DIM * g) & (lane < HEAD_DIM * (g + 1)) for g in range(width // HEAD_DIM)]


def _group_sum(x, masks):
    out = jnp.zeros_like(x)
    for msk in masks:
        s = jnp.sum(jnp.where(msk, x, 0.0), axis=-1, keepdims=True)
        out = jnp.where(msk, s, out)
    return out


def _head_norm(x, gain, masks):
    r = lax.rsqrt(_group_sum(x * x, masks) * (1.0 / HEAD_DIM) + NORM_EPS)
    xhat = x * r
    return xhat * gain, xhat, r


def _head_norm_bwd(dxn, xhat, r, gain, masks):
    dgain = jnp.sum(dxn * xhat, axis=0, keepdims=True)
    dxhat = dxn * gain
    mean_t = _group_sum(dxhat * xhat, masks) * (1.0 / HEAD_DIM)
    return r * (dxhat - xhat * mean_t), dgain


def _softmax_rows(s):
    e = jnp.exp(s - jnp.max(s, axis=-1, keepdims=True))
    return e * (1.0 / jnp.sum(e, axis=-1, keepdims=True))


def _rel_onehot():
    col = lax.broadcasted_iota(jnp.int32, (1, KEY_WIN), 1)
    off = jnp.where(col < KEY_WIN - LANES, col, col - KEY_WIN)
    idx = jnp.clip(8 * CHUNK - off, -(CHUNK - 1), LANES) + (CHUNK - 1)
    return (lax.broadcasted_iota(jnp.int32, (N_REL, KEY_WIN), 0) == idx).astype(F32)


def bias_blocks(rel16):
    heads = TOK_WIDTH // HEAD_DIM

    def body(rel_ref, o_ref, u_ref):
        u_ref[...] = jnp.dot(rel_ref[...], _rel_onehot(), precision=HIGHEST, preferred_element_type=F32)
        row = lax.broadcasted_iota(jnp.int32, (CHUNK, KEY_WIN), 0)
        col = lax.broadcasted_iota(jnp.int32, (CHUNK, KEY_WIN), 1)
        for h in range(heads):
            xv = jnp.broadcast_to(u_ref[h:h + 1, :], (CHUNK, KEY_WIN))
            for b in range(6):
                xv = jnp.where(((row >> b) & 1) == 1, pltpu.roll(xv, 1 << b, axis=1), xv)
            xv = jnp.where(col < BAND, xv, NEG_INF)
            for i in range(Q_BLOCK // CHUNK):
                o_ref[h, CHUNK * i:CHUNK * (i + 1), :] = pltpu.roll(xv, CHUNK * i, axis=1) if i else xv

    return pl.pallas_call(
        body, out_shape=jax.ShapeDtypeStruct((heads, Q_BLOCK, KEY_WIN), F32),
        scratch_shapes=[pltpu.VMEM((16, KEY_WIN), F32)], name="bias_blocks")(rel16)


def bias_grad(dbias):
    heads = dbias.shape[0]

    def body(db_ref, o_ref, y_ref):
        y_ref[...] = jnp.zeros_like(y_ref)
        row = lax.broadcasted_iota(jnp.int32, (CHUNK, KEY_WIN), 0)
        for h in range(heads):
            fv = db_ref[h, 0:CHUNK, :]
            for i in range(1, Q_BLOCK // CHUNK):
                fv = fv + pltpu.roll(db_ref[h, CHUNK * i:CHUNK * (i + 1), :], KEY_WIN - CHUNK * i, axis=1)
            for b in range(6):
                fv = jnp.where(((row >> b) & 1) == 1, pltpu.roll(fv, KEY_WIN - (1 << b), axis=1), fv)
            y_ref[h:h + 1, :] = jnp.sum(fv, axis=0, keepdims=True)
        o_ref[...] = lax.dot_general(y_ref[...], _rel_onehot(), (((1,), (1,)), ((), ())),
                                     precision=HIGHEST, preferred_element_type=F32)

    return pl.pallas_call(
        body, out_shape=jax.ShapeDtypeStruct((16, N_REL), F32),
        scratch_shapes=[pltpu.VMEM((16, KEY_WIN), F32)], name="bias_grad")(dbias)


def _attn_windows(seq):
    out = []
    for j in range(seq // Q_BLOCK):
        r0 = j * Q_BLOCK
        k0 = max(0, r0 - 8 * CHUNK)
        width = r0 + Q_BLOCK - k0
        out.append((r0, k0, width, KEY_WIN - width))
    return out


def attn_fwd(z, gq2, gk2, bias, batch, seq):
    n = z.shape[0]
    pairs = TOK_WIDTH // LANES

    def body(q_ref, k_ref, v_ref, gq_ref, gk_ref, b_ref, o_ref, qs_s, kn_s):
        masks = _group_masks(LANES)
        qs_s[...] = (_head_norm(q_ref[...].astype(F32), gq_ref[...], masks)[0] * ATTN_SCALE).astype(BF16)
        kn_s[...] = _head_norm(k_ref[...].astype(F32), gk_ref[...], masks)[0].astype(BF16)
        for r0, k0, width, c0 in _attn_windows(seq):
            qb = qs_s[r0:r0 + Q_BLOCK, :]
            kw = kn_s[k0:k0 + width, :]
            vw = v_ref[k0:k0 + width, :]
            out = jnp.zeros((Q_BLOCK, LANES), F32)
            for h, msk in enumerate(masks):
                qh = jnp.where(msk, qb, jnp.zeros_like(qb))
                s = _dot(qh, kw, 1, 1) + b_ref[h, :, c0:KEY_WIN]
                p = _softmax_rows(s).astype(BF16)
                out = jnp.where(msk, _dot(p, vw, 1, 0), out)
            o_ref[r0:r0 + Q_BLOCK, :] = out.astype(o_ref.dtype)

    def col(off):
        return pl.BlockSpec((seq, LANES), lambda b, p: (b, off + p))

    vec = pl.BlockSpec((1, LANES), lambda b, p: (0, 0))
    return pl.pallas_call(
        body, out_shape=jax.ShapeDtypeStruct((n, D_MODEL), BF16), grid=(batch, pairs),
        in_specs=[col(0), col(pairs), col(2 * pairs), vec, vec,
                  pl.BlockSpec((2, Q_BLOCK, KEY_WIN), lambda b, p: (p, 0, 0))],
        out_specs=pl.BlockSpec((seq, LANES), lambda b, p: (b, p)),
        scratch_shapes=[pltpu.VMEM((seq, LANES), BF16), pltpu.VMEM((seq, LANES), BF16)],
        compiler_params=_params("parallel", "arbitrary"), name="attn_fwd")(z, z, z, gq2, gk2, bias)


def attn_bwd(z, dcat, gq2, gk2, bias, batch, seq):
    n = z.shape[0]
    pairs = TOK_WIDTH // LANES

    def body(q_ref, k_ref, v_ref, do_ref, gq_ref, gk_ref, b_ref,
             dz_ref, db_ref, dgq_ref, dgk_ref, qs_s, kn_s, dqn_s, dkn_s, dv_s, dk_o, dv_o):
        pi, bi, which = pl.program_id(0), pl.program_id(1), pl.program_id(2)

        @pl.when(which == 0)
        def _():
            masks = _group_masks(LANES)

            @pl.when(bi == 0)
            def _():
                db_ref[...] = jnp.zeros_like(db_ref)

            @pl.when((bi == 0) & (pi == 0))
            def _():
                dgq_ref[...] = jnp.zeros_like(dgq_ref)
                dgk_ref[...] = jnp.zeros_like(dgk_ref)

            qn, qhat, rq = _head_norm(q_ref[...].astype(F32), gq_ref[...], masks)
            kn, khat, rk = _head_norm(k_ref[...].astype(F32), gk_ref[...], masks)
            qs_s[...] = (qn * ATTN_SCALE).astype(BF16)
            kn_s[...] = kn.astype(BF16)
            dkn_s[...] = jnp.zeros_like(dkn_s)
            dv_s[...] = jnp.zeros_like(dv_s)
            for r0, k0, width, c0 in _attn_windows(seq):
                qb = qs_s[r0:r0 + Q_BLOCK, :]
                dob = do_ref[r0:r0 + Q_BLOCK, :]
                kw = kn_s[k0:k0 + width, :]
                vw = v_ref[k0:k0 + width, :]
                dq_acc = jnp.zeros((Q_BLOCK, LANES), F32)
                dk_acc = jnp.zeros((width, LANES), F32)
                dv_acc = jnp.zeros((width, LANES), F32)
                for h, msk in enumerate(masks):
                    qh = jnp.where(msk, qb, jnp.zeros_like(qb))
                    doh = jnp.where(msk, dob, jnp.zeros_like(dob))
                    p = _softmax_rows(_dot(qh, kw, 1, 1) + b_ref[h, :, c0:KEY_WIN])
                    dp = _dot(doh, vw, 1, 1)
                    ds = p * (dp - jnp.sum(p * dp, axis=-1, keepdims=True))
                    db_ref[h, :, c0:KEY_WIN] += ds
                    dsb = ds.astype(BF16)
                    dq_acc = jnp.where(msk, _dot(dsb, kw, 1, 0), dq_acc)
                    dk_acc = jnp.where(msk, _dot(dsb, qb, 0, 0), dk_acc)
                    dv_acc = jnp.where(msk, _dot(p.astype(BF16), dob, 0, 0), dv_acc)
                dqn_s[r0:r0 + Q_BLOCK, :] = dq_acc * ATTN_SCALE
                dkn_s[k0:k0 + width, :] += dk_acc
                dv_s[k0:k0 + width, :] += dv_acc
            dq, dgq = _head_norm_bwd(dqn_s[...], qhat, rq, gq_ref[...], masks)
            dk, dgk = _head_norm_bwd(dkn_s[...], khat, rk, gk_ref[...], masks)
            dz_ref[...] = dq.astype(dz_ref.dtype)
            dk_o[...] = dk.astype(dk_o.dtype)
            dv_o[...] = dv_s[...].astype(dv_o.dtype)
            dgq_ref[...] += dgq
            dgk_ref[...] += dgk

        @pl.when(which == 1)
        def _():
            dz_ref[...] = dk_o[...]

        @pl.when(which == 2)
        def _():
            dz_ref[...] = dv_o[...]

    def ahead(p, b, t):
        nb = b + jnp.where(t > 0, 1, 0)
        wrap = jnp.where(nb >= batch, 1, 0)
        return jnp.minimum(p + wrap, pairs - 1), nb - wrap * batch

    def col(off):
        def index(p, b, t):
            np_, nb = ahead(p, b, t)
            return nb, off + np_
        return pl.BlockSpec((seq, LANES), index)

    vec = pl.BlockSpec((1, LANES), lambda p, b, t: (0, 0))
    blk = pl.BlockSpec((2, Q_BLOCK, KEY_WIN), lambda p, b, t: (p, 0, 0))
    blk_in = pl.BlockSpec((2, Q_BLOCK, KEY_WIN), lambda p, b, t: (ahead(p, b, t)[0], 0, 0))
    v_shape = jax.ShapeDtypeStruct((1, LANES), F32)
    return pl.pallas_call(
        body,
        out_shape=(jax.ShapeDtypeStruct(z.shape, BF16), jax.ShapeDtypeStruct(bias.shape, F32), v_shape, v_shape),
        grid=(pairs, batch, 3),
        in_specs=[col(0), col(pairs), col(2 * pairs), col(0), vec, vec, blk_in],
        out_specs=(pl.BlockSpec((seq, LANES), lambda p, b, t: (b, t * pairs + p)), blk, vec, vec),
        scratch_shapes=[pltpu.VMEM((seq, LANES), BF16), pltpu.VMEM((seq, LANES), BF16),
                        pltpu.VMEM((seq, LANES), F32), pltpu.VMEM((seq, LANES), F32), pltpu.VMEM((seq, LANES), F32),
                        pltpu.VMEM((seq, LANES), BF16), pltpu.VMEM((seq, LANES), BF16)],
        compiler_params=_params("arbitrary", "arbitrary", "arbitrary"), name="attn_bwd")(
            z, z, z, dcat, gq2, gk2, bias)


MEM_ROWS = 512


def memattn_fwd(z, kv, gq4, gk4, cat, batch, seq, qcol, name):
    mtok = kv.shape[0] // batch
    rows = min(MEM_ROWS, seq)

    def body(q_ref, kv_ref, gq_ref, gk_ref, cat_ref, o_ref):
        del cat_ref
        masks = _group_masks(MEM_WIDTH)
        kn = _head_norm(kv_ref[:, 0:MEM_WIDTH], gk_ref[...], masks)[0].astype(BF16)
        vm = kv_ref[:, MEM_WIDTH:2 * MEM_WIDTH].astype(BF16)
        for t in range(seq // rows):
            sl = slice(t * rows, (t + 1) * rows)
            qs = (_head_norm(q_ref[sl, :].astype(F32), gq_ref[...], masks)[0] * ATTN_SCALE).astype(BF16)
            out = jnp.zeros((rows, MEM_WIDTH), F32)
            for msk in masks:
                qh = jnp.where(msk, qs, jnp.zeros_like(qs))
                p = _softmax_rows(_dot(qh, kn, 1, 1)).astype(BF16)
                out = jnp.where(msk, _dot(p, vm, 1, 0), out)
            o_ref[sl, :] = out.astype(o_ref.dtype)

    vec = pl.BlockSpec((1, MEM_WIDTH), lambda b: (0, 0))
    return pl.pallas_call(
        body, out_shape=jax.ShapeDtypeStruct(cat.shape, cat.dtype), grid=(batch,),
        in_specs=[pl.BlockSpec((seq, MEM_WIDTH), lambda b: (b, qcol)),
                  pl.BlockSpec((mtok, 2 * MEM_WIDTH), lambda b: (b, 0)), vec, vec, ANY],
        out_specs=pl.BlockSpec((seq, MEM_WIDTH), lambda b: (b, TOK_WIDTH // MEM_WIDTH)),
        input_output_aliases={4: 0},
        compiler_params=_params("parallel"), name=name)(z, kv, gq4, gk4, cat)


def memattn_bwd(z, kv, dcat, gq4, gk4, dz, batch, seq, qcol, name):
    mtok = kv.shape[0] // batch
    rows = min(MEM_ROWS, seq)

    def body(q_ref, kv_ref, do_ref, gq_ref, gk_ref, dz_in_ref, dq_ref, dkv_ref, dgq_ref, dgk_ref):
        del dz_in_ref
        @pl.when(pl.program_id(0) == 0)
        def _():
            dgq_ref[...] = jnp.zeros_like(dgq_ref)
            dgk_ref[...] = jnp.zeros_like(dgk_ref)

        masks = _group_masks(MEM_WIDTH)
        kn_f, khat, rk = _head_norm(kv_ref[:, 0:MEM_WIDTH], gk_ref[...], masks)
        kn = kn_f.astype(BF16)
        vm = kv_ref[:, MEM_WIDTH:2 * MEM_WIDTH].astype(BF16)
        dkn = jnp.zeros((mtok, MEM_WIDTH), F32)
        dvm = jnp.zeros((mtok, MEM_WIDTH), F32)
        dgq = jnp.zeros((1, MEM_WIDTH), F32)
        for t in range(seq // rows):
            sl = slice(t * rows, (t + 1) * rows)
            qn_f, qhat, rq = _head_norm(q_ref[sl, :].astype(F32), gq_ref[...], masks)
            qs = (qn_f * ATTN_SCALE).astype(BF16)
            dob = do_ref[sl, :]
            dqn = jnp.zeros((rows, MEM_WIDTH), F32)
            for msk in masks:
                qh = jnp.where(msk, qs, jnp.zeros_like(qs))
                doh = jnp.where(msk, dob, jnp.zeros_like(dob))
                p = _softmax_rows(_dot(qh, kn, 1, 1))
                dp = _dot(doh, vm, 1, 1)
                ds = p * (dp - jnp.sum(p * dp, axis=-1, keepdims=True))
                dsb = ds.astype(BF16)
                dqn = jnp.where(msk, _dot(dsb, kn, 1, 0), dqn)
                dkn = dkn + jnp.where(msk, _dot(dsb, qs, 0, 0), 0.0)
                dvm = dvm + jnp.where(msk, _dot(p.astype(BF16), dob, 0, 0), 0.0)
            dq, dg = _head_norm_bwd(dqn * ATTN_SCALE, qhat, rq, gq_ref[...], masks)
            dq_ref[sl, :] = dq.astype(dq_ref.dtype)
            dgq = dgq + dg
        dk, dgk = _head_norm_bwd(dkn, khat, rk, gk_ref[...], masks)
        dkv_ref[:, 0:MEM_WIDTH] = dk
        dkv_ref[:, MEM_WIDTH:2 * MEM_WIDTH] = dvm
        dgq_ref[...] += dgq
        dgk_ref[...] += dgk

    vec = pl.BlockSpec((1, MEM_WIDTH), lambda b: (0, 0))
    kv_spec = pl.BlockSpec((mtok, 2 * MEM_WIDTH), lambda b: (b, 0))
    v_shape = jax.ShapeDtypeStruct((1, MEM_WIDTH), F32)
    q_spec = pl.BlockSpec((seq, MEM_WIDTH), lambda b: (b, qcol))
    return pl.pallas_call(
        body,
        out_shape=(jax.ShapeDtypeStruct(dz.shape, dz.dtype), jax.ShapeDtypeStruct(kv.shape, F32), v_shape, v_shape),
        grid=(batch,),
        in_specs=[q_spec, kv_spec, pl.BlockSpec((seq, MEM_WIDTH), lambda b: (b, TOK_WIDTH // MEM_WIDTH)), vec, vec, ANY],
        out_specs=(q_spec, kv_spec, vec, vec),
        input_output_aliases={5: 0},
        compiler_params=_params("arbitrary"), name=name)(z, kv, dcat, gq4, gk4, dz)


def mem_prep(mem, gain, wkv, name):
    t, d = mem.shape

    def body(m_ref, g_ref, w_ref, n_ref, kv_ref):
        mv = m_ref[...]
        r = lax.rsqrt(jnp.mean(mv * mv, axis=-1, keepdims=True) + NORM_EPS)
        nv = (mv * r * g_ref[...]).astype(BF16)
        n_ref[...] = nv
        kv_ref[...] = _dot(nv, w_ref[...], 1, 0)

    vmem = pl.BlockSpec(memory_space=pltpu.VMEM)
    return pl.pallas_call(
        body, out_shape=(jax.ShapeDtypeStruct((t, d), BF16), jax.ShapeDtypeStruct((t, wkv.shape[1]), F32)),
        in_specs=[vmem, vmem, vmem], out_specs=(vmem, vmem),
        compiler_params=pltpu.CompilerParams(vmem_limit_bytes=VMEM_LIMIT), name=name)(mem, gain, wkv)


def mem_norm_grad(dkv, wkv, mem, name):
    t, d = mem.shape

    def body(dkv_ref, w_ref, m_ref, dg_ref):
        dn = _dot(dkv_ref[...].astype(BF16), w_ref[...], 1, 1)
        mv = m_ref[...]
        r = lax.rsqrt(jnp.mean(mv * mv, axis=-1, keepdims=True) + NORM_EPS)
        dg_ref[...] = jnp.sum(dn * (mv * r), axis=0, keepdims=True)

    vmem = pl.BlockSpec(memory_space=pltpu.VMEM)
    return pl.pallas_call(
        body, out_shape=jax.ShapeDtypeStruct((1, d), F32), in_specs=[vmem, vmem, vmem], out_specs=vmem,
        compiler_params=pltpu.CompilerParams(vmem_limit_bytes=VMEM_LIMIT), name=name)(dkv, wkv, mem)


CONV_ROWS = 256


def _glu(a_ref, g_ref):
    return a_ref[...].astype(F32) * _sigmoid(g_ref[...].astype(F32))


def _layer_norm_stats(y):
    mu = jnp.mean(y, axis=-1, keepdims=True)
    yc = y - mu
    rstd = lax.rsqrt(jnp.mean(yc * yc, axis=-1, keepdims=True) + NORM_EPS)
    return yc * rstd, rstd


CONV_WIN = CONV_HALO + CONV_ROWS
SUBLANES = 8
SHIFT_ROWS = CONV_WIN - SUBLANES


def _preshift(win, shifted):
    for s in range(1, SUBLANES):
        shifted[s - 1, :, :] = win[s:s + SHIFT_ROWS, :]


TAP_ROWS = 64
TAP_TILES = [(r0, slice(c0, c0 + LANES)) for c0 in range(0, TOK_WIDTH, LANES) for r0 in range(0, CONV_ROWS, TAP_ROWS)]


def _tap(win, shifted, off, r0, lanes):
    s = off % SUBLANES
    base = off - s + r0
    if s == 0:
        return win[base:base + TAP_ROWS, lanes]
    return shifted[s - 1, base:base + TAP_ROWS, lanes]


def _fold_rows(x):
    return jnp.sum(x.reshape(TAP_ROWS // SUBLANES, SUBLANES, LANES), axis=0)


def conv_fwd(z, cw, cb, lg, lb, batch, seq):
    n = z.shape[0]
    nt = seq // CONV_ROWS
    sub = CONV_ROWS // CONV_HALO
    lead = CONV_HALO - (CONV_W - 1)

    def body(a_ref, g_ref, ap_ref, gp_ref, cw_ref, cb_ref, lg_ref, lb_ref, o_ref, y_ref, win, shifted):
        first = pl.program_id(1) == 0
        win[0:CONV_HALO, :] = jnp.where(first, 0.0, _glu(ap_ref, gp_ref))
        win[CONV_HALO:CONV_WIN, :] = _glu(a_ref, g_ref)
        _preshift(win, shifted)
        for r0, lanes in TAP_TILES:
            acc = jnp.zeros((TAP_ROWS, LANES), F32) + cb_ref[:, lanes]
            for w in range(CONV_W):
                acc = acc + _tap(win, shifted, lead + w, r0, lanes) * cw_ref[w:w + 1, lanes]
            y_ref[r0:r0 + TAP_ROWS, lanes] = acc
        yh, _ = _layer_norm_stats(y_ref[...])
        t = yh * lg_ref[...] + lb_ref[...]
        o_ref[...] = (t * _sigmoid(t)).astype(o_ref.dtype)

    def cur(c):
        return pl.BlockSpec((CONV_ROWS, TOK_WIDTH), lambda b, i: (b * nt + i, c))

    def prev(c):
        return pl.BlockSpec((CONV_HALO, TOK_WIDTH), lambda b, i: (jnp.maximum((b * nt + i) * sub - 1, 0), c))

    vec = pl.BlockSpec((1, TOK_WIDTH), lambda b, i: (0, 0))
    return pl.pallas_call(
        body, out_shape=(jax.ShapeDtypeStruct((n, D_MODEL), BF16), jax.ShapeDtypeStruct((n, TOK_WIDTH), F32)),
        grid=(batch, nt),
        in_specs=[cur(0), cur(1), prev(0), prev(1), pl.BlockSpec((32, TOK_WIDTH), lambda b, i: (0, 0)), vec, vec, vec],
        out_specs=(cur(0), cur(0)),
        scratch_shapes=[pltpu.VMEM((CONV_WIN, TOK_WIDTH), F32), pltpu.VMEM((SUBLANES - 1, SHIFT_ROWS, TOK_WIDTH), F32)],
        compiler_params=_params("parallel", "arbitrary"), name="conv_fwd")(z, z, z, z, cw, cb, lg, lb)


def conv_bwd(z, y, dcat, cw, lg, lb, batch, seq):
    n = z.shape[0]
    nt = seq // CONV_ROWS
    sub = CONV_ROWS // CONV_HALO
    lead = CONV_HALO - (CONV_W - 1)
    last_blk = n // CONV_HALO - 1

    def body(a_ref, g_ref, ap_ref, gp_ref, y_ref, yn_ref, do_ref, don_ref, cw_ref, lg_ref, lb_ref,
             dz_ref, dcw_ref, dsm_ref, win, shifted, dyw, dshifted, dg_o):
        b, i, which = pl.program_id(0), pl.program_id(1), pl.program_id(2)

        @pl.when(which == 0)
        def _():
            first, last = i == 0, i == nt - 1

            @pl.when((b == 0) & (i == 0))
            def _():
                dcw_ref[...] = jnp.zeros_like(dcw_ref)
                dsm_ref[...] = jnp.zeros_like(dsm_ref)

            win[0:CONV_HALO, :] = jnp.where(first, 0.0, _glu(ap_ref, gp_ref))
            win[CONV_HALO:CONV_WIN, :] = _glu(a_ref, g_ref)
            _preshift(win, shifted)
            yv = jnp.concatenate([y_ref[...], yn_ref[...]], axis=0)
            yh, rstd = _layer_norm_stats(yv)
            t = yh * lg_ref[...] + lb_ref[...]
            st = _sigmoid(t)
            dout = jnp.concatenate(
                [do_ref[...].astype(F32), jnp.where(last, 0.0, don_ref[...].astype(F32))], axis=0)
            dt = dout * st * (1.0 + t * (1.0 - st))
            dyh = dt * lg_ref[...]
            dy = rstd * (dyh - jnp.mean(dyh, axis=-1, keepdims=True)
                         - yh * jnp.mean(dyh * yh, axis=-1, keepdims=True))
            dyw[...] = dy
            _preshift(dyw, dshifted)
            dsm_ref[0:1, :] += jnp.sum(dy[0:CONV_ROWS], axis=0, keepdims=True)
            dsm_ref[1:2, :] += jnp.sum((dt * yh)[0:CONV_ROWS], axis=0, keepdims=True)
            dsm_ref[2:3, :] += jnp.sum(dt[0:CONV_ROWS], axis=0, keepdims=True)
            for c0 in range(0, TOK_WIDTH, LANES):
                lanes = slice(c0, c0 + LANES)
                dcw_acc = [jnp.zeros((SUBLANES, LANES), F32) for _ in range(CONV_W)]
                for r0 in range(0, CONV_ROWS, TAP_ROWS):
                    dyt = dyw[r0:r0 + TAP_ROWS, lanes]
                    dglu = jnp.zeros((TAP_ROWS, LANES), F32)
                    for w in range(CONV_W):
                        dcw_acc[w] = dcw_acc[w] + _fold_rows(dyt * _tap(win, shifted, lead + w, r0, lanes))
                        dglu = dglu + _tap(dyw, dshifted, CONV_W - 1 - w, r0, lanes) * cw_ref[w:w + 1, lanes]
                    avt = a_ref[r0:r0 + TAP_ROWS, lanes].astype(F32)
                    sgt = _sigmoid(g_ref[r0:r0 + TAP_ROWS, lanes].astype(F32))
                    dz_ref[r0:r0 + TAP_ROWS, lanes] = (dglu * sgt).astype(dz_ref.dtype)
                    dg_o[r0:r0 + TAP_ROWS, lanes] = (dglu * avt * sgt * (1.0 - sgt)).astype(dg_o.dtype)
                for w in range(CONV_W):
                    dcw_ref[w:w + 1, lanes] += jnp.sum(dcw_acc[w], axis=0, keepdims=True)

        @pl.when(which == 1)
        def _():
            dz_ref[...] = dg_o[...]

    def ahead(b, i, t):
        return jnp.minimum(b * nt + i + t, batch * nt - 1)

    def cur(c):
        return pl.BlockSpec((CONV_ROWS, TOK_WIDTH), lambda b, i, t: (ahead(b, i, t), c))

    def prev(c):
        return pl.BlockSpec((CONV_HALO, TOK_WIDTH), lambda b, i, t: (jnp.maximum(ahead(b, i, t) * sub - 1, 0), c))

    nxt = pl.BlockSpec((CONV_HALO, TOK_WIDTH),
                       lambda b, i, t: (jnp.minimum((ahead(b, i, t) + 1) * sub, last_blk), 0))
    vec = pl.BlockSpec((1, TOK_WIDTH), lambda b, i, t: (0, 0))
    full32 = pl.BlockSpec((32, TOK_WIDTH), lambda b, i, t: (0, 0))
    return pl.pallas_call(
        body,
        out_shape=(jax.ShapeDtypeStruct(z.shape, BF16), jax.ShapeDtypeStruct((32, TOK_WIDTH), F32),
                   jax.ShapeDtypeStruct((8, TOK_WIDTH), F32)),
        grid=(batch, nt, 2),
        in_specs=[cur(0), cur(1), prev(0), prev(1), cur(0), nxt, cur(0), nxt, full32, vec, vec],
        out_specs=(pl.BlockSpec((CONV_ROWS, TOK_WIDTH), lambda b, i, t: (b * nt + i, t)), full32,
                   pl.BlockSpec((8, TOK_WIDTH), lambda b, i, t: (0, 0))),
        scratch_shapes=[pltpu.VMEM((CONV_WIN, TOK_WIDTH), F32), pltpu.VMEM((SUBLANES - 1, SHIFT_ROWS, TOK_WIDTH), F32),
                        pltpu.VMEM((CONV_WIN, TOK_WIDTH), F32), pltpu.VMEM((SUBLANES - 1, SHIFT_ROWS, TOK_WIDTH), F32),
                        pltpu.VMEM((CONV_ROWS, TOK_WIDTH), BF16)],
        compiler_params=_params("arbitrary", "arbitrary", "arbitrary"), name="conv_bwd")(
            z, z, z, z, y, y, dcat, dcat, cw, lg, lb)


def loss_head(y, target):
    n, d = y.shape
    tm = _row_tile(n)
    nt = n // tm

    def body(y_ref, t_ref, dy_ref, dyb_ref, l_ref, acc_ref):
        i = pl.program_id(0)

        @pl.when(i == 0)
        def _():
            acc_ref[...] = jnp.zeros_like(acc_ref)

        err = y_ref[...] - t_ref[...]
        dy = err * (1.0 / d)
        dy_ref[...] = dy
        dyb_ref[...] = dy.astype(BF16)
        acc_ref[...] += jnp.sum(err * err, axis=0, keepdims=True)

        @pl.when(i == nt - 1)
        def _():
            total = jnp.sum(acc_ref[...], axis=-1, keepdims=True) * (0.5 / d)
            l_ref[...] = jnp.broadcast_to(total, l_ref.shape)

    row = pl.BlockSpec((tm, d), lambda i: (i, 0))
    return pl.pallas_call(
        body, out_shape=(jax.ShapeDtypeStruct((n, d), F32), jax.ShapeDtypeStruct((n, d), BF16),
                         jax.ShapeDtypeStruct((8, LANES), F32)), grid=(nt,),
        in_specs=[row, row], out_specs=(row, row, pl.BlockSpec((8, LANES), lambda i: (0, 0))),
        scratch_shapes=[pltpu.VMEM((1, d), F32)],
        compiler_params=_params("arbitrary"), name="loss_head")(y, target)


def col_sum(x, name="col_sum"):
    n, c = x.shape
    tm = _row_tile(n)

    def body(x_ref, o_ref):
        @pl.when(pl.program_id(0) == 0)
        def _():
            o_ref[...] = jnp.zeros_like(o_ref)

        o_ref[...] += jnp.sum(x_ref[...].astype(F32), axis=0, keepdims=True)

    return pl.pallas_call(
        body, out_shape=jax.ShapeDtypeStruct((1, c), F32), grid=(n // tm,),
        in_specs=[pl.BlockSpec((tm, c), lambda i: (i, 0))], out_specs=pl.BlockSpec((1, c), lambda i: (0, 0)),
        compiler_params=_params("arbitrary"), name=name)(x)


def adamw(w, g, m, v, name="adamw"):
    rows, cols = w.shape
    tr = rows
    for cand in (512, 256, 128, 64, 32, 16, 8):
        if rows % cand == 0 and rows > cand:
            tr = cand
            break
    c1 = 1.0 / (1.0 - ADAM_B1 ** ADAM_STEP)
    c2 = 1.0 / (1.0 - ADAM_B2 ** ADAM_STEP)

    def body(w_ref, g_ref, m_ref, v_ref, d_ref, nm_ref, nv_ref):
        gv = g_ref[...]
        nm = ADAM_B1 * m_ref[...] + (1.0 - ADAM_B1) * gv
        nv = ADAM_B2 * v_ref[...] + (1.0 - ADAM_B2) * (gv * gv)
        nm_ref[...] = nm
        nv_ref[...] = nv
        d_ref[...] = -ADAM_LR * ((nm * c1) / (jnp.sqrt(nv * c2) + ADAM_EPS) + ADAM_WD * w_ref[...])

    spec = pl.BlockSpec((tr, cols), lambda i: (i, 0))
    shape = jax.ShapeDtypeStruct((rows, cols), F32)
    return pl.pallas_call(
        body, out_shape=(shape, shape, shape), grid=(rows // tr,),
        in_specs=[spec, spec, spec, spec], out_specs=(spec, spec, spec),
        compiler_params=_params("parallel"), name=name)(w, g, m, v)


def _place():
    return lax.axis_index("x"), lax.axis_index("y"), lax.axis_index("c")


def _other_chips(x, y):
    return [(1 - x, y), (x, 1 - y), (1 - x, 1 - y)]


def small_exchange(slab, reduce):
    r = slab.shape[0]

    def body(in_ref, o_ref, *scratch):
        if reduce:
            buf, send_sems, recv_sems = scratch
        else:
            buf = o_ref
            send_sems, recv_sems = scratch
        x, y, c = _place()
        me = 4 * x + 2 * y + c
        buf[me] = in_ref[...]
        copies = []
        for k in range(1, N_DEV):
            peer = (x ^ (k >> 2), y ^ ((k >> 1) & 1), c ^ (k & 1))
            cp = pltpu.make_async_remote_copy(
                src_ref=in_ref, dst_ref=buf.at[me], send_sem=send_sems.at[k - 1], recv_sem=recv_sems.at[k - 1],
                device_id=peer, device_id_type=MESH)
            cp.start()
            copies.append(cp)
        for k in range(1, N_DEV):
            src = 4 * (x ^ (k >> 2)) + 2 * (y ^ ((k >> 1) & 1)) + (c ^ (k & 1))
            pltpu.make_async_remote_copy(
                src_ref=in_ref, dst_ref=buf.at[src], send_sem=send_sems.at[k - 1], recv_sem=recv_sems.at[k - 1],
                device_id=(x, y, c), device_id_type=MESH).wait_recv()
        for cp in copies:
            cp.wait_send()
        if reduce:
            total = buf[0]
            for d in range(1, N_DEV):
                total = total + buf[d]
            o_ref[...] = total

    sems = [pltpu.SemaphoreType.DMA((N_DEV - 1,)), pltpu.SemaphoreType.DMA((N_DEV - 1,))]
    if reduce:
        out_shape = jax.ShapeDtypeStruct((r, LANES), F32)
        scratch = [pltpu.VMEM((N_DEV, r, LANES), F32)] + sems
    else:
        out_shape = jax.ShapeDtypeStruct((N_DEV, r, LANES), F32)
        scratch = sems
    vmem = pl.BlockSpec(memory_space=pltpu.VMEM)
    return pl.pallas_call(
        body, out_shape=out_shape, in_specs=[vmem], out_specs=vmem, scratch_shapes=scratch,
        compiler_params=pltpu.CompilerParams(vmem_limit_bytes=VMEM_LIMIT),
        name="small_reduce" if reduce else "small_gather")(slab)


def reduce_small(arrays):
    na = len(arrays)

    def body(*refs):
        ins, outs, bufs = refs[:na], refs[na:2 * na], refs[2 * na:3 * na]
        send_sems, recv_sems = refs[3 * na:]
        x, y, c = _place()
        me = 4 * x + 2 * y + c
        copies = []
        for a in range(na):
            bufs[a][me] = ins[a][...]
            for k in range(1, N_DEV):
                cp = pltpu.make_async_remote_copy(
                    src_ref=ins[a], dst_ref=bufs[a].at[me], send_sem=send_sems.at[a, k - 1],
                    recv_sem=recv_sems.at[a, k - 1],
                    device_id=(x ^ (k >> 2), y ^ ((k >> 1) & 1), c ^ (k & 1)), device_id_type=MESH)
                cp.start()
                copies.append(cp)
        for a in range(na):
            for k in range(1, N_DEV):
                src = 4 * (x ^ (k >> 2)) + 2 * (y ^ ((k >> 1) & 1)) + (c ^ (k & 1))
                pltpu.make_async_remote_copy(
                    src_ref=ins[a], dst_ref=bufs[a].at[src], send_sem=send_sems.at[a, k - 1],
                    recv_sem=recv_sems.at[a, k - 1], device_id=(x, y, c), device_id_type=MESH).wait_recv()
        for cp in copies:
            cp.wait_send()
        for a in range(na):
            total = bufs[a][0]
            for dev in range(1, N_DEV):
                total = total + bufs[a][dev]
            outs[a][...] = total

    vmem = pl.BlockSpec(memory_space=pltpu.VMEM)
    return pl.pallas_call(
        body, out_shape=tuple(jax.ShapeDtypeStruct(a.shape, F32) for a in arrays),
        in_specs=[vmem] * na, out_specs=tuple([vmem] * na),
        scratch_shapes=[pltpu.VMEM((N_DEV,) + a.shape, F32) for a in arrays]
        + [pltpu.SemaphoreType.DMA((na, N_DEV - 1)), pltpu.SemaphoreType.DMA((na, N_DEV - 1))],
        compiler_params=pltpu.CompilerParams(vmem_limit_bytes=VMEM_LIMIT), name="small_reduce")(*arrays)


def adamw_small(ws, gs, ms, vs):
    na = len(ws)
    c1 = 1.0 / (1.0 - ADAM_B1 ** ADAM_STEP)
    c2 = 1.0 / (1.0 - ADAM_B2 ** ADAM_STEP)

    def body(*refs):
        w_refs, g_refs, m_refs, v_refs = (refs[i * na:(i + 1) * na] for i in range(4))
        d_refs, nm_refs, nv_refs = (refs[(4 + i) * na:(5 + i) * na] for i in range(3))
        for a in range(na):
            gv = g_refs[a][...]
            nm = ADAM_B1 * m_refs[a][...] + (1.0 - ADAM_B1) * gv
            nv = ADAM_B2 * v_refs[a][...] + (1.0 - ADAM_B2) * (gv * gv)
            nm_refs[a][...] = nm
            nv_refs[a][...] = nv
            d_refs[a][...] = -ADAM_LR * ((nm * c1) / (jnp.sqrt(nv * c2) + ADAM_EPS) + ADAM_WD * w_refs[a][...])

    vmem = pl.BlockSpec(memory_space=pltpu.VMEM)
    shapes = tuple(jax.ShapeDtypeStruct(w.shape, F32) for w in ws)
    outs = pl.pallas_call(
        body, out_shape=shapes * 3, in_specs=[vmem] * (4 * na), out_specs=tuple([vmem] * (3 * na)),
        compiler_params=pltpu.CompilerParams(vmem_limit_bytes=VMEM_LIMIT), name="adamw_small")(*ws, *gs, *ms, *vs)
    return outs[:na], outs[na:2 * na], outs[2 * na:]


def gather_weights(shards, name, collective_id):
    nw = len(shards)
    ns = [s.shape[0] for s in shards]
    in_refs = [jax.new_ref(s, memory_space=pltpu.MemorySpace.HBM) for s in shards]
    out_refs = [jax.empty_ref(jax.ShapeDtypeStruct((N_DEV * s.shape[0], s.shape[1]), s.dtype),
                              memory_space=pltpu.MemorySpace.HBM) for s in shards]

    @pl.kernel(mesh=plsc.ScalarSubcoreMesh(axis_name="seq", num_cores=1), name=name,
               scratch_types=(pltpu.SemaphoreType.DMA((nw, 7)), pltpu.SemaphoreType.DMA((nw, 7)),
                              pltpu.SemaphoreType.DMA((nw,))),
               compiler_params=pltpu.CompilerParams(collective_id=collective_id))
    def launch(send_sems, recv_sems, local_sems):
        x, y, c = _place()
        me, sib = (x, y, c), (x, y, 1 - c)
        chips = _other_chips(x, y)
        barrier = pltpu.get_barrier_semaphore()
        for peer in [sib] + [(*chip, c) for chip in chips]:
            pl.semaphore_signal(barrier, inc=1, device_id=peer, device_id_type=MESH)
        pl.semaphore_wait(barrier, 4)

        def rows(w, dev):
            return out_refs[w].at[pl.ds((4 * dev[0] + 2 * dev[1] + dev[2]) * ns[w], ns[w]), :]

        def copy(w, k, block, to, src=None):
            return pltpu.make_async_remote_copy(
                src_ref=rows(w, block) if src is None else src, dst_ref=rows(w, block),
                send_sem=send_sems.at[w, k], recv_sem=recv_sems.at[w, k], device_id=to, device_id_type=MESH)

        started, sends = [], []
        for w in range(nw):
            mine = pltpu.make_async_copy(in_refs[w], rows(w, me), local_sems.at[w])
            mine.start()
            started.append(mine)
            first = [copy(w, 0, me, sib, src=in_refs[w])]
            first += [copy(w, 1 + j, me, (*chip, c), src=in_refs[w]) for j, chip in enumerate(chips)]
            for cp in first:
                cp.start()
            sends += first
        for w in range(nw):
            for j, chip in enumerate(chips):
                copy(w, 1 + j, (*chip, c), me).wait_recv()
                fwd = copy(w, 4 + j, (*chip, c), sib)
                fwd.start()
                sends.append(fwd)
        for w in range(nw):
            copy(w, 0, sib, me).wait_recv()
            for j, chip in enumerate(chips):
                copy(w, 4 + j, (*chip, 1 - c), me).wait_recv()
        for cp in sends:
            cp.wait_send()
        for mine in started:
            mine.wait()

    launch()
    return [r[...] for r in out_refs]


def _sequencer_exchange(sources, out_rows, peers_of, copies_of, name, collective_id):
    nw = len(sources)
    in_refs = [jax.new_ref(s, memory_space=pltpu.MemorySpace.HBM) for s in sources]
    out_refs = [jax.empty_ref(jax.ShapeDtypeStruct((rows, s.shape[1]), s.dtype), memory_space=pltpu.MemorySpace.HBM)
                for rows, s in zip(out_rows, sources)]
    per = len(copies_of(0, 0, 0, 0))

    @pl.kernel(mesh=plsc.ScalarSubcoreMesh(axis_name="seq", num_cores=1), name=name,
               scratch_types=(pltpu.SemaphoreType.DMA((nw, per)), pltpu.SemaphoreType.DMA((nw, per))),
               compiler_params=pltpu.CompilerParams(collective_id=collective_id))
    def launch(send_sems, recv_sems):
        x, y, c = _place()
        peers = peers_of(x, y, c)
        barrier = pltpu.get_barrier_semaphore()
        for peer in peers:
            pl.semaphore_signal(barrier, inc=1, device_id=peer, device_id_type=MESH)
        pl.semaphore_wait(barrier, len(peers))
        copies = []
        for w in range(nw):
            for k, (src_blk, dst_blk, rows, peer) in enumerate(copies_of(x, y, c, w)):
                cp = pltpu.make_async_remote_copy(
                    src_ref=in_refs[w].at[pl.ds(src_blk * rows, rows), :],
                    dst_ref=out_refs[w].at[pl.ds(dst_blk * rows, rows), :],
                    send_sem=send_sems.at[w, k], recv_sem=recv_sems.at[w, k], device_id=peer, device_id_type=MESH)
                cp.start()
                copies.append(cp)
        for cp in copies:
            cp.wait_recv()
        for cp in copies:
            cp.wait_send()

    launch()
    return [r[...] for r in out_refs]


def scatter_to_sibling(grads, name, collective_id):
    ns = [g.shape[0] // N_DEV for g in grads]
    return _sequencer_exchange(
        grads, [4 * n for n in ns],
        lambda x, y, c: [(x, y, 1 - c)],
        lambda x, y, c, w: [(2 * q + 1 - c, q, ns[w], (x, y, 1 - c)) for q in range(4)],
        name, collective_id)


def scatter_to_chips(parts, name, collective_id):
    ns = [p.shape[0] // 4 for p in parts]
    return _sequencer_exchange(
        parts, [3 * n for n in ns],
        lambda x, y, c: [(*chip, c) for chip in _other_chips(x, y)],
        lambda x, y, c, w: [(2 * chip[0] + chip[1], j, ns[w], (*chip, c)) for j, chip in enumerate(_other_chips(x, y))],
        name, collective_id)


def add_sibling(grads, landeds, core, name):
    nw = len(grads)

    def body(c_ref, *refs):
        for w in range(nw):
            g_ref, l_ref, o_ref = refs[2 * w], refs[2 * w + 1], refs[2 * nw + w]
            o_ref[...] = (g_ref[...].astype(F32) + l_ref[...].astype(F32)).astype(o_ref.dtype)

    in_specs, out_specs, args = [], [], []
    for g, ld in zip(grads, landeds):
        n, cols = ld.shape[0] // 4, g.shape[1]
        in_specs += [pl.BlockSpec((n, cols), lambda q, c_ref: (2 * q + c_ref[0], 0)),
                     pl.BlockSpec((n, cols), lambda q, c_ref: (q, 0))]
        out_specs.append(pl.BlockSpec((n, cols), lambda q, c_ref: (q, 0)))
        args += [g, ld]
    grid_spec = pltpu.PrefetchScalarGridSpec(
        num_scalar_prefetch=1, grid=(4,), in_specs=in_specs, out_specs=tuple(out_specs))
    return pl.pallas_call(
        body, out_shape=tuple(jax.ShapeDtypeStruct(ld.shape, ld.dtype) for ld in landeds), grid_spec=grid_spec,
        compiler_params=_params("arbitrary"), name=name)(core, *args)


def adamw_shard(layer, w, m, v, part, landed, chip, earlier, name):
    n = landed.shape[0] // 3
    cols = w.shape[1]
    c1 = 1.0 / (1.0 - ADAM_B1 ** ADAM_STEP)
    c2 = 1.0 / (1.0 - ADAM_B2 ** ADAM_STEP)

    def body(q_ref, w_ref, m_ref, v_ref, p_ref, l0_ref, l1_ref, l2_ref, *rest):
        g_ref, d_ref, nm_ref, nv_ref = rest[-4:]
        gv = ((p_ref[...].astype(F32) + l0_ref[...].astype(F32)) + l1_ref[...].astype(F32)) + l2_ref[...].astype(F32)
        nm = ADAM_B1 * m_ref[...] + (1.0 - ADAM_B1) * gv
        nv = ADAM_B2 * v_ref[...] + (1.0 - ADAM_B2) * (gv * gv)
        g_ref[...] = gv
        nm_ref[...] = nm
        nv_ref[...] = nv
        d_ref[...] = -ADAM_LR * ((nm * c1) / (jnp.sqrt(nv * c2) + ADAM_EPS) + ADAM_WD * w_ref[...])

    sub = 2 if n % (2 * 16) == 0 else 1
    rows = n // sub
    own = pl.BlockSpec((rows, cols), lambda i, q_ref: (layer * sub + i, 0))

    def landed_spec(j):
        return pl.BlockSpec((rows, cols), lambda i, q_ref: (j * sub + i, 0))

    in_specs = [own, own, own, pl.BlockSpec((rows, cols), lambda i, q_ref: (q_ref[0] * sub + i, 0)),
                landed_spec(0), landed_spec(1), landed_spec(2)]
    args = [chip, w, m, v, part, landed, landed, landed]
    aliases = {}
    if earlier is not None:
        in_specs += [ANY] * 4
        args += list(earlier)
        aliases = {8 + k: k for k in range(4)}
    grid_spec = pltpu.PrefetchScalarGridSpec(
        num_scalar_prefetch=1, grid=(sub,), in_specs=in_specs, out_specs=(own, own, own, own))
    shape = jax.ShapeDtypeStruct(w.shape, F32)
    return pl.pallas_call(
        body, out_shape=(shape, shape, shape, shape), grid_spec=grid_spec, input_output_aliases=aliases,
        compiler_params=_params("arbitrary"), name=name)(*args)


def _pack(arrays):
    flat = jnp.concatenate([a.reshape(-1).astype(F32) for a in arrays])
    pad = (-flat.shape[0]) % (8 * LANES)
    return jnp.pad(flat, (0, pad)).reshape(-1, LANES)


def _unpack(slab, shapes):
    flat = slab.reshape(slab.shape[:-2] + (-1,))
    out, off = [], 0
    for shp in shapes:
        size = 1
        for s in shp:
            size *= s
        out.append(flat[..., off:off + size].reshape(flat.shape[:-1] + tuple(shp)))
        off += size
    return out


def kernel(x, mem, norm1_g, mem_norm_g, a_w_in, a_q_g, a_k_g, a_rel_bias, b_w_in, b_b_in, b_conv_w, b_conv_b, b_ln_g, b_ln_b, mq_g, mk_g, w_mem_kv, w_out, norm2_g, w_gate, w_up, w_down, loss_target, m_norm1_g, m_mem_norm_g, m_a_w_in, m_a_q_g, m_a_k_g, m_a_rel_bias, m_b_w_in, m_b_b_in, m_b_conv_w, m_b_conv_b, m_b_ln_g, m_b_ln_b, m_mq_g, m_mk_g, m_w_mem_kv, m_w_out, m_norm2_g, m_w_gate, m_w_up, m_w_down, v_norm1_g, v_mem_norm_g, v_a_w_in, v_a_q_g, v_a_k_g, v_a_rel_bias, v_b_w_in, v_b_b_in, v_b_conv_w, v_b_conv_b, v_b_ln_g, v_b_ln_b, v_mq_g, v_mk_g, v_w_mem_kv, v_w_out, v_norm2_g, v_w_gate, v_w_up, v_w_down):
    batch, seq, d = x.shape
    mtok = mem.shape[1]
    n = batch * seq
    ax, ay, ac = _place()
    me = 4 * ax + 2 * ay + ac
    core_arr = jnp.reshape(ac, (1,)).astype(jnp.int32)
    chip_arr = jnp.reshape(2 * ax + ay, (1,)).astype(jnp.int32)

    def t_bf16(w):
        return jnp.transpose(w).astype(BF16)

    def after(value, *earlier):
        return lax.optimization_barrier((value, *earlier))[0]

    def gather_mix(l, when, name, collective_id):
        srcs = [w_mem_kv[l].astype(BF16), w_out[l].astype(BF16)] + ([t_bf16(b_w_in[0])] if l == 1 else [])
        return gather_weights([after(srcs[0], when)] + srcs[1:], name, collective_id)

    def gather_ffn(l, when, name, collective_id):
        return gather_weights(
            [after(t_bf16(w_gate[l]), when), t_bf16(w_up[l]), w_down[l].astype(BF16)], name, collective_id)

    f_loc = b_b_in.shape[1]
    c_loc = b_conv_b.shape[1]

    def two(g):
        return jnp.concatenate([g, g], axis=-1)

    gq2, gk2 = two(a_q_g), two(a_k_g)
    rel16 = jnp.pad(a_rel_bias[0], ((0, 16 - a_rel_bias.shape[1]), (0, 0)))
    bias = bias_blocks(rel16)

    x0 = x.reshape(n, d)
    mem2 = mem.reshape(batch * mtok, d)

    saved = []
    xin = x0
    a_win_t, = gather_weights([t_bf16(a_w_in[0])], "gather_in_a", 1)
    wg_t, wu_t, wd, wo, wkv = [None] * 2, [None] * 2, [None] * 2, [None] * 2, [None] * 2
    h = after(rms_fwd(xin, norm1_g[0:1], name="rms1_fwd_0"), bias)
    target = loss_target.reshape(n, d)
    for l in range(2):
        gq4 = jnp.tile(mq_g[l:l + 1], (1, 4))
        gk4 = jnp.tile(mk_g[l:l + 1], (1, 4))
        y_conv = None
        if l == 0:
            wkv[0], wo[0] = gather_mix(0, h, "gather_mix_a", 2)
            z = mm_nt(h, a_win_t, name="in_proj_a")
            wg_t[0], wu_t[0], wd[0] = gather_ffn(0, z, "gather_ffn_a", 3)
            cat = attn_fwd(z, gq2, gk2, bias, batch, seq)
            wkv[1], wo[1], b_win_t = gather_mix(1, cat, "gather_mix_b", 4)
            qcol = 3 * TOK_WIDTH // MEM_WIDTH
        else:
            small_shapes = [(f_loc,), (CONV_W, c_loc), (c_loc,), (c_loc,), (c_loc,)]
            gathered = small_exchange(after(_pack([b_b_in, b_conv_w, b_conv_b, b_ln_g, b_ln_b]), xin), reduce=False)
            bb_g, cw_g, cb_g, lg_g, lb_g = _unpack(gathered, small_shapes)
            bb_full = bb_g.reshape(1, -1)
            cw_full = jnp.pad(jnp.transpose(cw_g, (1, 0, 2)).reshape(CONV_W, -1), ((0, 32 - CONV_W), (0, 0)))
            cb_full, lg_full, lb_full = cb_g.reshape(1, -1), lg_g.reshape(1, -1), lb_g.reshape(1, -1)
            z = mm_nt(h, b_win_t, bias=bb_full, name="in_proj_b")
            cat, y_conv = conv_fwd(z, cw_full, cb_full, lg_full, lb_full, batch, seq)
            qcol = 2 * TOK_WIDTH // MEM_WIDTH
        mem_n, kv = mem_prep(mem2, mem_norm_g[l:l + 1], wkv[l], name=f"mem_prep_{l}")
        cat = memattn_fwd(z, kv, gq4, gk4, cat, batch, seq, qcol, name=f"memattn_fwd_{l}")
        x1, h2 = proj_norm(cat, wo[l], xin, norm2_g[l:l + 1], name=f"out_proj_{l}")
        if l == 0:
            wg_t[1], wu_t[1], wd[1] = gather_ffn(1, x1, "gather_ffn_b", 5)
        if l == 0:
            gate, up, act, x2, h_next = ffn_fwd(h2, wg_t[0], wu_t[0], wd[0], x1, gain=norm1_g[1:2], name="ffn_fwd_0")
        else:
            gate, up, act, dx_b, loss_blk = ffn_fwd(h2, wg_t[1], wu_t[1], wd[1], x1, target=target, name="ffn_fwd_1")
        saved.append(dict(xin=xin, h=h, mem_n=mem_n, kv=kv, gq4=gq4, gk4=gk4, z=z, qcol=qcol, cat=cat, x1=x1, h2=h2,
                          gate=gate, up=up, act=act, y_conv=y_conv))
        if l == 0:
            xin, h = x2, h_next

    big = {}
    small = {}
    reduced = {}
    groups = 0

    def scatter_siblings(keys):
        nonlocal groups
        gid = groups
        groups += 1
        return gid, keys, scatter_to_sibling([big[k] for k in keys], f"scatter_sibling_{gid}", 8 + 2 * gid)

    def scatter_chips(stage1, when):
        gid, keys, landed1 = stage1
        parts = add_sibling([after(big[keys[0]], when)] + [big[k] for k in keys[1:]], landed1, core_arr,
                            name=f"add_sibling_{gid}")
        landed2 = scatter_to_chips(parts, f"scatter_chips_{gid}", 9 + 2 * gid)
        for k, p, ld in zip(keys, parts, landed2):
            reduced[k] = (p, ld)
        return parts, landed2

    def rows_of(w, transposed):
        w = jnp.swapaxes(w, 1, 2) if transposed else w
        return w.reshape(w.shape[0] * w.shape[1], w.shape[2])

    sharded = {
        "win0": (2, True), "win1": (6, True), "wkv": (14, False), "wo": (15, False),
        "wg": (17, True), "wu": (18, True), "wd": (19, False)}
    weights = [norm1_g, mem_norm_g, a_w_in, a_q_g, a_k_g, a_rel_bias, b_w_in, b_b_in, b_conv_w, b_conv_b, b_ln_g,
               b_ln_b, mq_g, mk_g, w_mem_kv, w_out, norm2_g, w_gate, w_up, w_down]
    moms = [m_norm1_g, m_mem_norm_g, m_a_w_in, m_a_q_g, m_a_k_g, m_a_rel_bias, m_b_w_in, m_b_b_in, m_b_conv_w,
            m_b_conv_b, m_b_ln_g, m_b_ln_b, m_mq_g, m_mk_g, m_w_mem_kv, m_w_out, m_norm2_g, m_w_gate, m_w_up, m_w_down]
    vels = [v_norm1_g, v_mem_norm_g, v_a_w_in, v_a_q_g, v_a_k_g, v_a_rel_bias, v_b_w_in, v_b_b_in, v_b_conv_w,
            v_b_conv_b, v_b_ln_g, v_b_ln_b, v_mq_g, v_mk_g, v_w_mem_kv, v_w_out, v_norm2_g, v_w_gate, v_w_up, v_w_down]
    updated = {}

    def update_layer(l, when):
        for key, (idx, transposed) in sharded.items():
            if key in ("win0", "win1"):
                if key != f"win{l}":
                    continue
                layer, rkey = 0, key
            else:
                layer, rkey = l, f"{key}{l}"
            part, landed = reduced[rkey]
            updated[key] = adamw_shard(
                layer, after(rows_of(weights[idx], transposed), when), rows_of(moms[idx], transposed),
                rows_of(vels[idx], transposed), part, landed, chip_arr, updated.get(key), name=f"adamw_{rkey}")

    mix_landed = None
    for l in (1, 0):
        sv = saved[l]
        dgate, dup, dx1_b, dcat, small[f"norm2_{l}"] = ffn_bwd(
            dx_b, wd[l], sv["gate"], sv["up"], wg_t[l], wu_t[l], sv["x1"], norm2_g[l:l + 1], wo[l], name=f"ffn_bwd_{l}")
        if l == 0:
            dgate = after(dgate, *mix_landed)
            update_layer(1, dx1_b)
        big[f"wg{l}"], big[f"wu{l}"], big[f"wd{l}"] = ffn_weight_grads(
            dgate, dup, sv["h2"], sv["act"], dx_b, name=f"grad_ffn_{l}")
        stage1 = scatter_siblings([f"wd{l}", f"wg{l}", f"wu{l}"])
        big[f"wo{l}"] = mm_tn(sv["cat"], dx1_b, name=f"grad_wo_{l}")
        parts, ffn_landed = scatter_chips(stage1, big[f"wo{l}"])
        dcat = after(dcat, *parts)
        if l == 0:
            dz, dbias, small["a_q"], small["a_k"] = attn_bwd(sv["z"], dcat, gq2, gk2, bias, batch, seq)
            small["rel"] = bias_grad(dbias)
            win_t = a_win_t
        else:
            dz, small["cw"], small["csum"] = conv_bwd(sv["z"], sv["y_conv"], dcat, cw_full, lg_full, lb_full, batch, seq)
            win_t = b_win_t
        dz = after(dz, *ffn_landed)
        dz, dkv, small[f"mq_{l}"], small[f"mk_{l}"] = memattn_bwd(
            sv["z"], sv["kv"], dcat, sv["gq4"], sv["gk4"], dz, batch, seq, sv["qcol"], name=f"memattn_bwd_{l}")
        big[f"win{l}"] = mm_tn(dz, sv["h"], name=f"grad_win_{l}")
        big[f"wkv{l}"] = mm_tn(sv["mem_n"], dkv, name=f"grad_wkv_{l}")
        stage1 = scatter_siblings([f"win{l}", f"wkv{l}", f"wo{l}"])
        dx_b, small[f"norm1_{l}"], dz_sum = in_proj_bwd(
            dz, win_t, sv["xin"], norm1_g[l:l + 1], dx1_b, BF16 if l == 1 else F32, name=f"in_proj_bwd_{l}")
        if l == 1:
            small["bb"] = dz_sum
        parts, mix_landed = scatter_chips(stage1, dx_b)
        dx_b = after(dx_b, *parts)
        small[f"memnorm_{l}"] = mem_norm_grad(dkv, wkv[l], mem2, name=f"mem_norm_grad_{l}")
    grad_x = dx_b.reshape(batch, seq, d)
    update_layer(0, dx_b)

    def shaped(rows, idx, transposed):
        shp = weights[idx].shape
        if transposed:
            return jnp.swapaxes(rows.reshape(shp[0], shp[2], shp[1]), 1, 2)
        return rows.reshape(shp)

    def fold(v, groups):
        return jnp.sum(v.reshape(groups, HEAD_DIM), axis=0, keepdims=True)

    heads = a_rel_bias.shape[1]
    small_list = [
        jnp.concatenate([small["norm1_0"], small["norm1_1"]]),
        jnp.concatenate([small["memnorm_0"], small["memnorm_1"]]),
        fold(small["a_q"], 2), fold(small["a_k"], 2), small["rel"][:heads][None],
        small["bb"], small["cw"][:CONV_W][None], small["csum"][0:1], small["csum"][1:2], small["csum"][2:3],
        jnp.concatenate([fold(small["mq_0"], 4), fold(small["mq_1"], 4)]),
        jnp.concatenate([fold(small["mk_0"], 4), fold(small["mk_1"], 4)]),
        jnp.concatenate([small["norm2_0"], small["norm2_1"]]),
    ]
    (g_norm1, g_memnorm, g_aq, g_ak, g_rel, g_bb_full, g_cw_full, g_cb_full, g_lg_full, g_lb_full,
     g_mq, g_mk, g_norm2, loss_sum) = reduce_small(small_list + [loss_blk])
    loss = loss_sum[0, 0]
    g_bb = lax.dynamic_slice_in_dim(g_bb_full, me * f_loc, f_loc, axis=1)
    g_cw = lax.dynamic_slice_in_dim(g_cw_full, me * c_loc, c_loc, axis=2)
    g_cb = lax.dynamic_slice_in_dim(g_cb_full, me * c_loc, c_loc, axis=1)
    g_lg = lax.dynamic_slice_in_dim(g_lg_full, me * c_loc, c_loc, axis=1)
    g_lb = lax.dynamic_slice_in_dim(g_lb_full, me * c_loc, c_loc, axis=1)

    grads = [g_norm1, g_memnorm, None, g_aq, g_ak, g_rel, None, g_bb, g_cw, g_cb, g_lg, g_lb,
             g_mq, g_mk, None, None, g_norm2, None, None, None]
    deltas, new_m, new_v = [None] * 20, [None] * 20, [None] * 20
    for key, (idx, transposed) in sharded.items():
        grads[idx], deltas[idx], new_m[idx], new_v[idx] = (shaped(r, idx, transposed) for r in updated[key])

    small_idx = [i for i in range(20) if i not in {idx for idx, _ in sharded.values()}]
    dl, nm, nv = adamw_small([weights[i] for i in small_idx], [grads[i] for i in small_idx],
                             [moms[i] for i in small_idx], [vels[i] for i in small_idx])
    for i, a, b, cc in zip(small_idx, dl, nm, nv):
        deltas[i], new_m[i], new_v[i] = a, b, cc

    return (loss, grad_x, *grads, *deltas, *new_m, *new_v)
```

```python
import functools

import jax
import jax.numpy as jnp
from jax import lax
from jax.experimental import pallas as pl
from jax.experimental.pallas import tpu as pltpu
from jax.experimental.pallas import tpu_sc as plsc

F32 = jnp.float32
BF16 = jnp.bfloat16
HIGHEST = lax.Precision.HIGHEST
MESH = pl.DeviceIdType.MESH
ANY = pl.BlockSpec(memory_space=pl.ANY)

N_DEV = 8
D_MODEL = 1024
HEAD_DIM = 64
TOK_WIDTH = 768
MEM_WIDTH = 256
CHUNK = 64
Q_BLOCK = 256
KEY_WIN = 768
BAND = 576
N_REL = 192
CONV_W = 31
CONV_HALO = 32
NORM_EPS = 1e-6
NEG_INF = -1e30
ATTN_SCALE = HEAD_DIM ** -0.5
LANES = 128
ROW_TILE = 512
VMEM_LIMIT = 56 * 1024 * 1024

ADAM_LR, ADAM_B1, ADAM_B2, ADAM_EPS, ADAM_WD, ADAM_STEP = 0.001, 0.9, 0.999, 1e-08, 0.01, 10


def _params(*sem):
    return pltpu.CompilerParams(dimension_semantics=sem, vmem_limit_bytes=VMEM_LIMIT)


WIDE_ROW_TILE = 1024


def _row_tile(m, rows=ROW_TILE):
    return rows if m % rows == 0 else m


def _col_tile(n, cap=1408):
    best = None
    for t in range(LANES, min(n, cap) + 1, LANES):
        if n % t == 0:
            best = t
    return best if best is not None else n


def _dot(a, b, ca, cb):
    return lax.dot_general(a, b, (((ca,), (cb,)), ((), ())), preferred_element_type=F32)


def _sigmoid(x):
    return 0.5 * jnp.tanh(0.5 * x) + 0.5


def mm_nt(a, b, bias=None, out_dtype=BF16, name="mm_nt"):
    m, k = a.shape
    n = b.shape[0]
    tm, tn = _row_tile(m, WIDE_ROW_TILE), _col_tile(n)

    def body(*refs):
        a_ref, b_ref = refs[0], refs[1]
        o_ref = refs[-1]
        acc = _dot(a_ref[...].astype(BF16), b_ref[...].astype(BF16), 1, 1)
        if bias is not None:
            acc = acc + refs[2][...]
        o_ref[...] = acc.astype(o_ref.dtype)

    in_specs = [pl.BlockSpec((tm, k), lambda j, i: (i, 0)), pl.BlockSpec((tn, k), lambda j, i: (j, 0))]
    args = [a, b]
    if bias is not None:
        in_specs.append(pl.BlockSpec((1, tn), lambda j, i: (0, j)))
        args.append(bias)
    return pl.pallas_call(
        body, out_shape=jax.ShapeDtypeStruct((m, n), out_dtype), grid=(n // tn, m // tm),
        in_specs=in_specs, out_specs=pl.BlockSpec((tm, tn), lambda j, i: (i, j)),
        compiler_params=_params("parallel", "arbitrary"), name=name)(*args)


def mm_nn(a, b, res=None, out_dtype=F32, name="mm_nn"):
    m, k = a.shape
    n = b.shape[1]
    tm, tn = _row_tile(m), _col_tile(n, 1024)

    def body(*refs):
        a_ref, b_ref = refs[0], refs[1]
        o_ref = refs[-1]
        acc = _dot(a_ref[...].astype(BF16), b_ref[...].astype(BF16), 1, 0)
        if res is not None:
            acc = acc + refs[2][...]
        o_ref[...] = acc.astype(o_ref.dtype)

    in_specs = [pl.BlockSpec((tm, k), lambda j, i: (i, 0)), pl.BlockSpec((k, tn), lambda j, i: (0, j))]
    args = [a, b]
    if res is not None:
        in_specs.append(pl.BlockSpec((tm, tn), lambda j, i: (i, j)))
        args.append(res)
    return pl.pallas_call(
        body, out_shape=jax.ShapeDtypeStruct((m, n), out_dtype), grid=(n // tn, m // tm),
        in_specs=in_specs, out_specs=pl.BlockSpec((tm, tn), lambda j, i: (i, j)),
        compiler_params=_params("parallel", "arbitrary"), name=name)(*args)


def mm2_nn(a1, b1, a2, b2, name="mm2_nn"):
    m, k = a1.shape
    n = b1.shape[1]
    tm = _row_tile(m)

    def body(a1_ref, b1_ref, a2_ref, b2_ref, o_ref):
        o_ref[...] = _dot(a1_ref[...], b1_ref[...], 1, 0) + _dot(a2_ref[...], b2_ref[...], 1, 0)

    a_spec = pl.BlockSpec((tm, k), lambda i: (i, 0))
    b_spec = pl.BlockSpec((k, n), lambda i: (0, 0))
    return pl.pallas_call(
        body, out_shape=jax.ShapeDtypeStruct((m, n), F32), grid=(m // tm,),
        in_specs=[a_spec, b_spec, a_spec, b_spec], out_specs=pl.BlockSpec((tm, n), lambda i: (i, 0)),
        compiler_params=_params("parallel"), name=name)(a1, b1, a2, b2)


def mm_tn(a, b, out_dtype=BF16, name="mm_tn"):
    t, r = a.shape
    c = b.shape[1]
    tr = _col_tile(r, 512)

    def body(a_ref, b_ref, o_ref):
        o_ref[...] = _dot(a_ref[...].astype(BF16), b_ref[...].astype(BF16), 0, 0).astype(o_ref.dtype)

    return pl.pallas_call(
        body, out_shape=jax.ShapeDtypeStruct((r, c), out_dtype), grid=(r // tr,),
        in_specs=[pl.BlockSpec((t, tr), lambda i: (0, i)), pl.BlockSpec((t, c), lambda i: (0, 0))],
        out_specs=pl.BlockSpec((tr, c), lambda i: (i, 0)),
        compiler_params=_params("parallel"), name=name)(a, b)


def _resident(shape):
    return pl.BlockSpec(shape, lambda i: (0, 0), pipeline_mode=pl.Buffered(1))


def proj_norm(a, b, res, gain, name):
    m, k = a.shape
    n = b.shape[1]
    tm = _row_tile(m, WIDE_ROW_TILE)

    def body(a_ref, b_ref, res_ref, g_ref, x_ref, h_ref):
        xv = res_ref[...] + _dot(a_ref[...], b_ref[...], 1, 0)
        x_ref[...] = xv
        r = lax.rsqrt(jnp.mean(xv * xv, axis=-1, keepdims=True) + NORM_EPS)
        h_ref[...] = (xv * r * g_ref[...]).astype(BF16)

    row = pl.BlockSpec((tm, n), lambda i: (i, 0))
    return pl.pallas_call(
        body, out_shape=(jax.ShapeDtypeStruct((m, n), F32), jax.ShapeDtypeStruct((m, n), BF16)), grid=(m // tm,),
        in_specs=[pl.BlockSpec((tm, k), lambda i: (i, 0)), _resident((k, n)), row, _resident((1, n))],
        out_specs=(row, row), compiler_params=_params("parallel"), name=name)(a, b, res, gain)


def proj_loss(a, b, res, target, name):
    m, k = a.shape
    n = b.shape[1]
    tm = _row_tile(m)
    nt = m // tm

    def body(a_ref, b_ref, res_ref, t_ref, dy_ref, dyb_ref, l_ref, acc_ref):
        i = pl.program_id(0)

        @pl.when(i == 0)
        def _():
            acc_ref[...] = jnp.zeros_like(acc_ref)

        err = res_ref[...] + _dot(a_ref[...], b_ref[...], 1, 0) - t_ref[...]
        dy = err * (1.0 / n)
        dy_ref[...] = dy
        dyb_ref[...] = dy.astype(BF16)
        acc_ref[...] += jnp.sum(err * err, axis=0, keepdims=True)

        @pl.when(i == nt - 1)
        def _():
            total = jnp.sum(acc_ref[...], axis=-1, keepdims=True) * (0.5 / n)
            l_ref[...] = jnp.broadcast_to(total, l_ref.shape)

    row = pl.BlockSpec((tm, n), lambda i: (i, 0))
    return pl.pallas_call(
        body, out_shape=(jax.ShapeDtypeStruct((m, n), F32), jax.ShapeDtypeStruct((m, n), BF16),
                         jax.ShapeDtypeStruct((8, LANES), F32)), grid=(nt,),
        in_specs=[pl.BlockSpec((tm, k), lambda i: (i, 0)), _resident((k, n)), row, row],
        out_specs=(row, row, pl.BlockSpec((8, LANES), lambda i: (0, 0))),
        scratch_shapes=[pltpu.VMEM((1, n), F32)],
        compiler_params=_params("arbitrary"), name=name)(a, b, res, target)


def in_proj_bwd(dz, w_t, x, gain, dres, out_dtype, name):
    m, n = x.shape
    k = dz.shape[1]
    tm = _row_tile(m, WIDE_ROW_TILE)

    def body(dz_ref, w_ref, x_ref, g_ref, dres_ref, dx_ref, dg_ref, cs_ref):
        @pl.when(pl.program_id(0) == 0)
        def _():
            dg_ref[...] = jnp.zeros_like(dg_ref)
            cs_ref[...] = jnp.zeros_like(cs_ref)

        dzv = dz_ref[...]
        cs_ref[...] += jnp.sum(dzv.astype(F32), axis=0, keepdims=True)
        dhv = _dot(dzv, w_ref[...], 1, 0)
        xv = x_ref[...]
        r = lax.rsqrt(jnp.mean(xv * xv, axis=-1, keepdims=True) + NORM_EPS)
        xhat = xv * r
        dg_ref[...] += jnp.sum(dhv * xhat, axis=0, keepdims=True)
        dxhat = dhv * g_ref[...]
        dx = dres_ref[...].astype(F32) + r * (dxhat - xhat * jnp.mean(dxhat * xhat, axis=-1, keepdims=True))
        dx_ref[...] = dx.astype(dx_ref.dtype)

    row = pl.BlockSpec((tm, n), lambda i: (i, 0))
    return pl.pallas_call(
        body, out_shape=(jax.ShapeDtypeStruct((m, n), out_dtype), jax.ShapeDtypeStruct((1, n), F32),
                         jax.ShapeDtypeStruct((1, k), F32)), grid=(m // tm,),
        in_specs=[pl.BlockSpec((tm, k), lambda i: (i, 0)), _resident(w_t.shape), row, _resident((1, n)), row],
        out_specs=(row, pl.BlockSpec((1, n), lambda i: (0, 0)), pl.BlockSpec((1, k), lambda i: (0, 0))),
        compiler_params=_params("arbitrary"), name=name)(dz, w_t, x, gain, dres)


FFN_ROWS = 256


def _ffn_row_tile(m):
    return FFN_ROWS if m % FFN_ROWS == 0 else m


def ffn_fwd(h2, wg_t, wu_t, wd, x1, gain=None, target=None, name="ffn_fwd"):
    n, d = h2.shape
    f = wg_t.shape[0]
    tm = _ffn_row_tile(n)
    nt = n // tm
    last = target is not None

    def body(h_ref, wg_ref, wu_ref, wd_ref, x1_ref, e_ref, g_ref, u_ref, a_ref, *rest):
        hv = h_ref[...]
        gv = _dot(hv, wg_ref[...], 1, 1)
        uv = _dot(hv, wu_ref[...], 1, 1)
        g_ref[...] = gv.astype(BF16)
        u_ref[...] = uv.astype(BF16)
        av = (gv * _sigmoid(gv) * uv).astype(BF16)
        a_ref[...] = av
        xv = x1_ref[...] + _dot(av, wd_ref[...], 1, 0)
        if not last:
            x_ref, hn_ref = rest
            x_ref[...] = xv
            r = lax.rsqrt(jnp.mean(xv * xv, axis=-1, keepdims=True) + NORM_EPS)
            hn_ref[...] = (xv * r * e_ref[...]).astype(BF16)
        else:
            dyb_ref, l_ref, acc_ref = rest
            i = pl.program_id(0)

            @pl.when(i == 0)
            def _():
                acc_ref[...] = jnp.zeros_like(acc_ref)

            err = xv - e_ref[...]
            dyb_ref[...] = (err * (1.0 / d)).astype(BF16)
            acc_ref[...] += jnp.sum(err * err, axis=0, keepdims=True)

            @pl.when(i == nt - 1)
            def _():
                total = jnp.sum(acc_ref[...], axis=-1, keepdims=True) * (0.5 / d)
                l_ref[...] = jnp.broadcast_to(total, l_ref.shape)

    row_d = pl.BlockSpec((tm, d), lambda i: (i, 0))
    row_f = pl.BlockSpec((tm, f), lambda i: (i, 0))
    act_shape = jax.ShapeDtypeStruct((n, f), BF16)
    if not last:
        extra_in, extra = _resident((1, d)), gain
        out_shape = (act_shape, act_shape, act_shape, jax.ShapeDtypeStruct((n, d), F32), jax.ShapeDtypeStruct((n, d), BF16))
        out_specs = (row_f, row_f, row_f, row_d, row_d)
        scratch = []
    else:
        extra_in, extra = row_d, target
        out_shape = (act_shape, act_shape, act_shape, jax.ShapeDtypeStruct((n, d), BF16),
                     jax.ShapeDtypeStruct((8, LANES), F32))
        out_specs = (row_f, row_f, row_f, row_d, pl.BlockSpec((8, LANES), lambda i: (0, 0)))
        scratch = [pltpu.VMEM((1, d), F32)]
    return pl.pallas_call(
        body, out_shape=out_shape, grid=(nt,),
        in_specs=[row_d, _resident((f, d)), _resident((f, d)), _resident((f, d)), row_d, extra_in],
        out_specs=out_specs, scratch_shapes=scratch,
        compiler_params=_params("arbitrary"), name=name)(h2, wg_t, wu_t, wd, x1, extra)


def ffn_bwd(dx_b, wd, gate, up, wg_t, wu_t, x1, gain, wo, name="ffn_bwd"):
    n, d = x1.shape
    f = wd.shape[0]
    tm = _ffn_row_tile(n)

    def body(dxb_ref, wd_ref, g_ref, u_ref, wg_ref, wu_ref, x_ref, gain_ref, wo_ref,
             dg_ref, du_ref, dxo_ref, dc_ref, dgain_ref):
        @pl.when(pl.program_id(0) == 0)
        def _():
            dgain_ref[...] = jnp.zeros_like(dgain_ref)

        dact = _dot(dxb_ref[...], wd_ref[...], 1, 1)
        gv = g_ref[...].astype(F32)
        uv = u_ref[...].astype(F32)
        sg = _sigmoid(gv)
        dgv = (dact * uv * sg * (1.0 + gv * (1.0 - sg))).astype(BF16)
        duv = (dact * gv * sg).astype(BF16)
        dg_ref[...] = dgv
        du_ref[...] = duv
        dhv = _dot(dgv, wg_ref[...], 1, 0) + _dot(duv, wu_ref[...], 1, 0)
        xv = x_ref[...]
        r = lax.rsqrt(jnp.mean(xv * xv, axis=-1, keepdims=True) + NORM_EPS)
        xhat = xv * r
        dgain_ref[...] += jnp.sum(dhv * xhat, axis=0, keepdims=True)
        dxhat = dhv * gain_ref[...]
        dxb = (dxb_ref[...].astype(F32) + r * (dxhat - xhat * jnp.mean(dxhat * xhat, axis=-1, keepdims=True))).astype(BF16)
        dxo_ref[...] = dxb
        dc_ref[...] = _dot(dxb, wo_ref[...], 1, 1).astype(BF16)

    row_d = pl.BlockSpec((tm, d), lambda i: (i, 0))
    row_f = pl.BlockSpec((tm, f), lambda i: (i, 0))
    w_spec = _resident((f, d))
    act_shape = jax.ShapeDtypeStruct((n, f), BF16)
    row_shape = jax.ShapeDtypeStruct((n, d), BF16)
    return pl.pallas_call(
        body, out_shape=(act_shape, act_shape, row_shape, jax.ShapeDtypeStruct((n, wo.shape[0]), BF16),
                         jax.ShapeDtypeStruct((1, d), F32)),
        grid=(n // tm,),
        in_specs=[row_d, w_spec, row_f, row_f, w_spec, w_spec, row_d, _resident((1, d)), _resident(wo.shape)],
        out_specs=(row_f, row_f, row_d, pl.BlockSpec((tm, wo.shape[0]), lambda i: (i, 0)),
                   pl.BlockSpec((1, d), lambda i: (0, 0))),
        compiler_params=_params("arbitrary"), name=name)(dx_b, wd, gate, up, wg_t, wu_t, x1, gain, wo)


def ffn_weight_grads(dgate, dup, h2, act, dx_b, name="ffn_weight_grads"):
    t, r = dgate.shape
    c = h2.shape[1]
    tr = _col_tile(r, 512)

    def body(a1_ref, a2_ref, a3_ref, b12_ref, b3_ref, o1_ref, o2_ref, o3_ref):
        bv = b12_ref[...]
        o1_ref[...] = _dot(a1_ref[...], bv, 0, 0).astype(o1_ref.dtype)
        o2_ref[...] = _dot(a2_ref[...], bv, 0, 0).astype(o2_ref.dtype)
        o3_ref[...] = _dot(a3_ref[...], b3_ref[...], 0, 0).astype(o3_ref.dtype)

    a_spec = pl.BlockSpec((t, tr), lambda i: (0, i))
    o_spec = pl.BlockSpec((tr, c), lambda i: (i, 0))
    shape = jax.ShapeDtypeStruct((r, c), BF16)
    return pl.pallas_call(
        body, out_shape=(shape, shape, shape), grid=(r // tr,),
        in_specs=[a_spec, a_spec, a_spec, _resident((t, c)), _resident((t, c))],
        out_specs=(o_spec, o_spec, o_spec), compiler_params=_params("parallel"), name=name)(dgate, dup, act, h2, dx_b)


def rms_fwd(x, g, name="rms_fwd"):
    n, d = x.shape
    tm = _row_tile(n)

    def body(x_ref, g_ref, o_ref):
        xv = x_ref[...]
        r = lax.rsqrt(jnp.mean(xv * xv, axis=-1, keepdims=True) + NORM_EPS)
        o_ref[...] = (xv * r * g_ref[...]).astype(o_ref.dtype)

    return pl.pallas_call(
        body, out_shape=jax.ShapeDtypeStruct((n, d), BF16), grid=(n // tm,),
        in_specs=[pl.BlockSpec((tm, d), lambda i: (i, 0)), pl.BlockSpec((1, d), lambda i: (0, 0))],
        out_specs=pl.BlockSpec((tm, d), lambda i: (i, 0)),
        compiler_params=_params("parallel"), name=name)(x, g)


def rms_bwd(dh, x, g, dres, name="rms_bwd"):
    n, d = x.shape
    tm = _row_tile(n)

    def body(dh_ref, x_ref, g_ref, dres_ref, dx_ref, dxb_ref, dg_ref):
        @pl.when(pl.program_id(0) == 0)
        def _():
            dg_ref[...] = jnp.zeros_like(dg_ref)

        xv = x_ref[...]
        dhv = dh_ref[...].astype(F32)
        r = lax.rsqrt(jnp.mean(xv * xv, axis=-1, keepdims=True) + NORM_EPS)
        xhat = xv * r
        dg_ref[...] += jnp.sum(dhv * xhat, axis=0, keepdims=True)
        dxhat = dhv * g_ref[...]
        mean_t = jnp.mean(dxhat * xhat, axis=-1, keepdims=True)
        dx = dres_ref[...] + r * (dxhat - xhat * mean_t)
        dx_ref[...] = dx
        dxb_ref[...] = dx.astype(BF16)

    row = pl.BlockSpec((tm, d), lambda i: (i, 0))
    vec = pl.BlockSpec((1, d), lambda i: (0, 0))
    return pl.pallas_call(
        body, out_shape=(jax.ShapeDtypeStruct((n, d), F32), jax.ShapeDtypeStruct((n, d), BF16),
                         jax.ShapeDtypeStruct((1, d), F32)), grid=(n // tm,),
        in_specs=[row, row, vec, row], out_specs=(row, row, vec),
        compiler_params=_params("arbitrary"), name=name)(dh, x, g, dres)


def gate_up(h2, wg_t, wu_t, name="gate_up"):
    n, d = h2.shape
    f = wg_t.shape[0]
    tm, tn = _row_tile(n), _col_tile(f)

    def body(h_ref, wg_ref, wu_ref, g_ref, u_ref, a_ref):
        hv = h_ref[...]
        gv = _dot(hv, wg_ref[...], 1, 1)
        uv = _dot(hv, wu_ref[...], 1, 1)
        g_ref[...] = gv.astype(BF16)
        u_ref[...] = uv.astype(BF16)
        a_ref[...] = (gv * _sigmoid(gv) * uv).astype(BF16)

    w_spec = pl.BlockSpec((tn, d), lambda j, i: (j, 0))
    o_spec = pl.BlockSpec((tm, tn), lambda j, i: (i, j))
    o_shape = jax.ShapeDtypeStruct((n, f), BF16)
    return pl.pallas_call(
        body, out_shape=(o_shape, o_shape, o_shape), grid=(f // tn, n // tm),
        in_specs=[pl.BlockSpec((tm, d), lambda j, i: (i, 0)), w_spec, w_spec], out_specs=(o_spec, o_spec, o_spec),
        compiler_params=_params("parallel", "arbitrary"), name=name)(h2, wg_t, wu_t)


def ffn_bwd_act(dx, wd, gate, up, name="ffn_bwd_act"):
    n, d = dx.shape
    f = wd.shape[0]
    tm, tn = _row_tile(n), _col_tile(f)

    def body(dx_ref, wd_ref, g_ref, u_ref, dg_ref, du_ref):
        dact = _dot(dx_ref[...].astype(BF16), wd_ref[...], 1, 1)
        gv = g_ref[...].astype(F32)
        uv = u_ref[...].astype(F32)
        sg = _sigmoid(gv)
        dg_ref[...] = (dact * uv * sg * (1.0 + gv * (1.0 - sg))).astype(BF16)
        du_ref[...] = (dact * gv * sg).astype(BF16)

    t_spec = pl.BlockSpec((tm, tn), lambda j, i: (i, j))
    o_shape = jax.ShapeDtypeStruct((n, f), BF16)
    return pl.pallas_call(
        body, out_shape=(o_shape, o_shape), grid=(f // tn, n // tm),
        in_specs=[pl.BlockSpec((tm, d), lambda j, i: (i, 0)), pl.BlockSpec((tn, d), lambda j, i: (j, 0)), t_spec, t_spec],
        out_specs=(t_spec, t_spec),
        compiler_params=_params("parallel", "arbitrary"), name=name)(dx, wd, gate, up)


def _group_masks(width):
    lane = lax.broadcasted_iota(jnp.int32, (1, width), 1)
    return [(lane >= HEAD_DIM * g) & (lane < HEAD_DIM * (g + 1)) for g in range(width // HEAD_DIM)]


def _group_sum(x, masks):
    out = jnp.zeros_like(x)
    for msk in masks:
        s = jnp.sum(jnp.where(msk, x, 0.0), axis=-1, keepdims=True)
        out = jnp.where(msk, s, out)
    return out


def _head_norm(x, gain, masks):
    r = lax.rsqrt(_group_sum(x * x, masks) * (1.0 / HEAD_DIM) + NORM_EPS)
    xhat = x * r
    return xhat * gain, xhat, r


def _head_norm_bwd(dxn, xhat, r, gain, masks):
    dgain = jnp.sum(dxn * xhat, axis=0, keepdims=True)
    dxhat = dxn * gain
    mean_t = _group_sum(dxhat * xhat, masks) * (1.0 / HEAD_DIM)
    return r * (dxhat - xhat * mean_t), dgain


def _softmax_rows(s):
    e = jnp.exp(s - jnp.max(s, axis=-1, keepdims=True))
    return e * (1.0 / jnp.sum(e, axis=-1, keepdims=True))


def _rel_onehot():
    col = lax.broadcasted_iota(jnp.int32, (1, KEY_WIN), 1)
    off = jnp.where(col < KEY_WIN - LANES, col, col - KEY_WIN)
    idx = jnp.clip(8 * CHUNK - off, -(CHUNK - 1), LANES) + (CHUNK - 1)
    return (lax.broadcasted_iota(jnp.int32, (N_REL, KEY_WIN), 0) == idx).astype(F32)


def bias_blocks(rel16):
    heads = TOK_WIDTH // HEAD_DIM

    def body(rel_ref, o_ref, u_ref):
        u_ref[...] = jnp.dot(rel_ref[...], _rel_onehot(), precision=HIGHEST, preferred_element_type=F32)
        row = lax.broadcasted_iota(jnp.int32, (CHUNK, KEY_WIN), 0)
        col = lax.broadcasted_iota(jnp.int32, (CHUNK, KEY_WIN), 1)
        for h in range(heads):
            xv = jnp.broadcast_to(u_ref[h:h + 1, :], (CHUNK, KEY_WIN))
            for b in range(6):
                xv = jnp.where(((row >> b) & 1) == 1, pltpu.roll(xv, 1 << b, axis=1), xv)
            xv = jnp.where(col < BAND, xv, NEG_INF)
            for i in range(Q_BLOCK // CHUNK):
                o_ref[h, CHUNK * i:CHUNK * (i + 1), :] = pltpu.roll(xv, CHUNK * i, axis=1) if i else xv

    return pl.pallas_call(
        body, out_shape=jax.ShapeDtypeStruct((heads, Q_BLOCK, KEY_WIN), F32),
        scratch_shapes=[pltpu.VMEM((16, KEY_WIN), F32)], name="bias_blocks")(rel16)


def bias_grad(dbias):
    heads = dbias.shape[0]

    def body(db_ref, o_ref, y_ref):
        y_ref[...] = jnp.zeros_like(y_ref)
        row = lax.broadcasted_iota(jnp.int32, (CHUNK, KEY_WIN), 0)
        for h in range(heads):
            fv = db_ref[h, 0:CHUNK, :]
            for i in range(1, Q_BLOCK // CHUNK):
                fv = fv + pltpu.roll(db_ref[h, CHUNK * i:CHUNK * (i + 1), :], KEY_WIN - CHUNK * i, axis=1)
            for b in range(6):
                fv = jnp.where(((row >> b) & 1) == 1, pltpu.roll(fv, KEY_WIN - (1 << b), axis=1), fv)
            y_ref[h:h + 1, :] = jnp.sum(fv, axis=0, keepdims=True)
        o_ref[...] = lax.dot_general(y_ref[...], _rel_onehot(), (((1,), (1,)), ((), ())),
                                     precision=HIGHEST, preferred_element_type=F32)

    return pl.pallas_call(
        body, out_shape=jax.ShapeDtypeStruct((16, N_REL), F32),
        scratch_shapes=[pltpu.VMEM((16, KEY_WIN), F32)], name="bias_grad")(dbias)


def _attn_windows(seq):
    out = []
    for j in range(seq // Q_BLOCK):
        r0 = j * Q_BLOCK
        k0 = max(0, r0 - 8 * CHUNK)
        width = r0 + Q_BLOCK - k0
        out.append((r0, k0, width, KEY_WIN - width))
    return out


def attn_fwd(z, gq2, gk2, bias, batch, seq):
    n = z.shape[0]
    pairs = TOK_WIDTH // LANES

    def body(q_ref, k_ref, v_ref, gq_ref, gk_ref, b_ref, o_ref, qs_s, kn_s):
        masks = _group_masks(LANES)
        qs_s[...] = (_head_norm(q_ref[...].astype(F32), gq_ref[...], masks)[0] * ATTN_SCALE).astype(BF16)
        kn_s[...] = _head_norm(k_ref[...].astype(F32), gk_ref[...], masks)[0].astype(BF16)
        for r0, k0, width, c0 in _attn_windows(seq):
            qb = qs_s[r0:r0 + Q_BLOCK, :]
            kw = kn_s[k0:k0 + width, :]
            vw = v_ref[k0:k0 + width, :]
            out = jnp.zeros((Q_BLOCK, LANES), F32)
            for h, msk in enumerate(masks):
                qh = jnp.where(msk, qb, jnp.zeros_like(qb))
                s = _dot(qh, kw, 1, 1) + b_ref[h, :, c0:KEY_WIN]
                p = _softmax_rows(s).astype(BF16)
                out = jnp.where(msk, _dot(p, vw, 1, 0), out)
            o_ref[r0:r0 + Q_BLOCK, :] = out.astype(o_ref.dtype)

    def col(off):
        return pl.BlockSpec((seq, LANES), lambda b, p: (b, off + p))

    vec = pl.BlockSpec((1, LANES), lambda b, p: (0, 0))
    return pl.pallas_call(
        body, out_shape=jax.ShapeDtypeStruct((n, D_MODEL), BF16), grid=(batch, pairs),
        in_specs=[col(0), col(pairs), col(2 * pairs), vec, vec,
                  pl.BlockSpec((2, Q_BLOCK, KEY_WIN), lambda b, p: (p, 0, 0))],
        out_specs=pl.BlockSpec((seq, LANES), lambda b, p: (b, p)),
        scratch_shapes=[pltpu.VMEM((seq, LANES), BF16), pltpu.VMEM((seq, LANES), BF16)],
        compiler_params=_params("parallel", "arbitrary"), name="attn_fwd")(z, z, z, gq2, gk2, bias)


def attn_bwd(z, dcat, gq2, gk2, bias, batch, seq):
    n = z.shape[0]
    pairs = TOK_WIDTH // LANES

    def body(q_ref, k_ref, v_ref, do_ref, gq_ref, gk_ref, b_ref,
             dz_ref, db_ref, dgq_ref, dgk_ref, qs_s, kn_s, dqn_s, dkn_s, dv_s, dk_o, dv_o):
        pi, bi, which = pl.program_id(0), pl.program_id(1), pl.program_id(2)

        @pl.when(which == 0)
        def _():
            masks = _group_masks(LANES)

            @pl.when(bi == 0)
            def _():
                db_ref[...] = jnp.zeros_like(db_ref)

            @pl.when((bi == 0) & (pi == 0))
            def _():
                dgq_ref[...] = jnp.zeros_like(dgq_ref)
                dgk_ref[...] = jnp.zeros_like(dgk_ref)

            qn, qhat, rq = _head_norm(q_ref[...].astype(F32), gq_ref[...], masks)
            kn, khat, rk = _head_norm(k_ref[...].astype(F32), gk_ref[...], masks)
            qs_s[...] = (qn * ATTN_SCALE).astype(BF16)
            kn_s[...] = kn.astype(BF16)
            dkn_s[...] = jnp.zeros_like(dkn_s)
            dv_s[...] = jnp.zeros_like(dv_s)
            for r0, k0, width, c0 in _attn_windows(seq):
                qb = qs_s[r0:r0 + Q_BLOCK, :]
                dob = do_ref[r0:r0 + Q_BLOCK, :]
                kw = kn_s[k0:k0 + width, :]
                vw = v_ref[k0:k0 + width, :]
                dq_acc = jnp.zeros((Q_BLOCK, LANES), F32)
                dk_acc = jnp.zeros((width, LANES), F32)
                dv_acc = jnp.zeros((width, LANES), F32)
                for h, msk in enumerate(masks):
                    qh = jnp.where(msk, qb, jnp.zeros_like(qb))
                    doh = jnp.where(msk, dob, jnp.zeros_like(dob))
                    p = _softmax_rows(_dot(qh, kw, 1, 1) + b_ref[h, :, c0:KEY_WIN])
                    dp = _dot(doh, vw, 1, 1)
                    ds = p * (dp - jnp.sum(p * dp, axis=-1, keepdims=True))
                    db_ref[h, :, c0:KEY_WIN] += ds
                    dsb = ds.astype(BF16)
                    dq_acc = jnp.where(msk, _dot(dsb, kw, 1, 0), dq_acc)
                    dk_acc = jnp.where(msk, _dot(dsb, qb, 0, 0), dk_acc)
                    dv_acc = jnp.where(msk, _dot(p.astype(BF16), dob, 0, 0), dv_acc)
                dqn_s[r0:r0 + Q_BLOCK, :] = dq_acc * ATTN_SCALE
                dkn_s[k0:k0 + width, :] += dk_acc
                dv_s[k0:k0 + width, :] += dv_acc
            dq, dgq = _head_norm_bwd(dqn_s[...], qhat, rq, gq_ref[...], masks)
            dk, dgk = _head_norm_bwd(dkn_s[...], khat, rk, gk_ref[...], masks)
            dz_ref[...] = dq.astype(dz_ref.dtype)
            dk_o[...] = dk.astype(dk_o.dtype)
            dv_o[...] = dv_s[...].astype(dv_o.dtype)
            dgq_ref[...] += dgq
            dgk_ref[...] += dgk

        @pl.when(which == 1)
        def _():
            dz_ref[...] = dk_o[...]

        @pl.when(which == 2)
        def _():
            dz_ref[...] = dv_o[...]

    def ahead(p, b, t):
        nb = b + jnp.where(t > 0, 1, 0)
        wrap = jnp.where(nb >= batch, 1, 0)
        return jnp.minimum(p + wrap, pairs - 1), nb - wrap * batch

    def col(off):
        def index(p, b, t):
            np_, nb = ahead(p, b, t)
            return nb, off + np_
        return pl.BlockSpec((seq, LANES), index)

    vec = pl.BlockSpec((1, LANES), lambda p, b, t: (0, 0))
    blk = pl.BlockSpec((2, Q_BLOCK, KEY_WIN), lambda p, b, t: (p, 0, 0))
    blk_in = pl.BlockSpec((2, Q_BLOCK, KEY_WIN), lambda p, b, t: (ahead(p, b, t)[0], 0, 0))
    v_shape = jax.ShapeDtypeStruct((1, LANES), F32)
    return pl.pallas_call(
        body,
        out_shape=(jax.ShapeDtypeStruct(z.shape, BF16), jax.ShapeDtypeStruct(bias.shape, F32), v_shape, v_shape),
        grid=(pairs, batch, 3),
        in_specs=[col(0), col(pairs), col(2 * pairs), col(0), vec, vec, blk_in],
        out_specs=(pl.BlockSpec((seq, LANES), lambda p, b, t: (b, t * pairs + p)), blk, vec, vec),
        scratch_shapes=[pltpu.VMEM((seq, LANES), BF16), pltpu.VMEM((seq, LANES), BF16),
                        pltpu.VMEM((seq, LANES), F32), pltpu.VMEM((seq, LANES), F32), pltpu.VMEM((seq, LANES), F32),
                        pltpu.VMEM((seq, LANES), BF16), pltpu.VMEM((seq, LANES), BF16)],
        compiler_params=_params("arbitrary", "arbitrary", "arbitrary"), name="attn_bwd")(
            z, z, z, dcat, gq2, gk2, bias)


MEM_ROWS = 512


def memattn_fwd(z, kv, gq4, gk4, cat, batch, seq, qcol, name):
    mtok = kv.shape[0] // batch
    rows = min(MEM_ROWS, seq)

    def body(q_ref, kv_ref, gq_ref, gk_ref, cat_ref, o_ref):
        del cat_ref
        masks = _group_masks(MEM_WIDTH)
        kn = _head_norm(kv_ref[:, 0:MEM_WIDTH], gk_ref[...], masks)[0].astype(BF16)
        vm = kv_ref[:, MEM_WIDTH:2 * MEM_WIDTH].astype(BF16)
        for t in range(seq // rows):
            sl = slice(t * rows, (t + 1) * rows)
            qs = (_head_norm(q_ref[sl, :].astype(F32), gq_ref[...], masks)[0] * ATTN_SCALE).astype(BF16)
            out = jnp.zeros((rows, MEM_WIDTH), F32)
            for msk in masks:
                qh = jnp.where(msk, qs, jnp.zeros_like(qs))
                p = _softmax_rows(_dot(qh, kn, 1, 1)).astype(BF16)
                out = jnp.where(msk, _dot(p, vm, 1, 0), out)
            o_ref[sl, :] = out.astype(o_ref.dtype)

    vec = pl.BlockSpec((1, MEM_WIDTH), lambda b: (0, 0))
    return pl.pallas_call(
        body, out_shape=jax.ShapeDtypeStruct(cat.shape, cat.dtype), grid=(batch,),
        in_specs=[pl.BlockSpec((seq, MEM_WIDTH), lambda b: (b, qcol)),
                  pl.BlockSpec((mtok, 2 * MEM_WIDTH), lambda b: (b, 0)), vec, vec, ANY],
        out_specs=pl.BlockSpec((seq, MEM_WIDTH), lambda b: (b, TOK_WIDTH // MEM_WIDTH)),
        input_output_aliases={4: 0},
        compiler_params=_params("parallel"), name=name)(z, kv, gq4, gk4, cat)


def memattn_bwd(z, kv, dcat, gq4, gk4, dz, batch, seq, qcol, name):
    mtok = kv.shape[0] // batch
    rows = min(MEM_ROWS, seq)

    def body(q_ref, kv_ref, do_ref, gq_ref, gk_ref, dz_in_ref, dq_ref, dkv_ref, dgq_ref, dgk_ref):
        del dz_in_ref
        @pl.when(pl.program_id(0) == 0)
        def _():
            dgq_ref[...] = jnp.zeros_like(dgq_ref)
            dgk_ref[...] = jnp.zeros_like(dgk_ref)

        masks = _group_masks(MEM_WIDTH)
        kn_f, khat, rk = _head_norm(kv_ref[:, 0:MEM_WIDTH], gk_ref[...], masks)
        kn = kn_f.astype(BF16)
        vm = kv_ref[:, MEM_WIDTH:2 * MEM_WIDTH].astype(BF16)
        dkn = jnp.zeros((mtok, MEM_WIDTH), F32)
        dvm = jnp.zeros((mtok, MEM_WIDTH), F32)
        dgq = jnp.zeros((1, MEM_WIDTH), F32)
        for t in range(seq // rows):
            sl = slice(t * rows, (t + 1) * rows)
            qn_f, qhat, rq = _head_norm(q_ref[sl, :].astype(F32), gq_ref[...], masks)
            qs = (qn_f * ATTN_SCALE).astype(BF16)
            dob = do_ref[sl, :]
            dqn = jnp.zeros((rows, MEM_WIDTH), F32)
            for msk in masks:
                qh = jnp.where(msk, qs, jnp.zeros_like(qs))
                doh = jnp.where(msk, dob, jnp.zeros_like(dob))
                p = _softmax_rows(_dot(qh, kn, 1, 1))
                dp = _dot(doh, vm, 1, 1)
                ds = p * (dp - jnp.sum(p * dp, axis=-1, keepdims=True))
                dsb = ds.astype(BF16)
                dqn = jnp.where(msk, _dot(dsb, kn, 1, 0), dqn)
                dkn = dkn + jnp.where(msk, _dot(dsb, qs, 0, 0), 0.0)
                dvm = dvm + jnp.where(msk, _dot(p.astype(BF16), dob, 0, 0), 0.0)
            dq, dg = _head_norm_bwd(dqn * ATTN_SCALE, qhat, rq, gq_ref[...], masks)
            dq_ref[sl, :] = dq.astype(dq_ref.dtype)
            dgq = dgq + dg
        dk, dgk = _head_norm_bwd(dkn, khat, rk, gk_ref[...], masks)
        dkv_ref[:, 0:MEM_WIDTH] = dk
        dkv_ref[:, MEM_WIDTH:2 * MEM_WIDTH] = dvm
        dgq_ref[...] += dgq
        dgk_ref[...] += dgk

    vec = pl.BlockSpec((1, MEM_WIDTH), lambda b: (0, 0))
    kv_spec = pl.BlockSpec((mtok, 2 * MEM_WIDTH), lambda b: (b, 0))
    v_shape = jax.ShapeDtypeStruct((1, MEM_WIDTH), F32)
    q_spec = pl.BlockSpec((seq, MEM_WIDTH), lambda b: (b, qcol))
    return pl.pallas_call(
        body,
        out_shape=(jax.ShapeDtypeStruct(dz.shape, dz.dtype), jax.ShapeDtypeStruct(kv.shape, F32), v_shape, v_shape),
        grid=(batch,),
        in_specs=[q_spec, kv_spec, pl.BlockSpec((seq, MEM_WIDTH), lambda b: (b, TOK_WIDTH // MEM_WIDTH)), vec, vec, ANY],
        out_specs=(q_spec, kv_spec, vec, vec),
        input_output_aliases={5: 0},
        compiler_params=_params("arbitrary"), name=name)(z, kv, dcat, gq4, gk4, dz)


def mem_prep(mem, gain, wkv, name):
    t, d = mem.shape

    def body(m_ref, g_ref, w_ref, n_ref, kv_ref):
        mv = m_ref[...]
        r = lax.rsqrt(jnp.mean(mv * mv, axis=-1, keepdims=True) + NORM_EPS)
        nv = (mv * r * g_ref[...]).astype(BF16)
        n_ref[...] = nv
        kv_ref[...] = _dot(nv, w_ref[...], 1, 0)

    vmem = pl.BlockSpec(memory_space=pltpu.VMEM)
    return pl.pallas_call(
        body, out_shape=(jax.ShapeDtypeStruct((t, d), BF16), jax.ShapeDtypeStruct((t, wkv.shape[1]), F32)),
        in_specs=[vmem, vmem, vmem], out_specs=(vmem, vmem),
        compiler_params=pltpu.CompilerParams(vmem_limit_bytes=VMEM_LIMIT), name=name)(mem, gain, wkv)


def mem_norm_grad(dkv, wkv, mem, name):
    t, d = mem.shape

    def body(dkv_ref, w_ref, m_ref, dg_ref):
        dn = _dot(dkv_ref[...].astype(BF16), w_ref[...], 1, 1)
        mv = m_ref[...]
        r = lax.rsqrt(jnp.mean(mv * mv, axis=-1, keepdims=True) + NORM_EPS)
        dg_ref[...] = jnp.sum(dn * (mv * r), axis=0, keepdims=True)

    vmem = pl.BlockSpec(memory_space=pltpu.VMEM)
    return pl.pallas_call(
        body, out_shape=jax.ShapeDtypeStruct((1, d), F32), in_specs=[vmem, vmem, vmem], out_specs=vmem,
        compiler_params=pltpu.CompilerParams(vmem_limit_bytes=VMEM_LIMIT), name=name)(dkv, wkv, mem)


CONV_ROWS = 256


def _glu(a_ref, g_ref):
    return a_ref[...].astype(F32) * _sigmoid(g_ref[...].astype(F32))


def _layer_norm_stats(y):
    mu = jnp.mean(y, axis=-1, keepdims=True)
    yc = y - mu
    rstd = lax.rsqrt(jnp.mean(yc * yc, axis=-1, keepdims=True) + NORM_EPS)
    return yc * rstd, rstd


CONV_WIN = CONV_HALO + CONV_ROWS
SUBLANES = 8
SHIFT_ROWS = CONV_WIN - SUBLANES


def _preshift(win, shifted):
    for s in range(1, SUBLANES):
        shifted[s - 1, :, :] = win[s:s + SHIFT_ROWS, :]


TAP_ROWS = 64
TAP_TILES = [(r0, slice(c0, c0 + LANES)) for c0 in range(0, TOK_WIDTH, LANES) for r0 in range(0, CONV_ROWS, TAP_ROWS)]


def _tap(win, shifted, off, r0, lanes):
    s = off % SUBLANES
    base = off - s + r0
    if s == 0:
        return win[base:base + TAP_ROWS, lanes]
    return shifted[s - 1, base:base + TAP_ROWS, lanes]


def _fold_rows(x):
    return jnp.sum(x.reshape(TAP_ROWS // SUBLANES, SUBLANES, LANES), axis=0)


def conv_fwd(z, cw, cb, lg, lb, batch, seq):
    n = z.shape[0]
    nt = seq // CONV_ROWS
    sub = CONV_ROWS // CONV_HALO
    lead = CONV_HALO - (CONV_W - 1)

    def body(a_ref, g_ref, ap_ref, gp_ref, cw_ref, cb_ref, lg_ref, lb_ref, o_ref, y_ref, win, shifted):
        first = pl.program_id(1) == 0
        win[0:CONV_HALO, :] = jnp.where(first, 0.0, _glu(ap_ref, gp_ref))
        win[CONV_HALO:CONV_WIN, :] = _glu(a_ref, g_ref)
        _preshift(win, shifted)
        for r0, lanes in TAP_TILES:
            acc = jnp.zeros((TAP_ROWS, LANES), F32) + cb_ref[:, lanes]
            for w in range(CONV_W):
                acc = acc + _tap(win, shifted, lead + w, r0, lanes) * cw_ref[w:w + 1, lanes]
            y_ref[r0:r0 + TAP_ROWS, lanes] = acc
        yh, _ = _layer_norm_stats(y_ref[...])
        t = yh * lg_ref[...] + lb_ref[...]
        o_ref[...] = (t * _sigmoid(t)).astype(o_ref.dtype)

    def cur(c):
        return pl.BlockSpec((CONV_ROWS, TOK_WIDTH), lambda b, i: (b * nt + i, c))

    def prev(c):
        return pl.BlockSpec((CONV_HALO, TOK_WIDTH), lambda b, i: (jnp.maximum((b * nt + i) * sub - 1, 0), c))

    vec = pl.BlockSpec((1, TOK_WIDTH), lambda b, i: (0, 0))
    return pl.pallas_call(
        body, out_shape=(jax.ShapeDtypeStruct((n, D_MODEL), BF16), jax.ShapeDtypeStruct((n, TOK_WIDTH), F32)),
        grid=(batch, nt),
        in_specs=[cur(0), cur(1), prev(0), prev(1), pl.BlockSpec((32, TOK_WIDTH), lambda b, i: (0, 0)), vec, vec, vec],
        out_specs=(cur(0), cur(0)),
        scratch_shapes=[pltpu.VMEM((CONV_WIN, TOK_WIDTH), F32), pltpu.VMEM((SUBLANES - 1, SHIFT_ROWS, TOK_WIDTH), F32)],
        compiler_params=_params("parallel", "arbitrary"), name="conv_fwd")(z, z, z, z, cw, cb, lg, lb)


def conv_bwd(z, y, dcat, cw, lg, lb, batch, seq):
    n = z.shape[0]
    nt = seq // CONV_ROWS
    sub = CONV_ROWS // CONV_HALO
    lead = CONV_HALO - (CONV_W - 1)
    last_blk = n // CONV_HALO - 1

    def body(a_ref, g_ref, ap_ref, gp_ref, y_ref, yn_ref, do_ref, don_ref, cw_ref, lg_ref, lb_ref,
             dz_ref, dcw_ref, dsm_ref, win, shifted, dyw, dshifted, dg_o):
        b, i, which = pl.program_id(0), pl.program_id(1), pl.program_id(2)

        @pl.when(which == 0)
        def _():
            first, last = i == 0, i == nt - 1

            @pl.when((b == 0) & (i == 0))
            def _():
                dcw_ref[...] = jnp.zeros_like(dcw_ref)
                dsm_ref[...] = jnp.zeros_like(dsm_ref)

            win[0:CONV_HALO, :] = jnp.where(first, 0.0, _glu(ap_ref, gp_ref))
            win[CONV_HALO:CONV_WIN, :] = _glu(a_ref, g_ref)
            _preshift(win, shifted)
            yv = jnp.concatenate([y_ref[...], yn_ref[...]], axis=0)
            yh, rstd = _layer_norm_stats(yv)
            t = yh * lg_ref[...] + lb_ref[...]
            st = _sigmoid(t)
            dout = jnp.concatenate(
                [do_ref[...].astype(F32), jnp.where(last, 0.0, don_ref[...].astype(F32))], axis=0)
            dt = dout * st * (1.0 + t * (1.0 - st))
            dyh = dt * lg_ref[...]
            dy = rstd * (dyh - jnp.mean(dyh, axis=-1, keepdims=True)
                         - yh * jnp.mean(dyh * yh, axis=-1, keepdims=True))
            dyw[...] = dy
            _preshift(dyw, dshifted)
            dsm_ref[0:1, :] += jnp.sum(dy[0:CONV_ROWS], axis=0, keepdims=True)
            dsm_ref[1:2, :] += jnp.sum((dt * yh)[0:CONV_ROWS], axis=0, keepdims=True)
            dsm_ref[2:3, :] += jnp.sum(dt[0:CONV_ROWS], axis=0, keepdims=True)
            for c0 in range(0, TOK_WIDTH, LANES):
                lanes = slice(c0, c0 + LANES)
                dcw_acc = [jnp.zeros((SUBLANES, LANES), F32) for _ in range(CONV_W)]
                for r0 in range(0, CONV_ROWS, TAP_ROWS):
                    dyt = dyw[r0:r0 + TAP_ROWS, lanes]
                    dglu = jnp.zeros((TAP_ROWS, LANES), F32)
                    for w in range(CONV_W):
                        dcw_acc[w] = dcw_acc[w] + _fold_rows(dyt * _tap(win, shifted, lead + w, r0, lanes))
                        dglu = dglu + _tap(dyw, dshifted, CONV_W - 1 - w, r0, lanes) * cw_ref[w:w + 1, lanes]
                    avt = a_ref[r0:r0 + TAP_ROWS, lanes].astype(F32)
                    sgt = _sigmoid(g_ref[r0:r0 + TAP_ROWS, lanes].astype(F32))
                    dz_ref[r0:r0 + TAP_ROWS, lanes] = (dglu * sgt).astype(dz_ref.dtype)
                    dg_o[r0:r0 + TAP_ROWS, lanes] = (dglu * avt * sgt * (1.0 - sgt)).astype(dg_o.dtype)
                for w in range(CONV_W):
                    dcw_ref[w:w + 1, lanes] += jnp.sum(dcw_acc[w], axis=0, keepdims=True)

        @pl.when(which == 1)
        def _():
            dz_ref[...] = dg_o[...]

    def ahead(b, i, t):
        return jnp.minimum(b * nt + i + t, batch * nt - 1)

    def cur(c):
        return pl.BlockSpec((CONV_ROWS, TOK_WIDTH), lambda b, i, t: (ahead(b, i, t), c))

    def prev(c):
        return pl.BlockSpec((CONV_HALO, TOK_WIDTH), lambda b, i, t: (jnp.maximum(ahead(b, i, t) * sub - 1, 0), c))

    nxt = pl.BlockSpec((CONV_HALO, TOK_WIDTH),
                       lambda b, i, t: (jnp.minimum((ahead(b, i, t) + 1) * sub, last_blk), 0))
    vec = pl.BlockSpec((1, TOK_WIDTH), lambda b, i, t: (0, 0))
    full32 = pl.BlockSpec((32, TOK_WIDTH), lambda b, i, t: (0, 0))
    return pl.pallas_call(
        body,
        out_shape=(jax.ShapeDtypeStruct(z.shape, BF16), jax.ShapeDtypeStruct((32, TOK_WIDTH), F32),
                   jax.ShapeDtypeStruct((8, TOK_WIDTH), F32)),
        grid=(batch, nt, 2),
        in_specs=[cur(0), cur(1), prev(0), prev(1), cur(0), nxt, cur(0), nxt, full32, vec, vec],
        out_specs=(pl.BlockSpec((CONV_ROWS, TOK_WIDTH), lambda b, i, t: (b * nt + i, t)), full32,
                   pl.BlockSpec((8, TOK_WIDTH), lambda b, i, t: (0, 0))),
        scratch_shapes=[pltpu.VMEM((CONV_WIN, TOK_WIDTH), F32), pltpu.VMEM((SUBLANES - 1, SHIFT_ROWS, TOK_WIDTH), F32),
                        pltpu.VMEM((CONV_WIN, TOK_WIDTH), F32), pltpu.VMEM((SUBLANES - 1, SHIFT_ROWS, TOK_WIDTH), F32),
                        pltpu.VMEM((CONV_ROWS, TOK_WIDTH), BF16)],
        compiler_params=_params("arbitrary", "arbitrary", "arbitrary"), name="conv_bwd")(
            z, z, z, z, y, y, dcat, dcat, cw, lg, lb)


def loss_head(y, target):
    n, d = y.shape
    tm = _row_tile(n)
    nt = n // tm

    def body(y_ref, t_ref, dy_ref, dyb_ref, l_ref, acc_ref):
        i = pl.program_id(0)

        @pl.when(i == 0)
        def _():
            acc_ref[...] = jnp.zeros_like(acc_ref)

        err = y_ref[...] - t_ref[...]
        dy = err * (1.0 / d)
        dy_ref[...] = dy
        dyb_ref[...] = dy.astype(BF16)
        acc_ref[...] += jnp.sum(err * err, axis=0, keepdims=True)

        @pl.when(i == nt - 1)
        def _():
            total = jnp.sum(acc_ref[...], axis=-1, keepdims=True) * (0.5 / d)
            l_ref[...] = jnp.broadcast_to(total, l_ref.shape)

    row = pl.BlockSpec((tm, d), lambda i: (i, 0))
    return pl.pallas_call(
        body, out_shape=(jax.ShapeDtypeStruct((n, d), F32), jax.ShapeDtypeStruct((n, d), BF16),
                         jax.ShapeDtypeStruct((8, LANES), F32)), grid=(nt,),
        in_specs=[row, row], out_specs=(row, row, pl.BlockSpec((8, LANES), lambda i: (0, 0))),
        scratch_shapes=[pltpu.VMEM((1, d), F32)],
        compiler_params=_params("arbitrary"), name="loss_head")(y, target)


def col_sum(x, name="col_sum"):
    n, c = x.shape
    tm = _row_tile(n)

    def body(x_ref, o_ref):
        @pl.when(pl.program_id(0) == 0)
        def _():
            o_ref[...] = jnp.zeros_like(o_ref)

        o_ref[...] += jnp.sum(x_ref[...].astype(F32), axis=0, keepdims=True)

    return pl.pallas_call(
        body, out_shape=jax.ShapeDtypeStruct((1, c), F32), grid=(n // tm,),
        in_specs=[pl.BlockSpec((tm, c), lambda i: (i, 0))], out_specs=pl.BlockSpec((1, c), lambda i: (0, 0)),
        compiler_params=_params("arbitrary"), name=name)(x)


def adamw(w, g, m, v, name="adamw"):
    rows, cols = w.shape
    tr = rows
    for cand in (512, 256, 128, 64, 32, 16, 8):
        if rows % cand == 0 and rows > cand:
            tr = cand
            break
    c1 = 1.0 / (1.0 - ADAM_B1 ** ADAM_STEP)
    c2 = 1.0 / (1.0 - ADAM_B2 ** ADAM_STEP)

    def body(w_ref, g_ref, m_ref, v_ref, d_ref, nm_ref, nv_ref):
        gv = g_ref[...]
        nm = ADAM_B1 * m_ref[...] + (1.0 - ADAM_B1) * gv
        nv = ADAM_B2 * v_ref[...] + (1.0 - ADAM_B2) * (gv * gv)
        nm_ref[...] = nm
        nv_ref[...] = nv
        d_ref[...] = -ADAM_LR * ((nm * c1) / (jnp.sqrt(nv * c2) + ADAM_EPS) + ADAM_WD * w_ref[...])

    spec = pl.BlockSpec((tr, cols), lambda i: (i, 0))
    shape = jax.ShapeDtypeStruct((rows, cols), F32)
    return pl.pallas_call(
        body, out_shape=(shape, shape, shape), grid=(rows // tr,),
        in_specs=[spec, spec, spec, spec], out_specs=(spec, spec, spec),
        compiler_params=_params("parallel"), name=name)(w, g, m, v)


def _place():
    return lax.axis_index("x"), lax.axis_index("y"), lax.axis_index("c")


def _other_chips(x, y):
    return [(1 - x, y), (x, 1 - y), (1 - x, 1 - y)]


def small_exchange(slab, reduce):
    r = slab.shape[0]

    def body(in_ref, o_ref, *scratch):
        if reduce:
            buf, send_sems, recv_sems = scratch
        else:
            buf = o_ref
            send_sems, recv_sems = scratch
        x, y, c = _place()
        me = 4 * x + 2 * y + c
        buf[me] = in_ref[...]
        copies = []
        for k in range(1, N_DEV):
            peer = (x ^ (k >> 2), y ^ ((k >> 1) & 1), c ^ (k & 1))
            cp = pltpu.make_async_remote_copy(
                src_ref=in_ref, dst_ref=buf.at[me], send_sem=send_sems.at[k - 1], recv_sem=recv_sems.at[k - 1],
                device_id=peer, device_id_type=MESH)
            cp.start()
            copies.append(cp)
        for k in range(1, N_DEV):
            src = 4 * (x ^ (k >> 2)) + 2 * (y ^ ((k >> 1) & 1)) + (c ^ (k & 1))
            pltpu.make_async_remote_copy(
                src_ref=in_ref, dst_ref=buf.at[src], send_sem=send_sems.at[k - 1], recv_sem=recv_sems.at[k - 1],
                device_id=(x, y, c), device_id_type=MESH).wait_recv()
        for cp in copies:
            cp.wait_send()
        if reduce:
            total = buf[0]
            for d in range(1, N_DEV):
                total = total + buf[d]
            o_ref[...] = total

    sems = [pltpu.SemaphoreType.DMA((N_DEV - 1,)), pltpu.SemaphoreType.DMA((N_DEV - 1,))]
    if reduce:
        out_shape = jax.ShapeDtypeStruct((r, LANES), F32)
        scratch = [pltpu.VMEM((N_DEV, r, LANES), F32)] + sems
    else:
        out_shape = jax.ShapeDtypeStruct((N_DEV, r, LANES), F32)
        scratch = sems
    vmem = pl.BlockSpec(memory_space=pltpu.VMEM)
    return pl.pallas_call(
        body, out_shape=out_shape, in_specs=[vmem], out_specs=vmem, scratch_shapes=scratch,
        compiler_params=pltpu.CompilerParams(vmem_limit_bytes=VMEM_LIMIT),
        name="small_reduce" if reduce else "small_gather")(slab)


def reduce_small(arrays):
    na = len(arrays)

    def body(*refs):
        ins, outs, bufs = refs[:na], refs[na:2 * na], refs[2 * na:3 * na]
        send_sems, recv_sems = refs[3 * na:]
        x, y, c = _place()
        me = 4 * x + 2 * y + c
        copies = []
        for a in range(na):
            bufs[a][me] = ins[a][...]
            for k in range(1, N_DEV):
                cp = pltpu.make_async_remote_copy(
                    src_ref=ins[a], dst_ref=bufs[a].at[me], send_sem=send_sems.at[a, k - 1],
                    recv_sem=recv_sems.at[a, k - 1],
                    device_id=(x ^ (k >> 2), y ^ ((k >> 1) & 1), c ^ (k & 1)), device_id_type=MESH)
                cp.start()
                copies.append(cp)
        for a in range(na):
            for k in range(1, N_DEV):
                src = 4 * (x ^ (k >> 2)) + 2 * (y ^ ((k >> 1) & 1)) + (c ^ (k & 1))
                pltpu.make_async_remote_copy(
                    src_ref=ins[a], dst_ref=bufs[a].at[src], send_sem=send_sems.at[a, k - 1],
                    recv_sem=recv_sems.at[a, k - 1], device_id=(x, y, c), device_id_type=MESH).wait_recv()
        for cp in copies:
            cp.wait_send()
        for a in range(na):
            total = bufs[a][0]
            for dev in range(1, N_DEV):
                total = total + bufs[a][dev]
            outs[a][...] = total

    vmem = pl.BlockSpec(memory_space=pltpu.VMEM)
    return pl.pallas_call(
        body, out_shape=tuple(jax.ShapeDtypeStruct(a.shape, F32) for a in arrays),
        in_specs=[vmem] * na, out_specs=tuple([vmem] * na),
        scratch_shapes=[pltpu.VMEM((N_DEV,) + a.shape, F32) for a in arrays]
        + [pltpu.SemaphoreType.DMA((na, N_DEV - 1)), pltpu.SemaphoreType.DMA((na, N_DEV - 1))],
        compiler_params=pltpu.CompilerParams(vmem_limit_bytes=VMEM_LIMIT), name="small_reduce")(*arrays)


def adamw_small(ws, gs, ms, vs):
    na = len(ws)
    c1 = 1.0 / (1.0 - ADAM_B1 ** ADAM_STEP)
    c2 = 1.0 / (1.0 - ADAM_B2 ** ADAM_STEP)

    def body(*refs):
        w_refs, g_refs, m_refs, v_refs = (refs[i * na:(i + 1) * na] for i in range(4))
        d_refs, nm_refs, nv_refs = (refs[(4 + i) * na:(5 + i) * na] for i in range(3))
        for a in range(na):
            gv = g_refs[a][...]
            nm = ADAM_B1 * m_refs[a][...] + (1.0 - ADAM_B1) * gv
            nv = ADAM_B2 * v_refs[a][...] + (1.0 - ADAM_B2) * (gv * gv)
            nm_refs[a][...] = nm
            nv_refs[a][...] = nv
            d_refs[a][...] = -ADAM_LR * ((nm * c1) / (jnp.sqrt(nv * c2) + ADAM_EPS) + ADAM_WD * w_refs[a][...])

    vmem = pl.BlockSpec(memory_space=pltpu.VMEM)
    shapes = tuple(jax.ShapeDtypeStruct(w.shape, F32) for w in ws)
    outs = pl.pallas_call(
        body, out_shape=shapes * 3, in_specs=[vmem] * (4 * na), out_specs=tuple([vmem] * (3 * na)),
        compiler_params=pltpu.CompilerParams(vmem_limit_bytes=VMEM_LIMIT), name="adamw_small")(*ws, *gs, *ms, *vs)
    return outs[:na], outs[na:2 * na], outs[2 * na:]


def gather_weights(shards, name, collective_id):
    nw = len(shards)
    ns = [s.shape[0] for s in shards]
    in_refs = [jax.new_ref(s, memory_space=pltpu.MemorySpace.HBM) for s in shards]
    out_refs = [jax.empty_ref(jax.ShapeDtypeStruct((N_DEV * s.shape[0], s.shape[1]), s.dtype),
                              memory_space=pltpu.MemorySpace.HBM) for s in shards]

    @pl.kernel(mesh=plsc.ScalarSubcoreMesh(axis_name="seq", num_cores=1), name=name,
               scratch_types=(pltpu.SemaphoreType.DMA((nw, 7)), pltpu.SemaphoreType.DMA((nw, 7)),
                              pltpu.SemaphoreType.DMA((nw,))),
               compiler_params=pltpu.CompilerParams(collective_id=collective_id))
    def launch(send_sems, recv_sems, local_sems):
        x, y, c = _place()
        me, sib = (x, y, c), (x, y, 1 - c)
        chips = _other_chips(x, y)
        barrier = pltpu.get_barrier_semaphore()
        for peer in [sib] + [(*chip, c) for chip in chips]:
            pl.semaphore_signal(barrier, inc=1, device_id=peer, device_id_type=MESH)
        pl.semaphore_wait(barrier, 4)

        def rows(w, dev):
            return out_refs[w].at[pl.ds((4 * dev[0] + 2 * dev[1] + dev[2]) * ns[w], ns[w]), :]

        def copy(w, k, block, to, src=None):
            return pltpu.make_async_remote_copy(
                src_ref=rows(w, block) if src is None else src, dst_ref=rows(w, block),
                send_sem=send_sems.at[w, k], recv_sem=recv_sems.at[w, k], device_id=to, device_id_type=MESH)

        started, sends = [], []
        for w in range(nw):
            mine = pltpu.make_async_copy(in_refs[w], rows(w, me), local_sems.at[w])
            mine.start()
            started.append(mine)
            first = [copy(w, 0, me, sib, src=in_refs[w])]
            first += [copy(w, 1 + j, me, (*chip, c), src=in_refs[w]) for j, chip in enumerate(chips)]
            for cp in first:
                cp.start()
            sends += first
        for w in range(nw):
            for j, chip in enumerate(chips):
                copy(w, 1 + j, (*chip, c), me).wait_recv()
                fwd = copy(w, 4 + j, (*chip, c), sib)
                fwd.start()
                sends.append(fwd)
        for w in range(nw):
            copy(w, 0, sib, me).wait_recv()
            for j, chip in enumerate(chips):
                copy(w, 4 + j, (*chip, 1 - c), me).wait_recv()
        for cp in sends:
            cp.wait_send()
        for mine in started:
            mine.wait()

    launch()
    return [r[...] for r in out_refs]


def _sequencer_exchange(sources, out_rows, peers_of, copies_of, name, collective_id):
    nw = len(sources)
    in_refs = [jax.new_ref(s, memory_space=pltpu.MemorySpace.HBM) for s in sources]
    out_refs = [jax.empty_ref(jax.ShapeDtypeStruct((rows, s.shape[1]), s.dtype), memory_space=pltpu.MemorySpace.HBM)
                for rows, s in zip(out_rows, sources)]
    per = len(copies_of(0, 0, 0, 0))

    @pl.kernel(mesh=plsc.ScalarSubcoreMesh(axis_name="seq", num_cores=1), name=name,
               scratch_types=(pltpu.SemaphoreType.DMA((nw, per)), pltpu.SemaphoreType.DMA((nw, per))),
               compiler_params=pltpu.CompilerParams(collective_id=collective_id))
    def launch(send_sems, recv_sems):
        x, y, c = _place()
        peers = peers_of(x, y, c)
        barrier = pltpu.get_barrier_semaphore()
        for peer in peers:
            pl.semaphore_signal(barrier, inc=1, device_id=peer, device_id_type=MESH)
        pl.semaphore_wait(barrier, len(peers))
        copies = []
        for w in range(nw):
            for k, (src_blk, dst_blk, rows, peer) in enumerate(copies_of(x, y, c, w)):
                cp = pltpu.make_async_remote_copy(
                    src_ref=in_refs[w].at[pl.ds(src_blk * rows, rows), :],
                    dst_ref=out_refs[w].at[pl.ds(dst_blk * rows, rows), :],
                    send_sem=send_sems.at[w, k], recv_sem=recv_sems.at[w, k], device_id=peer, device_id_type=MESH)
                cp.start()
                copies.append(cp)
        for cp in copies:
            cp.wait_recv()
        for cp in copies:
            cp.wait_send()

    launch()
    return [r[...] for r in out_refs]


def scatter_to_sibling(grads, name, collective_id):
    ns = [g.shape[0] // N_DEV for g in grads]
    return _sequencer_exchange(
        grads, [4 * n for n in ns],
        lambda x, y, c: [(x, y, 1 - c)],
        lambda x, y, c, w: [(2 * q + 1 - c, q, ns[w], (x, y, 1 - c)) for q in range(4)],
        name, collective_id)


def scatter_to_chips(parts, name, collective_id):
    ns = [p.shape[0] // 4 for p in parts]
    return _sequencer_exchange(
        parts, [3 * n for n in ns],
        lambda x, y, c: [(*chip, c) for chip in _other_chips(x, y)],
        lambda x, y, c, w: [(2 * chip[0] + chip[1], j, ns[w], (*chip, c)) for j, chip in enumerate(_other_chips(x, y))],
        name, collective_id)


def add_sibling(grads, landeds, core, name):
    nw = len(grads)

    def body(c_ref, *refs):
        for w in range(nw):
            g_ref, l_ref, o_ref = refs[2 * w], refs[2 * w + 1], refs[2 * nw + w]
            o_ref[...] = (g_ref[...].astype(F32) + l_ref[...].astype(F32)).astype(o_ref.dtype)

    in_specs, out_specs, args = [], [], []
    for g, ld in zip(grads, landeds):
        n, cols = ld.shape[0] // 4, g.shape[1]
        in_specs += [pl.BlockSpec((n, cols), lambda q, c_ref: (2 * q + c_ref[0], 0)),
                     pl.BlockSpec((n, cols), lambda q, c_ref: (q, 0))]
        out_specs.append(pl.BlockSpec((n, cols), lambda q, c_ref: (q, 0)))
        args += [g, ld]
    grid_spec = pltpu.PrefetchScalarGridSpec(
        num_scalar_prefetch=1, grid=(4,), in_specs=in_specs, out_specs=tuple(out_specs))
    return pl.pallas_call(
        body, out_shape=tuple(jax.ShapeDtypeStruct(ld.shape, ld.dtype) for ld in landeds), grid_spec=grid_spec,
        compiler_params=_params("arbitrary"), name=name)(core, *args)


def adamw_shard(layer, w, m, v, part, landed, chip, earlier, name):
    n = landed.shape[0] // 3
    cols = w.shape[1]
    c1 = 1.0 / (1.0 - ADAM_B1 ** ADAM_STEP)
    c2 = 1.0 / (1.0 - ADAM_B2 ** ADAM_STEP)

    def body(q_ref, w_ref, m_ref, v_ref, p_ref, l0_ref, l1_ref, l2_ref, *rest):
        g_ref, d_ref, nm_ref, nv_ref = rest[-4:]
        gv = ((p_ref[...].astype(F32) + l0_ref[...].astype(F32)) + l1_ref[...].astype(F32)) + l2_ref[...].astype(F32)
        nm = ADAM_B1 * m_ref[...] + (1.0 - ADAM_B1) * gv
        nv = ADAM_B2 * v_ref[...] + (1.0 - ADAM_B2) * (gv * gv)
        g_ref[...] = gv
        nm_ref[...] = nm
        nv_ref[...] = nv
        d_ref[...] = -ADAM_LR * ((nm * c1) / (jnp.sqrt(nv * c2) + ADAM_EPS) + ADAM_WD * w_ref[...])

    sub = 2 if n % (2 * 16) == 0 else 1
    rows = n // sub
    own = pl.BlockSpec((rows, cols), lambda i, q_ref: (layer * sub + i, 0))

    def landed_spec(j):
        return pl.BlockSpec((rows, cols), lambda i, q_ref: (j * sub + i, 0))

    in_specs = [own, own, own, pl.BlockSpec((rows, cols), lambda i, q_ref: (q_ref[0] * sub + i, 0)),
                landed_spec(0), landed_spec(1), landed_spec(2)]
    args = [chip, w, m, v, part, landed, landed, landed]
    aliases = {}
    if earlier is not None:
        in_specs += [ANY] * 4
        args += list(earlier)
        aliases = {8 + k: k for k in range(4)}
    grid_spec = pltpu.PrefetchScalarGridSpec(
        num_scalar_prefetch=1, grid=(sub,), in_specs=in_specs, out_specs=(own, own, own, own))
    shape = jax.ShapeDtypeStruct(w.shape, F32)
    return pl.pallas_call(
        body, out_shape=(shape, shape, shape, shape), grid_spec=grid_spec, input_output_aliases=aliases,
        compiler_params=_params("arbitrary"), name=name)(*args)


def _pack(arrays):
    flat = jnp.concatenate([a.reshape(-1).astype(F32) for a in arrays])
    pad = (-flat.shape[0]) % (8 * LANES)
    return jnp.pad(flat, (0, pad)).reshape(-1, LANES)


def _unpack(slab, shapes):
    flat = slab.reshape(slab.shape[:-2] + (-1,))
    out, off = [], 0
    for shp in shapes:
        size = 1
        for s in shp:
            size *= s
        out.append(flat[..., off:off + size].reshape(flat.shape[:-1] + tuple(shp)))
        off += size
    return out


def kernel(x, mem, norm1_g, mem_norm_g, a_w_in, a_q_g, a_k_g, a_rel_bias, b_w_in, b_b_in, b_conv_w, b_conv_b, b_ln_g, b_ln_b, mq_g, mk_g, w_mem_kv, w_out, norm2_g, w_gate, w_up, w_down, loss_target, m_norm1_g, m_mem_norm_g, m_a_w_in, m_a_q_g, m_a_k_g, m_a_rel_bias, m_b_w_in, m_b_b_in, m_b_conv_w, m_b_conv_b, m_b_ln_g, m_b_ln_b, m_mq_g, m_mk_g, m_w_mem_kv, m_w_out, m_norm2_g, m_w_gate, m_w_up, m_w_down, v_norm1_g, v_mem_norm_g, v_a_w_in, v_a_q_g, v_a_k_g, v_a_rel_bias, v_b_w_in, v_b_b_in, v_b_conv_w, v_b_conv_b, v_b_ln_g, v_b_ln_b, v_mq_g, v_mk_g, v_w_mem_kv, v_w_out, v_norm2_g, v_w_gate, v_w_up, v_w_down):
    batch, seq, d = x.shape
    mtok = mem.shape[1]
    n = batch * seq
    ax, ay, ac = _place()
    me = 4 * ax + 2 * ay + ac
    core_arr = jnp.reshape(ac, (1,)).astype(jnp.int32)
    chip_arr = jnp.reshape(2 * ax + ay, (1,)).astype(jnp.int32)

    def t_bf16(w):
        return jnp.transpose(w).astype(BF16)

    def after(value, *earlier):
        return lax.optimization_barrier((value, *earlier))[0]

    def gather_mix(l, when, name, collective_id):
        srcs = [w_mem_kv[l].astype(BF16), w_out[l].astype(BF16)]
        if l == 1:
            srcs += [t_bf16(b_w_in[0]), _pack([b_b_in, b_conv_w, b_conv_b, b_ln_g, b_ln_b])]
        return gather_weights([after(srcs[0], when)] + srcs[1:], name, collective_id)

    def gather_ffn(l, when, name, collective_id):
        return gather_weights(
            [after(t_bf16(w_gate[l]), when), t_bf16(w_up[l]), w_down[l].astype(BF16)], name, collective_id)

    f_loc = b_b_in.shape[1]
    c_loc = b_conv_b.shape[1]

    def two(g):
        return jnp.concatenate([g, g], axis=-1)

    gq2, gk2 = two(a_q_g), two(a_k_g)
    rel16 = jnp.pad(a_rel_bias[0], ((0, 16 - a_rel_bias.shape[1]), (0, 0)))
    bias = bias_blocks(rel16)

    x0 = x.reshape(n, d)
    mem2 = mem.reshape(batch * mtok, d)

    saved = []
    xin = x0
    a_win_t, = gather_weights([t_bf16(a_w_in[0])], "gather_in_a", 1)
    wg_t, wu_t, wd, wo, wkv = [None] * 2, [None] * 2, [None] * 2, [None] * 2, [None] * 2
    h = after(rms_fwd(xin, norm1_g[0:1], name="rms1_fwd_0"), bias)
    target = loss_target.reshape(n, d)
    for l in range(2):
        gq4 = jnp.tile(mq_g[l:l + 1], (1, 4))
        gk4 = jnp.tile(mk_g[l:l + 1], (1, 4))
        y_conv = None
        if l == 0:
            wkv[0], wo[0] = gather_mix(0, h, "gather_mix_a", 2)
            z = mm_nt(h, a_win_t, name="in_proj_a")
            wg_t[0], wu_t[0], wd[0] = gather_ffn(0, z, "gather_ffn_a", 3)
            cat = attn_fwd(z, gq2, gk2, bias, batch, seq)
            wkv[1], wo[1], b_win_t, conv_slabs = gather_mix(1, cat, "gather_mix_b", 4)
            qcol = 3 * TOK_WIDTH // MEM_WIDTH
        else:
            small_shapes = [(f_loc,), (CONV_W, c_loc), (c_loc,), (c_loc,), (c_loc,)]
            bb_g, cw_g, cb_g, lg_g, lb_g = _unpack(conv_slabs.reshape(N_DEV, -1, LANES), small_shapes)
            bb_full = bb_g.reshape(1, -1)
            cw_full = jnp.pad(jnp.transpose(cw_g, (1, 0, 2)).reshape(CONV_W, -1), ((0, 32 - CONV_W), (0, 0)))
            cb_full, lg_full, lb_full = cb_g.reshape(1, -1), lg_g.reshape(1, -1), lb_g.reshape(1, -1)
            z = mm_nt(h, b_win_t, bias=bb_full, name="in_proj_b")
            cat, y_conv = conv_fwd(z, cw_full, cb_full, lg_full, lb_full, batch, seq)
            qcol = 2 * TOK_WIDTH // MEM_WIDTH
        mem_n, kv = mem_prep(mem2, mem_norm_g[l:l + 1], wkv[l], name=f"mem_prep_{l}")
        cat = memattn_fwd(z, kv, gq4, gk4, cat, batch, seq, qcol, name=f"memattn_fwd_{l}")
        x1, h2 = proj_norm(cat, wo[l], xin, norm2_g[l:l + 1], name=f"out_proj_{l}")
        if l == 0:
            wg_t[1], wu_t[1], wd[1] = gather_ffn(1, x1, "gather_ffn_b", 5)
        if l == 0:
            gate, up, act, x2, h_next = ffn_fwd(h2, wg_t[0], wu_t[0], wd[0], x1, gain=norm1_g[1:2], name="ffn_fwd_0")
        else:
            gate, up, act, dx_b, loss_blk = ffn_fwd(h2, wg_t[1], wu_t[1], wd[1], x1, target=target, name="ffn_fwd_1")
        saved.append(dict(xin=xin, h=h, mem_n=mem_n, kv=kv, gq4=gq4, gk4=gk4, z=z, qcol=qcol, cat=cat, x1=x1, h2=h2,
                          gate=gate, up=up, act=act, y_conv=y_conv))
        if l == 0:
            xin, h = x2, h_next

    big = {}
    small = {}
    reduced = {}
    groups = 0

    def scatter_siblings(keys):
        nonlocal groups
        gid = groups
        groups += 1
        return gid, keys, scatter_to_sibling([big[k] for k in keys], f"scatter_sibling_{gid}", 8 + 2 * gid)

    def scatter_chips(stage1, when):
        gid, keys, landed1 = stage1
        parts = add_sibling([after(big[keys[0]], when)] + [big[k] for k in keys[1:]], landed1, core_arr,
                            name=f"add_sibling_{gid}")
        landed2 = scatter_to_chips(parts, f"scatter_chips_{gid}", 9 + 2 * gid)
        for k, p, ld in zip(keys, parts, landed2):
            reduced[k] = (p, ld)
        return parts, landed2

    def rows_of(w, transposed):
        w = jnp.swapaxes(w, 1, 2) if transposed else w
        return w.reshape(w.shape[0] * w.shape[1], w.shape[2])

    sharded = {
        "win0": (2, True), "win1": (6, True), "wkv": (14, False), "wo": (15, False),
        "wg": (17, True), "wu": (18, True), "wd": (19, False)}
    weights = [norm1_g, mem_norm_g, a_w_in, a_q_g, a_k_g, a_rel_bias, b_w_in, b_b_in, b_conv_w, b_conv_b, b_ln_g,
               b_ln_b, mq_g, mk_g, w_mem_kv, w_out, norm2_g, w_gate, w_up, w_down]
    moms = [m_norm1_g, m_mem_norm_g, m_a_w_in, m_a_q_g, m_a_k_g, m_a_rel_bias, m_b_w_in, m_b_b_in, m_b_conv_w,
            m_b_conv_b, m_b_ln_g, m_b_ln_b, m_mq_g, m_mk_g, m_w_mem_kv, m_w_out, m_norm2_g, m_w_gate, m_w_up, m_w_down]
    vels = [v_norm1_g, v_mem_norm_g, v_a_w_in, v_a_q_g, v_a_k_g, v_a_rel_bias, v_b_w_in, v_b_b_in, v_b_conv_w,
            v_b_conv_b, v_b_ln_g, v_b_ln_b, v_mq_g, v_mk_g, v_w_mem_kv, v_w_out, v_norm2_g, v_w_gate, v_w_up, v_w_down]
    updated = {}

    def update_layer(l, when):
        for key, (idx, transposed) in sharded.items():
            if key in ("win0", "win1"):
                if key != f"win{l}":
                    continue
                layer, rkey = 0, key
            else:
                layer, rkey = l, f"{key}{l}"
            part, landed = reduced[rkey]
            updated[key] = adamw_shard(
                layer, after(rows_of(weights[idx], transposed), when), rows_of(moms[idx], transposed),
                rows_of(vels[idx], transposed), part, landed, chip_arr, updated.get(key), name=f"adamw_{rkey}")

    mix_landed = None
    for l in (1, 0):
        sv = saved[l]
        dgate, dup, dx1_b, dcat, small[f"norm2_{l}"] = ffn_bwd(
            dx_b, wd[l], sv["gate"], sv["up"], wg_t[l], wu_t[l], sv["x1"], norm2_g[l:l + 1], wo[l], name=f"ffn_bwd_{l}")
        if l == 0:
            dgate = after(dgate, *mix_landed)
            update_layer(1, dx1_b)
        big[f"wg{l}"], big[f"wu{l}"], big[f"wd{l}"] = ffn_weight_grads(
            dgate, dup, sv["h2"], sv["act"], dx_b, name=f"grad_ffn_{l}")
        stage1 = scatter_siblings([f"wd{l}", f"wg{l}", f"wu{l}"])
        big[f"wo{l}"] = mm_tn(sv["cat"], dx1_b, name=f"grad_wo_{l}")
        parts, ffn_landed = scatter_chips(stage1, big[f"wo{l}"])
        dcat = after(dcat, *parts)
        if l == 0:
            dz, dbias, small["a_q"], small["a_k"] = attn_bwd(sv["z"], dcat, gq2, gk2, bias, batch, seq)
            small["rel"] = bias_grad(dbias)
            win_t = a_win_t
        else:
            dz, small["cw"], small["csum"] = conv_bwd(sv["z"], sv["y_conv"], dcat, cw_full, lg_full, lb_full, batch, seq)
            win_t = b_win_t
        dz = after(dz, *ffn_landed)
        dz, dkv, small[f"mq_{l}"], small[f"mk_{l}"] = memattn_bwd(
            sv["z"], sv["kv"], dcat, sv["gq4"], sv["gk4"], dz, batch, seq, sv["qcol"], name=f"memattn_bwd_{l}")
        big[f"win{l}"] = mm_tn(dz, sv["h"], name=f"grad_win_{l}")
        big[f"wkv{l}"] = mm_tn(sv["mem_n"], dkv, name=f"grad_wkv_{l}")
        stage1 = scatter_siblings([f"win{l}", f"wkv{l}", f"wo{l}"])
        dx_b, small[f"norm1_{l}"], dz_sum = in_proj_bwd(
            dz, win_t, sv["xin"], norm1_g[l:l + 1], dx1_b, BF16 if l == 1 else F32, name=f"in_proj_bwd_{l}")
        if l == 1:
            small["bb"] = dz_sum
        parts, mix_landed = scatter_chips(stage1, dx_b)
        dx_b = after(dx_b, *parts)
        small[f"memnorm_{l}"] = mem_norm_grad(dkv, wkv[l], mem2, name=f"mem_norm_grad_{l}")
    grad_x = dx_b.reshape(batch, seq, d)
    update_layer(0, dx_b)

    def shaped(rows, idx, transposed):
        shp = weights[idx].shape
        if transposed:
            return jnp.swapaxes(rows.reshape(shp[0], shp[2], shp[1]), 1, 2)
        return rows.reshape(shp)

    def fold(v, groups):
        return jnp.sum(v.reshape(groups, HEAD_DIM), axis=0, keepdims=True)

    heads = a_rel_bias.shape[1]
    small_list = [
        jnp.concatenate([small["norm1_0"], small["norm1_1"]]),
        jnp.concatenate([small["memnorm_0"], small["memnorm_1"]]),
        fold(small["a_q"], 2), fold(small["a_k"], 2), small["rel"][:heads][None],
        small["bb"], small["cw"][:CONV_W][None], small["csum"][0:1], small["csum"][1:2], small["csum"][2:3],
        jnp.concatenate([fold(small["mq_0"], 4), fold(small["mq_1"], 4)]),
        jnp.concatenate([fold(small["mk_0"], 4), fold(small["mk_1"], 4)]),
        jnp.concatenate([small["norm2_0"], small["norm2_1"]]),
    ]
    (g_norm1, g_memnorm, g_aq, g_ak, g_rel, g_bb_full, g_cw_full, g_cb_full, g_lg_full, g_lb_full,
     g_mq, g_mk, g_norm2, loss_sum) = reduce_small(small_list + [loss_blk])
    loss = loss_sum[0, 0]
    g_bb = lax.dynamic_slice_in_dim(g_bb_full, me * f_loc, f_loc, axis=1)
    g_cw = lax.dynamic_slice_in_dim(g_cw_full, me * c_loc, c_loc, axis=2)
    g_cb = lax.dynamic_slice_in_dim(g_cb_full, me * c_loc, c_loc, axis=1)
    g_lg = lax.dynamic_slice_in_dim(g_lg_full, me * c_loc, c_loc, axis=1)
    g_lb = lax.dynamic_slice_in_dim(g_lb_full, me * c_loc, c_loc, axis=1)

    grads = [g_norm1, g_memnorm, None, g_aq, g_ak, g_rel, None, g_bb, g_cw, g_cb, g_lg, g_lb,
             g_mq, g_mk, None, None, g_norm2, None, None, None]
    deltas, new_m, new_v = [None] * 20, [None] * 20, [None] * 20
    for key, (idx, transposed) in sharded.items():
        grads[idx], deltas[idx], new_m[idx], new_v[idx] = (shaped(r, idx, transposed) for r in updated[key])

    small_idx = [i for i in range(20) if i not in {idx for idx, _ in sharded.values()}]
    dl, nm, nv = adamw_small([weights[i] for i in small_idx], [grads[i] for i in small_idx],
                             [moms[i] for i in small_idx], [vels[i] for i in small_idx])
    for i, a, b, cc in zip(small_idx, dl, nm, nv):
        deltas[i], new_m[i], new_v[i] = a, b, cc

    return (loss, grad_x, *grads, *deltas, *new_m, *new_v)
```

```python
import functools

import jax
import jax.numpy as jnp
from jax import lax
from jax.experimental import pallas as pl
from jax.experimental.pallas import tpu as pltpu
from jax.experimental.pallas import tpu_sc as plsc

F32 = jnp.float32
BF16 = jnp.bfloat16
HIGHEST = lax.Precision.HIGHEST
MESH = pl.DeviceIdType.MESH
ANY = pl.BlockSpec(memory_space=pl.ANY)

N_DEV = 8
D_MODEL = 1024
HEAD_DIM = 64
TOK_WIDTH = 768
MEM_WIDTH = 256
CHUNK = 64
Q_BLOCK = 256
KEY_WIN = 768
BAND = 576
N_REL = 192
CONV_W = 31
CONV_HALO = 32
NORM_EPS = 1e-6
NEG_INF = -1e30
ATTN_SCALE = HEAD_DIM ** -0.5
LANES = 128
ROW_TILE = 512
VMEM_LIMIT = 56 * 1024 * 1024

ADAM_LR, ADAM_B1, ADAM_B2, ADAM_EPS, ADAM_WD, ADAM_STEP = 0.001, 0.9, 0.999, 1e-08, 0.01, 10


def _params(*sem):
    return pltpu.CompilerParams(dimension_semantics=sem, vmem_limit_bytes=VMEM_LIMIT)


WIDE_ROW_TILE = 1024


def _row_tile(m, rows=ROW_TILE):
    return rows if m % rows == 0 else m


def _col_tile(n, cap=1408):
    best = None
    for t in range(LANES, min(n, cap) + 1, LANES):
        if n % t == 0:
            best = t
    return best if best is not None else n


def _dot(a, b, ca, cb):
    return lax.dot_general(a, b, (((ca,), (cb,)), ((), ())), preferred_element_type=F32)


def _sigmoid(x):
    return 0.5 * jnp.tanh(0.5 * x) + 0.5


def mm_nt(a, b, bias=None, out_dtype=BF16, name="mm_nt"):
    m, k = a.shape
    n = b.shape[0]
    tm, tn = _row_tile(m, WIDE_ROW_TILE), _col_tile(n)

    def body(*refs):
        a_ref, b_ref = refs[0], refs[1]
        o_ref = refs[-1]
        acc = _dot(a_ref[...].astype(BF16), b_ref[...].astype(BF16), 1, 1)
        if bias is not None:
            acc = acc + refs[2][...]
        o_ref[...] = acc.astype(o_ref.dtype)

    in_specs = [pl.BlockSpec((tm, k), lambda j, i: (i, 0)), pl.BlockSpec((tn, k), lambda j, i: (j, 0))]
    args = [a, b]
    if bias is not None:
        in_specs.append(pl.BlockSpec((1, tn), lambda j, i: (0, j)))
        args.append(bias)
    return pl.pallas_call(
        body, out_shape=jax.ShapeDtypeStruct((m, n), out_dtype), grid=(n // tn, m // tm),
        in_specs=in_specs, out_specs=pl.BlockSpec((tm, tn), lambda j, i: (i, j)),
        compiler_params=_params("parallel", "arbitrary"), name=name)(*args)


def mm_tn(a, b, out_dtype=BF16, name="mm_tn"):
    t, r = a.shape
    c = b.shape[1]
    tr = _col_tile(r, 512)

    def body(a_ref, b_ref, o_ref):
        o_ref[...] = _dot(a_ref[...].astype(BF16), b_ref[...].astype(BF16), 0, 0).astype(o_ref.dtype)

    return pl.pallas_call(
        body, out_shape=jax.ShapeDtypeStruct((r, c), out_dtype), grid=(r // tr,),
        in_specs=[pl.BlockSpec((t, tr), lambda i: (0, i)), pl.BlockSpec((t, c), lambda i: (0, 0))],
        out_specs=pl.BlockSpec((tr, c), lambda i: (i, 0)),
        compiler_params=_params("parallel"), name=name)(a, b)


def _resident(shape):
    return pl.BlockSpec(shape, lambda i: (0, 0), pipeline_mode=pl.Buffered(1))


def proj_norm(a, b, res, gain, name):
    m, k = a.shape
    n = b.shape[1]
    tm = _row_tile(m, WIDE_ROW_TILE)

    def body(a_ref, b_ref, res_ref, g_ref, x_ref, h_ref):
        xv = res_ref[...] + _dot(a_ref[...], b_ref[...], 1, 0)
        x_ref[...] = xv
        r = lax.rsqrt(jnp.mean(xv * xv, axis=-1, keepdims=True) + NORM_EPS)
        h_ref[...] = (xv * r * g_ref[...]).astype(BF16)

    row = pl.BlockSpec((tm, n), lambda i: (i, 0))
    return pl.pallas_call(
        body, out_shape=(jax.ShapeDtypeStruct((m, n), F32), jax.ShapeDtypeStruct((m, n), BF16)), grid=(m // tm,),
        in_specs=[pl.BlockSpec((tm, k), lambda i: (i, 0)), _resident((k, n)), row, _resident((1, n))],
        out_specs=(row, row), compiler_params=_params("parallel"), name=name)(a, b, res, gain)


def in_proj_bwd(dz, w_t, x, gain, dres, out_dtype, name):
    m, n = x.shape
    k = dz.shape[1]
    tm = _row_tile(m, WIDE_ROW_TILE)

    def body(dz_ref, w_ref, x_ref, g_ref, dres_ref, dx_ref, dg_ref, cs_ref):
        @pl.when(pl.program_id(0) == 0)
        def _():
            dg_ref[...] = jnp.zeros_like(dg_ref)
            cs_ref[...] = jnp.zeros_like(cs_ref)

        dzv = dz_ref[...]
        cs_ref[...] += jnp.sum(dzv.astype(F32), axis=0, keepdims=True)
        dhv = _dot(dzv, w_ref[...], 1, 0)
        xv = x_ref[...]
        r = lax.rsqrt(jnp.mean(xv * xv, axis=-1, keepdims=True) + NORM_EPS)
        xhat = xv * r
        dg_ref[...] += jnp.sum(dhv * xhat, axis=0, keepdims=True)
        dxhat = dhv * g_ref[...]
        dx = dres_ref[...].astype(F32) + r * (dxhat - xhat * jnp.mean(dxhat * xhat, axis=-1, keepdims=True))
        dx_ref[...] = dx.astype(dx_ref.dtype)

    row = pl.BlockSpec((tm, n), lambda i: (i, 0))
    return pl.pallas_call(
        body, out_shape=(jax.ShapeDtypeStruct((m, n), out_dtype), jax.ShapeDtypeStruct((1, n), F32),
                         jax.ShapeDtypeStruct((1, k), F32)), grid=(m // tm,),
        in_specs=[pl.BlockSpec((tm, k), lambda i: (i, 0)), _resident(w_t.shape), row, _resident((1, n)), row],
        out_specs=(row, pl.BlockSpec((1, n), lambda i: (0, 0)), pl.BlockSpec((1, k), lambda i: (0, 0))),
        compiler_params=_params("arbitrary"), name=name)(dz, w_t, x, gain, dres)


FFN_ROWS = 256


def _ffn_row_tile(m):
    return FFN_ROWS if m % FFN_ROWS == 0 else m


def ffn_fwd(h2, wg_t, wu_t, wd, x1, gain=None, target=None, name="ffn_fwd"):
    n, d = h2.shape
    f = wg_t.shape[0]
    tm = _ffn_row_tile(n)
    nt = n // tm
    last = target is not None

    def body(h_ref, wg_ref, wu_ref, wd_ref, x1_ref, e_ref, g_ref, u_ref, a_ref, *rest):
        hv = h_ref[...]
        gv = _dot(hv, wg_ref[...], 1, 1)
        uv = _dot(hv, wu_ref[...], 1, 1)
        g_ref[...] = gv.astype(BF16)
        u_ref[...] = uv.astype(BF16)
        av = (gv * _sigmoid(gv) * uv).astype(BF16)
        a_ref[...] = av
        xv = x1_ref[...] + _dot(av, wd_ref[...], 1, 0)
        if not last:
            x_ref, hn_ref = rest
            x_ref[...] = xv
            r = lax.rsqrt(jnp.mean(xv * xv, axis=-1, keepdims=True) + NORM_EPS)
            hn_ref[...] = (xv * r * e_ref[...]).astype(BF16)
        else:
            dyb_ref, l_ref, acc_ref = rest
            i = pl.program_id(0)

            @pl.when(i == 0)
            def _():
                acc_ref[...] = jnp.zeros_like(acc_ref)

            err = xv - e_ref[...]
            dyb_ref[...] = (err * (1.0 / d)).astype(BF16)
            acc_ref[...] += jnp.sum(err * err, axis=0, keepdims=True)

            @pl.when(i == nt - 1)
            def _():
                total = jnp.sum(acc_ref[...], axis=-1, keepdims=True) * (0.5 / d)
                l_ref[...] = jnp.broadcast_to(total, l_ref.shape)

    row_d = pl.BlockSpec((tm, d), lambda i: (i, 0))
    row_f = pl.BlockSpec((tm, f), lambda i: (i, 0))
    act_shape = jax.ShapeDtypeStruct((n, f), BF16)
    if not last:
        extra_in, extra = _resident((1, d)), gain
        out_shape = (act_shape, act_shape, act_shape, jax.ShapeDtypeStruct((n, d), F32), jax.ShapeDtypeStruct((n, d), BF16))
        out_specs = (row_f, row_f, row_f, row_d, row_d)
        scratch = []
    else:
        extra_in, extra = row_d, target
        out_shape = (act_shape, act_shape, act_shape, jax.ShapeDtypeStruct((n, d), BF16),
                     jax.ShapeDtypeStruct((8, LANES), F32))
        out_specs = (row_f, row_f, row_f, row_d, pl.BlockSpec((8, LANES), lambda i: (0, 0)))
        scratch = [pltpu.VMEM((1, d), F32)]
    return pl.pallas_call(
        body, out_shape=out_shape, grid=(nt,),
        in_specs=[row_d, _resident((f, d)), _resident((f, d)), _resident((f, d)), row_d, extra_in],
        out_specs=out_specs, scratch_shapes=scratch,
        compiler_params=_params("arbitrary"), name=name)(h2, wg_t, wu_t, wd, x1, extra)


def ffn_bwd(dx_b, wd, gate, up, wg_t, wu_t, x1, gain, wo, name="ffn_bwd"):
    n, d = x1.shape
    f = wd.shape[0]
    tm = _ffn_row_tile(n)

    def body(dxb_ref, wd_ref, g_ref, u_ref, wg_ref, wu_ref, x_ref, gain_ref, wo_ref,
             dg_ref, du_ref, dxo_ref, dc_ref, dgain_ref):
        @pl.when(pl.program_id(0) == 0)
        def _():
            dgain_ref[...] = jnp.zeros_like(dgain_ref)

        dact = _dot(dxb_ref[...], wd_ref[...], 1, 1)
        gv = g_ref[...].astype(F32)
        uv = u_ref[...].astype(F32)
        sg = _sigmoid(gv)
        dgv = (dact * uv * sg * (1.0 + gv * (1.0 - sg))).astype(BF16)
        duv = (dact * gv * sg).astype(BF16)
        dg_ref[...] = dgv
        du_ref[...] = duv
        dhv = _dot(dgv, wg_ref[...], 1, 0) + _dot(duv, wu_ref[...], 1, 0)
        xv = x_ref[...]
        r = lax.rsqrt(jnp.mean(xv * xv, axis=-1, keepdims=True) + NORM_EPS)
        xhat = xv * r
        dgain_ref[...] += jnp.sum(dhv * xhat, axis=0, keepdims=True)
        dxhat = dhv * gain_ref[...]
        dxb = (dxb_ref[...].astype(F32) + r * (dxhat - xhat * jnp.mean(dxhat * xhat, axis=-1, keepdims=True))).astype(BF16)
        dxo_ref[...] = dxb
        dc_ref[...] = _dot(dxb, wo_ref[...], 1, 1).astype(BF16)

    row_d = pl.BlockSpec((tm, d), lambda i: (i, 0))
    row_f = pl.BlockSpec((tm, f), lambda i: (i, 0))
    w_spec = _resident((f, d))
    act_shape = jax.ShapeDtypeStruct((n, f), BF16)
    row_shape = jax.ShapeDtypeStruct((n, d), BF16)
    return pl.pallas_call(
        body, out_shape=(act_shape, act_shape, row_shape, jax.ShapeDtypeStruct((n, wo.shape[0]), BF16),
                         jax.ShapeDtypeStruct((1, d), F32)),
        grid=(n // tm,),
        in_specs=[row_d, w_spec, row_f, row_f, w_spec, w_spec, row_d, _resident((1, d)), _resident(wo.shape)],
        out_specs=(row_f, row_f, row_d, pl.BlockSpec((tm, wo.shape[0]), lambda i: (i, 0)),
                   pl.BlockSpec((1, d), lambda i: (0, 0))),
        compiler_params=_params("arbitrary"), name=name)(dx_b, wd, gate, up, wg_t, wu_t, x1, gain, wo)


def ffn_weight_grads(dgate, dup, h2, act, dx_b, name="ffn_weight_grads"):
    t, r = dgate.shape
    c = h2.shape[1]
    tr = _col_tile(r, 512)

    def body(a1_ref, a2_ref, a3_ref, b12_ref, b3_ref, o1_ref, o2_ref, o3_ref):
        bv = b12_ref[...]
        o1_ref[...] = _dot(a1_ref[...], bv, 0, 0).astype(o1_ref.dtype)
        o2_ref[...] = _dot(a2_ref[...], bv, 0, 0).astype(o2_ref.dtype)
        o3_ref[...] = _dot(a3_ref[...], b3_ref[...], 0, 0).astype(o3_ref.dtype)

    a_spec = pl.BlockSpec((t, tr), lambda i: (0, i))
    o_spec = pl.BlockSpec((tr, c), lambda i: (i, 0))
    shape = jax.ShapeDtypeStruct((r, c), BF16)
    return pl.pallas_call(
        body, out_shape=(shape, shape, shape), grid=(r // tr,),
        in_specs=[a_spec, a_spec, a_spec, _resident((t, c)), _resident((t, c))],
        out_specs=(o_spec, o_spec, o_spec), compiler_params=_params("parallel"), name=name)(dgate, dup, act, h2, dx_b)


def rms_fwd(x, g, name="rms_fwd"):
    n, d = x.shape
    tm = _row_tile(n)

    def body(x_ref, g_ref, o_ref):
        xv = x_ref[...]
        r = lax.rsqrt(jnp.mean(xv * xv, axis=-1, keepdims=True) + NORM_EPS)
        o_ref[...] = (xv * r * g_ref[...]).astype(o_ref.dtype)

    return pl.pallas_call(
        body, out_shape=jax.ShapeDtypeStruct((n, d), BF16), grid=(n // tm,),
        in_specs=[pl.BlockSpec((tm, d), lambda i: (i, 0)), pl.BlockSpec((1, d), lambda i: (0, 0))],
        out_specs=pl.BlockSpec((tm, d), lambda i: (i, 0)),
        compiler_params=_params("parallel"), name=name)(x, g)


def _group_masks(width):
    lane = lax.broadcasted_iota(jnp.int32, (1, width), 1)
    return [(lane >= HEAD_DIM * g) & (lane < HEAD_DIM * (g + 1)) for g in range(width // HEAD_DIM)]


def _group_sum(x, masks):
    out = jnp.zeros_like(x)
    for msk in masks:
        s = jnp.sum(jnp.where(msk, x, 0.0), axis=-1, keepdims=True)
        out = jnp.where(msk, s, out)
    return out


def _head_norm(x, gain, masks):
    r = lax.rsqrt(_group_sum(x * x, masks) * (1.0 / HEAD_DIM) + NORM_EPS)
    xhat = x * r
    return xhat * gain, xhat, r


def _head_norm_bwd(dxn, xhat, r, gain, masks):
    dgain = jnp.sum(dxn * xhat, axis=0, keepdims=True)
    dxhat = dxn * gain
    mean_t = _group_sum(dxhat * xhat, masks) * (1.0 / HEAD_DIM)
    return r * (dxhat - xhat * mean_t), dgain


def _softmax_rows(s):
    e = jnp.exp(s - jnp.max(s, axis=-1, keepdims=True))
    return e * (1.0 / jnp.sum(e, axis=-1, keepdims=True))


def _rel_onehot():
    col = lax.broadcasted_iota(jnp.int32, (1, KEY_WIN), 1)
    off = jnp.where(col < KEY_WIN - LANES, col, col - KEY_WIN)
    idx = jnp.clip(8 * CHUNK - off, -(CHUNK - 1), LANES) + (CHUNK - 1)
    return (lax.broadcasted_iota(jnp.int32, (N_REL, KEY_WIN), 0) == idx).astype(F32)


def bias_blocks(rel16):
    heads = TOK_WIDTH // HEAD_DIM

    def body(rel_ref, o_ref, u_ref):
        u_ref[...] = jnp.dot(rel_ref[...], _rel_onehot(), precision=HIGHEST, preferred_element_type=F32)
        row = lax.broadcasted_iota(jnp.int32, (CHUNK, KEY_WIN), 0)
        col = lax.broadcasted_iota(jnp.int32, (CHUNK, KEY_WIN), 1)
        for h in range(heads):
            xv = jnp.broadcast_to(u_ref[h:h + 1, :], (CHUNK, KEY_WIN))
            for b in range(6):
                xv = jnp.where(((row >> b) & 1) == 1, pltpu.roll(xv, 1 << b, axis=1), xv)
            xv = jnp.where(col < BAND, xv, NEG_INF)
            for i in range(Q_BLOCK // CHUNK):
                o_ref[h, CHUNK * i:CHUNK * (i + 1), :] = pltpu.roll(xv, CHUNK * i, axis=1) if i else xv

    return pl.pallas_call(
        body, out_shape=jax.ShapeDtypeStruct((heads, Q_BLOCK, KEY_WIN), F32),
        scratch_shapes=[pltpu.VMEM((16, KEY_WIN), F32)], name="bias_blocks")(rel16)


def bias_grad(dbias):
    heads = dbias.shape[0]

    def body(db_ref, o_ref, y_ref):
        y_ref[...] = jnp.zeros_like(y_ref)
        row = lax.broadcasted_iota(jnp.int32, (CHUNK, KEY_WIN), 0)
        for h in range(heads):
            fv = db_ref[h, 0:CHUNK, :]
            for i in range(1, Q_BLOCK // CHUNK):
                fv = fv + pltpu.roll(db_ref[h, CHUNK * i:CHUNK * (i + 1), :], KEY_WIN - CHUNK * i, axis=1)
            for b in range(6):
                fv = jnp.where(((row >> b) & 1) == 1, pltpu.roll(fv, KEY_WIN - (1 << b), axis=1), fv)
            y_ref[h:h + 1, :] = jnp.sum(fv, axis=0, keepdims=True)
        o_ref[...] = lax.dot_general(y_ref[...], _rel_onehot(), (((1,), (1,)), ((), ())),
                                     precision=HIGHEST, preferred_element_type=F32)

    return pl.pallas_call(
        body, out_shape=jax.ShapeDtypeStruct((16, N_REL), F32),
        scratch_shapes=[pltpu.VMEM((16, KEY_WIN), F32)], name="bias_grad")(dbias)


def _attn_windows(seq):
    out = []
    for j in range(seq // Q_BLOCK):
        r0 = j * Q_BLOCK
        k0 = max(0, r0 - 8 * CHUNK)
        width = r0 + Q_BLOCK - k0
        out.append((r0, k0, width, KEY_WIN - width))
    return out


def attn_fwd(z, gq2, gk2, bias, batch, seq):
    n = z.shape[0]
    pairs = TOK_WIDTH // LANES

    def body(q_ref, k_ref, v_ref, gq_ref, gk_ref, b_ref, o_ref, qs_s, kn_s):
        masks = _group_masks(LANES)
        qs_s[...] = (_head_norm(q_ref[...].astype(F32), gq_ref[...], masks)[0] * ATTN_SCALE).astype(BF16)
        kn_s[...] = _head_norm(k_ref[...].astype(F32), gk_ref[...], masks)[0].astype(BF16)
        for r0, k0, width, c0 in _attn_windows(seq):
            qb = qs_s[r0:r0 + Q_BLOCK, :]
            kw = kn_s[k0:k0 + width, :]
            vw = v_ref[k0:k0 + width, :]
            out = jnp.zeros((Q_BLOCK, LANES), F32)
            for h, msk in enumerate(masks):
                qh = jnp.where(msk, qb, jnp.zeros_like(qb))
                s = _dot(qh, kw, 1, 1) + b_ref[h, :, c0:KEY_WIN]
                p = _softmax_rows(s).astype(BF16)
                out = jnp.where(msk, _dot(p, vw, 1, 0), out)
            o_ref[r0:r0 + Q_BLOCK, :] = out.astype(o_ref.dtype)

    def col(off):
        return pl.BlockSpec((seq, LANES), lambda b, p: (b, off + p))

    vec = pl.BlockSpec((1, LANES), lambda b, p: (0, 0))
    return pl.pallas_call(
        body, out_shape=jax.ShapeDtypeStruct((n, D_MODEL), BF16), grid=(batch, pairs),
        in_specs=[col(0), col(pairs), col(2 * pairs), vec, vec,
                  pl.BlockSpec((2, Q_BLOCK, KEY_WIN), lambda b, p: (p, 0, 0))],
        out_specs=pl.BlockSpec((seq, LANES), lambda b, p: (b, p)),
        scratch_shapes=[pltpu.VMEM((seq, LANES), BF16), pltpu.VMEM((seq, LANES), BF16)],
        compiler_params=_params("parallel", "arbitrary"), name="attn_fwd")(z, z, z, gq2, gk2, bias)


def attn_bwd(z, dcat, gq2, gk2, bias, batch, seq):
    n = z.shape[0]
    pairs = TOK_WIDTH // LANES

    def body(q_ref, k_ref, v_ref, do_ref, gq_ref, gk_ref, b_ref,
             dz_ref, db_ref, dgq_ref, dgk_ref, qs_s, kn_s, dqn_s, dkn_s, dv_s, dk_o, dv_o):
        pi, bi, which = pl.program_id(0), pl.program_id(1), pl.program_id(2)

        @pl.when(which == 0)
        def _():
            masks = _group_masks(LANES)

            @pl.when(bi == 0)
            def _():
                db_ref[...] = jnp.zeros_like(db_ref)

            @pl.when((bi == 0) & (pi == 0))
            def _():
                dgq_ref[...] = jnp.zeros_like(dgq_ref)
                dgk_ref[...] = jnp.zeros_like(dgk_ref)

            qn, qhat, rq = _head_norm(q_ref[...].astype(F32), gq_ref[...], masks)
            kn, khat, rk = _head_norm(k_ref[...].astype(F32), gk_ref[...], masks)
            qs_s[...] = (qn * ATTN_SCALE).astype(BF16)
            kn_s[...] = kn.astype(BF16)
            dkn_s[...] = jnp.zeros_like(dkn_s)
            dv_s[...] = jnp.zeros_like(dv_s)
            for r0, k0, width, c0 in _attn_windows(seq):
                qb = qs_s[r0:r0 + Q_BLOCK, :]
                dob = do_ref[r0:r0 + Q_BLOCK, :]
                kw = kn_s[k0:k0 + width, :]
                vw = v_ref[k0:k0 + width, :]
                dq_acc = jnp.zeros((Q_BLOCK, LANES), F32)
                dk_acc = jnp.zeros((width, LANES), F32)
                dv_acc = jnp.zeros((width, LANES), F32)
                for h, msk in enumerate(masks):
                    qh = jnp.where(msk, qb, jnp.zeros_like(qb))
                    doh = jnp.where(msk, dob, jnp.zeros_like(dob))
                    p = _softmax_rows(_dot(qh, kw, 1, 1) + b_ref[h, :, c0:KEY_WIN])
                    dp = _dot(doh, vw, 1, 1)
                    ds = p * (dp - jnp.sum(p * dp, axis=-1, keepdims=True))
                    db_ref[h, :, c0:KEY_WIN] += ds
                    dsb = ds.astype(BF16)
                    dq_acc = jnp.where(msk, _dot(dsb, kw, 1, 0), dq_acc)
                    dk_acc = jnp.where(msk, _dot(dsb, qb, 0, 0), dk_acc)
                    dv_acc = jnp.where(msk, _dot(p.astype(BF16), dob, 0, 0), dv_acc)
                dqn_s[r0:r0 + Q_BLOCK, :] = dq_acc * ATTN_SCALE
                dkn_s[k0:k0 + width, :] += dk_acc
                dv_s[k0:k0 + width, :] += dv_acc
            dq, dgq = _head_norm_bwd(dqn_s[...], qhat, rq, gq_ref[...], masks)
            dk, dgk = _head_norm_bwd(dkn_s[...], khat, rk, gk_ref[...], masks)
            dz_ref[...] = dq.astype(dz_ref.dtype)
            dk_o[...] = dk.astype(dk_o.dtype)
            dv_o[...] = dv_s[...].astype(dv_o.dtype)
            dgq_ref[...] += dgq
            dgk_ref[...] += dgk

        @pl.when(which == 1)
        def _():
            dz_ref[...] = dk_o[...]

        @pl.when(which == 2)
        def _():
            dz_ref[...] = dv_o[...]

    def ahead(p, b, t):
        nb = b + jnp.where(t > 0, 1, 0)
        wrap = jnp.where(nb >= batch, 1, 0)
        return jnp.minimum(p + wrap, pairs - 1), nb - wrap * batch

    def col(off):
        def index(p, b, t):
            np_, nb = ahead(p, b, t)
            return nb, off + np_
        return pl.BlockSpec((seq, LANES), index)

    vec = pl.BlockSpec((1, LANES), lambda p, b, t: (0, 0))
    blk = pl.BlockSpec((2, Q_BLOCK, KEY_WIN), lambda p, b, t: (p, 0, 0))
    blk_in = pl.BlockSpec((2, Q_BLOCK, KEY_WIN), lambda p, b, t: (ahead(p, b, t)[0], 0, 0))
    v_shape = jax.ShapeDtypeStruct((1, LANES), F32)
    return pl.pallas_call(
        body,
        out_shape=(jax.ShapeDtypeStruct(z.shape, BF16), jax.ShapeDtypeStruct(bias.shape, F32), v_shape, v_shape),
        grid=(pairs, batch, 3),
        in_specs=[col(0), col(pairs), col(2 * pairs), col(0), vec, vec, blk_in],
        out_specs=(pl.BlockSpec((seq, LANES), lambda p, b, t: (b, t * pairs + p)), blk, vec, vec),
        scratch_shapes=[pltpu.VMEM((seq, LANES), BF16), pltpu.VMEM((seq, LANES), BF16),
                        pltpu.VMEM((seq, LANES), F32), pltpu.VMEM((seq, LANES), F32), pltpu.VMEM((seq, LANES), F32),
                        pltpu.VMEM((seq, LANES), BF16), pltpu.VMEM((seq, LANES), BF16)],
        compiler_params=_params("arbitrary", "arbitrary", "arbitrary"), name="attn_bwd")(
            z, z, z, dcat, gq2, gk2, bias)


MEM_ROWS = 512


def memattn_fwd(z, kv, gq4, gk4, cat, batch, seq, qcol, name):
    mtok = kv.shape[0] // batch
    rows = min(MEM_ROWS, seq)

    def body(q_ref, kv_ref, gq_ref, gk_ref, cat_ref, o_ref):
        del cat_ref
        masks = _group_masks(MEM_WIDTH)
        kn = _head_norm(kv_ref[:, 0:MEM_WIDTH], gk_ref[...], masks)[0].astype(BF16)
        vm = kv_ref[:, MEM_WIDTH:2 * MEM_WIDTH].astype(BF16)
        for t in range(seq // rows):
            sl = slice(t * rows, (t + 1) * rows)
            qs = (_head_norm(q_ref[sl, :].astype(F32), gq_ref[...], masks)[0] * ATTN_SCALE).astype(BF16)
            out = jnp.zeros((rows, MEM_WIDTH), F32)
            for msk in masks:
                qh = jnp.where(msk, qs, jnp.zeros_like(qs))
                p = _softmax_rows(_dot(qh, kn, 1, 1)).astype(BF16)
                out = jnp.where(msk, _dot(p, vm, 1, 0), out)
            o_ref[sl, :] = out.astype(o_ref.dtype)

    vec = pl.BlockSpec((1, MEM_WIDTH), lambda b: (0, 0))
    return pl.pallas_call(
        body, out_shape=jax.ShapeDtypeStruct(cat.shape, cat.dtype), grid=(batch,),
        in_specs=[pl.BlockSpec((seq, MEM_WIDTH), lambda b: (b, qcol)),
                  pl.BlockSpec((mtok, 2 * MEM_WIDTH), lambda b: (b, 0)), vec, vec, ANY],
        out_specs=pl.BlockSpec((seq, MEM_WIDTH), lambda b: (b, TOK_WIDTH // MEM_WIDTH)),
        input_output_aliases={4: 0},
        compiler_params=_params("parallel"), name=name)(z, kv, gq4, gk4, cat)


def memattn_bwd(z, kv, dcat, gq4, gk4, dz, batch, seq, qcol, name):
    mtok = kv.shape[0] // batch
    rows = min(MEM_ROWS, seq)

    def body(q_ref, kv_ref, do_ref, gq_ref, gk_ref, dz_in_ref, dq_ref, dkv_ref, dgq_ref, dgk_ref):
        del dz_in_ref
        @pl.when(pl.program_id(0) == 0)
        def _():
            dgq_ref[...] = jnp.zeros_like(dgq_ref)
            dgk_ref[...] = jnp.zeros_like(dgk_ref)

        masks = _group_masks(MEM_WIDTH)
        kn_f, khat, rk = _head_norm(kv_ref[:, 0:MEM_WIDTH], gk_ref[...], masks)
        kn = kn_f.astype(BF16)
        vm = kv_ref[:, MEM_WIDTH:2 * MEM_WIDTH].astype(BF16)
        dkn = jnp.zeros((mtok, MEM_WIDTH), F32)
        dvm = jnp.zeros((mtok, MEM_WIDTH), F32)
        dgq = jnp.zeros((1, MEM_WIDTH), F32)
        for t in range(seq // rows):
            sl = slice(t * rows, (t + 1) * rows)
            qn_f, qhat, rq = _head_norm(q_ref[sl, :].astype(F32), gq_ref[...], masks)
            qs = (qn_f * ATTN_SCALE).astype(BF16)
            dob = do_ref[sl, :]
            dqn = jnp.zeros((rows, MEM_WIDTH), F32)
            for msk in masks:
                qh = jnp.where(msk, qs, jnp.zeros_like(qs))
                doh = jnp.where(msk, dob, jnp.zeros_like(dob))
                p = _softmax_rows(_dot(qh, kn, 1, 1))
                dp = _dot(doh, vm, 1, 1)
                ds = p * (dp - jnp.sum(p * dp, axis=-1, keepdims=True))
                dsb = ds.astype(BF16)
                dqn = jnp.where(msk, _dot(dsb, kn, 1, 0), dqn)
                dkn = dkn + jnp.where(msk, _dot(dsb, qs, 0, 0), 0.0)
                dvm = dvm + jnp.where(msk, _dot(p.astype(BF16), dob, 0, 0), 0.0)
            dq, dg = _head_norm_bwd(dqn * ATTN_SCALE, qhat, rq, gq_ref[...], masks)
            dq_ref[sl, :] = dq.astype(dq_ref.dtype)
            dgq = dgq + dg
        dk, dgk = _head_norm_bwd(dkn, khat, rk, gk_ref[...], masks)
        dkv_ref[:, 0:MEM_WIDTH] = dk
        dkv_ref[:, MEM_WIDTH:2 * MEM_WIDTH] = dvm
        dgq_ref[...] += dgq
        dgk_ref[...] += dgk

    vec = pl.BlockSpec((1, MEM_WIDTH), lambda b: (0, 0))
    kv_spec = pl.BlockSpec((mtok, 2 * MEM_WIDTH), lambda b: (b, 0))
    v_shape = jax.ShapeDtypeStruct((1, MEM_WIDTH), F32)
    q_spec = pl.BlockSpec((seq, MEM_WIDTH), lambda b: (b, qcol))
    return pl.pallas_call(
        body,
        out_shape=(jax.ShapeDtypeStruct(dz.shape, dz.dtype), jax.ShapeDtypeStruct(kv.shape, F32), v_shape, v_shape),
        grid=(batch,),
        in_specs=[q_spec, kv_spec, pl.BlockSpec((seq, MEM_WIDTH), lambda b: (b, TOK_WIDTH // MEM_WIDTH)), vec, vec, ANY],
        out_specs=(q_spec, kv_spec, vec, vec),
        input_output_aliases={5: 0},
        compiler_params=_params("arbitrary"), name=name)(z, kv, dcat, gq4, gk4, dz)


def mem_prep(mem, gain, wkv, name):
    t, d = mem.shape

    def body(m_ref, g_ref, w_ref, n_ref, kv_ref):
        mv = m_ref[...]
        r = lax.rsqrt(jnp.mean(mv * mv, axis=-1, keepdims=True) + NORM_EPS)
        nv = (mv * r * g_ref[...]).astype(BF16)
        n_ref[...] = nv
        kv_ref[...] = _dot(nv, w_ref[...], 1, 0)

    vmem = pl.BlockSpec(memory_space=pltpu.VMEM)
    return pl.pallas_call(
        body, out_shape=(jax.ShapeDtypeStruct((t, d), BF16), jax.ShapeDtypeStruct((t, wkv.shape[1]), F32)),
        in_specs=[vmem, vmem, vmem], out_specs=(vmem, vmem),
        compiler_params=pltpu.CompilerParams(vmem_limit_bytes=VMEM_LIMIT), name=name)(mem, gain, wkv)


def mem_norm_grad(dkv, wkv, mem, name):
    t, d = mem.shape

    def body(dkv_ref, w_ref, m_ref, dg_ref):
        dn = _dot(dkv_ref[...].astype(BF16), w_ref[...], 1, 1)
        mv = m_ref[...]
        r = lax.rsqrt(jnp.mean(mv * mv, axis=-1, keepdims=True) + NORM_EPS)
        dg_ref[...] = jnp.sum(dn * (mv * r), axis=0, keepdims=True)

    vmem = pl.BlockSpec(memory_space=pltpu.VMEM)
    return pl.pallas_call(
        body, out_shape=jax.ShapeDtypeStruct((1, d), F32), in_specs=[vmem, vmem, vmem], out_specs=vmem,
        compiler_params=pltpu.CompilerParams(vmem_limit_bytes=VMEM_LIMIT), name=name)(dkv, wkv, mem)


CONV_ROWS = 256


def _glu(a_ref, g_ref):
    return a_ref[...].astype(F32) * _sigmoid(g_ref[...].astype(F32))


def _layer_norm_stats(y):
    mu = jnp.mean(y, axis=-1, keepdims=True)
    yc = y - mu
    rstd = lax.rsqrt(jnp.mean(yc * yc, axis=-1, keepdims=True) + NORM_EPS)
    return yc * rstd, rstd


CONV_WIN = CONV_HALO + CONV_ROWS
SUBLANES = 8
SHIFT_ROWS = CONV_WIN - SUBLANES


def _preshift(win, shifted):
    for s in range(1, SUBLANES):
        shifted[s - 1, :, :] = win[s:s + SHIFT_ROWS, :]


TAP_ROWS = 64
TAP_TILES = [(r0, slice(c0, c0 + LANES)) for c0 in range(0, TOK_WIDTH, LANES) for r0 in range(0, CONV_ROWS, TAP_ROWS)]


def _tap(win, shifted, off, r0, lanes):
    s = off % SUBLANES
    base = off - s + r0
    if s == 0:
        return win[base:base + TAP_ROWS, lanes]
    return shifted[s - 1, base:base + TAP_ROWS, lanes]


def _fold_rows(x):
    return jnp.sum(x.reshape(TAP_ROWS // SUBLANES, SUBLANES, LANES), axis=0)


def conv_fwd(z, cw, cb, lg, lb, batch, seq):
    n = z.shape[0]
    nt = seq // CONV_ROWS
    sub = CONV_ROWS // CONV_HALO
    lead = CONV_HALO - (CONV_W - 1)

    def body(a_ref, g_ref, ap_ref, gp_ref, cw_ref, cb_ref, lg_ref, lb_ref, o_ref, y_ref, win, shifted):
        first = pl.program_id(1) == 0
        win[0:CONV_HALO, :] = jnp.where(first, 0.0, _glu(ap_ref, gp_ref))
        win[CONV_HALO:CONV_WIN, :] = _glu(a_ref, g_ref)
        _preshift(win, shifted)
        for r0, lanes in TAP_TILES:
            acc = jnp.zeros((TAP_ROWS, LANES), F32) + cb_ref[:, lanes]
            for w in range(CONV_W):
                acc = acc + _tap(win, shifted, lead + w, r0, lanes) * cw_ref[w:w + 1, lanes]
            y_ref[r0:r0 + TAP_ROWS, lanes] = acc
        yh, _ = _layer_norm_stats(y_ref[...])
        t = yh * lg_ref[...] + lb_ref[...]
        o_ref[...] = (t * _sigmoid(t)).astype(o_ref.dtype)

    def cur(c):
        return pl.BlockSpec((CONV_ROWS, TOK_WIDTH), lambda b, i: (b * nt + i, c))

    def prev(c):
        return pl.BlockSpec((CONV_HALO, TOK_WIDTH), lambda b, i: (jnp.maximum((b * nt + i) * sub - 1, 0), c))

    vec = pl.BlockSpec((1, TOK_WIDTH), lambda b, i: (0, 0))
    return pl.pallas_call(
        body, out_shape=(jax.ShapeDtypeStruct((n, D_MODEL), BF16), jax.ShapeDtypeStruct((n, TOK_WIDTH), F32)),
        grid=(batch, nt),
        in_specs=[cur(0), cur(1), prev(0), prev(1), pl.BlockSpec((32, TOK_WIDTH), lambda b, i: (0, 0)), vec, vec, vec],
        out_specs=(cur(0), cur(0)),
        scratch_shapes=[pltpu.VMEM((CONV_WIN, TOK_WIDTH), F32), pltpu.VMEM((SUBLANES - 1, SHIFT_ROWS, TOK_WIDTH), F32)],
        compiler_params=_params("parallel", "arbitrary"), name="conv_fwd")(z, z, z, z, cw, cb, lg, lb)


def conv_bwd(z, y, dcat, cw, lg, lb, batch, seq):
    n = z.shape[0]
    nt = seq // CONV_ROWS
    sub = CONV_ROWS // CONV_HALO
    lead = CONV_HALO - (CONV_W - 1)
    last_blk = n // CONV_HALO - 1

    def body(a_ref, g_ref, ap_ref, gp_ref, y_ref, yn_ref, do_ref, don_ref, cw_ref, lg_ref, lb_ref,
             dz_ref, dcw_ref, dsm_ref, win, shifted, dyw, dshifted, dg_o):
        b, i, which = pl.program_id(0), pl.program_id(1), pl.program_id(2)

        @pl.when(which == 0)
        def _():
            first, last = i == 0, i == nt - 1

            @pl.when((b == 0) & (i == 0))
            def _():
                dcw_ref[...] = jnp.zeros_like(dcw_ref)
                dsm_ref[...] = jnp.zeros_like(dsm_ref)

            win[0:CONV_HALO, :] = jnp.where(first, 0.0, _glu(ap_ref, gp_ref))
            win[CONV_HALO:CONV_WIN, :] = _glu(a_ref, g_ref)
            _preshift(win, shifted)
            yv = jnp.concatenate([y_ref[...], yn_ref[...]], axis=0)
            yh, rstd = _layer_norm_stats(yv)
            t = yh * lg_ref[...] + lb_ref[...]
            st = _sigmoid(t)
            dout = jnp.concatenate(
                [do_ref[...].astype(F32), jnp.where(last, 0.0, don_ref[...].astype(F32))], axis=0)
            dt = dout * st * (1.0 + t * (1.0 - st))
            dyh = dt * lg_ref[...]
            dy = rstd * (dyh - jnp.mean(dyh, axis=-1, keepdims=True)
                         - yh * jnp.mean(dyh * yh, axis=-1, keepdims=True))
            dyw[...] = dy
            _preshift(dyw, dshifted)
            dsm_ref[0:1, :] += jnp.sum(dy[0:CONV_ROWS], axis=0, keepdims=True)
            dsm_ref[1:2, :] += jnp.sum((dt * yh)[0:CONV_ROWS], axis=0, keepdims=True)
            dsm_ref[2:3, :] += jnp.sum(dt[0:CONV_ROWS], axis=0, keepdims=True)
            for c0 in range(0, TOK_WIDTH, LANES):
                lanes = slice(c0, c0 + LANES)
                dcw_acc = [jnp.zeros((SUBLANES, LANES), F32) for _ in range(CONV_W)]
                for r0 in range(0, CONV_ROWS, TAP_ROWS):
                    dyt = dyw[r0:r0 + TAP_ROWS, lanes]
                    dglu = jnp.zeros((TAP_ROWS, LANES), F32)
                    for w in range(CONV_W):
                        dcw_acc[w] = dcw_acc[w] + _fold_rows(dyt * _tap(win, shifted, lead + w, r0, lanes))
                        dglu = dglu + _tap(dyw, dshifted, CONV_W - 1 - w, r0, lanes) * cw_ref[w:w + 1, lanes]
                    avt = a_ref[r0:r0 + TAP_ROWS, lanes].astype(F32)
                    sgt = _sigmoid(g_ref[r0:r0 + TAP_ROWS, lanes].astype(F32))
                    dz_ref[r0:r0 + TAP_ROWS, lanes] = (dglu * sgt).astype(dz_ref.dtype)
                    dg_o[r0:r0 + TAP_ROWS, lanes] = (dglu * avt * sgt * (1.0 - sgt)).astype(dg_o.dtype)
                for w in range(CONV_W):
                    dcw_ref[w:w + 1, lanes] += jnp.sum(dcw_acc[w], axis=0, keepdims=True)

        @pl.when(which == 1)
        def _():
            dz_ref[...] = dg_o[...]

    def ahead(b, i, t):
        return jnp.minimum(b * nt + i + t, batch * nt - 1)

    def cur(c):
        return pl.BlockSpec((CONV_ROWS, TOK_WIDTH), lambda b, i, t: (ahead(b, i, t), c))

    def prev(c):
        return pl.BlockSpec((CONV_HALO, TOK_WIDTH), lambda b, i, t: (jnp.maximum(ahead(b, i, t) * sub - 1, 0), c))

    nxt = pl.BlockSpec((CONV_HALO, TOK_WIDTH),
                       lambda b, i, t: (jnp.minimum((ahead(b, i, t) + 1) * sub, last_blk), 0))
    vec = pl.BlockSpec((1, TOK_WIDTH), lambda b, i, t: (0, 0))
    full32 = pl.BlockSpec((32, TOK_WIDTH), lambda b, i, t: (0, 0))
    return pl.pallas_call(
        body,
        out_shape=(jax.ShapeDtypeStruct(z.shape, BF16), jax.ShapeDtypeStruct((32, TOK_WIDTH), F32),
                   jax.ShapeDtypeStruct((8, TOK_WIDTH), F32)),
        grid=(batch, nt, 2),
        in_specs=[cur(0), cur(1), prev(0), prev(1), cur(0), nxt, cur(0), nxt, full32, vec, vec],
        out_specs=(pl.BlockSpec((CONV_ROWS, TOK_WIDTH), lambda b, i, t: (b * nt + i, t)), full32,
                   pl.BlockSpec((8, TOK_WIDTH), lambda b, i, t: (0, 0))),
        scratch_shapes=[pltpu.VMEM((CONV_WIN, TOK_WIDTH), F32), pltpu.VMEM((SUBLANES - 1, SHIFT_ROWS, TOK_WIDTH), F32),
                        pltpu.VMEM((CONV_WIN, TOK_WIDTH), F32), pltpu.VMEM((SUBLANES - 1, SHIFT_ROWS, TOK_WIDTH), F32),
                        pltpu.VMEM((CONV_ROWS, TOK_WIDTH), BF16)],
        compiler_params=_params("arbitrary", "arbitrary", "arbitrary"), name="conv_bwd")(
            z, z, z, z, y, y, dcat, dcat, cw, lg, lb)


def _place():
    return lax.axis_index("x"), lax.axis_index("y"), lax.axis_index("c")


def _other_chips(x, y):
    return [(1 - x, y), (x, 1 - y), (1 - x, 1 - y)]


def reduce_small(arrays):
    na = len(arrays)

    def body(*refs):
        ins, outs, bufs = refs[:na], refs[na:2 * na], refs[2 * na:3 * na]
        send_sems, recv_sems = refs[3 * na:]
        x, y, c = _place()
        me = 4 * x + 2 * y + c
        copies = []
        for a in range(na):
            bufs[a][me] = ins[a][...]
            for k in range(1, N_DEV):
                cp = pltpu.make_async_remote_copy(
                    src_ref=ins[a], dst_ref=bufs[a].at[me], send_sem=send_sems.at[a, k - 1],
                    recv_sem=recv_sems.at[a, k - 1],
                    device_id=(x ^ (k >> 2), y ^ ((k >> 1) & 1), c ^ (k & 1)), device_id_type=MESH)
                cp.start()
                copies.append(cp)
        for a in range(na):
            for k in range(1, N_DEV):
                src = 4 * (x ^ (k >> 2)) + 2 * (y ^ ((k >> 1) & 1)) + (c ^ (k & 1))
                pltpu.make_async_remote_copy(
                    src_ref=ins[a], dst_ref=bufs[a].at[src], send_sem=send_sems.at[a, k - 1],
                    recv_sem=recv_sems.at[a, k - 1], device_id=(x, y, c), device_id_type=MESH).wait_recv()
        for cp in copies:
            cp.wait_send()
        for a in range(na):
            total = bufs[a][0]
            for dev in range(1, N_DEV):
                total = total + bufs[a][dev]
            outs[a][...] = total

    vmem = pl.BlockSpec(memory_space=pltpu.VMEM)
    return pl.pallas_call(
        body, out_shape=tuple(jax.ShapeDtypeStruct(a.shape, F32) for a in arrays),
        in_specs=[vmem] * na, out_specs=tuple([vmem] * na),
        scratch_shapes=[pltpu.VMEM((N_DEV,) + a.shape, F32) for a in arrays]
        + [pltpu.SemaphoreType.DMA((na, N_DEV - 1)), pltpu.SemaphoreType.DMA((na, N_DEV - 1))],
        compiler_params=pltpu.CompilerParams(vmem_limit_bytes=VMEM_LIMIT), name="small_reduce")(*arrays)


def adamw_small(ws, gs, ms, vs):
    na = len(ws)
    c1 = 1.0 / (1.0 - ADAM_B1 ** ADAM_STEP)
    c2 = 1.0 / (1.0 - ADAM_B2 ** ADAM_STEP)

    def body(*refs):
        w_refs, g_refs, m_refs, v_refs = (refs[i * na:(i + 1) * na] for i in range(4))
        d_refs, nm_refs, nv_refs = (refs[(4 + i) * na:(5 + i) * na] for i in range(3))
        for a in range(na):
            gv = g_refs[a][...]
            nm = ADAM_B1 * m_refs[a][...] + (1.0 - ADAM_B1) * gv
            nv = ADAM_B2 * v_refs[a][...] + (1.0 - ADAM_B2) * (gv * gv)
            nm_refs[a][...] = nm
            nv_refs[a][...] = nv
            d_refs[a][...] = -ADAM_LR * ((nm * c1) / (jnp.sqrt(nv * c2) + ADAM_EPS) + ADAM_WD * w_refs[a][...])

    vmem = pl.BlockSpec(memory_space=pltpu.VMEM)
    shapes = tuple(jax.ShapeDtypeStruct(w.shape, F32) for w in ws)
    outs = pl.pallas_call(
        body, out_shape=shapes * 3, in_specs=[vmem] * (4 * na), out_specs=tuple([vmem] * (3 * na)),
        compiler_params=pltpu.CompilerParams(vmem_limit_bytes=VMEM_LIMIT), name="adamw_small")(*ws, *gs, *ms, *vs)
    return outs[:na], outs[na:2 * na], outs[2 * na:]


def gather_weights(shards, name, collective_id):
    nw = len(shards)
    ns = [s.shape[0] for s in shards]
    in_refs = [jax.new_ref(s, memory_space=pltpu.MemorySpace.HBM) for s in shards]
    out_refs = [jax.empty_ref(jax.ShapeDtypeStruct((N_DEV * s.shape[0], s.shape[1]), s.dtype),
                              memory_space=pltpu.MemorySpace.HBM) for s in shards]

    @pl.kernel(mesh=plsc.ScalarSubcoreMesh(axis_name="seq", num_cores=1), name=name,
               scratch_types=(pltpu.SemaphoreType.DMA((nw, 7)), pltpu.SemaphoreType.DMA((nw, 7)),
                              pltpu.SemaphoreType.DMA((nw,))),
               compiler_params=pltpu.CompilerParams(collective_id=collective_id))
    def launch(send_sems, recv_sems, local_sems):
        x, y, c = _place()
        me, sib = (x, y, c), (x, y, 1 - c)
        chips = _other_chips(x, y)
        barrier = pltpu.get_barrier_semaphore()
        for peer in [sib] + [(*chip, c) for chip in chips]:
            pl.semaphore_signal(barrier, inc=1, device_id=peer, device_id_type=MESH)
        pl.semaphore_wait(barrier, 4)

        def rows(w, dev):
            return out_refs[w].at[pl.ds((4 * dev[0] + 2 * dev[1] + dev[2]) * ns[w], ns[w]), :]

        def copy(w, k, block, to, src=None):
            return pltpu.make_async_remote_copy(
                src_ref=rows(w, block) if src is None else src, dst_ref=rows(w, block),
                send_sem=send_sems.at[w, k], recv_sem=recv_sems.at[w, k], device_id=to, device_id_type=MESH)

        started, sends = [], []
        for w in range(nw):
            mine = pltpu.make_async_copy(in_refs[w], rows(w, me), local_sems.at[w])
            mine.start()
            started.append(mine)
            first = [copy(w, 0, me, sib, src=in_refs[w])]
            first += [copy(w, 1 + j, me, (*chip, c), src=in_refs[w]) for j, chip in enumerate(chips)]
            for cp in first:
                cp.start()
            sends += first
        for w in range(nw):
            for j, chip in enumerate(chips):
                copy(w, 1 + j, (*chip, c), me).wait_recv()
                fwd = copy(w, 4 + j, (*chip, c), sib)
                fwd.start()
                sends.append(fwd)
        for w in range(nw):
            copy(w, 0, sib, me).wait_recv()
            for j, chip in enumerate(chips):
                copy(w, 4 + j, (*chip, 1 - c), me).wait_recv()
        for cp in sends:
            cp.wait_send()
        for mine in started:
            mine.wait()

    launch()
    return [r[...] for r in out_refs]


def _sequencer_exchange(sources, out_rows, peers_of, copies_of, name, collective_id):
    nw = len(sources)
    in_refs = [jax.new_ref(s, memory_space=pltpu.MemorySpace.HBM) for s in sources]
    out_refs = [jax.empty_ref(jax.ShapeDtypeStruct((rows, s.shape[1]), s.dtype), memory_space=pltpu.MemorySpace.HBM)
                for rows, s in zip(out_rows, sources)]
    per = len(copies_of(0, 0, 0, 0))

    @pl.kernel(mesh=plsc.ScalarSubcoreMesh(axis_name="seq", num_cores=1), name=name,
               scratch_types=(pltpu.SemaphoreType.DMA((nw, per)), pltpu.SemaphoreType.DMA((nw, per))),
               compiler_params=pltpu.CompilerParams(collective_id=collective_id))
    def launch(send_sems, recv_sems):
        x, y, c = _place()
        peers = peers_of(x, y, c)
        barrier = pltpu.get_barrier_semaphore()
        for peer in peers:
            pl.semaphore_signal(barrier, inc=1, device_id=peer, device_id_type=MESH)
        pl.semaphore_wait(barrier, len(peers))
        copies = []
        for w in range(nw):
            for k, (src_blk, dst_blk, rows, peer) in enumerate(copies_of(x, y, c, w)):
                cp = pltpu.make_async_remote_copy(
                    src_ref=in_refs[w].at[pl.ds(src_blk * rows, rows), :],
                    dst_ref=out_refs[w].at[pl.ds(dst_blk * rows, rows), :],
                    send_sem=send_sems.at[w, k], recv_sem=recv_sems.at[w, k], device_id=peer, device_id_type=MESH)
                cp.start()
                copies.append(cp)
        for cp in copies:
            cp.wait_recv()
        for cp in copies:
            cp.wait_send()

    launch()
    return [r[...] for r in out_refs]


def scatter_to_sibling(grads, name, collective_id):
    ns = [g.shape[0] // N_DEV for g in grads]
    return _sequencer_exchange(
        grads, [4 * n for n in ns],
        lambda x, y, c: [(x, y, 1 - c)],
        lambda x, y, c, w: [(2 * q + 1 - c, q, ns[w], (x, y, 1 - c)) for q in range(4)],
        name, collective_id)


def scatter_to_chips(parts, name, collective_id):
    ns = [p.shape[0] // 4 for p in parts]
    return _sequencer_exchange(
        parts, [3 * n for n in ns],
        lambda x, y, c: [(*chip, c) for chip in _other_chips(x, y)],
        lambda x, y, c, w: [(2 * chip[0] + chip[1], j, ns[w], (*chip, c)) for j, chip in enumerate(_other_chips(x, y))],
        name, collective_id)


def add_sibling(grads, landeds, core, name):
    nw = len(grads)

    def body(c_ref, *refs):
        for w in range(nw):
            g_ref, l_ref, o_ref = refs[2 * w], refs[2 * w + 1], refs[2 * nw + w]
            o_ref[...] = (g_ref[...].astype(F32) + l_ref[...].astype(F32)).astype(o_ref.dtype)

    in_specs, out_specs, args = [], [], []
    for g, ld in zip(grads, landeds):
        n, cols = ld.shape[0] // 4, g.shape[1]
        in_specs += [pl.BlockSpec((n, cols), lambda q, c_ref: (2 * q + c_ref[0], 0)),
                     pl.BlockSpec((n, cols), lambda q, c_ref: (q, 0))]
        out_specs.append(pl.BlockSpec((n, cols), lambda q, c_ref: (q, 0)))
        args += [g, ld]
    grid_spec = pltpu.PrefetchScalarGridSpec(
        num_scalar_prefetch=1, grid=(4,), in_specs=in_specs, out_specs=tuple(out_specs))
    return pl.pallas_call(
        body, out_shape=tuple(jax.ShapeDtypeStruct(ld.shape, ld.dtype) for ld in landeds), grid_spec=grid_spec,
        compiler_params=_params("arbitrary"), name=name)(core, *args)


def adamw_shard(layer, w, m, v, part, landed, chip, earlier, name):
    n = landed.shape[0] // 3
    cols = w.shape[1]
    c1 = 1.0 / (1.0 - ADAM_B1 ** ADAM_STEP)
    c2 = 1.0 / (1.0 - ADAM_B2 ** ADAM_STEP)

    def body(q_ref, w_ref, m_ref, v_ref, p_ref, l0_ref, l1_ref, l2_ref, *rest):
        g_ref, d_ref, nm_ref, nv_ref = rest[-4:]
        gv = ((p_ref[...].astype(F32) + l0_ref[...].astype(F32)) + l1_ref[...].astype(F32)) + l2_ref[...].astype(F32)
        nm = ADAM_B1 * m_ref[...] + (1.0 - ADAM_B1) * gv
        nv = ADAM_B2 * v_ref[...] + (1.0 - ADAM_B2) * (gv * gv)
        g_ref[...] = gv
        nm_ref[...] = nm
        nv_ref[...] = nv
        d_ref[...] = -ADAM_LR * ((nm * c1) / (jnp.sqrt(nv * c2) + ADAM_EPS) + ADAM_WD * w_ref[...])

    sub = 2 if n % (2 * 16) == 0 else 1
    rows = n // sub
    own = pl.BlockSpec((rows, cols), lambda i, q_ref: (layer * sub + i, 0))

    def landed_spec(j):
        return pl.BlockSpec((rows, cols), lambda i, q_ref: (j * sub + i, 0))

    in_specs = [own, own, own, pl.BlockSpec((rows, cols), lambda i, q_ref: (q_ref[0] * sub + i, 0)),
                landed_spec(0), landed_spec(1), landed_spec(2)]
    args = [chip, w, m, v, part, landed, landed, landed]
    aliases = {}
    if earlier is not None:
        in_specs += [ANY] * 4
        args += list(earlier)
        aliases = {8 + k: k for k in range(4)}
    grid_spec = pltpu.PrefetchScalarGridSpec(
        num_scalar_prefetch=1, grid=(sub,), in_specs=in_specs, out_specs=(own, own, own, own))
    shape = jax.ShapeDtypeStruct(w.shape, F32)
    return pl.pallas_call(
        body, out_shape=(shape, shape, shape, shape), grid_spec=grid_spec, input_output_aliases=aliases,
        compiler_params=_params("arbitrary"), name=name)(*args)


def _pack(arrays):
    flat = jnp.concatenate([a.reshape(-1).astype(F32) for a in arrays])
    pad = (-flat.shape[0]) % (8 * LANES)
    return jnp.pad(flat, (0, pad)).reshape(-1, LANES)


def _unpack(slab, shapes):
    flat = slab.reshape(slab.shape[:-2] + (-1,))
    out, off = [], 0
    for shp in shapes:
        size = 1
        for s in shp:
            size *= s
        out.append(flat[..., off:off + size].reshape(flat.shape[:-1] + tuple(shp)))
        off += size
    return out


def kernel(x, mem, norm1_g, mem_norm_g, a_w_in, a_q_g, a_k_g, a_rel_bias, b_w_in, b_b_in, b_conv_w, b_conv_b, b_ln_g, b_ln_b, mq_g, mk_g, w_mem_kv, w_out, norm2_g, w_gate, w_up, w_down, loss_target, m_norm1_g, m_mem_norm_g, m_a_w_in, m_a_q_g, m_a_k_g, m_a_rel_bias, m_b_w_in, m_b_b_in, m_b_conv_w, m_b_conv_b, m_b_ln_g, m_b_ln_b, m_mq_g, m_mk_g, m_w_mem_kv, m_w_out, m_norm2_g, m_w_gate, m_w_up, m_w_down, v_norm1_g, v_mem_norm_g, v_a_w_in, v_a_q_g, v_a_k_g, v_a_rel_bias, v_b_w_in, v_b_b_in, v_b_conv_w, v_b_conv_b, v_b_ln_g, v_b_ln_b, v_mq_g, v_mk_g, v_w_mem_kv, v_w_out, v_norm2_g, v_w_gate, v_w_up, v_w_down):
    batch, seq, d = x.shape
    mtok = mem.shape[1]
    n = batch * seq
    ax, ay, ac = _place()
    me = 4 * ax + 2 * ay + ac
    core_arr = jnp.reshape(ac, (1,)).astype(jnp.int32)
    chip_arr = jnp.reshape(2 * ax + ay, (1,)).astype(jnp.int32)

    def t_bf16(w):
        return jnp.transpose(w).astype(BF16)

    def after(value, *earlier):
        return lax.optimization_barrier((value, *earlier))[0]

    def gather_mix(l, when, name, collective_id):
        srcs = [w_mem_kv[l].astype(BF16), w_out[l].astype(BF16)]
        if l == 1:
            srcs += [t_bf16(b_w_in[0]), _pack([b_b_in, b_conv_w, b_conv_b, b_ln_g, b_ln_b])]
        return gather_weights([after(srcs[0], *when)] + srcs[1:], name, collective_id)

    def gather_ffn(l, when, name, collective_id):
        return gather_weights(
            [after(t_bf16(w_gate[l]), *when), t_bf16(w_up[l]), w_down[l].astype(BF16)], name, collective_id)

    f_loc = b_b_in.shape[1]
    c_loc = b_conv_b.shape[1]

    def two(g):
        return jnp.concatenate([g, g], axis=-1)

    gq2, gk2 = two(a_q_g), two(a_k_g)
    rel16 = jnp.pad(a_rel_bias[0], ((0, 16 - a_rel_bias.shape[1]), (0, 0)))
    bias = bias_blocks(rel16)

    x0 = x.reshape(n, d)
    mem2 = mem.reshape(batch * mtok, d)

    saved = []
    xin = x0
    a_win_t, = gather_weights([t_bf16(a_w_in[0])], "gather_in_a", 1)
    wg_t, wu_t, wd, wo, wkv = [None] * 2, [None] * 2, [None] * 2, [None] * 2, [None] * 2
    h = after(rms_fwd(xin, norm1_g[0:1], name="rms1_fwd_0"), bias)
    target = loss_target.reshape(n, d)
    for l in range(2):
        gq4 = jnp.tile(mq_g[l:l + 1], (1, 4))
        gk4 = jnp.tile(mk_g[l:l + 1], (1, 4))
        y_conv = None
        if l == 0:
            wkv[0], wo[0] = gather_mix(0, (h, a_win_t), "gather_mix_a", 2)
            z = mm_nt(h, a_win_t, name="in_proj_a")
            wg_t[0], wu_t[0], wd[0] = gather_ffn(0, (z, wkv[0]), "gather_ffn_a", 3)
            cat = attn_fwd(z, gq2, gk2, bias, batch, seq)
            wkv[1], wo[1], b_win_t, conv_slabs = gather_mix(1, (cat, wg_t[0]), "gather_mix_b", 4)
            qcol = 3 * TOK_WIDTH // MEM_WIDTH
        else:
            small_shapes = [(f_loc,), (CONV_W, c_loc), (c_loc,), (c_loc,), (c_loc,)]
            bb_g, cw_g, cb_g, lg_g, lb_g = _unpack(conv_slabs.reshape(N_DEV, -1, LANES), small_shapes)
            bb_full = bb_g.reshape(1, -1)
            cw_full = jnp.pad(jnp.transpose(cw_g, (1, 0, 2)).reshape(CONV_W, -1), ((0, 32 - CONV_W), (0, 0)))
            cb_full, lg_full, lb_full = cb_g.reshape(1, -1), lg_g.reshape(1, -1), lb_g.reshape(1, -1)
            z = mm_nt(h, b_win_t, bias=bb_full, name="in_proj_b")
            cat, y_conv = conv_fwd(z, cw_full, cb_full, lg_full, lb_full, batch, seq)
            qcol = 2 * TOK_WIDTH // MEM_WIDTH
        mem_n, kv = mem_prep(mem2, mem_norm_g[l:l + 1], wkv[l], name=f"mem_prep_{l}")
        cat = memattn_fwd(z, kv, gq4, gk4, cat, batch, seq, qcol, name=f"memattn_fwd_{l}")
        x1, h2 = proj_norm(cat, wo[l], xin, norm2_g[l:l + 1], name=f"out_proj_{l}")
        if l == 0:
            wg_t[1], wu_t[1], wd[1] = gather_ffn(1, (x1, b_win_t), "gather_ffn_b", 5)
        if l == 0:
            gate, up, act, x2, h_next = ffn_fwd(h2, wg_t[0], wu_t[0], wd[0], x1, gain=norm1_g[1:2], name="ffn_fwd_0")
        else:
            gate, up, act, dx_b, loss_blk = ffn_fwd(h2, wg_t[1], wu_t[1], wd[1], x1, target=target, name="ffn_fwd_1")
        saved.append(dict(xin=xin, h=h, mem_n=mem_n, kv=kv, gq4=gq4, gk4=gk4, z=z, qcol=qcol, cat=cat, x1=x1, h2=h2,
                          gate=gate, up=up, act=act, y_conv=y_conv))
        if l == 0:
            xin, h = x2, h_next

    big = {}
    small = {}
    reduced = {}
    groups = 0

    def scatter_siblings(keys):
        nonlocal groups
        gid = groups
        groups += 1
        return gid, keys, scatter_to_sibling([big[k] for k in keys], f"scatter_sibling_{gid}", 8 + 2 * gid)

    def scatter_chips(stage1, when):
        gid, keys, landed1 = stage1
        parts = add_sibling([after(big[keys[0]], when)] + [big[k] for k in keys[1:]], landed1, core_arr,
                            name=f"add_sibling_{gid}")
        landed2 = scatter_to_chips(parts, f"scatter_chips_{gid}", 9 + 2 * gid)
        for k, p, ld in zip(keys, parts, landed2):
            reduced[k] = (p, ld)
        return parts, landed2

    def rows_of(w, transposed):
        w = jnp.swapaxes(w, 1, 2) if transposed else w
        return w.reshape(w.shape[0] * w.shape[1], w.shape[2])

    sharded = {
        "win0": (2, True), "win1": (6, True), "wkv": (14, False), "wo": (15, False),
        "wg": (17, True), "wu": (18, True), "wd": (19, False)}
    weights = [norm1_g, mem_norm_g, a_w_in, a_q_g, a_k_g, a_rel_bias, b_w_in, b_b_in, b_conv_w, b_conv_b, b_ln_g,
               b_ln_b, mq_g, mk_g, w_mem_kv, w_out, norm2_g, w_gate, w_up, w_down]
    moms = [m_norm1_g, m_mem_norm_g, m_a_w_in, m_a_q_g, m_a_k_g, m_a_rel_bias, m_b_w_in, m_b_b_in, m_b_conv_w,
            m_b_conv_b, m_b_ln_g, m_b_ln_b, m_mq_g, m_mk_g, m_w_mem_kv, m_w_out, m_norm2_g, m_w_gate, m_w_up, m_w_down]
    vels = [v_norm1_g, v_mem_norm_g, v_a_w_in, v_a_q_g, v_a_k_g, v_a_rel_bias, v_b_w_in, v_b_b_in, v_b_conv_w,
            v_b_conv_b, v_b_ln_g, v_b_ln_b, v_mq_g, v_mk_g, v_w_mem_kv, v_w_out, v_norm2_g, v_w_gate, v_w_up, v_w_down]
    updated = {}

    def update_layer(l, when):
        for key, (idx, transposed) in sharded.items():
            if key in ("win0", "win1"):
                if key != f"win{l}":
                    continue
                layer, rkey = 0, key
            else:
                layer, rkey = l, f"{key}{l}"
            part, landed = reduced[rkey]
            updated[key] = adamw_shard(
                layer, after(rows_of(weights[idx], transposed), when), rows_of(moms[idx], transposed),
                rows_of(vels[idx], transposed), part, landed, chip_arr, updated.get(key), name=f"adamw_{rkey}")

    mix_landed = None
    for l in (1, 0):
        sv = saved[l]
        dgate, dup, dx1_b, dcat, small[f"norm2_{l}"] = ffn_bwd(
            dx_b, wd[l], sv["gate"], sv["up"], wg_t[l], wu_t[l], sv["x1"], norm2_g[l:l + 1], wo[l], name=f"ffn_bwd_{l}")
        if l == 0:
            dgate = after(dgate, *mix_landed)
            update_layer(1, dx1_b)
        big[f"wg{l}"], big[f"wu{l}"], big[f"wd{l}"] = ffn_weight_grads(
            dgate, dup, sv["h2"], sv["act"], dx_b, name=f"grad_ffn_{l}")
        stage1 = scatter_siblings([f"wd{l}", f"wg{l}", f"wu{l}"])
        big[f"wo{l}"] = mm_tn(sv["cat"], dx1_b, name=f"grad_wo_{l}")
        parts, ffn_landed = scatter_chips(stage1, big[f"wo{l}"])
        dcat = after(dcat, *parts)
        if l == 0:
            dz, dbias, small["a_q"], small["a_k"] = attn_bwd(sv["z"], dcat, gq2, gk2, bias, batch, seq)
            small["rel"] = bias_grad(dbias)
            win_t = a_win_t
        else:
            dz, small["cw"], small["csum"] = conv_bwd(sv["z"], sv["y_conv"], dcat, cw_full, lg_full, lb_full, batch, seq)
            win_t = b_win_t
        dz = after(dz, *ffn_landed)
        dz, dkv, small[f"mq_{l}"], small[f"mk_{l}"] = memattn_bwd(
            sv["z"], sv["kv"], dcat, sv["gq4"], sv["gk4"], dz, batch, seq, sv["qcol"], name=f"memattn_bwd_{l}")
        big[f"win{l}"] = mm_tn(dz, sv["h"], name=f"grad_win_{l}")
        big[f"wkv{l}"] = mm_tn(sv["mem_n"], dkv, name=f"grad_wkv_{l}")
        stage1 = scatter_siblings([f"win{l}", f"wkv{l}", f"wo{l}"])
        dx_b, small[f"norm1_{l}"], dz_sum = in_proj_bwd(
            dz, win_t, sv["xin"], norm1_g[l:l + 1], dx1_b, BF16 if l == 1 else F32, name=f"in_proj_bwd_{l}")
        if l == 1:
            small["bb"] = dz_sum
        parts, mix_landed = scatter_chips(stage1, dx_b)
        dx_b = after(dx_b, *parts)
        small[f"memnorm_{l}"] = mem_norm_grad(dkv, wkv[l], mem2, name=f"mem_norm_grad_{l}")
    grad_x = dx_b.reshape(batch, seq, d)
    update_layer(0, dx_b)

    def shaped(rows, idx, transposed):
        shp = weights[idx].shape
        if transposed:
            return jnp.swapaxes(rows.reshape(shp[0], shp[2], shp[1]), 1, 2)
        return rows.reshape(shp)

    def fold(v, groups):
        return jnp.sum(v.reshape(groups, HEAD_DIM), axis=0, keepdims=True)

    heads = a_rel_bias.shape[1]
    small_list = [
        jnp.concatenate([small["norm1_0"], small["norm1_1"]]),
        jnp.concatenate([small["memnorm_0"], small["memnorm_1"]]),
        fold(small["a_q"], 2), fold(small["a_k"], 2), small["rel"][:heads][None],
        small["bb"], small["cw"][:CONV_W][None], small["csum"][0:1], small["csum"][1:2], small["csum"][2:3],
        jnp.concatenate([fold(small["mq_0"], 4), fold(small["mq_1"], 4)]),
        jnp.concatenate([fold(small["mk_0"], 4), fold(small["mk_1"], 4)]),
        jnp.concatenate([small["norm2_0"], small["norm2_1"]]),
    ]
    (g_norm1, g_memnorm, g_aq, g_ak, g_rel, g_bb_full, g_cw_full, g_cb_full, g_lg_full, g_lb_full,
     g_mq, g_mk, g_norm2, loss_sum) = reduce_small(small_list + [loss_blk])
    loss = loss_sum[0, 0]
    g_bb = lax.dynamic_slice_in_dim(g_bb_full, me * f_loc, f_loc, axis=1)
    g_cw = lax.dynamic_slice_in_dim(g_cw_full, me * c_loc, c_loc, axis=2)
    g_cb = lax.dynamic_slice_in_dim(g_cb_full, me * c_loc, c_loc, axis=1)
    g_lg = lax.dynamic_slice_in_dim(g_lg_full, me * c_loc, c_loc, axis=1)
    g_lb = lax.dynamic_slice_in_dim(g_lb_full, me * c_loc, c_loc, axis=1)

    grads = [g_norm1, g_memnorm, None, g_aq, g_ak, g_rel, None, g_bb, g_cw, g_cb, g_lg, g_lb,
             g_mq, g_mk, None, None, g_norm2, None, None, None]
    deltas, new_m, new_v = [None] * 20, [None] * 20, [None] * 20
    for key, (idx, transposed) in sharded.items():
        grads[idx], deltas[idx], new_m[idx], new_v[idx] = (shaped(r, idx, transposed) for r in updated[key])

    small_idx = [i for i in range(20) if i not in {idx for idx, _ in sharded.values()}]
    dl, nm, nv = adamw_small([weights[i] for i in small_idx], [grads[i] for i in small_idx],
                             [moms[i] for i in small_idx], [vels[i] for i in small_idx])
    for i, a, b, cc in zip(small_idx, dl, nm, nv):
        deltas[i], new_m[i], new_v[i] = a, b, cc

    return (loss, grad_x, *grads, *deltas, *new_m, *new_v)
```

```python
import functools

import jax
import jax.numpy as jnp
from jax import lax
from jax.experimental import pallas as pl
from jax.experimental.pallas import tpu as pltpu
from jax.experimental.pallas import tpu_sc as plsc

F32 = jnp.float32
BF16 = jnp.bfloat16
HIGHEST = lax.Precision.HIGHEST
MESH = pl.DeviceIdType.MESH
ANY = pl.BlockSpec(memory_space=pl.ANY)

N_DEV = 8
D_MODEL = 1024
HEAD_DIM = 64
TOK_WIDTH = 768
MEM_WIDTH = 256
CHUNK = 64
Q_BLOCK = 256
KEY_WIN = 768
BAND = 576
N_REL = 192
CONV_W = 31
CONV_HALO = 32
NORM_EPS = 1e-6
NEG_INF = -1e30
ATTN_SCALE = HEAD_DIM ** -0.5
LANES = 128
ROW_TILE = 512
VMEM_LIMIT = 56 * 1024 * 1024

ADAM_LR, ADAM_B1, ADAM_B2, ADAM_EPS, ADAM_WD, ADAM_STEP = 0.001, 0.9, 0.999, 1e-08, 0.01, 10


def _params(*sem):
    return pltpu.CompilerParams(dimension_semantics=sem, vmem_limit_bytes=VMEM_LIMIT)


WIDE_ROW_TILE = 1024


def _row_tile(m, rows=ROW_TILE):
    return rows if m % rows == 0 else m


def _col_tile(n, cap=1408):
    best = None
    for t in range(LANES, min(n, cap) + 1, LANES):
        if n % t == 0:
            best = t
    return best if best is not None else n


def _dot(a, b, ca, cb):
    return lax.dot_general(a, b, (((ca,), (cb,)), ((), ())), preferred_element_type=F32)


def _sigmoid(x):
    return 0.5 * jnp.tanh(0.5 * x) + 0.5


def mm_nt(a, b, bias=None, out_dtype=BF16, name="mm_nt"):
    m, k = a.shape
    n = b.shape[0]
    tm, tn = _row_tile(m, WIDE_ROW_TILE), _col_tile(n)

    def body(*refs):
        a_ref, b_ref = refs[0], refs[1]
        o_ref = refs[-1]
        acc = _dot(a_ref[...].astype(BF16), b_ref[...].astype(BF16), 1, 1)
        if bias is not None:
            acc = acc + refs[2][...]
        o_ref[...] = acc.astype(o_ref.dtype)

    in_specs = [pl.BlockSpec((tm, k), lambda j, i: (i, 0)), pl.BlockSpec((tn, k), lambda j, i: (j, 0))]
    args = [a, b]
    if bias is not None:
        in_specs.append(pl.BlockSpec((1, tn), lambda j, i: (0, j)))
        args.append(bias)
    return pl.pallas_call(
        body, out_shape=jax.ShapeDtypeStruct((m, n), out_dtype), grid=(n // tn, m // tm),
        in_specs=in_specs, out_specs=pl.BlockSpec((tm, tn), lambda j, i: (i, j)),
        compiler_params=_params("parallel", "arbitrary"), name=name)(*args)


def mm_tn(a, b, out_dtype=BF16, name="mm_tn"):
    t, r = a.shape
    c = b.shape[1]
    tr = _col_tile(r, 512)

    def body(a_ref, b_ref, o_ref):
        o_ref[...] = _dot(a_ref[...].astype(BF16), b_ref[...].astype(BF16), 0, 0).astype(o_ref.dtype)

    return pl.pallas_call(
        body, out_shape=jax.ShapeDtypeStruct((r, c), out_dtype), grid=(r // tr,),
        in_specs=[pl.BlockSpec((t, tr), lambda i: (0, i)), pl.BlockSpec((t, c), lambda i: (0, 0))],
        out_specs=pl.BlockSpec((tr, c), lambda i: (i, 0)),
        compiler_params=_params("parallel"), name=name)(a, b)


def _resident(shape):
    return pl.BlockSpec(shape, lambda i: (0, 0), pipeline_mode=pl.Buffered(1))


def proj_norm(a, b, res, gain, name):
    m, k = a.shape
    n = b.shape[1]
    tm = _row_tile(m, WIDE_ROW_TILE)

    def body(a_ref, b_ref, res_ref, g_ref, x_ref, h_ref):
        xv = res_ref[...] + _dot(a_ref[...], b_ref[...], 1, 0)
        x_ref[...] = xv
        r = lax.rsqrt(jnp.mean(xv * xv, axis=-1, keepdims=True) + NORM_EPS)
        h_ref[...] = (xv * r * g_ref[...]).astype(BF16)

    row = pl.BlockSpec((tm, n), lambda i: (i, 0))
    return pl.pallas_call(
        body, out_shape=(jax.ShapeDtypeStruct((m, n), F32), jax.ShapeDtypeStruct((m, n), BF16)), grid=(m // tm,),
        in_specs=[pl.BlockSpec((tm, k), lambda i: (i, 0)), _resident((k, n)), row, _resident((1, n))],
        out_specs=(row, row), compiler_params=_params("parallel"), name=name)(a, b, res, gain)


def in_proj_bwd(dz, w_t, x, gain, dres, out_dtype, name):
    m, n = x.shape
    k = dz.shape[1]
    tm = _row_tile(m, WIDE_ROW_TILE)

    def body(dz_ref, w_ref, x_ref, g_ref, dres_ref, dx_ref, dg_ref, cs_ref):
        @pl.when(pl.program_id(0) == 0)
        def _():
            dg_ref[...] = jnp.zeros_like(dg_ref)
            cs_ref[...] = jnp.zeros_like(cs_ref)

        dzv = dz_ref[...]
        cs_ref[...] += jnp.sum(dzv.astype(F32), axis=0, keepdims=True)
        dhv = _dot(dzv, w_ref[...], 1, 0)
        xv = x_ref[...]
        r = lax.rsqrt(jnp.mean(xv * xv, axis=-1, keepdims=True) + NORM_EPS)
        xhat = xv * r
        dg_ref[...] += jnp.sum(dhv * xhat, axis=0, keepdims=True)
        dxhat = dhv * g_ref[...]
        dx = dres_ref[...].astype(F32) + r * (dxhat - xhat * jnp.mean(dxhat * xhat, axis=-1, keepdims=True))
        dx_ref[...] = dx.astype(dx_ref.dtype)

    row = pl.BlockSpec((tm, n), lambda i: (i, 0))
    return pl.pallas_call(
        body, out_shape=(jax.ShapeDtypeStruct((m, n), out_dtype), jax.ShapeDtypeStruct((1, n), F32),
                         jax.ShapeDtypeStruct((1, k), F32)), grid=(m // tm,),
        in_specs=[pl.BlockSpec((tm, k), lambda i: (i, 0)), _resident(w_t.shape), row, _resident((1, n)), row],
        out_specs=(row, pl.BlockSpec((1, n), lambda i: (0, 0)), pl.BlockSpec((1, k), lambda i: (0, 0))),
        compiler_params=_params("arbitrary"), name=name)(dz, w_t, x, gain, dres)


FFN_ROWS = 256


def _ffn_row_tile(m):
    return FFN_ROWS if m % FFN_ROWS == 0 else m


def _ffn_chunks(f, width=1024):
    return [(c0, min(c0 + width, f)) for c0 in range(0, f, width)]


def ffn_fwd(h2, wg_t, wu_t, wd, x1, gain=None, target=None, name="ffn_fwd"):
    n, d = h2.shape
    f = wg_t.shape[0]
    tm = _row_tile(n)
    nt = n // tm
    last = target is not None

    def body(h_ref, wg_ref, wu_ref, wd_ref, x1_ref, e_ref, g_ref, u_ref, a_ref, *rest):
        hv = h_ref[...]
        xv = x1_ref[...]
        for c0, c1 in _ffn_chunks(f):
            gv = _dot(hv, wg_ref[c0:c1, :], 1, 1)
            uv = _dot(hv, wu_ref[c0:c1, :], 1, 1)
            g_ref[:, c0:c1] = gv.astype(BF16)
            u_ref[:, c0:c1] = uv.astype(BF16)
            av = (gv * _sigmoid(gv) * uv).astype(BF16)
            a_ref[:, c0:c1] = av
            xv = xv + _dot(av, wd_ref[c0:c1, :], 1, 0)
        if not last:
            x_ref, hn_ref = rest
            x_ref[...] = xv
            r = lax.rsqrt(jnp.mean(xv * xv, axis=-1, keepdims=True) + NORM_EPS)
            hn_ref[...] = (xv * r * e_ref[...]).astype(BF16)
        else:
            dyb_ref, l_ref, acc_ref = rest
            i = pl.program_id(0)

            @pl.when(i == 0)
            def _():
                acc_ref[...] = jnp.zeros_like(acc_ref)

            err = xv - e_ref[...]
            dyb_ref[...] = (err * (1.0 / d)).astype(BF16)
            acc_ref[...] += jnp.sum(err * err, axis=0, keepdims=True)

            @pl.when(i == nt - 1)
            def _():
                total = jnp.sum(acc_ref[...], axis=-1, keepdims=True) * (0.5 / d)
                l_ref[...] = jnp.broadcast_to(total, l_ref.shape)

    row_d = pl.BlockSpec((tm, d), lambda i: (i, 0))
    row_f = pl.BlockSpec((tm, f), lambda i: (i, 0))
    act_shape = jax.ShapeDtypeStruct((n, f), BF16)
    if not last:
        extra_in, extra = _resident((1, d)), gain
        out_shape = (act_shape, act_shape, act_shape, jax.ShapeDtypeStruct((n, d), F32), jax.ShapeDtypeStruct((n, d), BF16))
        out_specs = (row_f, row_f, row_f, row_d, row_d)
        scratch = []
    else:
        extra_in, extra = row_d, target
        out_shape = (act_shape, act_shape, act_shape, jax.ShapeDtypeStruct((n, d), BF16),
                     jax.ShapeDtypeStruct((8, LANES), F32))
        out_specs = (row_f, row_f, row_f, row_d, pl.BlockSpec((8, LANES), lambda i: (0, 0)))
        scratch = [pltpu.VMEM((1, d), F32)]
    return pl.pallas_call(
        body, out_shape=out_shape, grid=(nt,),
        in_specs=[row_d, _resident((f, d)), _resident((f, d)), _resident((f, d)), row_d, extra_in],
        out_specs=out_specs, scratch_shapes=scratch,
        compiler_params=_params("arbitrary"), name=name)(h2, wg_t, wu_t, wd, x1, extra)


def ffn_bwd(dx_b, wd, gate, up, wg_t, wu_t, x1, gain, wo, name="ffn_bwd"):
    n, d = x1.shape
    f = wd.shape[0]
    tm = _ffn_row_tile(n)

    def body(dxb_ref, wd_ref, g_ref, u_ref, wg_ref, wu_ref, x_ref, gain_ref, wo_ref,
             dg_ref, du_ref, dxo_ref, dc_ref, dgain_ref):
        @pl.when(pl.program_id(0) == 0)
        def _():
            dgain_ref[...] = jnp.zeros_like(dgain_ref)

        dxv = dxb_ref[...]
        dhv = jnp.zeros((tm, d), F32)
        for c0, c1 in _ffn_chunks(f):
            dact = _dot(dxv, wd_ref[c0:c1, :], 1, 1)
            gv = g_ref[:, c0:c1].astype(F32)
            uv = u_ref[:, c0:c1].astype(F32)
            sg = _sigmoid(gv)
            dgv = (dact * uv * sg * (1.0 + gv * (1.0 - sg))).astype(BF16)
            duv = (dact * gv * sg).astype(BF16)
            dg_ref[:, c0:c1] = dgv
            du_ref[:, c0:c1] = duv
            dhv = dhv + _dot(dgv, wg_ref[c0:c1, :], 1, 0) + _dot(duv, wu_ref[c0:c1, :], 1, 0)
        xv = x_ref[...]
        r = lax.rsqrt(jnp.mean(xv * xv, axis=-1, keepdims=True) + NORM_EPS)
        xhat = xv * r
        dgain_ref[...] += jnp.sum(dhv * xhat, axis=0, keepdims=True)
        dxhat = dhv * gain_ref[...]
        dxb = (dxb_ref[...].astype(F32) + r * (dxhat - xhat * jnp.mean(dxhat * xhat, axis=-1, keepdims=True))).astype(BF16)
        dxo_ref[...] = dxb
        dc_ref[...] = _dot(dxb, wo_ref[...], 1, 1).astype(BF16)

    row_d = pl.BlockSpec((tm, d), lambda i: (i, 0))
    row_f = pl.BlockSpec((tm, f), lambda i: (i, 0))
    w_spec = _resident((f, d))
    act_shape = jax.ShapeDtypeStruct((n, f), BF16)
    row_shape = jax.ShapeDtypeStruct((n, d), BF16)
    return pl.pallas_call(
        body, out_shape=(act_shape, act_shape, row_shape, jax.ShapeDtypeStruct((n, wo.shape[0]), BF16),
                         jax.ShapeDtypeStruct((1, d), F32)),
        grid=(n // tm,),
        in_specs=[row_d, w_spec, row_f, row_f, w_spec, w_spec, row_d, _resident((1, d)), _resident(wo.shape)],
        out_specs=(row_f, row_f, row_d, pl.BlockSpec((tm, wo.shape[0]), lambda i: (i, 0)),
                   pl.BlockSpec((1, d), lambda i: (0, 0))),
        compiler_params=_params("arbitrary"), name=name)(dx_b, wd, gate, up, wg_t, wu_t, x1, gain, wo)


def ffn_weight_grads(dgate, dup, h2, act, dx_b, name="ffn_weight_grads"):
    t, r = dgate.shape
    c = h2.shape[1]
    tr = _col_tile(r, 512)

    def body(a1_ref, a2_ref, a3_ref, b12_ref, b3_ref, o1_ref, o2_ref, o3_ref):
        bv = b12_ref[...]
        o1_ref[...] = _dot(a1_ref[...], bv, 0, 0).astype(o1_ref.dtype)
        o2_ref[...] = _dot(a2_ref[...], bv, 0, 0).astype(o2_ref.dtype)
        o3_ref[...] = _dot(a3_ref[...], b3_ref[...], 0, 0).astype(o3_ref.dtype)

    a_spec = pl.BlockSpec((t, tr), lambda i: (0, i))
    o_spec = pl.BlockSpec((tr, c), lambda i: (i, 0))
    shape = jax.ShapeDtypeStruct((r, c), BF16)
    return pl.pallas_call(
        body, out_shape=(shape, shape, shape), grid=(r // tr,),
        in_specs=[a_spec, a_spec, a_spec, _resident((t, c)), _resident((t, c))],
        out_specs=(o_spec, o_spec, o_spec), compiler_params=_params("parallel"), name=name)(dgate, dup, act, h2, dx_b)


def rms_fwd(x, g, name="rms_fwd"):
    n, d = x.shape
    tm = _row_tile(n)

    def body(x_ref, g_ref, o_ref):
        xv = x_ref[...]
        r = lax.rsqrt(jnp.mean(xv * xv, axis=-1, keepdims=True) + NORM_EPS)
        o_ref[...] = (xv * r * g_ref[...]).astype(o_ref.dtype)

    return pl.pallas_call(
        body, out_shape=jax.ShapeDtypeStruct((n, d), BF16), grid=(n // tm,),
        in_specs=[pl.BlockSpec((tm, d), lambda i: (i, 0)), pl.BlockSpec((1, d), lambda i: (0, 0))],
        out_specs=pl.BlockSpec((tm, d), lambda i: (i, 0)),
        compiler_params=_params("parallel"), name=name)(x, g)


def _group_masks(width):
    lane = lax.broadcasted_iota(jnp.int32, (1, width), 1)
    return [(lane >= HEAD_DIM * g) & (lane < HEAD_DIM * (g + 1)) for g in range(width // HEAD_DIM)]


def _group_sum(x, masks):
    out = jnp.zeros_like(x)
    for msk in masks:
        s = jnp.sum(jnp.where(msk, x, 0.0), axis=-1, keepdims=True)
        out = jnp.where(msk, s, out)
    return out


def _head_norm(x, gain, masks):
    r = lax.rsqrt(_group_sum(x * x, masks) * (1.0 / HEAD_DIM) + NORM_EPS)
    xhat = x * r
    return xhat * gain, xhat, r


def _head_norm_bwd(dxn, xhat, r, gain, masks):
    dgain = jnp.sum(dxn * xhat, axis=0, keepdims=True)
    dxhat = dxn * gain
    mean_t = _group_sum(dxhat * xhat, masks) * (1.0 / HEAD_DIM)
    return r * (dxhat - xhat * mean_t), dgain


def _softmax_rows(s):
    e = jnp.exp(s - jnp.max(s, axis=-1, keepdims=True))
    return e * (1.0 / jnp.sum(e, axis=-1, keepdims=True))


def _rel_onehot():
    col = lax.broadcasted_iota(jnp.int32, (1, KEY_WIN), 1)
    off = jnp.where(col < KEY_WIN - LANES, col, col - KEY_WIN)
    idx = jnp.clip(8 * CHUNK - off, -(CHUNK - 1), LANES) + (CHUNK - 1)
    return (lax.broadcasted_iota(jnp.int32, (N_REL, KEY_WIN), 0) == idx).astype(F32)


def bias_blocks(rel16):
    heads = TOK_WIDTH // HEAD_DIM

    def body(rel_ref, o_ref, u_ref):
        u_ref[...] = jnp.dot(rel_ref[...], _rel_onehot(), precision=HIGHEST, preferred_element_type=F32)
        row = lax.broadcasted_iota(jnp.int32, (CHUNK, KEY_WIN), 0)
        col = lax.broadcasted_iota(jnp.int32, (CHUNK, KEY_WIN), 1)
        for h in range(heads):
            xv = jnp.broadcast_to(u_ref[h:h + 1, :], (CHUNK, KEY_WIN))
            for b in range(6):
                xv = jnp.where(((row >> b) & 1) == 1, pltpu.roll(xv, 1 << b, axis=1), xv)
            xv = jnp.where(col < BAND, xv, NEG_INF)
            for i in range(Q_BLOCK // CHUNK):
                o_ref[h, CHUNK * i:CHUNK * (i + 1), :] = pltpu.roll(xv, CHUNK * i, axis=1) if i else xv

    return pl.pallas_call(
        body, out_shape=jax.ShapeDtypeStruct((heads, Q_BLOCK, KEY_WIN), F32),
        scratch_shapes=[pltpu.VMEM((16, KEY_WIN), F32)], name="bias_blocks")(rel16)


def bias_grad(dbias):
    heads = dbias.shape[0]

    def body(db_ref, o_ref, y_ref):
        y_ref[...] = jnp.zeros_like(y_ref)
        row = lax.broadcasted_iota(jnp.int32, (CHUNK, KEY_WIN), 0)
        for h in range(heads):
            fv = db_ref[h, 0:CHUNK, :]
            for i in range(1, Q_BLOCK // CHUNK):
                fv = fv + pltpu.roll(db_ref[h, CHUNK * i:CHUNK * (i + 1), :], KEY_WIN - CHUNK * i, axis=1)
            for b in range(6):
                fv = jnp.where(((row >> b) & 1) == 1, pltpu.roll(fv, KEY_WIN - (1 << b), axis=1), fv)
            y_ref[h:h + 1, :] = jnp.sum(fv, axis=0, keepdims=True)
        o_ref[...] = lax.dot_general(y_ref[...], _rel_onehot(), (((1,), (1,)), ((), ())),
                                     precision=HIGHEST, preferred_element_type=F32)

    return pl.pallas_call(
        body, out_shape=jax.ShapeDtypeStruct((16, N_REL), F32),
        scratch_shapes=[pltpu.VMEM((16, KEY_WIN), F32)], name="bias_grad")(dbias)


def _attn_windows(seq):
    out = []
    for j in range(seq // Q_BLOCK):
        r0 = j * Q_BLOCK
        k0 = max(0, r0 - 8 * CHUNK)
        width = r0 + Q_BLOCK - k0
        out.append((r0, k0, width, KEY_WIN - width))
    return out


def attn_fwd(z, gq2, gk2, bias, batch, seq):
    n = z.shape[0]
    pairs = TOK_WIDTH // LANES

    def body(q_ref, k_ref, v_ref, gq_ref, gk_ref, b_ref, o_ref, qs_s, kn_s):
        masks = _group_masks(LANES)
        qs_s[...] = (_head_norm(q_ref[...].astype(F32), gq_ref[...], masks)[0] * ATTN_SCALE).astype(BF16)
        kn_s[...] = _head_norm(k_ref[...].astype(F32), gk_ref[...], masks)[0].astype(BF16)
        for r0, k0, width, c0 in _attn_windows(seq):
            qb = qs_s[r0:r0 + Q_BLOCK, :]
            kw = kn_s[k0:k0 + width, :]
            vw = v_ref[k0:k0 + width, :]
            out = jnp.zeros((Q_BLOCK, LANES), F32)
            for h, msk in enumerate(masks):
                qh = jnp.where(msk, qb, jnp.zeros_like(qb))
                s = _dot(qh, kw, 1, 1) + b_ref[h, :, c0:KEY_WIN]
                p = _softmax_rows(s).astype(BF16)
                out = jnp.where(msk, _dot(p, vw, 1, 0), out)
            o_ref[r0:r0 + Q_BLOCK, :] = out.astype(o_ref.dtype)

    def col(off):
        return pl.BlockSpec((seq, LANES), lambda b, p: (b, off + p))

    vec = pl.BlockSpec((1, LANES), lambda b, p: (0, 0))
    return pl.pallas_call(
        body, out_shape=jax.ShapeDtypeStruct((n, D_MODEL), BF16), grid=(batch, pairs),
        in_specs=[col(0), col(pairs), col(2 * pairs), vec, vec,
                  pl.BlockSpec((2, Q_BLOCK, KEY_WIN), lambda b, p: (p, 0, 0))],
        out_specs=pl.BlockSpec((seq, LANES), lambda b, p: (b, p)),
        scratch_shapes=[pltpu.VMEM((seq, LANES), BF16), pltpu.VMEM((seq, LANES), BF16)],
        compiler_params=_params("parallel", "arbitrary"), name="attn_fwd")(z, z, z, gq2, gk2, bias)


def attn_bwd(z, dcat, gq2, gk2, bias, batch, seq):
    n = z.shape[0]
    pairs = TOK_WIDTH // LANES

    def body(q_ref, k_ref, v_ref, do_ref, gq_ref, gk_ref, b_ref,
             dz_ref, db_ref, dgq_ref, dgk_ref, qs_s, kn_s, dqn_s, dkn_s, dv_s, dk_o, dv_o):
        pi, bi, which = pl.program_id(0), pl.program_id(1), pl.program_id(2)

        @pl.when(which == 0)
        def _():
            masks = _group_masks(LANES)

            @pl.when(bi == 0)
            def _():
                db_ref[...] = jnp.zeros_like(db_ref)

            @pl.when((bi == 0) & (pi == 0))
            def _():
                dgq_ref[...] = jnp.zeros_like(dgq_ref)
                dgk_ref[...] = jnp.zeros_like(dgk_ref)

            qn, qhat, rq = _head_norm(q_ref[...].astype(F32), gq_ref[...], masks)
            kn, khat, rk = _head_norm(k_ref[...].astype(F32), gk_ref[...], masks)
            qs_s[...] = (qn * ATTN_SCALE).astype(BF16)
            kn_s[...] = kn.astype(BF16)
            dkn_s[...] = jnp.zeros_like(dkn_s)
            dv_s[...] = jnp.zeros_like(dv_s)
            for r0, k0, width, c0 in _attn_windows(seq):
                qb = qs_s[r0:r0 + Q_BLOCK, :]
                dob = do_ref[r0:r0 + Q_BLOCK, :]
                kw = kn_s[k0:k0 + width, :]
                vw = v_ref[k0:k0 + width, :]
                dq_acc = jnp.zeros((Q_BLOCK, LANES), F32)
                dk_acc = jnp.zeros((width, LANES), F32)
                dv_acc = jnp.zeros((width, LANES), F32)
                for h, msk in enumerate(masks):
                    qh = jnp.where(msk, qb, jnp.zeros_like(qb))
                    doh = jnp.where(msk, dob, jnp.zeros_like(dob))
                    p = _softmax_rows(_dot(qh, kw, 1, 1) + b_ref[h, :, c0:KEY_WIN])
                    dp = _dot(doh, vw, 1, 1)
                    ds = p * (dp - jnp.sum(p * dp, axis=-1, keepdims=True))
                    db_ref[h, :, c0:KEY_WIN] += ds
                    dsb = ds.astype(BF16)
                    dq_acc = jnp.where(msk, _dot(dsb, kw, 1, 0), dq_acc)
                    dk_acc = jnp.where(msk, _dot(dsb, qb, 0, 0), dk_acc)
                    dv_acc = jnp.where(msk, _dot(p.astype(BF16), dob, 0, 0), dv_acc)
                dqn_s[r0:r0 + Q_BLOCK, :] = dq_acc * ATTN_SCALE
                dkn_s[k0:k0 + width, :] += dk_acc
                dv_s[k0:k0 + width, :] += dv_acc
            dq, dgq = _head_norm_bwd(dqn_s[...], qhat, rq, gq_ref[...], masks)
            dk, dgk = _head_norm_bwd(dkn_s[...], khat, rk, gk_ref[...], masks)
            dz_ref[...] = dq.astype(dz_ref.dtype)
            dk_o[...] = dk.astype(dk_o.dtype)
            dv_o[...] = dv_s[...].astype(dv_o.dtype)
            dgq_ref[...] += dgq
            dgk_ref[...] += dgk

        @pl.when(which == 1)
        def _():
            dz_ref[...] = dk_o[...]

        @pl.when(which == 2)
        def _():
            dz_ref[...] = dv_o[...]

    def ahead(p, b, t):
        nb = b + jnp.where(t > 0, 1, 0)
        wrap = jnp.where(nb >= batch, 1, 0)
        return jnp.minimum(p + wrap, pairs - 1), nb - wrap * batch

    def col(off):
        def index(p, b, t):
            np_, nb = ahead(p, b, t)
            return nb, off + np_
        return pl.BlockSpec((seq, LANES), index)

    vec = pl.BlockSpec((1, LANES), lambda p, b, t: (0, 0))
    blk = pl.BlockSpec((2, Q_BLOCK, KEY_WIN), lambda p, b, t: (p, 0, 0))
    blk_in = pl.BlockSpec((2, Q_BLOCK, KEY_WIN), lambda p, b, t: (ahead(p, b, t)[0], 0, 0))
    v_shape = jax.ShapeDtypeStruct((1, LANES), F32)
    return pl.pallas_call(
        body,
        out_shape=(jax.ShapeDtypeStruct(z.shape, BF16), jax.ShapeDtypeStruct(bias.shape, F32), v_shape, v_shape),
        grid=(pairs, batch, 3),
        in_specs=[col(0), col(pairs), col(2 * pairs), col(0), vec, vec, blk_in],
        out_specs=(pl.BlockSpec((seq, LANES), lambda p, b, t: (b, t * pairs + p)), blk, vec, vec),
        scratch_shapes=[pltpu.VMEM((seq, LANES), BF16), pltpu.VMEM((seq, LANES), BF16),
                        pltpu.VMEM((seq, LANES), F32), pltpu.VMEM((seq, LANES), F32), pltpu.VMEM((seq, LANES), F32),
                        pltpu.VMEM((seq, LANES), BF16), pltpu.VMEM((seq, LANES), BF16)],
        compiler_params=_params("arbitrary", "arbitrary", "arbitrary"), name="attn_bwd")(
            z, z, z, dcat, gq2, gk2, bias)


MEM_ROWS = 512


def memattn_fwd(z, kv, gq4, gk4, cat, batch, seq, qcol, name):
    mtok = kv.shape[0] // batch
    rows = min(MEM_ROWS, seq)

    def body(q_ref, kv_ref, gq_ref, gk_ref, cat_ref, o_ref):
        del cat_ref
        masks = _group_masks(MEM_WIDTH)
        kn = _head_norm(kv_ref[:, 0:MEM_WIDTH], gk_ref[...], masks)[0].astype(BF16)
        vm = kv_ref[:, MEM_WIDTH:2 * MEM_WIDTH].astype(BF16)
        for t in range(seq // rows):
            sl = slice(t * rows, (t + 1) * rows)
            qs = (_head_norm(q_ref[sl, :].astype(F32), gq_ref[...], masks)[0] * ATTN_SCALE).astype(BF16)
            out = jnp.zeros((rows, MEM_WIDTH), F32)
            for msk in masks:
                qh = jnp.where(msk, qs, jnp.zeros_like(qs))
                p = _softmax_rows(_dot(qh, kn, 1, 1)).astype(BF16)
                out = jnp.where(msk, _dot(p, vm, 1, 0), out)
            o_ref[sl, :] = out.astype(o_ref.dtype)

    vec = pl.BlockSpec((1, MEM_WIDTH), lambda b: (0, 0))
    return pl.pallas_call(
        body, out_shape=jax.ShapeDtypeStruct(cat.shape, cat.dtype), grid=(batch,),
        in_specs=[pl.BlockSpec((seq, MEM_WIDTH), lambda b: (b, qcol)),
                  pl.BlockSpec((mtok, 2 * MEM_WIDTH), lambda b: (b, 0)), vec, vec, ANY],
        out_specs=pl.BlockSpec((seq, MEM_WIDTH), lambda b: (b, TOK_WIDTH // MEM_WIDTH)),
        input_output_aliases={4: 0},
        compiler_params=_params("parallel"), name=name)(z, kv, gq4, gk4, cat)


def memattn_bwd(z, kv, dcat, gq4, gk4, dz, batch, seq, qcol, name):
    mtok = kv.shape[0] // batch
    rows = min(MEM_ROWS, seq)

    def body(q_ref, kv_ref, do_ref, gq_ref, gk_ref, dz_in_ref, dq_ref, dkv_ref, dgq_ref, dgk_ref):
        del dz_in_ref
        @pl.when(pl.program_id(0) == 0)
        def _():
            dgq_ref[...] = jnp.zeros_like(dgq_ref)
            dgk_ref[...] = jnp.zeros_like(dgk_ref)

        masks = _group_masks(MEM_WIDTH)
        kn_f, khat, rk = _head_norm(kv_ref[:, 0:MEM_WIDTH], gk_ref[...], masks)
        kn = kn_f.astype(BF16)
        vm = kv_ref[:, MEM_WIDTH:2 * MEM_WIDTH].astype(BF16)
        dkn = jnp.zeros((mtok, MEM_WIDTH), F32)
        dvm = jnp.zeros((mtok, MEM_WIDTH), F32)
        dgq = jnp.zeros((1, MEM_WIDTH), F32)
        for t in range(seq // rows):
            sl = slice(t * rows, (t + 1) * rows)
            qn_f, qhat, rq = _head_norm(q_ref[sl, :].astype(F32), gq_ref[...], masks)
            qs = (qn_f * ATTN_SCALE).astype(BF16)
            dob = do_ref[sl, :]
            dqn = jnp.zeros((rows, MEM_WIDTH), F32)
            for msk in masks:
                qh = jnp.where(msk, qs, jnp.zeros_like(qs))
                doh = jnp.where(msk, dob, jnp.zeros_like(dob))
                p = _softmax_rows(_dot(qh, kn, 1, 1))
                dp = _dot(doh, vm, 1, 1)
                ds = p * (dp - jnp.sum(p * dp, axis=-1, keepdims=True))
                dsb = ds.astype(BF16)
                dqn = jnp.where(msk, _dot(dsb, kn, 1, 0), dqn)
                dkn = dkn + jnp.where(msk, _dot(dsb, qs, 0, 0), 0.0)
                dvm = dvm + jnp.where(msk, _dot(p.astype(BF16), dob, 0, 0), 0.0)
            dq, dg = _head_norm_bwd(dqn * ATTN_SCALE, qhat, rq, gq_ref[...], masks)
            dq_ref[sl, :] = dq.astype(dq_ref.dtype)
            dgq = dgq + dg
        dk, dgk = _head_norm_bwd(dkn, khat, rk, gk_ref[...], masks)
        dkv_ref[:, 0:MEM_WIDTH] = dk
        dkv_ref[:, MEM_WIDTH:2 * MEM_WIDTH] = dvm
        dgq_ref[...] += dgq
        dgk_ref[...] += dgk

    vec = pl.BlockSpec((1, MEM_WIDTH), lambda b: (0, 0))
    kv_spec = pl.BlockSpec((mtok, 2 * MEM_WIDTH), lambda b: (b, 0))
    v_shape = jax.ShapeDtypeStruct((1, MEM_WIDTH), F32)
    q_spec = pl.BlockSpec((seq, MEM_WIDTH), lambda b: (b, qcol))
    return pl.pallas_call(
        body,
        out_shape=(jax.ShapeDtypeStruct(dz.shape, dz.dtype), jax.ShapeDtypeStruct(kv.shape, F32), v_shape, v_shape),
        grid=(batch,),
        in_specs=[q_spec, kv_spec, pl.BlockSpec((seq, MEM_WIDTH), lambda b: (b, TOK_WIDTH // MEM_WIDTH)), vec, vec, ANY],
        out_specs=(q_spec, kv_spec, vec, vec),
        input_output_aliases={5: 0},
        compiler_params=_params("arbitrary"), name=name)(z, kv, dcat, gq4, gk4, dz)


def mem_prep(mem, gain, wkv, name):
    t, d = mem.shape

    def body(m_ref, g_ref, w_ref, n_ref, kv_ref):
        mv = m_ref[...]
        r = lax.rsqrt(jnp.mean(mv * mv, axis=-1, keepdims=True) + NORM_EPS)
        nv = (mv * r * g_ref[...]).astype(BF16)
        n_ref[...] = nv
        kv_ref[...] = _dot(nv, w_ref[...], 1, 0)

    vmem = pl.BlockSpec(memory_space=pltpu.VMEM)
    return pl.pallas_call(
        body, out_shape=(jax.ShapeDtypeStruct((t, d), BF16), jax.ShapeDtypeStruct((t, wkv.shape[1]), F32)),
        in_specs=[vmem, vmem, vmem], out_specs=(vmem, vmem),
        compiler_params=pltpu.CompilerParams(vmem_limit_bytes=VMEM_LIMIT), name=name)(mem, gain, wkv)


def mem_norm_grad(dkv, wkv, mem, name):
    t, d = mem.shape

    def body(dkv_ref, w_ref, m_ref, dg_ref):
        dn = _dot(dkv_ref[...].astype(BF16), w_ref[...], 1, 1)
        mv = m_ref[...]
        r = lax.rsqrt(jnp.mean(mv * mv, axis=-1, keepdims=True) + NORM_EPS)
        dg_ref[...] = jnp.sum(dn * (mv * r), axis=0, keepdims=True)

    vmem = pl.BlockSpec(memory_space=pltpu.VMEM)
    return pl.pallas_call(
        body, out_shape=jax.ShapeDtypeStruct((1, d), F32), in_specs=[vmem, vmem, vmem], out_specs=vmem,
        compiler_params=pltpu.CompilerParams(vmem_limit_bytes=VMEM_LIMIT), name=name)(dkv, wkv, mem)


CONV_ROWS = 256


def _glu(a_ref, g_ref):
    return a_ref[...].astype(F32) * _sigmoid(g_ref[...].astype(F32))


def _layer_norm_stats(y):
    mu = jnp.mean(y, axis=-1, keepdims=True)
    yc = y - mu
    rstd = lax.rsqrt(jnp.mean(yc * yc, axis=-1, keepdims=True) + NORM_EPS)
    return yc * rstd, rstd


CONV_WIN = CONV_HALO + CONV_ROWS
SUBLANES = 8
SHIFT_ROWS = CONV_WIN - SUBLANES


def _preshift(win, shifted):
    for s in range(1, SUBLANES):
        shifted[s - 1, :, :] = win[s:s + SHIFT_ROWS, :]


TAP_ROWS = 64
TAP_TILES = [(r0, slice(c0, c0 + LANES)) for c0 in range(0, TOK_WIDTH, LANES) for r0 in range(0, CONV_ROWS, TAP_ROWS)]


def _tap(win, shifted, off, r0, lanes):
    s = off % SUBLANES
    base = off - s + r0
    if s == 0:
        return win[base:base + TAP_ROWS, lanes]
    return shifted[s - 1, base:base + TAP_ROWS, lanes]


def _fold_rows(x):
    return jnp.sum(x.reshape(TAP_ROWS // SUBLANES, SUBLANES, LANES), axis=0)


def conv_fwd(z, cw, cb, lg, lb, batch, seq):
    n = z.shape[0]
    nt = seq // CONV_ROWS
    sub = CONV_ROWS // CONV_HALO
    lead = CONV_HALO - (CONV_W - 1)

    def body(a_ref, g_ref, ap_ref, gp_ref, cw_ref, cb_ref, lg_ref, lb_ref, o_ref, y_ref, win, shifted):
        first = pl.program_id(1) == 0
        win[0:CONV_HALO, :] = jnp.where(first, 0.0, _glu(ap_ref, gp_ref))
        win[CONV_HALO:CONV_WIN, :] = _glu(a_ref, g_ref)
        _preshift(win, shifted)
        for r0, lanes in TAP_TILES:
            acc = jnp.zeros((TAP_ROWS, LANES), F32) + cb_ref[:, lanes]
            for w in range(CONV_W):
                acc = acc + _tap(win, shifted, lead + w, r0, lanes) * cw_ref[w:w + 1, lanes]
            y_ref[r0:r0 + TAP_ROWS, lanes] = acc
        yh, _ = _layer_norm_stats(y_ref[...])
        t = yh * lg_ref[...] + lb_ref[...]
        o_ref[...] = (t * _sigmoid(t)).astype(o_ref.dtype)

    def cur(c):
        return pl.BlockSpec((CONV_ROWS, TOK_WIDTH), lambda b, i: (b * nt + i, c))

    def prev(c):
        return pl.BlockSpec((CONV_HALO, TOK_WIDTH), lambda b, i: (jnp.maximum((b * nt + i) * sub - 1, 0), c))

    vec = pl.BlockSpec((1, TOK_WIDTH), lambda b, i: (0, 0))
    return pl.pallas_call(
        body, out_shape=(jax.ShapeDtypeStruct((n, D_MODEL), BF16), jax.ShapeDtypeStruct((n, TOK_WIDTH), F32)),
        grid=(batch, nt),
        in_specs=[cur(0), cur(1), prev(0), prev(1), pl.BlockSpec((32, TOK_WIDTH), lambda b, i: (0, 0)), vec, vec, vec],
        out_specs=(cur(0), cur(0)),
        scratch_shapes=[pltpu.VMEM((CONV_WIN, TOK_WIDTH), F32), pltpu.VMEM((SUBLANES - 1, SHIFT_ROWS, TOK_WIDTH), F32)],
        compiler_params=_params("parallel", "arbitrary"), name="conv_fwd")(z, z, z, z, cw, cb, lg, lb)


def conv_bwd(z, y, dcat, cw, lg, lb, batch, seq):
    n = z.shape[0]
    nt = seq // CONV_ROWS
    sub = CONV_ROWS // CONV_HALO
    lead = CONV_HALO - (CONV_W - 1)
    last_blk = n // CONV_HALO - 1

    def body(a_ref, g_ref, ap_ref, gp_ref, y_ref, yn_ref, do_ref, don_ref, cw_ref, lg_ref, lb_ref,
             dz_ref, dcw_ref, dsm_ref, win, shifted, dyw, dshifted, dg_o):
        b, i, which = pl.program_id(0), pl.program_id(1), pl.program_id(2)

        @pl.when(which == 0)
        def _():
            first, last = i == 0, i == nt - 1

            @pl.when((b == 0) & (i == 0))
            def _():
                dcw_ref[...] = jnp.zeros_like(dcw_ref)
                dsm_ref[...] = jnp.zeros_like(dsm_ref)

            win[0:CONV_HALO, :] = jnp.where(first, 0.0, _glu(ap_ref, gp_ref))
            win[CONV_HALO:CONV_WIN, :] = _glu(a_ref, g_ref)
            _preshift(win, shifted)
            yv = jnp.concatenate([y_ref[...], yn_ref[...]], axis=0)
            yh, rstd = _layer_norm_stats(yv)
            t = yh * lg_ref[...] + lb_ref[...]
            st = _sigmoid(t)
            dout = jnp.concatenate(
                [do_ref[...].astype(F32), jnp.where(last, 0.0, don_ref[...].astype(F32))], axis=0)
            dt = dout * st * (1.0 + t * (1.0 - st))
            dyh = dt * lg_ref[...]
            dy = rstd * (dyh - jnp.mean(dyh, axis=-1, keepdims=True)
                         - yh * jnp.mean(dyh * yh, axis=-1, keepdims=True))
            dyw[...] = dy
            _preshift(dyw, dshifted)
            dsm_ref[0:1, :] += jnp.sum(dy[0:CONV_ROWS], axis=0, keepdims=True)
            dsm_ref[1:2, :] += jnp.sum((dt * yh)[0:CONV_ROWS], axis=0, keepdims=True)
            dsm_ref[2:3, :] += jnp.sum(dt[0:CONV_ROWS], axis=0, keepdims=True)
            for c0 in range(0, TOK_WIDTH, LANES):
                lanes = slice(c0, c0 + LANES)
                dcw_acc = [jnp.zeros((SUBLANES, LANES), F32) for _ in range(CONV_W)]
                for r0 in range(0, CONV_ROWS, TAP_ROWS):
                    dyt = dyw[r0:r0 + TAP_ROWS, lanes]
                    dglu = jnp.zeros((TAP_ROWS, LANES), F32)
                    for w in range(CONV_W):
                        dcw_acc[w] = dcw_acc[w] + _fold_rows(dyt * _tap(win, shifted, lead + w, r0, lanes))
                        dglu = dglu + _tap(dyw, dshifted, CONV_W - 1 - w, r0, lanes) * cw_ref[w:w + 1, lanes]
                    avt = a_ref[r0:r0 + TAP_ROWS, lanes].astype(F32)
                    sgt = _sigmoid(g_ref[r0:r0 + TAP_ROWS, lanes].astype(F32))
                    dz_ref[r0:r0 + TAP_ROWS, lanes] = (dglu * sgt).astype(dz_ref.dtype)
                    dg_o[r0:r0 + TAP_ROWS, lanes] = (dglu * avt * sgt * (1.0 - sgt)).astype(dg_o.dtype)
                for w in range(CONV_W):
                    dcw_ref[w:w + 1, lanes] += jnp.sum(dcw_acc[w], axis=0, keepdims=True)

        @pl.when(which == 1)
        def _():
            dz_ref[...] = dg_o[...]

    def ahead(b, i, t):
        return jnp.minimum(b * nt + i + t, batch * nt - 1)

    def cur(c):
        return pl.BlockSpec((CONV_ROWS, TOK_WIDTH), lambda b, i, t: (ahead(b, i, t), c))

    def prev(c):
        return pl.BlockSpec((CONV_HALO, TOK_WIDTH), lambda b, i, t: (jnp.maximum(ahead(b, i, t) * sub - 1, 0), c))

    nxt = pl.BlockSpec((CONV_HALO, TOK_WIDTH),
                       lambda b, i, t: (jnp.minimum((ahead(b, i, t) + 1) * sub, last_blk), 0))
    vec = pl.BlockSpec((1, TOK_WIDTH), lambda b, i, t: (0, 0))
    full32 = pl.BlockSpec((32, TOK_WIDTH), lambda b, i, t: (0, 0))
    return pl.pallas_call(
        body,
        out_shape=(jax.ShapeDtypeStruct(z.shape, BF16), jax.ShapeDtypeStruct((32, TOK_WIDTH), F32),
                   jax.ShapeDtypeStruct((8, TOK_WIDTH), F32)),
        grid=(batch, nt, 2),
        in_specs=[cur(0), cur(1), prev(0), prev(1), cur(0), nxt, cur(0), nxt, full32, vec, vec],
        out_specs=(pl.BlockSpec((CONV_ROWS, TOK_WIDTH), lambda b, i, t: (b * nt + i, t)), full32,
                   pl.BlockSpec((8, TOK_WIDTH), lambda b, i, t: (0, 0))),
        scratch_shapes=[pltpu.VMEM((CONV_WIN, TOK_WIDTH), F32), pltpu.VMEM((SUBLANES - 1, SHIFT_ROWS, TOK_WIDTH), F32),
                        pltpu.VMEM((CONV_WIN, TOK_WIDTH), F32), pltpu.VMEM((SUBLANES - 1, SHIFT_ROWS, TOK_WIDTH), F32),
                        pltpu.VMEM((CONV_ROWS, TOK_WIDTH), BF16)],
        compiler_params=_params("arbitrary", "arbitrary", "arbitrary"), name="conv_bwd")(
            z, z, z, z, y, y, dcat, dcat, cw, lg, lb)


def _place():
    return lax.axis_index("x"), lax.axis_index("y"), lax.axis_index("c")


def _other_chips(x, y):
    return [(1 - x, y), (x, 1 - y), (1 - x, 1 - y)]


def reduce_small(arrays):
    na = len(arrays)

    def body(*refs):
        ins, outs, bufs = refs[:na], refs[na:2 * na], refs[2 * na:3 * na]
        send_sems, recv_sems = refs[3 * na:]
        x, y, c = _place()
        me = 4 * x + 2 * y + c
        copies = []
        for a in range(na):
            bufs[a][me] = ins[a][...]
            for k in range(1, N_DEV):
                cp = pltpu.make_async_remote_copy(
                    src_ref=ins[a], dst_ref=bufs[a].at[me], send_sem=send_sems.at[a, k - 1],
                    recv_sem=recv_sems.at[a, k - 1],
                    device_id=(x ^ (k >> 2), y ^ ((k >> 1) & 1), c ^ (k & 1)), device_id_type=MESH)
                cp.start()
                copies.append(cp)
        for a in range(na):
            for k in range(1, N_DEV):
                src = 4 * (x ^ (k >> 2)) + 2 * (y ^ ((k >> 1) & 1)) + (c ^ (k & 1))
                pltpu.make_async_remote_copy(
                    src_ref=ins[a], dst_ref=bufs[a].at[src], send_sem=send_sems.at[a, k - 1],
                    recv_sem=recv_sems.at[a, k - 1], device_id=(x, y, c), device_id_type=MESH).wait_recv()
        for cp in copies:
            cp.wait_send()
        for a in range(na):
            total = bufs[a][0]
            for dev in range(1, N_DEV):
                total = total + bufs[a][dev]
            outs[a][...] = total

    vmem = pl.BlockSpec(memory_space=pltpu.VMEM)
    return pl.pallas_call(
        body, out_shape=tuple(jax.ShapeDtypeStruct(a.shape, F32) for a in arrays),
        in_specs=[vmem] * na, out_specs=tuple([vmem] * na),
        scratch_shapes=[pltpu.VMEM((N_DEV,) + a.shape, F32) for a in arrays]
        + [pltpu.SemaphoreType.DMA((na, N_DEV - 1)), pltpu.SemaphoreType.DMA((na, N_DEV - 1))],
        compiler_params=pltpu.CompilerParams(vmem_limit_bytes=VMEM_LIMIT), name="small_reduce")(*arrays)


def adamw_small(ws, gs, ms, vs):
    na = len(ws)
    c1 = 1.0 / (1.0 - ADAM_B1 ** ADAM_STEP)
    c2 = 1.0 / (1.0 - ADAM_B2 ** ADAM_STEP)

    def body(*refs):
        w_refs, g_refs, m_refs, v_refs = (refs[i * na:(i + 1) * na] for i in range(4))
        d_refs, nm_refs, nv_refs = (refs[(4 + i) * na:(5 + i) * na] for i in range(3))
        for a in range(na):
            gv = g_refs[a][...]
            nm = ADAM_B1 * m_refs[a][...] + (1.0 - ADAM_B1) * gv
            nv = ADAM_B2 * v_refs[a][...] + (1.0 - ADAM_B2) * (gv * gv)
            nm_refs[a][...] = nm
            nv_refs[a][...] = nv
            d_refs[a][...] = -ADAM_LR * ((nm * c1) / (jnp.sqrt(nv * c2) + ADAM_EPS) + ADAM_WD * w_refs[a][...])

    vmem = pl.BlockSpec(memory_space=pltpu.VMEM)
    shapes = tuple(jax.ShapeDtypeStruct(w.shape, F32) for w in ws)
    outs = pl.pallas_call(
        body, out_shape=shapes * 3, in_specs=[vmem] * (4 * na), out_specs=tuple([vmem] * (3 * na)),
        compiler_params=pltpu.CompilerParams(vmem_limit_bytes=VMEM_LIMIT), name="adamw_small")(*ws, *gs, *ms, *vs)
    return outs[:na], outs[na:2 * na], outs[2 * na:]


def gather_weights(shards, name, collective_id):
    nw = len(shards)
    ns = [s.shape[0] for s in shards]
    in_refs = [jax.new_ref(s, memory_space=pltpu.MemorySpace.HBM) for s in shards]
    out_refs = [jax.empty_ref(jax.ShapeDtypeStruct((N_DEV * s.shape[0], s.shape[1]), s.dtype),
                              memory_space=pltpu.MemorySpace.HBM) for s in shards]

    @pl.kernel(mesh=plsc.ScalarSubcoreMesh(axis_name="seq", num_cores=1), name=name,
               scratch_types=(pltpu.SemaphoreType.DMA((nw, 7)), pltpu.SemaphoreType.DMA((nw, 7)),
                              pltpu.SemaphoreType.DMA((nw,))),
               compiler_params=pltpu.CompilerParams(collective_id=collective_id))
    def launch(send_sems, recv_sems, local_sems):
        x, y, c = _place()
        me, sib = (x, y, c), (x, y, 1 - c)
        chips = _other_chips(x, y)
        barrier = pltpu.get_barrier_semaphore()
        for peer in [sib] + [(*chip, c) for chip in chips]:
            pl.semaphore_signal(barrier, inc=1, device_id=peer, device_id_type=MESH)
        pl.semaphore_wait(barrier, 4)

        def rows(w, dev):
            return out_refs[w].at[pl.ds((4 * dev[0] + 2 * dev[1] + dev[2]) * ns[w], ns[w]), :]

        def copy(w, k, block, to, src=None):
            return pltpu.make_async_remote_copy(
                src_ref=rows(w, block) if src is None else src, dst_ref=rows(w, block),
                send_sem=send_sems.at[w, k], recv_sem=recv_sems.at[w, k], device_id=to, device_id_type=MESH)

        started, sends = [], []
        for w in range(nw):
            mine = pltpu.make_async_copy(in_refs[w], rows(w, me), local_sems.at[w])
            mine.start()
            started.append(mine)
            first = [copy(w, 0, me, sib, src=in_refs[w])]
            first += [copy(w, 1 + j, me, (*chip, c), src=in_refs[w]) for j, chip in enumerate(chips)]
            for cp in first:
                cp.start()
            sends += first
        for w in range(nw):
            for j, chip in enumerate(chips):
                copy(w, 1 + j, (*chip, c), me).wait_recv()
                fwd = copy(w, 4 + j, (*chip, c), sib)
                fwd.start()
                sends.append(fwd)
        for w in range(nw):
            copy(w, 0, sib, me).wait_recv()
            for j, chip in enumerate(chips):
                copy(w, 4 + j, (*chip, 1 - c), me).wait_recv()
        for cp in sends:
            cp.wait_send()
        for mine in started:
            mine.wait()

    launch()
    return [r[...] for r in out_refs]


def _sequencer_exchange(sources, out_rows, peers_of, copies_of, name, collective_id):
    nw = len(sources)
    in_refs = [jax.new_ref(s, memory_space=pltpu.MemorySpace.HBM) for s in sources]
    out_refs = [jax.empty_ref(jax.ShapeDtypeStruct((rows, s.shape[1]), s.dtype), memory_space=pltpu.MemorySpace.HBM)
                for rows, s in zip(out_rows, sources)]
    per = len(copies_of(0, 0, 0, 0))

    @pl.kernel(mesh=plsc.ScalarSubcoreMesh(axis_name="seq", num_cores=1), name=name,
               scratch_types=(pltpu.SemaphoreType.DMA((nw, per)), pltpu.SemaphoreType.DMA((nw, per))),
               compiler_params=pltpu.CompilerParams(collective_id=collective_id))
    def launch(send_sems, recv_sems):
        x, y, c = _place()
        peers = peers_of(x, y, c)
        barrier = pltpu.get_barrier_semaphore()
        for peer in peers:
            pl.semaphore_signal(barrier, inc=1, device_id=peer, device_id_type=MESH)
        pl.semaphore_wait(barrier, len(peers))
        copies = []
        for w in range(nw):
            for k, (src_blk, dst_blk, rows, peer) in enumerate(copies_of(x, y, c, w)):
                cp = pltpu.make_async_remote_copy(
                    src_ref=in_refs[w].at[pl.ds(src_blk * rows, rows), :],
                    dst_ref=out_refs[w].at[pl.ds(dst_blk * rows, rows), :],
                    send_sem=send_sems.at[w, k], recv_sem=recv_sems.at[w, k], device_id=peer, device_id_type=MESH)
                cp.start()
                copies.append(cp)
        for cp in copies:
            cp.wait_recv()
        for cp in copies:
            cp.wait_send()

    launch()
    return [r[...] for r in out_refs]


def scatter_to_sibling(grads, name, collective_id):
    ns = [g.shape[0] // N_DEV for g in grads]
    return _sequencer_exchange(
        grads, [4 * n for n in ns],
        lambda x, y, c: [(x, y, 1 - c)],
        lambda x, y, c, w: [(2 * q + 1 - c, q, ns[w], (x, y, 1 - c)) for q in range(4)],
        name, collective_id)


def scatter_to_chips(parts, name, collective_id):
    ns = [p.shape[0] // 4 for p in parts]
    return _sequencer_exchange(
        parts, [3 * n for n in ns],
        lambda x, y, c: [(*chip, c) for chip in _other_chips(x, y)],
        lambda x, y, c, w: [(2 * chip[0] + chip[1], j, ns[w], (*chip, c)) for j, chip in enumerate(_other_chips(x, y))],
        name, collective_id)


def add_sibling(grads, landeds, core, name):
    nw = len(grads)

    def body(c_ref, *refs):
        for w in range(nw):
            g_ref, l_ref, o_ref = refs[2 * w], refs[2 * w + 1], refs[2 * nw + w]
            o_ref[...] = (g_ref[...].astype(F32) + l_ref[...].astype(F32)).astype(o_ref.dtype)

    in_specs, out_specs, args = [], [], []
    for g, ld in zip(grads, landeds):
        n, cols = ld.shape[0] // 4, g.shape[1]
        in_specs += [pl.BlockSpec((n, cols), lambda q, c_ref: (2 * q + c_ref[0], 0)),
                     pl.BlockSpec((n, cols), lambda q, c_ref: (q, 0))]
        out_specs.append(pl.BlockSpec((n, cols), lambda q, c_ref: (q, 0)))
        args += [g, ld]
    grid_spec = pltpu.PrefetchScalarGridSpec(
        num_scalar_prefetch=1, grid=(4,), in_specs=in_specs, out_specs=tuple(out_specs))
    return pl.pallas_call(
        body, out_shape=tuple(jax.ShapeDtypeStruct(ld.shape, ld.dtype) for ld in landeds), grid_spec=grid_spec,
        compiler_params=_params("arbitrary"), name=name)(core, *args)


def adamw_shard(layer, w, m, v, part, landed, chip, earlier, name):
    n = landed.shape[0] // 3
    cols = w.shape[1]
    c1 = 1.0 / (1.0 - ADAM_B1 ** ADAM_STEP)
    c2 = 1.0 / (1.0 - ADAM_B2 ** ADAM_STEP)

    def body(q_ref, w_ref, m_ref, v_ref, p_ref, l0_ref, l1_ref, l2_ref, *rest):
        g_ref, d_ref, nm_ref, nv_ref = rest[-4:]
        gv = ((p_ref[...].astype(F32) + l0_ref[...].astype(F32)) + l1_ref[...].astype(F32)) + l2_ref[...].astype(F32)
        nm = ADAM_B1 * m_ref[...] + (1.0 - ADAM_B1) * gv
        nv = ADAM_B2 * v_ref[...] + (1.0 - ADAM_B2) * (gv * gv)
        g_ref[...] = gv
        nm_ref[...] = nm
        nv_ref[...] = nv
        d_ref[...] = -ADAM_LR * ((nm * c1) / (jnp.sqrt(nv * c2) + ADAM_EPS) + ADAM_WD * w_ref[...])

    sub = 2 if n % (2 * 16) == 0 else 1
    rows = n // sub
    own = pl.BlockSpec((rows, cols), lambda i, q_ref: (layer * sub + i, 0))

    def landed_spec(j):
        return pl.BlockSpec((rows, cols), lambda i, q_ref: (j * sub + i, 0))

    in_specs = [own, own, own, pl.BlockSpec((rows, cols), lambda i, q_ref: (q_ref[0] * sub + i, 0)),
                landed_spec(0), landed_spec(1), landed_spec(2)]
    args = [chip, w, m, v, part, landed, landed, landed]
    aliases = {}
    if earlier is not None:
        in_specs += [ANY] * 4
        args += list(earlier)
        aliases = {8 + k: k for k in range(4)}
    grid_spec = pltpu.PrefetchScalarGridSpec(
        num_scalar_prefetch=1, grid=(sub,), in_specs=in_specs, out_specs=(own, own, own, own))
    shape = jax.ShapeDtypeStruct(w.shape, F32)
    return pl.pallas_call(
        body, out_shape=(shape, shape, shape, shape), grid_spec=grid_spec, input_output_aliases=aliases,
        compiler_params=_params("arbitrary"), name=name)(*args)


def _pack(arrays):
    flat = jnp.concatenate([a.reshape(-1).astype(F32) for a in arrays])
    pad = (-flat.shape[0]) % (8 * LANES)
    return jnp.pad(flat, (0, pad)).reshape(-1, LANES)


def _unpack(slab, shapes):
    flat = slab.reshape(slab.shape[:-2] + (-1,))
    out, off = [], 0
    for shp in shapes:
        size = 1
        for s in shp:
            size *= s
        out.append(flat[..., off:off + size].reshape(flat.shape[:-1] + tuple(shp)))
        off += size
    return out


def kernel(x, mem, norm1_g, mem_norm_g, a_w_in, a_q_g, a_k_g, a_rel_bias, b_w_in, b_b_in, b_conv_w, b_conv_b, b_ln_g, b_ln_b, mq_g, mk_g, w_mem_kv, w_out, norm2_g, w_gate, w_up, w_down, loss_target, m_norm1_g, m_mem_norm_g, m_a_w_in, m_a_q_g, m_a_k_g, m_a_rel_bias, m_b_w_in, m_b_b_in, m_b_conv_w, m_b_conv_b, m_b_ln_g, m_b_ln_b, m_mq_g, m_mk_g, m_w_mem_kv, m_w_out, m_norm2_g, m_w_gate, m_w_up, m_w_down, v_norm1_g, v_mem_norm_g, v_a_w_in, v_a_q_g, v_a_k_g, v_a_rel_bias, v_b_w_in, v_b_b_in, v_b_conv_w, v_b_conv_b, v_b_ln_g, v_b_ln_b, v_mq_g, v_mk_g, v_w_mem_kv, v_w_out, v_norm2_g, v_w_gate, v_w_up, v_w_down):
    batch, seq, d = x.shape
    mtok = mem.shape[1]
    n = batch * seq
    ax, ay, ac = _place()
    me = 4 * ax + 2 * ay + ac
    core_arr = jnp.reshape(ac, (1,)).astype(jnp.int32)
    chip_arr = jnp.reshape(2 * ax + ay, (1,)).astype(jnp.int32)

    def t_bf16(w):
        return jnp.transpose(w).astype(BF16)

    def after(value, *earlier):
        return lax.optimization_barrier((value, *earlier))[0]

    def gather_mix(l, when, name, collective_id):
        srcs = [w_mem_kv[l].astype(BF16), w_out[l].astype(BF16)]
        if l == 1:
            srcs += [t_bf16(b_w_in[0]), _pack([b_b_in, b_conv_w, b_conv_b, b_ln_g, b_ln_b])]
        return gather_weights([after(srcs[0], *when)] + srcs[1:], name, collective_id)

    def gather_ffn(l, when, name, collective_id):
        return gather_weights(
            [after(t_bf16(w_gate[l]), *when), t_bf16(w_up[l]), w_down[l].astype(BF16)], name, collective_id)

    f_loc = b_b_in.shape[1]
    c_loc = b_conv_b.shape[1]

    def two(g):
        return jnp.concatenate([g, g], axis=-1)

    gq2, gk2 = two(a_q_g), two(a_k_g)
    rel16 = jnp.pad(a_rel_bias[0], ((0, 16 - a_rel_bias.shape[1]), (0, 0)))
    bias = bias_blocks(rel16)

    x0 = x.reshape(n, d)
    mem2 = mem.reshape(batch * mtok, d)

    saved = []
    xin = x0
    a_win_t, = gather_weights([t_bf16(a_w_in[0])], "gather_in_a", 1)
    wg_t, wu_t, wd, wo, wkv = [None] * 2, [None] * 2, [None] * 2, [None] * 2, [None] * 2
    h = after(rms_fwd(xin, norm1_g[0:1], name="rms1_fwd_0"), bias)
    target = loss_target.reshape(n, d)
    for l in range(2):
        gq4 = jnp.tile(mq_g[l:l + 1], (1, 4))
        gk4 = jnp.tile(mk_g[l:l + 1], (1, 4))
        y_conv = None
        if l == 0:
            wkv[0], wo[0] = gather_mix(0, (h, a_win_t), "gather_mix_a", 2)
            z = mm_nt(h, a_win_t, name="in_proj_a")
            wg_t[0], wu_t[0], wd[0] = gather_ffn(0, (z, wkv[0]), "gather_ffn_a", 3)
            cat = attn_fwd(z, gq2, gk2, bias, batch, seq)
            wkv[1], wo[1], b_win_t, conv_slabs = gather_mix(1, (cat, wg_t[0]), "gather_mix_b", 4)
            qcol = 3 * TOK_WIDTH // MEM_WIDTH
        else:
            small_shapes = [(f_loc,), (CONV_W, c_loc), (c_loc,), (c_loc,), (c_loc,)]
            bb_g, cw_g, cb_g, lg_g, lb_g = _unpack(conv_slabs.reshape(N_DEV, -1, LANES), small_shapes)
            bb_full = bb_g.reshape(1, -1)
            cw_full = jnp.pad(jnp.transpose(cw_g, (1, 0, 2)).reshape(CONV_W, -1), ((0, 32 - CONV_W), (0, 0)))
            cb_full, lg_full, lb_full = cb_g.reshape(1, -1), lg_g.reshape(1, -1), lb_g.reshape(1, -1)
            z = mm_nt(h, b_win_t, bias=bb_full, name="in_proj_b")
            cat, y_conv = conv_fwd(z, cw_full, cb_full, lg_full, lb_full, batch, seq)
            qcol = 2 * TOK_WIDTH // MEM_WIDTH
        mem_n, kv = mem_prep(mem2, mem_norm_g[l:l + 1], wkv[l], name=f"mem_prep_{l}")
        cat = memattn_fwd(z, kv, gq4, gk4, cat, batch, seq, qcol, name=f"memattn_fwd_{l}")
        x1, h2 = proj_norm(cat, wo[l], xin, norm2_g[l:l + 1], name=f"out_proj_{l}")
        if l == 0:
            wg_t[1], wu_t[1], wd[1] = gather_ffn(1, (x1, b_win_t), "gather_ffn_b", 5)
        if l == 0:
            gate, up, act, x2, h_next = ffn_fwd(h2, wg_t[0], wu_t[0], wd[0], x1, gain=norm1_g[1:2], name="ffn_fwd_0")
        else:
            gate, up, act, dx_b, loss_blk = ffn_fwd(h2, wg_t[1], wu_t[1], wd[1], x1, target=target, name="ffn_fwd_1")
        saved.append(dict(xin=xin, h=h, mem_n=mem_n, kv=kv, gq4=gq4, gk4=gk4, z=z, qcol=qcol, cat=cat, x1=x1, h2=h2,
                          gate=gate, up=up, act=act, y_conv=y_conv))
        if l == 0:
            xin, h = x2, h_next

    big = {}
    small = {}
    reduced = {}
    groups = 0

    def scatter_siblings(keys):
        nonlocal groups
        gid = groups
        groups += 1
        return gid, keys, scatter_to_sibling([big[k] for k in keys], f"scatter_sibling_{gid}", 8 + 2 * gid)

    def scatter_chips(stage1, when):
        gid, keys, landed1 = stage1
        parts = add_sibling([after(big[keys[0]], when)] + [big[k] for k in keys[1:]], landed1, core_arr,
                            name=f"add_sibling_{gid}")
        landed2 = scatter_to_chips(parts, f"scatter_chips_{gid}", 9 + 2 * gid)
        for k, p, ld in zip(keys, parts, landed2):
            reduced[k] = (p, ld)
        return parts, landed2

    def rows_of(w, transposed):
        w = jnp.swapaxes(w, 1, 2) if transposed else w
        return w.reshape(w.shape[0] * w.shape[1], w.shape[2])

    sharded = {
        "win0": (2, True), "win1": (6, True), "wkv": (14, False), "wo": (15, False),
        "wg": (17, True), "wu": (18, True), "wd": (19, False)}
    weights = [norm1_g, mem_norm_g, a_w_in, a_q_g, a_k_g, a_rel_bias, b_w_in, b_b_in, b_conv_w, b_conv_b, b_ln_g,
               b_ln_b, mq_g, mk_g, w_mem_kv, w_out, norm2_g, w_gate, w_up, w_down]
    moms = [m_norm1_g, m_mem_norm_g, m_a_w_in, m_a_q_g, m_a_k_g, m_a_rel_bias, m_b_w_in, m_b_b_in, m_b_conv_w,
            m_b_conv_b, m_b_ln_g, m_b_ln_b, m_mq_g, m_mk_g, m_w_mem_kv, m_w_out, m_norm2_g, m_w_gate, m_w_up, m_w_down]
    vels = [v_norm1_g, v_mem_norm_g, v_a_w_in, v_a_q_g, v_a_k_g, v_a_rel_bias, v_b_w_in, v_b_b_in, v_b_conv_w,
            v_b_conv_b, v_b_ln_g, v_b_ln_b, v_mq_g, v_mk_g, v_w_mem_kv, v_w_out, v_norm2_g, v_w_gate, v_w_up, v_w_down]
    updated = {}

    def update_layer(l, when):
        for key, (idx, transposed) in sharded.items():
            if key in ("win0", "win1"):
                if key != f"win{l}":
                    continue
                layer, rkey = 0, key
            else:
                layer, rkey = l, f"{key}{l}"
            part, landed = reduced[rkey]
            updated[key] = adamw_shard(
                layer, after(rows_of(weights[idx], transposed), when), rows_of(moms[idx], transposed),
                rows_of(vels[idx], transposed), part, landed, chip_arr, updated.get(key), name=f"adamw_{rkey}")

    mix_landed = None
    for l in (1, 0):
        sv = saved[l]
        dgate, dup, dx1_b, dcat, small[f"norm2_{l}"] = ffn_bwd(
            dx_b, wd[l], sv["gate"], sv["up"], wg_t[l], wu_t[l], sv["x1"], norm2_g[l:l + 1], wo[l], name=f"ffn_bwd_{l}")
        if l == 0:
            dgate = after(dgate, *mix_landed)
            update_layer(1, dx1_b)
        big[f"wg{l}"], big[f"wu{l}"], big[f"wd{l}"] = ffn_weight_grads(
            dgate, dup, sv["h2"], sv["act"], dx_b, name=f"grad_ffn_{l}")
        stage1 = scatter_siblings([f"wd{l}", f"wg{l}", f"wu{l}"])
        big[f"wo{l}"] = mm_tn(sv["cat"], dx1_b, name=f"grad_wo_{l}")
        parts, ffn_landed = scatter_chips(stage1, big[f"wo{l}"])
        dcat = after(dcat, *parts)
        if l == 0:
            dz, dbias, small["a_q"], small["a_k"] = attn_bwd(sv["z"], dcat, gq2, gk2, bias, batch, seq)
            small["rel"] = bias_grad(dbias)
            win_t = a_win_t
        else:
            dz, small["cw"], small["csum"] = conv_bwd(sv["z"], sv["y_conv"], dcat, cw_full, lg_full, lb_full, batch, seq)
            win_t = b_win_t
        dz = after(dz, *ffn_landed)
        dz, dkv, small[f"mq_{l}"], small[f"mk_{l}"] = memattn_bwd(
            sv["z"], sv["kv"], dcat, sv["gq4"], sv["gk4"], dz, batch, seq, sv["qcol"], name=f"memattn_bwd_{l}")
        big[f"win{l}"] = mm_tn(dz, sv["h"], name=f"grad_win_{l}")
        big[f"wkv{l}"] = mm_tn(sv["mem_n"], dkv, name=f"grad_wkv_{l}")
        stage1 = scatter_siblings([f"win{l}", f"wkv{l}", f"wo{l}"])
        dx_b, small[f"norm1_{l}"], dz_sum = in_proj_bwd(
            dz, win_t, sv["xin"], norm1_g[l:l + 1], dx1_b, BF16 if l == 1 else F32, name=f"in_proj_bwd_{l}")
        if l == 1:
            small["bb"] = dz_sum
        parts, mix_landed = scatter_chips(stage1, dx_b)
        dx_b = after(dx_b, *parts)
        small[f"memnorm_{l}"] = mem_norm_grad(dkv, wkv[l], mem2, name=f"mem_norm_grad_{l}")
    grad_x = dx_b.reshape(batch, seq, d)
    update_layer(0, dx_b)

    def shaped(rows, idx, transposed):
        shp = weights[idx].shape
        if transposed:
            return jnp.swapaxes(rows.reshape(shp[0], shp[2], shp[1]), 1, 2)
        return rows.reshape(shp)

    def fold(v, groups):
        return jnp.sum(v.reshape(groups, HEAD_DIM), axis=0, keepdims=True)

    heads = a_rel_bias.shape[1]
    small_list = [
        jnp.concatenate([small["norm1_0"], small["norm1_1"]]),
        jnp.concatenate([small["memnorm_0"], small["memnorm_1"]]),
        fold(small["a_q"], 2), fold(small["a_k"], 2), small["rel"][:heads][None],
        small["bb"], small["cw"][:CONV_W][None], small["csum"][0:1], small["csum"][1:2], small["csum"][2:3],
        jnp.concatenate([fold(small["mq_0"], 4), fold(small["mq_1"], 4)]),
        jnp.concatenate([fold(small["mk_0"], 4), fold(small["mk_1"], 4)]),
        jnp.concatenate([small["norm2_0"], small["norm2_1"]]),
    ]
    (g_norm1, g_memnorm, g_aq, g_ak, g_rel, g_bb_full, g_cw_full, g_cb_full, g_lg_full, g_lb_full,
     g_mq, g_mk, g_norm2, loss_sum) = reduce_small(small_list + [loss_blk])
    loss = loss_sum[0, 0]
    g_bb = lax.dynamic_slice_in_dim(g_bb_full, me * f_loc, f_loc, axis=1)
    g_cw = lax.dynamic_slice_in_dim(g_cw_full, me * c_loc, c_loc, axis=2)
    g_cb = lax.dynamic_slice_in_dim(g_cb_full, me * c_loc, c_loc, axis=1)
    g_lg = lax.dynamic_slice_in_dim(g_lg_full, me * c_loc, c_loc, axis=1)
    g_lb = lax.dynamic_slice_in_dim(g_lb_full, me * c_loc, c_loc, axis=1)

    grads = [g_norm1, g_memnorm, None, g_aq, g_ak, g_rel, None, g_bb, g_cw, g_cb, g_lg, g_lb,
             g_mq, g_mk, None, None, g_norm2, None, None, None]
    deltas, new_m, new_v = [None] * 20, [None] * 20, [None] * 20
    for key, (idx, transposed) in sharded.items():
        grads[idx], deltas[idx], new_m[idx], new_v[idx] = (shaped(r, idx, transposed) for r in updated[key])

    small_idx = [i for i in range(20) if i not in {idx for idx, _ in sharded.values()}]
    dl, nm, nv = adamw_small([weights[i] for i in small_idx], [grads[i] for i in small_idx],
                             [moms[i] for i in small_idx], [vels[i] for i in small_idx])
    for i, a, b, cc in zip(small_idx, dl, nm, nv):
        deltas[i], new_m[i], new_v[i] = a, b, cc

    return (loss, grad_x, *grads, *deltas, *new_m, *new_v)
```

```python
import jax
import jax.numpy as jnp
from jax import lax
from jax.experimental import pallas as pl
from jax.experimental.pallas import tpu as pltpu
from jax.experimental.pallas import tpu_sc as plsc

F32 = jnp.float32
BF16 = jnp.bfloat16
HIGHEST = lax.Precision.HIGHEST
MESH = pl.DeviceIdType.MESH
ANY = pl.BlockSpec(memory_space=pl.ANY)

N_DEV = 8
D_MODEL = 1024
HEAD_DIM = 64
TOK_WIDTH = 768
MEM_WIDTH = 256
CHUNK = 64
Q_BLOCK = 256
KEY_WIN = 768
BAND = 576
N_REL = 192
CONV_W = 31
CONV_HALO = 32
NORM_EPS = 1e-6
NEG_INF = -1e30
ATTN_SCALE = HEAD_DIM ** -0.5
LANES = 128
ROW_TILE = 512
VMEM_LIMIT = 56 * 1024 * 1024

ADAM_LR, ADAM_B1, ADAM_B2, ADAM_EPS, ADAM_WD, ADAM_STEP = 0.001, 0.9, 0.999, 1e-08, 0.01, 10


def _params(*sem):
    return pltpu.CompilerParams(dimension_semantics=sem, vmem_limit_bytes=VMEM_LIMIT)


WIDE_ROW_TILE = 1024


def _row_tile(m, rows=ROW_TILE):
    return rows if m % rows == 0 else m


def _col_tile(n, cap=1408):
    best = None
    for t in range(LANES, min(n, cap) + 1, LANES):
        if n % t == 0:
            best = t
    return best if best is not None else n


def _dot(a, b, ca, cb):
    return lax.dot_general(a, b, (((ca,), (cb,)), ((), ())), preferred_element_type=F32)


def _sigmoid(x):
    return 0.5 * jnp.tanh(0.5 * x) + 0.5


def mm_nt(a, b, bias=None, out_dtype=BF16, name="mm_nt"):
    m, k = a.shape
    n = b.shape[0]
    tm, tn = _row_tile(m, WIDE_ROW_TILE), _col_tile(n)

    def body(*refs):
        a_ref, b_ref = refs[0], refs[1]
        o_ref = refs[-1]
        acc = _dot(a_ref[...].astype(BF16), b_ref[...].astype(BF16), 1, 1)
        if bias is not None:
            acc = acc + refs[2][...]
        o_ref[...] = acc.astype(o_ref.dtype)

    in_specs = [pl.BlockSpec((tm, k), lambda j, i: (i, 0)), pl.BlockSpec((tn, k), lambda j, i: (j, 0))]
    args = [a, b]
    if bias is not None:
        in_specs.append(pl.BlockSpec((1, tn), lambda j, i: (0, j)))
        args.append(bias)
    return pl.pallas_call(
        body, out_shape=jax.ShapeDtypeStruct((m, n), out_dtype), grid=(n // tn, m // tm),
        in_specs=in_specs, out_specs=pl.BlockSpec((tm, tn), lambda j, i: (i, j)),
        compiler_params=_params("parallel", "arbitrary"), name=name)(*args)


def mm_tn(a, b, out_dtype=BF16, name="mm_tn"):
    t, r = a.shape
    c = b.shape[1]
    tr = _col_tile(r, 512)

    def body(a_ref, b_ref, o_ref):
        o_ref[...] = _dot(a_ref[...].astype(BF16), b_ref[...].astype(BF16), 0, 0).astype(o_ref.dtype)

    return pl.pallas_call(
        body, out_shape=jax.ShapeDtypeStruct((r, c), out_dtype), grid=(r // tr,),
        in_specs=[pl.BlockSpec((t, tr), lambda i: (0, i)), pl.BlockSpec((t, c), lambda i: (0, 0))],
        out_specs=pl.BlockSpec((tr, c), lambda i: (i, 0)),
        compiler_params=_params("parallel"), name=name)(a, b)


def _resident(shape):
    return pl.BlockSpec(shape, lambda i: (0, 0), pipeline_mode=pl.Buffered(1))


def proj_norm(a, b, res, gain, name):
    m, k = a.shape
    n = b.shape[1]
    tm = _row_tile(m)

    def body(a_ref, b_ref, res_ref, g_ref, x_ref, h_ref):
        xv = res_ref[...] + _dot(a_ref[...], b_ref[...], 1, 0)
        x_ref[...] = xv
        r = lax.rsqrt(jnp.mean(xv * xv, axis=-1, keepdims=True) + NORM_EPS)
        h_ref[...] = (xv * r * g_ref[...]).astype(BF16)

    row = pl.BlockSpec((tm, n), lambda i: (i, 0))
    return pl.pallas_call(
        body, out_shape=(jax.ShapeDtypeStruct((m, n), F32), jax.ShapeDtypeStruct((m, n), BF16)), grid=(m // tm,),
        in_specs=[pl.BlockSpec((tm, k), lambda i: (i, 0)), _resident((k, n)), row, _resident((1, n))],
        out_specs=(row, row), compiler_params=_params("parallel"), name=name)(a, b, res, gain)


def in_proj_bwd(dz, w_t, x, gain, dres, out_dtype, name):
    m, n = x.shape
    k = dz.shape[1]
    tm = _row_tile(m)

    def body(dz_ref, w_ref, x_ref, g_ref, dres_ref, dx_ref, dg_ref, cs_ref):
        @pl.when(pl.program_id(0) == 0)
        def _():
            dg_ref[...] = jnp.zeros_like(dg_ref)
            cs_ref[...] = jnp.zeros_like(cs_ref)

        dzv = dz_ref[...]
        cs_ref[...] += jnp.sum(dzv.astype(F32), axis=0, keepdims=True)
        dhv = _dot(dzv, w_ref[...], 1, 0)
        xv = x_ref[...]
        r = lax.rsqrt(jnp.mean(xv * xv, axis=-1, keepdims=True) + NORM_EPS)
        xhat = xv * r
        dg_ref[...] += jnp.sum(dhv * xhat, axis=0, keepdims=True)
        dxhat = dhv * g_ref[...]
        dx = dres_ref[...].astype(F32) + r * (dxhat - xhat * jnp.mean(dxhat * xhat, axis=-1, keepdims=True))
        dx_ref[...] = dx.astype(dx_ref.dtype)

    row = pl.BlockSpec((tm, n), lambda i: (i, 0))
    return pl.pallas_call(
        body, out_shape=(jax.ShapeDtypeStruct((m, n), out_dtype), jax.ShapeDtypeStruct((1, n), F32),
                         jax.ShapeDtypeStruct((1, k), F32)), grid=(m // tm,),
        in_specs=[pl.BlockSpec((tm, k), lambda i: (i, 0)), _resident(w_t.shape), row, _resident((1, n)), row],
        out_specs=(row, pl.BlockSpec((1, n), lambda i: (0, 0)), pl.BlockSpec((1, k), lambda i: (0, 0))),
        compiler_params=_params("arbitrary"), name=name)(dz, w_t, x, gain, dres)


FFN_ROWS = 256


def _ffn_row_tile(m):
    return FFN_ROWS if m % FFN_ROWS == 0 else m


def ffn_fwd(h2, wg_t, wu_t, wd, x1, gain=None, target=None, name="ffn_fwd"):
    n, d = h2.shape
    f = wg_t.shape[0]
    tm = _ffn_row_tile(n)
    nt = n // tm
    last = target is not None

    def body(h_ref, wg_ref, wu_ref, wd_ref, x1_ref, e_ref, g_ref, u_ref, a_ref, *rest):
        hv = h_ref[...]
        gv = _dot(hv, wg_ref[...], 1, 1)
        uv = _dot(hv, wu_ref[...], 1, 1)
        g_ref[...] = gv.astype(BF16)
        u_ref[...] = uv.astype(BF16)
        av = (gv * _sigmoid(gv) * uv).astype(BF16)
        a_ref[...] = av
        xv = x1_ref[...] + _dot(av, wd_ref[...], 1, 0)
        if not last:
            x_ref, hn_ref = rest
            x_ref[...] = xv
            r = lax.rsqrt(jnp.mean(xv * xv, axis=-1, keepdims=True) + NORM_EPS)
            hn_ref[...] = (xv * r * e_ref[...]).astype(BF16)
        else:
            dyb_ref, l_ref, acc_ref = rest
            i = pl.program_id(0)

            @pl.when(i == 0)
            def _():
                acc_ref[...] = jnp.zeros_like(acc_ref)

            err = xv - e_ref[...]
            dyb_ref[...] = (err * (1.0 / d)).astype(BF16)
            acc_ref[...] += jnp.sum(err * err, axis=0, keepdims=True)

            @pl.when(i == nt - 1)
            def _():
                total = jnp.sum(acc_ref[...], axis=-1, keepdims=True) * (0.5 / d)
                l_ref[...] = jnp.broadcast_to(total, l_ref.shape)

    row_d = pl.BlockSpec((tm, d), lambda i: (i, 0))
    row_f = pl.BlockSpec((tm, f), lambda i: (i, 0))
    act_shape = jax.ShapeDtypeStruct((n, f), BF16)
    if not last:
        extra_in, extra = _resident((1, d)), gain
        out_shape = (act_shape, act_shape, act_shape, jax.ShapeDtypeStruct((n, d), F32), jax.ShapeDtypeStruct((n, d), BF16))
        out_specs = (row_f, row_f, row_f, row_d, row_d)
        scratch = []
    else:
        extra_in, extra = row_d, target
        out_shape = (act_shape, act_shape, act_shape, jax.ShapeDtypeStruct((n, d), BF16),
                     jax.ShapeDtypeStruct((8, LANES), F32))
        out_specs = (row_f, row_f, row_f, row_d, pl.BlockSpec((8, LANES), lambda i: (0, 0)))
        scratch = [pltpu.VMEM((1, d), F32)]
    return pl.pallas_call(
        body, out_shape=out_shape, grid=(nt,),
        in_specs=[row_d, _resident((f, d)), _resident((f, d)), _resident((f, d)), row_d, extra_in],
        out_specs=out_specs, scratch_shapes=scratch,
        compiler_params=_params("arbitrary"), name=name)(h2, wg_t, wu_t, wd, x1, extra)


def ffn_bwd(dx_b, wd, gate, up, wg_t, wu_t, x1, gain, wo, name="ffn_bwd"):
    n, d = x1.shape
    f = wd.shape[0]
    tm = _ffn_row_tile(n)

    def body(dxb_ref, wd_ref, g_ref, u_ref, wg_ref, wu_ref, x_ref, gain_ref, wo_ref,
             dg_ref, du_ref, dxo_ref, dc_ref, dgain_ref):
        @pl.when(pl.program_id(0) == 0)
        def _():
            dgain_ref[...] = jnp.zeros_like(dgain_ref)

        dact = _dot(dxb_ref[...], wd_ref[...], 1, 1)
        gv = g_ref[...].astype(F32)
        uv = u_ref[...].astype(F32)
        sg = _sigmoid(gv)
        dgv = (dact * uv * sg * (1.0 + gv * (1.0 - sg))).astype(BF16)
        duv = (dact * gv * sg).astype(BF16)
        dg_ref[...] = dgv
        du_ref[...] = duv
        dhv = _dot(dgv, wg_ref[...], 1, 0) + _dot(duv, wu_ref[...], 1, 0)
        xv = x_ref[...]
        r = lax.rsqrt(jnp.mean(xv * xv, axis=-1, keepdims=True) + NORM_EPS)
        xhat = xv * r
        dgain_ref[...] += jnp.sum(dhv * xhat, axis=0, keepdims=True)
        dxhat = dhv * gain_ref[...]
        dxb = (dxb_ref[...].astype(F32) + r * (dxhat - xhat * jnp.mean(dxhat * xhat, axis=-1, keepdims=True))).astype(BF16)
        dxo_ref[...] = dxb
        dc_ref[...] = _dot(dxb, wo_ref[...], 1, 1).astype(BF16)

    row_d = pl.BlockSpec((tm, d), lambda i: (i, 0))
    row_f = pl.BlockSpec((tm, f), lambda i: (i, 0))
    w_spec = _resident((f, d))
    act_shape = jax.ShapeDtypeStruct((n, f), BF16)
    row_shape = jax.ShapeDtypeStruct((n, d), BF16)
    return pl.pallas_call(
        body, out_shape=(act_shape, act_shape, row_shape, jax.ShapeDtypeStruct((n, wo.shape[0]), BF16),
                         jax.ShapeDtypeStruct((1, d), F32)),
        grid=(n // tm,),
        in_specs=[row_d, w_spec, row_f, row_f, w_spec, w_spec, row_d, _resident((1, d)), _resident(wo.shape)],
        out_specs=(row_f, row_f, row_d, pl.BlockSpec((tm, wo.shape[0]), lambda i: (i, 0)),
                   pl.BlockSpec((1, d), lambda i: (0, 0))),
        compiler_params=_params("arbitrary"), name=name)(dx_b, wd, gate, up, wg_t, wu_t, x1, gain, wo)


def ffn_weight_grads(dgate, dup, h2, act, dx_b, name="ffn_weight_grads"):
    t, r = dgate.shape
    c = h2.shape[1]
    tr = _col_tile(r, 512)

    def body(a1_ref, a2_ref, a3_ref, b12_ref, b3_ref, o1_ref, o2_ref, o3_ref):
        bv = b12_ref[...]
        o1_ref[...] = _dot(a1_ref[...], bv, 0, 0).astype(o1_ref.dtype)
        o2_ref[...] = _dot(a2_ref[...], bv, 0, 0).astype(o2_ref.dtype)
        o3_ref[...] = _dot(a3_ref[...], b3_ref[...], 0, 0).astype(o3_ref.dtype)

    a_spec = pl.BlockSpec((t, tr), lambda i: (0, i))
    o_spec = pl.BlockSpec((tr, c), lambda i: (i, 0))
    shape = jax.ShapeDtypeStruct((r, c), BF16)
    return pl.pallas_call(
        body, out_shape=(shape, shape, shape), grid=(r // tr,),
        in_specs=[a_spec, a_spec, a_spec, _resident((t, c)), _resident((t, c))],
        out_specs=(o_spec, o_spec, o_spec), compiler_params=_params("parallel"), name=name)(dgate, dup, act, h2, dx_b)


def rms_fwd(x, g, name="rms_fwd"):
    n, d = x.shape
    tm = _row_tile(n)

    def body(x_ref, g_ref, o_ref):
        xv = x_ref[...]
        r = lax.rsqrt(jnp.mean(xv * xv, axis=-1, keepdims=True) + NORM_EPS)
        o_ref[...] = (xv * r * g_ref[...]).astype(o_ref.dtype)

    return pl.pallas_call(
        body, out_shape=jax.ShapeDtypeStruct((n, d), BF16), grid=(n // tm,),
        in_specs=[pl.BlockSpec((tm, d), lambda i: (i, 0)), pl.BlockSpec((1, d), lambda i: (0, 0))],
        out_specs=pl.BlockSpec((tm, d), lambda i: (i, 0)),
        compiler_params=_params("parallel"), name=name)(x, g)


def _group_masks(width):
    lane = lax.broadcasted_iota(jnp.int32, (1, width), 1)
    return [(lane >= HEAD_DIM * g) & (lane < HEAD_DIM * (g + 1)) for g in range(width // HEAD_DIM)]


def _group_sum(x, masks):
    out = jnp.zeros_like(x)
    for msk in masks:
        s = jnp.sum(jnp.where(msk, x, 0.0), axis=-1, keepdims=True)
        out = jnp.where(msk, s, out)
    return out


def _head_norm(x, gain, masks):
    r = lax.rsqrt(_group_sum(x * x, masks) * (1.0 / HEAD_DIM) + NORM_EPS)
    xhat = x * r
    return xhat * gain, xhat, r


def _head_norm_bwd(dxn, xhat, r, gain, masks):
    dgain = jnp.sum(dxn * xhat, axis=0, keepdims=True)
    dxhat = dxn * gain
    mean_t = _group_sum(dxhat * xhat, masks) * (1.0 / HEAD_DIM)
    return r * (dxhat - xhat * mean_t), dgain


def _softmax_rows(s):
    e = jnp.exp(s - jnp.max(s, axis=-1, keepdims=True))
    return e * (1.0 / jnp.sum(e, axis=-1, keepdims=True))


def _rel_onehot():
    col = lax.broadcasted_iota(jnp.int32, (1, KEY_WIN), 1)
    off = jnp.where(col < KEY_WIN - LANES, col, col - KEY_WIN)
    idx = jnp.clip(8 * CHUNK - off, -(CHUNK - 1), LANES) + (CHUNK - 1)
    return (lax.broadcasted_iota(jnp.int32, (N_REL, KEY_WIN), 0) == idx).astype(F32)


def bias_blocks(rel16):
    heads = TOK_WIDTH // HEAD_DIM

    def body(rel_ref, o_ref, u_ref):
        u_ref[...] = jnp.dot(rel_ref[...], _rel_onehot(), precision=HIGHEST, preferred_element_type=F32)
        row = lax.broadcasted_iota(jnp.int32, (CHUNK, KEY_WIN), 0)
        col = lax.broadcasted_iota(jnp.int32, (CHUNK, KEY_WIN), 1)
        for h in range(heads):
            xv = jnp.broadcast_to(u_ref[h:h + 1, :], (CHUNK, KEY_WIN))
            for b in range(6):
                xv = jnp.where(((row >> b) & 1) == 1, pltpu.roll(xv, 1 << b, axis=1), xv)
            xv = jnp.where(col < BAND, xv, NEG_INF)
            for i in range(Q_BLOCK // CHUNK):
                o_ref[h, CHUNK * i:CHUNK * (i + 1), :] = pltpu.roll(xv, CHUNK * i, axis=1) if i else xv

    return pl.pallas_call(
        body, out_shape=jax.ShapeDtypeStruct((heads, Q_BLOCK, KEY_WIN), F32),
        scratch_shapes=[pltpu.VMEM((16, KEY_WIN), F32)], name="bias_blocks")(rel16)


def bias_grad(dbias):
    heads = dbias.shape[0]

    def body(db_ref, o_ref, y_ref):
        y_ref[...] = jnp.zeros_like(y_ref)
        row = lax.broadcasted_iota(jnp.int32, (CHUNK, KEY_WIN), 0)
        for h in range(heads):
            fv = db_ref[h, 0:CHUNK, :]
            for i in range(1, Q_BLOCK // CHUNK):
                fv = fv + pltpu.roll(db_ref[h, CHUNK * i:CHUNK * (i + 1), :], KEY_WIN - CHUNK * i, axis=1)
            for b in range(6):
                fv = jnp.where(((row >> b) & 1) == 1, pltpu.roll(fv, KEY_WIN - (1 << b), axis=1), fv)
            y_ref[h:h + 1, :] = jnp.sum(fv, axis=0, keepdims=True)
        o_ref[...] = lax.dot_general(y_ref[...], _rel_onehot(), (((1,), (1,)), ((), ())),
                                     precision=HIGHEST, preferred_element_type=F32)

    return pl.pallas_call(
        body, out_shape=jax.ShapeDtypeStruct((16, N_REL), F32),
        scratch_shapes=[pltpu.VMEM((16, KEY_WIN), F32)], name="bias_grad")(dbias)


def _attn_windows(seq):
    out = []
    for j in range(seq // Q_BLOCK):
        r0 = j * Q_BLOCK
        k0 = max(0, r0 - 8 * CHUNK)
        width = r0 + Q_BLOCK - k0
        out.append((r0, k0, width, KEY_WIN - width))
    return out


def attn_fwd(z, gq2, gk2, bias, batch, seq):
    n = z.shape[0]
    pairs = TOK_WIDTH // LANES

    def body(q_ref, k_ref, v_ref, gq_ref, gk_ref, b_ref, o_ref, qs_s, kn_s):
        masks = _group_masks(LANES)
        qs_s[...] = (_head_norm(q_ref[...].astype(F32), gq_ref[...], masks)[0] * ATTN_SCALE).astype(BF16)
        kn_s[...] = _head_norm(k_ref[...].astype(F32), gk_ref[...], masks)[0].astype(BF16)
        for r0, k0, width, c0 in _attn_windows(seq):
            qb = qs_s[r0:r0 + Q_BLOCK, :]
            kw = kn_s[k0:k0 + width, :]
            vw = v_ref[k0:k0 + width, :]
            out = jnp.zeros((Q_BLOCK, LANES), F32)
            for h, msk in enumerate(masks):
                qh = jnp.where(msk, qb, jnp.zeros_like(qb))
                s = _dot(qh, kw, 1, 1) + b_ref[h, :, c0:KEY_WIN]
                p = _softmax_rows(s).astype(BF16)
                out = jnp.where(msk, _dot(p, vw, 1, 0), out)
            o_ref[r0:r0 + Q_BLOCK, :] = out.astype(o_ref.dtype)

    def col(off):
        return pl.BlockSpec((seq, LANES), lambda b, p: (b, off + p))

    vec = pl.BlockSpec((1, LANES), lambda b, p: (0, 0))
    return pl.pallas_call(
        body, out_shape=jax.ShapeDtypeStruct((n, D_MODEL), BF16), grid=(batch, pairs),
        in_specs=[col(0), col(pairs), col(2 * pairs), vec, vec,
                  pl.BlockSpec((2, Q_BLOCK, KEY_WIN), lambda b, p: (p, 0, 0))],
        out_specs=pl.BlockSpec((seq, LANES), lambda b, p: (b, p)),
        scratch_shapes=[pltpu.VMEM((seq, LANES), BF16), pltpu.VMEM((seq, LANES), BF16)],
        compiler_params=_params("parallel", "arbitrary"), name="attn_fwd")(z, z, z, gq2, gk2, bias)


def attn_bwd(z, dcat, gq2, gk2, bias, batch, seq):
    n = z.shape[0]
    pairs = TOK_WIDTH // LANES

    def body(q_ref, k_ref, v_ref, do_ref, gq_ref, gk_ref, b_ref,
             dz_ref, db_ref, dgq_ref, dgk_ref, qs_s, kn_s, dqn_s, dkn_s, dv_s, dk_o, dv_o):
        pi, bi, which = pl.program_id(0), pl.program_id(1), pl.program_id(2)

        @pl.when(which == 0)
        def _():
            masks = _group_masks(LANES)

            @pl.when(bi == 0)
            def _():
                db_ref[...] = jnp.zeros_like(db_ref)

            @pl.when((bi == 0) & (pi == 0))
            def _():
                dgq_ref[...] = jnp.zeros_like(dgq_ref)
                dgk_ref[...] = jnp.zeros_like(dgk_ref)

            qn, qhat, rq = _head_norm(q_ref[...].astype(F32), gq_ref[...], masks)
            kn, khat, rk = _head_norm(k_ref[...].astype(F32), gk_ref[...], masks)
            qs_s[...] = (qn * ATTN_SCALE).astype(BF16)
            kn_s[...] = kn.astype(BF16)
            dkn_s[...] = jnp.zeros_like(dkn_s)
            dv_s[...] = jnp.zeros_like(dv_s)
            for r0, k0, width, c0 in _attn_windows(seq):
                qb = qs_s[r0:r0 + Q_BLOCK, :]
                dob = do_ref[r0:r0 + Q_BLOCK, :]
                kw = kn_s[k0:k0 + width, :]
                vw = v_ref[k0:k0 + width, :]
                dq_acc = jnp.zeros((Q_BLOCK, LANES), F32)
                dk_acc = jnp.zeros((width, LANES), F32)
                dv_acc = jnp.zeros((width, LANES), F32)
                for h, msk in enumerate(masks):
                    qh = jnp.where(msk, qb, jnp.zeros_like(qb))
                    doh = jnp.where(msk, dob, jnp.zeros_like(dob))
                    p = _softmax_rows(_dot(qh, kw, 1, 1) + b_ref[h, :, c0:KEY_WIN])
                    dp = _dot(doh, vw, 1, 1)
                    ds = p * (dp - jnp.sum(p * dp, axis=-1, keepdims=True))
                    db_ref[h, :, c0:KEY_WIN] += ds
                    dsb = ds.astype(BF16)
                    dq_acc = jnp.where(msk, _dot(dsb, kw, 1, 0), dq_acc)
                    dk_acc = jnp.where(msk, _dot(dsb, qb, 0, 0), dk_acc)
                    dv_acc = jnp.where(msk, _dot(p.astype(BF16), dob, 0, 0), dv_acc)
                dqn_s[r0:r0 + Q_BLOCK, :] = dq_acc * ATTN_SCALE
                dkn_s[k0:k0 + width, :] += dk_acc
                dv_s[k0:k0 + width, :] += dv_acc
            dq, dgq = _head_norm_bwd(dqn_s[...], qhat, rq, gq_ref[...], masks)
            dk, dgk = _head_norm_bwd(dkn_s[...], khat, rk, gk_ref[...], masks)
            dz_ref[...] = dq.astype(dz_ref.dtype)
            dk_o[...] = dk.astype(dk_o.dtype)
            dv_o[...] = dv_s[...].astype(dv_o.dtype)
            dgq_ref[...] += dgq
            dgk_ref[...] += dgk

        @pl.when(which == 1)
        def _():
            dz_ref[...] = dk_o[...]

        @pl.when(which == 2)
        def _():
            dz_ref[...] = dv_o[...]

    def ahead(p, b, t):
        nb = b + jnp.where(t > 0, 1, 0)
        wrap = jnp.where(nb >= batch, 1, 0)
        return jnp.minimum(p + wrap, pairs - 1), nb - wrap * batch

    def col(off):
        def index(p, b, t):
            np_, nb = ahead(p, b, t)
            return nb, off + np_
        return pl.BlockSpec((seq, LANES), index)

    vec = pl.BlockSpec((1, LANES), lambda p, b, t: (0, 0))
    blk = pl.BlockSpec((2, Q_BLOCK, KEY_WIN), lambda p, b, t: (p, 0, 0))
    blk_in = pl.BlockSpec((2, Q_BLOCK, KEY_WIN), lambda p, b, t: (ahead(p, b, t)[0], 0, 0))
    v_shape = jax.ShapeDtypeStruct((1, LANES), F32)
    return pl.pallas_call(
        body,
        out_shape=(jax.ShapeDtypeStruct(z.shape, BF16), jax.ShapeDtypeStruct(bias.shape, F32), v_shape, v_shape),
        grid=(pairs, batch, 3),
        in_specs=[col(0), col(pairs), col(2 * pairs), col(0), vec, vec, blk_in],
        out_specs=(pl.BlockSpec((seq, LANES), lambda p, b, t: (b, t * pairs + p)), blk, vec, vec),
        scratch_shapes=[pltpu.VMEM((seq, LANES), BF16), pltpu.VMEM((seq, LANES), BF16),
                        pltpu.VMEM((seq, LANES), F32), pltpu.VMEM((seq, LANES), F32), pltpu.VMEM((seq, LANES), F32),
                        pltpu.VMEM((seq, LANES), BF16), pltpu.VMEM((seq, LANES), BF16)],
        compiler_params=_params("arbitrary", "arbitrary", "arbitrary"), name="attn_bwd")(
            z, z, z, dcat, gq2, gk2, bias)


MEM_ROWS = 512


def memattn_fwd(z, kv, gq4, gk4, cat, batch, seq, qcol, name):
    mtok = kv.shape[0] // batch
    rows = min(MEM_ROWS, seq)

    def body(q_ref, kv_ref, gq_ref, gk_ref, cat_ref, o_ref):
        del cat_ref
        masks = _group_masks(MEM_WIDTH)
        kn = _head_norm(kv_ref[:, 0:MEM_WIDTH], gk_ref[...], masks)[0].astype(BF16)
        vm = kv_ref[:, MEM_WIDTH:2 * MEM_WIDTH].astype(BF16)
        for t in range(seq // rows):
            sl = slice(t * rows, (t + 1) * rows)
            qs = (_head_norm(q_ref[sl, :].astype(F32), gq_ref[...], masks)[0] * ATTN_SCALE).astype(BF16)
            out = jnp.zeros((rows, MEM_WIDTH), F32)
            for msk in masks:
                qh = jnp.where(msk, qs, jnp.zeros_like(qs))
                p = _softmax_rows(_dot(qh, kn, 1, 1)).astype(BF16)
                out = jnp.where(msk, _dot(p, vm, 1, 0), out)
            o_ref[sl, :] = out.astype(o_ref.dtype)

    vec = pl.BlockSpec((1, MEM_WIDTH), lambda b: (0, 0))
    return pl.pallas_call(
        body, out_shape=jax.ShapeDtypeStruct(cat.shape, cat.dtype), grid=(batch,),
        in_specs=[pl.BlockSpec((seq, MEM_WIDTH), lambda b: (b, qcol)),
                  pl.BlockSpec((mtok, 2 * MEM_WIDTH), lambda b: (b, 0)), vec, vec, ANY],
        out_specs=pl.BlockSpec((seq, MEM_WIDTH), lambda b: (b, TOK_WIDTH // MEM_WIDTH)),
        input_output_aliases={4: 0},
        compiler_params=_params("parallel"), name=name)(z, kv, gq4, gk4, cat)


def memattn_bwd(z, kv, dcat, gq4, gk4, dz, batch, seq, qcol, name):
    mtok = kv.shape[0] // batch
    rows = min(MEM_ROWS, seq)

    def body(q_ref, kv_ref, do_ref, gq_ref, gk_ref, dz_in_ref, dq_ref, dkv_ref, dgq_ref, dgk_ref):
        del dz_in_ref
        @pl.when(pl.program_id(0) == 0)
        def _():
            dgq_ref[...] = jnp.zeros_like(dgq_ref)
            dgk_ref[...] = jnp.zeros_like(dgk_ref)

        masks = _group_masks(MEM_WIDTH)
        kn_f, khat, rk = _head_norm(kv_ref[:, 0:MEM_WIDTH], gk_ref[...], masks)
        kn = kn_f.astype(BF16)
        vm = kv_ref[:, MEM_WIDTH:2 * MEM_WIDTH].astype(BF16)
        dkn = jnp.zeros((mtok, MEM_WIDTH), F32)
        dvm = jnp.zeros((mtok, MEM_WIDTH), F32)
        dgq = jnp.zeros((1, MEM_WIDTH), F32)
        for t in range(seq // rows):
            sl = slice(t * rows, (t + 1) * rows)
            qn_f, qhat, rq = _head_norm(q_ref[sl, :].astype(F32), gq_ref[...], masks)
            qs = (qn_f * ATTN_SCALE).astype(BF16)
            dob = do_ref[sl, :]
            dqn = jnp.zeros((rows, MEM_WIDTH), F32)
            for msk in masks:
                qh = jnp.where(msk, qs, jnp.zeros_like(qs))
                doh = jnp.where(msk, dob, jnp.zeros_like(dob))
                p = _softmax_rows(_dot(qh, kn, 1, 1))
                dp = _dot(doh, vm, 1, 1)
                ds = p * (dp - jnp.sum(p * dp, axis=-1, keepdims=True))
                dsb = ds.astype(BF16)
                dqn = jnp.where(msk, _dot(dsb, kn, 1, 0), dqn)
                dkn = dkn + jnp.where(msk, _dot(dsb, qs, 0, 0), 0.0)
                dvm = dvm + jnp.where(msk, _dot(p.astype(BF16), dob, 0, 0), 0.0)
            dq, dg = _head_norm_bwd(dqn * ATTN_SCALE, qhat, rq, gq_ref[...], masks)
            dq_ref[sl, :] = dq.astype(dq_ref.dtype)
            dgq = dgq + dg
        dk, dgk = _head_norm_bwd(dkn, khat, rk, gk_ref[...], masks)
        dkv_ref[:, 0:MEM_WIDTH] = dk
        dkv_ref[:, MEM_WIDTH:2 * MEM_WIDTH] = dvm
        dgq_ref[...] += dgq
        dgk_ref[...] += dgk

    vec = pl.BlockSpec((1, MEM_WIDTH), lambda b: (0, 0))
    kv_spec = pl.BlockSpec((mtok, 2 * MEM_WIDTH), lambda b: (b, 0))
    v_shape = jax.ShapeDtypeStruct((1, MEM_WIDTH), F32)
    q_spec = pl.BlockSpec((seq, MEM_WIDTH), lambda b: (b, qcol))
    return pl.pallas_call(
        body,
        out_shape=(jax.ShapeDtypeStruct(dz.shape, dz.dtype), jax.ShapeDtypeStruct(kv.shape, F32), v_shape, v_shape),
        grid=(batch,),
        in_specs=[q_spec, kv_spec, pl.BlockSpec((seq, MEM_WIDTH), lambda b: (b, TOK_WIDTH // MEM_WIDTH)), vec, vec, ANY],
        out_specs=(q_spec, kv_spec, vec, vec),
        input_output_aliases={5: 0},
        compiler_params=_params("arbitrary"), name=name)(z, kv, dcat, gq4, gk4, dz)


def mem_prep(mem, gain, wkv, name):
    t, d = mem.shape

    def body(m_ref, g_ref, w_ref, n_ref, kv_ref):
        mv = m_ref[...]
        r = lax.rsqrt(jnp.mean(mv * mv, axis=-1, keepdims=True) + NORM_EPS)
        nv = (mv * r * g_ref[...]).astype(BF16)
        n_ref[...] = nv
        kv_ref[...] = _dot(nv, w_ref[...], 1, 0)

    vmem = pl.BlockSpec(memory_space=pltpu.VMEM)
    return pl.pallas_call(
        body, out_shape=(jax.ShapeDtypeStruct((t, d), BF16), jax.ShapeDtypeStruct((t, wkv.shape[1]), F32)),
        in_specs=[vmem, vmem, vmem], out_specs=(vmem, vmem),
        compiler_params=pltpu.CompilerParams(vmem_limit_bytes=VMEM_LIMIT), name=name)(mem, gain, wkv)


def mem_norm_grad(dkv, wkv, mem, name):
    t, d = mem.shape

    def body(dkv_ref, w_ref, m_ref, dg_ref):
        dn = _dot(dkv_ref[...].astype(BF16), w_ref[...], 1, 1)
        mv = m_ref[...]
        r = lax.rsqrt(jnp.mean(mv * mv, axis=-1, keepdims=True) + NORM_EPS)
        dg_ref[...] = jnp.sum(dn * (mv * r), axis=0, keepdims=True)

    vmem = pl.BlockSpec(memory_space=pltpu.VMEM)
    return pl.pallas_call(
        body, out_shape=jax.ShapeDtypeStruct((1, d), F32), in_specs=[vmem, vmem, vmem], out_specs=vmem,
        compiler_params=pltpu.CompilerParams(vmem_limit_bytes=VMEM_LIMIT), name=name)(dkv, wkv, mem)


CONV_ROWS = 256


def _glu(a_ref, g_ref):
    return a_ref[...].astype(F32) * _sigmoid(g_ref[...].astype(F32))


def _layer_norm_stats(y):
    mu = jnp.mean(y, axis=-1, keepdims=True)
    yc = y - mu
    rstd = lax.rsqrt(jnp.mean(yc * yc, axis=-1, keepdims=True) + NORM_EPS)
    return yc * rstd, rstd


CONV_WIN = CONV_HALO + CONV_ROWS
SUBLANES = 8
SHIFT_ROWS = CONV_WIN - SUBLANES


def _preshift(win, shifted):
    for s in range(1, SUBLANES):
        shifted[s - 1, :, :] = win[s:s + SHIFT_ROWS, :]


TAP_ROWS = 64
TAP_TILES = [(r0, slice(c0, c0 + LANES)) for c0 in range(0, TOK_WIDTH, LANES) for r0 in range(0, CONV_ROWS, TAP_ROWS)]


def _tap(win, shifted, off, r0, lanes):
    s = off % SUBLANES
    base = off - s + r0
    if s == 0:
        return win[base:base + TAP_ROWS, lanes]
    return shifted[s - 1, base:base + TAP_ROWS, lanes]


def _fold_rows(x):
    return jnp.sum(x.reshape(TAP_ROWS // SUBLANES, SUBLANES, LANES), axis=0)


def conv_fwd(z, cw, cb, lg, lb, batch, seq):
    n = z.shape[0]
    nt = seq // CONV_ROWS
    sub = CONV_ROWS // CONV_HALO
    lead = CONV_HALO - (CONV_W - 1)

    def body(a_ref, g_ref, ap_ref, gp_ref, cw_ref, cb_ref, lg_ref, lb_ref, o_ref, y_ref, win, shifted):
        first = pl.program_id(1) == 0
        win[0:CONV_HALO, :] = jnp.where(first, 0.0, _glu(ap_ref, gp_ref))
        win[CONV_HALO:CONV_WIN, :] = _glu(a_ref, g_ref)
        _preshift(win, shifted)
        for r0, lanes in TAP_TILES:
            acc = jnp.zeros((TAP_ROWS, LANES), F32) + cb_ref[:, lanes]
            for w in range(CONV_W):
                acc = acc + _tap(win, shifted, lead + w, r0, lanes) * cw_ref[w:w + 1, lanes]
            y_ref[r0:r0 + TAP_ROWS, lanes] = acc
        yh, _ = _layer_norm_stats(y_ref[...])
        t = yh * lg_ref[...] + lb_ref[...]
        o_ref[...] = (t * _sigmoid(t)).astype(o_ref.dtype)

    def cur(c):
        return pl.BlockSpec((CONV_ROWS, TOK_WIDTH), lambda b, i: (b * nt + i, c))

    def prev(c):
        return pl.BlockSpec((CONV_HALO, TOK_WIDTH), lambda b, i: (jnp.maximum((b * nt + i) * sub - 1, 0), c))

    vec = pl.BlockSpec((1, TOK_WIDTH), lambda b, i: (0, 0))
    return pl.pallas_call(
        body, out_shape=(jax.ShapeDtypeStruct((n, D_MODEL), BF16), jax.ShapeDtypeStruct((n, TOK_WIDTH), F32)),
        grid=(batch, nt),
        in_specs=[cur(0), cur(1), prev(0), prev(1), pl.BlockSpec((32, TOK_WIDTH), lambda b, i: (0, 0)), vec, vec, vec],
        out_specs=(cur(0), cur(0)),
        scratch_shapes=[pltpu.VMEM((CONV_WIN, TOK_WIDTH), F32), pltpu.VMEM((SUBLANES - 1, SHIFT_ROWS, TOK_WIDTH), F32)],
        compiler_params=_params("parallel", "arbitrary"), name="conv_fwd")(z, z, z, z, cw, cb, lg, lb)


def conv_bwd(z, y, dcat, cw, lg, lb, batch, seq):
    n = z.shape[0]
    nt = seq // CONV_ROWS
    sub = CONV_ROWS // CONV_HALO
    lead = CONV_HALO - (CONV_W - 1)
    last_blk = n // CONV_HALO - 1

    def body(a_ref, g_ref, ap_ref, gp_ref, y_ref, yn_ref, do_ref, don_ref, cw_ref, lg_ref, lb_ref,
             dz_ref, dcw_ref, dsm_ref, win, shifted, dyw, dshifted, dg_o):
        b, i, which = pl.program_id(0), pl.program_id(1), pl.program_id(2)

        @pl.when(which == 0)
        def _():
            first, last = i == 0, i == nt - 1

            @pl.when((b == 0) & (i == 0))
            def _():
                dcw_ref[...] = jnp.zeros_like(dcw_ref)
                dsm_ref[...] = jnp.zeros_like(dsm_ref)

            win[0:CONV_HALO, :] = jnp.where(first, 0.0, _glu(ap_ref, gp_ref))
            win[CONV_HALO:CONV_WIN, :] = _glu(a_ref, g_ref)
            _preshift(win, shifted)
            yv = jnp.concatenate([y_ref[...], yn_ref[...]], axis=0)
            yh, rstd = _layer_norm_stats(yv)
            t = yh * lg_ref[...] + lb_ref[...]
            st = _sigmoid(t)
            dout = jnp.concatenate(
                [do_ref[...].astype(F32), jnp.where(last, 0.0, don_ref[...].astype(F32))], axis=0)
            dt = dout * st * (1.0 + t * (1.0 - st))
            dyh = dt * lg_ref[...]
            dy = rstd * (dyh - jnp.mean(dyh, axis=-1, keepdims=True)
                         - yh * jnp.mean(dyh * yh, axis=-1, keepdims=True))
            dyw[...] = dy
            _preshift(dyw, dshifted)
            dsm_ref[0:1, :] += jnp.sum(dy[0:CONV_ROWS], axis=0, keepdims=True)
            dsm_ref[1:2, :] += jnp.sum((dt * yh)[0:CONV_ROWS], axis=0, keepdims=True)
            dsm_ref[2:3, :] += jnp.sum(dt[0:CONV_ROWS], axis=0, keepdims=True)
            for c0 in range(0, TOK_WIDTH, LANES):
                lanes = slice(c0, c0 + LANES)
                dcw_acc = [jnp.zeros((SUBLANES, LANES), F32) for _ in range(CONV_W)]
                for r0 in range(0, CONV_ROWS, TAP_ROWS):
                    dyt = dyw[r0:r0 + TAP_ROWS, lanes]
                    dglu = jnp.zeros((TAP_ROWS, LANES), F32)
                    for w in range(CONV_W):
                        dcw_acc[w] = dcw_acc[w] + _fold_rows(dyt * _tap(win, shifted, lead + w, r0, lanes))
                        dglu = dglu + _tap(dyw, dshifted, CONV_W - 1 - w, r0, lanes) * cw_ref[w:w + 1, lanes]
                    avt = a_ref[r0:r0 + TAP_ROWS, lanes].astype(F32)
                    sgt = _sigmoid(g_ref[r0:r0 + TAP_ROWS, lanes].astype(F32))
                    dz_ref[r0:r0 + TAP_ROWS, lanes] = (dglu * sgt).astype(dz_ref.dtype)
                    dg_o[r0:r0 + TAP_ROWS, lanes] = (dglu * avt * sgt * (1.0 - sgt)).astype(dg_o.dtype)
                for w in range(CONV_W):
                    dcw_ref[w:w + 1, lanes] += jnp.sum(dcw_acc[w], axis=0, keepdims=True)

        @pl.when(which == 1)
        def _():
            dz_ref[...] = dg_o[...]

    def ahead(b, i, t):
        return jnp.minimum(b * nt + i + t, batch * nt - 1)

    def cur(c):
        return pl.BlockSpec((CONV_ROWS, TOK_WIDTH), lambda b, i, t: (ahead(b, i, t), c))

    def prev(c):
        return pl.BlockSpec((CONV_HALO, TOK_WIDTH), lambda b, i, t: (jnp.maximum(ahead(b, i, t) * sub - 1, 0), c))

    nxt = pl.BlockSpec((CONV_HALO, TOK_WIDTH),
                       lambda b, i, t: (jnp.minimum((ahead(b, i, t) + 1) * sub, last_blk), 0))
    vec = pl.BlockSpec((1, TOK_WIDTH), lambda b, i, t: (0, 0))
    full32 = pl.BlockSpec((32, TOK_WIDTH), lambda b, i, t: (0, 0))
    return pl.pallas_call(
        body,
        out_shape=(jax.ShapeDtypeStruct(z.shape, BF16), jax.ShapeDtypeStruct((32, TOK_WIDTH), F32),
                   jax.ShapeDtypeStruct((8, TOK_WIDTH), F32)),
        grid=(batch, nt, 2),
        in_specs=[cur(0), cur(1), prev(0), prev(1), cur(0), nxt, cur(0), nxt, full32, vec, vec],
        out_specs=(pl.BlockSpec((CONV_ROWS, TOK_WIDTH), lambda b, i, t: (b * nt + i, t)), full32,
                   pl.BlockSpec((8, TOK_WIDTH), lambda b, i, t: (0, 0))),
        scratch_shapes=[pltpu.VMEM((CONV_WIN, TOK_WIDTH), F32), pltpu.VMEM((SUBLANES - 1, SHIFT_ROWS, TOK_WIDTH), F32),
                        pltpu.VMEM((CONV_WIN, TOK_WIDTH), F32), pltpu.VMEM((SUBLANES - 1, SHIFT_ROWS, TOK_WIDTH), F32),
                        pltpu.VMEM((CONV_ROWS, TOK_WIDTH), BF16)],
        compiler_params=_params("arbitrary", "arbitrary", "arbitrary"), name="conv_bwd")(
            z, z, z, z, y, y, dcat, dcat, cw, lg, lb)


def _place():
    return lax.axis_index("x"), lax.axis_index("y"), lax.axis_index("c")


def _other_chips(x, y):
    return [(1 - x, y), (x, 1 - y), (1 - x, 1 - y)]


def reduce_small(arrays):
    na = len(arrays)

    def body(*refs):
        ins, outs, bufs = refs[:na], refs[na:2 * na], refs[2 * na:3 * na]
        send_sems, recv_sems = refs[3 * na:]
        x, y, c = _place()
        me = 4 * x + 2 * y + c
        copies = []
        for a in range(na):
            bufs[a][me] = ins[a][...]
            for k in range(1, N_DEV):
                cp = pltpu.make_async_remote_copy(
                    src_ref=ins[a], dst_ref=bufs[a].at[me], send_sem=send_sems.at[a, k - 1],
                    recv_sem=recv_sems.at[a, k - 1],
                    device_id=(x ^ (k >> 2), y ^ ((k >> 1) & 1), c ^ (k & 1)), device_id_type=MESH)
                cp.start()
                copies.append(cp)
        for a in range(na):
            for k in range(1, N_DEV):
                src = 4 * (x ^ (k >> 2)) + 2 * (y ^ ((k >> 1) & 1)) + (c ^ (k & 1))
                pltpu.make_async_remote_copy(
                    src_ref=ins[a], dst_ref=bufs[a].at[src], send_sem=send_sems.at[a, k - 1],
                    recv_sem=recv_sems.at[a, k - 1], device_id=(x, y, c), device_id_type=MESH).wait_recv()
        for cp in copies:
            cp.wait_send()
        for a in range(na):
            total = bufs[a][0]
            for dev in range(1, N_DEV):
                total = total + bufs[a][dev]
            outs[a][...] = total

    vmem = pl.BlockSpec(memory_space=pltpu.VMEM)
    return pl.pallas_call(
        body, out_shape=tuple(jax.ShapeDtypeStruct(a.shape, F32) for a in arrays),
        in_specs=[vmem] * na, out_specs=tuple([vmem] * na),
        scratch_shapes=[pltpu.VMEM((N_DEV,) + a.shape, F32) for a in arrays]
        + [pltpu.SemaphoreType.DMA((na, N_DEV - 1)), pltpu.SemaphoreType.DMA((na, N_DEV - 1))],
        compiler_params=pltpu.CompilerParams(vmem_limit_bytes=VMEM_LIMIT), name="small_reduce")(*arrays)


def adamw_small(ws, gs, ms, vs):
    na = len(ws)
    c1 = 1.0 / (1.0 - ADAM_B1 ** ADAM_STEP)
    c2 = 1.0 / (1.0 - ADAM_B2 ** ADAM_STEP)

    def body(*refs):
        w_refs, g_refs, m_refs, v_refs = (refs[i * na:(i + 1) * na] for i in range(4))
        d_refs, nm_refs, nv_refs = (refs[(4 + i) * na:(5 + i) * na] for i in range(3))
        for a in range(na):
            gv = g_refs[a][...]
            nm = ADAM_B1 * m_refs[a][...] + (1.0 - ADAM_B1) * gv
            nv = ADAM_B2 * v_refs[a][...] + (1.0 - ADAM_B2) * (gv * gv)
            nm_refs[a][...] = nm
            nv_refs[a][...] = nv
            d_refs[a][...] = -ADAM_LR * ((nm * c1) / (jnp.sqrt(nv * c2) + ADAM_EPS) + ADAM_WD * w_refs[a][...])

    vmem = pl.BlockSpec(memory_space=pltpu.VMEM)
    shapes = tuple(jax.ShapeDtypeStruct(w.shape, F32) for w in ws)
    outs = pl.pallas_call(
        body, out_shape=shapes * 3, in_specs=[vmem] * (4 * na), out_specs=tuple([vmem] * (3 * na)),
        compiler_params=pltpu.CompilerParams(vmem_limit_bytes=VMEM_LIMIT), name="adamw_small")(*ws, *gs, *ms, *vs)
    return outs[:na], outs[na:2 * na], outs[2 * na:]


def gather_weights(shards, name, collective_id):
    nw = len(shards)
    ns = [s.shape[0] for s in shards]
    in_refs = [jax.new_ref(s, memory_space=pltpu.MemorySpace.HBM) for s in shards]
    out_refs = [jax.empty_ref(jax.ShapeDtypeStruct((N_DEV * s.shape[0], s.shape[1]), s.dtype),
                              memory_space=pltpu.MemorySpace.HBM) for s in shards]

    @pl.kernel(mesh=plsc.ScalarSubcoreMesh(axis_name="seq", num_cores=1), name=name,
               scratch_types=(pltpu.SemaphoreType.DMA((nw, 7)), pltpu.SemaphoreType.DMA((nw, 7)),
                              pltpu.SemaphoreType.DMA((nw,))),
               compiler_params=pltpu.CompilerParams(collective_id=collective_id))
    def launch(send_sems, recv_sems, local_sems):
        x, y, c = _place()
        me, sib = (x, y, c), (x, y, 1 - c)
        chips = _other_chips(x, y)
        barrier = pltpu.get_barrier_semaphore()
        for peer in [sib] + [(*chip, c) for chip in chips]:
            pl.semaphore_signal(barrier, inc=1, device_id=peer, device_id_type=MESH)
        pl.semaphore_wait(barrier, 4)

        def rows(w, dev):
            return out_refs[w].at[pl.ds((4 * dev[0] + 2 * dev[1] + dev[2]) * ns[w], ns[w]), :]

        def copy(w, k, block, to, src=None):
            return pltpu.make_async_remote_copy(
                src_ref=rows(w, block) if src is None else src, dst_ref=rows(w, block),
                send_sem=send_sems.at[w, k], recv_sem=recv_sems.at[w, k], device_id=to, device_id_type=MESH)

        started, sends = [], []
        for w in range(nw):
            mine = pltpu.make_async_copy(in_refs[w], rows(w, me), local_sems.at[w])
            mine.start()
            started.append(mine)
            first = [copy(w, 0, me, sib, src=in_refs[w])]
            first += [copy(w, 1 + j, me, (*chip, c), src=in_refs[w]) for j, chip in enumerate(chips)]
            for cp in first:
                cp.start()
            sends += first
        for w in range(nw):
            for j, chip in enumerate(chips):
                copy(w, 1 + j, (*chip, c), me).wait_recv()
                fwd = copy(w, 4 + j, (*chip, c), sib)
                fwd.start()
                sends.append(fwd)
        for w in range(nw):
            copy(w, 0, sib, me).wait_recv()
            for j, chip in enumerate(chips):
                copy(w, 4 + j, (*chip, 1 - c), me).wait_recv()
        for cp in sends:
            cp.wait_send()
        for mine in started:
            mine.wait()

    launch()
    return [r[...] for r in out_refs]


def _sequencer_exchange(sources, out_rows, peers_of, copies_of, name, collective_id):
    nw = len(sources)
    in_refs = [jax.new_ref(s, memory_space=pltpu.MemorySpace.HBM) for s in sources]
    out_refs = [jax.empty_ref(jax.ShapeDtypeStruct((rows, s.shape[1]), s.dtype), memory_space=pltpu.MemorySpace.HBM)
                for rows, s in zip(out_rows, sources)]
    per = len(copies_of(0, 0, 0, 0))

    @pl.kernel(mesh=plsc.ScalarSubcoreMesh(axis_name="seq", num_cores=1), name=name,
               scratch_types=(pltpu.SemaphoreType.DMA((nw, per)), pltpu.SemaphoreType.DMA((nw, per))),
               compiler_params=pltpu.CompilerParams(collective_id=collective_id))
    def launch(send_sems, recv_sems):
        x, y, c = _place()
        peers = peers_of(x, y, c)
        barrier = pltpu.get_barrier_semaphore()
        for peer in peers:
            pl.semaphore_signal(barrier, inc=1, device_id=peer, device_id_type=MESH)
        pl.semaphore_wait(barrier, len(peers))
        copies = []
        for w in range(nw):
            for k, (src_blk, dst_blk, rows, peer) in enumerate(copies_of(x, y, c, w)):
                cp = pltpu.make_async_remote_copy(
                    src_ref=in_refs[w].at[pl.ds(src_blk * rows, rows), :],
                    dst_ref=out_refs[w].at[pl.ds(dst_blk * rows, rows), :],
                    send_sem=send_sems.at[w, k], recv_sem=recv_sems.at[w, k], device_id=peer, device_id_type=MESH)
                cp.start()
                copies.append(cp)
        for cp in copies:
            cp.wait_recv()
        for cp in copies:
            cp.wait_send()

    launch()
    return [r[...] for r in out_refs]


def scatter_to_sibling(grads, name, collective_id):
    ns = [g.shape[0] // N_DEV for g in grads]
    return _sequencer_exchange(
        grads, [4 * n for n in ns],
        lambda x, y, c: [(x, y, 1 - c)],
        lambda x, y, c, w: [(2 * q + 1 - c, q, ns[w], (x, y, 1 - c)) for q in range(4)],
        name, collective_id)


def scatter_to_chips(parts, name, collective_id):
    ns = [p.shape[0] // 4 for p in parts]
    return _sequencer_exchange(
        parts, [3 * n for n in ns],
        lambda x, y, c: [(*chip, c) for chip in _other_chips(x, y)],
        lambda x, y, c, w: [(2 * chip[0] + chip[1], j, ns[w], (*chip, c)) for j, chip in enumerate(_other_chips(x, y))],
        name, collective_id)


def add_sibling(grads, landeds, core, name):
    nw = len(grads)

    def body(c_ref, *refs):
        for w in range(nw):
            g_ref, l_ref, o_ref = refs[2 * w], refs[2 * w + 1], refs[2 * nw + w]
            o_ref[...] = (g_ref[...].astype(F32) + l_ref[...].astype(F32)).astype(o_ref.dtype)

    in_specs, out_specs, args = [], [], []
    for g, ld in zip(grads, landeds):
        n, cols = ld.shape[0] // 4, g.shape[1]
        in_specs += [pl.BlockSpec((n, cols), lambda q, c_ref: (2 * q + c_ref[0], 0)),
                     pl.BlockSpec((n, cols), lambda q, c_ref: (q, 0))]
        out_specs.append(pl.BlockSpec((n, cols), lambda q, c_ref: (q, 0)))
        args += [g, ld]
    grid_spec = pltpu.PrefetchScalarGridSpec(
        num_scalar_prefetch=1, grid=(4,), in_specs=in_specs, out_specs=tuple(out_specs))
    return pl.pallas_call(
        body, out_shape=tuple(jax.ShapeDtypeStruct(ld.shape, ld.dtype) for ld in landeds), grid_spec=grid_spec,
        compiler_params=_params("arbitrary"), name=name)(core, *args)


def adamw_shard(layer, w, m, v, part, landed, chip, earlier, name):
    n = landed.shape[0] // 3
    cols = w.shape[1]
    c1 = 1.0 / (1.0 - ADAM_B1 ** ADAM_STEP)
    c2 = 1.0 / (1.0 - ADAM_B2 ** ADAM_STEP)

    def body(q_ref, w_ref, m_ref, v_ref, p_ref, l0_ref, l1_ref, l2_ref, *rest):
        g_ref, d_ref, nm_ref, nv_ref = rest[-4:]
        gv = ((p_ref[...].astype(F32) + l0_ref[...].astype(F32)) + l1_ref[...].astype(F32)) + l2_ref[...].astype(F32)
        nm = ADAM_B1 * m_ref[...] + (1.0 - ADAM_B1) * gv
        nv = ADAM_B2 * v_ref[...] + (1.0 - ADAM_B2) * (gv * gv)
        g_ref[...] = gv
        nm_ref[...] = nm
        nv_ref[...] = nv
        d_ref[...] = -ADAM_LR * ((nm * c1) / (jnp.sqrt(nv * c2) + ADAM_EPS) + ADAM_WD * w_ref[...])

    sub = 2 if n % (2 * 16) == 0 else 1
    rows = n // sub
    own = pl.BlockSpec((rows, cols), lambda i, q_ref: (layer * sub + i, 0))

    def landed_spec(j):
        return pl.BlockSpec((rows, cols), lambda i, q_ref: (j * sub + i, 0))

    in_specs = [own, own, own, pl.BlockSpec((rows, cols), lambda i, q_ref: (q_ref[0] * sub + i, 0)),
                landed_spec(0), landed_spec(1), landed_spec(2)]
    args = [chip, w, m, v, part, landed, landed, landed]
    aliases = {}
    if earlier is not None:
        in_specs += [ANY] * 4
        args += list(earlier)
        aliases = {8 + k: k for k in range(4)}
    grid_spec = pltpu.PrefetchScalarGridSpec(
        num_scalar_prefetch=1, grid=(sub,), in_specs=in_specs, out_specs=(own, own, own, own))
    shape = jax.ShapeDtypeStruct(w.shape, F32)
    return pl.pallas_call(
        body, out_shape=(shape, shape, shape, shape), grid_spec=grid_spec, input_output_aliases=aliases,
        compiler_params=_params("arbitrary"), name=name)(*args)


def _pack(arrays):
    flat = jnp.concatenate([a.reshape(-1).astype(F32) for a in arrays])
    pad = (-flat.shape[0]) % (8 * LANES)
    return jnp.pad(flat, (0, pad)).reshape(-1, LANES)


def _unpack(slab, shapes):
    flat = slab.reshape(slab.shape[:-2] + (-1,))
    out, off = [], 0
    for shp in shapes:
        size = 1
        for s in shp:
            size *= s
        out.append(flat[..., off:off + size].reshape(flat.shape[:-1] + tuple(shp)))
        off += size
    return out


def kernel(x, mem, norm1_g, mem_norm_g, a_w_in, a_q_g, a_k_g, a_rel_bias, b_w_in, b_b_in, b_conv_w, b_conv_b, b_ln_g, b_ln_b, mq_g, mk_g, w_mem_kv, w_out, norm2_g, w_gate, w_up, w_down, loss_target, m_norm1_g, m_mem_norm_g, m_a_w_in, m_a_q_g, m_a_k_g, m_a_rel_bias, m_b_w_in, m_b_b_in, m_b_conv_w, m_b_conv_b, m_b_ln_g, m_b_ln_b, m_mq_g, m_mk_g, m_w_mem_kv, m_w_out, m_norm2_g, m_w_gate, m_w_up, m_w_down, v_norm1_g, v_mem_norm_g, v_a_w_in, v_a_q_g, v_a_k_g, v_a_rel_bias, v_b_w_in, v_b_b_in, v_b_conv_w, v_b_conv_b, v_b_ln_g, v_b_ln_b, v_mq_g, v_mk_g, v_w_mem_kv, v_w_out, v_norm2_g, v_w_gate, v_w_up, v_w_down):
    batch, seq, d = x.shape
    mtok = mem.shape[1]
    n = batch * seq
    ax, ay, ac = _place()
    me = 4 * ax + 2 * ay + ac
    core_arr = jnp.reshape(ac, (1,)).astype(jnp.int32)
    chip_arr = jnp.reshape(2 * ax + ay, (1,)).astype(jnp.int32)

    def t_bf16(w):
        return jnp.transpose(w).astype(BF16)

    def after(value, *earlier):
        return lax.optimization_barrier((value, *earlier))[0]

    def gather_mix(l, when, name, collective_id):
        srcs = [w_mem_kv[l].astype(BF16), w_out[l].astype(BF16)]
        if l == 1:
            srcs += [t_bf16(b_w_in[0]), _pack([b_b_in, b_conv_w, b_conv_b, b_ln_g, b_ln_b])]
        return gather_weights([after(srcs[0], *when)] + srcs[1:], name, collective_id)

    def gather_ffn(l, when, name, collective_id):
        return gather_weights(
            [after(t_bf16(w_gate[l]), *when), t_bf16(w_up[l]), w_down[l].astype(BF16)], name, collective_id)

    f_loc = b_b_in.shape[1]
    c_loc = b_conv_b.shape[1]

    def two(g):
        return jnp.concatenate([g, g], axis=-1)

    gq2, gk2 = two(a_q_g), two(a_k_g)
    rel16 = jnp.pad(a_rel_bias[0], ((0, 16 - a_rel_bias.shape[1]), (0, 0)))
    bias = bias_blocks(rel16)

    x0 = x.reshape(n, d)
    mem2 = mem.reshape(batch * mtok, d)

    saved = []
    xin = x0
    a_win_t, = gather_weights([t_bf16(a_w_in[0])], "gather_in_a", 1)
    wg_t, wu_t, wd, wo, wkv = [None] * 2, [None] * 2, [None] * 2, [None] * 2, [None] * 2
    h = after(rms_fwd(xin, norm1_g[0:1], name="rms1_fwd_0"), bias)
    target = loss_target.reshape(n, d)
    for l in range(2):
        gq4 = jnp.tile(mq_g[l:l + 1], (1, 4))
        gk4 = jnp.tile(mk_g[l:l + 1], (1, 4))
        y_conv = None
        if l == 0:
            wkv[0], wo[0] = gather_mix(0, (h, a_win_t), "gather_mix_a", 2)
            z = mm_nt(h, a_win_t, name="in_proj_a")
            wg_t[0], wu_t[0], wd[0] = gather_ffn(0, (z, wkv[0]), "gather_ffn_a", 3)
            cat = attn_fwd(z, gq2, gk2, bias, batch, seq)
            wkv[1], wo[1], b_win_t, conv_slabs = gather_mix(1, (cat, wg_t[0]), "gather_mix_b", 4)
            qcol = 3 * TOK_WIDTH // MEM_WIDTH
        else:
            small_shapes = [(f_loc,), (CONV_W, c_loc), (c_loc,), (c_loc,), (c_loc,)]
            bb_g, cw_g, cb_g, lg_g, lb_g = _unpack(conv_slabs.reshape(N_DEV, -1, LANES), small_shapes)
            bb_full = bb_g.reshape(1, -1)
            cw_full = jnp.pad(jnp.transpose(cw_g, (1, 0, 2)).reshape(CONV_W, -1), ((0, 32 - CONV_W), (0, 0)))
            cb_full, lg_full, lb_full = cb_g.reshape(1, -1), lg_g.reshape(1, -1), lb_g.reshape(1, -1)
            z = mm_nt(h, b_win_t, bias=bb_full, name="in_proj_b")
            cat, y_conv = conv_fwd(z, cw_full, cb_full, lg_full, lb_full, batch, seq)
            qcol = 2 * TOK_WIDTH // MEM_WIDTH
        mem_n, kv = mem_prep(mem2, mem_norm_g[l:l + 1], wkv[l], name=f"mem_prep_{l}")
        cat = memattn_fwd(z, kv, gq4, gk4, cat, batch, seq, qcol, name=f"memattn_fwd_{l}")
        x1, h2 = proj_norm(cat, wo[l], xin, norm2_g[l:l + 1], name=f"out_proj_{l}")
        if l == 0:
            wg_t[1], wu_t[1], wd[1] = gather_ffn(1, (x1, b_win_t), "gather_ffn_b", 5)
        if l == 0:
            gate, up, act, x2, h_next = ffn_fwd(h2, wg_t[0], wu_t[0], wd[0], x1, gain=norm1_g[1:2], name="ffn_fwd_0")
        else:
            gate, up, act, dx_b, loss_blk = ffn_fwd(h2, wg_t[1], wu_t[1], wd[1], x1, target=target, name="ffn_fwd_1")
        saved.append(dict(xin=xin, h=h, mem_n=mem_n, kv=kv, gq4=gq4, gk4=gk4, z=z, qcol=qcol, cat=cat, x1=x1, h2=h2,
                          gate=gate, up=up, act=act, y_conv=y_conv))
        if l == 0:
            xin, h = x2, h_next

    big = {}
    small = {}
    reduced = {}
    groups = 0

    def scatter_siblings(keys):
        nonlocal groups
        gid = groups
        groups += 1
        return gid, keys, scatter_to_sibling([big[k] for k in keys], f"scatter_sibling_{gid}", 8 + 2 * gid)

    def scatter_chips(stage1, when):
        gid, keys, landed1 = stage1
        parts = add_sibling([after(big[keys[0]], when)] + [big[k] for k in keys[1:]], landed1, core_arr,
                            name=f"add_sibling_{gid}")
        landed2 = scatter_to_chips(parts, f"scatter_chips_{gid}", 9 + 2 * gid)
        for k, p, ld in zip(keys, parts, landed2):
            reduced[k] = (p, ld)
        return parts, landed2

    def rows_of(w, transposed):
        w = jnp.swapaxes(w, 1, 2) if transposed else w
        return w.reshape(w.shape[0] * w.shape[1], w.shape[2])

    sharded = {
        "win0": (2, True), "win1": (6, True), "wkv": (14, False), "wo": (15, False),
        "wg": (17, True), "wu": (18, True), "wd": (19, False)}
    weights = [norm1_g, mem_norm_g, a_w_in, a_q_g, a_k_g, a_rel_bias, b_w_in, b_b_in, b_conv_w, b_conv_b, b_ln_g,
               b_ln_b, mq_g, mk_g, w_mem_kv, w_out, norm2_g, w_gate, w_up, w_down]
    moms = [m_norm1_g, m_mem_norm_g, m_a_w_in, m_a_q_g, m_a_k_g, m_a_rel_bias, m_b_w_in, m_b_b_in, m_b_conv_w,
            m_b_conv_b, m_b_ln_g, m_b_ln_b, m_mq_g, m_mk_g, m_w_mem_kv, m_w_out, m_norm2_g, m_w_gate, m_w_up, m_w_down]
    vels = [v_norm1_g, v_mem_norm_g, v_a_w_in, v_a_q_g, v_a_k_g, v_a_rel_bias, v_b_w_in, v_b_b_in, v_b_conv_w,
            v_b_conv_b, v_b_ln_g, v_b_ln_b, v_mq_g, v_mk_g, v_w_mem_kv, v_w_out, v_norm2_g, v_w_gate, v_w_up, v_w_down]
    updated = {}

    def update_layer(l, when):
        for key, (idx, transposed) in sharded.items():
            if key in ("win0", "win1"):
                if key != f"win{l}":
                    continue
                layer, rkey = 0, key
            else:
                layer, rkey = l, f"{key}{l}"
            part, landed = reduced[rkey]
            updated[key] = adamw_shard(
                layer, after(rows_of(weights[idx], transposed), when), rows_of(moms[idx], transposed),
                rows_of(vels[idx], transposed), part, landed, chip_arr, updated.get(key), name=f"adamw_{rkey}")

    mix_landed = None
    for l in (1, 0):
        sv = saved[l]
        dgate, dup, dx1_b, dcat, small[f"norm2_{l}"] = ffn_bwd(
            dx_b, wd[l], sv["gate"], sv["up"], wg_t[l], wu_t[l], sv["x1"], norm2_g[l:l + 1], wo[l], name=f"ffn_bwd_{l}")
        if l == 0:
            dgate = after(dgate, *mix_landed)
            update_layer(1, dx1_b)
        big[f"wg{l}"], big[f"wu{l}"], big[f"wd{l}"] = ffn_weight_grads(
            dgate, dup, sv["h2"], sv["act"], dx_b, name=f"grad_ffn_{l}")
        stage1 = scatter_siblings([f"wd{l}", f"wg{l}", f"wu{l}"])
        big[f"wo{l}"] = mm_tn(sv["cat"], dx1_b, name=f"grad_wo_{l}")
        parts, ffn_landed = scatter_chips(stage1, big[f"wo{l}"])
        dcat = after(dcat, *parts)
        if l == 0:
            dz, dbias, small["a_q"], small["a_k"] = attn_bwd(sv["z"], dcat, gq2, gk2, bias, batch, seq)
            small["rel"] = bias_grad(dbias)
            win_t = a_win_t
        else:
            dz, small["cw"], small["csum"] = conv_bwd(sv["z"], sv["y_conv"], dcat, cw_full, lg_full, lb_full, batch, seq)
            win_t = b_win_t
        dz = after(dz, *ffn_landed)
        dz, dkv, small[f"mq_{l}"], small[f"mk_{l}"] = memattn_bwd(
            sv["z"], sv["kv"], dcat, sv["gq4"], sv["gk4"], dz, batch, seq, sv["qcol"], name=f"memattn_bwd_{l}")
        big[f"win{l}"] = mm_tn(dz, sv["h"], name=f"grad_win_{l}")
        big[f"wkv{l}"] = mm_tn(sv["mem_n"], dkv, name=f"grad_wkv_{l}")
        stage1 = scatter_siblings([f"win{l}", f"wkv{l}", f"wo{l}"])
        dx_b, small[f"norm1_{l}"], dz_sum = in_proj_bwd(
            dz, win_t, sv["xin"], norm1_g[l:l + 1], dx1_b, BF16 if l == 1 else F32, name=f"in_proj_bwd_{l}")
        if l == 1:
            small["bb"] = dz_sum
        parts, mix_landed = scatter_chips(stage1, dx_b)
        dx_b = after(dx_b, *parts)
        small[f"memnorm_{l}"] = mem_norm_grad(dkv, wkv[l], mem2, name=f"mem_norm_grad_{l}")
    grad_x = dx_b.reshape(batch, seq, d)
    update_layer(0, dx_b)

    def shaped(rows, idx, transposed):
        shp = weights[idx].shape
        if transposed:
            return jnp.swapaxes(rows.reshape(shp[0], shp[2], shp[1]), 1, 2)
        return rows.reshape(shp)

    def fold(v, groups):
        return jnp.sum(v.reshape(groups, HEAD_DIM), axis=0, keepdims=True)

    heads = a_rel_bias.shape[1]
    small_list = [
        jnp.concatenate([small["norm1_0"], small["norm1_1"]]),
        jnp.concatenate([small["memnorm_0"], small["memnorm_1"]]),
        fold(small["a_q"], 2), fold(small["a_k"], 2), small["rel"][:heads][None],
        small["bb"], small["cw"][:CONV_W][None], small["csum"][0:1], small["csum"][1:2], small["csum"][2:3],
        jnp.concatenate([fold(small["mq_0"], 4), fold(small["mq_1"], 4)]),
        jnp.concatenate([fold(small["mk_0"], 4), fold(small["mk_1"], 4)]),
        jnp.concatenate([small["norm2_0"], small["norm2_1"]]),
    ]
    (g_norm1, g_memnorm, g_aq, g_ak, g_rel, g_bb_full, g_cw_full, g_cb_full, g_lg_full, g_lb_full,
     g_mq, g_mk, g_norm2, loss_sum) = reduce_small(small_list + [loss_blk])
    loss = loss_sum[0, 0]
    g_bb = lax.dynamic_slice_in_dim(g_bb_full, me * f_loc, f_loc, axis=1)
    g_cw = lax.dynamic_slice_in_dim(g_cw_full, me * c_loc, c_loc, axis=2)
    g_cb = lax.dynamic_slice_in_dim(g_cb_full, me * c_loc, c_loc, axis=1)
    g_lg = lax.dynamic_slice_in_dim(g_lg_full, me * c_loc, c_loc, axis=1)
    g_lb = lax.dynamic_slice_in_dim(g_lb_full, me * c_loc, c_loc, axis=1)

    grads = [g_norm1, g_memnorm, None, g_aq, g_ak, g_rel, None, g_bb, g_cw, g_cb, g_lg, g_lb,
             g_mq, g_mk, None, None, g_norm2, None, None, None]
    deltas, new_m, new_v = [None] * 20, [None] * 20, [None] * 20
    for key, (idx, transposed) in sharded.items():
        grads[idx], deltas[idx], new_m[idx], new_v[idx] = (shaped(r, idx, transposed) for r in updated[key])

    small_idx = [i for i in range(20) if i not in {idx for idx, _ in sharded.values()}]
    dl, nm, nv = adamw_small([weights[i] for i in small_idx], [grads[i] for i in small_idx],
                             [moms[i] for i in small_idx], [vels[i] for i in small_idx])
    for i, a, b, cc in zip(small_idx, dl, nm, nv):
        deltas[i], new_m[i], new_v[i] = a, b, cc

    return (loss, grad_x, *grads, *deltas, *new_m, *new_v)
```

```python
import jax
import jax.numpy as jnp
from jax import lax
from jax.experimental import pallas as pl
from jax.experimental.pallas import tpu as pltpu
from jax.experimental.pallas import tpu_sc as plsc

F32 = jnp.float32
BF16 = jnp.bfloat16
HIGHEST = lax.Precision.HIGHEST
MESH = pl.DeviceIdType.MESH
ANY = pl.BlockSpec(memory_space=pl.ANY)

N_DEV = 8
D_MODEL = 1024
HEAD_DIM = 64
TOK_WIDTH = 768
MEM_WIDTH = 256
CHUNK = 64
Q_BLOCK = 256
KEY_WIN = 768
BAND = 576
N_REL = 192
CONV_W = 31
CONV_HALO = 32
NORM_EPS = 1e-6
NEG_INF = -1e30
ATTN_SCALE = HEAD_DIM ** -0.5
LANES = 128
ROW_TILE = 512
VMEM_LIMIT = 56 * 1024 * 1024

ADAM_LR, ADAM_B1, ADAM_B2, ADAM_EPS, ADAM_WD, ADAM_STEP = 0.001, 0.9, 0.999, 1e-08, 0.01, 10


def _params(*sem):
    return pltpu.CompilerParams(dimension_semantics=sem, vmem_limit_bytes=VMEM_LIMIT)


WIDE_ROW_TILE = 1024


def _row_tile(m, rows=ROW_TILE):
    return rows if m % rows == 0 else m


def _col_tile(n, cap=1408):
    best = None
    for t in range(LANES, min(n, cap) + 1, LANES):
        if n % t == 0:
            best = t
    return best if best is not None else n


def _dot(a, b, ca, cb):
    return lax.dot_general(a, b, (((ca,), (cb,)), ((), ())), preferred_element_type=F32)


def _sigmoid(x):
    return 0.5 * jnp.tanh(0.5 * x) + 0.5


def mm_nt(a, b, bias=None, out_dtype=BF16, name="mm_nt"):
    m, k = a.shape
    n = b.shape[0]
    tm, tn = _row_tile(m, WIDE_ROW_TILE), _col_tile(n)

    def body(*refs):
        a_ref, b_ref = refs[0], refs[1]
        o_ref = refs[-1]
        acc = _dot(a_ref[...].astype(BF16), b_ref[...].astype(BF16), 1, 1)
        if bias is not None:
            acc = acc + refs[2][...]
        o_ref[...] = acc.astype(o_ref.dtype)

    in_specs = [pl.BlockSpec((tm, k), lambda j, i: (i, 0)), pl.BlockSpec((tn, k), lambda j, i: (j, 0))]
    args = [a, b]
    if bias is not None:
        in_specs.append(pl.BlockSpec((1, tn), lambda j, i: (0, j)))
        args.append(bias)
    return pl.pallas_call(
        body, out_shape=jax.ShapeDtypeStruct((m, n), out_dtype), grid=(n // tn, m // tm),
        in_specs=in_specs, out_specs=pl.BlockSpec((tm, tn), lambda j, i: (i, j)),
        compiler_params=_params("parallel", "arbitrary"), name=name)(*args)


def mm_tn(a, b, out_dtype=BF16, name="mm_tn"):
    t, r = a.shape
    c = b.shape[1]
    tr = _col_tile(r, 512)

    def body(a_ref, b_ref, o_ref):
        o_ref[...] = _dot(a_ref[...].astype(BF16), b_ref[...].astype(BF16), 0, 0).astype(o_ref.dtype)

    return pl.pallas_call(
        body, out_shape=jax.ShapeDtypeStruct((r, c), out_dtype), grid=(r // tr,),
        in_specs=[pl.BlockSpec((t, tr), lambda i: (0, i)), pl.BlockSpec((t, c), lambda i: (0, 0))],
        out_specs=pl.BlockSpec((tr, c), lambda i: (i, 0)),
        compiler_params=_params("parallel"), name=name)(a, b)


def _resident(shape):
    return pl.BlockSpec(shape, lambda i: (0, 0), pipeline_mode=pl.Buffered(1))


def proj_norm(a, b, res, gain, name):
    m, k = a.shape
    n = b.shape[1]
    tm = _row_tile(m)

    def body(a_ref, b_ref, res_ref, g_ref, x_ref, h_ref):
        xv = res_ref[...] + _dot(a_ref[...], b_ref[...], 1, 0)
        x_ref[...] = xv
        r = lax.rsqrt(jnp.mean(xv * xv, axis=-1, keepdims=True) + NORM_EPS)
        h_ref[...] = (xv * r * g_ref[...]).astype(BF16)

    row = pl.BlockSpec((tm, n), lambda i: (i, 0))
    return pl.pallas_call(
        body, out_shape=(jax.ShapeDtypeStruct((m, n), F32), jax.ShapeDtypeStruct((m, n), BF16)), grid=(m // tm,),
        in_specs=[pl.BlockSpec((tm, k), lambda i: (i, 0)), _resident((k, n)), row, _resident((1, n))],
        out_specs=(row, row), compiler_params=_params("parallel"), name=name)(a, b, res, gain)


def in_proj_bwd(dz, w_t, x, gain, dres, out_dtype, name):
    m, n = x.shape
    k = dz.shape[1]
    tm = _row_tile(m)

    def body(dz_ref, w_ref, x_ref, g_ref, dres_ref, dx_ref, dg_ref, cs_ref):
        @pl.when(pl.program_id(0) == 0)
        def _():
            dg_ref[...] = jnp.zeros_like(dg_ref)
            cs_ref[...] = jnp.zeros_like(cs_ref)

        dzv = dz_ref[...]
        cs_ref[...] += jnp.sum(dzv.astype(F32), axis=0, keepdims=True)
        dhv = _dot(dzv, w_ref[...], 1, 0)
        xv = x_ref[...]
        r = lax.rsqrt(jnp.mean(xv * xv, axis=-1, keepdims=True) + NORM_EPS)
        xhat = xv * r
        dg_ref[...] += jnp.sum(dhv * xhat, axis=0, keepdims=True)
        dxhat = dhv * g_ref[...]
        dx = dres_ref[...].astype(F32) + r * (dxhat - xhat * jnp.mean(dxhat * xhat, axis=-1, keepdims=True))
        dx_ref[...] = dx.astype(dx_ref.dtype)

    row = pl.BlockSpec((tm, n), lambda i: (i, 0))
    return pl.pallas_call(
        body, out_shape=(jax.ShapeDtypeStruct((m, n), out_dtype), jax.ShapeDtypeStruct((1, n), F32),
                         jax.ShapeDtypeStruct((1, k), F32)), grid=(m // tm,),
        in_specs=[pl.BlockSpec((tm, k), lambda i: (i, 0)), _resident(w_t.shape), row, _resident((1, n)), row],
        out_specs=(row, pl.BlockSpec((1, n), lambda i: (0, 0)), pl.BlockSpec((1, k), lambda i: (0, 0))),
        compiler_params=_params("arbitrary"), name=name)(dz, w_t, x, gain, dres)


FFN_ROWS = 256


def _ffn_row_tile(m):
    return FFN_ROWS if m % FFN_ROWS == 0 else m


def ffn_fwd(h2, wg_t, wu_t, wd, x1, gain=None, target=None, name="ffn_fwd"):
    n, d = h2.shape
    f = wg_t.shape[0]
    tm = _ffn_row_tile(n)
    nt = n // tm
    last = target is not None

    def body(h_ref, wg_ref, wu_ref, wd_ref, x1_ref, e_ref, g_ref, u_ref, a_ref, *rest):
        hv = h_ref[...]
        gv = _dot(hv, wg_ref[...], 1, 1)
        uv = _dot(hv, wu_ref[...], 1, 1)
        g_ref[...] = gv.astype(BF16)
        u_ref[...] = uv.astype(BF16)
        av = (gv * _sigmoid(gv) * uv).astype(BF16)
        a_ref[...] = av
        xv = x1_ref[...] + _dot(av, wd_ref[...], 1, 0)
        if not last:
            x_ref, hn_ref = rest
            x_ref[...] = xv
            r = lax.rsqrt(jnp.mean(xv * xv, axis=-1, keepdims=True) + NORM_EPS)
            hn_ref[...] = (xv * r * e_ref[...]).astype(BF16)
        else:
            dyb_ref, l_ref, acc_ref = rest
            i = pl.program_id(0)

            @pl.when(i == 0)
            def _():
                acc_ref[...] = jnp.zeros_like(acc_ref)

            err = xv - e_ref[...]
            dyb_ref[...] = (err * (1.0 / d)).astype(BF16)
            acc_ref[...] += jnp.sum(err * err, axis=0, keepdims=True)

            @pl.when(i == nt - 1)
            def _():
                total = jnp.sum(acc_ref[...], axis=-1, keepdims=True) * (0.5 / d)
                l_ref[...] = jnp.broadcast_to(total, l_ref.shape)

    row_d = pl.BlockSpec((tm, d), lambda i: (i, 0))
    row_f = pl.BlockSpec((tm, f), lambda i: (i, 0))
    act_shape = jax.ShapeDtypeStruct((n, f), BF16)
    if not last:
        extra_in, extra = _resident((1, d)), gain
        out_shape = (act_shape, act_shape, act_shape, jax.ShapeDtypeStruct((n, d), F32), jax.ShapeDtypeStruct((n, d), BF16))
        out_specs = (row_f, row_f, row_f, row_d, row_d)
        scratch = []
    else:
        extra_in, extra = row_d, target
        out_shape = (act_shape, act_shape, act_shape, jax.ShapeDtypeStruct((n, d), BF16),
                     jax.ShapeDtypeStruct((8, LANES), F32))
        out_specs = (row_f, row_f, row_f, row_d, pl.BlockSpec((8, LANES), lambda i: (0, 0)))
        scratch = [pltpu.VMEM((1, d), F32)]
    return pl.pallas_call(
        body, out_shape=out_shape, grid=(nt,),
        in_specs=[row_d, _resident((f, d)), _resident((f, d)), _resident((f, d)), row_d, extra_in],
        out_specs=out_specs, scratch_shapes=scratch,
        compiler_params=_params("arbitrary"), name=name)(h2, wg_t, wu_t, wd, x1, extra)


def ffn_bwd(dx_b, wd, gate, up, wg_t, wu_t, x1, gain, wo, name="ffn_bwd"):
    n, d = x1.shape
    f = wd.shape[0]
    tm = _ffn_row_tile(n)

    def body(dxb_ref, wd_ref, g_ref, u_ref, wg_ref, wu_ref, x_ref, gain_ref, wo_ref,
             dg_ref, du_ref, dxo_ref, dc_ref, dgain_ref):
        @pl.when(pl.program_id(0) == 0)
        def _():
            dgain_ref[...] = jnp.zeros_like(dgain_ref)

        dact = _dot(dxb_ref[...], wd_ref[...], 1, 1)
        gv = g_ref[...].astype(F32)
        uv = u_ref[...].astype(F32)
        sg = _sigmoid(gv)
        dgv = (dact * uv * sg * (1.0 + gv * (1.0 - sg))).astype(BF16)
        duv = (dact * gv * sg).astype(BF16)
        dg_ref[...] = dgv
        du_ref[...] = duv
        dhv = _dot(dgv, wg_ref[...], 1, 0) + _dot(duv, wu_ref[...], 1, 0)
        xv = x_ref[...]
        r = lax.rsqrt(jnp.mean(xv * xv, axis=-1, keepdims=True) + NORM_EPS)
        xhat = xv * r
        dgain_ref[...] += jnp.sum(dhv * xhat, axis=0, keepdims=True)
        dxhat = dhv * gain_ref[...]
        dxb = (dxb_ref[...].astype(F32) + r * (dxhat - xhat * jnp.mean(dxhat * xhat, axis=-1, keepdims=True))).astype(BF16)
        dxo_ref[...] = dxb
        dc_ref[...] = _dot(dxb, wo_ref[...], 1, 1).astype(BF16)

    row_d = pl.BlockSpec((tm, d), lambda i: (i, 0))
    row_f = pl.BlockSpec((tm, f), lambda i: (i, 0))
    w_spec = _resident((f, d))
    act_shape = jax.ShapeDtypeStruct((n, f), BF16)
    row_shape = jax.ShapeDtypeStruct((n, d), BF16)
    return pl.pallas_call(
        body, out_shape=(act_shape, act_shape, row_shape, jax.ShapeDtypeStruct((n, wo.shape[0]), BF16),
                         jax.ShapeDtypeStruct((1, d), F32)),
        grid=(n // tm,),
        in_specs=[row_d, w_spec, row_f, row_f, w_spec, w_spec, row_d, _resident((1, d)), _resident(wo.shape)],
        out_specs=(row_f, row_f, row_d, pl.BlockSpec((tm, wo.shape[0]), lambda i: (i, 0)),
                   pl.BlockSpec((1, d), lambda i: (0, 0))),
        compiler_params=_params("arbitrary"), name=name)(dx_b, wd, gate, up, wg_t, wu_t, x1, gain, wo)


def ffn_weight_grads(dgate, dup, h2, act, dx_b, name="ffn_weight_grads"):
    t, r = dgate.shape
    c = h2.shape[1]
    tr = _col_tile(r, 512)

    def body(a1_ref, a2_ref, a3_ref, b12_ref, b3_ref, o1_ref, o2_ref, o3_ref):
        bv = b12_ref[...]
        o1_ref[...] = _dot(a1_ref[...], bv, 0, 0).astype(o1_ref.dtype)
        o2_ref[...] = _dot(a2_ref[...], bv, 0, 0).astype(o2_ref.dtype)
        o3_ref[...] = _dot(a3_ref[...], b3_ref[...], 0, 0).astype(o3_ref.dtype)

    a_spec = pl.BlockSpec((t, tr), lambda i: (0, i))
    o_spec = pl.BlockSpec((tr, c), lambda i: (i, 0))
    shape = jax.ShapeDtypeStruct((r, c), BF16)
    return pl.pallas_call(
        body, out_shape=(shape, shape, shape), grid=(r // tr,),
        in_specs=[a_spec, a_spec, a_spec, _resident((t, c)), _resident((t, c))],
        out_specs=(o_spec, o_spec, o_spec), compiler_params=_params("parallel"), name=name)(dgate, dup, act, h2, dx_b)


def rms_fwd(x, g, name="rms_fwd"):
    n, d = x.shape
    tm = _row_tile(n)

    def body(x_ref, g_ref, o_ref):
        xv = x_ref[...]
        r = lax.rsqrt(jnp.mean(xv * xv, axis=-1, keepdims=True) + NORM_EPS)
        o_ref[...] = (xv * r * g_ref[...]).astype(o_ref.dtype)

    return pl.pallas_call(
        body, out_shape=jax.ShapeDtypeStruct((n, d), BF16), grid=(n // tm,),
        in_specs=[pl.BlockSpec((tm, d), lambda i: (i, 0)), pl.BlockSpec((1, d), lambda i: (0, 0))],
        out_specs=pl.BlockSpec((tm, d), lambda i: (i, 0)),
        compiler_params=_params("parallel"), name=name)(x, g)


def _group_masks(width):
    lane = lax.broadcasted_iota(jnp.int32, (1, width), 1)
    return [(lane >= HEAD_DIM * g) & (lane < HEAD_DIM * (g + 1)) for g in range(width // HEAD_DIM)]


def _group_sum(x, masks):
    out = jnp.zeros_like(x)
    for msk in masks:
        s = jnp.sum(jnp.where(msk, x, 0.0), axis=-1, keepdims=True)
        out = jnp.where(msk, s, out)
    return out


def _head_norm(x, gain, masks):
    r = lax.rsqrt(_group_sum(x * x, masks) * (1.0 / HEAD_DIM) + NORM_EPS)
    xhat = x * r
    return xhat * gain, xhat, r


def _head_norm_bwd(dxn, xhat, r, gain, masks):
    dgain = jnp.sum(dxn * xhat, axis=0, keepdims=True)
    dxhat = dxn * gain
    mean_t = _group_sum(dxhat * xhat, masks) * (1.0 / HEAD_DIM)
    return r * (dxhat - xhat * mean_t), dgain


def _softmax_rows(s):
    e = jnp.exp(s - jnp.max(s, axis=-1, keepdims=True))
    return e * (1.0 / jnp.sum(e, axis=-1, keepdims=True))


def _rel_onehot():
    col = lax.broadcasted_iota(jnp.int32, (1, KEY_WIN), 1)
    off = jnp.where(col < KEY_WIN - LANES, col, col - KEY_WIN)
    idx = jnp.clip(8 * CHUNK - off, -(CHUNK - 1), LANES) + (CHUNK - 1)
    return (lax.broadcasted_iota(jnp.int32, (N_REL, KEY_WIN), 0) == idx).astype(F32)


def bias_blocks(rel16):
    heads = TOK_WIDTH // HEAD_DIM

    def body(rel_ref, o_ref, u_ref):
        u_ref[...] = jnp.dot(rel_ref[...], _rel_onehot(), precision=HIGHEST, preferred_element_type=F32)
        row = lax.broadcasted_iota(jnp.int32, (CHUNK, KEY_WIN), 0)
        col = lax.broadcasted_iota(jnp.int32, (CHUNK, KEY_WIN), 1)
        for h in range(heads):
            xv = jnp.broadcast_to(u_ref[h:h + 1, :], (CHUNK, KEY_WIN))
            for b in range(6):
                xv = jnp.where(((row >> b) & 1) == 1, pltpu.roll(xv, 1 << b, axis=1), xv)
            xv = jnp.where(col < BAND, xv, NEG_INF)
            for i in range(Q_BLOCK // CHUNK):
                o_ref[h, CHUNK * i:CHUNK * (i + 1), :] = pltpu.roll(xv, CHUNK * i, axis=1) if i else xv

    return pl.pallas_call(
        body, out_shape=jax.ShapeDtypeStruct((heads, Q_BLOCK, KEY_WIN), F32),
        scratch_shapes=[pltpu.VMEM((16, KEY_WIN), F32)], name="bias_blocks")(rel16)


def bias_grad(dbias):
    heads = dbias.shape[0]

    def body(db_ref, o_ref, y_ref):
        y_ref[...] = jnp.zeros_like(y_ref)
        row = lax.broadcasted_iota(jnp.int32, (CHUNK, KEY_WIN), 0)
        for h in range(heads):
            fv = db_ref[h, 0:CHUNK, :]
            for i in range(1, Q_BLOCK // CHUNK):
                fv = fv + pltpu.roll(db_ref[h, CHUNK * i:CHUNK * (i + 1), :], KEY_WIN - CHUNK * i, axis=1)
            for b in range(6):
                fv = jnp.where(((row >> b) & 1) == 1, pltpu.roll(fv, KEY_WIN - (1 << b), axis=1), fv)
            y_ref[h:h + 1, :] = jnp.sum(fv, axis=0, keepdims=True)
        o_ref[...] = lax.dot_general(y_ref[...], _rel_onehot(), (((1,), (1,)), ((), ())),
                                     precision=HIGHEST, preferred_element_type=F32)

    return pl.pallas_call(
        body, out_shape=jax.ShapeDtypeStruct((16, N_REL), F32),
        scratch_shapes=[pltpu.VMEM((16, KEY_WIN), F32)], name="bias_grad")(dbias)


def _attn_windows(seq):
    out = []
    for j in range(seq // Q_BLOCK):
        r0 = j * Q_BLOCK
        k0 = max(0, r0 - 8 * CHUNK)
        width = r0 + Q_BLOCK - k0
        out.append((r0, k0, width, KEY_WIN - width))
    return out


def attn_fwd(z, gq2, gk2, bias, batch, seq):
    n = z.shape[0]
    pairs = TOK_WIDTH // LANES

    def body(q_ref, k_ref, v_ref, gq_ref, gk_ref, b_ref, o_ref, qs_s, kn_s):
        masks = _group_masks(LANES)
        qs_s[...] = (_head_norm(q_ref[...].astype(F32), gq_ref[...], masks)[0] * ATTN_SCALE).astype(BF16)
        kn_s[...] = _head_norm(k_ref[...].astype(F32), gk_ref[...], masks)[0].astype(BF16)
        for r0, k0, width, c0 in _attn_windows(seq):
            qb = qs_s[r0:r0 + Q_BLOCK, :]
            kw = kn_s[k0:k0 + width, :]
            vw = v_ref[k0:k0 + width, :]
            out = jnp.zeros((Q_BLOCK, LANES), F32)
            for h, msk in enumerate(masks):
                qh = jnp.where(msk, qb, jnp.zeros_like(qb))
                s = _dot(qh, kw, 1, 1) + b_ref[h, :, c0:KEY_WIN]
                p = _softmax_rows(s).astype(BF16)
                out = jnp.where(msk, _dot(p, vw, 1, 0), out)
            o_ref[r0:r0 + Q_BLOCK, :] = out.astype(o_ref.dtype)

    def col(off):
        return pl.BlockSpec((seq, LANES), lambda b, p: (b, off + p))

    vec = pl.BlockSpec((1, LANES), lambda b, p: (0, 0))
    return pl.pallas_call(
        body, out_shape=jax.ShapeDtypeStruct((n, D_MODEL), BF16), grid=(batch, pairs),
        in_specs=[col(0), col(pairs), col(2 * pairs), vec, vec,
                  pl.BlockSpec((2, Q_BLOCK, KEY_WIN), lambda b, p: (p, 0, 0))],
        out_specs=pl.BlockSpec((seq, LANES), lambda b, p: (b, p)),
        scratch_shapes=[pltpu.VMEM((seq, LANES), BF16), pltpu.VMEM((seq, LANES), BF16)],
        compiler_params=_params("parallel", "arbitrary"), name="attn_fwd")(z, z, z, gq2, gk2, bias)


def attn_bwd(z, dcat, gq2, gk2, bias, batch, seq):
    n = z.shape[0]
    pairs = TOK_WIDTH // LANES

    def body(q_ref, k_ref, v_ref, do_ref, gq_ref, gk_ref, b_ref,
             dz_ref, db_ref, dgq_ref, dgk_ref, qs_s, kn_s, dqn_s, dkn_s, dv_s, dk_o, dv_o):
        pi, bi, which = pl.program_id(0), pl.program_id(1), pl.program_id(2)

        @pl.when(which == 0)
        def _():
            masks = _group_masks(LANES)

            @pl.when(bi == 0)
            def _():
                db_ref[...] = jnp.zeros_like(db_ref)

            @pl.when((bi == 0) & (pi == 0))
            def _():
                dgq_ref[...] = jnp.zeros_like(dgq_ref)
                dgk_ref[...] = jnp.zeros_like(dgk_ref)

            qn, qhat, rq = _head_norm(q_ref[...].astype(F32), gq_ref[...], masks)
            kn, khat, rk = _head_norm(k_ref[...].astype(F32), gk_ref[...], masks)
            qs_s[...] = (qn * ATTN_SCALE).astype(BF16)
            kn_s[...] = kn.astype(BF16)
            dkn_s[...] = jnp.zeros_like(dkn_s)
            dv_s[...] = jnp.zeros_like(dv_s)
            for r0, k0, width, c0 in _attn_windows(seq):
                qb = qs_s[r0:r0 + Q_BLOCK, :]
                dob = do_ref[r0:r0 + Q_BLOCK, :]
                kw = kn_s[k0:k0 + width, :]
                vw = v_ref[k0:k0 + width, :]
                dq_acc = jnp.zeros((Q_BLOCK, LANES), F32)
                dk_acc = jnp.zeros((width, LANES), F32)
                dv_acc = jnp.zeros((width, LANES), F32)
                for h, msk in enumerate(masks):
                    qh = jnp.where(msk, qb, jnp.zeros_like(qb))
                    doh = jnp.where(msk, dob, jnp.zeros_like(dob))
                    p = _softmax_rows(_dot(qh, kw, 1, 1) + b_ref[h, :, c0:KEY_WIN])
                    dp = _dot(doh, vw, 1, 1)
                    ds = p * (dp - jnp.sum(p * dp, axis=-1, keepdims=True))
                    db_ref[h, :, c0:KEY_WIN] += ds
                    dsb = ds.astype(BF16)
                    dq_acc = jnp.where(msk, _dot(dsb, kw, 1, 0), dq_acc)
                    dk_acc = jnp.where(msk, _dot(dsb, qb, 0, 0), dk_acc)
                    dv_acc = jnp.where(msk, _dot(p.astype(BF16), dob, 0, 0), dv_acc)
                dqn_s[r0:r0 + Q_BLOCK, :] = dq_acc * ATTN_SCALE
                dkn_s[k0:k0 + width, :] += dk_acc
                dv_s[k0:k0 + width, :] += dv_acc
            dq, dgq = _head_norm_bwd(dqn_s[...], qhat, rq, gq_ref[...], masks)
            dk, dgk = _head_norm_bwd(dkn_s[...], khat, rk, gk_ref[...], masks)
            dz_ref[...] = dq.astype(dz_ref.dtype)
            dk_o[...] = dk.astype(dk_o.dtype)
            dv_o[...] = dv_s[...].astype(dv_o.dtype)
            dgq_ref[...] += dgq
            dgk_ref[...] += dgk

        @pl.when(which == 1)
        def _():
            dz_ref[...] = dk_o[...]

        @pl.when(which == 2)
        def _():
            dz_ref[...] = dv_o[...]

    def ahead(p, b, t):
        nb = b + jnp.where(t > 0, 1, 0)
        wrap = jnp.where(nb >= batch, 1, 0)
        return jnp.minimum(p + wrap, pairs - 1), nb - wrap * batch

    def col(off):
        def index(p, b, t):
            np_, nb = ahead(p, b, t)
            return nb, off + np_
        return pl.BlockSpec((seq, LANES), index)

    vec = pl.BlockSpec((1, LANES), lambda p, b, t: (0, 0))
    blk = pl.BlockSpec((2, Q_BLOCK, KEY_WIN), lambda p, b, t: (p, 0, 0))
    blk_in = pl.BlockSpec((2, Q_BLOCK, KEY_WIN), lambda p, b, t: (ahead(p, b, t)[0], 0, 0))
    v_shape = jax.ShapeDtypeStruct((1, LANES), F32)
    return pl.pallas_call(
        body,
        out_shape=(jax.ShapeDtypeStruct(z.shape, BF16), jax.ShapeDtypeStruct(bias.shape, F32), v_shape, v_shape),
        grid=(pairs, batch, 3),
        in_specs=[col(0), col(pairs), col(2 * pairs), col(0), vec, vec, blk_in],
        out_specs=(pl.BlockSpec((seq, LANES), lambda p, b, t: (b, t * pairs + p)), blk, vec, vec),
        scratch_shapes=[pltpu.VMEM((seq, LANES), BF16), pltpu.VMEM((seq, LANES), BF16),
                        pltpu.VMEM((seq, LANES), F32), pltpu.VMEM((seq, LANES), F32), pltpu.VMEM((seq, LANES), F32),
                        pltpu.VMEM((seq, LANES), BF16), pltpu.VMEM((seq, LANES), BF16)],
        compiler_params=_params("arbitrary", "arbitrary", "arbitrary"), name="attn_bwd")(
            z, z, z, dcat, gq2, gk2, bias)


MEM_ROWS = 512


def memattn_fwd(z, kv, gq4, gk4, cat, batch, seq, qcol, name):
    mtok = kv.shape[0] // batch
    rows = min(MEM_ROWS, seq)

    def body(q_ref, kv_ref, gq_ref, gk_ref, cat_ref, o_ref):
        del cat_ref
        masks = _group_masks(MEM_WIDTH)
        kn = _head_norm(kv_ref[:, 0:MEM_WIDTH], gk_ref[...], masks)[0].astype(BF16)
        vm = kv_ref[:, MEM_WIDTH:2 * MEM_WIDTH].astype(BF16)
        for t in range(seq // rows):
            sl = slice(t * rows, (t + 1) * rows)
            qs = (_head_norm(q_ref[sl, :].astype(F32), gq_ref[...], masks)[0] * ATTN_SCALE).astype(BF16)
            out = jnp.zeros((rows, MEM_WIDTH), F32)
            for msk in masks:
                qh = jnp.where(msk, qs, jnp.zeros_like(qs))
                p = _softmax_rows(_dot(qh, kn, 1, 1)).astype(BF16)
                out = jnp.where(msk, _dot(p, vm, 1, 0), out)
            o_ref[sl, :] = out.astype(o_ref.dtype)

    vec = pl.BlockSpec((1, MEM_WIDTH), lambda b: (0, 0))
    return pl.pallas_call(
        body, out_shape=jax.ShapeDtypeStruct(cat.shape, cat.dtype), grid=(batch,),
        in_specs=[pl.BlockSpec((seq, MEM_WIDTH), lambda b: (b, qcol)),
                  pl.BlockSpec((mtok, 2 * MEM_WIDTH), lambda b: (b, 0)), vec, vec, ANY],
        out_specs=pl.BlockSpec((seq, MEM_WIDTH), lambda b: (b, TOK_WIDTH // MEM_WIDTH)),
        input_output_aliases={4: 0},
        compiler_params=_params("parallel"), name=name)(z, kv, gq4, gk4, cat)


def memattn_bwd(z, kv, dcat, gq4, gk4, dz, batch, seq, qcol, name):
    mtok = kv.shape[0] // batch
    rows = min(MEM_ROWS, seq)

    def body(q_ref, kv_ref, do_ref, gq_ref, gk_ref, dz_in_ref, dq_ref, dkv_ref, dgq_ref, dgk_ref):
        del dz_in_ref
        @pl.when(pl.program_id(0) == 0)
        def _():
            dgq_ref[...] = jnp.zeros_like(dgq_ref)
            dgk_ref[...] = jnp.zeros_like(dgk_ref)

        masks = _group_masks(MEM_WIDTH)
        kn_f, khat, rk = _head_norm(kv_ref[:, 0:MEM_WIDTH], gk_ref[...], masks)
        kn = kn_f.astype(BF16)
        vm = kv_ref[:, MEM_WIDTH:2 * MEM_WIDTH].astype(BF16)
        dkn = jnp.zeros((mtok, MEM_WIDTH), F32)
        dvm = jnp.zeros((mtok, MEM_WIDTH), F32)
        dgq = jnp.zeros((1, MEM_WIDTH), F32)
        for t in range(seq // rows):
            sl = slice(t * rows, (t + 1) * rows)
            qn_f, qhat, rq = _head_norm(q_ref[sl, :].astype(F32), gq_ref[...], masks)
            qs = (qn_f * ATTN_SCALE).astype(BF16)
            dob = do_ref[sl, :]
            dqn = jnp.zeros((rows, MEM_WIDTH), F32)
            for msk in masks:
                qh = jnp.where(msk, qs, jnp.zeros_like(qs))
                doh = jnp.where(msk, dob, jnp.zeros_like(dob))
                p = _softmax_rows(_dot(qh, kn, 1, 1))
                dp = _dot(doh, vm, 1, 1)
                ds = p * (dp - jnp.sum(p * dp, axis=-1, keepdims=True))
                dsb = ds.astype(BF16)
                dqn = jnp.where(msk, _dot(dsb, kn, 1, 0), dqn)
                dkn = dkn + jnp.where(msk, _dot(dsb, qs, 0, 0), 0.0)
                dvm = dvm + jnp.where(msk, _dot(p.astype(BF16), dob, 0, 0), 0.0)
            dq, dg = _head_norm_bwd(dqn * ATTN_SCALE, qhat, rq, gq_ref[...], masks)
            dq_ref[sl, :] = dq.astype(dq_ref.dtype)
            dgq = dgq + dg
        dk, dgk = _head_norm_bwd(dkn, khat, rk, gk_ref[...], masks)
        dkv_ref[:, 0:MEM_WIDTH] = dk
        dkv_ref[:, MEM_WIDTH:2 * MEM_WIDTH] = dvm
        dgq_ref[...] += dgq
        dgk_ref[...] += dgk

    vec = pl.BlockSpec((1, MEM_WIDTH), lambda b: (0, 0))
    kv_spec = pl.BlockSpec((mtok, 2 * MEM_WIDTH), lambda b: (b, 0))
    v_shape = jax.ShapeDtypeStruct((1, MEM_WIDTH), F32)
    q_spec = pl.BlockSpec((seq, MEM_WIDTH), lambda b: (b, qcol))
    return pl.pallas_call(
        body,
        out_shape=(jax.ShapeDtypeStruct(dz.shape, dz.dtype), jax.ShapeDtypeStruct(kv.shape, F32), v_shape, v_shape),
        grid=(batch,),
        in_specs=[q_spec, kv_spec, pl.BlockSpec((seq, MEM_WIDTH), lambda b: (b, TOK_WIDTH // MEM_WIDTH)), vec, vec, ANY],
        out_specs=(q_spec, kv_spec, vec, vec),
        input_output_aliases={5: 0},
        compiler_params=_params("arbitrary"), name=name)(z, kv, dcat, gq4, gk4, dz)


def mem_prep(mem, gain, wkv, name):
    t, d = mem.shape

    def body(m_ref, g_ref, w_ref, n_ref, kv_ref):
        mv = m_ref[...]
        r = lax.rsqrt(jnp.mean(mv * mv, axis=-1, keepdims=True) + NORM_EPS)
        nv = (mv * r * g_ref[...]).astype(BF16)
        n_ref[...] = nv
        kv_ref[...] = _dot(nv, w_ref[...], 1, 0)

    vmem = pl.BlockSpec(memory_space=pltpu.VMEM)
    return pl.pallas_call(
        body, out_shape=(jax.ShapeDtypeStruct((t, d), BF16), jax.ShapeDtypeStruct((t, wkv.shape[1]), F32)),
        in_specs=[vmem, vmem, vmem], out_specs=(vmem, vmem),
        compiler_params=pltpu.CompilerParams(vmem_limit_bytes=VMEM_LIMIT), name=name)(mem, gain, wkv)


def mem_norm_grad(dkv, wkv, mem, name):
    t, d = mem.shape

    def body(dkv_ref, w_ref, m_ref, dg_ref):
        dn = _dot(dkv_ref[...].astype(BF16), w_ref[...], 1, 1)
        mv = m_ref[...]
        r = lax.rsqrt(jnp.mean(mv * mv, axis=-1, keepdims=True) + NORM_EPS)
        dg_ref[...] = jnp.sum(dn * (mv * r), axis=0, keepdims=True)

    vmem = pl.BlockSpec(memory_space=pltpu.VMEM)
    return pl.pallas_call(
        body, out_shape=jax.ShapeDtypeStruct((1, d), F32), in_specs=[vmem, vmem, vmem], out_specs=vmem,
        compiler_params=pltpu.CompilerParams(vmem_limit_bytes=VMEM_LIMIT), name=name)(dkv, wkv, mem)


CONV_ROWS = 256


def _glu(a_ref, g_ref):
    return a_ref[...].astype(F32) * _sigmoid(g_ref[...].astype(F32))


def _layer_norm_stats(y):
    mu = jnp.mean(y, axis=-1, keepdims=True)
    yc = y - mu
    rstd = lax.rsqrt(jnp.mean(yc * yc, axis=-1, keepdims=True) + NORM_EPS)
    return yc * rstd, rstd


CONV_WIN = CONV_HALO + CONV_ROWS
SUBLANES = 8
SHIFT_ROWS = CONV_WIN - SUBLANES


def _preshift(win, shifted):
    for s in range(1, SUBLANES):
        shifted[s - 1, :, :] = win[s:s + SHIFT_ROWS, :]


TAP_ROWS = 64
TAP_TILES = [(r0, slice(c0, c0 + LANES)) for c0 in range(0, TOK_WIDTH, LANES) for r0 in range(0, CONV_ROWS, TAP_ROWS)]


def _tap(win, shifted, off, r0, lanes):
    s = off % SUBLANES
    base = off - s + r0
    if s == 0:
        return win[base:base + TAP_ROWS, lanes]
    return shifted[s - 1, base:base + TAP_ROWS, lanes]


def _fold_rows(x):
    return jnp.sum(x.reshape(TAP_ROWS // SUBLANES, SUBLANES, LANES), axis=0)


def conv_fwd(z, cw, cb, lg, lb, batch, seq):
    n = z.shape[0]
    nt = seq // CONV_ROWS
    sub = CONV_ROWS // CONV_HALO
    lead = CONV_HALO - (CONV_W - 1)

    def body(a_ref, g_ref, ap_ref, gp_ref, cw_ref, cb_ref, lg_ref, lb_ref, o_ref, y_ref, win, shifted):
        first = pl.program_id(1) == 0
        win[0:CONV_HALO, :] = jnp.where(first, 0.0, _glu(ap_ref, gp_ref))
        win[CONV_HALO:CONV_WIN, :] = _glu(a_ref, g_ref)
        _preshift(win, shifted)
        for r0, lanes in TAP_TILES:
            acc = jnp.zeros((TAP_ROWS, LANES), F32) + cb_ref[:, lanes]
            for w in range(CONV_W):
                acc = acc + _tap(win, shifted, lead + w, r0, lanes) * cw_ref[w:w + 1, lanes]
            y_ref[r0:r0 + TAP_ROWS, lanes] = acc
        yh, _ = _layer_norm_stats(y_ref[...])
        t = yh * lg_ref[...] + lb_ref[...]
        o_ref[...] = (t * _sigmoid(t)).astype(o_ref.dtype)

    def cur(c):
        return pl.BlockSpec((CONV_ROWS, TOK_WIDTH), lambda b, i: (b * nt + i, c))

    def prev(c):
        return pl.BlockSpec((CONV_HALO, TOK_WIDTH), lambda b, i: (jnp.maximum((b * nt + i) * sub - 1, 0), c))

    vec = pl.BlockSpec((1, TOK_WIDTH), lambda b, i: (0, 0))
    return pl.pallas_call(
        body, out_shape=(jax.ShapeDtypeStruct((n, D_MODEL), BF16), jax.ShapeDtypeStruct((n, TOK_WIDTH), F32)),
        grid=(batch, nt),
        in_specs=[cur(0), cur(1), prev(0), prev(1), pl.BlockSpec((32, TOK_WIDTH), lambda b, i: (0, 0)), vec, vec, vec],
        out_specs=(cur(0), cur(0)),
        scratch_shapes=[pltpu.VMEM((CONV_WIN, TOK_WIDTH), F32), pltpu.VMEM((SUBLANES - 1, SHIFT_ROWS, TOK_WIDTH), F32)],
        compiler_params=_params("parallel", "arbitrary"), name="conv_fwd")(z, z, z, z, cw, cb, lg, lb)


def conv_bwd(z, y, dcat, cw, lg, lb, batch, seq):
    n = z.shape[0]
    nt = seq // CONV_ROWS
    sub = CONV_ROWS // CONV_HALO
    lead = CONV_HALO - (CONV_W - 1)
    last_blk = n // CONV_HALO - 1

    def body(a_ref, g_ref, ap_ref, gp_ref, y_ref, yn_ref, do_ref, don_ref, cw_ref, lg_ref, lb_ref,
             dz_ref, dcw_ref, dsm_ref, win, shifted, dyw, dshifted, dg_o):
        b, i, which = pl.program_id(0), pl.program_id(1), pl.program_id(2)

        @pl.when(which == 0)
        def _():
            first, last = i == 0, i == nt - 1

            @pl.when((b == 0) & (i == 0))
            def _():
                dcw_ref[...] = jnp.zeros_like(dcw_ref)
                dsm_ref[...] = jnp.zeros_like(dsm_ref)

            win[0:CONV_HALO, :] = jnp.where(first, 0.0, _glu(ap_ref, gp_ref))
            win[CONV_HALO:CONV_WIN, :] = _glu(a_ref, g_ref)
            _preshift(win, shifted)
            yv = jnp.concatenate([y_ref[...], yn_ref[...]], axis=0)
            yh, rstd = _layer_norm_stats(yv)
            t = yh * lg_ref[...] + lb_ref[...]
            st = _sigmoid(t)
            dout = jnp.concatenate(
                [do_ref[...].astype(F32), jnp.where(last, 0.0, don_ref[...].astype(F32))], axis=0)
            dt = dout * st * (1.0 + t * (1.0 - st))
            dyh = dt * lg_ref[...]
            dy = rstd * (dyh - jnp.mean(dyh, axis=-1, keepdims=True)
                         - yh * jnp.mean(dyh * yh, axis=-1, keepdims=True))
            dyw[...] = dy
            _preshift(dyw, dshifted)
            dsm_ref[0:1, :] += jnp.sum(dy[0:CONV_ROWS], axis=0, keepdims=True)
            dsm_ref[1:2, :] += jnp.sum((dt * yh)[0:CONV_ROWS], axis=0, keepdims=True)
            dsm_ref[2:3, :] += jnp.sum(dt[0:CONV_ROWS], axis=0, keepdims=True)
            for c0 in range(0, TOK_WIDTH, LANES):
                lanes = slice(c0, c0 + LANES)
                dcw_acc = [jnp.zeros((SUBLANES, LANES), F32) for _ in range(CONV_W)]
                for r0 in range(0, CONV_ROWS, TAP_ROWS):
                    dyt = dyw[r0:r0 + TAP_ROWS, lanes]
                    dglu = jnp.zeros((TAP_ROWS, LANES), F32)
                    for w in range(CONV_W):
                        dcw_acc[w] = dcw_acc[w] + _fold_rows(dyt * _tap(win, shifted, lead + w, r0, lanes))
                        dglu = dglu + _tap(dyw, dshifted, CONV_W - 1 - w, r0, lanes) * cw_ref[w:w + 1, lanes]
                    avt = a_ref[r0:r0 + TAP_ROWS, lanes].astype(F32)
                    sgt = _sigmoid(g_ref[r0:r0 + TAP_ROWS, lanes].astype(F32))
                    dz_ref[r0:r0 + TAP_ROWS, lanes] = (dglu * sgt).astype(dz_ref.dtype)
                    dg_o[r0:r0 + TAP_ROWS, lanes] = (dglu * avt * sgt * (1.0 - sgt)).astype(dg_o.dtype)
                for w in range(CONV_W):
                    dcw_ref[w:w + 1, lanes] += jnp.sum(dcw_acc[w], axis=0, keepdims=True)

        @pl.when(which == 1)
        def _():
            dz_ref[...] = dg_o[...]

    def ahead(b, i, t):
        return jnp.minimum(b * nt + i + t, batch * nt - 1)

    def cur(c):
        return pl.BlockSpec((CONV_ROWS, TOK_WIDTH), lambda b, i, t: (ahead(b, i, t), c))

    def prev(c):
        return pl.BlockSpec((CONV_HALO, TOK_WIDTH), lambda b, i, t: (jnp.maximum(ahead(b, i, t) * sub - 1, 0), c))

    nxt = pl.BlockSpec((CONV_HALO, TOK_WIDTH),
                       lambda b, i, t: (jnp.minimum((ahead(b, i, t) + 1) * sub, last_blk), 0))
    vec = pl.BlockSpec((1, TOK_WIDTH), lambda b, i, t: (0, 0))
    full32 = pl.BlockSpec((32, TOK_WIDTH), lambda b, i, t: (0, 0))
    return pl.pallas_call(
        body,
        out_shape=(jax.ShapeDtypeStruct(z.shape, BF16), jax.ShapeDtypeStruct((32, TOK_WIDTH), F32),
                   jax.ShapeDtypeStruct((8, TOK_WIDTH), F32)),
        grid=(batch, nt, 2),
        in_specs=[cur(0), cur(1), prev(0), prev(1), cur(0), nxt, cur(0), nxt, full32, vec, vec],
        out_specs=(pl.BlockSpec((CONV_ROWS, TOK_WIDTH), lambda b, i, t: (b * nt + i, t)), full32,
                   pl.BlockSpec((8, TOK_WIDTH), lambda b, i, t: (0, 0))),
        scratch_shapes=[pltpu.VMEM((CONV_WIN, TOK_WIDTH), F32), pltpu.VMEM((SUBLANES - 1, SHIFT_ROWS, TOK_WIDTH), F32),
                        pltpu.VMEM((CONV_WIN, TOK_WIDTH), F32), pltpu.VMEM((SUBLANES - 1, SHIFT_ROWS, TOK_WIDTH), F32),
                        pltpu.VMEM((CONV_ROWS, TOK_WIDTH), BF16)],
        compiler_params=_params("arbitrary", "arbitrary", "arbitrary"), name="conv_bwd")(
            z, z, z, z, y, y, dcat, dcat, cw, lg, lb)


def _place():
    return lax.axis_index("x"), lax.axis_index("y"), lax.axis_index("c")


def _other_chips(x, y):
    return [(1 - x, y), (x, 1 - y), (1 - x, 1 - y)]


def reduce_small(arrays):
    na = len(arrays)

    def body(*refs):
        ins, outs, bufs = refs[:na], refs[na:2 * na], refs[2 * na:3 * na]
        send_sems, recv_sems = refs[3 * na:]
        x, y, c = _place()
        me = 4 * x + 2 * y + c
        copies = []
        for a in range(na):
            bufs[a][me] = ins[a][...]
            for k in range(1, N_DEV):
                cp = pltpu.make_async_remote_copy(
                    src_ref=ins[a], dst_ref=bufs[a].at[me], send_sem=send_sems.at[a, k - 1],
                    recv_sem=recv_sems.at[a, k - 1],
                    device_id=(x ^ (k >> 2), y ^ ((k >> 1) & 1), c ^ (k & 1)), device_id_type=MESH)
                cp.start()
                copies.append(cp)
        for a in range(na):
            for k in range(1, N_DEV):
                src = 4 * (x ^ (k >> 2)) + 2 * (y ^ ((k >> 1) & 1)) + (c ^ (k & 1))
                pltpu.make_async_remote_copy(
                    src_ref=ins[a], dst_ref=bufs[a].at[src], send_sem=send_sems.at[a, k - 1],
                    recv_sem=recv_sems.at[a, k - 1], device_id=(x, y, c), device_id_type=MESH).wait_recv()
        for cp in copies:
            cp.wait_send()
        for a in range(na):
            total = bufs[a][0]
            for dev in range(1, N_DEV):
                total = total + bufs[a][dev]
            outs[a][...] = total

    vmem = pl.BlockSpec(memory_space=pltpu.VMEM)
    return pl.pallas_call(
        body, out_shape=tuple(jax.ShapeDtypeStruct(a.shape, F32) for a in arrays),
        in_specs=[vmem] * na, out_specs=tuple([vmem] * na),
        scratch_shapes=[pltpu.VMEM((N_DEV,) + a.shape, F32) for a in arrays]
        + [pltpu.SemaphoreType.DMA((na, N_DEV - 1)), pltpu.SemaphoreType.DMA((na, N_DEV - 1))],
        compiler_params=pltpu.CompilerParams(vmem_limit_bytes=VMEM_LIMIT), name="small_reduce")(*arrays)


def adamw_small(ws, gs, ms, vs):
    na = len(ws)
    c1 = 1.0 / (1.0 - ADAM_B1 ** ADAM_STEP)
    c2 = 1.0 / (1.0 - ADAM_B2 ** ADAM_STEP)

    def body(*refs):
        w_refs, g_refs, m_refs, v_refs = (refs[i * na:(i + 1) * na] for i in range(4))
        d_refs, nm_refs, nv_refs = (refs[(4 + i) * na:(5 + i) * na] for i in range(3))
        for a in range(na):
            gv = g_refs[a][...]
            nm = ADAM_B1 * m_refs[a][...] + (1.0 - ADAM_B1) * gv
            nv = ADAM_B2 * v_refs[a][...] + (1.0 - ADAM_B2) * (gv * gv)
            nm_refs[a][...] = nm
            nv_refs[a][...] = nv
            d_refs[a][...] = -ADAM_LR * ((nm * c1) / (jnp.sqrt(nv * c2) + ADAM_EPS) + ADAM_WD * w_refs[a][...])

    vmem = pl.BlockSpec(memory_space=pltpu.VMEM)
    shapes = tuple(jax.ShapeDtypeStruct(w.shape, F32) for w in ws)
    outs = pl.pallas_call(
        body, out_shape=shapes * 3, in_specs=[vmem] * (4 * na), out_specs=tuple([vmem] * (3 * na)),
        compiler_params=pltpu.CompilerParams(vmem_limit_bytes=VMEM_LIMIT), name="adamw_small")(*ws, *gs, *ms, *vs)
    return outs[:na], outs[na:2 * na], outs[2 * na:]


def gather_weights(shards, name, collective_id):
    nw = len(shards)
    ns = [s.shape[0] for s in shards]
    in_refs = [jax.new_ref(s, memory_space=pltpu.MemorySpace.HBM) for s in shards]
    out_refs = [jax.empty_ref(jax.ShapeDtypeStruct((N_DEV * s.shape[0], s.shape[1]), s.dtype),
                              memory_space=pltpu.MemorySpace.HBM) for s in shards]

    @pl.kernel(mesh=plsc.ScalarSubcoreMesh(axis_name="seq", num_cores=1), name=name,
               scratch_types=(pltpu.SemaphoreType.DMA((nw, 7)), pltpu.SemaphoreType.DMA((nw, 7)),
                              pltpu.SemaphoreType.DMA((nw,))),
               compiler_params=pltpu.CompilerParams(collective_id=collective_id))
    def launch(send_sems, recv_sems, local_sems):
        x, y, c = _place()
        me, sib = (x, y, c), (x, y, 1 - c)
        chips = _other_chips(x, y)
        barrier = pltpu.get_barrier_semaphore()
        for peer in [sib] + [(*chip, c) for chip in chips]:
            pl.semaphore_signal(barrier, inc=1, device_id=peer, device_id_type=MESH)
        pl.semaphore_wait(barrier, 4)

        def rows(w, dev):
            return out_refs[w].at[pl.ds((4 * dev[0] + 2 * dev[1] + dev[2]) * ns[w], ns[w]), :]

        def copy(w, k, block, to, src=None):
            return pltpu.make_async_remote_copy(
                src_ref=rows(w, block) if src is None else src, dst_ref=rows(w, block),
                send_sem=send_sems.at[w, k], recv_sem=recv_sems.at[w, k], device_id=to, device_id_type=MESH)

        started, sends = [], []
        for w in range(nw):
            mine = pltpu.make_async_copy(in_refs[w], rows(w, me), local_sems.at[w])
            mine.start()
            started.append(mine)
            first = [copy(w, 0, me, sib, src=in_refs[w])]
            first += [copy(w, 1 + j, me, (*chip, c), src=in_refs[w]) for j, chip in enumerate(chips)]
            for cp in first:
                cp.start()
            sends += first
        for w in range(nw):
            for j, chip in enumerate(chips):
                copy(w, 1 + j, (*chip, c), me).wait_recv()
                fwd = copy(w, 4 + j, (*chip, c), sib)
                fwd.start()
                sends.append(fwd)
        for w in range(nw):
            copy(w, 0, sib, me).wait_recv()
            for j, chip in enumerate(chips):
                copy(w, 4 + j, (*chip, 1 - c), me).wait_recv()
        for cp in sends:
            cp.wait_send()
        for mine in started:
            mine.wait()

    launch()
    return [r[...] for r in out_refs]


def _sequencer_exchange(sources, out_rows, peers_of, copies_of, name, collective_id):
    nw = len(sources)
    in_refs = [jax.new_ref(s, memory_space=pltpu.MemorySpace.HBM) for s in sources]
    out_refs = [jax.empty_ref(jax.ShapeDtypeStruct((rows, s.shape[1]), s.dtype), memory_space=pltpu.MemorySpace.HBM)
                for rows, s in zip(out_rows, sources)]
    per = len(copies_of(0, 0, 0, 0))

    @pl.kernel(mesh=plsc.ScalarSubcoreMesh(axis_name="seq", num_cores=1), name=name,
               scratch_types=(pltpu.SemaphoreType.DMA((nw, per)), pltpu.SemaphoreType.DMA((nw, per))),
               compiler_params=pltpu.CompilerParams(collective_id=collective_id))
    def launch(send_sems, recv_sems):
        x, y, c = _place()
        peers = peers_of(x, y, c)
        barrier = pltpu.get_barrier_semaphore()
        for peer in peers:
            pl.semaphore_signal(barrier, inc=1, device_id=peer, device_id_type=MESH)
        pl.semaphore_wait(barrier, len(peers))
        copies = []
        for w in range(nw):
            for k, (src_blk, dst_blk, rows, peer) in enumerate(copies_of(x, y, c, w)):
                cp = pltpu.make_async_remote_copy(
                    src_ref=in_refs[w].at[pl.ds(src_blk * rows, rows), :],
                    dst_ref=out_refs[w].at[pl.ds(dst_blk * rows, rows), :],
                    send_sem=send_sems.at[w, k], recv_sem=recv_sems.at[w, k], device_id=peer, device_id_type=MESH)
                cp.start()
                copies.append(cp)
        for cp in copies:
            cp.wait_recv()
        for cp in copies:
            cp.wait_send()

    launch()
    return [r[...] for r in out_refs]


def scatter_to_sibling(grads, name, collective_id):
    ns = [g.shape[0] // N_DEV for g in grads]
    return _sequencer_exchange(
        grads, [4 * n for n in ns],
        lambda x, y, c: [(x, y, 1 - c)],
        lambda x, y, c, w: [(2 * q + 1 - c, q, ns[w], (x, y, 1 - c)) for q in range(4)],
        name, collective_id)


def scatter_to_chips(parts, name, collective_id):
    ns = [p.shape[0] // 4 for p in parts]
    return _sequencer_exchange(
        parts, [3 * n for n in ns],
        lambda x, y, c: [(*chip, c) for chip in _other_chips(x, y)],
        lambda x, y, c, w: [(2 * chip[0] + chip[1], j, ns[w], (*chip, c)) for j, chip in enumerate(_other_chips(x, y))],
        name, collective_id)


def add_sibling(grads, landeds, core, name):
    nw = len(grads)

    def body(c_ref, *refs):
        for w in range(nw):
            g_ref, l_ref, o_ref = refs[2 * w], refs[2 * w + 1], refs[2 * nw + w]
            o_ref[...] = (g_ref[...].astype(F32) + l_ref[...].astype(F32)).astype(o_ref.dtype)

    in_specs, out_specs, args = [], [], []
    for g, ld in zip(grads, landeds):
        n, cols = ld.shape[0] // 4, g.shape[1]
        in_specs += [pl.BlockSpec((n, cols), lambda q, c_ref: (2 * q + c_ref[0], 0)),
                     pl.BlockSpec((n, cols), lambda q, c_ref: (q, 0))]
        out_specs.append(pl.BlockSpec((n, cols), lambda q, c_ref: (q, 0)))
        args += [g, ld]
    grid_spec = pltpu.PrefetchScalarGridSpec(
        num_scalar_prefetch=1, grid=(4,), in_specs=in_specs, out_specs=tuple(out_specs))
    return pl.pallas_call(
        body, out_shape=tuple(jax.ShapeDtypeStruct(ld.shape, ld.dtype) for ld in landeds), grid_spec=grid_spec,
        compiler_params=_params("arbitrary"), name=name)(core, *args)


ADAMW_HALVES = 2


def adamw_shards(items, chip, name):
    c1 = 1.0 / (1.0 - ADAM_B1 ** ADAM_STEP)
    c2 = 1.0 / (1.0 - ADAM_B2 ** ADAM_STEP)
    ni = len(items)

    def body(q_ref, *refs):
        outs = refs[len(refs) - 4 * ni:]
        for k in range(ni):
            w_ref, m_ref, v_ref, p_ref, l0_ref, l1_ref, l2_ref = refs[7 * k:7 * k + 7]
            g_ref, d_ref, nm_ref, nv_ref = outs[4 * k:4 * k + 4]
            gv = ((p_ref[...].astype(F32) + l0_ref[...].astype(F32)) + l1_ref[...].astype(F32)) + l2_ref[...].astype(F32)
            nm = ADAM_B1 * m_ref[...] + (1.0 - ADAM_B1) * gv
            nv = ADAM_B2 * v_ref[...] + (1.0 - ADAM_B2) * (gv * gv)
            g_ref[...] = gv
            nm_ref[...] = nm
            nv_ref[...] = nv
            d_ref[...] = -ADAM_LR * ((nm * c1) / (jnp.sqrt(nv * c2) + ADAM_EPS) + ADAM_WD * w_ref[...])

    sub = ADAMW_HALVES
    in_specs, out_specs, out_shape, args, donated = [], [], [], [chip], []
    for layer, w, m, v, part, landed, earlier in items:
        rows, cols = landed.shape[0] // (3 * sub), w.shape[1]

        def block(first, rows=rows, cols=cols):
            return pl.BlockSpec((rows, cols), lambda i, q_ref: (first(q_ref) * sub + i, 0))

        own = block(lambda q_ref, layer=layer: layer)
        in_specs += [own, own, own, block(lambda q_ref: q_ref[0])] + [block(lambda q_ref, j=j: j) for j in range(3)]
        args += [w, m, v, part, landed, landed, landed]
        out_specs += [own] * 4
        out_shape += [jax.ShapeDtypeStruct(w.shape, F32)] * 4
        donated.append(earlier)
    aliases = {}
    for k, earlier in enumerate(donated):
        if earlier is not None:
            for j in range(4):
                aliases[len(args)] = 4 * k + j
                in_specs.append(ANY)
                args.append(earlier[j])
    grid_spec = pltpu.PrefetchScalarGridSpec(
        num_scalar_prefetch=1, grid=(sub,), in_specs=in_specs, out_specs=tuple(out_specs))
    outs = pl.pallas_call(
        body, out_shape=tuple(out_shape), grid_spec=grid_spec, input_output_aliases=aliases,
        compiler_params=_params("arbitrary"), name=name)(*args)
    return [tuple(outs[4 * k:4 * k + 4]) for k in range(ni)]


def _pack(arrays):
    flat = jnp.concatenate([a.reshape(-1).astype(F32) for a in arrays])
    pad = (-flat.shape[0]) % (8 * LANES)
    return jnp.pad(flat, (0, pad)).reshape(-1, LANES)


def _unpack(slab, shapes):
    flat = slab.reshape(slab.shape[:-2] + (-1,))
    out, off = [], 0
    for shp in shapes:
        size = 1
        for s in shp:
            size *= s
        out.append(flat[..., off:off + size].reshape(flat.shape[:-1] + tuple(shp)))
        off += size
    return out


def kernel(x, mem, norm1_g, mem_norm_g, a_w_in, a_q_g, a_k_g, a_rel_bias, b_w_in, b_b_in, b_conv_w, b_conv_b, b_ln_g, b_ln_b, mq_g, mk_g, w_mem_kv, w_out, norm2_g, w_gate, w_up, w_down, loss_target, m_norm1_g, m_mem_norm_g, m_a_w_in, m_a_q_g, m_a_k_g, m_a_rel_bias, m_b_w_in, m_b_b_in, m_b_conv_w, m_b_conv_b, m_b_ln_g, m_b_ln_b, m_mq_g, m_mk_g, m_w_mem_kv, m_w_out, m_norm2_g, m_w_gate, m_w_up, m_w_down, v_norm1_g, v_mem_norm_g, v_a_w_in, v_a_q_g, v_a_k_g, v_a_rel_bias, v_b_w_in, v_b_b_in, v_b_conv_w, v_b_conv_b, v_b_ln_g, v_b_ln_b, v_mq_g, v_mk_g, v_w_mem_kv, v_w_out, v_norm2_g, v_w_gate, v_w_up, v_w_down):
    batch, seq, d = x.shape
    mtok = mem.shape[1]
    n = batch * seq
    ax, ay, ac = _place()
    me = 4 * ax + 2 * ay + ac
    core_arr = jnp.reshape(ac, (1,)).astype(jnp.int32)
    chip_arr = jnp.reshape(2 * ax + ay, (1,)).astype(jnp.int32)

    def t_bf16(w):
        return jnp.transpose(w).astype(BF16)

    def after(value, *earlier):
        return lax.optimization_barrier((value, *earlier))[0]

    def gather_mix(l, when, name, collective_id):
        srcs = [w_mem_kv[l].astype(BF16), w_out[l].astype(BF16)]
        if l == 1:
            srcs += [t_bf16(b_w_in[0]), _pack([b_b_in, b_conv_w, b_conv_b, b_ln_g, b_ln_b])]
        return gather_weights([after(srcs[0], *when)] + srcs[1:], name, collective_id)

    def gather_ffn(l, when, name, collective_id):
        return gather_weights(
            [after(t_bf16(w_gate[l]), *when), t_bf16(w_up[l]), w_down[l].astype(BF16)], name, collective_id)

    f_loc = b_b_in.shape[1]
    c_loc = b_conv_b.shape[1]

    def two(g):
        return jnp.concatenate([g, g], axis=-1)

    gq2, gk2 = two(a_q_g), two(a_k_g)
    rel16 = jnp.pad(a_rel_bias[0], ((0, 16 - a_rel_bias.shape[1]), (0, 0)))
    bias = bias_blocks(rel16)

    x0 = x.reshape(n, d)
    mem2 = mem.reshape(batch * mtok, d)

    saved = []
    xin = x0
    a_win_t, = gather_weights([t_bf16(a_w_in[0])], "gather_in_a", 1)
    wg_t, wu_t, wd, wo, wkv = [None] * 2, [None] * 2, [None] * 2, [None] * 2, [None] * 2
    h = after(rms_fwd(xin, norm1_g[0:1], name="rms1_fwd_0"), bias)
    target = loss_target.reshape(n, d)
    for l in range(2):
        gq4 = jnp.tile(mq_g[l:l + 1], (1, 4))
        gk4 = jnp.tile(mk_g[l:l + 1], (1, 4))
        y_conv = None
        if l == 0:
            wkv[0], wo[0] = gather_mix(0, (h, a_win_t), "gather_mix_a", 2)
            z = mm_nt(h, a_win_t, name="in_proj_a")
            wg_t[0], wu_t[0], wd[0] = gather_ffn(0, (z, wkv[0]), "gather_ffn_a", 3)
            cat = attn_fwd(z, gq2, gk2, bias, batch, seq)
            wkv[1], wo[1], b_win_t, conv_slabs = gather_mix(1, (cat, wg_t[0]), "gather_mix_b", 4)
            qcol = 3 * TOK_WIDTH // MEM_WIDTH
        else:
            small_shapes = [(f_loc,), (CONV_W, c_loc), (c_loc,), (c_loc,), (c_loc,)]
            bb_g, cw_g, cb_g, lg_g, lb_g = _unpack(conv_slabs.reshape(N_DEV, -1, LANES), small_shapes)
            bb_full = bb_g.reshape(1, -1)
            cw_full = jnp.pad(jnp.transpose(cw_g, (1, 0, 2)).reshape(CONV_W, -1), ((0, 32 - CONV_W), (0, 0)))
            cb_full, lg_full, lb_full = cb_g.reshape(1, -1), lg_g.reshape(1, -1), lb_g.reshape(1, -1)
            z = mm_nt(h, b_win_t, bias=bb_full, name="in_proj_b")
            cat, y_conv = conv_fwd(z, cw_full, cb_full, lg_full, lb_full, batch, seq)
            qcol = 2 * TOK_WIDTH // MEM_WIDTH
        mem_n, kv = mem_prep(mem2, mem_norm_g[l:l + 1], wkv[l], name=f"mem_prep_{l}")
        cat = memattn_fwd(z, kv, gq4, gk4, cat, batch, seq, qcol, name=f"memattn_fwd_{l}")
        x1, h2 = proj_norm(cat, wo[l], xin, norm2_g[l:l + 1], name=f"out_proj_{l}")
        if l == 0:
            wg_t[1], wu_t[1], wd[1] = gather_ffn(1, (x1, b_win_t), "gather_ffn_b", 5)
        if l == 0:
            gate, up, act, x2, h_next = ffn_fwd(h2, wg_t[0], wu_t[0], wd[0], x1, gain=norm1_g[1:2], name="ffn_fwd_0")
        else:
            gate, up, act, dx_b, loss_blk = ffn_fwd(h2, wg_t[1], wu_t[1], wd[1], x1, target=target, name="ffn_fwd_1")
        saved.append(dict(xin=xin, h=h, mem_n=mem_n, kv=kv, gq4=gq4, gk4=gk4, z=z, qcol=qcol, cat=cat, x1=x1, h2=h2,
                          gate=gate, up=up, act=act, y_conv=y_conv))
        if l == 0:
            xin, h = x2, h_next

    big = {}
    small = {}
    reduced = {}
    groups = 0

    def scatter_siblings(keys):
        nonlocal groups
        gid = groups
        groups += 1
        return gid, keys, scatter_to_sibling([big[k] for k in keys], f"scatter_sibling_{gid}", 8 + 2 * gid)

    def scatter_chips(stage1, when):
        gid, keys, landed1 = stage1
        parts = add_sibling([after(big[keys[0]], when)] + [big[k] for k in keys[1:]], landed1, core_arr,
                            name=f"add_sibling_{gid}")
        landed2 = scatter_to_chips(parts, f"scatter_chips_{gid}", 9 + 2 * gid)
        for k, p, ld in zip(keys, parts, landed2):
            reduced[k] = (p, ld)
        return parts, landed2

    def rows_of(w, transposed):
        w = jnp.swapaxes(w, 1, 2) if transposed else w
        return w.reshape(w.shape[0] * w.shape[1], w.shape[2])

    sharded = {
        "win0": (2, True), "win1": (6, True), "wkv": (14, False), "wo": (15, False),
        "wg": (17, True), "wu": (18, True), "wd": (19, False)}
    weights = [norm1_g, mem_norm_g, a_w_in, a_q_g, a_k_g, a_rel_bias, b_w_in, b_b_in, b_conv_w, b_conv_b, b_ln_g,
               b_ln_b, mq_g, mk_g, w_mem_kv, w_out, norm2_g, w_gate, w_up, w_down]
    moms = [m_norm1_g, m_mem_norm_g, m_a_w_in, m_a_q_g, m_a_k_g, m_a_rel_bias, m_b_w_in, m_b_b_in, m_b_conv_w,
            m_b_conv_b, m_b_ln_g, m_b_ln_b, m_mq_g, m_mk_g, m_w_mem_kv, m_w_out, m_norm2_g, m_w_gate, m_w_up, m_w_down]
    vels = [v_norm1_g, v_mem_norm_g, v_a_w_in, v_a_q_g, v_a_k_g, v_a_rel_bias, v_b_w_in, v_b_b_in, v_b_conv_w,
            v_b_conv_b, v_b_ln_g, v_b_ln_b, v_mq_g, v_mk_g, v_w_mem_kv, v_w_out, v_norm2_g, v_w_gate, v_w_up, v_w_down]
    updated = {}

    def update_layer(l, when):
        for group, keys in (("ffn", ("wg", "wu", "wd")), ("mix", (f"win{l}", "wkv", "wo"))):
            items = []
            for key in keys:
                idx, transposed = sharded[key]
                layer, rkey = (0, key) if key.startswith("win") else (l, f"{key}{l}")
                part, landed = reduced[rkey]
                w_rows = rows_of(weights[idx], transposed)
                items.append((layer, after(w_rows, when) if not items else w_rows, rows_of(moms[idx], transposed),
                              rows_of(vels[idx], transposed), part, landed, updated.get(key)))
            for key, result in zip(keys, adamw_shards(items, chip_arr, name=f"adamw_{group}_{l}")):
                updated[key] = result

    mix_landed = None
    for l in (1, 0):
        sv = saved[l]
        dgate, dup, dx1_b, dcat, small[f"norm2_{l}"] = ffn_bwd(
            dx_b, wd[l], sv["gate"], sv["up"], wg_t[l], wu_t[l], sv["x1"], norm2_g[l:l + 1], wo[l], name=f"ffn_bwd_{l}")
        if l == 0:
            dgate = after(dgate, *mix_landed)
            update_layer(1, dx1_b)
        big[f"wg{l}"], big[f"wu{l}"], big[f"wd{l}"] = ffn_weight_grads(
            dgate, dup, sv["h2"], sv["act"], dx_b, name=f"grad_ffn_{l}")
        stage1 = scatter_siblings([f"wd{l}", f"wg{l}", f"wu{l}"])
        big[f"wo{l}"] = mm_tn(sv["cat"], dx1_b, name=f"grad_wo_{l}")
        parts, ffn_landed = scatter_chips(stage1, big[f"wo{l}"])
        dcat = after(dcat, *parts)
        if l == 0:
            dz, dbias, small["a_q"], small["a_k"] = attn_bwd(sv["z"], dcat, gq2, gk2, bias, batch, seq)
            small["rel"] = bias_grad(dbias)
            win_t = a_win_t
        else:
            dz, small["cw"], small["csum"] = conv_bwd(sv["z"], sv["y_conv"], dcat, cw_full, lg_full, lb_full, batch, seq)
            win_t = b_win_t
        dz = after(dz, *ffn_landed)
        dz, dkv, small[f"mq_{l}"], small[f"mk_{l}"] = memattn_bwd(
            sv["z"], sv["kv"], dcat, sv["gq4"], sv["gk4"], dz, batch, seq, sv["qcol"], name=f"memattn_bwd_{l}")
        big[f"win{l}"] = mm_tn(dz, sv["h"], name=f"grad_win_{l}")
        big[f"wkv{l}"] = mm_tn(sv["mem_n"], dkv, name=f"grad_wkv_{l}")
        stage1 = scatter_siblings([f"win{l}", f"wkv{l}", f"wo{l}"])
        dx_b, small[f"norm1_{l}"], dz_sum = in_proj_bwd(
            dz, win_t, sv["xin"], norm1_g[l:l + 1], dx1_b, BF16 if l == 1 else F32, name=f"in_proj_bwd_{l}")
        if l == 1:
            small["bb"] = dz_sum
        parts, mix_landed = scatter_chips(stage1, dx_b)
        dx_b = after(dx_b, *parts)
        small[f"memnorm_{l}"] = mem_norm_grad(dkv, wkv[l], mem2, name=f"mem_norm_grad_{l}")
    grad_x = dx_b.reshape(batch, seq, d)
    update_layer(0, dx_b)

    def shaped(rows, idx, transposed):
        shp = weights[idx].shape
        if transposed:
            return jnp.swapaxes(rows.reshape(shp[0], shp[2], shp[1]), 1, 2)
        return rows.reshape(shp)

    def fold(v, groups):
        return jnp.sum(v.reshape(groups, HEAD_DIM), axis=0, keepdims=True)

    heads = a_rel_bias.shape[1]
    small_list = [
        jnp.concatenate([small["norm1_0"], small["norm1_1"]]),
        jnp.concatenate([small["memnorm_0"], small["memnorm_1"]]),
        fold(small["a_q"], 2), fold(small["a_k"], 2), small["rel"][:heads][None],
        small["bb"], small["cw"][:CONV_W][None], small["csum"][0:1], small["csum"][1:2], small["csum"][2:3],
        jnp.concatenate([fold(small["mq_0"], 4), fold(small["mq_1"], 4)]),
        jnp.concatenate([fold(small["mk_0"], 4), fold(small["mk_1"], 4)]),
        jnp.concatenate([small["norm2_0"], small["norm2_1"]]),
    ]
    (g_norm1, g_memnorm, g_aq, g_ak, g_rel, g_bb_full, g_cw_full, g_cb_full, g_lg_full, g_lb_full,
     g_mq, g_mk, g_norm2, loss_sum) = reduce_small(small_list + [loss_blk])
    loss = loss_sum[0, 0]
    g_bb = lax.dynamic_slice_in_dim(g_bb_full, me * f_loc, f_loc, axis=1)
    g_cw = lax.dynamic_slice_in_dim(g_cw_full, me * c_loc, c_loc, axis=2)
    g_cb = lax.dynamic_slice_in_dim(g_cb_full, me * c_loc, c_loc, axis=1)
    g_lg = lax.dynamic_slice_in_dim(g_lg_full, me * c_loc, c_loc, axis=1)
    g_lb = lax.dynamic_slice_in_dim(g_lb_full, me * c_loc, c_loc, axis=1)

    grads = [g_norm1, g_memnorm, None, g_aq, g_ak, g_rel, None, g_bb, g_cw, g_cb, g_lg, g_lb,
             g_mq, g_mk, None, None, g_norm2, None, None, None]
    deltas, new_m, new_v = [None] * 20, [None] * 20, [None] * 20
    for key, (idx, transposed) in sharded.items():
        grads[idx], deltas[idx], new_m[idx], new_v[idx] = (shaped(r, idx, transposed) for r in updated[key])

    small_idx = [i for i in range(20) if i not in {idx for idx, _ in sharded.values()}]
    dl, nm, nv = adamw_small([weights[i] for i in small_idx], [grads[i] for i in small_idx],
                             [moms[i] for i in small_idx], [vels[i] for i in small_idx])
    for i, a, b, cc in zip(small_idx, dl, nm, nv):
        deltas[i], new_m[i], new_v[i] = a, b, cc

    return (loss, grad_x, *grads, *deltas, *new_m, *new_v)
```

```python
import jax
import jax.numpy as jnp
from jax import lax
from jax.experimental import pallas as pl
from jax.experimental.pallas import tpu as pltpu
from jax.experimental.pallas import tpu_sc as plsc

F32 = jnp.float32
BF16 = jnp.bfloat16
HIGHEST = lax.Precision.HIGHEST
MESH = pl.DeviceIdType.MESH
ANY = pl.BlockSpec(memory_space=pl.ANY)

N_DEV = 8
D_MODEL = 1024
HEAD_DIM = 64
TOK_WIDTH = 768
MEM_WIDTH = 256
CHUNK = 64
Q_BLOCK = 256
KEY_WIN = 768
BAND = 576
N_REL = 192
CONV_W = 31
CONV_HALO = 32
NORM_EPS = 1e-6
NEG_INF = -1e30
ATTN_SCALE = HEAD_DIM ** -0.5
LANES = 128
ROW_TILE = 512
VMEM_LIMIT = 56 * 1024 * 1024

ADAM_LR, ADAM_B1, ADAM_B2, ADAM_EPS, ADAM_WD, ADAM_STEP = 0.001, 0.9, 0.999, 1e-08, 0.01, 10


def _params(*sem):
    return pltpu.CompilerParams(dimension_semantics=sem, vmem_limit_bytes=VMEM_LIMIT)


WIDE_ROW_TILE = 1024


def _row_tile(m, rows=ROW_TILE):
    return rows if m % rows == 0 else m


def _col_tile(n, cap=1408):
    best = None
    for t in range(LANES, min(n, cap) + 1, LANES):
        if n % t == 0:
            best = t
    return best if best is not None else n


def _dot(a, b, ca, cb):
    return lax.dot_general(a, b, (((ca,), (cb,)), ((), ())), preferred_element_type=F32)


def _sigmoid(x):
    return 0.5 * jnp.tanh(0.5 * x) + 0.5


def mm_nt(a, b, bias=None, out_dtype=BF16, name="mm_nt"):
    m, k = a.shape
    n = b.shape[0]
    tm, tn = _row_tile(m, WIDE_ROW_TILE), _col_tile(n)

    def body(*refs):
        a_ref, b_ref = refs[0], refs[1]
        o_ref = refs[-1]
        acc = _dot(a_ref[...].astype(BF16), b_ref[...].astype(BF16), 1, 1)
        if bias is not None:
            acc = acc + refs[2][...]
        o_ref[...] = acc.astype(o_ref.dtype)

    in_specs = [pl.BlockSpec((tm, k), lambda j, i: (i, 0)), pl.BlockSpec((tn, k), lambda j, i: (j, 0))]
    args = [a, b]
    if bias is not None:
        in_specs.append(pl.BlockSpec((1, tn), lambda j, i: (0, j)))
        args.append(bias)
    return pl.pallas_call(
        body, out_shape=jax.ShapeDtypeStruct((m, n), out_dtype), grid=(n // tn, m // tm),
        in_specs=in_specs, out_specs=pl.BlockSpec((tm, tn), lambda j, i: (i, j)),
        compiler_params=_params("parallel", "arbitrary"), name=name)(*args)


def mm_tn(a, b, out_dtype=BF16, name="mm_tn"):
    t, r = a.shape
    c = b.shape[1]
    tr = _col_tile(r, 512)

    def body(a_ref, b_ref, o_ref):
        o_ref[...] = _dot(a_ref[...].astype(BF16), b_ref[...].astype(BF16), 0, 0).astype(o_ref.dtype)

    return pl.pallas_call(
        body, out_shape=jax.ShapeDtypeStruct((r, c), out_dtype), grid=(r // tr,),
        in_specs=[pl.BlockSpec((t, tr), lambda i: (0, i)), pl.BlockSpec((t, c), lambda i: (0, 0))],
        out_specs=pl.BlockSpec((tr, c), lambda i: (i, 0)),
        compiler_params=_params("parallel"), name=name)(a, b)


def _resident(shape):
    return pl.BlockSpec(shape, lambda i: (0, 0), pipeline_mode=pl.Buffered(1))


def proj_norm(a, b, res, gain, name):
    m, k = a.shape
    n = b.shape[1]
    tm = _row_tile(m)

    def body(a_ref, b_ref, res_ref, g_ref, x_ref, h_ref):
        xv = res_ref[...] + _dot(a_ref[...], b_ref[...], 1, 0)
        x_ref[...] = xv
        r = lax.rsqrt(jnp.mean(xv * xv, axis=-1, keepdims=True) + NORM_EPS)
        h_ref[...] = (xv * r * g_ref[...]).astype(BF16)

    row = pl.BlockSpec((tm, n), lambda i: (i, 0))
    return pl.pallas_call(
        body, out_shape=(jax.ShapeDtypeStruct((m, n), F32), jax.ShapeDtypeStruct((m, n), BF16)), grid=(m // tm,),
        in_specs=[pl.BlockSpec((tm, k), lambda i: (i, 0)), _resident((k, n)), row, _resident((1, n))],
        out_specs=(row, row), compiler_params=_params("parallel"), name=name)(a, b, res, gain)


def in_proj_bwd(dz, w_t, x, gain, dres, out_dtype, name):
    m, n = x.shape
    k = dz.shape[1]
    tm = _row_tile(m)

    def body(dz_ref, w_ref, x_ref, g_ref, dres_ref, dx_ref, dg_ref, cs_ref):
        @pl.when(pl.program_id(0) == 0)
        def _():
            dg_ref[...] = jnp.zeros_like(dg_ref)
            cs_ref[...] = jnp.zeros_like(cs_ref)

        dzv = dz_ref[...]
        cs_ref[...] += jnp.sum(dzv.astype(F32), axis=0, keepdims=True)
        dhv = _dot(dzv, w_ref[...], 1, 0)
        xv = x_ref[...]
        r = lax.rsqrt(jnp.mean(xv * xv, axis=-1, keepdims=True) + NORM_EPS)
        xhat = xv * r
        dg_ref[...] += jnp.sum(dhv * xhat, axis=0, keepdims=True)
        dxhat = dhv * g_ref[...]
        dx = dres_ref[...].astype(F32) + r * (dxhat - xhat * jnp.mean(dxhat * xhat, axis=-1, keepdims=True))
        dx_ref[...] = dx.astype(dx_ref.dtype)

    row = pl.BlockSpec((tm, n), lambda i: (i, 0))
    return pl.pallas_call(
        body, out_shape=(jax.ShapeDtypeStruct((m, n), out_dtype), jax.ShapeDtypeStruct((1, n), F32),
                         jax.ShapeDtypeStruct((1, k), F32)), grid=(m // tm,),
        in_specs=[pl.BlockSpec((tm, k), lambda i: (i, 0)), _resident(w_t.shape), row, _resident((1, n)), row],
        out_specs=(row, pl.BlockSpec((1, n), lambda i: (0, 0)), pl.BlockSpec((1, k), lambda i: (0, 0))),
        compiler_params=_params("arbitrary"), name=name)(dz, w_t, x, gain, dres)


FFN_ROWS = 256


def _ffn_row_tile(m):
    return FFN_ROWS if m % FFN_ROWS == 0 else m


def ffn_fwd(h2, wg_t, wu_t, wd, x1, gain=None, target=None, name="ffn_fwd"):
    n, d = h2.shape
    f = wg_t.shape[0]
    tm = _ffn_row_tile(n)
    nt = n // tm
    last = target is not None

    def body(h_ref, wg_ref, wu_ref, wd_ref, x1_ref, e_ref, g_ref, u_ref, a_ref, *rest):
        hv = h_ref[...]
        gv = _dot(hv, wg_ref[...], 1, 1)
        uv = _dot(hv, wu_ref[...], 1, 1)
        g_ref[...] = gv.astype(BF16)
        u_ref[...] = uv.astype(BF16)
        av = (gv * _sigmoid(gv) * uv).astype(BF16)
        a_ref[...] = av
        xv = x1_ref[...] + _dot(av, wd_ref[...], 1, 0)
        if not last:
            x_ref, hn_ref = rest
            x_ref[...] = xv
            r = lax.rsqrt(jnp.mean(xv * xv, axis=-1, keepdims=True) + NORM_EPS)
            hn_ref[...] = (xv * r * e_ref[...]).astype(BF16)
        else:
            dyb_ref, l_ref, acc_ref = rest
            i = pl.program_id(0)

            @pl.when(i == 0)
            def _():
                acc_ref[...] = jnp.zeros_like(acc_ref)

            err = xv - e_ref[...]
            dyb_ref[...] = (err * (1.0 / d)).astype(BF16)
            acc_ref[...] += jnp.sum(err * err, axis=0, keepdims=True)

            @pl.when(i == nt - 1)
            def _():
                total = jnp.sum(acc_ref[...], axis=-1, keepdims=True) * (0.5 / d)
                l_ref[...] = jnp.broadcast_to(total, l_ref.shape)

    row_d = pl.BlockSpec((tm, d), lambda i: (i, 0))
    row_f = pl.BlockSpec((tm, f), lambda i: (i, 0))
    act_shape = jax.ShapeDtypeStruct((n, f), BF16)
    if not last:
        extra_in, extra = _resident((1, d)), gain
        out_shape = (act_shape, act_shape, act_shape, jax.ShapeDtypeStruct((n, d), F32), jax.ShapeDtypeStruct((n, d), BF16))
        out_specs = (row_f, row_f, row_f, row_d, row_d)
        scratch = []
    else:
        extra_in, extra = row_d, target
        out_shape = (act_shape, act_shape, act_shape, jax.ShapeDtypeStruct((n, d), BF16),
                     jax.ShapeDtypeStruct((8, LANES), F32))
        out_specs = (row_f, row_f, row_f, row_d, pl.BlockSpec((8, LANES), lambda i: (0, 0)))
        scratch = [pltpu.VMEM((1, d), F32)]
    return pl.pallas_call(
        body, out_shape=out_shape, grid=(nt,),
        in_specs=[row_d, _resident((f, d)), _resident((f, d)), _resident((f, d)), row_d, extra_in],
        out_specs=out_specs, scratch_shapes=scratch,
        compiler_params=_params("arbitrary"), name=name)(h2, wg_t, wu_t, wd, x1, extra)


def ffn_bwd(dx_b, wd, gate, up, wg_t, wu_t, x1, gain, wo, name="ffn_bwd"):
    n, d = x1.shape
    f = wd.shape[0]
    tm = _ffn_row_tile(n)

    def body(dxb_ref, wd_ref, g_ref, u_ref, wg_ref, wu_ref, x_ref, gain_ref, wo_ref,
             dg_ref, du_ref, dxo_ref, dc_ref, dgain_ref):
        @pl.when(pl.program_id(0) == 0)
        def _():
            dgain_ref[...] = jnp.zeros_like(dgain_ref)

        dact = _dot(dxb_ref[...], wd_ref[...], 1, 1)
        gv = g_ref[...].astype(F32)
        uv = u_ref[...].astype(F32)
        sg = _sigmoid(gv)
        dgv = (dact * uv * sg * (1.0 + gv * (1.0 - sg))).astype(BF16)
        duv = (dact * gv * sg).astype(BF16)
        dg_ref[...] = dgv
        du_ref[...] = duv
        dhv = _dot(dgv, wg_ref[...], 1, 0) + _dot(duv, wu_ref[...], 1, 0)
        xv = x_ref[...]
        r = lax.rsqrt(jnp.mean(xv * xv, axis=-1, keepdims=True) + NORM_EPS)
        xhat = xv * r
        dgain_ref[...] += jnp.sum(dhv * xhat, axis=0, keepdims=True)
        dxhat = dhv * gain_ref[...]
        dxb = (dxb_ref[...].astype(F32) + r * (dxhat - xhat * jnp.mean(dxhat * xhat, axis=-1, keepdims=True))).astype(BF16)
        dxo_ref[...] = dxb
        dc_ref[...] = _dot(dxb, wo_ref[...], 1, 1).astype(BF16)

    row_d = pl.BlockSpec((tm, d), lambda i: (i, 0))
    row_f = pl.BlockSpec((tm, f), lambda i: (i, 0))
    w_spec = _resident((f, d))
    act_shape = jax.ShapeDtypeStruct((n, f), BF16)
    row_shape = jax.ShapeDtypeStruct((n, d), BF16)
    return pl.pallas_call(
        body, out_shape=(act_shape, act_shape, row_shape, jax.ShapeDtypeStruct((n, wo.shape[0]), BF16),
                         jax.ShapeDtypeStruct((1, d), F32)),
        grid=(n // tm,),
        in_specs=[row_d, w_spec, row_f, row_f, w_spec, w_spec, row_d, _resident((1, d)), _resident(wo.shape)],
        out_specs=(row_f, row_f, row_d, pl.BlockSpec((tm, wo.shape[0]), lambda i: (i, 0)),
                   pl.BlockSpec((1, d), lambda i: (0, 0))),
        compiler_params=_params("arbitrary"), name=name)(dx_b, wd, gate, up, wg_t, wu_t, x1, gain, wo)


def ffn_weight_grads(dgate, dup, h2, act, dx_b, name="ffn_weight_grads"):
    t, r = dgate.shape
    c = h2.shape[1]
    tr = _col_tile(r, 512)

    def body(a1_ref, a2_ref, a3_ref, b12_ref, b3_ref, o1_ref, o2_ref, o3_ref):
        bv = b12_ref[...]
        o1_ref[...] = _dot(a1_ref[...], bv, 0, 0).astype(o1_ref.dtype)
        o2_ref[...] = _dot(a2_ref[...], bv, 0, 0).astype(o2_ref.dtype)
        o3_ref[...] = _dot(a3_ref[...], b3_ref[...], 0, 0).astype(o3_ref.dtype)

    a_spec = pl.BlockSpec((t, tr), lambda i: (0, i))
    o_spec = pl.BlockSpec((tr, c), lambda i: (i, 0))
    shape = jax.ShapeDtypeStruct((r, c), BF16)
    return pl.pallas_call(
        body, out_shape=(shape, shape, shape), grid=(r // tr,),
        in_specs=[a_spec, a_spec, a_spec, _resident((t, c)), _resident((t, c))],
        out_specs=(o_spec, o_spec, o_spec), compiler_params=_params("parallel"), name=name)(dgate, dup, act, h2, dx_b)


def rms_fwd(x, g, name="rms_fwd"):
    n, d = x.shape
    tm = _row_tile(n)

    def body(x_ref, g_ref, o_ref):
        xv = x_ref[...]
        r = lax.rsqrt(jnp.mean(xv * xv, axis=-1, keepdims=True) + NORM_EPS)
        o_ref[...] = (xv * r * g_ref[...]).astype(o_ref.dtype)

    return pl.pallas_call(
        body, out_shape=jax.ShapeDtypeStruct((n, d), BF16), grid=(n // tm,),
        in_specs=[pl.BlockSpec((tm, d), lambda i: (i, 0)), pl.BlockSpec((1, d), lambda i: (0, 0))],
        out_specs=pl.BlockSpec((tm, d), lambda i: (i, 0)),
        compiler_params=_params("parallel"), name=name)(x, g)


def _group_masks(width):
    lane = lax.broadcasted_iota(jnp.int32, (1, width), 1)
    return [(lane >= HEAD_DIM * g) & (lane < HEAD_DIM * (g + 1)) for g in range(width // HEAD_DIM)]


def _group_sum(x, masks):
    out = jnp.zeros_like(x)
    for msk in masks:
        s = jnp.sum(jnp.where(msk, x, 0.0), axis=-1, keepdims=True)
        out = jnp.where(msk, s, out)
    return out


def _head_norm(x, gain, masks):
    r = lax.rsqrt(_group_sum(x * x, masks) * (1.0 / HEAD_DIM) + NORM_EPS)
    xhat = x * r
    return xhat * gain, xhat, r


def _head_norm_bwd(dxn, xhat, r, gain, masks):
    dgain = jnp.sum(dxn * xhat, axis=0, keepdims=True)
    dxhat = dxn * gain
    mean_t = _group_sum(dxhat * xhat, masks) * (1.0 / HEAD_DIM)
    return r * (dxhat - xhat * mean_t), dgain


def _softmax_rows(s):
    e = jnp.exp(s - jnp.max(s, axis=-1, keepdims=True))
    return e * (1.0 / jnp.sum(e, axis=-1, keepdims=True))


def _rel_onehot():
    col = lax.broadcasted_iota(jnp.int32, (1, KEY_WIN), 1)
    off = jnp.where(col < KEY_WIN - LANES, col, col - KEY_WIN)
    idx = jnp.clip(8 * CHUNK - off, -(CHUNK - 1), LANES) + (CHUNK - 1)
    return (lax.broadcasted_iota(jnp.int32, (N_REL, KEY_WIN), 0) == idx).astype(F32)


def bias_blocks(rel16):
    heads = TOK_WIDTH // HEAD_DIM

    def body(rel_ref, o_ref, u_ref):
        u_ref[...] = jnp.dot(rel_ref[...], _rel_onehot(), precision=HIGHEST, preferred_element_type=F32)
        row = lax.broadcasted_iota(jnp.int32, (CHUNK, KEY_WIN), 0)
        col = lax.broadcasted_iota(jnp.int32, (CHUNK, KEY_WIN), 1)
        for h in range(heads):
            xv = jnp.broadcast_to(u_ref[h:h + 1, :], (CHUNK, KEY_WIN))
            for b in range(6):
                xv = jnp.where(((row >> b) & 1) == 1, pltpu.roll(xv, 1 << b, axis=1), xv)
            xv = jnp.where(col < BAND, xv, NEG_INF)
            for i in range(Q_BLOCK // CHUNK):
                o_ref[h, CHUNK * i:CHUNK * (i + 1), :] = pltpu.roll(xv, CHUNK * i, axis=1) if i else xv

    return pl.pallas_call(
        body, out_shape=jax.ShapeDtypeStruct((heads, Q_BLOCK, KEY_WIN), F32),
        scratch_shapes=[pltpu.VMEM((16, KEY_WIN), F32)], name="bias_blocks")(rel16)


def bias_grad(dbias):
    heads = dbias.shape[0]

    def body(db_ref, o_ref, y_ref):
        y_ref[...] = jnp.zeros_like(y_ref)
        row = lax.broadcasted_iota(jnp.int32, (CHUNK, KEY_WIN), 0)
        for h in range(heads):
            fv = db_ref[h, 0:CHUNK, :]
            for i in range(1, Q_BLOCK // CHUNK):
                fv = fv + pltpu.roll(db_ref[h, CHUNK * i:CHUNK * (i + 1), :], KEY_WIN - CHUNK * i, axis=1)
            for b in range(6):
                fv = jnp.where(((row >> b) & 1) == 1, pltpu.roll(fv, KEY_WIN - (1 << b), axis=1), fv)
            y_ref[h:h + 1, :] = jnp.sum(fv, axis=0, keepdims=True)
        o_ref[...] = lax.dot_general(y_ref[...], _rel_onehot(), (((1,), (1,)), ((), ())),
                                     precision=HIGHEST, preferred_element_type=F32)

    return pl.pallas_call(
        body, out_shape=jax.ShapeDtypeStruct((16, N_REL), F32),
        scratch_shapes=[pltpu.VMEM((16, KEY_WIN), F32)], name="bias_grad")(dbias)


def _attn_windows(seq):
    out = []
    for j in range(seq // Q_BLOCK):
        r0 = j * Q_BLOCK
        k0 = max(0, r0 - 8 * CHUNK)
        width = r0 + Q_BLOCK - k0
        out.append((r0, k0, width, KEY_WIN - width))
    return out


def attn_fwd(z, gq2, gk2, bias, batch, seq):
    n = z.shape[0]
    pairs = TOK_WIDTH // LANES

    def body(q_ref, k_ref, v_ref, gq_ref, gk_ref, b_ref, o_ref, qs_s, kn_s):
        masks = _group_masks(LANES)
        qs_s[...] = (_head_norm(q_ref[...].astype(F32), gq_ref[...], masks)[0] * ATTN_SCALE).astype(BF16)
        kn_s[...] = _head_norm(k_ref[...].astype(F32), gk_ref[...], masks)[0].astype(BF16)
        for r0, k0, width, c0 in _attn_windows(seq):
            qb = qs_s[r0:r0 + Q_BLOCK, :]
            kw = kn_s[k0:k0 + width, :]
            vw = v_ref[k0:k0 + width, :]
            out = jnp.zeros((Q_BLOCK, LANES), F32)
            for h, msk in enumerate(masks):
                qh = jnp.where(msk, qb, jnp.zeros_like(qb))
                s = _dot(qh, kw, 1, 1) + b_ref[h, :, c0:KEY_WIN]
                p = _softmax_rows(s).astype(BF16)
                out = jnp.where(msk, _dot(p, vw, 1, 0), out)
            o_ref[r0:r0 + Q_BLOCK, :] = out.astype(o_ref.dtype)

    def col(off):
        return pl.BlockSpec((seq, LANES), lambda b, p: (b, off + p))

    vec = pl.BlockSpec((1, LANES), lambda b, p: (0, 0))
    return pl.pallas_call(
        body, out_shape=jax.ShapeDtypeStruct((n, D_MODEL), BF16), grid=(batch, pairs),
        in_specs=[col(0), col(pairs), col(2 * pairs), vec, vec,
                  pl.BlockSpec((2, Q_BLOCK, KEY_WIN), lambda b, p: (p, 0, 0))],
        out_specs=pl.BlockSpec((seq, LANES), lambda b, p: (b, p)),
        scratch_shapes=[pltpu.VMEM((seq, LANES), BF16), pltpu.VMEM((seq, LANES), BF16)],
        compiler_params=_params("parallel", "arbitrary"), name="attn_fwd")(z, z, z, gq2, gk2, bias)


def attn_bwd(z, dcat, gq2, gk2, bias, batch, seq):
    n = z.shape[0]
    pairs = TOK_WIDTH // LANES

    def body(q_ref, k_ref, v_ref, do_ref, gq_ref, gk_ref, b_ref,
             dz_ref, db_ref, dgq_ref, dgk_ref, qs_s, kn_s, dqn_s, dkn_s, dv_s, dk_o, dv_o):
        pi, bi, which = pl.program_id(0), pl.program_id(1), pl.program_id(2)

        @pl.when(which == 0)
        def _():
            masks = _group_masks(LANES)

            @pl.when(bi == 0)
            def _():
                db_ref[...] = jnp.zeros_like(db_ref)

            @pl.when((bi == 0) & (pi == 0))
            def _():
                dgq_ref[...] = jnp.zeros_like(dgq_ref)
                dgk_ref[...] = jnp.zeros_like(dgk_ref)

            qn, qhat, rq = _head_norm(q_ref[...].astype(F32), gq_ref[...], masks)
            kn, khat, rk = _head_norm(k_ref[...].astype(F32), gk_ref[...], masks)
            qs_s[...] = (qn * ATTN_SCALE).astype(BF16)
            kn_s[...] = kn.astype(BF16)
            dkn_s[...] = jnp.zeros_like(dkn_s)
            dv_s[...] = jnp.zeros_like(dv_s)
            for r0, k0, width, c0 in _attn_windows(seq):
                qb = qs_s[r0:r0 + Q_BLOCK, :]
                dob = do_ref[r0:r0 + Q_BLOCK, :]
                kw = kn_s[k0:k0 + width, :]
                vw = v_ref[k0:k0 + width, :]
                dq_acc = jnp.zeros((Q_BLOCK, LANES), F32)
                dk_acc = jnp.zeros((width, LANES), F32)
                dv_acc = jnp.zeros((width, LANES), F32)
                for h, msk in enumerate(masks):
                    qh = jnp.where(msk, qb, jnp.zeros_like(qb))
                    doh = jnp.where(msk, dob, jnp.zeros_like(dob))
                    p = _softmax_rows(_dot(qh, kw, 1, 1) + b_ref[h, :, c0:KEY_WIN])
                    dp = _dot(doh, vw, 1, 1)
                    ds = p * (dp - jnp.sum(p * dp, axis=-1, keepdims=True))
                    db_ref[h, :, c0:KEY_WIN] += ds
                    dsb = ds.astype(BF16)
                    dq_acc = jnp.where(msk, _dot(dsb, kw, 1, 0), dq_acc)
                    dk_acc = jnp.where(msk, _dot(dsb, qb, 0, 0), dk_acc)
                    dv_acc = jnp.where(msk, _dot(p.astype(BF16), dob, 0, 0), dv_acc)
                dqn_s[r0:r0 + Q_BLOCK, :] = dq_acc * ATTN_SCALE
                dkn_s[k0:k0 + width, :] += dk_acc
                dv_s[k0:k0 + width, :] += dv_acc
            dq, dgq = _head_norm_bwd(dqn_s[...], qhat, rq, gq_ref[...], masks)
            dk, dgk = _head_norm_bwd(dkn_s[...], khat, rk, gk_ref[...], masks)
            dz_ref[...] = dq.astype(dz_ref.dtype)
            dk_o[...] = dk.astype(dk_o.dtype)
            dv_o[...] = dv_s[...].astype(dv_o.dtype)
            dgq_ref[...] += dgq
            dgk_ref[...] += dgk

        @pl.when(which == 1)
        def _():
            dz_ref[...] = dk_o[...]

        @pl.when(which == 2)
        def _():
            dz_ref[...] = dv_o[...]

    def ahead(p, b, t):
        nb = b + jnp.where(t > 0, 1, 0)
        wrap = jnp.where(nb >= batch, 1, 0)
        return jnp.minimum(p + wrap, pairs - 1), nb - wrap * batch

    def col(off):
        def index(p, b, t):
            np_, nb = ahead(p, b, t)
            return nb, off + np_
        return pl.BlockSpec((seq, LANES), index)

    vec = pl.BlockSpec((1, LANES), lambda p, b, t: (0, 0))
    blk = pl.BlockSpec((2, Q_BLOCK, KEY_WIN), lambda p, b, t: (p, 0, 0))
    blk_in = pl.BlockSpec((2, Q_BLOCK, KEY_WIN), lambda p, b, t: (ahead(p, b, t)[0], 0, 0))
    v_shape = jax.ShapeDtypeStruct((1, LANES), F32)
    return pl.pallas_call(
        body,
        out_shape=(jax.ShapeDtypeStruct(z.shape, BF16), jax.ShapeDtypeStruct(bias.shape, F32), v_shape, v_shape),
        grid=(pairs, batch, 3),
        in_specs=[col(0), col(pairs), col(2 * pairs), col(0), vec, vec, blk_in],
        out_specs=(pl.BlockSpec((seq, LANES), lambda p, b, t: (b, t * pairs + p)), blk, vec, vec),
        scratch_shapes=[pltpu.VMEM((seq, LANES), BF16), pltpu.VMEM((seq, LANES), BF16),
                        pltpu.VMEM((seq, LANES), F32), pltpu.VMEM((seq, LANES), F32), pltpu.VMEM((seq, LANES), F32),
                        pltpu.VMEM((seq, LANES), BF16), pltpu.VMEM((seq, LANES), BF16)],
        compiler_params=_params("arbitrary", "arbitrary", "arbitrary"), name="attn_bwd")(
            z, z, z, dcat, gq2, gk2, bias)


MEM_ROWS = 512


def memattn_fwd(z, kv, gq4, gk4, cat, batch, seq, qcol, name):
    mtok = kv.shape[0] // batch
    rows = min(MEM_ROWS, seq)

    def body(q_ref, kv_ref, gq_ref, gk_ref, cat_ref, o_ref):
        del cat_ref
        masks = _group_masks(MEM_WIDTH)
        kn = _head_norm(kv_ref[:, 0:MEM_WIDTH], gk_ref[...], masks)[0].astype(BF16)
        vm = kv_ref[:, MEM_WIDTH:2 * MEM_WIDTH].astype(BF16)
        for t in range(seq // rows):
            sl = slice(t * rows, (t + 1) * rows)
            qs = (_head_norm(q_ref[sl, :].astype(F32), gq_ref[...], masks)[0] * ATTN_SCALE).astype(BF16)
            out = jnp.zeros((rows, MEM_WIDTH), F32)
            for msk in masks:
                qh = jnp.where(msk, qs, jnp.zeros_like(qs))
                p = _softmax_rows(_dot(qh, kn, 1, 1)).astype(BF16)
                out = jnp.where(msk, _dot(p, vm, 1, 0), out)
            o_ref[sl, :] = out.astype(o_ref.dtype)

    vec = pl.BlockSpec((1, MEM_WIDTH), lambda b: (0, 0))
    return pl.pallas_call(
        body, out_shape=jax.ShapeDtypeStruct(cat.shape, cat.dtype), grid=(batch,),
        in_specs=[pl.BlockSpec((seq, MEM_WIDTH), lambda b: (b, qcol)),
                  pl.BlockSpec((mtok, 2 * MEM_WIDTH), lambda b: (b, 0)), vec, vec, ANY],
        out_specs=pl.BlockSpec((seq, MEM_WIDTH), lambda b: (b, TOK_WIDTH // MEM_WIDTH)),
        input_output_aliases={4: 0},
        compiler_params=_params("parallel"), name=name)(z, kv, gq4, gk4, cat)


def memattn_bwd(z, kv, dcat, gq4, gk4, dz, batch, seq, qcol, name):
    mtok = kv.shape[0] // batch
    rows = min(MEM_ROWS, seq)

    def body(q_ref, kv_ref, do_ref, gq_ref, gk_ref, dz_in_ref, dq_ref, dkv_ref, dgq_ref, dgk_ref):
        del dz_in_ref
        @pl.when(pl.program_id(0) == 0)
        def _():
            dgq_ref[...] = jnp.zeros_like(dgq_ref)
            dgk_ref[...] = jnp.zeros_like(dgk_ref)

        masks = _group_masks(MEM_WIDTH)
        kn_f, khat, rk = _head_norm(kv_ref[:, 0:MEM_WIDTH], gk_ref[...], masks)
        kn = kn_f.astype(BF16)
        vm = kv_ref[:, MEM_WIDTH:2 * MEM_WIDTH].astype(BF16)
        dkn = jnp.zeros((mtok, MEM_WIDTH), F32)
        dvm = jnp.zeros((mtok, MEM_WIDTH), F32)
        dgq = jnp.zeros((1, MEM_WIDTH), F32)
        for t in range(seq // rows):
            sl = slice(t * rows, (t + 1) * rows)
            qn_f, qhat, rq = _head_norm(q_ref[sl, :].astype(F32), gq_ref[...], masks)
            qs = (qn_f * ATTN_SCALE).astype(BF16)
            dob = do_ref[sl, :]
            dqn = jnp.zeros((rows, MEM_WIDTH), F32)
            for msk in masks:
                qh = jnp.where(msk, qs, jnp.zeros_like(qs))
                doh = jnp.where(msk, dob, jnp.zeros_like(dob))
                p = _softmax_rows(_dot(qh, kn, 1, 1))
                dp = _dot(doh, vm, 1, 1)
                ds = p * (dp - jnp.sum(p * dp, axis=-1, keepdims=True))
                dsb = ds.astype(BF16)
                dqn = jnp.where(msk, _dot(dsb, kn, 1, 0), dqn)
                dkn = dkn + jnp.where(msk, _dot(dsb, qs, 0, 0), 0.0)
                dvm = dvm + jnp.where(msk, _dot(p.astype(BF16), dob, 0, 0), 0.0)
            dq, dg = _head_norm_bwd(dqn * ATTN_SCALE, qhat, rq, gq_ref[...], masks)
            dq_ref[sl, :] = dq.astype(dq_ref.dtype)
            dgq = dgq + dg
        dk, dgk = _head_norm_bwd(dkn, khat, rk, gk_ref[...], masks)
        dkv_ref[:, 0:MEM_WIDTH] = dk
        dkv_ref[:, MEM_WIDTH:2 * MEM_WIDTH] = dvm
        dgq_ref[...] += dgq
        dgk_ref[...] += dgk

    vec = pl.BlockSpec((1, MEM_WIDTH), lambda b: (0, 0))
    kv_spec = pl.BlockSpec((mtok, 2 * MEM_WIDTH), lambda b: (b, 0))
    v_shape = jax.ShapeDtypeStruct((1, MEM_WIDTH), F32)
    q_spec = pl.BlockSpec((seq, MEM_WIDTH), lambda b: (b, qcol))
    return pl.pallas_call(
        body,
        out_shape=(jax.ShapeDtypeStruct(dz.shape, dz.dtype), jax.ShapeDtypeStruct(kv.shape, F32), v_shape, v_shape),
        grid=(batch,),
        in_specs=[q_spec, kv_spec, pl.BlockSpec((seq, MEM_WIDTH), lambda b: (b, TOK_WIDTH // MEM_WIDTH)), vec, vec, ANY],
        out_specs=(q_spec, kv_spec, vec, vec),
        input_output_aliases={5: 0},
        compiler_params=_params("arbitrary"), name=name)(z, kv, dcat, gq4, gk4, dz)


def mem_prep(mem, gain, wkv, name):
    t, d = mem.shape

    def body(m_ref, g_ref, w_ref, n_ref, kv_ref):
        mv = m_ref[...]
        r = lax.rsqrt(jnp.mean(mv * mv, axis=-1, keepdims=True) + NORM_EPS)
        nv = (mv * r * g_ref[...]).astype(BF16)
        n_ref[...] = nv
        kv_ref[...] = _dot(nv, w_ref[...], 1, 0)

    vmem = pl.BlockSpec(memory_space=pltpu.VMEM)
    return pl.pallas_call(
        body, out_shape=(jax.ShapeDtypeStruct((t, d), BF16), jax.ShapeDtypeStruct((t, wkv.shape[1]), F32)),
        in_specs=[vmem, vmem, vmem], out_specs=(vmem, vmem),
        compiler_params=pltpu.CompilerParams(vmem_limit_bytes=VMEM_LIMIT), name=name)(mem, gain, wkv)


def mem_grads(dkv, wkv, mem, mem_n, name):
    t, d = mem.shape

    def body(dkv_ref, w_ref, m_ref, n_ref, dw_ref, dg_ref):
        dkv_b = dkv_ref[...].astype(BF16)
        dw_ref[...] = _dot(n_ref[...], dkv_b, 0, 0).astype(dw_ref.dtype)
        dn = _dot(dkv_b, w_ref[...], 1, 1)
        mv = m_ref[...]
        r = lax.rsqrt(jnp.mean(mv * mv, axis=-1, keepdims=True) + NORM_EPS)
        dg_ref[...] = jnp.sum(dn * (mv * r), axis=0, keepdims=True)

    vmem = pl.BlockSpec(memory_space=pltpu.VMEM)
    return pl.pallas_call(
        body, out_shape=(jax.ShapeDtypeStruct(wkv.shape, BF16), jax.ShapeDtypeStruct((1, d), F32)),
        in_specs=[vmem, vmem, vmem, vmem], out_specs=(vmem, vmem),
        compiler_params=pltpu.CompilerParams(vmem_limit_bytes=VMEM_LIMIT), name=name)(dkv, wkv, mem, mem_n)


CONV_ROWS = 256


def _glu(a_ref, g_ref):
    return a_ref[...].astype(F32) * _sigmoid(g_ref[...].astype(F32))


def _layer_norm_stats(y):
    mu = jnp.mean(y, axis=-1, keepdims=True)
    yc = y - mu
    rstd = lax.rsqrt(jnp.mean(yc * yc, axis=-1, keepdims=True) + NORM_EPS)
    return yc * rstd, rstd


CONV_WIN = CONV_HALO + CONV_ROWS
SUBLANES = 8
SHIFT_ROWS = CONV_WIN - SUBLANES


def _preshift(win, shifted):
    for s in range(1, SUBLANES):
        shifted[s - 1, :, :] = win[s:s + SHIFT_ROWS, :]


TAP_ROWS = 64
TAP_TILES = [(r0, slice(c0, c0 + LANES)) for c0 in range(0, TOK_WIDTH, LANES) for r0 in range(0, CONV_ROWS, TAP_ROWS)]


def _tap(win, shifted, off, r0, lanes):
    s = off % SUBLANES
    base = off - s + r0
    if s == 0:
        return win[base:base + TAP_ROWS, lanes]
    return shifted[s - 1, base:base + TAP_ROWS, lanes]


def _fold_rows(x):
    return jnp.sum(x.reshape(TAP_ROWS // SUBLANES, SUBLANES, LANES), axis=0)


def conv_fwd(z, cw, cb, lg, lb, batch, seq):
    n = z.shape[0]
    nt = seq // CONV_ROWS
    sub = CONV_ROWS // CONV_HALO
    lead = CONV_HALO - (CONV_W - 1)

    def body(a_ref, g_ref, ap_ref, gp_ref, cw_ref, cb_ref, lg_ref, lb_ref, o_ref, y_ref, win, shifted):
        first = pl.program_id(1) == 0
        win[0:CONV_HALO, :] = jnp.where(first, 0.0, _glu(ap_ref, gp_ref))
        win[CONV_HALO:CONV_WIN, :] = _glu(a_ref, g_ref)
        _preshift(win, shifted)
        for r0, lanes in TAP_TILES:
            acc = jnp.zeros((TAP_ROWS, LANES), F32) + cb_ref[:, lanes]
            for w in range(CONV_W):
                acc = acc + _tap(win, shifted, lead + w, r0, lanes) * cw_ref[w:w + 1, lanes]
            y_ref[r0:r0 + TAP_ROWS, lanes] = acc
        yh, _ = _layer_norm_stats(y_ref[...])
        t = yh * lg_ref[...] + lb_ref[...]
        o_ref[...] = (t * _sigmoid(t)).astype(o_ref.dtype)

    def cur(c):
        return pl.BlockSpec((CONV_ROWS, TOK_WIDTH), lambda b, i: (b * nt + i, c))

    def prev(c):
        return pl.BlockSpec((CONV_HALO, TOK_WIDTH), lambda b, i: (jnp.maximum((b * nt + i) * sub - 1, 0), c))

    vec = pl.BlockSpec((1, TOK_WIDTH), lambda b, i: (0, 0))
    return pl.pallas_call(
        body, out_shape=(jax.ShapeDtypeStruct((n, D_MODEL), BF16), jax.ShapeDtypeStruct((n, TOK_WIDTH), F32)),
        grid=(batch, nt),
        in_specs=[cur(0), cur(1), prev(0), prev(1), pl.BlockSpec((32, TOK_WIDTH), lambda b, i: (0, 0)), vec, vec, vec],
        out_specs=(cur(0), cur(0)),
        scratch_shapes=[pltpu.VMEM((CONV_WIN, TOK_WIDTH), F32), pltpu.VMEM((SUBLANES - 1, SHIFT_ROWS, TOK_WIDTH), F32)],
        compiler_params=_params("parallel", "arbitrary"), name="conv_fwd")(z, z, z, z, cw, cb, lg, lb)


def conv_bwd(z, y, dcat, cw, lg, lb, batch, seq):
    n = z.shape[0]
    nt = seq // CONV_ROWS
    sub = CONV_ROWS // CONV_HALO
    lead = CONV_HALO - (CONV_W - 1)
    last_blk = n // CONV_HALO - 1

    def body(a_ref, g_ref, ap_ref, gp_ref, y_ref, yn_ref, do_ref, don_ref, cw_ref, lg_ref, lb_ref,
             dz_ref, dcw_ref, dsm_ref, win, shifted, dyw, dshifted, dg_o):
        b, i, which = pl.program_id(0), pl.program_id(1), pl.program_id(2)

        @pl.when(which == 0)
        def _():
            first, last = i == 0, i == nt - 1

            @pl.when((b == 0) & (i == 0))
            def _():
                dcw_ref[...] = jnp.zeros_like(dcw_ref)
                dsm_ref[...] = jnp.zeros_like(dsm_ref)

            win[0:CONV_HALO, :] = jnp.where(first, 0.0, _glu(ap_ref, gp_ref))
            win[CONV_HALO:CONV_WIN, :] = _glu(a_ref, g_ref)
            _preshift(win, shifted)
            yv = jnp.concatenate([y_ref[...], yn_ref[...]], axis=0)
            yh, rstd = _layer_norm_stats(yv)
            t = yh * lg_ref[...] + lb_ref[...]
            st = _sigmoid(t)
            dout = jnp.concatenate(
                [do_ref[...].astype(F32), jnp.where(last, 0.0, don_ref[...].astype(F32))], axis=0)
            dt = dout * st * (1.0 + t * (1.0 - st))
            dyh = dt * lg_ref[...]
            dy = rstd * (dyh - jnp.mean(dyh, axis=-1, keepdims=True)
                         - yh * jnp.mean(dyh * yh, axis=-1, keepdims=True))
            dyw[...] = dy
            _preshift(dyw, dshifted)
            dsm_ref[0:1, :] += jnp.sum(dy[0:CONV_ROWS], axis=0, keepdims=True)
            dsm_ref[1:2, :] += jnp.sum((dt * yh)[0:CONV_ROWS], axis=0, keepdims=True)
            dsm_ref[2:3, :] += jnp.sum(dt[0:CONV_ROWS], axis=0, keepdims=True)
            for c0 in range(0, TOK_WIDTH, LANES):
                lanes = slice(c0, c0 + LANES)
                dcw_acc = [jnp.zeros((SUBLANES, LANES), F32) for _ in range(CONV_W)]
                for r0 in range(0, CONV_ROWS, TAP_ROWS):
                    dyt = dyw[r0:r0 + TAP_ROWS, lanes]
                    dglu = jnp.zeros((TAP_ROWS, LANES), F32)
                    for w in range(CONV_W):
                        dcw_acc[w] = dcw_acc[w] + _fold_rows(dyt * _tap(win, shifted, lead + w, r0, lanes))
                        dglu = dglu + _tap(dyw, dshifted, CONV_W - 1 - w, r0, lanes) * cw_ref[w:w + 1, lanes]
                    avt = a_ref[r0:r0 + TAP_ROWS, lanes].astype(F32)
                    sgt = _sigmoid(g_ref[r0:r0 + TAP_ROWS, lanes].astype(F32))
                    dz_ref[r0:r0 + TAP_ROWS, lanes] = (dglu * sgt).astype(dz_ref.dtype)
                    dg_o[r0:r0 + TAP_ROWS, lanes] = (dglu * avt * sgt * (1.0 - sgt)).astype(dg_o.dtype)
                for w in range(CONV_W):
                    dcw_ref[w:w + 1, lanes] += jnp.sum(dcw_acc[w], axis=0, keepdims=True)

        @pl.when(which == 1)
        def _():
            dz_ref[...] = dg_o[...]

    def ahead(b, i, t):
        return jnp.minimum(b * nt + i + t, batch * nt - 1)

    def cur(c):
        return pl.BlockSpec((CONV_ROWS, TOK_WIDTH), lambda b, i, t: (ahead(b, i, t), c))

    def prev(c):
        return pl.BlockSpec((CONV_HALO, TOK_WIDTH), lambda b, i, t: (jnp.maximum(ahead(b, i, t) * sub - 1, 0), c))

    nxt = pl.BlockSpec((CONV_HALO, TOK_WIDTH),
                       lambda b, i, t: (jnp.minimum((ahead(b, i, t) + 1) * sub, last_blk), 0))
    vec = pl.BlockSpec((1, TOK_WIDTH), lambda b, i, t: (0, 0))
    full32 = pl.BlockSpec((32, TOK_WIDTH), lambda b, i, t: (0, 0))
    return pl.pallas_call(
        body,
        out_shape=(jax.ShapeDtypeStruct(z.shape, BF16), jax.ShapeDtypeStruct((32, TOK_WIDTH), F32),
                   jax.ShapeDtypeStruct((8, TOK_WIDTH), F32)),
        grid=(batch, nt, 2),
        in_specs=[cur(0), cur(1), prev(0), prev(1), cur(0), nxt, cur(0), nxt, full32, vec, vec],
        out_specs=(pl.BlockSpec((CONV_ROWS, TOK_WIDTH), lambda b, i, t: (b * nt + i, t)), full32,
                   pl.BlockSpec((8, TOK_WIDTH), lambda b, i, t: (0, 0))),
        scratch_shapes=[pltpu.VMEM((CONV_WIN, TOK_WIDTH), F32), pltpu.VMEM((SUBLANES - 1, SHIFT_ROWS, TOK_WIDTH), F32),
                        pltpu.VMEM((CONV_WIN, TOK_WIDTH), F32), pltpu.VMEM((SUBLANES - 1, SHIFT_ROWS, TOK_WIDTH), F32),
                        pltpu.VMEM((CONV_ROWS, TOK_WIDTH), BF16)],
        compiler_params=_params("arbitrary", "arbitrary", "arbitrary"), name="conv_bwd")(
            z, z, z, z, y, y, dcat, dcat, cw, lg, lb)


def _place():
    return lax.axis_index("x"), lax.axis_index("y"), lax.axis_index("c")


def _other_chips(x, y):
    return [(1 - x, y), (x, 1 - y), (1 - x, 1 - y)]


def reduce_small(arrays):
    na = len(arrays)

    def body(*refs):
        ins, outs, bufs = refs[:na], refs[na:2 * na], refs[2 * na:3 * na]
        send_sems, recv_sems = refs[3 * na:]
        x, y, c = _place()
        me = 4 * x + 2 * y + c
        copies = []
        for a in range(na):
            bufs[a][me] = ins[a][...]
            for k in range(1, N_DEV):
                cp = pltpu.make_async_remote_copy(
                    src_ref=ins[a], dst_ref=bufs[a].at[me], send_sem=send_sems.at[a, k - 1],
                    recv_sem=recv_sems.at[a, k - 1],
                    device_id=(x ^ (k >> 2), y ^ ((k >> 1) & 1), c ^ (k & 1)), device_id_type=MESH)
                cp.start()
                copies.append(cp)
        for a in range(na):
            for k in range(1, N_DEV):
                src = 4 * (x ^ (k >> 2)) + 2 * (y ^ ((k >> 1) & 1)) + (c ^ (k & 1))
                pltpu.make_async_remote_copy(
                    src_ref=ins[a], dst_ref=bufs[a].at[src], send_sem=send_sems.at[a, k - 1],
                    recv_sem=recv_sems.at[a, k - 1], device_id=(x, y, c), device_id_type=MESH).wait_recv()
        for cp in copies:
            cp.wait_send()
        for a in range(na):
            total = bufs[a][0]
            for dev in range(1, N_DEV):
                total = total + bufs[a][dev]
            outs[a][...] = total

    vmem = pl.BlockSpec(memory_space=pltpu.VMEM)
    return pl.pallas_call(
        body, out_shape=tuple(jax.ShapeDtypeStruct(a.shape, F32) for a in arrays),
        in_specs=[vmem] * na, out_specs=tuple([vmem] * na),
        scratch_shapes=[pltpu.VMEM((N_DEV,) + a.shape, F32) for a in arrays]
        + [pltpu.SemaphoreType.DMA((na, N_DEV - 1)), pltpu.SemaphoreType.DMA((na, N_DEV - 1))],
        compiler_params=pltpu.CompilerParams(vmem_limit_bytes=VMEM_LIMIT), name="small_reduce")(*arrays)


def adamw_small(ws, gs, ms, vs):
    na = len(ws)
    c1 = 1.0 / (1.0 - ADAM_B1 ** ADAM_STEP)
    c2 = 1.0 / (1.0 - ADAM_B2 ** ADAM_STEP)

    def body(*refs):
        w_refs, g_refs, m_refs, v_refs = (refs[i * na:(i + 1) * na] for i in range(4))
        d_refs, nm_refs, nv_refs = (refs[(4 + i) * na:(5 + i) * na] for i in range(3))
        for a in range(na):
            gv = g_refs[a][...]
            nm = ADAM_B1 * m_refs[a][...] + (1.0 - ADAM_B1) * gv
            nv = ADAM_B2 * v_refs[a][...] + (1.0 - ADAM_B2) * (gv * gv)
            nm_refs[a][...] = nm
            nv_refs[a][...] = nv
            d_refs[a][...] = -ADAM_LR * ((nm * c1) / (jnp.sqrt(nv * c2) + ADAM_EPS) + ADAM_WD * w_refs[a][...])

    vmem = pl.BlockSpec(memory_space=pltpu.VMEM)
    shapes = tuple(jax.ShapeDtypeStruct(w.shape, F32) for w in ws)
    outs = pl.pallas_call(
        body, out_shape=shapes * 3, in_specs=[vmem] * (4 * na), out_specs=tuple([vmem] * (3 * na)),
        compiler_params=pltpu.CompilerParams(vmem_limit_bytes=VMEM_LIMIT), name="adamw_small")(*ws, *gs, *ms, *vs)
    return outs[:na], outs[na:2 * na], outs[2 * na:]


def gather_weights(shards, name, collective_id):
    nw = len(shards)
    ns = [s.shape[0] for s in shards]
    in_refs = [jax.new_ref(s, memory_space=pltpu.MemorySpace.HBM) for s in shards]
    out_refs = [jax.empty_ref(jax.ShapeDtypeStruct((N_DEV * s.shape[0], s.shape[1]), s.dtype),
                              memory_space=pltpu.MemorySpace.HBM) for s in shards]

    @pl.kernel(mesh=plsc.ScalarSubcoreMesh(axis_name="seq", num_cores=1), name=name,
               scratch_types=(pltpu.SemaphoreType.DMA((nw, 7)), pltpu.SemaphoreType.DMA((nw, 7)),
                              pltpu.SemaphoreType.DMA((nw,))),
               compiler_params=pltpu.CompilerParams(collective_id=collective_id))
    def launch(send_sems, recv_sems, local_sems):
        x, y, c = _place()
        me, sib = (x, y, c), (x, y, 1 - c)
        chips = _other_chips(x, y)
        barrier = pltpu.get_barrier_semaphore()
        for peer in [sib] + [(*chip, c) for chip in chips]:
            pl.semaphore_signal(barrier, inc=1, device_id=peer, device_id_type=MESH)
        pl.semaphore_wait(barrier, 4)

        def rows(w, dev):
            return out_refs[w].at[pl.ds((4 * dev[0] + 2 * dev[1] + dev[2]) * ns[w], ns[w]), :]

        def copy(w, k, block, to, src=None):
            return pltpu.make_async_remote_copy(
                src_ref=rows(w, block) if src is None else src, dst_ref=rows(w, block),
                send_sem=send_sems.at[w, k], recv_sem=recv_sems.at[w, k], device_id=to, device_id_type=MESH)

        started, sends = [], []
        for w in range(nw):
            mine = pltpu.make_async_copy(in_refs[w], rows(w, me), local_sems.at[w])
            mine.start()
            started.append(mine)
            first = [copy(w, 0, me, sib, src=in_refs[w])]
            first += [copy(w, 1 + j, me, (*chip, c), src=in_refs[w]) for j, chip in enumerate(chips)]
            for cp in first:
                cp.start()
            sends += first
        for w in range(nw):
            for j, chip in enumerate(chips):
                copy(w, 1 + j, (*chip, c), me).wait_recv()
                fwd = copy(w, 4 + j, (*chip, c), sib)
                fwd.start()
                sends.append(fwd)
        for w in range(nw):
            copy(w, 0, sib, me).wait_recv()
            for j, chip in enumerate(chips):
                copy(w, 4 + j, (*chip, 1 - c), me).wait_recv()
        for cp in sends:
            cp.wait_send()
        for mine in started:
            mine.wait()

    launch()
    return [r[...] for r in out_refs]


def _sequencer_exchange(sources, out_rows, peers_of, copies_of, name, collective_id):
    nw = len(sources)
    in_refs = [jax.new_ref(s, memory_space=pltpu.MemorySpace.HBM) for s in sources]
    out_refs = [jax.empty_ref(jax.ShapeDtypeStruct((rows, s.shape[1]), s.dtype), memory_space=pltpu.MemorySpace.HBM)
                for rows, s in zip(out_rows, sources)]
    per = len(copies_of(0, 0, 0, 0))

    @pl.kernel(mesh=plsc.ScalarSubcoreMesh(axis_name="seq", num_cores=1), name=name,
               scratch_types=(pltpu.SemaphoreType.DMA((nw, per)), pltpu.SemaphoreType.DMA((nw, per))),
               compiler_params=pltpu.CompilerParams(collective_id=collective_id))
    def launch(send_sems, recv_sems):
        x, y, c = _place()
        peers = peers_of(x, y, c)
        barrier = pltpu.get_barrier_semaphore()
        for peer in peers:
            pl.semaphore_signal(barrier, inc=1, device_id=peer, device_id_type=MESH)
        pl.semaphore_wait(barrier, len(peers))
        copies = []
        for w in range(nw):
            for k, (src_blk, dst_blk, rows, peer) in enumerate(copies_of(x, y, c, w)):
                cp = pltpu.make_async_remote_copy(
                    src_ref=in_refs[w].at[pl.ds(src_blk * rows, rows), :],
                    dst_ref=out_refs[w].at[pl.ds(dst_blk * rows, rows), :],
                    send_sem=send_sems.at[w, k], recv_sem=recv_sems.at[w, k], device_id=peer, device_id_type=MESH)
                cp.start()
                copies.append(cp)
        for cp in copies:
            cp.wait_recv()
        for cp in copies:
            cp.wait_send()

    launch()
    return [r[...] for r in out_refs]


def scatter_to_sibling(grads, name, collective_id):
    ns = [g.shape[0] // N_DEV for g in grads]
    return _sequencer_exchange(
        grads, [4 * n for n in ns],
        lambda x, y, c: [(x, y, 1 - c)],
        lambda x, y, c, w: [(2 * q + 1 - c, q, ns[w], (x, y, 1 - c)) for q in range(4)],
        name, collective_id)


def scatter_to_chips(parts, name, collective_id):
    ns = [p.shape[0] // 4 for p in parts]
    return _sequencer_exchange(
        parts, [3 * n for n in ns],
        lambda x, y, c: [(*chip, c) for chip in _other_chips(x, y)],
        lambda x, y, c, w: [(2 * chip[0] + chip[1], j, ns[w], (*chip, c)) for j, chip in enumerate(_other_chips(x, y))],
        name, collective_id)


def add_sibling(grads, landeds, core, name):
    nw = len(grads)

    def body(c_ref, *refs):
        for w in range(nw):
            g_ref, l_ref, o_ref = refs[2 * w], refs[2 * w + 1], refs[2 * nw + w]
            o_ref[...] = (g_ref[...].astype(F32) + l_ref[...].astype(F32)).astype(o_ref.dtype)

    in_specs, out_specs, args = [], [], []
    for g, ld in zip(grads, landeds):
        n, cols = ld.shape[0] // 4, g.shape[1]
        in_specs += [pl.BlockSpec((n, cols), lambda q, c_ref: (2 * q + c_ref[0], 0)),
                     pl.BlockSpec((n, cols), lambda q, c_ref: (q, 0))]
        out_specs.append(pl.BlockSpec((n, cols), lambda q, c_ref: (q, 0)))
        args += [g, ld]
    grid_spec = pltpu.PrefetchScalarGridSpec(
        num_scalar_prefetch=1, grid=(4,), in_specs=in_specs, out_specs=tuple(out_specs))
    return pl.pallas_call(
        body, out_shape=tuple(jax.ShapeDtypeStruct(ld.shape, ld.dtype) for ld in landeds), grid_spec=grid_spec,
        compiler_params=_params("arbitrary"), name=name)(core, *args)


ADAMW_HALVES = 2


def adamw_shards(items, chip, name):
    c1 = 1.0 / (1.0 - ADAM_B1 ** ADAM_STEP)
    c2 = 1.0 / (1.0 - ADAM_B2 ** ADAM_STEP)
    ni = len(items)

    def body(q_ref, *refs):
        outs = refs[len(refs) - 4 * ni:]
        for k in range(ni):
            w_ref, m_ref, v_ref, p_ref, l0_ref, l1_ref, l2_ref = refs[7 * k:7 * k + 7]
            g_ref, d_ref, nm_ref, nv_ref = outs[4 * k:4 * k + 4]
            gv = ((p_ref[...].astype(F32) + l0_ref[...].astype(F32)) + l1_ref[...].astype(F32)) + l2_ref[...].astype(F32)
            nm = ADAM_B1 * m_ref[...] + (1.0 - ADAM_B1) * gv
            nv = ADAM_B2 * v_ref[...] + (1.0 - ADAM_B2) * (gv * gv)
            g_ref[...] = gv
            nm_ref[...] = nm
            nv_ref[...] = nv
            d_ref[...] = -ADAM_LR * ((nm * c1) / (jnp.sqrt(nv * c2) + ADAM_EPS) + ADAM_WD * w_ref[...])

    sub = ADAMW_HALVES
    in_specs, out_specs, out_shape, args, donated = [], [], [], [chip], []
    for layer, w, m, v, part, landed, earlier in items:
        rows, cols = landed.shape[0] // (3 * sub), w.shape[1]

        def block(first, rows=rows, cols=cols):
            return pl.BlockSpec((rows, cols), lambda i, q_ref: (first(q_ref) * sub + i, 0))

        own = block(lambda q_ref, layer=layer: layer)
        in_specs += [own, own, own, block(lambda q_ref: q_ref[0])] + [block(lambda q_ref, j=j: j) for j in range(3)]
        args += [w, m, v, part, landed, landed, landed]
        out_specs += [own] * 4
        out_shape += [jax.ShapeDtypeStruct(w.shape, F32)] * 4
        donated.append(earlier)
    aliases = {}
    for k, earlier in enumerate(donated):
        if earlier is not None:
            for j in range(4):
                aliases[len(args)] = 4 * k + j
                in_specs.append(ANY)
                args.append(earlier[j])
    grid_spec = pltpu.PrefetchScalarGridSpec(
        num_scalar_prefetch=1, grid=(sub,), in_specs=in_specs, out_specs=tuple(out_specs))
    outs = pl.pallas_call(
        body, out_shape=tuple(out_shape), grid_spec=grid_spec, input_output_aliases=aliases,
        compiler_params=_params("arbitrary"), name=name)(*args)
    return [tuple(outs[4 * k:4 * k + 4]) for k in range(ni)]


def _pack(arrays):
    flat = jnp.concatenate([a.reshape(-1).astype(F32) for a in arrays])
    pad = (-flat.shape[0]) % (8 * LANES)
    return jnp.pad(flat, (0, pad)).reshape(-1, LANES)


def _unpack(slab, shapes):
    flat = slab.reshape(slab.shape[:-2] + (-1,))
    out, off = [], 0
    for shp in shapes:
        size = 1
        for s in shp:
            size *= s
        out.append(flat[..., off:off + size].reshape(flat.shape[:-1] + tuple(shp)))
        off += size
    return out


def kernel(x, mem, norm1_g, mem_norm_g, a_w_in, a_q_g, a_k_g, a_rel_bias, b_w_in, b_b_in, b_conv_w, b_conv_b, b_ln_g, b_ln_b, mq_g, mk_g, w_mem_kv, w_out, norm2_g, w_gate, w_up, w_down, loss_target, m_norm1_g, m_mem_norm_g, m_a_w_in, m_a_q_g, m_a_k_g, m_a_rel_bias, m_b_w_in, m_b_b_in, m_b_conv_w, m_b_conv_b, m_b_ln_g, m_b_ln_b, m_mq_g, m_mk_g, m_w_mem_kv, m_w_out, m_norm2_g, m_w_gate, m_w_up, m_w_down, v_norm1_g, v_mem_norm_g, v_a_w_in, v_a_q_g, v_a_k_g, v_a_rel_bias, v_b_w_in, v_b_b_in, v_b_conv_w, v_b_conv_b, v_b_ln_g, v_b_ln_b, v_mq_g, v_mk_g, v_w_mem_kv, v_w_out, v_norm2_g, v_w_gate, v_w_up, v_w_down):
    batch, seq, d = x.shape
    mtok = mem.shape[1]
    n = batch * seq
    ax, ay, ac = _place()
    me = 4 * ax + 2 * ay + ac
    core_arr = jnp.reshape(ac, (1,)).astype(jnp.int32)
    chip_arr = jnp.reshape(2 * ax + ay, (1,)).astype(jnp.int32)

    def t_bf16(w):
        return jnp.transpose(w).astype(BF16)

    def after(value, *earlier):
        return lax.optimization_barrier((value, *earlier))[0]

    def gather_mix(l, when, name, collective_id):
        srcs = [w_mem_kv[l].astype(BF16), w_out[l].astype(BF16)]
        if l == 1:
            srcs += [t_bf16(b_w_in[0]), _pack([b_b_in, b_conv_w, b_conv_b, b_ln_g, b_ln_b])]
        return gather_weights([after(srcs[0], *when)] + srcs[1:], name, collective_id)

    def gather_ffn(l, when, name, collective_id):
        return gather_weights(
            [after(t_bf16(w_gate[l]), *when), t_bf16(w_up[l]), w_down[l].astype(BF16)], name, collective_id)

    f_loc = b_b_in.shape[1]
    c_loc = b_conv_b.shape[1]

    def two(g):
        return jnp.concatenate([g, g], axis=-1)

    gq2, gk2 = two(a_q_g), two(a_k_g)
    rel16 = jnp.pad(a_rel_bias[0], ((0, 16 - a_rel_bias.shape[1]), (0, 0)))
    bias = bias_blocks(rel16)

    x0 = x.reshape(n, d)
    mem2 = mem.reshape(batch * mtok, d)

    saved = []
    xin = x0
    a_win_t, = gather_weights([t_bf16(a_w_in[0])], "gather_in_a", 1)
    wg_t, wu_t, wd, wo, wkv = [None] * 2, [None] * 2, [None] * 2, [None] * 2, [None] * 2
    h = after(rms_fwd(xin, norm1_g[0:1], name="rms1_fwd_0"), bias)
    target = loss_target.reshape(n, d)
    for l in range(2):
        gq4 = jnp.tile(mq_g[l:l + 1], (1, 4))
        gk4 = jnp.tile(mk_g[l:l + 1], (1, 4))
        y_conv = None
        if l == 0:
            wkv[0], wo[0] = gather_mix(0, (h, a_win_t), "gather_mix_a", 2)
            z = mm_nt(h, a_win_t, name="in_proj_a")
            wg_t[0], wu_t[0], wd[0] = gather_ffn(0, (z, wkv[0]), "gather_ffn_a", 3)
            cat = attn_fwd(z, gq2, gk2, bias, batch, seq)
            wkv[1], wo[1], b_win_t, conv_slabs = gather_mix(1, (cat, wg_t[0]), "gather_mix_b", 4)
            qcol = 3 * TOK_WIDTH // MEM_WIDTH
        else:
            small_shapes = [(f_loc,), (CONV_W, c_loc), (c_loc,), (c_loc,), (c_loc,)]
            bb_g, cw_g, cb_g, lg_g, lb_g = _unpack(conv_slabs.reshape(N_DEV, -1, LANES), small_shapes)
            bb_full = bb_g.reshape(1, -1)
            cw_full = jnp.pad(jnp.transpose(cw_g, (1, 0, 2)).reshape(CONV_W, -1), ((0, 32 - CONV_W), (0, 0)))
            cb_full, lg_full, lb_full = cb_g.reshape(1, -1), lg_g.reshape(1, -1), lb_g.reshape(1, -1)
            z = mm_nt(h, b_win_t, bias=bb_full, name="in_proj_b")
            cat, y_conv = conv_fwd(z, cw_full, cb_full, lg_full, lb_full, batch, seq)
            qcol = 2 * TOK_WIDTH // MEM_WIDTH
        mem_n, kv = mem_prep(mem2, mem_norm_g[l:l + 1], wkv[l], name=f"mem_prep_{l}")
        cat = memattn_fwd(z, kv, gq4, gk4, cat, batch, seq, qcol, name=f"memattn_fwd_{l}")
        x1, h2 = proj_norm(cat, wo[l], xin, norm2_g[l:l + 1], name=f"out_proj_{l}")
        if l == 0:
            wg_t[1], wu_t[1], wd[1] = gather_ffn(1, (x1, b_win_t), "gather_ffn_b", 5)
        if l == 0:
            gate, up, act, x2, h_next = ffn_fwd(h2, wg_t[0], wu_t[0], wd[0], x1, gain=norm1_g[1:2], name="ffn_fwd_0")
        else:
            gate, up, act, dx_b, loss_blk = ffn_fwd(h2, wg_t[1], wu_t[1], wd[1], x1, target=target, name="ffn_fwd_1")
        saved.append(dict(xin=xin, h=h, mem_n=mem_n, kv=kv, gq4=gq4, gk4=gk4, z=z, qcol=qcol, cat=cat, x1=x1, h2=h2,
                          gate=gate, up=up, act=act, y_conv=y_conv))
        if l == 0:
            xin, h = x2, h_next

    big = {}
    small = {}
    reduced = {}
    groups = 0

    def scatter_siblings(keys):
        nonlocal groups
        gid = groups
        groups += 1
        return gid, keys, scatter_to_sibling([big[k] for k in keys], f"scatter_sibling_{gid}", 8 + 2 * gid)

    def scatter_chips(stage1, when):
        gid, keys, landed1 = stage1
        parts = add_sibling([after(big[keys[0]], when)] + [big[k] for k in keys[1:]], landed1, core_arr,
                            name=f"add_sibling_{gid}")
        landed2 = scatter_to_chips(parts, f"scatter_chips_{gid}", 9 + 2 * gid)
        for k, p, ld in zip(keys, parts, landed2):
            reduced[k] = (p, ld)
        return parts, landed2

    def rows_of(w, transposed):
        w = jnp.swapaxes(w, 1, 2) if transposed else w
        return w.reshape(w.shape[0] * w.shape[1], w.shape[2])

    sharded = {
        "win0": (2, True), "win1": (6, True), "wkv": (14, False), "wo": (15, False),
        "wg": (17, True), "wu": (18, True), "wd": (19, False)}
    weights = [norm1_g, mem_norm_g, a_w_in, a_q_g, a_k_g, a_rel_bias, b_w_in, b_b_in, b_conv_w, b_conv_b, b_ln_g,
               b_ln_b, mq_g, mk_g, w_mem_kv, w_out, norm2_g, w_gate, w_up, w_down]
    moms = [m_norm1_g, m_mem_norm_g, m_a_w_in, m_a_q_g, m_a_k_g, m_a_rel_bias, m_b_w_in, m_b_b_in, m_b_conv_w,
            m_b_conv_b, m_b_ln_g, m_b_ln_b, m_mq_g, m_mk_g, m_w_mem_kv, m_w_out, m_norm2_g, m_w_gate, m_w_up, m_w_down]
    vels = [v_norm1_g, v_mem_norm_g, v_a_w_in, v_a_q_g, v_a_k_g, v_a_rel_bias, v_b_w_in, v_b_b_in, v_b_conv_w,
            v_b_conv_b, v_b_ln_g, v_b_ln_b, v_mq_g, v_mk_g, v_w_mem_kv, v_w_out, v_norm2_g, v_w_gate, v_w_up, v_w_down]
    updated = {}

    def update_layer(l, when):
        for group, keys in (("ffn", ("wg", "wu", "wd")), ("mix", (f"win{l}", "wkv", "wo"))):
            items = []
            for key in keys:
                idx, transposed = sharded[key]
                layer, rkey = (0, key) if key.startswith("win") else (l, f"{key}{l}")
                part, landed = reduced[rkey]
                w_rows = rows_of(weights[idx], transposed)
                items.append((layer, after(w_rows, when) if not items else w_rows, rows_of(moms[idx], transposed),
                              rows_of(vels[idx], transposed), part, landed, updated.get(key)))
            for key, result in zip(keys, adamw_shards(items, chip_arr, name=f"adamw_{group}_{l}")):
                updated[key] = result

    mix_landed = None
    for l in (1, 0):
        sv = saved[l]
        dgate, dup, dx1_b, dcat, small[f"norm2_{l}"] = ffn_bwd(
            dx_b, wd[l], sv["gate"], sv["up"], wg_t[l], wu_t[l], sv["x1"], norm2_g[l:l + 1], wo[l], name=f"ffn_bwd_{l}")
        if l == 0:
            dgate = after(dgate, *mix_landed)
            update_layer(1, dx1_b)
        big[f"wg{l}"], big[f"wu{l}"], big[f"wd{l}"] = ffn_weight_grads(
            dgate, dup, sv["h2"], sv["act"], dx_b, name=f"grad_ffn_{l}")
        stage1 = scatter_siblings([f"wd{l}", f"wg{l}", f"wu{l}"])
        big[f"wo{l}"] = mm_tn(sv["cat"], dx1_b, name=f"grad_wo_{l}")
        parts, ffn_landed = scatter_chips(stage1, big[f"wo{l}"])
        dcat = after(dcat, *parts)
        if l == 0:
            dz, dbias, small["a_q"], small["a_k"] = attn_bwd(sv["z"], dcat, gq2, gk2, bias, batch, seq)
            small["rel"] = bias_grad(dbias)
            win_t = a_win_t
        else:
            dz, small["cw"], small["csum"] = conv_bwd(sv["z"], sv["y_conv"], dcat, cw_full, lg_full, lb_full, batch, seq)
            win_t = b_win_t
        dz = after(dz, *ffn_landed)
        dz, dkv, small[f"mq_{l}"], small[f"mk_{l}"] = memattn_bwd(
            sv["z"], sv["kv"], dcat, sv["gq4"], sv["gk4"], dz, batch, seq, sv["qcol"], name=f"memattn_bwd_{l}")
        big[f"win{l}"] = mm_tn(dz, sv["h"], name=f"grad_win_{l}")
        big[f"wkv{l}"], small[f"memnorm_{l}"] = mem_grads(dkv, wkv[l], mem2, sv["mem_n"], name=f"mem_grads_{l}")
        stage1 = scatter_siblings([f"win{l}", f"wkv{l}", f"wo{l}"])
        dx_b, small[f"norm1_{l}"], dz_sum = in_proj_bwd(
            dz, win_t, sv["xin"], norm1_g[l:l + 1], dx1_b, BF16 if l == 1 else F32, name=f"in_proj_bwd_{l}")
        if l == 1:
            small["bb"] = dz_sum
        parts, mix_landed = scatter_chips(stage1, dx_b)
        dx_b = after(dx_b, *parts)
    grad_x = dx_b.reshape(batch, seq, d)
    update_layer(0, dx_b)

    def shaped(rows, idx, transposed):
        shp = weights[idx].shape
        if transposed:
            return jnp.swapaxes(rows.reshape(shp[0], shp[2], shp[1]), 1, 2)
        return rows.reshape(shp)

    def fold(v, groups):
        return jnp.sum(v.reshape(groups, HEAD_DIM), axis=0, keepdims=True)

    heads = a_rel_bias.shape[1]
    small_list = [
        jnp.concatenate([small["norm1_0"], small["norm1_1"]]),
        jnp.concatenate([small["memnorm_0"], small["memnorm_1"]]),
        fold(small["a_q"], 2), fold(small["a_k"], 2), small["rel"][:heads],
        small["bb"], small["cw"][:CONV_W], small["csum"][0:1], small["csum"][1:2], small["csum"][2:3],
        jnp.concatenate([fold(small["mq_0"], 4), fold(small["mq_1"], 4)]),
        jnp.concatenate([fold(small["mk_0"], 4), fold(small["mk_1"], 4)]),
        jnp.concatenate([small["norm2_0"], small["norm2_1"]]),
    ]
    (g_norm1, g_memnorm, g_aq, g_ak, g_rel, g_bb_full, g_cw_full, g_cb_full, g_lg_full, g_lb_full,
     g_mq, g_mk, g_norm2, loss_sum) = reduce_small(small_list + [loss_blk])
    loss = loss_sum[0, 0]
    g_bb = lax.dynamic_slice_in_dim(g_bb_full, me * f_loc, f_loc, axis=1)
    g_cw = lax.dynamic_slice_in_dim(g_cw_full, me * c_loc, c_loc, axis=1)
    g_cb = lax.dynamic_slice_in_dim(g_cb_full, me * c_loc, c_loc, axis=1)
    g_lg = lax.dynamic_slice_in_dim(g_lg_full, me * c_loc, c_loc, axis=1)
    g_lb = lax.dynamic_slice_in_dim(g_lb_full, me * c_loc, c_loc, axis=1)

    grads = [g_norm1, g_memnorm, None, g_aq, g_ak, g_rel, None, g_bb, g_cw, g_cb, g_lg, g_lb,
             g_mq, g_mk, None, None, g_norm2, None, None, None]
    deltas, new_m, new_v = [None] * 20, [None] * 20, [None] * 20
    for key, (idx, transposed) in sharded.items():
        grads[idx], deltas[idx], new_m[idx], new_v[idx] = (shaped(r, idx, transposed) for r in updated[key])

    def flat2(a):
        return a.reshape(a.shape[-2:])

    small_idx = [i for i in range(20) if i not in {idx for idx, _ in sharded.values()}]
    dl, nm, nv = adamw_small([flat2(weights[i]) for i in small_idx], [flat2(grads[i]) for i in small_idx],
                             [flat2(moms[i]) for i in small_idx], [flat2(vels[i]) for i in small_idx])
    for i, a, b, cc in zip(small_idx, dl, nm, nv):
        shp = weights[i].shape
        grads[i], deltas[i], new_m[i], new_v[i] = grads[i].reshape(shp), a.reshape(shp), b.reshape(shp), cc.reshape(shp)

    return (loss, grad_x, *grads, *deltas, *new_m, *new_v)
```

```python
import jax
import jax.numpy as jnp
from jax import lax
from jax.experimental import pallas as pl
from jax.experimental.pallas import tpu as pltpu
from jax.experimental.pallas import tpu_sc as plsc

F32 = jnp.float32
BF16 = jnp.bfloat16
HIGHEST = lax.Precision.HIGHEST
MESH = pl.DeviceIdType.MESH
ANY = pl.BlockSpec(memory_space=pl.ANY)

N_DEV = 8
D_MODEL = 1024
HEAD_DIM = 64
TOK_WIDTH = 768
MEM_WIDTH = 256
CHUNK = 64
Q_BLOCK = 256
KEY_WIN = 768
BAND = 576
N_REL = 192
CONV_W = 31
CONV_HALO = 32
NORM_EPS = 1e-6
NEG_INF = -1e30
ATTN_SCALE = HEAD_DIM ** -0.5
LANES = 128
ROW_TILE = 512
VMEM_LIMIT = 56 * 1024 * 1024

ADAM_LR, ADAM_B1, ADAM_B2, ADAM_EPS, ADAM_WD, ADAM_STEP = 0.001, 0.9, 0.999, 1e-08, 0.01, 10


def _params(*sem):
    return pltpu.CompilerParams(dimension_semantics=sem, vmem_limit_bytes=VMEM_LIMIT)


WIDE_ROW_TILE = 1024


def _row_tile(m, rows=ROW_TILE):
    return rows if m % rows == 0 else m


def _col_tile(n, cap=1408):
    best = None
    for t in range(LANES, min(n, cap) + 1, LANES):
        if n % t == 0:
            best = t
    return best if best is not None else n


def _dot(a, b, ca, cb):
    return lax.dot_general(a, b, (((ca,), (cb,)), ((), ())), preferred_element_type=F32)


def _sigmoid(x):
    return 0.5 * jnp.tanh(0.5 * x) + 0.5


def mm_nt(a, b, bias=None, out_dtype=BF16, name="mm_nt"):
    m, k = a.shape
    n = b.shape[0]
    tm, tn = _row_tile(m, WIDE_ROW_TILE), _col_tile(n)

    def body(*refs):
        a_ref, b_ref = refs[0], refs[1]
        o_ref = refs[-1]
        acc = _dot(a_ref[...].astype(BF16), b_ref[...].astype(BF16), 1, 1)
        if bias is not None:
            acc = acc + refs[2][...]
        o_ref[...] = acc.astype(o_ref.dtype)

    in_specs = [pl.BlockSpec((tm, k), lambda j, i: (i, 0)), pl.BlockSpec((tn, k), lambda j, i: (j, 0))]
    args = [a, b]
    if bias is not None:
        in_specs.append(pl.BlockSpec((1, tn), lambda j, i: (0, j)))
        args.append(bias)
    return pl.pallas_call(
        body, out_shape=jax.ShapeDtypeStruct((m, n), out_dtype), grid=(n // tn, m // tm),
        in_specs=in_specs, out_specs=pl.BlockSpec((tm, tn), lambda j, i: (i, j)),
        compiler_params=_params("parallel", "arbitrary"), name=name)(*args)


def mm_tn(a, b, out_dtype=BF16, name="mm_tn"):
    t, r = a.shape
    c = b.shape[1]
    tr = _col_tile(r, 512)

    def body(a_ref, b_ref, o_ref):
        o_ref[...] = _dot(a_ref[...].astype(BF16), b_ref[...].astype(BF16), 0, 0).astype(o_ref.dtype)

    return pl.pallas_call(
        body, out_shape=jax.ShapeDtypeStruct((r, c), out_dtype), grid=(r // tr,),
        in_specs=[pl.BlockSpec((t, tr), lambda i: (0, i)), pl.BlockSpec((t, c), lambda i: (0, 0))],
        out_specs=pl.BlockSpec((tr, c), lambda i: (i, 0)),
        compiler_params=_params("parallel"), name=name)(a, b)


def _resident(shape):
    return pl.BlockSpec(shape, lambda i: (0, 0), pipeline_mode=pl.Buffered(1))


def proj_norm(a, b, res, gain, name):
    m, k = a.shape
    n = b.shape[1]
    tm = _row_tile(m)

    def body(a_ref, b_ref, res_ref, g_ref, x_ref, h_ref):
        xv = res_ref[...] + _dot(a_ref[...], b_ref[...], 1, 0)
        x_ref[...] = xv
        r = lax.rsqrt(jnp.mean(xv * xv, axis=-1, keepdims=True) + NORM_EPS)
        h_ref[...] = (xv * r * g_ref[...]).astype(BF16)

    row = pl.BlockSpec((tm, n), lambda i: (i, 0))
    return pl.pallas_call(
        body, out_shape=(jax.ShapeDtypeStruct((m, n), F32), jax.ShapeDtypeStruct((m, n), BF16)), grid=(m // tm,),
        in_specs=[pl.BlockSpec((tm, k), lambda i: (i, 0)), _resident((k, n)), row, _resident((1, n))],
        out_specs=(row, row), compiler_params=_params("parallel"), name=name)(a, b, res, gain)


def in_proj_bwd(dz, w_t, x, gain, dres, out_dtype, name):
    m, n = x.shape
    k = dz.shape[1]
    tm = _row_tile(m)

    def body(dz_ref, w_ref, x_ref, g_ref, dres_ref, dx_ref, dg_ref, cs_ref):
        @pl.when(pl.program_id(0) == 0)
        def _():
            dg_ref[...] = jnp.zeros_like(dg_ref)
            cs_ref[...] = jnp.zeros_like(cs_ref)

        dzv = dz_ref[...]
        cs_ref[...] += jnp.sum(dzv.astype(F32), axis=0, keepdims=True)
        dhv = _dot(dzv, w_ref[...], 1, 0)
        xv = x_ref[...]
        r = lax.rsqrt(jnp.mean(xv * xv, axis=-1, keepdims=True) + NORM_EPS)
        xhat = xv * r
        dg_ref[...] += jnp.sum(dhv * xhat, axis=0, keepdims=True)
        dxhat = dhv * g_ref[...]
        dx = dres_ref[...].astype(F32) + r * (dxhat - xhat * jnp.mean(dxhat * xhat, axis=-1, keepdims=True))
        dx_ref[...] = dx.astype(dx_ref.dtype)

    row = pl.BlockSpec((tm, n), lambda i: (i, 0))
    return pl.pallas_call(
        body, out_shape=(jax.ShapeDtypeStruct((m, n), out_dtype), jax.ShapeDtypeStruct((1, n), F32),
                         jax.ShapeDtypeStruct((1, k), F32)), grid=(m // tm,),
        in_specs=[pl.BlockSpec((tm, k), lambda i: (i, 0)), _resident(w_t.shape), row, _resident((1, n)), row],
        out_specs=(row, pl.BlockSpec((1, n), lambda i: (0, 0)), pl.BlockSpec((1, k), lambda i: (0, 0))),
        compiler_params=_params("arbitrary"), name=name)(dz, w_t, x, gain, dres)


FFN_ROWS = 256


def _ffn_row_tile(m):
    return FFN_ROWS if m % FFN_ROWS == 0 else m


def ffn_fwd(h2, wg_t, wu_t, wd, x1, gain=None, target=None, name="ffn_fwd"):
    n, d = h2.shape
    f = wg_t.shape[0]
    tm = _ffn_row_tile(n)
    nt = n // tm
    last = target is not None

    def body(h_ref, wg_ref, wu_ref, wd_ref, x1_ref, e_ref, g_ref, u_ref, a_ref, *rest):
        hv = h_ref[...]
        gv = _dot(hv, wg_ref[...], 1, 1)
        uv = _dot(hv, wu_ref[...], 1, 1)
        g_ref[...] = gv.astype(BF16)
        u_ref[...] = uv.astype(BF16)
        av = (gv * _sigmoid(gv) * uv).astype(BF16)
        a_ref[...] = av
        xv = x1_ref[...] + _dot(av, wd_ref[...], 1, 0)
        if not last:
            x_ref, hn_ref = rest
            x_ref[...] = xv
            r = lax.rsqrt(jnp.mean(xv * xv, axis=-1, keepdims=True) + NORM_EPS)
            hn_ref[...] = (xv * r * e_ref[...]).astype(BF16)
        else:
            dyb_ref, l_ref, acc_ref = rest
            i = pl.program_id(0)

            @pl.when(i == 0)
            def _():
                acc_ref[...] = jnp.zeros_like(acc_ref)

            err = xv - e_ref[...]
            dyb_ref[...] = (err * (1.0 / d)).astype(BF16)
            acc_ref[...] += jnp.sum(err * err, axis=0, keepdims=True)

            @pl.when(i == nt - 1)
            def _():
                total = jnp.sum(acc_ref[...], axis=-1, keepdims=True) * (0.5 / d)
                l_ref[...] = jnp.broadcast_to(total, l_ref.shape)

    row_d = pl.BlockSpec((tm, d), lambda i: (i, 0))
    row_f = pl.BlockSpec((tm, f), lambda i: (i, 0))
    act_shape = jax.ShapeDtypeStruct((n, f), BF16)
    if not last:
        extra_in, extra = _resident((1, d)), gain
        out_shape = (act_shape, act_shape, act_shape, jax.ShapeDtypeStruct((n, d), F32), jax.ShapeDtypeStruct((n, d), BF16))
        out_specs = (row_f, row_f, row_f, row_d, row_d)
        scratch = []
    else:
        extra_in, extra = row_d, target
        out_shape = (act_shape, act_shape, act_shape, jax.ShapeDtypeStruct((n, d), BF16),
                     jax.ShapeDtypeStruct((8, LANES), F32))
        out_specs = (row_f, row_f, row_f, row_d, pl.BlockSpec((8, LANES), lambda i: (0, 0)))
        scratch = [pltpu.VMEM((1, d), F32)]
    return pl.pallas_call(
        body, out_shape=out_shape, grid=(nt,),
        in_specs=[row_d, _resident((f, d)), _resident((f, d)), _resident((f, d)), row_d, extra_in],
        out_specs=out_specs, scratch_shapes=scratch,
        compiler_params=_params("arbitrary"), name=name)(h2, wg_t, wu_t, wd, x1, extra)


def ffn_bwd(dx_b, wd, gate, up, wg_t, wu_t, x1, gain, wo, name="ffn_bwd"):
    n, d = x1.shape
    f = wd.shape[0]
    tm = _ffn_row_tile(n)

    def body(dxb_ref, wd_ref, g_ref, u_ref, wg_ref, wu_ref, x_ref, gain_ref, wo_ref,
             dg_ref, du_ref, dxo_ref, dc_ref, dgain_ref):
        @pl.when(pl.program_id(0) == 0)
        def _():
            dgain_ref[...] = jnp.zeros_like(dgain_ref)

        dact = _dot(dxb_ref[...], wd_ref[...], 1, 1)
        gv = g_ref[...].astype(F32)
        uv = u_ref[...].astype(F32)
        sg = _sigmoid(gv)
        dgv = (dact * uv * sg * (1.0 + gv * (1.0 - sg))).astype(BF16)
        duv = (dact * gv * sg).astype(BF16)
        dg_ref[...] = dgv
        du_ref[...] = duv
        dhv = _dot(dgv, wg_ref[...], 1, 0) + _dot(duv, wu_ref[...], 1, 0)
        xv = x_ref[...]
        r = lax.rsqrt(jnp.mean(xv * xv, axis=-1, keepdims=True) + NORM_EPS)
        xhat = xv * r
        dgain_ref[...] += jnp.sum(dhv * xhat, axis=0, keepdims=True)
        dxhat = dhv * gain_ref[...]
        dxb = (dxb_ref[...].astype(F32) + r * (dxhat - xhat * jnp.mean(dxhat * xhat, axis=-1, keepdims=True))).astype(BF16)
        dxo_ref[...] = dxb
        dc_ref[...] = _dot(dxb, wo_ref[...], 1, 1).astype(BF16)

    row_d = pl.BlockSpec((tm, d), lambda i: (i, 0))
    row_f = pl.BlockSpec((tm, f), lambda i: (i, 0))
    w_spec = _resident((f, d))
    act_shape = jax.ShapeDtypeStruct((n, f), BF16)
    row_shape = jax.ShapeDtypeStruct((n, d), BF16)
    return pl.pallas_call(
        body, out_shape=(act_shape, act_shape, row_shape, jax.ShapeDtypeStruct((n, wo.shape[0]), BF16),
                         jax.ShapeDtypeStruct((1, d), F32)),
        grid=(n // tm,),
        in_specs=[row_d, w_spec, row_f, row_f, w_spec, w_spec, row_d, _resident((1, d)), _resident(wo.shape)],
        out_specs=(row_f, row_f, row_d, pl.BlockSpec((tm, wo.shape[0]), lambda i: (i, 0)),
                   pl.BlockSpec((1, d), lambda i: (0, 0))),
        compiler_params=_params("arbitrary"), name=name)(dx_b, wd, gate, up, wg_t, wu_t, x1, gain, wo)


def ffn_weight_grads(dgate, dup, h2, act, dx_b, name="ffn_weight_grads"):
    t, r = dgate.shape
    c = h2.shape[1]
    tr = _col_tile(r, 512)

    def body(a1_ref, a2_ref, a3_ref, b12_ref, b3_ref, o1_ref, o2_ref, o3_ref):
        bv = b12_ref[...]
        o1_ref[...] = _dot(a1_ref[...], bv, 0, 0).astype(o1_ref.dtype)
        o2_ref[...] = _dot(a2_ref[...], bv, 0, 0).astype(o2_ref.dtype)
        o3_ref[...] = _dot(a3_ref[...], b3_ref[...], 0, 0).astype(o3_ref.dtype)

    a_spec = pl.BlockSpec((t, tr), lambda i: (0, i))
    o_spec = pl.BlockSpec((tr, c), lambda i: (i, 0))
    shape = jax.ShapeDtypeStruct((r, c), BF16)
    return pl.pallas_call(
        body, out_shape=(shape, shape, shape), grid=(r // tr,),
        in_specs=[a_spec, a_spec, a_spec, _resident((t, c)), _resident((t, c))],
        out_specs=(o_spec, o_spec, o_spec), compiler_params=_params("parallel"), name=name)(dgate, dup, act, h2, dx_b)


def rms_fwd(x, g, name="rms_fwd"):
    n, d = x.shape
    tm = _row_tile(n)

    def body(x_ref, g_ref, o_ref):
        xv = x_ref[...]
        r = lax.rsqrt(jnp.mean(xv * xv, axis=-1, keepdims=True) + NORM_EPS)
        o_ref[...] = (xv * r * g_ref[...]).astype(o_ref.dtype)

    return pl.pallas_call(
        body, out_shape=jax.ShapeDtypeStruct((n, d), BF16), grid=(n // tm,),
        in_specs=[pl.BlockSpec((tm, d), lambda i: (i, 0)), pl.BlockSpec((1, d), lambda i: (0, 0))],
        out_specs=pl.BlockSpec((tm, d), lambda i: (i, 0)),
        compiler_params=_params("parallel"), name=name)(x, g)


def _group_masks(width):
    lane = lax.broadcasted_iota(jnp.int32, (1, width), 1)
    return [(lane >= HEAD_DIM * g) & (lane < HEAD_DIM * (g + 1)) for g in range(width // HEAD_DIM)]


def _group_sum(x, masks):
    out = jnp.zeros_like(x)
    for msk in masks:
        s = jnp.sum(jnp.where(msk, x, 0.0), axis=-1, keepdims=True)
        out = jnp.where(msk, s, out)
    return out


def _head_norm(x, gain, masks):
    r = lax.rsqrt(_group_sum(x * x, masks) * (1.0 / HEAD_DIM) + NORM_EPS)
    xhat = x * r
    return xhat * gain, xhat, r


def _head_norm_bwd(dxn, xhat, r, gain, masks):
    dgain = jnp.sum(dxn * xhat, axis=0, keepdims=True)
    dxhat = dxn * gain
    mean_t = _group_sum(dxhat * xhat, masks) * (1.0 / HEAD_DIM)
    return r * (dxhat - xhat * mean_t), dgain


def _softmax_rows(s):
    e = jnp.exp(s - jnp.max(s, axis=-1, keepdims=True))
    return e * (1.0 / jnp.sum(e, axis=-1, keepdims=True))


def _rel_onehot():
    col = lax.broadcasted_iota(jnp.int32, (1, KEY_WIN), 1)
    off = jnp.where(col < KEY_WIN - LANES, col, col - KEY_WIN)
    idx = jnp.clip(8 * CHUNK - off, -(CHUNK - 1), LANES) + (CHUNK - 1)
    return (lax.broadcasted_iota(jnp.int32, (N_REL, KEY_WIN), 0) == idx).astype(F32)


def bias_blocks(rel16):
    heads = TOK_WIDTH // HEAD_DIM

    def body(rel_ref, o_ref, u_ref):
        u_ref[...] = jnp.dot(rel_ref[...], _rel_onehot(), precision=HIGHEST, preferred_element_type=F32)
        row = lax.broadcasted_iota(jnp.int32, (CHUNK, KEY_WIN), 0)
        col = lax.broadcasted_iota(jnp.int32, (CHUNK, KEY_WIN), 1)
        for h in range(heads):
            xv = jnp.broadcast_to(u_ref[h:h + 1, :], (CHUNK, KEY_WIN))
            for b in range(6):
                xv = jnp.where(((row >> b) & 1) == 1, pltpu.roll(xv, 1 << b, axis=1), xv)
            xv = jnp.where(col < BAND, xv, NEG_INF)
            for i in range(Q_BLOCK // CHUNK):
                o_ref[h, CHUNK * i:CHUNK * (i + 1), :] = pltpu.roll(xv, CHUNK * i, axis=1) if i else xv

    return pl.pallas_call(
        body, out_shape=jax.ShapeDtypeStruct((heads, Q_BLOCK, KEY_WIN), F32),
        scratch_shapes=[pltpu.VMEM((16, KEY_WIN), F32)], name="bias_blocks")(rel16)


def bias_grad(dbias):
    heads = dbias.shape[0]

    def body(db_ref, o_ref, y_ref):
        y_ref[...] = jnp.zeros_like(y_ref)
        row = lax.broadcasted_iota(jnp.int32, (CHUNK, KEY_WIN), 0)
        for h in range(heads):
            fv = db_ref[h, 0:CHUNK, :]
            for i in range(1, Q_BLOCK // CHUNK):
                fv = fv + pltpu.roll(db_ref[h, CHUNK * i:CHUNK * (i + 1), :], KEY_WIN - CHUNK * i, axis=1)
            for b in range(6):
                fv = jnp.where(((row >> b) & 1) == 1, pltpu.roll(fv, KEY_WIN - (1 << b), axis=1), fv)
            y_ref[h:h + 1, :] = jnp.sum(fv, axis=0, keepdims=True)
        o_ref[...] = lax.dot_general(y_ref[...], _rel_onehot(), (((1,), (1,)), ((), ())),
                                     precision=HIGHEST, preferred_element_type=F32)

    return pl.pallas_call(
        body, out_shape=jax.ShapeDtypeStruct((16, N_REL), F32),
        scratch_shapes=[pltpu.VMEM((16, KEY_WIN), F32)], name="bias_grad")(dbias)


def _attn_windows(seq):
    out = []
    for j in range(seq // Q_BLOCK):
        r0 = j * Q_BLOCK
        k0 = max(0, r0 - 8 * CHUNK)
        width = r0 + Q_BLOCK - k0
        out.append((r0, k0, width, KEY_WIN - width))
    return out


def attn_fwd(z, gq2, gk2, bias, batch, seq):
    n = z.shape[0]
    pairs = TOK_WIDTH // LANES

    def body(q_ref, k_ref, v_ref, gq_ref, gk_ref, b_ref, o_ref, qs_s, kn_s):
        masks = _group_masks(LANES)
        qs_s[...] = (_head_norm(q_ref[...].astype(F32), gq_ref[...], masks)[0] * ATTN_SCALE).astype(BF16)
        kn_s[...] = _head_norm(k_ref[...].astype(F32), gk_ref[...], masks)[0].astype(BF16)
        for r0, k0, width, c0 in _attn_windows(seq):
            qb = qs_s[r0:r0 + Q_BLOCK, :]
            kw = kn_s[k0:k0 + width, :]
            vw = v_ref[k0:k0 + width, :]
            out = jnp.zeros((Q_BLOCK, LANES), F32)
            for h, msk in enumerate(masks):
                qh = jnp.where(msk, qb, jnp.zeros_like(qb))
                s = _dot(qh, kw, 1, 1) + b_ref[h, :, c0:KEY_WIN]
                p = _softmax_rows(s).astype(BF16)
                out = jnp.where(msk, _dot(p, vw, 1, 0), out)
            o_ref[r0:r0 + Q_BLOCK, :] = out.astype(o_ref.dtype)

    def col(off):
        return pl.BlockSpec((seq, LANES), lambda b, p: (b, off + p))

    vec = pl.BlockSpec((1, LANES), lambda b, p: (0, 0))
    return pl.pallas_call(
        body, out_shape=jax.ShapeDtypeStruct((n, D_MODEL), BF16), grid=(batch, pairs),
        in_specs=[col(0), col(pairs), col(2 * pairs), vec, vec,
                  pl.BlockSpec((2, Q_BLOCK, KEY_WIN), lambda b, p: (p, 0, 0))],
        out_specs=pl.BlockSpec((seq, LANES), lambda b, p: (b, p)),
        scratch_shapes=[pltpu.VMEM((seq, LANES), BF16), pltpu.VMEM((seq, LANES), BF16)],
        compiler_params=_params("parallel", "arbitrary"), name="attn_fwd")(z, z, z, gq2, gk2, bias)


def attn_bwd(z, dcat, gq2, gk2, bias, batch, seq):
    n = z.shape[0]
    pairs = TOK_WIDTH // LANES

    def body(q_ref, k_ref, v_ref, do_ref, gq_ref, gk_ref, b_ref,
             dz_ref, db_ref, dgq_ref, dgk_ref, qs_s, kn_s, dqn_s, dkn_s, dv_s, dk_o, dv_o):
        pi, bi, which = pl.program_id(0), pl.program_id(1), pl.program_id(2)

        @pl.when(which == 0)
        def _():
            masks = _group_masks(LANES)

            @pl.when(bi == 0)
            def _():
                db_ref[...] = jnp.zeros_like(db_ref)

            @pl.when((bi == 0) & (pi == 0))
            def _():
                dgq_ref[...] = jnp.zeros_like(dgq_ref)
                dgk_ref[...] = jnp.zeros_like(dgk_ref)

            qn, qhat, rq = _head_norm(q_ref[...].astype(F32), gq_ref[...], masks)
            kn, khat, rk = _head_norm(k_ref[...].astype(F32), gk_ref[...], masks)
            qs_s[...] = (qn * ATTN_SCALE).astype(BF16)
            kn_s[...] = kn.astype(BF16)
            dkn_s[...] = jnp.zeros_like(dkn_s)
            dv_s[...] = jnp.zeros_like(dv_s)
            for r0, k0, width, c0 in _attn_windows(seq):
                qb = qs_s[r0:r0 + Q_BLOCK, :]
                dob = do_ref[r0:r0 + Q_BLOCK, :]
                kw = kn_s[k0:k0 + width, :]
                vw = v_ref[k0:k0 + width, :]
                dq_acc = jnp.zeros((Q_BLOCK, LANES), F32)
                dk_acc = jnp.zeros((width, LANES), F32)
                dv_acc = jnp.zeros((width, LANES), F32)
                for h, msk in enumerate(masks):
                    qh = jnp.where(msk, qb, jnp.zeros_like(qb))
                    doh = jnp.where(msk, dob, jnp.zeros_like(dob))
                    p = _softmax_rows(_dot(qh, kw, 1, 1) + b_ref[h, :, c0:KEY_WIN])
                    dp = _dot(doh, vw, 1, 1)
                    ds = p * (dp - jnp.sum(p * dp, axis=-1, keepdims=True))
                    db_ref[h, :, c0:KEY_WIN] += ds
                    dsb = ds.astype(BF16)
                    dq_acc = jnp.where(msk, _dot(dsb, kw, 1, 0), dq_acc)
                    dk_acc = jnp.where(msk, _dot(dsb, qb, 0, 0), dk_acc)
                    dv_acc = jnp.where(msk, _dot(p.astype(BF16), dob, 0, 0), dv_acc)
                dqn_s[r0:r0 + Q_BLOCK, :] = dq_acc * ATTN_SCALE
                dkn_s[k0:k0 + width, :] += dk_acc
                dv_s[k0:k0 + width, :] += dv_acc
            dq, dgq = _head_norm_bwd(dqn_s[...], qhat, rq, gq_ref[...], masks)
            dk, dgk = _head_norm_bwd(dkn_s[...], khat, rk, gk_ref[...], masks)
            dz_ref[...] = dq.astype(dz_ref.dtype)
            dk_o[...] = dk.astype(dk_o.dtype)
            dv_o[...] = dv_s[...].astype(dv_o.dtype)
            dgq_ref[...] += dgq
            dgk_ref[...] += dgk

        @pl.when(which == 1)
        def _():
            dz_ref[...] = dk_o[...]

        @pl.when(which == 2)
        def _():
            dz_ref[...] = dv_o[...]

    def ahead(p, b, t):
        nb = b + jnp.where(t > 0, 1, 0)
        wrap = jnp.where(nb >= batch, 1, 0)
        return jnp.minimum(p + wrap, pairs - 1), nb - wrap * batch

    def col(off):
        def index(p, b, t):
            np_, nb = ahead(p, b, t)
            return nb, off + np_
        return pl.BlockSpec((seq, LANES), index)

    vec = pl.BlockSpec((1, LANES), lambda p, b, t: (0, 0))
    blk = pl.BlockSpec((2, Q_BLOCK, KEY_WIN), lambda p, b, t: (p, 0, 0))
    blk_in = pl.BlockSpec((2, Q_BLOCK, KEY_WIN), lambda p, b, t: (ahead(p, b, t)[0], 0, 0))
    v_shape = jax.ShapeDtypeStruct((1, LANES), F32)
    return pl.pallas_call(
        body,
        out_shape=(jax.ShapeDtypeStruct(z.shape, BF16), jax.ShapeDtypeStruct(bias.shape, F32), v_shape, v_shape),
        grid=(pairs, batch, 3),
        in_specs=[col(0), col(pairs), col(2 * pairs), col(0), vec, vec, blk_in],
        out_specs=(pl.BlockSpec((seq, LANES), lambda p, b, t: (b, t * pairs + p)), blk, vec, vec),
        scratch_shapes=[pltpu.VMEM((seq, LANES), BF16), pltpu.VMEM((seq, LANES), BF16),
                        pltpu.VMEM((seq, LANES), F32), pltpu.VMEM((seq, LANES), F32), pltpu.VMEM((seq, LANES), F32),
                        pltpu.VMEM((seq, LANES), BF16), pltpu.VMEM((seq, LANES), BF16)],
        compiler_params=_params("arbitrary", "arbitrary", "arbitrary"), name="attn_bwd")(
            z, z, z, dcat, gq2, gk2, bias)


MEM_ROWS = 512


def memattn_fwd(z, mem, mem_gain, wkv, gq4, gk4, cat, batch, seq, qcol, name):
    mtok = mem.shape[0] // batch
    d = mem.shape[1]
    rows = min(MEM_ROWS, seq)

    def body(q_ref, m_ref, mg_ref, w_ref, gq_ref, gk_ref, cat_ref, o_ref, n_ref, kv_ref):
        del cat_ref
        masks = _group_masks(MEM_WIDTH)
        mv = m_ref[...]
        r = lax.rsqrt(jnp.mean(mv * mv, axis=-1, keepdims=True) + NORM_EPS)
        nv = (mv * r * mg_ref[...]).astype(BF16)
        n_ref[...] = nv
        kv_ref[...] = _dot(nv, w_ref[...], 1, 0)
        kn = _head_norm(kv_ref[:, 0:MEM_WIDTH], gk_ref[...], masks)[0].astype(BF16)
        vm = kv_ref[:, MEM_WIDTH:2 * MEM_WIDTH].astype(BF16)
        for t in range(seq // rows):
            sl = slice(t * rows, (t + 1) * rows)
            qs = (_head_norm(q_ref[sl, :].astype(F32), gq_ref[...], masks)[0] * ATTN_SCALE).astype(BF16)
            out = jnp.zeros((rows, MEM_WIDTH), F32)
            for msk in masks:
                qh = jnp.where(msk, qs, jnp.zeros_like(qs))
                p = _softmax_rows(_dot(qh, kn, 1, 1)).astype(BF16)
                out = jnp.where(msk, _dot(p, vm, 1, 0), out)
            o_ref[sl, :] = out.astype(o_ref.dtype)

    vec = pl.BlockSpec((1, MEM_WIDTH), lambda b: (0, 0))
    mem_spec = pl.BlockSpec((mtok, d), lambda b: (b, 0))
    kv_spec = pl.BlockSpec((mtok, 2 * MEM_WIDTH), lambda b: (b, 0))
    return pl.pallas_call(
        body, out_shape=(jax.ShapeDtypeStruct(cat.shape, cat.dtype), jax.ShapeDtypeStruct(mem.shape, BF16),
                         jax.ShapeDtypeStruct((mem.shape[0], 2 * MEM_WIDTH), F32)), grid=(batch,),
        in_specs=[pl.BlockSpec((seq, MEM_WIDTH), lambda b: (b, qcol)), mem_spec, pl.BlockSpec((1, d), lambda b: (0, 0)),
                  pl.BlockSpec(wkv.shape, lambda b: (0, 0)), vec, vec, ANY],
        out_specs=(pl.BlockSpec((seq, MEM_WIDTH), lambda b: (b, TOK_WIDTH // MEM_WIDTH)), mem_spec, kv_spec),
        input_output_aliases={6: 0},
        compiler_params=_params("parallel"), name=name)(z, mem, mem_gain, wkv, gq4, gk4, cat)


def memattn_bwd(z, kv, dcat, gq4, gk4, dz, batch, seq, qcol, name):
    mtok = kv.shape[0] // batch
    rows = min(MEM_ROWS, seq)

    def body(q_ref, kv_ref, do_ref, gq_ref, gk_ref, dz_in_ref, dq_ref, dkv_ref, dgq_ref, dgk_ref):
        del dz_in_ref
        @pl.when(pl.program_id(0) == 0)
        def _():
            dgq_ref[...] = jnp.zeros_like(dgq_ref)
            dgk_ref[...] = jnp.zeros_like(dgk_ref)

        masks = _group_masks(MEM_WIDTH)
        kn_f, khat, rk = _head_norm(kv_ref[:, 0:MEM_WIDTH], gk_ref[...], masks)
        kn = kn_f.astype(BF16)
        vm = kv_ref[:, MEM_WIDTH:2 * MEM_WIDTH].astype(BF16)
        dkn = jnp.zeros((mtok, MEM_WIDTH), F32)
        dvm = jnp.zeros((mtok, MEM_WIDTH), F32)
        dgq = jnp.zeros((1, MEM_WIDTH), F32)
        for t in range(seq // rows):
            sl = slice(t * rows, (t + 1) * rows)
            qn_f, qhat, rq = _head_norm(q_ref[sl, :].astype(F32), gq_ref[...], masks)
            qs = (qn_f * ATTN_SCALE).astype(BF16)
            dob = do_ref[sl, :]
            dqn = jnp.zeros((rows, MEM_WIDTH), F32)
            for msk in masks:
                qh = jnp.where(msk, qs, jnp.zeros_like(qs))
                doh = jnp.where(msk, dob, jnp.zeros_like(dob))
                p = _softmax_rows(_dot(qh, kn, 1, 1))
                dp = _dot(doh, vm, 1, 1)
                ds = p * (dp - jnp.sum(p * dp, axis=-1, keepdims=True))
                dsb = ds.astype(BF16)
                dqn = jnp.where(msk, _dot(dsb, kn, 1, 0), dqn)
                dkn = dkn + jnp.where(msk, _dot(dsb, qs, 0, 0), 0.0)
                dvm = dvm + jnp.where(msk, _dot(p.astype(BF16), dob, 0, 0), 0.0)
            dq, dg = _head_norm_bwd(dqn * ATTN_SCALE, qhat, rq, gq_ref[...], masks)
            dq_ref[sl, :] = dq.astype(dq_ref.dtype)
            dgq = dgq + dg
        dk, dgk = _head_norm_bwd(dkn, khat, rk, gk_ref[...], masks)
        dkv_ref[:, 0:MEM_WIDTH] = dk
        dkv_ref[:, MEM_WIDTH:2 * MEM_WIDTH] = dvm
        dgq_ref[...] += dgq
        dgk_ref[...] += dgk

    vec = pl.BlockSpec((1, MEM_WIDTH), lambda b: (0, 0))
    kv_spec = pl.BlockSpec((mtok, 2 * MEM_WIDTH), lambda b: (b, 0))
    v_shape = jax.ShapeDtypeStruct((1, MEM_WIDTH), F32)
    q_spec = pl.BlockSpec((seq, MEM_WIDTH), lambda b: (b, qcol))
    return pl.pallas_call(
        body,
        out_shape=(jax.ShapeDtypeStruct(dz.shape, dz.dtype), jax.ShapeDtypeStruct(kv.shape, F32), v_shape, v_shape),
        grid=(batch,),
        in_specs=[q_spec, kv_spec, pl.BlockSpec((seq, MEM_WIDTH), lambda b: (b, TOK_WIDTH // MEM_WIDTH)), vec, vec, ANY],
        out_specs=(q_spec, kv_spec, vec, vec),
        input_output_aliases={5: 0},
        compiler_params=_params("arbitrary"), name=name)(z, kv, dcat, gq4, gk4, dz)


def mem_grads(dkv, wkv, mem, mem_n, name):
    t, d = mem.shape

    def body(dkv_ref, w_ref, m_ref, n_ref, dw_ref, dg_ref):
        dkv_b = dkv_ref[...].astype(BF16)
        dw_ref[...] = _dot(n_ref[...], dkv_b, 0, 0).astype(dw_ref.dtype)
        dn = _dot(dkv_b, w_ref[...], 1, 1)
        mv = m_ref[...]
        r = lax.rsqrt(jnp.mean(mv * mv, axis=-1, keepdims=True) + NORM_EPS)
        dg_ref[...] = jnp.sum(dn * (mv * r), axis=0, keepdims=True)

    vmem = pl.BlockSpec(memory_space=pltpu.VMEM)
    return pl.pallas_call(
        body, out_shape=(jax.ShapeDtypeStruct(wkv.shape, BF16), jax.ShapeDtypeStruct((1, d), F32)),
        in_specs=[vmem, vmem, vmem, vmem], out_specs=(vmem, vmem),
        compiler_params=pltpu.CompilerParams(vmem_limit_bytes=VMEM_LIMIT), name=name)(dkv, wkv, mem, mem_n)


CONV_ROWS = 256


def _glu(a_ref, g_ref):
    return a_ref[...].astype(F32) * _sigmoid(g_ref[...].astype(F32))


def _layer_norm_stats(y):
    mu = jnp.mean(y, axis=-1, keepdims=True)
    yc = y - mu
    rstd = lax.rsqrt(jnp.mean(yc * yc, axis=-1, keepdims=True) + NORM_EPS)
    return yc * rstd, rstd


CONV_WIN = CONV_HALO + CONV_ROWS
SUBLANES = 8
SHIFT_ROWS = CONV_WIN - SUBLANES


def _preshift(win, shifted):
    for s in range(1, SUBLANES):
        shifted[s - 1, :, :] = win[s:s + SHIFT_ROWS, :]


TAP_ROWS = 64
TAP_TILES = [(r0, slice(c0, c0 + LANES)) for c0 in range(0, TOK_WIDTH, LANES) for r0 in range(0, CONV_ROWS, TAP_ROWS)]


def _tap(win, shifted, off, r0, lanes):
    s = off % SUBLANES
    base = off - s + r0
    if s == 0:
        return win[base:base + TAP_ROWS, lanes]
    return shifted[s - 1, base:base + TAP_ROWS, lanes]


def _fold_rows(x):
    return jnp.sum(x.reshape(TAP_ROWS // SUBLANES, SUBLANES, LANES), axis=0)


def conv_fwd(z, cw, cb, lg, lb, batch, seq):
    n = z.shape[0]
    nt = seq // CONV_ROWS
    sub = CONV_ROWS // CONV_HALO
    lead = CONV_HALO - (CONV_W - 1)

    def body(a_ref, g_ref, ap_ref, gp_ref, cw_ref, cb_ref, lg_ref, lb_ref, o_ref, y_ref, win, shifted):
        first = pl.program_id(1) == 0
        win[0:CONV_HALO, :] = jnp.where(first, 0.0, _glu(ap_ref, gp_ref))
        win[CONV_HALO:CONV_WIN, :] = _glu(a_ref, g_ref)
        _preshift(win, shifted)
        for r0, lanes in TAP_TILES:
            acc = jnp.zeros((TAP_ROWS, LANES), F32) + cb_ref[:, lanes]
            for w in range(CONV_W):
                acc = acc + _tap(win, shifted, lead + w, r0, lanes) * cw_ref[w:w + 1, lanes]
            y_ref[r0:r0 + TAP_ROWS, lanes] = acc
        yh, _ = _layer_norm_stats(y_ref[...])
        t = yh * lg_ref[...] + lb_ref[...]
        o_ref[...] = (t * _sigmoid(t)).astype(o_ref.dtype)

    def cur(c):
        return pl.BlockSpec((CONV_ROWS, TOK_WIDTH), lambda b, i: (b * nt + i, c))

    def prev(c):
        return pl.BlockSpec((CONV_HALO, TOK_WIDTH), lambda b, i: (jnp.maximum((b * nt + i) * sub - 1, 0), c))

    vec = pl.BlockSpec((1, TOK_WIDTH), lambda b, i: (0, 0))
    return pl.pallas_call(
        body, out_shape=(jax.ShapeDtypeStruct((n, D_MODEL), BF16), jax.ShapeDtypeStruct((n, TOK_WIDTH), F32)),
        grid=(batch, nt),
        in_specs=[cur(0), cur(1), prev(0), prev(1), pl.BlockSpec((32, TOK_WIDTH), lambda b, i: (0, 0)), vec, vec, vec],
        out_specs=(cur(0), cur(0)),
        scratch_shapes=[pltpu.VMEM((CONV_WIN, TOK_WIDTH), F32), pltpu.VMEM((SUBLANES - 1, SHIFT_ROWS, TOK_WIDTH), F32)],
        compiler_params=_params("parallel", "arbitrary"), name="conv_fwd")(z, z, z, z, cw, cb, lg, lb)


def conv_bwd(z, y, dcat, cw, lg, lb, batch, seq):
    n = z.shape[0]
    nt = seq // CONV_ROWS
    sub = CONV_ROWS // CONV_HALO
    lead = CONV_HALO - (CONV_W - 1)
    last_blk = n // CONV_HALO - 1

    def body(a_ref, g_ref, ap_ref, gp_ref, y_ref, yn_ref, do_ref, don_ref, cw_ref, lg_ref, lb_ref,
             dz_ref, dcw_ref, dsm_ref, win, shifted, dyw, dshifted, dg_o):
        b, i, which = pl.program_id(0), pl.program_id(1), pl.program_id(2)

        @pl.when(which == 0)
        def _():
            first, last = i == 0, i == nt - 1

            @pl.when((b == 0) & (i == 0))
            def _():
                dcw_ref[...] = jnp.zeros_like(dcw_ref)
                dsm_ref[...] = jnp.zeros_like(dsm_ref)

            win[0:CONV_HALO, :] = jnp.where(first, 0.0, _glu(ap_ref, gp_ref))
            win[CONV_HALO:CONV_WIN, :] = _glu(a_ref, g_ref)
            _preshift(win, shifted)
            yv = jnp.concatenate([y_ref[...], yn_ref[...]], axis=0)
            yh, rstd = _layer_norm_stats(yv)
            t = yh * lg_ref[...] + lb_ref[...]
            st = _sigmoid(t)
            dout = jnp.concatenate(
                [do_ref[...].astype(F32), jnp.where(last, 0.0, don_ref[...].astype(F32))], axis=0)
            dt = dout * st * (1.0 + t * (1.0 - st))
            dyh = dt * lg_ref[...]
            dy = rstd * (dyh - jnp.mean(dyh, axis=-1, keepdims=True)
                         - yh * jnp.mean(dyh * yh, axis=-1, keepdims=True))
            dyw[...] = dy
            _preshift(dyw, dshifted)
            dsm_ref[0:1, :] += jnp.sum(dy[0:CONV_ROWS], axis=0, keepdims=True)
            dsm_ref[1:2, :] += jnp.sum((dt * yh)[0:CONV_ROWS], axis=0, keepdims=True)
            dsm_ref[2:3, :] += jnp.sum(dt[0:CONV_ROWS], axis=0, keepdims=True)
            for c0 in range(0, TOK_WIDTH, LANES):
                lanes = slice(c0, c0 + LANES)
                dcw_acc = [jnp.zeros((SUBLANES, LANES), F32) for _ in range(CONV_W)]
                for r0 in range(0, CONV_ROWS, TAP_ROWS):
                    dyt = dyw[r0:r0 + TAP_ROWS, lanes]
                    dglu = jnp.zeros((TAP_ROWS, LANES), F32)
                    for w in range(CONV_W):
                        dcw_acc[w] = dcw_acc[w] + _fold_rows(dyt * _tap(win, shifted, lead + w, r0, lanes))
                        dglu = dglu + _tap(dyw, dshifted, CONV_W - 1 - w, r0, lanes) * cw_ref[w:w + 1, lanes]
                    avt = a_ref[r0:r0 + TAP_ROWS, lanes].astype(F32)
                    sgt = _sigmoid(g_ref[r0:r0 + TAP_ROWS, lanes].astype(F32))
                    dz_ref[r0:r0 + TAP_ROWS, lanes] = (dglu * sgt).astype(dz_ref.dtype)
                    dg_o[r0:r0 + TAP_ROWS, lanes] = (dglu * avt * sgt * (1.0 - sgt)).astype(dg_o.dtype)
                for w in range(CONV_W):
                    dcw_ref[w:w + 1, lanes] += jnp.sum(dcw_acc[w], axis=0, keepdims=True)

        @pl.when(which == 1)
        def _():
            dz_ref[...] = dg_o[...]

    def ahead(b, i, t):
        return jnp.minimum(b * nt + i + t, batch * nt - 1)

    def cur(c):
        return pl.BlockSpec((CONV_ROWS, TOK_WIDTH), lambda b, i, t: (ahead(b, i, t), c))

    def prev(c):
        return pl.BlockSpec((CONV_HALO, TOK_WIDTH), lambda b, i, t: (jnp.maximum(ahead(b, i, t) * sub - 1, 0), c))

    nxt = pl.BlockSpec((CONV_HALO, TOK_WIDTH),
                       lambda b, i, t: (jnp.minimum((ahead(b, i, t) + 1) * sub, last_blk), 0))
    vec = pl.BlockSpec((1, TOK_WIDTH), lambda b, i, t: (0, 0))
    full32 = pl.BlockSpec((32, TOK_WIDTH), lambda b, i, t: (0, 0))
    return pl.pallas_call(
        body,
        out_shape=(jax.ShapeDtypeStruct(z.shape, BF16), jax.ShapeDtypeStruct((32, TOK_WIDTH), F32),
                   jax.ShapeDtypeStruct((8, TOK_WIDTH), F32)),
        grid=(batch, nt, 2),
        in_specs=[cur(0), cur(1), prev(0), prev(1), cur(0), nxt, cur(0), nxt, full32, vec, vec],
        out_specs=(pl.BlockSpec((CONV_ROWS, TOK_WIDTH), lambda b, i, t: (b * nt + i, t)), full32,
                   pl.BlockSpec((8, TOK_WIDTH), lambda b, i, t: (0, 0))),
        scratch_shapes=[pltpu.VMEM((CONV_WIN, TOK_WIDTH), F32), pltpu.VMEM((SUBLANES - 1, SHIFT_ROWS, TOK_WIDTH), F32),
                        pltpu.VMEM((CONV_WIN, TOK_WIDTH), F32), pltpu.VMEM((SUBLANES - 1, SHIFT_ROWS, TOK_WIDTH), F32),
                        pltpu.VMEM((CONV_ROWS, TOK_WIDTH), BF16)],
        compiler_params=_params("arbitrary", "arbitrary", "arbitrary"), name="conv_bwd")(
            z, z, z, z, y, y, dcat, dcat, cw, lg, lb)


def _place():
    return lax.axis_index("x"), lax.axis_index("y"), lax.axis_index("c")


def _other_chips(x, y):
    return [(1 - x, y), (x, 1 - y), (1 - x, 1 - y)]


def reduce_small(arrays):
    na = len(arrays)

    def body(*refs):
        ins, outs, bufs = refs[:na], refs[na:2 * na], refs[2 * na:3 * na]
        send_sems, recv_sems = refs[3 * na:]
        x, y, c = _place()
        me = 4 * x + 2 * y + c
        copies = []
        for a in range(na):
            bufs[a][me] = ins[a][...]
            for k in range(1, N_DEV):
                cp = pltpu.make_async_remote_copy(
                    src_ref=ins[a], dst_ref=bufs[a].at[me], send_sem=send_sems.at[a, k - 1],
                    recv_sem=recv_sems.at[a, k - 1],
                    device_id=(x ^ (k >> 2), y ^ ((k >> 1) & 1), c ^ (k & 1)), device_id_type=MESH)
                cp.start()
                copies.append(cp)
        for a in range(na):
            for k in range(1, N_DEV):
                src = 4 * (x ^ (k >> 2)) + 2 * (y ^ ((k >> 1) & 1)) + (c ^ (k & 1))
                pltpu.make_async_remote_copy(
                    src_ref=ins[a], dst_ref=bufs[a].at[src], send_sem=send_sems.at[a, k - 1],
                    recv_sem=recv_sems.at[a, k - 1], device_id=(x, y, c), device_id_type=MESH).wait_recv()
        for cp in copies:
            cp.wait_send()
        for a in range(na):
            total = bufs[a][0]
            for dev in range(1, N_DEV):
                total = total + bufs[a][dev]
            outs[a][...] = total

    vmem = pl.BlockSpec(memory_space=pltpu.VMEM)
    return pl.pallas_call(
        body, out_shape=tuple(jax.ShapeDtypeStruct(a.shape, F32) for a in arrays),
        in_specs=[vmem] * na, out_specs=tuple([vmem] * na),
        scratch_shapes=[pltpu.VMEM((N_DEV,) + a.shape, F32) for a in arrays]
        + [pltpu.SemaphoreType.DMA((na, N_DEV - 1)), pltpu.SemaphoreType.DMA((na, N_DEV - 1))],
        compiler_params=pltpu.CompilerParams(vmem_limit_bytes=VMEM_LIMIT), name="small_reduce")(*arrays)


def adamw_small(ws, gs, ms, vs):
    na = len(ws)
    c1 = 1.0 / (1.0 - ADAM_B1 ** ADAM_STEP)
    c2 = 1.0 / (1.0 - ADAM_B2 ** ADAM_STEP)

    def body(*refs):
        w_refs, g_refs, m_refs, v_refs = (refs[i * na:(i + 1) * na] for i in range(4))
        d_refs, nm_refs, nv_refs = (refs[(4 + i) * na:(5 + i) * na] for i in range(3))
        for a in range(na):
            gv = g_refs[a][...]
            nm = ADAM_B1 * m_refs[a][...] + (1.0 - ADAM_B1) * gv
            nv = ADAM_B2 * v_refs[a][...] + (1.0 - ADAM_B2) * (gv * gv)
            nm_refs[a][...] = nm
            nv_refs[a][...] = nv
            d_refs[a][...] = -ADAM_LR * ((nm * c1) / (jnp.sqrt(nv * c2) + ADAM_EPS) + ADAM_WD * w_refs[a][...])

    vmem = pl.BlockSpec(memory_space=pltpu.VMEM)
    shapes = tuple(jax.ShapeDtypeStruct(w.shape, F32) for w in ws)
    outs = pl.pallas_call(
        body, out_shape=shapes * 3, in_specs=[vmem] * (4 * na), out_specs=tuple([vmem] * (3 * na)),
        compiler_params=pltpu.CompilerParams(vmem_limit_bytes=VMEM_LIMIT), name="adamw_small")(*ws, *gs, *ms, *vs)
    return outs[:na], outs[na:2 * na], outs[2 * na:]


def gather_weights(shards, name, collective_id):
    nw = len(shards)
    ns = [s.shape[0] for s in shards]
    in_refs = [jax.new_ref(s, memory_space=pltpu.MemorySpace.HBM) for s in shards]
    out_refs = [jax.empty_ref(jax.ShapeDtypeStruct((N_DEV * s.shape[0], s.shape[1]), s.dtype),
                              memory_space=pltpu.MemorySpace.HBM) for s in shards]

    @pl.kernel(mesh=plsc.ScalarSubcoreMesh(axis_name="seq", num_cores=1), name=name,
               scratch_types=(pltpu.SemaphoreType.DMA((nw, 7)), pltpu.SemaphoreType.DMA((nw, 7)),
                              pltpu.SemaphoreType.DMA((nw,))),
               compiler_params=pltpu.CompilerParams(collective_id=collective_id))
    def launch(send_sems, recv_sems, local_sems):
        x, y, c = _place()
        me, sib = (x, y, c), (x, y, 1 - c)
        chips = _other_chips(x, y)
        barrier = pltpu.get_barrier_semaphore()
        for peer in [sib] + [(*chip, c) for chip in chips]:
            pl.semaphore_signal(barrier, inc=1, device_id=peer, device_id_type=MESH)
        pl.semaphore_wait(barrier, 4)

        def rows(w, dev):
            return out_refs[w].at[pl.ds((4 * dev[0] + 2 * dev[1] + dev[2]) * ns[w], ns[w]), :]

        def copy(w, k, block, to, src=None):
            return pltpu.make_async_remote_copy(
                src_ref=rows(w, block) if src is None else src, dst_ref=rows(w, block),
                send_sem=send_sems.at[w, k], recv_sem=recv_sems.at[w, k], device_id=to, device_id_type=MESH)

        started, sends = [], []
        for w in range(nw):
            mine = pltpu.make_async_copy(in_refs[w], rows(w, me), local_sems.at[w])
            mine.start()
            started.append(mine)
            first = [copy(w, 0, me, sib, src=in_refs[w])]
            first += [copy(w, 1 + j, me, (*chip, c), src=in_refs[w]) for j, chip in enumerate(chips)]
            for cp in first:
                cp.start()
            sends += first
        for w in range(nw):
            for j, chip in enumerate(chips):
                copy(w, 1 + j, (*chip, c), me).wait_recv()
                fwd = copy(w, 4 + j, (*chip, c), sib)
                fwd.start()
                sends.append(fwd)
        for w in range(nw):
            copy(w, 0, sib, me).wait_recv()
            for j, chip in enumerate(chips):
                copy(w, 4 + j, (*chip, 1 - c), me).wait_recv()
        for cp in sends:
            cp.wait_send()
        for mine in started:
            mine.wait()

    launch()
    return [r[...] for r in out_refs]


def _sequencer_exchange(sources, out_rows, peers_of, copies_of, name, collective_id):
    nw = len(sources)
    in_refs = [jax.new_ref(s, memory_space=pltpu.MemorySpace.HBM) for s in sources]
    out_refs = [jax.empty_ref(jax.ShapeDtypeStruct((rows, s.shape[1]), s.dtype), memory_space=pltpu.MemorySpace.HBM)
                for rows, s in zip(out_rows, sources)]
    per = len(copies_of(0, 0, 0, 0))

    @pl.kernel(mesh=plsc.ScalarSubcoreMesh(axis_name="seq", num_cores=1), name=name,
               scratch_types=(pltpu.SemaphoreType.DMA((nw, per)), pltpu.SemaphoreType.DMA((nw, per))),
               compiler_params=pltpu.CompilerParams(collective_id=collective_id))
    def launch(send_sems, recv_sems):
        x, y, c = _place()
        peers = peers_of(x, y, c)
        barrier = pltpu.get_barrier_semaphore()
        for peer in peers:
            pl.semaphore_signal(barrier, inc=1, device_id=peer, device_id_type=MESH)
        pl.semaphore_wait(barrier, len(peers))
        copies = []
        for w in range(nw):
            for k, (src_blk, dst_blk, rows, peer) in enumerate(copies_of(x, y, c, w)):
                cp = pltpu.make_async_remote_copy(
                    src_ref=in_refs[w].at[pl.ds(src_blk * rows, rows), :],
                    dst_ref=out_refs[w].at[pl.ds(dst_blk * rows, rows), :],
                    send_sem=send_sems.at[w, k], recv_sem=recv_sems.at[w, k], device_id=peer, device_id_type=MESH)
                cp.start()
                copies.append(cp)
        for cp in copies:
            cp.wait_recv()
        for cp in copies:
            cp.wait_send()

    launch()
    return [r[...] for r in out_refs]


def scatter_to_sibling(grads, name, collective_id):
    ns = [g.shape[0] // N_DEV for g in grads]
    return _sequencer_exchange(
        grads, [4 * n for n in ns],
        lambda x, y, c: [(x, y, 1 - c)],
        lambda x, y, c, w: [(2 * q + 1 - c, q, ns[w], (x, y, 1 - c)) for q in range(4)],
        name, collective_id)


def scatter_to_chips(parts, name, collective_id):
    ns = [p.shape[0] // 4 for p in parts]
    return _sequencer_exchange(
        parts, [3 * n for n in ns],
        lambda x, y, c: [(*chip, c) for chip in _other_chips(x, y)],
        lambda x, y, c, w: [(2 * chip[0] + chip[1], j, ns[w], (*chip, c)) for j, chip in enumerate(_other_chips(x, y))],
        name, collective_id)


def add_sibling(grads, landeds, core, name):
    nw = len(grads)

    def body(c_ref, *refs):
        for w in range(nw):
            g_ref, l_ref, o_ref = refs[2 * w], refs[2 * w + 1], refs[2 * nw + w]
            o_ref[...] = (g_ref[...].astype(F32) + l_ref[...].astype(F32)).astype(o_ref.dtype)

    in_specs, out_specs, args = [], [], []
    for g, ld in zip(grads, landeds):
        n, cols = ld.shape[0] // 4, g.shape[1]
        in_specs += [pl.BlockSpec((n, cols), lambda q, c_ref: (2 * q + c_ref[0], 0)),
                     pl.BlockSpec((n, cols), lambda q, c_ref: (q, 0))]
        out_specs.append(pl.BlockSpec((n, cols), lambda q, c_ref: (q, 0)))
        args += [g, ld]
    grid_spec = pltpu.PrefetchScalarGridSpec(
        num_scalar_prefetch=1, grid=(4,), in_specs=in_specs, out_specs=tuple(out_specs))
    return pl.pallas_call(
        body, out_shape=tuple(jax.ShapeDtypeStruct(ld.shape, ld.dtype) for ld in landeds), grid_spec=grid_spec,
        compiler_params=_params("arbitrary"), name=name)(core, *args)


ADAMW_HALVES = 2


def adamw_shards(items, chip, name):
    c1 = 1.0 / (1.0 - ADAM_B1 ** ADAM_STEP)
    c2 = 1.0 / (1.0 - ADAM_B2 ** ADAM_STEP)
    ni = len(items)

    def body(q_ref, *refs):
        outs = refs[len(refs) - 4 * ni:]
        for k in range(ni):
            w_ref, m_ref, v_ref, p_ref, l0_ref, l1_ref, l2_ref = refs[7 * k:7 * k + 7]
            g_ref, d_ref, nm_ref, nv_ref = outs[4 * k:4 * k + 4]
            gv = ((p_ref[...].astype(F32) + l0_ref[...].astype(F32)) + l1_ref[...].astype(F32)) + l2_ref[...].astype(F32)
            nm = ADAM_B1 * m_ref[...] + (1.0 - ADAM_B1) * gv
            nv = ADAM_B2 * v_ref[...] + (1.0 - ADAM_B2) * (gv * gv)
            g_ref[...] = gv
            nm_ref[...] = nm
            nv_ref[...] = nv
            d_ref[...] = -ADAM_LR * ((nm * c1) / (jnp.sqrt(nv * c2) + ADAM_EPS) + ADAM_WD * w_ref[...])

    sub = ADAMW_HALVES
    in_specs, out_specs, out_shape, args, donated = [], [], [], [chip], []
    for layer, w, m, v, part, landed, earlier in items:
        rows, cols = landed.shape[0] // (3 * sub), w.shape[1]

        def block(first, rows=rows, cols=cols):
            return pl.BlockSpec((rows, cols), lambda i, q_ref: (first(q_ref) * sub + i, 0))

        own = block(lambda q_ref, layer=layer: layer)
        in_specs += [own, own, own, block(lambda q_ref: q_ref[0])] + [block(lambda q_ref, j=j: j) for j in range(3)]
        args += [w, m, v, part, landed, landed, landed]
        out_specs += [own] * 4
        out_shape += [jax.ShapeDtypeStruct(w.shape, F32)] * 4
        donated.append(earlier)
    aliases = {}
    for k, earlier in enumerate(donated):
        if earlier is not None:
            for j in range(4):
                aliases[len(args)] = 4 * k + j
                in_specs.append(ANY)
                args.append(earlier[j])
    grid_spec = pltpu.PrefetchScalarGridSpec(
        num_scalar_prefetch=1, grid=(sub,), in_specs=in_specs, out_specs=tuple(out_specs))
    outs = pl.pallas_call(
        body, out_shape=tuple(out_shape), grid_spec=grid_spec, input_output_aliases=aliases,
        compiler_params=_params("arbitrary"), name=name)(*args)
    return [tuple(outs[4 * k:4 * k + 4]) for k in range(ni)]


def _pack(arrays):
    flat = jnp.concatenate([a.reshape(-1).astype(F32) for a in arrays])
    pad = (-flat.shape[0]) % (8 * LANES)
    return jnp.pad(flat, (0, pad)).reshape(-1, LANES)


def _unpack(slab, shapes):
    flat = slab.reshape(slab.shape[:-2] + (-1,))
    out, off = [], 0
    for shp in shapes:
        size = 1
        for s in shp:
            size *= s
        out.append(flat[..., off:off + size].reshape(flat.shape[:-1] + tuple(shp)))
        off += size
    return out


def kernel(x, mem, norm1_g, mem_norm_g, a_w_in, a_q_g, a_k_g, a_rel_bias, b_w_in, b_b_in, b_conv_w, b_conv_b, b_ln_g, b_ln_b, mq_g, mk_g, w_mem_kv, w_out, norm2_g, w_gate, w_up, w_down, loss_target, m_norm1_g, m_mem_norm_g, m_a_w_in, m_a_q_g, m_a_k_g, m_a_rel_bias, m_b_w_in, m_b_b_in, m_b_conv_w, m_b_conv_b, m_b_ln_g, m_b_ln_b, m_mq_g, m_mk_g, m_w_mem_kv, m_w_out, m_norm2_g, m_w_gate, m_w_up, m_w_down, v_norm1_g, v_mem_norm_g, v_a_w_in, v_a_q_g, v_a_k_g, v_a_rel_bias, v_b_w_in, v_b_b_in, v_b_conv_w, v_b_conv_b, v_b_ln_g, v_b_ln_b, v_mq_g, v_mk_g, v_w_mem_kv, v_w_out, v_norm2_g, v_w_gate, v_w_up, v_w_down):
    batch, seq, d = x.shape
    mtok = mem.shape[1]
    n = batch * seq
    ax, ay, ac = _place()
    me = 4 * ax + 2 * ay + ac
    core_arr = jnp.reshape(ac, (1,)).astype(jnp.int32)
    chip_arr = jnp.reshape(2 * ax + ay, (1,)).astype(jnp.int32)

    def t_bf16(w):
        return jnp.transpose(w).astype(BF16)

    def after(value, *earlier):
        return lax.optimization_barrier((value, *earlier))[0]

    def gather_mix(l, when, name, collective_id):
        srcs = [w_mem_kv[l].astype(BF16), w_out[l].astype(BF16)]
        if l == 1:
            srcs += [t_bf16(b_w_in[0]), _pack([b_b_in, b_conv_w, b_conv_b, b_ln_g, b_ln_b])]
        return gather_weights([after(srcs[0], *when)] + srcs[1:], name, collective_id)

    def gather_ffn(l, when, name, collective_id):
        return gather_weights(
            [after(t_bf16(w_gate[l]), *when), t_bf16(w_up[l]), w_down[l].astype(BF16)], name, collective_id)

    f_loc = b_b_in.shape[1]
    c_loc = b_conv_b.shape[1]

    def two(g):
        return jnp.concatenate([g, g], axis=-1)

    gq2, gk2 = two(a_q_g), two(a_k_g)
    rel16 = jnp.pad(a_rel_bias[0], ((0, 16 - a_rel_bias.shape[1]), (0, 0)))
    bias = bias_blocks(rel16)

    x0 = x.reshape(n, d)
    mem2 = mem.reshape(batch * mtok, d)

    saved = []
    xin = x0
    a_win_t, = gather_weights([t_bf16(a_w_in[0])], "gather_in_a", 1)
    wg_t, wu_t, wd, wo, wkv = [None] * 2, [None] * 2, [None] * 2, [None] * 2, [None] * 2
    h = after(rms_fwd(xin, norm1_g[0:1], name="rms1_fwd_0"), bias)
    target = loss_target.reshape(n, d)
    for l in range(2):
        gq4 = jnp.tile(mq_g[l:l + 1], (1, 4))
        gk4 = jnp.tile(mk_g[l:l + 1], (1, 4))
        y_conv = None
        if l == 0:
            wkv[0], wo[0] = gather_mix(0, (h, a_win_t), "gather_mix_a", 2)
            z = mm_nt(h, a_win_t, name="in_proj_a")
            wg_t[0], wu_t[0], wd[0] = gather_ffn(0, (z, wkv[0]), "gather_ffn_a", 3)
            cat = attn_fwd(z, gq2, gk2, bias, batch, seq)
            wkv[1], wo[1], b_win_t, conv_slabs = gather_mix(1, (cat, wg_t[0]), "gather_mix_b", 4)
            qcol = 3 * TOK_WIDTH // MEM_WIDTH
        else:
            small_shapes = [(f_loc,), (CONV_W, c_loc), (c_loc,), (c_loc,), (c_loc,)]
            bb_g, cw_g, cb_g, lg_g, lb_g = _unpack(conv_slabs.reshape(N_DEV, -1, LANES), small_shapes)
            bb_full = bb_g.reshape(1, -1)
            cw_full = jnp.pad(jnp.transpose(cw_g, (1, 0, 2)).reshape(CONV_W, -1), ((0, 32 - CONV_W), (0, 0)))
            cb_full, lg_full, lb_full = cb_g.reshape(1, -1), lg_g.reshape(1, -1), lb_g.reshape(1, -1)
            z = mm_nt(h, b_win_t, bias=bb_full, name="in_proj_b")
            cat, y_conv = conv_fwd(z, cw_full, cb_full, lg_full, lb_full, batch, seq)
            qcol = 2 * TOK_WIDTH // MEM_WIDTH
        cat, mem_n, kv = memattn_fwd(
            z, mem2, mem_norm_g[l:l + 1], wkv[l], gq4, gk4, cat, batch, seq, qcol, name=f"memattn_fwd_{l}")
        x1, h2 = proj_norm(cat, wo[l], xin, norm2_g[l:l + 1], name=f"out_proj_{l}")
        if l == 0:
            wg_t[1], wu_t[1], wd[1] = gather_ffn(1, (x1, b_win_t), "gather_ffn_b", 5)
        if l == 0:
            gate, up, act, x2, h_next = ffn_fwd(h2, wg_t[0], wu_t[0], wd[0], x1, gain=norm1_g[1:2], name="ffn_fwd_0")
        else:
            gate, up, act, dx_b, loss_blk = ffn_fwd(h2, wg_t[1], wu_t[1], wd[1], x1, target=target, name="ffn_fwd_1")
        saved.append(dict(xin=xin, h=h, mem_n=mem_n, kv=kv, gq4=gq4, gk4=gk4, z=z, qcol=qcol, cat=cat, x1=x1, h2=h2,
                          gate=gate, up=up, act=act, y_conv=y_conv))
        if l == 0:
            xin, h = x2, h_next

    big = {}
    small = {}
    reduced = {}
    groups = 0

    def scatter_siblings(keys):
        nonlocal groups
        gid = groups
        groups += 1
        return gid, keys, scatter_to_sibling([big[k] for k in keys], f"scatter_sibling_{gid}", 8 + 2 * gid)

    def scatter_chips(stage1, when):
        gid, keys, landed1 = stage1
        parts = add_sibling([after(big[keys[0]], when)] + [big[k] for k in keys[1:]], landed1, core_arr,
                            name=f"add_sibling_{gid}")
        landed2 = scatter_to_chips(parts, f"scatter_chips_{gid}", 9 + 2 * gid)
        for k, p, ld in zip(keys, parts, landed2):
            reduced[k] = (p, ld)
        return parts, landed2

    def rows_of(w, transposed):
        w = jnp.swapaxes(w, 1, 2) if transposed else w
        return w.reshape(w.shape[0] * w.shape[1], w.shape[2])

    sharded = {
        "win0": (2, True), "win1": (6, True), "wkv": (14, False), "wo": (15, False),
        "wg": (17, True), "wu": (18, True), "wd": (19, False)}
    weights = [norm1_g, mem_norm_g, a_w_in, a_q_g, a_k_g, a_rel_bias, b_w_in, b_b_in, b_conv_w, b_conv_b, b_ln_g,
               b_ln_b, mq_g, mk_g, w_mem_kv, w_out, norm2_g, w_gate, w_up, w_down]
    moms = [m_norm1_g, m_mem_norm_g, m_a_w_in, m_a_q_g, m_a_k_g, m_a_rel_bias, m_b_w_in, m_b_b_in, m_b_conv_w,
            m_b_conv_b, m_b_ln_g, m_b_ln_b, m_mq_g, m_mk_g, m_w_mem_kv, m_w_out, m_norm2_g, m_w_gate, m_w_up, m_w_down]
    vels = [v_norm1_g, v_mem_norm_g, v_a_w_in, v_a_q_g, v_a_k_g, v_a_rel_bias, v_b_w_in, v_b_b_in, v_b_conv_w,
            v_b_conv_b, v_b_ln_g, v_b_ln_b, v_mq_g, v_mk_g, v_w_mem_kv, v_w_out, v_norm2_g, v_w_gate, v_w_up, v_w_down]
    updated = {}

    def update_layer(l, when):
        for group, keys in (("ffn", ("wg", "wu", "wd")), ("mix", (f"win{l}", "wkv", "wo"))):
            items = []
            for key in keys:
                idx, transposed = sharded[key]
                layer, rkey = (0, key) if key.startswith("win") else (l, f"{key}{l}")
                part, landed = reduced[rkey]
                w_rows = rows_of(weights[idx], transposed)
                items.append((layer, after(w_rows, when) if not items else w_rows, rows_of(moms[idx], transposed),
                              rows_of(vels[idx], transposed), part, landed, updated.get(key)))
            for key, result in zip(keys, adamw_shards(items, chip_arr, name=f"adamw_{group}_{l}")):
                updated[key] = result

    mix_landed = None
    for l in (1, 0):
        sv = saved[l]
        dgate, dup, dx1_b, dcat, small[f"norm2_{l}"] = ffn_bwd(
            dx_b, wd[l], sv["gate"], sv["up"], wg_t[l], wu_t[l], sv["x1"], norm2_g[l:l + 1], wo[l], name=f"ffn_bwd_{l}")
        if l == 0:
            dgate = after(dgate, *mix_landed)
            update_layer(1, dx1_b)
        big[f"wg{l}"], big[f"wu{l}"], big[f"wd{l}"] = ffn_weight_grads(
            dgate, dup, sv["h2"], sv["act"], dx_b, name=f"grad_ffn_{l}")
        stage1 = scatter_siblings([f"wd{l}", f"wg{l}", f"wu{l}"])
        big[f"wo{l}"] = mm_tn(sv["cat"], dx1_b, name=f"grad_wo_{l}")
        parts, ffn_landed = scatter_chips(stage1, big[f"wo{l}"])
        dcat = after(dcat, *parts)
        if l == 0:
            dz, dbias, small["a_q"], small["a_k"] = attn_bwd(sv["z"], dcat, gq2, gk2, bias, batch, seq)
            small["rel"] = bias_grad(dbias)
            win_t = a_win_t
        else:
            dz, small["cw"], small["csum"] = conv_bwd(sv["z"], sv["y_conv"], dcat, cw_full, lg_full, lb_full, batch, seq)
            win_t = b_win_t
        dz = after(dz, *ffn_landed)
        dz, dkv, small[f"mq_{l}"], small[f"mk_{l}"] = memattn_bwd(
            sv["z"], sv["kv"], dcat, sv["gq4"], sv["gk4"], dz, batch, seq, sv["qcol"], name=f"memattn_bwd_{l}")
        big[f"win{l}"] = mm_tn(dz, sv["h"], name=f"grad_win_{l}")
        big[f"wkv{l}"], small[f"memnorm_{l}"] = mem_grads(dkv, wkv[l], mem2, sv["mem_n"], name=f"mem_grads_{l}")
        stage1 = scatter_siblings([f"win{l}", f"wkv{l}", f"wo{l}"])
        dx_b, small[f"norm1_{l}"], dz_sum = in_proj_bwd(
            dz, win_t, sv["xin"], norm1_g[l:l + 1], dx1_b, BF16 if l == 1 else F32, name=f"in_proj_bwd_{l}")
        if l == 1:
            small["bb"] = dz_sum
        parts, mix_landed = scatter_chips(stage1, dx_b)
        dx_b = after(dx_b, *parts)
    grad_x = dx_b.reshape(batch, seq, d)
    update_layer(0, dx_b)

    def shaped(rows, idx, transposed):
        shp = weights[idx].shape
        if transposed:
            return jnp.swapaxes(rows.reshape(shp[0], shp[2], shp[1]), 1, 2)
        return rows.reshape(shp)

    def fold(v, groups):
        return jnp.sum(v.reshape(groups, HEAD_DIM), axis=0, keepdims=True)

    heads = a_rel_bias.shape[1]
    small_list = [
        jnp.concatenate([small["norm1_0"], small["norm1_1"]]),
        jnp.concatenate([small["memnorm_0"], small["memnorm_1"]]),
        fold(small["a_q"], 2), fold(small["a_k"], 2), small["rel"][:heads],
        small["bb"], small["cw"][:CONV_W], small["csum"][0:1], small["csum"][1:2], small["csum"][2:3],
        jnp.concatenate([fold(small["mq_0"], 4), fold(small["mq_1"], 4)]),
        jnp.concatenate([fold(small["mk_0"], 4), fold(small["mk_1"], 4)]),
        jnp.concatenate([small["norm2_0"], small["norm2_1"]]),
    ]
    (g_norm1, g_memnorm, g_aq, g_ak, g_rel, g_bb_full, g_cw_full, g_cb_full, g_lg_full, g_lb_full,
     g_mq, g_mk, g_norm2, loss_sum) = reduce_small(small_list + [loss_blk])
    loss = loss_sum[0, 0]
    g_bb = lax.dynamic_slice_in_dim(g_bb_full, me * f_loc, f_loc, axis=1)
    g_cw = lax.dynamic_slice_in_dim(g_cw_full, me * c_loc, c_loc, axis=1)
    g_cb = lax.dynamic_slice_in_dim(g_cb_full, me * c_loc, c_loc, axis=1)
    g_lg = lax.dynamic_slice_in_dim(g_lg_full, me * c_loc, c_loc, axis=1)
    g_lb = lax.dynamic_slice_in_dim(g_lb_full, me * c_loc, c_loc, axis=1)

    grads = [g_norm1, g_memnorm, None, g_aq, g_ak, g_rel, None, g_bb, g_cw, g_cb, g_lg, g_lb,
             g_mq, g_mk, None, None, g_norm2, None, None, None]
    deltas, new_m, new_v = [None] * 20, [None] * 20, [None] * 20
    for key, (idx, transposed) in sharded.items():
        grads[idx], deltas[idx], new_m[idx], new_v[idx] = (shaped(r, idx, transposed) for r in updated[key])

    def flat2(a):
        return a.reshape(a.shape[-2:])

    small_idx = [i for i in range(20) if i not in {idx for idx, _ in sharded.values()}]
    dl, nm, nv = adamw_small([flat2(weights[i]) for i in small_idx], [flat2(grads[i]) for i in small_idx],
                             [flat2(moms[i]) for i in small_idx], [flat2(vels[i]) for i in small_idx])
    for i, a, b, cc in zip(small_idx, dl, nm, nv):
        shp = weights[i].shape
        grads[i], deltas[i], new_m[i], new_v[i] = grads[i].reshape(shp), a.reshape(shp), b.reshape(shp), cc.reshape(shp)

    return (loss, grad_x, *grads, *deltas, *new_m, *new_v)
```

```python
import jax
import jax.numpy as jnp
from jax import lax
from jax.experimental import pallas as pl
from jax.experimental.pallas import tpu as pltpu
from jax.experimental.pallas import tpu_sc as plsc

F32 = jnp.float32
BF16 = jnp.bfloat16
HIGHEST = lax.Precision.HIGHEST
MESH = pl.DeviceIdType.MESH
ANY = pl.BlockSpec(memory_space=pl.ANY)

N_DEV = 8
D_MODEL = 1024
HEAD_DIM = 64
TOK_WIDTH = 768
MEM_WIDTH = 256
CHUNK = 64
Q_BLOCK = 256
KEY_WIN = 768
BAND = 576
N_REL = 192
CONV_W = 31
CONV_HALO = 32
NORM_EPS = 1e-6
NEG_INF = -1e30
ATTN_SCALE = HEAD_DIM ** -0.5
LANES = 128
ROW_TILE = 512
VMEM_LIMIT = 56 * 1024 * 1024

ADAM_LR, ADAM_B1, ADAM_B2, ADAM_EPS, ADAM_WD, ADAM_STEP = 0.001, 0.9, 0.999, 1e-08, 0.01, 10


def _params(*sem):
    return pltpu.CompilerParams(dimension_semantics=sem, vmem_limit_bytes=VMEM_LIMIT)


WIDE_ROW_TILE = 1024


def _row_tile(m, rows=ROW_TILE):
    return rows if m % rows == 0 else m


def _col_tile(n, cap=1408):
    best = None
    for t in range(LANES, min(n, cap) + 1, LANES):
        if n % t == 0:
            best = t
    return best if best is not None else n


def _dot(a, b, ca, cb):
    return lax.dot_general(a, b, (((ca,), (cb,)), ((), ())), preferred_element_type=F32)


def _sigmoid(x):
    return 0.5 * jnp.tanh(0.5 * x) + 0.5


def mm_nt(a, b, bias=None, out_dtype=BF16, name="mm_nt"):
    m, k = a.shape
    n = b.shape[0]
    tm, tn = _row_tile(m, WIDE_ROW_TILE), _col_tile(n)

    def body(*refs):
        a_ref, b_ref = refs[0], refs[1]
        o_ref = refs[-1]
        acc = _dot(a_ref[...].astype(BF16), b_ref[...].astype(BF16), 1, 1)
        if bias is not None:
            acc = acc + refs[2][...]
        o_ref[...] = acc.astype(o_ref.dtype)

    in_specs = [pl.BlockSpec((tm, k), lambda j, i: (i, 0)), pl.BlockSpec((tn, k), lambda j, i: (j, 0))]
    args = [a, b]
    if bias is not None:
        in_specs.append(pl.BlockSpec((1, tn), lambda j, i: (0, j)))
        args.append(bias)
    return pl.pallas_call(
        body, out_shape=jax.ShapeDtypeStruct((m, n), out_dtype), grid=(n // tn, m // tm),
        in_specs=in_specs, out_specs=pl.BlockSpec((tm, tn), lambda j, i: (i, j)),
        compiler_params=_params("parallel", "arbitrary"), name=name)(*args)


def mm_tn(a, b, out_dtype=BF16, name="mm_tn"):
    t, r = a.shape
    c = b.shape[1]
    tr = _col_tile(r, 512)

    def body(a_ref, b_ref, o_ref):
        o_ref[...] = _dot(a_ref[...].astype(BF16), b_ref[...].astype(BF16), 0, 0).astype(o_ref.dtype)

    return pl.pallas_call(
        body, out_shape=jax.ShapeDtypeStruct((r, c), out_dtype), grid=(r // tr,),
        in_specs=[pl.BlockSpec((t, tr), lambda i: (0, i)), pl.BlockSpec((t, c), lambda i: (0, 0))],
        out_specs=pl.BlockSpec((tr, c), lambda i: (i, 0)),
        compiler_params=_params("parallel"), name=name)(a, b)


def _resident(shape):
    return pl.BlockSpec(shape, lambda i: (0, 0), pipeline_mode=pl.Buffered(1))


def proj_norm(a, b, res, gain, name):
    m, k = a.shape
    n = b.shape[1]
    tm = _row_tile(m)

    def body(a_ref, b_ref, res_ref, g_ref, x_ref, h_ref):
        xv = res_ref[...] + _dot(a_ref[...], b_ref[...], 1, 0)
        x_ref[...] = xv
        r = lax.rsqrt(jnp.mean(xv * xv, axis=-1, keepdims=True) + NORM_EPS)
        h_ref[...] = (xv * r * g_ref[...]).astype(BF16)

    row = pl.BlockSpec((tm, n), lambda i: (i, 0))
    return pl.pallas_call(
        body, out_shape=(jax.ShapeDtypeStruct((m, n), F32), jax.ShapeDtypeStruct((m, n), BF16)), grid=(m // tm,),
        in_specs=[pl.BlockSpec((tm, k), lambda i: (i, 0)), _resident((k, n)), row, _resident((1, n))],
        out_specs=(row, row), compiler_params=_params("parallel"), name=name)(a, b, res, gain)


def in_proj_bwd(dz, w_t, x, gain, dres, out_dtype, name):
    m, n = x.shape
    k = dz.shape[1]
    tm = _row_tile(m)

    def body(dz_ref, w_ref, x_ref, g_ref, dres_ref, dx_ref, dg_ref, cs_ref):
        @pl.when(pl.program_id(0) == 0)
        def _():
            dg_ref[...] = jnp.zeros_like(dg_ref)
            cs_ref[...] = jnp.zeros_like(cs_ref)

        dzv = dz_ref[...]
        cs_ref[...] += jnp.sum(dzv.astype(F32), axis=0, keepdims=True)
        dhv = _dot(dzv, w_ref[...], 1, 0)
        xv = x_ref[...]
        r = lax.rsqrt(jnp.mean(xv * xv, axis=-1, keepdims=True) + NORM_EPS)
        xhat = xv * r
        dg_ref[...] += jnp.sum(dhv * xhat, axis=0, keepdims=True)
        dxhat = dhv * g_ref[...]
        dx = dres_ref[...].astype(F32) + r * (dxhat - xhat * jnp.mean(dxhat * xhat, axis=-1, keepdims=True))
        dx_ref[...] = dx.astype(dx_ref.dtype)

    row = pl.BlockSpec((tm, n), lambda i: (i, 0))
    return pl.pallas_call(
        body, out_shape=(jax.ShapeDtypeStruct((m, n), out_dtype), jax.ShapeDtypeStruct((1, n), F32),
                         jax.ShapeDtypeStruct((1, k), F32)), grid=(m // tm,),
        in_specs=[pl.BlockSpec((tm, k), lambda i: (i, 0)), _resident(w_t.shape), row, _resident((1, n)), row],
        out_specs=(row, pl.BlockSpec((1, n), lambda i: (0, 0)), pl.BlockSpec((1, k), lambda i: (0, 0))),
        compiler_params=_params("arbitrary"), name=name)(dz, w_t, x, gain, dres)


FFN_ROWS = 256


def _ffn_row_tile(m):
    return FFN_ROWS if m % FFN_ROWS == 0 else m


def ffn_fwd(h2, wg_t, wu_t, wd, x1, gain=None, target=None, name="ffn_fwd"):
    n, d = h2.shape
    f = wg_t.shape[0]
    tm = _ffn_row_tile(n)
    nt = n // tm
    last = target is not None

    def body(h_ref, wg_ref, wu_ref, wd_ref, x1_ref, e_ref, g_ref, u_ref, a_ref, *rest):
        hv = h_ref[...]
        gv = _dot(hv, wg_ref[...], 1, 1)
        uv = _dot(hv, wu_ref[...], 1, 1)
        g_ref[...] = gv.astype(BF16)
        u_ref[...] = uv.astype(BF16)
        av = (gv * _sigmoid(gv) * uv).astype(BF16)
        a_ref[...] = av
        xv = x1_ref[...] + _dot(av, wd_ref[...], 1, 0)
        if not last:
            x_ref, hn_ref = rest
            x_ref[...] = xv
            r = lax.rsqrt(jnp.mean(xv * xv, axis=-1, keepdims=True) + NORM_EPS)
            hn_ref[...] = (xv * r * e_ref[...]).astype(BF16)
        else:
            dyb_ref, l_ref, acc_ref = rest
            i = pl.program_id(0)

            @pl.when(i == 0)
            def _():
                acc_ref[...] = jnp.zeros_like(acc_ref)

            err = xv - e_ref[...]
            dyb_ref[...] = (err * (1.0 / d)).astype(BF16)
            acc_ref[...] += jnp.sum(err * err, axis=0, keepdims=True)

            @pl.when(i == nt - 1)
            def _():
                total = jnp.sum(acc_ref[...], axis=-1, keepdims=True) * (0.5 / d)
                l_ref[...] = jnp.broadcast_to(total, l_ref.shape)

    row_d = pl.BlockSpec((tm, d), lambda i: (i, 0))
    row_f = pl.BlockSpec((tm, f), lambda i: (i, 0))
    act_shape = jax.ShapeDtypeStruct((n, f), BF16)
    if not last:
        extra_in, extra = _resident((1, d)), gain
        out_shape = (act_shape, act_shape, act_shape, jax.ShapeDtypeStruct((n, d), F32), jax.ShapeDtypeStruct((n, d), BF16))
        out_specs = (row_f, row_f, row_f, row_d, row_d)
        scratch = []
    else:
        extra_in, extra = row_d, target
        out_shape = (act_shape, act_shape, act_shape, jax.ShapeDtypeStruct((n, d), BF16),
                     jax.ShapeDtypeStruct((8, LANES), F32))
        out_specs = (row_f, row_f, row_f, row_d, pl.BlockSpec((8, LANES), lambda i: (0, 0)))
        scratch = [pltpu.VMEM((1, d), F32)]
    return pl.pallas_call(
        body, out_shape=out_shape, grid=(nt,),
        in_specs=[row_d, _resident((f, d)), _resident((f, d)), _resident((f, d)), row_d, extra_in],
        out_specs=out_specs, scratch_shapes=scratch,
        compiler_params=_params("arbitrary"), name=name)(h2, wg_t, wu_t, wd, x1, extra)


def ffn_bwd(dx_b, wd, gate, up, wg_t, wu_t, x1, gain, wo, name="ffn_bwd"):
    n, d = x1.shape
    f = wd.shape[0]
    tm = _ffn_row_tile(n)

    def body(dxb_ref, wd_ref, g_ref, u_ref, wg_ref, wu_ref, x_ref, gain_ref, wo_ref,
             dg_ref, du_ref, dxo_ref, dc_ref, dgain_ref):
        @pl.when(pl.program_id(0) == 0)
        def _():
            dgain_ref[...] = jnp.zeros_like(dgain_ref)

        dact = _dot(dxb_ref[...], wd_ref[...], 1, 1)
        gv = g_ref[...].astype(F32)
        uv = u_ref[...].astype(F32)
        sg = _sigmoid(gv)
        dgv = (dact * uv * sg * (1.0 + gv * (1.0 - sg))).astype(BF16)
        duv = (dact * gv * sg).astype(BF16)
        dg_ref[...] = dgv
        du_ref[...] = duv
        dhv = _dot(dgv, wg_ref[...], 1, 0) + _dot(duv, wu_ref[...], 1, 0)
        xv = x_ref[...]
        r = lax.rsqrt(jnp.mean(xv * xv, axis=-1, keepdims=True) + NORM_EPS)
        xhat = xv * r
        dgain_ref[...] += jnp.sum(dhv * xhat, axis=0, keepdims=True)
        dxhat = dhv * gain_ref[...]
        dxb = (dxb_ref[...].astype(F32) + r * (dxhat - xhat * jnp.mean(dxhat * xhat, axis=-1, keepdims=True))).astype(BF16)
        dxo_ref[...] = dxb
        dc_ref[...] = _dot(dxb, wo_ref[...], 1, 1).astype(BF16)

    row_d = pl.BlockSpec((tm, d), lambda i: (i, 0))
    row_f = pl.BlockSpec((tm, f), lambda i: (i, 0))
    w_spec = _resident((f, d))
    act_shape = jax.ShapeDtypeStruct((n, f), BF16)
    row_shape = jax.ShapeDtypeStruct((n, d), BF16)
    return pl.pallas_call(
        body, out_shape=(act_shape, act_shape, row_shape, jax.ShapeDtypeStruct((n, wo.shape[0]), BF16),
                         jax.ShapeDtypeStruct((1, d), F32)),
        grid=(n // tm,),
        in_specs=[row_d, w_spec, row_f, row_f, w_spec, w_spec, row_d, _resident((1, d)), _resident(wo.shape)],
        out_specs=(row_f, row_f, row_d, pl.BlockSpec((tm, wo.shape[0]), lambda i: (i, 0)),
                   pl.BlockSpec((1, d), lambda i: (0, 0))),
        compiler_params=_params("arbitrary"), name=name)(dx_b, wd, gate, up, wg_t, wu_t, x1, gain, wo)


def ffn_weight_grads(dgate, dup, h2, act, dx_b, name="ffn_weight_grads"):
    t, r = dgate.shape
    c = h2.shape[1]
    tr = _col_tile(r, 512)

    def body(a1_ref, a2_ref, a3_ref, b12_ref, b3_ref, o1_ref, o2_ref, o3_ref):
        bv = b12_ref[...]
        o1_ref[...] = _dot(a1_ref[...], bv, 0, 0).astype(o1_ref.dtype)
        o2_ref[...] = _dot(a2_ref[...], bv, 0, 0).astype(o2_ref.dtype)
        o3_ref[...] = _dot(a3_ref[...], b3_ref[...], 0, 0).astype(o3_ref.dtype)

    a_spec = pl.BlockSpec((t, tr), lambda i: (0, i))
    o_spec = pl.BlockSpec((tr, c), lambda i: (i, 0))
    shape = jax.ShapeDtypeStruct((r, c), BF16)
    return pl.pallas_call(
        body, out_shape=(shape, shape, shape), grid=(r // tr,),
        in_specs=[a_spec, a_spec, a_spec, _resident((t, c)), _resident((t, c))],
        out_specs=(o_spec, o_spec, o_spec), compiler_params=_params("parallel"), name=name)(dgate, dup, act, h2, dx_b)


def rms_fwd(x, g, name="rms_fwd"):
    n, d = x.shape
    tm = _row_tile(n)

    def body(x_ref, g_ref, o_ref):
        xv = x_ref[...]
        r = lax.rsqrt(jnp.mean(xv * xv, axis=-1, keepdims=True) + NORM_EPS)
        o_ref[...] = (xv * r * g_ref[...]).astype(o_ref.dtype)

    return pl.pallas_call(
        body, out_shape=jax.ShapeDtypeStruct((n, d), BF16), grid=(n // tm,),
        in_specs=[pl.BlockSpec((tm, d), lambda i: (i, 0)), pl.BlockSpec((1, d), lambda i: (0, 0))],
        out_specs=pl.BlockSpec((tm, d), lambda i: (i, 0)),
        compiler_params=_params("parallel"), name=name)(x, g)


def _group_masks(width):
    lane = lax.broadcasted_iota(jnp.int32, (1, width), 1)
    return [(lane >= HEAD_DIM * g) & (lane < HEAD_DIM * (g + 1)) for g in range(width // HEAD_DIM)]


def _group_sum(x, masks):
    out = jnp.zeros_like(x)
    for msk in masks:
        s = jnp.sum(jnp.where(msk, x, 0.0), axis=-1, keepdims=True)
        out = jnp.where(msk, s, out)
    return out


def _head_norm(x, gain, masks):
    r = lax.rsqrt(_group_sum(x * x, masks) * (1.0 / HEAD_DIM) + NORM_EPS)
    xhat = x * r
    return xhat * gain, xhat, r


def _head_norm_bwd(dxn, xhat, r, gain, masks):
    dgain = jnp.sum(dxn * xhat, axis=0, keepdims=True)
    dxhat = dxn * gain
    mean_t = _group_sum(dxhat * xhat, masks) * (1.0 / HEAD_DIM)
    return r * (dxhat - xhat * mean_t), dgain


def _softmax_rows(s):
    e = jnp.exp(s - jnp.max(s, axis=-1, keepdims=True))
    return e * (1.0 / jnp.sum(e, axis=-1, keepdims=True))


def _rel_onehot():
    col = lax.broadcasted_iota(jnp.int32, (1, KEY_WIN), 1)
    off = jnp.where(col < KEY_WIN - LANES, col, col - KEY_WIN)
    idx = jnp.clip(8 * CHUNK - off, -(CHUNK - 1), LANES) + (CHUNK - 1)
    return (lax.broadcasted_iota(jnp.int32, (N_REL, KEY_WIN), 0) == idx).astype(F32)


def bias_blocks(rel16):
    heads = TOK_WIDTH // HEAD_DIM

    def body(rel_ref, o_ref, u_ref):
        u_ref[...] = jnp.dot(rel_ref[...], _rel_onehot(), precision=HIGHEST, preferred_element_type=F32)
        row = lax.broadcasted_iota(jnp.int32, (CHUNK, KEY_WIN), 0)
        col = lax.broadcasted_iota(jnp.int32, (CHUNK, KEY_WIN), 1)
        for h in range(heads):
            xv = jnp.broadcast_to(u_ref[h:h + 1, :], (CHUNK, KEY_WIN))
            for b in range(6):
                xv = jnp.where(((row >> b) & 1) == 1, pltpu.roll(xv, 1 << b, axis=1), xv)
            xv = jnp.where(col < BAND, xv, NEG_INF)
            for i in range(Q_BLOCK // CHUNK):
                o_ref[h, CHUNK * i:CHUNK * (i + 1), :] = pltpu.roll(xv, CHUNK * i, axis=1) if i else xv

    return pl.pallas_call(
        body, out_shape=jax.ShapeDtypeStruct((heads, Q_BLOCK, KEY_WIN), F32),
        scratch_shapes=[pltpu.VMEM((16, KEY_WIN), F32)], name="bias_blocks")(rel16)


def bias_grad(dbias):
    heads = dbias.shape[0]

    def body(db_ref, o_ref, y_ref):
        y_ref[...] = jnp.zeros_like(y_ref)
        row = lax.broadcasted_iota(jnp.int32, (CHUNK, KEY_WIN), 0)
        for h in range(heads):
            fv = db_ref[h, 0:CHUNK, :]
            for i in range(1, Q_BLOCK // CHUNK):
                fv = fv + pltpu.roll(db_ref[h, CHUNK * i:CHUNK * (i + 1), :], KEY_WIN - CHUNK * i, axis=1)
            for b in range(6):
                fv = jnp.where(((row >> b) & 1) == 1, pltpu.roll(fv, KEY_WIN - (1 << b), axis=1), fv)
            y_ref[h:h + 1, :] = jnp.sum(fv, axis=0, keepdims=True)
        o_ref[...] = lax.dot_general(y_ref[...], _rel_onehot(), (((1,), (1,)), ((), ())),
                                     precision=HIGHEST, preferred_element_type=F32)

    return pl.pallas_call(
        body, out_shape=jax.ShapeDtypeStruct((16, N_REL), F32),
        scratch_shapes=[pltpu.VMEM((16, KEY_WIN), F32)], name="bias_grad")(dbias)


def _attn_windows(seq):
    out = []
    for j in range(seq // Q_BLOCK):
        r0 = j * Q_BLOCK
        k0 = max(0, r0 - 8 * CHUNK)
        width = r0 + Q_BLOCK - k0
        out.append((r0, k0, width, KEY_WIN - width))
    return out


def attn_fwd(z, gq2, gk2, bias, batch, seq):
    n = z.shape[0]
    pairs = TOK_WIDTH // LANES

    def body(q_ref, k_ref, v_ref, gq_ref, gk_ref, b_ref, o_ref, qs_s, kn_s):
        masks = _group_masks(LANES)
        qs_s[...] = (_head_norm(q_ref[...].astype(F32), gq_ref[...], masks)[0] * ATTN_SCALE).astype(BF16)
        kn_s[...] = _head_norm(k_ref[...].astype(F32), gk_ref[...], masks)[0].astype(BF16)
        for r0, k0, width, c0 in _attn_windows(seq):
            qb = qs_s[r0:r0 + Q_BLOCK, :]
            kw = kn_s[k0:k0 + width, :]
            vw = v_ref[k0:k0 + width, :]
            out = jnp.zeros((Q_BLOCK, LANES), F32)
            for h, msk in enumerate(masks):
                qh = jnp.where(msk, qb, jnp.zeros_like(qb))
                s = _dot(qh, kw, 1, 1) + b_ref[h, :, c0:KEY_WIN]
                p = _softmax_rows(s).astype(BF16)
                out = jnp.where(msk, _dot(p, vw, 1, 0), out)
            o_ref[r0:r0 + Q_BLOCK, :] = out.astype(o_ref.dtype)

    def col(off):
        return pl.BlockSpec((seq, LANES), lambda b, p: (b, off + p))

    vec = pl.BlockSpec((1, LANES), lambda b, p: (0, 0))
    return pl.pallas_call(
        body, out_shape=jax.ShapeDtypeStruct((n, D_MODEL), BF16), grid=(batch, pairs),
        in_specs=[col(0), col(pairs), col(2 * pairs), vec, vec,
                  pl.BlockSpec((2, Q_BLOCK, KEY_WIN), lambda b, p: (p, 0, 0))],
        out_specs=pl.BlockSpec((seq, LANES), lambda b, p: (b, p)),
        scratch_shapes=[pltpu.VMEM((seq, LANES), BF16), pltpu.VMEM((seq, LANES), BF16)],
        compiler_params=_params("parallel", "arbitrary"), name="attn_fwd")(z, z, z, gq2, gk2, bias)


def attn_bwd(z, dcat, gq2, gk2, bias, batch, seq):
    n = z.shape[0]
    pairs = TOK_WIDTH // LANES

    def body(q_ref, k_ref, v_ref, do_ref, gq_ref, gk_ref, b_ref,
             dz_ref, db_ref, dgq_ref, dgk_ref, qs_s, kn_s, dqn_s, dkn_s, dv_s, dk_o, dv_o):
        pi, bi, which = pl.program_id(0), pl.program_id(1), pl.program_id(2)

        @pl.when(which == 0)
        def _():
            masks = _group_masks(LANES)

            @pl.when(bi == 0)
            def _():
                db_ref[...] = jnp.zeros_like(db_ref)

            @pl.when((bi == 0) & (pi == 0))
            def _():
                dgq_ref[...] = jnp.zeros_like(dgq_ref)
                dgk_ref[...] = jnp.zeros_like(dgk_ref)

            qn, qhat, rq = _head_norm(q_ref[...].astype(F32), gq_ref[...], masks)
            kn, khat, rk = _head_norm(k_ref[...].astype(F32), gk_ref[...], masks)
            qs_s[...] = (qn * ATTN_SCALE).astype(BF16)
            kn_s[...] = kn.astype(BF16)
            dkn_s[...] = jnp.zeros_like(dkn_s)
            dv_s[...] = jnp.zeros_like(dv_s)
            for r0, k0, width, c0 in _attn_windows(seq):
                qb = qs_s[r0:r0 + Q_BLOCK, :]
                dob = do_ref[r0:r0 + Q_BLOCK, :]
                kw = kn_s[k0:k0 + width, :]
                vw = v_ref[k0:k0 + width, :]
                dq_acc = jnp.zeros((Q_BLOCK, LANES), F32)
                dk_acc = jnp.zeros((width, LANES), F32)
                dv_acc = jnp.zeros((width, LANES), F32)
                for h, msk in enumerate(masks):
                    qh = jnp.where(msk, qb, jnp.zeros_like(qb))
                    doh = jnp.where(msk, dob, jnp.zeros_like(dob))
                    p = _softmax_rows(_dot(qh, kw, 1, 1) + b_ref[h, :, c0:KEY_WIN])
                    dp = _dot(doh, vw, 1, 1)
                    ds = p * (dp - jnp.sum(p * dp, axis=-1, keepdims=True))
                    db_ref[h, :, c0:KEY_WIN] += ds
                    dsb = ds.astype(BF16)
                    dq_acc = jnp.where(msk, _dot(dsb, kw, 1, 0), dq_acc)
                    dk_acc = jnp.where(msk, _dot(dsb, qb, 0, 0), dk_acc)
                    dv_acc = jnp.where(msk, _dot(p.astype(BF16), dob, 0, 0), dv_acc)
                dqn_s[r0:r0 + Q_BLOCK, :] = dq_acc * ATTN_SCALE
                dkn_s[k0:k0 + width, :] += dk_acc
                dv_s[k0:k0 + width, :] += dv_acc
            dq, dgq = _head_norm_bwd(dqn_s[...], qhat, rq, gq_ref[...], masks)
            dk, dgk = _head_norm_bwd(dkn_s[...], khat, rk, gk_ref[...], masks)
            dz_ref[...] = dq.astype(dz_ref.dtype)
            dk_o[...] = dk.astype(dk_o.dtype)
            dv_o[...] = dv_s[...].astype(dv_o.dtype)
            dgq_ref[...] += dgq
            dgk_ref[...] += dgk

        @pl.when(which == 1)
        def _():
            dz_ref[...] = dk_o[...]

        @pl.when(which == 2)
        def _():
            dz_ref[...] = dv_o[...]

    def ahead(p, b, t):
        nb = b + jnp.where(t > 0, 1, 0)
        wrap = jnp.where(nb >= batch, 1, 0)
        return jnp.minimum(p + wrap, pairs - 1), nb - wrap * batch

    def col(off):
        def index(p, b, t):
            np_, nb = ahead(p, b, t)
            return nb, off + np_
        return pl.BlockSpec((seq, LANES), index)

    vec = pl.BlockSpec((1, LANES), lambda p, b, t: (0, 0))
    blk = pl.BlockSpec((2, Q_BLOCK, KEY_WIN), lambda p, b, t: (p, 0, 0))
    blk_in = pl.BlockSpec((2, Q_BLOCK, KEY_WIN), lambda p, b, t: (ahead(p, b, t)[0], 0, 0))
    v_shape = jax.ShapeDtypeStruct((1, LANES), F32)
    return pl.pallas_call(
        body,
        out_shape=(jax.ShapeDtypeStruct(z.shape, BF16), jax.ShapeDtypeStruct(bias.shape, F32), v_shape, v_shape),
        grid=(pairs, batch, 3),
        in_specs=[col(0), col(pairs), col(2 * pairs), col(0), vec, vec, blk_in],
        out_specs=(pl.BlockSpec((seq, LANES), lambda p, b, t: (b, t * pairs + p)), blk, vec, vec),
        scratch_shapes=[pltpu.VMEM((seq, LANES), BF16), pltpu.VMEM((seq, LANES), BF16),
                        pltpu.VMEM((seq, LANES), F32), pltpu.VMEM((seq, LANES), F32), pltpu.VMEM((seq, LANES), F32),
                        pltpu.VMEM((seq, LANES), BF16), pltpu.VMEM((seq, LANES), BF16)],
        compiler_params=_params("arbitrary", "arbitrary", "arbitrary"), name="attn_bwd")(
            z, z, z, dcat, gq2, gk2, bias)


MEM_ROWS = 512


def memattn_fwd(z, mem, mem_gain, wkv, gq4, gk4, cat, batch, seq, qcol, name):
    mtok = mem.shape[0] // batch
    d = mem.shape[1]
    rows = min(MEM_ROWS, seq)

    def body(q_ref, m_ref, mg_ref, w_ref, gq_ref, gk_ref, cat_ref, o_ref, n_ref, kv_ref):
        del cat_ref
        masks = _group_masks(MEM_WIDTH)
        mv = m_ref[...]
        r = lax.rsqrt(jnp.mean(mv * mv, axis=-1, keepdims=True) + NORM_EPS)
        nv = (mv * r * mg_ref[...]).astype(BF16)
        n_ref[...] = nv
        kv_ref[...] = _dot(nv, w_ref[...], 1, 0)
        kn = _head_norm(kv_ref[:, 0:MEM_WIDTH], gk_ref[...], masks)[0].astype(BF16)
        vm = kv_ref[:, MEM_WIDTH:2 * MEM_WIDTH].astype(BF16)
        for t in range(seq // rows):
            sl = slice(t * rows, (t + 1) * rows)
            qs = (_head_norm(q_ref[sl, :].astype(F32), gq_ref[...], masks)[0] * ATTN_SCALE).astype(BF16)
            out = jnp.zeros((rows, MEM_WIDTH), F32)
            for msk in masks:
                qh = jnp.where(msk, qs, jnp.zeros_like(qs))
                p = _softmax_rows(_dot(qh, kn, 1, 1)).astype(BF16)
                out = jnp.where(msk, _dot(p, vm, 1, 0), out)
            o_ref[sl, :] = out.astype(o_ref.dtype)

    vec = pl.BlockSpec((1, MEM_WIDTH), lambda b: (0, 0))
    mem_spec = pl.BlockSpec((mtok, d), lambda b: (b, 0))
    kv_spec = pl.BlockSpec((mtok, 2 * MEM_WIDTH), lambda b: (b, 0))
    return pl.pallas_call(
        body, out_shape=(jax.ShapeDtypeStruct(cat.shape, cat.dtype), jax.ShapeDtypeStruct(mem.shape, BF16),
                         jax.ShapeDtypeStruct((mem.shape[0], 2 * MEM_WIDTH), F32)), grid=(batch,),
        in_specs=[pl.BlockSpec((seq, MEM_WIDTH), lambda b: (b, qcol)), mem_spec, pl.BlockSpec((1, d), lambda b: (0, 0)),
                  pl.BlockSpec(wkv.shape, lambda b: (0, 0)), vec, vec, ANY],
        out_specs=(pl.BlockSpec((seq, MEM_WIDTH), lambda b: (b, TOK_WIDTH // MEM_WIDTH)), mem_spec, kv_spec),
        input_output_aliases={6: 0},
        compiler_params=_params("parallel"), name=name)(z, mem, mem_gain, wkv, gq4, gk4, cat)


def memattn_bwd(z, kv, dcat, gq4, gk4, mem, mem_n, wkv, dz, batch, seq, qcol, name):
    mtok = kv.shape[0] // batch
    d = mem.shape[1]
    rows = min(MEM_ROWS, seq)

    def body(q_ref, kv_ref, do_ref, gq_ref, gk_ref, m_ref, n_ref, w_ref, dz_in_ref,
             dq_ref, dgq_ref, dgk_ref, dw_ref, dmg_ref, dw_acc):
        del dz_in_ref
        @pl.when(pl.program_id(0) == 0)
        def _():
            dgq_ref[...] = jnp.zeros_like(dgq_ref)
            dgk_ref[...] = jnp.zeros_like(dgk_ref)
            dmg_ref[...] = jnp.zeros_like(dmg_ref)
            dw_acc[...] = jnp.zeros_like(dw_acc)

        masks = _group_masks(MEM_WIDTH)
        kn_f, khat, rk = _head_norm(kv_ref[:, 0:MEM_WIDTH], gk_ref[...], masks)
        kn = kn_f.astype(BF16)
        vm = kv_ref[:, MEM_WIDTH:2 * MEM_WIDTH].astype(BF16)
        dkn = jnp.zeros((mtok, MEM_WIDTH), F32)
        dvm = jnp.zeros((mtok, MEM_WIDTH), F32)
        dgq = jnp.zeros((1, MEM_WIDTH), F32)
        for t in range(seq // rows):
            sl = slice(t * rows, (t + 1) * rows)
            qn_f, qhat, rq = _head_norm(q_ref[sl, :].astype(F32), gq_ref[...], masks)
            qs = (qn_f * ATTN_SCALE).astype(BF16)
            dob = do_ref[sl, :]
            dqn = jnp.zeros((rows, MEM_WIDTH), F32)
            for msk in masks:
                qh = jnp.where(msk, qs, jnp.zeros_like(qs))
                doh = jnp.where(msk, dob, jnp.zeros_like(dob))
                p = _softmax_rows(_dot(qh, kn, 1, 1))
                dp = _dot(doh, vm, 1, 1)
                ds = p * (dp - jnp.sum(p * dp, axis=-1, keepdims=True))
                dsb = ds.astype(BF16)
                dqn = jnp.where(msk, _dot(dsb, kn, 1, 0), dqn)
                dkn = dkn + jnp.where(msk, _dot(dsb, qs, 0, 0), 0.0)
                dvm = dvm + jnp.where(msk, _dot(p.astype(BF16), dob, 0, 0), 0.0)
            dq, dg = _head_norm_bwd(dqn * ATTN_SCALE, qhat, rq, gq_ref[...], masks)
            dq_ref[sl, :] = dq.astype(dq_ref.dtype)
            dgq = dgq + dg
        dk, dgk = _head_norm_bwd(dkn, khat, rk, gk_ref[...], masks)
        dgq_ref[...] += dgq
        dgk_ref[...] += dgk
        dkv_b = jnp.concatenate([dk, dvm], axis=-1).astype(BF16)
        dw_acc[...] += _dot(n_ref[...], dkv_b, 0, 0)
        dn = _dot(dkv_b, w_ref[...], 1, 1)
        mv = m_ref[...]
        rm = lax.rsqrt(jnp.mean(mv * mv, axis=-1, keepdims=True) + NORM_EPS)
        dmg_ref[...] += jnp.sum(dn * (mv * rm), axis=0, keepdims=True)

        @pl.when(pl.program_id(0) == batch - 1)
        def _():
            dw_ref[...] = dw_acc[...].astype(dw_ref.dtype)

    vec = pl.BlockSpec((1, MEM_WIDTH), lambda b: (0, 0))
    kv_spec = pl.BlockSpec((mtok, 2 * MEM_WIDTH), lambda b: (b, 0))
    mem_spec = pl.BlockSpec((mtok, d), lambda b: (b, 0))
    w_spec = pl.BlockSpec(wkv.shape, lambda b: (0, 0))
    v_shape = jax.ShapeDtypeStruct((1, MEM_WIDTH), F32)
    q_spec = pl.BlockSpec((seq, MEM_WIDTH), lambda b: (b, qcol))
    return pl.pallas_call(
        body,
        out_shape=(jax.ShapeDtypeStruct(dz.shape, dz.dtype), v_shape, v_shape, jax.ShapeDtypeStruct(wkv.shape, BF16),
                   jax.ShapeDtypeStruct((1, d), F32)),
        grid=(batch,),
        in_specs=[q_spec, kv_spec, pl.BlockSpec((seq, MEM_WIDTH), lambda b: (b, TOK_WIDTH // MEM_WIDTH)), vec, vec,
                  mem_spec, mem_spec, w_spec, ANY],
        out_specs=(q_spec, vec, vec, w_spec, pl.BlockSpec((1, d), lambda b: (0, 0))),
        scratch_shapes=[pltpu.VMEM(wkv.shape, F32)],
        input_output_aliases={8: 0},
        compiler_params=_params("arbitrary"), name=name)(z, kv, dcat, gq4, gk4, mem, mem_n, wkv, dz)


CONV_ROWS = 256


def _glu(a_ref, g_ref):
    return a_ref[...].astype(F32) * _sigmoid(g_ref[...].astype(F32))


def _layer_norm_stats(y):
    mu = jnp.mean(y, axis=-1, keepdims=True)
    yc = y - mu
    rstd = lax.rsqrt(jnp.mean(yc * yc, axis=-1, keepdims=True) + NORM_EPS)
    return yc * rstd, rstd


CONV_WIN = CONV_HALO + CONV_ROWS
SUBLANES = 8
SHIFT_ROWS = CONV_WIN - SUBLANES


def _preshift(win, shifted):
    for s in range(1, SUBLANES):
        shifted[s - 1, :, :] = win[s:s + SHIFT_ROWS, :]


TAP_ROWS = 64
TAP_TILES = [(r0, slice(c0, c0 + LANES)) for c0 in range(0, TOK_WIDTH, LANES) for r0 in range(0, CONV_ROWS, TAP_ROWS)]


def _tap(win, shifted, off, r0, lanes):
    s = off % SUBLANES
    base = off - s + r0
    if s == 0:
        return win[base:base + TAP_ROWS, lanes]
    return shifted[s - 1, base:base + TAP_ROWS, lanes]


def _fold_rows(x):
    return jnp.sum(x.reshape(TAP_ROWS // SUBLANES, SUBLANES, LANES), axis=0)


def conv_fwd(z, cw, cb, lg, lb, batch, seq):
    n = z.shape[0]
    nt = seq // CONV_ROWS
    sub = CONV_ROWS // CONV_HALO
    lead = CONV_HALO - (CONV_W - 1)

    def body(a_ref, g_ref, ap_ref, gp_ref, cw_ref, cb_ref, lg_ref, lb_ref, o_ref, y_ref, win, shifted):
        first = pl.program_id(1) == 0
        win[0:CONV_HALO, :] = jnp.where(first, 0.0, _glu(ap_ref, gp_ref))
        win[CONV_HALO:CONV_WIN, :] = _glu(a_ref, g_ref)
        _preshift(win, shifted)
        for r0, lanes in TAP_TILES:
            acc = jnp.zeros((TAP_ROWS, LANES), F32) + cb_ref[:, lanes]
            for w in range(CONV_W):
                acc = acc + _tap(win, shifted, lead + w, r0, lanes) * cw_ref[w:w + 1, lanes]
            y_ref[r0:r0 + TAP_ROWS, lanes] = acc
        yh, _ = _layer_norm_stats(y_ref[...])
        t = yh * lg_ref[...] + lb_ref[...]
        o_ref[...] = (t * _sigmoid(t)).astype(o_ref.dtype)

    def cur(c):
        return pl.BlockSpec((CONV_ROWS, TOK_WIDTH), lambda b, i: (b * nt + i, c))

    def prev(c):
        return pl.BlockSpec((CONV_HALO, TOK_WIDTH), lambda b, i: (jnp.maximum((b * nt + i) * sub - 1, 0), c))

    vec = pl.BlockSpec((1, TOK_WIDTH), lambda b, i: (0, 0))
    return pl.pallas_call(
        body, out_shape=(jax.ShapeDtypeStruct((n, D_MODEL), BF16), jax.ShapeDtypeStruct((n, TOK_WIDTH), F32)),
        grid=(batch, nt),
        in_specs=[cur(0), cur(1), prev(0), prev(1), pl.BlockSpec((32, TOK_WIDTH), lambda b, i: (0, 0)), vec, vec, vec],
        out_specs=(cur(0), cur(0)),
        scratch_shapes=[pltpu.VMEM((CONV_WIN, TOK_WIDTH), F32), pltpu.VMEM((SUBLANES - 1, SHIFT_ROWS, TOK_WIDTH), F32)],
        compiler_params=_params("parallel", "arbitrary"), name="conv_fwd")(z, z, z, z, cw, cb, lg, lb)


def conv_bwd(z, y, dcat, cw, lg, lb, batch, seq):
    n = z.shape[0]
    nt = seq // CONV_ROWS
    sub = CONV_ROWS // CONV_HALO
    lead = CONV_HALO - (CONV_W - 1)
    last_blk = n // CONV_HALO - 1

    def body(a_ref, g_ref, ap_ref, gp_ref, y_ref, yn_ref, do_ref, don_ref, cw_ref, lg_ref, lb_ref,
             dz_ref, dcw_ref, dsm_ref, win, shifted, dyw, dshifted, dg_o):
        b, i, which = pl.program_id(0), pl.program_id(1), pl.program_id(2)

        @pl.when(which == 0)
        def _():
            first, last = i == 0, i == nt - 1

            @pl.when((b == 0) & (i == 0))
            def _():
                dcw_ref[...] = jnp.zeros_like(dcw_ref)
                dsm_ref[...] = jnp.zeros_like(dsm_ref)

            win[0:CONV_HALO, :] = jnp.where(first, 0.0, _glu(ap_ref, gp_ref))
            win[CONV_HALO:CONV_WIN, :] = _glu(a_ref, g_ref)
            _preshift(win, shifted)
            yv = jnp.concatenate([y_ref[...], yn_ref[...]], axis=0)
            yh, rstd = _layer_norm_stats(yv)
            t = yh * lg_ref[...] + lb_ref[...]
            st = _sigmoid(t)
            dout = jnp.concatenate(
                [do_ref[...].astype(F32), jnp.where(last, 0.0, don_ref[...].astype(F32))], axis=0)
            dt = dout * st * (1.0 + t * (1.0 - st))
            dyh = dt * lg_ref[...]
            dy = rstd * (dyh - jnp.mean(dyh, axis=-1, keepdims=True)
                         - yh * jnp.mean(dyh * yh, axis=-1, keepdims=True))
            dyw[...] = dy
            _preshift(dyw, dshifted)
            dsm_ref[0:1, :] += jnp.sum(dy[0:CONV_ROWS], axis=0, keepdims=True)
            dsm_ref[1:2, :] += jnp.sum((dt * yh)[0:CONV_ROWS], axis=0, keepdims=True)
            dsm_ref[2:3, :] += jnp.sum(dt[0:CONV_ROWS], axis=0, keepdims=True)
            for c0 in range(0, TOK_WIDTH, LANES):
                lanes = slice(c0, c0 + LANES)
                dcw_acc = [jnp.zeros((SUBLANES, LANES), F32) for _ in range(CONV_W)]
                for r0 in range(0, CONV_ROWS, TAP_ROWS):
                    dyt = dyw[r0:r0 + TAP_ROWS, lanes]
                    dglu = jnp.zeros((TAP_ROWS, LANES), F32)
                    for w in range(CONV_W):
                        dcw_acc[w] = dcw_acc[w] + _fold_rows(dyt * _tap(win, shifted, lead + w, r0, lanes))
                        dglu = dglu + _tap(dyw, dshifted, CONV_W - 1 - w, r0, lanes) * cw_ref[w:w + 1, lanes]
                    avt = a_ref[r0:r0 + TAP_ROWS, lanes].astype(F32)
                    sgt = _sigmoid(g_ref[r0:r0 + TAP_ROWS, lanes].astype(F32))
                    dz_ref[r0:r0 + TAP_ROWS, lanes] = (dglu * sgt).astype(dz_ref.dtype)
                    dg_o[r0:r0 + TAP_ROWS, lanes] = (dglu * avt * sgt * (1.0 - sgt)).astype(dg_o.dtype)
                for w in range(CONV_W):
                    dcw_ref[w:w + 1, lanes] += jnp.sum(dcw_acc[w], axis=0, keepdims=True)

        @pl.when(which == 1)
        def _():
            dz_ref[...] = dg_o[...]

    def ahead(b, i, t):
        return jnp.minimum(b * nt + i + t, batch * nt - 1)

    def cur(c):
        return pl.BlockSpec((CONV_ROWS, TOK_WIDTH), lambda b, i, t: (ahead(b, i, t), c))

    def prev(c):
        return pl.BlockSpec((CONV_HALO, TOK_WIDTH), lambda b, i, t: (jnp.maximum(ahead(b, i, t) * sub - 1, 0), c))

    nxt = pl.BlockSpec((CONV_HALO, TOK_WIDTH),
                       lambda b, i, t: (jnp.minimum((ahead(b, i, t) + 1) * sub, last_blk), 0))
    vec = pl.BlockSpec((1, TOK_WIDTH), lambda b, i, t: (0, 0))
    full32 = pl.BlockSpec((32, TOK_WIDTH), lambda b, i, t: (0, 0))
    return pl.pallas_call(
        body,
        out_shape=(jax.ShapeDtypeStruct(z.shape, BF16), jax.ShapeDtypeStruct((32, TOK_WIDTH), F32),
                   jax.ShapeDtypeStruct((8, TOK_WIDTH), F32)),
        grid=(batch, nt, 2),
        in_specs=[cur(0), cur(1), prev(0), prev(1), cur(0), nxt, cur(0), nxt, full32, vec, vec],
        out_specs=(pl.BlockSpec((CONV_ROWS, TOK_WIDTH), lambda b, i, t: (b * nt + i, t)), full32,
                   pl.BlockSpec((8, TOK_WIDTH), lambda b, i, t: (0, 0))),
        scratch_shapes=[pltpu.VMEM((CONV_WIN, TOK_WIDTH), F32), pltpu.VMEM((SUBLANES - 1, SHIFT_ROWS, TOK_WIDTH), F32),
                        pltpu.VMEM((CONV_WIN, TOK_WIDTH), F32), pltpu.VMEM((SUBLANES - 1, SHIFT_ROWS, TOK_WIDTH), F32),
                        pltpu.VMEM((CONV_ROWS, TOK_WIDTH), BF16)],
        compiler_params=_params("arbitrary", "arbitrary", "arbitrary"), name="conv_bwd")(
            z, z, z, z, y, y, dcat, dcat, cw, lg, lb)


def _place():
    return lax.axis_index("x"), lax.axis_index("y"), lax.axis_index("c")


def _other_chips(x, y):
    return [(1 - x, y), (x, 1 - y), (1 - x, 1 - y)]


def reduce_small(arrays):
    na = len(arrays)

    def body(*refs):
        ins, outs, bufs = refs[:na], refs[na:2 * na], refs[2 * na:3 * na]
        send_sems, recv_sems = refs[3 * na:]
        x, y, c = _place()
        me = 4 * x + 2 * y + c
        copies = []
        for a in range(na):
            bufs[a][me] = ins[a][...]
            for k in range(1, N_DEV):
                cp = pltpu.make_async_remote_copy(
                    src_ref=ins[a], dst_ref=bufs[a].at[me], send_sem=send_sems.at[a, k - 1],
                    recv_sem=recv_sems.at[a, k - 1],
                    device_id=(x ^ (k >> 2), y ^ ((k >> 1) & 1), c ^ (k & 1)), device_id_type=MESH)
                cp.start()
                copies.append(cp)
        for a in range(na):
            for k in range(1, N_DEV):
                src = 4 * (x ^ (k >> 2)) + 2 * (y ^ ((k >> 1) & 1)) + (c ^ (k & 1))
                pltpu.make_async_remote_copy(
                    src_ref=ins[a], dst_ref=bufs[a].at[src], send_sem=send_sems.at[a, k - 1],
                    recv_sem=recv_sems.at[a, k - 1], device_id=(x, y, c), device_id_type=MESH).wait_recv()
        for cp in copies:
            cp.wait_send()
        for a in range(na):
            total = bufs[a][0]
            for dev in range(1, N_DEV):
                total = total + bufs[a][dev]
            outs[a][...] = total

    vmem = pl.BlockSpec(memory_space=pltpu.VMEM)
    return pl.pallas_call(
        body, out_shape=tuple(jax.ShapeDtypeStruct(a.shape, F32) for a in arrays),
        in_specs=[vmem] * na, out_specs=tuple([vmem] * na),
        scratch_shapes=[pltpu.VMEM((N_DEV,) + a.shape, F32) for a in arrays]
        + [pltpu.SemaphoreType.DMA((na, N_DEV - 1)), pltpu.SemaphoreType.DMA((na, N_DEV - 1))],
        compiler_params=pltpu.CompilerParams(vmem_limit_bytes=VMEM_LIMIT), name="small_reduce")(*arrays)


def adamw_small(ws, gs, ms, vs):
    na = len(ws)
    c1 = 1.0 / (1.0 - ADAM_B1 ** ADAM_STEP)
    c2 = 1.0 / (1.0 - ADAM_B2 ** ADAM_STEP)

    def body(*refs):
        w_refs, g_refs, m_refs, v_refs = (refs[i * na:(i + 1) * na] for i in range(4))
        d_refs, nm_refs, nv_refs = (refs[(4 + i) * na:(5 + i) * na] for i in range(3))
        for a in range(na):
            gv = g_refs[a][...]
            nm = ADAM_B1 * m_refs[a][...] + (1.0 - ADAM_B1) * gv
            nv = ADAM_B2 * v_refs[a][...] + (1.0 - ADAM_B2) * (gv * gv)
            nm_refs[a][...] = nm
            nv_refs[a][...] = nv
            d_refs[a][...] = -ADAM_LR * ((nm * c1) / (jnp.sqrt(nv * c2) + ADAM_EPS) + ADAM_WD * w_refs[a][...])

    vmem = pl.BlockSpec(memory_space=pltpu.VMEM)
    shapes = tuple(jax.ShapeDtypeStruct(w.shape, F32) for w in ws)
    outs = pl.pallas_call(
        body, out_shape=shapes * 3, in_specs=[vmem] * (4 * na), out_specs=tuple([vmem] * (3 * na)),
        compiler_params=pltpu.CompilerParams(vmem_limit_bytes=VMEM_LIMIT), name="adamw_small")(*ws, *gs, *ms, *vs)
    return outs[:na], outs[na:2 * na], outs[2 * na:]


def gather_weights(shards, name, collective_id):
    nw = len(shards)
    ns = [s.shape[0] for s in shards]
    in_refs = [jax.new_ref(s, memory_space=pltpu.MemorySpace.HBM) for s in shards]
    out_refs = [jax.empty_ref(jax.ShapeDtypeStruct((N_DEV * s.shape[0], s.shape[1]), s.dtype),
                              memory_space=pltpu.MemorySpace.HBM) for s in shards]

    @pl.kernel(mesh=plsc.ScalarSubcoreMesh(axis_name="seq", num_cores=1), name=name,
               scratch_types=(pltpu.SemaphoreType.DMA((nw, 7)), pltpu.SemaphoreType.DMA((nw, 7)),
                              pltpu.SemaphoreType.DMA((nw,))),
               compiler_params=pltpu.CompilerParams(collective_id=collective_id))
    def launch(send_sems, recv_sems, local_sems):
        x, y, c = _place()
        me, sib = (x, y, c), (x, y, 1 - c)
        chips = _other_chips(x, y)
        barrier = pltpu.get_barrier_semaphore()
        for peer in [sib] + [(*chip, c) for chip in chips]:
            pl.semaphore_signal(barrier, inc=1, device_id=peer, device_id_type=MESH)
        pl.semaphore_wait(barrier, 4)

        def rows(w, dev):
            return out_refs[w].at[pl.ds((4 * dev[0] + 2 * dev[1] + dev[2]) * ns[w], ns[w]), :]

        def copy(w, k, block, to, src=None):
            return pltpu.make_async_remote_copy(
                src_ref=rows(w, block) if src is None else src, dst_ref=rows(w, block),
                send_sem=send_sems.at[w, k], recv_sem=recv_sems.at[w, k], device_id=to, device_id_type=MESH)

        started, sends = [], []
        for w in range(nw):
            mine = pltpu.make_async_copy(in_refs[w], rows(w, me), local_sems.at[w])
            mine.start()
            started.append(mine)
            first = [copy(w, 0, me, sib, src=in_refs[w])]
            first += [copy(w, 1 + j, me, (*chip, c), src=in_refs[w]) for j, chip in enumerate(chips)]
            for cp in first:
                cp.start()
            sends += first
        for w in range(nw):
            for j, chip in enumerate(chips):
                copy(w, 1 + j, (*chip, c), me).wait_recv()
                fwd = copy(w, 4 + j, (*chip, c), sib)
                fwd.start()
                sends.append(fwd)
        for w in range(nw):
            copy(w, 0, sib, me).wait_recv()
            for j, chip in enumerate(chips):
                copy(w, 4 + j, (*chip, 1 - c), me).wait_recv()
        for cp in sends:
            cp.wait_send()
        for mine in started:
            mine.wait()

    launch()
    return [r[...] for r in out_refs]


def _sequencer_exchange(sources, out_rows, peers_of, copies_of, name, collective_id):
    nw = len(sources)
    in_refs = [jax.new_ref(s, memory_space=pltpu.MemorySpace.HBM) for s in sources]
    out_refs = [jax.empty_ref(jax.ShapeDtypeStruct((rows, s.shape[1]), s.dtype), memory_space=pltpu.MemorySpace.HBM)
                for rows, s in zip(out_rows, sources)]
    per = len(copies_of(0, 0, 0, 0))

    @pl.kernel(mesh=plsc.ScalarSubcoreMesh(axis_name="seq", num_cores=1), name=name,
               scratch_types=(pltpu.SemaphoreType.DMA((nw, per)), pltpu.SemaphoreType.DMA((nw, per))),
               compiler_params=pltpu.CompilerParams(collective_id=collective_id))
    def launch(send_sems, recv_sems):
        x, y, c = _place()
        peers = peers_of(x, y, c)
        barrier = pltpu.get_barrier_semaphore()
        for peer in peers:
            pl.semaphore_signal(barrier, inc=1, device_id=peer, device_id_type=MESH)
        pl.semaphore_wait(barrier, len(peers))
        copies = []
        for w in range(nw):
            for k, (src_blk, dst_blk, rows, peer) in enumerate(copies_of(x, y, c, w)):
                cp = pltpu.make_async_remote_copy(
                    src_ref=in_refs[w].at[pl.ds(src_blk * rows, rows), :],
                    dst_ref=out_refs[w].at[pl.ds(dst_blk * rows, rows), :],
                    send_sem=send_sems.at[w, k], recv_sem=recv_sems.at[w, k], device_id=peer, device_id_type=MESH)
                cp.start()
                copies.append(cp)
        for cp in copies:
            cp.wait_recv()
        for cp in copies:
            cp.wait_send()

    launch()
    return [r[...] for r in out_refs]


def scatter_to_sibling(grads, name, collective_id):
    ns = [g.shape[0] // N_DEV for g in grads]
    return _sequencer_exchange(
        grads, [4 * n for n in ns],
        lambda x, y, c: [(x, y, 1 - c)],
        lambda x, y, c, w: [(2 * q + 1 - c, q, ns[w], (x, y, 1 - c)) for q in range(4)],
        name, collective_id)


def scatter_to_chips(parts, name, collective_id):
    ns = [p.shape[0] // 4 for p in parts]
    return _sequencer_exchange(
        parts, [3 * n for n in ns],
        lambda x, y, c: [(*chip, c) for chip in _other_chips(x, y)],
        lambda x, y, c, w: [(2 * chip[0] + chip[1], j, ns[w], (*chip, c)) for j, chip in enumerate(_other_chips(x, y))],
        name, collective_id)


def add_sibling(grads, landeds, core, name):
    nw = len(grads)

    def body(c_ref, *refs):
        for w in range(nw):
            g_ref, l_ref, o_ref = refs[2 * w], refs[2 * w + 1], refs[2 * nw + w]
            o_ref[...] = (g_ref[...].astype(F32) + l_ref[...].astype(F32)).astype(o_ref.dtype)

    in_specs, out_specs, args = [], [], []
    for g, ld in zip(grads, landeds):
        n, cols = ld.shape[0] // 4, g.shape[1]
        in_specs += [pl.BlockSpec((n, cols), lambda q, c_ref: (2 * q + c_ref[0], 0)),
                     pl.BlockSpec((n, cols), lambda q, c_ref: (q, 0))]
        out_specs.append(pl.BlockSpec((n, cols), lambda q, c_ref: (q, 0)))
        args += [g, ld]
    grid_spec = pltpu.PrefetchScalarGridSpec(
        num_scalar_prefetch=1, grid=(4,), in_specs=in_specs, out_specs=tuple(out_specs))
    return pl.pallas_call(
        body, out_shape=tuple(jax.ShapeDtypeStruct(ld.shape, ld.dtype) for ld in landeds), grid_spec=grid_spec,
        compiler_params=_params("arbitrary"), name=name)(core, *args)


ADAMW_HALVES = 2


def adamw_shards(items, chip, name):
    c1 = 1.0 / (1.0 - ADAM_B1 ** ADAM_STEP)
    c2 = 1.0 / (1.0 - ADAM_B2 ** ADAM_STEP)
    ni = len(items)

    def body(q_ref, *refs):
        outs = refs[len(refs) - 4 * ni:]
        for k in range(ni):
            w_ref, m_ref, v_ref, p_ref, l0_ref, l1_ref, l2_ref = refs[7 * k:7 * k + 7]
            g_ref, d_ref, nm_ref, nv_ref = outs[4 * k:4 * k + 4]
            gv = ((p_ref[...].astype(F32) + l0_ref[...].astype(F32)) + l1_ref[...].astype(F32)) + l2_ref[...].astype(F32)
            nm = ADAM_B1 * m_ref[...] + (1.0 - ADAM_B1) * gv
            nv = ADAM_B2 * v_ref[...] + (1.0 - ADAM_B2) * (gv * gv)
            g_ref[...] = gv
            nm_ref[...] = nm
            nv_ref[...] = nv
            d_ref[...] = -ADAM_LR * ((nm * c1) / (jnp.sqrt(nv * c2) + ADAM_EPS) + ADAM_WD * w_ref[...])

    sub = ADAMW_HALVES
    in_specs, out_specs, out_shape, args, donated = [], [], [], [chip], []
    for layer, w, m, v, part, landed, earlier in items:
        rows, cols = landed.shape[0] // (3 * sub), w.shape[1]

        def block(first, rows=rows, cols=cols):
            return pl.BlockSpec((rows, cols), lambda i, q_ref: (first(q_ref) * sub + i, 0))

        own = block(lambda q_ref, layer=layer: layer)
        in_specs += [own, own, own, block(lambda q_ref: q_ref[0])] + [block(lambda q_ref, j=j: j) for j in range(3)]
        args += [w, m, v, part, landed, landed, landed]
        out_specs += [own] * 4
        out_shape += [jax.ShapeDtypeStruct(w.shape, F32)] * 4
        donated.append(earlier)
    aliases = {}
    for k, earlier in enumerate(donated):
        if earlier is not None:
            for j in range(4):
                aliases[len(args)] = 4 * k + j
                in_specs.append(ANY)
                args.append(earlier[j])
    grid_spec = pltpu.PrefetchScalarGridSpec(
        num_scalar_prefetch=1, grid=(sub,), in_specs=in_specs, out_specs=tuple(out_specs))
    outs = pl.pallas_call(
        body, out_shape=tuple(out_shape), grid_spec=grid_spec, input_output_aliases=aliases,
        compiler_params=_params("arbitrary"), name=name)(*args)
    return [tuple(outs[4 * k:4 * k + 4]) for k in range(ni)]


def _pack(arrays):
    flat = jnp.concatenate([a.reshape(-1).astype(F32) for a in arrays])
    pad = (-flat.shape[0]) % (8 * LANES)
    return jnp.pad(flat, (0, pad)).reshape(-1, LANES)


def _unpack(slab, shapes):
    flat = slab.reshape(slab.shape[:-2] + (-1,))
    out, off = [], 0
    for shp in shapes:
        size = 1
        for s in shp:
            size *= s
        out.append(flat[..., off:off + size].reshape(flat.shape[:-1] + tuple(shp)))
        off += size
    return out


def kernel(x, mem, norm1_g, mem_norm_g, a_w_in, a_q_g, a_k_g, a_rel_bias, b_w_in, b_b_in, b_conv_w, b_conv_b, b_ln_g, b_ln_b, mq_g, mk_g, w_mem_kv, w_out, norm2_g, w_gate, w_up, w_down, loss_target, m_norm1_g, m_mem_norm_g, m_a_w_in, m_a_q_g, m_a_k_g, m_a_rel_bias, m_b_w_in, m_b_b_in, m_b_conv_w, m_b_conv_b, m_b_ln_g, m_b_ln_b, m_mq_g, m_mk_g, m_w_mem_kv, m_w_out, m_norm2_g, m_w_gate, m_w_up, m_w_down, v_norm1_g, v_mem_norm_g, v_a_w_in, v_a_q_g, v_a_k_g, v_a_rel_bias, v_b_w_in, v_b_b_in, v_b_conv_w, v_b_conv_b, v_b_ln_g, v_b_ln_b, v_mq_g, v_mk_g, v_w_mem_kv, v_w_out, v_norm2_g, v_w_gate, v_w_up, v_w_down):
    batch, seq, d = x.shape
    mtok = mem.shape[1]
    n = batch * seq
    ax, ay, ac = _place()
    me = 4 * ax + 2 * ay + ac
    core_arr = jnp.reshape(ac, (1,)).astype(jnp.int32)
    chip_arr = jnp.reshape(2 * ax + ay, (1,)).astype(jnp.int32)

    def t_bf16(w):
        return jnp.transpose(w).astype(BF16)

    def after(value, *earlier):
        return lax.optimization_barrier((value, *earlier))[0]

    def gather_mix(l, when, name, collective_id):
        srcs = [w_mem_kv[l].astype(BF16), w_out[l].astype(BF16)]
        if l == 1:
            srcs += [t_bf16(b_w_in[0]), _pack([b_b_in, b_conv_w, b_conv_b, b_ln_g, b_ln_b])]
        return gather_weights([after(srcs[0], *when)] + srcs[1:], name, collective_id)

    def gather_ffn(l, when, name, collective_id):
        return gather_weights(
            [after(t_bf16(w_gate[l]), *when), t_bf16(w_up[l]), w_down[l].astype(BF16)], name, collective_id)

    f_loc = b_b_in.shape[1]
    c_loc = b_conv_b.shape[1]

    def two(g):
        return jnp.concatenate([g, g], axis=-1)

    gq2, gk2 = two(a_q_g), two(a_k_g)
    rel16 = jnp.pad(a_rel_bias[0], ((0, 16 - a_rel_bias.shape[1]), (0, 0)))
    bias = bias_blocks(rel16)

    x0 = x.reshape(n, d)
    mem2 = mem.reshape(batch * mtok, d)

    saved = []
    xin = x0
    a_win_t, = gather_weights([t_bf16(a_w_in[0])], "gather_in_a", 1)
    wg_t, wu_t, wd, wo, wkv = [None] * 2, [None] * 2, [None] * 2, [None] * 2, [None] * 2
    h = after(rms_fwd(xin, norm1_g[0:1], name="rms1_fwd_0"), bias)
    target = loss_target.reshape(n, d)
    for l in range(2):
        gq4 = jnp.tile(mq_g[l:l + 1], (1, 4))
        gk4 = jnp.tile(mk_g[l:l + 1], (1, 4))
        y_conv = None
        if l == 0:
            wkv[0], wo[0] = gather_mix(0, (h, a_win_t), "gather_mix_a", 2)
            z = mm_nt(h, a_win_t, name="in_proj_a")
            wg_t[0], wu_t[0], wd[0] = gather_ffn(0, (z, wkv[0]), "gather_ffn_a", 3)
            cat = attn_fwd(z, gq2, gk2, bias, batch, seq)
            wkv[1], wo[1], b_win_t, conv_slabs = gather_mix(1, (cat, wg_t[0]), "gather_mix_b", 4)
            qcol = 3 * TOK_WIDTH // MEM_WIDTH
        else:
            small_shapes = [(f_loc,), (CONV_W, c_loc), (c_loc,), (c_loc,), (c_loc,)]
            bb_g, cw_g, cb_g, lg_g, lb_g = _unpack(conv_slabs.reshape(N_DEV, -1, LANES), small_shapes)
            bb_full = bb_g.reshape(1, -1)
            cw_full = jnp.pad(jnp.transpose(cw_g, (1, 0, 2)).reshape(CONV_W, -1), ((0, 32 - CONV_W), (0, 0)))
            cb_full, lg_full, lb_full = cb_g.reshape(1, -1), lg_g.reshape(1, -1), lb_g.reshape(1, -1)
            z = mm_nt(h, b_win_t, bias=bb_full, name="in_proj_b")
            cat, y_conv = conv_fwd(z, cw_full, cb_full, lg_full, lb_full, batch, seq)
            qcol = 2 * TOK_WIDTH // MEM_WIDTH
        cat, mem_n, kv = memattn_fwd(
            z, mem2, mem_norm_g[l:l + 1], wkv[l], gq4, gk4, cat, batch, seq, qcol, name=f"memattn_fwd_{l}")
        x1, h2 = proj_norm(cat, wo[l], xin, norm2_g[l:l + 1], name=f"out_proj_{l}")
        if l == 0:
            wg_t[1], wu_t[1], wd[1] = gather_ffn(1, (x1, b_win_t), "gather_ffn_b", 5)
        if l == 0:
            gate, up, act, x2, h_next = ffn_fwd(h2, wg_t[0], wu_t[0], wd[0], x1, gain=norm1_g[1:2], name="ffn_fwd_0")
        else:
            gate, up, act, dx_b, loss_blk = ffn_fwd(h2, wg_t[1], wu_t[1], wd[1], x1, target=target, name="ffn_fwd_1")
        saved.append(dict(xin=xin, h=h, mem_n=mem_n, kv=kv, gq4=gq4, gk4=gk4, z=z, qcol=qcol, cat=cat, x1=x1, h2=h2,
                          gate=gate, up=up, act=act, y_conv=y_conv))
        if l == 0:
            xin, h = x2, h_next

    big = {}
    small = {}
    reduced = {}
    groups = 0

    def scatter_siblings(keys):
        nonlocal groups
        gid = groups
        groups += 1
        return gid, keys, scatter_to_sibling([big[k] for k in keys], f"scatter_sibling_{gid}", 8 + 2 * gid)

    def scatter_chips(stage1, when):
        gid, keys, landed1 = stage1
        parts = add_sibling([after(big[keys[0]], when)] + [big[k] for k in keys[1:]], landed1, core_arr,
                            name=f"add_sibling_{gid}")
        landed2 = scatter_to_chips(parts, f"scatter_chips_{gid}", 9 + 2 * gid)
        for k, p, ld in zip(keys, parts, landed2):
            reduced[k] = (p, ld)
        return parts, landed2

    def rows_of(w, transposed):
        w = jnp.swapaxes(w, 1, 2) if transposed else w
        return w.reshape(w.shape[0] * w.shape[1], w.shape[2])

    sharded = {
        "win0": (2, True), "win1": (6, True), "wkv": (14, False), "wo": (15, False),
        "wg": (17, True), "wu": (18, True), "wd": (19, False)}
    weights = [norm1_g, mem_norm_g, a_w_in, a_q_g, a_k_g, a_rel_bias, b_w_in, b_b_in, b_conv_w, b_conv_b, b_ln_g,
               b_ln_b, mq_g, mk_g, w_mem_kv, w_out, norm2_g, w_gate, w_up, w_down]
    moms = [m_norm1_g, m_mem_norm_g, m_a_w_in, m_a_q_g, m_a_k_g, m_a_rel_bias, m_b_w_in, m_b_b_in, m_b_conv_w,
            m_b_conv_b, m_b_ln_g, m_b_ln_b, m_mq_g, m_mk_g, m_w_mem_kv, m_w_out, m_norm2_g, m_w_gate, m_w_up, m_w_down]
    vels = [v_norm1_g, v_mem_norm_g, v_a_w_in, v_a_q_g, v_a_k_g, v_a_rel_bias, v_b_w_in, v_b_b_in, v_b_conv_w,
            v_b_conv_b, v_b_ln_g, v_b_ln_b, v_mq_g, v_mk_g, v_w_mem_kv, v_w_out, v_norm2_g, v_w_gate, v_w_up, v_w_down]
    updated = {}

    def update_layer(l, when):
        for group, keys in (("ffn", ("wg", "wu", "wd")), ("mix", (f"win{l}", "wkv", "wo"))):
            items = []
            for key in keys:
                idx, transposed = sharded[key]
                layer, rkey = (0, key) if key.startswith("win") else (l, f"{key}{l}")
                part, landed = reduced[rkey]
                w_rows = rows_of(weights[idx], transposed)
                items.append((layer, after(w_rows, when) if not items else w_rows, rows_of(moms[idx], transposed),
                              rows_of(vels[idx], transposed), part, landed, updated.get(key)))
            for key, result in zip(keys, adamw_shards(items, chip_arr, name=f"adamw_{group}_{l}")):
                updated[key] = result

    mix_landed = None
    for l in (1, 0):
        sv = saved[l]
        dgate, dup, dx1_b, dcat, small[f"norm2_{l}"] = ffn_bwd(
            dx_b, wd[l], sv["gate"], sv["up"], wg_t[l], wu_t[l], sv["x1"], norm2_g[l:l + 1], wo[l], name=f"ffn_bwd_{l}")
        if l == 0:
            dgate = after(dgate, *mix_landed)
            update_layer(1, dx1_b)
        big[f"wg{l}"], big[f"wu{l}"], big[f"wd{l}"] = ffn_weight_grads(
            dgate, dup, sv["h2"], sv["act"], dx_b, name=f"grad_ffn_{l}")
        stage1 = scatter_siblings([f"wd{l}", f"wg{l}", f"wu{l}"])
        big[f"wo{l}"] = mm_tn(sv["cat"], dx1_b, name=f"grad_wo_{l}")
        parts, ffn_landed = scatter_chips(stage1, big[f"wo{l}"])
        dcat = after(dcat, *parts)
        if l == 0:
            dz, dbias, small["a_q"], small["a_k"] = attn_bwd(sv["z"], dcat, gq2, gk2, bias, batch, seq)
            small["rel"] = bias_grad(dbias)
            win_t = a_win_t
        else:
            dz, small["cw"], small["csum"] = conv_bwd(sv["z"], sv["y_conv"], dcat, cw_full, lg_full, lb_full, batch, seq)
            win_t = b_win_t
        dz = after(dz, *ffn_landed)
        dz, small[f"mq_{l}"], small[f"mk_{l}"], big[f"wkv{l}"], small[f"memnorm_{l}"] = memattn_bwd(
            sv["z"], sv["kv"], dcat, sv["gq4"], sv["gk4"], mem2, sv["mem_n"], wkv[l], dz, batch, seq, sv["qcol"],
            name=f"memattn_bwd_{l}")
        big[f"win{l}"] = mm_tn(dz, sv["h"], name=f"grad_win_{l}")
        stage1 = scatter_siblings([f"win{l}", f"wkv{l}", f"wo{l}"])
        dx_b, small[f"norm1_{l}"], dz_sum = in_proj_bwd(
            dz, win_t, sv["xin"], norm1_g[l:l + 1], dx1_b, BF16 if l == 1 else F32, name=f"in_proj_bwd_{l}")
        if l == 1:
            small["bb"] = dz_sum
        parts, mix_landed = scatter_chips(stage1, dx_b)
        dx_b = after(dx_b, *parts)
    grad_x = dx_b.reshape(batch, seq, d)
    update_layer(0, dx_b)

    def shaped(rows, idx, transposed):
        shp = weights[idx].shape
        if transposed:
            return jnp.swapaxes(rows.reshape(shp[0], shp[2], shp[1]), 1, 2)
        return rows.reshape(shp)

    def fold(v, groups):
        return jnp.sum(v.reshape(groups, HEAD_DIM), axis=0, keepdims=True)

    heads = a_rel_bias.shape[1]
    small_list = [
        jnp.concatenate([small["norm1_0"], small["norm1_1"]]),
        jnp.concatenate([small["memnorm_0"], small["memnorm_1"]]),
        fold(small["a_q"], 2), fold(small["a_k"], 2), small["rel"][:heads],
        small["bb"], small["cw"][:CONV_W], small["csum"][0:1], small["csum"][1:2], small["csum"][2:3],
        jnp.concatenate([fold(small["mq_0"], 4), fold(small["mq_1"], 4)]),
        jnp.concatenate([fold(small["mk_0"], 4), fold(small["mk_1"], 4)]),
        jnp.concatenate([small["norm2_0"], small["norm2_1"]]),
    ]
    (g_norm1, g_memnorm, g_aq, g_ak, g_rel, g_bb_full, g_cw_full, g_cb_full, g_lg_full, g_lb_full,
     g_mq, g_mk, g_norm2, loss_sum) = reduce_small(small_list + [loss_blk])
    loss = loss_sum[0, 0]
    g_bb = lax.dynamic_slice_in_dim(g_bb_full, me * f_loc, f_loc, axis=1)
    g_cw = lax.dynamic_slice_in_dim(g_cw_full, me * c_loc, c_loc, axis=1)
    g_cb = lax.dynamic_slice_in_dim(g_cb_full, me * c_loc, c_loc, axis=1)
    g_lg = lax.dynamic_slice_in_dim(g_lg_full, me * c_loc, c_loc, axis=1)
    g_lb = lax.dynamic_slice_in_dim(g_lb_full, me * c_loc, c_loc, axis=1)

    grads = [g_norm1, g_memnorm, None, g_aq, g_ak, g_rel, None, g_bb, g_cw, g_cb, g_lg, g_lb,
             g_mq, g_mk, None, None, g_norm2, None, None, None]
    deltas, new_m, new_v = [None] * 20, [None] * 20, [None] * 20
    for key, (idx, transposed) in sharded.items():
        grads[idx], deltas[idx], new_m[idx], new_v[idx] = (shaped(r, idx, transposed) for r in updated[key])

    def flat2(a):
        return a.reshape(a.shape[-2:])

    small_idx = [i for i in range(20) if i not in {idx for idx, _ in sharded.values()}]
    dl, nm, nv = adamw_small([flat2(weights[i]) for i in small_idx], [flat2(grads[i]) for i in small_idx],
                             [flat2(moms[i]) for i in small_idx], [flat2(vels[i]) for i in small_idx])
    for i, a, b, cc in zip(small_idx, dl, nm, nv):
        shp = weights[i].shape
        grads[i], deltas[i], new_m[i], new_v[i] = grads[i].reshape(shp), a.reshape(shp), b.reshape(shp), cc.reshape(shp)

    return (loss, grad_x, *grads, *deltas, *new_m, *new_v)
```

```python
import jax
import jax.numpy as jnp
from jax import lax
from jax.experimental import pallas as pl
from jax.experimental.pallas import tpu as pltpu
from jax.experimental.pallas import tpu_sc as plsc

F32 = jnp.float32
BF16 = jnp.bfloat16
HIGHEST = lax.Precision.HIGHEST
MESH = pl.DeviceIdType.MESH
ANY = pl.BlockSpec(memory_space=pl.ANY)

N_DEV = 8
D_MODEL = 1024
HEAD_DIM = 64
TOK_WIDTH = 768
MEM_WIDTH = 256
CHUNK = 64
Q_BLOCK = 256
KEY_WIN = 768
BAND = 576
N_REL = 192
CONV_W = 31
CONV_HALO = 32
NORM_EPS = 1e-6
NEG_INF = -1e30
ATTN_SCALE = HEAD_DIM ** -0.5
LANES = 128
ROW_TILE = 512
VMEM_LIMIT = 56 * 1024 * 1024

ADAM_LR, ADAM_B1, ADAM_B2, ADAM_EPS, ADAM_WD, ADAM_STEP = 0.001, 0.9, 0.999, 1e-08, 0.01, 10


def _params(*sem):
    return pltpu.CompilerParams(dimension_semantics=sem, vmem_limit_bytes=VMEM_LIMIT)


WIDE_ROW_TILE = 1024


def _row_tile(m, rows=ROW_TILE):
    return rows if m % rows == 0 else m


def _col_tile(n, cap=1408):
    best = None
    for t in range(LANES, min(n, cap) + 1, LANES):
        if n % t == 0:
            best = t
    return best if best is not None else n


def _dot(a, b, ca, cb):
    return lax.dot_general(a, b, (((ca,), (cb,)), ((), ())), preferred_element_type=F32)


def _sigmoid(x):
    return 0.5 * jnp.tanh(0.5 * x) + 0.5


def mm_nt(a, b, bias=None, out_dtype=BF16, name="mm_nt"):
    m, k = a.shape
    n = b.shape[0]
    tm, tn = _row_tile(m, WIDE_ROW_TILE), _col_tile(n)

    def body(*refs):
        a_ref, b_ref = refs[0], refs[1]
        o_ref = refs[-1]
        acc = _dot(a_ref[...].astype(BF16), b_ref[...].astype(BF16), 1, 1)
        if bias is not None:
            acc = acc + refs[2][...]
        o_ref[...] = acc.astype(o_ref.dtype)

    in_specs = [pl.BlockSpec((tm, k), lambda j, i: (i, 0)), pl.BlockSpec((tn, k), lambda j, i: (j, 0))]
    args = [a, b]
    if bias is not None:
        in_specs.append(pl.BlockSpec((1, tn), lambda j, i: (0, j)))
        args.append(bias)
    return pl.pallas_call(
        body, out_shape=jax.ShapeDtypeStruct((m, n), out_dtype), grid=(n // tn, m // tm),
        in_specs=in_specs, out_specs=pl.BlockSpec((tm, tn), lambda j, i: (i, j)),
        compiler_params=_params("parallel", "arbitrary"), name=name)(*args)


def mm_tn(a, b, out_dtype=BF16, name="mm_tn"):
    t, r = a.shape
    c = b.shape[1]
    tr = _col_tile(r, 512)

    def body(a_ref, b_ref, o_ref):
        o_ref[...] = _dot(a_ref[...].astype(BF16), b_ref[...].astype(BF16), 0, 0).astype(o_ref.dtype)

    return pl.pallas_call(
        body, out_shape=jax.ShapeDtypeStruct((r, c), out_dtype), grid=(r // tr,),
        in_specs=[pl.BlockSpec((t, tr), lambda i: (0, i)), pl.BlockSpec((t, c), lambda i: (0, 0))],
        out_specs=pl.BlockSpec((tr, c), lambda i: (i, 0)),
        compiler_params=_params("parallel"), name=name)(a, b)


def _resident(shape):
    return pl.BlockSpec(shape, lambda i: (0, 0), pipeline_mode=pl.Buffered(1))


def proj_norm(a, b, res, gain, name):
    m, k = a.shape
    n = b.shape[1]
    tm = _row_tile(m)

    def body(a_ref, b_ref, res_ref, g_ref, x_ref, h_ref):
        xv = res_ref[...] + _dot(a_ref[...], b_ref[...], 1, 0)
        x_ref[...] = xv
        r = lax.rsqrt(jnp.mean(xv * xv, axis=-1, keepdims=True) + NORM_EPS)
        h_ref[...] = (xv * r * g_ref[...]).astype(BF16)

    row = pl.BlockSpec((tm, n), lambda i: (i, 0))
    return pl.pallas_call(
        body, out_shape=(jax.ShapeDtypeStruct((m, n), F32), jax.ShapeDtypeStruct((m, n), BF16)), grid=(m // tm,),
        in_specs=[pl.BlockSpec((tm, k), lambda i: (i, 0)), _resident((k, n)), row, _resident((1, n))],
        out_specs=(row, row), compiler_params=_params("parallel"), name=name)(a, b, res, gain)


def in_proj_bwd(pieces, w_t, x, gain, dres, out_dtype, name):
    m, n = x.shape
    k = pieces[0].shape[1]
    tm = _row_tile(m)
    npc = len(pieces)
    offs = [sum(p.shape[1] for p in pieces[:i]) for i in range(npc + 1)]

    def body(*refs):
        dz_refs = refs[:npc]
        w_ref, x_ref, g_ref, dres_ref, dx_ref, dg_ref, cs_ref = refs[npc:]

        @pl.when(pl.program_id(0) == 0)
        def _():
            dg_ref[...] = jnp.zeros_like(dg_ref)
            cs_ref[...] = jnp.zeros_like(cs_ref)

        cs_ref[...] += jnp.sum(dz_refs[0][...].astype(F32), axis=0, keepdims=True)
        dhv = _dot(dz_refs[0][...], w_ref[offs[0]:offs[1], :], 1, 0)
        for i in range(1, npc):
            dhv = dhv + _dot(dz_refs[i][...], w_ref[offs[i]:offs[i + 1], :], 1, 0)
        xv = x_ref[...]
        r = lax.rsqrt(jnp.mean(xv * xv, axis=-1, keepdims=True) + NORM_EPS)
        xhat = xv * r
        dg_ref[...] += jnp.sum(dhv * xhat, axis=0, keepdims=True)
        dxhat = dhv * g_ref[...]
        dx = dres_ref[...].astype(F32) + r * (dxhat - xhat * jnp.mean(dxhat * xhat, axis=-1, keepdims=True))
        dx_ref[...] = dx.astype(dx_ref.dtype)

    row = pl.BlockSpec((tm, n), lambda i: (i, 0))
    return pl.pallas_call(
        body, out_shape=(jax.ShapeDtypeStruct((m, n), out_dtype), jax.ShapeDtypeStruct((1, n), F32),
                         jax.ShapeDtypeStruct((1, k), F32)), grid=(m // tm,),
        in_specs=[pl.BlockSpec((tm, p.shape[1]), lambda i: (i, 0)) for p in pieces]
        + [_resident(w_t.shape), row, _resident((1, n)), row],
        out_specs=(row, pl.BlockSpec((1, n), lambda i: (0, 0)), pl.BlockSpec((1, k), lambda i: (0, 0))),
        compiler_params=_params("arbitrary"), name=name)(*pieces, w_t, x, gain, dres)


def grad_pieces(pieces, b, name):
    t, c = b.shape
    tr = 2 * LANES
    tiles = [p.shape[1] // tr for p in pieces]
    starts = [sum(tiles[:i]) for i in range(len(pieces) + 1)]

    def body(*refs):
        a_refs, b_ref, o_ref = refs[:len(pieces)], refs[len(pieces)], refs[len(pieces) + 1]
        i = pl.program_id(0)
        for p, a_ref in enumerate(a_refs):
            @pl.when((i >= starts[p]) & (i < starts[p + 1]))
            def _(a_ref=a_ref):
                o_ref[...] = _dot(a_ref[...], b_ref[...], 0, 0).astype(o_ref.dtype)

    def a_spec(p):
        return pl.BlockSpec((t, tr), lambda i: (0, jnp.clip(i - starts[p], 0, tiles[p] - 1)))

    return pl.pallas_call(
        body, out_shape=jax.ShapeDtypeStruct((starts[-1] * tr, c), BF16), grid=(starts[-1],),
        in_specs=[a_spec(p) for p in range(len(pieces))] + [_resident((t, c))],
        out_specs=pl.BlockSpec((tr, c), lambda i: (i, 0)),
        compiler_params=_params("arbitrary"), name=name)(*pieces, b)


FFN_ROWS = 256


def _ffn_row_tile(m):
    return FFN_ROWS if m % FFN_ROWS == 0 else m


def ffn_fwd(h2, wg_t, wu_t, wd, x1, gain=None, target=None, name="ffn_fwd"):
    n, d = h2.shape
    f = wg_t.shape[0]
    tm = _ffn_row_tile(n)
    nt = n // tm
    last = target is not None

    def body(h_ref, wg_ref, wu_ref, wd_ref, x1_ref, e_ref, g_ref, u_ref, a_ref, *rest):
        hv = h_ref[...]
        gv = _dot(hv, wg_ref[...], 1, 1)
        uv = _dot(hv, wu_ref[...], 1, 1)
        g_ref[...] = gv.astype(BF16)
        u_ref[...] = uv.astype(BF16)
        av = (gv * _sigmoid(gv) * uv).astype(BF16)
        a_ref[...] = av
        xv = x1_ref[...] + _dot(av, wd_ref[...], 1, 0)
        if not last:
            x_ref, hn_ref = rest
            x_ref[...] = xv
            r = lax.rsqrt(jnp.mean(xv * xv, axis=-1, keepdims=True) + NORM_EPS)
            hn_ref[...] = (xv * r * e_ref[...]).astype(BF16)
        else:
            dyb_ref, l_ref, acc_ref = rest
            i = pl.program_id(0)

            @pl.when(i == 0)
            def _():
                acc_ref[...] = jnp.zeros_like(acc_ref)

            err = xv - e_ref[...]
            dyb_ref[...] = (err * (1.0 / d)).astype(BF16)
            acc_ref[...] += jnp.sum(err * err, axis=0, keepdims=True)

            @pl.when(i == nt - 1)
            def _():
                total = jnp.sum(acc_ref[...], axis=-1, keepdims=True) * (0.5 / d)
                l_ref[...] = jnp.broadcast_to(total, l_ref.shape)

    row_d = pl.BlockSpec((tm, d), lambda i: (i, 0))
    row_f = pl.BlockSpec((tm, f), lambda i: (i, 0))
    act_shape = jax.ShapeDtypeStruct((n, f), BF16)
    if not last:
        extra_in, extra = _resident((1, d)), gain
        out_shape = (act_shape, act_shape, act_shape, jax.ShapeDtypeStruct((n, d), F32), jax.ShapeDtypeStruct((n, d), BF16))
        out_specs = (row_f, row_f, row_f, row_d, row_d)
        scratch = []
    else:
        extra_in, extra = row_d, target
        out_shape = (act_shape, act_shape, act_shape, jax.ShapeDtypeStruct((n, d), BF16),
                     jax.ShapeDtypeStruct((8, LANES), F32))
        out_specs = (row_f, row_f, row_f, row_d, pl.BlockSpec((8, LANES), lambda i: (0, 0)))
        scratch = [pltpu.VMEM((1, d), F32)]
    return pl.pallas_call(
        body, out_shape=out_shape, grid=(nt,),
        in_specs=[row_d, _resident((f, d)), _resident((f, d)), _resident((f, d)), row_d, extra_in],
        out_specs=out_specs, scratch_shapes=scratch,
        compiler_params=_params("arbitrary"), name=name)(h2, wg_t, wu_t, wd, x1, extra)


def ffn_bwd(dx_b, wd, gate, up, wg_t, wu_t, x1, gain, wo, name="ffn_bwd"):
    n, d = x1.shape
    f = wd.shape[0]
    tm = _ffn_row_tile(n)

    def body(dxb_ref, wd_ref, g_ref, u_ref, wg_ref, wu_ref, x_ref, gain_ref, wo_ref,
             dg_ref, du_ref, dxo_ref, dc_ref, dgain_ref):
        @pl.when(pl.program_id(0) == 0)
        def _():
            dgain_ref[...] = jnp.zeros_like(dgain_ref)

        dact = _dot(dxb_ref[...], wd_ref[...], 1, 1)
        gv = g_ref[...].astype(F32)
        uv = u_ref[...].astype(F32)
        sg = _sigmoid(gv)
        dgv = (dact * uv * sg * (1.0 + gv * (1.0 - sg))).astype(BF16)
        duv = (dact * gv * sg).astype(BF16)
        dg_ref[...] = dgv
        du_ref[...] = duv
        dhv = _dot(dgv, wg_ref[...], 1, 0) + _dot(duv, wu_ref[...], 1, 0)
        xv = x_ref[...]
        r = lax.rsqrt(jnp.mean(xv * xv, axis=-1, keepdims=True) + NORM_EPS)
        xhat = xv * r
        dgain_ref[...] += jnp.sum(dhv * xhat, axis=0, keepdims=True)
        dxhat = dhv * gain_ref[...]
        dxb = (dxb_ref[...].astype(F32) + r * (dxhat - xhat * jnp.mean(dxhat * xhat, axis=-1, keepdims=True))).astype(BF16)
        dxo_ref[...] = dxb
        dc_ref[...] = _dot(dxb, wo_ref[...], 1, 1).astype(BF16)

    row_d = pl.BlockSpec((tm, d), lambda i: (i, 0))
    row_f = pl.BlockSpec((tm, f), lambda i: (i, 0))
    w_spec = _resident((f, d))
    act_shape = jax.ShapeDtypeStruct((n, f), BF16)
    row_shape = jax.ShapeDtypeStruct((n, d), BF16)
    return pl.pallas_call(
        body, out_shape=(act_shape, act_shape, row_shape, jax.ShapeDtypeStruct((n, wo.shape[0]), BF16),
                         jax.ShapeDtypeStruct((1, d), F32)),
        grid=(n // tm,),
        in_specs=[row_d, w_spec, row_f, row_f, w_spec, w_spec, row_d, _resident((1, d)), _resident(wo.shape)],
        out_specs=(row_f, row_f, row_d, pl.BlockSpec((tm, wo.shape[0]), lambda i: (i, 0)),
                   pl.BlockSpec((1, d), lambda i: (0, 0))),
        compiler_params=_params("arbitrary"), name=name)(dx_b, wd, gate, up, wg_t, wu_t, x1, gain, wo)


def ffn_weight_grads(dgate, dup, h2, act, dx_b, name="ffn_weight_grads"):
    t, r = dgate.shape
    c = h2.shape[1]
    tr = _col_tile(r, 512)

    def body(a1_ref, a2_ref, a3_ref, b12_ref, b3_ref, o1_ref, o2_ref, o3_ref):
        bv = b12_ref[...]
        o1_ref[...] = _dot(a1_ref[...], bv, 0, 0).astype(o1_ref.dtype)
        o2_ref[...] = _dot(a2_ref[...], bv, 0, 0).astype(o2_ref.dtype)
        o3_ref[...] = _dot(a3_ref[...], b3_ref[...], 0, 0).astype(o3_ref.dtype)

    a_spec = pl.BlockSpec((t, tr), lambda i: (0, i))
    o_spec = pl.BlockSpec((tr, c), lambda i: (i, 0))
    shape = jax.ShapeDtypeStruct((r, c), BF16)
    return pl.pallas_call(
        body, out_shape=(shape, shape, shape), grid=(r // tr,),
        in_specs=[a_spec, a_spec, a_spec, _resident((t, c)), _resident((t, c))],
        out_specs=(o_spec, o_spec, o_spec), compiler_params=_params("parallel"), name=name)(dgate, dup, act, h2, dx_b)


def rms_fwd(x, g, name="rms_fwd"):
    n, d = x.shape
    tm = _row_tile(n)

    def body(x_ref, g_ref, o_ref):
        xv = x_ref[...]
        r = lax.rsqrt(jnp.mean(xv * xv, axis=-1, keepdims=True) + NORM_EPS)
        o_ref[...] = (xv * r * g_ref[...]).astype(o_ref.dtype)

    return pl.pallas_call(
        body, out_shape=jax.ShapeDtypeStruct((n, d), BF16), grid=(n // tm,),
        in_specs=[pl.BlockSpec((tm, d), lambda i: (i, 0)), pl.BlockSpec((1, d), lambda i: (0, 0))],
        out_specs=pl.BlockSpec((tm, d), lambda i: (i, 0)),
        compiler_params=_params("parallel"), name=name)(x, g)


def _group_masks(width):
    lane = lax.broadcasted_iota(jnp.int32, (1, width), 1)
    return [(lane >= HEAD_DIM * g) & (lane < HEAD_DIM * (g + 1)) for g in range(width // HEAD_DIM)]


def _group_sum(x, masks):
    out = jnp.zeros_like(x)
    for msk in masks:
        s = jnp.sum(jnp.where(msk, x, 0.0), axis=-1, keepdims=True)
        out = jnp.where(msk, s, out)
    return out


def _head_norm(x, gain, masks):
    r = lax.rsqrt(_group_sum(x * x, masks) * (1.0 / HEAD_DIM) + NORM_EPS)
    xhat = x * r
    return xhat * gain, xhat, r


def _head_norm_bwd(dxn, xhat, r, gain, masks):
    dgain = jnp.sum(dxn * xhat, axis=0, keepdims=True)
    dxhat = dxn * gain
    mean_t = _group_sum(dxhat * xhat, masks) * (1.0 / HEAD_DIM)
    return r * (dxhat - xhat * mean_t), dgain


def _softmax_rows(s):
    e = jnp.exp(s - jnp.max(s, axis=-1, keepdims=True))
    return e * (1.0 / jnp.sum(e, axis=-1, keepdims=True))


def _rel_onehot():
    col = lax.broadcasted_iota(jnp.int32, (1, KEY_WIN), 1)
    off = jnp.where(col < KEY_WIN - LANES, col, col - KEY_WIN)
    idx = jnp.clip(8 * CHUNK - off, -(CHUNK - 1), LANES) + (CHUNK - 1)
    return (lax.broadcasted_iota(jnp.int32, (N_REL, KEY_WIN), 0) == idx).astype(F32)


def bias_blocks(rel16):
    heads = TOK_WIDTH // HEAD_DIM

    def body(rel_ref, o_ref, u_ref):
        u_ref[...] = jnp.dot(rel_ref[...], _rel_onehot(), precision=HIGHEST, preferred_element_type=F32)
        row = lax.broadcasted_iota(jnp.int32, (CHUNK, KEY_WIN), 0)
        col = lax.broadcasted_iota(jnp.int32, (CHUNK, KEY_WIN), 1)
        for h in range(heads):
            xv = jnp.broadcast_to(u_ref[h:h + 1, :], (CHUNK, KEY_WIN))
            for b in range(6):
                xv = jnp.where(((row >> b) & 1) == 1, pltpu.roll(xv, 1 << b, axis=1), xv)
            xv = jnp.where(col < BAND, xv, NEG_INF)
            for i in range(Q_BLOCK // CHUNK):
                o_ref[h, CHUNK * i:CHUNK * (i + 1), :] = pltpu.roll(xv, CHUNK * i, axis=1) if i else xv

    return pl.pallas_call(
        body, out_shape=jax.ShapeDtypeStruct((heads, Q_BLOCK, KEY_WIN), F32),
        scratch_shapes=[pltpu.VMEM((16, KEY_WIN), F32)], name="bias_blocks")(rel16)


def bias_grad(dbias):
    heads = dbias.shape[0]

    def body(db_ref, o_ref, y_ref):
        y_ref[...] = jnp.zeros_like(y_ref)
        row = lax.broadcasted_iota(jnp.int32, (CHUNK, KEY_WIN), 0)
        for h in range(heads):
            fv = db_ref[h, 0:CHUNK, :]
            for i in range(1, Q_BLOCK // CHUNK):
                fv = fv + pltpu.roll(db_ref[h, CHUNK * i:CHUNK * (i + 1), :], KEY_WIN - CHUNK * i, axis=1)
            for b in range(6):
                fv = jnp.where(((row >> b) & 1) == 1, pltpu.roll(fv, KEY_WIN - (1 << b), axis=1), fv)
            y_ref[h:h + 1, :] = jnp.sum(fv, axis=0, keepdims=True)
        o_ref[...] = lax.dot_general(y_ref[...], _rel_onehot(), (((1,), (1,)), ((), ())),
                                     precision=HIGHEST, preferred_element_type=F32)

    return pl.pallas_call(
        body, out_shape=jax.ShapeDtypeStruct((16, N_REL), F32),
        scratch_shapes=[pltpu.VMEM((16, KEY_WIN), F32)], name="bias_grad")(dbias)


def _attn_windows(seq):
    out = []
    for j in range(seq // Q_BLOCK):
        r0 = j * Q_BLOCK
        k0 = max(0, r0 - 8 * CHUNK)
        width = r0 + Q_BLOCK - k0
        out.append((r0, k0, width, KEY_WIN - width))
    return out


def attn_fwd(z, gq2, gk2, bias, batch, seq):
    n = z.shape[0]
    pairs = TOK_WIDTH // LANES

    def body(q_ref, k_ref, v_ref, gq_ref, gk_ref, b_ref, o_ref, qs_s, kn_s):
        masks = _group_masks(LANES)
        qs_s[...] = (_head_norm(q_ref[...].astype(F32), gq_ref[...], masks)[0] * ATTN_SCALE).astype(BF16)
        kn_s[...] = _head_norm(k_ref[...].astype(F32), gk_ref[...], masks)[0].astype(BF16)
        for r0, k0, width, c0 in _attn_windows(seq):
            qb = qs_s[r0:r0 + Q_BLOCK, :]
            kw = kn_s[k0:k0 + width, :]
            vw = v_ref[k0:k0 + width, :]
            out = jnp.zeros((Q_BLOCK, LANES), F32)
            for h, msk in enumerate(masks):
                qh = jnp.where(msk, qb, jnp.zeros_like(qb))
                s = _dot(qh, kw, 1, 1) + b_ref[h, :, c0:KEY_WIN]
                p = _softmax_rows(s).astype(BF16)
                out = jnp.where(msk, _dot(p, vw, 1, 0), out)
            o_ref[r0:r0 + Q_BLOCK, :] = out.astype(o_ref.dtype)

    def col(off):
        return pl.BlockSpec((seq, LANES), lambda b, p: (b, off + p))

    vec = pl.BlockSpec((1, LANES), lambda b, p: (0, 0))
    return pl.pallas_call(
        body, out_shape=jax.ShapeDtypeStruct((n, D_MODEL), BF16), grid=(batch, pairs),
        in_specs=[col(0), col(pairs), col(2 * pairs), vec, vec,
                  pl.BlockSpec((2, Q_BLOCK, KEY_WIN), lambda b, p: (p, 0, 0))],
        out_specs=pl.BlockSpec((seq, LANES), lambda b, p: (b, p)),
        scratch_shapes=[pltpu.VMEM((seq, LANES), BF16), pltpu.VMEM((seq, LANES), BF16)],
        compiler_params=_params("parallel", "arbitrary"), name="attn_fwd")(z, z, z, gq2, gk2, bias)


def attn_bwd(z, dcat, gq2, gk2, bias, batch, seq):
    n = z.shape[0]
    pairs = TOK_WIDTH // LANES

    def body(q_ref, k_ref, v_ref, do_ref, gq_ref, gk_ref, b_ref,
             dq_ref, dk_ref, dv_ref, db_ref, dgq_ref, dgk_ref, qs_s, kn_s, dqn_s, dkn_s, dv_s):
        pi, bi = pl.program_id(0), pl.program_id(1)
        masks = _group_masks(LANES)

        @pl.when(bi == 0)
        def _():
            db_ref[...] = jnp.zeros_like(db_ref)

        @pl.when((bi == 0) & (pi == 0))
        def _():
            dgq_ref[...] = jnp.zeros_like(dgq_ref)
            dgk_ref[...] = jnp.zeros_like(dgk_ref)

        qn, qhat, rq = _head_norm(q_ref[...].astype(F32), gq_ref[...], masks)
        kn, khat, rk = _head_norm(k_ref[...].astype(F32), gk_ref[...], masks)
        qs_s[...] = (qn * ATTN_SCALE).astype(BF16)
        kn_s[...] = kn.astype(BF16)
        dkn_s[...] = jnp.zeros_like(dkn_s)
        dv_s[...] = jnp.zeros_like(dv_s)
        for r0, k0, width, c0 in _attn_windows(seq):
            qb = qs_s[r0:r0 + Q_BLOCK, :]
            dob = do_ref[r0:r0 + Q_BLOCK, :]
            kw = kn_s[k0:k0 + width, :]
            vw = v_ref[k0:k0 + width, :]
            dq_acc = jnp.zeros((Q_BLOCK, LANES), F32)
            dk_acc = jnp.zeros((width, LANES), F32)
            dv_acc = jnp.zeros((width, LANES), F32)
            for h, msk in enumerate(masks):
                qh = jnp.where(msk, qb, jnp.zeros_like(qb))
                doh = jnp.where(msk, dob, jnp.zeros_like(dob))
                p = _softmax_rows(_dot(qh, kw, 1, 1) + b_ref[h, :, c0:KEY_WIN])
                dp = _dot(doh, vw, 1, 1)
                ds = p * (dp - jnp.sum(p * dp, axis=-1, keepdims=True))
                db_ref[h, :, c0:KEY_WIN] += ds
                dsb = ds.astype(BF16)
                dq_acc = jnp.where(msk, _dot(dsb, kw, 1, 0), dq_acc)
                dk_acc = jnp.where(msk, _dot(dsb, qb, 0, 0), dk_acc)
                dv_acc = jnp.where(msk, _dot(p.astype(BF16), dob, 0, 0), dv_acc)
            dqn_s[r0:r0 + Q_BLOCK, :] = dq_acc * ATTN_SCALE
            dkn_s[k0:k0 + width, :] += dk_acc
            dv_s[k0:k0 + width, :] += dv_acc
        dq, dgq = _head_norm_bwd(dqn_s[...], qhat, rq, gq_ref[...], masks)
        dk, dgk = _head_norm_bwd(dkn_s[...], khat, rk, gk_ref[...], masks)
        dq_ref[...] = dq.astype(dq_ref.dtype)
        dk_ref[...] = dk.astype(dk_ref.dtype)
        dv_ref[...] = dv_s[...].astype(dv_ref.dtype)
        dgq_ref[...] += dgq
        dgk_ref[...] += dgk

    def col(off):
        return pl.BlockSpec((seq, LANES), lambda p, b: (b, off + p))

    vec = pl.BlockSpec((1, LANES), lambda p, b: (0, 0))
    blk = pl.BlockSpec((2, Q_BLOCK, KEY_WIN), lambda p, b: (p, 0, 0))
    o_shape = jax.ShapeDtypeStruct((n, TOK_WIDTH), BF16)
    v_shape = jax.ShapeDtypeStruct((1, LANES), F32)
    return pl.pallas_call(
        body,
        out_shape=(o_shape, o_shape, o_shape, jax.ShapeDtypeStruct(bias.shape, F32), v_shape, v_shape),
        grid=(pairs, batch),
        in_specs=[col(0), col(pairs), col(2 * pairs), col(0), vec, vec, blk],
        out_specs=(col(0), col(0), col(0), blk, vec, vec),
        scratch_shapes=[pltpu.VMEM((seq, LANES), BF16), pltpu.VMEM((seq, LANES), BF16),
                        pltpu.VMEM((seq, LANES), F32), pltpu.VMEM((seq, LANES), F32), pltpu.VMEM((seq, LANES), F32)],
        compiler_params=_params("arbitrary", "arbitrary"), name="attn_bwd")(z, z, z, dcat, gq2, gk2, bias)


MEM_ROWS = 512


def memattn_fwd(z, mem, mem_gain, wkv, gq4, gk4, cat, batch, seq, qcol, name):
    mtok = mem.shape[0] // batch
    d = mem.shape[1]
    rows = min(MEM_ROWS, seq)

    def body(q_ref, m_ref, mg_ref, w_ref, gq_ref, gk_ref, cat_ref, o_ref, n_ref, kv_ref):
        del cat_ref
        masks = _group_masks(MEM_WIDTH)
        mv = m_ref[...]
        r = lax.rsqrt(jnp.mean(mv * mv, axis=-1, keepdims=True) + NORM_EPS)
        nv = (mv * r * mg_ref[...]).astype(BF16)
        n_ref[...] = nv
        kv_ref[...] = _dot(nv, w_ref[...], 1, 0)
        kn = _head_norm(kv_ref[:, 0:MEM_WIDTH], gk_ref[...], masks)[0].astype(BF16)
        vm = kv_ref[:, MEM_WIDTH:2 * MEM_WIDTH].astype(BF16)
        for t in range(seq // rows):
            sl = slice(t * rows, (t + 1) * rows)
            qs = (_head_norm(q_ref[sl, :].astype(F32), gq_ref[...], masks)[0] * ATTN_SCALE).astype(BF16)
            out = jnp.zeros((rows, MEM_WIDTH), F32)
            for msk in masks:
                qh = jnp.where(msk, qs, jnp.zeros_like(qs))
                p = _softmax_rows(_dot(qh, kn, 1, 1)).astype(BF16)
                out = jnp.where(msk, _dot(p, vm, 1, 0), out)
            o_ref[sl, :] = out.astype(o_ref.dtype)

    vec = pl.BlockSpec((1, MEM_WIDTH), lambda b: (0, 0))
    mem_spec = pl.BlockSpec((mtok, d), lambda b: (b, 0))
    kv_spec = pl.BlockSpec((mtok, 2 * MEM_WIDTH), lambda b: (b, 0))
    return pl.pallas_call(
        body, out_shape=(jax.ShapeDtypeStruct(cat.shape, cat.dtype), jax.ShapeDtypeStruct(mem.shape, BF16),
                         jax.ShapeDtypeStruct((mem.shape[0], 2 * MEM_WIDTH), F32)), grid=(batch,),
        in_specs=[pl.BlockSpec((seq, MEM_WIDTH), lambda b: (b, qcol)), mem_spec, pl.BlockSpec((1, d), lambda b: (0, 0)),
                  pl.BlockSpec(wkv.shape, lambda b: (0, 0)), vec, vec, ANY],
        out_specs=(pl.BlockSpec((seq, MEM_WIDTH), lambda b: (b, TOK_WIDTH // MEM_WIDTH)), mem_spec, kv_spec),
        input_output_aliases={6: 0},
        compiler_params=_params("parallel"), name=name)(z, mem, mem_gain, wkv, gq4, gk4, cat)


def memattn_bwd(z, kv, dcat, gq4, gk4, mem, mem_n, wkv, dz, batch, seq, qcol, name):
    mtok = kv.shape[0] // batch
    d = mem.shape[1]
    rows = min(MEM_ROWS, seq)

    def body(q_ref, kv_ref, do_ref, gq_ref, gk_ref, m_ref, n_ref, w_ref, *rest):
        dq_ref, dgq_ref, dgk_ref, dw_ref, dmg_ref, dw_acc = rest[-6:]

        @pl.when(pl.program_id(0) == 0)
        def _():
            dgq_ref[...] = jnp.zeros_like(dgq_ref)
            dgk_ref[...] = jnp.zeros_like(dgk_ref)
            dmg_ref[...] = jnp.zeros_like(dmg_ref)
            dw_acc[...] = jnp.zeros_like(dw_acc)

        masks = _group_masks(MEM_WIDTH)
        kn_f, khat, rk = _head_norm(kv_ref[:, 0:MEM_WIDTH], gk_ref[...], masks)
        kn = kn_f.astype(BF16)
        vm = kv_ref[:, MEM_WIDTH:2 * MEM_WIDTH].astype(BF16)
        dkn = jnp.zeros((mtok, MEM_WIDTH), F32)
        dvm = jnp.zeros((mtok, MEM_WIDTH), F32)
        dgq = jnp.zeros((1, MEM_WIDTH), F32)
        for t in range(seq // rows):
            sl = slice(t * rows, (t + 1) * rows)
            qn_f, qhat, rq = _head_norm(q_ref[sl, :].astype(F32), gq_ref[...], masks)
            qs = (qn_f * ATTN_SCALE).astype(BF16)
            dob = do_ref[sl, :]
            dqn = jnp.zeros((rows, MEM_WIDTH), F32)
            for msk in masks:
                qh = jnp.where(msk, qs, jnp.zeros_like(qs))
                doh = jnp.where(msk, dob, jnp.zeros_like(dob))
                p = _softmax_rows(_dot(qh, kn, 1, 1))
                dp = _dot(doh, vm, 1, 1)
                ds = p * (dp - jnp.sum(p * dp, axis=-1, keepdims=True))
                dsb = ds.astype(BF16)
                dqn = jnp.where(msk, _dot(dsb, kn, 1, 0), dqn)
                dkn = dkn + jnp.where(msk, _dot(dsb, qs, 0, 0), 0.0)
                dvm = dvm + jnp.where(msk, _dot(p.astype(BF16), dob, 0, 0), 0.0)
            dq, dg = _head_norm_bwd(dqn * ATTN_SCALE, qhat, rq, gq_ref[...], masks)
            dq_ref[sl, :] = dq.astype(dq_ref.dtype)
            dgq = dgq + dg
        dk, dgk = _head_norm_bwd(dkn, khat, rk, gk_ref[...], masks)
        dgq_ref[...] += dgq
        dgk_ref[...] += dgk
        dkv_b = jnp.concatenate([dk, dvm], axis=-1).astype(BF16)
        dw_acc[...] += _dot(n_ref[...], dkv_b, 0, 0)
        dn = _dot(dkv_b, w_ref[...], 1, 1)
        mv = m_ref[...]
        rm = lax.rsqrt(jnp.mean(mv * mv, axis=-1, keepdims=True) + NORM_EPS)
        dmg_ref[...] += jnp.sum(dn * (mv * rm), axis=0, keepdims=True)

        @pl.when(pl.program_id(0) == batch - 1)
        def _():
            dw_ref[...] = dw_acc[...].astype(dw_ref.dtype)

    vec = pl.BlockSpec((1, MEM_WIDTH), lambda b: (0, 0))
    kv_spec = pl.BlockSpec((mtok, 2 * MEM_WIDTH), lambda b: (b, 0))
    mem_spec = pl.BlockSpec((mtok, d), lambda b: (b, 0))
    w_spec = pl.BlockSpec(wkv.shape, lambda b: (0, 0))
    v_shape = jax.ShapeDtypeStruct((1, MEM_WIDTH), F32)
    q_spec = pl.BlockSpec((seq, MEM_WIDTH), lambda b: (b, qcol))
    in_specs = [q_spec, kv_spec, pl.BlockSpec((seq, MEM_WIDTH), lambda b: (b, TOK_WIDTH // MEM_WIDTH)), vec, vec,
                mem_spec, mem_spec, w_spec]
    args = [z, kv, dcat, gq4, gk4, mem, mem_n, wkv]
    if dz is None:
        dq_shape, dq_spec, aliases = jax.ShapeDtypeStruct((z.shape[0], MEM_WIDTH), BF16), \
            pl.BlockSpec((seq, MEM_WIDTH), lambda b: (b, 0)), {}
    else:
        dq_shape, dq_spec, aliases = jax.ShapeDtypeStruct(dz.shape, dz.dtype), q_spec, {len(args): 0}
        in_specs.append(ANY)
        args.append(dz)
    return pl.pallas_call(
        body,
        out_shape=(dq_shape, v_shape, v_shape, jax.ShapeDtypeStruct(wkv.shape, BF16), jax.ShapeDtypeStruct((1, d), F32)),
        grid=(batch,), in_specs=in_specs,
        out_specs=(dq_spec, vec, vec, w_spec, pl.BlockSpec((1, d), lambda b: (0, 0))),
        scratch_shapes=[pltpu.VMEM(wkv.shape, F32)], input_output_aliases=aliases,
        compiler_params=_params("arbitrary"), name=name)(*args)


CONV_ROWS = 256


def _glu(a_ref, g_ref):
    return a_ref[...].astype(F32) * _sigmoid(g_ref[...].astype(F32))


def _layer_norm_stats(y):
    mu = jnp.mean(y, axis=-1, keepdims=True)
    yc = y - mu
    rstd = lax.rsqrt(jnp.mean(yc * yc, axis=-1, keepdims=True) + NORM_EPS)
    return yc * rstd, rstd


CONV_WIN = CONV_HALO + CONV_ROWS
SUBLANES = 8
SHIFT_ROWS = CONV_WIN - SUBLANES


def _preshift(win, shifted):
    for s in range(1, SUBLANES):
        shifted[s - 1, :, :] = win[s:s + SHIFT_ROWS, :]


TAP_ROWS = 64
TAP_TILES = [(r0, slice(c0, c0 + LANES)) for c0 in range(0, TOK_WIDTH, LANES) for r0 in range(0, CONV_ROWS, TAP_ROWS)]


def _tap(win, shifted, off, r0, lanes):
    s = off % SUBLANES
    base = off - s + r0
    if s == 0:
        return win[base:base + TAP_ROWS, lanes]
    return shifted[s - 1, base:base + TAP_ROWS, lanes]


def _fold_rows(x):
    return jnp.sum(x.reshape(TAP_ROWS // SUBLANES, SUBLANES, LANES), axis=0)


def conv_fwd(z, cw, cb, lg, lb, batch, seq):
    n = z.shape[0]
    nt = seq // CONV_ROWS
    sub = CONV_ROWS // CONV_HALO
    lead = CONV_HALO - (CONV_W - 1)

    def body(a_ref, g_ref, ap_ref, gp_ref, cw_ref, cb_ref, lg_ref, lb_ref, o_ref, y_ref, win, shifted):
        first = pl.program_id(1) == 0
        win[0:CONV_HALO, :] = jnp.where(first, 0.0, _glu(ap_ref, gp_ref))
        win[CONV_HALO:CONV_WIN, :] = _glu(a_ref, g_ref)
        _preshift(win, shifted)
        for r0, lanes in TAP_TILES:
            acc = jnp.zeros((TAP_ROWS, LANES), F32) + cb_ref[:, lanes]
            for w in range(CONV_W):
                acc = acc + _tap(win, shifted, lead + w, r0, lanes) * cw_ref[w:w + 1, lanes]
            y_ref[r0:r0 + TAP_ROWS, lanes] = acc
        yh, _ = _layer_norm_stats(y_ref[...])
        t = yh * lg_ref[...] + lb_ref[...]
        o_ref[...] = (t * _sigmoid(t)).astype(o_ref.dtype)

    def cur(c):
        return pl.BlockSpec((CONV_ROWS, TOK_WIDTH), lambda b, i: (b * nt + i, c))

    def prev(c):
        return pl.BlockSpec((CONV_HALO, TOK_WIDTH), lambda b, i: (jnp.maximum((b * nt + i) * sub - 1, 0), c))

    vec = pl.BlockSpec((1, TOK_WIDTH), lambda b, i: (0, 0))
    return pl.pallas_call(
        body, out_shape=(jax.ShapeDtypeStruct((n, D_MODEL), BF16), jax.ShapeDtypeStruct((n, TOK_WIDTH), F32)),
        grid=(batch, nt),
        in_specs=[cur(0), cur(1), prev(0), prev(1), pl.BlockSpec((32, TOK_WIDTH), lambda b, i: (0, 0)), vec, vec, vec],
        out_specs=(cur(0), cur(0)),
        scratch_shapes=[pltpu.VMEM((CONV_WIN, TOK_WIDTH), F32), pltpu.VMEM((SUBLANES - 1, SHIFT_ROWS, TOK_WIDTH), F32)],
        compiler_params=_params("parallel", "arbitrary"), name="conv_fwd")(z, z, z, z, cw, cb, lg, lb)


def conv_bwd(z, y, dcat, cw, lg, lb, batch, seq):
    n = z.shape[0]
    nt = seq // CONV_ROWS
    sub = CONV_ROWS // CONV_HALO
    lead = CONV_HALO - (CONV_W - 1)
    last_blk = n // CONV_HALO - 1

    def body(a_ref, g_ref, ap_ref, gp_ref, y_ref, yn_ref, do_ref, don_ref, cw_ref, lg_ref, lb_ref,
             dz_ref, dcw_ref, dsm_ref, win, shifted, dyw, dshifted, dg_o):
        b, i, which = pl.program_id(0), pl.program_id(1), pl.program_id(2)

        @pl.when(which == 0)
        def _():
            first, last = i == 0, i == nt - 1

            @pl.when((b == 0) & (i == 0))
            def _():
                dcw_ref[...] = jnp.zeros_like(dcw_ref)
                dsm_ref[...] = jnp.zeros_like(dsm_ref)

            win[0:CONV_HALO, :] = jnp.where(first, 0.0, _glu(ap_ref, gp_ref))
            win[CONV_HALO:CONV_WIN, :] = _glu(a_ref, g_ref)
            _preshift(win, shifted)
            yv = jnp.concatenate([y_ref[...], yn_ref[...]], axis=0)
            yh, rstd = _layer_norm_stats(yv)
            t = yh * lg_ref[...] + lb_ref[...]
            st = _sigmoid(t)
            dout = jnp.concatenate(
                [do_ref[...].astype(F32), jnp.where(last, 0.0, don_ref[...].astype(F32))], axis=0)
            dt = dout * st * (1.0 + t * (1.0 - st))
            dyh = dt * lg_ref[...]
            dy = rstd * (dyh - jnp.mean(dyh, axis=-1, keepdims=True)
                         - yh * jnp.mean(dyh * yh, axis=-1, keepdims=True))
            dyw[...] = dy
            _preshift(dyw, dshifted)
            dsm_ref[0:1, :] += jnp.sum(dy[0:CONV_ROWS], axis=0, keepdims=True)
            dsm_ref[1:2, :] += jnp.sum((dt * yh)[0:CONV_ROWS], axis=0, keepdims=True)
            dsm_ref[2:3, :] += jnp.sum(dt[0:CONV_ROWS], axis=0, keepdims=True)
            for c0 in range(0, TOK_WIDTH, LANES):
                lanes = slice(c0, c0 + LANES)
                dcw_acc = [jnp.zeros((SUBLANES, LANES), F32) for _ in range(CONV_W)]
                for r0 in range(0, CONV_ROWS, TAP_ROWS):
                    dyt = dyw[r0:r0 + TAP_ROWS, lanes]
                    dglu = jnp.zeros((TAP_ROWS, LANES), F32)
                    for w in range(CONV_W):
                        dcw_acc[w] = dcw_acc[w] + _fold_rows(dyt * _tap(win, shifted, lead + w, r0, lanes))
                        dglu = dglu + _tap(dyw, dshifted, CONV_W - 1 - w, r0, lanes) * cw_ref[w:w + 1, lanes]
                    avt = a_ref[r0:r0 + TAP_ROWS, lanes].astype(F32)
                    sgt = _sigmoid(g_ref[r0:r0 + TAP_ROWS, lanes].astype(F32))
                    dz_ref[r0:r0 + TAP_ROWS, lanes] = (dglu * sgt).astype(dz_ref.dtype)
                    dg_o[r0:r0 + TAP_ROWS, lanes] = (dglu * avt * sgt * (1.0 - sgt)).astype(dg_o.dtype)
                for w in range(CONV_W):
                    dcw_ref[w:w + 1, lanes] += jnp.sum(dcw_acc[w], axis=0, keepdims=True)

        @pl.when(which == 1)
        def _():
            dz_ref[...] = dg_o[...]

    def ahead(b, i, t):
        return jnp.minimum(b * nt + i + t, batch * nt - 1)

    def cur(c):
        return pl.BlockSpec((CONV_ROWS, TOK_WIDTH), lambda b, i, t: (ahead(b, i, t), c))

    def prev(c):
        return pl.BlockSpec((CONV_HALO, TOK_WIDTH), lambda b, i, t: (jnp.maximum(ahead(b, i, t) * sub - 1, 0), c))

    nxt = pl.BlockSpec((CONV_HALO, TOK_WIDTH),
                       lambda b, i, t: (jnp.minimum((ahead(b, i, t) + 1) * sub, last_blk), 0))
    vec = pl.BlockSpec((1, TOK_WIDTH), lambda b, i, t: (0, 0))
    full32 = pl.BlockSpec((32, TOK_WIDTH), lambda b, i, t: (0, 0))
    return pl.pallas_call(
        body,
        out_shape=(jax.ShapeDtypeStruct(z.shape, BF16), jax.ShapeDtypeStruct((32, TOK_WIDTH), F32),
                   jax.ShapeDtypeStruct((8, TOK_WIDTH), F32)),
        grid=(batch, nt, 2),
        in_specs=[cur(0), cur(1), prev(0), prev(1), cur(0), nxt, cur(0), nxt, full32, vec, vec],
        out_specs=(pl.BlockSpec((CONV_ROWS, TOK_WIDTH), lambda b, i, t: (b * nt + i, t)), full32,
                   pl.BlockSpec((8, TOK_WIDTH), lambda b, i, t: (0, 0))),
        scratch_shapes=[pltpu.VMEM((CONV_WIN, TOK_WIDTH), F32), pltpu.VMEM((SUBLANES - 1, SHIFT_ROWS, TOK_WIDTH), F32),
                        pltpu.VMEM((CONV_WIN, TOK_WIDTH), F32), pltpu.VMEM((SUBLANES - 1, SHIFT_ROWS, TOK_WIDTH), F32),
                        pltpu.VMEM((CONV_ROWS, TOK_WIDTH), BF16)],
        compiler_params=_params("arbitrary", "arbitrary", "arbitrary"), name="conv_bwd")(
            z, z, z, z, y, y, dcat, dcat, cw, lg, lb)


def _place():
    return lax.axis_index("x"), lax.axis_index("y"), lax.axis_index("c")


def _other_chips(x, y):
    return [(1 - x, y), (x, 1 - y), (1 - x, 1 - y)]


def reduce_small(arrays):
    na = len(arrays)

    def body(*refs):
        ins, outs, bufs = refs[:na], refs[na:2 * na], refs[2 * na:3 * na]
        send_sems, recv_sems = refs[3 * na:]
        x, y, c = _place()
        me = 4 * x + 2 * y + c
        copies = []
        for a in range(na):
            bufs[a][me] = ins[a][...]
            for k in range(1, N_DEV):
                cp = pltpu.make_async_remote_copy(
                    src_ref=ins[a], dst_ref=bufs[a].at[me], send_sem=send_sems.at[a, k - 1],
                    recv_sem=recv_sems.at[a, k - 1],
                    device_id=(x ^ (k >> 2), y ^ ((k >> 1) & 1), c ^ (k & 1)), device_id_type=MESH)
                cp.start()
                copies.append(cp)
        for a in range(na):
            for k in range(1, N_DEV):
                src = 4 * (x ^ (k >> 2)) + 2 * (y ^ ((k >> 1) & 1)) + (c ^ (k & 1))
                pltpu.make_async_remote_copy(
                    src_ref=ins[a], dst_ref=bufs[a].at[src], send_sem=send_sems.at[a, k - 1],
                    recv_sem=recv_sems.at[a, k - 1], device_id=(x, y, c), device_id_type=MESH).wait_recv()
        for cp in copies:
            cp.wait_send()
        for a in range(na):
            total = bufs[a][0]
            for dev in range(1, N_DEV):
                total = total + bufs[a][dev]
            outs[a][...] = total

    vmem = pl.BlockSpec(memory_space=pltpu.VMEM)
    return pl.pallas_call(
        body, out_shape=tuple(jax.ShapeDtypeStruct(a.shape, F32) for a in arrays),
        in_specs=[vmem] * na, out_specs=tuple([vmem] * na),
        scratch_shapes=[pltpu.VMEM((N_DEV,) + a.shape, F32) for a in arrays]
        + [pltpu.SemaphoreType.DMA((na, N_DEV - 1)), pltpu.SemaphoreType.DMA((na, N_DEV - 1))],
        compiler_params=pltpu.CompilerParams(vmem_limit_bytes=VMEM_LIMIT), name="small_reduce")(*arrays)


def adamw_small(ws, gs, ms, vs):
    na = len(ws)
    c1 = 1.0 / (1.0 - ADAM_B1 ** ADAM_STEP)
    c2 = 1.0 / (1.0 - ADAM_B2 ** ADAM_STEP)

    def body(*refs):
        w_refs, g_refs, m_refs, v_refs = (refs[i * na:(i + 1) * na] for i in range(4))
        d_refs, nm_refs, nv_refs = (refs[(4 + i) * na:(5 + i) * na] for i in range(3))
        for a in range(na):
            gv = g_refs[a][...]
            nm = ADAM_B1 * m_refs[a][...] + (1.0 - ADAM_B1) * gv
            nv = ADAM_B2 * v_refs[a][...] + (1.0 - ADAM_B2) * (gv * gv)
            nm_refs[a][...] = nm
            nv_refs[a][...] = nv
            d_refs[a][...] = -ADAM_LR * ((nm * c1) / (jnp.sqrt(nv * c2) + ADAM_EPS) + ADAM_WD * w_refs[a][...])

    vmem = pl.BlockSpec(memory_space=pltpu.VMEM)
    shapes = tuple(jax.ShapeDtypeStruct(w.shape, F32) for w in ws)
    outs = pl.pallas_call(
        body, out_shape=shapes * 3, in_specs=[vmem] * (4 * na), out_specs=tuple([vmem] * (3 * na)),
        compiler_params=pltpu.CompilerParams(vmem_limit_bytes=VMEM_LIMIT), name="adamw_small")(*ws, *gs, *ms, *vs)
    return outs[:na], outs[na:2 * na], outs[2 * na:]


def gather_weights(shards, name, collective_id):
    nw = len(shards)
    ns = [s.shape[0] for s in shards]
    in_refs = [jax.new_ref(s, memory_space=pltpu.MemorySpace.HBM) for s in shards]
    out_refs = [jax.empty_ref(jax.ShapeDtypeStruct((N_DEV * s.shape[0], s.shape[1]), s.dtype),
                              memory_space=pltpu.MemorySpace.HBM) for s in shards]

    @pl.kernel(mesh=plsc.ScalarSubcoreMesh(axis_name="seq", num_cores=1), name=name,
               scratch_types=(pltpu.SemaphoreType.DMA((nw, 7)), pltpu.SemaphoreType.DMA((nw, 7)),
                              pltpu.SemaphoreType.DMA((nw,))),
               compiler_params=pltpu.CompilerParams(collective_id=collective_id))
    def launch(send_sems, recv_sems, local_sems):
        x, y, c = _place()
        me, sib = (x, y, c), (x, y, 1 - c)
        chips = _other_chips(x, y)
        barrier = pltpu.get_barrier_semaphore()
        for peer in [sib] + [(*chip, c) for chip in chips]:
            pl.semaphore_signal(barrier, inc=1, device_id=peer, device_id_type=MESH)
        pl.semaphore_wait(barrier, 4)

        def rows(w, dev):
            return out_refs[w].at[pl.ds((4 * dev[0] + 2 * dev[1] + dev[2]) * ns[w], ns[w]), :]

        def copy(w, k, block, to, src=None):
            return pltpu.make_async_remote_copy(
                src_ref=rows(w, block) if src is None else src, dst_ref=rows(w, block),
                send_sem=send_sems.at[w, k], recv_sem=recv_sems.at[w, k], device_id=to, device_id_type=MESH)

        started, sends = [], []
        for w in range(nw):
            mine = pltpu.make_async_copy(in_refs[w], rows(w, me), local_sems.at[w])
            mine.start()
            started.append(mine)
            first = [copy(w, 0, me, sib, src=in_refs[w])]
            first += [copy(w, 1 + j, me, (*chip, c), src=in_refs[w]) for j, chip in enumerate(chips)]
            for cp in first:
                cp.start()
            sends += first
        for w in range(nw):
            for j, chip in enumerate(chips):
                copy(w, 1 + j, (*chip, c), me).wait_recv()
                fwd = copy(w, 4 + j, (*chip, c), sib)
                fwd.start()
                sends.append(fwd)
        for w in range(nw):
            copy(w, 0, sib, me).wait_recv()
            for j, chip in enumerate(chips):
                copy(w, 4 + j, (*chip, 1 - c), me).wait_recv()
        for cp in sends:
            cp.wait_send()
        for mine in started:
            mine.wait()

    launch()
    return [r[...] for r in out_refs]


def _sequencer_exchange(sources, out_rows, peers_of, copies_of, name, collective_id):
    nw = len(sources)
    in_refs = [jax.new_ref(s, memory_space=pltpu.MemorySpace.HBM) for s in sources]
    out_refs = [jax.empty_ref(jax.ShapeDtypeStruct((rows, s.shape[1]), s.dtype), memory_space=pltpu.MemorySpace.HBM)
                for rows, s in zip(out_rows, sources)]
    per = len(copies_of(0, 0, 0, 0))

    @pl.kernel(mesh=plsc.ScalarSubcoreMesh(axis_name="seq", num_cores=1), name=name,
               scratch_types=(pltpu.SemaphoreType.DMA((nw, per)), pltpu.SemaphoreType.DMA((nw, per))),
               compiler_params=pltpu.CompilerParams(collective_id=collective_id))
    def launch(send_sems, recv_sems):
        x, y, c = _place()
        peers = peers_of(x, y, c)
        barrier = pltpu.get_barrier_semaphore()
        for peer in peers:
            pl.semaphore_signal(barrier, inc=1, device_id=peer, device_id_type=MESH)
        pl.semaphore_wait(barrier, len(peers))
        copies = []
        for w in range(nw):
            for k, (src_blk, dst_blk, rows, peer) in enumerate(copies_of(x, y, c, w)):
                cp = pltpu.make_async_remote_copy(
                    src_ref=in_refs[w].at[pl.ds(src_blk * rows, rows), :],
                    dst_ref=out_refs[w].at[pl.ds(dst_blk * rows, rows), :],
                    send_sem=send_sems.at[w, k], recv_sem=recv_sems.at[w, k], device_id=peer, device_id_type=MESH)
                cp.start()
                copies.append(cp)
        for cp in copies:
            cp.wait_recv()
        for cp in copies:
            cp.wait_send()

    launch()
    return [r[...] for r in out_refs]


def scatter_to_sibling(grads, name, collective_id):
    ns = [g.shape[0] // N_DEV for g in grads]
    return _sequencer_exchange(
        grads, [4 * n for n in ns],
        lambda x, y, c: [(x, y, 1 - c)],
        lambda x, y, c, w: [(2 * q + 1 - c, q, ns[w], (x, y, 1 - c)) for q in range(4)],
        name, collective_id)


def scatter_to_chips(parts, name, collective_id):
    ns = [p.shape[0] // 4 for p in parts]
    return _sequencer_exchange(
        parts, [3 * n for n in ns],
        lambda x, y, c: [(*chip, c) for chip in _other_chips(x, y)],
        lambda x, y, c, w: [(2 * chip[0] + chip[1], j, ns[w], (*chip, c)) for j, chip in enumerate(_other_chips(x, y))],
        name, collective_id)


def add_sibling(grads, landeds, core, name):
    nw = len(grads)

    def body(c_ref, *refs):
        for w in range(nw):
            g_ref, l_ref, o_ref = refs[2 * w], refs[2 * w + 1], refs[2 * nw + w]
            o_ref[...] = (g_ref[...].astype(F32) + l_ref[...].astype(F32)).astype(o_ref.dtype)

    in_specs, out_specs, args = [], [], []
    for g, ld in zip(grads, landeds):
        n, cols = ld.shape[0] // 4, g.shape[1]
        in_specs += [pl.BlockSpec((n, cols), lambda q, c_ref: (2 * q + c_ref[0], 0)),
                     pl.BlockSpec((n, cols), lambda q, c_ref: (q, 0))]
        out_specs.append(pl.BlockSpec((n, cols), lambda q, c_ref: (q, 0)))
        args += [g, ld]
    grid_spec = pltpu.PrefetchScalarGridSpec(
        num_scalar_prefetch=1, grid=(4,), in_specs=in_specs, out_specs=tuple(out_specs))
    return pl.pallas_call(
        body, out_shape=tuple(jax.ShapeDtypeStruct(ld.shape, ld.dtype) for ld in landeds), grid_spec=grid_spec,
        compiler_params=_params("arbitrary"), name=name)(core, *args)


ADAMW_HALVES = 2


def adamw_shards(items, chip, name):
    c1 = 1.0 / (1.0 - ADAM_B1 ** ADAM_STEP)
    c2 = 1.0 / (1.0 - ADAM_B2 ** ADAM_STEP)
    ni = len(items)

    def body(q_ref, *refs):
        outs = refs[len(refs) - 4 * ni:]
        for k in range(ni):
            w_ref, m_ref, v_ref, p_ref, l0_ref, l1_ref, l2_ref = refs[7 * k:7 * k + 7]
            g_ref, d_ref, nm_ref, nv_ref = outs[4 * k:4 * k + 4]
            gv = ((p_ref[...].astype(F32) + l0_ref[...].astype(F32)) + l1_ref[...].astype(F32)) + l2_ref[...].astype(F32)
            nm = ADAM_B1 * m_ref[...] + (1.0 - ADAM_B1) * gv
            nv = ADAM_B2 * v_ref[...] + (1.0 - ADAM_B2) * (gv * gv)
            g_ref[...] = gv
            nm_ref[...] = nm
            nv_ref[...] = nv
            d_ref[...] = -ADAM_LR * ((nm * c1) / (jnp.sqrt(nv * c2) + ADAM_EPS) + ADAM_WD * w_ref[...])

    sub = ADAMW_HALVES
    in_specs, out_specs, out_shape, args, donated = [], [], [], [chip], []
    for layer, w, m, v, part, landed, earlier in items:
        rows, cols = landed.shape[0] // (3 * sub), w.shape[1]

        def block(first, rows=rows, cols=cols):
            return pl.BlockSpec((rows, cols), lambda i, q_ref: (first(q_ref) * sub + i, 0))

        own = block(lambda q_ref, layer=layer: layer)
        in_specs += [own, own, own, block(lambda q_ref: q_ref[0])] + [block(lambda q_ref, j=j: j) for j in range(3)]
        args += [w, m, v, part, landed, landed, landed]
        out_specs += [own] * 4
        out_shape += [jax.ShapeDtypeStruct(w.shape, F32)] * 4
        donated.append(earlier)
    aliases = {}
    for k, earlier in enumerate(donated):
        if earlier is not None:
            for j in range(4):
                aliases[len(args)] = 4 * k + j
                in_specs.append(ANY)
                args.append(earlier[j])
    grid_spec = pltpu.PrefetchScalarGridSpec(
        num_scalar_prefetch=1, grid=(sub,), in_specs=in_specs, out_specs=tuple(out_specs))
    outs = pl.pallas_call(
        body, out_shape=tuple(out_shape), grid_spec=grid_spec, input_output_aliases=aliases,
        compiler_params=_params("arbitrary"), name=name)(*args)
    return [tuple(outs[4 * k:4 * k + 4]) for k in range(ni)]


def _pack(arrays):
    flat = jnp.concatenate([a.reshape(-1).astype(F32) for a in arrays])
    pad = (-flat.shape[0]) % (8 * LANES)
    return jnp.pad(flat, (0, pad)).reshape(-1, LANES)


def _unpack(slab, shapes):
    flat = slab.reshape(slab.shape[:-2] + (-1,))
    out, off = [], 0
    for shp in shapes:
        size = 1
        for s in shp:
            size *= s
        out.append(flat[..., off:off + size].reshape(flat.shape[:-1] + tuple(shp)))
        off += size
    return out


def kernel(x, mem, norm1_g, mem_norm_g, a_w_in, a_q_g, a_k_g, a_rel_bias, b_w_in, b_b_in, b_conv_w, b_conv_b, b_ln_g, b_ln_b, mq_g, mk_g, w_mem_kv, w_out, norm2_g, w_gate, w_up, w_down, loss_target, m_norm1_g, m_mem_norm_g, m_a_w_in, m_a_q_g, m_a_k_g, m_a_rel_bias, m_b_w_in, m_b_b_in, m_b_conv_w, m_b_conv_b, m_b_ln_g, m_b_ln_b, m_mq_g, m_mk_g, m_w_mem_kv, m_w_out, m_norm2_g, m_w_gate, m_w_up, m_w_down, v_norm1_g, v_mem_norm_g, v_a_w_in, v_a_q_g, v_a_k_g, v_a_rel_bias, v_b_w_in, v_b_b_in, v_b_conv_w, v_b_conv_b, v_b_ln_g, v_b_ln_b, v_mq_g, v_mk_g, v_w_mem_kv, v_w_out, v_norm2_g, v_w_gate, v_w_up, v_w_down):
    batch, seq, d = x.shape
    mtok = mem.shape[1]
    n = batch * seq
    ax, ay, ac = _place()
    me = 4 * ax + 2 * ay + ac
    core_arr = jnp.reshape(ac, (1,)).astype(jnp.int32)
    chip_arr = jnp.reshape(2 * ax + ay, (1,)).astype(jnp.int32)

    def t_bf16(w):
        return jnp.transpose(w).astype(BF16)

    def after(value, *earlier):
        return lax.optimization_barrier((value, *earlier))[0]

    def gather_mix(l, when, name, collective_id):
        srcs = [w_mem_kv[l].astype(BF16), w_out[l].astype(BF16)]
        if l == 1:
            srcs += [t_bf16(b_w_in[0]), _pack([b_b_in, b_conv_w, b_conv_b, b_ln_g, b_ln_b])]
        return gather_weights([after(srcs[0], *when)] + srcs[1:], name, collective_id)

    def gather_ffn(l, when, name, collective_id):
        return gather_weights(
            [after(t_bf16(w_gate[l]), *when), t_bf16(w_up[l]), w_down[l].astype(BF16)], name, collective_id)

    f_loc = b_b_in.shape[1]
    c_loc = b_conv_b.shape[1]

    def two(g):
        return jnp.concatenate([g, g], axis=-1)

    gq2, gk2 = two(a_q_g), two(a_k_g)
    rel16 = jnp.pad(a_rel_bias[0], ((0, 16 - a_rel_bias.shape[1]), (0, 0)))
    bias = bias_blocks(rel16)

    x0 = x.reshape(n, d)
    mem2 = mem.reshape(batch * mtok, d)

    saved = []
    xin = x0
    a_win_t, = gather_weights([t_bf16(a_w_in[0])], "gather_in_a", 1)
    wg_t, wu_t, wd, wo, wkv = [None] * 2, [None] * 2, [None] * 2, [None] * 2, [None] * 2
    h = after(rms_fwd(xin, norm1_g[0:1], name="rms1_fwd_0"), bias)
    target = loss_target.reshape(n, d)
    for l in range(2):
        gq4 = jnp.tile(mq_g[l:l + 1], (1, 4))
        gk4 = jnp.tile(mk_g[l:l + 1], (1, 4))
        y_conv = None
        if l == 0:
            wkv[0], wo[0] = gather_mix(0, (h, a_win_t), "gather_mix_a", 2)
            z = mm_nt(h, a_win_t, name="in_proj_a")
            wg_t[0], wu_t[0], wd[0] = gather_ffn(0, (z, wkv[0]), "gather_ffn_a", 3)
            cat = attn_fwd(z, gq2, gk2, bias, batch, seq)
            wkv[1], wo[1], b_win_t, conv_slabs = gather_mix(1, (cat, wg_t[0]), "gather_mix_b", 4)
            qcol = 3 * TOK_WIDTH // MEM_WIDTH
        else:
            small_shapes = [(f_loc,), (CONV_W, c_loc), (c_loc,), (c_loc,), (c_loc,)]
            bb_g, cw_g, cb_g, lg_g, lb_g = _unpack(conv_slabs.reshape(N_DEV, -1, LANES), small_shapes)
            bb_full = bb_g.reshape(1, -1)
            cw_full = jnp.pad(jnp.transpose(cw_g, (1, 0, 2)).reshape(CONV_W, -1), ((0, 32 - CONV_W), (0, 0)))
            cb_full, lg_full, lb_full = cb_g.reshape(1, -1), lg_g.reshape(1, -1), lb_g.reshape(1, -1)
            z = mm_nt(h, b_win_t, bias=bb_full, name="in_proj_b")
            cat, y_conv = conv_fwd(z, cw_full, cb_full, lg_full, lb_full, batch, seq)
            qcol = 2 * TOK_WIDTH // MEM_WIDTH
        cat, mem_n, kv = memattn_fwd(
            z, mem2, mem_norm_g[l:l + 1], wkv[l], gq4, gk4, cat, batch, seq, qcol, name=f"memattn_fwd_{l}")
        x1, h2 = proj_norm(cat, wo[l], xin, norm2_g[l:l + 1], name=f"out_proj_{l}")
        if l == 0:
            wg_t[1], wu_t[1], wd[1] = gather_ffn(1, (x1, b_win_t), "gather_ffn_b", 5)
        if l == 0:
            gate, up, act, x2, h_next = ffn_fwd(h2, wg_t[0], wu_t[0], wd[0], x1, gain=norm1_g[1:2], name="ffn_fwd_0")
        else:
            gate, up, act, dx_b, loss_blk = ffn_fwd(h2, wg_t[1], wu_t[1], wd[1], x1, target=target, name="ffn_fwd_1")
        saved.append(dict(xin=xin, h=h, mem_n=mem_n, kv=kv, gq4=gq4, gk4=gk4, z=z, qcol=qcol, cat=cat, x1=x1, h2=h2,
                          gate=gate, up=up, act=act, y_conv=y_conv))
        if l == 0:
            xin, h = x2, h_next

    big = {}
    small = {}
    reduced = {}
    groups = 0

    def scatter_siblings(keys):
        nonlocal groups
        gid = groups
        groups += 1
        return gid, keys, scatter_to_sibling([big[k] for k in keys], f"scatter_sibling_{gid}", 8 + 2 * gid)

    def scatter_chips(stage1, when):
        gid, keys, landed1 = stage1
        parts = add_sibling([after(big[keys[0]], when)] + [big[k] for k in keys[1:]], landed1, core_arr,
                            name=f"add_sibling_{gid}")
        landed2 = scatter_to_chips(parts, f"scatter_chips_{gid}", 9 + 2 * gid)
        for k, p, ld in zip(keys, parts, landed2):
            reduced[k] = (p, ld)
        return parts, landed2

    def rows_of(w, transposed):
        w = jnp.swapaxes(w, 1, 2) if transposed else w
        return w.reshape(w.shape[0] * w.shape[1], w.shape[2])

    sharded = {
        "win0": (2, True), "win1": (6, True), "wkv": (14, False), "wo": (15, False),
        "wg": (17, True), "wu": (18, True), "wd": (19, False)}
    weights = [norm1_g, mem_norm_g, a_w_in, a_q_g, a_k_g, a_rel_bias, b_w_in, b_b_in, b_conv_w, b_conv_b, b_ln_g,
               b_ln_b, mq_g, mk_g, w_mem_kv, w_out, norm2_g, w_gate, w_up, w_down]
    moms = [m_norm1_g, m_mem_norm_g, m_a_w_in, m_a_q_g, m_a_k_g, m_a_rel_bias, m_b_w_in, m_b_b_in, m_b_conv_w,
            m_b_conv_b, m_b_ln_g, m_b_ln_b, m_mq_g, m_mk_g, m_w_mem_kv, m_w_out, m_norm2_g, m_w_gate, m_w_up, m_w_down]
    vels = [v_norm1_g, v_mem_norm_g, v_a_w_in, v_a_q_g, v_a_k_g, v_a_rel_bias, v_b_w_in, v_b_b_in, v_b_conv_w,
            v_b_conv_b, v_b_ln_g, v_b_ln_b, v_mq_g, v_mk_g, v_w_mem_kv, v_w_out, v_norm2_g, v_w_gate, v_w_up, v_w_down]
    updated = {}

    def update_layer(l, when):
        for group, keys in (("ffn", ("wg", "wu", "wd")), ("mix", (f"win{l}", "wkv", "wo"))):
            items = []
            for key in keys:
                idx, transposed = sharded[key]
                layer, rkey = (0, key) if key.startswith("win") else (l, f"{key}{l}")
                part, landed = reduced[rkey]
                w_rows = rows_of(weights[idx], transposed)
                items.append((layer, after(w_rows, when) if not items else w_rows, rows_of(moms[idx], transposed),
                              rows_of(vels[idx], transposed), part, landed, updated.get(key)))
            for key, result in zip(keys, adamw_shards(items, chip_arr, name=f"adamw_{group}_{l}")):
                updated[key] = result

    mix_landed = None
    for l in (1, 0):
        sv = saved[l]
        dgate, dup, dx1_b, dcat, small[f"norm2_{l}"] = ffn_bwd(
            dx_b, wd[l], sv["gate"], sv["up"], wg_t[l], wu_t[l], sv["x1"], norm2_g[l:l + 1], wo[l], name=f"ffn_bwd_{l}")
        if l == 0:
            dgate = after(dgate, *mix_landed)
            update_layer(1, dx1_b)
        big[f"wg{l}"], big[f"wu{l}"], big[f"wd{l}"] = ffn_weight_grads(
            dgate, dup, sv["h2"], sv["act"], dx_b, name=f"grad_ffn_{l}")
        stage1 = scatter_siblings([f"wd{l}", f"wg{l}", f"wu{l}"])
        big[f"wo{l}"] = mm_tn(sv["cat"], dx1_b, name=f"grad_wo_{l}")
        parts, ffn_landed = scatter_chips(stage1, big[f"wo{l}"])
        dcat = after(dcat, *parts)
        if l == 0:
            dq, dk, dv, dbias, small["a_q"], small["a_k"] = attn_bwd(sv["z"], dcat, gq2, gk2, bias, batch, seq)
            small["rel"] = bias_grad(dbias)
            win_t = a_win_t
            dz = None
            dcat = after(dcat, dq, *ffn_landed)
        else:
            dz, small["cw"], small["csum"] = conv_bwd(sv["z"], sv["y_conv"], dcat, cw_full, lg_full, lb_full, batch, seq)
            win_t = b_win_t
            dz = after(dz, *ffn_landed)
        dqm, small[f"mq_{l}"], small[f"mk_{l}"], big[f"wkv{l}"], small[f"memnorm_{l}"] = memattn_bwd(
            sv["z"], sv["kv"], dcat, sv["gq4"], sv["gk4"], mem2, sv["mem_n"], wkv[l], dz, batch, seq, sv["qcol"],
            name=f"memattn_bwd_{l}")
        if l == 0:
            pieces = [dq, dk, dv, dqm]
            big["win0"] = grad_pieces(pieces, sv["h"], name="grad_win_0")
        else:
            pieces = [dqm]
            big["win1"] = mm_tn(dqm, sv["h"], name="grad_win_1")
        stage1 = scatter_siblings([f"win{l}", f"wkv{l}", f"wo{l}"])
        dx_b, small[f"norm1_{l}"], dz_sum = in_proj_bwd(
            pieces, win_t, sv["xin"], norm1_g[l:l + 1], dx1_b, BF16 if l == 1 else F32, name=f"in_proj_bwd_{l}")
        if l == 1:
            small["bb"] = dz_sum
        parts, mix_landed = scatter_chips(stage1, dx_b)
        dx_b = after(dx_b, *parts)
    grad_x = dx_b.reshape(batch, seq, d)
    update_layer(0, dx_b)

    def shaped(rows, idx, transposed):
        shp = weights[idx].shape
        if transposed:
            return jnp.swapaxes(rows.reshape(shp[0], shp[2], shp[1]), 1, 2)
        return rows.reshape(shp)

    def fold(v, groups):
        return jnp.sum(v.reshape(groups, HEAD_DIM), axis=0, keepdims=True)

    heads = a_rel_bias.shape[1]
    small_list = [
        jnp.concatenate([small["norm1_0"], small["norm1_1"]]),
        jnp.concatenate([small["memnorm_0"], small["memnorm_1"]]),
        fold(small["a_q"], 2), fold(small["a_k"], 2), small["rel"][:heads],
        small["bb"], small["cw"][:CONV_W], small["csum"][0:1], small["csum"][1:2], small["csum"][2:3],
        jnp.concatenate([fold(small["mq_0"], 4), fold(small["mq_1"], 4)]),
        jnp.concatenate([fold(small["mk_0"], 4), fold(small["mk_1"], 4)]),
        jnp.concatenate([small["norm2_0"], small["norm2_1"]]),
    ]
    (g_norm1, g_memnorm, g_aq, g_ak, g_rel, g_bb_full, g_cw_full, g_cb_full, g_lg_full, g_lb_full,
     g_mq, g_mk, g_norm2, loss_sum) = reduce_small(small_list + [loss_blk])
    loss = loss_sum[0, 0]
    g_bb = lax.dynamic_slice_in_dim(g_bb_full, me * f_loc, f_loc, axis=1)
    g_cw = lax.dynamic_slice_in_dim(g_cw_full, me * c_loc, c_loc, axis=1)
    g_cb = lax.dynamic_slice_in_dim(g_cb_full, me * c_loc, c_loc, axis=1)
    g_lg = lax.dynamic_slice_in_dim(g_lg_full, me * c_loc, c_loc, axis=1)
    g_lb = lax.dynamic_slice_in_dim(g_lb_full, me * c_loc, c_loc, axis=1)

    grads = [g_norm1, g_memnorm, None, g_aq, g_ak, g_rel, None, g_bb, g_cw, g_cb, g_lg, g_lb,
             g_mq, g_mk, None, None, g_norm2, None, None, None]
    deltas, new_m, new_v = [None] * 20, [None] * 20, [None] * 20
    for key, (idx, transposed) in sharded.items():
        grads[idx], deltas[idx], new_m[idx], new_v[idx] = (shaped(r, idx, transposed) for r in updated[key])

    def flat2(a):
        return a.reshape(a.shape[-2:])

    small_idx = [i for i in range(20) if i not in {idx for idx, _ in sharded.values()}]
    dl, nm, nv = adamw_small([flat2(weights[i]) for i in small_idx], [flat2(grads[i]) for i in small_idx],
                             [flat2(moms[i]) for i in small_idx], [flat2(vels[i]) for i in small_idx])
    for i, a, b, cc in zip(small_idx, dl, nm, nv):
        shp = weights[i].shape
        grads[i], deltas[i], new_m[i], new_v[i] = grads[i].reshape(shp), a.reshape(shp), b.reshape(shp), cc.reshape(shp)

    return (loss, grad_x, *grads, *deltas, *new_m, *new_v)
```

```python
import jax
import jax.numpy as jnp
from jax import lax
from jax.experimental import pallas as pl
from jax.experimental.pallas import tpu as pltpu
from jax.experimental.pallas import tpu_sc as plsc

F32 = jnp.float32
BF16 = jnp.bfloat16
HIGHEST = lax.Precision.HIGHEST
MESH = pl.DeviceIdType.MESH
ANY = pl.BlockSpec(memory_space=pl.ANY)

N_DEV = 8
D_MODEL = 1024
HEAD_DIM = 64
TOK_WIDTH = 768
MEM_WIDTH = 256
CHUNK = 64
Q_BLOCK = 256
KEY_WIN = 768
BAND = 576
N_REL = 192
CONV_W = 31
CONV_HALO = 32
NORM_EPS = 1e-6
NEG_INF = -1e30
ATTN_SCALE = HEAD_DIM ** -0.5
LANES = 128
ROW_TILE = 512
VMEM_LIMIT = 56 * 1024 * 1024

ADAM_LR, ADAM_B1, ADAM_B2, ADAM_EPS, ADAM_WD, ADAM_STEP = 0.001, 0.9, 0.999, 1e-08, 0.01, 10


def _params(*sem):
    return pltpu.CompilerParams(dimension_semantics=sem, vmem_limit_bytes=VMEM_LIMIT)


WIDE_ROW_TILE = 1024


def _row_tile(m, rows=ROW_TILE):
    return rows if m % rows == 0 else m


def _col_tile(n, cap=1408):
    best = None
    for t in range(LANES, min(n, cap) + 1, LANES):
        if n % t == 0:
            best = t
    return best if best is not None else n


def _dot(a, b, ca, cb):
    return lax.dot_general(a, b, (((ca,), (cb,)), ((), ())), preferred_element_type=F32)


def _sigmoid(x):
    return 0.5 * jnp.tanh(0.5 * x) + 0.5


def mm_nt(a, b, bias=None, out_dtype=BF16, name="mm_nt"):
    m, k = a.shape
    n = b.shape[0]
    tm, tn = _row_tile(m, WIDE_ROW_TILE), _col_tile(n)

    def body(*refs):
        a_ref, b_ref = refs[0], refs[1]
        o_ref = refs[-1]
        acc = _dot(a_ref[...].astype(BF16), b_ref[...].astype(BF16), 1, 1)
        if bias is not None:
            acc = acc + refs[2][...]
        o_ref[...] = acc.astype(o_ref.dtype)

    in_specs = [pl.BlockSpec((tm, k), lambda j, i: (i, 0)), pl.BlockSpec((tn, k), lambda j, i: (j, 0))]
    args = [a, b]
    if bias is not None:
        in_specs.append(pl.BlockSpec((1, tn), lambda j, i: (0, j)))
        args.append(bias)
    return pl.pallas_call(
        body, out_shape=jax.ShapeDtypeStruct((m, n), out_dtype), grid=(n // tn, m // tm),
        in_specs=in_specs, out_specs=pl.BlockSpec((tm, tn), lambda j, i: (i, j)),
        compiler_params=_params("parallel", "arbitrary"), name=name)(*args)


def mm_tn(a, b, out_dtype=BF16, name="mm_tn"):
    t, r = a.shape
    c = b.shape[1]
    tr = _col_tile(r, 512)

    def body(a_ref, b_ref, o_ref):
        o_ref[...] = _dot(a_ref[...].astype(BF16), b_ref[...].astype(BF16), 0, 0).astype(o_ref.dtype)

    return pl.pallas_call(
        body, out_shape=jax.ShapeDtypeStruct((r, c), out_dtype), grid=(r // tr,),
        in_specs=[pl.BlockSpec((t, tr), lambda i: (0, i)), pl.BlockSpec((t, c), lambda i: (0, 0))],
        out_specs=pl.BlockSpec((tr, c), lambda i: (i, 0)),
        compiler_params=_params("parallel"), name=name)(a, b)


def _resident(shape):
    return pl.BlockSpec(shape, lambda i: (0, 0), pipeline_mode=pl.Buffered(1))


def proj_norm(a, b, res, gain, name):
    m, k = a.shape
    n = b.shape[1]
    tm = _row_tile(m)

    def body(a_ref, b_ref, res_ref, g_ref, x_ref, h_ref):
        xv = res_ref[...] + _dot(a_ref[...], b_ref[...], 1, 0)
        x_ref[...] = xv
        r = lax.rsqrt(jnp.mean(xv * xv, axis=-1, keepdims=True) + NORM_EPS)
        h_ref[...] = (xv * r * g_ref[...]).astype(BF16)

    row = pl.BlockSpec((tm, n), lambda i: (i, 0))
    return pl.pallas_call(
        body, out_shape=(jax.ShapeDtypeStruct((m, n), F32), jax.ShapeDtypeStruct((m, n), BF16)), grid=(m // tm,),
        in_specs=[pl.BlockSpec((tm, k), lambda i: (i, 0)), _resident((k, n)), row, _resident((1, n))],
        out_specs=(row, row), compiler_params=_params("parallel"), name=name)(a, b, res, gain)


def in_proj_bwd(pieces, w_t, x, gain, dres, out_dtype, name):
    m, n = x.shape
    k = pieces[0].shape[1]
    tm = _row_tile(m)
    npc = len(pieces)
    offs = [sum(p.shape[1] for p in pieces[:i]) for i in range(npc + 1)]

    def body(*refs):
        dz_refs = refs[:npc]
        w_ref, x_ref, g_ref, dres_ref, dx_ref, dg_ref, cs_ref = refs[npc:]

        @pl.when(pl.program_id(0) == 0)
        def _():
            dg_ref[...] = jnp.zeros_like(dg_ref)
            cs_ref[...] = jnp.zeros_like(cs_ref)

        cs_ref[...] += jnp.sum(dz_refs[0][...].astype(F32), axis=0, keepdims=True)
        dhv = _dot(dz_refs[0][...], w_ref[offs[0]:offs[1], :], 1, 0)
        for i in range(1, npc):
            dhv = dhv + _dot(dz_refs[i][...], w_ref[offs[i]:offs[i + 1], :], 1, 0)
        xv = x_ref[...]
        r = lax.rsqrt(jnp.mean(xv * xv, axis=-1, keepdims=True) + NORM_EPS)
        xhat = xv * r
        dg_ref[...] += jnp.sum(dhv * xhat, axis=0, keepdims=True)
        dxhat = dhv * g_ref[...]
        dx = dres_ref[...].astype(F32) + r * (dxhat - xhat * jnp.mean(dxhat * xhat, axis=-1, keepdims=True))
        dx_ref[...] = dx.astype(dx_ref.dtype)

    row = pl.BlockSpec((tm, n), lambda i: (i, 0))
    return pl.pallas_call(
        body, out_shape=(jax.ShapeDtypeStruct((m, n), out_dtype), jax.ShapeDtypeStruct((1, n), F32),
                         jax.ShapeDtypeStruct((1, k), F32)), grid=(m // tm,),
        in_specs=[pl.BlockSpec((tm, p.shape[1]), lambda i: (i, 0)) for p in pieces]
        + [_resident(w_t.shape), row, _resident((1, n)), row],
        out_specs=(row, pl.BlockSpec((1, n), lambda i: (0, 0)), pl.BlockSpec((1, k), lambda i: (0, 0))),
        compiler_params=_params("arbitrary"), name=name)(*pieces, w_t, x, gain, dres)


def grad_pieces(pieces, b, name):
    t, c = b.shape
    tr = 2 * LANES
    tiles = [p.shape[1] // tr for p in pieces]
    starts = [sum(tiles[:i]) for i in range(len(pieces) + 1)]

    def body(*refs):
        a_refs, b_ref, o_ref = refs[:len(pieces)], refs[len(pieces)], refs[len(pieces) + 1]
        i = pl.program_id(0)
        for p, a_ref in enumerate(a_refs):
            @pl.when((i >= starts[p]) & (i < starts[p + 1]))
            def _(a_ref=a_ref):
                o_ref[...] = _dot(a_ref[...], b_ref[...], 0, 0).astype(o_ref.dtype)

    def a_spec(p):
        return pl.BlockSpec((t, tr), lambda i: (0, jnp.clip(i - starts[p], 0, tiles[p] - 1)))

    return pl.pallas_call(
        body, out_shape=jax.ShapeDtypeStruct((starts[-1] * tr, c), BF16), grid=(starts[-1],),
        in_specs=[a_spec(p) for p in range(len(pieces))] + [_resident((t, c))],
        out_specs=pl.BlockSpec((tr, c), lambda i: (i, 0)),
        compiler_params=_params("arbitrary"), name=name)(*pieces, b)


FFN_ROWS = 256


def _ffn_row_tile(m):
    return FFN_ROWS if m % FFN_ROWS == 0 else m


def ffn_fwd(h2, wg_t, wu_t, wd, x1, gain=None, target=None, name="ffn_fwd"):
    n, d = h2.shape
    f = wg_t.shape[0]
    tm = _ffn_row_tile(n)
    nt = n // tm
    last = target is not None

    def body(h_ref, wg_ref, wu_ref, wd_ref, x1_ref, e_ref, g_ref, u_ref, a_ref, *rest):
        hv = h_ref[...]
        gv = _dot(hv, wg_ref[...], 1, 1)
        uv = _dot(hv, wu_ref[...], 1, 1)
        g_ref[...] = gv.astype(BF16)
        u_ref[...] = uv.astype(BF16)
        av = (gv * _sigmoid(gv) * uv).astype(BF16)
        a_ref[...] = av
        xv = x1_ref[...] + _dot(av, wd_ref[...], 1, 0)
        if not last:
            x_ref, hn_ref = rest
            x_ref[...] = xv
            r = lax.rsqrt(jnp.mean(xv * xv, axis=-1, keepdims=True) + NORM_EPS)
            hn_ref[...] = (xv * r * e_ref[...]).astype(BF16)
        else:
            dyb_ref, l_ref, acc_ref = rest
            i = pl.program_id(0)

            @pl.when(i == 0)
            def _():
                acc_ref[...] = jnp.zeros_like(acc_ref)

            err = xv - e_ref[...]
            dyb_ref[...] = (err * (1.0 / d)).astype(BF16)
            acc_ref[...] += jnp.sum(err * err, axis=0, keepdims=True)

            @pl.when(i == nt - 1)
            def _():
                total = jnp.sum(acc_ref[...], axis=-1, keepdims=True) * (0.5 / d)
                l_ref[...] = jnp.broadcast_to(total, l_ref.shape)

    row_d = pl.BlockSpec((tm, d), lambda i: (i, 0))
    row_f = pl.BlockSpec((tm, f), lambda i: (i, 0))
    act_shape = jax.ShapeDtypeStruct((n, f), BF16)
    if not last:
        extra_in, extra = _resident((1, d)), gain
        out_shape = (act_shape, act_shape, act_shape, jax.ShapeDtypeStruct((n, d), F32), jax.ShapeDtypeStruct((n, d), BF16))
        out_specs = (row_f, row_f, row_f, row_d, row_d)
        scratch = []
    else:
        extra_in, extra = row_d, target
        out_shape = (act_shape, act_shape, act_shape, jax.ShapeDtypeStruct((n, d), BF16),
                     jax.ShapeDtypeStruct((8, LANES), F32))
        out_specs = (row_f, row_f, row_f, row_d, pl.BlockSpec((8, LANES), lambda i: (0, 0)))
        scratch = [pltpu.VMEM((1, d), F32)]
    return pl.pallas_call(
        body, out_shape=out_shape, grid=(nt,),
        in_specs=[row_d, _resident((f, d)), _resident((f, d)), _resident((f, d)), row_d, extra_in],
        out_specs=out_specs, scratch_shapes=scratch,
        compiler_params=_params("arbitrary"), name=name)(h2, wg_t, wu_t, wd, x1, extra)


def ffn_bwd(dx_b, wd, gate, up, wg_t, wu_t, x1, gain, wo, name="ffn_bwd"):
    n, d = x1.shape
    f = wd.shape[0]
    tm = _ffn_row_tile(n)

    def body(dxb_ref, wd_ref, g_ref, u_ref, wg_ref, wu_ref, x_ref, gain_ref, wo_ref,
             dg_ref, du_ref, dxo_ref, dc_ref, dgain_ref):
        @pl.when(pl.program_id(0) == 0)
        def _():
            dgain_ref[...] = jnp.zeros_like(dgain_ref)

        dact = _dot(dxb_ref[...], wd_ref[...], 1, 1)
        gv = g_ref[...].astype(F32)
        uv = u_ref[...].astype(F32)
        sg = _sigmoid(gv)
        dgv = (dact * uv * sg * (1.0 + gv * (1.0 - sg))).astype(BF16)
        duv = (dact * gv * sg).astype(BF16)
        dg_ref[...] = dgv
        du_ref[...] = duv
        dhv = _dot(dgv, wg_ref[...], 1, 0) + _dot(duv, wu_ref[...], 1, 0)
        xv = x_ref[...]
        r = lax.rsqrt(jnp.mean(xv * xv, axis=-1, keepdims=True) + NORM_EPS)
        xhat = xv * r
        dgain_ref[...] += jnp.sum(dhv * xhat, axis=0, keepdims=True)
        dxhat = dhv * gain_ref[...]
        dxb = (dxb_ref[...].astype(F32) + r * (dxhat - xhat * jnp.mean(dxhat * xhat, axis=-1, keepdims=True))).astype(BF16)
        dxo_ref[...] = dxb
        dc_ref[...] = _dot(dxb, wo_ref[...], 1, 1).astype(BF16)

    row_d = pl.BlockSpec((tm, d), lambda i: (i, 0))
    row_f = pl.BlockSpec((tm, f), lambda i: (i, 0))
    w_spec = _resident((f, d))
    act_shape = jax.ShapeDtypeStruct((n, f), BF16)
    row_shape = jax.ShapeDtypeStruct((n, d), BF16)
    return pl.pallas_call(
        body, out_shape=(act_shape, act_shape, row_shape, jax.ShapeDtypeStruct((n, wo.shape[0]), BF16),
                         jax.ShapeDtypeStruct((1, d), F32)),
        grid=(n // tm,),
        in_specs=[row_d, w_spec, row_f, row_f, w_spec, w_spec, row_d, _resident((1, d)), _resident(wo.shape)],
        out_specs=(row_f, row_f, row_d, pl.BlockSpec((tm, wo.shape[0]), lambda i: (i, 0)),
                   pl.BlockSpec((1, d), lambda i: (0, 0))),
        compiler_params=_params("arbitrary"), name=name)(dx_b, wd, gate, up, wg_t, wu_t, x1, gain, wo)


def ffn_weight_grads(dgate, dup, h2, act, dx_b, name="ffn_weight_grads"):
    t, r = dgate.shape
    c = h2.shape[1]
    tr = _col_tile(r, 512)

    def body(a1_ref, a2_ref, a3_ref, b12_ref, b3_ref, o1_ref, o2_ref, o3_ref):
        bv = b12_ref[...]
        o1_ref[...] = _dot(a1_ref[...], bv, 0, 0).astype(o1_ref.dtype)
        o2_ref[...] = _dot(a2_ref[...], bv, 0, 0).astype(o2_ref.dtype)
        o3_ref[...] = _dot(a3_ref[...], b3_ref[...], 0, 0).astype(o3_ref.dtype)

    a_spec = pl.BlockSpec((t, tr), lambda i: (0, i))
    o_spec = pl.BlockSpec((tr, c), lambda i: (i, 0))
    shape = jax.ShapeDtypeStruct((r, c), BF16)
    return pl.pallas_call(
        body, out_shape=(shape, shape, shape), grid=(r // tr,),
        in_specs=[a_spec, a_spec, a_spec, _resident((t, c)), _resident((t, c))],
        out_specs=(o_spec, o_spec, o_spec), compiler_params=_params("parallel"), name=name)(dgate, dup, act, h2, dx_b)


def rms_fwd(x, g, name="rms_fwd"):
    n, d = x.shape
    tm = _row_tile(n)

    def body(x_ref, g_ref, o_ref):
        xv = x_ref[...]
        r = lax.rsqrt(jnp.mean(xv * xv, axis=-1, keepdims=True) + NORM_EPS)
        o_ref[...] = (xv * r * g_ref[...]).astype(o_ref.dtype)

    return pl.pallas_call(
        body, out_shape=jax.ShapeDtypeStruct((n, d), BF16), grid=(n // tm,),
        in_specs=[pl.BlockSpec((tm, d), lambda i: (i, 0)), pl.BlockSpec((1, d), lambda i: (0, 0))],
        out_specs=pl.BlockSpec((tm, d), lambda i: (i, 0)),
        compiler_params=_params("parallel"), name=name)(x, g)


def _group_masks(width):
    lane = lax.broadcasted_iota(jnp.int32, (1, width), 1)
    return [(lane >= HEAD_DIM * g) & (lane < HEAD_DIM * (g + 1)) for g in range(width // HEAD_DIM)]


def _group_sum(x, masks):
    out = jnp.zeros_like(x)
    for msk in masks:
        s = jnp.sum(jnp.where(msk, x, 0.0), axis=-1, keepdims=True)
        out = jnp.where(msk, s, out)
    return out


def _head_norm(x, gain, masks):
    r = lax.rsqrt(_group_sum(x * x, masks) * (1.0 / HEAD_DIM) + NORM_EPS)
    xhat = x * r
    return xhat * gain, xhat, r


def _head_norm_bwd(dxn, xhat, r, gain, masks):
    dgain = jnp.sum(dxn * xhat, axis=0, keepdims=True)
    dxhat = dxn * gain
    mean_t = _group_sum(dxhat * xhat, masks) * (1.0 / HEAD_DIM)
    return r * (dxhat - xhat * mean_t), dgain


def _softmax_rows(s):
    e = jnp.exp(s - jnp.max(s, axis=-1, keepdims=True))
    return e * (1.0 / jnp.sum(e, axis=-1, keepdims=True))


def _rel_onehot():
    col = lax.broadcasted_iota(jnp.int32, (1, KEY_WIN), 1)
    off = jnp.where(col < KEY_WIN - LANES, col, col - KEY_WIN)
    idx = jnp.clip(8 * CHUNK - off, -(CHUNK - 1), LANES) + (CHUNK - 1)
    return (lax.broadcasted_iota(jnp.int32, (N_REL, KEY_WIN), 0) == idx).astype(F32)


def bias_blocks(rel16):
    heads = TOK_WIDTH // HEAD_DIM

    def body(rel_ref, o_ref, u_ref):
        u_ref[...] = jnp.dot(rel_ref[...], _rel_onehot(), precision=HIGHEST, preferred_element_type=F32)
        row = lax.broadcasted_iota(jnp.int32, (CHUNK, KEY_WIN), 0)
        col = lax.broadcasted_iota(jnp.int32, (CHUNK, KEY_WIN), 1)
        for h in range(heads):
            xv = jnp.broadcast_to(u_ref[h:h + 1, :], (CHUNK, KEY_WIN))
            for b in range(6):
                xv = jnp.where(((row >> b) & 1) == 1, pltpu.roll(xv, 1 << b, axis=1), xv)
            xv = jnp.where(col < BAND, xv, NEG_INF)
            for i in range(Q_BLOCK // CHUNK):
                o_ref[h, CHUNK * i:CHUNK * (i + 1), :] = pltpu.roll(xv, CHUNK * i, axis=1) if i else xv

    return pl.pallas_call(
        body, out_shape=jax.ShapeDtypeStruct((heads, Q_BLOCK, KEY_WIN), F32),
        scratch_shapes=[pltpu.VMEM((16, KEY_WIN), F32)], name="bias_blocks")(rel16)


def bias_grad(dbias):
    heads = dbias.shape[0]

    def body(db_ref, o_ref, y_ref):
        y_ref[...] = jnp.zeros_like(y_ref)
        row = lax.broadcasted_iota(jnp.int32, (CHUNK, KEY_WIN), 0)
        for h in range(heads):
            fv = db_ref[h, 0:CHUNK, :]
            for i in range(1, Q_BLOCK // CHUNK):
                fv = fv + pltpu.roll(db_ref[h, CHUNK * i:CHUNK * (i + 1), :], KEY_WIN - CHUNK * i, axis=1)
            for b in range(6):
                fv = jnp.where(((row >> b) & 1) == 1, pltpu.roll(fv, KEY_WIN - (1 << b), axis=1), fv)
            y_ref[h:h + 1, :] = jnp.sum(fv, axis=0, keepdims=True)
        o_ref[...] = lax.dot_general(y_ref[...], _rel_onehot(), (((1,), (1,)), ((), ())),
                                     precision=HIGHEST, preferred_element_type=F32)

    return pl.pallas_call(
        body, out_shape=jax.ShapeDtypeStruct((16, N_REL), F32),
        scratch_shapes=[pltpu.VMEM((16, KEY_WIN), F32)], name="bias_grad")(dbias)


def _attn_windows(seq):
    out = []
    for j in range(seq // Q_BLOCK):
        r0 = j * Q_BLOCK
        k0 = max(0, r0 - 8 * CHUNK)
        width = r0 + Q_BLOCK - k0
        out.append((r0, k0, width, KEY_WIN - width))
    return out


def attn_fwd(z, gq2, gk2, bias, batch, seq):
    n = z.shape[0]
    pairs = TOK_WIDTH // LANES

    def body(q_ref, k_ref, v_ref, gq_ref, gk_ref, b_ref, o_ref, qs_s, kn_s):
        masks = _group_masks(LANES)
        qs_s[...] = (_head_norm(q_ref[...].astype(F32), gq_ref[...], masks)[0] * ATTN_SCALE).astype(BF16)
        kn_s[...] = _head_norm(k_ref[...].astype(F32), gk_ref[...], masks)[0].astype(BF16)
        for r0, k0, width, c0 in _attn_windows(seq):
            qb = qs_s[r0:r0 + Q_BLOCK, :]
            kw = kn_s[k0:k0 + width, :]
            vw = v_ref[k0:k0 + width, :]
            out = jnp.zeros((Q_BLOCK, LANES), F32)
            for h, msk in enumerate(masks):
                qh = jnp.where(msk, qb, jnp.zeros_like(qb))
                s = _dot(qh, kw, 1, 1) + b_ref[h, :, c0:KEY_WIN]
                p = _softmax_rows(s).astype(BF16)
                out = jnp.where(msk, _dot(p, vw, 1, 0), out)
            o_ref[r0:r0 + Q_BLOCK, :] = out.astype(o_ref.dtype)

    def col(off):
        return pl.BlockSpec((seq, LANES), lambda b, p: (b, off + p))

    vec = pl.BlockSpec((1, LANES), lambda b, p: (0, 0))
    return pl.pallas_call(
        body, out_shape=jax.ShapeDtypeStruct((n, D_MODEL), BF16), grid=(batch, pairs),
        in_specs=[col(0), col(pairs), col(2 * pairs), vec, vec,
                  pl.BlockSpec((2, Q_BLOCK, KEY_WIN), lambda b, p: (p, 0, 0))],
        out_specs=pl.BlockSpec((seq, LANES), lambda b, p: (b, p)),
        scratch_shapes=[pltpu.VMEM((seq, LANES), BF16), pltpu.VMEM((seq, LANES), BF16)],
        compiler_params=_params("parallel", "arbitrary"), name="attn_fwd")(z, z, z, gq2, gk2, bias)


def attn_bwd(z, dcat, gq2, gk2, bias, batch, seq):
    n = z.shape[0]
    pairs = TOK_WIDTH // LANES

    def body(q_ref, k_ref, v_ref, do_ref, gq_ref, gk_ref, b_ref,
             dq_ref, dk_ref, dv_ref, db_ref, dgq_ref, dgk_ref, qs_s, kn_s, dqn_s, dkn_s, dv_s):
        pi, bi = pl.program_id(0), pl.program_id(1)
        masks = _group_masks(LANES)

        @pl.when(bi == 0)
        def _():
            db_ref[...] = jnp.zeros_like(db_ref)

        @pl.when((bi == 0) & (pi == 0))
        def _():
            dgq_ref[...] = jnp.zeros_like(dgq_ref)
            dgk_ref[...] = jnp.zeros_like(dgk_ref)

        qn, qhat, rq = _head_norm(q_ref[...].astype(F32), gq_ref[...], masks)
        kn, khat, rk = _head_norm(k_ref[...].astype(F32), gk_ref[...], masks)
        qs_s[...] = (qn * ATTN_SCALE).astype(BF16)
        kn_s[...] = kn.astype(BF16)
        dkn_s[...] = jnp.zeros_like(dkn_s)
        dv_s[...] = jnp.zeros_like(dv_s)
        for r0, k0, width, c0 in _attn_windows(seq):
            qb = qs_s[r0:r0 + Q_BLOCK, :]
            dob = do_ref[r0:r0 + Q_BLOCK, :]
            kw = kn_s[k0:k0 + width, :]
            vw = v_ref[k0:k0 + width, :]
            dq_acc = jnp.zeros((Q_BLOCK, LANES), F32)
            dk_acc = jnp.zeros((width, LANES), F32)
            dv_acc = jnp.zeros((width, LANES), F32)
            for h, msk in enumerate(masks):
                qh = jnp.where(msk, qb, jnp.zeros_like(qb))
                doh = jnp.where(msk, dob, jnp.zeros_like(dob))
                p = _softmax_rows(_dot(qh, kw, 1, 1) + b_ref[h, :, c0:KEY_WIN])
                dp = _dot(doh, vw, 1, 1)
                ds = p * (dp - jnp.sum(p * dp, axis=-1, keepdims=True))
                db_ref[h, :, c0:KEY_WIN] += ds
                dsb = ds.astype(BF16)
                dq_acc = jnp.where(msk, _dot(dsb, kw, 1, 0), dq_acc)
                dk_acc = jnp.where(msk, _dot(dsb, qb, 0, 0), dk_acc)
                dv_acc = jnp.where(msk, _dot(p.astype(BF16), dob, 0, 0), dv_acc)
            dqn_s[r0:r0 + Q_BLOCK, :] = dq_acc * ATTN_SCALE
            dkn_s[k0:k0 + width, :] += dk_acc
            dv_s[k0:k0 + width, :] += dv_acc
        dq, dgq = _head_norm_bwd(dqn_s[...], qhat, rq, gq_ref[...], masks)
        dk, dgk = _head_norm_bwd(dkn_s[...], khat, rk, gk_ref[...], masks)
        dq_ref[...] = dq.astype(dq_ref.dtype)
        dk_ref[...] = dk.astype(dk_ref.dtype)
        dv_ref[...] = dv_s[...].astype(dv_ref.dtype)
        dgq_ref[...] += dgq
        dgk_ref[...] += dgk

    def col(off):
        return pl.BlockSpec((seq, LANES), lambda p, b: (b, off + p))

    vec = pl.BlockSpec((1, LANES), lambda p, b: (0, 0))
    blk = pl.BlockSpec((2, Q_BLOCK, KEY_WIN), lambda p, b: (p, 0, 0))
    o_shape = jax.ShapeDtypeStruct((n, TOK_WIDTH), BF16)
    v_shape = jax.ShapeDtypeStruct((1, LANES), F32)
    return pl.pallas_call(
        body,
        out_shape=(o_shape, o_shape, o_shape, jax.ShapeDtypeStruct(bias.shape, F32), v_shape, v_shape),
        grid=(pairs, batch),
        in_specs=[col(0), col(pairs), col(2 * pairs), col(0), vec, vec, blk],
        out_specs=(col(0), col(0), col(0), blk, vec, vec),
        scratch_shapes=[pltpu.VMEM((seq, LANES), BF16), pltpu.VMEM((seq, LANES), BF16),
                        pltpu.VMEM((seq, LANES), F32), pltpu.VMEM((seq, LANES), F32), pltpu.VMEM((seq, LANES), F32)],
        compiler_params=_params("arbitrary", "arbitrary"), name="attn_bwd")(z, z, z, dcat, gq2, gk2, bias)


MEM_ROWS = 512


def memattn_fwd(z, mem, mem_gain, wkv, gq4, gk4, cat, batch, seq, qcol, name):
    mtok = mem.shape[0] // batch
    d = mem.shape[1]
    rows = min(MEM_ROWS, seq)

    def body(q_ref, m_ref, mg_ref, w_ref, gq_ref, gk_ref, cat_ref, o_ref, n_ref, kv_ref):
        del cat_ref
        masks = _group_masks(MEM_WIDTH)
        mv = m_ref[...]
        r = lax.rsqrt(jnp.mean(mv * mv, axis=-1, keepdims=True) + NORM_EPS)
        nv = (mv * r * mg_ref[...]).astype(BF16)
        n_ref[...] = nv
        kv_ref[...] = _dot(nv, w_ref[...], 1, 0)
        kn = _head_norm(kv_ref[:, 0:MEM_WIDTH], gk_ref[...], masks)[0].astype(BF16)
        vm = kv_ref[:, MEM_WIDTH:2 * MEM_WIDTH].astype(BF16)
        for t in range(seq // rows):
            sl = slice(t * rows, (t + 1) * rows)
            qs = (_head_norm(q_ref[sl, :].astype(F32), gq_ref[...], masks)[0] * ATTN_SCALE).astype(BF16)
            out = jnp.zeros((rows, MEM_WIDTH), F32)
            for msk in masks:
                qh = jnp.where(msk, qs, jnp.zeros_like(qs))
                p = _softmax_rows(_dot(qh, kn, 1, 1)).astype(BF16)
                out = jnp.where(msk, _dot(p, vm, 1, 0), out)
            o_ref[sl, :] = out.astype(o_ref.dtype)

    vec = pl.BlockSpec((1, MEM_WIDTH), lambda b: (0, 0))
    mem_spec = pl.BlockSpec((mtok, d), lambda b: (b, 0))
    kv_spec = pl.BlockSpec((mtok, 2 * MEM_WIDTH), lambda b: (b, 0))
    return pl.pallas_call(
        body, out_shape=(jax.ShapeDtypeStruct(cat.shape, cat.dtype), jax.ShapeDtypeStruct(mem.shape, BF16),
                         jax.ShapeDtypeStruct((mem.shape[0], 2 * MEM_WIDTH), F32)), grid=(batch,),
        in_specs=[pl.BlockSpec((seq, MEM_WIDTH), lambda b: (b, qcol)), mem_spec, pl.BlockSpec((1, d), lambda b: (0, 0)),
                  pl.BlockSpec(wkv.shape, lambda b: (0, 0)), vec, vec, ANY],
        out_specs=(pl.BlockSpec((seq, MEM_WIDTH), lambda b: (b, TOK_WIDTH // MEM_WIDTH)), mem_spec, kv_spec),
        input_output_aliases={6: 0},
        compiler_params=_params("parallel"), name=name)(z, mem, mem_gain, wkv, gq4, gk4, cat)


def memattn_bwd(z, kv, dcat, gq4, gk4, mem, mem_n, wkv, dz, batch, seq, qcol, name):
    mtok = kv.shape[0] // batch
    d = mem.shape[1]
    rows = min(MEM_ROWS, seq)

    def body(q_ref, kv_ref, do_ref, gq_ref, gk_ref, m_ref, n_ref, w_ref, *rest):
        dq_ref, dgq_ref, dgk_ref, dw_ref, dmg_ref, dw_acc = rest[-6:]

        @pl.when(pl.program_id(0) == 0)
        def _():
            dgq_ref[...] = jnp.zeros_like(dgq_ref)
            dgk_ref[...] = jnp.zeros_like(dgk_ref)
            dmg_ref[...] = jnp.zeros_like(dmg_ref)
            dw_acc[...] = jnp.zeros_like(dw_acc)

        masks = _group_masks(MEM_WIDTH)
        kn_f, khat, rk = _head_norm(kv_ref[:, 0:MEM_WIDTH], gk_ref[...], masks)
        kn = kn_f.astype(BF16)
        vm = kv_ref[:, MEM_WIDTH:2 * MEM_WIDTH].astype(BF16)
        dkn = jnp.zeros((mtok, MEM_WIDTH), F32)
        dvm = jnp.zeros((mtok, MEM_WIDTH), F32)
        dgq = jnp.zeros((1, MEM_WIDTH), F32)
        for t in range(seq // rows):
            sl = slice(t * rows, (t + 1) * rows)
            qn_f, qhat, rq = _head_norm(q_ref[sl, :].astype(F32), gq_ref[...], masks)
            qs = (qn_f * ATTN_SCALE).astype(BF16)
            dob = do_ref[sl, :]
            dqn = jnp.zeros((rows, MEM_WIDTH), F32)
            for msk in masks:
                qh = jnp.where(msk, qs, jnp.zeros_like(qs))
                doh = jnp.where(msk, dob, jnp.zeros_like(dob))
                p = _softmax_rows(_dot(qh, kn, 1, 1))
                dp = _dot(doh, vm, 1, 1)
                ds = p * (dp - jnp.sum(p * dp, axis=-1, keepdims=True))
                dsb = ds.astype(BF16)
                dqn = jnp.where(msk, _dot(dsb, kn, 1, 0), dqn)
                dkn = dkn + jnp.where(msk, _dot(dsb, qs, 0, 0), 0.0)
                dvm = dvm + jnp.where(msk, _dot(p.astype(BF16), dob, 0, 0), 0.0)
            dq, dg = _head_norm_bwd(dqn * ATTN_SCALE, qhat, rq, gq_ref[...], masks)
            dq_ref[sl, :] = dq.astype(dq_ref.dtype)
            dgq = dgq + dg
        dk, dgk = _head_norm_bwd(dkn, khat, rk, gk_ref[...], masks)
        dgq_ref[...] += dgq
        dgk_ref[...] += dgk
        dkv_b = jnp.concatenate([dk, dvm], axis=-1).astype(BF16)
        dw_acc[...] += _dot(n_ref[...], dkv_b, 0, 0)
        dn = _dot(dkv_b, w_ref[...], 1, 1)
        mv = m_ref[...]
        rm = lax.rsqrt(jnp.mean(mv * mv, axis=-1, keepdims=True) + NORM_EPS)
        dmg_ref[...] += jnp.sum(dn * (mv * rm), axis=0, keepdims=True)

        @pl.when(pl.program_id(0) == batch - 1)
        def _():
            dw_ref[...] = dw_acc[...].astype(dw_ref.dtype)

    vec = pl.BlockSpec((1, MEM_WIDTH), lambda b: (0, 0))
    kv_spec = pl.BlockSpec((mtok, 2 * MEM_WIDTH), lambda b: (b, 0))
    mem_spec = pl.BlockSpec((mtok, d), lambda b: (b, 0))
    w_spec = pl.BlockSpec(wkv.shape, lambda b: (0, 0))
    v_shape = jax.ShapeDtypeStruct((1, MEM_WIDTH), F32)
    q_spec = pl.BlockSpec((seq, MEM_WIDTH), lambda b: (b, qcol))
    in_specs = [q_spec, kv_spec, pl.BlockSpec((seq, MEM_WIDTH), lambda b: (b, TOK_WIDTH // MEM_WIDTH)), vec, vec,
                mem_spec, mem_spec, w_spec]
    args = [z, kv, dcat, gq4, gk4, mem, mem_n, wkv]
    if dz is None:
        dq_shape, dq_spec, aliases = jax.ShapeDtypeStruct((z.shape[0], MEM_WIDTH), BF16), \
            pl.BlockSpec((seq, MEM_WIDTH), lambda b: (b, 0)), {}
    else:
        dq_shape, dq_spec, aliases = jax.ShapeDtypeStruct(dz.shape, dz.dtype), q_spec, {len(args): 0}
        in_specs.append(ANY)
        args.append(dz)
    return pl.pallas_call(
        body,
        out_shape=(dq_shape, v_shape, v_shape, jax.ShapeDtypeStruct(wkv.shape, BF16), jax.ShapeDtypeStruct((1, d), F32)),
        grid=(batch,), in_specs=in_specs,
        out_specs=(dq_spec, vec, vec, w_spec, pl.BlockSpec((1, d), lambda b: (0, 0))),
        scratch_shapes=[pltpu.VMEM(wkv.shape, F32)], input_output_aliases=aliases,
        compiler_params=_params("arbitrary"), name=name)(*args)


CONV_ROWS = 512


def _glu(a_ref, g_ref):
    return a_ref[...].astype(F32) * _sigmoid(g_ref[...].astype(F32))


def _layer_norm_stats(y):
    mu = jnp.mean(y, axis=-1, keepdims=True)
    yc = y - mu
    rstd = lax.rsqrt(jnp.mean(yc * yc, axis=-1, keepdims=True) + NORM_EPS)
    return yc * rstd, rstd


CONV_WIN = CONV_HALO + CONV_ROWS
SUBLANES = 8
SHIFT_ROWS = CONV_WIN - SUBLANES


def _preshift(win, shifted):
    for s in range(1, SUBLANES):
        shifted[s - 1, :, :] = win[s:s + SHIFT_ROWS, :]


TAP_ROWS = 64
TAP_TILES = [(r0, slice(c0, c0 + LANES)) for c0 in range(0, TOK_WIDTH, LANES) for r0 in range(0, CONV_ROWS, TAP_ROWS)]


def _tap(win, shifted, off, r0, lanes):
    s = off % SUBLANES
    base = off - s + r0
    if s == 0:
        return win[base:base + TAP_ROWS, lanes]
    return shifted[s - 1, base:base + TAP_ROWS, lanes]


def _fold_rows(x):
    return jnp.sum(x.reshape(TAP_ROWS // SUBLANES, SUBLANES, LANES), axis=0)


def conv_fwd(z, cw, cb, lg, lb, batch, seq):
    n = z.shape[0]
    nt = seq // CONV_ROWS
    sub = CONV_ROWS // CONV_HALO
    lead = CONV_HALO - (CONV_W - 1)

    def body(a_ref, g_ref, ap_ref, gp_ref, cw_ref, cb_ref, lg_ref, lb_ref, o_ref, y_ref, win, shifted):
        first = pl.program_id(1) == 0
        win[0:CONV_HALO, :] = jnp.where(first, 0.0, _glu(ap_ref, gp_ref))
        win[CONV_HALO:CONV_WIN, :] = _glu(a_ref, g_ref)
        _preshift(win, shifted)
        for r0, lanes in TAP_TILES:
            acc = jnp.zeros((TAP_ROWS, LANES), F32) + cb_ref[:, lanes]
            for w in range(CONV_W):
                acc = acc + _tap(win, shifted, lead + w, r0, lanes) * cw_ref[w:w + 1, lanes]
            y_ref[r0:r0 + TAP_ROWS, lanes] = acc
        yh, _ = _layer_norm_stats(y_ref[...])
        t = yh * lg_ref[...] + lb_ref[...]
        o_ref[...] = (t * _sigmoid(t)).astype(o_ref.dtype)

    def cur(c):
        return pl.BlockSpec((CONV_ROWS, TOK_WIDTH), lambda b, i: (b * nt + i, c))

    def prev(c):
        return pl.BlockSpec((CONV_HALO, TOK_WIDTH), lambda b, i: (jnp.maximum((b * nt + i) * sub - 1, 0), c))

    vec = pl.BlockSpec((1, TOK_WIDTH), lambda b, i: (0, 0))
    return pl.pallas_call(
        body, out_shape=(jax.ShapeDtypeStruct((n, D_MODEL), BF16), jax.ShapeDtypeStruct((n, TOK_WIDTH), F32)),
        grid=(batch, nt),
        in_specs=[cur(0), cur(1), prev(0), prev(1), pl.BlockSpec((32, TOK_WIDTH), lambda b, i: (0, 0)), vec, vec, vec],
        out_specs=(cur(0), cur(0)),
        scratch_shapes=[pltpu.VMEM((CONV_WIN, TOK_WIDTH), F32), pltpu.VMEM((SUBLANES - 1, SHIFT_ROWS, TOK_WIDTH), F32)],
        compiler_params=_params("parallel", "arbitrary"), name="conv_fwd")(z, z, z, z, cw, cb, lg, lb)


def conv_bwd(z, y, dcat, cw, lg, lb, batch, seq):
    n = z.shape[0]
    nt = seq // CONV_ROWS
    sub = CONV_ROWS // CONV_HALO
    lead = CONV_HALO - (CONV_W - 1)
    last_blk = n // CONV_HALO - 1

    def body(a_ref, g_ref, ap_ref, gp_ref, y_ref, yn_ref, do_ref, don_ref, cw_ref, lg_ref, lb_ref,
             dz_ref, dcw_ref, dsm_ref, win, shifted, dyw, dshifted, dg_o):
        b, i, which = pl.program_id(0), pl.program_id(1), pl.program_id(2)

        @pl.when(which == 0)
        def _():
            first, last = i == 0, i == nt - 1

            @pl.when((b == 0) & (i == 0))
            def _():
                dcw_ref[...] = jnp.zeros_like(dcw_ref)
                dsm_ref[...] = jnp.zeros_like(dsm_ref)

            win[0:CONV_HALO, :] = jnp.where(first, 0.0, _glu(ap_ref, gp_ref))
            win[CONV_HALO:CONV_WIN, :] = _glu(a_ref, g_ref)
            _preshift(win, shifted)
            yv = jnp.concatenate([y_ref[...], yn_ref[...]], axis=0)
            yh, rstd = _layer_norm_stats(yv)
            t = yh * lg_ref[...] + lb_ref[...]
            st = _sigmoid(t)
            dout = jnp.concatenate(
                [do_ref[...].astype(F32), jnp.where(last, 0.0, don_ref[...].astype(F32))], axis=0)
            dt = dout * st * (1.0 + t * (1.0 - st))
            dyh = dt * lg_ref[...]
            dy = rstd * (dyh - jnp.mean(dyh, axis=-1, keepdims=True)
                         - yh * jnp.mean(dyh * yh, axis=-1, keepdims=True))
            dyw[...] = dy
            _preshift(dyw, dshifted)
            dsm_ref[0:1, :] += jnp.sum(dy[0:CONV_ROWS], axis=0, keepdims=True)
            dsm_ref[1:2, :] += jnp.sum((dt * yh)[0:CONV_ROWS], axis=0, keepdims=True)
            dsm_ref[2:3, :] += jnp.sum(dt[0:CONV_ROWS], axis=0, keepdims=True)
            for c0 in range(0, TOK_WIDTH, LANES):
                lanes = slice(c0, c0 + LANES)
                dcw_acc = [jnp.zeros((SUBLANES, LANES), F32) for _ in range(CONV_W)]
                for r0 in range(0, CONV_ROWS, TAP_ROWS):
                    dyt = dyw[r0:r0 + TAP_ROWS, lanes]
                    dglu = jnp.zeros((TAP_ROWS, LANES), F32)
                    for w in range(CONV_W):
                        dcw_acc[w] = dcw_acc[w] + _fold_rows(dyt * _tap(win, shifted, lead + w, r0, lanes))
                        dglu = dglu + _tap(dyw, dshifted, CONV_W - 1 - w, r0, lanes) * cw_ref[w:w + 1, lanes]
                    avt = a_ref[r0:r0 + TAP_ROWS, lanes].astype(F32)
                    sgt = _sigmoid(g_ref[r0:r0 + TAP_ROWS, lanes].astype(F32))
                    dz_ref[r0:r0 + TAP_ROWS, lanes] = (dglu * sgt).astype(dz_ref.dtype)
                    dg_o[r0:r0 + TAP_ROWS, lanes] = (dglu * avt * sgt * (1.0 - sgt)).astype(dg_o.dtype)
                for w in range(CONV_W):
                    dcw_ref[w:w + 1, lanes] += jnp.sum(dcw_acc[w], axis=0, keepdims=True)

        @pl.when(which == 1)
        def _():
            dz_ref[...] = dg_o[...]

    def ahead(b, i, t):
        return jnp.minimum(b * nt + i + t, batch * nt - 1)

    def cur(c):
        return pl.BlockSpec((CONV_ROWS, TOK_WIDTH), lambda b, i, t: (ahead(b, i, t), c))

    def prev(c):
        return pl.BlockSpec((CONV_HALO, TOK_WIDTH), lambda b, i, t: (jnp.maximum(ahead(b, i, t) * sub - 1, 0), c))

    nxt = pl.BlockSpec((CONV_HALO, TOK_WIDTH),
                       lambda b, i, t: (jnp.minimum((ahead(b, i, t) + 1) * sub, last_blk), 0))
    vec = pl.BlockSpec((1, TOK_WIDTH), lambda b, i, t: (0, 0))
    full32 = pl.BlockSpec((32, TOK_WIDTH), lambda b, i, t: (0, 0))
    return pl.pallas_call(
        body,
        out_shape=(jax.ShapeDtypeStruct(z.shape, BF16), jax.ShapeDtypeStruct((32, TOK_WIDTH), F32),
                   jax.ShapeDtypeStruct((8, TOK_WIDTH), F32)),
        grid=(batch, nt, 2),
        in_specs=[cur(0), cur(1), prev(0), prev(1), cur(0), nxt, cur(0), nxt, full32, vec, vec],
        out_specs=(pl.BlockSpec((CONV_ROWS, TOK_WIDTH), lambda b, i, t: (b * nt + i, t)), full32,
                   pl.BlockSpec((8, TOK_WIDTH), lambda b, i, t: (0, 0))),
        scratch_shapes=[pltpu.VMEM((CONV_WIN, TOK_WIDTH), F32), pltpu.VMEM((SUBLANES - 1, SHIFT_ROWS, TOK_WIDTH), F32),
                        pltpu.VMEM((CONV_WIN, TOK_WIDTH), F32), pltpu.VMEM((SUBLANES - 1, SHIFT_ROWS, TOK_WIDTH), F32),
                        pltpu.VMEM((CONV_ROWS, TOK_WIDTH), BF16)],
        compiler_params=_params("arbitrary", "arbitrary", "arbitrary"), name="conv_bwd")(
            z, z, z, z, y, y, dcat, dcat, cw, lg, lb)


def _place():
    return lax.axis_index("x"), lax.axis_index("y"), lax.axis_index("c")


def _other_chips(x, y):
    return [(1 - x, y), (x, 1 - y), (1 - x, 1 - y)]


def reduce_small(arrays):
    na = len(arrays)

    def body(*refs):
        ins, outs, bufs = refs[:na], refs[na:2 * na], refs[2 * na:3 * na]
        send_sems, recv_sems = refs[3 * na:]
        x, y, c = _place()
        me = 4 * x + 2 * y + c
        copies = []
        for a in range(na):
            bufs[a][me] = ins[a][...]
            for k in range(1, N_DEV):
                cp = pltpu.make_async_remote_copy(
                    src_ref=ins[a], dst_ref=bufs[a].at[me], send_sem=send_sems.at[a, k - 1],
                    recv_sem=recv_sems.at[a, k - 1],
                    device_id=(x ^ (k >> 2), y ^ ((k >> 1) & 1), c ^ (k & 1)), device_id_type=MESH)
                cp.start()
                copies.append(cp)
        for a in range(na):
            for k in range(1, N_DEV):
                src = 4 * (x ^ (k >> 2)) + 2 * (y ^ ((k >> 1) & 1)) + (c ^ (k & 1))
                pltpu.make_async_remote_copy(
                    src_ref=ins[a], dst_ref=bufs[a].at[src], send_sem=send_sems.at[a, k - 1],
                    recv_sem=recv_sems.at[a, k - 1], device_id=(x, y, c), device_id_type=MESH).wait_recv()
        for cp in copies:
            cp.wait_send()
        for a in range(na):
            total = bufs[a][0]
            for dev in range(1, N_DEV):
                total = total + bufs[a][dev]
            outs[a][...] = total

    vmem = pl.BlockSpec(memory_space=pltpu.VMEM)
    return pl.pallas_call(
        body, out_shape=tuple(jax.ShapeDtypeStruct(a.shape, F32) for a in arrays),
        in_specs=[vmem] * na, out_specs=tuple([vmem] * na),
        scratch_shapes=[pltpu.VMEM((N_DEV,) + a.shape, F32) for a in arrays]
        + [pltpu.SemaphoreType.DMA((na, N_DEV - 1)), pltpu.SemaphoreType.DMA((na, N_DEV - 1))],
        compiler_params=pltpu.CompilerParams(vmem_limit_bytes=VMEM_LIMIT), name="small_reduce")(*arrays)


def adamw_small(ws, gs, ms, vs):
    na = len(ws)
    c1 = 1.0 / (1.0 - ADAM_B1 ** ADAM_STEP)
    c2 = 1.0 / (1.0 - ADAM_B2 ** ADAM_STEP)

    def body(*refs):
        w_refs, g_refs, m_refs, v_refs = (refs[i * na:(i + 1) * na] for i in range(4))
        d_refs, nm_refs, nv_refs = (refs[(4 + i) * na:(5 + i) * na] for i in range(3))
        for a in range(na):
            gv = g_refs[a][...]
            nm = ADAM_B1 * m_refs[a][...] + (1.0 - ADAM_B1) * gv
            nv = ADAM_B2 * v_refs[a][...] + (1.0 - ADAM_B2) * (gv * gv)
            nm_refs[a][...] = nm
            nv_refs[a][...] = nv
            d_refs[a][...] = -ADAM_LR * ((nm * c1) / (jnp.sqrt(nv * c2) + ADAM_EPS) + ADAM_WD * w_refs[a][...])

    vmem = pl.BlockSpec(memory_space=pltpu.VMEM)
    shapes = tuple(jax.ShapeDtypeStruct(w.shape, F32) for w in ws)
    outs = pl.pallas_call(
        body, out_shape=shapes * 3, in_specs=[vmem] * (4 * na), out_specs=tuple([vmem] * (3 * na)),
        compiler_params=pltpu.CompilerParams(vmem_limit_bytes=VMEM_LIMIT), name="adamw_small")(*ws, *gs, *ms, *vs)
    return outs[:na], outs[na:2 * na], outs[2 * na:]


def gather_weights(shards, name, collective_id):
    nw = len(shards)
    ns = [s.shape[0] for s in shards]
    in_refs = [jax.new_ref(s, memory_space=pltpu.MemorySpace.HBM) for s in shards]
    out_refs = [jax.empty_ref(jax.ShapeDtypeStruct((N_DEV * s.shape[0], s.shape[1]), s.dtype),
                              memory_space=pltpu.MemorySpace.HBM) for s in shards]

    @pl.kernel(mesh=plsc.ScalarSubcoreMesh(axis_name="seq", num_cores=1), name=name,
               scratch_types=(pltpu.SemaphoreType.DMA((nw, 7)), pltpu.SemaphoreType.DMA((nw, 7)),
                              pltpu.SemaphoreType.DMA((nw,))),
               compiler_params=pltpu.CompilerParams(collective_id=collective_id))
    def launch(send_sems, recv_sems, local_sems):
        x, y, c = _place()
        me, sib = (x, y, c), (x, y, 1 - c)
        chips = _other_chips(x, y)
        barrier = pltpu.get_barrier_semaphore()
        for peer in [sib] + [(*chip, c) for chip in chips]:
            pl.semaphore_signal(barrier, inc=1, device_id=peer, device_id_type=MESH)
        pl.semaphore_wait(barrier, 4)

        def rows(w, dev):
            return out_refs[w].at[pl.ds((4 * dev[0] + 2 * dev[1] + dev[2]) * ns[w], ns[w]), :]

        def copy(w, k, block, to, src=None):
            return pltpu.make_async_remote_copy(
                src_ref=rows(w, block) if src is None else src, dst_ref=rows(w, block),
                send_sem=send_sems.at[w, k], recv_sem=recv_sems.at[w, k], device_id=to, device_id_type=MESH)

        started, sends = [], []
        for w in range(nw):
            mine = pltpu.make_async_copy(in_refs[w], rows(w, me), local_sems.at[w])
            mine.start()
            started.append(mine)
            first = [copy(w, 0, me, sib, src=in_refs[w])]
            first += [copy(w, 1 + j, me, (*chip, c), src=in_refs[w]) for j, chip in enumerate(chips)]
            for cp in first:
                cp.start()
            sends += first
        for w in range(nw):
            for j, chip in enumerate(chips):
                copy(w, 1 + j, (*chip, c), me).wait_recv()
                fwd = copy(w, 4 + j, (*chip, c), sib)
                fwd.start()
                sends.append(fwd)
        for w in range(nw):
            copy(w, 0, sib, me).wait_recv()
            for j, chip in enumerate(chips):
                copy(w, 4 + j, (*chip, 1 - c), me).wait_recv()
        for cp in sends:
            cp.wait_send()
        for mine in started:
            mine.wait()

    launch()
    return [r[...] for r in out_refs]


def _sequencer_exchange(sources, out_rows, peers_of, copies_of, name, collective_id):
    nw = len(sources)
    in_refs = [jax.new_ref(s, memory_space=pltpu.MemorySpace.HBM) for s in sources]
    out_refs = [jax.empty_ref(jax.ShapeDtypeStruct((rows, s.shape[1]), s.dtype), memory_space=pltpu.MemorySpace.HBM)
                for rows, s in zip(out_rows, sources)]
    per = len(copies_of(0, 0, 0, 0))

    @pl.kernel(mesh=plsc.ScalarSubcoreMesh(axis_name="seq", num_cores=1), name=name,
               scratch_types=(pltpu.SemaphoreType.DMA((nw, per)), pltpu.SemaphoreType.DMA((nw, per))),
               compiler_params=pltpu.CompilerParams(collective_id=collective_id))
    def launch(send_sems, recv_sems):
        x, y, c = _place()
        peers = peers_of(x, y, c)
        barrier = pltpu.get_barrier_semaphore()
        for peer in peers:
            pl.semaphore_signal(barrier, inc=1, device_id=peer, device_id_type=MESH)
        pl.semaphore_wait(barrier, len(peers))
        copies = []
        for w in range(nw):
            for k, (src_blk, dst_blk, rows, peer) in enumerate(copies_of(x, y, c, w)):
                cp = pltpu.make_async_remote_copy(
                    src_ref=in_refs[w].at[pl.ds(src_blk * rows, rows), :],
                    dst_ref=out_refs[w].at[pl.ds(dst_blk * rows, rows), :],
                    send_sem=send_sems.at[w, k], recv_sem=recv_sems.at[w, k], device_id=peer, device_id_type=MESH)
                cp.start()
                copies.append(cp)
        for cp in copies:
            cp.wait_recv()
        for cp in copies:
            cp.wait_send()

    launch()
    return [r[...] for r in out_refs]


def scatter_to_sibling(grads, name, collective_id):
    ns = [g.shape[0] // N_DEV for g in grads]
    return _sequencer_exchange(
        grads, [4 * n for n in ns],
        lambda x, y, c: [(x, y, 1 - c)],
        lambda x, y, c, w: [(2 * q + 1 - c, q, ns[w], (x, y, 1 - c)) for q in range(4)],
        name, collective_id)


def scatter_to_chips(parts, name, collective_id):
    ns = [p.shape[0] // 4 for p in parts]
    return _sequencer_exchange(
        parts, [3 * n for n in ns],
        lambda x, y, c: [(*chip, c) for chip in _other_chips(x, y)],
        lambda x, y, c, w: [(2 * chip[0] + chip[1], j, ns[w], (*chip, c)) for j, chip in enumerate(_other_chips(x, y))],
        name, collective_id)


def add_sibling(grads, landeds, core, name):
    nw = len(grads)

    def body(c_ref, *refs):
        for w in range(nw):
            g_ref, l_ref, o_ref = refs[2 * w], refs[2 * w + 1], refs[2 * nw + w]
            o_ref[...] = (g_ref[...].astype(F32) + l_ref[...].astype(F32)).astype(o_ref.dtype)

    in_specs, out_specs, args = [], [], []
    for g, ld in zip(grads, landeds):
        n, cols = ld.shape[0] // 4, g.shape[1]
        in_specs += [pl.BlockSpec((n, cols), lambda q, c_ref: (2 * q + c_ref[0], 0)),
                     pl.BlockSpec((n, cols), lambda q, c_ref: (q, 0))]
        out_specs.append(pl.BlockSpec((n, cols), lambda q, c_ref: (q, 0)))
        args += [g, ld]
    grid_spec = pltpu.PrefetchScalarGridSpec(
        num_scalar_prefetch=1, grid=(4,), in_specs=in_specs, out_specs=tuple(out_specs))
    return pl.pallas_call(
        body, out_shape=tuple(jax.ShapeDtypeStruct(ld.shape, ld.dtype) for ld in landeds), grid_spec=grid_spec,
        compiler_params=_params("arbitrary"), name=name)(core, *args)


ADAMW_HALVES = 2


def adamw_shards(items, chip, name):
    c1 = 1.0 / (1.0 - ADAM_B1 ** ADAM_STEP)
    c2 = 1.0 / (1.0 - ADAM_B2 ** ADAM_STEP)
    ni = len(items)

    def body(q_ref, *refs):
        outs = refs[len(refs) - 4 * ni:]
        for k in range(ni):
            w_ref, m_ref, v_ref, p_ref, l0_ref, l1_ref, l2_ref = refs[7 * k:7 * k + 7]
            g_ref, d_ref, nm_ref, nv_ref = outs[4 * k:4 * k + 4]
            gv = ((p_ref[...].astype(F32) + l0_ref[...].astype(F32)) + l1_ref[...].astype(F32)) + l2_ref[...].astype(F32)
            nm = ADAM_B1 * m_ref[...] + (1.0 - ADAM_B1) * gv
            nv = ADAM_B2 * v_ref[...] + (1.0 - ADAM_B2) * (gv * gv)
            g_ref[...] = gv
            nm_ref[...] = nm
            nv_ref[...] = nv
            d_ref[...] = -ADAM_LR * ((nm * c1) / (jnp.sqrt(nv * c2) + ADAM_EPS) + ADAM_WD * w_ref[...])

    sub = ADAMW_HALVES
    in_specs, out_specs, out_shape, args, donated = [], [], [], [chip], []
    for layer, w, m, v, part, landed, earlier in items:
        rows, cols = landed.shape[0] // (3 * sub), w.shape[1]

        def block(first, rows=rows, cols=cols):
            return pl.BlockSpec((rows, cols), lambda i, q_ref: (first(q_ref) * sub + i, 0))

        own = block(lambda q_ref, layer=layer: layer)
        in_specs += [own, own, own, block(lambda q_ref: q_ref[0])] + [block(lambda q_ref, j=j: j) for j in range(3)]
        args += [w, m, v, part, landed, landed, landed]
        out_specs += [own] * 4
        out_shape += [jax.ShapeDtypeStruct(w.shape, F32)] * 4
        donated.append(earlier)
    aliases = {}
    for k, earlier in enumerate(donated):
        if earlier is not None:
            for j in range(4):
                aliases[len(args)] = 4 * k + j
                in_specs.append(ANY)
                args.append(earlier[j])
    grid_spec = pltpu.PrefetchScalarGridSpec(
        num_scalar_prefetch=1, grid=(sub,), in_specs=in_specs, out_specs=tuple(out_specs))
    outs = pl.pallas_call(
        body, out_shape=tuple(out_shape), grid_spec=grid_spec, input_output_aliases=aliases,
        compiler_params=_params("arbitrary"), name=name)(*args)
    return [tuple(outs[4 * k:4 * k + 4]) for k in range(ni)]


def _pack(arrays):
    flat = jnp.concatenate([a.reshape(-1).astype(F32) for a in arrays])
    pad = (-flat.shape[0]) % (8 * LANES)
    return jnp.pad(flat, (0, pad)).reshape(-1, LANES)


def _unpack(slab, shapes):
    flat = slab.reshape(slab.shape[:-2] + (-1,))
    out, off = [], 0
    for shp in shapes:
        size = 1
        for s in shp:
            size *= s
        out.append(flat[..., off:off + size].reshape(flat.shape[:-1] + tuple(shp)))
        off += size
    return out


def kernel(x, mem, norm1_g, mem_norm_g, a_w_in, a_q_g, a_k_g, a_rel_bias, b_w_in, b_b_in, b_conv_w, b_conv_b, b_ln_g, b_ln_b, mq_g, mk_g, w_mem_kv, w_out, norm2_g, w_gate, w_up, w_down, loss_target, m_norm1_g, m_mem_norm_g, m_a_w_in, m_a_q_g, m_a_k_g, m_a_rel_bias, m_b_w_in, m_b_b_in, m_b_conv_w, m_b_conv_b, m_b_ln_g, m_b_ln_b, m_mq_g, m_mk_g, m_w_mem_kv, m_w_out, m_norm2_g, m_w_gate, m_w_up, m_w_down, v_norm1_g, v_mem_norm_g, v_a_w_in, v_a_q_g, v_a_k_g, v_a_rel_bias, v_b_w_in, v_b_b_in, v_b_conv_w, v_b_conv_b, v_b_ln_g, v_b_ln_b, v_mq_g, v_mk_g, v_w_mem_kv, v_w_out, v_norm2_g, v_w_gate, v_w_up, v_w_down):
    batch, seq, d = x.shape
    mtok = mem.shape[1]
    n = batch * seq
    ax, ay, ac = _place()
    me = 4 * ax + 2 * ay + ac
    core_arr = jnp.reshape(ac, (1,)).astype(jnp.int32)
    chip_arr = jnp.reshape(2 * ax + ay, (1,)).astype(jnp.int32)

    def t_bf16(w):
        return jnp.transpose(w).astype(BF16)

    def after(value, *earlier):
        return lax.optimization_barrier((value, *earlier))[0]

    def gather_mix(l, when, name, collective_id):
        srcs = [w_mem_kv[l].astype(BF16), w_out[l].astype(BF16)]
        if l == 1:
            srcs += [t_bf16(b_w_in[0]), _pack([b_b_in, b_conv_w, b_conv_b, b_ln_g, b_ln_b])]
        return gather_weights([after(srcs[0], *when)] + srcs[1:], name, collective_id)

    def gather_ffn(l, when, name, collective_id):
        return gather_weights(
            [after(t_bf16(w_gate[l]), *when), t_bf16(w_up[l]), w_down[l].astype(BF16)], name, collective_id)

    f_loc = b_b_in.shape[1]
    c_loc = b_conv_b.shape[1]

    def two(g):
        return jnp.concatenate([g, g], axis=-1)

    gq2, gk2 = two(a_q_g), two(a_k_g)
    rel16 = jnp.pad(a_rel_bias[0], ((0, 16 - a_rel_bias.shape[1]), (0, 0)))
    bias = bias_blocks(rel16)

    x0 = x.reshape(n, d)
    mem2 = mem.reshape(batch * mtok, d)

    saved = []
    xin = x0
    a_win_t, = gather_weights([t_bf16(a_w_in[0])], "gather_in_a", 1)
    wg_t, wu_t, wd, wo, wkv = [None] * 2, [None] * 2, [None] * 2, [None] * 2, [None] * 2
    h = after(rms_fwd(xin, norm1_g[0:1], name="rms1_fwd_0"), bias)
    target = loss_target.reshape(n, d)
    for l in range(2):
        gq4 = jnp.tile(mq_g[l:l + 1], (1, 4))
        gk4 = jnp.tile(mk_g[l:l + 1], (1, 4))
        y_conv = None
        if l == 0:
            wkv[0], wo[0] = gather_mix(0, (h, a_win_t), "gather_mix_a", 2)
            z = mm_nt(h, a_win_t, name="in_proj_a")
            wg_t[0], wu_t[0], wd[0] = gather_ffn(0, (z, wkv[0]), "gather_ffn_a", 3)
            cat = attn_fwd(z, gq2, gk2, bias, batch, seq)
            wkv[1], wo[1], b_win_t, conv_slabs = gather_mix(1, (cat, wg_t[0]), "gather_mix_b", 4)
            qcol = 3 * TOK_WIDTH // MEM_WIDTH
        else:
            small_shapes = [(f_loc,), (CONV_W, c_loc), (c_loc,), (c_loc,), (c_loc,)]
            bb_g, cw_g, cb_g, lg_g, lb_g = _unpack(conv_slabs.reshape(N_DEV, -1, LANES), small_shapes)
            bb_full = bb_g.reshape(1, -1)
            cw_full = jnp.pad(jnp.transpose(cw_g, (1, 0, 2)).reshape(CONV_W, -1), ((0, 32 - CONV_W), (0, 0)))
            cb_full, lg_full, lb_full = cb_g.reshape(1, -1), lg_g.reshape(1, -1), lb_g.reshape(1, -1)
            z = mm_nt(h, b_win_t, bias=bb_full, name="in_proj_b")
            cat, y_conv = conv_fwd(z, cw_full, cb_full, lg_full, lb_full, batch, seq)
            qcol = 2 * TOK_WIDTH // MEM_WIDTH
        cat, mem_n, kv = memattn_fwd(
            z, mem2, mem_norm_g[l:l + 1], wkv[l], gq4, gk4, cat, batch, seq, qcol, name=f"memattn_fwd_{l}")
        x1, h2 = proj_norm(cat, wo[l], xin, norm2_g[l:l + 1], name=f"out_proj_{l}")
        if l == 0:
            wg_t[1], wu_t[1], wd[1] = gather_ffn(1, (x1, b_win_t), "gather_ffn_b", 5)
        if l == 0:
            gate, up, act, x2, h_next = ffn_fwd(h2, wg_t[0], wu_t[0], wd[0], x1, gain=norm1_g[1:2], name="ffn_fwd_0")
        else:
            gate, up, act, dx_b, loss_blk = ffn_fwd(h2, wg_t[1], wu_t[1], wd[1], x1, target=target, name="ffn_fwd_1")
        saved.append(dict(xin=xin, h=h, mem_n=mem_n, kv=kv, gq4=gq4, gk4=gk4, z=z, qcol=qcol, cat=cat, x1=x1, h2=h2,
                          gate=gate, up=up, act=act, y_conv=y_conv))
        if l == 0:
            xin, h = x2, h_next

    big = {}
    small = {}
    reduced = {}
    groups = 0

    def scatter_siblings(keys):
        nonlocal groups
        gid = groups
        groups += 1
        return gid, keys, scatter_to_sibling([big[k] for k in keys], f"scatter_sibling_{gid}", 8 + 2 * gid)

    def scatter_chips(stage1, when):
        gid, keys, landed1 = stage1
        parts = add_sibling([after(big[keys[0]], when)] + [big[k] for k in keys[1:]], landed1, core_arr,
                            name=f"add_sibling_{gid}")
        landed2 = scatter_to_chips(parts, f"scatter_chips_{gid}", 9 + 2 * gid)
        for k, p, ld in zip(keys, parts, landed2):
            reduced[k] = (p, ld)
        return parts, landed2

    def rows_of(w, transposed):
        w = jnp.swapaxes(w, 1, 2) if transposed else w
        return w.reshape(w.shape[0] * w.shape[1], w.shape[2])

    sharded = {
        "win0": (2, True), "win1": (6, True), "wkv": (14, False), "wo": (15, False),
        "wg": (17, True), "wu": (18, True), "wd": (19, False)}
    weights = [norm1_g, mem_norm_g, a_w_in, a_q_g, a_k_g, a_rel_bias, b_w_in, b_b_in, b_conv_w, b_conv_b, b_ln_g,
               b_ln_b, mq_g, mk_g, w_mem_kv, w_out, norm2_g, w_gate, w_up, w_down]
    moms = [m_norm1_g, m_mem_norm_g, m_a_w_in, m_a_q_g, m_a_k_g, m_a_rel_bias, m_b_w_in, m_b_b_in, m_b_conv_w,
            m_b_conv_b, m_b_ln_g, m_b_ln_b, m_mq_g, m_mk_g, m_w_mem_kv, m_w_out, m_norm2_g, m_w_gate, m_w_up, m_w_down]
    vels = [v_norm1_g, v_mem_norm_g, v_a_w_in, v_a_q_g, v_a_k_g, v_a_rel_bias, v_b_w_in, v_b_b_in, v_b_conv_w,
            v_b_conv_b, v_b_ln_g, v_b_ln_b, v_mq_g, v_mk_g, v_w_mem_kv, v_w_out, v_norm2_g, v_w_gate, v_w_up, v_w_down]
    updated = {}

    def update_layer(l, when):
        for group, keys in (("ffn", ("wg", "wu", "wd")), ("mix", (f"win{l}", "wkv", "wo"))):
            items = []
            for key in keys:
                idx, transposed = sharded[key]
                layer, rkey = (0, key) if key.startswith("win") else (l, f"{key}{l}")
                part, landed = reduced[rkey]
                w_rows = rows_of(weights[idx], transposed)
                items.append((layer, after(w_rows, when) if not items else w_rows, rows_of(moms[idx], transposed),
                              rows_of(vels[idx], transposed), part, landed, updated.get(key)))
            for key, result in zip(keys, adamw_shards(items, chip_arr, name=f"adamw_{group}_{l}")):
                updated[key] = result

    mix_landed = None
    for l in (1, 0):
        sv = saved[l]
        dgate, dup, dx1_b, dcat, small[f"norm2_{l}"] = ffn_bwd(
            dx_b, wd[l], sv["gate"], sv["up"], wg_t[l], wu_t[l], sv["x1"], norm2_g[l:l + 1], wo[l], name=f"ffn_bwd_{l}")
        if l == 0:
            dgate = after(dgate, *mix_landed)
            update_layer(1, dx1_b)
        big[f"wg{l}"], big[f"wu{l}"], big[f"wd{l}"] = ffn_weight_grads(
            dgate, dup, sv["h2"], sv["act"], dx_b, name=f"grad_ffn_{l}")
        stage1 = scatter_siblings([f"wd{l}", f"wg{l}", f"wu{l}"])
        big[f"wo{l}"] = mm_tn(sv["cat"], dx1_b, name=f"grad_wo_{l}")
        parts, ffn_landed = scatter_chips(stage1, big[f"wo{l}"])
        dcat = after(dcat, *parts)
        if l == 0:
            dq, dk, dv, dbias, small["a_q"], small["a_k"] = attn_bwd(sv["z"], dcat, gq2, gk2, bias, batch, seq)
            small["rel"] = bias_grad(dbias)
            win_t = a_win_t
            dz = None
            dcat = after(dcat, dq, *ffn_landed)
        else:
            dz, small["cw"], small["csum"] = conv_bwd(sv["z"], sv["y_conv"], dcat, cw_full, lg_full, lb_full, batch, seq)
            win_t = b_win_t
            dz = after(dz, *ffn_landed)
        dqm, small[f"mq_{l}"], small[f"mk_{l}"], big[f"wkv{l}"], small[f"memnorm_{l}"] = memattn_bwd(
            sv["z"], sv["kv"], dcat, sv["gq4"], sv["gk4"], mem2, sv["mem_n"], wkv[l], dz, batch, seq, sv["qcol"],
            name=f"memattn_bwd_{l}")
        if l == 0:
            pieces = [dq, dk, dv, dqm]
            big["win0"] = grad_pieces(pieces, sv["h"], name="grad_win_0")
        else:
            pieces = [dqm]
            big["win1"] = mm_tn(dqm, sv["h"], name="grad_win_1")
        stage1 = scatter_siblings([f"win{l}", f"wkv{l}", f"wo{l}"])
        dx_b, small[f"norm1_{l}"], dz_sum = in_proj_bwd(
            pieces, win_t, sv["xin"], norm1_g[l:l + 1], dx1_b, BF16 if l == 1 else F32, name=f"in_proj_bwd_{l}")
        if l == 1:
            small["bb"] = dz_sum
        parts, mix_landed = scatter_chips(stage1, dx_b)
        dx_b = after(dx_b, *parts)
    grad_x = dx_b.reshape(batch, seq, d)
    update_layer(0, dx_b)

    def shaped(rows, idx, transposed):
        shp = weights[idx].shape
        if transposed:
            return jnp.swapaxes(rows.reshape(shp[0], shp[2], shp[1]), 1, 2)
        return rows.reshape(shp)

    def fold(v, groups):
        return jnp.sum(v.reshape(groups, HEAD_DIM), axis=0, keepdims=True)

    heads = a_rel_bias.shape[1]
    small_list = [
        jnp.concatenate([small["norm1_0"], small["norm1_1"]]),
        jnp.concatenate([small["memnorm_0"], small["memnorm_1"]]),
        fold(small["a_q"], 2), fold(small["a_k"], 2), small["rel"][:heads],
        small["bb"], small["cw"][:CONV_W], small["csum"][0:1], small["csum"][1:2], small["csum"][2:3],
        jnp.concatenate([fold(small["mq_0"], 4), fold(small["mq_1"], 4)]),
        jnp.concatenate([fold(small["mk_0"], 4), fold(small["mk_1"], 4)]),
        jnp.concatenate([small["norm2_0"], small["norm2_1"]]),
    ]
    (g_norm1, g_memnorm, g_aq, g_ak, g_rel, g_bb_full, g_cw_full, g_cb_full, g_lg_full, g_lb_full,
     g_mq, g_mk, g_norm2, loss_sum) = reduce_small(small_list + [loss_blk])
    loss = loss_sum[0, 0]
    g_bb = lax.dynamic_slice_in_dim(g_bb_full, me * f_loc, f_loc, axis=1)
    g_cw = lax.dynamic_slice_in_dim(g_cw_full, me * c_loc, c_loc, axis=1)
    g_cb = lax.dynamic_slice_in_dim(g_cb_full, me * c_loc, c_loc, axis=1)
    g_lg = lax.dynamic_slice_in_dim(g_lg_full, me * c_loc, c_loc, axis=1)
    g_lb = lax.dynamic_slice_in_dim(g_lb_full, me * c_loc, c_loc, axis=1)

    grads = [g_norm1, g_memnorm, None, g_aq, g_ak, g_rel, None, g_bb, g_cw, g_cb, g_lg, g_lb,
             g_mq, g_mk, None, None, g_norm2, None, None, None]
    deltas, new_m, new_v = [None] * 20, [None] * 20, [None] * 20
    for key, (idx, transposed) in sharded.items():
        grads[idx], deltas[idx], new_m[idx], new_v[idx] = (shaped(r, idx, transposed) for r in updated[key])

    def flat2(a):
        return a.reshape(a.shape[-2:])

    small_idx = [i for i in range(20) if i not in {idx for idx, _ in sharded.values()}]
    dl, nm, nv = adamw_small([flat2(weights[i]) for i in small_idx], [flat2(grads[i]) for i in small_idx],
                             [flat2(moms[i]) for i in small_idx], [flat2(vels[i]) for i in small_idx])
    for i, a, b, cc in zip(small_idx, dl, nm, nv):
        shp = weights[i].shape
        grads[i], deltas[i], new_m[i], new_v[i] = grads[i].reshape(shp), a.reshape(shp), b.reshape(shp), cc.reshape(shp)

    return (loss, grad_x, *grads, *deltas, *new_m, *new_v)
```

```python
import jax
import jax.numpy as jnp
from jax import lax
from jax.experimental import pallas as pl
from jax.experimental.pallas import tpu as pltpu
from jax.experimental.pallas import tpu_sc as plsc

F32 = jnp.float32
BF16 = jnp.bfloat16
HIGHEST = lax.Precision.HIGHEST
MESH = pl.DeviceIdType.MESH
ANY = pl.BlockSpec(memory_space=pl.ANY)

N_DEV = 8
D_MODEL = 1024
HEAD_DIM = 64
TOK_WIDTH = 768
MEM_WIDTH = 256
CHUNK = 64
Q_BLOCK = 256
KEY_WIN = 768
BAND = 576
N_REL = 192
CONV_W = 31
CONV_HALO = 32
NORM_EPS = 1e-6
NEG_INF = -1e30
ATTN_SCALE = HEAD_DIM ** -0.5
LANES = 128
ROW_TILE = 512
VMEM_LIMIT = 56 * 1024 * 1024

ADAM_LR, ADAM_B1, ADAM_B2, ADAM_EPS, ADAM_WD, ADAM_STEP = 0.001, 0.9, 0.999, 1e-08, 0.01, 10


def _params(*sem):
    return pltpu.CompilerParams(dimension_semantics=sem, vmem_limit_bytes=VMEM_LIMIT)


WIDE_ROW_TILE = 1024


def _row_tile(m, rows=ROW_TILE):
    return rows if m % rows == 0 else m


def _col_tile(n, cap=1408):
    best = None
    for t in range(LANES, min(n, cap) + 1, LANES):
        if n % t == 0:
            best = t
    return best if best is not None else n


def _dot(a, b, ca, cb):
    return lax.dot_general(a, b, (((ca,), (cb,)), ((), ())), preferred_element_type=F32)


def _sigmoid(x):
    return 0.5 * jnp.tanh(0.5 * x) + 0.5


def mm_nt(a, b, bias=None, out_dtype=BF16, name="mm_nt"):
    m, k = a.shape
    n = b.shape[0]
    tm, tn = _row_tile(m, WIDE_ROW_TILE), _col_tile(n)

    def body(*refs):
        a_ref, b_ref = refs[0], refs[1]
        o_ref = refs[-1]
        acc = _dot(a_ref[...].astype(BF16), b_ref[...].astype(BF16), 1, 1)
        if bias is not None:
            acc = acc + refs[2][...]
        o_ref[...] = acc.astype(o_ref.dtype)

    in_specs = [pl.BlockSpec((tm, k), lambda j, i: (i, 0)), pl.BlockSpec((tn, k), lambda j, i: (j, 0))]
    args = [a, b]
    if bias is not None:
        in_specs.append(pl.BlockSpec((1, tn), lambda j, i: (0, j)))
        args.append(bias)
    return pl.pallas_call(
        body, out_shape=jax.ShapeDtypeStruct((m, n), out_dtype), grid=(n // tn, m // tm),
        in_specs=in_specs, out_specs=pl.BlockSpec((tm, tn), lambda j, i: (i, j)),
        compiler_params=_params("parallel", "arbitrary"), name=name)(*args)


def mm_tn(a, b, out_dtype=BF16, name="mm_tn"):
    t, r = a.shape
    c = b.shape[1]
    tr = _col_tile(r, 512)

    def body(a_ref, b_ref, o_ref):
        o_ref[...] = _dot(a_ref[...].astype(BF16), b_ref[...].astype(BF16), 0, 0).astype(o_ref.dtype)

    return pl.pallas_call(
        body, out_shape=jax.ShapeDtypeStruct((r, c), out_dtype), grid=(r // tr,),
        in_specs=[pl.BlockSpec((t, tr), lambda i: (0, i)), pl.BlockSpec((t, c), lambda i: (0, 0))],
        out_specs=pl.BlockSpec((tr, c), lambda i: (i, 0)),
        compiler_params=_params("parallel"), name=name)(a, b)


def _resident(shape):
    return pl.BlockSpec(shape, lambda i: (0, 0), pipeline_mode=pl.Buffered(1))


def proj_norm(a, b, res, gain, name):
    m, k = a.shape
    n = b.shape[1]
    tm = _row_tile(m)

    def body(a_ref, b_ref, res_ref, g_ref, x_ref, h_ref):
        xv = res_ref[...] + _dot(a_ref[...], b_ref[...], 1, 0)
        x_ref[...] = xv
        r = lax.rsqrt(jnp.mean(xv * xv, axis=-1, keepdims=True) + NORM_EPS)
        h_ref[...] = (xv * r * g_ref[...]).astype(BF16)

    row = pl.BlockSpec((tm, n), lambda i: (i, 0))
    return pl.pallas_call(
        body, out_shape=(jax.ShapeDtypeStruct((m, n), F32), jax.ShapeDtypeStruct((m, n), BF16)), grid=(m // tm,),
        in_specs=[pl.BlockSpec((tm, k), lambda i: (i, 0)), _resident((k, n)), row, _resident((1, n))],
        out_specs=(row, row), compiler_params=_params("parallel"), name=name)(a, b, res, gain)


def in_proj_bwd(pieces, w_t, x, gain, dres, out_dtype, name):
    m, n = x.shape
    k = pieces[0].shape[1]
    tm = _row_tile(m)
    npc = len(pieces)
    offs = [sum(p.shape[1] for p in pieces[:i]) for i in range(npc + 1)]

    def body(*refs):
        dz_refs = refs[:npc]
        w_ref, x_ref, g_ref, dres_ref, dx_ref, dg_ref, cs_ref = refs[npc:]

        @pl.when(pl.program_id(0) == 0)
        def _():
            dg_ref[...] = jnp.zeros_like(dg_ref)
            cs_ref[...] = jnp.zeros_like(cs_ref)

        cs_ref[...] += jnp.sum(dz_refs[0][...].astype(F32), axis=0, keepdims=True)
        dhv = _dot(dz_refs[0][...], w_ref[offs[0]:offs[1], :], 1, 0)
        for i in range(1, npc):
            dhv = dhv + _dot(dz_refs[i][...], w_ref[offs[i]:offs[i + 1], :], 1, 0)
        xv = x_ref[...]
        r = lax.rsqrt(jnp.mean(xv * xv, axis=-1, keepdims=True) + NORM_EPS)
        xhat = xv * r
        dg_ref[...] += jnp.sum(dhv * xhat, axis=0, keepdims=True)
        dxhat = dhv * g_ref[...]
        dx = dres_ref[...].astype(F32) + r * (dxhat - xhat * jnp.mean(dxhat * xhat, axis=-1, keepdims=True))
        dx_ref[...] = dx.astype(dx_ref.dtype)

    row = pl.BlockSpec((tm, n), lambda i: (i, 0))
    return pl.pallas_call(
        body, out_shape=(jax.ShapeDtypeStruct((m, n), out_dtype), jax.ShapeDtypeStruct((1, n), F32),
                         jax.ShapeDtypeStruct((1, k), F32)), grid=(m // tm,),
        in_specs=[pl.BlockSpec((tm, p.shape[1]), lambda i: (i, 0)) for p in pieces]
        + [_resident(w_t.shape), row, _resident((1, n)), row],
        out_specs=(row, pl.BlockSpec((1, n), lambda i: (0, 0)), pl.BlockSpec((1, k), lambda i: (0, 0))),
        compiler_params=_params("arbitrary"), name=name)(*pieces, w_t, x, gain, dres)


def grad_pieces(pieces, b, name):
    t, c = b.shape
    tr = 2 * LANES
    tiles = [p.shape[1] // tr for p in pieces]
    starts = [sum(tiles[:i]) for i in range(len(pieces) + 1)]

    def body(*refs):
        a_refs, b_ref, o_ref = refs[:len(pieces)], refs[len(pieces)], refs[len(pieces) + 1]
        i = pl.program_id(0)
        for p, a_ref in enumerate(a_refs):
            @pl.when((i >= starts[p]) & (i < starts[p + 1]))
            def _(a_ref=a_ref):
                o_ref[...] = _dot(a_ref[...], b_ref[...], 0, 0).astype(o_ref.dtype)

    def a_spec(p):
        return pl.BlockSpec((t, tr), lambda i: (0, jnp.clip(i - starts[p], 0, tiles[p] - 1)))

    return pl.pallas_call(
        body, out_shape=jax.ShapeDtypeStruct((starts[-1] * tr, c), BF16), grid=(starts[-1],),
        in_specs=[a_spec(p) for p in range(len(pieces))] + [_resident((t, c))],
        out_specs=pl.BlockSpec((tr, c), lambda i: (i, 0)),
        compiler_params=_params("arbitrary"), name=name)(*pieces, b)


FFN_ROWS = 256


def _ffn_row_tile(m):
    return FFN_ROWS if m % FFN_ROWS == 0 else m


def ffn_fwd(h2, wg_t, wu_t, wd, x1, gain=None, target=None, name="ffn_fwd"):
    n, d = h2.shape
    f = wg_t.shape[0]
    tm = _ffn_row_tile(n)
    nt = n // tm
    last = target is not None

    def body(h_ref, wg_ref, wu_ref, wd_ref, x1_ref, e_ref, g_ref, u_ref, a_ref, *rest):
        hv = h_ref[...]
        gv = _dot(hv, wg_ref[...], 1, 1)
        uv = _dot(hv, wu_ref[...], 1, 1)
        g_ref[...] = gv.astype(BF16)
        u_ref[...] = uv.astype(BF16)
        av = (gv * _sigmoid(gv) * uv).astype(BF16)
        a_ref[...] = av
        xv = x1_ref[...] + _dot(av, wd_ref[...], 1, 0)
        if not last:
            x_ref, hn_ref = rest
            x_ref[...] = xv
            r = lax.rsqrt(jnp.mean(xv * xv, axis=-1, keepdims=True) + NORM_EPS)
            hn_ref[...] = (xv * r * e_ref[...]).astype(BF16)
        else:
            dyb_ref, l_ref, acc_ref = rest
            i = pl.program_id(0)

            @pl.when(i == 0)
            def _():
                acc_ref[...] = jnp.zeros_like(acc_ref)

            err = xv - e_ref[...]
            dyb_ref[...] = (err * (1.0 / d)).astype(BF16)
            acc_ref[...] += jnp.sum(err * err, axis=0, keepdims=True)

            @pl.when(i == nt - 1)
            def _():
                total = jnp.sum(acc_ref[...], axis=-1, keepdims=True) * (0.5 / d)
                l_ref[...] = jnp.broadcast_to(total, l_ref.shape)

    row_d = pl.BlockSpec((tm, d), lambda i: (i, 0))
    row_f = pl.BlockSpec((tm, f), lambda i: (i, 0))
    act_shape = jax.ShapeDtypeStruct((n, f), BF16)
    if not last:
        extra_in, extra = _resident((1, d)), gain
        out_shape = (act_shape, act_shape, act_shape, jax.ShapeDtypeStruct((n, d), F32), jax.ShapeDtypeStruct((n, d), BF16))
        out_specs = (row_f, row_f, row_f, row_d, row_d)
        scratch = []
    else:
        extra_in, extra = row_d, target
        out_shape = (act_shape, act_shape, act_shape, jax.ShapeDtypeStruct((n, d), BF16),
                     jax.ShapeDtypeStruct((8, LANES), F32))
        out_specs = (row_f, row_f, row_f, row_d, pl.BlockSpec((8, LANES), lambda i: (0, 0)))
        scratch = [pltpu.VMEM((1, d), F32)]
    return pl.pallas_call(
        body, out_shape=out_shape, grid=(nt,),
        in_specs=[row_d, _resident((f, d)), _resident((f, d)), _resident((f, d)), row_d, extra_in],
        out_specs=out_specs, scratch_shapes=scratch,
        compiler_params=_params("arbitrary"), name=name)(h2, wg_t, wu_t, wd, x1, extra)


def ffn_bwd(dx_b, wd, gate, up, wg_t, wu_t, x1, gain, wo, name="ffn_bwd"):
    n, d = x1.shape
    f = wd.shape[0]
    tm = _ffn_row_tile(n)

    def body(dxb_ref, wd_ref, g_ref, u_ref, wg_ref, wu_ref, x_ref, gain_ref, wo_ref,
             dg_ref, du_ref, dxo_ref, dc_ref, dgain_ref):
        @pl.when(pl.program_id(0) == 0)
        def _():
            dgain_ref[...] = jnp.zeros_like(dgain_ref)

        dact = _dot(dxb_ref[...], wd_ref[...], 1, 1)
        gv = g_ref[...].astype(F32)
        uv = u_ref[...].astype(F32)
        sg = _sigmoid(gv)
        dgv = (dact * uv * sg * (1.0 + gv * (1.0 - sg))).astype(BF16)
        duv = (dact * gv * sg).astype(BF16)
        dg_ref[...] = dgv
        du_ref[...] = duv
        dhv = _dot(dgv, wg_ref[...], 1, 0) + _dot(duv, wu_ref[...], 1, 0)
        xv = x_ref[...]
        r = lax.rsqrt(jnp.mean(xv * xv, axis=-1, keepdims=True) + NORM_EPS)
        xhat = xv * r
        dgain_ref[...] += jnp.sum(dhv * xhat, axis=0, keepdims=True)
        dxhat = dhv * gain_ref[...]
        dxb = (dxb_ref[...].astype(F32) + r * (dxhat - xhat * jnp.mean(dxhat * xhat, axis=-1, keepdims=True))).astype(BF16)
        dxo_ref[...] = dxb
        dc_ref[...] = _dot(dxb, wo_ref[...], 1, 1).astype(BF16)

    row_d = pl.BlockSpec((tm, d), lambda i: (i, 0))
    row_f = pl.BlockSpec((tm, f), lambda i: (i, 0))
    w_spec = _resident((f, d))
    act_shape = jax.ShapeDtypeStruct((n, f), BF16)
    row_shape = jax.ShapeDtypeStruct((n, d), BF16)
    return pl.pallas_call(
        body, out_shape=(act_shape, act_shape, row_shape, jax.ShapeDtypeStruct((n, wo.shape[0]), BF16),
                         jax.ShapeDtypeStruct((1, d), F32)),
        grid=(n // tm,),
        in_specs=[row_d, w_spec, row_f, row_f, w_spec, w_spec, row_d, _resident((1, d)), _resident(wo.shape)],
        out_specs=(row_f, row_f, row_d, pl.BlockSpec((tm, wo.shape[0]), lambda i: (i, 0)),
                   pl.BlockSpec((1, d), lambda i: (0, 0))),
        compiler_params=_params("arbitrary"), name=name)(dx_b, wd, gate, up, wg_t, wu_t, x1, gain, wo)


def ffn_weight_grads(dgate, dup, h2, act, dx_b, name="ffn_weight_grads"):
    t, r = dgate.shape
    c = h2.shape[1]
    tr = _col_tile(r, 512)

    def body(a1_ref, a2_ref, a3_ref, b12_ref, b3_ref, o1_ref, o2_ref, o3_ref):
        bv = b12_ref[...]
        o1_ref[...] = _dot(a1_ref[...], bv, 0, 0).astype(o1_ref.dtype)
        o2_ref[...] = _dot(a2_ref[...], bv, 0, 0).astype(o2_ref.dtype)
        o3_ref[...] = _dot(a3_ref[...], b3_ref[...], 0, 0).astype(o3_ref.dtype)

    a_spec = pl.BlockSpec((t, tr), lambda i: (0, i))
    o_spec = pl.BlockSpec((tr, c), lambda i: (i, 0))
    shape = jax.ShapeDtypeStruct((r, c), BF16)
    return pl.pallas_call(
        body, out_shape=(shape, shape, shape), grid=(r // tr,),
        in_specs=[a_spec, a_spec, a_spec, _resident((t, c)), _resident((t, c))],
        out_specs=(o_spec, o_spec, o_spec), compiler_params=_params("parallel"), name=name)(dgate, dup, act, h2, dx_b)


def rms_fwd(x, g, name="rms_fwd"):
    n, d = x.shape
    tm = _row_tile(n)

    def body(x_ref, g_ref, o_ref):
        xv = x_ref[...]
        r = lax.rsqrt(jnp.mean(xv * xv, axis=-1, keepdims=True) + NORM_EPS)
        o_ref[...] = (xv * r * g_ref[...]).astype(o_ref.dtype)

    return pl.pallas_call(
        body, out_shape=jax.ShapeDtypeStruct((n, d), BF16), grid=(n // tm,),
        in_specs=[pl.BlockSpec((tm, d), lambda i: (i, 0)), pl.BlockSpec((1, d), lambda i: (0, 0))],
        out_specs=pl.BlockSpec((tm, d), lambda i: (i, 0)),
        compiler_params=_params("parallel"), name=name)(x, g)


def _group_masks(width):
    lane = lax.broadcasted_iota(jnp.int32, (1, width), 1)
    return [(lane >= HEAD_DIM * g) & (lane < HEAD_DIM * (g + 1)) for g in range(width // HEAD_DIM)]


def _group_sum(x, masks):
    out = jnp.zeros_like(x)
    for msk in masks:
        s = jnp.sum(jnp.where(msk, x, 0.0), axis=-1, keepdims=True)
        out = jnp.where(msk, s, out)
    return out


def _head_norm(x, gain, masks):
    r = lax.rsqrt(_group_sum(x * x, masks) * (1.0 / HEAD_DIM) + NORM_EPS)
    xhat = x * r
    return xhat * gain, xhat, r


def _head_norm_bwd(dxn, xhat, r, gain, masks):
    dgain = jnp.sum(dxn * xhat, axis=0, keepdims=True)
    dxhat = dxn * gain
    mean_t = _group_sum(dxhat * xhat, masks) * (1.0 / HEAD_DIM)
    return r * (dxhat - xhat * mean_t), dgain


def _softmax_rows(s):
    e = jnp.exp(s - jnp.max(s, axis=-1, keepdims=True))
    return e * (1.0 / jnp.sum(e, axis=-1, keepdims=True))


def _rel_onehot():
    col = lax.broadcasted_iota(jnp.int32, (1, KEY_WIN), 1)
    off = jnp.where(col < KEY_WIN - LANES, col, col - KEY_WIN)
    idx = jnp.clip(8 * CHUNK - off, -(CHUNK - 1), LANES) + (CHUNK - 1)
    return (lax.broadcasted_iota(jnp.int32, (N_REL, KEY_WIN), 0) == idx).astype(F32)


def bias_blocks(rel16):
    heads = TOK_WIDTH // HEAD_DIM

    def body(rel_ref, o_ref, u_ref):
        u_ref[...] = jnp.dot(rel_ref[...], _rel_onehot(), precision=HIGHEST, preferred_element_type=F32)
        row = lax.broadcasted_iota(jnp.int32, (CHUNK, KEY_WIN), 0)
        col = lax.broadcasted_iota(jnp.int32, (CHUNK, KEY_WIN), 1)
        for h in range(heads):
            xv = jnp.broadcast_to(u_ref[h:h + 1, :], (CHUNK, KEY_WIN))
            for b in range(6):
                xv = jnp.where(((row >> b) & 1) == 1, pltpu.roll(xv, 1 << b, axis=1), xv)
            xv = jnp.where(col < BAND, xv, NEG_INF)
            for i in range(Q_BLOCK // CHUNK):
                o_ref[h, CHUNK * i:CHUNK * (i + 1), :] = pltpu.roll(xv, CHUNK * i, axis=1) if i else xv

    return pl.pallas_call(
        body, out_shape=jax.ShapeDtypeStruct((heads, Q_BLOCK, KEY_WIN), F32),
        scratch_shapes=[pltpu.VMEM((16, KEY_WIN), F32)], name="bias_blocks")(rel16)


def bias_grad(dbias):
    heads = dbias.shape[0]

    def body(db_ref, o_ref, y_ref):
        y_ref[...] = jnp.zeros_like(y_ref)
        row = lax.broadcasted_iota(jnp.int32, (CHUNK, KEY_WIN), 0)
        for h in range(heads):
            fv = db_ref[h, 0:CHUNK, :]
            for i in range(1, Q_BLOCK // CHUNK):
                fv = fv + pltpu.roll(db_ref[h, CHUNK * i:CHUNK * (i + 1), :], KEY_WIN - CHUNK * i, axis=1)
            for b in range(6):
                fv = jnp.where(((row >> b) & 1) == 1, pltpu.roll(fv, KEY_WIN - (1 << b), axis=1), fv)
            y_ref[h:h + 1, :] = jnp.sum(fv, axis=0, keepdims=True)
        o_ref[...] = lax.dot_general(y_ref[...], _rel_onehot(), (((1,), (1,)), ((), ())),
                                     precision=HIGHEST, preferred_element_type=F32)

    return pl.pallas_call(
        body, out_shape=jax.ShapeDtypeStruct((16, N_REL), F32),
        scratch_shapes=[pltpu.VMEM((16, KEY_WIN), F32)], name="bias_grad")(dbias)


def _attn_windows(seq):
    out = []
    for j in range(seq // Q_BLOCK):
        r0 = j * Q_BLOCK
        k0 = max(0, r0 - 8 * CHUNK)
        width = r0 + Q_BLOCK - k0
        out.append((r0, k0, width, KEY_WIN - width))
    return out


def attn_fwd(z, gq2, gk2, bias, batch, seq):
    n = z.shape[0]
    pairs = TOK_WIDTH // LANES

    def body(q_ref, k_ref, v_ref, gq_ref, gk_ref, b_ref, o_ref, qs_s, kn_s):
        masks = _group_masks(LANES)
        qs_s[...] = (_head_norm(q_ref[...].astype(F32), gq_ref[...], masks)[0] * ATTN_SCALE).astype(BF16)
        kn_s[...] = _head_norm(k_ref[...].astype(F32), gk_ref[...], masks)[0].astype(BF16)
        for r0, k0, width, c0 in _attn_windows(seq):
            qb = qs_s[r0:r0 + Q_BLOCK, :]
            kw = kn_s[k0:k0 + width, :]
            vw = v_ref[k0:k0 + width, :]
            out = jnp.zeros((Q_BLOCK, LANES), F32)
            for h, msk in enumerate(masks):
                qh = jnp.where(msk, qb, jnp.zeros_like(qb))
                s = _dot(qh, kw, 1, 1) + b_ref[h, :, c0:KEY_WIN]
                p = _softmax_rows(s).astype(BF16)
                out = jnp.where(msk, _dot(p, vw, 1, 0), out)
            o_ref[r0:r0 + Q_BLOCK, :] = out.astype(o_ref.dtype)

    def col(off):
        return pl.BlockSpec((seq, LANES), lambda b, p: (b, off + p))

    vec = pl.BlockSpec((1, LANES), lambda b, p: (0, 0))
    return pl.pallas_call(
        body, out_shape=jax.ShapeDtypeStruct((n, D_MODEL), BF16), grid=(batch, pairs),
        in_specs=[col(0), col(pairs), col(2 * pairs), vec, vec,
                  pl.BlockSpec((2, Q_BLOCK, KEY_WIN), lambda b, p: (p, 0, 0))],
        out_specs=pl.BlockSpec((seq, LANES), lambda b, p: (b, p)),
        scratch_shapes=[pltpu.VMEM((seq, LANES), BF16), pltpu.VMEM((seq, LANES), BF16)],
        compiler_params=_params("parallel", "arbitrary"), name="attn_fwd")(z, z, z, gq2, gk2, bias)


def attn_bwd(z, dcat, gq2, gk2, bias, batch, seq):
    n = z.shape[0]
    pairs = TOK_WIDTH // LANES

    def body(q_ref, k_ref, v_ref, do_ref, gq_ref, gk_ref, b_ref,
             dq_ref, dk_ref, dv_ref, db_ref, dgq_ref, dgk_ref, qs_s, kn_s, dqn_s, dkn_s, dv_s):
        pi, bi = pl.program_id(0), pl.program_id(1)
        masks = _group_masks(LANES)

        @pl.when(bi == 0)
        def _():
            db_ref[...] = jnp.zeros_like(db_ref)

        @pl.when((bi == 0) & (pi == 0))
        def _():
            dgq_ref[...] = jnp.zeros_like(dgq_ref)
            dgk_ref[...] = jnp.zeros_like(dgk_ref)

        qn, qhat, rq = _head_norm(q_ref[...].astype(F32), gq_ref[...], masks)
        kn, khat, rk = _head_norm(k_ref[...].astype(F32), gk_ref[...], masks)
        qs_s[...] = (qn * ATTN_SCALE).astype(BF16)
        kn_s[...] = kn.astype(BF16)
        dkn_s[...] = jnp.zeros_like(dkn_s)
        dv_s[...] = jnp.zeros_like(dv_s)
        for r0, k0, width, c0 in _attn_windows(seq):
            qb = qs_s[r0:r0 + Q_BLOCK, :]
            dob = do_ref[r0:r0 + Q_BLOCK, :]
            kw = kn_s[k0:k0 + width, :]
            vw = v_ref[k0:k0 + width, :]
            dq_acc = jnp.zeros((Q_BLOCK, LANES), F32)
            dk_acc = jnp.zeros((width, LANES), F32)
            dv_acc = jnp.zeros((width, LANES), F32)
            for h, msk in enumerate(masks):
                qh = jnp.where(msk, qb, jnp.zeros_like(qb))
                doh = jnp.where(msk, dob, jnp.zeros_like(dob))
                p = _softmax_rows(_dot(qh, kw, 1, 1) + b_ref[h, :, c0:KEY_WIN])
                dp = _dot(doh, vw, 1, 1)
                ds = p * (dp - jnp.sum(p * dp, axis=-1, keepdims=True))
                db_ref[h, :, c0:KEY_WIN] += ds
                dsb = ds.astype(BF16)
                dq_acc = jnp.where(msk, _dot(dsb, kw, 1, 0), dq_acc)
                dk_acc = jnp.where(msk, _dot(dsb, qb, 0, 0), dk_acc)
                dv_acc = jnp.where(msk, _dot(p.astype(BF16), dob, 0, 0), dv_acc)
            dqn_s[r0:r0 + Q_BLOCK, :] = dq_acc * ATTN_SCALE
            dkn_s[k0:k0 + width, :] += dk_acc
            dv_s[k0:k0 + width, :] += dv_acc
        dq, dgq = _head_norm_bwd(dqn_s[...], qhat, rq, gq_ref[...], masks)
        dk, dgk = _head_norm_bwd(dkn_s[...], khat, rk, gk_ref[...], masks)
        dq_ref[...] = dq.astype(dq_ref.dtype)
        dk_ref[...] = dk.astype(dk_ref.dtype)
        dv_ref[...] = dv_s[...].astype(dv_ref.dtype)
        dgq_ref[...] += dgq
        dgk_ref[...] += dgk

    def col(off):
        return pl.BlockSpec((seq, LANES), lambda p, b: (b, off + p))

    vec = pl.BlockSpec((1, LANES), lambda p, b: (0, 0))
    blk = pl.BlockSpec((2, Q_BLOCK, KEY_WIN), lambda p, b: (p, 0, 0))
    o_shape = jax.ShapeDtypeStruct((n, TOK_WIDTH), BF16)
    v_shape = jax.ShapeDtypeStruct((1, LANES), F32)
    return pl.pallas_call(
        body,
        out_shape=(o_shape, o_shape, o_shape, jax.ShapeDtypeStruct(bias.shape, F32), v_shape, v_shape),
        grid=(pairs, batch),
        in_specs=[col(0), col(pairs), col(2 * pairs), col(0), vec, vec, blk],
        out_specs=(col(0), col(0), col(0), blk, vec, vec),
        scratch_shapes=[pltpu.VMEM((seq, LANES), BF16), pltpu.VMEM((seq, LANES), BF16),
                        pltpu.VMEM((seq, LANES), F32), pltpu.VMEM((seq, LANES), F32), pltpu.VMEM((seq, LANES), F32)],
        compiler_params=_params("arbitrary", "arbitrary"), name="attn_bwd")(z, z, z, dcat, gq2, gk2, bias)


MEM_ROWS = 512


def memattn_fwd(z, mem, mem_gain, wkv, gq4, gk4, cat, batch, seq, qcol, name):
    mtok = mem.shape[0] // batch
    d = mem.shape[1]
    rows = min(MEM_ROWS, seq)

    def body(q_ref, m_ref, mg_ref, w_ref, gq_ref, gk_ref, cat_ref, o_ref, n_ref, kv_ref):
        del cat_ref
        masks = _group_masks(MEM_WIDTH)
        mv = m_ref[...]
        r = lax.rsqrt(jnp.mean(mv * mv, axis=-1, keepdims=True) + NORM_EPS)
        nv = (mv * r * mg_ref[...]).astype(BF16)
        n_ref[...] = nv
        kv_ref[...] = _dot(nv, w_ref[...], 1, 0)
        kn = _head_norm(kv_ref[:, 0:MEM_WIDTH], gk_ref[...], masks)[0].astype(BF16)
        vm = kv_ref[:, MEM_WIDTH:2 * MEM_WIDTH].astype(BF16)
        for t in range(seq // rows):
            sl = slice(t * rows, (t + 1) * rows)
            qs = (_head_norm(q_ref[sl, :].astype(F32), gq_ref[...], masks)[0] * ATTN_SCALE).astype(BF16)
            out = jnp.zeros((rows, MEM_WIDTH), F32)
            for msk in masks:
                qh = jnp.where(msk, qs, jnp.zeros_like(qs))
                p = _softmax_rows(_dot(qh, kn, 1, 1)).astype(BF16)
                out = jnp.where(msk, _dot(p, vm, 1, 0), out)
            o_ref[sl, :] = out.astype(o_ref.dtype)

    vec = pl.BlockSpec((1, MEM_WIDTH), lambda b: (0, 0))
    mem_spec = pl.BlockSpec((mtok, d), lambda b: (b, 0))
    kv_spec = pl.BlockSpec((mtok, 2 * MEM_WIDTH), lambda b: (b, 0))
    return pl.pallas_call(
        body, out_shape=(jax.ShapeDtypeStruct(cat.shape, cat.dtype), jax.ShapeDtypeStruct(mem.shape, BF16),
                         jax.ShapeDtypeStruct((mem.shape[0], 2 * MEM_WIDTH), F32)), grid=(batch,),
        in_specs=[pl.BlockSpec((seq, MEM_WIDTH), lambda b: (b, qcol)), mem_spec, pl.BlockSpec((1, d), lambda b: (0, 0)),
                  pl.BlockSpec(wkv.shape, lambda b: (0, 0)), vec, vec, ANY],
        out_specs=(pl.BlockSpec((seq, MEM_WIDTH), lambda b: (b, TOK_WIDTH // MEM_WIDTH)), mem_spec, kv_spec),
        input_output_aliases={6: 0},
        compiler_params=_params("parallel"), name=name)(z, mem, mem_gain, wkv, gq4, gk4, cat)


def memattn_bwd(z, kv, dcat, gq4, gk4, mem, mem_n, wkv, dz, batch, seq, qcol, name):
    mtok = kv.shape[0] // batch
    d = mem.shape[1]
    rows = min(MEM_ROWS, seq)

    def body(q_ref, kv_ref, do_ref, gq_ref, gk_ref, m_ref, n_ref, w_ref, *rest):
        dq_ref, dgq_ref, dgk_ref, dw_ref, dmg_ref, dw_acc = rest[-6:]

        @pl.when(pl.program_id(0) == 0)
        def _():
            dgq_ref[...] = jnp.zeros_like(dgq_ref)
            dgk_ref[...] = jnp.zeros_like(dgk_ref)
            dmg_ref[...] = jnp.zeros_like(dmg_ref)
            dw_acc[...] = jnp.zeros_like(dw_acc)

        masks = _group_masks(MEM_WIDTH)
        kn_f, khat, rk = _head_norm(kv_ref[:, 0:MEM_WIDTH], gk_ref[...], masks)
        kn = kn_f.astype(BF16)
        vm = kv_ref[:, MEM_WIDTH:2 * MEM_WIDTH].astype(BF16)
        dkn = jnp.zeros((mtok, MEM_WIDTH), F32)
        dvm = jnp.zeros((mtok, MEM_WIDTH), F32)
        dgq = jnp.zeros((1, MEM_WIDTH), F32)
        for t in range(seq // rows):
            sl = slice(t * rows, (t + 1) * rows)
            qn_f, qhat, rq = _head_norm(q_ref[sl, :].astype(F32), gq_ref[...], masks)
            qs = (qn_f * ATTN_SCALE).astype(BF16)
            dob = do_ref[sl, :]
            dqn = jnp.zeros((rows, MEM_WIDTH), F32)
            for msk in masks:
                qh = jnp.where(msk, qs, jnp.zeros_like(qs))
                doh = jnp.where(msk, dob, jnp.zeros_like(dob))
                p = _softmax_rows(_dot(qh, kn, 1, 1))
                dp = _dot(doh, vm, 1, 1)
                ds = p * (dp - jnp.sum(p * dp, axis=-1, keepdims=True))
                dsb = ds.astype(BF16)
                dqn = jnp.where(msk, _dot(dsb, kn, 1, 0), dqn)
                dkn = dkn + jnp.where(msk, _dot(dsb, qs, 0, 0), 0.0)
                dvm = dvm + jnp.where(msk, _dot(p.astype(BF16), dob, 0, 0), 0.0)
            dq, dg = _head_norm_bwd(dqn * ATTN_SCALE, qhat, rq, gq_ref[...], masks)
            dq_ref[sl, :] = dq.astype(dq_ref.dtype)
            dgq = dgq + dg
        dk, dgk = _head_norm_bwd(dkn, khat, rk, gk_ref[...], masks)
        dgq_ref[...] += dgq
        dgk_ref[...] += dgk
        dkv_b = jnp.concatenate([dk, dvm], axis=-1).astype(BF16)
        dw_acc[...] += _dot(n_ref[...], dkv_b, 0, 0)
        dn = _dot(dkv_b, w_ref[...], 1, 1)
        mv = m_ref[...]
        rm = lax.rsqrt(jnp.mean(mv * mv, axis=-1, keepdims=True) + NORM_EPS)
        dmg_ref[...] += jnp.sum(dn * (mv * rm), axis=0, keepdims=True)

        @pl.when(pl.program_id(0) == batch - 1)
        def _():
            dw_ref[...] = dw_acc[...].astype(dw_ref.dtype)

    vec = pl.BlockSpec((1, MEM_WIDTH), lambda b: (0, 0))
    kv_spec = pl.BlockSpec((mtok, 2 * MEM_WIDTH), lambda b: (b, 0))
    mem_spec = pl.BlockSpec((mtok, d), lambda b: (b, 0))
    w_spec = pl.BlockSpec(wkv.shape, lambda b: (0, 0))
    v_shape = jax.ShapeDtypeStruct((1, MEM_WIDTH), F32)
    q_spec = pl.BlockSpec((seq, MEM_WIDTH), lambda b: (b, qcol))
    in_specs = [q_spec, kv_spec, pl.BlockSpec((seq, MEM_WIDTH), lambda b: (b, TOK_WIDTH // MEM_WIDTH)), vec, vec,
                mem_spec, mem_spec, w_spec]
    args = [z, kv, dcat, gq4, gk4, mem, mem_n, wkv]
    if dz is None:
        dq_shape, dq_spec, aliases = jax.ShapeDtypeStruct((z.shape[0], MEM_WIDTH), BF16), \
            pl.BlockSpec((seq, MEM_WIDTH), lambda b: (b, 0)), {}
    else:
        dq_shape, dq_spec, aliases = jax.ShapeDtypeStruct(dz.shape, dz.dtype), q_spec, {len(args): 0}
        in_specs.append(ANY)
        args.append(dz)
    return pl.pallas_call(
        body,
        out_shape=(dq_shape, v_shape, v_shape, jax.ShapeDtypeStruct(wkv.shape, BF16), jax.ShapeDtypeStruct((1, d), F32)),
        grid=(batch,), in_specs=in_specs,
        out_specs=(dq_spec, vec, vec, w_spec, pl.BlockSpec((1, d), lambda b: (0, 0))),
        scratch_shapes=[pltpu.VMEM(wkv.shape, F32)], input_output_aliases=aliases,
        compiler_params=_params("arbitrary"), name=name)(*args)


CONV_ROWS = 512


def _glu(a_ref, g_ref):
    return a_ref[...].astype(F32) * _sigmoid(g_ref[...].astype(F32))


def _layer_norm_stats(y):
    mu = jnp.mean(y, axis=-1, keepdims=True)
    yc = y - mu
    rstd = lax.rsqrt(jnp.mean(yc * yc, axis=-1, keepdims=True) + NORM_EPS)
    return yc * rstd, rstd


CONV_WIN = CONV_HALO + CONV_ROWS
SUBLANES = 8
SHIFT_ROWS = CONV_WIN - SUBLANES


def _preshift(win, shifted):
    for s in range(1, SUBLANES):
        shifted[s - 1, :, :] = win[s:s + SHIFT_ROWS, :]


TAP_ROWS = 64
TAP_TILES = [(r0, slice(c0, c0 + LANES)) for c0 in range(0, TOK_WIDTH, LANES) for r0 in range(0, CONV_ROWS, TAP_ROWS)]


def _tap(win, shifted, off, r0, lanes):
    s = off % SUBLANES
    base = off - s + r0
    if s == 0:
        return win[base:base + TAP_ROWS, lanes]
    return shifted[s - 1, base:base + TAP_ROWS, lanes]


def _fold_rows(x):
    return jnp.sum(x.reshape(TAP_ROWS // SUBLANES, SUBLANES, LANES), axis=0)


def conv_fwd(z, cw, cb, lg, lb, batch, seq):
    n = z.shape[0]
    nt = seq // CONV_ROWS
    sub = CONV_ROWS // CONV_HALO
    lead = CONV_HALO - (CONV_W - 1)

    def body(a_ref, g_ref, ap_ref, gp_ref, cw_ref, cb_ref, lg_ref, lb_ref, o_ref, y_ref, win, shifted):
        first = pl.program_id(1) == 0
        win[0:CONV_HALO, :] = jnp.where(first, 0.0, _glu(ap_ref, gp_ref))
        win[CONV_HALO:CONV_WIN, :] = _glu(a_ref, g_ref)
        _preshift(win, shifted)
        for r0, lanes in TAP_TILES:
            acc = jnp.zeros((TAP_ROWS, LANES), F32) + cb_ref[:, lanes]
            for w in range(CONV_W):
                acc = acc + _tap(win, shifted, lead + w, r0, lanes) * cw_ref[w:w + 1, lanes]
            y_ref[r0:r0 + TAP_ROWS, lanes] = acc
        yh, _ = _layer_norm_stats(y_ref[...])
        t = yh * lg_ref[...] + lb_ref[...]
        o_ref[...] = (t * _sigmoid(t)).astype(o_ref.dtype)

    def cur(c):
        return pl.BlockSpec((CONV_ROWS, TOK_WIDTH), lambda b, i: (b * nt + i, c))

    def prev(c):
        return pl.BlockSpec((CONV_HALO, TOK_WIDTH), lambda b, i: (jnp.maximum((b * nt + i) * sub - 1, 0), c))

    vec = pl.BlockSpec((1, TOK_WIDTH), lambda b, i: (0, 0))
    return pl.pallas_call(
        body, out_shape=(jax.ShapeDtypeStruct((n, D_MODEL), BF16), jax.ShapeDtypeStruct((n, TOK_WIDTH), F32)),
        grid=(batch, nt),
        in_specs=[cur(0), cur(1), prev(0), prev(1), pl.BlockSpec((32, TOK_WIDTH), lambda b, i: (0, 0)), vec, vec, vec],
        out_specs=(cur(0), cur(0)),
        scratch_shapes=[pltpu.VMEM((CONV_WIN, TOK_WIDTH), F32), pltpu.VMEM((SUBLANES - 1, SHIFT_ROWS, TOK_WIDTH), F32)],
        compiler_params=_params("parallel", "arbitrary"), name="conv_fwd")(z, z, z, z, cw, cb, lg, lb)


def conv_bwd(z, y, dcat, cw, lg, lb, batch, seq):
    n = z.shape[0]
    nt = seq // CONV_ROWS
    sub = CONV_ROWS // CONV_HALO
    lead = CONV_HALO - (CONV_W - 1)
    last_blk = n // CONV_HALO - 1

    def body(a_ref, g_ref, ap_ref, gp_ref, y_ref, yn_ref, do_ref, don_ref, cw_ref, lg_ref, lb_ref,
             dz_ref, dcw_ref, dsm_ref, win, shifted, dyw, dshifted, dg_o):
        b, i, which = pl.program_id(0), pl.program_id(1), pl.program_id(2)

        @pl.when(which == 0)
        def _():
            first, last = i == 0, i == nt - 1

            @pl.when((b == 0) & (i == 0))
            def _():
                dcw_ref[...] = jnp.zeros_like(dcw_ref)
                dsm_ref[...] = jnp.zeros_like(dsm_ref)

            win[0:CONV_HALO, :] = jnp.where(first, 0.0, _glu(ap_ref, gp_ref))
            win[CONV_HALO:CONV_WIN, :] = _glu(a_ref, g_ref)
            _preshift(win, shifted)
            yv = jnp.concatenate([y_ref[...], yn_ref[...]], axis=0)
            yh, rstd = _layer_norm_stats(yv)
            t = yh * lg_ref[...] + lb_ref[...]
            st = _sigmoid(t)
            dout = jnp.concatenate(
                [do_ref[...].astype(F32), jnp.where(last, 0.0, don_ref[...].astype(F32))], axis=0)
            dt = dout * st * (1.0 + t * (1.0 - st))
            dyh = dt * lg_ref[...]
            dy = rstd * (dyh - jnp.mean(dyh, axis=-1, keepdims=True)
                         - yh * jnp.mean(dyh * yh, axis=-1, keepdims=True))
            dyw[...] = dy
            _preshift(dyw, dshifted)
            dsm_ref[0:1, :] += jnp.sum(dy[0:CONV_ROWS], axis=0, keepdims=True)
            dsm_ref[1:2, :] += jnp.sum((dt * yh)[0:CONV_ROWS], axis=0, keepdims=True)
            dsm_ref[2:3, :] += jnp.sum(dt[0:CONV_ROWS], axis=0, keepdims=True)
            for c0 in range(0, TOK_WIDTH, LANES):
                lanes = slice(c0, c0 + LANES)
                dcw_acc = [jnp.zeros((SUBLANES, LANES), F32) for _ in range(CONV_W)]
                for r0 in range(0, CONV_ROWS, TAP_ROWS):
                    dyt = dyw[r0:r0 + TAP_ROWS, lanes]
                    dglu = jnp.zeros((TAP_ROWS, LANES), F32)
                    for w in range(CONV_W):
                        dcw_acc[w] = dcw_acc[w] + _fold_rows(dyt * _tap(win, shifted, lead + w, r0, lanes))
                        dglu = dglu + _tap(dyw, dshifted, CONV_W - 1 - w, r0, lanes) * cw_ref[w:w + 1, lanes]
                    avt = a_ref[r0:r0 + TAP_ROWS, lanes].astype(F32)
                    sgt = _sigmoid(g_ref[r0:r0 + TAP_ROWS, lanes].astype(F32))
                    dz_ref[r0:r0 + TAP_ROWS, lanes] = (dglu * sgt).astype(dz_ref.dtype)
                    dg_o[r0:r0 + TAP_ROWS, lanes] = (dglu * avt * sgt * (1.0 - sgt)).astype(dg_o.dtype)
                for w in range(CONV_W):
                    dcw_ref[w:w + 1, lanes] += jnp.sum(dcw_acc[w], axis=0, keepdims=True)

        @pl.when(which == 1)
        def _():
            dz_ref[...] = dg_o[...]

    def ahead(b, i, t):
        return jnp.minimum(b * nt + i + t, batch * nt - 1)

    def cur(c):
        return pl.BlockSpec((CONV_ROWS, TOK_WIDTH), lambda b, i, t: (ahead(b, i, t), c))

    def prev(c):
        return pl.BlockSpec((CONV_HALO, TOK_WIDTH), lambda b, i, t: (jnp.maximum(ahead(b, i, t) * sub - 1, 0), c))

    nxt = pl.BlockSpec((CONV_HALO, TOK_WIDTH),
                       lambda b, i, t: (jnp.minimum((ahead(b, i, t) + 1) * sub, last_blk), 0))
    vec = pl.BlockSpec((1, TOK_WIDTH), lambda b, i, t: (0, 0))
    full32 = pl.BlockSpec((32, TOK_WIDTH), lambda b, i, t: (0, 0))
    return pl.pallas_call(
        body,
        out_shape=(jax.ShapeDtypeStruct(z.shape, BF16), jax.ShapeDtypeStruct((32, TOK_WIDTH), F32),
                   jax.ShapeDtypeStruct((8, TOK_WIDTH), F32)),
        grid=(batch, nt, 2),
        in_specs=[cur(0), cur(1), prev(0), prev(1), cur(0), nxt, cur(0), nxt, full32, vec, vec],
        out_specs=(pl.BlockSpec((CONV_ROWS, TOK_WIDTH), lambda b, i, t: (b * nt + i, t)), full32,
                   pl.BlockSpec((8, TOK_WIDTH), lambda b, i, t: (0, 0))),
        scratch_shapes=[pltpu.VMEM((CONV_WIN, TOK_WIDTH), F32), pltpu.VMEM((SUBLANES - 1, SHIFT_ROWS, TOK_WIDTH), F32),
                        pltpu.VMEM((CONV_WIN, TOK_WIDTH), F32), pltpu.VMEM((SUBLANES - 1, SHIFT_ROWS, TOK_WIDTH), F32),
                        pltpu.VMEM((CONV_ROWS, TOK_WIDTH), BF16)],
        compiler_params=_params("arbitrary", "arbitrary", "arbitrary"), name="conv_bwd")(
            z, z, z, z, y, y, dcat, dcat, cw, lg, lb)


def _place():
    return lax.axis_index("x"), lax.axis_index("y"), lax.axis_index("c")


def _other_chips(x, y):
    return [(1 - x, y), (x, 1 - y), (1 - x, 1 - y)]


def reduce_small(arrays):
    na = len(arrays)
    widths = sorted({a.shape[1] for a in arrays})
    members = {w: [] for w in widths}
    for i, a in enumerate(arrays):
        used = sum(r for _, _, r in members[a.shape[1]])
        members[a.shape[1]].append((i, used, a.shape[0]))
    slab_rows = {w: -(-sum(r for _, _, r in members[w]) // 8) * 8 for w in widths}
    ng = len(widths)

    def body(*refs):
        ins, outs = refs[:na], refs[na:2 * na]
        slabs, bufs = refs[2 * na:2 * na + ng], refs[2 * na + ng:2 * na + 2 * ng]
        send_sems, recv_sems = refs[2 * na + 2 * ng:]
        x, y, c = _place()
        me = 4 * x + 2 * y + c
        copies = []
        for g, w in enumerate(widths):
            slabs[g][...] = jnp.zeros_like(slabs[g])
            for i, r0, rows in members[w]:
                slabs[g][r0:r0 + rows, :] = ins[i][...]
            bufs[g][me] = slabs[g][...]
            for k in range(1, N_DEV):
                cp = pltpu.make_async_remote_copy(
                    src_ref=slabs[g], dst_ref=bufs[g].at[me], send_sem=send_sems.at[g, k - 1],
                    recv_sem=recv_sems.at[g, k - 1],
                    device_id=(x ^ (k >> 2), y ^ ((k >> 1) & 1), c ^ (k & 1)), device_id_type=MESH)
                cp.start()
                copies.append(cp)
        for g in range(ng):
            for k in range(1, N_DEV):
                src = 4 * (x ^ (k >> 2)) + 2 * (y ^ ((k >> 1) & 1)) + (c ^ (k & 1))
                pltpu.make_async_remote_copy(
                    src_ref=slabs[g], dst_ref=bufs[g].at[src], send_sem=send_sems.at[g, k - 1],
                    recv_sem=recv_sems.at[g, k - 1], device_id=(x, y, c), device_id_type=MESH).wait_recv()
        for cp in copies:
            cp.wait_send()
        for g, w in enumerate(widths):
            for i, r0, rows in members[w]:
                total = bufs[g][0, r0:r0 + rows, :]
                for dev in range(1, N_DEV):
                    total = total + bufs[g][dev, r0:r0 + rows, :]
                outs[i][...] = total

    vmem = pl.BlockSpec(memory_space=pltpu.VMEM)
    return pl.pallas_call(
        body, out_shape=tuple(jax.ShapeDtypeStruct(a.shape, F32) for a in arrays),
        in_specs=[vmem] * na, out_specs=tuple([vmem] * na),
        scratch_shapes=[pltpu.VMEM((slab_rows[w], w), F32) for w in widths]
        + [pltpu.VMEM((N_DEV, slab_rows[w], w), F32) for w in widths]
        + [pltpu.SemaphoreType.DMA((ng, N_DEV - 1)), pltpu.SemaphoreType.DMA((ng, N_DEV - 1))],
        compiler_params=pltpu.CompilerParams(vmem_limit_bytes=VMEM_LIMIT), name="small_reduce")(*arrays)


def adamw_small(ws, gs, ms, vs):
    na = len(ws)
    c1 = 1.0 / (1.0 - ADAM_B1 ** ADAM_STEP)
    c2 = 1.0 / (1.0 - ADAM_B2 ** ADAM_STEP)

    def body(*refs):
        w_refs, g_refs, m_refs, v_refs = (refs[i * na:(i + 1) * na] for i in range(4))
        d_refs, nm_refs, nv_refs = (refs[(4 + i) * na:(5 + i) * na] for i in range(3))
        for a in range(na):
            gv = g_refs[a][...]
            nm = ADAM_B1 * m_refs[a][...] + (1.0 - ADAM_B1) * gv
            nv = ADAM_B2 * v_refs[a][...] + (1.0 - ADAM_B2) * (gv * gv)
            nm_refs[a][...] = nm
            nv_refs[a][...] = nv
            d_refs[a][...] = -ADAM_LR * ((nm * c1) / (jnp.sqrt(nv * c2) + ADAM_EPS) + ADAM_WD * w_refs[a][...])

    vmem = pl.BlockSpec(memory_space=pltpu.VMEM)
    shapes = tuple(jax.ShapeDtypeStruct(w.shape, F32) for w in ws)
    outs = pl.pallas_call(
        body, out_shape=shapes * 3, in_specs=[vmem] * (4 * na), out_specs=tuple([vmem] * (3 * na)),
        compiler_params=pltpu.CompilerParams(vmem_limit_bytes=VMEM_LIMIT), name="adamw_small")(*ws, *gs, *ms, *vs)
    return outs[:na], outs[na:2 * na], outs[2 * na:]


def gather_weights(shards, name, collective_id):
    nw = len(shards)
    ns = [s.shape[0] for s in shards]
    in_refs = [jax.new_ref(s, memory_space=pltpu.MemorySpace.HBM) for s in shards]
    out_refs = [jax.empty_ref(jax.ShapeDtypeStruct((N_DEV * s.shape[0], s.shape[1]), s.dtype),
                              memory_space=pltpu.MemorySpace.HBM) for s in shards]

    @pl.kernel(mesh=plsc.ScalarSubcoreMesh(axis_name="seq", num_cores=1), name=name,
               scratch_types=(pltpu.SemaphoreType.DMA((nw, 7)), pltpu.SemaphoreType.DMA((nw, 7)),
                              pltpu.SemaphoreType.DMA((nw,))),
               compiler_params=pltpu.CompilerParams(collective_id=collective_id))
    def launch(send_sems, recv_sems, local_sems):
        x, y, c = _place()
        me, sib = (x, y, c), (x, y, 1 - c)
        chips = _other_chips(x, y)
        barrier = pltpu.get_barrier_semaphore()
        for peer in [sib] + [(*chip, c) for chip in chips]:
            pl.semaphore_signal(barrier, inc=1, device_id=peer, device_id_type=MESH)
        pl.semaphore_wait(barrier, 4)

        def rows(w, dev):
            return out_refs[w].at[pl.ds((4 * dev[0] + 2 * dev[1] + dev[2]) * ns[w], ns[w]), :]

        def copy(w, k, block, to, src=None):
            return pltpu.make_async_remote_copy(
                src_ref=rows(w, block) if src is None else src, dst_ref=rows(w, block),
                send_sem=send_sems.at[w, k], recv_sem=recv_sems.at[w, k], device_id=to, device_id_type=MESH)

        started, sends = [], []
        for w in range(nw):
            mine = pltpu.make_async_copy(in_refs[w], rows(w, me), local_sems.at[w])
            mine.start()
            started.append(mine)
            first = [copy(w, 0, me, sib, src=in_refs[w])]
            first += [copy(w, 1 + j, me, (*chip, c), src=in_refs[w]) for j, chip in enumerate(chips)]
            for cp in first:
                cp.start()
            sends += first
        for w in range(nw):
            for j, chip in enumerate(chips):
                copy(w, 1 + j, (*chip, c), me).wait_recv()
                fwd = copy(w, 4 + j, (*chip, c), sib)
                fwd.start()
                sends.append(fwd)
        for w in range(nw):
            copy(w, 0, sib, me).wait_recv()
            for j, chip in enumerate(chips):
                copy(w, 4 + j, (*chip, 1 - c), me).wait_recv()
        for cp in sends:
            cp.wait_send()
        for mine in started:
            mine.wait()

    launch()
    return [r[...] for r in out_refs]


def _sequencer_exchange(sources, out_rows, peers_of, copies_of, name, collective_id):
    nw = len(sources)
    in_refs = [jax.new_ref(s, memory_space=pltpu.MemorySpace.HBM) for s in sources]
    out_refs = [jax.empty_ref(jax.ShapeDtypeStruct((rows, s.shape[1]), s.dtype), memory_space=pltpu.MemorySpace.HBM)
                for rows, s in zip(out_rows, sources)]
    per = len(copies_of(0, 0, 0, 0))

    @pl.kernel(mesh=plsc.ScalarSubcoreMesh(axis_name="seq", num_cores=1), name=name,
               scratch_types=(pltpu.SemaphoreType.DMA((nw, per)), pltpu.SemaphoreType.DMA((nw, per))),
               compiler_params=pltpu.CompilerParams(collective_id=collective_id))
    def launch(send_sems, recv_sems):
        x, y, c = _place()
        peers = peers_of(x, y, c)
        barrier = pltpu.get_barrier_semaphore()
        for peer in peers:
            pl.semaphore_signal(barrier, inc=1, device_id=peer, device_id_type=MESH)
        pl.semaphore_wait(barrier, len(peers))
        copies = []
        for w in range(nw):
            for k, (src_blk, dst_blk, rows, peer) in enumerate(copies_of(x, y, c, w)):
                cp = pltpu.make_async_remote_copy(
                    src_ref=in_refs[w].at[pl.ds(src_blk * rows, rows), :],
                    dst_ref=out_refs[w].at[pl.ds(dst_blk * rows, rows), :],
                    send_sem=send_sems.at[w, k], recv_sem=recv_sems.at[w, k], device_id=peer, device_id_type=MESH)
                cp.start()
                copies.append(cp)
        for cp in copies:
            cp.wait_recv()
        for cp in copies:
            cp.wait_send()

    launch()
    return [r[...] for r in out_refs]


def scatter_to_sibling(grads, name, collective_id):
    ns = [g.shape[0] // N_DEV for g in grads]
    return _sequencer_exchange(
        grads, [4 * n for n in ns],
        lambda x, y, c: [(x, y, 1 - c)],
        lambda x, y, c, w: [(2 * q + 1 - c, q, ns[w], (x, y, 1 - c)) for q in range(4)],
        name, collective_id)


def scatter_to_chips(parts, name, collective_id):
    ns = [p.shape[0] // 4 for p in parts]
    return _sequencer_exchange(
        parts, [3 * n for n in ns],
        lambda x, y, c: [(*chip, c) for chip in _other_chips(x, y)],
        lambda x, y, c, w: [(2 * chip[0] + chip[1], j, ns[w], (*chip, c)) for j, chip in enumerate(_other_chips(x, y))],
        name, collective_id)


def add_sibling(grads, landeds, core, name):
    nw = len(grads)

    def body(c_ref, *refs):
        for w in range(nw):
            g_ref, l_ref, o_ref = refs[2 * w], refs[2 * w + 1], refs[2 * nw + w]
            o_ref[...] = (g_ref[...].astype(F32) + l_ref[...].astype(F32)).astype(o_ref.dtype)

    in_specs, out_specs, args = [], [], []
    for g, ld in zip(grads, landeds):
        n, cols = ld.shape[0] // 4, g.shape[1]
        in_specs += [pl.BlockSpec((n, cols), lambda q, c_ref: (2 * q + c_ref[0], 0)),
                     pl.BlockSpec((n, cols), lambda q, c_ref: (q, 0))]
        out_specs.append(pl.BlockSpec((n, cols), lambda q, c_ref: (q, 0)))
        args += [g, ld]
    grid_spec = pltpu.PrefetchScalarGridSpec(
        num_scalar_prefetch=1, grid=(4,), in_specs=in_specs, out_specs=tuple(out_specs))
    return pl.pallas_call(
        body, out_shape=tuple(jax.ShapeDtypeStruct(ld.shape, ld.dtype) for ld in landeds), grid_spec=grid_spec,
        compiler_params=_params("arbitrary"), name=name)(core, *args)


ADAMW_HALVES = 2


def adamw_shards(items, chip, name):
    c1 = 1.0 / (1.0 - ADAM_B1 ** ADAM_STEP)
    c2 = 1.0 / (1.0 - ADAM_B2 ** ADAM_STEP)
    ni = len(items)

    def body(q_ref, *refs):
        outs = refs[len(refs) - 4 * ni:]
        for k in range(ni):
            w_ref, m_ref, v_ref, p_ref, l0_ref, l1_ref, l2_ref = refs[7 * k:7 * k + 7]
            g_ref, d_ref, nm_ref, nv_ref = outs[4 * k:4 * k + 4]
            gv = ((p_ref[...].astype(F32) + l0_ref[...].astype(F32)) + l1_ref[...].astype(F32)) + l2_ref[...].astype(F32)
            nm = ADAM_B1 * m_ref[...] + (1.0 - ADAM_B1) * gv
            nv = ADAM_B2 * v_ref[...] + (1.0 - ADAM_B2) * (gv * gv)
            g_ref[...] = gv
            nm_ref[...] = nm
            nv_ref[...] = nv
            d_ref[...] = -ADAM_LR * ((nm * c1) / (jnp.sqrt(nv * c2) + ADAM_EPS) + ADAM_WD * w_ref[...])

    sub = ADAMW_HALVES
    in_specs, out_specs, out_shape, args, donated = [], [], [], [chip], []
    for layer, w, m, v, part, landed, earlier in items:
        rows, cols = landed.shape[0] // (3 * sub), w.shape[1]

        def block(first, rows=rows, cols=cols):
            return pl.BlockSpec((rows, cols), lambda i, q_ref: (first(q_ref) * sub + i, 0))

        own = block(lambda q_ref, layer=layer: layer)
        in_specs += [own, own, own, block(lambda q_ref: q_ref[0])] + [block(lambda q_ref, j=j: j) for j in range(3)]
        args += [w, m, v, part, landed, landed, landed]
        out_specs += [own] * 4
        out_shape += [jax.ShapeDtypeStruct(w.shape, F32)] * 4
        donated.append(earlier)
    aliases = {}
    for k, earlier in enumerate(donated):
        if earlier is not None:
            for j in range(4):
                aliases[len(args)] = 4 * k + j
                in_specs.append(ANY)
                args.append(earlier[j])
    grid_spec = pltpu.PrefetchScalarGridSpec(
        num_scalar_prefetch=1, grid=(sub,), in_specs=in_specs, out_specs=tuple(out_specs))
    outs = pl.pallas_call(
        body, out_shape=tuple(out_shape), grid_spec=grid_spec, input_output_aliases=aliases,
        compiler_params=_params("arbitrary"), name=name)(*args)
    return [tuple(outs[4 * k:4 * k + 4]) for k in range(ni)]


def _pack(arrays):
    flat = jnp.concatenate([a.reshape(-1).astype(F32) for a in arrays])
    pad = (-flat.shape[0]) % (8 * LANES)
    return jnp.pad(flat, (0, pad)).reshape(-1, LANES)


def _unpack(slab, shapes):
    flat = slab.reshape(slab.shape[:-2] + (-1,))
    out, off = [], 0
    for shp in shapes:
        size = 1
        for s in shp:
            size *= s
        out.append(flat[..., off:off + size].reshape(flat.shape[:-1] + tuple(shp)))
        off += size
    return out


def kernel(x, mem, norm1_g, mem_norm_g, a_w_in, a_q_g, a_k_g, a_rel_bias, b_w_in, b_b_in, b_conv_w, b_conv_b, b_ln_g, b_ln_b, mq_g, mk_g, w_mem_kv, w_out, norm2_g, w_gate, w_up, w_down, loss_target, m_norm1_g, m_mem_norm_g, m_a_w_in, m_a_q_g, m_a_k_g, m_a_rel_bias, m_b_w_in, m_b_b_in, m_b_conv_w, m_b_conv_b, m_b_ln_g, m_b_ln_b, m_mq_g, m_mk_g, m_w_mem_kv, m_w_out, m_norm2_g, m_w_gate, m_w_up, m_w_down, v_norm1_g, v_mem_norm_g, v_a_w_in, v_a_q_g, v_a_k_g, v_a_rel_bias, v_b_w_in, v_b_b_in, v_b_conv_w, v_b_conv_b, v_b_ln_g, v_b_ln_b, v_mq_g, v_mk_g, v_w_mem_kv, v_w_out, v_norm2_g, v_w_gate, v_w_up, v_w_down):
    batch, seq, d = x.shape
    mtok = mem.shape[1]
    n = batch * seq
    ax, ay, ac = _place()
    me = 4 * ax + 2 * ay + ac
    core_arr = jnp.reshape(ac, (1,)).astype(jnp.int32)
    chip_arr = jnp.reshape(2 * ax + ay, (1,)).astype(jnp.int32)

    def t_bf16(w):
        return jnp.transpose(w).astype(BF16)

    def after(value, *earlier):
        return lax.optimization_barrier((value, *earlier))[0]

    def gather_mix(l, when, name, collective_id):
        srcs = [w_mem_kv[l].astype(BF16), w_out[l].astype(BF16)]
        if l == 1:
            srcs += [t_bf16(b_w_in[0]), _pack([b_b_in, b_conv_w, b_conv_b, b_ln_g, b_ln_b])]
        return gather_weights([after(srcs[0], *when)] + srcs[1:], name, collective_id)

    def gather_ffn(l, when, name, collective_id):
        return gather_weights(
            [after(t_bf16(w_gate[l]), *when), t_bf16(w_up[l]), w_down[l].astype(BF16)], name, collective_id)

    f_loc = b_b_in.shape[1]
    c_loc = b_conv_b.shape[1]

    def two(g):
        return jnp.concatenate([g, g], axis=-1)

    gq2, gk2 = two(a_q_g), two(a_k_g)
    rel16 = jnp.pad(a_rel_bias[0], ((0, 16 - a_rel_bias.shape[1]), (0, 0)))
    bias = bias_blocks(rel16)

    x0 = x.reshape(n, d)
    mem2 = mem.reshape(batch * mtok, d)

    saved = []
    xin = x0
    a_win_t, = gather_weights([t_bf16(a_w_in[0])], "gather_in_a", 1)
    wg_t, wu_t, wd, wo, wkv = [None] * 2, [None] * 2, [None] * 2, [None] * 2, [None] * 2
    h = after(rms_fwd(xin, norm1_g[0:1], name="rms1_fwd_0"), bias)
    target = loss_target.reshape(n, d)
    for l in range(2):
        gq4 = jnp.tile(mq_g[l:l + 1], (1, 4))
        gk4 = jnp.tile(mk_g[l:l + 1], (1, 4))
        y_conv = None
        if l == 0:
            wkv[0], wo[0] = gather_mix(0, (h, a_win_t), "gather_mix_a", 2)
            z = mm_nt(h, a_win_t, name="in_proj_a")
            wg_t[0], wu_t[0], wd[0] = gather_ffn(0, (z, wkv[0]), "gather_ffn_a", 3)
            cat = attn_fwd(z, gq2, gk2, bias, batch, seq)
            wkv[1], wo[1], b_win_t, conv_slabs = gather_mix(1, (cat, wg_t[0]), "gather_mix_b", 4)
            qcol = 3 * TOK_WIDTH // MEM_WIDTH
        else:
            small_shapes = [(f_loc,), (CONV_W, c_loc), (c_loc,), (c_loc,), (c_loc,)]
            bb_g, cw_g, cb_g, lg_g, lb_g = _unpack(conv_slabs.reshape(N_DEV, -1, LANES), small_shapes)
            bb_full = bb_g.reshape(1, -1)
            cw_full = jnp.pad(jnp.transpose(cw_g, (1, 0, 2)).reshape(CONV_W, -1), ((0, 32 - CONV_W), (0, 0)))
            cb_full, lg_full, lb_full = cb_g.reshape(1, -1), lg_g.reshape(1, -1), lb_g.reshape(1, -1)
            z = mm_nt(h, b_win_t, bias=bb_full, name="in_proj_b")
            cat, y_conv = conv_fwd(z, cw_full, cb_full, lg_full, lb_full, batch, seq)
            qcol = 2 * TOK_WIDTH // MEM_WIDTH
        cat, mem_n, kv = memattn_fwd(
            z, mem2, mem_norm_g[l:l + 1], wkv[l], gq4, gk4, cat, batch, seq, qcol, name=f"memattn_fwd_{l}")
        x1, h2 = proj_norm(cat, wo[l], xin, norm2_g[l:l + 1], name=f"out_proj_{l}")
        if l == 0:
            wg_t[1], wu_t[1], wd[1] = gather_ffn(1, (x1, b_win_t), "gather_ffn_b", 5)
        if l == 0:
            gate, up, act, x2, h_next = ffn_fwd(h2, wg_t[0], wu_t[0], wd[0], x1, gain=norm1_g[1:2], name="ffn_fwd_0")
        else:
            gate, up, act, dx_b, loss_blk = ffn_fwd(h2, wg_t[1], wu_t[1], wd[1], x1, target=target, name="ffn_fwd_1")
        saved.append(dict(xin=xin, h=h, mem_n=mem_n, kv=kv, gq4=gq4, gk4=gk4, z=z, qcol=qcol, cat=cat, x1=x1, h2=h2,
                          gate=gate, up=up, act=act, y_conv=y_conv))
        if l == 0:
            xin, h = x2, h_next

    big = {}
    small = {}
    reduced = {}
    groups = 0

    def scatter_siblings(keys):
        nonlocal groups
        gid = groups
        groups += 1
        return gid, keys, scatter_to_sibling([big[k] for k in keys], f"scatter_sibling_{gid}", 8 + 2 * gid)

    def scatter_chips(stage1, when):
        gid, keys, landed1 = stage1
        parts = add_sibling([after(big[keys[0]], when)] + [big[k] for k in keys[1:]], landed1, core_arr,
                            name=f"add_sibling_{gid}")
        landed2 = scatter_to_chips(parts, f"scatter_chips_{gid}", 9 + 2 * gid)
        for k, p, ld in zip(keys, parts, landed2):
            reduced[k] = (p, ld)
        return parts, landed2

    def rows_of(w, transposed):
        w = jnp.swapaxes(w, 1, 2) if transposed else w
        return w.reshape(w.shape[0] * w.shape[1], w.shape[2])

    sharded = {
        "win0": (2, True), "win1": (6, True), "wkv": (14, False), "wo": (15, False),
        "wg": (17, True), "wu": (18, True), "wd": (19, False)}
    weights = [norm1_g, mem_norm_g, a_w_in, a_q_g, a_k_g, a_rel_bias, b_w_in, b_b_in, b_conv_w, b_conv_b, b_ln_g,
               b_ln_b, mq_g, mk_g, w_mem_kv, w_out, norm2_g, w_gate, w_up, w_down]
    moms = [m_norm1_g, m_mem_norm_g, m_a_w_in, m_a_q_g, m_a_k_g, m_a_rel_bias, m_b_w_in, m_b_b_in, m_b_conv_w,
            m_b_conv_b, m_b_ln_g, m_b_ln_b, m_mq_g, m_mk_g, m_w_mem_kv, m_w_out, m_norm2_g, m_w_gate, m_w_up, m_w_down]
    vels = [v_norm1_g, v_mem_norm_g, v_a_w_in, v_a_q_g, v_a_k_g, v_a_rel_bias, v_b_w_in, v_b_b_in, v_b_conv_w,
            v_b_conv_b, v_b_ln_g, v_b_ln_b, v_mq_g, v_mk_g, v_w_mem_kv, v_w_out, v_norm2_g, v_w_gate, v_w_up, v_w_down]
    updated = {}

    def update_layer(l, when):
        for group, keys in (("ffn", ("wg", "wu", "wd")), ("mix", (f"win{l}", "wkv", "wo"))):
            items = []
            for key in keys:
                idx, transposed = sharded[key]
                layer, rkey = (0, key) if key.startswith("win") else (l, f"{key}{l}")
                part, landed = reduced[rkey]
                w_rows = rows_of(weights[idx], transposed)
                items.append((layer, after(w_rows, when) if not items else w_rows, rows_of(moms[idx], transposed),
                              rows_of(vels[idx], transposed), part, landed, updated.get(key)))
            for key, result in zip(keys, adamw_shards(items, chip_arr, name=f"adamw_{group}_{l}")):
                updated[key] = result

    mix_landed = None
    for l in (1, 0):
        sv = saved[l]
        dgate, dup, dx1_b, dcat, small[f"norm2_{l}"] = ffn_bwd(
            dx_b, wd[l], sv["gate"], sv["up"], wg_t[l], wu_t[l], sv["x1"], norm2_g[l:l + 1], wo[l], name=f"ffn_bwd_{l}")
        if l == 0:
            dgate = after(dgate, *mix_landed)
            update_layer(1, dx1_b)
        big[f"wg{l}"], big[f"wu{l}"], big[f"wd{l}"] = ffn_weight_grads(
            dgate, dup, sv["h2"], sv["act"], dx_b, name=f"grad_ffn_{l}")
        stage1 = scatter_siblings([f"wd{l}", f"wg{l}", f"wu{l}"])
        big[f"wo{l}"] = mm_tn(sv["cat"], dx1_b, name=f"grad_wo_{l}")
        parts, ffn_landed = scatter_chips(stage1, big[f"wo{l}"])
        dcat = after(dcat, *parts)
        if l == 0:
            dq, dk, dv, dbias, small["a_q"], small["a_k"] = attn_bwd(sv["z"], dcat, gq2, gk2, bias, batch, seq)
            small["rel"] = bias_grad(dbias)
            win_t = a_win_t
            dz = None
            dcat = after(dcat, dq, *ffn_landed)
        else:
            dz, small["cw"], small["csum"] = conv_bwd(sv["z"], sv["y_conv"], dcat, cw_full, lg_full, lb_full, batch, seq)
            win_t = b_win_t
            dz = after(dz, *ffn_landed)
        dqm, small[f"mq_{l}"], small[f"mk_{l}"], big[f"wkv{l}"], small[f"memnorm_{l}"] = memattn_bwd(
            sv["z"], sv["kv"], dcat, sv["gq4"], sv["gk4"], mem2, sv["mem_n"], wkv[l], dz, batch, seq, sv["qcol"],
            name=f"memattn_bwd_{l}")
        if l == 0:
            pieces = [dq, dk, dv, dqm]
            big["win0"] = grad_pieces(pieces, sv["h"], name="grad_win_0")
        else:
            pieces = [dqm]
            big["win1"] = mm_tn(dqm, sv["h"], name="grad_win_1")
        stage1 = scatter_siblings([f"win{l}", f"wkv{l}", f"wo{l}"])
        dx_b, small[f"norm1_{l}"], dz_sum = in_proj_bwd(
            pieces, win_t, sv["xin"], norm1_g[l:l + 1], dx1_b, BF16 if l == 1 else F32, name=f"in_proj_bwd_{l}")
        if l == 1:
            small["bb"] = dz_sum
        parts, mix_landed = scatter_chips(stage1, dx_b)
        dx_b = after(dx_b, *parts)
    grad_x = dx_b.reshape(batch, seq, d)
    update_layer(0, dx_b)

    def shaped(rows, idx, transposed):
        shp = weights[idx].shape
        if transposed:
            return jnp.swapaxes(rows.reshape(shp[0], shp[2], shp[1]), 1, 2)
        return rows.reshape(shp)

    def fold(v, groups):
        return jnp.sum(v.reshape(groups, HEAD_DIM), axis=0, keepdims=True)

    heads = a_rel_bias.shape[1]
    small_list = [
        jnp.concatenate([small["norm1_0"], small["norm1_1"]]),
        jnp.concatenate([small["memnorm_0"], small["memnorm_1"]]),
        fold(small["a_q"], 2), fold(small["a_k"], 2), small["rel"][:heads],
        small["bb"], small["cw"][:CONV_W], small["csum"][0:1], small["csum"][1:2], small["csum"][2:3],
        jnp.concatenate([fold(small["mq_0"], 4), fold(small["mq_1"], 4)]),
        jnp.concatenate([fold(small["mk_0"], 4), fold(small["mk_1"], 4)]),
        jnp.concatenate([small["norm2_0"], small["norm2_1"]]),
    ]
    (g_norm1, g_memnorm, g_aq, g_ak, g_rel, g_bb_full, g_cw_full, g_cb_full, g_lg_full, g_lb_full,
     g_mq, g_mk, g_norm2, loss_sum) = reduce_small(small_list + [loss_blk])
    loss = loss_sum[0, 0]
    g_bb = lax.dynamic_slice_in_dim(g_bb_full, me * f_loc, f_loc, axis=1)
    g_cw = lax.dynamic_slice_in_dim(g_cw_full, me * c_loc, c_loc, axis=1)
    g_cb = lax.dynamic_slice_in_dim(g_cb_full, me * c_loc, c_loc, axis=1)
    g_lg = lax.dynamic_slice_in_dim(g_lg_full, me * c_loc, c_loc, axis=1)
    g_lb = lax.dynamic_slice_in_dim(g_lb_full, me * c_loc, c_loc, axis=1)

    grads = [g_norm1, g_memnorm, None, g_aq, g_ak, g_rel, None, g_bb, g_cw, g_cb, g_lg, g_lb,
             g_mq, g_mk, None, None, g_norm2, None, None, None]
    deltas, new_m, new_v = [None] * 20, [None] * 20, [None] * 20
    for key, (idx, transposed) in sharded.items():
        grads[idx], deltas[idx], new_m[idx], new_v[idx] = (shaped(r, idx, transposed) for r in updated[key])

    def flat2(a):
        return a.reshape(a.shape[-2:])

    small_idx = [i for i in range(20) if i not in {idx for idx, _ in sharded.values()}]
    dl, nm, nv = adamw_small([flat2(weights[i]) for i in small_idx], [flat2(grads[i]) for i in small_idx],
                             [flat2(moms[i]) for i in small_idx], [flat2(vels[i]) for i in small_idx])
    for i, a, b, cc in zip(small_idx, dl, nm, nv):
        shp = weights[i].shape
        grads[i], deltas[i], new_m[i], new_v[i] = grads[i].reshape(shp), a.reshape(shp), b.reshape(shp), cc.reshape(shp)

    return (loss, grad_x, *grads, *deltas, *new_m, *new_v)
```

```python
import jax
import jax.numpy as jnp
from jax import lax
from jax.experimental import pallas as pl
from jax.experimental.pallas import tpu as pltpu
from jax.experimental.pallas import tpu_sc as plsc

F32 = jnp.float32
BF16 = jnp.bfloat16
HIGHEST = lax.Precision.HIGHEST
MESH = pl.DeviceIdType.MESH
ANY = pl.BlockSpec(memory_space=pl.ANY)

N_DEV = 8
D_MODEL = 1024
HEAD_DIM = 64
TOK_WIDTH = 768
MEM_WIDTH = 256
CHUNK = 64
Q_BLOCK = 256
KEY_WIN = 768
BAND = 576
N_REL = 192
CONV_W = 31
CONV_HALO = 32
NORM_EPS = 1e-6
NEG_INF = -1e30
ATTN_SCALE = HEAD_DIM ** -0.5
LANES = 128
ROW_TILE = 512
VMEM_LIMIT = 56 * 1024 * 1024

ADAM_LR, ADAM_B1, ADAM_B2, ADAM_EPS, ADAM_WD, ADAM_STEP = 0.001, 0.9, 0.999, 1e-08, 0.01, 10


def _params(*sem):
    return pltpu.CompilerParams(dimension_semantics=sem, vmem_limit_bytes=VMEM_LIMIT)


WIDE_ROW_TILE = 1024


def _row_tile(m, rows=ROW_TILE):
    return rows if m % rows == 0 else m


def _col_tile(n, cap=1408):
    best = None
    for t in range(LANES, min(n, cap) + 1, LANES):
        if n % t == 0:
            best = t
    return best if best is not None else n


def _dot(a, b, ca, cb):
    return lax.dot_general(a, b, (((ca,), (cb,)), ((), ())), preferred_element_type=F32)


def _sigmoid(x):
    return 0.5 * jnp.tanh(0.5 * x) + 0.5


def mm_nt(a, b, bias=None, out_dtype=BF16, name="mm_nt"):
    m, k = a.shape
    n = b.shape[0]
    tm, tn = _row_tile(m, WIDE_ROW_TILE), _col_tile(n)

    def body(*refs):
        a_ref, b_ref = refs[0], refs[1]
        o_ref = refs[-1]
        acc = _dot(a_ref[...].astype(BF16), b_ref[...].astype(BF16), 1, 1)
        if bias is not None:
            acc = acc + refs[2][...]
        o_ref[...] = acc.astype(o_ref.dtype)

    in_specs = [pl.BlockSpec((tm, k), lambda j, i: (i, 0)), pl.BlockSpec((tn, k), lambda j, i: (j, 0))]
    args = [a, b]
    if bias is not None:
        in_specs.append(pl.BlockSpec((1, tn), lambda j, i: (0, j)))
        args.append(bias)
    return pl.pallas_call(
        body, out_shape=jax.ShapeDtypeStruct((m, n), out_dtype), grid=(n // tn, m // tm),
        in_specs=in_specs, out_specs=pl.BlockSpec((tm, tn), lambda j, i: (i, j)),
        compiler_params=_params("parallel", "arbitrary"), name=name)(*args)


def mm_tn(a, b, out_dtype=BF16, name="mm_tn"):
    t, r = a.shape
    c = b.shape[1]
    tr = _col_tile(r, 512)

    def body(a_ref, b_ref, o_ref):
        o_ref[...] = _dot(a_ref[...].astype(BF16), b_ref[...].astype(BF16), 0, 0).astype(o_ref.dtype)

    return pl.pallas_call(
        body, out_shape=jax.ShapeDtypeStruct((r, c), out_dtype), grid=(r // tr,),
        in_specs=[pl.BlockSpec((t, tr), lambda i: (0, i)), pl.BlockSpec((t, c), lambda i: (0, 0))],
        out_specs=pl.BlockSpec((tr, c), lambda i: (i, 0)),
        compiler_params=_params("parallel"), name=name)(a, b)


def _resident(shape):
    return pl.BlockSpec(shape, lambda i: (0, 0), pipeline_mode=pl.Buffered(1))


def proj_norm(a, b, res, gain, name):
    m, k = a.shape
    n = b.shape[1]
    tm = _row_tile(m)

    def body(a_ref, b_ref, res_ref, g_ref, x_ref, h_ref):
        xv = res_ref[...] + _dot(a_ref[...], b_ref[...], 1, 0)
        x_ref[...] = xv
        r = lax.rsqrt(jnp.mean(xv * xv, axis=-1, keepdims=True) + NORM_EPS)
        h_ref[...] = (xv * r * g_ref[...]).astype(BF16)

    row = pl.BlockSpec((tm, n), lambda i: (i, 0))
    return pl.pallas_call(
        body, out_shape=(jax.ShapeDtypeStruct((m, n), F32), jax.ShapeDtypeStruct((m, n), BF16)), grid=(m // tm,),
        in_specs=[pl.BlockSpec((tm, k), lambda i: (i, 0)), _resident((k, n)), row, _resident((1, n))],
        out_specs=(row, row), compiler_params=_params("parallel"), name=name)(a, b, res, gain)


def in_proj_bwd(pieces, w_t, x, gain, dres, out_dtype, name):
    m, n = x.shape
    k = pieces[0].shape[1]
    tm = _row_tile(m)
    npc = len(pieces)
    offs = [sum(p.shape[1] for p in pieces[:i]) for i in range(npc + 1)]

    def body(*refs):
        dz_refs = refs[:npc]
        w_ref, x_ref, g_ref, dres_ref, dx_ref, dg_ref, cs_ref = refs[npc:]

        @pl.when(pl.program_id(0) == 0)
        def _():
            dg_ref[...] = jnp.zeros_like(dg_ref)
            cs_ref[...] = jnp.zeros_like(cs_ref)

        cs_ref[...] += jnp.sum(dz_refs[0][...].astype(F32), axis=0, keepdims=True)
        dhv = _dot(dz_refs[0][...], w_ref[offs[0]:offs[1], :], 1, 0)
        for i in range(1, npc):
            dhv = dhv + _dot(dz_refs[i][...], w_ref[offs[i]:offs[i + 1], :], 1, 0)
        xv = x_ref[...]
        r = lax.rsqrt(jnp.mean(xv * xv, axis=-1, keepdims=True) + NORM_EPS)
        xhat = xv * r
        dg_ref[...] += jnp.sum(dhv * xhat, axis=0, keepdims=True)
        dxhat = dhv * g_ref[...]
        dx = dres_ref[...].astype(F32) + r * (dxhat - xhat * jnp.mean(dxhat * xhat, axis=-1, keepdims=True))
        dx_ref[...] = dx.astype(dx_ref.dtype)

    row = pl.BlockSpec((tm, n), lambda i: (i, 0))
    return pl.pallas_call(
        body, out_shape=(jax.ShapeDtypeStruct((m, n), out_dtype), jax.ShapeDtypeStruct((1, n), F32),
                         jax.ShapeDtypeStruct((1, k), F32)), grid=(m // tm,),
        in_specs=[pl.BlockSpec((tm, p.shape[1]), lambda i: (i, 0)) for p in pieces]
        + [_resident(w_t.shape), row, _resident((1, n)), row],
        out_specs=(row, pl.BlockSpec((1, n), lambda i: (0, 0)), pl.BlockSpec((1, k), lambda i: (0, 0))),
        compiler_params=_params("arbitrary"), name=name)(*pieces, w_t, x, gain, dres)


def grad_pieces(pieces, b, name):
    t, c = b.shape
    tr = 2 * LANES
    tiles = [p.shape[1] // tr for p in pieces]
    starts = [sum(tiles[:i]) for i in range(len(pieces) + 1)]

    def body(*refs):
        a_refs, b_ref, o_ref = refs[:len(pieces)], refs[len(pieces)], refs[len(pieces) + 1]
        i = pl.program_id(0)
        for p, a_ref in enumerate(a_refs):
            @pl.when((i >= starts[p]) & (i < starts[p + 1]))
            def _(a_ref=a_ref):
                o_ref[...] = _dot(a_ref[...], b_ref[...], 0, 0).astype(o_ref.dtype)

    def a_spec(p):
        return pl.BlockSpec((t, tr), lambda i: (0, jnp.clip(i - starts[p], 0, tiles[p] - 1)))

    return pl.pallas_call(
        body, out_shape=jax.ShapeDtypeStruct((starts[-1] * tr, c), BF16), grid=(starts[-1],),
        in_specs=[a_spec(p) for p in range(len(pieces))] + [_resident((t, c))],
        out_specs=pl.BlockSpec((tr, c), lambda i: (i, 0)),
        compiler_params=_params("arbitrary"), name=name)(*pieces, b)


FFN_ROWS = 256


def _ffn_row_tile(m):
    return FFN_ROWS if m % FFN_ROWS == 0 else m


def ffn_fwd(h2, wg_t, wu_t, wd, x1, gain=None, target=None, name="ffn_fwd"):
    n, d = h2.shape
    f = wg_t.shape[0]
    tm = _ffn_row_tile(n)
    nt = n // tm
    last = target is not None

    def body(h_ref, wg_ref, wu_ref, wd_ref, x1_ref, e_ref, g_ref, u_ref, a_ref, *rest):
        hv = h_ref[...]
        gv = _dot(hv, wg_ref[...], 1, 1)
        uv = _dot(hv, wu_ref[...], 1, 1)
        g_ref[...] = gv.astype(BF16)
        u_ref[...] = uv.astype(BF16)
        av = (gv * _sigmoid(gv) * uv).astype(BF16)
        a_ref[...] = av
        xv = x1_ref[...] + _dot(av, wd_ref[...], 1, 0)
        if not last:
            x_ref, hn_ref = rest
            x_ref[...] = xv
            r = lax.rsqrt(jnp.mean(xv * xv, axis=-1, keepdims=True) + NORM_EPS)
            hn_ref[...] = (xv * r * e_ref[...]).astype(BF16)
        else:
            dyb_ref, l_ref, acc_ref = rest
            i = pl.program_id(0)

            @pl.when(i == 0)
            def _():
                acc_ref[...] = jnp.zeros_like(acc_ref)

            err = xv - e_ref[...]
            dyb_ref[...] = (err * (1.0 / d)).astype(BF16)
            acc_ref[...] += jnp.sum(err * err, axis=0, keepdims=True)

            @pl.when(i == nt - 1)
            def _():
                total = jnp.sum(acc_ref[...], axis=-1, keepdims=True) * (0.5 / d)
                l_ref[...] = jnp.broadcast_to(total, l_ref.shape)

    row_d = pl.BlockSpec((tm, d), lambda i: (i, 0))
    row_f = pl.BlockSpec((tm, f), lambda i: (i, 0))
    act_shape = jax.ShapeDtypeStruct((n, f), BF16)
    if not last:
        extra_in, extra = _resident((1, d)), gain
        out_shape = (act_shape, act_shape, act_shape, jax.ShapeDtypeStruct((n, d), F32), jax.ShapeDtypeStruct((n, d), BF16))
        out_specs = (row_f, row_f, row_f, row_d, row_d)
        scratch = []
    else:
        extra_in, extra = row_d, target
        out_shape = (act_shape, act_shape, act_shape, jax.ShapeDtypeStruct((n, d), BF16),
                     jax.ShapeDtypeStruct((8, LANES), F32))
        out_specs = (row_f, row_f, row_f, row_d, pl.BlockSpec((8, LANES), lambda i: (0, 0)))
        scratch = [pltpu.VMEM((1, d), F32)]
    return pl.pallas_call(
        body, out_shape=out_shape, grid=(nt,),
        in_specs=[row_d, _resident((f, d)), _resident((f, d)), _resident((f, d)), row_d, extra_in],
        out_specs=out_specs, scratch_shapes=scratch,
        compiler_params=_params("arbitrary"), name=name)(h2, wg_t, wu_t, wd, x1, extra)


def ffn_bwd(dx_b, wd, gate, up, wg_t, wu_t, x1, gain, wo, name="ffn_bwd"):
    n, d = x1.shape
    f = wd.shape[0]
    tm = _ffn_row_tile(n)

    def body(dxb_ref, wd_ref, g_ref, u_ref, wg_ref, wu_ref, x_ref, gain_ref, wo_ref,
             dg_ref, du_ref, dxo_ref, dc_ref, dgain_ref):
        @pl.when(pl.program_id(0) == 0)
        def _():
            dgain_ref[...] = jnp.zeros_like(dgain_ref)

        dact = _dot(dxb_ref[...], wd_ref[...], 1, 1)
        gv = g_ref[...].astype(F32)
        uv = u_ref[...].astype(F32)
        sg = _sigmoid(gv)
        dgv = (dact * uv * sg * (1.0 + gv * (1.0 - sg))).astype(BF16)
        duv = (dact * gv * sg).astype(BF16)
        dg_ref[...] = dgv
        du_ref[...] = duv
        dhv = _dot(dgv, wg_ref[...], 1, 0) + _dot(duv, wu_ref[...], 1, 0)
        xv = x_ref[...]
        r = lax.rsqrt(jnp.mean(xv * xv, axis=-1, keepdims=True) + NORM_EPS)
        xhat = xv * r
        dgain_ref[...] += jnp.sum(dhv * xhat, axis=0, keepdims=True)
        dxhat = dhv * gain_ref[...]
        dxb = (dxb_ref[...].astype(F32) + r * (dxhat - xhat * jnp.mean(dxhat * xhat, axis=-1, keepdims=True))).astype(BF16)
        dxo_ref[...] = dxb
        dc_ref[...] = _dot(dxb, wo_ref[...], 1, 1).astype(BF16)

    row_d = pl.BlockSpec((tm, d), lambda i: (i, 0))
    row_f = pl.BlockSpec((tm, f), lambda i: (i, 0))
    w_spec = _resident((f, d))
    act_shape = jax.ShapeDtypeStruct((n, f), BF16)
    row_shape = jax.ShapeDtypeStruct((n, d), BF16)
    return pl.pallas_call(
        body, out_shape=(act_shape, act_shape, row_shape, jax.ShapeDtypeStruct((n, wo.shape[0]), BF16),
                         jax.ShapeDtypeStruct((1, d), F32)),
        grid=(n // tm,),
        in_specs=[row_d, w_spec, row_f, row_f, w_spec, w_spec, row_d, _resident((1, d)), _resident(wo.shape)],
        out_specs=(row_f, row_f, row_d, pl.BlockSpec((tm, wo.shape[0]), lambda i: (i, 0)),
                   pl.BlockSpec((1, d), lambda i: (0, 0))),
        compiler_params=_params("arbitrary"), name=name)(dx_b, wd, gate, up, wg_t, wu_t, x1, gain, wo)


def ffn_weight_grads(dgate, dup, h2, act, dx_b, name="ffn_weight_grads"):
    t, r = dgate.shape
    c = h2.shape[1]
    tr = _col_tile(r, 512)

    def body(a1_ref, a2_ref, a3_ref, b12_ref, b3_ref, o1_ref, o2_ref, o3_ref):
        bv = b12_ref[...]
        o1_ref[...] = _dot(a1_ref[...], bv, 0, 0).astype(o1_ref.dtype)
        o2_ref[...] = _dot(a2_ref[...], bv, 0, 0).astype(o2_ref.dtype)
        o3_ref[...] = _dot(a3_ref[...], b3_ref[...], 0, 0).astype(o3_ref.dtype)

    a_spec = pl.BlockSpec((t, tr), lambda i: (0, i))
    o_spec = pl.BlockSpec((tr, c), lambda i: (i, 0))
    shape = jax.ShapeDtypeStruct((r, c), BF16)
    return pl.pallas_call(
        body, out_shape=(shape, shape, shape), grid=(r // tr,),
        in_specs=[a_spec, a_spec, a_spec, _resident((t, c)), _resident((t, c))],
        out_specs=(o_spec, o_spec, o_spec), compiler_params=_params("parallel"), name=name)(dgate, dup, act, h2, dx_b)


def rms_fwd(x, g, name="rms_fwd"):
    n, d = x.shape
    tm = _row_tile(n)

    def body(x_ref, g_ref, o_ref):
        xv = x_ref[...]
        r = lax.rsqrt(jnp.mean(xv * xv, axis=-1, keepdims=True) + NORM_EPS)
        o_ref[...] = (xv * r * g_ref[...]).astype(o_ref.dtype)

    return pl.pallas_call(
        body, out_shape=jax.ShapeDtypeStruct((n, d), BF16), grid=(n // tm,),
        in_specs=[pl.BlockSpec((tm, d), lambda i: (i, 0)), pl.BlockSpec((1, d), lambda i: (0, 0))],
        out_specs=pl.BlockSpec((tm, d), lambda i: (i, 0)),
        compiler_params=_params("parallel"), name=name)(x, g)


def _group_masks(width):
    lane = lax.broadcasted_iota(jnp.int32, (1, width), 1)
    return [(lane >= HEAD_DIM * g) & (lane < HEAD_DIM * (g + 1)) for g in range(width // HEAD_DIM)]


def _group_sum(x, masks):
    out = jnp.zeros_like(x)
    for msk in masks:
        s = jnp.sum(jnp.where(msk, x, 0.0), axis=-1, keepdims=True)
        out = jnp.where(msk, s, out)
    return out


def _head_norm(x, gain, masks):
    r = lax.rsqrt(_group_sum(x * x, masks) * (1.0 / HEAD_DIM) + NORM_EPS)
    xhat = x * r
    return xhat * gain, xhat, r


def _head_norm_bwd(dxn, xhat, r, gain, masks):
    dgain = jnp.sum(dxn * xhat, axis=0, keepdims=True)
    dxhat = dxn * gain
    mean_t = _group_sum(dxhat * xhat, masks) * (1.0 / HEAD_DIM)
    return r * (dxhat - xhat * mean_t), dgain


def _softmax_rows(s):
    e = jnp.exp(s - jnp.max(s, axis=-1, keepdims=True))
    return e * (1.0 / jnp.sum(e, axis=-1, keepdims=True))


def _rel_onehot():
    col = lax.broadcasted_iota(jnp.int32, (1, KEY_WIN), 1)
    off = jnp.where(col < KEY_WIN - LANES, col, col - KEY_WIN)
    idx = jnp.clip(8 * CHUNK - off, -(CHUNK - 1), LANES) + (CHUNK - 1)
    return (lax.broadcasted_iota(jnp.int32, (N_REL, KEY_WIN), 0) == idx).astype(F32)


def bias_blocks(rel16):
    heads = TOK_WIDTH // HEAD_DIM

    def body(rel_ref, o_ref, u_ref):
        u_ref[...] = jnp.dot(rel_ref[...], _rel_onehot(), precision=HIGHEST, preferred_element_type=F32)
        row = lax.broadcasted_iota(jnp.int32, (CHUNK, KEY_WIN), 0)
        col = lax.broadcasted_iota(jnp.int32, (CHUNK, KEY_WIN), 1)
        for h in range(heads):
            xv = jnp.broadcast_to(u_ref[h:h + 1, :], (CHUNK, KEY_WIN))
            for b in range(6):
                xv = jnp.where(((row >> b) & 1) == 1, pltpu.roll(xv, 1 << b, axis=1), xv)
            xv = jnp.where(col < BAND, xv, NEG_INF)
            for i in range(Q_BLOCK // CHUNK):
                o_ref[h, CHUNK * i:CHUNK * (i + 1), :] = pltpu.roll(xv, CHUNK * i, axis=1) if i else xv

    return pl.pallas_call(
        body, out_shape=jax.ShapeDtypeStruct((heads, Q_BLOCK, KEY_WIN), F32),
        scratch_shapes=[pltpu.VMEM((16, KEY_WIN), F32)], name="bias_blocks")(rel16)


def bias_grad(dbias):
    heads = dbias.shape[0]

    def body(db_ref, o_ref, y_ref):
        y_ref[...] = jnp.zeros_like(y_ref)
        row = lax.broadcasted_iota(jnp.int32, (CHUNK, KEY_WIN), 0)
        for h in range(heads):
            fv = db_ref[h, 0:CHUNK, :]
            for i in range(1, Q_BLOCK // CHUNK):
                fv = fv + pltpu.roll(db_ref[h, CHUNK * i:CHUNK * (i + 1), :], KEY_WIN - CHUNK * i, axis=1)
            for b in range(6):
                fv = jnp.where(((row >> b) & 1) == 1, pltpu.roll(fv, KEY_WIN - (1 << b), axis=1), fv)
            y_ref[h:h + 1, :] = jnp.sum(fv, axis=0, keepdims=True)
        o_ref[...] = lax.dot_general(y_ref[...], _rel_onehot(), (((1,), (1,)), ((), ())),
                                     precision=HIGHEST, preferred_element_type=F32)

    return pl.pallas_call(
        body, out_shape=jax.ShapeDtypeStruct((16, N_REL), F32),
        scratch_shapes=[pltpu.VMEM((16, KEY_WIN), F32)], name="bias_grad")(dbias)


def _attn_windows(seq):
    out = []
    for j in range(seq // Q_BLOCK):
        r0 = j * Q_BLOCK
        k0 = max(0, r0 - 8 * CHUNK)
        width = r0 + Q_BLOCK - k0
        out.append((r0, k0, width, KEY_WIN - width))
    return out


def attn_fwd(z, gq2, gk2, bias, batch, seq):
    n = z.shape[0]
    pairs = TOK_WIDTH // LANES

    def body(q_ref, k_ref, v_ref, gq_ref, gk_ref, b_ref, o_ref, qs_s, kn_s):
        masks = _group_masks(LANES)
        qs_s[...] = (_head_norm(q_ref[...].astype(F32), gq_ref[...], masks)[0] * ATTN_SCALE).astype(BF16)
        kn_s[...] = _head_norm(k_ref[...].astype(F32), gk_ref[...], masks)[0].astype(BF16)
        for r0, k0, width, c0 in _attn_windows(seq):
            qb = qs_s[r0:r0 + Q_BLOCK, :]
            kw = kn_s[k0:k0 + width, :]
            vw = v_ref[k0:k0 + width, :]
            out = jnp.zeros((Q_BLOCK, LANES), F32)
            for h, msk in enumerate(masks):
                qh = jnp.where(msk, qb, jnp.zeros_like(qb))
                s = _dot(qh, kw, 1, 1) + b_ref[h, :, c0:KEY_WIN]
                p = _softmax_rows(s).astype(BF16)
                out = jnp.where(msk, _dot(p, vw, 1, 0), out)
            o_ref[r0:r0 + Q_BLOCK, :] = out.astype(o_ref.dtype)

    def col(off):
        return pl.BlockSpec((seq, LANES), lambda b, p: (b, off + p))

    vec = pl.BlockSpec((1, LANES), lambda b, p: (0, 0))
    return pl.pallas_call(
        body, out_shape=jax.ShapeDtypeStruct((n, D_MODEL), BF16), grid=(batch, pairs),
        in_specs=[col(0), col(pairs), col(2 * pairs), vec, vec,
                  pl.BlockSpec((2, Q_BLOCK, KEY_WIN), lambda b, p: (p, 0, 0))],
        out_specs=pl.BlockSpec((seq, LANES), lambda b, p: (b, p)),
        scratch_shapes=[pltpu.VMEM((seq, LANES), BF16), pltpu.VMEM((seq, LANES), BF16)],
        compiler_params=_params("parallel", "arbitrary"), name="attn_fwd")(z, z, z, gq2, gk2, bias)


def attn_bwd(z, dcat, gq2, gk2, bias, batch, seq):
    n = z.shape[0]
    pairs = TOK_WIDTH // LANES

    def body(q_ref, k_ref, v_ref, do_ref, gq_ref, gk_ref, b_ref,
             dq_ref, dk_ref, dv_ref, db_ref, dgq_ref, dgk_ref, qs_s, kn_s, dqn_s, dkn_s, dv_s):
        pi, bi = pl.program_id(0), pl.program_id(1)
        masks = _group_masks(LANES)

        @pl.when(bi == 0)
        def _():
            db_ref[...] = jnp.zeros_like(db_ref)

        @pl.when((bi == 0) & (pi == 0))
        def _():
            dgq_ref[...] = jnp.zeros_like(dgq_ref)
            dgk_ref[...] = jnp.zeros_like(dgk_ref)

        qn, qhat, rq = _head_norm(q_ref[...].astype(F32), gq_ref[...], masks)
        kn, khat, rk = _head_norm(k_ref[...].astype(F32), gk_ref[...], masks)
        qs_s[...] = (qn * ATTN_SCALE).astype(BF16)
        kn_s[...] = kn.astype(BF16)
        dkn_s[...] = jnp.zeros_like(dkn_s)
        dv_s[...] = jnp.zeros_like(dv_s)
        for r0, k0, width, c0 in _attn_windows(seq):
            qb = qs_s[r0:r0 + Q_BLOCK, :]
            dob = do_ref[r0:r0 + Q_BLOCK, :]
            kw = kn_s[k0:k0 + width, :]
            vw = v_ref[k0:k0 + width, :]
            dq_acc = jnp.zeros((Q_BLOCK, LANES), F32)
            dk_acc = jnp.zeros((width, LANES), F32)
            dv_acc = jnp.zeros((width, LANES), F32)
            for h, msk in enumerate(masks):
                qh = jnp.where(msk, qb, jnp.zeros_like(qb))
                doh = jnp.where(msk, dob, jnp.zeros_like(dob))
                p = _softmax_rows(_dot(qh, kw, 1, 1) + b_ref[h, :, c0:KEY_WIN])
                dp = _dot(doh, vw, 1, 1)
                ds = p * (dp - jnp.sum(p * dp, axis=-1, keepdims=True))
                db_ref[h, :, c0:KEY_WIN] += ds
                dsb = ds.astype(BF16)
                dq_acc = jnp.where(msk, _dot(dsb, kw, 1, 0), dq_acc)
                dk_acc = jnp.where(msk, _dot(dsb, qb, 0, 0), dk_acc)
                dv_acc = jnp.where(msk, _dot(p.astype(BF16), dob, 0, 0), dv_acc)
            dqn_s[r0:r0 + Q_BLOCK, :] = dq_acc * ATTN_SCALE
            dkn_s[k0:k0 + width, :] += dk_acc
            dv_s[k0:k0 + width, :] += dv_acc
        dq, dgq = _head_norm_bwd(dqn_s[...], qhat, rq, gq_ref[...], masks)
        dk, dgk = _head_norm_bwd(dkn_s[...], khat, rk, gk_ref[...], masks)
        dq_ref[...] = dq.astype(dq_ref.dtype)
        dk_ref[...] = dk.astype(dk_ref.dtype)
        dv_ref[...] = dv_s[...].astype(dv_ref.dtype)
        dgq_ref[...] += dgq
        dgk_ref[...] += dgk

    def col(off):
        return pl.BlockSpec((seq, LANES), lambda p, b: (b, off + p))

    vec = pl.BlockSpec((1, LANES), lambda p, b: (0, 0))
    blk = pl.BlockSpec((2, Q_BLOCK, KEY_WIN), lambda p, b: (p, 0, 0))
    o_shape = jax.ShapeDtypeStruct((n, TOK_WIDTH), BF16)
    v_shape = jax.ShapeDtypeStruct((1, LANES), F32)
    return pl.pallas_call(
        body,
        out_shape=(o_shape, o_shape, o_shape, jax.ShapeDtypeStruct(bias.shape, F32), v_shape, v_shape),
        grid=(pairs, batch),
        in_specs=[col(0), col(pairs), col(2 * pairs), col(0), vec, vec, blk],
        out_specs=(col(0), col(0), col(0), blk, vec, vec),
        scratch_shapes=[pltpu.VMEM((seq, LANES), BF16), pltpu.VMEM((seq, LANES), BF16),
                        pltpu.VMEM((seq, LANES), F32), pltpu.VMEM((seq, LANES), F32), pltpu.VMEM((seq, LANES), F32)],
        compiler_params=_params("arbitrary", "arbitrary"), name="attn_bwd")(z, z, z, dcat, gq2, gk2, bias)


MEM_ROWS_FWD = 1024
MEM_ROWS_BWD = 2048


def memattn_fwd(z, mem, mem_gain, wkv, gq4, gk4, cat, batch, seq, qcol, name):
    mtok = mem.shape[0] // batch
    d = mem.shape[1]
    rows = min(MEM_ROWS_FWD, seq)

    def body(q_ref, m_ref, mg_ref, w_ref, gq_ref, gk_ref, cat_ref, o_ref, n_ref, kv_ref):
        del cat_ref
        masks = _group_masks(MEM_WIDTH)
        mv = m_ref[...]
        r = lax.rsqrt(jnp.mean(mv * mv, axis=-1, keepdims=True) + NORM_EPS)
        nv = (mv * r * mg_ref[...]).astype(BF16)
        n_ref[...] = nv
        kv_ref[...] = _dot(nv, w_ref[...], 1, 0)
        kn = _head_norm(kv_ref[:, 0:MEM_WIDTH], gk_ref[...], masks)[0].astype(BF16)
        vm = kv_ref[:, MEM_WIDTH:2 * MEM_WIDTH].astype(BF16)
        for t in range(seq // rows):
            sl = slice(t * rows, (t + 1) * rows)
            qs = (_head_norm(q_ref[sl, :].astype(F32), gq_ref[...], masks)[0] * ATTN_SCALE).astype(BF16)
            out = jnp.zeros((rows, MEM_WIDTH), F32)
            for msk in masks:
                qh = jnp.where(msk, qs, jnp.zeros_like(qs))
                p = _softmax_rows(_dot(qh, kn, 1, 1)).astype(BF16)
                out = jnp.where(msk, _dot(p, vm, 1, 0), out)
            o_ref[sl, :] = out.astype(o_ref.dtype)

    vec = pl.BlockSpec((1, MEM_WIDTH), lambda b: (0, 0))
    mem_spec = pl.BlockSpec((mtok, d), lambda b: (b, 0))
    kv_spec = pl.BlockSpec((mtok, 2 * MEM_WIDTH), lambda b: (b, 0))
    return pl.pallas_call(
        body, out_shape=(jax.ShapeDtypeStruct(cat.shape, cat.dtype), jax.ShapeDtypeStruct(mem.shape, BF16),
                         jax.ShapeDtypeStruct((mem.shape[0], 2 * MEM_WIDTH), F32)), grid=(batch,),
        in_specs=[pl.BlockSpec((seq, MEM_WIDTH), lambda b: (b, qcol)), mem_spec, pl.BlockSpec((1, d), lambda b: (0, 0)),
                  pl.BlockSpec(wkv.shape, lambda b: (0, 0)), vec, vec, ANY],
        out_specs=(pl.BlockSpec((seq, MEM_WIDTH), lambda b: (b, TOK_WIDTH // MEM_WIDTH)), mem_spec, kv_spec),
        input_output_aliases={6: 0},
        compiler_params=_params("parallel"), name=name)(z, mem, mem_gain, wkv, gq4, gk4, cat)


def memattn_bwd(z, kv, dcat, gq4, gk4, mem, mem_n, wkv, dz, batch, seq, qcol, name):
    mtok = kv.shape[0] // batch
    d = mem.shape[1]
    rows = min(MEM_ROWS_BWD, seq)

    def body(q_ref, kv_ref, do_ref, gq_ref, gk_ref, m_ref, n_ref, w_ref, *rest):
        dq_ref, dgq_ref, dgk_ref, dw_ref, dmg_ref, dw_acc = rest[-6:]

        @pl.when(pl.program_id(0) == 0)
        def _():
            dgq_ref[...] = jnp.zeros_like(dgq_ref)
            dgk_ref[...] = jnp.zeros_like(dgk_ref)
            dmg_ref[...] = jnp.zeros_like(dmg_ref)
            dw_acc[...] = jnp.zeros_like(dw_acc)

        masks = _group_masks(MEM_WIDTH)
        kn_f, khat, rk = _head_norm(kv_ref[:, 0:MEM_WIDTH], gk_ref[...], masks)
        kn = kn_f.astype(BF16)
        vm = kv_ref[:, MEM_WIDTH:2 * MEM_WIDTH].astype(BF16)
        dkn = jnp.zeros((mtok, MEM_WIDTH), F32)
        dvm = jnp.zeros((mtok, MEM_WIDTH), F32)
        dgq = jnp.zeros((1, MEM_WIDTH), F32)
        for t in range(seq // rows):
            sl = slice(t * rows, (t + 1) * rows)
            qn_f, qhat, rq = _head_norm(q_ref[sl, :].astype(F32), gq_ref[...], masks)
            qs = (qn_f * ATTN_SCALE).astype(BF16)
            dob = do_ref[sl, :]
            dqn = jnp.zeros((rows, MEM_WIDTH), F32)
            for msk in masks:
                qh = jnp.where(msk, qs, jnp.zeros_like(qs))
                doh = jnp.where(msk, dob, jnp.zeros_like(dob))
                p = _softmax_rows(_dot(qh, kn, 1, 1))
                dp = _dot(doh, vm, 1, 1)
                ds = p * (dp - jnp.sum(p * dp, axis=-1, keepdims=True))
                dsb = ds.astype(BF16)
                dqn = jnp.where(msk, _dot(dsb, kn, 1, 0), dqn)
                dkn = dkn + jnp.where(msk, _dot(dsb, qs, 0, 0), 0.0)
                dvm = dvm + jnp.where(msk, _dot(p.astype(BF16), dob, 0, 0), 0.0)
            dq, dg = _head_norm_bwd(dqn * ATTN_SCALE, qhat, rq, gq_ref[...], masks)
            dq_ref[sl, :] = dq.astype(dq_ref.dtype)
            dgq = dgq + dg
        dk, dgk = _head_norm_bwd(dkn, khat, rk, gk_ref[...], masks)
        dgq_ref[...] += dgq
        dgk_ref[...] += dgk
        dkv_b = jnp.concatenate([dk, dvm], axis=-1).astype(BF16)
        dw_acc[...] += _dot(n_ref[...], dkv_b, 0, 0)
        dn = _dot(dkv_b, w_ref[...], 1, 1)
        mv = m_ref[...]
        rm = lax.rsqrt(jnp.mean(mv * mv, axis=-1, keepdims=True) + NORM_EPS)
        dmg_ref[...] += jnp.sum(dn * (mv * rm), axis=0, keepdims=True)

        @pl.when(pl.program_id(0) == batch - 1)
        def _():
            dw_ref[...] = dw_acc[...].astype(dw_ref.dtype)

    vec = pl.BlockSpec((1, MEM_WIDTH), lambda b: (0, 0))
    kv_spec = pl.BlockSpec((mtok, 2 * MEM_WIDTH), lambda b: (b, 0))
    mem_spec = pl.BlockSpec((mtok, d), lambda b: (b, 0))
    w_spec = pl.BlockSpec(wkv.shape, lambda b: (0, 0))
    v_shape = jax.ShapeDtypeStruct((1, MEM_WIDTH), F32)
    q_spec = pl.BlockSpec((seq, MEM_WIDTH), lambda b: (b, qcol))
    in_specs = [q_spec, kv_spec, pl.BlockSpec((seq, MEM_WIDTH), lambda b: (b, TOK_WIDTH // MEM_WIDTH)), vec, vec,
                mem_spec, mem_spec, w_spec]
    args = [z, kv, dcat, gq4, gk4, mem, mem_n, wkv]
    if dz is None:
        dq_shape, dq_spec, aliases = jax.ShapeDtypeStruct((z.shape[0], MEM_WIDTH), BF16), \
            pl.BlockSpec((seq, MEM_WIDTH), lambda b: (b, 0)), {}
    else:
        dq_shape, dq_spec, aliases = jax.ShapeDtypeStruct(dz.shape, dz.dtype), q_spec, {len(args): 0}
        in_specs.append(ANY)
        args.append(dz)
    return pl.pallas_call(
        body,
        out_shape=(dq_shape, v_shape, v_shape, jax.ShapeDtypeStruct(wkv.shape, BF16), jax.ShapeDtypeStruct((1, d), F32)),
        grid=(batch,), in_specs=in_specs,
        out_specs=(dq_spec, vec, vec, w_spec, pl.BlockSpec((1, d), lambda b: (0, 0))),
        scratch_shapes=[pltpu.VMEM(wkv.shape, F32)], input_output_aliases=aliases,
        compiler_params=_params("arbitrary"), name=name)(*args)


CONV_ROWS = 512


def _glu(a_ref, g_ref):
    return a_ref[...].astype(F32) * _sigmoid(g_ref[...].astype(F32))


def _layer_norm_stats(y):
    mu = jnp.mean(y, axis=-1, keepdims=True)
    yc = y - mu
    rstd = lax.rsqrt(jnp.mean(yc * yc, axis=-1, keepdims=True) + NORM_EPS)
    return yc * rstd, rstd


CONV_WIN = CONV_HALO + CONV_ROWS
SUBLANES = 8
SHIFT_ROWS = CONV_WIN - SUBLANES


def _preshift(win, shifted):
    for s in range(1, SUBLANES):
        shifted[s - 1, :, :] = win[s:s + SHIFT_ROWS, :]


TAP_ROWS = 64
TAP_TILES = [(r0, slice(c0, c0 + LANES)) for c0 in range(0, TOK_WIDTH, LANES) for r0 in range(0, CONV_ROWS, TAP_ROWS)]


def _tap(win, shifted, off, r0, lanes):
    s = off % SUBLANES
    base = off - s + r0
    if s == 0:
        return win[base:base + TAP_ROWS, lanes]
    return shifted[s - 1, base:base + TAP_ROWS, lanes]


def _fold_rows(x):
    return jnp.sum(x.reshape(TAP_ROWS // SUBLANES, SUBLANES, LANES), axis=0)


def conv_fwd(z, cw, cb, lg, lb, batch, seq):
    n = z.shape[0]
    nt = seq // CONV_ROWS
    sub = CONV_ROWS // CONV_HALO
    lead = CONV_HALO - (CONV_W - 1)

    def body(a_ref, g_ref, ap_ref, gp_ref, cw_ref, cb_ref, lg_ref, lb_ref, o_ref, y_ref, win, shifted):
        first = pl.program_id(1) == 0
        win[0:CONV_HALO, :] = jnp.where(first, 0.0, _glu(ap_ref, gp_ref))
        win[CONV_HALO:CONV_WIN, :] = _glu(a_ref, g_ref)
        _preshift(win, shifted)
        for r0, lanes in TAP_TILES:
            acc = jnp.zeros((TAP_ROWS, LANES), F32) + cb_ref[:, lanes]
            for w in range(CONV_W):
                acc = acc + _tap(win, shifted, lead + w, r0, lanes) * cw_ref[w:w + 1, lanes]
            y_ref[r0:r0 + TAP_ROWS, lanes] = acc
        yh, _ = _layer_norm_stats(y_ref[...])
        t = yh * lg_ref[...] + lb_ref[...]
        o_ref[...] = (t * _sigmoid(t)).astype(o_ref.dtype)

    def cur(c):
        return pl.BlockSpec((CONV_ROWS, TOK_WIDTH), lambda b, i: (b * nt + i, c))

    def prev(c):
        return pl.BlockSpec((CONV_HALO, TOK_WIDTH), lambda b, i: (jnp.maximum((b * nt + i) * sub - 1, 0), c))

    vec = pl.BlockSpec((1, TOK_WIDTH), lambda b, i: (0, 0))
    return pl.pallas_call(
        body, out_shape=(jax.ShapeDtypeStruct((n, D_MODEL), BF16), jax.ShapeDtypeStruct((n, TOK_WIDTH), F32)),
        grid=(batch, nt),
        in_specs=[cur(0), cur(1), prev(0), prev(1), pl.BlockSpec((32, TOK_WIDTH), lambda b, i: (0, 0)), vec, vec, vec],
        out_specs=(cur(0), cur(0)),
        scratch_shapes=[pltpu.VMEM((CONV_WIN, TOK_WIDTH), F32), pltpu.VMEM((SUBLANES - 1, SHIFT_ROWS, TOK_WIDTH), F32)],
        compiler_params=_params("parallel", "arbitrary"), name="conv_fwd")(z, z, z, z, cw, cb, lg, lb)


def conv_bwd(z, y, dcat, cw, lg, lb, batch, seq):
    n = z.shape[0]
    nt = seq // CONV_ROWS
    sub = CONV_ROWS // CONV_HALO
    lead = CONV_HALO - (CONV_W - 1)
    last_blk = n // CONV_HALO - 1

    def body(a_ref, g_ref, ap_ref, gp_ref, y_ref, yn_ref, do_ref, don_ref, cw_ref, lg_ref, lb_ref,
             dz_ref, dcw_ref, dsm_ref, win, shifted, dyw, dshifted, dg_o):
        b, i, which = pl.program_id(0), pl.program_id(1), pl.program_id(2)

        @pl.when(which == 0)
        def _():
            first, last = i == 0, i == nt - 1

            @pl.when((b == 0) & (i == 0))
            def _():
                dcw_ref[...] = jnp.zeros_like(dcw_ref)
                dsm_ref[...] = jnp.zeros_like(dsm_ref)

            win[0:CONV_HALO, :] = jnp.where(first, 0.0, _glu(ap_ref, gp_ref))
            win[CONV_HALO:CONV_WIN, :] = _glu(a_ref, g_ref)
            _preshift(win, shifted)
            yv = jnp.concatenate([y_ref[...], yn_ref[...]], axis=0)
            yh, rstd = _layer_norm_stats(yv)
            t = yh * lg_ref[...] + lb_ref[...]
            st = _sigmoid(t)
            dout = jnp.concatenate(
                [do_ref[...].astype(F32), jnp.where(last, 0.0, don_ref[...].astype(F32))], axis=0)
            dt = dout * st * (1.0 + t * (1.0 - st))
            dyh = dt * lg_ref[...]
            dy = rstd * (dyh - jnp.mean(dyh, axis=-1, keepdims=True)
                         - yh * jnp.mean(dyh * yh, axis=-1, keepdims=True))
            dyw[...] = dy
            _preshift(dyw, dshifted)
            dsm_ref[0:1, :] += jnp.sum(dy[0:CONV_ROWS], axis=0, keepdims=True)
            dsm_ref[1:2, :] += jnp.sum((dt * yh)[0:CONV_ROWS], axis=0, keepdims=True)
            dsm_ref[2:3, :] += jnp.sum(dt[0:CONV_ROWS], axis=0, keepdims=True)
            for c0 in range(0, TOK_WIDTH, LANES):
                lanes = slice(c0, c0 + LANES)
                dcw_acc = [jnp.zeros((SUBLANES, LANES), F32) for _ in range(CONV_W)]
                for r0 in range(0, CONV_ROWS, TAP_ROWS):
                    dyt = dyw[r0:r0 + TAP_ROWS, lanes]
                    dglu = jnp.zeros((TAP_ROWS, LANES), F32)
                    for w in range(CONV_W):
                        dcw_acc[w] = dcw_acc[w] + _fold_rows(dyt * _tap(win, shifted, lead + w, r0, lanes))
                        dglu = dglu + _tap(dyw, dshifted, CONV_W - 1 - w, r0, lanes) * cw_ref[w:w + 1, lanes]
                    avt = a_ref[r0:r0 + TAP_ROWS, lanes].astype(F32)
                    sgt = _sigmoid(g_ref[r0:r0 + TAP_ROWS, lanes].astype(F32))
                    dz_ref[r0:r0 + TAP_ROWS, lanes] = (dglu * sgt).astype(dz_ref.dtype)
                    dg_o[r0:r0 + TAP_ROWS, lanes] = (dglu * avt * sgt * (1.0 - sgt)).astype(dg_o.dtype)
                for w in range(CONV_W):
                    dcw_ref[w:w + 1, lanes] += jnp.sum(dcw_acc[w], axis=0, keepdims=True)

        @pl.when(which == 1)
        def _():
            dz_ref[...] = dg_o[...]

    def ahead(b, i, t):
        return jnp.minimum(b * nt + i + t, batch * nt - 1)

    def cur(c):
        return pl.BlockSpec((CONV_ROWS, TOK_WIDTH), lambda b, i, t: (ahead(b, i, t), c))

    def prev(c):
        return pl.BlockSpec((CONV_HALO, TOK_WIDTH), lambda b, i, t: (jnp.maximum(ahead(b, i, t) * sub - 1, 0), c))

    nxt = pl.BlockSpec((CONV_HALO, TOK_WIDTH),
                       lambda b, i, t: (jnp.minimum((ahead(b, i, t) + 1) * sub, last_blk), 0))
    vec = pl.BlockSpec((1, TOK_WIDTH), lambda b, i, t: (0, 0))
    full32 = pl.BlockSpec((32, TOK_WIDTH), lambda b, i, t: (0, 0))
    return pl.pallas_call(
        body,
        out_shape=(jax.ShapeDtypeStruct(z.shape, BF16), jax.ShapeDtypeStruct((32, TOK_WIDTH), F32),
                   jax.ShapeDtypeStruct((8, TOK_WIDTH), F32)),
        grid=(batch, nt, 2),
        in_specs=[cur(0), cur(1), prev(0), prev(1), cur(0), nxt, cur(0), nxt, full32, vec, vec],
        out_specs=(pl.BlockSpec((CONV_ROWS, TOK_WIDTH), lambda b, i, t: (b * nt + i, t)), full32,
                   pl.BlockSpec((8, TOK_WIDTH), lambda b, i, t: (0, 0))),
        scratch_shapes=[pltpu.VMEM((CONV_WIN, TOK_WIDTH), F32), pltpu.VMEM((SUBLANES - 1, SHIFT_ROWS, TOK_WIDTH), F32),
                        pltpu.VMEM((CONV_WIN, TOK_WIDTH), F32), pltpu.VMEM((SUBLANES - 1, SHIFT_ROWS, TOK_WIDTH), F32),
                        pltpu.VMEM((CONV_ROWS, TOK_WIDTH), BF16)],
        compiler_params=_params("arbitrary", "arbitrary", "arbitrary"), name="conv_bwd")(
            z, z, z, z, y, y, dcat, dcat, cw, lg, lb)


def _place():
    return lax.axis_index("x"), lax.axis_index("y"), lax.axis_index("c")


def _other_chips(x, y):
    return [(1 - x, y), (x, 1 - y), (1 - x, 1 - y)]


def reduce_small(arrays):
    na = len(arrays)

    def body(*refs):
        ins, outs, bufs = refs[:na], refs[na:2 * na], refs[2 * na:3 * na]
        send_sems, recv_sems = refs[3 * na:]
        x, y, c = _place()
        me = 4 * x + 2 * y + c
        copies = []
        for a in range(na):
            bufs[a][me] = ins[a][...]
            for k in range(1, N_DEV):
                cp = pltpu.make_async_remote_copy(
                    src_ref=ins[a], dst_ref=bufs[a].at[me], send_sem=send_sems.at[a, k - 1],
                    recv_sem=recv_sems.at[a, k - 1],
                    device_id=(x ^ (k >> 2), y ^ ((k >> 1) & 1), c ^ (k & 1)), device_id_type=MESH)
                cp.start()
                copies.append(cp)
        for a in range(na):
            for k in range(1, N_DEV):
                src = 4 * (x ^ (k >> 2)) + 2 * (y ^ ((k >> 1) & 1)) + (c ^ (k & 1))
                pltpu.make_async_remote_copy(
                    src_ref=ins[a], dst_ref=bufs[a].at[src], send_sem=send_sems.at[a, k - 1],
                    recv_sem=recv_sems.at[a, k - 1], device_id=(x, y, c), device_id_type=MESH).wait_recv()
        for cp in copies:
            cp.wait_send()
        for a in range(na):
            total = bufs[a][0]
            for dev in range(1, N_DEV):
                total = total + bufs[a][dev]
            outs[a][...] = total

    vmem = pl.BlockSpec(memory_space=pltpu.VMEM)
    return pl.pallas_call(
        body, out_shape=tuple(jax.ShapeDtypeStruct(a.shape, F32) for a in arrays),
        in_specs=[vmem] * na, out_specs=tuple([vmem] * na),
        scratch_shapes=[pltpu.VMEM((N_DEV,) + a.shape, F32) for a in arrays]
        + [pltpu.SemaphoreType.DMA((na, N_DEV - 1)), pltpu.SemaphoreType.DMA((na, N_DEV - 1))],
        compiler_params=pltpu.CompilerParams(vmem_limit_bytes=VMEM_LIMIT), name="small_reduce")(*arrays)


def adamw_small(ws, gs, ms, vs):
    na = len(ws)
    c1 = 1.0 / (1.0 - ADAM_B1 ** ADAM_STEP)
    c2 = 1.0 / (1.0 - ADAM_B2 ** ADAM_STEP)

    def body(*refs):
        w_refs, g_refs, m_refs, v_refs = (refs[i * na:(i + 1) * na] for i in range(4))
        d_refs, nm_refs, nv_refs = (refs[(4 + i) * na:(5 + i) * na] for i in range(3))
        for a in range(na):
            gv = g_refs[a][...]
            nm = ADAM_B1 * m_refs[a][...] + (1.0 - ADAM_B1) * gv
            nv = ADAM_B2 * v_refs[a][...] + (1.0 - ADAM_B2) * (gv * gv)
            nm_refs[a][...] = nm
            nv_refs[a][...] = nv
            d_refs[a][...] = -ADAM_LR * ((nm * c1) / (jnp.sqrt(nv * c2) + ADAM_EPS) + ADAM_WD * w_refs[a][...])

    vmem = pl.BlockSpec(memory_space=pltpu.VMEM)
    shapes = tuple(jax.ShapeDtypeStruct(w.shape, F32) for w in ws)
    outs = pl.pallas_call(
        body, out_shape=shapes * 3, in_specs=[vmem] * (4 * na), out_specs=tuple([vmem] * (3 * na)),
        compiler_params=pltpu.CompilerParams(vmem_limit_bytes=VMEM_LIMIT), name="adamw_small")(*ws, *gs, *ms, *vs)
    return outs[:na], outs[na:2 * na], outs[2 * na:]


def gather_weights(shards, name, collective_id):
    nw = len(shards)
    ns = [s.shape[0] for s in shards]
    in_refs = [jax.new_ref(s, memory_space=pltpu.MemorySpace.HBM) for s in shards]
    out_refs = [jax.empty_ref(jax.ShapeDtypeStruct((N_DEV * s.shape[0], s.shape[1]), s.dtype),
                              memory_space=pltpu.MemorySpace.HBM) for s in shards]

    @pl.kernel(mesh=plsc.ScalarSubcoreMesh(axis_name="seq", num_cores=1), name=name,
               scratch_types=(pltpu.SemaphoreType.DMA((nw, 7)), pltpu.SemaphoreType.DMA((nw, 7)),
                              pltpu.SemaphoreType.DMA((nw,))),
               compiler_params=pltpu.CompilerParams(collective_id=collective_id))
    def launch(send_sems, recv_sems, local_sems):
        x, y, c = _place()
        me, sib = (x, y, c), (x, y, 1 - c)
        chips = _other_chips(x, y)
        barrier = pltpu.get_barrier_semaphore()
        for peer in [sib] + [(*chip, c) for chip in chips]:
            pl.semaphore_signal(barrier, inc=1, device_id=peer, device_id_type=MESH)
        pl.semaphore_wait(barrier, 4)

        def rows(w, dev):
            return out_refs[w].at[pl.ds((4 * dev[0] + 2 * dev[1] + dev[2]) * ns[w], ns[w]), :]

        def copy(w, k, block, to, src=None):
            return pltpu.make_async_remote_copy(
                src_ref=rows(w, block) if src is None else src, dst_ref=rows(w, block),
                send_sem=send_sems.at[w, k], recv_sem=recv_sems.at[w, k], device_id=to, device_id_type=MESH)

        started, sends = [], []
        for w in range(nw):
            mine = pltpu.make_async_copy(in_refs[w], rows(w, me), local_sems.at[w])
            mine.start()
            started.append(mine)
            first = [copy(w, 0, me, sib, src=in_refs[w])]
            first += [copy(w, 1 + j, me, (*chip, c), src=in_refs[w]) for j, chip in enumerate(chips)]
            for cp in first:
                cp.start()
            sends += first
        for w in range(nw):
            for j, chip in enumerate(chips):
                copy(w, 1 + j, (*chip, c), me).wait_recv()
                fwd = copy(w, 4 + j, (*chip, c), sib)
                fwd.start()
                sends.append(fwd)
        for w in range(nw):
            copy(w, 0, sib, me).wait_recv()
            for j, chip in enumerate(chips):
                copy(w, 4 + j, (*chip, 1 - c), me).wait_recv()
        for cp in sends:
            cp.wait_send()
        for mine in started:
            mine.wait()

    launch()
    return [r[...] for r in out_refs]


def _sequencer_exchange(sources, out_rows, peers_of, copies_of, name, collective_id):
    nw = len(sources)
    in_refs = [jax.new_ref(s, memory_space=pltpu.MemorySpace.HBM) for s in sources]
    out_refs = [jax.empty_ref(jax.ShapeDtypeStruct((rows, s.shape[1]), s.dtype), memory_space=pltpu.MemorySpace.HBM)
                for rows, s in zip(out_rows, sources)]
    per = len(copies_of(0, 0, 0, 0))

    @pl.kernel(mesh=plsc.ScalarSubcoreMesh(axis_name="seq", num_cores=1), name=name,
               scratch_types=(pltpu.SemaphoreType.DMA((nw, per)), pltpu.SemaphoreType.DMA((nw, per))),
               compiler_params=pltpu.CompilerParams(collective_id=collective_id))
    def launch(send_sems, recv_sems):
        x, y, c = _place()
        peers = peers_of(x, y, c)
        barrier = pltpu.get_barrier_semaphore()
        for peer in peers:
            pl.semaphore_signal(barrier, inc=1, device_id=peer, device_id_type=MESH)
        pl.semaphore_wait(barrier, len(peers))
        copies = []
        for w in range(nw):
            for k, (src_blk, dst_blk, rows, peer) in enumerate(copies_of(x, y, c, w)):
                cp = pltpu.make_async_remote_copy(
                    src_ref=in_refs[w].at[pl.ds(src_blk * rows, rows), :],
                    dst_ref=out_refs[w].at[pl.ds(dst_blk * rows, rows), :],
                    send_sem=send_sems.at[w, k], recv_sem=recv_sems.at[w, k], device_id=peer, device_id_type=MESH)
                cp.start()
                copies.append(cp)
        for cp in copies:
            cp.wait_recv()
        for cp in copies:
            cp.wait_send()

    launch()
    return [r[...] for r in out_refs]


def scatter_to_sibling(grads, name, collective_id):
    ns = [g.shape[0] // N_DEV for g in grads]
    return _sequencer_exchange(
        grads, [4 * n for n in ns],
        lambda x, y, c: [(x, y, 1 - c)],
        lambda x, y, c, w: [(2 * q + 1 - c, q, ns[w], (x, y, 1 - c)) for q in range(4)],
        name, collective_id)


def scatter_to_chips(parts, name, collective_id):
    ns = [p.shape[0] // 4 for p in parts]
    return _sequencer_exchange(
        parts, [3 * n for n in ns],
        lambda x, y, c: [(*chip, c) for chip in _other_chips(x, y)],
        lambda x, y, c, w: [(2 * chip[0] + chip[1], j, ns[w], (*chip, c)) for j, chip in enumerate(_other_chips(x, y))],
        name, collective_id)


def add_sibling(grads, landeds, core, name):
    nw = len(grads)

    def body(c_ref, *refs):
        for w in range(nw):
            g_ref, l_ref, o_ref = refs[2 * w], refs[2 * w + 1], refs[2 * nw + w]
            o_ref[...] = (g_ref[...].astype(F32) + l_ref[...].astype(F32)).astype(o_ref.dtype)

    in_specs, out_specs, args = [], [], []
    for g, ld in zip(grads, landeds):
        n, cols = ld.shape[0] // 4, g.shape[1]
        in_specs += [pl.BlockSpec((n, cols), lambda q, c_ref: (2 * q + c_ref[0], 0)),
                     pl.BlockSpec((n, cols), lambda q, c_ref: (q, 0))]
        out_specs.append(pl.BlockSpec((n, cols), lambda q, c_ref: (q, 0)))
        args += [g, ld]
    grid_spec = pltpu.PrefetchScalarGridSpec(
        num_scalar_prefetch=1, grid=(4,), in_specs=in_specs, out_specs=tuple(out_specs))
    return pl.pallas_call(
        body, out_shape=tuple(jax.ShapeDtypeStruct(ld.shape, ld.dtype) for ld in landeds), grid_spec=grid_spec,
        compiler_params=_params("arbitrary"), name=name)(core, *args)


ADAMW_HALVES = 2


def adamw_shards(items, chip, name):
    c1 = 1.0 / (1.0 - ADAM_B1 ** ADAM_STEP)
    c2 = 1.0 / (1.0 - ADAM_B2 ** ADAM_STEP)
    ni = len(items)

    def body(q_ref, *refs):
        outs = refs[len(refs) - 4 * ni:]
        for k in range(ni):
            w_ref, m_ref, v_ref, p_ref, l0_ref, l1_ref, l2_ref = refs[7 * k:7 * k + 7]
            g_ref, d_ref, nm_ref, nv_ref = outs[4 * k:4 * k + 4]
            gv = ((p_ref[...].astype(F32) + l0_ref[...].astype(F32)) + l1_ref[...].astype(F32)) + l2_ref[...].astype(F32)
            nm = ADAM_B1 * m_ref[...] + (1.0 - ADAM_B1) * gv
            nv = ADAM_B2 * v_ref[...] + (1.0 - ADAM_B2) * (gv * gv)
            g_ref[...] = gv
            nm_ref[...] = nm
            nv_ref[...] = nv
            d_ref[...] = -ADAM_LR * ((nm * c1) / (jnp.sqrt(nv * c2) + ADAM_EPS) + ADAM_WD * w_ref[...])

    sub = ADAMW_HALVES
    in_specs, out_specs, out_shape, args, donated = [], [], [], [chip], []
    for layer, w, m, v, part, landed, earlier in items:
        rows, cols = landed.shape[0] // (3 * sub), w.shape[1]

        def block(first, rows=rows, cols=cols):
            return pl.BlockSpec((rows, cols), lambda i, q_ref: (first(q_ref) * sub + i, 0))

        own = block(lambda q_ref, layer=layer: layer)
        in_specs += [own, own, own, block(lambda q_ref: q_ref[0])] + [block(lambda q_ref, j=j: j) for j in range(3)]
        args += [w, m, v, part, landed, landed, landed]
        out_specs += [own] * 4
        out_shape += [jax.ShapeDtypeStruct(w.shape, F32)] * 4
        donated.append(earlier)
    aliases = {}
    for k, earlier in enumerate(donated):
        if earlier is not None:
            for j in range(4):
                aliases[len(args)] = 4 * k + j
                in_specs.append(ANY)
                args.append(earlier[j])
    grid_spec = pltpu.PrefetchScalarGridSpec(
        num_scalar_prefetch=1, grid=(sub,), in_specs=in_specs, out_specs=tuple(out_specs))
    outs = pl.pallas_call(
        body, out_shape=tuple(out_shape), grid_spec=grid_spec, input_output_aliases=aliases,
        compiler_params=_params("arbitrary"), name=name)(*args)
    return [tuple(outs[4 * k:4 * k + 4]) for k in range(ni)]


def _pack(arrays):
    flat = jnp.concatenate([a.reshape(-1).astype(F32) for a in arrays])
    pad = (-flat.shape[0]) % (8 * LANES)
    return jnp.pad(flat, (0, pad)).reshape(-1, LANES)


def _unpack(slab, shapes):
    flat = slab.reshape(slab.shape[:-2] + (-1,))
    out, off = [], 0
    for shp in shapes:
        size = 1
        for s in shp:
            size *= s
        out.append(flat[..., off:off + size].reshape(flat.shape[:-1] + tuple(shp)))
        off += size
    return out


def kernel(x, mem, norm1_g, mem_norm_g, a_w_in, a_q_g, a_k_g, a_rel_bias, b_w_in, b_b_in, b_conv_w, b_conv_b, b_ln_g, b_ln_b, mq_g, mk_g, w_mem_kv, w_out, norm2_g, w_gate, w_up, w_down, loss_target, m_norm1_g, m_mem_norm_g, m_a_w_in, m_a_q_g, m_a_k_g, m_a_rel_bias, m_b_w_in, m_b_b_in, m_b_conv_w, m_b_conv_b, m_b_ln_g, m_b_ln_b, m_mq_g, m_mk_g, m_w_mem_kv, m_w_out, m_norm2_g, m_w_gate, m_w_up, m_w_down, v_norm1_g, v_mem_norm_g, v_a_w_in, v_a_q_g, v_a_k_g, v_a_rel_bias, v_b_w_in, v_b_b_in, v_b_conv_w, v_b_conv_b, v_b_ln_g, v_b_ln_b, v_mq_g, v_mk_g, v_w_mem_kv, v_w_out, v_norm2_g, v_w_gate, v_w_up, v_w_down):
    batch, seq, d = x.shape
    mtok = mem.shape[1]
    n = batch * seq
    ax, ay, ac = _place()
    me = 4 * ax + 2 * ay + ac
    core_arr = jnp.reshape(ac, (1,)).astype(jnp.int32)
    chip_arr = jnp.reshape(2 * ax + ay, (1,)).astype(jnp.int32)

    def t_bf16(w):
        return jnp.transpose(w).astype(BF16)

    def after(value, *earlier):
        return lax.optimization_barrier((value, *earlier))[0]

    def gather_mix(l, when, name, collective_id):
        srcs = [w_mem_kv[l].astype(BF16), w_out[l].astype(BF16)]
        if l == 1:
            srcs += [t_bf16(b_w_in[0]), _pack([b_b_in, b_conv_w, b_conv_b, b_ln_g, b_ln_b])]
        return gather_weights([after(srcs[0], *when)] + srcs[1:], name, collective_id)

    def gather_ffn(l, when, name, collective_id):
        return gather_weights(
            [after(t_bf16(w_gate[l]), *when), t_bf16(w_up[l]), w_down[l].astype(BF16)], name, collective_id)

    f_loc = b_b_in.shape[1]
    c_loc = b_conv_b.shape[1]

    def two(g):
        return jnp.concatenate([g, g], axis=-1)

    gq2, gk2 = two(a_q_g), two(a_k_g)
    rel16 = jnp.pad(a_rel_bias[0], ((0, 16 - a_rel_bias.shape[1]), (0, 0)))
    bias = bias_blocks(rel16)

    x0 = x.reshape(n, d)
    mem2 = mem.reshape(batch * mtok, d)

    saved = []
    xin = x0
    a_win_t, = gather_weights([t_bf16(a_w_in[0])], "gather_in_a", 1)
    wg_t, wu_t, wd, wo, wkv = [None] * 2, [None] * 2, [None] * 2, [None] * 2, [None] * 2
    h = after(rms_fwd(xin, norm1_g[0:1], name="rms1_fwd_0"), bias)
    target = loss_target.reshape(n, d)
    for l in range(2):
        gq4 = jnp.tile(mq_g[l:l + 1], (1, 4))
        gk4 = jnp.tile(mk_g[l:l + 1], (1, 4))
        y_conv = None
        if l == 0:
            wkv[0], wo[0] = gather_mix(0, (h, a_win_t), "gather_mix_a", 2)
            z = mm_nt(h, a_win_t, name="in_proj_a")
            wg_t[0], wu_t[0], wd[0] = gather_ffn(0, (z, wkv[0]), "gather_ffn_a", 3)
            cat = attn_fwd(z, gq2, gk2, bias, batch, seq)
            wkv[1], wo[1], b_win_t, conv_slabs = gather_mix(1, (cat, wg_t[0]), "gather_mix_b", 4)
            qcol = 3 * TOK_WIDTH // MEM_WIDTH
        else:
            small_shapes = [(f_loc,), (CONV_W, c_loc), (c_loc,), (c_loc,), (c_loc,)]
            bb_g, cw_g, cb_g, lg_g, lb_g = _unpack(conv_slabs.reshape(N_DEV, -1, LANES), small_shapes)
            bb_full = bb_g.reshape(1, -1)
            cw_full = jnp.pad(jnp.transpose(cw_g, (1, 0, 2)).reshape(CONV_W, -1), ((0, 32 - CONV_W), (0, 0)))
            cb_full, lg_full, lb_full = cb_g.reshape(1, -1), lg_g.reshape(1, -1), lb_g.reshape(1, -1)
            z = mm_nt(h, b_win_t, bias=bb_full, name="in_proj_b")
            cat, y_conv = conv_fwd(z, cw_full, cb_full, lg_full, lb_full, batch, seq)
            qcol = 2 * TOK_WIDTH // MEM_WIDTH
        cat, mem_n, kv = memattn_fwd(
            z, mem2, mem_norm_g[l:l + 1], wkv[l], gq4, gk4, cat, batch, seq, qcol, name=f"memattn_fwd_{l}")
        x1, h2 = proj_norm(cat, wo[l], xin, norm2_g[l:l + 1], name=f"out_proj_{l}")
        if l == 0:
            wg_t[1], wu_t[1], wd[1] = gather_ffn(1, (x1, b_win_t), "gather_ffn_b", 5)
        if l == 0:
            gate, up, act, x2, h_next = ffn_fwd(h2, wg_t[0], wu_t[0], wd[0], x1, gain=norm1_g[1:2], name="ffn_fwd_0")
        else:
            gate, up, act, dx_b, loss_blk = ffn_fwd(h2, wg_t[1], wu_t[1], wd[1], x1, target=target, name="ffn_fwd_1")
        saved.append(dict(xin=xin, h=h, mem_n=mem_n, kv=kv, gq4=gq4, gk4=gk4, z=z, qcol=qcol, cat=cat, x1=x1, h2=h2,
                          gate=gate, up=up, act=act, y_conv=y_conv))
        if l == 0:
            xin, h = x2, h_next

    big = {}
    small = {}
    reduced = {}
    groups = 0

    def scatter_siblings(keys):
        nonlocal groups
        gid = groups
        groups += 1
        return gid, keys, scatter_to_sibling([big[k] for k in keys], f"scatter_sibling_{gid}", 8 + 2 * gid)

    def scatter_chips(stage1, when):
        gid, keys, landed1 = stage1
        parts = add_sibling([after(big[keys[0]], when)] + [big[k] for k in keys[1:]], landed1, core_arr,
                            name=f"add_sibling_{gid}")
        landed2 = scatter_to_chips(parts, f"scatter_chips_{gid}", 9 + 2 * gid)
        for k, p, ld in zip(keys, parts, landed2):
            reduced[k] = (p, ld)
        return parts, landed2

    def rows_of(w, transposed):
        w = jnp.swapaxes(w, 1, 2) if transposed else w
        return w.reshape(w.shape[0] * w.shape[1], w.shape[2])

    sharded = {
        "win0": (2, True), "win1": (6, True), "wkv": (14, False), "wo": (15, False),
        "wg": (17, True), "wu": (18, True), "wd": (19, False)}
    weights = [norm1_g, mem_norm_g, a_w_in, a_q_g, a_k_g, a_rel_bias, b_w_in, b_b_in, b_conv_w, b_conv_b, b_ln_g,
               b_ln_b, mq_g, mk_g, w_mem_kv, w_out, norm2_g, w_gate, w_up, w_down]
    moms = [m_norm1_g, m_mem_norm_g, m_a_w_in, m_a_q_g, m_a_k_g, m_a_rel_bias, m_b_w_in, m_b_b_in, m_b_conv_w,
            m_b_conv_b, m_b_ln_g, m_b_ln_b, m_mq_g, m_mk_g, m_w_mem_kv, m_w_out, m_norm2_g, m_w_gate, m_w_up, m_w_down]
    vels = [v_norm1_g, v_mem_norm_g, v_a_w_in, v_a_q_g, v_a_k_g, v_a_rel_bias, v_b_w_in, v_b_b_in, v_b_conv_w,
            v_b_conv_b, v_b_ln_g, v_b_ln_b, v_mq_g, v_mk_g, v_w_mem_kv, v_w_out, v_norm2_g, v_w_gate, v_w_up, v_w_down]
    updated = {}

    def update_layer(l, when):
        for group, keys in (("ffn", ("wg", "wu", "wd")), ("mix", (f"win{l}", "wkv", "wo"))):
            items = []
            for key in keys:
                idx, transposed = sharded[key]
                layer, rkey = (0, key) if key.startswith("win") else (l, f"{key}{l}")
                part, landed = reduced[rkey]
                w_rows = rows_of(weights[idx], transposed)
                items.append((layer, after(w_rows, when) if not items else w_rows, rows_of(moms[idx], transposed),
                              rows_of(vels[idx], transposed), part, landed, updated.get(key)))
            for key, result in zip(keys, adamw_shards(items, chip_arr, name=f"adamw_{group}_{l}")):
                updated[key] = result

    mix_landed = None
    for l in (1, 0):
        sv = saved[l]
        dgate, dup, dx1_b, dcat, small[f"norm2_{l}"] = ffn_bwd(
            dx_b, wd[l], sv["gate"], sv["up"], wg_t[l], wu_t[l], sv["x1"], norm2_g[l:l + 1], wo[l], name=f"ffn_bwd_{l}")
        if l == 0:
            dgate = after(dgate, *mix_landed)
            update_layer(1, dx1_b)
        big[f"wg{l}"], big[f"wu{l}"], big[f"wd{l}"] = ffn_weight_grads(
            dgate, dup, sv["h2"], sv["act"], dx_b, name=f"grad_ffn_{l}")
        stage1 = scatter_siblings([f"wd{l}", f"wg{l}", f"wu{l}"])
        big[f"wo{l}"] = mm_tn(sv["cat"], dx1_b, name=f"grad_wo_{l}")
        parts, ffn_landed = scatter_chips(stage1, big[f"wo{l}"])
        dcat = after(dcat, *parts)
        if l == 0:
            dq, dk, dv, dbias, small["a_q"], small["a_k"] = attn_bwd(sv["z"], dcat, gq2, gk2, bias, batch, seq)
            small["rel"] = bias_grad(dbias)
            win_t = a_win_t
            dz = None
            dcat = after(dcat, dq, *ffn_landed)
        else:
            dz, small["cw"], small["csum"] = conv_bwd(sv["z"], sv["y_conv"], dcat, cw_full, lg_full, lb_full, batch, seq)
            win_t = b_win_t
            dz = after(dz, *ffn_landed)
        dqm, small[f"mq_{l}"], small[f"mk_{l}"], big[f"wkv{l}"], small[f"memnorm_{l}"] = memattn_bwd(
            sv["z"], sv["kv"], dcat, sv["gq4"], sv["gk4"], mem2, sv["mem_n"], wkv[l], dz, batch, seq, sv["qcol"],
            name=f"memattn_bwd_{l}")
        if l == 0:
            pieces = [dq, dk, dv, dqm]
            big["win0"] = grad_pieces(pieces, sv["h"], name="grad_win_0")
        else:
            pieces = [dqm]
            big["win1"] = mm_tn(dqm, sv["h"], name="grad_win_1")
        stage1 = scatter_siblings([f"win{l}", f"wkv{l}", f"wo{l}"])
        dx_b, small[f"norm1_{l}"], dz_sum = in_proj_bwd(
            pieces, win_t, sv["xin"], norm1_g[l:l + 1], dx1_b, BF16 if l == 1 else F32, name=f"in_proj_bwd_{l}")
        if l == 1:
            small["bb"] = dz_sum
        parts, mix_landed = scatter_chips(stage1, dx_b)
        dx_b = after(dx_b, *parts)
    grad_x = dx_b.reshape(batch, seq, d)
    update_layer(0, dx_b)

    def shaped(rows, idx, transposed):
        shp = weights[idx].shape
        if transposed:
            return jnp.swapaxes(rows.reshape(shp[0], shp[2], shp[1]), 1, 2)
        return rows.reshape(shp)

    def fold(v, groups):
        return jnp.sum(v.reshape(groups, HEAD_DIM), axis=0, keepdims=True)

    heads = a_rel_bias.shape[1]
    small_list = [
        jnp.concatenate([small["norm1_0"], small["norm1_1"]]),
        jnp.concatenate([small["memnorm_0"], small["memnorm_1"]]),
        fold(small["a_q"], 2), fold(small["a_k"], 2), small["rel"][:heads],
        small["bb"], small["cw"][:CONV_W], small["csum"][0:1], small["csum"][1:2], small["csum"][2:3],
        jnp.concatenate([fold(small["mq_0"], 4), fold(small["mq_1"], 4)]),
        jnp.concatenate([fold(small["mk_0"], 4), fold(small["mk_1"], 4)]),
        jnp.concatenate([small["norm2_0"], small["norm2_1"]]),
    ]
    (g_norm1, g_memnorm, g_aq, g_ak, g_rel, g_bb_full, g_cw_full, g_cb_full, g_lg_full, g_lb_full,
     g_mq, g_mk, g_norm2, loss_sum) = reduce_small(small_list + [loss_blk])
    loss = loss_sum[0, 0]
    g_bb = lax.dynamic_slice_in_dim(g_bb_full, me * f_loc, f_loc, axis=1)
    g_cw = lax.dynamic_slice_in_dim(g_cw_full, me * c_loc, c_loc, axis=1)
    g_cb = lax.dynamic_slice_in_dim(g_cb_full, me * c_loc, c_loc, axis=1)
    g_lg = lax.dynamic_slice_in_dim(g_lg_full, me * c_loc, c_loc, axis=1)
    g_lb = lax.dynamic_slice_in_dim(g_lb_full, me * c_loc, c_loc, axis=1)

    grads = [g_norm1, g_memnorm, None, g_aq, g_ak, g_rel, None, g_bb, g_cw, g_cb, g_lg, g_lb,
             g_mq, g_mk, None, None, g_norm2, None, None, None]
    deltas, new_m, new_v = [None] * 20, [None] * 20, [None] * 20
    for key, (idx, transposed) in sharded.items():
        grads[idx], deltas[idx], new_m[idx], new_v[idx] = (shaped(r, idx, transposed) for r in updated[key])

    def flat2(a):
        return a.reshape(a.shape[-2:])

    small_idx = [i for i in range(20) if i not in {idx for idx, _ in sharded.values()}]
    dl, nm, nv = adamw_small([flat2(weights[i]) for i in small_idx], [flat2(grads[i]) for i in small_idx],
                             [flat2(moms[i]) for i in small_idx], [flat2(vels[i]) for i in small_idx])
    for i, a, b, cc in zip(small_idx, dl, nm, nv):
        shp = weights[i].shape
        grads[i], deltas[i], new_m[i], new_v[i] = grads[i].reshape(shp), a.reshape(shp), b.reshape(shp), cc.reshape(shp)

    return (loss, grad_x, *grads, *deltas, *new_m, *new_v)
```

```python
import jax
import jax.numpy as jnp
from jax import lax
from jax.experimental import pallas as pl
from jax.experimental.pallas import tpu as pltpu
from jax.experimental.pallas import tpu_sc as plsc

F32 = jnp.float32
BF16 = jnp.bfloat16
HIGHEST = lax.Precision.HIGHEST
MESH = pl.DeviceIdType.MESH
ANY = pl.BlockSpec(memory_space=pl.ANY)

N_DEV = 8
D_MODEL = 1024
HEAD_DIM = 64
TOK_WIDTH = 768
MEM_WIDTH = 256
CHUNK = 64
Q_BLOCK = 256
KEY_WIN = 768
BAND = 576
N_REL = 192
CONV_W = 31
CONV_HALO = 32
NORM_EPS = 1e-6
NEG_INF = -1e30
ATTN_SCALE = HEAD_DIM ** -0.5
LANES = 128
ROW_TILE = 512
VMEM_LIMIT = 56 * 1024 * 1024

ADAM_LR, ADAM_B1, ADAM_B2, ADAM_EPS, ADAM_WD, ADAM_STEP = 0.001, 0.9, 0.999, 1e-08, 0.01, 10


def _params(*sem):
    return pltpu.CompilerParams(dimension_semantics=sem, vmem_limit_bytes=VMEM_LIMIT)


WIDE_ROW_TILE = 1024


def _row_tile(m, rows=ROW_TILE):
    return rows if m % rows == 0 else m


def _col_tile(n, cap=1408):
    best = None
    for t in range(LANES, min(n, cap) + 1, LANES):
        if n % t == 0:
            best = t
    return best if best is not None else n


def _dot(a, b, ca, cb):
    return lax.dot_general(a, b, (((ca,), (cb,)), ((), ())), preferred_element_type=F32)


def _sigmoid(x):
    return 0.5 * jnp.tanh(0.5 * x) + 0.5


def mm_nt(a, b, bias=None, out_dtype=BF16, name="mm_nt"):
    m, k = a.shape
    n = b.shape[0]
    tm, tn = _row_tile(m, WIDE_ROW_TILE), _col_tile(n)

    def body(*refs):
        a_ref, b_ref = refs[0], refs[1]
        o_ref = refs[-1]
        acc = _dot(a_ref[...].astype(BF16), b_ref[...].astype(BF16), 1, 1)
        if bias is not None:
            acc = acc + refs[2][...]
        o_ref[...] = acc.astype(o_ref.dtype)

    in_specs = [pl.BlockSpec((tm, k), lambda j, i: (i, 0)), pl.BlockSpec((tn, k), lambda j, i: (j, 0))]
    args = [a, b]
    if bias is not None:
        in_specs.append(pl.BlockSpec((1, tn), lambda j, i: (0, j)))
        args.append(bias)
    return pl.pallas_call(
        body, out_shape=jax.ShapeDtypeStruct((m, n), out_dtype), grid=(n // tn, m // tm),
        in_specs=in_specs, out_specs=pl.BlockSpec((tm, tn), lambda j, i: (i, j)),
        compiler_params=_params("parallel", "arbitrary"), name=name)(*args)


def mm_tn(a, b, out_dtype=BF16, name="mm_tn"):
    t, r = a.shape
    c = b.shape[1]
    tr = _col_tile(r, 512)

    def body(a_ref, b_ref, o_ref):
        o_ref[...] = _dot(a_ref[...].astype(BF16), b_ref[...].astype(BF16), 0, 0).astype(o_ref.dtype)

    return pl.pallas_call(
        body, out_shape=jax.ShapeDtypeStruct((r, c), out_dtype), grid=(r // tr,),
        in_specs=[pl.BlockSpec((t, tr), lambda i: (0, i)), pl.BlockSpec((t, c), lambda i: (0, 0))],
        out_specs=pl.BlockSpec((tr, c), lambda i: (i, 0)),
        compiler_params=_params("parallel"), name=name)(a, b)


def _resident(shape):
    return pl.BlockSpec(shape, lambda i: (0, 0), pipeline_mode=pl.Buffered(1))


def proj_norm(a, b, res, gain, name):
    m, k = a.shape
    n = b.shape[1]
    tm = _row_tile(m)

    def body(a_ref, b_ref, res_ref, g_ref, x_ref, h_ref):
        xv = res_ref[...] + _dot(a_ref[...], b_ref[...], 1, 0)
        x_ref[...] = xv
        r = lax.rsqrt(jnp.mean(xv * xv, axis=-1, keepdims=True) + NORM_EPS)
        h_ref[...] = (xv * r * g_ref[...]).astype(BF16)

    row = pl.BlockSpec((tm, n), lambda i: (i, 0))
    return pl.pallas_call(
        body, out_shape=(jax.ShapeDtypeStruct((m, n), F32), jax.ShapeDtypeStruct((m, n), BF16)), grid=(m // tm,),
        in_specs=[pl.BlockSpec((tm, k), lambda i: (i, 0)), _resident((k, n)), row, _resident((1, n))],
        out_specs=(row, row), compiler_params=_params("parallel"), name=name)(a, b, res, gain)


def in_proj_bwd(pieces, w_t, x, gain, dres, out_dtype, name):
    m, n = x.shape
    k = pieces[0].shape[1]
    tm = _row_tile(m)
    npc = len(pieces)
    offs = [sum(p.shape[1] for p in pieces[:i]) for i in range(npc + 1)]

    def body(*refs):
        dz_refs = refs[:npc]
        w_ref, x_ref, g_ref, dres_ref, dx_ref, dg_ref, cs_ref = refs[npc:]

        @pl.when(pl.program_id(0) == 0)
        def _():
            dg_ref[...] = jnp.zeros_like(dg_ref)
            cs_ref[...] = jnp.zeros_like(cs_ref)

        cs_ref[...] += jnp.sum(dz_refs[0][...].astype(F32), axis=0, keepdims=True)
        dhv = _dot(dz_refs[0][...], w_ref[offs[0]:offs[1], :], 1, 0)
        for i in range(1, npc):
            dhv = dhv + _dot(dz_refs[i][...], w_ref[offs[i]:offs[i + 1], :], 1, 0)
        xv = x_ref[...]
        r = lax.rsqrt(jnp.mean(xv * xv, axis=-1, keepdims=True) + NORM_EPS)
        xhat = xv * r
        dg_ref[...] += jnp.sum(dhv * xhat, axis=0, keepdims=True)
        dxhat = dhv * g_ref[...]
        dx = dres_ref[...].astype(F32) + r * (dxhat - xhat * jnp.mean(dxhat * xhat, axis=-1, keepdims=True))
        dx_ref[...] = dx.astype(dx_ref.dtype)

    row = pl.BlockSpec((tm, n), lambda i: (i, 0))
    return pl.pallas_call(
        body, out_shape=(jax.ShapeDtypeStruct((m, n), out_dtype), jax.ShapeDtypeStruct((1, n), F32),
                         jax.ShapeDtypeStruct((1, k), F32)), grid=(m // tm,),
        in_specs=[pl.BlockSpec((tm, p.shape[1]), lambda i: (i, 0)) for p in pieces]
        + [_resident(w_t.shape), row, _resident((1, n)), row],
        out_specs=(row, pl.BlockSpec((1, n), lambda i: (0, 0)), pl.BlockSpec((1, k), lambda i: (0, 0))),
        compiler_params=_params("arbitrary"), name=name)(*pieces, w_t, x, gain, dres)


def grad_pieces(pieces, b, name):
    t, c = b.shape
    tr = 2 * LANES
    tiles = [p.shape[1] // tr for p in pieces]
    starts = [sum(tiles[:i]) for i in range(len(pieces) + 1)]

    def body(*refs):
        a_refs, b_ref, o_ref = refs[:len(pieces)], refs[len(pieces)], refs[len(pieces) + 1]
        i = pl.program_id(0)
        for p, a_ref in enumerate(a_refs):
            @pl.when((i >= starts[p]) & (i < starts[p + 1]))
            def _(a_ref=a_ref):
                o_ref[...] = _dot(a_ref[...], b_ref[...], 0, 0).astype(o_ref.dtype)

    def a_spec(p):
        return pl.BlockSpec((t, tr), lambda i: (0, jnp.clip(i - starts[p], 0, tiles[p] - 1)))

    return pl.pallas_call(
        body, out_shape=jax.ShapeDtypeStruct((starts[-1] * tr, c), BF16), grid=(starts[-1],),
        in_specs=[a_spec(p) for p in range(len(pieces))] + [_resident((t, c))],
        out_specs=pl.BlockSpec((tr, c), lambda i: (i, 0)),
        compiler_params=_params("arbitrary"), name=name)(*pieces, b)


FFN_ROWS = 256


def _ffn_row_tile(m):
    return FFN_ROWS if m % FFN_ROWS == 0 else m


def ffn_fwd(h2, wg_t, wu_t, wd, x1, gain=None, target=None, name="ffn_fwd"):
    n, d = h2.shape
    f = wg_t.shape[0]
    tm = _ffn_row_tile(n)
    nt = n // tm
    last = target is not None

    def body(h_ref, wg_ref, wu_ref, wd_ref, x1_ref, e_ref, g_ref, u_ref, a_ref, *rest):
        hv = h_ref[...]
        gv = _dot(hv, wg_ref[...], 1, 1)
        uv = _dot(hv, wu_ref[...], 1, 1)
        g_ref[...] = gv.astype(BF16)
        u_ref[...] = uv.astype(BF16)
        av = (gv * _sigmoid(gv) * uv).astype(BF16)
        a_ref[...] = av
        xv = x1_ref[...] + _dot(av, wd_ref[...], 1, 0)
        if not last:
            x_ref, hn_ref = rest
            x_ref[...] = xv
            r = lax.rsqrt(jnp.mean(xv * xv, axis=-1, keepdims=True) + NORM_EPS)
            hn_ref[...] = (xv * r * e_ref[...]).astype(BF16)
        else:
            dyb_ref, l_ref, acc_ref = rest
            i = pl.program_id(0)

            @pl.when(i == 0)
            def _():
                acc_ref[...] = jnp.zeros_like(acc_ref)

            err = xv - e_ref[...]
            dyb_ref[...] = (err * (1.0 / d)).astype(BF16)
            acc_ref[...] += jnp.sum(err * err, axis=0, keepdims=True)

            @pl.when(i == nt - 1)
            def _():
                total = jnp.sum(acc_ref[...], axis=-1, keepdims=True) * (0.5 / d)
                l_ref[...] = jnp.broadcast_to(total, l_ref.shape)

    row_d = pl.BlockSpec((tm, d), lambda i: (i, 0))
    row_f = pl.BlockSpec((tm, f), lambda i: (i, 0))
    act_shape = jax.ShapeDtypeStruct((n, f), BF16)
    if not last:
        extra_in, extra = _resident((1, d)), gain
        out_shape = (act_shape, act_shape, act_shape, jax.ShapeDtypeStruct((n, d), F32), jax.ShapeDtypeStruct((n, d), BF16))
        out_specs = (row_f, row_f, row_f, row_d, row_d)
        scratch = []
    else:
        extra_in, extra = row_d, target
        out_shape = (act_shape, act_shape, act_shape, jax.ShapeDtypeStruct((n, d), BF16),
                     jax.ShapeDtypeStruct((8, LANES), F32))
        out_specs = (row_f, row_f, row_f, row_d, pl.BlockSpec((8, LANES), lambda i: (0, 0)))
        scratch = [pltpu.VMEM((1, d), F32)]
    return pl.pallas_call(
        body, out_shape=out_shape, grid=(nt,),
        in_specs=[row_d, _resident((f, d)), _resident((f, d)), _resident((f, d)), row_d, extra_in],
        out_specs=out_specs, scratch_shapes=scratch,
        compiler_params=_params("arbitrary"), name=name)(h2, wg_t, wu_t, wd, x1, extra)


def ffn_bwd(dx_b, wd, gate, up, wg_t, wu_t, x1, gain, wo, name="ffn_bwd"):
    n, d = x1.shape
    f = wd.shape[0]
    tm = _ffn_row_tile(n)

    def body(dxb_ref, wd_ref, g_ref, u_ref, wg_ref, wu_ref, x_ref, gain_ref, wo_ref,
             dg_ref, du_ref, dxo_ref, dc_ref, dgain_ref):
        @pl.when(pl.program_id(0) == 0)
        def _():
            dgain_ref[...] = jnp.zeros_like(dgain_ref)

        dact = _dot(dxb_ref[...], wd_ref[...], 1, 1)
        gv = g_ref[...].astype(F32)
        uv = u_ref[...].astype(F32)
        sg = _sigmoid(gv)
        dgv = (dact * uv * sg * (1.0 + gv * (1.0 - sg))).astype(BF16)
        duv = (dact * gv * sg).astype(BF16)
        dg_ref[...] = dgv
        du_ref[...] = duv
        dhv = _dot(dgv, wg_ref[...], 1, 0) + _dot(duv, wu_ref[...], 1, 0)
        xv = x_ref[...]
        r = lax.rsqrt(jnp.mean(xv * xv, axis=-1, keepdims=True) + NORM_EPS)
        xhat = xv * r
        dgain_ref[...] += jnp.sum(dhv * xhat, axis=0, keepdims=True)
        dxhat = dhv * gain_ref[...]
        dxb = (dxb_ref[...].astype(F32) + r * (dxhat - xhat * jnp.mean(dxhat * xhat, axis=-1, keepdims=True))).astype(BF16)
        dxo_ref[...] = dxb
        dc_ref[...] = _dot(dxb, wo_ref[...], 1, 1).astype(BF16)

    row_d = pl.BlockSpec((tm, d), lambda i: (i, 0))
    row_f = pl.BlockSpec((tm, f), lambda i: (i, 0))
    w_spec = _resident((f, d))
    act_shape = jax.ShapeDtypeStruct((n, f), BF16)
    row_shape = jax.ShapeDtypeStruct((n, d), BF16)
    return pl.pallas_call(
        body, out_shape=(act_shape, act_shape, row_shape, jax.ShapeDtypeStruct((n, wo.shape[0]), BF16),
                         jax.ShapeDtypeStruct((1, d), F32)),
        grid=(n // tm,),
        in_specs=[row_d, w_spec, row_f, row_f, w_spec, w_spec, row_d, _resident((1, d)), _resident(wo.shape)],
        out_specs=(row_f, row_f, row_d, pl.BlockSpec((tm, wo.shape[0]), lambda i: (i, 0)),
                   pl.BlockSpec((1, d), lambda i: (0, 0))),
        compiler_params=_params("arbitrary"), name=name)(dx_b, wd, gate, up, wg_t, wu_t, x1, gain, wo)


def ffn_weight_grads(dgate, dup, h2, act, dx_b, name="ffn_weight_grads"):
    t, r = dgate.shape
    c = h2.shape[1]
    tr = _col_tile(r, 512)

    def body(a1_ref, a2_ref, a3_ref, b12_ref, b3_ref, o1_ref, o2_ref, o3_ref):
        bv = b12_ref[...]
        o1_ref[...] = _dot(a1_ref[...], bv, 0, 0).astype(o1_ref.dtype)
        o2_ref[...] = _dot(a2_ref[...], bv, 0, 0).astype(o2_ref.dtype)
        o3_ref[...] = _dot(a3_ref[...], b3_ref[...], 0, 0).astype(o3_ref.dtype)

    a_spec = pl.BlockSpec((t, tr), lambda i: (0, i))
    o_spec = pl.BlockSpec((tr, c), lambda i: (i, 0))
    shape = jax.ShapeDtypeStruct((r, c), BF16)
    return pl.pallas_call(
        body, out_shape=(shape, shape, shape), grid=(r // tr,),
        in_specs=[a_spec, a_spec, a_spec, _resident((t, c)), _resident((t, c))],
        out_specs=(o_spec, o_spec, o_spec), compiler_params=_params("parallel"), name=name)(dgate, dup, act, h2, dx_b)


def rms_fwd(x, g, name="rms_fwd"):
    n, d = x.shape
    tm = _row_tile(n)

    def body(x_ref, g_ref, o_ref):
        xv = x_ref[...]
        r = lax.rsqrt(jnp.mean(xv * xv, axis=-1, keepdims=True) + NORM_EPS)
        o_ref[...] = (xv * r * g_ref[...]).astype(o_ref.dtype)

    return pl.pallas_call(
        body, out_shape=jax.ShapeDtypeStruct((n, d), BF16), grid=(n // tm,),
        in_specs=[pl.BlockSpec((tm, d), lambda i: (i, 0)), pl.BlockSpec((1, d), lambda i: (0, 0))],
        out_specs=pl.BlockSpec((tm, d), lambda i: (i, 0)),
        compiler_params=_params("parallel"), name=name)(x, g)


def _group_masks(width):
    lane = lax.broadcasted_iota(jnp.int32, (1, width), 1)
    return [(lane >= HEAD_DIM * g) & (lane < HEAD_DIM * (g + 1)) for g in range(width // HEAD_DIM)]


def _group_sum(x, masks):
    out = jnp.zeros_like(x)
    for msk in masks:
        s = jnp.sum(jnp.where(msk, x, 0.0), axis=-1, keepdims=True)
        out = jnp.where(msk, s, out)
    return out


def _head_norm(x, gain, masks):
    r = lax.rsqrt(_group_sum(x * x, masks) * (1.0 / HEAD_DIM) + NORM_EPS)
    xhat = x * r
    return xhat * gain, xhat, r


def _head_norm_bwd(dxn, xhat, r, gain, masks):
    dgain = jnp.sum(dxn * xhat, axis=0, keepdims=True)
    dxhat = dxn * gain
    mean_t = _group_sum(dxhat * xhat, masks) * (1.0 / HEAD_DIM)
    return r * (dxhat - xhat * mean_t), dgain


def _softmax_rows(s):
    e = jnp.exp(s - jnp.max(s, axis=-1, keepdims=True))
    return e * (1.0 / jnp.sum(e, axis=-1, keepdims=True))


def _rel_onehot():
    col = lax.broadcasted_iota(jnp.int32, (1, KEY_WIN), 1)
    off = jnp.where(col < KEY_WIN - LANES, col, col - KEY_WIN)
    idx = jnp.clip(8 * CHUNK - off, -(CHUNK - 1), LANES) + (CHUNK - 1)
    return (lax.broadcasted_iota(jnp.int32, (N_REL, KEY_WIN), 0) == idx).astype(F32)


def bias_blocks(rel16):
    heads = TOK_WIDTH // HEAD_DIM

    def body(rel_ref, o_ref, u_ref):
        u_ref[...] = jnp.dot(rel_ref[...], _rel_onehot(), precision=HIGHEST, preferred_element_type=F32)
        row = lax.broadcasted_iota(jnp.int32, (CHUNK, KEY_WIN), 0)
        col = lax.broadcasted_iota(jnp.int32, (CHUNK, KEY_WIN), 1)
        for h in range(heads):
            xv = jnp.broadcast_to(u_ref[h:h + 1, :], (CHUNK, KEY_WIN))
            for b in range(6):
                xv = jnp.where(((row >> b) & 1) == 1, pltpu.roll(xv, 1 << b, axis=1), xv)
            xv = jnp.where(col < BAND, xv, NEG_INF)
            for i in range(Q_BLOCK // CHUNK):
                o_ref[h, CHUNK * i:CHUNK * (i + 1), :] = pltpu.roll(xv, CHUNK * i, axis=1) if i else xv

    return pl.pallas_call(
        body, out_shape=jax.ShapeDtypeStruct((heads, Q_BLOCK, KEY_WIN), F32),
        scratch_shapes=[pltpu.VMEM((16, KEY_WIN), F32)], name="bias_blocks")(rel16)


def bias_grad(dbias):
    heads = dbias.shape[0]

    def body(db_ref, o_ref, y_ref):
        y_ref[...] = jnp.zeros_like(y_ref)
        row = lax.broadcasted_iota(jnp.int32, (CHUNK, KEY_WIN), 0)
        for h in range(heads):
            fv = db_ref[h, 0:CHUNK, :]
            for i in range(1, Q_BLOCK // CHUNK):
                fv = fv + pltpu.roll(db_ref[h, CHUNK * i:CHUNK * (i + 1), :], KEY_WIN - CHUNK * i, axis=1)
            for b in range(6):
                fv = jnp.where(((row >> b) & 1) == 1, pltpu.roll(fv, KEY_WIN - (1 << b), axis=1), fv)
            y_ref[h:h + 1, :] = jnp.sum(fv, axis=0, keepdims=True)
        o_ref[...] = lax.dot_general(y_ref[...], _rel_onehot(), (((1,), (1,)), ((), ())),
                                     precision=HIGHEST, preferred_element_type=F32)

    return pl.pallas_call(
        body, out_shape=jax.ShapeDtypeStruct((16, N_REL), F32),
        scratch_shapes=[pltpu.VMEM((16, KEY_WIN), F32)], name="bias_grad")(dbias)


def _attn_windows(seq):
    out = []
    for j in range(seq // Q_BLOCK):
        r0 = j * Q_BLOCK
        k0 = max(0, r0 - 8 * CHUNK)
        width = r0 + Q_BLOCK - k0
        out.append((r0, k0, width, KEY_WIN - width))
    return out


def attn_fwd(z, gq2, gk2, bias, batch, seq):
    n = z.shape[0]
    pairs = TOK_WIDTH // LANES

    def body(q_ref, k_ref, v_ref, gq_ref, gk_ref, b_ref, o_ref, qs_s, kn_s):
        masks = _group_masks(LANES)
        qs_s[...] = (_head_norm(q_ref[...].astype(F32), gq_ref[...], masks)[0] * ATTN_SCALE).astype(BF16)
        kn_s[...] = _head_norm(k_ref[...].astype(F32), gk_ref[...], masks)[0].astype(BF16)
        for r0, k0, width, c0 in _attn_windows(seq):
            qb = qs_s[r0:r0 + Q_BLOCK, :]
            kw = kn_s[k0:k0 + width, :]
            vw = v_ref[k0:k0 + width, :]
            out = jnp.zeros((Q_BLOCK, LANES), F32)
            for h, msk in enumerate(masks):
                qh = jnp.where(msk, qb, jnp.zeros_like(qb))
                s = _dot(qh, kw, 1, 1) + b_ref[h, :, c0:KEY_WIN]
                p = _softmax_rows(s).astype(BF16)
                out = jnp.where(msk, _dot(p, vw, 1, 0), out)
            o_ref[r0:r0 + Q_BLOCK, :] = out.astype(o_ref.dtype)

    def col(off):
        return pl.BlockSpec((seq, LANES), lambda b, p: (b, off + p))

    vec = pl.BlockSpec((1, LANES), lambda b, p: (0, 0))
    return pl.pallas_call(
        body, out_shape=jax.ShapeDtypeStruct((n, D_MODEL), BF16), grid=(batch, pairs),
        in_specs=[col(0), col(pairs), col(2 * pairs), vec, vec,
                  pl.BlockSpec((2, Q_BLOCK, KEY_WIN), lambda b, p: (p, 0, 0))],
        out_specs=pl.BlockSpec((seq, LANES), lambda b, p: (b, p)),
        scratch_shapes=[pltpu.VMEM((seq, LANES), BF16), pltpu.VMEM((seq, LANES), BF16)],
        compiler_params=_params("parallel", "arbitrary"), name="attn_fwd")(z, z, z, gq2, gk2, bias)


def attn_bwd(z, dcat, gq2, gk2, bias, batch, seq):
    n = z.shape[0]
    pairs = TOK_WIDTH // LANES

    def body(q_ref, k_ref, v_ref, do_ref, gq_ref, gk_ref, b_ref,
             dq_ref, dk_ref, dv_ref, db_ref, dgq_ref, dgk_ref, qs_s, kn_s, dqn_s, dkn_s, dv_s):
        pi, bi = pl.program_id(0), pl.program_id(1)
        masks = _group_masks(LANES)

        @pl.when(bi == 0)
        def _():
            db_ref[...] = jnp.zeros_like(db_ref)

        @pl.when((bi == 0) & (pi == 0))
        def _():
            dgq_ref[...] = jnp.zeros_like(dgq_ref)
            dgk_ref[...] = jnp.zeros_like(dgk_ref)

        qn, qhat, rq = _head_norm(q_ref[...].astype(F32), gq_ref[...], masks)
        kn, khat, rk = _head_norm(k_ref[...].astype(F32), gk_ref[...], masks)
        qs_s[...] = (qn * ATTN_SCALE).astype(BF16)
        kn_s[...] = kn.astype(BF16)
        dkn_s[...] = jnp.zeros_like(dkn_s)
        dv_s[...] = jnp.zeros_like(dv_s)
        for r0, k0, width, c0 in _attn_windows(seq):
            qb = qs_s[r0:r0 + Q_BLOCK, :]
            dob = do_ref[r0:r0 + Q_BLOCK, :]
            kw = kn_s[k0:k0 + width, :]
            vw = v_ref[k0:k0 + width, :]
            dq_acc = jnp.zeros((Q_BLOCK, LANES), F32)
            dk_acc = jnp.zeros((width, LANES), F32)
            dv_acc = jnp.zeros((width, LANES), F32)
            for h, msk in enumerate(masks):
                qh = jnp.where(msk, qb, jnp.zeros_like(qb))
                doh = jnp.where(msk, dob, jnp.zeros_like(dob))
                p = _softmax_rows(_dot(qh, kw, 1, 1) + b_ref[h, :, c0:KEY_WIN])
                dp = _dot(doh, vw, 1, 1)
                ds = p * (dp - jnp.sum(p * dp, axis=-1, keepdims=True))
                db_ref[h, :, c0:KEY_WIN] += ds
                dsb = ds.astype(BF16)
                dq_acc = jnp.where(msk, _dot(dsb, kw, 1, 0), dq_acc)
                dk_acc = jnp.where(msk, _dot(dsb, qb, 0, 0), dk_acc)
                dv_acc = jnp.where(msk, _dot(p.astype(BF16), dob, 0, 0), dv_acc)
            dqn_s[r0:r0 + Q_BLOCK, :] = dq_acc * ATTN_SCALE
            dkn_s[k0:k0 + width, :] += dk_acc
            dv_s[k0:k0 + width, :] += dv_acc
        dq, dgq = _head_norm_bwd(dqn_s[...], qhat, rq, gq_ref[...], masks)
        dk, dgk = _head_norm_bwd(dkn_s[...], khat, rk, gk_ref[...], masks)
        dq_ref[...] = dq.astype(dq_ref.dtype)
        dk_ref[...] = dk.astype(dk_ref.dtype)
        dv_ref[...] = dv_s[...].astype(dv_ref.dtype)
        dgq_ref[...] += dgq
        dgk_ref[...] += dgk

    def col(off):
        return pl.BlockSpec((seq, LANES), lambda p, b: (b, off + p))

    vec = pl.BlockSpec((1, LANES), lambda p, b: (0, 0))
    blk = pl.BlockSpec((2, Q_BLOCK, KEY_WIN), lambda p, b: (p, 0, 0))
    o_shape = jax.ShapeDtypeStruct((n, TOK_WIDTH), BF16)
    v_shape = jax.ShapeDtypeStruct((1, LANES), F32)
    return pl.pallas_call(
        body,
        out_shape=(o_shape, o_shape, o_shape, jax.ShapeDtypeStruct(bias.shape, F32), v_shape, v_shape),
        grid=(pairs, batch),
        in_specs=[col(0), col(pairs), col(2 * pairs), col(0), vec, vec, blk],
        out_specs=(col(0), col(0), col(0), blk, vec, vec),
        scratch_shapes=[pltpu.VMEM((seq, LANES), BF16), pltpu.VMEM((seq, LANES), BF16),
                        pltpu.VMEM((seq, LANES), F32), pltpu.VMEM((seq, LANES), F32), pltpu.VMEM((seq, LANES), F32)],
        compiler_params=_params("arbitrary", "arbitrary"), name="attn_bwd")(z, z, z, dcat, gq2, gk2, bias)


MEM_ROWS_FWD = 1024
MEM_ROWS_BWD = 2048


def memattn_fwd(z, mem, mem_gain, wkv, gq4, gk4, cat, batch, seq, qcol, name):
    mtok = mem.shape[0] // batch
    d = mem.shape[1]
    rows = min(MEM_ROWS_FWD, seq)

    def body(q_ref, m_ref, mg_ref, w_ref, gq_ref, gk_ref, cat_ref, o_ref, n_ref, kv_ref):
        del cat_ref
        masks = _group_masks(MEM_WIDTH)
        mv = m_ref[...]
        r = lax.rsqrt(jnp.mean(mv * mv, axis=-1, keepdims=True) + NORM_EPS)
        nv = (mv * r * mg_ref[...]).astype(BF16)
        n_ref[...] = nv
        kv_ref[...] = _dot(nv, w_ref[...], 1, 0)
        kn = _head_norm(kv_ref[:, 0:MEM_WIDTH], gk_ref[...], masks)[0].astype(BF16)
        vm = kv_ref[:, MEM_WIDTH:2 * MEM_WIDTH].astype(BF16)
        for t in range(seq // rows):
            sl = slice(t * rows, (t + 1) * rows)
            qs = (_head_norm(q_ref[sl, :].astype(F32), gq_ref[...], masks)[0] * ATTN_SCALE).astype(BF16)
            out = jnp.zeros((rows, MEM_WIDTH), F32)
            for msk in masks:
                qh = jnp.where(msk, qs, jnp.zeros_like(qs))
                p = _softmax_rows(_dot(qh, kn, 1, 1)).astype(BF16)
                out = jnp.where(msk, _dot(p, vm, 1, 0), out)
            o_ref[sl, :] = out.astype(o_ref.dtype)

    vec = pl.BlockSpec((1, MEM_WIDTH), lambda b: (0, 0))
    mem_spec = pl.BlockSpec((mtok, d), lambda b: (b, 0))
    kv_spec = pl.BlockSpec((mtok, 2 * MEM_WIDTH), lambda b: (b, 0))
    return pl.pallas_call(
        body, out_shape=(jax.ShapeDtypeStruct(cat.shape, cat.dtype), jax.ShapeDtypeStruct(mem.shape, BF16),
                         jax.ShapeDtypeStruct((mem.shape[0], 2 * MEM_WIDTH), F32)), grid=(batch,),
        in_specs=[pl.BlockSpec((seq, MEM_WIDTH), lambda b: (b, qcol)), mem_spec, pl.BlockSpec((1, d), lambda b: (0, 0)),
                  pl.BlockSpec(wkv.shape, lambda b: (0, 0)), vec, vec, ANY],
        out_specs=(pl.BlockSpec((seq, MEM_WIDTH), lambda b: (b, TOK_WIDTH // MEM_WIDTH)), mem_spec, kv_spec),
        input_output_aliases={6: 0},
        compiler_params=_params("parallel"), name=name)(z, mem, mem_gain, wkv, gq4, gk4, cat)


def memattn_bwd(z, kv, dcat, gq4, gk4, mem, mem_n, wkv, dz, batch, seq, qcol, name):
    mtok = kv.shape[0] // batch
    d = mem.shape[1]
    rows = min(MEM_ROWS_BWD, seq)

    def body(q_ref, kv_ref, do_ref, gq_ref, gk_ref, m_ref, n_ref, w_ref, *rest):
        dq_ref, dgq_ref, dgk_ref, dw_ref, dmg_ref, dw_acc = rest[-6:]

        @pl.when(pl.program_id(0) == 0)
        def _():
            dgq_ref[...] = jnp.zeros_like(dgq_ref)
            dgk_ref[...] = jnp.zeros_like(dgk_ref)
            dmg_ref[...] = jnp.zeros_like(dmg_ref)
            dw_acc[...] = jnp.zeros_like(dw_acc)

        masks = _group_masks(MEM_WIDTH)
        kn_f, khat, rk = _head_norm(kv_ref[:, 0:MEM_WIDTH], gk_ref[...], masks)
        kn = kn_f.astype(BF16)
        vm = kv_ref[:, MEM_WIDTH:2 * MEM_WIDTH].astype(BF16)
        dkn = jnp.zeros((mtok, MEM_WIDTH), F32)
        dvm = jnp.zeros((mtok, MEM_WIDTH), F32)
        dgq = jnp.zeros((1, MEM_WIDTH), F32)
        for t in range(seq // rows):
            sl = slice(t * rows, (t + 1) * rows)
            qn_f, qhat, rq = _head_norm(q_ref[sl, :].astype(F32), gq_ref[...], masks)
            qs = (qn_f * ATTN_SCALE).astype(BF16)
            dob = do_ref[sl, :]
            dqn = jnp.zeros((rows, MEM_WIDTH), F32)
            for msk in masks:
                qh = jnp.where(msk, qs, jnp.zeros_like(qs))
                doh = jnp.where(msk, dob, jnp.zeros_like(dob))
                p = _softmax_rows(_dot(qh, kn, 1, 1))
                dp = _dot(doh, vm, 1, 1)
                ds = p * (dp - jnp.sum(p * dp, axis=-1, keepdims=True))
                dsb = ds.astype(BF16)
                dqn = jnp.where(msk, _dot(dsb, kn, 1, 0), dqn)
                dkn = dkn + jnp.where(msk, _dot(dsb, qs, 0, 0), 0.0)
                dvm = dvm + jnp.where(msk, _dot(p.astype(BF16), dob, 0, 0), 0.0)
            dq, dg = _head_norm_bwd(dqn * ATTN_SCALE, qhat, rq, gq_ref[...], masks)
            dq_ref[sl, :] = dq.astype(dq_ref.dtype)
            dgq = dgq + dg
        dk, dgk = _head_norm_bwd(dkn, khat, rk, gk_ref[...], masks)
        dgq_ref[...] += dgq
        dgk_ref[...] += dgk
        dkv_b = jnp.concatenate([dk, dvm], axis=-1).astype(BF16)
        dw_acc[...] += _dot(n_ref[...], dkv_b, 0, 0)
        dn = _dot(dkv_b, w_ref[...], 1, 1)
        mv = m_ref[...]
        rm = lax.rsqrt(jnp.mean(mv * mv, axis=-1, keepdims=True) + NORM_EPS)
        dmg_ref[...] += jnp.sum(dn * (mv * rm), axis=0, keepdims=True)

        @pl.when(pl.program_id(0) == batch - 1)
        def _():
            dw_ref[...] = dw_acc[...].astype(dw_ref.dtype)

    vec = pl.BlockSpec((1, MEM_WIDTH), lambda b: (0, 0))
    kv_spec = pl.BlockSpec((mtok, 2 * MEM_WIDTH), lambda b: (b, 0))
    mem_spec = pl.BlockSpec((mtok, d), lambda b: (b, 0))
    w_spec = pl.BlockSpec(wkv.shape, lambda b: (0, 0))
    v_shape = jax.ShapeDtypeStruct((1, MEM_WIDTH), F32)
    q_spec = pl.BlockSpec((seq, MEM_WIDTH), lambda b: (b, qcol))
    in_specs = [q_spec, kv_spec, pl.BlockSpec((seq, MEM_WIDTH), lambda b: (b, TOK_WIDTH // MEM_WIDTH)), vec, vec,
                mem_spec, mem_spec, w_spec]
    args = [z, kv, dcat, gq4, gk4, mem, mem_n, wkv]
    if dz is None:
        dq_shape, dq_spec, aliases = jax.ShapeDtypeStruct((z.shape[0], MEM_WIDTH), BF16), \
            pl.BlockSpec((seq, MEM_WIDTH), lambda b: (b, 0)), {}
    else:
        dq_shape, dq_spec, aliases = jax.ShapeDtypeStruct(dz.shape, dz.dtype), q_spec, {len(args): 0}
        in_specs.append(ANY)
        args.append(dz)
    return pl.pallas_call(
        body,
        out_shape=(dq_shape, v_shape, v_shape, jax.ShapeDtypeStruct(wkv.shape, BF16), jax.ShapeDtypeStruct((1, d), F32)),
        grid=(batch,), in_specs=in_specs,
        out_specs=(dq_spec, vec, vec, w_spec, pl.BlockSpec((1, d), lambda b: (0, 0))),
        scratch_shapes=[pltpu.VMEM(wkv.shape, F32)], input_output_aliases=aliases,
        compiler_params=_params("arbitrary"), name=name)(*args)


CONV_ROWS = 512


def _glu(a_ref, g_ref):
    return a_ref[...].astype(F32) * _sigmoid(g_ref[...].astype(F32))


def _layer_norm_stats(y):
    mu = jnp.mean(y, axis=-1, keepdims=True)
    yc = y - mu
    rstd = lax.rsqrt(jnp.mean(yc * yc, axis=-1, keepdims=True) + NORM_EPS)
    return yc * rstd, rstd


CONV_WIN = CONV_HALO + CONV_ROWS
SUBLANES = 8
SHIFT_ROWS = CONV_WIN - SUBLANES


def _preshift(win, shifted):
    for s in range(1, SUBLANES):
        shifted[s - 1, :, :] = win[s:s + SHIFT_ROWS, :]


TAP_ROWS = 16
TAP_TILES = [(r0, slice(c0, c0 + LANES)) for c0 in range(0, TOK_WIDTH, LANES) for r0 in range(0, CONV_ROWS, TAP_ROWS)]


def _tap(win, shifted, off, r0, lanes):
    s = off % SUBLANES
    base = off - s + r0
    if s == 0:
        return win[base:base + TAP_ROWS, lanes]
    return shifted[s - 1, base:base + TAP_ROWS, lanes]


def _fold_rows(x):
    return jnp.sum(x.reshape(TAP_ROWS // SUBLANES, SUBLANES, LANES), axis=0)


def conv_fwd(z, cw, cb, lg, lb, batch, seq):
    n = z.shape[0]
    nt = seq // CONV_ROWS
    sub = CONV_ROWS // CONV_HALO
    lead = CONV_HALO - (CONV_W - 1)

    def body(a_ref, g_ref, ap_ref, gp_ref, cw_ref, cb_ref, lg_ref, lb_ref, o_ref, y_ref, win, shifted):
        first = pl.program_id(1) == 0
        win[0:CONV_HALO, :] = jnp.where(first, 0.0, _glu(ap_ref, gp_ref))
        win[CONV_HALO:CONV_WIN, :] = _glu(a_ref, g_ref)
        _preshift(win, shifted)
        for r0, lanes in TAP_TILES:
            acc = jnp.zeros((TAP_ROWS, LANES), F32) + cb_ref[:, lanes]
            for w in range(CONV_W):
                acc = acc + _tap(win, shifted, lead + w, r0, lanes) * cw_ref[w:w + 1, lanes]
            y_ref[r0:r0 + TAP_ROWS, lanes] = acc
        yh, _ = _layer_norm_stats(y_ref[...])
        t = yh * lg_ref[...] + lb_ref[...]
        o_ref[...] = (t * _sigmoid(t)).astype(o_ref.dtype)

    def cur(c):
        return pl.BlockSpec((CONV_ROWS, TOK_WIDTH), lambda b, i: (b * nt + i, c))

    def prev(c):
        return pl.BlockSpec((CONV_HALO, TOK_WIDTH), lambda b, i: (jnp.maximum((b * nt + i) * sub - 1, 0), c))

    vec = pl.BlockSpec((1, TOK_WIDTH), lambda b, i: (0, 0))
    return pl.pallas_call(
        body, out_shape=(jax.ShapeDtypeStruct((n, D_MODEL), BF16), jax.ShapeDtypeStruct((n, TOK_WIDTH), F32)),
        grid=(batch, nt),
        in_specs=[cur(0), cur(1), prev(0), prev(1), pl.BlockSpec((32, TOK_WIDTH), lambda b, i: (0, 0)), vec, vec, vec],
        out_specs=(cur(0), cur(0)),
        scratch_shapes=[pltpu.VMEM((CONV_WIN, TOK_WIDTH), F32), pltpu.VMEM((SUBLANES - 1, SHIFT_ROWS, TOK_WIDTH), F32)],
        compiler_params=_params("parallel", "arbitrary"), name="conv_fwd")(z, z, z, z, cw, cb, lg, lb)


def conv_bwd(z, y, dcat, cw, lg, lb, batch, seq):
    n = z.shape[0]
    nt = seq // CONV_ROWS
    sub = CONV_ROWS // CONV_HALO
    lead = CONV_HALO - (CONV_W - 1)
    last_blk = n // CONV_HALO - 1

    def body(a_ref, g_ref, ap_ref, gp_ref, y_ref, yn_ref, do_ref, don_ref, cw_ref, lg_ref, lb_ref,
             dz_ref, dcw_ref, dsm_ref, win, shifted, dyw, dshifted, dg_o):
        b, i, which = pl.program_id(0), pl.program_id(1), pl.program_id(2)

        @pl.when(which == 0)
        def _():
            first, last = i == 0, i == nt - 1

            @pl.when((b == 0) & (i == 0))
            def _():
                dcw_ref[...] = jnp.zeros_like(dcw_ref)
                dsm_ref[...] = jnp.zeros_like(dsm_ref)

            win[0:CONV_HALO, :] = jnp.where(first, 0.0, _glu(ap_ref, gp_ref))
            win[CONV_HALO:CONV_WIN, :] = _glu(a_ref, g_ref)
            _preshift(win, shifted)
            yv = jnp.concatenate([y_ref[...], yn_ref[...]], axis=0)
            yh, rstd = _layer_norm_stats(yv)
            t = yh * lg_ref[...] + lb_ref[...]
            st = _sigmoid(t)
            dout = jnp.concatenate(
                [do_ref[...].astype(F32), jnp.where(last, 0.0, don_ref[...].astype(F32))], axis=0)
            dt = dout * st * (1.0 + t * (1.0 - st))
            dyh = dt * lg_ref[...]
            dy = rstd * (dyh - jnp.mean(dyh, axis=-1, keepdims=True)
                         - yh * jnp.mean(dyh * yh, axis=-1, keepdims=True))
            dyw[...] = dy
            _preshift(dyw, dshifted)
            dsm_ref[0:1, :] += jnp.sum(dy[0:CONV_ROWS], axis=0, keepdims=True)
            dsm_ref[1:2, :] += jnp.sum((dt * yh)[0:CONV_ROWS], axis=0, keepdims=True)
            dsm_ref[2:3, :] += jnp.sum(dt[0:CONV_ROWS], axis=0, keepdims=True)
            for c0 in range(0, TOK_WIDTH, LANES):
                lanes = slice(c0, c0 + LANES)
                dcw_acc = [jnp.zeros((SUBLANES, LANES), F32) for _ in range(CONV_W)]
                for r0 in range(0, CONV_ROWS, TAP_ROWS):
                    dyt = dyw[r0:r0 + TAP_ROWS, lanes]
                    dglu = jnp.zeros((TAP_ROWS, LANES), F32)
                    for w in range(CONV_W):
                        dcw_acc[w] = dcw_acc[w] + _fold_rows(dyt * _tap(win, shifted, lead + w, r0, lanes))
                        dglu = dglu + _tap(dyw, dshifted, CONV_W - 1 - w, r0, lanes) * cw_ref[w:w + 1, lanes]
                    avt = a_ref[r0:r0 + TAP_ROWS, lanes].astype(F32)
                    sgt = _sigmoid(g_ref[r0:r0 + TAP_ROWS, lanes].astype(F32))
                    dz_ref[r0:r0 + TAP_ROWS, lanes] = (dglu * sgt).astype(dz_ref.dtype)
                    dg_o[r0:r0 + TAP_ROWS, lanes] = (dglu * avt * sgt * (1.0 - sgt)).astype(dg_o.dtype)
                for w in range(CONV_W):
                    dcw_ref[w:w + 1, lanes] += jnp.sum(dcw_acc[w], axis=0, keepdims=True)

        @pl.when(which == 1)
        def _():
            dz_ref[...] = dg_o[...]

    def ahead(b, i, t):
        return jnp.minimum(b * nt + i + t, batch * nt - 1)

    def cur(c):
        return pl.BlockSpec((CONV_ROWS, TOK_WIDTH), lambda b, i, t: (ahead(b, i, t), c))

    def prev(c):
        return pl.BlockSpec((CONV_HALO, TOK_WIDTH), lambda b, i, t: (jnp.maximum(ahead(b, i, t) * sub - 1, 0), c))

    nxt = pl.BlockSpec((CONV_HALO, TOK_WIDTH),
                       lambda b, i, t: (jnp.minimum((ahead(b, i, t) + 1) * sub, last_blk), 0))
    vec = pl.BlockSpec((1, TOK_WIDTH), lambda b, i, t: (0, 0))
    full32 = pl.BlockSpec((32, TOK_WIDTH), lambda b, i, t: (0, 0))
    return pl.pallas_call(
        body,
        out_shape=(jax.ShapeDtypeStruct(z.shape, BF16), jax.ShapeDtypeStruct((32, TOK_WIDTH), F32),
                   jax.ShapeDtypeStruct((8, TOK_WIDTH), F32)),
        grid=(batch, nt, 2),
        in_specs=[cur(0), cur(1), prev(0), prev(1), cur(0), nxt, cur(0), nxt, full32, vec, vec],
        out_specs=(pl.BlockSpec((CONV_ROWS, TOK_WIDTH), lambda b, i, t: (b * nt + i, t)), full32,
                   pl.BlockSpec((8, TOK_WIDTH), lambda b, i, t: (0, 0))),
        scratch_shapes=[pltpu.VMEM((CONV_WIN, TOK_WIDTH), F32), pltpu.VMEM((SUBLANES - 1, SHIFT_ROWS, TOK_WIDTH), F32),
                        pltpu.VMEM((CONV_WIN, TOK_WIDTH), F32), pltpu.VMEM((SUBLANES - 1, SHIFT_ROWS, TOK_WIDTH), F32),
                        pltpu.VMEM((CONV_ROWS, TOK_WIDTH), BF16)],
        compiler_params=_params("arbitrary", "arbitrary", "arbitrary"), name="conv_bwd")(
            z, z, z, z, y, y, dcat, dcat, cw, lg, lb)


def _place():
    return lax.axis_index("x"), lax.axis_index("y"), lax.axis_index("c")


def _other_chips(x, y):
    return [(1 - x, y), (x, 1 - y), (1 - x, 1 - y)]


def reduce_small(arrays):
    na = len(arrays)

    def body(*refs):
        ins, outs, bufs = refs[:na], refs[na:2 * na], refs[2 * na:3 * na]
        send_sems, recv_sems = refs[3 * na:]
        x, y, c = _place()
        me = 4 * x + 2 * y + c
        copies = []
        for a in range(na):
            bufs[a][me] = ins[a][...]
            for k in range(1, N_DEV):
                cp = pltpu.make_async_remote_copy(
                    src_ref=ins[a], dst_ref=bufs[a].at[me], send_sem=send_sems.at[a, k - 1],
                    recv_sem=recv_sems.at[a, k - 1],
                    device_id=(x ^ (k >> 2), y ^ ((k >> 1) & 1), c ^ (k & 1)), device_id_type=MESH)
                cp.start()
                copies.append(cp)
        for a in range(na):
            for k in range(1, N_DEV):
                src = 4 * (x ^ (k >> 2)) + 2 * (y ^ ((k >> 1) & 1)) + (c ^ (k & 1))
                pltpu.make_async_remote_copy(
                    src_ref=ins[a], dst_ref=bufs[a].at[src], send_sem=send_sems.at[a, k - 1],
                    recv_sem=recv_sems.at[a, k - 1], device_id=(x, y, c), device_id_type=MESH).wait_recv()
        for cp in copies:
            cp.wait_send()
        for a in range(na):
            total = bufs[a][0]
            for dev in range(1, N_DEV):
                total = total + bufs[a][dev]
            outs[a][...] = total

    vmem = pl.BlockSpec(memory_space=pltpu.VMEM)
    return pl.pallas_call(
        body, out_shape=tuple(jax.ShapeDtypeStruct(a.shape, F32) for a in arrays),
        in_specs=[vmem] * na, out_specs=tuple([vmem] * na),
        scratch_shapes=[pltpu.VMEM((N_DEV,) + a.shape, F32) for a in arrays]
        + [pltpu.SemaphoreType.DMA((na, N_DEV - 1)), pltpu.SemaphoreType.DMA((na, N_DEV - 1))],
        compiler_params=pltpu.CompilerParams(vmem_limit_bytes=VMEM_LIMIT), name="small_reduce")(*arrays)


def adamw_small(ws, gs, ms, vs):
    na = len(ws)
    c1 = 1.0 / (1.0 - ADAM_B1 ** ADAM_STEP)
    c2 = 1.0 / (1.0 - ADAM_B2 ** ADAM_STEP)

    def body(*refs):
        w_refs, g_refs, m_refs, v_refs = (refs[i * na:(i + 1) * na] for i in range(4))
        d_refs, nm_refs, nv_refs = (refs[(4 + i) * na:(5 + i) * na] for i in range(3))
        for a in range(na):
            gv = g_refs[a][...]
            nm = ADAM_B1 * m_refs[a][...] + (1.0 - ADAM_B1) * gv
            nv = ADAM_B2 * v_refs[a][...] + (1.0 - ADAM_B2) * (gv * gv)
            nm_refs[a][...] = nm
            nv_refs[a][...] = nv
            d_refs[a][...] = -ADAM_LR * ((nm * c1) / (jnp.sqrt(nv * c2) + ADAM_EPS) + ADAM_WD * w_refs[a][...])

    vmem = pl.BlockSpec(memory_space=pltpu.VMEM)
    shapes = tuple(jax.ShapeDtypeStruct(w.shape, F32) for w in ws)
    outs = pl.pallas_call(
        body, out_shape=shapes * 3, in_specs=[vmem] * (4 * na), out_specs=tuple([vmem] * (3 * na)),
        compiler_params=pltpu.CompilerParams(vmem_limit_bytes=VMEM_LIMIT), name="adamw_small")(*ws, *gs, *ms, *vs)
    return outs[:na], outs[na:2 * na], outs[2 * na:]


def gather_weights(shards, name, collective_id):
    nw = len(shards)
    ns = [s.shape[0] for s in shards]
    in_refs = [jax.new_ref(s, memory_space=pltpu.MemorySpace.HBM) for s in shards]
    out_refs = [jax.empty_ref(jax.ShapeDtypeStruct((N_DEV * s.shape[0], s.shape[1]), s.dtype),
                              memory_space=pltpu.MemorySpace.HBM) for s in shards]

    @pl.kernel(mesh=plsc.ScalarSubcoreMesh(axis_name="seq", num_cores=1), name=name,
               scratch_types=(pltpu.SemaphoreType.DMA((nw, 7)), pltpu.SemaphoreType.DMA((nw, 7)),
                              pltpu.SemaphoreType.DMA((nw,))),
               compiler_params=pltpu.CompilerParams(collective_id=collective_id))
    def launch(send_sems, recv_sems, local_sems):
        x, y, c = _place()
        me, sib = (x, y, c), (x, y, 1 - c)
        chips = _other_chips(x, y)
        barrier = pltpu.get_barrier_semaphore()
        for peer in [sib] + [(*chip, c) for chip in chips]:
            pl.semaphore_signal(barrier, inc=1, device_id=peer, device_id_type=MESH)
        pl.semaphore_wait(barrier, 4)

        def rows(w, dev):
            return out_refs[w].at[pl.ds((4 * dev[0] + 2 * dev[1] + dev[2]) * ns[w], ns[w]), :]

        def copy(w, k, block, to, src=None):
            return pltpu.make_async_remote_copy(
                src_ref=rows(w, block) if src is None else src, dst_ref=rows(w, block),
                send_sem=send_sems.at[w, k], recv_sem=recv_sems.at[w, k], device_id=to, device_id_type=MESH)

        started, sends = [], []
        for w in range(nw):
            mine = pltpu.make_async_copy(in_refs[w], rows(w, me), local_sems.at[w])
            mine.start()
            started.append(mine)
            first = [copy(w, 0, me, sib, src=in_refs[w])]
            first += [copy(w, 1 + j, me, (*chip, c), src=in_refs[w]) for j, chip in enumerate(chips)]
            for cp in first:
                cp.start()
            sends += first
        for w in range(nw):
            for j, chip in enumerate(chips):
                copy(w, 1 + j, (*chip, c), me).wait_recv()
                fwd = copy(w, 4 + j, (*chip, c), sib)
                fwd.start()
                sends.append(fwd)
        for w in range(nw):
            copy(w, 0, sib, me).wait_recv()
            for j, chip in enumerate(chips):
                copy(w, 4 + j, (*chip, 1 - c), me).wait_recv()
        for cp in sends:
            cp.wait_send()
        for mine in started:
            mine.wait()

    launch()
    return [r[...] for r in out_refs]


def _sequencer_exchange(sources, out_rows, peers_of, copies_of, name, collective_id):
    nw = len(sources)
    in_refs = [jax.new_ref(s, memory_space=pltpu.MemorySpace.HBM) for s in sources]
    out_refs = [jax.empty_ref(jax.ShapeDtypeStruct((rows, s.shape[1]), s.dtype), memory_space=pltpu.MemorySpace.HBM)
                for rows, s in zip(out_rows, sources)]
    per = len(copies_of(0, 0, 0, 0))

    @pl.kernel(mesh=plsc.ScalarSubcoreMesh(axis_name="seq", num_cores=1), name=name,
               scratch_types=(pltpu.SemaphoreType.DMA((nw, per)), pltpu.SemaphoreType.DMA((nw, per))),
               compiler_params=pltpu.CompilerParams(collective_id=collective_id))
    def launch(send_sems, recv_sems):
        x, y, c = _place()
        peers = peers_of(x, y, c)
        barrier = pltpu.get_barrier_semaphore()
        for peer in peers:
            pl.semaphore_signal(barrier, inc=1, device_id=peer, device_id_type=MESH)
        pl.semaphore_wait(barrier, len(peers))
        copies = []
        for w in range(nw):
            for k, (src_blk, dst_blk, rows, peer) in enumerate(copies_of(x, y, c, w)):
                cp = pltpu.make_async_remote_copy(
                    src_ref=in_refs[w].at[pl.ds(src_blk * rows, rows), :],
                    dst_ref=out_refs[w].at[pl.ds(dst_blk * rows, rows), :],
                    send_sem=send_sems.at[w, k], recv_sem=recv_sems.at[w, k], device_id=peer, device_id_type=MESH)
                cp.start()
                copies.append(cp)
        for cp in copies:
            cp.wait_recv()
        for cp in copies:
            cp.wait_send()

    launch()
    return [r[...] for r in out_refs]


def scatter_to_sibling(grads, name, collective_id):
    ns = [g.shape[0] // N_DEV for g in grads]
    return _sequencer_exchange(
        grads, [4 * n for n in ns],
        lambda x, y, c: [(x, y, 1 - c)],
        lambda x, y, c, w: [(2 * q + 1 - c, q, ns[w], (x, y, 1 - c)) for q in range(4)],
        name, collective_id)


def scatter_to_chips(parts, name, collective_id):
    ns = [p.shape[0] // 4 for p in parts]
    return _sequencer_exchange(
        parts, [3 * n for n in ns],
        lambda x, y, c: [(*chip, c) for chip in _other_chips(x, y)],
        lambda x, y, c, w: [(2 * chip[0] + chip[1], j, ns[w], (*chip, c)) for j, chip in enumerate(_other_chips(x, y))],
        name, collective_id)


def add_sibling(grads, landeds, core, name):
    nw = len(grads)

    def body(c_ref, *refs):
        for w in range(nw):
            g_ref, l_ref, o_ref = refs[2 * w], refs[2 * w + 1], refs[2 * nw + w]
            o_ref[...] = (g_ref[...].astype(F32) + l_ref[...].astype(F32)).astype(o_ref.dtype)

    in_specs, out_specs, args = [], [], []
    for g, ld in zip(grads, landeds):
        n, cols = ld.shape[0] // 4, g.shape[1]
        in_specs += [pl.BlockSpec((n, cols), lambda q, c_ref: (2 * q + c_ref[0], 0)),
                     pl.BlockSpec((n, cols), lambda q, c_ref: (q, 0))]
        out_specs.append(pl.BlockSpec((n, cols), lambda q, c_ref: (q, 0)))
        args += [g, ld]
    grid_spec = pltpu.PrefetchScalarGridSpec(
        num_scalar_prefetch=1, grid=(4,), in_specs=in_specs, out_specs=tuple(out_specs))
    return pl.pallas_call(
        body, out_shape=tuple(jax.ShapeDtypeStruct(ld.shape, ld.dtype) for ld in landeds), grid_spec=grid_spec,
        compiler_params=_params("arbitrary"), name=name)(core, *args)


ADAMW_HALVES = 2


def adamw_shards(items, chip, name):
    c1 = 1.0 / (1.0 - ADAM_B1 ** ADAM_STEP)
    c2 = 1.0 / (1.0 - ADAM_B2 ** ADAM_STEP)
    ni = len(items)

    def body(q_ref, *refs):
        outs = refs[len(refs) - 4 * ni:]
        for k in range(ni):
            w_ref, m_ref, v_ref, p_ref, l0_ref, l1_ref, l2_ref = refs[7 * k:7 * k + 7]
            g_ref, d_ref, nm_ref, nv_ref = outs[4 * k:4 * k + 4]
            gv = ((p_ref[...].astype(F32) + l0_ref[...].astype(F32)) + l1_ref[...].astype(F32)) + l2_ref[...].astype(F32)
            nm = ADAM_B1 * m_ref[...] + (1.0 - ADAM_B1) * gv
            nv = ADAM_B2 * v_ref[...] + (1.0 - ADAM_B2) * (gv * gv)
            g_ref[...] = gv
            nm_ref[...] = nm
            nv_ref[...] = nv
            d_ref[...] = -ADAM_LR * ((nm * c1) / (jnp.sqrt(nv * c2) + ADAM_EPS) + ADAM_WD * w_ref[...])

    sub = ADAMW_HALVES
    in_specs, out_specs, out_shape, args, donated = [], [], [], [chip], []
    for layer, w, m, v, part, landed, earlier in items:
        rows, cols = landed.shape[0] // (3 * sub), w.shape[1]

        def block(first, rows=rows, cols=cols):
            return pl.BlockSpec((rows, cols), lambda i, q_ref: (first(q_ref) * sub + i, 0))

        own = block(lambda q_ref, layer=layer: layer)
        in_specs += [own, own, own, block(lambda q_ref: q_ref[0])] + [block(lambda q_ref, j=j: j) for j in range(3)]
        args += [w, m, v, part, landed, landed, landed]
        out_specs += [own] * 4
        out_shape += [jax.ShapeDtypeStruct(w.shape, F32)] * 4
        donated.append(earlier)
    aliases = {}
    for k, earlier in enumerate(donated):
        if earlier is not None:
            for j in range(4):
                aliases[len(args)] = 4 * k + j
                in_specs.append(ANY)
                args.append(earlier[j])
    grid_spec = pltpu.PrefetchScalarGridSpec(
        num_scalar_prefetch=1, grid=(sub,), in_specs=in_specs, out_specs=tuple(out_specs))
    outs = pl.pallas_call(
        body, out_shape=tuple(out_shape), grid_spec=grid_spec, input_output_aliases=aliases,
        compiler_params=_params("arbitrary"), name=name)(*args)
    return [tuple(outs[4 * k:4 * k + 4]) for k in range(ni)]


def _pack(arrays):
    flat = jnp.concatenate([a.reshape(-1).astype(F32) for a in arrays])
    pad = (-flat.shape[0]) % (8 * LANES)
    return jnp.pad(flat, (0, pad)).reshape(-1, LANES)


def _unpack(slab, shapes):
    flat = slab.reshape(slab.shape[:-2] + (-1,))
    out, off = [], 0
    for shp in shapes:
        size = 1
        for s in shp:
            size *= s
        out.append(flat[..., off:off + size].reshape(flat.shape[:-1] + tuple(shp)))
        off += size
    return out


def kernel(x, mem, norm1_g, mem_norm_g, a_w_in, a_q_g, a_k_g, a_rel_bias, b_w_in, b_b_in, b_conv_w, b_conv_b, b_ln_g, b_ln_b, mq_g, mk_g, w_mem_kv, w_out, norm2_g, w_gate, w_up, w_down, loss_target, m_norm1_g, m_mem_norm_g, m_a_w_in, m_a_q_g, m_a_k_g, m_a_rel_bias, m_b_w_in, m_b_b_in, m_b_conv_w, m_b_conv_b, m_b_ln_g, m_b_ln_b, m_mq_g, m_mk_g, m_w_mem_kv, m_w_out, m_norm2_g, m_w_gate, m_w_up, m_w_down, v_norm1_g, v_mem_norm_g, v_a_w_in, v_a_q_g, v_a_k_g, v_a_rel_bias, v_b_w_in, v_b_b_in, v_b_conv_w, v_b_conv_b, v_b_ln_g, v_b_ln_b, v_mq_g, v_mk_g, v_w_mem_kv, v_w_out, v_norm2_g, v_w_gate, v_w_up, v_w_down):
    batch, seq, d = x.shape
    mtok = mem.shape[1]
    n = batch * seq
    ax, ay, ac = _place()
    me = 4 * ax + 2 * ay + ac
    core_arr = jnp.reshape(ac, (1,)).astype(jnp.int32)
    chip_arr = jnp.reshape(2 * ax + ay, (1,)).astype(jnp.int32)

    def t_bf16(w):
        return jnp.transpose(w).astype(BF16)

    def after(value, *earlier):
        return lax.optimization_barrier((value, *earlier))[0]

    def gather_mix(l, when, name, collective_id):
        srcs = [w_mem_kv[l].astype(BF16), w_out[l].astype(BF16)]
        if l == 1:
            srcs += [t_bf16(b_w_in[0]), _pack([b_b_in, b_conv_w, b_conv_b, b_ln_g, b_ln_b])]
        return gather_weights([after(srcs[0], *when)] + srcs[1:], name, collective_id)

    def gather_ffn(l, when, name, collective_id):
        return gather_weights(
            [after(t_bf16(w_gate[l]), *when), t_bf16(w_up[l]), w_down[l].astype(BF16)], name, collective_id)

    f_loc = b_b_in.shape[1]
    c_loc = b_conv_b.shape[1]

    def two(g):
        return jnp.concatenate([g, g], axis=-1)

    gq2, gk2 = two(a_q_g), two(a_k_g)
    rel16 = jnp.pad(a_rel_bias[0], ((0, 16 - a_rel_bias.shape[1]), (0, 0)))
    bias = bias_blocks(rel16)

    x0 = x.reshape(n, d)
    mem2 = mem.reshape(batch * mtok, d)

    saved = []
    xin = x0
    a_win_t, = gather_weights([t_bf16(a_w_in[0])], "gather_in_a", 1)
    wg_t, wu_t, wd, wo, wkv = [None] * 2, [None] * 2, [None] * 2, [None] * 2, [None] * 2
    h = after(rms_fwd(xin, norm1_g[0:1], name="rms1_fwd_0"), bias)
    target = loss_target.reshape(n, d)
    for l in range(2):
        gq4 = jnp.tile(mq_g[l:l + 1], (1, 4))
        gk4 = jnp.tile(mk_g[l:l + 1], (1, 4))
        y_conv = None
        if l == 0:
            wkv[0], wo[0] = gather_mix(0, (h, a_win_t), "gather_mix_a", 2)
            z = mm_nt(h, a_win_t, name="in_proj_a")
            wg_t[0], wu_t[0], wd[0] = gather_ffn(0, (z, wkv[0]), "gather_ffn_a", 3)
            cat = attn_fwd(z, gq2, gk2, bias, batch, seq)
            wkv[1], wo[1], b_win_t, conv_slabs = gather_mix(1, (cat, wg_t[0]), "gather_mix_b", 4)
            qcol = 3 * TOK_WIDTH // MEM_WIDTH
        else:
            small_shapes = [(f_loc,), (CONV_W, c_loc), (c_loc,), (c_loc,), (c_loc,)]
            bb_g, cw_g, cb_g, lg_g, lb_g = _unpack(conv_slabs.reshape(N_DEV, -1, LANES), small_shapes)
            bb_full = bb_g.reshape(1, -1)
            cw_full = jnp.pad(jnp.transpose(cw_g, (1, 0, 2)).reshape(CONV_W, -1), ((0, 32 - CONV_W), (0, 0)))
            cb_full, lg_full, lb_full = cb_g.reshape(1, -1), lg_g.reshape(1, -1), lb_g.reshape(1, -1)
            z = mm_nt(h, b_win_t, bias=bb_full, name="in_proj_b")
            cat, y_conv = conv_fwd(z, cw_full, cb_full, lg_full, lb_full, batch, seq)
            qcol = 2 * TOK_WIDTH // MEM_WIDTH
        cat, mem_n, kv = memattn_fwd(
            z, mem2, mem_norm_g[l:l + 1], wkv[l], gq4, gk4, cat, batch, seq, qcol, name=f"memattn_fwd_{l}")
        x1, h2 = proj_norm(cat, wo[l], xin, norm2_g[l:l + 1], name=f"out_proj_{l}")
        if l == 0:
            wg_t[1], wu_t[1], wd[1] = gather_ffn(1, (x1, b_win_t), "gather_ffn_b", 5)
        if l == 0:
            gate, up, act, x2, h_next = ffn_fwd(h2, wg_t[0], wu_t[0], wd[0], x1, gain=norm1_g[1:2], name="ffn_fwd_0")
        else:
            gate, up, act, dx_b, loss_blk = ffn_fwd(h2, wg_t[1], wu_t[1], wd[1], x1, target=target, name="ffn_fwd_1")
        saved.append(dict(xin=xin, h=h, mem_n=mem_n, kv=kv, gq4=gq4, gk4=gk4, z=z, qcol=qcol, cat=cat, x1=x1, h2=h2,
                          gate=gate, up=up, act=act, y_conv=y_conv))
        if l == 0:
            xin, h = x2, h_next

    big = {}
    small = {}
    reduced = {}
    groups = 0

    def scatter_siblings(keys):
        nonlocal groups
        gid = groups
        groups += 1
        return gid, keys, scatter_to_sibling([big[k] for k in keys], f"scatter_sibling_{gid}", 8 + 2 * gid)

    def scatter_chips(stage1, when):
        gid, keys, landed1 = stage1
        parts = add_sibling([after(big[keys[0]], when)] + [big[k] for k in keys[1:]], landed1, core_arr,
                            name=f"add_sibling_{gid}")
        landed2 = scatter_to_chips(parts, f"scatter_chips_{gid}", 9 + 2 * gid)
        for k, p, ld in zip(keys, parts, landed2):
            reduced[k] = (p, ld)
        return parts, landed2

    def rows_of(w, transposed):
        w = jnp.swapaxes(w, 1, 2) if transposed else w
        return w.reshape(w.shape[0] * w.shape[1], w.shape[2])

    sharded = {
        "win0": (2, True), "win1": (6, True), "wkv": (14, False), "wo": (15, False),
        "wg": (17, True), "wu": (18, True), "wd": (19, False)}
    weights = [norm1_g, mem_norm_g, a_w_in, a_q_g, a_k_g, a_rel_bias, b_w_in, b_b_in, b_conv_w, b_conv_b, b_ln_g,
               b_ln_b, mq_g, mk_g, w_mem_kv, w_out, norm2_g, w_gate, w_up, w_down]
    moms = [m_norm1_g, m_mem_norm_g, m_a_w_in, m_a_q_g, m_a_k_g, m_a_rel_bias, m_b_w_in, m_b_b_in, m_b_conv_w,
            m_b_conv_b, m_b_ln_g, m_b_ln_b, m_mq_g, m_mk_g, m_w_mem_kv, m_w_out, m_norm2_g, m_w_gate, m_w_up, m_w_down]
    vels = [v_norm1_g, v_mem_norm_g, v_a_w_in, v_a_q_g, v_a_k_g, v_a_rel_bias, v_b_w_in, v_b_b_in, v_b_conv_w,
            v_b_conv_b, v_b_ln_g, v_b_ln_b, v_mq_g, v_mk_g, v_w_mem_kv, v_w_out, v_norm2_g, v_w_gate, v_w_up, v_w_down]
    updated = {}

    def update_layer(l, when):
        for group, keys in (("ffn", ("wg", "wu", "wd")), ("mix", (f"win{l}", "wkv", "wo"))):
            items = []
            for key in keys:
                idx, transposed = sharded[key]
                layer, rkey = (0, key) if key.startswith("win") else (l, f"{key}{l}")
                part, landed = reduced[rkey]
                w_rows = rows_of(weights[idx], transposed)
                items.append((layer, after(w_rows, when) if not items else w_rows, rows_of(moms[idx], transposed),
                              rows_of(vels[idx], transposed), part, landed, updated.get(key)))
            for key, result in zip(keys, adamw_shards(items, chip_arr, name=f"adamw_{group}_{l}")):
                updated[key] = result

    mix_landed = None
    for l in (1, 0):
        sv = saved[l]
        dgate, dup, dx1_b, dcat, small[f"norm2_{l}"] = ffn_bwd(
            dx_b, wd[l], sv["gate"], sv["up"], wg_t[l], wu_t[l], sv["x1"], norm2_g[l:l + 1], wo[l], name=f"ffn_bwd_{l}")
        if l == 0:
            dgate = after(dgate, *mix_landed)
            update_layer(1, dx1_b)
        big[f"wg{l}"], big[f"wu{l}"], big[f"wd{l}"] = ffn_weight_grads(
            dgate, dup, sv["h2"], sv["act"], dx_b, name=f"grad_ffn_{l}")
        stage1 = scatter_siblings([f"wd{l}", f"wg{l}", f"wu{l}"])
        big[f"wo{l}"] = mm_tn(sv["cat"], dx1_b, name=f"grad_wo_{l}")
        parts, ffn_landed = scatter_chips(stage1, big[f"wo{l}"])
        dcat = after(dcat, *parts)
        if l == 0:
            dq, dk, dv, dbias, small["a_q"], small["a_k"] = attn_bwd(sv["z"], dcat, gq2, gk2, bias, batch, seq)
            small["rel"] = bias_grad(dbias)
            win_t = a_win_t
            dz = None
            dcat = after(dcat, dq, *ffn_landed)
        else:
            dz, small["cw"], small["csum"] = conv_bwd(sv["z"], sv["y_conv"], dcat, cw_full, lg_full, lb_full, batch, seq)
            win_t = b_win_t
            dz = after(dz, *ffn_landed)
        dqm, small[f"mq_{l}"], small[f"mk_{l}"], big[f"wkv{l}"], small[f"memnorm_{l}"] = memattn_bwd(
            sv["z"], sv["kv"], dcat, sv["gq4"], sv["gk4"], mem2, sv["mem_n"], wkv[l], dz, batch, seq, sv["qcol"],
            name=f"memattn_bwd_{l}")
        if l == 0:
            pieces = [dq, dk, dv, dqm]
            big["win0"] = grad_pieces(pieces, sv["h"], name="grad_win_0")
        else:
            pieces = [dqm]
            big["win1"] = mm_tn(dqm, sv["h"], name="grad_win_1")
        stage1 = scatter_siblings([f"win{l}", f"wkv{l}", f"wo{l}"])
        dx_b, small[f"norm1_{l}"], dz_sum = in_proj_bwd(
            pieces, win_t, sv["xin"], norm1_g[l:l + 1], dx1_b, BF16 if l == 1 else F32, name=f"in_proj_bwd_{l}")
        if l == 1:
            small["bb"] = dz_sum
        parts, mix_landed = scatter_chips(stage1, dx_b)
        dx_b = after(dx_b, *parts)
    grad_x = dx_b.reshape(batch, seq, d)
    update_layer(0, dx_b)

    def shaped(rows, idx, transposed):
        shp = weights[idx].shape
        if transposed:
            return jnp.swapaxes(rows.reshape(shp[0], shp[2], shp[1]), 1, 2)
        return rows.reshape(shp)

    def fold(v, groups):
        return jnp.sum(v.reshape(groups, HEAD_DIM), axis=0, keepdims=True)

    heads = a_rel_bias.shape[1]
    small_list = [
        jnp.concatenate([small["norm1_0"], small["norm1_1"]]),
        jnp.concatenate([small["memnorm_0"], small["memnorm_1"]]),
        fold(small["a_q"], 2), fold(small["a_k"], 2), small["rel"][:heads],
        small["bb"], small["cw"][:CONV_W], small["csum"][0:1], small["csum"][1:2], small["csum"][2:3],
        jnp.concatenate([fold(small["mq_0"], 4), fold(small["mq_1"], 4)]),
        jnp.concatenate([fold(small["mk_0"], 4), fold(small["mk_1"], 4)]),
        jnp.concatenate([small["norm2_0"], small["norm2_1"]]),
    ]
    (g_norm1, g_memnorm, g_aq, g_ak, g_rel, g_bb_full, g_cw_full, g_cb_full, g_lg_full, g_lb_full,
     g_mq, g_mk, g_norm2, loss_sum) = reduce_small(small_list + [loss_blk])
    loss = loss_sum[0, 0]
    g_bb = lax.dynamic_slice_in_dim(g_bb_full, me * f_loc, f_loc, axis=1)
    g_cw = lax.dynamic_slice_in_dim(g_cw_full, me * c_loc, c_loc, axis=1)
    g_cb = lax.dynamic_slice_in_dim(g_cb_full, me * c_loc, c_loc, axis=1)
    g_lg = lax.dynamic_slice_in_dim(g_lg_full, me * c_loc, c_loc, axis=1)
    g_lb = lax.dynamic_slice_in_dim(g_lb_full, me * c_loc, c_loc, axis=1)

    grads = [g_norm1, g_memnorm, None, g_aq, g_ak, g_rel, None, g_bb, g_cw, g_cb, g_lg, g_lb,
             g_mq, g_mk, None, None, g_norm2, None, None, None]
    deltas, new_m, new_v = [None] * 20, [None] * 20, [None] * 20
    for key, (idx, transposed) in sharded.items():
        grads[idx], deltas[idx], new_m[idx], new_v[idx] = (shaped(r, idx, transposed) for r in updated[key])

    def flat2(a):
        return a.reshape(a.shape[-2:])

    small_idx = [i for i in range(20) if i not in {idx for idx, _ in sharded.values()}]
    dl, nm, nv = adamw_small([flat2(weights[i]) for i in small_idx], [flat2(grads[i]) for i in small_idx],
                             [flat2(moms[i]) for i in small_idx], [flat2(vels[i]) for i in small_idx])
    for i, a, b, cc in zip(small_idx, dl, nm, nv):
        shp = weights[i].shape
        grads[i], deltas[i], new_m[i], new_v[i] = grads[i].reshape(shp), a.reshape(shp), b.reshape(shp), cc.reshape(shp)

    return (loss, grad_x, *grads, *deltas, *new_m, *new_v)
```

```python
import jax
import jax.numpy as jnp
from jax import lax
from jax.experimental import pallas as pl
from jax.experimental.pallas import tpu as pltpu
from jax.experimental.pallas import tpu_sc as plsc

F32 = jnp.float32
BF16 = jnp.bfloat16
HIGHEST = lax.Precision.HIGHEST
MESH = pl.DeviceIdType.MESH
ANY = pl.BlockSpec(memory_space=pl.ANY)

N_DEV = 8
D_MODEL = 1024
HEAD_DIM = 64
TOK_WIDTH = 768
MEM_WIDTH = 256
CHUNK = 64
Q_BLOCK = 256
KEY_WIN = 768
BAND = 576
N_REL = 192
CONV_W = 31
CONV_HALO = 32
NORM_EPS = 1e-6
NEG_INF = -1e30
ATTN_SCALE = HEAD_DIM ** -0.5
LANES = 128
ROW_TILE = 512
VMEM_LIMIT = 56 * 1024 * 1024

ADAM_LR, ADAM_B1, ADAM_B2, ADAM_EPS, ADAM_WD, ADAM_STEP = 0.001, 0.9, 0.999, 1e-08, 0.01, 10


def _params(*sem):
    return pltpu.CompilerParams(dimension_semantics=sem, vmem_limit_bytes=VMEM_LIMIT)


WIDE_ROW_TILE = 2048


def _row_tile(m, rows=ROW_TILE):
    return rows if m % rows == 0 else m


def _col_tile(n, cap=1408):
    best = None
    for t in range(LANES, min(n, cap) + 1, LANES):
        if n % t == 0:
            best = t
    return best if best is not None else n


def _dot(a, b, ca, cb):
    return lax.dot_general(a, b, (((ca,), (cb,)), ((), ())), preferred_element_type=F32)


def _sigmoid(x):
    return 0.5 * jnp.tanh(0.5 * x) + 0.5


def mm_nt(a, b, bias=None, out_dtype=BF16, name="mm_nt"):
    m, k = a.shape
    n = b.shape[0]
    tm, tn = _row_tile(m, WIDE_ROW_TILE), _col_tile(n)

    def body(*refs):
        a_ref, b_ref = refs[0], refs[1]
        o_ref = refs[-1]
        acc = _dot(a_ref[...].astype(BF16), b_ref[...].astype(BF16), 1, 1)
        if bias is not None:
            acc = acc + refs[2][...]
        o_ref[...] = acc.astype(o_ref.dtype)

    in_specs = [pl.BlockSpec((tm, k), lambda j, i: (i, 0)), pl.BlockSpec((tn, k), lambda j, i: (j, 0))]
    args = [a, b]
    if bias is not None:
        in_specs.append(pl.BlockSpec((1, tn), lambda j, i: (0, j)))
        args.append(bias)
    return pl.pallas_call(
        body, out_shape=jax.ShapeDtypeStruct((m, n), out_dtype), grid=(n // tn, m // tm),
        in_specs=in_specs, out_specs=pl.BlockSpec((tm, tn), lambda j, i: (i, j)),
        compiler_params=_params("parallel", "arbitrary"), name=name)(*args)


def mm_tn(a, b, out_dtype=BF16, name="mm_tn"):
    t, r = a.shape
    c = b.shape[1]
    tr = _col_tile(r, 512)

    def body(a_ref, b_ref, o_ref):
        o_ref[...] = _dot(a_ref[...].astype(BF16), b_ref[...].astype(BF16), 0, 0).astype(o_ref.dtype)

    return pl.pallas_call(
        body, out_shape=jax.ShapeDtypeStruct((r, c), out_dtype), grid=(r // tr,),
        in_specs=[pl.BlockSpec((t, tr), lambda i: (0, i)), pl.BlockSpec((t, c), lambda i: (0, 0))],
        out_specs=pl.BlockSpec((tr, c), lambda i: (i, 0)),
        compiler_params=_params("parallel"), name=name)(a, b)


def _resident(shape):
    return pl.BlockSpec(shape, lambda i: (0, 0), pipeline_mode=pl.Buffered(1))


def proj_norm(a, b, res, gain, name):
    m, k = a.shape
    n = b.shape[1]
    tm = _row_tile(m)

    def body(a_ref, b_ref, res_ref, g_ref, x_ref, h_ref):
        xv = res_ref[...] + _dot(a_ref[...], b_ref[...], 1, 0)
        x_ref[...] = xv
        r = lax.rsqrt(jnp.mean(xv * xv, axis=-1, keepdims=True) + NORM_EPS)
        h_ref[...] = (xv * r * g_ref[...]).astype(BF16)

    row = pl.BlockSpec((tm, n), lambda i: (i, 0))
    return pl.pallas_call(
        body, out_shape=(jax.ShapeDtypeStruct((m, n), F32), jax.ShapeDtypeStruct((m, n), BF16)), grid=(m // tm,),
        in_specs=[pl.BlockSpec((tm, k), lambda i: (i, 0)), _resident((k, n)), row, _resident((1, n))],
        out_specs=(row, row), compiler_params=_params("parallel"), name=name)(a, b, res, gain)


def in_proj_bwd(pieces, w_t, x, gain, dres, out_dtype, name):
    m, n = x.shape
    k = pieces[0].shape[1]
    tm = _row_tile(m)
    npc = len(pieces)
    offs = [sum(p.shape[1] for p in pieces[:i]) for i in range(npc + 1)]

    def body(*refs):
        dz_refs = refs[:npc]
        w_ref, x_ref, g_ref, dres_ref, dx_ref, dg_ref, cs_ref = refs[npc:]

        @pl.when(pl.program_id(0) == 0)
        def _():
            dg_ref[...] = jnp.zeros_like(dg_ref)
            cs_ref[...] = jnp.zeros_like(cs_ref)

        cs_ref[...] += jnp.sum(dz_refs[0][...].astype(F32), axis=0, keepdims=True)
        dhv = _dot(dz_refs[0][...], w_ref[offs[0]:offs[1], :], 1, 0)
        for i in range(1, npc):
            dhv = dhv + _dot(dz_refs[i][...], w_ref[offs[i]:offs[i + 1], :], 1, 0)
        xv = x_ref[...]
        r = lax.rsqrt(jnp.mean(xv * xv, axis=-1, keepdims=True) + NORM_EPS)
        xhat = xv * r
        dg_ref[...] += jnp.sum(dhv * xhat, axis=0, keepdims=True)
        dxhat = dhv * g_ref[...]
        dx = dres_ref[...].astype(F32) + r * (dxhat - xhat * jnp.mean(dxhat * xhat, axis=-1, keepdims=True))
        dx_ref[...] = dx.astype(dx_ref.dtype)

    row = pl.BlockSpec((tm, n), lambda i: (i, 0))
    return pl.pallas_call(
        body, out_shape=(jax.ShapeDtypeStruct((m, n), out_dtype), jax.ShapeDtypeStruct((1, n), F32),
                         jax.ShapeDtypeStruct((1, k), F32)), grid=(m // tm,),
        in_specs=[pl.BlockSpec((tm, p.shape[1]), lambda i: (i, 0)) for p in pieces]
        + [_resident(w_t.shape), row, _resident((1, n)), row],
        out_specs=(row, pl.BlockSpec((1, n), lambda i: (0, 0)), pl.BlockSpec((1, k), lambda i: (0, 0))),
        compiler_params=_params("arbitrary"), name=name)(*pieces, w_t, x, gain, dres)


def grad_pieces(pieces, b, name):
    t, c = b.shape
    tr = 2 * LANES
    tiles = [p.shape[1] // tr for p in pieces]
    starts = [sum(tiles[:i]) for i in range(len(pieces) + 1)]

    def body(*refs):
        a_refs, b_ref, o_ref = refs[:len(pieces)], refs[len(pieces)], refs[len(pieces) + 1]
        i = pl.program_id(0)
        for p, a_ref in enumerate(a_refs):
            @pl.when((i >= starts[p]) & (i < starts[p + 1]))
            def _(a_ref=a_ref):
                o_ref[...] = _dot(a_ref[...], b_ref[...], 0, 0).astype(o_ref.dtype)

    def a_spec(p):
        return pl.BlockSpec((t, tr), lambda i: (0, jnp.clip(i - starts[p], 0, tiles[p] - 1)))

    return pl.pallas_call(
        body, out_shape=jax.ShapeDtypeStruct((starts[-1] * tr, c), BF16), grid=(starts[-1],),
        in_specs=[a_spec(p) for p in range(len(pieces))] + [_resident((t, c))],
        out_specs=pl.BlockSpec((tr, c), lambda i: (i, 0)),
        compiler_params=_params("arbitrary"), name=name)(*pieces, b)


FFN_ROWS = 256


def _ffn_row_tile(m):
    return FFN_ROWS if m % FFN_ROWS == 0 else m


def ffn_fwd(h2, wg_t, wu_t, wd, x1, gain=None, target=None, name="ffn_fwd"):
    n, d = h2.shape
    f = wg_t.shape[0]
    tm = _ffn_row_tile(n)
    nt = n // tm
    last = target is not None

    def body(h_ref, wg_ref, wu_ref, wd_ref, x1_ref, e_ref, g_ref, u_ref, a_ref, *rest):
        hv = h_ref[...]
        gv = _dot(hv, wg_ref[...], 1, 1)
        uv = _dot(hv, wu_ref[...], 1, 1)
        g_ref[...] = gv.astype(BF16)
        u_ref[...] = uv.astype(BF16)
        av = (gv * _sigmoid(gv) * uv).astype(BF16)
        a_ref[...] = av
        xv = x1_ref[...] + _dot(av, wd_ref[...], 1, 0)
        if not last:
            x_ref, hn_ref = rest
            x_ref[...] = xv
            r = lax.rsqrt(jnp.mean(xv * xv, axis=-1, keepdims=True) + NORM_EPS)
            hn_ref[...] = (xv * r * e_ref[...]).astype(BF16)
        else:
            dyb_ref, l_ref, acc_ref = rest
            i = pl.program_id(0)

            @pl.when(i == 0)
            def _():
                acc_ref[...] = jnp.zeros_like(acc_ref)

            err = xv - e_ref[...]
            dyb_ref[...] = (err * (1.0 / d)).astype(BF16)
            acc_ref[...] += jnp.sum(err * err, axis=0, keepdims=True)

            @pl.when(i == nt - 1)
            def _():
                total = jnp.sum(acc_ref[...], axis=-1, keepdims=True) * (0.5 / d)
                l_ref[...] = jnp.broadcast_to(total, l_ref.shape)

    row_d = pl.BlockSpec((tm, d), lambda i: (i, 0))
    row_f = pl.BlockSpec((tm, f), lambda i: (i, 0))
    act_shape = jax.ShapeDtypeStruct((n, f), BF16)
    if not last:
        extra_in, extra = _resident((1, d)), gain
        out_shape = (act_shape, act_shape, act_shape, jax.ShapeDtypeStruct((n, d), F32), jax.ShapeDtypeStruct((n, d), BF16))
        out_specs = (row_f, row_f, row_f, row_d, row_d)
        scratch = []
    else:
        extra_in, extra = row_d, target
        out_shape = (act_shape, act_shape, act_shape, jax.ShapeDtypeStruct((n, d), BF16),
                     jax.ShapeDtypeStruct((8, LANES), F32))
        out_specs = (row_f, row_f, row_f, row_d, pl.BlockSpec((8, LANES), lambda i: (0, 0)))
        scratch = [pltpu.VMEM((1, d), F32)]
    return pl.pallas_call(
        body, out_shape=out_shape, grid=(nt,),
        in_specs=[row_d, _resident((f, d)), _resident((f, d)), _resident((f, d)), row_d, extra_in],
        out_specs=out_specs, scratch_shapes=scratch,
        compiler_params=_params("arbitrary"), name=name)(h2, wg_t, wu_t, wd, x1, extra)


def ffn_bwd(dx_b, wd, gate, up, wg_t, wu_t, x1, gain, wo, name="ffn_bwd"):
    n, d = x1.shape
    f = wd.shape[0]
    tm = _ffn_row_tile(n)

    def body(dxb_ref, wd_ref, g_ref, u_ref, wg_ref, wu_ref, x_ref, gain_ref, wo_ref,
             dg_ref, du_ref, dxo_ref, dc_ref, dgain_ref):
        @pl.when(pl.program_id(0) == 0)
        def _():
            dgain_ref[...] = jnp.zeros_like(dgain_ref)

        dact = _dot(dxb_ref[...], wd_ref[...], 1, 1)
        gv = g_ref[...].astype(F32)
        uv = u_ref[...].astype(F32)
        sg = _sigmoid(gv)
        dgv = (dact * uv * sg * (1.0 + gv * (1.0 - sg))).astype(BF16)
        duv = (dact * gv * sg).astype(BF16)
        dg_ref[...] = dgv
        du_ref[...] = duv
        dhv = _dot(dgv, wg_ref[...], 1, 0) + _dot(duv, wu_ref[...], 1, 0)
        xv = x_ref[...]
        r = lax.rsqrt(jnp.mean(xv * xv, axis=-1, keepdims=True) + NORM_EPS)
        xhat = xv * r
        dgain_ref[...] += jnp.sum(dhv * xhat, axis=0, keepdims=True)
        dxhat = dhv * gain_ref[...]
        dxb = (dxb_ref[...].astype(F32) + r * (dxhat - xhat * jnp.mean(dxhat * xhat, axis=-1, keepdims=True))).astype(BF16)
        dxo_ref[...] = dxb
        dc_ref[...] = _dot(dxb, wo_ref[...], 1, 1).astype(BF16)

    row_d = pl.BlockSpec((tm, d), lambda i: (i, 0))
    row_f = pl.BlockSpec((tm, f), lambda i: (i, 0))
    w_spec = _resident((f, d))
    act_shape = jax.ShapeDtypeStruct((n, f), BF16)
    row_shape = jax.ShapeDtypeStruct((n, d), BF16)
    return pl.pallas_call(
        body, out_shape=(act_shape, act_shape, row_shape, jax.ShapeDtypeStruct((n, wo.shape[0]), BF16),
                         jax.ShapeDtypeStruct((1, d), F32)),
        grid=(n // tm,),
        in_specs=[row_d, w_spec, row_f, row_f, w_spec, w_spec, row_d, _resident((1, d)), _resident(wo.shape)],
        out_specs=(row_f, row_f, row_d, pl.BlockSpec((tm, wo.shape[0]), lambda i: (i, 0)),
                   pl.BlockSpec((1, d), lambda i: (0, 0))),
        compiler_params=_params("arbitrary"), name=name)(dx_b, wd, gate, up, wg_t, wu_t, x1, gain, wo)


def ffn_weight_grads(dgate, dup, h2, act, dx_b, name="ffn_weight_grads"):
    t, r = dgate.shape
    c = h2.shape[1]
    tr = _col_tile(r, 512)

    def body(a1_ref, a2_ref, a3_ref, b12_ref, b3_ref, o1_ref, o2_ref, o3_ref):
        bv = b12_ref[...]
        o1_ref[...] = _dot(a1_ref[...], bv, 0, 0).astype(o1_ref.dtype)
        o2_ref[...] = _dot(a2_ref[...], bv, 0, 0).astype(o2_ref.dtype)
        o3_ref[...] = _dot(a3_ref[...], b3_ref[...], 0, 0).astype(o3_ref.dtype)

    a_spec = pl.BlockSpec((t, tr), lambda i: (0, i))
    o_spec = pl.BlockSpec((tr, c), lambda i: (i, 0))
    shape = jax.ShapeDtypeStruct((r, c), BF16)
    return pl.pallas_call(
        body, out_shape=(shape, shape, shape), grid=(r // tr,),
        in_specs=[a_spec, a_spec, a_spec, _resident((t, c)), _resident((t, c))],
        out_specs=(o_spec, o_spec, o_spec), compiler_params=_params("parallel"), name=name)(dgate, dup, act, h2, dx_b)


def rms_fwd(x, g, name="rms_fwd"):
    n, d = x.shape
    tm = _row_tile(n)

    def body(x_ref, g_ref, o_ref):
        xv = x_ref[...]
        r = lax.rsqrt(jnp.mean(xv * xv, axis=-1, keepdims=True) + NORM_EPS)
        o_ref[...] = (xv * r * g_ref[...]).astype(o_ref.dtype)

    return pl.pallas_call(
        body, out_shape=jax.ShapeDtypeStruct((n, d), BF16), grid=(n // tm,),
        in_specs=[pl.BlockSpec((tm, d), lambda i: (i, 0)), pl.BlockSpec((1, d), lambda i: (0, 0))],
        out_specs=pl.BlockSpec((tm, d), lambda i: (i, 0)),
        compiler_params=_params("parallel"), name=name)(x, g)


def _group_masks(width):
    lane = lax.broadcasted_iota(jnp.int32, (1, width), 1)
    return [(lane >= HEAD_DIM * g) & (lane < HEAD_DIM * (g + 1)) for g in range(width // HEAD_DIM)]


def _group_sum(x, masks):
    out = jnp.zeros_like(x)
    for msk in masks:
        s = jnp.sum(jnp.where(msk, x, 0.0), axis=-1, keepdims=True)
        out = jnp.where(msk, s, out)
    return out


def _head_norm(x, gain, masks):
    r = lax.rsqrt(_group_sum(x * x, masks) * (1.0 / HEAD_DIM) + NORM_EPS)
    xhat = x * r
    return xhat * gain, xhat, r


def _head_norm_bwd(dxn, xhat, r, gain, masks):
    dgain = jnp.sum(dxn * xhat, axis=0, keepdims=True)
    dxhat = dxn * gain
    mean_t = _group_sum(dxhat * xhat, masks) * (1.0 / HEAD_DIM)
    return r * (dxhat - xhat * mean_t), dgain


def _softmax_rows(s):
    e = jnp.exp(s - jnp.max(s, axis=-1, keepdims=True))
    return e * (1.0 / jnp.sum(e, axis=-1, keepdims=True))


def _rel_onehot():
    col = lax.broadcasted_iota(jnp.int32, (1, KEY_WIN), 1)
    off = jnp.where(col < KEY_WIN - LANES, col, col - KEY_WIN)
    idx = jnp.clip(8 * CHUNK - off, -(CHUNK - 1), LANES) + (CHUNK - 1)
    return (lax.broadcasted_iota(jnp.int32, (N_REL, KEY_WIN), 0) == idx).astype(F32)


def bias_blocks(rel16):
    heads = TOK_WIDTH // HEAD_DIM

    def body(rel_ref, o_ref, u_ref):
        u_ref[...] = jnp.dot(rel_ref[...], _rel_onehot(), precision=HIGHEST, preferred_element_type=F32)
        row = lax.broadcasted_iota(jnp.int32, (CHUNK, KEY_WIN), 0)
        col = lax.broadcasted_iota(jnp.int32, (CHUNK, KEY_WIN), 1)
        for h in range(heads):
            xv = jnp.broadcast_to(u_ref[h:h + 1, :], (CHUNK, KEY_WIN))
            for b in range(6):
                xv = jnp.where(((row >> b) & 1) == 1, pltpu.roll(xv, 1 << b, axis=1), xv)
            xv = jnp.where(col < BAND, xv, NEG_INF)
            for i in range(Q_BLOCK // CHUNK):
                o_ref[h, CHUNK * i:CHUNK * (i + 1), :] = pltpu.roll(xv, CHUNK * i, axis=1) if i else xv

    return pl.pallas_call(
        body, out_shape=jax.ShapeDtypeStruct((heads, Q_BLOCK, KEY_WIN), F32),
        scratch_shapes=[pltpu.VMEM((16, KEY_WIN), F32)], name="bias_blocks")(rel16)


def bias_grad(dbias):
    heads = dbias.shape[0]

    def body(db_ref, o_ref, y_ref):
        y_ref[...] = jnp.zeros_like(y_ref)
        row = lax.broadcasted_iota(jnp.int32, (CHUNK, KEY_WIN), 0)
        for h in range(heads):
            fv = db_ref[h, 0:CHUNK, :]
            for i in range(1, Q_BLOCK // CHUNK):
                fv = fv + pltpu.roll(db_ref[h, CHUNK * i:CHUNK * (i + 1), :], KEY_WIN - CHUNK * i, axis=1)
            for b in range(6):
                fv = jnp.where(((row >> b) & 1) == 1, pltpu.roll(fv, KEY_WIN - (1 << b), axis=1), fv)
            y_ref[h:h + 1, :] = jnp.sum(fv, axis=0, keepdims=True)
        o_ref[...] = lax.dot_general(y_ref[...], _rel_onehot(), (((1,), (1,)), ((), ())),
                                     precision=HIGHEST, preferred_element_type=F32)

    return pl.pallas_call(
        body, out_shape=jax.ShapeDtypeStruct((16, N_REL), F32),
        scratch_shapes=[pltpu.VMEM((16, KEY_WIN), F32)], name="bias_grad")(dbias)


def _attn_windows(seq):
    out = []
    for j in range(seq // Q_BLOCK):
        r0 = j * Q_BLOCK
        k0 = max(0, r0 - 8 * CHUNK)
        width = r0 + Q_BLOCK - k0
        out.append((r0, k0, width, KEY_WIN - width))
    return out


def attn_fwd(z, gq2, gk2, bias, batch, seq):
    n = z.shape[0]
    pairs = TOK_WIDTH // LANES

    def body(q_ref, k_ref, v_ref, gq_ref, gk_ref, b_ref, o_ref, qs_s, kn_s):
        masks = _group_masks(LANES)
        qs_s[...] = (_head_norm(q_ref[...].astype(F32), gq_ref[...], masks)[0] * ATTN_SCALE).astype(BF16)
        kn_s[...] = _head_norm(k_ref[...].astype(F32), gk_ref[...], masks)[0].astype(BF16)
        for r0, k0, width, c0 in _attn_windows(seq):
            qb = qs_s[r0:r0 + Q_BLOCK, :]
            kw = kn_s[k0:k0 + width, :]
            vw = v_ref[k0:k0 + width, :]
            out = jnp.zeros((Q_BLOCK, LANES), F32)
            for h, msk in enumerate(masks):
                qh = jnp.where(msk, qb, jnp.zeros_like(qb))
                s = _dot(qh, kw, 1, 1) + b_ref[h, :, c0:KEY_WIN]
                p = _softmax_rows(s).astype(BF16)
                out = jnp.where(msk, _dot(p, vw, 1, 0), out)
            o_ref[r0:r0 + Q_BLOCK, :] = out.astype(o_ref.dtype)

    def col(off):
        return pl.BlockSpec((seq, LANES), lambda b, p: (b, off + p))

    vec = pl.BlockSpec((1, LANES), lambda b, p: (0, 0))
    return pl.pallas_call(
        body, out_shape=jax.ShapeDtypeStruct((n, D_MODEL), BF16), grid=(batch, pairs),
        in_specs=[col(0), col(pairs), col(2 * pairs), vec, vec,
                  pl.BlockSpec((2, Q_BLOCK, KEY_WIN), lambda b, p: (p, 0, 0))],
        out_specs=pl.BlockSpec((seq, LANES), lambda b, p: (b, p)),
        scratch_shapes=[pltpu.VMEM((seq, LANES), BF16), pltpu.VMEM((seq, LANES), BF16)],
        compiler_params=_params("parallel", "arbitrary"), name="attn_fwd")(z, z, z, gq2, gk2, bias)


def attn_bwd(z, dcat, gq2, gk2, bias, batch, seq):
    n = z.shape[0]
    pairs = TOK_WIDTH // LANES

    def body(q_ref, k_ref, v_ref, do_ref, gq_ref, gk_ref, b_ref,
             dq_ref, dk_ref, dv_ref, db_ref, dgq_ref, dgk_ref, qs_s, kn_s, dqn_s, dkn_s, dv_s):
        pi, bi = pl.program_id(0), pl.program_id(1)
        masks = _group_masks(LANES)

        @pl.when(bi == 0)
        def _():
            db_ref[...] = jnp.zeros_like(db_ref)

        @pl.when((bi == 0) & (pi == 0))
        def _():
            dgq_ref[...] = jnp.zeros_like(dgq_ref)
            dgk_ref[...] = jnp.zeros_like(dgk_ref)

        qn, qhat, rq = _head_norm(q_ref[...].astype(F32), gq_ref[...], masks)
        kn, khat, rk = _head_norm(k_ref[...].astype(F32), gk_ref[...], masks)
        qs_s[...] = (qn * ATTN_SCALE).astype(BF16)
        kn_s[...] = kn.astype(BF16)
        dkn_s[...] = jnp.zeros_like(dkn_s)
        dv_s[...] = jnp.zeros_like(dv_s)
        for r0, k0, width, c0 in _attn_windows(seq):
            qb = qs_s[r0:r0 + Q_BLOCK, :]
            dob = do_ref[r0:r0 + Q_BLOCK, :]
            kw = kn_s[k0:k0 + width, :]
            vw = v_ref[k0:k0 + width, :]
            dq_acc = jnp.zeros((Q_BLOCK, LANES), F32)
            dk_acc = jnp.zeros((width, LANES), F32)
            dv_acc = jnp.zeros((width, LANES), F32)
            for h, msk in enumerate(masks):
                qh = jnp.where(msk, qb, jnp.zeros_like(qb))
                doh = jnp.where(msk, dob, jnp.zeros_like(dob))
                p = _softmax_rows(_dot(qh, kw, 1, 1) + b_ref[h, :, c0:KEY_WIN])
                dp = _dot(doh, vw, 1, 1)
                ds = p * (dp - jnp.sum(p * dp, axis=-1, keepdims=True))
                db_ref[h, :, c0:KEY_WIN] += ds
                dsb = ds.astype(BF16)
                dq_acc = jnp.where(msk, _dot(dsb, kw, 1, 0), dq_acc)
                dk_acc = jnp.where(msk, _dot(dsb, qb, 0, 0), dk_acc)
                dv_acc = jnp.where(msk, _dot(p.astype(BF16), dob, 0, 0), dv_acc)
            dqn_s[r0:r0 + Q_BLOCK, :] = dq_acc * ATTN_SCALE
            dkn_s[k0:k0 + width, :] += dk_acc
            dv_s[k0:k0 + width, :] += dv_acc
        dq, dgq = _head_norm_bwd(dqn_s[...], qhat, rq, gq_ref[...], masks)
        dk, dgk = _head_norm_bwd(dkn_s[...], khat, rk, gk_ref[...], masks)
        dq_ref[...] = dq.astype(dq_ref.dtype)
        dk_ref[...] = dk.astype(dk_ref.dtype)
        dv_ref[...] = dv_s[...].astype(dv_ref.dtype)
        dgq_ref[...] += dgq
        dgk_ref[...] += dgk

    def col(off):
        return pl.BlockSpec((seq, LANES), lambda p, b: (b, off + p))

    vec = pl.BlockSpec((1, LANES), lambda p, b: (0, 0))
    blk = pl.BlockSpec((2, Q_BLOCK, KEY_WIN), lambda p, b: (p, 0, 0))
    o_shape = jax.ShapeDtypeStruct((n, TOK_WIDTH), BF16)
    v_shape = jax.ShapeDtypeStruct((1, LANES), F32)
    return pl.pallas_call(
        body,
        out_shape=(o_shape, o_shape, o_shape, jax.ShapeDtypeStruct(bias.shape, F32), v_shape, v_shape),
        grid=(pairs, batch),
        in_specs=[col(0), col(pairs), col(2 * pairs), col(0), vec, vec, blk],
        out_specs=(col(0), col(0), col(0), blk, vec, vec),
        scratch_shapes=[pltpu.VMEM((seq, LANES), BF16), pltpu.VMEM((seq, LANES), BF16),
                        pltpu.VMEM((seq, LANES), F32), pltpu.VMEM((seq, LANES), F32), pltpu.VMEM((seq, LANES), F32)],
        compiler_params=_params("arbitrary", "arbitrary"), name="attn_bwd")(z, z, z, dcat, gq2, gk2, bias)


MEM_ROWS_FWD = 1024
MEM_ROWS_BWD = 2048


def memattn_fwd(z, mem, mem_gain, wkv, gq4, gk4, cat, batch, seq, qcol, name):
    mtok = mem.shape[0] // batch
    d = mem.shape[1]
    rows = min(MEM_ROWS_FWD, seq)

    def body(q_ref, m_ref, mg_ref, w_ref, gq_ref, gk_ref, cat_ref, o_ref, n_ref, kv_ref):
        del cat_ref
        masks = _group_masks(MEM_WIDTH)
        mv = m_ref[...]
        r = lax.rsqrt(jnp.mean(mv * mv, axis=-1, keepdims=True) + NORM_EPS)
        nv = (mv * r * mg_ref[...]).astype(BF16)
        n_ref[...] = nv
        kv_ref[...] = _dot(nv, w_ref[...], 1, 0)
        kn = _head_norm(kv_ref[:, 0:MEM_WIDTH], gk_ref[...], masks)[0].astype(BF16)
        vm = kv_ref[:, MEM_WIDTH:2 * MEM_WIDTH].astype(BF16)
        for t in range(seq // rows):
            sl = slice(t * rows, (t + 1) * rows)
            qs = (_head_norm(q_ref[sl, :].astype(F32), gq_ref[...], masks)[0] * ATTN_SCALE).astype(BF16)
            out = jnp.zeros((rows, MEM_WIDTH), F32)
            for msk in masks:
                qh = jnp.where(msk, qs, jnp.zeros_like(qs))
                p = _softmax_rows(_dot(qh, kn, 1, 1)).astype(BF16)
                out = jnp.where(msk, _dot(p, vm, 1, 0), out)
            o_ref[sl, :] = out.astype(o_ref.dtype)

    vec = pl.BlockSpec((1, MEM_WIDTH), lambda b: (0, 0))
    mem_spec = pl.BlockSpec((mtok, d), lambda b: (b, 0))
    kv_spec = pl.BlockSpec((mtok, 2 * MEM_WIDTH), lambda b: (b, 0))
    return pl.pallas_call(
        body, out_shape=(jax.ShapeDtypeStruct(cat.shape, cat.dtype), jax.ShapeDtypeStruct(mem.shape, BF16),
                         jax.ShapeDtypeStruct((mem.shape[0], 2 * MEM_WIDTH), F32)), grid=(batch,),
        in_specs=[pl.BlockSpec((seq, MEM_WIDTH), lambda b: (b, qcol)), mem_spec, pl.BlockSpec((1, d), lambda b: (0, 0)),
                  pl.BlockSpec(wkv.shape, lambda b: (0, 0)), vec, vec, ANY],
        out_specs=(pl.BlockSpec((seq, MEM_WIDTH), lambda b: (b, TOK_WIDTH // MEM_WIDTH)), mem_spec, kv_spec),
        input_output_aliases={6: 0},
        compiler_params=_params("parallel"), name=name)(z, mem, mem_gain, wkv, gq4, gk4, cat)


def memattn_bwd(z, kv, dcat, gq4, gk4, mem, mem_n, wkv, dz, batch, seq, qcol, name):
    mtok = kv.shape[0] // batch
    d = mem.shape[1]
    rows = min(MEM_ROWS_BWD, seq)

    def body(q_ref, kv_ref, do_ref, gq_ref, gk_ref, m_ref, n_ref, w_ref, *rest):
        dq_ref, dgq_ref, dgk_ref, dw_ref, dmg_ref, dw_acc = rest[-6:]

        @pl.when(pl.program_id(0) == 0)
        def _():
            dgq_ref[...] = jnp.zeros_like(dgq_ref)
            dgk_ref[...] = jnp.zeros_like(dgk_ref)
            dmg_ref[...] = jnp.zeros_like(dmg_ref)
            dw_acc[...] = jnp.zeros_like(dw_acc)

        masks = _group_masks(MEM_WIDTH)
        kn_f, khat, rk = _head_norm(kv_ref[:, 0:MEM_WIDTH], gk_ref[...], masks)
        kn = kn_f.astype(BF16)
        vm = kv_ref[:, MEM_WIDTH:2 * MEM_WIDTH].astype(BF16)
        dkn = jnp.zeros((mtok, MEM_WIDTH), F32)
        dvm = jnp.zeros((mtok, MEM_WIDTH), F32)
        dgq = jnp.zeros((1, MEM_WIDTH), F32)
        for t in range(seq // rows):
            sl = slice(t * rows, (t + 1) * rows)
            qn_f, qhat, rq = _head_norm(q_ref[sl, :].astype(F32), gq_ref[...], masks)
            qs = (qn_f * ATTN_SCALE).astype(BF16)
            dob = do_ref[sl, :]
            dqn = jnp.zeros((rows, MEM_WIDTH), F32)
            for msk in masks:
                qh = jnp.where(msk, qs, jnp.zeros_like(qs))
                doh = jnp.where(msk, dob, jnp.zeros_like(dob))
                p = _softmax_rows(_dot(qh, kn, 1, 1))
                dp = _dot(doh, vm, 1, 1)
                ds = p * (dp - jnp.sum(p * dp, axis=-1, keepdims=True))
                dsb = ds.astype(BF16)
                dqn = jnp.where(msk, _dot(dsb, kn, 1, 0), dqn)
                dkn = dkn + jnp.where(msk, _dot(dsb, qs, 0, 0), 0.0)
                dvm = dvm + jnp.where(msk, _dot(p.astype(BF16), dob, 0, 0), 0.0)
            dq, dg = _head_norm_bwd(dqn * ATTN_SCALE, qhat, rq, gq_ref[...], masks)
            dq_ref[sl, :] = dq.astype(dq_ref.dtype)
            dgq = dgq + dg
        dk, dgk = _head_norm_bwd(dkn, khat, rk, gk_ref[...], masks)
        dgq_ref[...] += dgq
        dgk_ref[...] += dgk
        dkv_b = jnp.concatenate([dk, dvm], axis=-1).astype(BF16)
        dw_acc[...] += _dot(n_ref[...], dkv_b, 0, 0)
        dn = _dot(dkv_b, w_ref[...], 1, 1)
        mv = m_ref[...]
        rm = lax.rsqrt(jnp.mean(mv * mv, axis=-1, keepdims=True) + NORM_EPS)
        dmg_ref[...] += jnp.sum(dn * (mv * rm), axis=0, keepdims=True)

        @pl.when(pl.program_id(0) == batch - 1)
        def _():
            dw_ref[...] = dw_acc[...].astype(dw_ref.dtype)

    vec = pl.BlockSpec((1, MEM_WIDTH), lambda b: (0, 0))
    kv_spec = pl.BlockSpec((mtok, 2 * MEM_WIDTH), lambda b: (b, 0))
    mem_spec = pl.BlockSpec((mtok, d), lambda b: (b, 0))
    w_spec = pl.BlockSpec(wkv.shape, lambda b: (0, 0))
    v_shape = jax.ShapeDtypeStruct((1, MEM_WIDTH), F32)
    q_spec = pl.BlockSpec((seq, MEM_WIDTH), lambda b: (b, qcol))
    in_specs = [q_spec, kv_spec, pl.BlockSpec((seq, MEM_WIDTH), lambda b: (b, TOK_WIDTH // MEM_WIDTH)), vec, vec,
                mem_spec, mem_spec, w_spec]
    args = [z, kv, dcat, gq4, gk4, mem, mem_n, wkv]
    if dz is None:
        dq_shape, dq_spec, aliases = jax.ShapeDtypeStruct((z.shape[0], MEM_WIDTH), BF16), \
            pl.BlockSpec((seq, MEM_WIDTH), lambda b: (b, 0)), {}
    else:
        dq_shape, dq_spec, aliases = jax.ShapeDtypeStruct(dz.shape, dz.dtype), q_spec, {len(args): 0}
        in_specs.append(ANY)
        args.append(dz)
    return pl.pallas_call(
        body,
        out_shape=(dq_shape, v_shape, v_shape, jax.ShapeDtypeStruct(wkv.shape, BF16), jax.ShapeDtypeStruct((1, d), F32)),
        grid=(batch,), in_specs=in_specs,
        out_specs=(dq_spec, vec, vec, w_spec, pl.BlockSpec((1, d), lambda b: (0, 0))),
        scratch_shapes=[pltpu.VMEM(wkv.shape, F32)], input_output_aliases=aliases,
        compiler_params=_params("arbitrary"), name=name)(*args)


CONV_ROWS = 512


def _glu(a_ref, g_ref):
    return a_ref[...].astype(F32) * _sigmoid(g_ref[...].astype(F32))


def _layer_norm_stats(y):
    mu = jnp.mean(y, axis=-1, keepdims=True)
    yc = y - mu
    rstd = lax.rsqrt(jnp.mean(yc * yc, axis=-1, keepdims=True) + NORM_EPS)
    return yc * rstd, rstd


CONV_WIN = CONV_HALO + CONV_ROWS
SUBLANES = 8
SHIFT_ROWS = CONV_WIN - SUBLANES


def _preshift(win, shifted):
    for s in range(1, SUBLANES):
        shifted[s - 1, :, :] = win[s:s + SHIFT_ROWS, :]


TAP_ROWS = 16
TAP_TILES = [(r0, slice(c0, c0 + LANES)) for c0 in range(0, TOK_WIDTH, LANES) for r0 in range(0, CONV_ROWS, TAP_ROWS)]


def _tap(win, shifted, off, r0, lanes):
    s = off % SUBLANES
    base = off - s + r0
    if s == 0:
        return win[base:base + TAP_ROWS, lanes]
    return shifted[s - 1, base:base + TAP_ROWS, lanes]


def _fold_rows(x):
    return jnp.sum(x.reshape(TAP_ROWS // SUBLANES, SUBLANES, LANES), axis=0)


def conv_fwd(z, cw, cb, lg, lb, batch, seq):
    n = z.shape[0]
    nt = seq // CONV_ROWS
    sub = CONV_ROWS // CONV_HALO
    lead = CONV_HALO - (CONV_W - 1)

    def body(a_ref, g_ref, ap_ref, gp_ref, cw_ref, cb_ref, lg_ref, lb_ref, o_ref, y_ref, win, shifted):
        first = pl.program_id(1) == 0
        win[0:CONV_HALO, :] = jnp.where(first, 0.0, _glu(ap_ref, gp_ref))
        win[CONV_HALO:CONV_WIN, :] = _glu(a_ref, g_ref)
        _preshift(win, shifted)
        for r0, lanes in TAP_TILES:
            acc = jnp.zeros((TAP_ROWS, LANES), F32) + cb_ref[:, lanes]
            for w in range(CONV_W):
                acc = acc + _tap(win, shifted, lead + w, r0, lanes) * cw_ref[w:w + 1, lanes]
            y_ref[r0:r0 + TAP_ROWS, lanes] = acc
        yh, _ = _layer_norm_stats(y_ref[...])
        t = yh * lg_ref[...] + lb_ref[...]
        o_ref[...] = (t * _sigmoid(t)).astype(o_ref.dtype)

    def cur(c):
        return pl.BlockSpec((CONV_ROWS, TOK_WIDTH), lambda b, i: (b * nt + i, c))

    def prev(c):
        return pl.BlockSpec((CONV_HALO, TOK_WIDTH), lambda b, i: (jnp.maximum((b * nt + i) * sub - 1, 0), c))

    vec = pl.BlockSpec((1, TOK_WIDTH), lambda b, i: (0, 0))
    return pl.pallas_call(
        body, out_shape=(jax.ShapeDtypeStruct((n, D_MODEL), BF16), jax.ShapeDtypeStruct((n, TOK_WIDTH), F32)),
        grid=(batch, nt),
        in_specs=[cur(0), cur(1), prev(0), prev(1), pl.BlockSpec((32, TOK_WIDTH), lambda b, i: (0, 0)), vec, vec, vec],
        out_specs=(cur(0), cur(0)),
        scratch_shapes=[pltpu.VMEM((CONV_WIN, TOK_WIDTH), F32), pltpu.VMEM((SUBLANES - 1, SHIFT_ROWS, TOK_WIDTH), F32)],
        compiler_params=_params("parallel", "arbitrary"), name="conv_fwd")(z, z, z, z, cw, cb, lg, lb)


def conv_bwd(z, y, dcat, cw, lg, lb, batch, seq):
    n = z.shape[0]
    nt = seq // CONV_ROWS
    sub = CONV_ROWS // CONV_HALO
    lead = CONV_HALO - (CONV_W - 1)
    last_blk = n // CONV_HALO - 1

    def body(a_ref, g_ref, ap_ref, gp_ref, y_ref, yn_ref, do_ref, don_ref, cw_ref, lg_ref, lb_ref,
             dz_ref, dcw_ref, dsm_ref, win, shifted, dyw, dshifted, dg_o):
        b, i, which = pl.program_id(0), pl.program_id(1), pl.program_id(2)

        @pl.when(which == 0)
        def _():
            first, last = i == 0, i == nt - 1

            @pl.when((b == 0) & (i == 0))
            def _():
                dcw_ref[...] = jnp.zeros_like(dcw_ref)
                dsm_ref[...] = jnp.zeros_like(dsm_ref)

            win[0:CONV_HALO, :] = jnp.where(first, 0.0, _glu(ap_ref, gp_ref))
            win[CONV_HALO:CONV_WIN, :] = _glu(a_ref, g_ref)
            _preshift(win, shifted)
            yv = jnp.concatenate([y_ref[...], yn_ref[...]], axis=0)
            yh, rstd = _layer_norm_stats(yv)
            t = yh * lg_ref[...] + lb_ref[...]
            st = _sigmoid(t)
            dout = jnp.concatenate(
                [do_ref[...].astype(F32), jnp.where(last, 0.0, don_ref[...].astype(F32))], axis=0)
            dt = dout * st * (1.0 + t * (1.0 - st))
            dyh = dt * lg_ref[...]
            dy = rstd * (dyh - jnp.mean(dyh, axis=-1, keepdims=True)
                         - yh * jnp.mean(dyh * yh, axis=-1, keepdims=True))
            dyw[...] = dy
            _preshift(dyw, dshifted)
            dsm_ref[0:1, :] += jnp.sum(dy[0:CONV_ROWS], axis=0, keepdims=True)
            dsm_ref[1:2, :] += jnp.sum((dt * yh)[0:CONV_ROWS], axis=0, keepdims=True)
            dsm_ref[2:3, :] += jnp.sum(dt[0:CONV_ROWS], axis=0, keepdims=True)
            for c0 in range(0, TOK_WIDTH, LANES):
                lanes = slice(c0, c0 + LANES)
                dcw_acc = [jnp.zeros((SUBLANES, LANES), F32) for _ in range(CONV_W)]
                for r0 in range(0, CONV_ROWS, TAP_ROWS):
                    dyt = dyw[r0:r0 + TAP_ROWS, lanes]
                    dglu = jnp.zeros((TAP_ROWS, LANES), F32)
                    for w in range(CONV_W):
                        dcw_acc[w] = dcw_acc[w] + _fold_rows(dyt * _tap(win, shifted, lead + w, r0, lanes))
                        dglu = dglu + _tap(dyw, dshifted, CONV_W - 1 - w, r0, lanes) * cw_ref[w:w + 1, lanes]
                    avt = a_ref[r0:r0 + TAP_ROWS, lanes].astype(F32)
                    sgt = _sigmoid(g_ref[r0:r0 + TAP_ROWS, lanes].astype(F32))
                    dz_ref[r0:r0 + TAP_ROWS, lanes] = (dglu * sgt).astype(dz_ref.dtype)
                    dg_o[r0:r0 + TAP_ROWS, lanes] = (dglu * avt * sgt * (1.0 - sgt)).astype(dg_o.dtype)
                for w in range(CONV_W):
                    dcw_ref[w:w + 1, lanes] += jnp.sum(dcw_acc[w], axis=0, keepdims=True)

        @pl.when(which == 1)
        def _():
            dz_ref[...] = dg_o[...]

    def ahead(b, i, t):
        return jnp.minimum(b * nt + i + t, batch * nt - 1)

    def cur(c):
        return pl.BlockSpec((CONV_ROWS, TOK_WIDTH), lambda b, i, t: (ahead(b, i, t), c))

    def prev(c):
        return pl.BlockSpec((CONV_HALO, TOK_WIDTH), lambda b, i, t: (jnp.maximum(ahead(b, i, t) * sub - 1, 0), c))

    nxt = pl.BlockSpec((CONV_HALO, TOK_WIDTH),
                       lambda b, i, t: (jnp.minimum((ahead(b, i, t) + 1) * sub, last_blk), 0))
    vec = pl.BlockSpec((1, TOK_WIDTH), lambda b, i, t: (0, 0))
    full32 = pl.BlockSpec((32, TOK_WIDTH), lambda b, i, t: (0, 0))
    return pl.pallas_call(
        body,
        out_shape=(jax.ShapeDtypeStruct(z.shape, BF16), jax.ShapeDtypeStruct((32, TOK_WIDTH), F32),
                   jax.ShapeDtypeStruct((8, TOK_WIDTH), F32)),
        grid=(batch, nt, 2),
        in_specs=[cur(0), cur(1), prev(0), prev(1), cur(0), nxt, cur(0), nxt, full32, vec, vec],
        out_specs=(pl.BlockSpec((CONV_ROWS, TOK_WIDTH), lambda b, i, t: (b * nt + i, t)), full32,
                   pl.BlockSpec((8, TOK_WIDTH), lambda b, i, t: (0, 0))),
        scratch_shapes=[pltpu.VMEM((CONV_WIN, TOK_WIDTH), F32), pltpu.VMEM((SUBLANES - 1, SHIFT_ROWS, TOK_WIDTH), F32),
                        pltpu.VMEM((CONV_WIN, TOK_WIDTH), F32), pltpu.VMEM((SUBLANES - 1, SHIFT_ROWS, TOK_WIDTH), F32),
                        pltpu.VMEM((CONV_ROWS, TOK_WIDTH), BF16)],
        compiler_params=_params("arbitrary", "arbitrary", "arbitrary"), name="conv_bwd")(
            z, z, z, z, y, y, dcat, dcat, cw, lg, lb)


def _place():
    return lax.axis_index("x"), lax.axis_index("y"), lax.axis_index("c")


def _other_chips(x, y):
    return [(1 - x, y), (x, 1 - y), (1 - x, 1 - y)]


def reduce_small(arrays):
    na = len(arrays)

    def body(*refs):
        ins, outs, bufs = refs[:na], refs[na:2 * na], refs[2 * na:3 * na]
        send_sems, recv_sems = refs[3 * na:]
        x, y, c = _place()
        me = 4 * x + 2 * y + c
        copies = []
        for a in range(na):
            bufs[a][me] = ins[a][...]
            for k in range(1, N_DEV):
                cp = pltpu.make_async_remote_copy(
                    src_ref=ins[a], dst_ref=bufs[a].at[me], send_sem=send_sems.at[a, k - 1],
                    recv_sem=recv_sems.at[a, k - 1],
                    device_id=(x ^ (k >> 2), y ^ ((k >> 1) & 1), c ^ (k & 1)), device_id_type=MESH)
                cp.start()
                copies.append(cp)
        for a in range(na):
            for k in range(1, N_DEV):
                src = 4 * (x ^ (k >> 2)) + 2 * (y ^ ((k >> 1) & 1)) + (c ^ (k & 1))
                pltpu.make_async_remote_copy(
                    src_ref=ins[a], dst_ref=bufs[a].at[src], send_sem=send_sems.at[a, k - 1],
                    recv_sem=recv_sems.at[a, k - 1], device_id=(x, y, c), device_id_type=MESH).wait_recv()
        for cp in copies:
            cp.wait_send()
        for a in range(na):
            total = bufs[a][0]
            for dev in range(1, N_DEV):
                total = total + bufs[a][dev]
            outs[a][...] = total

    vmem = pl.BlockSpec(memory_space=pltpu.VMEM)
    return pl.pallas_call(
        body, out_shape=tuple(jax.ShapeDtypeStruct(a.shape, F32) for a in arrays),
        in_specs=[vmem] * na, out_specs=tuple([vmem] * na),
        scratch_shapes=[pltpu.VMEM((N_DEV,) + a.shape, F32) for a in arrays]
        + [pltpu.SemaphoreType.DMA((na, N_DEV - 1)), pltpu.SemaphoreType.DMA((na, N_DEV - 1))],
        compiler_params=pltpu.CompilerParams(vmem_limit_bytes=VMEM_LIMIT), name="small_reduce")(*arrays)


def adamw_small(ws, gs, ms, vs):
    na = len(ws)
    c1 = 1.0 / (1.0 - ADAM_B1 ** ADAM_STEP)
    c2 = 1.0 / (1.0 - ADAM_B2 ** ADAM_STEP)

    def body(*refs):
        w_refs, g_refs, m_refs, v_refs = (refs[i * na:(i + 1) * na] for i in range(4))
        d_refs, nm_refs, nv_refs = (refs[(4 + i) * na:(5 + i) * na] for i in range(3))
        for a in range(na):
            gv = g_refs[a][...]
            nm = ADAM_B1 * m_refs[a][...] + (1.0 - ADAM_B1) * gv
            nv = ADAM_B2 * v_refs[a][...] + (1.0 - ADAM_B2) * (gv * gv)
            nm_refs[a][...] = nm
            nv_refs[a][...] = nv
            d_refs[a][...] = -ADAM_LR * ((nm * c1) / (jnp.sqrt(nv * c2) + ADAM_EPS) + ADAM_WD * w_refs[a][...])

    vmem = pl.BlockSpec(memory_space=pltpu.VMEM)
    shapes = tuple(jax.ShapeDtypeStruct(w.shape, F32) for w in ws)
    outs = pl.pallas_call(
        body, out_shape=shapes * 3, in_specs=[vmem] * (4 * na), out_specs=tuple([vmem] * (3 * na)),
        compiler_params=pltpu.CompilerParams(vmem_limit_bytes=VMEM_LIMIT), name="adamw_small")(*ws, *gs, *ms, *vs)
    return outs[:na], outs[na:2 * na], outs[2 * na:]


def gather_weights(shards, name, collective_id):
    nw = len(shards)
    ns = [s.shape[0] for s in shards]
    in_refs = [jax.new_ref(s, memory_space=pltpu.MemorySpace.HBM) for s in shards]
    out_refs = [jax.empty_ref(jax.ShapeDtypeStruct((N_DEV * s.shape[0], s.shape[1]), s.dtype),
                              memory_space=pltpu.MemorySpace.HBM) for s in shards]

    @pl.kernel(mesh=plsc.ScalarSubcoreMesh(axis_name="seq", num_cores=1), name=name,
               scratch_types=(pltpu.SemaphoreType.DMA((nw, 7)), pltpu.SemaphoreType.DMA((nw, 7)),
                              pltpu.SemaphoreType.DMA((nw,))),
               compiler_params=pltpu.CompilerParams(collective_id=collective_id))
    def launch(send_sems, recv_sems, local_sems):
        x, y, c = _place()
        me, sib = (x, y, c), (x, y, 1 - c)
        chips = _other_chips(x, y)
        barrier = pltpu.get_barrier_semaphore()
        for peer in [sib] + [(*chip, c) for chip in chips]:
            pl.semaphore_signal(barrier, inc=1, device_id=peer, device_id_type=MESH)
        pl.semaphore_wait(barrier, 4)

        def rows(w, dev):
            return out_refs[w].at[pl.ds((4 * dev[0] + 2 * dev[1] + dev[2]) * ns[w], ns[w]), :]

        def copy(w, k, block, to, src=None):
            return pltpu.make_async_remote_copy(
                src_ref=rows(w, block) if src is None else src, dst_ref=rows(w, block),
                send_sem=send_sems.at[w, k], recv_sem=recv_sems.at[w, k], device_id=to, device_id_type=MESH)

        started, sends = [], []
        for w in range(nw):
            mine = pltpu.make_async_copy(in_refs[w], rows(w, me), local_sems.at[w])
            mine.start()
            started.append(mine)
            first = [copy(w, 0, me, sib, src=in_refs[w])]
            first += [copy(w, 1 + j, me, (*chip, c), src=in_refs[w]) for j, chip in enumerate(chips)]
            for cp in first:
                cp.start()
            sends += first
        for w in range(nw):
            for j, chip in enumerate(chips):
                copy(w, 1 + j, (*chip, c), me).wait_recv()
                fwd = copy(w, 4 + j, (*chip, c), sib)
                fwd.start()
                sends.append(fwd)
        for w in range(nw):
            copy(w, 0, sib, me).wait_recv()
            for j, chip in enumerate(chips):
                copy(w, 4 + j, (*chip, 1 - c), me).wait_recv()
        for cp in sends:
            cp.wait_send()
        for mine in started:
            mine.wait()

    launch()
    return [r[...] for r in out_refs]


def _sequencer_exchange(sources, out_rows, peers_of, copies_of, name, collective_id):
    nw = len(sources)
    in_refs = [jax.new_ref(s, memory_space=pltpu.MemorySpace.HBM) for s in sources]
    out_refs = [jax.empty_ref(jax.ShapeDtypeStruct((rows, s.shape[1]), s.dtype), memory_space=pltpu.MemorySpace.HBM)
                for rows, s in zip(out_rows, sources)]
    per = len(copies_of(0, 0, 0, 0))

    @pl.kernel(mesh=plsc.ScalarSubcoreMesh(axis_name="seq", num_cores=1), name=name,
               scratch_types=(pltpu.SemaphoreType.DMA((nw, per)), pltpu.SemaphoreType.DMA((nw, per))),
               compiler_params=pltpu.CompilerParams(collective_id=collective_id))
    def launch(send_sems, recv_sems):
        x, y, c = _place()
        peers = peers_of(x, y, c)
        barrier = pltpu.get_barrier_semaphore()
        for peer in peers:
            pl.semaphore_signal(barrier, inc=1, device_id=peer, device_id_type=MESH)
        pl.semaphore_wait(barrier, len(peers))
        copies = []
        for w in range(nw):
            for k, (src_blk, dst_blk, rows, peer) in enumerate(copies_of(x, y, c, w)):
                cp = pltpu.make_async_remote_copy(
                    src_ref=in_refs[w].at[pl.ds(src_blk * rows, rows), :],
                    dst_ref=out_refs[w].at[pl.ds(dst_blk * rows, rows), :],
                    send_sem=send_sems.at[w, k], recv_sem=recv_sems.at[w, k], device_id=peer, device_id_type=MESH)
                cp.start()
                copies.append(cp)
        for cp in copies:
            cp.wait_recv()
        for cp in copies:
            cp.wait_send()

    launch()
    return [r[...] for r in out_refs]


def scatter_to_sibling(grads, name, collective_id):
    ns = [g.shape[0] // N_DEV for g in grads]
    return _sequencer_exchange(
        grads, [4 * n for n in ns],
        lambda x, y, c: [(x, y, 1 - c)],
        lambda x, y, c, w: [(2 * q + 1 - c, q, ns[w], (x, y, 1 - c)) for q in range(4)],
        name, collective_id)


def scatter_to_chips(parts, name, collective_id):
    ns = [p.shape[0] // 4 for p in parts]
    return _sequencer_exchange(
        parts, [3 * n for n in ns],
        lambda x, y, c: [(*chip, c) for chip in _other_chips(x, y)],
        lambda x, y, c, w: [(2 * chip[0] + chip[1], j, ns[w], (*chip, c)) for j, chip in enumerate(_other_chips(x, y))],
        name, collective_id)


def add_sibling(grads, landeds, core, name):
    nw = len(grads)

    def body(c_ref, *refs):
        for w in range(nw):
            g_ref, l_ref, o_ref = refs[2 * w], refs[2 * w + 1], refs[2 * nw + w]
            o_ref[...] = (g_ref[...].astype(F32) + l_ref[...].astype(F32)).astype(o_ref.dtype)

    in_specs, out_specs, args = [], [], []
    for g, ld in zip(grads, landeds):
        n, cols = ld.shape[0] // 4, g.shape[1]
        in_specs += [pl.BlockSpec((n, cols), lambda q, c_ref: (2 * q + c_ref[0], 0)),
                     pl.BlockSpec((n, cols), lambda q, c_ref: (q, 0))]
        out_specs.append(pl.BlockSpec((n, cols), lambda q, c_ref: (q, 0)))
        args += [g, ld]
    grid_spec = pltpu.PrefetchScalarGridSpec(
        num_scalar_prefetch=1, grid=(4,), in_specs=in_specs, out_specs=tuple(out_specs))
    return pl.pallas_call(
        body, out_shape=tuple(jax.ShapeDtypeStruct(ld.shape, ld.dtype) for ld in landeds), grid_spec=grid_spec,
        compiler_params=_params("arbitrary"), name=name)(core, *args)


ADAMW_HALVES = 2


def adamw_shards(items, chip, name):
    c1 = 1.0 / (1.0 - ADAM_B1 ** ADAM_STEP)
    c2 = 1.0 / (1.0 - ADAM_B2 ** ADAM_STEP)
    ni = len(items)

    def body(q_ref, *refs):
        outs = refs[len(refs) - 4 * ni:]
        for k in range(ni):
            w_ref, m_ref, v_ref, p_ref, l0_ref, l1_ref, l2_ref = refs[7 * k:7 * k + 7]
            g_ref, d_ref, nm_ref, nv_ref = outs[4 * k:4 * k + 4]
            gv = ((p_ref[...].astype(F32) + l0_ref[...].astype(F32)) + l1_ref[...].astype(F32)) + l2_ref[...].astype(F32)
            nm = ADAM_B1 * m_ref[...] + (1.0 - ADAM_B1) * gv
            nv = ADAM_B2 * v_ref[...] + (1.0 - ADAM_B2) * (gv * gv)
            g_ref[...] = gv
            nm_ref[...] = nm
            nv_ref[...] = nv
            d_ref[...] = -ADAM_LR * ((nm * c1) / (jnp.sqrt(nv * c2) + ADAM_EPS) + ADAM_WD * w_ref[...])

    sub = ADAMW_HALVES
    in_specs, out_specs, out_shape, args, donated = [], [], [], [chip], []
    for layer, w, m, v, part, landed, earlier in items:
        rows, cols = landed.shape[0] // (3 * sub), w.shape[1]

        def block(first, rows=rows, cols=cols):
            return pl.BlockSpec((rows, cols), lambda i, q_ref: (first(q_ref) * sub + i, 0))

        own = block(lambda q_ref, layer=layer: layer)
        in_specs += [own, own, own, block(lambda q_ref: q_ref[0])] + [block(lambda q_ref, j=j: j) for j in range(3)]
        args += [w, m, v, part, landed, landed, landed]
        out_specs += [own] * 4
        out_shape += [jax.ShapeDtypeStruct(w.shape, F32)] * 4
        donated.append(earlier)
    aliases = {}
    for k, earlier in enumerate(donated):
        if earlier is not None:
            for j in range(4):
                aliases[len(args)] = 4 * k + j
                in_specs.append(ANY)
                args.append(earlier[j])
    grid_spec = pltpu.PrefetchScalarGridSpec(
        num_scalar_prefetch=1, grid=(sub,), in_specs=in_specs, out_specs=tuple(out_specs))
    outs = pl.pallas_call(
        body, out_shape=tuple(out_shape), grid_spec=grid_spec, input_output_aliases=aliases,
        compiler_params=_params("arbitrary"), name=name)(*args)
    return [tuple(outs[4 * k:4 * k + 4]) for k in range(ni)]


def _pack(arrays):
    flat = jnp.concatenate([a.reshape(-1).astype(F32) for a in arrays])
    pad = (-flat.shape[0]) % (8 * LANES)
    return jnp.pad(flat, (0, pad)).reshape(-1, LANES)


def _unpack(slab, shapes):
    flat = slab.reshape(slab.shape[:-2] + (-1,))
    out, off = [], 0
    for shp in shapes:
        size = 1
        for s in shp:
            size *= s
        out.append(flat[..., off:off + size].reshape(flat.shape[:-1] + tuple(shp)))
        off += size
    return out


def kernel(x, mem, norm1_g, mem_norm_g, a_w_in, a_q_g, a_k_g, a_rel_bias, b_w_in, b_b_in, b_conv_w, b_conv_b, b_ln_g, b_ln_b, mq_g, mk_g, w_mem_kv, w_out, norm2_g, w_gate, w_up, w_down, loss_target, m_norm1_g, m_mem_norm_g, m_a_w_in, m_a_q_g, m_a_k_g, m_a_rel_bias, m_b_w_in, m_b_b_in, m_b_conv_w, m_b_conv_b, m_b_ln_g, m_b_ln_b, m_mq_g, m_mk_g, m_w_mem_kv, m_w_out, m_norm2_g, m_w_gate, m_w_up, m_w_down, v_norm1_g, v_mem_norm_g, v_a_w_in, v_a_q_g, v_a_k_g, v_a_rel_bias, v_b_w_in, v_b_b_in, v_b_conv_w, v_b_conv_b, v_b_ln_g, v_b_ln_b, v_mq_g, v_mk_g, v_w_mem_kv, v_w_out, v_norm2_g, v_w_gate, v_w_up, v_w_down):
    batch, seq, d = x.shape
    mtok = mem.shape[1]
    n = batch * seq
    ax, ay, ac = _place()
    me = 4 * ax + 2 * ay + ac
    core_arr = jnp.reshape(ac, (1,)).astype(jnp.int32)
    chip_arr = jnp.reshape(2 * ax + ay, (1,)).astype(jnp.int32)

    def t_bf16(w):
        return jnp.transpose(w).astype(BF16)

    def after(value, *earlier):
        return lax.optimization_barrier((value, *earlier))[0]

    def gather_mix(l, when, name, collective_id):
        srcs = [w_mem_kv[l].astype(BF16), w_out[l].astype(BF16)]
        if l == 1:
            srcs += [t_bf16(b_w_in[0]), _pack([b_b_in, b_conv_w, b_conv_b, b_ln_g, b_ln_b])]
        return gather_weights([after(srcs[0], *when)] + srcs[1:], name, collective_id)

    def gather_ffn(l, when, name, collective_id):
        return gather_weights(
            [after(t_bf16(w_gate[l]), *when), t_bf16(w_up[l]), w_down[l].astype(BF16)], name, collective_id)

    f_loc = b_b_in.shape[1]
    c_loc = b_conv_b.shape[1]

    def two(g):
        return jnp.concatenate([g, g], axis=-1)

    gq2, gk2 = two(a_q_g), two(a_k_g)
    rel16 = jnp.pad(a_rel_bias[0], ((0, 16 - a_rel_bias.shape[1]), (0, 0)))
    bias = bias_blocks(rel16)

    x0 = x.reshape(n, d)
    mem2 = mem.reshape(batch * mtok, d)

    saved = []
    xin = x0
    a_win_t, = gather_weights([t_bf16(a_w_in[0])], "gather_in_a", 1)
    wg_t, wu_t, wd, wo, wkv = [None] * 2, [None] * 2, [None] * 2, [None] * 2, [None] * 2
    h = after(rms_fwd(xin, norm1_g[0:1], name="rms1_fwd_0"), bias)
    target = loss_target.reshape(n, d)
    for l in range(2):
        gq4 = jnp.tile(mq_g[l:l + 1], (1, 4))
        gk4 = jnp.tile(mk_g[l:l + 1], (1, 4))
        y_conv = None
        if l == 0:
            wkv[0], wo[0] = gather_mix(0, (h, a_win_t), "gather_mix_a", 2)
            z = mm_nt(h, a_win_t, name="in_proj_a")
            wg_t[0], wu_t[0], wd[0] = gather_ffn(0, (z, wkv[0]), "gather_ffn_a", 3)
            cat = attn_fwd(z, gq2, gk2, bias, batch, seq)
            wkv[1], wo[1], b_win_t, conv_slabs = gather_mix(1, (cat, wg_t[0]), "gather_mix_b", 4)
            qcol = 3 * TOK_WIDTH // MEM_WIDTH
        else:
            small_shapes = [(f_loc,), (CONV_W, c_loc), (c_loc,), (c_loc,), (c_loc,)]
            bb_g, cw_g, cb_g, lg_g, lb_g = _unpack(conv_slabs.reshape(N_DEV, -1, LANES), small_shapes)
            bb_full = bb_g.reshape(1, -1)
            cw_full = jnp.pad(jnp.transpose(cw_g, (1, 0, 2)).reshape(CONV_W, -1), ((0, 32 - CONV_W), (0, 0)))
            cb_full, lg_full, lb_full = cb_g.reshape(1, -1), lg_g.reshape(1, -1), lb_g.reshape(1, -1)
            z = mm_nt(h, b_win_t, bias=bb_full, name="in_proj_b")
            cat, y_conv = conv_fwd(z, cw_full, cb_full, lg_full, lb_full, batch, seq)
            qcol = 2 * TOK_WIDTH // MEM_WIDTH
        cat, mem_n, kv = memattn_fwd(
            z, mem2, mem_norm_g[l:l + 1], wkv[l], gq4, gk4, cat, batch, seq, qcol, name=f"memattn_fwd_{l}")
        x1, h2 = proj_norm(cat, wo[l], xin, norm2_g[l:l + 1], name=f"out_proj_{l}")
        if l == 0:
            wg_t[1], wu_t[1], wd[1] = gather_ffn(1, (x1, b_win_t), "gather_ffn_b", 5)
        if l == 0:
            gate, up, act, x2, h_next = ffn_fwd(h2, wg_t[0], wu_t[0], wd[0], x1, gain=norm1_g[1:2], name="ffn_fwd_0")
        else:
            gate, up, act, dx_b, loss_blk = ffn_fwd(h2, wg_t[1], wu_t[1], wd[1], x1, target=target, name="ffn_fwd_1")
        saved.append(dict(xin=xin, h=h, mem_n=mem_n, kv=kv, gq4=gq4, gk4=gk4, z=z, qcol=qcol, cat=cat, x1=x1, h2=h2,
                          gate=gate, up=up, act=act, y_conv=y_conv))
        if l == 0:
            xin, h = x2, h_next

    big = {}
    small = {}
    reduced = {}
    groups = 0

    def scatter_siblings(keys):
        nonlocal groups
        gid = groups
        groups += 1
        return gid, keys, scatter_to_sibling([big[k] for k in keys], f"scatter_sibling_{gid}", 8 + 2 * gid)

    def scatter_chips(stage1, when):
        gid, keys, landed1 = stage1
        parts = add_sibling([after(big[keys[0]], when)] + [big[k] for k in keys[1:]], landed1, core_arr,
                            name=f"add_sibling_{gid}")
        landed2 = scatter_to_chips(parts, f"scatter_chips_{gid}", 9 + 2 * gid)
        for k, p, ld in zip(keys, parts, landed2):
            reduced[k] = (p, ld)
        return parts, landed2

    def rows_of(w, transposed):
        w = jnp.swapaxes(w, 1, 2) if transposed else w
        return w.reshape(w.shape[0] * w.shape[1], w.shape[2])

    sharded = {
        "win0": (2, True), "win1": (6, True), "wkv": (14, False), "wo": (15, False),
        "wg": (17, True), "wu": (18, True), "wd": (19, False)}
    weights = [norm1_g, mem_norm_g, a_w_in, a_q_g, a_k_g, a_rel_bias, b_w_in, b_b_in, b_conv_w, b_conv_b, b_ln_g,
               b_ln_b, mq_g, mk_g, w_mem_kv, w_out, norm2_g, w_gate, w_up, w_down]
    moms = [m_norm1_g, m_mem_norm_g, m_a_w_in, m_a_q_g, m_a_k_g, m_a_rel_bias, m_b_w_in, m_b_b_in, m_b_conv_w,
            m_b_conv_b, m_b_ln_g, m_b_ln_b, m_mq_g, m_mk_g, m_w_mem_kv, m_w_out, m_norm2_g, m_w_gate, m_w_up, m_w_down]
    vels = [v_norm1_g, v_mem_norm_g, v_a_w_in, v_a_q_g, v_a_k_g, v_a_rel_bias, v_b_w_in, v_b_b_in, v_b_conv_w,
            v_b_conv_b, v_b_ln_g, v_b_ln_b, v_mq_g, v_mk_g, v_w_mem_kv, v_w_out, v_norm2_g, v_w_gate, v_w_up, v_w_down]
    updated = {}

    def update_layer(l, when):
        for group, keys in (("ffn", ("wg", "wu", "wd")), ("mix", (f"win{l}", "wkv", "wo"))):
            items = []
            for key in keys:
                idx, transposed = sharded[key]
                layer, rkey = (0, key) if key.startswith("win") else (l, f"{key}{l}")
                part, landed = reduced[rkey]
                w_rows = rows_of(weights[idx], transposed)
                items.append((layer, after(w_rows, when) if not items else w_rows, rows_of(moms[idx], transposed),
                              rows_of(vels[idx], transposed), part, landed, updated.get(key)))
            for key, result in zip(keys, adamw_shards(items, chip_arr, name=f"adamw_{group}_{l}")):
                updated[key] = result

    mix_landed = None
    for l in (1, 0):
        sv = saved[l]
        dgate, dup, dx1_b, dcat, small[f"norm2_{l}"] = ffn_bwd(
            dx_b, wd[l], sv["gate"], sv["up"], wg_t[l], wu_t[l], sv["x1"], norm2_g[l:l + 1], wo[l], name=f"ffn_bwd_{l}")
        if l == 0:
            dgate = after(dgate, *mix_landed)
            update_layer(1, dx1_b)
        big[f"wg{l}"], big[f"wu{l}"], big[f"wd{l}"] = ffn_weight_grads(
            dgate, dup, sv["h2"], sv["act"], dx_b, name=f"grad_ffn_{l}")
        stage1 = scatter_siblings([f"wd{l}", f"wg{l}", f"wu{l}"])
        big[f"wo{l}"] = mm_tn(sv["cat"], dx1_b, name=f"grad_wo_{l}")
        parts, ffn_landed = scatter_chips(stage1, big[f"wo{l}"])
        dcat = after(dcat, *parts)
        if l == 0:
            dq, dk, dv, dbias, small["a_q"], small["a_k"] = attn_bwd(sv["z"], dcat, gq2, gk2, bias, batch, seq)
            small["rel"] = bias_grad(dbias)
            win_t = a_win_t
            dz = None
            dcat = after(dcat, dq, *ffn_landed)
        else:
            dz, small["cw"], small["csum"] = conv_bwd(sv["z"], sv["y_conv"], dcat, cw_full, lg_full, lb_full, batch, seq)
            win_t = b_win_t
            dz = after(dz, *ffn_landed)
        dqm, small[f"mq_{l}"], small[f"mk_{l}"], big[f"wkv{l}"], small[f"memnorm_{l}"] = memattn_bwd(
            sv["z"], sv["kv"], dcat, sv["gq4"], sv["gk4"], mem2, sv["mem_n"], wkv[l], dz, batch, seq, sv["qcol"],
            name=f"memattn_bwd_{l}")
        if l == 0:
            pieces = [dq, dk, dv, dqm]
            big["win0"] = grad_pieces(pieces, sv["h"], name="grad_win_0")
        else:
            pieces = [dqm]
            big["win1"] = mm_tn(dqm, sv["h"], name="grad_win_1")
        stage1 = scatter_siblings([f"win{l}", f"wkv{l}", f"wo{l}"])
        dx_b, small[f"norm1_{l}"], dz_sum = in_proj_bwd(
            pieces, win_t, sv["xin"], norm1_g[l:l + 1], dx1_b, BF16 if l == 1 else F32, name=f"in_proj_bwd_{l}")
        if l == 1:
            small["bb"] = dz_sum
        parts, mix_landed = scatter_chips(stage1, dx_b)
        dx_b = after(dx_b, *parts)
    grad_x = dx_b.reshape(batch, seq, d)
    update_layer(0, dx_b)

    def shaped(rows, idx, transposed):
        shp = weights[idx].shape
        if transposed:
            return jnp.swapaxes(rows.reshape(shp[0], shp[2], shp[1]), 1, 2)
        return rows.reshape(shp)

    def fold(v, groups):
        return jnp.sum(v.reshape(groups, HEAD_DIM), axis=0, keepdims=True)

    heads = a_rel_bias.shape[1]
    small_list = [
        jnp.concatenate([small["norm1_0"], small["norm1_1"]]),
        jnp.concatenate([small["memnorm_0"], small["memnorm_1"]]),
        fold(small["a_q"], 2), fold(small["a_k"], 2), small["rel"][:heads],
        small["bb"], small["cw"][:CONV_W], small["csum"][0:1], small["csum"][1:2], small["csum"][2:3],
        jnp.concatenate([fold(small["mq_0"], 4), fold(small["mq_1"], 4)]),
        jnp.concatenate([fold(small["mk_0"], 4), fold(small["mk_1"], 4)]),
        jnp.concatenate([small["norm2_0"], small["norm2_1"]]),
    ]
    (g_norm1, g_memnorm, g_aq, g_ak, g_rel, g_bb_full, g_cw_full, g_cb_full, g_lg_full, g_lb_full,
     g_mq, g_mk, g_norm2, loss_sum) = reduce_small(small_list + [loss_blk])
    loss = loss_sum[0, 0]
    g_bb = lax.dynamic_slice_in_dim(g_bb_full, me * f_loc, f_loc, axis=1)
    g_cw = lax.dynamic_slice_in_dim(g_cw_full, me * c_loc, c_loc, axis=1)
    g_cb = lax.dynamic_slice_in_dim(g_cb_full, me * c_loc, c_loc, axis=1)
    g_lg = lax.dynamic_slice_in_dim(g_lg_full, me * c_loc, c_loc, axis=1)
    g_lb = lax.dynamic_slice_in_dim(g_lb_full, me * c_loc, c_loc, axis=1)

    grads = [g_norm1, g_memnorm, None, g_aq, g_ak, g_rel, None, g_bb, g_cw, g_cb, g_lg, g_lb,
             g_mq, g_mk, None, None, g_norm2, None, None, None]
    deltas, new_m, new_v = [None] * 20, [None] * 20, [None] * 20
    for key, (idx, transposed) in sharded.items():
        grads[idx], deltas[idx], new_m[idx], new_v[idx] = (shaped(r, idx, transposed) for r in updated[key])

    def flat2(a):
        return a.reshape(a.shape[-2:])

    small_idx = [i for i in range(20) if i not in {idx for idx, _ in sharded.values()}]
    dl, nm, nv = adamw_small([flat2(weights[i]) for i in small_idx], [flat2(grads[i]) for i in small_idx],
                             [flat2(moms[i]) for i in small_idx], [flat2(vels[i]) for i in small_idx])
    for i, a, b, cc in zip(small_idx, dl, nm, nv):
        shp = weights[i].shape
        grads[i], deltas[i], new_m[i], new_v[i] = grads[i].reshape(shp), a.reshape(shp), b.reshape(shp), cc.reshape(shp)

    return (loss, grad_x, *grads, *deltas, *new_m, *new_v)
```

```python
import jax
import jax.numpy as jnp
from jax import lax
from jax.experimental import pallas as pl
from jax.experimental.pallas import tpu as pltpu
from jax.experimental.pallas import tpu_sc as plsc

F32 = jnp.float32
BF16 = jnp.bfloat16
HIGHEST = lax.Precision.HIGHEST
MESH = pl.DeviceIdType.MESH
ANY = pl.BlockSpec(memory_space=pl.ANY)

N_DEV = 8
D_MODEL = 1024
HEAD_DIM = 64
TOK_WIDTH = 768
MEM_WIDTH = 256
CHUNK = 64
Q_BLOCK = 256
KEY_WIN = 768
BAND = 576
N_REL = 192
CONV_W = 31
CONV_HALO = 32
NORM_EPS = 1e-6
NEG_INF = -1e30
ATTN_SCALE = HEAD_DIM ** -0.5
LANES = 128
ROW_TILE = 512
VMEM_LIMIT = 56 * 1024 * 1024

ADAM_LR, ADAM_B1, ADAM_B2, ADAM_EPS, ADAM_WD, ADAM_STEP = 0.001, 0.9, 0.999, 1e-08, 0.01, 10


def _params(*sem):
    return pltpu.CompilerParams(dimension_semantics=sem, vmem_limit_bytes=VMEM_LIMIT)


WIDE_ROW_TILE = 1024


def _row_tile(m, rows=ROW_TILE):
    return rows if m % rows == 0 else m


def _col_tile(n, cap=1408):
    best = None
    for t in range(LANES, min(n, cap) + 1, LANES):
        if n % t == 0:
            best = t
    return best if best is not None else n


def _dot(a, b, ca, cb):
    return lax.dot_general(a, b, (((ca,), (cb,)), ((), ())), preferred_element_type=F32)


def _sigmoid(x):
    return 0.5 * jnp.tanh(0.5 * x) + 0.5


def mm_nt(a, b, bias=None, out_dtype=BF16, name="mm_nt"):
    m, k = a.shape
    n = b.shape[0]
    tm, tn = _row_tile(m, WIDE_ROW_TILE), _col_tile(n)

    def body(*refs):
        a_ref, b_ref = refs[0], refs[1]
        o_ref = refs[-1]
        acc = _dot(a_ref[...].astype(BF16), b_ref[...].astype(BF16), 1, 1)
        if bias is not None:
            acc = acc + refs[2][...]
        o_ref[...] = acc.astype(o_ref.dtype)

    in_specs = [pl.BlockSpec((tm, k), lambda j, i: (i, 0)), pl.BlockSpec((tn, k), lambda j, i: (j, 0))]
    args = [a, b]
    if bias is not None:
        in_specs.append(pl.BlockSpec((1, tn), lambda j, i: (0, j)))
        args.append(bias)
    return pl.pallas_call(
        body, out_shape=jax.ShapeDtypeStruct((m, n), out_dtype), grid=(n // tn, m // tm),
        in_specs=in_specs, out_specs=pl.BlockSpec((tm, tn), lambda j, i: (i, j)),
        compiler_params=_params("parallel", "arbitrary"), name=name)(*args)


def mm_tn(a, b, out_dtype=BF16, name="mm_tn"):
    t, r = a.shape
    c = b.shape[1]
    tr = _col_tile(r, 512)

    def body(a_ref, b_ref, o_ref):
        o_ref[...] = _dot(a_ref[...].astype(BF16), b_ref[...].astype(BF16), 0, 0).astype(o_ref.dtype)

    return pl.pallas_call(
        body, out_shape=jax.ShapeDtypeStruct((r, c), out_dtype), grid=(r // tr,),
        in_specs=[pl.BlockSpec((t, tr), lambda i: (0, i)), pl.BlockSpec((t, c), lambda i: (0, 0))],
        out_specs=pl.BlockSpec((tr, c), lambda i: (i, 0)),
        compiler_params=_params("parallel"), name=name)(a, b)


def _resident(shape):
    return pl.BlockSpec(shape, lambda i: (0, 0), pipeline_mode=pl.Buffered(1))


def proj_norm(a, b, res, gain, name):
    m, k = a.shape
    n = b.shape[1]
    tm = _row_tile(m)

    def body(a_ref, b_ref, res_ref, g_ref, x_ref, h_ref):
        xv = res_ref[...] + _dot(a_ref[...], b_ref[...], 1, 0)
        x_ref[...] = xv
        r = lax.rsqrt(jnp.mean(xv * xv, axis=-1, keepdims=True) + NORM_EPS)
        h_ref[...] = (xv * r * g_ref[...]).astype(BF16)

    row = pl.BlockSpec((tm, n), lambda i: (i, 0))
    return pl.pallas_call(
        body, out_shape=(jax.ShapeDtypeStruct((m, n), F32), jax.ShapeDtypeStruct((m, n), BF16)), grid=(m // tm,),
        in_specs=[pl.BlockSpec((tm, k), lambda i: (i, 0)), _resident((k, n)), row, _resident((1, n))],
        out_specs=(row, row), compiler_params=_params("parallel"), name=name)(a, b, res, gain)


def in_proj_bwd(pieces, w_t, x, gain, dres, out_dtype, name):
    m, n = x.shape
    k = pieces[0].shape[1]
    tm = _row_tile(m)
    npc = len(pieces)
    offs = [sum(p.shape[1] for p in pieces[:i]) for i in range(npc + 1)]

    def body(*refs):
        dz_refs = refs[:npc]
        w_ref, x_ref, g_ref, dres_ref, dx_ref, dg_ref, cs_ref = refs[npc:]

        @pl.when(pl.program_id(0) == 0)
        def _():
            dg_ref[...] = jnp.zeros_like(dg_ref)
            cs_ref[...] = jnp.zeros_like(cs_ref)

        cs_ref[...] += jnp.sum(dz_refs[0][...].astype(F32), axis=0, keepdims=True)
        dhv = _dot(dz_refs[0][...], w_ref[offs[0]:offs[1], :], 1, 0)
        for i in range(1, npc):
            dhv = dhv + _dot(dz_refs[i][...], w_ref[offs[i]:offs[i + 1], :], 1, 0)
        xv = x_ref[...]
        r = lax.rsqrt(jnp.mean(xv * xv, axis=-1, keepdims=True) + NORM_EPS)
        xhat = xv * r
        dg_ref[...] += jnp.sum(dhv * xhat, axis=0, keepdims=True)
        dxhat = dhv * g_ref[...]
        dx = dres_ref[...].astype(F32) + r * (dxhat - xhat * jnp.mean(dxhat * xhat, axis=-1, keepdims=True))
        dx_ref[...] = dx.astype(dx_ref.dtype)

    row = pl.BlockSpec((tm, n), lambda i: (i, 0))
    return pl.pallas_call(
        body, out_shape=(jax.ShapeDtypeStruct((m, n), out_dtype), jax.ShapeDtypeStruct((1, n), F32),
                         jax.ShapeDtypeStruct((1, k), F32)), grid=(m // tm,),
        in_specs=[pl.BlockSpec((tm, p.shape[1]), lambda i: (i, 0)) for p in pieces]
        + [_resident(w_t.shape), row, _resident((1, n)), row],
        out_specs=(row, pl.BlockSpec((1, n), lambda i: (0, 0)), pl.BlockSpec((1, k), lambda i: (0, 0))),
        compiler_params=_params("arbitrary"), name=name)(*pieces, w_t, x, gain, dres)


def grad_pieces(pieces, b, name):
    t, c = b.shape
    tr = 2 * LANES
    tiles = [p.shape[1] // tr for p in pieces]
    starts = [sum(tiles[:i]) for i in range(len(pieces) + 1)]

    def body(*refs):
        a_refs, b_ref, o_ref = refs[:len(pieces)], refs[len(pieces)], refs[len(pieces) + 1]
        i = pl.program_id(0)
        for p, a_ref in enumerate(a_refs):
            @pl.when((i >= starts[p]) & (i < starts[p + 1]))
            def _(a_ref=a_ref):
                o_ref[...] = _dot(a_ref[...], b_ref[...], 0, 0).astype(o_ref.dtype)

    def a_spec(p):
        return pl.BlockSpec((t, tr), lambda i: (0, jnp.clip(i - starts[p], 0, tiles[p] - 1)))

    return pl.pallas_call(
        body, out_shape=jax.ShapeDtypeStruct((starts[-1] * tr, c), BF16), grid=(starts[-1],),
        in_specs=[a_spec(p) for p in range(len(pieces))] + [_resident((t, c))],
        out_specs=pl.BlockSpec((tr, c), lambda i: (i, 0)),
        compiler_params=_params("arbitrary"), name=name)(*pieces, b)


FFN_ROWS = 256


def _ffn_row_tile(m):
    return FFN_ROWS if m % FFN_ROWS == 0 else m


def _fetch_weights(hbm_refs, vmem_refs, sems):
    copies = [pltpu.make_async_copy(h, v, sems.at[k]) for k, (h, v) in enumerate(zip(hbm_refs, vmem_refs))]
    first = pl.program_id(0) == 0

    @pl.when(first)
    def _():
        for cp in copies:
            cp.start()

    def ready(k):
        @pl.when(first)
        def _():
            copies[k].wait()

    return ready


def ffn_fwd(h2, wg_t, wu_t, wd, x1, gain=None, target=None, name="ffn_fwd"):
    n, d = h2.shape
    f = wg_t.shape[0]
    tm = _ffn_row_tile(n)
    nt = n // tm
    last = target is not None

    def body(h_ref, wg_hbm, wu_hbm, wd_hbm, x1_ref, e_ref, g_ref, u_ref, a_ref, *rest):
        wg_ref, wu_ref, wd_ref, sems = rest[-4:]
        rest = rest[:-4]
        ready = _fetch_weights([wg_hbm, wu_hbm, wd_hbm], [wg_ref, wu_ref, wd_ref], sems)
        hv = h_ref[...]
        ready(0)
        gv = _dot(hv, wg_ref[...], 1, 1)
        ready(1)
        uv = _dot(hv, wu_ref[...], 1, 1)
        g_ref[...] = gv.astype(BF16)
        u_ref[...] = uv.astype(BF16)
        av = (gv * _sigmoid(gv) * uv).astype(BF16)
        a_ref[...] = av
        ready(2)
        xv = x1_ref[...] + _dot(av, wd_ref[...], 1, 0)
        if not last:
            x_ref, hn_ref = rest
            x_ref[...] = xv
            r = lax.rsqrt(jnp.mean(xv * xv, axis=-1, keepdims=True) + NORM_EPS)
            hn_ref[...] = (xv * r * e_ref[...]).astype(BF16)
        else:
            dyb_ref, l_ref, acc_ref = rest
            i = pl.program_id(0)

            @pl.when(i == 0)
            def _():
                acc_ref[...] = jnp.zeros_like(acc_ref)

            err = xv - e_ref[...]
            dyb_ref[...] = (err * (1.0 / d)).astype(BF16)
            acc_ref[...] += jnp.sum(err * err, axis=0, keepdims=True)

            @pl.when(i == nt - 1)
            def _():
                total = jnp.sum(acc_ref[...], axis=-1, keepdims=True) * (0.5 / d)
                l_ref[...] = jnp.broadcast_to(total, l_ref.shape)

    row_d = pl.BlockSpec((tm, d), lambda i: (i, 0))
    row_f = pl.BlockSpec((tm, f), lambda i: (i, 0))
    act_shape = jax.ShapeDtypeStruct((n, f), BF16)
    if not last:
        extra_in, extra = _resident((1, d)), gain
        out_shape = (act_shape, act_shape, act_shape, jax.ShapeDtypeStruct((n, d), F32), jax.ShapeDtypeStruct((n, d), BF16))
        out_specs = (row_f, row_f, row_f, row_d, row_d)
        scratch = []
    else:
        extra_in, extra = row_d, target
        out_shape = (act_shape, act_shape, act_shape, jax.ShapeDtypeStruct((n, d), BF16),
                     jax.ShapeDtypeStruct((8, LANES), F32))
        out_specs = (row_f, row_f, row_f, row_d, pl.BlockSpec((8, LANES), lambda i: (0, 0)))
        scratch = [pltpu.VMEM((1, d), F32)]
    scratch += [pltpu.VMEM((f, d), BF16)] * 3 + [pltpu.SemaphoreType.DMA((3,))]
    return pl.pallas_call(
        body, out_shape=out_shape, grid=(nt,),
        in_specs=[row_d, ANY, ANY, ANY, row_d, extra_in],
        out_specs=out_specs, scratch_shapes=scratch,
        compiler_params=_params("arbitrary"), name=name)(h2, wg_t, wu_t, wd, x1, extra)


def ffn_bwd(dx_b, wd, gate, up, wg_t, wu_t, x1, gain, wo, name="ffn_bwd"):
    n, d = x1.shape
    f = wd.shape[0]
    tm = _ffn_row_tile(n)

    def body(dxb_ref, wd_hbm, g_ref, u_ref, wg_hbm, wu_hbm, x_ref, gain_ref, wo_ref,
             dg_ref, du_ref, dxo_ref, dc_ref, dgain_ref, wd_ref, wg_ref, wu_ref, sems):
        ready = _fetch_weights([wd_hbm, wg_hbm, wu_hbm], [wd_ref, wg_ref, wu_ref], sems)

        @pl.when(pl.program_id(0) == 0)
        def _():
            dgain_ref[...] = jnp.zeros_like(dgain_ref)

        ready(0)
        dact = _dot(dxb_ref[...], wd_ref[...], 1, 1)
        gv = g_ref[...].astype(F32)
        uv = u_ref[...].astype(F32)
        sg = _sigmoid(gv)
        dgv = (dact * uv * sg * (1.0 + gv * (1.0 - sg))).astype(BF16)
        duv = (dact * gv * sg).astype(BF16)
        dg_ref[...] = dgv
        du_ref[...] = duv
        ready(1)
        ready(2)
        dhv = _dot(dgv, wg_ref[...], 1, 0) + _dot(duv, wu_ref[...], 1, 0)
        xv = x_ref[...]
        r = lax.rsqrt(jnp.mean(xv * xv, axis=-1, keepdims=True) + NORM_EPS)
        xhat = xv * r
        dgain_ref[...] += jnp.sum(dhv * xhat, axis=0, keepdims=True)
        dxhat = dhv * gain_ref[...]
        dxb = (dxb_ref[...].astype(F32) + r * (dxhat - xhat * jnp.mean(dxhat * xhat, axis=-1, keepdims=True))).astype(BF16)
        dxo_ref[...] = dxb
        dc_ref[...] = _dot(dxb, wo_ref[...], 1, 1).astype(BF16)

    row_d = pl.BlockSpec((tm, d), lambda i: (i, 0))
    row_f = pl.BlockSpec((tm, f), lambda i: (i, 0))
    act_shape = jax.ShapeDtypeStruct((n, f), BF16)
    row_shape = jax.ShapeDtypeStruct((n, d), BF16)
    return pl.pallas_call(
        body, out_shape=(act_shape, act_shape, row_shape, jax.ShapeDtypeStruct((n, wo.shape[0]), BF16),
                         jax.ShapeDtypeStruct((1, d), F32)),
        grid=(n // tm,),
        in_specs=[row_d, ANY, row_f, row_f, ANY, ANY, row_d, _resident((1, d)), _resident(wo.shape)],
        out_specs=(row_f, row_f, row_d, pl.BlockSpec((tm, wo.shape[0]), lambda i: (i, 0)),
                   pl.BlockSpec((1, d), lambda i: (0, 0))),
        scratch_shapes=[pltpu.VMEM((f, d), BF16)] * 3 + [pltpu.SemaphoreType.DMA((3,))],
        compiler_params=_params("arbitrary"), name=name)(dx_b, wd, gate, up, wg_t, wu_t, x1, gain, wo)


def ffn_weight_grads(dgate, dup, h2, act, dx_b, name="ffn_weight_grads"):
    t, r = dgate.shape
    c = h2.shape[1]
    tr = _col_tile(r, 512)

    def body(a1_ref, a2_ref, a3_ref, b12_ref, b3_ref, o1_ref, o2_ref, o3_ref):
        bv = b12_ref[...]
        o1_ref[...] = _dot(a1_ref[...], bv, 0, 0).astype(o1_ref.dtype)
        o2_ref[...] = _dot(a2_ref[...], bv, 0, 0).astype(o2_ref.dtype)
        o3_ref[...] = _dot(a3_ref[...], b3_ref[...], 0, 0).astype(o3_ref.dtype)

    a_spec = pl.BlockSpec((t, tr), lambda i: (0, i))
    o_spec = pl.BlockSpec((tr, c), lambda i: (i, 0))
    shape = jax.ShapeDtypeStruct((r, c), BF16)
    return pl.pallas_call(
        body, out_shape=(shape, shape, shape), grid=(r // tr,),
        in_specs=[a_spec, a_spec, a_spec, _resident((t, c)), _resident((t, c))],
        out_specs=(o_spec, o_spec, o_spec), compiler_params=_params("parallel"), name=name)(dgate, dup, act, h2, dx_b)


def rms_fwd(x, g, name="rms_fwd"):
    n, d = x.shape
    tm = _row_tile(n)

    def body(x_ref, g_ref, o_ref):
        xv = x_ref[...]
        r = lax.rsqrt(jnp.mean(xv * xv, axis=-1, keepdims=True) + NORM_EPS)
        o_ref[...] = (xv * r * g_ref[...]).astype(o_ref.dtype)

    return pl.pallas_call(
        body, out_shape=jax.ShapeDtypeStruct((n, d), BF16), grid=(n // tm,),
        in_specs=[pl.BlockSpec((tm, d), lambda i: (i, 0)), pl.BlockSpec((1, d), lambda i: (0, 0))],
        out_specs=pl.BlockSpec((tm, d), lambda i: (i, 0)),
        compiler_params=_params("parallel"), name=name)(x, g)


def _group_masks(width):
    lane = lax.broadcasted_iota(jnp.int32, (1, width), 1)
    return [(lane >= HEAD_DIM * g) & (lane < HEAD_DIM * (g + 1)) for g in range(width // HEAD_DIM)]


def _group_sum(x, masks):
    out = jnp.zeros_like(x)
    for msk in masks:
        s = jnp.sum(jnp.where(msk, x, 0.0), axis=-1, keepdims=True)
        out = jnp.where(msk, s, out)
    return out


def _head_norm(x, gain, masks):
    r = lax.rsqrt(_group_sum(x * x, masks) * (1.0 / HEAD_DIM) + NORM_EPS)
    xhat = x * r
    return xhat * gain, xhat, r


def _head_norm_bwd(dxn, xhat, r, gain, masks):
    dgain = jnp.sum(dxn * xhat, axis=0, keepdims=True)
    dxhat = dxn * gain
    mean_t = _group_sum(dxhat * xhat, masks) * (1.0 / HEAD_DIM)
    return r * (dxhat - xhat * mean_t), dgain


def _softmax_rows(s):
    e = jnp.exp(s - jnp.max(s, axis=-1, keepdims=True))
    return e * (1.0 / jnp.sum(e, axis=-1, keepdims=True))


def _rel_onehot():
    col = lax.broadcasted_iota(jnp.int32, (1, KEY_WIN), 1)
    off = jnp.where(col < KEY_WIN - LANES, col, col - KEY_WIN)
    idx = jnp.clip(8 * CHUNK - off, -(CHUNK - 1), LANES) + (CHUNK - 1)
    return (lax.broadcasted_iota(jnp.int32, (N_REL, KEY_WIN), 0) == idx).astype(F32)


def bias_blocks(rel16):
    heads = TOK_WIDTH // HEAD_DIM

    def body(rel_ref, o_ref, u_ref):
        u_ref[...] = jnp.dot(rel_ref[...], _rel_onehot(), precision=HIGHEST, preferred_element_type=F32)
        row = lax.broadcasted_iota(jnp.int32, (CHUNK, KEY_WIN), 0)
        col = lax.broadcasted_iota(jnp.int32, (CHUNK, KEY_WIN), 1)
        for h in range(heads):
            xv = jnp.broadcast_to(u_ref[h:h + 1, :], (CHUNK, KEY_WIN))
            for b in range(6):
                xv = jnp.where(((row >> b) & 1) == 1, pltpu.roll(xv, 1 << b, axis=1), xv)
            xv = jnp.where(col < BAND, xv, NEG_INF)
            for i in range(Q_BLOCK // CHUNK):
                o_ref[h, CHUNK * i:CHUNK * (i + 1), :] = pltpu.roll(xv, CHUNK * i, axis=1) if i else xv

    return pl.pallas_call(
        body, out_shape=jax.ShapeDtypeStruct((heads, Q_BLOCK, KEY_WIN), F32),
        scratch_shapes=[pltpu.VMEM((16, KEY_WIN), F32)], name="bias_blocks")(rel16)


def bias_grad(dbias):
    heads = dbias.shape[0]

    def body(db_ref, o_ref, y_ref):
        y_ref[...] = jnp.zeros_like(y_ref)
        row = lax.broadcasted_iota(jnp.int32, (CHUNK, KEY_WIN), 0)
        for h in range(heads):
            fv = db_ref[h, 0:CHUNK, :]
            for i in range(1, Q_BLOCK // CHUNK):
                fv = fv + pltpu.roll(db_ref[h, CHUNK * i:CHUNK * (i + 1), :], KEY_WIN - CHUNK * i, axis=1)
            for b in range(6):
                fv = jnp.where(((row >> b) & 1) == 1, pltpu.roll(fv, KEY_WIN - (1 << b), axis=1), fv)
            y_ref[h:h + 1, :] = jnp.sum(fv, axis=0, keepdims=True)
        o_ref[...] = lax.dot_general(y_ref[...], _rel_onehot(), (((1,), (1,)), ((), ())),
                                     precision=HIGHEST, preferred_element_type=F32)

    return pl.pallas_call(
        body, out_shape=jax.ShapeDtypeStruct((16, N_REL), F32),
        scratch_shapes=[pltpu.VMEM((16, KEY_WIN), F32)], name="bias_grad")(dbias)


def _attn_windows(seq):
    out = []
    for j in range(seq // Q_BLOCK):
        r0 = j * Q_BLOCK
        k0 = max(0, r0 - 8 * CHUNK)
        width = r0 + Q_BLOCK - k0
        out.append((r0, k0, width, KEY_WIN - width))
    return out


def attn_fwd(z, gq2, gk2, bias, batch, seq):
    n = z.shape[0]
    pairs = TOK_WIDTH // LANES

    def body(q_ref, k_ref, v_ref, gq_ref, gk_ref, b_ref, o_ref, qs_s, kn_s):
        masks = _group_masks(LANES)
        qs_s[...] = (_head_norm(q_ref[...].astype(F32), gq_ref[...], masks)[0] * ATTN_SCALE).astype(BF16)
        kn_s[...] = _head_norm(k_ref[...].astype(F32), gk_ref[...], masks)[0].astype(BF16)
        for r0, k0, width, c0 in _attn_windows(seq):
            qb = qs_s[r0:r0 + Q_BLOCK, :]
            kw = kn_s[k0:k0 + width, :]
            vw = v_ref[k0:k0 + width, :]
            out = jnp.zeros((Q_BLOCK, LANES), F32)
            for h, msk in enumerate(masks):
                qh = jnp.where(msk, qb, jnp.zeros_like(qb))
                s = _dot(qh, kw, 1, 1) + b_ref[h, :, c0:KEY_WIN]
                p = _softmax_rows(s).astype(BF16)
                out = jnp.where(msk, _dot(p, vw, 1, 0), out)
            o_ref[r0:r0 + Q_BLOCK, :] = out.astype(o_ref.dtype)

    def col(off):
        return pl.BlockSpec((seq, LANES), lambda b, p: (b, off + p))

    vec = pl.BlockSpec((1, LANES), lambda b, p: (0, 0))
    return pl.pallas_call(
        body, out_shape=jax.ShapeDtypeStruct((n, D_MODEL), BF16), grid=(batch, pairs),
        in_specs=[col(0), col(pairs), col(2 * pairs), vec, vec,
                  pl.BlockSpec((2, Q_BLOCK, KEY_WIN), lambda b, p: (p, 0, 0))],
        out_specs=pl.BlockSpec((seq, LANES), lambda b, p: (b, p)),
        scratch_shapes=[pltpu.VMEM((seq, LANES), BF16), pltpu.VMEM((seq, LANES), BF16)],
        compiler_params=_params("parallel", "arbitrary"), name="attn_fwd")(z, z, z, gq2, gk2, bias)


def attn_bwd(z, dcat, gq2, gk2, bias, batch, seq):
    n = z.shape[0]
    pairs = TOK_WIDTH // LANES

    def body(q_ref, k_ref, v_ref, do_ref, gq_ref, gk_ref, b_ref,
             dq_ref, dk_ref, dv_ref, db_ref, dgq_ref, dgk_ref, qs_s, kn_s, dqn_s, dkn_s, dv_s):
        pi, bi = pl.program_id(0), pl.program_id(1)
        masks = _group_masks(LANES)

        @pl.when(bi == 0)
        def _():
            db_ref[...] = jnp.zeros_like(db_ref)

        @pl.when((bi == 0) & (pi == 0))
        def _():
            dgq_ref[...] = jnp.zeros_like(dgq_ref)
            dgk_ref[...] = jnp.zeros_like(dgk_ref)

        qn, qhat, rq = _head_norm(q_ref[...].astype(F32), gq_ref[...], masks)
        kn, khat, rk = _head_norm(k_ref[...].astype(F32), gk_ref[...], masks)
        qs_s[...] = (qn * ATTN_SCALE).astype(BF16)
        kn_s[...] = kn.astype(BF16)
        dkn_s[...] = jnp.zeros_like(dkn_s)
        dv_s[...] = jnp.zeros_like(dv_s)
        for r0, k0, width, c0 in _attn_windows(seq):
            qb = qs_s[r0:r0 + Q_BLOCK, :]
            dob = do_ref[r0:r0 + Q_BLOCK, :]
            kw = kn_s[k0:k0 + width, :]
            vw = v_ref[k0:k0 + width, :]
            dq_acc = jnp.zeros((Q_BLOCK, LANES), F32)
            dk_acc = jnp.zeros((width, LANES), F32)
            dv_acc = jnp.zeros((width, LANES), F32)
            for h, msk in enumerate(masks):
                qh = jnp.where(msk, qb, jnp.zeros_like(qb))
                doh = jnp.where(msk, dob, jnp.zeros_like(dob))
                p = _softmax_rows(_dot(qh, kw, 1, 1) + b_ref[h, :, c0:KEY_WIN])
                dp = _dot(doh, vw, 1, 1)
                ds = p * (dp - jnp.sum(p * dp, axis=-1, keepdims=True))
                db_ref[h, :, c0:KEY_WIN] += ds
                dsb = ds.astype(BF16)
                dq_acc = jnp.where(msk, _dot(dsb, kw, 1, 0), dq_acc)
                dk_acc = jnp.where(msk, _dot(dsb, qb, 0, 0), dk_acc)
                dv_acc = jnp.where(msk, _dot(p.astype(BF16), dob, 0, 0), dv_acc)
            dqn_s[r0:r0 + Q_BLOCK, :] = dq_acc * ATTN_SCALE
            dkn_s[k0:k0 + width, :] += dk_acc
            dv_s[k0:k0 + width, :] += dv_acc
        dq, dgq = _head_norm_bwd(dqn_s[...], qhat, rq, gq_ref[...], masks)
        dk, dgk = _head_norm_bwd(dkn_s[...], khat, rk, gk_ref[...], masks)
        dq_ref[...] = dq.astype(dq_ref.dtype)
        dk_ref[...] = dk.astype(dk_ref.dtype)
        dv_ref[...] = dv_s[...].astype(dv_ref.dtype)
        dgq_ref[...] += dgq
        dgk_ref[...] += dgk

    def col(off):
        return pl.BlockSpec((seq, LANES), lambda p, b: (b, off + p))

    vec = pl.BlockSpec((1, LANES), lambda p, b: (0, 0))
    blk = pl.BlockSpec((2, Q_BLOCK, KEY_WIN), lambda p, b: (p, 0, 0))
    o_shape = jax.ShapeDtypeStruct((n, TOK_WIDTH), BF16)
    v_shape = jax.ShapeDtypeStruct((1, LANES), F32)
    return pl.pallas_call(
        body,
        out_shape=(o_shape, o_shape, o_shape, jax.ShapeDtypeStruct(bias.shape, F32), v_shape, v_shape),
        grid=(pairs, batch),
        in_specs=[col(0), col(pairs), col(2 * pairs), col(0), vec, vec, blk],
        out_specs=(col(0), col(0), col(0), blk, vec, vec),
        scratch_shapes=[pltpu.VMEM((seq, LANES), BF16), pltpu.VMEM((seq, LANES), BF16),
                        pltpu.VMEM((seq, LANES), F32), pltpu.VMEM((seq, LANES), F32), pltpu.VMEM((seq, LANES), F32)],
        compiler_params=_params("arbitrary", "arbitrary"), name="attn_bwd")(z, z, z, dcat, gq2, gk2, bias)


MEM_ROWS_FWD = 1024
MEM_ROWS_BWD = 2048


def memattn_fwd(z, mem, mem_gain, wkv, gq4, gk4, cat, batch, seq, qcol, name):
    mtok = mem.shape[0] // batch
    d = mem.shape[1]
    rows = min(MEM_ROWS_FWD, seq)

    def body(q_ref, m_ref, mg_ref, w_ref, gq_ref, gk_ref, cat_ref, o_ref, n_ref, kv_ref):
        del cat_ref
        masks = _group_masks(MEM_WIDTH)
        mv = m_ref[...]
        r = lax.rsqrt(jnp.mean(mv * mv, axis=-1, keepdims=True) + NORM_EPS)
        nv = (mv * r * mg_ref[...]).astype(BF16)
        n_ref[...] = nv
        kv_ref[...] = _dot(nv, w_ref[...], 1, 0)
        kn = _head_norm(kv_ref[:, 0:MEM_WIDTH], gk_ref[...], masks)[0].astype(BF16)
        vm = kv_ref[:, MEM_WIDTH:2 * MEM_WIDTH].astype(BF16)
        for t in range(seq // rows):
            sl = slice(t * rows, (t + 1) * rows)
            qs = (_head_norm(q_ref[sl, :].astype(F32), gq_ref[...], masks)[0] * ATTN_SCALE).astype(BF16)
            out = jnp.zeros((rows, MEM_WIDTH), F32)
            for msk in masks:
                qh = jnp.where(msk, qs, jnp.zeros_like(qs))
                p = _softmax_rows(_dot(qh, kn, 1, 1)).astype(BF16)
                out = jnp.where(msk, _dot(p, vm, 1, 0), out)
            o_ref[sl, :] = out.astype(o_ref.dtype)

    vec = pl.BlockSpec((1, MEM_WIDTH), lambda b: (0, 0))
    mem_spec = pl.BlockSpec((mtok, d), lambda b: (b, 0))
    kv_spec = pl.BlockSpec((mtok, 2 * MEM_WIDTH), lambda b: (b, 0))
    return pl.pallas_call(
        body, out_shape=(jax.ShapeDtypeStruct(cat.shape, cat.dtype), jax.ShapeDtypeStruct(mem.shape, BF16),
                         jax.ShapeDtypeStruct((mem.shape[0], 2 * MEM_WIDTH), F32)), grid=(batch,),
        in_specs=[pl.BlockSpec((seq, MEM_WIDTH), lambda b: (b, qcol)), mem_spec, pl.BlockSpec((1, d), lambda b: (0, 0)),
                  pl.BlockSpec(wkv.shape, lambda b: (0, 0)), vec, vec, ANY],
        out_specs=(pl.BlockSpec((seq, MEM_WIDTH), lambda b: (b, TOK_WIDTH // MEM_WIDTH)), mem_spec, kv_spec),
        input_output_aliases={6: 0},
        compiler_params=_params("parallel"), name=name)(z, mem, mem_gain, wkv, gq4, gk4, cat)


def memattn_bwd(z, kv, dcat, gq4, gk4, mem, mem_n, wkv, dz, batch, seq, qcol, name):
    mtok = kv.shape[0] // batch
    d = mem.shape[1]
    rows = min(MEM_ROWS_BWD, seq)

    def body(q_ref, kv_ref, do_ref, gq_ref, gk_ref, m_ref, n_ref, w_ref, *rest):
        dq_ref, dgq_ref, dgk_ref, dw_ref, dmg_ref, dw_acc = rest[-6:]

        @pl.when(pl.program_id(0) == 0)
        def _():
            dgq_ref[...] = jnp.zeros_like(dgq_ref)
            dgk_ref[...] = jnp.zeros_like(dgk_ref)
            dmg_ref[...] = jnp.zeros_like(dmg_ref)
            dw_acc[...] = jnp.zeros_like(dw_acc)

        masks = _group_masks(MEM_WIDTH)
        kn_f, khat, rk = _head_norm(kv_ref[:, 0:MEM_WIDTH], gk_ref[...], masks)
        kn = kn_f.astype(BF16)
        vm = kv_ref[:, MEM_WIDTH:2 * MEM_WIDTH].astype(BF16)
        dkn = jnp.zeros((mtok, MEM_WIDTH), F32)
        dvm = jnp.zeros((mtok, MEM_WIDTH), F32)
        dgq = jnp.zeros((1, MEM_WIDTH), F32)
        for t in range(seq // rows):
            sl = slice(t * rows, (t + 1) * rows)
            qn_f, qhat, rq = _head_norm(q_ref[sl, :].astype(F32), gq_ref[...], masks)
            qs = (qn_f * ATTN_SCALE).astype(BF16)
            dob = do_ref[sl, :]
            dqn = jnp.zeros((rows, MEM_WIDTH), F32)
            for msk in masks:
                qh = jnp.where(msk, qs, jnp.zeros_like(qs))
                doh = jnp.where(msk, dob, jnp.zeros_like(dob))
                p = _softmax_rows(_dot(qh, kn, 1, 1))
                dp = _dot(doh, vm, 1, 1)
                ds = p * (dp - jnp.sum(p * dp, axis=-1, keepdims=True))
                dsb = ds.astype(BF16)
                dqn = jnp.where(msk, _dot(dsb, kn, 1, 0), dqn)
                dkn = dkn + jnp.where(msk, _dot(dsb, qs, 0, 0), 0.0)
                dvm = dvm + jnp.where(msk, _dot(p.astype(BF16), dob, 0, 0), 0.0)
            dq, dg = _head_norm_bwd(dqn * ATTN_SCALE, qhat, rq, gq_ref[...], masks)
            dq_ref[sl, :] = dq.astype(dq_ref.dtype)
            dgq = dgq + dg
        dk, dgk = _head_norm_bwd(dkn, khat, rk, gk_ref[...], masks)
        dgq_ref[...] += dgq
        dgk_ref[...] += dgk
        dkv_b = jnp.concatenate([dk, dvm], axis=-1).astype(BF16)
        dw_acc[...] += _dot(n_ref[...], dkv_b, 0, 0)
        dn = _dot(dkv_b, w_ref[...], 1, 1)
        mv = m_ref[...]
        rm = lax.rsqrt(jnp.mean(mv * mv, axis=-1, keepdims=True) + NORM_EPS)
        dmg_ref[...] += jnp.sum(dn * (mv * rm), axis=0, keepdims=True)

        @pl.when(pl.program_id(0) == batch - 1)
        def _():
            dw_ref[...] = dw_acc[...].astype(dw_ref.dtype)

    vec = pl.BlockSpec((1, MEM_WIDTH), lambda b: (0, 0))
    kv_spec = pl.BlockSpec((mtok, 2 * MEM_WIDTH), lambda b: (b, 0))
    mem_spec = pl.BlockSpec((mtok, d), lambda b: (b, 0))
    w_spec = pl.BlockSpec(wkv.shape, lambda b: (0, 0))
    v_shape = jax.ShapeDtypeStruct((1, MEM_WIDTH), F32)
    q_spec = pl.BlockSpec((seq, MEM_WIDTH), lambda b: (b, qcol))
    in_specs = [q_spec, kv_spec, pl.BlockSpec((seq, MEM_WIDTH), lambda b: (b, TOK_WIDTH // MEM_WIDTH)), vec, vec,
                mem_spec, mem_spec, w_spec]
    args = [z, kv, dcat, gq4, gk4, mem, mem_n, wkv]
    if dz is None:
        dq_shape, dq_spec, aliases = jax.ShapeDtypeStruct((z.shape[0], MEM_WIDTH), BF16), \
            pl.BlockSpec((seq, MEM_WIDTH), lambda b: (b, 0)), {}
    else:
        dq_shape, dq_spec, aliases = jax.ShapeDtypeStruct(dz.shape, dz.dtype), q_spec, {len(args): 0}
        in_specs.append(ANY)
        args.append(dz)
    return pl.pallas_call(
        body,
        out_shape=(dq_shape, v_shape, v_shape, jax.ShapeDtypeStruct(wkv.shape, BF16), jax.ShapeDtypeStruct((1, d), F32)),
        grid=(batch,), in_specs=in_specs,
        out_specs=(dq_spec, vec, vec, w_spec, pl.BlockSpec((1, d), lambda b: (0, 0))),
        scratch_shapes=[pltpu.VMEM(wkv.shape, F32)], input_output_aliases=aliases,
        compiler_params=_params("arbitrary"), name=name)(*args)


CONV_ROWS = 512


def _glu(a_ref, g_ref):
    return a_ref[...].astype(F32) * _sigmoid(g_ref[...].astype(F32))


def _layer_norm_stats(y):
    mu = jnp.mean(y, axis=-1, keepdims=True)
    yc = y - mu
    rstd = lax.rsqrt(jnp.mean(yc * yc, axis=-1, keepdims=True) + NORM_EPS)
    return yc * rstd, rstd


CONV_WIN = CONV_HALO + CONV_ROWS
SUBLANES = 8
SHIFT_ROWS = CONV_WIN - SUBLANES


def _preshift(win, shifted):
    for s in range(1, SUBLANES):
        shifted[s - 1, :, :] = win[s:s + SHIFT_ROWS, :]


TAP_ROWS = 16
TAP_TILES = [(r0, slice(c0, c0 + LANES)) for c0 in range(0, TOK_WIDTH, LANES) for r0 in range(0, CONV_ROWS, TAP_ROWS)]


def _tap(win, shifted, off, r0, lanes):
    s = off % SUBLANES
    base = off - s + r0
    if s == 0:
        return win[base:base + TAP_ROWS, lanes]
    return shifted[s - 1, base:base + TAP_ROWS, lanes]


def _fold_rows(x):
    return jnp.sum(x.reshape(TAP_ROWS // SUBLANES, SUBLANES, LANES), axis=0)


def conv_fwd(z, cw, cb, lg, lb, batch, seq):
    n = z.shape[0]
    nt = seq // CONV_ROWS
    sub = CONV_ROWS // CONV_HALO
    lead = CONV_HALO - (CONV_W - 1)

    def body(a_ref, g_ref, ap_ref, gp_ref, cw_ref, cb_ref, lg_ref, lb_ref, o_ref, y_ref, win, shifted):
        first = pl.program_id(1) == 0
        win[0:CONV_HALO, :] = jnp.where(first, 0.0, _glu(ap_ref, gp_ref))
        win[CONV_HALO:CONV_WIN, :] = _glu(a_ref, g_ref)
        _preshift(win, shifted)
        for r0, lanes in TAP_TILES:
            acc = jnp.zeros((TAP_ROWS, LANES), F32) + cb_ref[:, lanes]
            for w in range(CONV_W):
                acc = acc + _tap(win, shifted, lead + w, r0, lanes) * cw_ref[w:w + 1, lanes]
            y_ref[r0:r0 + TAP_ROWS, lanes] = acc
        yh, _ = _layer_norm_stats(y_ref[...])
        t = yh * lg_ref[...] + lb_ref[...]
        o_ref[...] = (t * _sigmoid(t)).astype(o_ref.dtype)

    def cur(c):
        return pl.BlockSpec((CONV_ROWS, TOK_WIDTH), lambda b, i: (b * nt + i, c))

    def prev(c):
        return pl.BlockSpec((CONV_HALO, TOK_WIDTH), lambda b, i: (jnp.maximum((b * nt + i) * sub - 1, 0), c))

    vec = pl.BlockSpec((1, TOK_WIDTH), lambda b, i: (0, 0))
    return pl.pallas_call(
        body, out_shape=(jax.ShapeDtypeStruct((n, D_MODEL), BF16), jax.ShapeDtypeStruct((n, TOK_WIDTH), F32)),
        grid=(batch, nt),
        in_specs=[cur(0), cur(1), prev(0), prev(1), pl.BlockSpec((32, TOK_WIDTH), lambda b, i: (0, 0)), vec, vec, vec],
        out_specs=(cur(0), cur(0)),
        scratch_shapes=[pltpu.VMEM((CONV_WIN, TOK_WIDTH), F32), pltpu.VMEM((SUBLANES - 1, SHIFT_ROWS, TOK_WIDTH), F32)],
        compiler_params=_params("parallel", "arbitrary"), name="conv_fwd")(z, z, z, z, cw, cb, lg, lb)


def conv_bwd(z, y, dcat, cw, lg, lb, batch, seq):
    n = z.shape[0]
    nt = seq // CONV_ROWS
    sub = CONV_ROWS // CONV_HALO
    lead = CONV_HALO - (CONV_W - 1)
    last_blk = n // CONV_HALO - 1

    def body(a_ref, g_ref, ap_ref, gp_ref, y_ref, yn_ref, do_ref, don_ref, cw_ref, lg_ref, lb_ref,
             dz_ref, dcw_ref, dsm_ref, win, shifted, dyw, dshifted, dg_o):
        b, i, which = pl.program_id(0), pl.program_id(1), pl.program_id(2)

        @pl.when(which == 0)
        def _():
            first, last = i == 0, i == nt - 1

            @pl.when((b == 0) & (i == 0))
            def _():
                dcw_ref[...] = jnp.zeros_like(dcw_ref)
                dsm_ref[...] = jnp.zeros_like(dsm_ref)

            win[0:CONV_HALO, :] = jnp.where(first, 0.0, _glu(ap_ref, gp_ref))
            win[CONV_HALO:CONV_WIN, :] = _glu(a_ref, g_ref)
            _preshift(win, shifted)
            yv = jnp.concatenate([y_ref[...], yn_ref[...]], axis=0)
            yh, rstd = _layer_norm_stats(yv)
            t = yh * lg_ref[...] + lb_ref[...]
            st = _sigmoid(t)
            dout = jnp.concatenate(
                [do_ref[...].astype(F32), jnp.where(last, 0.0, don_ref[...].astype(F32))], axis=0)
            dt = dout * st * (1.0 + t * (1.0 - st))
            dyh = dt * lg_ref[...]
            dy = rstd * (dyh - jnp.mean(dyh, axis=-1, keepdims=True)
                         - yh * jnp.mean(dyh * yh, axis=-1, keepdims=True))
            dyw[...] = dy
            _preshift(dyw, dshifted)
            dsm_ref[0:1, :] += jnp.sum(dy[0:CONV_ROWS], axis=0, keepdims=True)
            dsm_ref[1:2, :] += jnp.sum((dt * yh)[0:CONV_ROWS], axis=0, keepdims=True)
            dsm_ref[2:3, :] += jnp.sum(dt[0:CONV_ROWS], axis=0, keepdims=True)
            for c0 in range(0, TOK_WIDTH, LANES):
                lanes = slice(c0, c0 + LANES)
                dcw_acc = [jnp.zeros((SUBLANES, LANES), F32) for _ in range(CONV_W)]
                for r0 in range(0, CONV_ROWS, TAP_ROWS):
                    dyt = dyw[r0:r0 + TAP_ROWS, lanes]
                    dglu = jnp.zeros((TAP_ROWS, LANES), F32)
                    for w in range(CONV_W):
                        dcw_acc[w] = dcw_acc[w] + _fold_rows(dyt * _tap(win, shifted, lead + w, r0, lanes))
                        dglu = dglu + _tap(dyw, dshifted, CONV_W - 1 - w, r0, lanes) * cw_ref[w:w + 1, lanes]
                    avt = a_ref[r0:r0 + TAP_ROWS, lanes].astype(F32)
                    sgt = _sigmoid(g_ref[r0:r0 + TAP_ROWS, lanes].astype(F32))
                    dz_ref[r0:r0 + TAP_ROWS, lanes] = (dglu * sgt).astype(dz_ref.dtype)
                    dg_o[r0:r0 + TAP_ROWS, lanes] = (dglu * avt * sgt * (1.0 - sgt)).astype(dg_o.dtype)
                for w in range(CONV_W):
                    dcw_ref[w:w + 1, lanes] += jnp.sum(dcw_acc[w], axis=0, keepdims=True)

        @pl.when(which == 1)
        def _():
            dz_ref[...] = dg_o[...]

    def ahead(b, i, t):
        return jnp.minimum(b * nt + i + t, batch * nt - 1)

    def cur(c):
        return pl.BlockSpec((CONV_ROWS, TOK_WIDTH), lambda b, i, t: (ahead(b, i, t), c))

    def prev(c):
        return pl.BlockSpec((CONV_HALO, TOK_WIDTH), lambda b, i, t: (jnp.maximum(ahead(b, i, t) * sub - 1, 0), c))

    nxt = pl.BlockSpec((CONV_HALO, TOK_WIDTH),
                       lambda b, i, t: (jnp.minimum((ahead(b, i, t) + 1) * sub, last_blk), 0))
    vec = pl.BlockSpec((1, TOK_WIDTH), lambda b, i, t: (0, 0))
    full32 = pl.BlockSpec((32, TOK_WIDTH), lambda b, i, t: (0, 0))
    return pl.pallas_call(
        body,
        out_shape=(jax.ShapeDtypeStruct(z.shape, BF16), jax.ShapeDtypeStruct((32, TOK_WIDTH), F32),
                   jax.ShapeDtypeStruct((8, TOK_WIDTH), F32)),
        grid=(batch, nt, 2),
        in_specs=[cur(0), cur(1), prev(0), prev(1), cur(0), nxt, cur(0), nxt, full32, vec, vec],
        out_specs=(pl.BlockSpec((CONV_ROWS, TOK_WIDTH), lambda b, i, t: (b * nt + i, t)), full32,
                   pl.BlockSpec((8, TOK_WIDTH), lambda b, i, t: (0, 0))),
        scratch_shapes=[pltpu.VMEM((CONV_WIN, TOK_WIDTH), F32), pltpu.VMEM((SUBLANES - 1, SHIFT_ROWS, TOK_WIDTH), F32),
                        pltpu.VMEM((CONV_WIN, TOK_WIDTH), F32), pltpu.VMEM((SUBLANES - 1, SHIFT_ROWS, TOK_WIDTH), F32),
                        pltpu.VMEM((CONV_ROWS, TOK_WIDTH), BF16)],
        compiler_params=_params("arbitrary", "arbitrary", "arbitrary"), name="conv_bwd")(
            z, z, z, z, y, y, dcat, dcat, cw, lg, lb)


def _place():
    return lax.axis_index("x"), lax.axis_index("y"), lax.axis_index("c")


def _other_chips(x, y):
    return [(1 - x, y), (x, 1 - y), (1 - x, 1 - y)]


def reduce_small(arrays):
    na = len(arrays)

    def body(*refs):
        ins, outs, bufs = refs[:na], refs[na:2 * na], refs[2 * na:3 * na]
        send_sems, recv_sems = refs[3 * na:]
        x, y, c = _place()
        me = 4 * x + 2 * y + c
        copies = []
        for a in range(na):
            bufs[a][me] = ins[a][...]
            for k in range(1, N_DEV):
                cp = pltpu.make_async_remote_copy(
                    src_ref=ins[a], dst_ref=bufs[a].at[me], send_sem=send_sems.at[a, k - 1],
                    recv_sem=recv_sems.at[a, k - 1],
                    device_id=(x ^ (k >> 2), y ^ ((k >> 1) & 1), c ^ (k & 1)), device_id_type=MESH)
                cp.start()
                copies.append(cp)
        for a in range(na):
            for k in range(1, N_DEV):
                src = 4 * (x ^ (k >> 2)) + 2 * (y ^ ((k >> 1) & 1)) + (c ^ (k & 1))
                pltpu.make_async_remote_copy(
                    src_ref=ins[a], dst_ref=bufs[a].at[src], send_sem=send_sems.at[a, k - 1],
                    recv_sem=recv_sems.at[a, k - 1], device_id=(x, y, c), device_id_type=MESH).wait_recv()
        for cp in copies:
            cp.wait_send()
        for a in range(na):
            total = bufs[a][0]
            for dev in range(1, N_DEV):
                total = total + bufs[a][dev]
            outs[a][...] = total

    vmem = pl.BlockSpec(memory_space=pltpu.VMEM)
    return pl.pallas_call(
        body, out_shape=tuple(jax.ShapeDtypeStruct(a.shape, F32) for a in arrays),
        in_specs=[vmem] * na, out_specs=tuple([vmem] * na),
        scratch_shapes=[pltpu.VMEM((N_DEV,) + a.shape, F32) for a in arrays]
        + [pltpu.SemaphoreType.DMA((na, N_DEV - 1)), pltpu.SemaphoreType.DMA((na, N_DEV - 1))],
        compiler_params=pltpu.CompilerParams(vmem_limit_bytes=VMEM_LIMIT), name="small_reduce")(*arrays)


def adamw_small(ws, gs, ms, vs):
    na = len(ws)
    c1 = 1.0 / (1.0 - ADAM_B1 ** ADAM_STEP)
    c2 = 1.0 / (1.0 - ADAM_B2 ** ADAM_STEP)

    def body(*refs):
        w_refs, g_refs, m_refs, v_refs = (refs[i * na:(i + 1) * na] for i in range(4))
        d_refs, nm_refs, nv_refs = (refs[(4 + i) * na:(5 + i) * na] for i in range(3))
        for a in range(na):
            gv = g_refs[a][...]
            nm = ADAM_B1 * m_refs[a][...] + (1.0 - ADAM_B1) * gv
            nv = ADAM_B2 * v_refs[a][...] + (1.0 - ADAM_B2) * (gv * gv)
            nm_refs[a][...] = nm
            nv_refs[a][...] = nv
            d_refs[a][...] = -ADAM_LR * ((nm * c1) / (jnp.sqrt(nv * c2) + ADAM_EPS) + ADAM_WD * w_refs[a][...])

    vmem = pl.BlockSpec(memory_space=pltpu.VMEM)
    shapes = tuple(jax.ShapeDtypeStruct(w.shape, F32) for w in ws)
    outs = pl.pallas_call(
        body, out_shape=shapes * 3, in_specs=[vmem] * (4 * na), out_specs=tuple([vmem] * (3 * na)),
        compiler_params=pltpu.CompilerParams(vmem_limit_bytes=VMEM_LIMIT), name="adamw_small")(*ws, *gs, *ms, *vs)
    return outs[:na], outs[na:2 * na], outs[2 * na:]


def gather_weights(shards, name, collective_id):
    nw = len(shards)
    ns = [s.shape[0] for s in shards]
    in_refs = [jax.new_ref(s, memory_space=pltpu.MemorySpace.HBM) for s in shards]
    out_refs = [jax.empty_ref(jax.ShapeDtypeStruct((N_DEV * s.shape[0], s.shape[1]), s.dtype),
                              memory_space=pltpu.MemorySpace.HBM) for s in shards]

    @pl.kernel(mesh=plsc.ScalarSubcoreMesh(axis_name="seq", num_cores=1), name=name,
               scratch_types=(pltpu.SemaphoreType.DMA((nw, 7)), pltpu.SemaphoreType.DMA((nw, 7)),
                              pltpu.SemaphoreType.DMA((nw,))),
               compiler_params=pltpu.CompilerParams(collective_id=collective_id))
    def launch(send_sems, recv_sems, local_sems):
        x, y, c = _place()
        me, sib = (x, y, c), (x, y, 1 - c)
        chips = _other_chips(x, y)
        barrier = pltpu.get_barrier_semaphore()
        for peer in [sib] + [(*chip, c) for chip in chips]:
            pl.semaphore_signal(barrier, inc=1, device_id=peer, device_id_type=MESH)
        pl.semaphore_wait(barrier, 4)

        def rows(w, dev):
            return out_refs[w].at[pl.ds((4 * dev[0] + 2 * dev[1] + dev[2]) * ns[w], ns[w]), :]

        def copy(w, k, block, to, src=None):
            return pltpu.make_async_remote_copy(
                src_ref=rows(w, block) if src is None else src, dst_ref=rows(w, block),
                send_sem=send_sems.at[w, k], recv_sem=recv_sems.at[w, k], device_id=to, device_id_type=MESH)

        started, sends = [], []
        for w in range(nw):
            mine = pltpu.make_async_copy(in_refs[w], rows(w, me), local_sems.at[w])
            mine.start()
            started.append(mine)
            first = [copy(w, 0, me, sib, src=in_refs[w])]
            first += [copy(w, 1 + j, me, (*chip, c), src=in_refs[w]) for j, chip in enumerate(chips)]
            for cp in first:
                cp.start()
            sends += first
        for w in range(nw):
            for j, chip in enumerate(chips):
                copy(w, 1 + j, (*chip, c), me).wait_recv()
                fwd = copy(w, 4 + j, (*chip, c), sib)
                fwd.start()
                sends.append(fwd)
        for w in range(nw):
            copy(w, 0, sib, me).wait_recv()
            for j, chip in enumerate(chips):
                copy(w, 4 + j, (*chip, 1 - c), me).wait_recv()
        for cp in sends:
            cp.wait_send()
        for mine in started:
            mine.wait()

    launch()
    return [r[...] for r in out_refs]


def _sequencer_exchange(sources, out_rows, peers_of, copies_of, name, collective_id):
    nw = len(sources)
    in_refs = [jax.new_ref(s, memory_space=pltpu.MemorySpace.HBM) for s in sources]
    out_refs = [jax.empty_ref(jax.ShapeDtypeStruct((rows, s.shape[1]), s.dtype), memory_space=pltpu.MemorySpace.HBM)
                for rows, s in zip(out_rows, sources)]
    per = len(copies_of(0, 0, 0, 0))

    @pl.kernel(mesh=plsc.ScalarSubcoreMesh(axis_name="seq", num_cores=1), name=name,
               scratch_types=(pltpu.SemaphoreType.DMA((nw, per)), pltpu.SemaphoreType.DMA((nw, per))),
               compiler_params=pltpu.CompilerParams(collective_id=collective_id))
    def launch(send_sems, recv_sems):
        x, y, c = _place()
        peers = peers_of(x, y, c)
        barrier = pltpu.get_barrier_semaphore()
        for peer in peers:
            pl.semaphore_signal(barrier, inc=1, device_id=peer, device_id_type=MESH)
        pl.semaphore_wait(barrier, len(peers))
        copies = []
        for w in range(nw):
            for k, (src_blk, dst_blk, rows, peer) in enumerate(copies_of(x, y, c, w)):
                cp = pltpu.make_async_remote_copy(
                    src_ref=in_refs[w].at[pl.ds(src_blk * rows, rows), :],
                    dst_ref=out_refs[w].at[pl.ds(dst_blk * rows, rows), :],
                    send_sem=send_sems.at[w, k], recv_sem=recv_sems.at[w, k], device_id=peer, device_id_type=MESH)
                cp.start()
                copies.append(cp)
        for cp in copies:
            cp.wait_recv()
        for cp in copies:
            cp.wait_send()

    launch()
    return [r[...] for r in out_refs]


def scatter_to_sibling(grads, name, collective_id):
    ns = [g.shape[0] // N_DEV for g in grads]
    return _sequencer_exchange(
        grads, [4 * n for n in ns],
        lambda x, y, c: [(x, y, 1 - c)],
        lambda x, y, c, w: [(2 * q + 1 - c, q, ns[w], (x, y, 1 - c)) for q in range(4)],
        name, collective_id)


def scatter_to_chips(parts, name, collective_id):
    ns = [p.shape[0] // 4 for p in parts]
    return _sequencer_exchange(
        parts, [3 * n for n in ns],
        lambda x, y, c: [(*chip, c) for chip in _other_chips(x, y)],
        lambda x, y, c, w: [(2 * chip[0] + chip[1], j, ns[w], (*chip, c)) for j, chip in enumerate(_other_chips(x, y))],
        name, collective_id)


def add_sibling(grads, landeds, core, name):
    nw = len(grads)

    def body(c_ref, *refs):
        for w in range(nw):
            g_ref, l_ref, o_ref = refs[2 * w], refs[2 * w + 1], refs[2 * nw + w]
            o_ref[...] = (g_ref[...].astype(F32) + l_ref[...].astype(F32)).astype(o_ref.dtype)

    in_specs, out_specs, args = [], [], []
    for g, ld in zip(grads, landeds):
        n, cols = ld.shape[0] // 4, g.shape[1]
        in_specs += [pl.BlockSpec((n, cols), lambda q, c_ref: (2 * q + c_ref[0], 0)),
                     pl.BlockSpec((n, cols), lambda q, c_ref: (q, 0))]
        out_specs.append(pl.BlockSpec((n, cols), lambda q, c_ref: (q, 0)))
        args += [g, ld]
    grid_spec = pltpu.PrefetchScalarGridSpec(
        num_scalar_prefetch=1, grid=(4,), in_specs=in_specs, out_specs=tuple(out_specs))
    return pl.pallas_call(
        body, out_shape=tuple(jax.ShapeDtypeStruct(ld.shape, ld.dtype) for ld in landeds), grid_spec=grid_spec,
        compiler_params=_params("arbitrary"), name=name)(core, *args)


ADAMW_HALVES = 2


def adamw_shards(items, chip, name):
    c1 = 1.0 / (1.0 - ADAM_B1 ** ADAM_STEP)
    c2 = 1.0 / (1.0 - ADAM_B2 ** ADAM_STEP)
    ni = len(items)

    def body(q_ref, *refs):
        outs = refs[len(refs) - 4 * ni:]
        for k in range(ni):
            w_ref, m_ref, v_ref, p_ref, l0_ref, l1_ref, l2_ref = refs[7 * k:7 * k + 7]
            g_ref, d_ref, nm_ref, nv_ref = outs[4 * k:4 * k + 4]
            gv = ((p_ref[...].astype(F32) + l0_ref[...].astype(F32)) + l1_ref[...].astype(F32)) + l2_ref[...].astype(F32)
            nm = ADAM_B1 * m_ref[...] + (1.0 - ADAM_B1) * gv
            nv = ADAM_B2 * v_ref[...] + (1.0 - ADAM_B2) * (gv * gv)
            g_ref[...] = gv
            nm_ref[...] = nm
            nv_ref[...] = nv
            d_ref[...] = -ADAM_LR * ((nm * c1) / (jnp.sqrt(nv * c2) + ADAM_EPS) + ADAM_WD * w_ref[...])

    sub = ADAMW_HALVES
    in_specs, out_specs, out_shape, args, donated = [], [], [], [chip], []
    for layer, w, m, v, part, landed, earlier in items:
        rows, cols = landed.shape[0] // (3 * sub), w.shape[1]

        def block(first, rows=rows, cols=cols):
            return pl.BlockSpec((rows, cols), lambda i, q_ref: (first(q_ref) * sub + i, 0))

        own = block(lambda q_ref, layer=layer: layer)
        in_specs += [own, own, own, block(lambda q_ref: q_ref[0])] + [block(lambda q_ref, j=j: j) for j in range(3)]
        args += [w, m, v, part, landed, landed, landed]
        out_specs += [own] * 4
        out_shape += [jax.ShapeDtypeStruct(w.shape, F32)] * 4
        donated.append(earlier)
    aliases = {}
    for k, earlier in enumerate(donated):
        if earlier is not None:
            for j in range(4):
                aliases[len(args)] = 4 * k + j
                in_specs.append(ANY)
                args.append(earlier[j])
    grid_spec = pltpu.PrefetchScalarGridSpec(
        num_scalar_prefetch=1, grid=(sub,), in_specs=in_specs, out_specs=tuple(out_specs))
    outs = pl.pallas_call(
        body, out_shape=tuple(out_shape), grid_spec=grid_spec, input_output_aliases=aliases,
        compiler_params=_params("arbitrary"), name=name)(*args)
    return [tuple(outs[4 * k:4 * k + 4]) for k in range(ni)]


def _pack(arrays):
    flat = jnp.concatenate([a.reshape(-1).astype(F32) for a in arrays])
    pad = (-flat.shape[0]) % (8 * LANES)
    return jnp.pad(flat, (0, pad)).reshape(-1, LANES)


def _unpack(slab, shapes):
    flat = slab.reshape(slab.shape[:-2] + (-1,))
    out, off = [], 0
    for shp in shapes:
        size = 1
        for s in shp:
            size *= s
        out.append(flat[..., off:off + size].reshape(flat.shape[:-1] + tuple(shp)))
        off += size
    return out


def kernel(x, mem, norm1_g, mem_norm_g, a_w_in, a_q_g, a_k_g, a_rel_bias, b_w_in, b_b_in, b_conv_w, b_conv_b, b_ln_g, b_ln_b, mq_g, mk_g, w_mem_kv, w_out, norm2_g, w_gate, w_up, w_down, loss_target, m_norm1_g, m_mem_norm_g, m_a_w_in, m_a_q_g, m_a_k_g, m_a_rel_bias, m_b_w_in, m_b_b_in, m_b_conv_w, m_b_conv_b, m_b_ln_g, m_b_ln_b, m_mq_g, m_mk_g, m_w_mem_kv, m_w_out, m_norm2_g, m_w_gate, m_w_up, m_w_down, v_norm1_g, v_mem_norm_g, v_a_w_in, v_a_q_g, v_a_k_g, v_a_rel_bias, v_b_w_in, v_b_b_in, v_b_conv_w, v_b_conv_b, v_b_ln_g, v_b_ln_b, v_mq_g, v_mk_g, v_w_mem_kv, v_w_out, v_norm2_g, v_w_gate, v_w_up, v_w_down):
    batch, seq, d = x.shape
    mtok = mem.shape[1]
    n = batch * seq
    ax, ay, ac = _place()
    me = 4 * ax + 2 * ay + ac
    core_arr = jnp.reshape(ac, (1,)).astype(jnp.int32)
    chip_arr = jnp.reshape(2 * ax + ay, (1,)).astype(jnp.int32)

    def t_bf16(w):
        return jnp.transpose(w).astype(BF16)

    def after(value, *earlier):
        return lax.optimization_barrier((value, *earlier))[0]

    def gather_mix(l, when, name, collective_id):
        srcs = [w_mem_kv[l].astype(BF16), w_out[l].astype(BF16)]
        if l == 1:
            srcs += [t_bf16(b_w_in[0]), _pack([b_b_in, b_conv_w, b_conv_b, b_ln_g, b_ln_b])]
        return gather_weights([after(srcs[0], *when)] + srcs[1:], name, collective_id)

    def gather_ffn(l, when, name, collective_id):
        return gather_weights(
            [after(t_bf16(w_gate[l]), *when), t_bf16(w_up[l]), w_down[l].astype(BF16)], name, collective_id)

    f_loc = b_b_in.shape[1]
    c_loc = b_conv_b.shape[1]

    def two(g):
        return jnp.concatenate([g, g], axis=-1)

    gq2, gk2 = two(a_q_g), two(a_k_g)
    rel16 = jnp.pad(a_rel_bias[0], ((0, 16 - a_rel_bias.shape[1]), (0, 0)))
    bias = bias_blocks(rel16)

    x0 = x.reshape(n, d)
    mem2 = mem.reshape(batch * mtok, d)

    saved = []
    xin = x0
    a_win_t, = gather_weights([t_bf16(a_w_in[0])], "gather_in_a", 1)
    wg_t, wu_t, wd, wo, wkv = [None] * 2, [None] * 2, [None] * 2, [None] * 2, [None] * 2
    h = after(rms_fwd(xin, norm1_g[0:1], name="rms1_fwd_0"), bias)
    target = loss_target.reshape(n, d)
    for l in range(2):
        gq4 = jnp.tile(mq_g[l:l + 1], (1, 4))
        gk4 = jnp.tile(mk_g[l:l + 1], (1, 4))
        y_conv = None
        if l == 0:
            wkv[0], wo[0] = gather_mix(0, (h, a_win_t), "gather_mix_a", 2)
            z = mm_nt(h, a_win_t, name="in_proj_a")
            wg_t[0], wu_t[0], wd[0] = gather_ffn(0, (z, wkv[0]), "gather_ffn_a", 3)
            cat = attn_fwd(z, gq2, gk2, bias, batch, seq)
            wkv[1], wo[1], b_win_t, conv_slabs = gather_mix(1, (cat, wg_t[0]), "gather_mix_b", 4)
            qcol = 3 * TOK_WIDTH // MEM_WIDTH
        else:
            small_shapes = [(f_loc,), (CONV_W, c_loc), (c_loc,), (c_loc,), (c_loc,)]
            bb_g, cw_g, cb_g, lg_g, lb_g = _unpack(conv_slabs.reshape(N_DEV, -1, LANES), small_shapes)
            bb_full = bb_g.reshape(1, -1)
            cw_full = jnp.pad(jnp.transpose(cw_g, (1, 0, 2)).reshape(CONV_W, -1), ((0, 32 - CONV_W), (0, 0)))
            cb_full, lg_full, lb_full = cb_g.reshape(1, -1), lg_g.reshape(1, -1), lb_g.reshape(1, -1)
            z = mm_nt(h, b_win_t, bias=bb_full, name="in_proj_b")
            cat, y_conv = conv_fwd(z, cw_full, cb_full, lg_full, lb_full, batch, seq)
            qcol = 2 * TOK_WIDTH // MEM_WIDTH
        cat, mem_n, kv = memattn_fwd(
            z, mem2, mem_norm_g[l:l + 1], wkv[l], gq4, gk4, cat, batch, seq, qcol, name=f"memattn_fwd_{l}")
        x1, h2 = proj_norm(cat, wo[l], xin, norm2_g[l:l + 1], name=f"out_proj_{l}")
        if l == 0:
            wg_t[1], wu_t[1], wd[1] = gather_ffn(1, (x1, b_win_t), "gather_ffn_b", 5)
        if l == 0:
            gate, up, act, x2, h_next = ffn_fwd(h2, wg_t[0], wu_t[0], wd[0], x1, gain=norm1_g[1:2], name="ffn_fwd_0")
        else:
            gate, up, act, dx_b, loss_blk = ffn_fwd(h2, wg_t[1], wu_t[1], wd[1], x1, target=target, name="ffn_fwd_1")
        saved.append(dict(xin=xin, h=h, mem_n=mem_n, kv=kv, gq4=gq4, gk4=gk4, z=z, qcol=qcol, cat=cat, x1=x1, h2=h2,
                          gate=gate, up=up, act=act, y_conv=y_conv))
        if l == 0:
            xin, h = x2, h_next

    big = {}
    small = {}
    reduced = {}
    groups = 0

    def scatter_siblings(keys):
        nonlocal groups
        gid = groups
        groups += 1
        return gid, keys, scatter_to_sibling([big[k] for k in keys], f"scatter_sibling_{gid}", 8 + 2 * gid)

    def scatter_chips(stage1, when):
        gid, keys, landed1 = stage1
        parts = add_sibling([after(big[keys[0]], when)] + [big[k] for k in keys[1:]], landed1, core_arr,
                            name=f"add_sibling_{gid}")
        landed2 = scatter_to_chips(parts, f"scatter_chips_{gid}", 9 + 2 * gid)
        for k, p, ld in zip(keys, parts, landed2):
            reduced[k] = (p, ld)
        return parts, landed2

    def rows_of(w, transposed):
        w = jnp.swapaxes(w, 1, 2) if transposed else w
        return w.reshape(w.shape[0] * w.shape[1], w.shape[2])

    sharded = {
        "win0": (2, True), "win1": (6, True), "wkv": (14, False), "wo": (15, False),
        "wg": (17, True), "wu": (18, True), "wd": (19, False)}
    weights = [norm1_g, mem_norm_g, a_w_in, a_q_g, a_k_g, a_rel_bias, b_w_in, b_b_in, b_conv_w, b_conv_b, b_ln_g,
               b_ln_b, mq_g, mk_g, w_mem_kv, w_out, norm2_g, w_gate, w_up, w_down]
    moms = [m_norm1_g, m_mem_norm_g, m_a_w_in, m_a_q_g, m_a_k_g, m_a_rel_bias, m_b_w_in, m_b_b_in, m_b_conv_w,
            m_b_conv_b, m_b_ln_g, m_b_ln_b, m_mq_g, m_mk_g, m_w_mem_kv, m_w_out, m_norm2_g, m_w_gate, m_w_up, m_w_down]
    vels = [v_norm1_g, v_mem_norm_g, v_a_w_in, v_a_q_g, v_a_k_g, v_a_rel_bias, v_b_w_in, v_b_b_in, v_b_conv_w,
            v_b_conv_b, v_b_ln_g, v_b_ln_b, v_mq_g, v_mk_g, v_w_mem_kv, v_w_out, v_norm2_g, v_w_gate, v_w_up, v_w_down]
    updated = {}

    def update_layer(l, when):
        for group, keys in (("ffn", ("wg", "wu", "wd")), ("mix", (f"win{l}", "wkv", "wo"))):
            items = []
            for key in keys:
                idx, transposed = sharded[key]
                layer, rkey = (0, key) if key.startswith("win") else (l, f"{key}{l}")
                part, landed = reduced[rkey]
                w_rows = rows_of(weights[idx], transposed)
                items.append((layer, after(w_rows, when) if not items else w_rows, rows_of(moms[idx], transposed),
                              rows_of(vels[idx], transposed), part, landed, updated.get(key)))
            for key, result in zip(keys, adamw_shards(items, chip_arr, name=f"adamw_{group}_{l}")):
                updated[key] = result

    mix_landed = None
    for l in (1, 0):
        sv = saved[l]
        dgate, dup, dx1_b, dcat, small[f"norm2_{l}"] = ffn_bwd(
            dx_b, wd[l], sv["gate"], sv["up"], wg_t[l], wu_t[l], sv["x1"], norm2_g[l:l + 1], wo[l], name=f"ffn_bwd_{l}")
        if l == 0:
            dgate = after(dgate, *mix_landed)
            update_layer(1, dx1_b)
        big[f"wg{l}"], big[f"wu{l}"], big[f"wd{l}"] = ffn_weight_grads(
            dgate, dup, sv["h2"], sv["act"], dx_b, name=f"grad_ffn_{l}")
        stage1 = scatter_siblings([f"wd{l}", f"wg{l}", f"wu{l}"])
        big[f"wo{l}"] = mm_tn(sv["cat"], dx1_b, name=f"grad_wo_{l}")
        parts, ffn_landed = scatter_chips(stage1, big[f"wo{l}"])
        dcat = after(dcat, *parts)
        if l == 0:
            dq, dk, dv, dbias, small["a_q"], small["a_k"] = attn_bwd(sv["z"], dcat, gq2, gk2, bias, batch, seq)
            small["rel"] = bias_grad(dbias)
            win_t = a_win_t
            dz = None
            dcat = after(dcat, dq, *ffn_landed)
        else:
            dz, small["cw"], small["csum"] = conv_bwd(sv["z"], sv["y_conv"], dcat, cw_full, lg_full, lb_full, batch, seq)
            win_t = b_win_t
            dz = after(dz, *ffn_landed)
        dqm, small[f"mq_{l}"], small[f"mk_{l}"], big[f"wkv{l}"], small[f"memnorm_{l}"] = memattn_bwd(
            sv["z"], sv["kv"], dcat, sv["gq4"], sv["gk4"], mem2, sv["mem_n"], wkv[l], dz, batch, seq, sv["qcol"],
            name=f"memattn_bwd_{l}")
        if l == 0:
            pieces = [dq, dk, dv, dqm]
            big["win0"] = grad_pieces(pieces, sv["h"], name="grad_win_0")
        else:
            pieces = [dqm]
            big["win1"] = mm_tn(dqm, sv["h"], name="grad_win_1")
        stage1 = scatter_siblings([f"win{l}", f"wkv{l}", f"wo{l}"])
        dx_b, small[f"norm1_{l}"], dz_sum = in_proj_bwd(
            pieces, win_t, sv["xin"], norm1_g[l:l + 1], dx1_b, BF16 if l == 1 else F32, name=f"in_proj_bwd_{l}")
        if l == 1:
            small["bb"] = dz_sum
        parts, mix_landed = scatter_chips(stage1, dx_b)
        dx_b = after(dx_b, *parts)
    grad_x = dx_b.reshape(batch, seq, d)
    update_layer(0, dx_b)

    def shaped(rows, idx, transposed):
        shp = weights[idx].shape
        if transposed:
            return jnp.swapaxes(rows.reshape(shp[0], shp[2], shp[1]), 1, 2)
        return rows.reshape(shp)

    def fold(v, groups):
        return jnp.sum(v.reshape(groups, HEAD_DIM), axis=0, keepdims=True)

    heads = a_rel_bias.shape[1]
    small_list = [
        jnp.concatenate([small["norm1_0"], small["norm1_1"]]),
        jnp.concatenate([small["memnorm_0"], small["memnorm_1"]]),
        fold(small["a_q"], 2), fold(small["a_k"], 2), small["rel"][:heads],
        small["bb"], small["cw"][:CONV_W], small["csum"][0:1], small["csum"][1:2], small["csum"][2:3],
        jnp.concatenate([fold(small["mq_0"], 4), fold(small["mq_1"], 4)]),
        jnp.concatenate([fold(small["mk_0"], 4), fold(small["mk_1"], 4)]),
        jnp.concatenate([small["norm2_0"], small["norm2_1"]]),
    ]
    (g_norm1, g_memnorm, g_aq, g_ak, g_rel, g_bb_full, g_cw_full, g_cb_full, g_lg_full, g_lb_full,
     g_mq, g_mk, g_norm2, loss_sum) = reduce_small(small_list + [loss_blk])
    loss = loss_sum[0, 0]
    g_bb = lax.dynamic_slice_in_dim(g_bb_full, me * f_loc, f_loc, axis=1)
    g_cw = lax.dynamic_slice_in_dim(g_cw_full, me * c_loc, c_loc, axis=1)
    g_cb = lax.dynamic_slice_in_dim(g_cb_full, me * c_loc, c_loc, axis=1)
    g_lg = lax.dynamic_slice_in_dim(g_lg_full, me * c_loc, c_loc, axis=1)
    g_lb = lax.dynamic_slice_in_dim(g_lb_full, me * c_loc, c_loc, axis=1)

    grads = [g_norm1, g_memnorm, None, g_aq, g_ak, g_rel, None, g_bb, g_cw, g_cb, g_lg, g_lb,
             g_mq, g_mk, None, None, g_norm2, None, None, None]
    deltas, new_m, new_v = [None] * 20, [None] * 20, [None] * 20
    for key, (idx, transposed) in sharded.items():
        grads[idx], deltas[idx], new_m[idx], new_v[idx] = (shaped(r, idx, transposed) for r in updated[key])

    def flat2(a):
        return a.reshape(a.shape[-2:])

    small_idx = [i for i in range(20) if i not in {idx for idx, _ in sharded.values()}]
    dl, nm, nv = adamw_small([flat2(weights[i]) for i in small_idx], [flat2(grads[i]) for i in small_idx],
                             [flat2(moms[i]) for i in small_idx], [flat2(vels[i]) for i in small_idx])
    for i, a, b, cc in zip(small_idx, dl, nm, nv):
        shp = weights[i].shape
        grads[i], deltas[i], new_m[i], new_v[i] = grads[i].reshape(shp), a.reshape(shp), b.reshape(shp), cc.reshape(shp)

    return (loss, grad_x, *grads, *deltas, *new_m, *new_v)
```

```python
import jax
import jax.numpy as jnp
from jax import lax
from jax.experimental import pallas as pl
from jax.experimental.pallas import tpu as pltpu
from jax.experimental.pallas import tpu_sc as plsc

F32 = jnp.float32
BF16 = jnp.bfloat16
HIGHEST = lax.Precision.HIGHEST
MESH = pl.DeviceIdType.MESH
ANY = pl.BlockSpec(memory_space=pl.ANY)

N_DEV = 8
D_MODEL = 1024
HEAD_DIM = 64
TOK_WIDTH = 768
MEM_WIDTH = 256
CHUNK = 64
Q_BLOCK = 256
KEY_WIN = 768
BAND = 576
N_REL = 192
CONV_W = 31
CONV_HALO = 32
NORM_EPS = 1e-6
NEG_INF = -1e30
ATTN_SCALE = HEAD_DIM ** -0.5
LANES = 128
ROW_TILE = 512
VMEM_LIMIT = 56 * 1024 * 1024

ADAM_LR, ADAM_B1, ADAM_B2, ADAM_EPS, ADAM_WD, ADAM_STEP = 0.001, 0.9, 0.999, 1e-08, 0.01, 10


def _params(*sem):
    return pltpu.CompilerParams(dimension_semantics=sem, vmem_limit_bytes=VMEM_LIMIT)


WIDE_ROW_TILE = 1024


def _row_tile(m, rows=ROW_TILE):
    return rows if m % rows == 0 else m


def _col_tile(n, cap=1408):
    best = None
    for t in range(LANES, min(n, cap) + 1, LANES):
        if n % t == 0:
            best = t
    return best if best is not None else n


def _dot(a, b, ca, cb):
    return lax.dot_general(a, b, (((ca,), (cb,)), ((), ())), preferred_element_type=F32)


def _sigmoid(x):
    return 0.5 * jnp.tanh(0.5 * x) + 0.5


def mm_nt(a, b, bias=None, out_dtype=BF16, name="mm_nt"):
    m, k = a.shape
    n = b.shape[0]
    tm, tn = _row_tile(m, WIDE_ROW_TILE), _col_tile(n)

    def body(*refs):
        a_ref, b_ref = refs[0], refs[1]
        o_ref = refs[-1]
        acc = _dot(a_ref[...].astype(BF16), b_ref[...].astype(BF16), 1, 1)
        if bias is not None:
            acc = acc + refs[2][...]
        o_ref[...] = acc.astype(o_ref.dtype)

    in_specs = [pl.BlockSpec((tm, k), lambda j, i: (i, 0)), pl.BlockSpec((tn, k), lambda j, i: (j, 0))]
    args = [a, b]
    if bias is not None:
        in_specs.append(pl.BlockSpec((1, tn), lambda j, i: (0, j)))
        args.append(bias)
    return pl.pallas_call(
        body, out_shape=jax.ShapeDtypeStruct((m, n), out_dtype), grid=(n // tn, m // tm),
        in_specs=in_specs, out_specs=pl.BlockSpec((tm, tn), lambda j, i: (i, j)),
        compiler_params=_params("parallel", "arbitrary"), name=name)(*args)


def mm_tn(a, b, out_dtype=BF16, name="mm_tn"):
    t, r = a.shape
    c = b.shape[1]
    tr = _col_tile(r, 512)

    def body(a_ref, b_ref, o_ref):
        o_ref[...] = _dot(a_ref[...].astype(BF16), b_ref[...].astype(BF16), 0, 0).astype(o_ref.dtype)

    return pl.pallas_call(
        body, out_shape=jax.ShapeDtypeStruct((r, c), out_dtype), grid=(r // tr,),
        in_specs=[pl.BlockSpec((t, tr), lambda i: (0, i)), pl.BlockSpec((t, c), lambda i: (0, 0))],
        out_specs=pl.BlockSpec((tr, c), lambda i: (i, 0)),
        compiler_params=_params("parallel"), name=name)(a, b)


def _resident(shape):
    return pl.BlockSpec(shape, lambda i: (0, 0), pipeline_mode=pl.Buffered(1))


def proj_norm(a, b, res, gain, name):
    m, k = a.shape
    n = b.shape[1]
    tm = _row_tile(m)

    def body(a_ref, b_ref, res_ref, g_ref, x_ref, h_ref):
        xv = res_ref[...] + _dot(a_ref[...], b_ref[...], 1, 0)
        x_ref[...] = xv
        r = lax.rsqrt(jnp.mean(xv * xv, axis=-1, keepdims=True) + NORM_EPS)
        h_ref[...] = (xv * r * g_ref[...]).astype(BF16)

    row = pl.BlockSpec((tm, n), lambda i: (i, 0))
    return pl.pallas_call(
        body, out_shape=(jax.ShapeDtypeStruct((m, n), F32), jax.ShapeDtypeStruct((m, n), BF16)), grid=(m // tm,),
        in_specs=[pl.BlockSpec((tm, k), lambda i: (i, 0)), _resident((k, n)), row, _resident((1, n))],
        out_specs=(row, row), compiler_params=_params("parallel"), name=name)(a, b, res, gain)


def in_proj_bwd(pieces, w_t, x, gain, dres, out_dtype, name):
    m, n = x.shape
    k = pieces[0].shape[1]
    tm = _row_tile(m)
    npc = len(pieces)
    offs = [sum(p.shape[1] for p in pieces[:i]) for i in range(npc + 1)]

    def body(*refs):
        dz_refs = refs[:npc]
        w_ref, x_ref, g_ref, dres_ref, dx_ref, dg_ref, cs_ref = refs[npc:]

        @pl.when(pl.program_id(0) == 0)
        def _():
            dg_ref[...] = jnp.zeros_like(dg_ref)
            cs_ref[...] = jnp.zeros_like(cs_ref)

        cs_ref[...] += jnp.sum(dz_refs[0][...].astype(F32), axis=0, keepdims=True)
        dhv = _dot(dz_refs[0][...], w_ref[offs[0]:offs[1], :], 1, 0)
        for i in range(1, npc):
            dhv = dhv + _dot(dz_refs[i][...], w_ref[offs[i]:offs[i + 1], :], 1, 0)
        xv = x_ref[...]
        r = lax.rsqrt(jnp.mean(xv * xv, axis=-1, keepdims=True) + NORM_EPS)
        xhat = xv * r
        dg_ref[...] += jnp.sum(dhv * xhat, axis=0, keepdims=True)
        dxhat = dhv * g_ref[...]
        dx = dres_ref[...].astype(F32) + r * (dxhat - xhat * jnp.mean(dxhat * xhat, axis=-1, keepdims=True))
        dx_ref[...] = dx.astype(dx_ref.dtype)

    row = pl.BlockSpec((tm, n), lambda i: (i, 0))
    return pl.pallas_call(
        body, out_shape=(jax.ShapeDtypeStruct((m, n), out_dtype), jax.ShapeDtypeStruct((1, n), F32),
                         jax.ShapeDtypeStruct((1, k), F32)), grid=(m // tm,),
        in_specs=[pl.BlockSpec((tm, p.shape[1]), lambda i: (i, 0)) for p in pieces]
        + [_resident(w_t.shape), row, _resident((1, n)), row],
        out_specs=(row, pl.BlockSpec((1, n), lambda i: (0, 0)), pl.BlockSpec((1, k), lambda i: (0, 0))),
        compiler_params=_params("arbitrary"), name=name)(*pieces, w_t, x, gain, dres)


def grad_pieces(pieces, b, name):
    t, c = b.shape
    tr = 2 * LANES
    tiles = [p.shape[1] // tr for p in pieces]
    starts = [sum(tiles[:i]) for i in range(len(pieces) + 1)]

    def body(*refs):
        a_refs, b_ref, o_ref = refs[:len(pieces)], refs[len(pieces)], refs[len(pieces) + 1]
        i = pl.program_id(0)
        for p, a_ref in enumerate(a_refs):
            @pl.when((i >= starts[p]) & (i < starts[p + 1]))
            def _(a_ref=a_ref):
                o_ref[...] = _dot(a_ref[...], b_ref[...], 0, 0).astype(o_ref.dtype)

    def a_spec(p):
        return pl.BlockSpec((t, tr), lambda i: (0, jnp.clip(i - starts[p], 0, tiles[p] - 1)))

    return pl.pallas_call(
        body, out_shape=jax.ShapeDtypeStruct((starts[-1] * tr, c), BF16), grid=(starts[-1],),
        in_specs=[a_spec(p) for p in range(len(pieces))] + [_resident((t, c))],
        out_specs=pl.BlockSpec((tr, c), lambda i: (i, 0)),
        compiler_params=_params("arbitrary"), name=name)(*pieces, b)


FFN_ROWS = 256


def _ffn_row_tile(m):
    return FFN_ROWS if m % FFN_ROWS == 0 else m


def ffn_fwd(h2, wg_t, wu_t, wd, x1, gain=None, target=None, name="ffn_fwd"):
    n, d = h2.shape
    f = wg_t.shape[0]
    tm = _ffn_row_tile(n)
    nt = n // tm
    last = target is not None

    def body(h_ref, wg_ref, wu_ref, wd_ref, x1_ref, e_ref, g_ref, u_ref, a_ref, *rest):
        hv = h_ref[...]
        gv = _dot(hv, wg_ref[...], 1, 1)
        uv = _dot(hv, wu_ref[...], 1, 1)
        g_ref[...] = gv.astype(BF16)
        u_ref[...] = uv.astype(BF16)
        av = (gv * _sigmoid(gv) * uv).astype(BF16)
        a_ref[...] = av
        xv = x1_ref[...] + _dot(av, wd_ref[...], 1, 0)
        if not last:
            x_ref, hn_ref = rest
            x_ref[...] = xv
            r = lax.rsqrt(jnp.mean(xv * xv, axis=-1, keepdims=True) + NORM_EPS)
            hn_ref[...] = (xv * r * e_ref[...]).astype(BF16)
        else:
            dyb_ref, l_ref, acc_ref = rest
            i = pl.program_id(0)

            @pl.when(i == 0)
            def _():
                acc_ref[...] = jnp.zeros_like(acc_ref)

            err = xv - e_ref[...]
            dyb_ref[...] = (err * (1.0 / d)).astype(BF16)
            acc_ref[...] += jnp.sum(err * err, axis=0, keepdims=True)

            @pl.when(i == nt - 1)
            def _():
                total = jnp.sum(acc_ref[...], axis=-1, keepdims=True) * (0.5 / d)
                l_ref[...] = jnp.broadcast_to(total, l_ref.shape)

    row_d = pl.BlockSpec((tm, d), lambda i: (i, 0))
    row_f = pl.BlockSpec((tm, f), lambda i: (i, 0))
    act_shape = jax.ShapeDtypeStruct((n, f), BF16)
    if not last:
        extra_in, extra = _resident((1, d)), gain
        out_shape = (act_shape, act_shape, act_shape, jax.ShapeDtypeStruct((n, d), F32), jax.ShapeDtypeStruct((n, d), BF16))
        out_specs = (row_f, row_f, row_f, row_d, row_d)
        scratch = []
    else:
        extra_in, extra = row_d, target
        out_shape = (act_shape, act_shape, act_shape, jax.ShapeDtypeStruct((n, d), BF16),
                     jax.ShapeDtypeStruct((8, LANES), F32))
        out_specs = (row_f, row_f, row_f, row_d, pl.BlockSpec((8, LANES), lambda i: (0, 0)))
        scratch = [pltpu.VMEM((1, d), F32)]
    return pl.pallas_call(
        body, out_shape=out_shape, grid=(nt,),
        in_specs=[row_d, _resident((f, d)), _resident((f, d)), _resident((f, d)), row_d, extra_in],
        out_specs=out_specs, scratch_shapes=scratch,
        compiler_params=_params("arbitrary"), name=name)(h2, wg_t, wu_t, wd, x1, extra)


def ffn_bwd(dx_b, wd, gate, up, wg_t, wu_t, x1, gain, wo, name="ffn_bwd"):
    n, d = x1.shape
    f = wd.shape[0]
    tm = _ffn_row_tile(n)

    def body(dxb_ref, wd_ref, g_ref, u_ref, wg_ref, wu_ref, x_ref, gain_ref, wo_ref,
             dg_ref, du_ref, dxo_ref, dc_ref, dgain_ref):
        @pl.when(pl.program_id(0) == 0)
        def _():
            dgain_ref[...] = jnp.zeros_like(dgain_ref)

        dact = _dot(dxb_ref[...], wd_ref[...], 1, 1)
        gv = g_ref[...].astype(F32)
        uv = u_ref[...].astype(F32)
        sg = _sigmoid(gv)
        dgv = (dact * uv * sg * (1.0 + gv * (1.0 - sg))).astype(BF16)
        duv = (dact * gv * sg).astype(BF16)
        dg_ref[...] = dgv
        du_ref[...] = duv
        dhv = _dot(dgv, wg_ref[...], 1, 0) + _dot(duv, wu_ref[...], 1, 0)
        xv = x_ref[...]
        r = lax.rsqrt(jnp.mean(xv * xv, axis=-1, keepdims=True) + NORM_EPS)
        xhat = xv * r
        dgain_ref[...] += jnp.sum(dhv * xhat, axis=0, keepdims=True)
        dxhat = dhv * gain_ref[...]
        dxb = (dxb_ref[...].astype(F32) + r * (dxhat - xhat * jnp.mean(dxhat * xhat, axis=-1, keepdims=True))).astype(BF16)
        dxo_ref[...] = dxb
        dc_ref[...] = _dot(dxb, wo_ref[...], 1, 1).astype(BF16)

    row_d = pl.BlockSpec((tm, d), lambda i: (i, 0))
    row_f = pl.BlockSpec((tm, f), lambda i: (i, 0))
    w_spec = _resident((f, d))
    act_shape = jax.ShapeDtypeStruct((n, f), BF16)
    row_shape = jax.ShapeDtypeStruct((n, d), BF16)
    return pl.pallas_call(
        body, out_shape=(act_shape, act_shape, row_shape, jax.ShapeDtypeStruct((n, wo.shape[0]), BF16),
                         jax.ShapeDtypeStruct((1, d), F32)),
        grid=(n // tm,),
        in_specs=[row_d, w_spec, row_f, row_f, w_spec, w_spec, row_d, _resident((1, d)), _resident(wo.shape)],
        out_specs=(row_f, row_f, row_d, pl.BlockSpec((tm, wo.shape[0]), lambda i: (i, 0)),
                   pl.BlockSpec((1, d), lambda i: (0, 0))),
        compiler_params=_params("arbitrary"), name=name)(dx_b, wd, gate, up, wg_t, wu_t, x1, gain, wo)


def ffn_weight_grads(dgate, dup, h2, act, dx_b, name="ffn_weight_grads"):
    t, r = dgate.shape
    c = h2.shape[1]
    tr = _col_tile(r, 512)

    def body(a1_ref, a2_ref, a3_ref, b12_ref, b3_ref, o1_ref, o2_ref, o3_ref):
        bv = b12_ref[...]
        o1_ref[...] = _dot(a1_ref[...], bv, 0, 0).astype(o1_ref.dtype)
        o2_ref[...] = _dot(a2_ref[...], bv, 0, 0).astype(o2_ref.dtype)
        o3_ref[...] = _dot(a3_ref[...], b3_ref[...], 0, 0).astype(o3_ref.dtype)

    a_spec = pl.BlockSpec((t, tr), lambda i: (0, i))
    o_spec = pl.BlockSpec((tr, c), lambda i: (i, 0))
    shape = jax.ShapeDtypeStruct((r, c), BF16)
    return pl.pallas_call(
        body, out_shape=(shape, shape, shape), grid=(r // tr,),
        in_specs=[a_spec, a_spec, a_spec, _resident((t, c)), _resident((t, c))],
        out_specs=(o_spec, o_spec, o_spec), compiler_params=_params("parallel"), name=name)(dgate, dup, act, h2, dx_b)


def rms_fwd(x, g, name="rms_fwd"):
    n, d = x.shape
    tm = _row_tile(n)

    def body(x_ref, g_ref, o_ref):
        xv = x_ref[...]
        r = lax.rsqrt(jnp.mean(xv * xv, axis=-1, keepdims=True) + NORM_EPS)
        o_ref[...] = (xv * r * g_ref[...]).astype(o_ref.dtype)

    return pl.pallas_call(
        body, out_shape=jax.ShapeDtypeStruct((n, d), BF16), grid=(n // tm,),
        in_specs=[pl.BlockSpec((tm, d), lambda i: (i, 0)), pl.BlockSpec((1, d), lambda i: (0, 0))],
        out_specs=pl.BlockSpec((tm, d), lambda i: (i, 0)),
        compiler_params=_params("parallel"), name=name)(x, g)


def _group_masks(width):
    lane = lax.broadcasted_iota(jnp.int32, (1, width), 1)
    return [(lane >= HEAD_DIM * g) & (lane < HEAD_DIM * (g + 1)) for g in range(width // HEAD_DIM)]


def _group_sum(x, masks):
    out = jnp.zeros_like(x)
    for msk in masks:
        s = jnp.sum(jnp.where(msk, x, 0.0), axis=-1, keepdims=True)
        out = jnp.where(msk, s, out)
    return out


def _head_norm(x, gain, masks):
    r = lax.rsqrt(_group_sum(x * x, masks) * (1.0 / HEAD_DIM) + NORM_EPS)
    xhat = x * r
    return xhat * gain, xhat, r


def _head_norm_bwd(dxn, xhat, r, gain, masks):
    dgain = jnp.sum(dxn * xhat, axis=0, keepdims=True)
    dxhat = dxn * gain
    mean_t = _group_sum(dxhat * xhat, masks) * (1.0 / HEAD_DIM)
    return r * (dxhat - xhat * mean_t), dgain


def _softmax_rows(s):
    e = jnp.exp(s - jnp.max(s, axis=-1, keepdims=True))
    return e * (1.0 / jnp.sum(e, axis=-1, keepdims=True))


def _rel_onehot():
    col = lax.broadcasted_iota(jnp.int32, (1, KEY_WIN), 1)
    off = jnp.where(col < KEY_WIN - LANES, col, col - KEY_WIN)
    idx = jnp.clip(8 * CHUNK - off, -(CHUNK - 1), LANES) + (CHUNK - 1)
    return (lax.broadcasted_iota(jnp.int32, (N_REL, KEY_WIN), 0) == idx).astype(F32)


def bias_blocks(rel16):
    heads = TOK_WIDTH // HEAD_DIM

    def body(rel_ref, o_ref, u_ref):
        u_ref[...] = jnp.dot(rel_ref[...], _rel_onehot(), precision=HIGHEST, preferred_element_type=F32)
        row = lax.broadcasted_iota(jnp.int32, (CHUNK, KEY_WIN), 0)
        col = lax.broadcasted_iota(jnp.int32, (CHUNK, KEY_WIN), 1)
        for h in range(heads):
            xv = jnp.broadcast_to(u_ref[h:h + 1, :], (CHUNK, KEY_WIN))
            for b in range(6):
                xv = jnp.where(((row >> b) & 1) == 1, pltpu.roll(xv, 1 << b, axis=1), xv)
            xv = jnp.where(col < BAND, xv, NEG_INF)
            for i in range(Q_BLOCK // CHUNK):
                o_ref[h, CHUNK * i:CHUNK * (i + 1), :] = pltpu.roll(xv, CHUNK * i, axis=1) if i else xv

    return pl.pallas_call(
        body, out_shape=jax.ShapeDtypeStruct((heads, Q_BLOCK, KEY_WIN), F32),
        scratch_shapes=[pltpu.VMEM((16, KEY_WIN), F32)], name="bias_blocks")(rel16)


def bias_grad(dbias):
    heads = dbias.shape[0]

    def body(db_ref, o_ref, y_ref):
        y_ref[...] = jnp.zeros_like(y_ref)
        row = lax.broadcasted_iota(jnp.int32, (CHUNK, KEY_WIN), 0)
        for h in range(heads):
            fv = db_ref[h, 0:CHUNK, :]
            for i in range(1, Q_BLOCK // CHUNK):
                fv = fv + pltpu.roll(db_ref[h, CHUNK * i:CHUNK * (i + 1), :], KEY_WIN - CHUNK * i, axis=1)
            for b in range(6):
                fv = jnp.where(((row >> b) & 1) == 1, pltpu.roll(fv, KEY_WIN - (1 << b), axis=1), fv)
            y_ref[h:h + 1, :] = jnp.sum(fv, axis=0, keepdims=True)
        o_ref[...] = lax.dot_general(y_ref[...], _rel_onehot(), (((1,), (1,)), ((), ())),
                                     precision=HIGHEST, preferred_element_type=F32)

    return pl.pallas_call(
        body, out_shape=jax.ShapeDtypeStruct((16, N_REL), F32),
        scratch_shapes=[pltpu.VMEM((16, KEY_WIN), F32)], name="bias_grad")(dbias)


def _attn_windows(seq):
    out = []
    for j in range(seq // Q_BLOCK):
        r0 = j * Q_BLOCK
        k0 = max(0, r0 - 8 * CHUNK)
        width = r0 + Q_BLOCK - k0
        out.append((r0, k0, width, KEY_WIN - width))
    return out


def attn_fwd(z, gq2, gk2, bias, batch, seq):
    n = z.shape[0]
    pairs = TOK_WIDTH // LANES

    def body(q_ref, k_ref, v_ref, gq_ref, gk_ref, b_ref, o_ref, qs_s, kn_s):
        masks = _group_masks(LANES)
        qs_s[...] = (_head_norm(q_ref[...].astype(F32), gq_ref[...], masks)[0] * ATTN_SCALE).astype(BF16)
        kn_s[...] = _head_norm(k_ref[...].astype(F32), gk_ref[...], masks)[0].astype(BF16)
        for r0, k0, width, c0 in _attn_windows(seq):
            qb = qs_s[r0:r0 + Q_BLOCK, :]
            kw = kn_s[k0:k0 + width, :]
            vw = v_ref[k0:k0 + width, :]
            out = jnp.zeros((Q_BLOCK, LANES), F32)
            for h, msk in enumerate(masks):
                qh = jnp.where(msk, qb, jnp.zeros_like(qb))
                s = _dot(qh, kw, 1, 1) + b_ref[h, :, c0:KEY_WIN]
                p = _softmax_rows(s).astype(BF16)
                out = jnp.where(msk, _dot(p, vw, 1, 0), out)
            o_ref[r0:r0 + Q_BLOCK, :] = out.astype(o_ref.dtype)

    def col(off):
        return pl.BlockSpec((seq, LANES), lambda b, p: (b, off + p))

    vec = pl.BlockSpec((1, LANES), lambda b, p: (0, 0))
    return pl.pallas_call(
        body, out_shape=jax.ShapeDtypeStruct((n, D_MODEL), BF16), grid=(batch, pairs),
        in_specs=[col(0), col(pairs), col(2 * pairs), vec, vec,
                  pl.BlockSpec((2, Q_BLOCK, KEY_WIN), lambda b, p: (p, 0, 0))],
        out_specs=pl.BlockSpec((seq, LANES), lambda b, p: (b, p)),
        scratch_shapes=[pltpu.VMEM((seq, LANES), BF16), pltpu.VMEM((seq, LANES), BF16)],
        compiler_params=_params("parallel", "arbitrary"), name="attn_fwd")(z, z, z, gq2, gk2, bias)


def attn_bwd(z, dcat, gq2, gk2, bias, batch, seq):
    n = z.shape[0]
    pairs = TOK_WIDTH // LANES

    def body(q_ref, k_ref, v_ref, do_ref, gq_ref, gk_ref, b_ref,
             dq_ref, dk_ref, dv_ref, db_ref, dgq_ref, dgk_ref, qs_s, kn_s, dqn_s, dkn_s, dv_s):
        pi, bi = pl.program_id(0), pl.program_id(1)
        masks = _group_masks(LANES)

        @pl.when(bi == 0)
        def _():
            db_ref[...] = jnp.zeros_like(db_ref)

        @pl.when((bi == 0) & (pi == 0))
        def _():
            dgq_ref[...] = jnp.zeros_like(dgq_ref)
            dgk_ref[...] = jnp.zeros_like(dgk_ref)

        qn, qhat, rq = _head_norm(q_ref[...].astype(F32), gq_ref[...], masks)
        kn, khat, rk = _head_norm(k_ref[...].astype(F32), gk_ref[...], masks)
        qs_s[...] = (qn * ATTN_SCALE).astype(BF16)
        kn_s[...] = kn.astype(BF16)
        dkn_s[...] = jnp.zeros_like(dkn_s)
        dv_s[...] = jnp.zeros_like(dv_s)
        for r0, k0, width, c0 in _attn_windows(seq):
            qb = qs_s[r0:r0 + Q_BLOCK, :]
            dob = do_ref[r0:r0 + Q_BLOCK, :]
            kw = kn_s[k0:k0 + width, :]
            vw = v_ref[k0:k0 + width, :]
            dq_acc = jnp.zeros((Q_BLOCK, LANES), F32)
            dk_acc = jnp.zeros((width, LANES), F32)
            dv_acc = jnp.zeros((width, LANES), F32)
            for h, msk in enumerate(masks):
                qh = jnp.where(msk, qb, jnp.zeros_like(qb))
                doh = jnp.where(msk, dob, jnp.zeros_like(dob))
                p = _softmax_rows(_dot(qh, kw, 1, 1) + b_ref[h, :, c0:KEY_WIN])
                dp = _dot(doh, vw, 1, 1)
                ds = p * (dp - jnp.sum(p * dp, axis=-1, keepdims=True))
                db_ref[h, :, c0:KEY_WIN] += ds
                dsb = ds.astype(BF16)
                dq_acc = jnp.where(msk, _dot(dsb, kw, 1, 0), dq_acc)
                dk_acc = jnp.where(msk, _dot(dsb, qb, 0, 0), dk_acc)
                dv_acc = jnp.where(msk, _dot(p.astype(BF16), dob, 0, 0), dv_acc)
            dqn_s[r0:r0 + Q_BLOCK, :] = dq_acc * ATTN_SCALE
            dkn_s[k0:k0 + width, :] += dk_acc
            dv_s[k0:k0 + width, :] += dv_acc
        dq, dgq = _head_norm_bwd(dqn_s[...], qhat, rq, gq_ref[...], masks)
        dk, dgk = _head_norm_bwd(dkn_s[...], khat, rk, gk_ref[...], masks)
        dq_ref[...] = dq.astype(dq_ref.dtype)
        dk_ref[...] = dk.astype(dk_ref.dtype)
        dv_ref[...] = dv_s[...].astype(dv_ref.dtype)
        dgq_ref[...] += dgq
        dgk_ref[...] += dgk

    def col(off):
        return pl.BlockSpec((seq, LANES), lambda p, b: (b, off + p))

    vec = pl.BlockSpec((1, LANES), lambda p, b: (0, 0))
    blk = pl.BlockSpec((2, Q_BLOCK, KEY_WIN), lambda p, b: (p, 0, 0))
    o_shape = jax.ShapeDtypeStruct((n, TOK_WIDTH), BF16)
    v_shape = jax.ShapeDtypeStruct((1, LANES), F32)
    return pl.pallas_call(
        body,
        out_shape=(o_shape, o_shape, o_shape, jax.ShapeDtypeStruct(bias.shape, F32), v_shape, v_shape),
        grid=(pairs, batch),
        in_specs=[col(0), col(pairs), col(2 * pairs), col(0), vec, vec, blk],
        out_specs=(col(0), col(0), col(0), blk, vec, vec),
        scratch_shapes=[pltpu.VMEM((seq, LANES), BF16), pltpu.VMEM((seq, LANES), BF16),
                        pltpu.VMEM((seq, LANES), F32), pltpu.VMEM((seq, LANES), F32), pltpu.VMEM((seq, LANES), F32)],
        compiler_params=_params("arbitrary", "arbitrary"), name="attn_bwd")(z, z, z, dcat, gq2, gk2, bias)


MEM_ROWS_FWD = 1024
MEM_ROWS_BWD = 2048


def memattn_fwd(z, mem, mem_gain, wkv, gq4, gk4, cat, batch, seq, qcol, name):
    mtok = mem.shape[0] // batch
    d = mem.shape[1]
    rows = min(MEM_ROWS_FWD, seq)

    def body(q_ref, m_ref, mg_ref, w_ref, gq_ref, gk_ref, cat_ref, o_ref, n_ref, kv_ref):
        del cat_ref
        masks = _group_masks(MEM_WIDTH)
        mv = m_ref[...]
        r = lax.rsqrt(jnp.mean(mv * mv, axis=-1, keepdims=True) + NORM_EPS)
        nv = (mv * r * mg_ref[...]).astype(BF16)
        n_ref[...] = nv
        kv_ref[...] = _dot(nv, w_ref[...], 1, 0)
        kn = _head_norm(kv_ref[:, 0:MEM_WIDTH], gk_ref[...], masks)[0].astype(BF16)
        vm = kv_ref[:, MEM_WIDTH:2 * MEM_WIDTH].astype(BF16)
        for t in range(seq // rows):
            sl = slice(t * rows, (t + 1) * rows)
            qs = (_head_norm(q_ref[sl, :].astype(F32), gq_ref[...], masks)[0] * ATTN_SCALE).astype(BF16)
            out = jnp.zeros((rows, MEM_WIDTH), F32)
            for msk in masks:
                qh = jnp.where(msk, qs, jnp.zeros_like(qs))
                p = _softmax_rows(_dot(qh, kn, 1, 1)).astype(BF16)
                out = jnp.where(msk, _dot(p, vm, 1, 0), out)
            o_ref[sl, :] = out.astype(o_ref.dtype)

    vec = pl.BlockSpec((1, MEM_WIDTH), lambda b: (0, 0))
    mem_spec = pl.BlockSpec((mtok, d), lambda b: (b, 0))
    kv_spec = pl.BlockSpec((mtok, 2 * MEM_WIDTH), lambda b: (b, 0))
    return pl.pallas_call(
        body, out_shape=(jax.ShapeDtypeStruct(cat.shape, cat.dtype), jax.ShapeDtypeStruct(mem.shape, BF16),
                         jax.ShapeDtypeStruct((mem.shape[0], 2 * MEM_WIDTH), F32)), grid=(batch,),
        in_specs=[pl.BlockSpec((seq, MEM_WIDTH), lambda b: (b, qcol)), mem_spec, pl.BlockSpec((1, d), lambda b: (0, 0)),
                  pl.BlockSpec(wkv.shape, lambda b: (0, 0)), vec, vec, ANY],
        out_specs=(pl.BlockSpec((seq, MEM_WIDTH), lambda b: (b, TOK_WIDTH // MEM_WIDTH)), mem_spec, kv_spec),
        input_output_aliases={6: 0},
        compiler_params=_params("parallel"), name=name)(z, mem, mem_gain, wkv, gq4, gk4, cat)


def memattn_bwd(z, kv, dcat, gq4, gk4, mem, mem_n, wkv, dz, batch, seq, qcol, name):
    mtok = kv.shape[0] // batch
    d = mem.shape[1]
    rows = min(MEM_ROWS_BWD, seq)

    def body(q_ref, kv_ref, do_ref, gq_ref, gk_ref, m_ref, n_ref, w_ref, *rest):
        dq_ref, dgq_ref, dgk_ref, dw_ref, dmg_ref, dw_acc = rest[-6:]

        @pl.when(pl.program_id(0) == 0)
        def _():
            dgq_ref[...] = jnp.zeros_like(dgq_ref)
            dgk_ref[...] = jnp.zeros_like(dgk_ref)
            dmg_ref[...] = jnp.zeros_like(dmg_ref)
            dw_acc[...] = jnp.zeros_like(dw_acc)

        masks = _group_masks(MEM_WIDTH)
        kn_f, khat, rk = _head_norm(kv_ref[:, 0:MEM_WIDTH], gk_ref[...], masks)
        kn = kn_f.astype(BF16)
        vm = kv_ref[:, MEM_WIDTH:2 * MEM_WIDTH].astype(BF16)
        dkn = jnp.zeros((mtok, MEM_WIDTH), F32)
        dvm = jnp.zeros((mtok, MEM_WIDTH), F32)
        dgq = jnp.zeros((1, MEM_WIDTH), F32)
        for t in range(seq // rows):
            sl = slice(t * rows, (t + 1) * rows)
            qn_f, qhat, rq = _head_norm(q_ref[sl, :].astype(F32), gq_ref[...], masks)
            qs = (qn_f * ATTN_SCALE).astype(BF16)
            dob = do_ref[sl, :]
            dqn = jnp.zeros((rows, MEM_WIDTH), F32)
            for msk in masks:
                qh = jnp.where(msk, qs, jnp.zeros_like(qs))
                doh = jnp.where(msk, dob, jnp.zeros_like(dob))
                p = _softmax_rows(_dot(qh, kn, 1, 1))
                dp = _dot(doh, vm, 1, 1)
                ds = p * (dp - jnp.sum(p * dp, axis=-1, keepdims=True))
                dsb = ds.astype(BF16)
                dqn = jnp.where(msk, _dot(dsb, kn, 1, 0), dqn)
                dkn = dkn + jnp.where(msk, _dot(dsb, qs, 0, 0), 0.0)
                dvm = dvm + jnp.where(msk, _dot(p.astype(BF16), dob, 0, 0), 0.0)
            dq, dg = _head_norm_bwd(dqn * ATTN_SCALE, qhat, rq, gq_ref[...], masks)
            dq_ref[sl, :] = dq.astype(dq_ref.dtype)
            dgq = dgq + dg
        dk, dgk = _head_norm_bwd(dkn, khat, rk, gk_ref[...], masks)
        dgq_ref[...] += dgq
        dgk_ref[...] += dgk
        dkv_b = jnp.concatenate([dk, dvm], axis=-1).astype(BF16)
        dw_acc[...] += _dot(n_ref[...], dkv_b, 0, 0)
        dn = _dot(dkv_b, w_ref[...], 1, 1)
        mv = m_ref[...]
        rm = lax.rsqrt(jnp.mean(mv * mv, axis=-1, keepdims=True) + NORM_EPS)
        dmg_ref[...] += jnp.sum(dn * (mv * rm), axis=0, keepdims=True)

        @pl.when(pl.program_id(0) == batch - 1)
        def _():
            dw_ref[...] = dw_acc[...].astype(dw_ref.dtype)

    vec = pl.BlockSpec((1, MEM_WIDTH), lambda b: (0, 0))
    kv_spec = pl.BlockSpec((mtok, 2 * MEM_WIDTH), lambda b: (b, 0))
    mem_spec = pl.BlockSpec((mtok, d), lambda b: (b, 0))
    w_spec = pl.BlockSpec(wkv.shape, lambda b: (0, 0))
    v_shape = jax.ShapeDtypeStruct((1, MEM_WIDTH), F32)
    q_spec = pl.BlockSpec((seq, MEM_WIDTH), lambda b: (b, qcol))
    in_specs = [q_spec, kv_spec, pl.BlockSpec((seq, MEM_WIDTH), lambda b: (b, TOK_WIDTH // MEM_WIDTH)), vec, vec,
                mem_spec, mem_spec, w_spec]
    args = [z, kv, dcat, gq4, gk4, mem, mem_n, wkv]
    if dz is None:
        dq_shape, dq_spec, aliases = jax.ShapeDtypeStruct((z.shape[0], MEM_WIDTH), BF16), \
            pl.BlockSpec((seq, MEM_WIDTH), lambda b: (b, 0)), {}
    else:
        dq_shape, dq_spec, aliases = jax.ShapeDtypeStruct(dz.shape, dz.dtype), q_spec, {len(args): 0}
        in_specs.append(ANY)
        args.append(dz)
    return pl.pallas_call(
        body,
        out_shape=(dq_shape, v_shape, v_shape, jax.ShapeDtypeStruct(wkv.shape, BF16), jax.ShapeDtypeStruct((1, d), F32)),
        grid=(batch,), in_specs=in_specs,
        out_specs=(dq_spec, vec, vec, w_spec, pl.BlockSpec((1, d), lambda b: (0, 0))),
        scratch_shapes=[pltpu.VMEM(wkv.shape, F32)], input_output_aliases=aliases,
        compiler_params=_params("arbitrary"), name=name)(*args)


CONV_ROWS = 512


def _glu(a_ref, g_ref):
    return a_ref[...].astype(F32) * _sigmoid(g_ref[...].astype(F32))


def _layer_norm_stats(y):
    mu = jnp.mean(y, axis=-1, keepdims=True)
    yc = y - mu
    rstd = lax.rsqrt(jnp.mean(yc * yc, axis=-1, keepdims=True) + NORM_EPS)
    return yc * rstd, rstd


CONV_WIN = CONV_HALO + CONV_ROWS
SUBLANES = 8
SHIFT_ROWS = CONV_WIN - SUBLANES


def _preshift(win, shifted):
    for s in range(1, SUBLANES):
        shifted[s - 1, :, :] = win[s:s + SHIFT_ROWS, :]


TAP_ROWS = 16
TAP_TILES = [(r0, slice(c0, c0 + LANES)) for c0 in range(0, TOK_WIDTH, LANES) for r0 in range(0, CONV_ROWS, TAP_ROWS)]


def _tap(win, shifted, off, r0, lanes):
    s = off % SUBLANES
    base = off - s + r0
    if s == 0:
        return win[base:base + TAP_ROWS, lanes]
    return shifted[s - 1, base:base + TAP_ROWS, lanes]


def _fold_rows(x):
    return jnp.sum(x.reshape(TAP_ROWS // SUBLANES, SUBLANES, LANES), axis=0)


def conv_fwd(z, cw, cb, lg, lb, batch, seq):
    n = z.shape[0]
    nt = seq // CONV_ROWS
    sub = CONV_ROWS // CONV_HALO
    lead = CONV_HALO - (CONV_W - 1)

    def body(a_ref, g_ref, ap_ref, gp_ref, cw_ref, cb_ref, lg_ref, lb_ref, o_ref, y_ref, win, shifted):
        first = pl.program_id(1) == 0
        win[0:CONV_HALO, :] = jnp.where(first, 0.0, _glu(ap_ref, gp_ref))
        win[CONV_HALO:CONV_WIN, :] = _glu(a_ref, g_ref)
        _preshift(win, shifted)
        for r0, lanes in TAP_TILES:
            acc = jnp.zeros((TAP_ROWS, LANES), F32) + cb_ref[:, lanes]
            for w in range(CONV_W):
                acc = acc + _tap(win, shifted, lead + w, r0, lanes) * cw_ref[w:w + 1, lanes]
            y_ref[r0:r0 + TAP_ROWS, lanes] = acc
        yh, _ = _layer_norm_stats(y_ref[...])
        t = yh * lg_ref[...] + lb_ref[...]
        o_ref[...] = (t * _sigmoid(t)).astype(o_ref.dtype)

    def cur(c):
        return pl.BlockSpec((CONV_ROWS, TOK_WIDTH), lambda b, i: (b * nt + i, c))

    def prev(c):
        return pl.BlockSpec((CONV_HALO, TOK_WIDTH), lambda b, i: (jnp.maximum((b * nt + i) * sub - 1, 0), c))

    vec = pl.BlockSpec((1, TOK_WIDTH), lambda b, i: (0, 0))
    return pl.pallas_call(
        body, out_shape=(jax.ShapeDtypeStruct((n, D_MODEL), BF16), jax.ShapeDtypeStruct((n, TOK_WIDTH), F32)),
        grid=(batch, nt),
        in_specs=[cur(0), cur(1), prev(0), prev(1), pl.BlockSpec((32, TOK_WIDTH), lambda b, i: (0, 0)), vec, vec, vec],
        out_specs=(cur(0), cur(0)),
        scratch_shapes=[pltpu.VMEM((CONV_WIN, TOK_WIDTH), F32), pltpu.VMEM((SUBLANES - 1, SHIFT_ROWS, TOK_WIDTH), F32)],
        compiler_params=_params("parallel", "arbitrary"), name="conv_fwd")(z, z, z, z, cw, cb, lg, lb)


def conv_bwd(z, y, dcat, cw, lg, lb, batch, seq):
    n = z.shape[0]
    nt = seq // CONV_ROWS
    sub = CONV_ROWS // CONV_HALO
    lead = CONV_HALO - (CONV_W - 1)
    last_blk = n // CONV_HALO - 1

    def body(a_ref, g_ref, ap_ref, gp_ref, y_ref, yn_ref, do_ref, don_ref, cw_ref, lg_ref, lb_ref,
             dz_ref, dcw_ref, dsm_ref, win, shifted, dyw, dshifted, dg_o):
        b, i, which = pl.program_id(0), pl.program_id(1), pl.program_id(2)

        @pl.when(which == 0)
        def _():
            first, last = i == 0, i == nt - 1

            @pl.when((b == 0) & (i == 0))
            def _():
                dcw_ref[...] = jnp.zeros_like(dcw_ref)
                dsm_ref[...] = jnp.zeros_like(dsm_ref)

            win[0:CONV_HALO, :] = jnp.where(first, 0.0, _glu(ap_ref, gp_ref))
            win[CONV_HALO:CONV_WIN, :] = _glu(a_ref, g_ref)
            _preshift(win, shifted)
            yv = jnp.concatenate([y_ref[...], yn_ref[...]], axis=0)
            yh, rstd = _layer_norm_stats(yv)
            t = yh * lg_ref[...] + lb_ref[...]
            st = _sigmoid(t)
            dout = jnp.concatenate(
                [do_ref[...].astype(F32), jnp.where(last, 0.0, don_ref[...].astype(F32))], axis=0)
            dt = dout * st * (1.0 + t * (1.0 - st))
            dyh = dt * lg_ref[...]
            dy = rstd * (dyh - jnp.mean(dyh, axis=-1, keepdims=True)
                         - yh * jnp.mean(dyh * yh, axis=-1, keepdims=True))
            dyw[...] = dy
            _preshift(dyw, dshifted)
            dsm_ref[0:1, :] += jnp.sum(dy[0:CONV_ROWS], axis=0, keepdims=True)
            dsm_ref[1:2, :] += jnp.sum((dt * yh)[0:CONV_ROWS], axis=0, keepdims=True)
            dsm_ref[2:3, :] += jnp.sum(dt[0:CONV_ROWS], axis=0, keepdims=True)
            for c0 in range(0, TOK_WIDTH, LANES):
                lanes = slice(c0, c0 + LANES)
                dcw_acc = [jnp.zeros((SUBLANES, LANES), F32) for _ in range(CONV_W)]
                for r0 in range(0, CONV_ROWS, TAP_ROWS):
                    dyt = dyw[r0:r0 + TAP_ROWS, lanes]
                    dglu = jnp.zeros((TAP_ROWS, LANES), F32)
                    for w in range(CONV_W):
                        dcw_acc[w] = dcw_acc[w] + _fold_rows(dyt * _tap(win, shifted, lead + w, r0, lanes))
                        dglu = dglu + _tap(dyw, dshifted, CONV_W - 1 - w, r0, lanes) * cw_ref[w:w + 1, lanes]
                    avt = a_ref[r0:r0 + TAP_ROWS, lanes].astype(F32)
                    sgt = _sigmoid(g_ref[r0:r0 + TAP_ROWS, lanes].astype(F32))
                    dz_ref[r0:r0 + TAP_ROWS, lanes] = (dglu * sgt).astype(dz_ref.dtype)
                    dg_o[r0:r0 + TAP_ROWS, lanes] = (dglu * avt * sgt * (1.0 - sgt)).astype(dg_o.dtype)
                for w in range(CONV_W):
                    dcw_ref[w:w + 1, lanes] += jnp.sum(dcw_acc[w], axis=0, keepdims=True)

        @pl.when(which == 1)
        def _():
            dz_ref[...] = dg_o[...]

    def ahead(b, i, t):
        return jnp.minimum(b * nt + i + t, batch * nt - 1)

    def cur(c):
        return pl.BlockSpec((CONV_ROWS, TOK_WIDTH), lambda b, i, t: (ahead(b, i, t), c))

    def prev(c):
        return pl.BlockSpec((CONV_HALO, TOK_WIDTH), lambda b, i, t: (jnp.maximum(ahead(b, i, t) * sub - 1, 0), c))

    nxt = pl.BlockSpec((CONV_HALO, TOK_WIDTH),
                       lambda b, i, t: (jnp.minimum((ahead(b, i, t) + 1) * sub, last_blk), 0))
    vec = pl.BlockSpec((1, TOK_WIDTH), lambda b, i, t: (0, 0))
    full32 = pl.BlockSpec((32, TOK_WIDTH), lambda b, i, t: (0, 0))
    return pl.pallas_call(
        body,
        out_shape=(jax.ShapeDtypeStruct(z.shape, BF16), jax.ShapeDtypeStruct((32, TOK_WIDTH), F32),
                   jax.ShapeDtypeStruct((8, TOK_WIDTH), F32)),
        grid=(batch, nt, 2),
        in_specs=[cur(0), cur(1), prev(0), prev(1), cur(0), nxt, cur(0), nxt, full32, vec, vec],
        out_specs=(pl.BlockSpec((CONV_ROWS, TOK_WIDTH), lambda b, i, t: (b * nt + i, t)), full32,
                   pl.BlockSpec((8, TOK_WIDTH), lambda b, i, t: (0, 0))),
        scratch_shapes=[pltpu.VMEM((CONV_WIN, TOK_WIDTH), F32), pltpu.VMEM((SUBLANES - 1, SHIFT_ROWS, TOK_WIDTH), F32),
                        pltpu.VMEM((CONV_WIN, TOK_WIDTH), F32), pltpu.VMEM((SUBLANES - 1, SHIFT_ROWS, TOK_WIDTH), F32),
                        pltpu.VMEM((CONV_ROWS, TOK_WIDTH), BF16)],
        compiler_params=_params("arbitrary", "arbitrary", "arbitrary"), name="conv_bwd")(
            z, z, z, z, y, y, dcat, dcat, cw, lg, lb)


def _place():
    return lax.axis_index("x"), lax.axis_index("y"), lax.axis_index("c")


def _other_chips(x, y):
    return [(1 - x, y), (x, 1 - y), (1 - x, 1 - y)]


def reduce_small(arrays):
    na = len(arrays)

    def body(*refs):
        ins, outs, bufs = refs[:na], refs[na:2 * na], refs[2 * na:3 * na]
        send_sems, recv_sems = refs[3 * na:]
        x, y, c = _place()
        me = 4 * x + 2 * y + c
        copies = []
        for a in range(na):
            bufs[a][me] = ins[a][...]
            for k in range(1, N_DEV):
                cp = pltpu.make_async_remote_copy(
                    src_ref=ins[a], dst_ref=bufs[a].at[me], send_sem=send_sems.at[a, k - 1],
                    recv_sem=recv_sems.at[a, k - 1],
                    device_id=(x ^ (k >> 2), y ^ ((k >> 1) & 1), c ^ (k & 1)), device_id_type=MESH)
                cp.start()
                copies.append(cp)
        for a in range(na):
            for k in range(1, N_DEV):
                src = 4 * (x ^ (k >> 2)) + 2 * (y ^ ((k >> 1) & 1)) + (c ^ (k & 1))
                pltpu.make_async_remote_copy(
                    src_ref=ins[a], dst_ref=bufs[a].at[src], send_sem=send_sems.at[a, k - 1],
                    recv_sem=recv_sems.at[a, k - 1], device_id=(x, y, c), device_id_type=MESH).wait_recv()
        for cp in copies:
            cp.wait_send()
        for a in range(na):
            total = bufs[a][0]
            for dev in range(1, N_DEV):
                total = total + bufs[a][dev]
            outs[a][...] = total

    vmem = pl.BlockSpec(memory_space=pltpu.VMEM)
    return pl.pallas_call(
        body, out_shape=tuple(jax.ShapeDtypeStruct(a.shape, F32) for a in arrays),
        in_specs=[vmem] * na, out_specs=tuple([vmem] * na),
        scratch_shapes=[pltpu.VMEM((N_DEV,) + a.shape, F32) for a in arrays]
        + [pltpu.SemaphoreType.DMA((na, N_DEV - 1)), pltpu.SemaphoreType.DMA((na, N_DEV - 1))],
        compiler_params=pltpu.CompilerParams(vmem_limit_bytes=VMEM_LIMIT), name="small_reduce")(*arrays)


def adamw_small(ws, gs, ms, vs):
    na = len(ws)
    c1 = 1.0 / (1.0 - ADAM_B1 ** ADAM_STEP)
    c2 = 1.0 / (1.0 - ADAM_B2 ** ADAM_STEP)

    def body(*refs):
        w_refs, g_refs, m_refs, v_refs = (refs[i * na:(i + 1) * na] for i in range(4))
        d_refs, nm_refs, nv_refs = (refs[(4 + i) * na:(5 + i) * na] for i in range(3))
        for a in range(na):
            gv = g_refs[a][...]
            nm = ADAM_B1 * m_refs[a][...] + (1.0 - ADAM_B1) * gv
            nv = ADAM_B2 * v_refs[a][...] + (1.0 - ADAM_B2) * (gv * gv)
            nm_refs[a][...] = nm
            nv_refs[a][...] = nv
            d_refs[a][...] = -ADAM_LR * ((nm * c1) / (jnp.sqrt(nv * c2) + ADAM_EPS) + ADAM_WD * w_refs[a][...])

    vmem = pl.BlockSpec(memory_space=pltpu.VMEM)
    shapes = tuple(jax.ShapeDtypeStruct(w.shape, F32) for w in ws)
    outs = pl.pallas_call(
        body, out_shape=shapes * 3, in_specs=[vmem] * (4 * na), out_specs=tuple([vmem] * (3 * na)),
        compiler_params=pltpu.CompilerParams(vmem_limit_bytes=VMEM_LIMIT), name="adamw_small")(*ws, *gs, *ms, *vs)
    return outs[:na], outs[na:2 * na], outs[2 * na:]


def gather_weights(shards, name, collective_id):
    nw = len(shards)
    ns = [s.shape[0] for s in shards]
    in_refs = [jax.new_ref(s, memory_space=pltpu.MemorySpace.HBM) for s in shards]
    out_refs = [jax.empty_ref(jax.ShapeDtypeStruct((N_DEV * s.shape[0], s.shape[1]), s.dtype),
                              memory_space=pltpu.MemorySpace.HBM) for s in shards]

    @pl.kernel(mesh=plsc.ScalarSubcoreMesh(axis_name="seq", num_cores=1), name=name,
               scratch_types=(pltpu.SemaphoreType.DMA((nw, 7)), pltpu.SemaphoreType.DMA((nw, 7)),
                              pltpu.SemaphoreType.DMA((nw,))),
               compiler_params=pltpu.CompilerParams(collective_id=collective_id))
    def launch(send_sems, recv_sems, local_sems):
        x, y, c = _place()
        me, sib = (x, y, c), (x, y, 1 - c)
        chips = _other_chips(x, y)
        barrier = pltpu.get_barrier_semaphore()
        for peer in [sib] + [(*chip, c) for chip in chips]:
            pl.semaphore_signal(barrier, inc=1, device_id=peer, device_id_type=MESH)
        pl.semaphore_wait(barrier, 4)

        def rows(w, dev):
            return out_refs[w].at[pl.ds((4 * dev[0] + 2 * dev[1] + dev[2]) * ns[w], ns[w]), :]

        def copy(w, k, block, to, src=None):
            return pltpu.make_async_remote_copy(
                src_ref=rows(w, block) if src is None else src, dst_ref=rows(w, block),
                send_sem=send_sems.at[w, k], recv_sem=recv_sems.at[w, k], device_id=to, device_id_type=MESH)

        started, sends = [], []
        for w in range(nw):
            mine = pltpu.make_async_copy(in_refs[w], rows(w, me), local_sems.at[w])
            mine.start()
            started.append(mine)
            first = [copy(w, 0, me, sib, src=in_refs[w])]
            first += [copy(w, 1 + j, me, (*chip, c), src=in_refs[w]) for j, chip in enumerate(chips)]
            for cp in first:
                cp.start()
            sends += first
        for w in range(nw):
            for j, chip in enumerate(chips):
                copy(w, 1 + j, (*chip, c), me).wait_recv()
                fwd = copy(w, 4 + j, (*chip, c), sib)
                fwd.start()
                sends.append(fwd)
        for w in range(nw):
            copy(w, 0, sib, me).wait_recv()
            for j, chip in enumerate(chips):
                copy(w, 4 + j, (*chip, 1 - c), me).wait_recv()
        for cp in sends:
            cp.wait_send()
        for mine in started:
            mine.wait()

    launch()
    return [r[...] for r in out_refs]


def _sequencer_exchange(sources, out_rows, peers_of, copies_of, name, collective_id):
    nw = len(sources)
    in_refs = [jax.new_ref(s, memory_space=pltpu.MemorySpace.HBM) for s in sources]
    out_refs = [jax.empty_ref(jax.ShapeDtypeStruct((rows, s.shape[1]), s.dtype), memory_space=pltpu.MemorySpace.HBM)
                for rows, s in zip(out_rows, sources)]
    per = len(copies_of(0, 0, 0, 0))

    @pl.kernel(mesh=plsc.ScalarSubcoreMesh(axis_name="seq", num_cores=1), name=name,
               scratch_types=(pltpu.SemaphoreType.DMA((nw, per)), pltpu.SemaphoreType.DMA((nw, per))),
               compiler_params=pltpu.CompilerParams(collective_id=collective_id))
    def launch(send_sems, recv_sems):
        x, y, c = _place()
        peers = peers_of(x, y, c)
        barrier = pltpu.get_barrier_semaphore()
        for peer in peers:
            pl.semaphore_signal(barrier, inc=1, device_id=peer, device_id_type=MESH)
        pl.semaphore_wait(barrier, len(peers))
        copies = []
        for w in range(nw):
            for k, (src_blk, dst_blk, rows, peer) in enumerate(copies_of(x, y, c, w)):
                cp = pltpu.make_async_remote_copy(
                    src_ref=in_refs[w].at[pl.ds(src_blk * rows, rows), :],
                    dst_ref=out_refs[w].at[pl.ds(dst_blk * rows, rows), :],
                    send_sem=send_sems.at[w, k], recv_sem=recv_sems.at[w, k], device_id=peer, device_id_type=MESH)
                cp.start()
                copies.append(cp)
        for cp in copies:
            cp.wait_recv()
        for cp in copies:
            cp.wait_send()

    launch()
    return [r[...] for r in out_refs]


def scatter_to_sibling(grads, name, collective_id):
    ns = [g.shape[0] // N_DEV for g in grads]
    return _sequencer_exchange(
        grads, [4 * n for n in ns],
        lambda x, y, c: [(x, y, 1 - c)],
        lambda x, y, c, w: [(2 * q + 1 - c, q, ns[w], (x, y, 1 - c)) for q in range(4)],
        name, collective_id)


def scatter_to_chips(parts, name, collective_id):
    ns = [p.shape[0] // 4 for p in parts]
    return _sequencer_exchange(
        parts, [3 * n for n in ns],
        lambda x, y, c: [(*chip, c) for chip in _other_chips(x, y)],
        lambda x, y, c, w: [(2 * chip[0] + chip[1], j, ns[w], (*chip, c)) for j, chip in enumerate(_other_chips(x, y))],
        name, collective_id)


def add_sibling(grads, landeds, core, name):
    nw = len(grads)

    def body(c_ref, *refs):
        for w in range(nw):
            g_ref, l_ref, o_ref = refs[2 * w], refs[2 * w + 1], refs[2 * nw + w]
            o_ref[...] = (g_ref[...].astype(F32) + l_ref[...].astype(F32)).astype(o_ref.dtype)

    in_specs, out_specs, args = [], [], []
    for g, ld in zip(grads, landeds):
        n, cols = ld.shape[0] // 4, g.shape[1]
        in_specs += [pl.BlockSpec((n, cols), lambda q, c_ref: (2 * q + c_ref[0], 0)),
                     pl.BlockSpec((n, cols), lambda q, c_ref: (q, 0))]
        out_specs.append(pl.BlockSpec((n, cols), lambda q, c_ref: (q, 0)))
        args += [g, ld]
    grid_spec = pltpu.PrefetchScalarGridSpec(
        num_scalar_prefetch=1, grid=(4,), in_specs=in_specs, out_specs=tuple(out_specs))
    return pl.pallas_call(
        body, out_shape=tuple(jax.ShapeDtypeStruct(ld.shape, ld.dtype) for ld in landeds), grid_spec=grid_spec,
        compiler_params=_params("arbitrary"), name=name)(core, *args)


ADAMW_HALVES = 2


def adamw_shards(items, chip, name):
    c1 = 1.0 / (1.0 - ADAM_B1 ** ADAM_STEP)
    c2 = 1.0 / (1.0 - ADAM_B2 ** ADAM_STEP)
    ni = len(items)

    def body(q_ref, *refs):
        outs = refs[len(refs) - 4 * ni:]
        for k in range(ni):
            w_ref, m_ref, v_ref, p_ref, l0_ref, l1_ref, l2_ref = refs[7 * k:7 * k + 7]
            g_ref, d_ref, nm_ref, nv_ref = outs[4 * k:4 * k + 4]
            gv = ((p_ref[...].astype(F32) + l0_ref[...].astype(F32)) + l1_ref[...].astype(F32)) + l2_ref[...].astype(F32)
            nm = ADAM_B1 * m_ref[...] + (1.0 - ADAM_B1) * gv
            nv = ADAM_B2 * v_ref[...] + (1.0 - ADAM_B2) * (gv * gv)
            g_ref[...] = gv
            nm_ref[...] = nm
            nv_ref[...] = nv
            d_ref[...] = -ADAM_LR * ((nm * c1) / (jnp.sqrt(nv * c2) + ADAM_EPS) + ADAM_WD * w_ref[...])

    sub = ADAMW_HALVES
    in_specs, out_specs, out_shape, args, donated = [], [], [], [chip], []
    for layer, w, m, v, part, landed, earlier in items:
        rows, cols = landed.shape[0] // (3 * sub), w.shape[1]

        def block(first, rows=rows, cols=cols):
            return pl.BlockSpec((rows, cols), lambda i, q_ref: (first(q_ref) * sub + i, 0))

        own = block(lambda q_ref, layer=layer: layer)
        in_specs += [own, own, own, block(lambda q_ref: q_ref[0])] + [block(lambda q_ref, j=j: j) for j in range(3)]
        args += [w, m, v, part, landed, landed, landed]
        out_specs += [own] * 4
        out_shape += [jax.ShapeDtypeStruct(w.shape, F32)] * 4
        donated.append(earlier)
    aliases = {}
    for k, earlier in enumerate(donated):
        if earlier is not None:
            for j in range(4):
                aliases[len(args)] = 4 * k + j
                in_specs.append(ANY)
                args.append(earlier[j])
    grid_spec = pltpu.PrefetchScalarGridSpec(
        num_scalar_prefetch=1, grid=(sub,), in_specs=in_specs, out_specs=tuple(out_specs))
    outs = pl.pallas_call(
        body, out_shape=tuple(out_shape), grid_spec=grid_spec, input_output_aliases=aliases,
        compiler_params=_params("arbitrary"), name=name)(*args)
    return [tuple(outs[4 * k:4 * k + 4]) for k in range(ni)]


def _pack(arrays):
    flat = jnp.concatenate([a.reshape(-1).astype(F32) for a in arrays])
    pad = (-flat.shape[0]) % (8 * LANES)
    return jnp.pad(flat, (0, pad)).reshape(-1, LANES)


def _unpack(slab, shapes):
    flat = slab.reshape(slab.shape[:-2] + (-1,))
    out, off = [], 0
    for shp in shapes:
        size = 1
        for s in shp:
            size *= s
        out.append(flat[..., off:off + size].reshape(flat.shape[:-1] + tuple(shp)))
        off += size
    return out


def kernel(x, mem, norm1_g, mem_norm_g, a_w_in, a_q_g, a_k_g, a_rel_bias, b_w_in, b_b_in, b_conv_w, b_conv_b, b_ln_g, b_ln_b, mq_g, mk_g, w_mem_kv, w_out, norm2_g, w_gate, w_up, w_down, loss_target, m_norm1_g, m_mem_norm_g, m_a_w_in, m_a_q_g, m_a_k_g, m_a_rel_bias, m_b_w_in, m_b_b_in, m_b_conv_w, m_b_conv_b, m_b_ln_g, m_b_ln_b, m_mq_g, m_mk_g, m_w_mem_kv, m_w_out, m_norm2_g, m_w_gate, m_w_up, m_w_down, v_norm1_g, v_mem_norm_g, v_a_w_in, v_a_q_g, v_a_k_g, v_a_rel_bias, v_b_w_in, v_b_b_in, v_b_conv_w, v_b_conv_b, v_b_ln_g, v_b_ln_b, v_mq_g, v_mk_g, v_w_mem_kv, v_w_out, v_norm2_g, v_w_gate, v_w_up, v_w_down):
    batch, seq, d = x.shape
    mtok = mem.shape[1]
    n = batch * seq
    ax, ay, ac = _place()
    me = 4 * ax + 2 * ay + ac
    core_arr = jnp.reshape(ac, (1,)).astype(jnp.int32)
    chip_arr = jnp.reshape(2 * ax + ay, (1,)).astype(jnp.int32)

    def t_bf16(w):
        return jnp.transpose(w).astype(BF16)

    def after(value, *earlier):
        return lax.optimization_barrier((value, *earlier))[0]

    def gather_mix(l, when, name, collective_id):
        srcs = [w_mem_kv[l].astype(BF16), w_out[l].astype(BF16)]
        if l == 1:
            srcs += [t_bf16(b_w_in[0]), _pack([b_b_in, b_conv_w, b_conv_b, b_ln_g, b_ln_b])]
        return gather_weights([after(srcs[0], *when)] + srcs[1:], name, collective_id)

    def gather_ffn(l, when, name, collective_id):
        return gather_weights(
            [after(t_bf16(w_gate[l]), *when), t_bf16(w_up[l]), w_down[l].astype(BF16)], name, collective_id)

    f_loc = b_b_in.shape[1]
    c_loc = b_conv_b.shape[1]

    def two(g):
        return jnp.concatenate([g, g], axis=-1)

    gq2, gk2 = two(a_q_g), two(a_k_g)
    rel16 = jnp.pad(a_rel_bias[0], ((0, 16 - a_rel_bias.shape[1]), (0, 0)))
    bias = bias_blocks(rel16)

    x0 = x.reshape(n, d)
    mem2 = mem.reshape(batch * mtok, d)

    saved = []
    xin = x0
    a_win_t, = gather_weights([t_bf16(a_w_in[0])], "gather_in_a", 1)
    wg_t, wu_t, wd, wo, wkv = [None] * 2, [None] * 2, [None] * 2, [None] * 2, [None] * 2
    h = after(rms_fwd(xin, norm1_g[0:1], name="rms1_fwd_0"), bias)
    target = loss_target.reshape(n, d)
    for l in range(2):
        gq4 = jnp.tile(mq_g[l:l + 1], (1, 4))
        gk4 = jnp.tile(mk_g[l:l + 1], (1, 4))
        y_conv = None
        if l == 0:
            wkv[0], wo[0] = gather_mix(0, (h, a_win_t), "gather_mix_a", 2)
            z = mm_nt(h, a_win_t, name="in_proj_a")
            wg_t[0], wu_t[0], wd[0] = gather_ffn(0, (z, wkv[0]), "gather_ffn_a", 3)
            cat = attn_fwd(z, gq2, gk2, bias, batch, seq)
            wkv[1], wo[1], b_win_t, conv_slabs = gather_mix(1, (cat, wg_t[0]), "gather_mix_b", 4)
            qcol = 3 * TOK_WIDTH // MEM_WIDTH
        else:
            small_shapes = [(f_loc,), (CONV_W, c_loc), (c_loc,), (c_loc,), (c_loc,)]
            bb_g, cw_g, cb_g, lg_g, lb_g = _unpack(conv_slabs.reshape(N_DEV, -1, LANES), small_shapes)
            bb_full = bb_g.reshape(1, -1)
            cw_full = jnp.pad(jnp.transpose(cw_g, (1, 0, 2)).reshape(CONV_W, -1), ((0, 32 - CONV_W), (0, 0)))
            cb_full, lg_full, lb_full = cb_g.reshape(1, -1), lg_g.reshape(1, -1), lb_g.reshape(1, -1)
            z = mm_nt(h, b_win_t, bias=bb_full, name="in_proj_b")
            cat, y_conv = conv_fwd(z, cw_full, cb_full, lg_full, lb_full, batch, seq)
            qcol = 2 * TOK_WIDTH // MEM_WIDTH
        cat, mem_n, kv = memattn_fwd(
            z, mem2, mem_norm_g[l:l + 1], wkv[l], gq4, gk4, cat, batch, seq, qcol, name=f"memattn_fwd_{l}")
        x1, h2 = proj_norm(cat, wo[l], xin, norm2_g[l:l + 1], name=f"out_proj_{l}")
        if l == 0:
            wg_t[1], wu_t[1], wd[1] = gather_ffn(1, (x1, b_win_t), "gather_ffn_b", 5)
        if l == 0:
            gate, up, act, x2, h_next = ffn_fwd(h2, wg_t[0], wu_t[0], wd[0], x1, gain=norm1_g[1:2], name="ffn_fwd_0")
        else:
            gate, up, act, dx_b, loss_blk = ffn_fwd(h2, wg_t[1], wu_t[1], wd[1], x1, target=target, name="ffn_fwd_1")
        saved.append(dict(xin=xin, h=h, mem_n=mem_n, kv=kv, gq4=gq4, gk4=gk4, z=z, qcol=qcol, cat=cat, x1=x1, h2=h2,
                          gate=gate, up=up, act=act, y_conv=y_conv))
        if l == 0:
            xin, h = x2, h_next

    big = {}
    small = {}
    reduced = {}
    groups = 0

    def scatter_siblings(keys):
        nonlocal groups
        gid = groups
        groups += 1
        return gid, keys, scatter_to_sibling([big[k] for k in keys], f"scatter_sibling_{gid}", 8 + 2 * gid)

    def scatter_chips(stage1, when):
        gid, keys, landed1 = stage1
        parts = add_sibling([after(big[keys[0]], when)] + [big[k] for k in keys[1:]], landed1, core_arr,
                            name=f"add_sibling_{gid}")
        landed2 = scatter_to_chips(parts, f"scatter_chips_{gid}", 9 + 2 * gid)
        for k, p, ld in zip(keys, parts, landed2):
            reduced[k] = (p, ld)
        return parts, landed2

    def rows_of(w, transposed):
        w = jnp.swapaxes(w, 1, 2) if transposed else w
        return w.reshape(w.shape[0] * w.shape[1], w.shape[2])

    sharded = {
        "win0": (2, True), "win1": (6, True), "wkv": (14, False), "wo": (15, False),
        "wg": (17, True), "wu": (18, True), "wd": (19, False)}
    weights = [norm1_g, mem_norm_g, a_w_in, a_q_g, a_k_g, a_rel_bias, b_w_in, b_b_in, b_conv_w, b_conv_b, b_ln_g,
               b_ln_b, mq_g, mk_g, w_mem_kv, w_out, norm2_g, w_gate, w_up, w_down]
    moms = [m_norm1_g, m_mem_norm_g, m_a_w_in, m_a_q_g, m_a_k_g, m_a_rel_bias, m_b_w_in, m_b_b_in, m_b_conv_w,
            m_b_conv_b, m_b_ln_g, m_b_ln_b, m_mq_g, m_mk_g, m_w_mem_kv, m_w_out, m_norm2_g, m_w_gate, m_w_up, m_w_down]
    vels = [v_norm1_g, v_mem_norm_g, v_a_w_in, v_a_q_g, v_a_k_g, v_a_rel_bias, v_b_w_in, v_b_b_in, v_b_conv_w,
            v_b_conv_b, v_b_ln_g, v_b_ln_b, v_mq_g, v_mk_g, v_w_mem_kv, v_w_out, v_norm2_g, v_w_gate, v_w_up, v_w_down]
    updated = {}

    def update_layer(l, when):
        for group, keys in (("ffn", ("wg", "wu", "wd")), ("mix", (f"win{l}", "wkv", "wo"))):
            items = []
            for key in keys:
                idx, transposed = sharded[key]
                layer, rkey = (0, key) if key.startswith("win") else (l, f"{key}{l}")
                part, landed = reduced[rkey]
                w_rows = rows_of(weights[idx], transposed)
                items.append((layer, after(w_rows, when) if not items else w_rows, rows_of(moms[idx], transposed),
                              rows_of(vels[idx], transposed), part, landed, updated.get(key)))
            for key, result in zip(keys, adamw_shards(items, chip_arr, name=f"adamw_{group}_{l}")):
                updated[key] = result

    mix_landed = None
    for l in (1, 0):
        sv = saved[l]
        dgate, dup, dx1_b, dcat, small[f"norm2_{l}"] = ffn_bwd(
            dx_b, wd[l], sv["gate"], sv["up"], wg_t[l], wu_t[l], sv["x1"], norm2_g[l:l + 1], wo[l], name=f"ffn_bwd_{l}")
        if l == 0:
            dgate = after(dgate, *mix_landed)
            update_layer(1, dx1_b)
        big[f"wg{l}"], big[f"wu{l}"], big[f"wd{l}"] = ffn_weight_grads(
            dgate, dup, sv["h2"], sv["act"], dx_b, name=f"grad_ffn_{l}")
        stage1 = scatter_siblings([f"wd{l}", f"wg{l}", f"wu{l}"])
        big[f"wo{l}"] = mm_tn(sv["cat"], dx1_b, name=f"grad_wo_{l}")
        parts, ffn_landed = scatter_chips(stage1, big[f"wo{l}"])
        dcat = after(dcat, *parts)
        if l == 0:
            dq, dk, dv, dbias, small["a_q"], small["a_k"] = attn_bwd(sv["z"], dcat, gq2, gk2, bias, batch, seq)
            small["rel"] = bias_grad(dbias)
            win_t = a_win_t
            dz = None
            dcat = after(dcat, dq, *ffn_landed)
            big["wo0"] = after(big["wo0"], *ffn_landed)
            stage_wo = scatter_siblings(["wo0"])
        else:
            dz, small["cw"], small["csum"] = conv_bwd(sv["z"], sv["y_conv"], dcat, cw_full, lg_full, lb_full, batch, seq)
            win_t = b_win_t
            dz = after(dz, *ffn_landed)
        dqm, small[f"mq_{l}"], small[f"mk_{l}"], big[f"wkv{l}"], small[f"memnorm_{l}"] = memattn_bwd(
            sv["z"], sv["kv"], dcat, sv["gq4"], sv["gk4"], mem2, sv["mem_n"], wkv[l], dz, batch, seq, sv["qcol"],
            name=f"memattn_bwd_{l}")
        if l == 0:
            pieces = [dq, dk, dv, dqm]
            wo_parts, wo_landed = scatter_chips(stage_wo, dqm)
            big["win0"] = grad_pieces([after(dq, *wo_parts), dk, dv, dqm], sv["h"], name="grad_win_0")
            big["win0"] = after(big["win0"], *wo_landed)
        else:
            pieces = [dqm]
            big["win1"] = mm_tn(dqm, sv["h"], name="grad_win_1")
        stage1 = scatter_siblings([f"win{l}", f"wkv{l}"] + ([f"wo{l}"] if l == 1 else []))
        dx_b, small[f"norm1_{l}"], dz_sum = in_proj_bwd(
            pieces, win_t, sv["xin"], norm1_g[l:l + 1], dx1_b, BF16 if l == 1 else F32, name=f"in_proj_bwd_{l}")
        if l == 1:
            small["bb"] = dz_sum
        parts, mix_landed = scatter_chips(stage1, dx_b)
        dx_b = after(dx_b, *parts)
    grad_x = dx_b.reshape(batch, seq, d)
    update_layer(0, dx_b)

    def shaped(rows, idx, transposed):
        shp = weights[idx].shape
        if transposed:
            return jnp.swapaxes(rows.reshape(shp[0], shp[2], shp[1]), 1, 2)
        return rows.reshape(shp)

    def fold(v, groups):
        return jnp.sum(v.reshape(groups, HEAD_DIM), axis=0, keepdims=True)

    heads = a_rel_bias.shape[1]
    small_list = [
        jnp.concatenate([small["norm1_0"], small["norm1_1"]]),
        jnp.concatenate([small["memnorm_0"], small["memnorm_1"]]),
        fold(small["a_q"], 2), fold(small["a_k"], 2), small["rel"][:heads],
        small["bb"], small["cw"][:CONV_W], small["csum"][0:1], small["csum"][1:2], small["csum"][2:3],
        jnp.concatenate([fold(small["mq_0"], 4), fold(small["mq_1"], 4)]),
        jnp.concatenate([fold(small["mk_0"], 4), fold(small["mk_1"], 4)]),
        jnp.concatenate([small["norm2_0"], small["norm2_1"]]),
    ]
    (g_norm1, g_memnorm, g_aq, g_ak, g_rel, g_bb_full, g_cw_full, g_cb_full, g_lg_full, g_lb_full,
     g_mq, g_mk, g_norm2, loss_sum) = reduce_small(small_list + [loss_blk])
    loss = loss_sum[0, 0]
    g_bb = lax.dynamic_slice_in_dim(g_bb_full, me * f_loc, f_loc, axis=1)
    g_cw = lax.dynamic_slice_in_dim(g_cw_full, me * c_loc, c_loc, axis=1)
    g_cb = lax.dynamic_slice_in_dim(g_cb_full, me * c_loc, c_loc, axis=1)
    g_lg = lax.dynamic_slice_in_dim(g_lg_full, me * c_loc, c_loc, axis=1)
    g_lb = lax.dynamic_slice_in_dim(g_lb_full, me * c_loc, c_loc, axis=1)

    grads = [g_norm1, g_memnorm, None, g_aq, g_ak, g_rel, None, g_bb, g_cw, g_cb, g_lg, g_lb,
             g_mq, g_mk, None, None, g_norm2, None, None, None]
    deltas, new_m, new_v = [None] * 20, [None] * 20, [None] * 20
    for key, (idx, transposed) in sharded.items():
        grads[idx], deltas[idx], new_m[idx], new_v[idx] = (shaped(r, idx, transposed) for r in updated[key])

    def flat2(a):
        return a.reshape(a.shape[-2:])

    small_idx = [i for i in range(20) if i not in {idx for idx, _ in sharded.values()}]
    dl, nm, nv = adamw_small([flat2(weights[i]) for i in small_idx], [flat2(grads[i]) for i in small_idx],
                             [flat2(moms[i]) for i in small_idx], [flat2(vels[i]) for i in small_idx])
    for i, a, b, cc in zip(small_idx, dl, nm, nv):
        shp = weights[i].shape
        grads[i], deltas[i], new_m[i], new_v[i] = grads[i].reshape(shp), a.reshape(shp), b.reshape(shp), cc.reshape(shp)

    return (loss, grad_x, *grads, *deltas, *new_m, *new_v)
```
